```python
import math
import jax, jax.numpy as jnp
from jax import lax
import numpy as np

D_MODEL = 1024
BATCH = 16
SEQ = 2048
DEPTH = 1

HEAD_DIM = 64
SB_HEADS = 8
DIL_PAIRS = ((128, 1), (512, 4), (2048, 16))
DIL_HEADS_PER_GROUP = 4
DIL_HEADS = DIL_HEADS_PER_GROUP * len(DIL_PAIRS)
SB_WIDTH = SB_HEADS * HEAD_DIM
DIL_WIDTH = DIL_HEADS * HEAD_DIM
DIL_OUT_WIDTH = DIL_HEADS_PER_GROUP * HEAD_DIM
IN_WIDTH = 3 * SB_WIDTH + 3 * DIL_WIDTH + 2 * D_MODEL
D_FF = ((8 * D_MODEL + 3 * 256 - 1) // (3 * 256)) * 256
Q_BLOCK = 128
RMS_EPS = 1e-6
ALIBI_MAX_BIAS = 8.0
SPLITS = tuple(int(c) for c in np.cumsum([SB_WIDTH, SB_WIDTH, SB_WIDTH, DIL_WIDTH, DIL_WIDTH, DIL_WIDTH, D_MODEL]))

kernel_name = "hybrid_stickbreak_dilated_gated"


def rms_norm(x, g):
    xf = x.astype(jnp.float32)
    y = xf * lax.rsqrt(jnp.mean(xf * xf, axis=-1, keepdims=True) + RMS_EPS) * g.astype(jnp.float32)
    return y.astype(x.dtype)


def alibi_slopes(n):
    return jnp.exp2(-ALIBI_MAX_BIAS * jnp.arange(1, n + 1, dtype=jnp.float32) / n)


def stick_breaking_attention(q, k, v):
    b, s, h, dh = q.shape
    nb = s // Q_BLOCK
    scale = 1.0 / math.sqrt(dh)
    qb = q.reshape(b, nb, Q_BLOCK, h, dh).transpose(1, 0, 3, 2, 4)
    kpos = jnp.arange(s)

    def block(args):
        q_blk, t0 = args
        z = jnp.einsum('bhqd,bkhd->bhqk', q_blk, k, preferred_element_type=jnp.float32) * scale
        tpos = t0 + jnp.arange(Q_BLOCK)
        causal = kpos[None, :] < tpos[:, None]
        log_keep = jnp.where(causal, jax.nn.log_sigmoid(-z), 0.0)
        log_after = lax.cumsum(log_keep, axis=3, reverse=True) - log_keep
        a = jnp.where(causal, jnp.exp(jax.nn.log_sigmoid(z) + log_after), 0.0)
        return jnp.einsum('bhqk,bkhd->bqhd', a.astype(v.dtype), v)

    out = lax.map(block, (qb, jnp.arange(nb) * Q_BLOCK))
    return out.transpose(1, 0, 2, 3, 4).reshape(b, s, h * dh)


def dilated_group_attention(q, k, v, window, dilation, slopes):
    b, s, h, dh = q.shape
    L = s // dilation
    w = window // dilation
    blk = w
    nb = -(-L // blk)
    lp = nb * blk

    def to_sub(t):
        t = t.reshape(b, L, dilation, h, dh).transpose(0, 2, 3, 1, 4)
        return jnp.pad(t, ((0, 0), (0, 0), (0, 0), (0, lp - L), (0, 0)))

    def band(t):
        t = jnp.pad(t, ((0, 0), (0, 0), (0, 0), (blk, 0), (0, 0))).reshape(b, dilation, h, nb + 1, blk, dh)
        return jnp.concatenate([t[:, :, :, :-1], t[:, :, :, 1:]], axis=4)

    qb = to_sub(q).reshape(b, dilation, h, nb, blk, dh)
    kb = band(to_sub(k))
    vb = band(to_sub(v))
    scores = jnp.einsum('brhnqd,brhnkd->brhnqk', qb, kb, preferred_element_type=jnp.float32) / math.sqrt(dh)
    qa = jnp.arange(blk)
    kc = jnp.arange(2 * blk)
    dist = blk + qa[:, None] - kc[None, :]
    key_idx = (jnp.arange(nb)[:, None] - 1) * blk + kc[None, :]
    valid = ((dist >= 0) & (dist <= w))[None, :, :] & (key_idx >= 0)[:, None, :]
    scores = scores - slopes[:, None, None, None] * (dist * dilation).astype(jnp.float32)
    scores = jnp.where(valid, scores, -jnp.inf)
    m = scores.max(-1)
    p = jnp.exp(scores - m[..., None])
    l = p.sum(-1)
    num = jnp.einsum('brhnqk,brhnkd->brhnqd', p, vb.astype(jnp.float32))

    def from_sub(t):
        t = t.reshape((b, dilation, h, lp) + t.shape[5:])[:, :, :, :L]
        t = jnp.moveaxis(t, 3, 1)
        return t.reshape((b, s, h) + t.shape[4:])

    return from_sub(num), from_sub(m), from_sub(l)


def dilated_mixture_attention(q, k, v):
    b, s, _, dh = q.shape
    slopes = alibi_slopes(DIL_HEADS)
    nums, ms, ls = [], [], []
    for g, (window, dilation) in enumerate(DIL_PAIRS):
        sl = slice(g * DIL_HEADS_PER_GROUP, (g + 1) * DIL_HEADS_PER_GROUP)
        n_g, m_g, l_g = dilated_group_attention(q[:, :, sl], k[:, :, sl], v[:, :, sl], window, dilation, slopes[sl])
        nums.append(n_g); ms.append(m_g); ls.append(l_g)
    m = jnp.stack(ms)
    wts = jnp.exp(m - m.max(0))
    den = (wts * jnp.stack(ls)).sum(0)
    num = (wts[..., None] * jnp.stack(nums)).sum(0)
    out = num / den[..., None]
    return out.reshape(b, s, DIL_OUT_WIDTH)


def _fwd_setup_inputs(seed: int = 0) -> dict:
    key = jax.random.key(seed)
    ks = jax.random.split(key, 11)
    f32 = jnp.float32

    def nrm(k, shape, fan_in):
        return jax.random.normal(k, shape, f32) * (fan_in ** -0.5)

    return {
        "x": jax.random.normal(ks[0], (BATCH, SEQ, D_MODEL), f32),
        "norm_mix_g": 1.0 + 0.01 * jax.random.normal(ks[1], (DEPTH, D_MODEL), f32),
        "w_in": nrm(ks[2], (DEPTH, D_MODEL, IN_WIDTH), D_MODEL),
        "w_sb_up": nrm(ks[3], (DEPTH, SB_WIDTH, D_MODEL), SB_WIDTH),
        "w_dil_up": nrm(ks[4], (DEPTH, DIL_OUT_WIDTH, D_MODEL), DIL_OUT_WIDTH),
        "w_out": nrm(ks[5], (DEPTH, D_MODEL, D_MODEL), D_MODEL),
        "norm_ffn_g": 1.0 + 0.01 * jax.random.normal(ks[6], (DEPTH, D_MODEL), f32),
        "w_ffn_in": nrm(ks[7], (DEPTH, D_MODEL, 2 * D_FF), D_MODEL),
        "w_ffn_out": nrm(ks[8], (DEPTH, D_FF, D_MODEL), D_FF),
        "norm_final_g": 1.0 + 0.01 * jax.random.normal(ks[9], (D_MODEL,), f32),
    }


def _fwd_reference(x, norm_mix_g, w_in, w_sb_up, w_dil_up, w_out, norm_ffn_g, w_ffn_in, w_ffn_out, norm_final_g):
    b, s, _ = x.shape
    for i in range(DEPTH):
        u = rms_norm(x, norm_mix_g[i])
        proj = u @ w_in[i]
        q_sb, k_sb, v_sb, q_dl, k_dl, v_dl, gate_sb, gate_dl = jnp.split(proj, SPLITS, axis=-1)
        heads_sb = lambda t: t.reshape(b, s, SB_HEADS, HEAD_DIM)
        heads_dl = lambda t: t.reshape(b, s, DIL_HEADS, HEAD_DIM)
        o_sb = stick_breaking_attention(heads_sb(q_sb), heads_sb(k_sb), heads_sb(v_sb))
        o_dl = dilated_mixture_attention(heads_dl(q_dl), heads_dl(k_dl), heads_dl(v_dl)).astype(x.dtype)
        y_sb = o_sb @ w_sb_up[i]
        y_dl = o_dl @ w_dil_up[i]
        merged = jax.nn.sigmoid(gate_sb) * y_sb + jax.nn.sigmoid(gate_dl) * y_dl
        x = x + merged @ w_out[i]
        u2 = rms_norm(x, norm_ffn_g[i])
        g_ff, up_ff = jnp.split(u2 @ w_ffn_in[i], 2, axis=-1)
        x = x + (jax.nn.silu(g_ff) * up_ff) @ w_ffn_out[i]
    return rms_norm(x, norm_final_g)


import jax as _jax
import jax.numpy as _jnp

TWIN_FORMAT = 'train_step'
FWD_PARAMS = ['x', 'norm_mix_g', 'w_in', 'w_sb_up', 'w_dil_up', 'w_out', 'norm_ffn_g', 'w_ffn_in', 'w_ffn_out', 'norm_final_g']
TWIN_WEIGHTS = ['norm_mix_g', 'w_in', 'w_sb_up', 'w_dil_up', 'w_out', 'norm_ffn_g', 'w_ffn_in', 'w_ffn_out', 'norm_final_g']
TWIN_DIFF_INPUT = 'x'
TWIN_INPUTS = ['x', 'norm_mix_g', 'w_in', 'w_sb_up', 'w_dil_up', 'w_out', 'norm_ffn_g', 'w_ffn_in', 'w_ffn_out', 'norm_final_g', 'loss_target', 'm_norm_mix_g', 'm_w_in', 'm_w_sb_up', 'm_w_dil_up', 'm_w_out', 'm_norm_ffn_g', 'm_w_ffn_in', 'm_w_ffn_out', 'm_norm_final_g', 'v_norm_mix_g', 'v_w_in', 'v_w_sb_up', 'v_w_dil_up', 'v_w_out', 'v_norm_ffn_g', 'v_w_ffn_in', 'v_w_ffn_out', 'v_norm_final_g']
TWIN_OUTPUTS = ['loss', 'grad_x', 'grad_norm_mix_g', 'grad_w_in', 'grad_w_sb_up', 'grad_w_dil_up', 'grad_w_out', 'grad_norm_ffn_g', 'grad_w_ffn_in', 'grad_w_ffn_out', 'grad_norm_final_g', 'delta_norm_mix_g', 'delta_w_in', 'delta_w_sb_up', 'delta_w_dil_up', 'delta_w_out', 'delta_norm_ffn_g', 'delta_w_ffn_in', 'delta_w_ffn_out', 'delta_norm_final_g', 'new_m_norm_mix_g', 'new_m_w_in', 'new_m_w_sb_up', 'new_m_w_dil_up', 'new_m_w_out', 'new_m_norm_ffn_g', 'new_m_w_ffn_in', 'new_m_w_ffn_out', 'new_m_norm_final_g', 'new_v_norm_mix_g', 'new_v_w_in', 'new_v_w_sb_up', 'new_v_w_dil_up', 'new_v_w_out', 'new_v_norm_ffn_g', 'new_v_w_ffn_in', 'new_v_w_ffn_out', 'new_v_norm_final_g']
TWIN_LEAF_KINDS = {'loss': 'loss', 'grad_x': 'grad_x', 'grad_norm_mix_g': 'grad_w', 'grad_w_in': 'grad_w', 'grad_w_sb_up': 'grad_w', 'grad_w_dil_up': 'grad_w', 'grad_w_out': 'grad_w', 'grad_norm_ffn_g': 'grad_w', 'grad_w_ffn_in': 'grad_w', 'grad_w_ffn_out': 'grad_w', 'grad_norm_final_g': 'grad_w', 'delta_norm_mix_g': 'delta_w', 'delta_w_in': 'delta_w', 'delta_w_sb_up': 'delta_w', 'delta_w_dil_up': 'delta_w', 'delta_w_out': 'delta_w', 'delta_norm_ffn_g': 'delta_w', 'delta_w_ffn_in': 'delta_w', 'delta_w_ffn_out': 'delta_w', 'delta_norm_final_g': 'delta_w', 'new_m_norm_mix_g': 'new_m', 'new_m_w_in': 'new_m', 'new_m_w_sb_up': 'new_m', 'new_m_w_dil_up': 'new_m', 'new_m_w_out': 'new_m', 'new_m_norm_ffn_g': 'new_m', 'new_m_w_ffn_in': 'new_m', 'new_m_w_ffn_out': 'new_m', 'new_m_norm_final_g': 'new_m', 'new_v_norm_mix_g': 'new_v', 'new_v_w_in': 'new_v', 'new_v_w_sb_up': 'new_v', 'new_v_w_dil_up': 'new_v', 'new_v_w_out': 'new_v', 'new_v_norm_ffn_g': 'new_v', 'new_v_w_ffn_in': 'new_v', 'new_v_w_ffn_out': 'new_v', 'new_v_norm_final_g': 'new_v'}


def _forward(args):
    return _fwd_reference(*[args[k] for k in FWD_PARAMS])


def _output_shape():
    out = _jax.eval_shape(lambda: _forward(_fwd_setup_inputs(0)))
    return out.shape, out.dtype

N_MICROBATCH = 1
ADAM_LR = 0.001
ADAM_B1 = 0.9
ADAM_B2 = 0.999
ADAM_EPS = 1e-08
ADAM_WD = 0.01
ADAM_STEP = 10
PER_EXAMPLE_BATCH_AXIS = {'x': 0, 'loss_target': 0}
SHARED_INPUTS = []
_WEIGHT_DTYPES = {'norm_mix_g': _jnp.float32, 'w_in': _jnp.float32, 'w_sb_up': _jnp.float32, 'w_dil_up': _jnp.float32, 'w_out': _jnp.float32, 'norm_ffn_g': _jnp.float32, 'w_ffn_in': _jnp.float32, 'w_ffn_out': _jnp.float32, 'norm_final_g': _jnp.float32}
MOMENT_SCALE = {'norm_mix_g': 9.505111e-02, 'w_in': 3.996802e-02, 'w_sb_up': 6.427659e-02, 'w_dil_up': 3.285840e-02, 'w_out': 7.228431e-02, 'norm_ffn_g': 1.282035e-01, 'w_ffn_in': 5.272745e-02, 'w_ffn_out': 8.650068e-02, 'norm_final_g': 3.196470e+01}


def _to_microbatches(a, axis):
    t = _jnp.moveaxis(a, axis, 0)
    t = t.reshape((N_MICROBATCH, t.shape[0] // N_MICROBATCH) + t.shape[1:])
    return _jnp.moveaxis(t, 1, axis + 1)


def setup_inputs(seed: int = 0) -> dict:
    inp = _fwd_setup_inputs(seed)
    key = _jax.random.fold_in(_jax.random.key(seed), 7919)
    shape, _ = _output_shape()
    out = dict(inp)
    out["loss_target"] = _jax.random.normal(_jax.random.fold_in(key, 0), shape, _jnp.float32)
    for i, name in enumerate(TWIN_WEIGHTS):
        w = inp[name].astype(_jnp.float32)
        if MOMENT_SCALE is None:
            s = _jnp.sqrt(_jnp.mean(_jnp.square(w)) + 1e-30)
        else:
            s = MOMENT_SCALE[name]
        km, kv = _jax.random.split(_jax.random.fold_in(key, i + 1))
        out[name] = w
        out["m_" + name] = s * _jax.random.normal(km, w.shape, _jnp.float32)
        out["v_" + name] = (s * s) * _jax.random.uniform(kv, w.shape, _jnp.float32, 0.5, 1.5)
    if N_MICROBATCH > 1:
        for name, axis in PER_EXAMPLE_BATCH_AXIS.items():
            out[name] = _to_microbatches(out[name], axis)
    return {'x': out['x'], 'norm_mix_g': out['norm_mix_g'], 'w_in': out['w_in'], 'w_sb_up': out['w_sb_up'], 'w_dil_up': out['w_dil_up'], 'w_out': out['w_out'], 'norm_ffn_g': out['norm_ffn_g'], 'w_ffn_in': out['w_ffn_in'], 'w_ffn_out': out['w_ffn_out'], 'norm_final_g': out['norm_final_g'], 'loss_target': out['loss_target'], 'm_norm_mix_g': out['m_norm_mix_g'], 'm_w_in': out['m_w_in'], 'm_w_sb_up': out['m_w_sb_up'], 'm_w_dil_up': out['m_w_dil_up'], 'm_w_out': out['m_w_out'], 'm_norm_ffn_g': out['m_norm_ffn_g'], 'm_w_ffn_in': out['m_w_ffn_in'], 'm_w_ffn_out': out['m_w_ffn_out'], 'm_norm_final_g': out['m_norm_final_g'], 'v_norm_mix_g': out['v_norm_mix_g'], 'v_w_in': out['v_w_in'], 'v_w_sb_up': out['v_w_sb_up'], 'v_w_dil_up': out['v_w_dil_up'], 'v_w_out': out['v_w_out'], 'v_norm_ffn_g': out['v_norm_ffn_g'], 'v_w_ffn_in': out['v_w_ffn_in'], 'v_w_ffn_out': out['v_w_ffn_out'], 'v_norm_final_g': out['v_norm_final_g']}


def _loss(weights, diff, rest, loss_target):
    with _jax.named_scope("forward"):
        args = {**rest, TWIN_DIFF_INPUT: diff, **{k: w.astype(_WEIGHT_DTYPES[k]) for k, w in weights.items()}}
        y = _forward(args)
    with _jax.named_scope("loss_head"):
        err = _jnp.square(y.astype(_jnp.float32) - loss_target)
        return 0.5 * _jnp.sum(_jnp.mean(err, axis=-1)) if err.ndim else 0.5 * err


def _adamw(w, g, m, v):
    m = ADAM_B1 * m + (1.0 - ADAM_B1) * g
    v = ADAM_B2 * v + (1.0 - ADAM_B2) * _jnp.square(g)
    m_hat = m / (1.0 - ADAM_B1 ** ADAM_STEP)
    v_hat = v / (1.0 - ADAM_B2 ** ADAM_STEP)
    delta = -ADAM_LR * (m_hat / (_jnp.sqrt(v_hat) + ADAM_EPS) + ADAM_WD * w)
    return delta, m, v


def reference(x, norm_mix_g, w_in, w_sb_up, w_dil_up, w_out, norm_ffn_g, w_ffn_in, w_ffn_out, norm_final_g, loss_target, m_norm_mix_g, m_w_in, m_w_sb_up, m_w_dil_up, m_w_out, m_norm_ffn_g, m_w_ffn_in, m_w_ffn_out, m_norm_final_g, v_norm_mix_g, v_w_in, v_w_sb_up, v_w_dil_up, v_w_out, v_norm_ffn_g, v_w_ffn_in, v_w_ffn_out, v_norm_final_g):
    given = dict(x=x, norm_mix_g=norm_mix_g, w_in=w_in, w_sb_up=w_sb_up, w_dil_up=w_dil_up, w_out=w_out, norm_ffn_g=norm_ffn_g, w_ffn_in=w_ffn_in, w_ffn_out=w_ffn_out, norm_final_g=norm_final_g, loss_target=loss_target, m_norm_mix_g=m_norm_mix_g, m_w_in=m_w_in, m_w_sb_up=m_w_sb_up, m_w_dil_up=m_w_dil_up, m_w_out=m_w_out, m_norm_ffn_g=m_norm_ffn_g, m_w_ffn_in=m_w_ffn_in, m_w_ffn_out=m_w_ffn_out, m_norm_final_g=m_norm_final_g, v_norm_mix_g=v_norm_mix_g, v_w_in=v_w_in, v_w_sb_up=v_w_sb_up, v_w_dil_up=v_w_dil_up, v_w_out=v_w_out, v_norm_ffn_g=v_norm_ffn_g, v_w_ffn_in=v_w_ffn_in, v_w_ffn_out=v_w_ffn_out, v_norm_final_g=v_norm_final_g)
    weights = {n: given[n] for n in TWIN_WEIGHTS}
    shared = {n: given[n] for n in SHARED_INPUTS}
    per_example = {n: given[n] for n in ['x']}
    grad_fn = _jax.value_and_grad(_loss, argnums=(0, 1))

    def one_microbatch(ex, loss_target):
        ex = dict(ex)
        diff = ex.pop(TWIN_DIFF_INPUT)
        return grad_fn(weights, diff, {**shared, **ex}, loss_target)

    if N_MICROBATCH == 1:
        loss, (grad_w, grad_x) = one_microbatch(per_example, given["loss_target"])
    else:
        def body(carry, xs):
            loss_sum, grad_sum = carry
            l_k, (gw_k, gx_k) = one_microbatch(xs[0], xs[1])
            with _jax.named_scope("update"):
                return (loss_sum + l_k, _jax.tree.map(_jnp.add, grad_sum, gw_k)), gx_k

        init = (_jnp.zeros((), _jnp.float32), _jax.tree.map(_jnp.zeros_like, weights))
        (loss, grad_w), grad_x = _jax.lax.scan(body, init, (per_example, given["loss_target"]))
    with _jax.named_scope("update"):
        delta_w, new_m, new_v = {}, {}, {}
        for n in TWIN_WEIGHTS:
            delta_w[n], new_m[n], new_v[n] = _adamw(weights[n], grad_w[n], given["m_" + n], given["v_" + n])
    return (loss, grad_x, *[grad_w[n] for n in TWIN_WEIGHTS], *[delta_w[n] for n in TWIN_WEIGHTS],
            *[new_m[n] for n in TWIN_WEIGHTS], *[new_v[n] for n in TWIN_WEIGHTS])
```

```python
import functools
import math

import jax
import jax.numpy as jnp
from jax import lax
from jax.experimental import pallas as pl
from jax.experimental.pallas import tpu as pltpu

F32 = jnp.float32
BF16 = jnp.bfloat16
MESH = pl.DeviceIdType.MESH

HEAD_DIM = 64
SB_HEADS = 8
DIL_PAIRS = ((128, 1), (512, 4), (2048, 16))
DIL_HEADS_PER_GROUP = 4
DIL_HEADS = DIL_HEADS_PER_GROUP * len(DIL_PAIRS)
SB_WIDTH = SB_HEADS * HEAD_DIM
DIL_WIDTH = DIL_HEADS * HEAD_DIM
DIL_OUT_WIDTH = DIL_HEADS_PER_GROUP * HEAD_DIM
QKV_WIDTH = 3 * SB_WIDTH + 3 * DIL_WIDTH
RMS_EPS = 1e-6
ALIBI_MAX_BIAS = 8.0
ADAM_LR = 0.001
ADAM_B1 = 0.9
ADAM_B2 = 0.999
ADAM_EPS = 1e-08
ADAM_WD = 0.01
ADAM_STEP = 10

LANES = 128
BLK = 128
NEG = -1e30
N_CHIPS = 4
VMEM_CAP = 56 * 1024 * 1024


def _vmem_limit(tile_bytes):
    return int(min(VMEM_CAP, max(32 * 1024 * 1024, 3 * tile_bytes + 8 * 1024 * 1024)))


def _nbytes(shape, dtype):
    return math.prod(shape) * jnp.dtype(dtype).itemsize


def _dot(a, b):
    return jnp.dot(a, b, preferred_element_type=F32)


def _dot_nt(a, b):
    return lax.dot_general(a, b, (((1,), (1,)), ((), ())), preferred_element_type=F32)


def _dot_tn(a, b):
    return lax.dot_general(a, b, (((0,), (0,)), ((), ())), preferred_element_type=F32)


def _split2(x):
    hi = x.astype(BF16)
    lo = (x - hi.astype(F32)).astype(BF16)
    return hi, lo


def _sigmoid(x):
    return 1.0 / (1.0 + jnp.exp(-x))


def _mm(a, b, *, ta=False, tb=False, add=None, out_dtype=F32, tm, tn, tk, name):
    if ta:
        kdim, m = a.shape
    else:
        m, kdim = a.shape
    if tb:
        n, k2 = b.shape
    else:
        k2, n = b.shape
    assert kdim == k2 and m % tm == 0 and n % tn == 0 and kdim % tk == 0, (name, a.shape, b.shape)
    nk = kdim // tk
    a_spec = pl.BlockSpec((tk, tm), lambda i, j, k: (k, i)) if ta else pl.BlockSpec((tm, tk), lambda i, j, k: (i, k))
    b_spec = pl.BlockSpec((tn, tk), lambda i, j, k: (j, k)) if tb else pl.BlockSpec((tk, tn), lambda i, j, k: (k, j))
    o_spec = pl.BlockSpec((tm, tn), lambda i, j, k: (i, j))
    dims = ((((0,) if ta else (1,)), ((1,) if tb else (0,))), ((), ()))
    has_add = add is not None
    use_scratch = nk > 1 and jnp.dtype(out_dtype) != jnp.dtype(F32)

    def body(*refs):
        a_ref, b_ref = refs[0], refs[1]
        add_ref = refs[2] if has_add else None
        o_ref = refs[3] if has_add else refs[2]
        acc_ref = refs[-1] if use_scratch else o_ref
        prod = lax.dot_general(a_ref[...].astype(BF16), b_ref[...].astype(BF16), dims, preferred_element_type=F32)
        if nk == 1:
            if has_add:
                prod = prod + add_ref[...]
            o_ref[...] = prod.astype(out_dtype)
            return
        k = pl.program_id(2)

        @pl.when(k == 0)
        def _():
            acc_ref[...] = prod + add_ref[...] if has_add else prod

        @pl.when(k > 0)
        def _():
            acc_ref[...] += prod

        if use_scratch:
            @pl.when(k == nk - 1)
            def _():
                o_ref[...] = acc_ref[...].astype(out_dtype)

    tile_bytes = (_nbytes((tm, tk), a.dtype) + _nbytes((tk, tn), b.dtype) + _nbytes((tm, tn), out_dtype)
                  + (_nbytes((tm, tn), F32) if has_add else 0) + _nbytes((tm, tn), F32))
    in_specs = [a_spec, b_spec] + ([o_spec] if has_add else [])
    args = (a, b) + ((add,) if has_add else ())
    return pl.pallas_call(
        body, name=name, grid=(m // tm, n // tn, nk),
        in_specs=in_specs, out_specs=o_spec,
        out_shape=jax.ShapeDtypeStruct((m, n), out_dtype),
        scratch_shapes=[pltpu.VMEM((tm, tn), F32)] if use_scratch else [],
        compiler_params=pltpu.CompilerParams(
            dimension_semantics=("parallel", "parallel", "arbitrary"),
            vmem_limit_bytes=_vmem_limit(tile_bytes)),
    )(*args)


def _rowwise(fn, rows, params, outs, accs, *, tm, name):
    t = rows[0].shape[0]
    assert t % tm == 0, (name, t, tm)
    n_r, n_p, n_o = len(rows), len(params), len(outs)

    def body(*refs):
        vals = [r[...] for r in refs[:n_r + n_p]]
        res = fn(*vals)
        o_refs = refs[n_r + n_p:n_r + n_p + n_o]
        a_refs = refs[n_r + n_p + n_o:]
        for o_ref, v in zip(o_refs, res[:n_o]):
            o_ref[...] = v.astype(o_ref.dtype)
        if accs:
            @pl.when(pl.program_id(0) == 0)
            def _():
                for a_ref in a_refs:
                    a_ref[...] = jnp.zeros(a_ref.shape, F32)

            for a_ref, v in zip(a_refs, res[n_o:]):
                a_ref[...] += v

    in_specs = [pl.BlockSpec((tm, r.shape[1]), lambda i: (i, 0)) for r in rows]
    in_specs += [pl.BlockSpec(p.shape, lambda i: (0, 0)) for p in params]
    out_specs = [pl.BlockSpec((tm, w), lambda i: (i, 0)) for w, _ in outs]
    out_specs += [pl.BlockSpec(s, lambda i: (0, 0)) for s in accs]
    out_shape = [jax.ShapeDtypeStruct((t, w), d) for w, d in outs]
    out_shape += [jax.ShapeDtypeStruct(s, F32) for s in accs]
    tile_bytes = sum(_nbytes((tm, r.shape[1]), r.dtype) for r in rows) + sum(_nbytes((tm, w), F32) for w, _ in outs)
    res = pl.pallas_call(
        body, name=name, grid=(t // tm,), in_specs=in_specs, out_specs=out_specs, out_shape=out_shape,
        compiler_params=pltpu.CompilerParams(
            dimension_semantics=("arbitrary",) if accs else ("parallel",),
            vmem_limit_bytes=_vmem_limit(2 * tile_bytes)),
    )(*rows, *params)
    return res


def _rms_stats(x):
    r = lax.rsqrt(jnp.mean(x * x, axis=-1, keepdims=True) + RMS_EPS)
    return x * r, r


def _rms_bwd(dy, xhat, r, g):
    dxhat = dy * g
    dx = r * (dxhat - xhat * jnp.mean(dxhat * xhat, axis=-1, keepdims=True))
    return dx, dy * xhat


def _sb_consts():
    lane = lax.broadcasted_iota(jnp.int32, (BLK, LANES), 1)
    head0 = lane < HEAD_DIM
    row = lax.broadcasted_iota(jnp.int32, (2 * BLK, BLK), 0) % BLK
    col = lax.broadcasted_iota(jnp.int32, (2 * BLK, BLK), 1)
    causal = col < row
    jj = lax.broadcasted_iota(jnp.int32, (BLK, 2 * BLK), 0)
    ss = lax.broadcasted_iota(jnp.int32, (BLK, 2 * BLK), 1)
    suffix = jnp.where((ss >= BLK) | (jj > ss), 1.0, 0.0).astype(BF16)
    return head0, causal, suffix


def _stack_heads(x, head0):
    zero = jnp.zeros_like(x)
    return jnp.concatenate([jnp.where(head0, x, zero), jnp.where(head0, zero, x)], axis=0)


def _sb_logits(q01, k, causal, masked):
    z = _dot_nt(q01, k) * (1.0 / math.sqrt(HEAD_DIM))
    sp = jnp.log(1.0 + jnp.exp(-jnp.abs(z)))
    log_keep = -(jnp.maximum(z, 0.0) + sp)
    log_beta = jnp.minimum(z, 0.0) - sp
    if masked:
        log_keep = jnp.where(causal, log_keep, 0.0)
    return log_keep, log_beta


def _suffix_sums(x, suffix):
    hi, lo = _split2(x)
    s2 = _dot(hi, suffix) + _dot(lo, suffix)
    return s2[:, :BLK], s2[:, BLK:]


def _sb_fwd(qkv, b_sz, s_len):
    nq = s_len // BLK
    n_pairs = SB_WIDTH // LANES

    def body(q_ref, k_ref, v_ref, o_ref):
        head0, causal, suffix = _sb_consts()

        def q_block(i, _):
            qs = pl.multiple_of(i * BLK, BLK)
            q01 = _stack_heads(q_ref[pl.ds(qs, BLK), :].astype(BF16), head0)

            def tile(j, carry, acc, masked):
                ks = pl.multiple_of(j * BLK, BLK)
                k = k_ref[pl.ds(ks, BLK), :].astype(BF16)
                v = v_ref[pl.ds(ks, BLK), :].astype(BF16)
                log_keep, log_beta = _sb_logits(q01, k, causal, masked)
                after, total = _suffix_sums(log_keep, suffix)
                a = jnp.exp(log_beta + carry + after)
                if masked:
                    a = jnp.where(causal, a, 0.0)
                a_hi, a_lo = _split2(a)
                a_cat = jnp.concatenate([a_hi[:BLK], a_hi[BLK:], a_lo[:BLK], a_lo[BLK:]], axis=1)
                v01 = _stack_heads(v, head0)
                acc = acc + _dot(a_cat, jnp.concatenate([v01, v01], axis=0))
                return carry + total, acc

            carry = jnp.zeros((2 * BLK, BLK), F32)
            acc = jnp.zeros((BLK, LANES), F32)
            carry, acc = tile(i, carry, acc, True)

            def inner(t, ca):
                return tile(i - 1 - t, ca[0], ca[1], False)

            carry, acc = lax.fori_loop(0, i, inner, (carry, acc))
            o_ref[pl.ds(qs, BLK), :] = acc
            return 0

        lax.fori_loop(0, nq, q_block, 0)

    blk = lambda off: pl.BlockSpec((None, s_len, LANES), lambda b, p: (b, 0, off + p))
    return pl.pallas_call(
        body, name="sb_fwd", grid=(b_sz, n_pairs),
        in_specs=[blk(0), blk(n_pairs), blk(2 * n_pairs)],
        out_specs=pl.BlockSpec((None, s_len, LANES), lambda b, p: (b, 0, p)),
        out_shape=jax.ShapeDtypeStruct((b_sz, s_len, SB_WIDTH), F32),
        compiler_params=pltpu.CompilerParams(dimension_semantics=("parallel", "parallel"),
                                             vmem_limit_bytes=48 * 1024 * 1024),
    )(qkv, qkv, qkv)


def _sb_bwd(qkv, o_sb, do_sb, b_sz, s_len):
    nq = s_len // BLK
    n_pairs = SB_WIDTH // LANES

    def body(q_ref, k_ref, v_ref, o_ref, do_ref, dq_ref, dk_ref, dv_ref, dk_acc, dv_acc):
        head0, causal, suffix = _sb_consts()
        lrow = lax.broadcasted_iota(jnp.int32, (LANES, LANES), 0)
        ones_h0 = jnp.where(lrow < HEAD_DIM, 1.0, 0.0).astype(BF16)
        ones_h1 = jnp.where(lrow >= HEAD_DIM, 1.0, 0.0).astype(BF16)
        dk_acc[...] = jnp.zeros(dk_acc.shape, F32)
        dv_acc[...] = jnp.zeros(dv_acc.shape, F32)

        def q_block(i, _):
            qs = pl.multiple_of(i * BLK, BLK)
            q01 = _stack_heads(q_ref[pl.ds(qs, BLK), :].astype(BF16), head0)
            do_b = do_ref[pl.ds(qs, BLK), :].astype(BF16)
            do01 = _stack_heads(do_b, head0)
            dd_hi, dd_lo = _split2(do_b.astype(F32) * o_ref[pl.ds(qs, BLK), :])
            tot = jnp.concatenate([_dot(dd_hi, ones_h0) + _dot(dd_lo, ones_h0),
                                   _dot(dd_hi, ones_h1) + _dot(dd_lo, ones_h1)], axis=0)

            def tile(j, carry, rcarry, dq, masked):
                ks = pl.multiple_of(j * BLK, BLK)
                k = k_ref[pl.ds(ks, BLK), :].astype(BF16)
                v = v_ref[pl.ds(ks, BLK), :].astype(BF16)
                log_keep, log_beta = _sb_logits(q01, k, causal, masked)
                after, total = _suffix_sums(log_keep, suffix)
                a = jnp.exp(log_beta + carry + after)
                if masked:
                    a = jnp.where(causal, a, 0.0)
                e = a * _dot_nt(do01, v)
                e_after, e_total = _suffix_sums(e, suffix)
                before = tot - (rcarry + e_after + e)
                beta = jnp.exp(log_beta)
                dz = (e * (1.0 - beta) - beta * before) * (1.0 / math.sqrt(HEAD_DIM))
                if masked:
                    dz = jnp.where(causal, dz, 0.0)
                dz_b = dz.astype(BF16)
                dq = dq + _dot(dz_b, k)
                dk_acc[pl.ds(ks, BLK), :] += _dot_tn(dz_b, q01)
                dv_acc[pl.ds(ks, BLK), :] += _dot_tn(a.astype(BF16), do01)
                return carry + total, rcarry + e_total, dq

            zero = jnp.zeros((2 * BLK, BLK), F32)
            carry, rcarry, dq = tile(i, zero, zero, zero, True)

            def inner(t, c):
                return tile(i - 1 - t, c[0], c[1], c[2], False)

            carry, rcarry, dq = lax.fori_loop(0, i, inner, (carry, rcarry, dq))
            dq_ref[pl.ds(qs, BLK), :] = jnp.where(head0, dq[:BLK], dq[BLK:]).astype(dq_ref.dtype)
            return 0

        lax.fori_loop(0, nq, q_block, 0)
        dk_ref[...] = dk_acc[...].astype(dk_ref.dtype)
        dv_ref[...] = dv_acc[...].astype(dv_ref.dtype)

    blk = lambda off: pl.BlockSpec((None, s_len, LANES), lambda b, p: (b, 0, off + p))
    out_sd = jax.ShapeDtypeStruct((b_sz, s_len, SB_WIDTH), BF16)
    return pl.pallas_call(
        body, name="sb_bwd", grid=(b_sz, n_pairs),
        in_specs=[blk(0), blk(n_pairs), blk(2 * n_pairs), blk(0), blk(0)],
        out_specs=[blk(0), blk(0), blk(0)],
        out_shape=[out_sd, out_sd, out_sd],
        scratch_shapes=[pltpu.VMEM((s_len, LANES), F32), pltpu.VMEM((s_len, LANES), F32)],
        compiler_params=pltpu.CompilerParams(dimension_semantics=("parallel", "parallel"),
                                             vmem_limit_bytes=48 * 1024 * 1024),
    )(qkv, qkv, qkv, o_sb, do_sb)


def _dil_consts(group, pair_idx, dilation):
    lane = lax.broadcasted_iota(jnp.int32, (BLK, LANES), 1)
    head0 = lane < HEAD_DIM
    row = lax.broadcasted_iota(jnp.int32, (2 * BLK, BLK), 0)
    qa = row % BLK
    kb = lax.broadcasted_iota(jnp.int32, (2 * BLK, BLK), 1)
    head = (group * DIL_HEADS_PER_GROUP + 2 * pair_idx + row // BLK).astype(F32)
    slope = jnp.exp((-ALIBI_MAX_BIAS * math.log(2.0) / DIL_HEADS) * (head + 1.0))
    valid_cur = kb <= qa
    valid_prev = kb >= qa
    bias_cur = -slope * ((qa - kb) * dilation).astype(F32)
    bias_prev = -slope * ((BLK + qa - kb) * dilation).astype(F32)
    return head0, valid_cur, valid_prev, bias_cur, bias_prev


def _dil_units(s_len, dilation):
    nb = s_len // dilation // BLK
    return [(r, n) for r in range(dilation) for n in range(nb)]


def _dil_rows(n, r, dilation):
    if dilation == 1:
        return pl.ds(n * BLK, BLK)
    return pl.ds(n * BLK * dilation + r, BLK, stride=dilation)


def _dil_scores(q01, k, bias, valid):
    s = _dot_nt(q01, k) * (1.0 / math.sqrt(HEAD_DIM)) + bias
    return jnp.where(valid, s, NEG)


def _dil_fwd(qkv, b_sz, s_len):
    n_pairs = DIL_OUT_WIDTH // LANES
    q_off = 3 * SB_WIDTH // LANES
    per_kind = DIL_WIDTH // LANES

    def body(*refs):
        qkv_refs = refs[:9]
        o_ref, lse_ref, m_s, l_s = refs[9:]
        pair_idx = pl.program_id(1)
        m_s[...] = jnp.full(m_s.shape, NEG, F32)
        l_s[...] = jnp.zeros(l_s.shape, F32)
        o_ref[...] = jnp.zeros(o_ref.shape, F32)
        for g, (_, dilation) in enumerate(DIL_PAIRS):
            q_ref, k_ref, v_ref = qkv_refs[3 * g:3 * g + 3]
            head0, valid_cur, valid_prev, bias_cur, bias_prev = _dil_consts(g, pair_idx, dilation)
            for r, n in _dil_units(s_len, dilation):
                rows = _dil_rows(n, r, dilation)
                q01 = _stack_heads(q_ref[rows, :].astype(BF16), head0)
                k_c = k_ref[rows, :].astype(BF16)
                v_c = v_ref[rows, :].astype(BF16)
                scores = [_dil_scores(q01, k_c, bias_cur, valid_cur)]
                values = [_stack_heads(v_c, head0)]
                if n > 0:
                    prev = _dil_rows(n - 1, r, dilation)
                    scores.append(_dil_scores(q01, k_ref[prev, :].astype(BF16), bias_prev, valid_prev))
                    values.append(_stack_heads(v_ref[prev, :].astype(BF16), head0))
                m_blk = functools.reduce(jnp.maximum, [jnp.max(s, axis=-1, keepdims=True) for s in scores])
                m_old = jnp.concatenate([m_s.at[0][rows, :], m_s.at[1][rows, :]], axis=0)
                l_old = jnp.concatenate([l_s.at[0][rows, :], l_s.at[1][rows, :]], axis=0)
                m_new = jnp.maximum(m_old, m_blk)
                probs = [jnp.exp(s - m_new) for s in scores]
                l_blk = functools.reduce(jnp.add, [jnp.sum(p, axis=-1, keepdims=True) for p in probs])
                alpha = jnp.exp(m_old - m_new)
                l_new = alpha * l_old + l_blk
                alpha_tok = jnp.where(head0, alpha[:BLK], alpha[BLK:])
                p_cat = jnp.concatenate([h for p in probs for h in (p[:BLK].astype(BF16), p[BLK:].astype(BF16))], axis=1)
                o_ref[rows, :] = alpha_tok * o_ref[rows, :] + _dot(p_cat, jnp.concatenate(values, axis=0))
                m_s.at[0][rows, :] = m_new[:BLK]
                m_s.at[1][rows, :] = m_new[BLK:]
                l_s.at[0][rows, :] = l_new[:BLK]
                l_s.at[1][rows, :] = l_new[BLK:]
        lane = lax.broadcasted_iota(jnp.int32, (BLK, LANES), 1)
        for c in range(s_len // BLK):
            rows = pl.ds(c * BLK, BLK)
            l0, l1 = l_s.at[0][rows, :], l_s.at[1][rows, :]
            o_ref[rows, :] = o_ref[rows, :] / jnp.where(lane < HEAD_DIM, l0, l1)
            lse_ref.at[0][rows, :] = m_s.at[0][rows, :] + jnp.log(l0)
            lse_ref.at[1][rows, :] = m_s.at[1][rows, :] + jnp.log(l1)

    in_specs = []
    for g in range(len(DIL_PAIRS)):
        for kind in range(3):
            off = q_off + kind * per_kind + g * n_pairs
            in_specs.append(pl.BlockSpec((None, s_len, LANES), lambda b, p, off=off: (b, 0, off + p)))
    return pl.pallas_call(
        body, name="dil_fwd", grid=(b_sz, n_pairs),
        in_specs=in_specs,
        out_specs=[pl.BlockSpec((None, s_len, LANES), lambda b, p: (b, 0, p)),
                   pl.BlockSpec((None, None, 2, s_len, LANES), lambda b, p: (b, p, 0, 0, 0))],
        out_shape=[jax.ShapeDtypeStruct((b_sz, s_len, DIL_OUT_WIDTH), F32),
                   jax.ShapeDtypeStruct((b_sz, n_pairs, 2, s_len, LANES), F32)],
        scratch_shapes=[pltpu.VMEM((2, s_len, LANES), F32), pltpu.VMEM((2, s_len, LANES), F32)],
        compiler_params=pltpu.CompilerParams(dimension_semantics=("parallel", "parallel"),
                                             vmem_limit_bytes=VMEM_CAP),
    )(*([qkv] * 9))


def _dil_bwd(qkv, o_dl, lse, do_dl, b_sz, s_len):
    n_pairs = DIL_OUT_WIDTH // LANES
    n_groups = len(DIL_PAIRS)
    q_off = 3 * SB_WIDTH // LANES
    per_kind = DIL_WIDTH // LANES

    def body(q_ref, k_ref, v_ref, o_ref, lse_ref, do_ref, dq_ref, dk_ref, dv_ref, d_s):
        pair_idx = pl.program_id(1)
        group = pl.program_id(2)
        lrow = lax.broadcasted_iota(jnp.int32, (LANES, LANES), 0)
        ones_h0 = jnp.where(lrow < HEAD_DIM, 1.0, 0.0).astype(BF16)
        ones_h1 = jnp.where(lrow >= HEAD_DIM, 1.0, 0.0).astype(BF16)
        for c in range(s_len // BLK):
            rows = pl.ds(c * BLK, BLK)
            dd_hi, dd_lo = _split2(do_ref[rows, :] * o_ref[rows, :])
            d_s.at[0][rows, :] = _dot(dd_hi, ones_h0) + _dot(dd_lo, ones_h0)
            d_s.at[1][rows, :] = _dot(dd_hi, ones_h1) + _dot(dd_lo, ones_h1)
        dk_ref[...] = jnp.zeros(dk_ref.shape, F32)
        dv_ref[...] = jnp.zeros(dv_ref.shape, F32)

        def one_group(g, dilation):
            head0, valid_cur, valid_prev, bias_cur, bias_prev = _dil_consts(g, pair_idx, dilation)
            for r, n in _dil_units(s_len, dilation):
                rows = _dil_rows(n, r, dilation)
                q01 = _stack_heads(q_ref[rows, :].astype(BF16), head0)
                do01 = _stack_heads(do_ref[rows, :].astype(BF16), head0)
                lse01 = jnp.concatenate([lse_ref.at[0][rows, :], lse_ref.at[1][rows, :]], axis=0)
                d01 = jnp.concatenate([d_s.at[0][rows, :], d_s.at[1][rows, :]], axis=0)
                dq = jnp.zeros((2 * BLK, LANES), F32)
                blocks = [(rows, bias_cur, valid_cur)]
                if n > 0:
                    blocks.append((_dil_rows(n - 1, r, dilation), bias_prev, valid_prev))
                for krows, bias, valid in blocks:
                    k = k_ref[krows, :].astype(BF16)
                    v = v_ref[krows, :].astype(BF16)
                    p = jnp.exp(_dil_scores(q01, k, bias, valid) - lse01)
                    ds = (p * (_dot_nt(do01, v) - d01) * (1.0 / math.sqrt(HEAD_DIM))).astype(BF16)
                    dq = dq + _dot(ds, k)
                    dk_ref[krows, :] = dk_ref[krows, :] + _dot_tn(ds, q01)
                    dv_ref[krows, :] = dv_ref[krows, :] + _dot_tn(p.astype(BF16), do01)
                dq_ref[rows, :] = jnp.where(head0, dq[:BLK], dq[BLK:])

        for g, (_, dilation) in enumerate(DIL_PAIRS):
            pl.when(group == g)(functools.partial(one_group, g, dilation))

    def qkv_spec(kind):
        return pl.BlockSpec((None, s_len, LANES),
                            lambda b, p, g: (b, 0, q_off + kind * per_kind + g * n_pairs + p))

    tok_spec = pl.BlockSpec((None, s_len, LANES), lambda b, p, g: (b, 0, p))
    out_spec = pl.BlockSpec((None, s_len, LANES), lambda b, p, g: (b, 0, g * n_pairs + p))
    out_sd = jax.ShapeDtypeStruct((b_sz, s_len, DIL_WIDTH), F32)
    return pl.pallas_call(
        body, name="dil_bwd", grid=(b_sz, n_pairs, n_groups),
        in_specs=[qkv_spec(0), qkv_spec(1), qkv_spec(2), tok_spec,
                  pl.BlockSpec((None, None, 2, s_len, LANES), lambda b, p, g: (b, p, 0, 0, 0)), tok_spec],
        out_specs=[out_spec, out_spec, out_spec],
        out_shape=[out_sd, out_sd, out_sd],
        scratch_shapes=[pltpu.VMEM((2, s_len, LANES), F32)],
        compiler_params=pltpu.CompilerParams(dimension_semantics=("parallel", "parallel", "arbitrary"),
                                             vmem_limit_bytes=VMEM_CAP),
    )(qkv, qkv, qkv, o_dl, lse, do_dl)


def _mesh_pos():
    return lax.axis_index("x"), lax.axis_index("y"), lax.axis_index("c")


def _other_chips(x, y):
    return [(1 - x, y), (x, 1 - y), (1 - x, 1 - y)]


def _hbm_specs(n):
    return [pl.BlockSpec(memory_space=pl.ANY)] * n


def _gather_weights(shards):
    n = len(shards)

    def body(*refs):
        ins, outs = refs[:n], refs[n:2 * n]
        send_sems, recv_sems, local_sems = refs[2 * n:]
        x, y, c = _mesh_pos()
        mine = 2 * x + y
        chips = _other_chips(x, y)
        locals_, sends = [], []
        for k in range(n):
            half = shards[k].shape[0] // 2
            rows = pl.ds(c * half, half)
            local = pltpu.make_async_copy(ins[k], outs[k].at[mine], local_sems.at[k])
            local.start()
            locals_.append(local)
            for r, (px, py) in enumerate(chips):
                cp = pltpu.make_async_remote_copy(
                    src_ref=ins[k].at[rows, :], dst_ref=outs[k].at[mine, rows, :],
                    send_sem=send_sems.at[6 * k + r], recv_sem=recv_sems.at[6 * k + r],
                    device_id=(px, py, c), device_id_type=MESH)
                cp.start()
                sends.append(cp)
        for k in range(n):
            half = shards[k].shape[0] // 2
            rows = pl.ds(c * half, half)
            for r, (px, py) in enumerate(chips):
                theirs = outs[k].at[2 * px + py, rows, :]
                pltpu.make_async_remote_copy(
                    src_ref=theirs, dst_ref=theirs, send_sem=send_sems.at[6 * k + r], recv_sem=recv_sems.at[6 * k + r],
                    device_id=(px, py, c), device_id_type=MESH).wait_recv()
                fwd = pltpu.make_async_remote_copy(
                    src_ref=theirs, dst_ref=theirs, send_sem=send_sems.at[6 * k + 3 + r], recv_sem=recv_sems.at[6 * k + 3 + r],
                    device_id=(x, y, 1 - c), device_id_type=MESH)
                fwd.start()
                sends.append(fwd)
        for k in range(n):
            half = shards[k].shape[0] // 2
            sib_rows = pl.ds((1 - c) * half, half)
            for r, (px, py) in enumerate(chips):
                landed = outs[k].at[2 * px + py, sib_rows, :]
                pltpu.make_async_remote_copy(
                    src_ref=landed, dst_ref=landed, send_sem=send_sems.at[6 * k + 3 + r], recv_sem=recv_sems.at[6 * k + 3 + r],
                    device_id=(x, y, 1 - c), device_id_type=MESH).wait_recv()
        for cp in sends:
            cp.wait_send()
        for local in locals_:
            local.wait()

    return pl.pallas_call(
        body, name="gather_weights",
        in_specs=_hbm_specs(n), out_specs=_hbm_specs(n),
        out_shape=[jax.ShapeDtypeStruct((N_CHIPS,) + s.shape, s.dtype) for s in shards],
        scratch_shapes=[pltpu.SemaphoreType.DMA((6 * n,)), pltpu.SemaphoreType.DMA((6 * n,)),
                        pltpu.SemaphoreType.DMA((n,))],
    )(*shards)


def _pair_exchange(grads):
    n = len(grads)

    def body(*refs):
        ins, outs = refs[:n], refs[n:2 * n]
        send_sems, recv_sems = refs[2 * n:]
        x, y, c = _mesh_pos()
        copies = []
        for k in range(n):
            half = grads[k].shape[1] // 2
            cp = pltpu.make_async_remote_copy(
                src_ref=ins[k].at[:, pl.ds((1 - c) * half, half), :], dst_ref=outs[k],
                send_sem=send_sems.at[k], recv_sem=recv_sems.at[k],
                device_id=(x, y, 1 - c), device_id_type=MESH)
            cp.start()
            copies.append(cp)
        for cp in copies:
            cp.wait()

    return pl.pallas_call(
        body, name="grad_pair_exchange",
        in_specs=_hbm_specs(n), out_specs=_hbm_specs(n),
        out_shape=[jax.ShapeDtypeStruct((N_CHIPS, g.shape[1] // 2, g.shape[2]), F32) for g in grads],
        scratch_shapes=[pltpu.SemaphoreType.DMA((n,)), pltpu.SemaphoreType.DMA((n,))],
    )(*grads)


def _pair_sum(grad, other, name):
    _, rows, cols = grad.shape
    half = rows // 2
    core = jnp.reshape(lax.axis_index("c"), (1,)).astype(jnp.int32)

    def body(core_ref, g_ref, p_ref, s_ref, sb_ref):
        s = g_ref[...] + p_ref[...]
        s_ref[...] = s
        sb_ref[...] = s.astype(BF16)

    blk = pl.BlockSpec((None, half, cols), lambda p, core_ref: (p, 0, 0))
    return pl.pallas_call(
        body, name=name,
        grid_spec=pltpu.PrefetchScalarGridSpec(
            num_scalar_prefetch=1, grid=(N_CHIPS,),
            in_specs=[pl.BlockSpec((None, half, cols), lambda p, core_ref: (p, core_ref[0], 0)), blk],
            out_specs=[blk, blk]),
        out_shape=[jax.ShapeDtypeStruct((N_CHIPS, half, cols), F32),
                   jax.ShapeDtypeStruct((N_CHIPS, half, cols), BF16)],
        compiler_params=pltpu.CompilerParams(dimension_semantics=("parallel",),
                                             vmem_limit_bytes=_vmem_limit(4 * half * cols * 4)),
    )(core, grad, other)


def _chip_exchange(sums_f32, sums_bf16):
    n = len(sums_f32)

    def body(*refs):
        f_ins, b_ins = refs[:n], refs[n:2 * n]
        owns, lands = refs[2 * n:3 * n], refs[3 * n:4 * n]
        send_sems, recv_sems, local_sems = refs[4 * n:]
        x, y, c = _mesh_pos()
        mine = 2 * x + y
        chips = _other_chips(x, y)
        copies, locals_ = [], []
        for k in range(n):
            local = pltpu.make_async_copy(f_ins[k].at[mine], owns[k], local_sems.at[k])
            local.start()
            locals_.append(local)
            for r, (px, py) in enumerate(chips):
                cp = pltpu.make_async_remote_copy(
                    src_ref=b_ins[k].at[2 * px + py], dst_ref=lands[k].at[r],
                    send_sem=send_sems.at[3 * k + r], recv_sem=recv_sems.at[3 * k + r],
                    device_id=(px, py, c), device_id_type=MESH)
                cp.start()
                copies.append(cp)
        for cp in copies:
            cp.wait()
        for local in locals_:
            local.wait()

    return pl.pallas_call(
        body, name="grad_chip_exchange",
        in_specs=_hbm_specs(2 * n), out_specs=_hbm_specs(2 * n),
        out_shape=[jax.ShapeDtypeStruct(s.shape[1:], F32) for s in sums_f32]
        + [jax.ShapeDtypeStruct((N_CHIPS - 1,) + s.shape[1:], BF16) for s in sums_bf16],
        scratch_shapes=[pltpu.SemaphoreType.DMA((3 * n,)), pltpu.SemaphoreType.DMA((3 * n,)),
                        pltpu.SemaphoreType.DMA((n,))],
    )(*sums_f32, *sums_bf16)


def _chip_sum(own, landed, name):
    rows, cols = own.shape

    def body(o_ref, l_ref, out_ref):
        out_ref[...] = ((o_ref[...] + l_ref[0].astype(F32)) + l_ref[1].astype(F32)) + l_ref[2].astype(F32)

    return pl.pallas_call(
        body, name=name, grid=(1,),
        in_specs=[pl.BlockSpec((rows, cols), lambda i: (0, 0)), pl.BlockSpec((N_CHIPS - 1, rows, cols), lambda i: (0, 0, 0))],
        out_specs=pl.BlockSpec((rows, cols), lambda i: (0, 0)),
        out_shape=jax.ShapeDtypeStruct((rows, cols), F32),
        compiler_params=pltpu.CompilerParams(vmem_limit_bytes=_vmem_limit(3 * rows * cols * 4)),
    )(own, landed)


def _halves_to_full(halves):
    n = len(halves)

    def body(*refs):
        ins, outs = refs[:n], refs[n:2 * n]
        send_sems, recv_sems, local_sems = refs[2 * n:]
        x, y, c = _mesh_pos()
        copies, locals_ = [], []
        for k in range(n):
            half = halves[k].shape[0]
            rows = outs[k].at[pl.ds(c * half, half), :]
            local = pltpu.make_async_copy(ins[k], rows, local_sems.at[k])
            local.start()
            locals_.append(local)
            cp = pltpu.make_async_remote_copy(
                src_ref=ins[k], dst_ref=rows, send_sem=send_sems.at[k], recv_sem=recv_sems.at[k],
                device_id=(x, y, 1 - c), device_id_type=MESH)
            cp.start()
            copies.append(cp)
        for k in range(n):
            half = halves[k].shape[0]
            theirs = outs[k].at[pl.ds((1 - c) * half, half), :]
            pltpu.make_async_remote_copy(
                src_ref=ins[k], dst_ref=theirs, send_sem=send_sems.at[k], recv_sem=recv_sems.at[k],
                device_id=(x, y, 1 - c), device_id_type=MESH).wait_recv()
        for cp in copies:
            cp.wait_send()
        for local in locals_:
            local.wait()

    return pl.pallas_call(
        body, name="grad_halves_to_full",
        in_specs=_hbm_specs(n), out_specs=_hbm_specs(n),
        out_shape=[jax.ShapeDtypeStruct((2 * h.shape[0], h.shape[1]), F32) for h in halves],
        scratch_shapes=[pltpu.SemaphoreType.DMA((n,)), pltpu.SemaphoreType.DMA((n,)), pltpu.SemaphoreType.DMA((n,))],
    )(*halves)


def _all_sum_small(v):
    rows, cols = v.shape
    n_dev = 8

    def body(v_ref, out_ref, buf, send_sems, recv_sems):
        x, y, c = _mesh_pos()
        me = 4 * x + 2 * y + c
        buf[me] = v_ref[...]
        peers = []
        for r in range(1, n_dev):
            px = 1 - x if r & 4 else x
            py = 1 - y if r & 2 else y
            pc = 1 - c if r & 1 else c
            peers.append((px, py, pc))
        copies = []
        for r, peer in enumerate(peers):
            cp = pltpu.make_async_remote_copy(
                src_ref=v_ref, dst_ref=buf.at[me], send_sem=send_sems.at[r], recv_sem=recv_sems.at[r],
                device_id=peer, device_id_type=MESH)
            cp.start()
            copies.append(cp)
        for r, (px, py, pc) in enumerate(peers):
            pltpu.make_async_remote_copy(
                src_ref=v_ref, dst_ref=buf.at[4 * px + 2 * py + pc], send_sem=send_sems.at[r], recv_sem=recv_sems.at[r],
                device_id=(px, py, pc), device_id_type=MESH).wait_recv()
        for cp in copies:
            cp.wait_send()
        acc = buf[0]
        for d in range(1, n_dev):
            acc = acc + buf[d]
        out_ref[...] = acc
        out_ref[3:4, :] = jnp.broadcast_to(jnp.sum(acc[3:4, :], axis=1, keepdims=True), (1, cols))

    vm = pl.BlockSpec(memory_space=pltpu.VMEM)
    return pl.pallas_call(
        body, name="all_sum_small", in_specs=[vm], out_specs=vm,
        out_shape=jax.ShapeDtypeStruct((rows, cols), F32),
        scratch_shapes=[pltpu.VMEM((n_dev, rows, cols), F32),
                        pltpu.SemaphoreType.DMA((n_dev - 1,)), pltpu.SemaphoreType.DMA((n_dev - 1,))],
    )(v)


def _adamw_math(w, g, m, v):
    m = ADAM_B1 * m + (1.0 - ADAM_B1) * g
    v = ADAM_B2 * v + (1.0 - ADAM_B2) * (g * g)
    m_hat = m / (1.0 - ADAM_B1 ** ADAM_STEP)
    v_hat = v / (1.0 - ADAM_B2 ** ADAM_STEP)
    delta = -ADAM_LR * (m_hat / (jnp.sqrt(v_hat) + ADAM_EPS) + ADAM_WD * w)
    return delta, m, v


def _adamw(w, g, m, v, name):
    rows, cols = w.shape
    tm = rows // 2 if (rows // 2) % 8 == 0 else rows
    return _rowwise(_adamw_math, [w, g, m, v], [], [(cols, F32)] * 3, [], tm=tm, name=name)


def _unshard_cols(gathered):
    n, r, c = gathered.shape
    return jnp.transpose(gathered, (1, 0, 2)).reshape(r, n * c)


def _shard_cols(full):
    r, nc = full.shape
    return jnp.transpose(full.reshape(r, N_CHIPS, nc // N_CHIPS), (1, 0, 2))


def _fwd_bwd(x, loss_target, g_mix, g_ffn, g_fin, wf_in, wf_sb_up, wf_dil_up, wf_out, wf_ffn_in, wf_ffn_out):
    b_sz, s_len, d_model = x.shape
    t = b_sz * s_len
    d_ff = wf_ffn_out.shape[0]
    x2d = x.reshape(t, d_model)
    tgt2d = loss_target.reshape(t, d_model)
    wf_qkv, wf_gate = wf_in[:, :QKV_WIDTH], wf_in[:, QKV_WIDTH:]

    (u,) = _rowwise(lambda xv, g: (_rms_stats(xv)[0] * g,), [x2d], [g_mix], [(d_model, BF16)], [], tm=512, name="norm_mix")
    qkv = _mm(u, wf_qkv, tm=1024, tn=768, tk=d_model, name="proj_qkv")
    gates = _mm(u, wf_gate, tm=1024, tn=512, tk=d_model, name="proj_gates")
    qkv3 = qkv.reshape(b_sz, s_len, QKV_WIDTH)
    o_sb = _sb_fwd(qkv3, b_sz, s_len)
    o_dl, lse = _dil_fwd(qkv3, b_sz, s_len)
    o_sb2, o_dl2 = o_sb.reshape(t, SB_WIDTH), o_dl.reshape(t, DIL_OUT_WIDTH)
    y_sb = _mm(o_sb2, wf_sb_up, tm=1024, tn=1024, tk=SB_WIDTH, name="sb_up")
    y_dl = _mm(o_dl2, wf_dil_up, tm=1024, tn=1024, tk=DIL_OUT_WIDTH, name="dil_up")

    def merge_fn(gt, ys, yd):
        return (_sigmoid(gt[:, :d_model]) * ys + _sigmoid(gt[:, d_model:]) * yd,)

    (merged,) = _rowwise(merge_fn, [gates, y_sb, y_dl], [], [(d_model, BF16)], [], tm=512, name="merge")
    x1 = _mm(merged, wf_out, add=x2d, tm=512, tn=1024, tk=d_model, name="mix_out")
    (u2,) = _rowwise(lambda xv, g: (_rms_stats(xv)[0] * g,), [x1], [g_ffn], [(d_model, BF16)], [], tm=512, name="norm_ffn")
    h = _mm(u2, wf_ffn_in, tm=1024, tn=512, tk=d_model, name="ffn_in")

    def act_fn(hv):
        gate = hv[:, :d_ff]
        return (gate * _sigmoid(gate) * hv[:, d_ff:],)

    (act,) = _rowwise(act_fn, [h], [], [(d_ff, BF16)], [], tm=256, name="ffn_act")
    x2 = _mm(act, wf_ffn_out, add=x1, tm=512, tn=1024, tk=d_ff, name="ffn_out")

    def head_fn(xv, tg, g):
        xhat, r = _rms_stats(xv)
        err = xhat * g - tg
        dy = err * (1.0 / d_model)
        dx, dg_rows = _rms_bwd(dy, xhat, r, g)
        loss_lanes = (0.5 / d_model) * jnp.sum(err * err, axis=0, keepdims=True)
        return dx, jnp.sum(dg_rows, axis=0, keepdims=True), loss_lanes

    dx2, dg_fin, loss_lanes = _rowwise(head_fn, [x2, tgt2d], [g_fin], [(d_model, F32)], [(1, d_model), (1, d_model)],
                                       tm=512, name="loss_head")

    dact = _mm(dx2, wf_ffn_out, tb=True, tm=512, tn=d_ff // 2, tk=d_model, name="ffn_out_dx")
    gw_ffn_out = _mm(act, dx2, ta=True, tm=d_ff // 2, tn=1024, tk=512, name="ffn_out_dw")

    def dact_fn(hv, da):
        gate, up = hv[:, :d_ff], hv[:, d_ff:]
        sg = _sigmoid(gate)
        dgate = da * up * (sg * (1.0 + gate * (1.0 - sg)))
        return (jnp.concatenate([dgate, da * (gate * sg)], axis=1),)

    (dh,) = _rowwise(dact_fn, [h, dact], [], [(2 * d_ff, BF16)], [], tm=256, name="ffn_act_bwd")
    du2 = _mm(dh, wf_ffn_in, tb=True, tm=1024, tn=1024, tk=512, name="ffn_in_dx")
    gw_ffn_in = _mm(u2, dh, ta=True, tm=1024, tn=512, tk=1024, name="ffn_in_dw")

    def norm_bwd_fn(dres, du_, xv, g):
        xhat, r = _rms_stats(xv)
        dx, dg_rows = _rms_bwd(du_, xhat, r, g)
        return dres + dx, jnp.sum(dg_rows, axis=0, keepdims=True)

    dx1, dg_ffn = _rowwise(norm_bwd_fn, [dx2, du2, x1], [g_ffn], [(d_model, F32)], [(1, d_model)], tm=512, name="norm_ffn_bwd")

    dmerged = _mm(dx1, wf_out, tb=True, tm=512, tn=1024, tk=d_model, name="mix_out_dx")
    gw_out = _mm(merged, dx1, ta=True, tm=1024, tn=512, tk=512, name="mix_out_dw")

    def merge_bwd_fn(gt, ys, yd, dm):
        s_sb, s_dl = _sigmoid(gt[:, :d_model]), _sigmoid(gt[:, d_model:])
        dgates = jnp.concatenate([dm * ys * s_sb * (1.0 - s_sb), dm * yd * s_dl * (1.0 - s_dl)], axis=1)
        return dgates, dm * s_sb, dm * s_dl

    dgates, dy_sb, dy_dl = _rowwise(merge_bwd_fn, [gates, y_sb, y_dl, dmerged], [],
                                    [(2 * d_model, BF16), (d_model, BF16), (d_model, BF16)], [], tm=256, name="merge_bwd")
    do_sb = _mm(dy_sb, wf_sb_up, tb=True, tm=1024, tn=SB_WIDTH, tk=d_model, name="sb_up_dx")
    gw_sb_up = _mm(o_sb2, dy_sb, ta=True, tm=SB_WIDTH, tn=1024, tk=512, name="sb_up_dw")
    do_dl = _mm(dy_dl, wf_dil_up, tb=True, tm=1024, tn=DIL_OUT_WIDTH, tk=d_model, name="dil_up_dx")
    gw_dil_up = _mm(o_dl2, dy_dl, ta=True, tm=DIL_OUT_WIDTH, tn=1024, tk=512, name="dil_up_dw")
    dq_sb, dk_sb, dv_sb = _sb_bwd(qkv3, o_sb, do_sb.reshape(b_sz, s_len, SB_WIDTH), b_sz, s_len)
    dq_dl, dk_dl, dv_dl = _dil_bwd(qkv3, o_dl, lse, do_dl.reshape(b_sz, s_len, DIL_OUT_WIDTH), b_sz, s_len)
    dproj = jnp.concatenate(
        [a.reshape(t, -1) for a in (dq_sb, dk_sb, dv_sb)]
        + [a.reshape(t, -1).astype(BF16) for a in (dq_dl, dk_dl, dv_dl)] + [dgates], axis=1)
    du = _mm(dproj, wf_in, tb=True, tm=512, tn=1024, tk=wf_in.shape[1] // 2, name="proj_dx")
    gw_in = _mm(u, dproj, ta=True, tm=512, tn=wf_in.shape[1] // 2, tk=512, name="proj_dw")
    dx, dg_mix = _rowwise(norm_bwd_fn, [dx1, du, x2d], [g_mix], [(d_model, F32)], [(1, d_model)], tm=512, name="norm_mix_bwd")

    return dx, [gw_in, gw_sb_up, gw_dil_up, gw_out, gw_ffn_in, gw_ffn_out], dg_mix, dg_ffn, dg_fin, loss_lanes


def kernel(x, norm_mix_g, w_in, w_sb_up, w_dil_up, w_out, norm_ffn_g, w_ffn_in, w_ffn_out, norm_final_g, loss_target, m_norm_mix_g, m_w_in, m_w_sb_up, m_w_dil_up, m_w_out, m_norm_ffn_g, m_w_ffn_in, m_w_ffn_out, m_norm_final_g, v_norm_mix_g, v_w_in, v_w_sb_up, v_w_dil_up, v_w_out, v_norm_ffn_g, v_w_ffn_in, v_w_ffn_out, v_norm_final_g):
    b_sz, s_len, d_model = x.shape
    d_ff = w_ffn_out.shape[1] * N_CHIPS
    g_mix, g_ffn, g_fin = norm_mix_g, norm_ffn_g, norm_final_g.reshape(1, d_model)

    names = ["w_in", "w_sb_up", "w_dil_up", "w_out", "w_ffn_in", "w_ffn_out"]
    shards = {"w_in": w_in[0], "w_sb_up": w_sb_up[0], "w_dil_up": w_dil_up[0], "w_out": w_out[0],
              "w_ffn_in": w_ffn_in[0], "w_ffn_out": w_ffn_out[0]}
    shards_bf16 = [_rowwise(lambda w: (w,), [shards[k]], [], [(shards[k].shape[1], BF16)], [],
                            tm=shards[k].shape[0], name="cast_" + k)[0] for k in names]
    gathered = dict(zip(names, _gather_weights(shards_bf16)))
    wf_in = _unshard_cols(gathered["w_in"])
    wf_sb_up = _unshard_cols(gathered["w_sb_up"])
    wf_dil_up = _unshard_cols(gathered["w_dil_up"])
    wf_out = gathered["w_out"].reshape(d_model, d_model)
    wf_ffn_in = _unshard_cols(gathered["w_ffn_in"])
    wf_ffn_out = gathered["w_ffn_out"].reshape(d_ff, d_model)

    dx, gws, dg_mix, dg_ffn, dg_fin, loss_lanes = _fwd_bwd(
        x, loss_target, g_mix, g_ffn, g_fin, wf_in, wf_sb_up, wf_dil_up, wf_out, wf_ffn_in, wf_ffn_out)
    gw_in, gw_sb_up, gw_dil_up, gw_out, gw_ffn_in, gw_ffn_out = gws

    full_grads = [_shard_cols(gw_in), _shard_cols(gw_sb_up), _shard_cols(gw_dil_up),
                  gw_out.reshape(N_CHIPS, d_model // N_CHIPS, d_model), _shard_cols(gw_ffn_in),
                  gw_ffn_out.reshape(N_CHIPS, d_ff // N_CHIPS, d_model)]
    others = _pair_exchange(full_grads)
    pair = [_pair_sum(g, o, "grad_pair_sum_" + k) for g, o, k in zip(full_grads, others, names)]
    exchanged = _chip_exchange([p[0] for p in pair], [p[1] for p in pair])
    owns, landed = exchanged[:len(names)], exchanged[len(names):]
    halves = [_chip_sum(o, l, "grad_chip_sum_" + k) for o, l, k in zip(owns, landed, names)]
    grads = dict(zip(names, _halves_to_full(halves)))

    small = jnp.concatenate([dg_mix, dg_ffn, dg_fin, loss_lanes, jnp.zeros((4, d_model), F32)], axis=0)
    small = _all_sum_small(small)
    loss = small[3, 0]
    gains = jnp.concatenate([g_mix, g_ffn, g_fin, jnp.zeros((5, d_model), F32)], axis=0)
    gains_m = jnp.concatenate([m_norm_mix_g, m_norm_ffn_g, m_norm_final_g.reshape(1, d_model), jnp.zeros((5, d_model), F32)], axis=0)
    gains_v = jnp.concatenate([v_norm_mix_g, v_norm_ffn_g, v_norm_final_g.reshape(1, d_model), jnp.ones((5, d_model), F32)], axis=0)
    gd, gm, gv = _rowwise(_adamw_math, [gains, small, gains_m, gains_v], [], [(d_model, F32)] * 3, [], tm=8, name="adamw_gains")

    moments = {"w_in": (m_w_in, v_w_in), "w_sb_up": (m_w_sb_up, v_w_sb_up), "w_dil_up": (m_w_dil_up, v_w_dil_up),
               "w_out": (m_w_out, v_w_out), "w_ffn_in": (m_w_ffn_in, v_w_ffn_in), "w_ffn_out": (m_w_ffn_out, v_w_ffn_out)}
    upd = {k: _adamw(shards[k], grads[k], moments[k][0][0], moments[k][1][0], "adamw_" + k) for k in names}

    def w_out_of(i):
        return [upd[k][i][None] for k in names]

    def ordered(mix, ws, ffn_g, fin):
        return [mix, ws[0], ws[1], ws[2], ws[3], ffn_g, ws[4], ws[5], fin]

    grad_ws = [grads[k][None] for k in names]
    outs = [loss, dx.reshape(b_sz, s_len, d_model)]
    outs += ordered(small[0:1], grad_ws, small[1:2], small[2])
    outs += ordered(gd[0:1], w_out_of(0), gd[1:2], gd[2])
    outs += ordered(gm[0:1], w_out_of(1), gm[1:2], gm[2])
    outs += ordered(gv[0:1], w_out_of(2), gv[1:2], gv[2])
    return tuple(outs)
```

```python
import functools
import math

import jax
import jax.numpy as jnp
from jax import lax
from jax.experimental import pallas as pl
from jax.experimental.pallas import tpu as pltpu

F32 = jnp.float32
BF16 = jnp.bfloat16
MESH = pl.DeviceIdType.MESH

HEAD_DIM = 64
SB_HEADS = 8
DIL_PAIRS = ((128, 1), (512, 4), (2048, 16))
DIL_HEADS_PER_GROUP = 4
DIL_HEADS = DIL_HEADS_PER_GROUP * len(DIL_PAIRS)
SB_WIDTH = SB_HEADS * HEAD_DIM
DIL_WIDTH = DIL_HEADS * HEAD_DIM
DIL_OUT_WIDTH = DIL_HEADS_PER_GROUP * HEAD_DIM
QKV_WIDTH = 3 * SB_WIDTH + 3 * DIL_WIDTH
RMS_EPS = 1e-6
ALIBI_MAX_BIAS = 8.0
ADAM_LR = 0.001
ADAM_B1 = 0.9
ADAM_B2 = 0.999
ADAM_EPS = 1e-08
ADAM_WD = 0.01
ADAM_STEP = 10

LANES = 128
BLK = 128
NEG = -1e30
N_CHIPS = 4
VMEM_CAP = 56 * 1024 * 1024


def _vmem_limit(tile_bytes):
    return int(min(VMEM_CAP, max(32 * 1024 * 1024, 3 * tile_bytes + 8 * 1024 * 1024)))


def _nbytes(shape, dtype):
    return math.prod(shape) * jnp.dtype(dtype).itemsize


def _dot(a, b):
    return jnp.dot(a, b, preferred_element_type=F32)


def _dot_nt(a, b):
    return lax.dot_general(a, b, (((1,), (1,)), ((), ())), preferred_element_type=F32)


def _dot_tn(a, b):
    return lax.dot_general(a, b, (((0,), (0,)), ((), ())), preferred_element_type=F32)


def _split2(x):
    hi = x.astype(BF16)
    lo = (x - hi.astype(F32)).astype(BF16)
    return hi, lo


def _sigmoid(x):
    return 1.0 / (1.0 + jnp.exp(-x))


def _mm(a, b, *, ta=False, tb=False, add=None, out_dtype=F32, tm, tn, tk, name):
    if ta:
        kdim, m = a.shape
    else:
        m, kdim = a.shape
    if tb:
        n, k2 = b.shape
    else:
        k2, n = b.shape
    assert kdim == k2 and m % tm == 0 and n % tn == 0 and kdim % tk == 0, (name, a.shape, b.shape)
    nk = kdim // tk
    a_spec = pl.BlockSpec((tk, tm), lambda i, j, k: (k, i)) if ta else pl.BlockSpec((tm, tk), lambda i, j, k: (i, k))
    b_spec = pl.BlockSpec((tn, tk), lambda i, j, k: (j, k)) if tb else pl.BlockSpec((tk, tn), lambda i, j, k: (k, j))
    o_spec = pl.BlockSpec((tm, tn), lambda i, j, k: (i, j))
    dims = ((((0,) if ta else (1,)), ((1,) if tb else (0,))), ((), ()))
    has_add = add is not None
    use_scratch = nk > 1 and jnp.dtype(out_dtype) != jnp.dtype(F32)

    def body(*refs):
        a_ref, b_ref = refs[0], refs[1]
        add_ref = refs[2] if has_add else None
        o_ref = refs[3] if has_add else refs[2]
        acc_ref = refs[-1] if use_scratch else o_ref
        prod = lax.dot_general(a_ref[...].astype(BF16), b_ref[...].astype(BF16), dims, preferred_element_type=F32)
        if nk == 1:
            if has_add:
                prod = prod + add_ref[...]
            o_ref[...] = prod.astype(out_dtype)
            return
        k = pl.program_id(2)

        @pl.when(k == 0)
        def _():
            acc_ref[...] = prod + add_ref[...] if has_add else prod

        @pl.when(k > 0)
        def _():
            acc_ref[...] += prod

        if use_scratch:
            @pl.when(k == nk - 1)
            def _():
                o_ref[...] = acc_ref[...].astype(out_dtype)

    tile_bytes = (_nbytes((tm, tk), a.dtype) + _nbytes((tk, tn), b.dtype) + _nbytes((tm, tn), out_dtype)
                  + (_nbytes((tm, tn), F32) if has_add else 0) + _nbytes((tm, tn), F32))
    in_specs = [a_spec, b_spec] + ([o_spec] if has_add else [])
    args = (a, b) + ((add,) if has_add else ())
    return pl.pallas_call(
        body, name=name, grid=(m // tm, n // tn, nk),
        in_specs=in_specs, out_specs=o_spec,
        out_shape=jax.ShapeDtypeStruct((m, n), out_dtype),
        scratch_shapes=[pltpu.VMEM((tm, tn), F32)] if use_scratch else [],
        compiler_params=pltpu.CompilerParams(
            dimension_semantics=("parallel", "parallel", "arbitrary"),
            vmem_limit_bytes=_vmem_limit(tile_bytes)),
    )(*args)


def _rowwise(fn, rows, params, outs, accs, *, tm, name):
    t = rows[0].shape[0]
    assert t % tm == 0, (name, t, tm)
    n_r, n_p, n_o = len(rows), len(params), len(outs)

    def body(*refs):
        vals = [r[...] for r in refs[:n_r + n_p]]
        res = fn(*vals)
        o_refs = refs[n_r + n_p:n_r + n_p + n_o]
        a_refs = refs[n_r + n_p + n_o:]
        for o_ref, v in zip(o_refs, res[:n_o]):
            o_ref[...] = v.astype(o_ref.dtype)
        if accs:
            @pl.when(pl.program_id(0) == 0)
            def _():
                for a_ref in a_refs:
                    a_ref[...] = jnp.zeros(a_ref.shape, F32)

            for a_ref, v in zip(a_refs, res[n_o:]):
                a_ref[...] += v

    in_specs = [pl.BlockSpec((tm, r.shape[1]), lambda i: (i, 0)) for r in rows]
    in_specs += [pl.BlockSpec(p.shape, lambda i: (0, 0)) for p in params]
    out_specs = [pl.BlockSpec((tm, w), lambda i: (i, 0)) for w, _ in outs]
    out_specs += [pl.BlockSpec(s, lambda i: (0, 0)) for s in accs]
    out_shape = [jax.ShapeDtypeStruct((t, w), d) for w, d in outs]
    out_shape += [jax.ShapeDtypeStruct(s, F32) for s in accs]
    tile_bytes = sum(_nbytes((tm, r.shape[1]), r.dtype) for r in rows) + sum(_nbytes((tm, w), F32) for w, _ in outs)
    res = pl.pallas_call(
        body, name=name, grid=(t // tm,), in_specs=in_specs, out_specs=out_specs, out_shape=out_shape,
        compiler_params=pltpu.CompilerParams(
            dimension_semantics=("arbitrary",) if accs else ("parallel",),
            vmem_limit_bytes=_vmem_limit(2 * tile_bytes)),
    )(*rows, *params)
    return res


def _rms_stats(x):
    r = lax.rsqrt(jnp.mean(x * x, axis=-1, keepdims=True) + RMS_EPS)
    return x * r, r


def _rms_bwd(dy, xhat, r, g):
    dxhat = dy * g
    dx = r * (dxhat - xhat * jnp.mean(dxhat * xhat, axis=-1, keepdims=True))
    return dx, dy * xhat


def _sb_consts():
    lane = lax.broadcasted_iota(jnp.int32, (BLK, LANES), 1)
    head0 = lane < HEAD_DIM
    row = lax.broadcasted_iota(jnp.int32, (2 * BLK, BLK), 0) % BLK
    col = lax.broadcasted_iota(jnp.int32, (2 * BLK, BLK), 1)
    causal = col < row
    jj = lax.broadcasted_iota(jnp.int32, (BLK, 2 * BLK), 0)
    ss = lax.broadcasted_iota(jnp.int32, (BLK, 2 * BLK), 1)
    suffix = jnp.where((ss >= BLK) | (jj > ss), 1.0, 0.0).astype(BF16)
    return head0, causal, suffix


def _stack_heads(x, head0):
    zero = jnp.zeros_like(x)
    return jnp.concatenate([jnp.where(head0, x, zero), jnp.where(head0, zero, x)], axis=0)


def _sb_logits(q01, k, causal, masked):
    z = _dot_nt(q01, k) * (1.0 / math.sqrt(HEAD_DIM))
    sp = jnp.log(1.0 + jnp.exp(-jnp.abs(z)))
    log_keep = -(jnp.maximum(z, 0.0) + sp)
    log_beta = jnp.minimum(z, 0.0) - sp
    if masked:
        log_keep = jnp.where(causal, log_keep, 0.0)
    return log_keep, log_beta


def _suffix_sums(x, suffix):
    hi, lo = _split2(x)
    s2 = _dot(hi, suffix) + _dot(lo, suffix)
    return s2[:, :BLK], s2[:, BLK:]


def _sb_fwd(qkv, b_sz, s_len):
    nq = s_len // BLK
    n_pairs = SB_WIDTH // LANES

    def body(q_ref, k_ref, v_ref, o_ref):
        head0, causal, suffix = _sb_consts()

        def q_block(i, _):
            qs = pl.multiple_of(i * BLK, BLK)
            q01 = _stack_heads(q_ref[pl.ds(qs, BLK), :].astype(BF16), head0)

            def tile(j, carry, acc, masked):
                ks = pl.multiple_of(j * BLK, BLK)
                k = k_ref[pl.ds(ks, BLK), :].astype(BF16)
                v = v_ref[pl.ds(ks, BLK), :].astype(BF16)
                log_keep, log_beta = _sb_logits(q01, k, causal, masked)
                after, total = _suffix_sums(log_keep, suffix)
                a = jnp.exp(log_beta + carry + after)
                if masked:
                    a = jnp.where(causal, a, 0.0)
                a_hi, a_lo = _split2(a)
                a_cat = jnp.concatenate([a_hi[:BLK], a_hi[BLK:], a_lo[:BLK], a_lo[BLK:]], axis=1)
                v01 = _stack_heads(v, head0)
                acc = acc + _dot(a_cat, jnp.concatenate([v01, v01], axis=0))
                return carry + total, acc

            carry = jnp.zeros((2 * BLK, BLK), F32)
            acc = jnp.zeros((BLK, LANES), F32)
            carry, acc = tile(i, carry, acc, True)

            def inner(t, ca):
                return tile(i - 1 - t, ca[0], ca[1], False)

            carry, acc = lax.fori_loop(0, i, inner, (carry, acc))
            o_ref[pl.ds(qs, BLK), :] = acc
            return 0

        lax.fori_loop(0, nq, q_block, 0)

    blk = lambda off: pl.BlockSpec((None, s_len, LANES), lambda b, p: (b, 0, off + p))
    return pl.pallas_call(
        body, name="sb_fwd", grid=(b_sz, n_pairs),
        in_specs=[blk(0), blk(n_pairs), blk(2 * n_pairs)],
        out_specs=pl.BlockSpec((None, s_len, LANES), lambda b, p: (b, 0, p)),
        out_shape=jax.ShapeDtypeStruct((b_sz, s_len, SB_WIDTH), F32),
        compiler_params=pltpu.CompilerParams(dimension_semantics=("parallel", "parallel"),
                                             vmem_limit_bytes=48 * 1024 * 1024),
    )(qkv, qkv, qkv)


def _sb_bwd(qkv, o_sb, do_sb, b_sz, s_len):
    nq = s_len // BLK
    n_pairs = SB_WIDTH // LANES

    def body(q_ref, k_ref, v_ref, o_ref, do_ref, dq_ref, dk_ref, dv_ref, dk_acc, dv_acc):
        head0, causal, suffix = _sb_consts()
        lrow = lax.broadcasted_iota(jnp.int32, (LANES, LANES), 0)
        ones_h0 = jnp.where(lrow < HEAD_DIM, 1.0, 0.0).astype(BF16)
        ones_h1 = jnp.where(lrow >= HEAD_DIM, 1.0, 0.0).astype(BF16)
        dk_acc[...] = jnp.zeros(dk_acc.shape, F32)
        dv_acc[...] = jnp.zeros(dv_acc.shape, F32)

        def q_block(i, _):
            qs = pl.multiple_of(i * BLK, BLK)
            q01 = _stack_heads(q_ref[pl.ds(qs, BLK), :].astype(BF16), head0)
            do_b = do_ref[pl.ds(qs, BLK), :].astype(BF16)
            do01 = _stack_heads(do_b, head0)
            dd_hi, dd_lo = _split2(do_b.astype(F32) * o_ref[pl.ds(qs, BLK), :])
            tot = jnp.concatenate([_dot(dd_hi, ones_h0) + _dot(dd_lo, ones_h0),
                                   _dot(dd_hi, ones_h1) + _dot(dd_lo, ones_h1)], axis=0)

            def tile(j, carry, rcarry, dq, masked):
                ks = pl.multiple_of(j * BLK, BLK)
                k = k_ref[pl.ds(ks, BLK), :].astype(BF16)
                v = v_ref[pl.ds(ks, BLK), :].astype(BF16)
                log_keep, log_beta = _sb_logits(q01, k, causal, masked)
                after, total = _suffix_sums(log_keep, suffix)
                a = jnp.exp(log_beta + carry + after)
                if masked:
                    a = jnp.where(causal, a, 0.0)
                e = a * _dot_nt(do01, v)
                e_after, e_total = _suffix_sums(e, suffix)
                before = tot - (rcarry + e_after + e)
                beta = jnp.exp(log_beta)
                dz = (e * (1.0 - beta) - beta * before) * (1.0 / math.sqrt(HEAD_DIM))
                if masked:
                    dz = jnp.where(causal, dz, 0.0)
                dz_b = dz.astype(BF16)
                dq = dq + _dot(dz_b, k)
                dk_acc[pl.ds(ks, BLK), :] += _dot_tn(dz_b, q01)
                dv_acc[pl.ds(ks, BLK), :] += _dot_tn(a.astype(BF16), do01)
                return carry + total, rcarry + e_total, dq

            zero = jnp.zeros((2 * BLK, BLK), F32)
            carry, rcarry, dq = tile(i, zero, zero, zero, True)

            def inner(t, c):
                return tile(i - 1 - t, c[0], c[1], c[2], False)

            carry, rcarry, dq = lax.fori_loop(0, i, inner, (carry, rcarry, dq))
            dq_ref[pl.ds(qs, BLK), :] = jnp.where(head0, dq[:BLK], dq[BLK:]).astype(dq_ref.dtype)
            return 0

        lax.fori_loop(0, nq, q_block, 0)
        dk_ref[...] = dk_acc[...].astype(dk_ref.dtype)
        dv_ref[...] = dv_acc[...].astype(dv_ref.dtype)

    blk = lambda off: pl.BlockSpec((None, s_len, LANES), lambda b, p: (b, 0, off + p))
    out_sd = jax.ShapeDtypeStruct((b_sz, s_len, SB_WIDTH), BF16)
    return pl.pallas_call(
        body, name="sb_bwd", grid=(b_sz, n_pairs),
        in_specs=[blk(0), blk(n_pairs), blk(2 * n_pairs), blk(0), blk(0)],
        out_specs=[blk(0), blk(0), blk(0)],
        out_shape=[out_sd, out_sd, out_sd],
        scratch_shapes=[pltpu.VMEM((s_len, LANES), F32), pltpu.VMEM((s_len, LANES), F32)],
        compiler_params=pltpu.CompilerParams(dimension_semantics=("parallel", "parallel"),
                                             vmem_limit_bytes=48 * 1024 * 1024),
    )(qkv, qkv, qkv, o_sb, do_sb)


def _dil_consts(group, pair_idx, dilation):
    lane = lax.broadcasted_iota(jnp.int32, (BLK, LANES), 1)
    head0 = lane < HEAD_DIM
    row = lax.broadcasted_iota(jnp.int32, (2 * BLK, BLK), 0)
    qa = row % BLK
    kb = lax.broadcasted_iota(jnp.int32, (2 * BLK, BLK), 1)
    head = (group * DIL_HEADS_PER_GROUP + 2 * pair_idx + row // BLK).astype(F32)
    slope = jnp.exp((-ALIBI_MAX_BIAS * math.log(2.0) / DIL_HEADS) * (head + 1.0))
    valid_cur = kb <= qa
    valid_prev = kb >= qa
    bias_cur = -slope * ((qa - kb) * dilation).astype(F32)
    bias_prev = -slope * ((BLK + qa - kb) * dilation).astype(F32)
    return head0, valid_cur, valid_prev, bias_cur, bias_prev


def _dil_units(s_len, dilation):
    nb = s_len // dilation // BLK
    return [(r, n) for r in range(dilation) for n in range(nb)]


def _dil_rows(n, r, dilation):
    if dilation == 1:
        return pl.ds(n * BLK, BLK)
    return pl.ds(n * BLK * dilation + r, BLK, stride=dilation)


def _dil_scores(q01, k, bias, valid):
    s = _dot_nt(q01, k) * (1.0 / math.sqrt(HEAD_DIM)) + bias
    return jnp.where(valid, s, NEG)


def _dil_fwd(qkv, b_sz, s_len):
    n_pairs = DIL_OUT_WIDTH // LANES
    q_off = 3 * SB_WIDTH // LANES
    per_kind = DIL_WIDTH // LANES

    def body(*refs):
        qkv_refs = refs[:9]
        o_ref, lse_ref, m_s, l_s = refs[9:]
        pair_idx = pl.program_id(1)
        m_s[...] = jnp.full(m_s.shape, NEG, F32)
        l_s[...] = jnp.zeros(l_s.shape, F32)
        o_ref[...] = jnp.zeros(o_ref.shape, F32)
        for g, (_, dilation) in enumerate(DIL_PAIRS):
            q_ref, k_ref, v_ref = qkv_refs[3 * g:3 * g + 3]
            head0, valid_cur, valid_prev, bias_cur, bias_prev = _dil_consts(g, pair_idx, dilation)
            for r, n in _dil_units(s_len, dilation):
                rows = _dil_rows(n, r, dilation)
                q01 = _stack_heads(q_ref[rows, :].astype(BF16), head0)
                k_c = k_ref[rows, :].astype(BF16)
                v_c = v_ref[rows, :].astype(BF16)
                scores = [_dil_scores(q01, k_c, bias_cur, valid_cur)]
                values = [_stack_heads(v_c, head0)]
                if n > 0:
                    prev = _dil_rows(n - 1, r, dilation)
                    scores.append(_dil_scores(q01, k_ref[prev, :].astype(BF16), bias_prev, valid_prev))
                    values.append(_stack_heads(v_ref[prev, :].astype(BF16), head0))
                m_blk = functools.reduce(jnp.maximum, [jnp.max(s, axis=-1, keepdims=True) for s in scores])
                m_old = jnp.concatenate([m_s.at[0][rows, :], m_s.at[1][rows, :]], axis=0)
                l_old = jnp.concatenate([l_s.at[0][rows, :], l_s.at[1][rows, :]], axis=0)
                m_new = jnp.maximum(m_old, m_blk)
                probs = [jnp.exp(s - m_new) for s in scores]
                l_blk = functools.reduce(jnp.add, [jnp.sum(p, axis=-1, keepdims=True) for p in probs])
                alpha = jnp.exp(m_old - m_new)
                l_new = alpha * l_old + l_blk
                alpha_tok = jnp.where(head0, alpha[:BLK], alpha[BLK:])
                p_cat = jnp.concatenate([h for p in probs for h in (p[:BLK].astype(BF16), p[BLK:].astype(BF16))], axis=1)
                o_ref[rows, :] = alpha_tok * o_ref[rows, :] + _dot(p_cat, jnp.concatenate(values, axis=0))
                m_s.at[0][rows, :] = m_new[:BLK]
                m_s.at[1][rows, :] = m_new[BLK:]
                l_s.at[0][rows, :] = l_new[:BLK]
                l_s.at[1][rows, :] = l_new[BLK:]
        lane = lax.broadcasted_iota(jnp.int32, (BLK, LANES), 1)
        for c in range(s_len // BLK):
            rows = pl.ds(c * BLK, BLK)
            l0, l1 = l_s.at[0][rows, :], l_s.at[1][rows, :]
            o_ref[rows, :] = o_ref[rows, :] / jnp.where(lane < HEAD_DIM, l0, l1)
            lse_ref.at[0][rows, :] = m_s.at[0][rows, :] + jnp.log(l0)
            lse_ref.at[1][rows, :] = m_s.at[1][rows, :] + jnp.log(l1)

    in_specs = []
    for g in range(len(DIL_PAIRS)):
        for kind in range(3):
            off = q_off + kind * per_kind + g * n_pairs
            in_specs.append(pl.BlockSpec((None, s_len, LANES), lambda b, p, off=off: (b, 0, off + p)))
    return pl.pallas_call(
        body, name="dil_fwd", grid=(b_sz, n_pairs),
        in_specs=in_specs,
        out_specs=[pl.BlockSpec((None, s_len, LANES), lambda b, p: (b, 0, p)),
                   pl.BlockSpec((None, None, 2, s_len, LANES), lambda b, p: (b, p, 0, 0, 0))],
        out_shape=[jax.ShapeDtypeStruct((b_sz, s_len, DIL_OUT_WIDTH), F32),
                   jax.ShapeDtypeStruct((b_sz, n_pairs, 2, s_len, LANES), F32)],
        scratch_shapes=[pltpu.VMEM((2, s_len, LANES), F32), pltpu.VMEM((2, s_len, LANES), F32)],
        compiler_params=pltpu.CompilerParams(dimension_semantics=("parallel", "parallel"),
                                             vmem_limit_bytes=VMEM_CAP),
    )(*([qkv] * 9))


def _dil_bwd(qkv, o_dl, lse, do_dl, b_sz, s_len):
    n_pairs = DIL_OUT_WIDTH // LANES
    n_groups = len(DIL_PAIRS)
    q_off = 3 * SB_WIDTH // LANES
    per_kind = DIL_WIDTH // LANES

    def body(q_ref, k_ref, v_ref, o_ref, lse_ref, do_ref, dq_ref, dk_ref, dv_ref, d_s):
        pair_idx = pl.program_id(1)
        group = pl.program_id(2)
        lrow = lax.broadcasted_iota(jnp.int32, (LANES, LANES), 0)
        ones_h0 = jnp.where(lrow < HEAD_DIM, 1.0, 0.0).astype(BF16)
        ones_h1 = jnp.where(lrow >= HEAD_DIM, 1.0, 0.0).astype(BF16)
        for c in range(s_len // BLK):
            rows = pl.ds(c * BLK, BLK)
            dd_hi, dd_lo = _split2(do_ref[rows, :] * o_ref[rows, :])
            d_s.at[0][rows, :] = _dot(dd_hi, ones_h0) + _dot(dd_lo, ones_h0)
            d_s.at[1][rows, :] = _dot(dd_hi, ones_h1) + _dot(dd_lo, ones_h1)
        dk_ref[...] = jnp.zeros(dk_ref.shape, F32)
        dv_ref[...] = jnp.zeros(dv_ref.shape, F32)

        def one_group(g, dilation):
            head0, valid_cur, valid_prev, bias_cur, bias_prev = _dil_consts(g, pair_idx, dilation)
            for r, n in _dil_units(s_len, dilation):
                rows = _dil_rows(n, r, dilation)
                q01 = _stack_heads(q_ref[rows, :].astype(BF16), head0)
                do01 = _stack_heads(do_ref[rows, :].astype(BF16), head0)
                lse01 = jnp.concatenate([lse_ref.at[0][rows, :], lse_ref.at[1][rows, :]], axis=0)
                d01 = jnp.concatenate([d_s.at[0][rows, :], d_s.at[1][rows, :]], axis=0)
                dq = jnp.zeros((2 * BLK, LANES), F32)
                blocks = [(rows, bias_cur, valid_cur)]
                if n > 0:
                    blocks.append((_dil_rows(n - 1, r, dilation), bias_prev, valid_prev))
                for krows, bias, valid in blocks:
                    k = k_ref[krows, :].astype(BF16)
                    v = v_ref[krows, :].astype(BF16)
                    p = jnp.exp(_dil_scores(q01, k, bias, valid) - lse01)
                    ds = (p * (_dot_nt(do01, v) - d01) * (1.0 / math.sqrt(HEAD_DIM))).astype(BF16)
                    dq = dq + _dot(ds, k)
                    dk_ref[krows, :] = dk_ref[krows, :] + _dot_tn(ds, q01)
                    dv_ref[krows, :] = dv_ref[krows, :] + _dot_tn(p.astype(BF16), do01)
                dq_ref[rows, :] = jnp.where(head0, dq[:BLK], dq[BLK:])

        for g, (_, dilation) in enumerate(DIL_PAIRS):
            pl.when(group == g)(functools.partial(one_group, g, dilation))

    def qkv_spec(kind):
        return pl.BlockSpec((None, s_len, LANES),
                            lambda b, p, g: (b, 0, q_off + kind * per_kind + g * n_pairs + p))

    tok_spec = pl.BlockSpec((None, s_len, LANES), lambda b, p, g: (b, 0, p))
    out_spec = pl.BlockSpec((None, s_len, LANES), lambda b, p, g: (b, 0, g * n_pairs + p))
    out_sd = jax.ShapeDtypeStruct((b_sz, s_len, DIL_WIDTH), F32)
    return pl.pallas_call(
        body, name="dil_bwd", grid=(b_sz, n_pairs, n_groups),
        in_specs=[qkv_spec(0), qkv_spec(1), qkv_spec(2), tok_spec,
                  pl.BlockSpec((None, None, 2, s_len, LANES), lambda b, p, g: (b, p, 0, 0, 0)), tok_spec],
        out_specs=[out_spec, out_spec, out_spec],
        out_shape=[out_sd, out_sd, out_sd],
        scratch_shapes=[pltpu.VMEM((2, s_len, LANES), F32)],
        compiler_params=pltpu.CompilerParams(dimension_semantics=("parallel", "parallel", "arbitrary"),
                                             vmem_limit_bytes=VMEM_CAP),
    )(qkv, qkv, qkv, o_dl, lse, do_dl)


def _mesh_pos():
    return lax.axis_index("x"), lax.axis_index("y"), lax.axis_index("c")


def _other_chips(x, y):
    return [(1 - x, y), (x, 1 - y), (1 - x, 1 - y)]


def _hbm_specs(n):
    return [pl.BlockSpec(memory_space=pl.ANY)] * n


def _cast_to_slab(w, name):
    rows, cols = w.shape
    mine = jnp.reshape(2 * lax.axis_index("x") + lax.axis_index("y"), (1,)).astype(jnp.int32)

    def body(idx_ref, w_ref, o_ref):
        o_ref[...] = w_ref[...].astype(BF16)

    return pl.pallas_call(
        body, name=name,
        grid_spec=pltpu.PrefetchScalarGridSpec(
            num_scalar_prefetch=1, grid=(1,),
            in_specs=[pl.BlockSpec((rows, cols), lambda i, idx: (0, 0))],
            out_specs=pl.BlockSpec((None, rows, cols), lambda i, idx: (idx[0], 0, 0))),
        out_shape=jax.ShapeDtypeStruct((N_CHIPS, rows, cols), BF16),
        compiler_params=pltpu.CompilerParams(vmem_limit_bytes=_vmem_limit(rows * cols * 6)),
    )(mine, w)


def _gather_weights(slabs):
    n = len(slabs)

    def body(*refs):
        outs = refs[n:2 * n]
        send_sems, recv_sems = refs[2 * n:]
        x, y, c = _mesh_pos()
        mine = 2 * x + y
        chips = _other_chips(x, y)
        sends = []
        for k in range(n):
            half = slabs[k].shape[1] // 2
            rows = outs[k].at[mine, pl.ds(c * half, half), :]
            for r, (px, py) in enumerate(chips):
                cp = pltpu.make_async_remote_copy(
                    src_ref=rows, dst_ref=rows,
                    send_sem=send_sems.at[6 * k + r], recv_sem=recv_sems.at[6 * k + r],
                    device_id=(px, py, c), device_id_type=MESH)
                cp.start()
                sends.append(cp)
        for k in range(n):
            half = slabs[k].shape[1] // 2
            rows = pl.ds(c * half, half)
            for r, (px, py) in enumerate(chips):
                theirs = outs[k].at[2 * px + py, rows, :]
                pltpu.make_async_remote_copy(
                    src_ref=theirs, dst_ref=theirs, send_sem=send_sems.at[6 * k + r], recv_sem=recv_sems.at[6 * k + r],
                    device_id=(px, py, c), device_id_type=MESH).wait_recv()
                fwd = pltpu.make_async_remote_copy(
                    src_ref=theirs, dst_ref=theirs, send_sem=send_sems.at[6 * k + 3 + r], recv_sem=recv_sems.at[6 * k + 3 + r],
                    device_id=(x, y, 1 - c), device_id_type=MESH)
                fwd.start()
                sends.append(fwd)
        for k in range(n):
            half = slabs[k].shape[1] // 2
            sib_rows = pl.ds((1 - c) * half, half)
            for r, (px, py) in enumerate(chips):
                landed = outs[k].at[2 * px + py, sib_rows, :]
                pltpu.make_async_remote_copy(
                    src_ref=landed, dst_ref=landed, send_sem=send_sems.at[6 * k + 3 + r], recv_sem=recv_sems.at[6 * k + 3 + r],
                    device_id=(x, y, 1 - c), device_id_type=MESH).wait_recv()
        for cp in sends:
            cp.wait_send()

    return pl.pallas_call(
        body, name="gather_weights",
        in_specs=_hbm_specs(n), out_specs=_hbm_specs(n),
        out_shape=[jax.ShapeDtypeStruct(s.shape, s.dtype) for s in slabs],
        input_output_aliases={k: k for k in range(n)},
        scratch_shapes=[pltpu.SemaphoreType.DMA((6 * n,)), pltpu.SemaphoreType.DMA((6 * n,))],
    )(*slabs)


def _pair_exchange(grads):
    n = len(grads)

    def body(*refs):
        ins, outs = refs[:n], refs[n:2 * n]
        send_sems, recv_sems = refs[2 * n:]
        x, y, c = _mesh_pos()
        copies = []
        for k in range(n):
            half = grads[k].shape[1] // 2
            cp = pltpu.make_async_remote_copy(
                src_ref=ins[k].at[:, pl.ds((1 - c) * half, half), :], dst_ref=outs[k],
                send_sem=send_sems.at[k], recv_sem=recv_sems.at[k],
                device_id=(x, y, 1 - c), device_id_type=MESH)
            cp.start()
            copies.append(cp)
        for cp in copies:
            cp.wait()

    return pl.pallas_call(
        body, name="grad_pair_exchange",
        in_specs=_hbm_specs(n), out_specs=_hbm_specs(n),
        out_shape=[jax.ShapeDtypeStruct((N_CHIPS, g.shape[1] // 2, g.shape[2]), F32) for g in grads],
        scratch_shapes=[pltpu.SemaphoreType.DMA((n,)), pltpu.SemaphoreType.DMA((n,))],
    )(*grads)


def _pair_sum(grad, other, name):
    _, rows, cols = grad.shape
    half = rows // 2
    core = jnp.reshape(lax.axis_index("c"), (1,)).astype(jnp.int32)

    def body(core_ref, g_ref, p_ref, s_ref, sb_ref):
        s = g_ref[...] + p_ref[...]
        s_ref[...] = s
        sb_ref[...] = s.astype(BF16)

    blk = pl.BlockSpec((None, half, cols), lambda p, core_ref: (p, 0, 0))
    return pl.pallas_call(
        body, name=name,
        grid_spec=pltpu.PrefetchScalarGridSpec(
            num_scalar_prefetch=1, grid=(N_CHIPS,),
            in_specs=[pl.BlockSpec((None, half, cols), lambda p, core_ref: (p, core_ref[0], 0)), blk],
            out_specs=[blk, blk]),
        out_shape=[jax.ShapeDtypeStruct((N_CHIPS, half, cols), F32),
                   jax.ShapeDtypeStruct((N_CHIPS, half, cols), BF16)],
        compiler_params=pltpu.CompilerParams(dimension_semantics=("parallel",),
                                             vmem_limit_bytes=_vmem_limit(4 * half * cols * 4)),
    )(core, grad, other)


def _chip_exchange(sums_bf16):
    n = len(sums_bf16)

    def body(*refs):
        b_ins, lands = refs[:n], refs[n:2 * n]
        send_sems, recv_sems = refs[2 * n:]
        x, y, c = _mesh_pos()
        copies = []
        for k in range(n):
            for r, (px, py) in enumerate(_other_chips(x, y)):
                cp = pltpu.make_async_remote_copy(
                    src_ref=b_ins[k].at[2 * px + py], dst_ref=lands[k].at[r],
                    send_sem=send_sems.at[3 * k + r], recv_sem=recv_sems.at[3 * k + r],
                    device_id=(px, py, c), device_id_type=MESH)
                cp.start()
                copies.append(cp)
        for cp in copies:
            cp.wait()

    return pl.pallas_call(
        body, name="grad_chip_exchange",
        in_specs=_hbm_specs(n), out_specs=_hbm_specs(n),
        out_shape=[jax.ShapeDtypeStruct((N_CHIPS - 1,) + s.shape[1:], BF16) for s in sums_bf16],
        scratch_shapes=[pltpu.SemaphoreType.DMA((3 * n,)), pltpu.SemaphoreType.DMA((3 * n,))],
    )(*sums_bf16)


def _chip_sum(sums_f32, landed, name):
    _, rows, cols = sums_f32.shape
    x, y, c = _mesh_pos()
    idx = jnp.stack([2 * x + y, c]).astype(jnp.int32)

    def body(idx_ref, o_ref, l_ref, out_ref):
        out_ref[...] = ((o_ref[...] + l_ref[0].astype(F32)) + l_ref[1].astype(F32)) + l_ref[2].astype(F32)

    return pl.pallas_call(
        body, name=name,
        grid_spec=pltpu.PrefetchScalarGridSpec(
            num_scalar_prefetch=1, grid=(1,),
            in_specs=[pl.BlockSpec((None, rows, cols), lambda i, idx: (idx[0], 0, 0)),
                      pl.BlockSpec((N_CHIPS - 1, rows, cols), lambda i, idx: (0, 0, 0))],
            out_specs=pl.BlockSpec((rows, cols), lambda i, idx: (idx[1], 0))),
        out_shape=jax.ShapeDtypeStruct((2 * rows, cols), F32),
        compiler_params=pltpu.CompilerParams(vmem_limit_bytes=_vmem_limit(3 * rows * cols * 4)),
    )(idx, sums_f32, landed)


def _halves_to_full(fulls):
    n = len(fulls)

    def body(*refs):
        outs = refs[n:2 * n]
        send_sems, recv_sems = refs[2 * n:]
        x, y, c = _mesh_pos()
        copies = []
        for k in range(n):
            half = fulls[k].shape[0] // 2
            rows = outs[k].at[pl.ds(c * half, half), :]
            cp = pltpu.make_async_remote_copy(
                src_ref=rows, dst_ref=rows, send_sem=send_sems.at[k], recv_sem=recv_sems.at[k],
                device_id=(x, y, 1 - c), device_id_type=MESH)
            cp.start()
            copies.append(cp)
        for k in range(n):
            half = fulls[k].shape[0] // 2
            theirs = outs[k].at[pl.ds((1 - c) * half, half), :]
            pltpu.make_async_remote_copy(
                src_ref=theirs, dst_ref=theirs, send_sem=send_sems.at[k], recv_sem=recv_sems.at[k],
                device_id=(x, y, 1 - c), device_id_type=MESH).wait_recv()
        for cp in copies:
            cp.wait_send()

    return pl.pallas_call(
        body, name="grad_halves_to_full",
        in_specs=_hbm_specs(n), out_specs=_hbm_specs(n),
        out_shape=[jax.ShapeDtypeStruct(f.shape, F32) for f in fulls],
        input_output_aliases={k: k for k in range(n)},
        scratch_shapes=[pltpu.SemaphoreType.DMA((n,)), pltpu.SemaphoreType.DMA((n,))],
    )(*fulls)


def _all_sum_small(v):
    rows, cols = v.shape
    n_dev = 8

    def body(v_ref, out_ref, buf, send_sems, recv_sems):
        x, y, c = _mesh_pos()
        me = 4 * x + 2 * y + c
        buf[me] = v_ref[...]
        peers = []
        for r in range(1, n_dev):
            px = 1 - x if r & 4 else x
            py = 1 - y if r & 2 else y
            pc = 1 - c if r & 1 else c
            peers.append((px, py, pc))
        copies = []
        for r, peer in enumerate(peers):
            cp = pltpu.make_async_remote_copy(
                src_ref=v_ref, dst_ref=buf.at[me], send_sem=send_sems.at[r], recv_sem=recv_sems.at[r],
                device_id=peer, device_id_type=MESH)
            cp.start()
            copies.append(cp)
        for r, (px, py, pc) in enumerate(peers):
            pltpu.make_async_remote_copy(
                src_ref=v_ref, dst_ref=buf.at[4 * px + 2 * py + pc], send_sem=send_sems.at[r], recv_sem=recv_sems.at[r],
                device_id=(px, py, pc), device_id_type=MESH).wait_recv()
        for cp in copies:
            cp.wait_send()
        acc = buf[0]
        for d in range(1, n_dev):
            acc = acc + buf[d]
        out_ref[...] = acc
        out_ref[3:4, :] = jnp.broadcast_to(jnp.sum(acc[3:4, :], axis=1, keepdims=True), (1, cols))

    vm = pl.BlockSpec(memory_space=pltpu.VMEM)
    return pl.pallas_call(
        body, name="all_sum_small", in_specs=[vm], out_specs=vm,
        out_shape=jax.ShapeDtypeStruct((rows, cols), F32),
        scratch_shapes=[pltpu.VMEM((n_dev, rows, cols), F32),
                        pltpu.SemaphoreType.DMA((n_dev - 1,)), pltpu.SemaphoreType.DMA((n_dev - 1,))],
    )(v)


def _adamw_math(w, g, m, v):
    m = ADAM_B1 * m + (1.0 - ADAM_B1) * g
    v = ADAM_B2 * v + (1.0 - ADAM_B2) * (g * g)
    m_hat = m / (1.0 - ADAM_B1 ** ADAM_STEP)
    v_hat = v / (1.0 - ADAM_B2 ** ADAM_STEP)
    delta = -ADAM_LR * (m_hat / (jnp.sqrt(v_hat) + ADAM_EPS) + ADAM_WD * w)
    return delta, m, v


def _adamw(w, g, m, v, name):
    rows, cols = w.shape
    tm = rows // 2 if (rows // 2) % 8 == 0 else rows
    return _rowwise(_adamw_math, [w, g, m, v], [], [(cols, F32)] * 3, [], tm=tm, name=name)


def _unshard_cols(gathered):
    n, r, c = gathered.shape
    return jnp.transpose(gathered, (1, 0, 2)).reshape(r, n * c)


def _shard_cols(full):
    r, nc = full.shape
    return jnp.transpose(full.reshape(r, N_CHIPS, nc // N_CHIPS), (1, 0, 2))


def _fwd_bwd(x, loss_target, g_mix, g_ffn, g_fin, wf_in, wf_sb_up, wf_dil_up, wf_out, wf_ffn_in, wf_ffn_out):
    b_sz, s_len, d_model = x.shape
    t = b_sz * s_len
    d_ff = wf_ffn_out.shape[0]
    x2d = x.reshape(t, d_model)
    tgt2d = loss_target.reshape(t, d_model)
    wf_qkv, wf_gate = wf_in[:, :QKV_WIDTH], wf_in[:, QKV_WIDTH:]

    (u,) = _rowwise(lambda xv, g: (_rms_stats(xv)[0] * g,), [x2d], [g_mix], [(d_model, BF16)], [], tm=512, name="norm_mix")
    qkv = _mm(u, wf_qkv, tm=1024, tn=768, tk=d_model, name="proj_qkv")
    gates = _mm(u, wf_gate, tm=1024, tn=512, tk=d_model, name="proj_gates")
    qkv3 = qkv.reshape(b_sz, s_len, QKV_WIDTH)
    o_sb = _sb_fwd(qkv3, b_sz, s_len)
    o_dl, lse = _dil_fwd(qkv3, b_sz, s_len)
    o_sb2, o_dl2 = o_sb.reshape(t, SB_WIDTH), o_dl.reshape(t, DIL_OUT_WIDTH)
    y_sb = _mm(o_sb2, wf_sb_up, tm=1024, tn=1024, tk=SB_WIDTH, name="sb_up")
    y_dl = _mm(o_dl2, wf_dil_up, tm=1024, tn=1024, tk=DIL_OUT_WIDTH, name="dil_up")

    def merge_fn(gt, ys, yd):
        return (_sigmoid(gt[:, :d_model]) * ys + _sigmoid(gt[:, d_model:]) * yd,)

    (merged,) = _rowwise(merge_fn, [gates, y_sb, y_dl], [], [(d_model, BF16)], [], tm=512, name="merge")
    x1 = _mm(merged, wf_out, add=x2d, tm=512, tn=1024, tk=d_model, name="mix_out")
    (u2,) = _rowwise(lambda xv, g: (_rms_stats(xv)[0] * g,), [x1], [g_ffn], [(d_model, BF16)], [], tm=512, name="norm_ffn")
    h = _mm(u2, wf_ffn_in, tm=1024, tn=512, tk=d_model, name="ffn_in")

    def act_fn(hv):
        gate = hv[:, :d_ff]
        return (gate * _sigmoid(gate) * hv[:, d_ff:],)

    (act,) = _rowwise(act_fn, [h], [], [(d_ff, BF16)], [], tm=256, name="ffn_act")
    x2 = _mm(act, wf_ffn_out, add=x1, tm=512, tn=1024, tk=d_ff, name="ffn_out")

    def head_fn(xv, tg, g):
        xhat, r = _rms_stats(xv)
        err = xhat * g - tg
        dy = err * (1.0 / d_model)
        dx, dg_rows = _rms_bwd(dy, xhat, r, g)
        loss_lanes = (0.5 / d_model) * jnp.sum(err * err, axis=0, keepdims=True)
        return dx, jnp.sum(dg_rows, axis=0, keepdims=True), loss_lanes

    dx2, dg_fin, loss_lanes = _rowwise(head_fn, [x2, tgt2d], [g_fin], [(d_model, F32)], [(1, d_model), (1, d_model)],
                                       tm=512, name="loss_head")

    dact = _mm(dx2, wf_ffn_out, tb=True, tm=512, tn=d_ff // 2, tk=d_model, name="ffn_out_dx")
    gw_ffn_out = _mm(act, dx2, ta=True, tm=d_ff // 2, tn=1024, tk=512, name="ffn_out_dw")

    def dact_fn(hv, da):
        gate, up = hv[:, :d_ff], hv[:, d_ff:]
        sg = _sigmoid(gate)
        dgate = da * up * (sg * (1.0 + gate * (1.0 - sg)))
        return (jnp.concatenate([dgate, da * (gate * sg)], axis=1),)

    (dh,) = _rowwise(dact_fn, [h, dact], [], [(2 * d_ff, BF16)], [], tm=256, name="ffn_act_bwd")
    du2 = _mm(dh, wf_ffn_in, tb=True, tm=1024, tn=1024, tk=512, name="ffn_in_dx")
    gw_ffn_in = _mm(u2, dh, ta=True, tm=1024, tn=512, tk=1024, name="ffn_in_dw")

    def norm_bwd_fn(dres, du_, xv, g):
        xhat, r = _rms_stats(xv)
        dx, dg_rows = _rms_bwd(du_, xhat, r, g)
        return dres + dx, jnp.sum(dg_rows, axis=0, keepdims=True)

    dx1, dg_ffn = _rowwise(norm_bwd_fn, [dx2, du2, x1], [g_ffn], [(d_model, F32)], [(1, d_model)], tm=512, name="norm_ffn_bwd")

    dmerged = _mm(dx1, wf_out, tb=True, tm=512, tn=1024, tk=d_model, name="mix_out_dx")
    gw_out = _mm(merged, dx1, ta=True, tm=1024, tn=512, tk=512, name="mix_out_dw")

    def merge_bwd_fn(gt, ys, yd, dm):
        s_sb, s_dl = _sigmoid(gt[:, :d_model]), _sigmoid(gt[:, d_model:])
        dgates = jnp.concatenate([dm * ys * s_sb * (1.0 - s_sb), dm * yd * s_dl * (1.0 - s_dl)], axis=1)
        return dgates, dm * s_sb, dm * s_dl

    dgates, dy_sb, dy_dl = _rowwise(merge_bwd_fn, [gates, y_sb, y_dl, dmerged], [],
                                    [(2 * d_model, BF16), (d_model, BF16), (d_model, BF16)], [], tm=256, name="merge_bwd")
    do_sb = _mm(dy_sb, wf_sb_up, tb=True, tm=1024, tn=SB_WIDTH, tk=d_model, name="sb_up_dx")
    gw_sb_up = _mm(o_sb2, dy_sb, ta=True, tm=SB_WIDTH, tn=1024, tk=512, name="sb_up_dw")
    do_dl = _mm(dy_dl, wf_dil_up, tb=True, tm=1024, tn=DIL_OUT_WIDTH, tk=d_model, name="dil_up_dx")
    gw_dil_up = _mm(o_dl2, dy_dl, ta=True, tm=DIL_OUT_WIDTH, tn=1024, tk=512, name="dil_up_dw")
    dq_sb, dk_sb, dv_sb = _sb_bwd(qkv3, o_sb, do_sb.reshape(b_sz, s_len, SB_WIDTH), b_sz, s_len)
    dq_dl, dk_dl, dv_dl = _dil_bwd(qkv3, o_dl, lse, do_dl.reshape(b_sz, s_len, DIL_OUT_WIDTH), b_sz, s_len)
    dproj = jnp.concatenate(
        [a.reshape(t, -1) for a in (dq_sb, dk_sb, dv_sb)]
        + [a.reshape(t, -1).astype(BF16) for a in (dq_dl, dk_dl, dv_dl)] + [dgates], axis=1)
    du = _mm(dproj, wf_in, tb=True, tm=512, tn=1024, tk=wf_in.shape[1] // 2, name="proj_dx")
    gw_in = _mm(u, dproj, ta=True, tm=512, tn=wf_in.shape[1] // 2, tk=512, name="proj_dw")
    dx, dg_mix = _rowwise(norm_bwd_fn, [dx1, du, x2d], [g_mix], [(d_model, F32)], [(1, d_model)], tm=512, name="norm_mix_bwd")

    return dx, [gw_in, gw_sb_up, gw_dil_up, gw_out, gw_ffn_in, gw_ffn_out], dg_mix, dg_ffn, dg_fin, loss_lanes


def kernel(x, norm_mix_g, w_in, w_sb_up, w_dil_up, w_out, norm_ffn_g, w_ffn_in, w_ffn_out, norm_final_g, loss_target, m_norm_mix_g, m_w_in, m_w_sb_up, m_w_dil_up, m_w_out, m_norm_ffn_g, m_w_ffn_in, m_w_ffn_out, m_norm_final_g, v_norm_mix_g, v_w_in, v_w_sb_up, v_w_dil_up, v_w_out, v_norm_ffn_g, v_w_ffn_in, v_w_ffn_out, v_norm_final_g):
    b_sz, s_len, d_model = x.shape
    d_ff = w_ffn_out.shape[1] * N_CHIPS
    g_mix, g_ffn, g_fin = norm_mix_g, norm_ffn_g, norm_final_g.reshape(1, d_model)

    names = ["w_in", "w_sb_up", "w_dil_up", "w_out", "w_ffn_in", "w_ffn_out"]
    shards = {"w_in": w_in[0], "w_sb_up": w_sb_up[0], "w_dil_up": w_dil_up[0], "w_out": w_out[0],
              "w_ffn_in": w_ffn_in[0], "w_ffn_out": w_ffn_out[0]}
    gathered = dict(zip(names, _gather_weights([_cast_to_slab(shards[k], "cast_" + k) for k in names])))
    wf_in = _unshard_cols(gathered["w_in"])
    wf_sb_up = _unshard_cols(gathered["w_sb_up"])
    wf_dil_up = _unshard_cols(gathered["w_dil_up"])
    wf_out = gathered["w_out"].reshape(d_model, d_model)
    wf_ffn_in = _unshard_cols(gathered["w_ffn_in"])
    wf_ffn_out = gathered["w_ffn_out"].reshape(d_ff, d_model)

    dx, gws, dg_mix, dg_ffn, dg_fin, loss_lanes = _fwd_bwd(
        x, loss_target, g_mix, g_ffn, g_fin, wf_in, wf_sb_up, wf_dil_up, wf_out, wf_ffn_in, wf_ffn_out)
    gw_in, gw_sb_up, gw_dil_up, gw_out, gw_ffn_in, gw_ffn_out = gws

    full_grads = [_shard_cols(gw_in), _shard_cols(gw_sb_up), _shard_cols(gw_dil_up),
                  gw_out.reshape(N_CHIPS, d_model // N_CHIPS, d_model), _shard_cols(gw_ffn_in),
                  gw_ffn_out.reshape(N_CHIPS, d_ff // N_CHIPS, d_model)]
    others = _pair_exchange(full_grads)
    pair = [_pair_sum(g, o, "grad_pair_sum_" + k) for g, o, k in zip(full_grads, others, names)]
    landed = _chip_exchange([p[1] for p in pair])
    halves = [_chip_sum(p[0], l, "grad_chip_sum_" + k) for p, l, k in zip(pair, landed, names)]
    grads = dict(zip(names, _halves_to_full(halves)))

    small = jnp.concatenate([dg_mix, dg_ffn, dg_fin, loss_lanes, jnp.zeros((4, d_model), F32)], axis=0)
    small = _all_sum_small(small)
    loss = small[3, 0]
    gains = jnp.concatenate([g_mix, g_ffn, g_fin, jnp.zeros((5, d_model), F32)], axis=0)
    gains_m = jnp.concatenate([m_norm_mix_g, m_norm_ffn_g, m_norm_final_g.reshape(1, d_model), jnp.zeros((5, d_model), F32)], axis=0)
    gains_v = jnp.concatenate([v_norm_mix_g, v_norm_ffn_g, v_norm_final_g.reshape(1, d_model), jnp.ones((5, d_model), F32)], axis=0)
    gd, gm, gv = _rowwise(_adamw_math, [gains, small, gains_m, gains_v], [], [(d_model, F32)] * 3, [], tm=8, name="adamw_gains")

    moments = {"w_in": (m_w_in, v_w_in), "w_sb_up": (m_w_sb_up, v_w_sb_up), "w_dil_up": (m_w_dil_up, v_w_dil_up),
               "w_out": (m_w_out, v_w_out), "w_ffn_in": (m_w_ffn_in, v_w_ffn_in), "w_ffn_out": (m_w_ffn_out, v_w_ffn_out)}
    upd = {k: _adamw(shards[k], grads[k], moments[k][0][0], moments[k][1][0], "adamw_" + k) for k in names}

    def w_out_of(i):
        return [upd[k][i][None] for k in names]

    def ordered(mix, ws, ffn_g, fin):
        return [mix, ws[0], ws[1], ws[2], ws[3], ffn_g, ws[4], ws[5], fin]

    grad_ws = [grads[k][None] for k in names]
    outs = [loss, dx.reshape(b_sz, s_len, d_model)]
    outs += ordered(small[0:1], grad_ws, small[1:2], small[2])
    outs += ordered(gd[0:1], w_out_of(0), gd[1:2], gd[2])
    outs += ordered(gm[0:1], w_out_of(1), gm[1:2], gm[2])
    outs += ordered(gv[0:1], w_out_of(2), gv[1:2], gv[2])
    return tuple(outs)
```

```python
import functools
import math

import jax
import jax.numpy as jnp
from jax import lax
from jax.experimental import pallas as pl
from jax.experimental.pallas import tpu as pltpu

F32 = jnp.float32
BF16 = jnp.bfloat16
MESH = pl.DeviceIdType.MESH

HEAD_DIM = 64
SB_HEADS = 8
DIL_PAIRS = ((128, 1), (512, 4), (2048, 16))
DIL_HEADS_PER_GROUP = 4
DIL_HEADS = DIL_HEADS_PER_GROUP * len(DIL_PAIRS)
SB_WIDTH = SB_HEADS * HEAD_DIM
DIL_WIDTH = DIL_HEADS * HEAD_DIM
DIL_OUT_WIDTH = DIL_HEADS_PER_GROUP * HEAD_DIM
QKV_WIDTH = 3 * SB_WIDTH + 3 * DIL_WIDTH
RMS_EPS = 1e-6
ALIBI_MAX_BIAS = 8.0
ADAM_LR = 0.001
ADAM_B1 = 0.9
ADAM_B2 = 0.999
ADAM_EPS = 1e-08
ADAM_WD = 0.01
ADAM_STEP = 10

LANES = 128
BLK = 128
NEG = -1e30
SB_FWD_CHAINS = 4
SB_BWD_CHAINS = 4
N_CHIPS = 4
VMEM_CAP = 56 * 1024 * 1024


def _vmem_limit(tile_bytes):
    return int(min(VMEM_CAP, max(32 * 1024 * 1024, 3 * tile_bytes + 8 * 1024 * 1024)))


def _nbytes(shape, dtype):
    return math.prod(shape) * jnp.dtype(dtype).itemsize


def _dot(a, b):
    return jnp.dot(a, b, preferred_element_type=F32)


def _dot_nt(a, b):
    return lax.dot_general(a, b, (((1,), (1,)), ((), ())), preferred_element_type=F32)


def _dot_tn(a, b):
    return lax.dot_general(a, b, (((0,), (0,)), ((), ())), preferred_element_type=F32)


def _split2(x):
    hi = x.astype(BF16)
    lo = (x - hi.astype(F32)).astype(BF16)
    return hi, lo


def _sigmoid(x):
    return 1.0 / (1.0 + jnp.exp(-x))


def _mm(a, b, *, ta=False, tb=False, add=None, out_dtype=F32, tm, tn, tk, name):
    if ta:
        kdim, m = a.shape
    else:
        m, kdim = a.shape
    if tb:
        n, k2 = b.shape
    else:
        k2, n = b.shape
    assert kdim == k2 and m % tm == 0 and n % tn == 0 and kdim % tk == 0, (name, a.shape, b.shape)
    nk = kdim // tk
    a_spec = pl.BlockSpec((tk, tm), lambda i, j, k: (k, i)) if ta else pl.BlockSpec((tm, tk), lambda i, j, k: (i, k))
    b_spec = pl.BlockSpec((tn, tk), lambda i, j, k: (j, k)) if tb else pl.BlockSpec((tk, tn), lambda i, j, k: (k, j))
    o_spec = pl.BlockSpec((tm, tn), lambda i, j, k: (i, j))
    dims = ((((0,) if ta else (1,)), ((1,) if tb else (0,))), ((), ()))
    has_add = add is not None
    use_scratch = nk > 1 and jnp.dtype(out_dtype) != jnp.dtype(F32)

    def body(*refs):
        a_ref, b_ref = refs[0], refs[1]
        add_ref = refs[2] if has_add else None
        o_ref = refs[3] if has_add else refs[2]
        acc_ref = refs[-1] if use_scratch else o_ref
        prod = lax.dot_general(a_ref[...].astype(BF16), b_ref[...].astype(BF16), dims, preferred_element_type=F32)
        if nk == 1:
            if has_add:
                prod = prod + add_ref[...]
            o_ref[...] = prod.astype(out_dtype)
            return
        k = pl.program_id(2)

        @pl.when(k == 0)
        def _():
            acc_ref[...] = prod + add_ref[...] if has_add else prod

        @pl.when(k > 0)
        def _():
            acc_ref[...] += prod

        if use_scratch:
            @pl.when(k == nk - 1)
            def _():
                o_ref[...] = acc_ref[...].astype(out_dtype)

    tile_bytes = (_nbytes((tm, tk), a.dtype) + _nbytes((tk, tn), b.dtype) + _nbytes((tm, tn), out_dtype)
                  + (_nbytes((tm, tn), F32) if has_add else 0) + _nbytes((tm, tn), F32))
    in_specs = [a_spec, b_spec] + ([o_spec] if has_add else [])
    args = (a, b) + ((add,) if has_add else ())
    return pl.pallas_call(
        body, name=name, grid=(m // tm, n // tn, nk),
        in_specs=in_specs, out_specs=o_spec,
        out_shape=jax.ShapeDtypeStruct((m, n), out_dtype),
        scratch_shapes=[pltpu.VMEM((tm, tn), F32)] if use_scratch else [],
        compiler_params=pltpu.CompilerParams(
            dimension_semantics=("parallel", "parallel", "arbitrary"),
            vmem_limit_bytes=_vmem_limit(tile_bytes)),
    )(*args)


def _rowwise(fn, rows, params, outs, accs, *, tm, name):
    t = rows[0].shape[0]
    assert t % tm == 0, (name, t, tm)
    n_r, n_p, n_o = len(rows), len(params), len(outs)

    def body(*refs):
        vals = [r[...] for r in refs[:n_r + n_p]]
        res = fn(*vals)
        o_refs = refs[n_r + n_p:n_r + n_p + n_o]
        a_refs = refs[n_r + n_p + n_o:]
        for o_ref, v in zip(o_refs, res[:n_o]):
            o_ref[...] = v.astype(o_ref.dtype)
        if accs:
            @pl.when(pl.program_id(0) == 0)
            def _():
                for a_ref in a_refs:
                    a_ref[...] = jnp.zeros(a_ref.shape, F32)

            for a_ref, v in zip(a_refs, res[n_o:]):
                a_ref[...] += v

    in_specs = [pl.BlockSpec((tm, r.shape[1]), lambda i: (i, 0)) for r in rows]
    in_specs += [pl.BlockSpec(p.shape, lambda i: (0, 0)) for p in params]
    out_specs = [pl.BlockSpec((tm, w), lambda i: (i, 0)) for w, _ in outs]
    out_specs += [pl.BlockSpec(s, lambda i: (0, 0)) for s in accs]
    out_shape = [jax.ShapeDtypeStruct((t, w), d) for w, d in outs]
    out_shape += [jax.ShapeDtypeStruct(s, F32) for s in accs]
    tile_bytes = sum(_nbytes((tm, r.shape[1]), r.dtype) for r in rows) + sum(_nbytes((tm, w), F32) for w, _ in outs)
    res = pl.pallas_call(
        body, name=name, grid=(t // tm,), in_specs=in_specs, out_specs=out_specs, out_shape=out_shape,
        compiler_params=pltpu.CompilerParams(
            dimension_semantics=("arbitrary",) if accs else ("parallel",),
            vmem_limit_bytes=_vmem_limit(2 * tile_bytes)),
    )(*rows, *params)
    return res


def _rms_stats(x):
    r = lax.rsqrt(jnp.mean(x * x, axis=-1, keepdims=True) + RMS_EPS)
    return x * r, r


def _rms_bwd(dy, xhat, r, g):
    dxhat = dy * g
    dx = r * (dxhat - xhat * jnp.mean(dxhat * xhat, axis=-1, keepdims=True))
    return dx, dy * xhat


def _sb_consts():
    lane = lax.broadcasted_iota(jnp.int32, (BLK, LANES), 1)
    head0 = lane < HEAD_DIM
    row = lax.broadcasted_iota(jnp.int32, (2 * BLK, BLK), 0) % BLK
    col = lax.broadcasted_iota(jnp.int32, (2 * BLK, BLK), 1)
    causal = col < row
    jj = lax.broadcasted_iota(jnp.int32, (BLK, 2 * BLK), 0)
    ss = lax.broadcasted_iota(jnp.int32, (BLK, 2 * BLK), 1)
    suffix = jnp.where((ss >= BLK) | (jj > ss), 1.0, 0.0).astype(BF16)
    return head0, causal, suffix


def _stack_heads(x, head0):
    zero = jnp.zeros_like(x)
    return jnp.concatenate([jnp.where(head0, x, zero), jnp.where(head0, zero, x)], axis=0)


def _sb_logits(z, causal, masked):
    sp = jnp.log(1.0 + jnp.exp(-jnp.abs(z)))
    log_keep = -(jnp.maximum(z, 0.0) + sp)
    log_beta = jnp.minimum(z, 0.0) - sp
    if masked:
        log_keep = jnp.where(causal, log_keep, 0.0)
    return log_keep, log_beta


def _suffix_sums(x, suffix):
    hi, lo = _split2(x)
    s2 = _dot(hi, suffix) + _dot(lo, suffix)
    return s2[:, :BLK], s2[:, BLK:]


def _lane_blocks(x, n):
    return [x[:, p * LANES:(p + 1) * LANES] for p in range(n)]


def _sb_fwd(qkv, b_sz, s_len):
    nq = s_len // BLK
    n_pairs = SB_WIDTH // LANES
    ch = SB_FWD_CHAINS
    n_steps = n_pairs // ch
    scale = 1.0 / math.sqrt(HEAD_DIM)

    def body(q_ref, k_ref, v_ref, o_ref):
        head0, causal, suffix = _sb_consts()

        def q_block(i, _):
            qs = pl.multiple_of(i * BLK, BLK)
            q_all = (q_ref[pl.ds(qs, BLK), :] * scale).astype(BF16)
            q01 = [_stack_heads(q, head0) for q in _lane_blocks(q_all, ch)]

            def tile(j, state, masked):
                ks = pl.multiple_of(j * BLK, BLK)
                ks_ = _lane_blocks(k_ref[pl.ds(ks, BLK), :].astype(BF16), ch)
                vs_ = _lane_blocks(v_ref[pl.ds(ks, BLK), :].astype(BF16), ch)
                zs = [_dot_nt(q01[p], ks_[p]) for p in range(ch)]
                logits = [_sb_logits(z, causal, masked) for z in zs]
                sums = [_suffix_sums(lg[0], suffix) for lg in logits]
                out = []
                for p in range(ch):
                    carry, acc = state[2 * p], state[2 * p + 1]
                    after, total = sums[p]
                    a = jnp.exp(logits[p][1] + carry + after)
                    if masked:
                        a = jnp.where(causal, a, 0.0)
                    a_hi, a_lo = _split2(a)
                    a_cat = jnp.concatenate([a_hi[:BLK], a_hi[BLK:], a_lo[:BLK], a_lo[BLK:]], axis=1)
                    v01 = _stack_heads(vs_[p], head0)
                    out += [carry + total, acc + _dot(a_cat, jnp.concatenate([v01, v01], axis=0))]
                return tuple(out)

            state = (jnp.zeros((2 * BLK, BLK), F32), jnp.zeros((BLK, LANES), F32)) * ch
            state = tile(i, state, True)
            state = lax.fori_loop(0, i, lambda t, st: tile(i - 1 - t, st, False), state)
            o_ref[pl.ds(qs, BLK), :] = jnp.concatenate([state[2 * p + 1] for p in range(ch)], axis=1)
            return 0

        lax.fori_loop(0, nq, q_block, 0)

    blk = lambda off: pl.BlockSpec((None, s_len, ch * LANES), lambda b, p: (b, 0, off + p))
    return pl.pallas_call(
        body, name="sb_fwd", grid=(b_sz, n_steps),
        in_specs=[blk(0), blk(n_steps), blk(2 * n_steps)],
        out_specs=blk(0),
        out_shape=jax.ShapeDtypeStruct((b_sz, s_len, SB_WIDTH), F32),
        compiler_params=pltpu.CompilerParams(dimension_semantics=("parallel", "parallel"),
                                             vmem_limit_bytes=VMEM_CAP),
    )(qkv, qkv, qkv)


def _sb_bwd(qkv, o_sb, do_sb, b_sz, s_len):
    nq = s_len // BLK
    n_pairs = SB_WIDTH // LANES
    ch = SB_BWD_CHAINS
    n_steps = n_pairs // ch
    scale = 1.0 / math.sqrt(HEAD_DIM)

    def body(q_ref, k_ref, v_ref, o_ref, do_ref, dq_ref, dk_ref, dv_ref, dk_acc, dv_acc):
        head0, causal, suffix = _sb_consts()
        lrow = lax.broadcasted_iota(jnp.int32, (LANES, LANES), 0)
        ones_h0 = jnp.where(lrow < HEAD_DIM, 1.0, 0.0).astype(BF16)
        ones_h1 = jnp.where(lrow >= HEAD_DIM, 1.0, 0.0).astype(BF16)
        dk_acc[...] = jnp.zeros(dk_acc.shape, F32)
        dv_acc[...] = jnp.zeros(dv_acc.shape, F32)

        def q_block(i, _):
            qs = pl.multiple_of(i * BLK, BLK)
            q_all = (q_ref[pl.ds(qs, BLK), :] * scale).astype(BF16)
            do_all = do_ref[pl.ds(qs, BLK), :].astype(BF16)
            dd_all = do_all.astype(F32) * o_ref[pl.ds(qs, BLK), :]
            q01 = [_stack_heads(q, head0) for q in _lane_blocks(q_all, ch)]
            do01 = [_stack_heads(d, head0) for d in _lane_blocks(do_all, ch)]
            tot = []
            for dd in _lane_blocks(dd_all, ch):
                dd_hi, dd_lo = _split2(dd)
                tot.append(jnp.concatenate([_dot(dd_hi, ones_h0) + _dot(dd_lo, ones_h0),
                                            _dot(dd_hi, ones_h1) + _dot(dd_lo, ones_h1)], axis=0))

            def tile(j, state, masked):
                ks = pl.multiple_of(j * BLK, BLK)
                ks_ = _lane_blocks(k_ref[pl.ds(ks, BLK), :].astype(BF16), ch)
                vs_ = _lane_blocks(v_ref[pl.ds(ks, BLK), :].astype(BF16), ch)
                zs = [_dot_nt(q01[p], ks_[p]) for p in range(ch)]
                das = [_dot_nt(do01[p], vs_[p]) for p in range(ch)]
                logits = [_sb_logits(z, causal, masked) for z in zs]
                sums = [_suffix_sums(lg[0], suffix) for lg in logits]
                a_s, e_s = [], []
                for p in range(ch):
                    a = jnp.exp(logits[p][1] + state[3 * p] + sums[p][0])
                    if masked:
                        a = jnp.where(causal, a, 0.0)
                    a_s.append(a)
                    e_s.append(a * das[p])
                e_sums = [_suffix_sums(e, suffix) for e in e_s]
                out, dks, dvs = [], [], []
                for p in range(ch):
                    carry, rcarry, dq = state[3 * p:3 * p + 3]
                    e = e_s[p]
                    before = tot[p] - (rcarry + e_sums[p][0] + e)
                    beta = jnp.exp(logits[p][1])
                    dz = e * (1.0 - beta) - beta * before
                    if masked:
                        dz = jnp.where(causal, dz, 0.0)
                    dz_b = dz.astype(BF16)
                    dks.append(_dot_tn(dz_b, q01[p]))
                    dvs.append(_dot_tn(a_s[p].astype(BF16), do01[p]))
                    out += [carry + sums[p][1], rcarry + e_sums[p][1], dq + _dot(dz_b, ks_[p])]
                dk_acc[pl.ds(ks, BLK), :] += jnp.concatenate(dks, axis=1)
                dv_acc[pl.ds(ks, BLK), :] += jnp.concatenate(dvs, axis=1)
                return tuple(out)

            state = (jnp.zeros((2 * BLK, BLK), F32),) * (3 * ch)
            state = tile(i, state, True)
            state = lax.fori_loop(0, i, lambda t, st: tile(i - 1 - t, st, False), state)
            dq = [jnp.where(head0, state[3 * p + 2][:BLK], state[3 * p + 2][BLK:]) for p in range(ch)]
            dq_ref[pl.ds(qs, BLK), :] = (jnp.concatenate(dq, axis=1) * scale).astype(dq_ref.dtype)
            return 0

        lax.fori_loop(0, nq, q_block, 0)
        dk_ref[...] = dk_acc[...].astype(dk_ref.dtype)
        dv_ref[...] = dv_acc[...].astype(dv_ref.dtype)

    blk = lambda off: pl.BlockSpec((None, s_len, ch * LANES), lambda b, p: (b, 0, off + p))
    once = lambda off: pl.BlockSpec((None, s_len, ch * LANES), lambda b, p: (b, 0, off + p),
                                    pipeline_mode=pl.Buffered(1))
    out_sd = jax.ShapeDtypeStruct((b_sz, s_len, SB_WIDTH), BF16)
    return pl.pallas_call(
        body, name="sb_bwd", grid=(b_sz, n_steps),
        in_specs=[once(0), once(n_steps), once(2 * n_steps), once(0), once(0)],
        out_specs=[blk(0), blk(0), blk(0)],
        out_shape=[out_sd, out_sd, out_sd],
        scratch_shapes=[pltpu.VMEM((s_len, ch * LANES), F32), pltpu.VMEM((s_len, ch * LANES), F32)],
        compiler_params=pltpu.CompilerParams(dimension_semantics=("parallel", "parallel"),
                                             vmem_limit_bytes=VMEM_CAP),
    )(qkv, qkv, qkv, o_sb, do_sb)


def _dil_consts(group, pair_idx, dilation):
    lane = lax.broadcasted_iota(jnp.int32, (BLK, LANES), 1)
    head0 = lane < HEAD_DIM
    row = lax.broadcasted_iota(jnp.int32, (2 * BLK, BLK), 0)
    qa = row % BLK
    kb = lax.broadcasted_iota(jnp.int32, (2 * BLK, BLK), 1)
    head = (group * DIL_HEADS_PER_GROUP + 2 * pair_idx + row // BLK).astype(F32)
    slope = jnp.exp((-ALIBI_MAX_BIAS * math.log(2.0) / DIL_HEADS) * (head + 1.0))
    valid_cur = kb <= qa
    valid_prev = kb >= qa
    bias_cur = -slope * ((qa - kb) * dilation).astype(F32)
    bias_prev = -slope * ((BLK + qa - kb) * dilation).astype(F32)
    return head0, valid_cur, valid_prev, bias_cur, bias_prev


def _dil_units(s_len, dilation):
    nb = s_len // dilation // BLK
    return [(r, n) for r in range(dilation) for n in range(nb)]


def _dil_rows(n, r, dilation):
    if dilation == 1:
        return pl.ds(n * BLK, BLK)
    return pl.ds(n * BLK * dilation + r, BLK, stride=dilation)


def _dil_scores(q01, k, bias, valid):
    s = _dot_nt(q01, k) * (1.0 / math.sqrt(HEAD_DIM)) + bias
    return jnp.where(valid, s, NEG)


def _dil_fwd(qkv, b_sz, s_len):
    n_pairs = DIL_OUT_WIDTH // LANES
    q_off = 3 * SB_WIDTH // LANES
    per_kind = DIL_WIDTH // LANES

    def body(*refs):
        qkv_refs = refs[:9]
        o_ref, lse_ref, m_s, l_s = refs[9:]
        pair_idx = pl.program_id(1)
        m_s[...] = jnp.full(m_s.shape, NEG, F32)
        l_s[...] = jnp.zeros(l_s.shape, F32)
        o_ref[...] = jnp.zeros(o_ref.shape, F32)
        for g, (_, dilation) in enumerate(DIL_PAIRS):
            q_ref, k_ref, v_ref = qkv_refs[3 * g:3 * g + 3]
            head0, valid_cur, valid_prev, bias_cur, bias_prev = _dil_consts(g, pair_idx, dilation)
            for r, n in _dil_units(s_len, dilation):
                rows = _dil_rows(n, r, dilation)
                q01 = _stack_heads(q_ref[rows, :].astype(BF16), head0)
                k_c = k_ref[rows, :].astype(BF16)
                v_c = v_ref[rows, :].astype(BF16)
                scores = [_dil_scores(q01, k_c, bias_cur, valid_cur)]
                values = [_stack_heads(v_c, head0)]
                if n > 0:
                    prev = _dil_rows(n - 1, r, dilation)
                    scores.append(_dil_scores(q01, k_ref[prev, :].astype(BF16), bias_prev, valid_prev))
                    values.append(_stack_heads(v_ref[prev, :].astype(BF16), head0))
                m_blk = functools.reduce(jnp.maximum, [jnp.max(s, axis=-1, keepdims=True) for s in scores])
                m_old = jnp.concatenate([m_s.at[0][rows, :], m_s.at[1][rows, :]], axis=0)
                l_old = jnp.concatenate([l_s.at[0][rows, :], l_s.at[1][rows, :]], axis=0)
                m_new = jnp.maximum(m_old, m_blk)
                probs = [jnp.exp(s - m_new) for s in scores]
                l_blk = functools.reduce(jnp.add, [jnp.sum(p, axis=-1, keepdims=True) for p in probs])
                alpha = jnp.exp(m_old - m_new)
                l_new = alpha * l_old + l_blk
                alpha_tok = jnp.where(head0, alpha[:BLK], alpha[BLK:])
                p_cat = jnp.concatenate([h for p in probs for h in (p[:BLK].astype(BF16), p[BLK:].astype(BF16))], axis=1)
                o_ref[rows, :] = alpha_tok * o_ref[rows, :] + _dot(p_cat, jnp.concatenate(values, axis=0))
                m_s.at[0][rows, :] = m_new[:BLK]
                m_s.at[1][rows, :] = m_new[BLK:]
                l_s.at[0][rows, :] = l_new[:BLK]
                l_s.at[1][rows, :] = l_new[BLK:]
        lane = lax.broadcasted_iota(jnp.int32, (BLK, LANES), 1)
        for c in range(s_len // BLK):
            rows = pl.ds(c * BLK, BLK)
            l0, l1 = l_s.at[0][rows, :], l_s.at[1][rows, :]
            o_ref[rows, :] = o_ref[rows, :] / jnp.where(lane < HEAD_DIM, l0, l1)
            lse_ref.at[0][rows, :] = m_s.at[0][rows, :] + jnp.log(l0)
            lse_ref.at[1][rows, :] = m_s.at[1][rows, :] + jnp.log(l1)

    in_specs = []
    for g in range(len(DIL_PAIRS)):
        for kind in range(3):
            off = q_off + kind * per_kind + g * n_pairs
            in_specs.append(pl.BlockSpec((None, s_len, LANES), lambda b, p, off=off: (b, 0, off + p)))
    return pl.pallas_call(
        body, name="dil_fwd", grid=(b_sz, n_pairs),
        in_specs=in_specs,
        out_specs=[pl.BlockSpec((None, s_len, LANES), lambda b, p: (b, 0, p)),
                   pl.BlockSpec((None, None, 2, s_len, LANES), lambda b, p: (b, p, 0, 0, 0))],
        out_shape=[jax.ShapeDtypeStruct((b_sz, s_len, DIL_OUT_WIDTH), F32),
                   jax.ShapeDtypeStruct((b_sz, n_pairs, 2, s_len, LANES), F32)],
        scratch_shapes=[pltpu.VMEM((2, s_len, LANES), F32), pltpu.VMEM((2, s_len, LANES), F32)],
        compiler_params=pltpu.CompilerParams(dimension_semantics=("parallel", "parallel"),
                                             vmem_limit_bytes=VMEM_CAP),
    )(*([qkv] * 9))


def _dil_bwd(qkv, o_dl, lse, do_dl, b_sz, s_len):
    n_pairs = DIL_OUT_WIDTH // LANES
    n_groups = len(DIL_PAIRS)
    q_off = 3 * SB_WIDTH // LANES
    per_kind = DIL_WIDTH // LANES

    def body(q_ref, k_ref, v_ref, o_ref, lse_ref, do_ref, dq_ref, dk_ref, dv_ref, d_s):
        pair_idx = pl.program_id(1)
        group = pl.program_id(2)
        lrow = lax.broadcasted_iota(jnp.int32, (LANES, LANES), 0)
        ones_h0 = jnp.where(lrow < HEAD_DIM, 1.0, 0.0).astype(BF16)
        ones_h1 = jnp.where(lrow >= HEAD_DIM, 1.0, 0.0).astype(BF16)
        for c in range(s_len // BLK):
            rows = pl.ds(c * BLK, BLK)
            dd_hi, dd_lo = _split2(do_ref[rows, :] * o_ref[rows, :])
            d_s.at[0][rows, :] = _dot(dd_hi, ones_h0) + _dot(dd_lo, ones_h0)
            d_s.at[1][rows, :] = _dot(dd_hi, ones_h1) + _dot(dd_lo, ones_h1)
        dk_ref[...] = jnp.zeros(dk_ref.shape, F32)
        dv_ref[...] = jnp.zeros(dv_ref.shape, F32)

        def one_group(g, dilation):
            head0, valid_cur, valid_prev, bias_cur, bias_prev = _dil_consts(g, pair_idx, dilation)
            for r, n in _dil_units(s_len, dilation):
                rows = _dil_rows(n, r, dilation)
                q01 = _stack_heads(q_ref[rows, :].astype(BF16), head0)
                do01 = _stack_heads(do_ref[rows, :].astype(BF16), head0)
                lse01 = jnp.concatenate([lse_ref.at[0][rows, :], lse_ref.at[1][rows, :]], axis=0)
                d01 = jnp.concatenate([d_s.at[0][rows, :], d_s.at[1][rows, :]], axis=0)
                dq = jnp.zeros((2 * BLK, LANES), F32)
                blocks = [(rows, bias_cur, valid_cur)]
                if n > 0:
                    blocks.append((_dil_rows(n - 1, r, dilation), bias_prev, valid_prev))
                for krows, bias, valid in blocks:
                    k = k_ref[krows, :].astype(BF16)
                    v = v_ref[krows, :].astype(BF16)
                    p = jnp.exp(_dil_scores(q01, k, bias, valid) - lse01)
                    ds = (p * (_dot_nt(do01, v) - d01) * (1.0 / math.sqrt(HEAD_DIM))).astype(BF16)
                    dq = dq + _dot(ds, k)
                    dk_ref[krows, :] = dk_ref[krows, :] + _dot_tn(ds, q01)
                    dv_ref[krows, :] = dv_ref[krows, :] + _dot_tn(p.astype(BF16), do01)
                dq_ref[rows, :] = jnp.where(head0, dq[:BLK], dq[BLK:])

        for g, (_, dilation) in enumerate(DIL_PAIRS):
            pl.when(group == g)(functools.partial(one_group, g, dilation))

    def qkv_spec(kind):
        return pl.BlockSpec((None, s_len, LANES),
                            lambda b, p, g: (b, 0, q_off + kind * per_kind + g * n_pairs + p))

    tok_spec = pl.BlockSpec((None, s_len, LANES), lambda b, p, g: (b, 0, p))
    out_spec = pl.BlockSpec((None, s_len, LANES), lambda b, p, g: (b, 0, g * n_pairs + p))
    out_sd = jax.ShapeDtypeStruct((b_sz, s_len, DIL_WIDTH), F32)
    return pl.pallas_call(
        body, name="dil_bwd", grid=(b_sz, n_pairs, n_groups),
        in_specs=[qkv_spec(0), qkv_spec(1), qkv_spec(2), tok_spec,
                  pl.BlockSpec((None, None, 2, s_len, LANES), lambda b, p, g: (b, p, 0, 0, 0)), tok_spec],
        out_specs=[out_spec, out_spec, out_spec],
        out_shape=[out_sd, out_sd, out_sd],
        scratch_shapes=[pltpu.VMEM((2, s_len, LANES), F32)],
        compiler_params=pltpu.CompilerParams(dimension_semantics=("parallel", "parallel", "arbitrary"),
                                             vmem_limit_bytes=VMEM_CAP),
    )(qkv, qkv, qkv, o_dl, lse, do_dl)


def _mesh_pos():
    return lax.axis_index("x"), lax.axis_index("y"), lax.axis_index("c")


def _other_chips(x, y):
    return [(1 - x, y), (x, 1 - y), (1 - x, 1 - y)]


def _hbm_specs(n):
    return [pl.BlockSpec(memory_space=pl.ANY)] * n


def _cast_to_slab(w, name):
    rows, cols = w.shape
    mine = jnp.reshape(2 * lax.axis_index("x") + lax.axis_index("y"), (1,)).astype(jnp.int32)

    def body(idx_ref, w_ref, o_ref):
        o_ref[...] = w_ref[...].astype(BF16)

    return pl.pallas_call(
        body, name=name,
        grid_spec=pltpu.PrefetchScalarGridSpec(
            num_scalar_prefetch=1, grid=(1,),
            in_specs=[pl.BlockSpec((rows, cols), lambda i, idx: (0, 0))],
            out_specs=pl.BlockSpec((None, rows, cols), lambda i, idx: (idx[0], 0, 0))),
        out_shape=jax.ShapeDtypeStruct((N_CHIPS, rows, cols), BF16),
        compiler_params=pltpu.CompilerParams(vmem_limit_bytes=_vmem_limit(rows * cols * 6)),
    )(mine, w)


def _gather_weights(slabs):
    n = len(slabs)

    def body(*refs):
        outs = refs[n:2 * n]
        send_sems, recv_sems = refs[2 * n:]
        x, y, c = _mesh_pos()
        mine = 2 * x + y
        chips = _other_chips(x, y)
        sends = []
        for k in range(n):
            half = slabs[k].shape[1] // 2
            rows = outs[k].at[mine, pl.ds(c * half, half), :]
            for r, (px, py) in enumerate(chips):
                cp = pltpu.make_async_remote_copy(
                    src_ref=rows, dst_ref=rows,
                    send_sem=send_sems.at[6 * k + r], recv_sem=recv_sems.at[6 * k + r],
                    device_id=(px, py, c), device_id_type=MESH)
                cp.start()
                sends.append(cp)
        for k in range(n):
            half = slabs[k].shape[1] // 2
            rows = pl.ds(c * half, half)
            for r, (px, py) in enumerate(chips):
                theirs = outs[k].at[2 * px + py, rows, :]
                pltpu.make_async_remote_copy(
                    src_ref=theirs, dst_ref=theirs, send_sem=send_sems.at[6 * k + r], recv_sem=recv_sems.at[6 * k + r],
                    device_id=(px, py, c), device_id_type=MESH).wait_recv()
                fwd = pltpu.make_async_remote_copy(
                    src_ref=theirs, dst_ref=theirs, send_sem=send_sems.at[6 * k + 3 + r], recv_sem=recv_sems.at[6 * k + 3 + r],
                    device_id=(x, y, 1 - c), device_id_type=MESH)
                fwd.start()
                sends.append(fwd)
        for k in range(n):
            half = slabs[k].shape[1] // 2
            sib_rows = pl.ds((1 - c) * half, half)
            for r, (px, py) in enumerate(chips):
                landed = outs[k].at[2 * px + py, sib_rows, :]
                pltpu.make_async_remote_copy(
                    src_ref=landed, dst_ref=landed, send_sem=send_sems.at[6 * k + 3 + r], recv_sem=recv_sems.at[6 * k + 3 + r],
                    device_id=(x, y, 1 - c), device_id_type=MESH).wait_recv()
        for cp in sends:
            cp.wait_send()

    return pl.pallas_call(
        body, name="gather_weights",
        in_specs=_hbm_specs(n), out_specs=_hbm_specs(n),
        out_shape=[jax.ShapeDtypeStruct(s.shape, s.dtype) for s in slabs],
        input_output_aliases={k: k for k in range(n)},
        scratch_shapes=[pltpu.SemaphoreType.DMA((6 * n,)), pltpu.SemaphoreType.DMA((6 * n,))],
    )(*slabs)


def _pair_exchange(grads):
    n = len(grads)

    def body(*refs):
        ins, outs = refs[:n], refs[n:2 * n]
        send_sems, recv_sems = refs[2 * n:]
        x, y, c = _mesh_pos()
        copies = []
        for k in range(n):
            half = grads[k].shape[1] // 2
            cp = pltpu.make_async_remote_copy(
                src_ref=ins[k].at[:, pl.ds((1 - c) * half, half), :], dst_ref=outs[k],
                send_sem=send_sems.at[k], recv_sem=recv_sems.at[k],
                device_id=(x, y, 1 - c), device_id_type=MESH)
            cp.start()
            copies.append(cp)
        for cp in copies:
            cp.wait()

    return pl.pallas_call(
        body, name="grad_pair_exchange",
        in_specs=_hbm_specs(n), out_specs=_hbm_specs(n),
        out_shape=[jax.ShapeDtypeStruct((N_CHIPS, g.shape[1] // 2, g.shape[2]), F32) for g in grads],
        scratch_shapes=[pltpu.SemaphoreType.DMA((n,)), pltpu.SemaphoreType.DMA((n,))],
    )(*grads)


def _pair_sum(grad, other, name):
    _, rows, cols = grad.shape
    half = rows // 2
    core = jnp.reshape(lax.axis_index("c"), (1,)).astype(jnp.int32)

    def body(core_ref, g_ref, p_ref, s_ref, sb_ref):
        s = g_ref[...] + p_ref[...]
        s_ref[...] = s
        sb_ref[...] = s.astype(BF16)

    blk = pl.BlockSpec((None, half, cols), lambda p, core_ref: (p, 0, 0))
    return pl.pallas_call(
        body, name=name,
        grid_spec=pltpu.PrefetchScalarGridSpec(
            num_scalar_prefetch=1, grid=(N_CHIPS,),
            in_specs=[pl.BlockSpec((None, half, cols), lambda p, core_ref: (p, core_ref[0], 0)), blk],
            out_specs=[blk, blk]),
        out_shape=[jax.ShapeDtypeStruct((N_CHIPS, half, cols), F32),
                   jax.ShapeDtypeStruct((N_CHIPS, half, cols), BF16)],
        compiler_params=pltpu.CompilerParams(dimension_semantics=("parallel",),
                                             vmem_limit_bytes=_vmem_limit(4 * half * cols * 4)),
    )(core, grad, other)


def _chip_exchange(sums_bf16):
    n = len(sums_bf16)

    def body(*refs):
        b_ins, lands = refs[:n], refs[n:2 * n]
        send_sems, recv_sems = refs[2 * n:]
        x, y, c = _mesh_pos()
        copies = []
        for k in range(n):
            for r, (px, py) in enumerate(_other_chips(x, y)):
                cp = pltpu.make_async_remote_copy(
                    src_ref=b_ins[k].at[2 * px + py], dst_ref=lands[k].at[r],
                    send_sem=send_sems.at[3 * k + r], recv_sem=recv_sems.at[3 * k + r],
                    device_id=(px, py, c), device_id_type=MESH)
                cp.start()
                copies.append(cp)
        for cp in copies:
            cp.wait()

    return pl.pallas_call(
        body, name="grad_chip_exchange",
        in_specs=_hbm_specs(n), out_specs=_hbm_specs(n),
        out_shape=[jax.ShapeDtypeStruct((N_CHIPS - 1,) + s.shape[1:], BF16) for s in sums_bf16],
        scratch_shapes=[pltpu.SemaphoreType.DMA((3 * n,)), pltpu.SemaphoreType.DMA((3 * n,))],
    )(*sums_bf16)


def _chip_sum(sums_f32, landed, name):
    _, rows, cols = sums_f32.shape
    x, y, c = _mesh_pos()
    idx = jnp.stack([2 * x + y, c]).astype(jnp.int32)

    def body(idx_ref, o_ref, l_ref, out_ref):
        out_ref[...] = ((o_ref[...] + l_ref[0].astype(F32)) + l_ref[1].astype(F32)) + l_ref[2].astype(F32)

    return pl.pallas_call(
        body, name=name,
        grid_spec=pltpu.PrefetchScalarGridSpec(
            num_scalar_prefetch=1, grid=(1,),
            in_specs=[pl.BlockSpec((None, rows, cols), lambda i, idx: (idx[0], 0, 0)),
                      pl.BlockSpec((N_CHIPS - 1, rows, cols), lambda i, idx: (0, 0, 0))],
            out_specs=pl.BlockSpec((rows, cols), lambda i, idx: (idx[1], 0))),
        out_shape=jax.ShapeDtypeStruct((2 * rows, cols), F32),
        compiler_params=pltpu.CompilerParams(vmem_limit_bytes=_vmem_limit(3 * rows * cols * 4)),
    )(idx, sums_f32, landed)


def _halves_to_full(fulls):
    n = len(fulls)

    def body(*refs):
        outs = refs[n:2 * n]
        send_sems, recv_sems = refs[2 * n:]
        x, y, c = _mesh_pos()
        copies = []
        for k in range(n):
            half = fulls[k].shape[0] // 2
            rows = outs[k].at[pl.ds(c * half, half), :]
            cp = pltpu.make_async_remote_copy(
                src_ref=rows, dst_ref=rows, send_sem=send_sems.at[k], recv_sem=recv_sems.at[k],
                device_id=(x, y, 1 - c), device_id_type=MESH)
            cp.start()
            copies.append(cp)
        for k in range(n):
            half = fulls[k].shape[0] // 2
            theirs = outs[k].at[pl.ds((1 - c) * half, half), :]
            pltpu.make_async_remote_copy(
                src_ref=theirs, dst_ref=theirs, send_sem=send_sems.at[k], recv_sem=recv_sems.at[k],
                device_id=(x, y, 1 - c), device_id_type=MESH).wait_recv()
        for cp in copies:
            cp.wait_send()

    return pl.pallas_call(
        body, name="grad_halves_to_full",
        in_specs=_hbm_specs(n), out_specs=_hbm_specs(n),
        out_shape=[jax.ShapeDtypeStruct(f.shape, F32) for f in fulls],
        input_output_aliases={k: k for k in range(n)},
        scratch_shapes=[pltpu.SemaphoreType.DMA((n,)), pltpu.SemaphoreType.DMA((n,))],
    )(*fulls)


def _all_sum_small(v):
    rows, cols = v.shape
    n_dev = 8

    def body(v_ref, out_ref, buf, send_sems, recv_sems):
        x, y, c = _mesh_pos()
        me = 4 * x + 2 * y + c
        buf[me] = v_ref[...]
        peers = []
        for r in range(1, n_dev):
            px = 1 - x if r & 4 else x
            py = 1 - y if r & 2 else y
            pc = 1 - c if r & 1 else c
            peers.append((px, py, pc))
        copies = []
        for r, peer in enumerate(peers):
            cp = pltpu.make_async_remote_copy(
                src_ref=v_ref, dst_ref=buf.at[me], send_sem=send_sems.at[r], recv_sem=recv_sems.at[r],
                device_id=peer, device_id_type=MESH)
            cp.start()
            copies.append(cp)
        for r, (px, py, pc) in enumerate(peers):
            pltpu.make_async_remote_copy(
                src_ref=v_ref, dst_ref=buf.at[4 * px + 2 * py + pc], send_sem=send_sems.at[r], recv_sem=recv_sems.at[r],
                device_id=(px, py, pc), device_id_type=MESH).wait_recv()
        for cp in copies:
            cp.wait_send()
        acc = buf[0]
        for d in range(1, n_dev):
            acc = acc + buf[d]
        out_ref[...] = acc
        out_ref[3:4, :] = jnp.broadcast_to(jnp.sum(acc[3:4, :], axis=1, keepdims=True), (1, cols))

    vm = pl.BlockSpec(memory_space=pltpu.VMEM)
    return pl.pallas_call(
        body, name="all_sum_small", in_specs=[vm], out_specs=vm,
        out_shape=jax.ShapeDtypeStruct((rows, cols), F32),
        scratch_shapes=[pltpu.VMEM((n_dev, rows, cols), F32),
                        pltpu.SemaphoreType.DMA((n_dev - 1,)), pltpu.SemaphoreType.DMA((n_dev - 1,))],
    )(v)


def _adamw_math(w, g, m, v):
    m = ADAM_B1 * m + (1.0 - ADAM_B1) * g
    v = ADAM_B2 * v + (1.0 - ADAM_B2) * (g * g)
    m_hat = m / (1.0 - ADAM_B1 ** ADAM_STEP)
    v_hat = v / (1.0 - ADAM_B2 ** ADAM_STEP)
    delta = -ADAM_LR * (m_hat / (jnp.sqrt(v_hat) + ADAM_EPS) + ADAM_WD * w)
    return delta, m, v


def _adamw(w, g, m, v, name):
    rows, cols = w.shape
    tm = rows // 2 if (rows // 2) % 8 == 0 else rows
    return _rowwise(_adamw_math, [w, g, m, v], [], [(cols, F32)] * 3, [], tm=tm, name=name)


def _unshard_cols(gathered):
    n, r, c = gathered.shape
    return jnp.transpose(gathered, (1, 0, 2)).reshape(r, n * c)


def _shard_cols(full):
    r, nc = full.shape
    return jnp.transpose(full.reshape(r, N_CHIPS, nc // N_CHIPS), (1, 0, 2))


def _fwd_bwd(x, loss_target, g_mix, g_ffn, g_fin, wf_in, wf_sb_up, wf_dil_up, wf_out, wf_ffn_in, wf_ffn_out):
    b_sz, s_len, d_model = x.shape
    t = b_sz * s_len
    d_ff = wf_ffn_out.shape[0]
    x2d = x.reshape(t, d_model)
    tgt2d = loss_target.reshape(t, d_model)
    wf_qkv, wf_gate = wf_in[:, :QKV_WIDTH], wf_in[:, QKV_WIDTH:]

    (u,) = _rowwise(lambda xv, g: (_rms_stats(xv)[0] * g,), [x2d], [g_mix], [(d_model, BF16)], [], tm=512, name="norm_mix")
    qkv = _mm(u, wf_qkv, tm=1024, tn=768, tk=d_model, name="proj_qkv")
    gates = _mm(u, wf_gate, tm=1024, tn=512, tk=d_model, name="proj_gates")
    qkv3 = qkv.reshape(b_sz, s_len, QKV_WIDTH)
    o_sb = _sb_fwd(qkv3, b_sz, s_len)
    o_dl, lse = _dil_fwd(qkv3, b_sz, s_len)
    o_sb2, o_dl2 = o_sb.reshape(t, SB_WIDTH), o_dl.reshape(t, DIL_OUT_WIDTH)
    y_sb = _mm(o_sb2, wf_sb_up, tm=1024, tn=1024, tk=SB_WIDTH, name="sb_up")
    y_dl = _mm(o_dl2, wf_dil_up, tm=1024, tn=1024, tk=DIL_OUT_WIDTH, name="dil_up")

    def merge_fn(gt, ys, yd):
        return (_sigmoid(gt[:, :d_model]) * ys + _sigmoid(gt[:, d_model:]) * yd,)

    (merged,) = _rowwise(merge_fn, [gates, y_sb, y_dl], [], [(d_model, BF16)], [], tm=512, name="merge")
    x1 = _mm(merged, wf_out, add=x2d, tm=512, tn=1024, tk=d_model, name="mix_out")
    (u2,) = _rowwise(lambda xv, g: (_rms_stats(xv)[0] * g,), [x1], [g_ffn], [(d_model, BF16)], [], tm=512, name="norm_ffn")
    h = _mm(u2, wf_ffn_in, tm=1024, tn=512, tk=d_model, name="ffn_in")

    def act_fn(hv):
        gate = hv[:, :d_ff]
        return (gate * _sigmoid(gate) * hv[:, d_ff:],)

    (act,) = _rowwise(act_fn, [h], [], [(d_ff, BF16)], [], tm=256, name="ffn_act")
    x2 = _mm(act, wf_ffn_out, add=x1, tm=512, tn=1024, tk=d_ff, name="ffn_out")

    def head_fn(xv, tg, g):
        xhat, r = _rms_stats(xv)
        err = xhat * g - tg
        dy = err * (1.0 / d_model)
        dx, dg_rows = _rms_bwd(dy, xhat, r, g)
        loss_lanes = (0.5 / d_model) * jnp.sum(err * err, axis=0, keepdims=True)
        return dx, jnp.sum(dg_rows, axis=0, keepdims=True), loss_lanes

    dx2, dg_fin, loss_lanes = _rowwise(head_fn, [x2, tgt2d], [g_fin], [(d_model, F32)], [(1, d_model), (1, d_model)],
                                       tm=512, name="loss_head")

    dact = _mm(dx2, wf_ffn_out, tb=True, tm=512, tn=d_ff // 2, tk=d_model, name="ffn_out_dx")
    gw_ffn_out = _mm(act, dx2, ta=True, tm=d_ff // 2, tn=1024, tk=512, name="ffn_out_dw")

    def dact_fn(hv, da):
        gate, up = hv[:, :d_ff], hv[:, d_ff:]
        sg = _sigmoid(gate)
        dgate = da * up * (sg * (1.0 + gate * (1.0 - sg)))
        return (jnp.concatenate([dgate, da * (gate * sg)], axis=1),)

    (dh,) = _rowwise(dact_fn, [h, dact], [], [(2 * d_ff, BF16)], [], tm=256, name="ffn_act_bwd")
    du2 = _mm(dh, wf_ffn_in, tb=True, tm=1024, tn=1024, tk=512, name="ffn_in_dx")
    gw_ffn_in = _mm(u2, dh, ta=True, tm=1024, tn=512, tk=1024, name="ffn_in_dw")

    def norm_bwd_fn(dres, du_, xv, g):
        xhat, r = _rms_stats(xv)
        dx, dg_rows = _rms_bwd(du_, xhat, r, g)
        return dres + dx, jnp.sum(dg_rows, axis=0, keepdims=True)

    dx1, dg_ffn = _rowwise(norm_bwd_fn, [dx2, du2, x1], [g_ffn], [(d_model, F32)], [(1, d_model)], tm=512, name="norm_ffn_bwd")

    dmerged = _mm(dx1, wf_out, tb=True, tm=512, tn=1024, tk=d_model, name="mix_out_dx")
    gw_out = _mm(merged, dx1, ta=True, tm=1024, tn=512, tk=512, name="mix_out_dw")

    def merge_bwd_fn(gt, ys, yd, dm):
        s_sb, s_dl = _sigmoid(gt[:, :d_model]), _sigmoid(gt[:, d_model:])
        dgates = jnp.concatenate([dm * ys * s_sb * (1.0 - s_sb), dm * yd * s_dl * (1.0 - s_dl)], axis=1)
        return dgates, dm * s_sb, dm * s_dl

    dgates, dy_sb, dy_dl = _rowwise(merge_bwd_fn, [gates, y_sb, y_dl, dmerged], [],
                                    [(2 * d_model, BF16), (d_model, BF16), (d_model, BF16)], [], tm=256, name="merge_bwd")
    do_sb = _mm(dy_sb, wf_sb_up, tb=True, tm=1024, tn=SB_WIDTH, tk=d_model, name="sb_up_dx")
    gw_sb_up = _mm(o_sb2, dy_sb, ta=True, tm=SB_WIDTH, tn=1024, tk=512, name="sb_up_dw")
    do_dl = _mm(dy_dl, wf_dil_up, tb=True, tm=1024, tn=DIL_OUT_WIDTH, tk=d_model, name="dil_up_dx")
    gw_dil_up = _mm(o_dl2, dy_dl, ta=True, tm=DIL_OUT_WIDTH, tn=1024, tk=512, name="dil_up_dw")
    dq_sb, dk_sb, dv_sb = _sb_bwd(qkv3, o_sb, do_sb.reshape(b_sz, s_len, SB_WIDTH), b_sz, s_len)
    dq_dl, dk_dl, dv_dl = _dil_bwd(qkv3, o_dl, lse, do_dl.reshape(b_sz, s_len, DIL_OUT_WIDTH), b_sz, s_len)
    dproj = jnp.concatenate(
        [a.reshape(t, -1) for a in (dq_sb, dk_sb, dv_sb)]
        + [a.reshape(t, -1).astype(BF16) for a in (dq_dl, dk_dl, dv_dl)] + [dgates], axis=1)
    du = _mm(dproj, wf_in, tb=True, tm=512, tn=1024, tk=wf_in.shape[1] // 2, name="proj_dx")
    gw_in = _mm(u, dproj, ta=True, tm=512, tn=wf_in.shape[1] // 2, tk=512, name="proj_dw")
    dx, dg_mix = _rowwise(norm_bwd_fn, [dx1, du, x2d], [g_mix], [(d_model, F32)], [(1, d_model)], tm=512, name="norm_mix_bwd")

    return dx, [gw_in, gw_sb_up, gw_dil_up, gw_out, gw_ffn_in, gw_ffn_out], dg_mix, dg_ffn, dg_fin, loss_lanes


def kernel(x, norm_mix_g, w_in, w_sb_up, w_dil_up, w_out, norm_ffn_g, w_ffn_in, w_ffn_out, norm_final_g, loss_target, m_norm_mix_g, m_w_in, m_w_sb_up, m_w_dil_up, m_w_out, m_norm_ffn_g, m_w_ffn_in, m_w_ffn_out, m_norm_final_g, v_norm_mix_g, v_w_in, v_w_sb_up, v_w_dil_up, v_w_out, v_norm_ffn_g, v_w_ffn_in, v_w_ffn_out, v_norm_final_g):
    b_sz, s_len, d_model = x.shape
    d_ff = w_ffn_out.shape[1] * N_CHIPS
    g_mix, g_ffn, g_fin = norm_mix_g, norm_ffn_g, norm_final_g.reshape(1, d_model)

    names = ["w_in", "w_sb_up", "w_dil_up", "w_out", "w_ffn_in", "w_ffn_out"]
    shards = {"w_in": w_in[0], "w_sb_up": w_sb_up[0], "w_dil_up": w_dil_up[0], "w_out": w_out[0],
              "w_ffn_in": w_ffn_in[0], "w_ffn_out": w_ffn_out[0]}
    gathered = dict(zip(names, _gather_weights([_cast_to_slab(shards[k], "cast_" + k) for k in names])))
    wf_in = _unshard_cols(gathered["w_in"])
    wf_sb_up = _unshard_cols(gathered["w_sb_up"])
    wf_dil_up = _unshard_cols(gathered["w_dil_up"])
    wf_out = gathered["w_out"].reshape(d_model, d_model)
    wf_ffn_in = _unshard_cols(gathered["w_ffn_in"])
    wf_ffn_out = gathered["w_ffn_out"].reshape(d_ff, d_model)

    dx, gws, dg_mix, dg_ffn, dg_fin, loss_lanes = _fwd_bwd(
        x, loss_target, g_mix, g_ffn, g_fin, wf_in, wf_sb_up, wf_dil_up, wf_out, wf_ffn_in, wf_ffn_out)
    gw_in, gw_sb_up, gw_dil_up, gw_out, gw_ffn_in, gw_ffn_out = gws

    full_grads = [_shard_cols(gw_in), _shard_cols(gw_sb_up), _shard_cols(gw_dil_up),
                  gw_out.reshape(N_CHIPS, d_model // N_CHIPS, d_model), _shard_cols(gw_ffn_in),
                  gw_ffn_out.reshape(N_CHIPS, d_ff // N_CHIPS, d_model)]
    others = _pair_exchange(full_grads)
    pair = [_pair_sum(g, o, "grad_pair_sum_" + k) for g, o, k in zip(full_grads, others, names)]
    landed = _chip_exchange([p[1] for p in pair])
    halves = [_chip_sum(p[0], l, "grad_chip_sum_" + k) for p, l, k in zip(pair, landed, names)]
    grads = dict(zip(names, _halves_to_full(halves)))

    small = jnp.concatenate([dg_mix, dg_ffn, dg_fin, loss_lanes, jnp.zeros((4, d_model), F32)], axis=0)
    small = _all_sum_small(small)
    loss = small[3, 0]
    gains = jnp.concatenate([g_mix, g_ffn, g_fin, jnp.zeros((5, d_model), F32)], axis=0)
    gains_m = jnp.concatenate([m_norm_mix_g, m_norm_ffn_g, m_norm_final_g.reshape(1, d_model), jnp.zeros((5, d_model), F32)], axis=0)
    gains_v = jnp.concatenate([v_norm_mix_g, v_norm_ffn_g, v_norm_final_g.reshape(1, d_model), jnp.ones((5, d_model), F32)], axis=0)
    gd, gm, gv = _rowwise(_adamw_math, [gains, small, gains_m, gains_v], [], [(d_model, F32)] * 3, [], tm=8, name="adamw_gains")

    moments = {"w_in": (m_w_in, v_w_in), "w_sb_up": (m_w_sb_up, v_w_sb_up), "w_dil_up": (m_w_dil_up, v_w_dil_up),
               "w_out": (m_w_out, v_w_out), "w_ffn_in": (m_w_ffn_in, v_w_ffn_in), "w_ffn_out": (m_w_ffn_out, v_w_ffn_out)}
    upd = {k: _adamw(shards[k], grads[k], moments[k][0][0], moments[k][1][0], "adamw_" + k) for k in names}

    def w_out_of(i):
        return [upd[k][i][None] for k in names]

    def ordered(mix, ws, ffn_g, fin):
        return [mix, ws[0], ws[1], ws[2], ws[3], ffn_g, ws[4], ws[5], fin]

    grad_ws = [grads[k][None] for k in names]
    outs = [loss, dx.reshape(b_sz, s_len, d_model)]
    outs += ordered(small[0:1], grad_ws, small[1:2], small[2])
    outs += ordered(gd[0:1], w_out_of(0), gd[1:2], gd[2])
    outs += ordered(gm[0:1], w_out_of(1), gm[1:2], gm[2])
    outs += ordered(gv[0:1], w_out_of(2), gv[1:2], gv[2])
    return tuple(outs)
```

```python
import functools
import math

import jax
import jax.numpy as jnp
from jax import lax
from jax.experimental import pallas as pl
from jax.experimental.pallas import tpu as pltpu

F32 = jnp.float32
BF16 = jnp.bfloat16
MESH = pl.DeviceIdType.MESH

HEAD_DIM = 64
SB_HEADS = 8
DIL_PAIRS = ((128, 1), (512, 4), (2048, 16))
DIL_HEADS_PER_GROUP = 4
DIL_HEADS = DIL_HEADS_PER_GROUP * len(DIL_PAIRS)
SB_WIDTH = SB_HEADS * HEAD_DIM
DIL_WIDTH = DIL_HEADS * HEAD_DIM
DIL_OUT_WIDTH = DIL_HEADS_PER_GROUP * HEAD_DIM
QKV_WIDTH = 3 * SB_WIDTH + 3 * DIL_WIDTH
RMS_EPS = 1e-6
ALIBI_MAX_BIAS = 8.0
ADAM_LR = 0.001
ADAM_B1 = 0.9
ADAM_B2 = 0.999
ADAM_EPS = 1e-08
ADAM_WD = 0.01
ADAM_STEP = 10

LANES = 128
BLK = 128
NEG = -1e30
SB_FWD_CHAINS = 4
SB_BWD_CHAINS = 4
N_CHIPS = 4
VMEM_CAP = 56 * 1024 * 1024


def _vmem_limit(tile_bytes):
    return int(min(VMEM_CAP, max(32 * 1024 * 1024, 3 * tile_bytes + 8 * 1024 * 1024)))


def _nbytes(shape, dtype):
    return math.prod(shape) * jnp.dtype(dtype).itemsize


def _dot(a, b):
    return jnp.dot(a, b, preferred_element_type=F32)


def _dot_nt(a, b):
    return lax.dot_general(a, b, (((1,), (1,)), ((), ())), preferred_element_type=F32)


def _dot_tn(a, b):
    return lax.dot_general(a, b, (((0,), (0,)), ((), ())), preferred_element_type=F32)


def _split2(x):
    hi = x.astype(BF16)
    lo = (x - hi.astype(F32)).astype(BF16)
    return hi, lo


def _sigmoid(x):
    return 1.0 / (1.0 + jnp.exp(-x))


def _mm(a, b, *, ta=False, tb=False, add=None, out_dtype=F32, tm, tn, tk, name):
    if ta:
        kdim, m = a.shape
    else:
        m, kdim = a.shape
    if tb:
        n, k2 = b.shape
    else:
        k2, n = b.shape
    assert kdim == k2 and m % tm == 0 and n % tn == 0 and kdim % tk == 0, (name, a.shape, b.shape)
    nk = kdim // tk
    a_spec = pl.BlockSpec((tk, tm), lambda i, j, k: (k, i)) if ta else pl.BlockSpec((tm, tk), lambda i, j, k: (i, k))
    b_spec = pl.BlockSpec((tn, tk), lambda i, j, k: (j, k)) if tb else pl.BlockSpec((tk, tn), lambda i, j, k: (k, j))
    o_spec = pl.BlockSpec((tm, tn), lambda i, j, k: (i, j))
    dims = ((((0,) if ta else (1,)), ((1,) if tb else (0,))), ((), ()))
    has_add = add is not None
    use_scratch = nk > 1 and jnp.dtype(out_dtype) != jnp.dtype(F32)

    def body(*refs):
        a_ref, b_ref = refs[0], refs[1]
        add_ref = refs[2] if has_add else None
        o_ref = refs[3] if has_add else refs[2]
        acc_ref = refs[-1] if use_scratch else o_ref
        prod = lax.dot_general(a_ref[...].astype(BF16), b_ref[...].astype(BF16), dims, preferred_element_type=F32)
        if nk == 1:
            if has_add:
                prod = prod + add_ref[...]
            o_ref[...] = prod.astype(out_dtype)
            return
        k = pl.program_id(2)

        @pl.when(k == 0)
        def _():
            acc_ref[...] = prod + add_ref[...] if has_add else prod

        @pl.when(k > 0)
        def _():
            acc_ref[...] += prod

        if use_scratch:
            @pl.when(k == nk - 1)
            def _():
                o_ref[...] = acc_ref[...].astype(out_dtype)

    tile_bytes = (_nbytes((tm, tk), a.dtype) + _nbytes((tk, tn), b.dtype) + _nbytes((tm, tn), out_dtype)
                  + (_nbytes((tm, tn), F32) if has_add else 0) + _nbytes((tm, tn), F32))
    in_specs = [a_spec, b_spec] + ([o_spec] if has_add else [])
    args = (a, b) + ((add,) if has_add else ())
    return pl.pallas_call(
        body, name=name, grid=(m // tm, n // tn, nk),
        in_specs=in_specs, out_specs=o_spec,
        out_shape=jax.ShapeDtypeStruct((m, n), out_dtype),
        scratch_shapes=[pltpu.VMEM((tm, tn), F32)] if use_scratch else [],
        compiler_params=pltpu.CompilerParams(
            dimension_semantics=("parallel", "parallel", "arbitrary"),
            vmem_limit_bytes=_vmem_limit(tile_bytes)),
    )(*args)


def _rowwise(fn, rows, params, outs, accs, *, tm, name):
    t = rows[0].shape[0]
    assert t % tm == 0, (name, t, tm)
    n_r, n_p, n_o = len(rows), len(params), len(outs)

    def body(*refs):
        vals = [r[...] for r in refs[:n_r + n_p]]
        res = fn(*vals)
        o_refs = refs[n_r + n_p:n_r + n_p + n_o]
        a_refs = refs[n_r + n_p + n_o:]
        for o_ref, v in zip(o_refs, res[:n_o]):
            o_ref[...] = v.astype(o_ref.dtype)
        if accs:
            @pl.when(pl.program_id(0) == 0)
            def _():
                for a_ref in a_refs:
                    a_ref[...] = jnp.zeros(a_ref.shape, F32)

            for a_ref, v in zip(a_refs, res[n_o:]):
                a_ref[...] += v

    in_specs = [pl.BlockSpec((tm, r.shape[1]), lambda i: (i, 0)) for r in rows]
    in_specs += [pl.BlockSpec(p.shape, lambda i: (0, 0)) for p in params]
    out_specs = [pl.BlockSpec((tm, w), lambda i: (i, 0)) for w, _ in outs]
    out_specs += [pl.BlockSpec(s, lambda i: (0, 0)) for s in accs]
    out_shape = [jax.ShapeDtypeStruct((t, w), d) for w, d in outs]
    out_shape += [jax.ShapeDtypeStruct(s, F32) for s in accs]
    tile_bytes = sum(_nbytes((tm, r.shape[1]), r.dtype) for r in rows) + sum(_nbytes((tm, w), F32) for w, _ in outs)
    res = pl.pallas_call(
        body, name=name, grid=(t // tm,), in_specs=in_specs, out_specs=out_specs, out_shape=out_shape,
        compiler_params=pltpu.CompilerParams(
            dimension_semantics=("arbitrary",) if accs else ("parallel",),
            vmem_limit_bytes=_vmem_limit(2 * tile_bytes)),
    )(*rows, *params)
    return res


def _rms_stats(x):
    r = lax.rsqrt(jnp.mean(x * x, axis=-1, keepdims=True) + RMS_EPS)
    return x * r, r


def _rms_bwd(dy, xhat, r, g):
    dxhat = dy * g
    dx = r * (dxhat - xhat * jnp.mean(dxhat * xhat, axis=-1, keepdims=True))
    return dx, dy * xhat


def _sb_consts():
    lane = lax.broadcasted_iota(jnp.int32, (BLK, LANES), 1)
    head0 = lane < HEAD_DIM
    row = lax.broadcasted_iota(jnp.int32, (2 * BLK, BLK), 0) % BLK
    col = lax.broadcasted_iota(jnp.int32, (2 * BLK, BLK), 1)
    causal = col < row
    jj = lax.broadcasted_iota(jnp.int32, (BLK, BLK), 0)
    ss = lax.broadcasted_iota(jnp.int32, (BLK, BLK), 1)
    suffix = jnp.where(jj > ss, 1.0, 0.0).astype(BF16)
    return head0, causal, suffix


def _stack_heads(x, head0):
    zero = jnp.zeros_like(x)
    return jnp.concatenate([jnp.where(head0, x, zero), jnp.where(head0, zero, x)], axis=0)


def _sb_logits(z, causal, masked):
    sp = jnp.log(1.0 + jnp.exp(-jnp.abs(z)))
    log_keep = -(jnp.maximum(z, 0.0) + sp)
    log_beta = jnp.minimum(z, 0.0) - sp
    if masked:
        log_keep = jnp.where(causal, log_keep, 0.0)
    return log_keep, log_beta


def _suffix_sums(x, suffix):
    hi, lo = _split2(x)
    after = _dot(hi, suffix) + _dot(lo, suffix)
    total = jnp.broadcast_to(after[:, 0:1] + x[:, 0:1], x.shape)
    return after, total


def _lane_blocks(x, n):
    return [x[:, p * LANES:(p + 1) * LANES] for p in range(n)]


def _sb_fwd(qkv, b_sz, s_len, slabs):
    n_slabs = len(slabs)
    nq = s_len // BLK
    n_pairs = SB_WIDTH // LANES
    ch = SB_FWD_CHAINS
    n_steps = n_pairs // ch
    scale = 1.0 / math.sqrt(HEAD_DIM)

    def body(q_ref, k_ref, v_ref, *rest):
        o_ref = rest[n_slabs]
        slab_refs = rest[n_slabs + 1:2 * n_slabs + 1]
        send_sems, recv_sems = rest[2 * n_slabs + 1:]
        step = pl.program_id(0) * n_steps + pl.program_id(1)
        head0, causal, suffix = _sb_consts()

        @pl.when(step == 0)
        def _():
            _gather_issue(slab_refs, send_sems, recv_sems)

        def q_block(i, _):
            qs = pl.multiple_of(i * BLK, BLK)
            q_all = (q_ref[pl.ds(qs, BLK), :] * scale).astype(BF16)
            q01 = [_stack_heads(q, head0) for q in _lane_blocks(q_all, ch)]

            def tile(j, state, masked):
                ks = pl.multiple_of(j * BLK, BLK)
                ks_ = _lane_blocks(k_ref[pl.ds(ks, BLK), :].astype(BF16), ch)
                vs_ = _lane_blocks(v_ref[pl.ds(ks, BLK), :].astype(BF16), ch)
                zs = [_dot_nt(q01[p], ks_[p]) for p in range(ch)]
                logits = [_sb_logits(z, causal, masked) for z in zs]
                sums = [_suffix_sums(lg[0], suffix) for lg in logits]
                out = []
                for p in range(ch):
                    carry, acc = state[2 * p], state[2 * p + 1]
                    after, total = sums[p]
                    a = jnp.exp(logits[p][1] + carry + after)
                    if masked:
                        a = jnp.where(causal, a, 0.0)
                    a_hi, a_lo = _split2(a)
                    a_cat = jnp.concatenate([a_hi[:BLK], a_hi[BLK:], a_lo[:BLK], a_lo[BLK:]], axis=1)
                    v01 = _stack_heads(vs_[p], head0)
                    out += [carry + total, acc + _dot(a_cat, jnp.concatenate([v01, v01], axis=0))]
                return tuple(out)

            state = (jnp.zeros((2 * BLK, BLK), F32), jnp.zeros((BLK, LANES), F32)) * ch
            state = tile(i, state, True)
            state = lax.fori_loop(0, i, lambda t, st: tile(i - 1 - t, st, False), state)
            o_ref[pl.ds(qs, BLK), :] = jnp.concatenate([state[2 * p + 1] for p in range(ch)], axis=1)
            return 0

        lax.fori_loop(0, nq, q_block, 0)

        @pl.when(step == b_sz * n_steps - 1)
        def _():
            _gather_complete(slab_refs, send_sems, recv_sems)

    blk = lambda off: pl.BlockSpec((None, s_len, ch * LANES), lambda b, p: (b, 0, off + p))
    res = pl.pallas_call(
        body, name="sb_fwd", grid=(b_sz, n_steps),
        in_specs=[blk(0), blk(n_steps), blk(2 * n_steps)] + _hbm_specs(n_slabs),
        out_specs=[blk(0)] + _hbm_specs(n_slabs),
        out_shape=[jax.ShapeDtypeStruct((b_sz, s_len, SB_WIDTH), F32)]
        + [jax.ShapeDtypeStruct(a.shape, a.dtype) for a in slabs],
        input_output_aliases={3 + k: 1 + k for k in range(n_slabs)},
        scratch_shapes=_gather_sems(n_slabs),
        compiler_params=pltpu.CompilerParams(dimension_semantics=("arbitrary", "arbitrary"),
                                             vmem_limit_bytes=VMEM_CAP),
    )(qkv, qkv, qkv, *slabs)
    return res[0], res[1:]


def _sb_bwd(qkv, o_sb, do_sb, b_sz, s_len, sums_bf16):
    n_sums = len(sums_bf16)
    nq = s_len // BLK
    n_pairs = SB_WIDTH // LANES
    ch = SB_BWD_CHAINS
    n_steps = n_pairs // ch
    scale = 1.0 / math.sqrt(HEAD_DIM)

    def body(q_ref, k_ref, v_ref, o_ref, do_ref, *rest):
        sum_refs = rest[:n_sums]
        dq_ref, dk_ref, dv_ref = rest[n_sums:n_sums + 3]
        land_refs = rest[n_sums + 3:2 * n_sums + 3]
        dk_acc, dv_acc, send_sems, recv_sems = rest[2 * n_sums + 3:]
        step = pl.program_id(0) * n_steps + pl.program_id(1)
        copies = _chip_copies(sum_refs, land_refs, send_sems, recv_sems)

        @pl.when(step == 0)
        def _():
            for cp in copies:
                cp.start()

        head0, causal, suffix = _sb_consts()
        lrow = lax.broadcasted_iota(jnp.int32, (LANES, LANES), 0)
        ones_h0 = jnp.where(lrow < HEAD_DIM, 1.0, 0.0).astype(BF16)
        ones_h1 = jnp.where(lrow >= HEAD_DIM, 1.0, 0.0).astype(BF16)
        dk_acc[...] = jnp.zeros(dk_acc.shape, F32)
        dv_acc[...] = jnp.zeros(dv_acc.shape, F32)

        def q_block(i, _):
            qs = pl.multiple_of(i * BLK, BLK)
            q_all = (q_ref[pl.ds(qs, BLK), :] * scale).astype(BF16)
            do_all = do_ref[pl.ds(qs, BLK), :].astype(BF16)
            dd_all = do_all.astype(F32) * o_ref[pl.ds(qs, BLK), :]
            q01 = [_stack_heads(q, head0) for q in _lane_blocks(q_all, ch)]
            do01 = [_stack_heads(d, head0) for d in _lane_blocks(do_all, ch)]
            tot = []
            for dd in _lane_blocks(dd_all, ch):
                dd_hi, dd_lo = _split2(dd)
                tot.append(jnp.concatenate([_dot(dd_hi, ones_h0) + _dot(dd_lo, ones_h0),
                                            _dot(dd_hi, ones_h1) + _dot(dd_lo, ones_h1)], axis=0))

            def tile(j, state, masked):
                ks = pl.multiple_of(j * BLK, BLK)
                ks_ = _lane_blocks(k_ref[pl.ds(ks, BLK), :].astype(BF16), ch)
                vs_ = _lane_blocks(v_ref[pl.ds(ks, BLK), :].astype(BF16), ch)
                zs = [_dot_nt(q01[p], ks_[p]) for p in range(ch)]
                das = [_dot_nt(do01[p], vs_[p]) for p in range(ch)]
                logits = [_sb_logits(z, causal, masked) for z in zs]
                sums = [_suffix_sums(lg[0], suffix) for lg in logits]
                a_s, e_s = [], []
                for p in range(ch):
                    a = jnp.exp(logits[p][1] + state[3 * p] + sums[p][0])
                    if masked:
                        a = jnp.where(causal, a, 0.0)
                    a_s.append(a)
                    e_s.append(a * das[p])
                e_sums = [_suffix_sums(e, suffix) for e in e_s]
                out, dks, dvs = [], [], []
                for p in range(ch):
                    carry, rcarry, dq = state[3 * p:3 * p + 3]
                    e = e_s[p]
                    before = tot[p] - (rcarry + e_sums[p][0] + e)
                    beta = jnp.exp(logits[p][1])
                    dz = e * (1.0 - beta) - beta * before
                    if masked:
                        dz = jnp.where(causal, dz, 0.0)
                    dz_b = dz.astype(BF16)
                    dks.append(_dot_tn(dz_b, q01[p]))
                    dvs.append(_dot_tn(a_s[p].astype(BF16), do01[p]))
                    out += [carry + sums[p][1], rcarry + e_sums[p][1], dq + _dot(dz_b, ks_[p])]
                dk_acc[pl.ds(ks, BLK), :] += jnp.concatenate(dks, axis=1)
                dv_acc[pl.ds(ks, BLK), :] += jnp.concatenate(dvs, axis=1)
                return tuple(out)

            state = (jnp.zeros((2 * BLK, BLK), F32),) * (3 * ch)
            state = tile(i, state, True)
            state = lax.fori_loop(0, i, lambda t, st: tile(i - 1 - t, st, False), state)
            dq = [jnp.where(head0, state[3 * p + 2][:BLK], state[3 * p + 2][BLK:]) for p in range(ch)]
            dq_ref[pl.ds(qs, BLK), :] = (jnp.concatenate(dq, axis=1) * scale).astype(dq_ref.dtype)
            return 0

        lax.fori_loop(0, nq, q_block, 0)
        dk_ref[...] = dk_acc[...].astype(dk_ref.dtype)
        dv_ref[...] = dv_acc[...].astype(dv_ref.dtype)

        @pl.when(step == b_sz * n_steps - 1)
        def _():
            for cp in copies:
                cp.wait()

    blk = lambda off: pl.BlockSpec((None, s_len, ch * LANES), lambda b, p: (b, 0, off + p))
    once = lambda off: pl.BlockSpec((None, s_len, ch * LANES), lambda b, p: (b, 0, off + p),
                                    pipeline_mode=pl.Buffered(1))
    out_sd = jax.ShapeDtypeStruct((b_sz, s_len, SB_WIDTH), BF16)
    res = pl.pallas_call(
        body, name="sb_bwd", grid=(b_sz, n_steps),
        in_specs=[once(0), once(n_steps), once(2 * n_steps), once(0), once(0)] + _hbm_specs(n_sums),
        out_specs=[blk(0), blk(0), blk(0)] + _hbm_specs(n_sums),
        out_shape=[out_sd, out_sd, out_sd] + _chip_landing(sums_bf16),
        scratch_shapes=[pltpu.VMEM((s_len, ch * LANES), F32), pltpu.VMEM((s_len, ch * LANES), F32)] + _chip_sems(n_sums),
        compiler_params=pltpu.CompilerParams(dimension_semantics=("arbitrary", "arbitrary"),
                                             vmem_limit_bytes=VMEM_CAP),
    )(qkv, qkv, qkv, o_sb, do_sb, *sums_bf16)
    return res[:3], res[3:]


def _dil_consts(group, pair_idx, dilation):
    lane = lax.broadcasted_iota(jnp.int32, (BLK, LANES), 1)
    head0 = lane < HEAD_DIM
    row = lax.broadcasted_iota(jnp.int32, (2 * BLK, BLK), 0)
    qa = row % BLK
    kb = lax.broadcasted_iota(jnp.int32, (2 * BLK, BLK), 1)
    head = (group * DIL_HEADS_PER_GROUP + 2 * pair_idx + row // BLK).astype(F32)
    slope = jnp.exp((-ALIBI_MAX_BIAS * math.log(2.0) / DIL_HEADS) * (head + 1.0))
    valid_cur = kb <= qa
    valid_prev = kb >= qa
    bias_cur = -slope * ((qa - kb) * dilation).astype(F32)
    bias_prev = -slope * ((BLK + qa - kb) * dilation).astype(F32)
    return head0, valid_cur, valid_prev, bias_cur, bias_prev


def _dil_units(s_len, dilation):
    nb = s_len // dilation // BLK
    return [(r, n) for r in range(dilation) for n in range(nb)]


def _dil_rows(n, r, dilation):
    if dilation == 1:
        return pl.ds(n * BLK, BLK)
    return pl.ds(n * BLK * dilation + r, BLK, stride=dilation)


def _dil_scores(q01, k, bias, valid):
    s = _dot_nt(q01, k) * (1.0 / math.sqrt(HEAD_DIM)) + bias
    return jnp.where(valid, s, NEG)


def _dil_fwd(qkv, b_sz, s_len):
    n_pairs = DIL_OUT_WIDTH // LANES
    q_off = 3 * SB_WIDTH // LANES
    per_kind = DIL_WIDTH // LANES

    def body(*refs):
        qkv_refs = refs[:9]
        o_ref, lse_ref, m_s, l_s = refs[9:]
        pair_idx = pl.program_id(1)
        m_s[...] = jnp.full(m_s.shape, NEG, F32)
        l_s[...] = jnp.zeros(l_s.shape, F32)
        o_ref[...] = jnp.zeros(o_ref.shape, F32)
        for g, (_, dilation) in enumerate(DIL_PAIRS):
            q_ref, k_ref, v_ref = qkv_refs[3 * g:3 * g + 3]
            head0, valid_cur, valid_prev, bias_cur, bias_prev = _dil_consts(g, pair_idx, dilation)
            for r, n in _dil_units(s_len, dilation):
                rows = _dil_rows(n, r, dilation)
                q01 = _stack_heads(q_ref[rows, :].astype(BF16), head0)
                k_c = k_ref[rows, :].astype(BF16)
                v_c = v_ref[rows, :].astype(BF16)
                scores = [_dil_scores(q01, k_c, bias_cur, valid_cur)]
                values = [_stack_heads(v_c, head0)]
                if n > 0:
                    prev = _dil_rows(n - 1, r, dilation)
                    scores.append(_dil_scores(q01, k_ref[prev, :].astype(BF16), bias_prev, valid_prev))
                    values.append(_stack_heads(v_ref[prev, :].astype(BF16), head0))
                m_blk = functools.reduce(jnp.maximum, [jnp.max(s, axis=-1, keepdims=True) for s in scores])
                m_old = jnp.concatenate([m_s.at[0][rows, :], m_s.at[1][rows, :]], axis=0)
                l_old = jnp.concatenate([l_s.at[0][rows, :], l_s.at[1][rows, :]], axis=0)
                m_new = jnp.maximum(m_old, m_blk)
                probs = [jnp.exp(s - m_new) for s in scores]
                l_blk = functools.reduce(jnp.add, [jnp.sum(p, axis=-1, keepdims=True) for p in probs])
                alpha = jnp.exp(m_old - m_new)
                l_new = alpha * l_old + l_blk
                alpha_tok = jnp.where(head0, alpha[:BLK], alpha[BLK:])
                p_cat = jnp.concatenate([h for p in probs for h in (p[:BLK].astype(BF16), p[BLK:].astype(BF16))], axis=1)
                o_ref[rows, :] = alpha_tok * o_ref[rows, :] + _dot(p_cat, jnp.concatenate(values, axis=0))
                m_s.at[0][rows, :] = m_new[:BLK]
                m_s.at[1][rows, :] = m_new[BLK:]
                l_s.at[0][rows, :] = l_new[:BLK]
                l_s.at[1][rows, :] = l_new[BLK:]
        lane = lax.broadcasted_iota(jnp.int32, (BLK, LANES), 1)
        for c in range(s_len // BLK):
            rows = pl.ds(c * BLK, BLK)
            l0, l1 = l_s.at[0][rows, :], l_s.at[1][rows, :]
            o_ref[rows, :] = o_ref[rows, :] / jnp.where(lane < HEAD_DIM, l0, l1)
            lse_ref.at[0][rows, :] = m_s.at[0][rows, :] + jnp.log(l0)
            lse_ref.at[1][rows, :] = m_s.at[1][rows, :] + jnp.log(l1)

    in_specs = []
    for g in range(len(DIL_PAIRS)):
        for kind in range(3):
            off = q_off + kind * per_kind + g * n_pairs
            in_specs.append(pl.BlockSpec((None, s_len, LANES), lambda b, p, off=off: (b, 0, off + p)))
    return pl.pallas_call(
        body, name="dil_fwd", grid=(b_sz, n_pairs),
        in_specs=in_specs,
        out_specs=[pl.BlockSpec((None, s_len, LANES), lambda b, p: (b, 0, p)),
                   pl.BlockSpec((None, None, 2, s_len, LANES), lambda b, p: (b, p, 0, 0, 0))],
        out_shape=[jax.ShapeDtypeStruct((b_sz, s_len, DIL_OUT_WIDTH), F32),
                   jax.ShapeDtypeStruct((b_sz, n_pairs, 2, s_len, LANES), F32)],
        scratch_shapes=[pltpu.VMEM((2, s_len, LANES), F32), pltpu.VMEM((2, s_len, LANES), F32)],
        compiler_params=pltpu.CompilerParams(dimension_semantics=("parallel", "parallel"),
                                             vmem_limit_bytes=VMEM_CAP),
    )(*([qkv] * 9))


def _dil_bwd(qkv, o_dl, lse, do_dl, b_sz, s_len):
    n_pairs = DIL_OUT_WIDTH // LANES
    n_groups = len(DIL_PAIRS)
    q_off = 3 * SB_WIDTH // LANES
    per_kind = DIL_WIDTH // LANES

    def body(q_ref, k_ref, v_ref, o_ref, lse_ref, do_ref, dq_ref, dk_ref, dv_ref, d_s):
        pair_idx = pl.program_id(1)
        group = pl.program_id(2)
        lrow = lax.broadcasted_iota(jnp.int32, (LANES, LANES), 0)
        ones_h0 = jnp.where(lrow < HEAD_DIM, 1.0, 0.0).astype(BF16)
        ones_h1 = jnp.where(lrow >= HEAD_DIM, 1.0, 0.0).astype(BF16)
        for c in range(s_len // BLK):
            rows = pl.ds(c * BLK, BLK)
            dd_hi, dd_lo = _split2(do_ref[rows, :] * o_ref[rows, :])
            d_s.at[0][rows, :] = _dot(dd_hi, ones_h0) + _dot(dd_lo, ones_h0)
            d_s.at[1][rows, :] = _dot(dd_hi, ones_h1) + _dot(dd_lo, ones_h1)
        dk_ref[...] = jnp.zeros(dk_ref.shape, F32)
        dv_ref[...] = jnp.zeros(dv_ref.shape, F32)

        def one_group(g, dilation):
            head0, valid_cur, valid_prev, bias_cur, bias_prev = _dil_consts(g, pair_idx, dilation)
            for r, n in _dil_units(s_len, dilation):
                rows = _dil_rows(n, r, dilation)
                q01 = _stack_heads(q_ref[rows, :].astype(BF16), head0)
                do01 = _stack_heads(do_ref[rows, :].astype(BF16), head0)
                lse01 = jnp.concatenate([lse_ref.at[0][rows, :], lse_ref.at[1][rows, :]], axis=0)
                d01 = jnp.concatenate([d_s.at[0][rows, :], d_s.at[1][rows, :]], axis=0)
                dq = jnp.zeros((2 * BLK, LANES), F32)
                blocks = [(rows, bias_cur, valid_cur)]
                if n > 0:
                    blocks.append((_dil_rows(n - 1, r, dilation), bias_prev, valid_prev))
                for krows, bias, valid in blocks:
                    k = k_ref[krows, :].astype(BF16)
                    v = v_ref[krows, :].astype(BF16)
                    p = jnp.exp(_dil_scores(q01, k, bias, valid) - lse01)
                    ds = (p * (_dot_nt(do01, v) - d01) * (1.0 / math.sqrt(HEAD_DIM))).astype(BF16)
                    dq = dq + _dot(ds, k)
                    dk_ref[krows, :] = dk_ref[krows, :] + _dot_tn(ds, q01)
                    dv_ref[krows, :] = dv_ref[krows, :] + _dot_tn(p.astype(BF16), do01)
                dq_ref[rows, :] = jnp.where(head0, dq[:BLK], dq[BLK:])

        for g, (_, dilation) in enumerate(DIL_PAIRS):
            pl.when(group == g)(functools.partial(one_group, g, dilation))

    def qkv_spec(kind):
        return pl.BlockSpec((None, s_len, LANES),
                            lambda b, p, g: (b, 0, q_off + kind * per_kind + g * n_pairs + p))

    tok_spec = pl.BlockSpec((None, s_len, LANES), lambda b, p, g: (b, 0, p))
    out_spec = pl.BlockSpec((None, s_len, LANES), lambda b, p, g: (b, 0, g * n_pairs + p))
    out_sd = jax.ShapeDtypeStruct((b_sz, s_len, DIL_WIDTH), F32)
    return pl.pallas_call(
        body, name="dil_bwd", grid=(b_sz, n_pairs, n_groups),
        in_specs=[qkv_spec(0), qkv_spec(1), qkv_spec(2), tok_spec,
                  pl.BlockSpec((None, None, 2, s_len, LANES), lambda b, p, g: (b, p, 0, 0, 0)), tok_spec],
        out_specs=[out_spec, out_spec, out_spec],
        out_shape=[out_sd, out_sd, out_sd],
        scratch_shapes=[pltpu.VMEM((2, s_len, LANES), F32)],
        compiler_params=pltpu.CompilerParams(dimension_semantics=("parallel", "parallel", "arbitrary"),
                                             vmem_limit_bytes=VMEM_CAP),
    )(qkv, qkv, qkv, o_dl, lse, do_dl)


def _mesh_pos():
    return lax.axis_index("x"), lax.axis_index("y"), lax.axis_index("c")


def _other_chips(x, y):
    return [(1 - x, y), (x, 1 - y), (1 - x, 1 - y)]


def _hbm_specs(n):
    return [pl.BlockSpec(memory_space=pl.ANY)] * n


def _cast_to_slab(w, name):
    rows, cols = w.shape
    mine = jnp.reshape(2 * lax.axis_index("x") + lax.axis_index("y"), (1,)).astype(jnp.int32)

    def body(idx_ref, w_ref, o_ref):
        o_ref[...] = w_ref[...].astype(BF16)

    return pl.pallas_call(
        body, name=name,
        grid_spec=pltpu.PrefetchScalarGridSpec(
            num_scalar_prefetch=1, grid=(1,),
            in_specs=[pl.BlockSpec((rows, cols), lambda i, idx: (0, 0))],
            out_specs=pl.BlockSpec((None, rows, cols), lambda i, idx: (idx[0], 0, 0))),
        out_shape=jax.ShapeDtypeStruct((N_CHIPS, rows, cols), BF16),
        compiler_params=pltpu.CompilerParams(vmem_limit_bytes=_vmem_limit(rows * cols * 6)),
    )(mine, w)


def _gather_issue(slabs, send_sems, recv_sems):
    x, y, c = _mesh_pos()
    for k, slab in enumerate(slabs):
        half = slab.shape[1] // 2
        rows = slab.at[2 * x + y, pl.ds(c * half, half), :]
        for r, (px, py) in enumerate(_other_chips(x, y)):
            pltpu.make_async_remote_copy(
                src_ref=rows, dst_ref=rows, send_sem=send_sems.at[6 * k + r], recv_sem=recv_sems.at[6 * k + r],
                device_id=(px, py, c), device_id_type=MESH).start()


def _gather_complete(slabs, send_sems, recv_sems):
    x, y, c = _mesh_pos()
    chips = _other_chips(x, y)

    def copy(k, sem, block, rows, to):
        ref = slabs[k].at[block, rows, :]
        return pltpu.make_async_remote_copy(
            src_ref=ref, dst_ref=ref, send_sem=send_sems.at[sem], recv_sem=recv_sems.at[sem],
            device_id=to, device_id_type=MESH)

    for k, slab in enumerate(slabs):
        half = slab.shape[1] // 2
        for r, (px, py) in enumerate(chips):
            copy(k, 6 * k + r, 2 * px + py, pl.ds(c * half, half), (px, py, c)).wait_recv()
            copy(k, 6 * k + 3 + r, 2 * px + py, pl.ds(c * half, half), (x, y, 1 - c)).start()
    for k, slab in enumerate(slabs):
        half = slab.shape[1] // 2
        for r, (px, py) in enumerate(chips):
            copy(k, 6 * k + 3 + r, 2 * px + py, pl.ds((1 - c) * half, half), (x, y, 1 - c)).wait_recv()
    for k, slab in enumerate(slabs):
        half = slab.shape[1] // 2
        for r, (px, py) in enumerate(chips):
            copy(k, 6 * k + r, 2 * x + y, pl.ds(c * half, half), (px, py, c)).wait_send()
            copy(k, 6 * k + 3 + r, 2 * px + py, pl.ds(c * half, half), (x, y, 1 - c)).wait_send()


def _gather_sems(n):
    return [pltpu.SemaphoreType.DMA((6 * n,)), pltpu.SemaphoreType.DMA((6 * n,))]


def _gather_weights(slabs):
    n = len(slabs)

    def body(*refs):
        outs = refs[n:2 * n]
        send_sems, recv_sems = refs[2 * n:]
        _gather_issue(outs, send_sems, recv_sems)
        _gather_complete(outs, send_sems, recv_sems)

    return pl.pallas_call(
        body, name="gather_weights",
        in_specs=_hbm_specs(n), out_specs=_hbm_specs(n),
        out_shape=[jax.ShapeDtypeStruct(s.shape, s.dtype) for s in slabs],
        input_output_aliases={k: k for k in range(n)},
        scratch_shapes=_gather_sems(n),
    )(*slabs)


def _pair_exchange(grads, tag):
    n = len(grads)

    def body(*refs):
        ins, outs = refs[:n], refs[n:2 * n]
        send_sems, recv_sems = refs[2 * n:]
        x, y, c = _mesh_pos()
        copies = []
        for k in range(n):
            half = grads[k].shape[1] // 2
            cp = pltpu.make_async_remote_copy(
                src_ref=ins[k].at[:, pl.ds((1 - c) * half, half), :], dst_ref=outs[k],
                send_sem=send_sems.at[k], recv_sem=recv_sems.at[k],
                device_id=(x, y, 1 - c), device_id_type=MESH)
            cp.start()
            copies.append(cp)
        for cp in copies:
            cp.wait()

    return pl.pallas_call(
        body, name="grad_pair_exchange_" + tag,
        in_specs=_hbm_specs(n), out_specs=_hbm_specs(n),
        out_shape=[jax.ShapeDtypeStruct((N_CHIPS, g.shape[1] // 2, g.shape[2]), F32) for g in grads],
        scratch_shapes=[pltpu.SemaphoreType.DMA((n,)), pltpu.SemaphoreType.DMA((n,))],
    )(*grads)


def _pair_sum(grad, other, name):
    _, rows, cols = grad.shape
    half = rows // 2
    core = jnp.reshape(lax.axis_index("c"), (1,)).astype(jnp.int32)

    def body(core_ref, g_ref, p_ref, s_ref, sb_ref):
        s = g_ref[...] + p_ref[...]
        s_ref[...] = s
        sb_ref[...] = s.astype(BF16)

    blk = pl.BlockSpec((None, half, cols), lambda p, core_ref: (p, 0, 0))
    return pl.pallas_call(
        body, name=name,
        grid_spec=pltpu.PrefetchScalarGridSpec(
            num_scalar_prefetch=1, grid=(N_CHIPS,),
            in_specs=[pl.BlockSpec((None, half, cols), lambda p, core_ref: (p, core_ref[0], 0)), blk],
            out_specs=[blk, blk]),
        out_shape=[jax.ShapeDtypeStruct((N_CHIPS, half, cols), F32),
                   jax.ShapeDtypeStruct((N_CHIPS, half, cols), BF16)],
        compiler_params=pltpu.CompilerParams(dimension_semantics=("parallel",),
                                             vmem_limit_bytes=_vmem_limit(4 * half * cols * 4)),
    )(core, grad, other)


def _chip_copies(sums_bf16, lands, send_sems, recv_sems):
    x, y, c = _mesh_pos()
    return [pltpu.make_async_remote_copy(
        src_ref=sums_bf16[k].at[2 * px + py], dst_ref=lands[k].at[r],
        send_sem=send_sems.at[3 * k + r], recv_sem=recv_sems.at[3 * k + r],
        device_id=(px, py, c), device_id_type=MESH)
        for k in range(len(sums_bf16)) for r, (px, py) in enumerate(_other_chips(x, y))]


def _chip_sems(n):
    return [pltpu.SemaphoreType.DMA((3 * n,)), pltpu.SemaphoreType.DMA((3 * n,))]


def _chip_landing(sums_bf16):
    return [jax.ShapeDtypeStruct((N_CHIPS - 1,) + s.shape[1:], BF16) for s in sums_bf16]


def _chip_exchange(sums_bf16):
    n = len(sums_bf16)

    def body(*refs):
        copies = _chip_copies(refs[:n], refs[n:2 * n], *refs[2 * n:])
        for cp in copies:
            cp.start()
        for cp in copies:
            cp.wait()

    return pl.pallas_call(
        body, name="grad_chip_exchange",
        in_specs=_hbm_specs(n), out_specs=_hbm_specs(n),
        out_shape=_chip_landing(sums_bf16), scratch_shapes=_chip_sems(n),
    )(*sums_bf16)


def _chip_sum(sums_f32, landed, name):
    _, rows, cols = sums_f32.shape
    x, y, c = _mesh_pos()
    idx = jnp.stack([2 * x + y, c]).astype(jnp.int32)

    def body(idx_ref, o_ref, l_ref, out_ref):
        out_ref[...] = ((o_ref[...] + l_ref[0].astype(F32)) + l_ref[1].astype(F32)) + l_ref[2].astype(F32)

    return pl.pallas_call(
        body, name=name,
        grid_spec=pltpu.PrefetchScalarGridSpec(
            num_scalar_prefetch=1, grid=(1,),
            in_specs=[pl.BlockSpec((None, rows, cols), lambda i, idx: (idx[0], 0, 0)),
                      pl.BlockSpec((N_CHIPS - 1, rows, cols), lambda i, idx: (0, 0, 0))],
            out_specs=pl.BlockSpec((rows, cols), lambda i, idx: (idx[1], 0))),
        out_shape=jax.ShapeDtypeStruct((2 * rows, cols), F32),
        compiler_params=pltpu.CompilerParams(vmem_limit_bytes=_vmem_limit(3 * rows * cols * 4)),
    )(idx, sums_f32, landed)


def _halves_to_full(fulls, tag):
    n = len(fulls)

    def body(*refs):
        outs = refs[n:2 * n]
        send_sems, recv_sems = refs[2 * n:]
        x, y, c = _mesh_pos()
        copies = []
        for k in range(n):
            half = fulls[k].shape[0] // 2
            rows = outs[k].at[pl.ds(c * half, half), :]
            cp = pltpu.make_async_remote_copy(
                src_ref=rows, dst_ref=rows, send_sem=send_sems.at[k], recv_sem=recv_sems.at[k],
                device_id=(x, y, 1 - c), device_id_type=MESH)
            cp.start()
            copies.append(cp)
        for k in range(n):
            half = fulls[k].shape[0] // 2
            theirs = outs[k].at[pl.ds((1 - c) * half, half), :]
            pltpu.make_async_remote_copy(
                src_ref=theirs, dst_ref=theirs, send_sem=send_sems.at[k], recv_sem=recv_sems.at[k],
                device_id=(x, y, 1 - c), device_id_type=MESH).wait_recv()
        for cp in copies:
            cp.wait_send()

    return pl.pallas_call(
        body, name="grad_halves_to_full_" + tag,
        in_specs=_hbm_specs(n), out_specs=_hbm_specs(n),
        out_shape=[jax.ShapeDtypeStruct(f.shape, F32) for f in fulls],
        input_output_aliases={k: k for k in range(n)},
        scratch_shapes=[pltpu.SemaphoreType.DMA((n,)), pltpu.SemaphoreType.DMA((n,))],
    )(*fulls)


def _all_sum_small(v):
    rows, cols = v.shape
    n_dev = 8

    def body(v_ref, out_ref, buf, send_sems, recv_sems):
        x, y, c = _mesh_pos()
        me = 4 * x + 2 * y + c
        buf[me] = v_ref[...]
        peers = []
        for r in range(1, n_dev):
            px = 1 - x if r & 4 else x
            py = 1 - y if r & 2 else y
            pc = 1 - c if r & 1 else c
            peers.append((px, py, pc))
        copies = []
        for r, peer in enumerate(peers):
            cp = pltpu.make_async_remote_copy(
                src_ref=v_ref, dst_ref=buf.at[me], send_sem=send_sems.at[r], recv_sem=recv_sems.at[r],
                device_id=peer, device_id_type=MESH)
            cp.start()
            copies.append(cp)
        for r, (px, py, pc) in enumerate(peers):
            pltpu.make_async_remote_copy(
                src_ref=v_ref, dst_ref=buf.at[4 * px + 2 * py + pc], send_sem=send_sems.at[r], recv_sem=recv_sems.at[r],
                device_id=(px, py, pc), device_id_type=MESH).wait_recv()
        for cp in copies:
            cp.wait_send()
        acc = buf[0]
        for d in range(1, n_dev):
            acc = acc + buf[d]
        out_ref[...] = acc
        out_ref[3:4, :] = jnp.broadcast_to(jnp.sum(acc[3:4, :], axis=1, keepdims=True), (1, cols))

    vm = pl.BlockSpec(memory_space=pltpu.VMEM)
    return pl.pallas_call(
        body, name="all_sum_small", in_specs=[vm], out_specs=vm,
        out_shape=jax.ShapeDtypeStruct((rows, cols), F32),
        scratch_shapes=[pltpu.VMEM((n_dev, rows, cols), F32),
                        pltpu.SemaphoreType.DMA((n_dev - 1,)), pltpu.SemaphoreType.DMA((n_dev - 1,))],
    )(v)


def _adamw_math(w, g, m, v):
    m = ADAM_B1 * m + (1.0 - ADAM_B1) * g
    v = ADAM_B2 * v + (1.0 - ADAM_B2) * (g * g)
    m_hat = m / (1.0 - ADAM_B1 ** ADAM_STEP)
    v_hat = v / (1.0 - ADAM_B2 ** ADAM_STEP)
    delta = -ADAM_LR * (m_hat / (jnp.sqrt(v_hat) + ADAM_EPS) + ADAM_WD * w)
    return delta, m, v


def _adamw(w, g, m, v, name):
    rows, cols = w.shape
    tm = rows // 2 if (rows // 2) % 8 == 0 else rows
    return _rowwise(_adamw_math, [w, g, m, v], [], [(cols, F32)] * 3, [], tm=tm, name=name)


def _unshard_cols(gathered):
    n, r, c = gathered.shape
    return jnp.transpose(gathered, (1, 0, 2)).reshape(r, n * c)


def _shard_cols(full):
    r, nc = full.shape
    return jnp.transpose(full.reshape(r, N_CHIPS, nc // N_CHIPS), (1, 0, 2))


LATE = ["w_sb_up", "w_dil_up", "w_out", "w_ffn_in", "w_ffn_out"]


def _late_weights(slabs, d_model, d_ff):
    g = dict(zip(LATE, slabs))
    return (_unshard_cols(g["w_sb_up"]), _unshard_cols(g["w_dil_up"]), g["w_out"].reshape(d_model, d_model),
            _unshard_cols(g["w_ffn_in"]), g["w_ffn_out"].reshape(d_ff, d_model))


def _chip_major(grads, d_model, d_ff):
    row_sharded = {"w_out": d_model // N_CHIPS, "w_ffn_out": d_ff // N_CHIPS}
    return [g.reshape(N_CHIPS, row_sharded[k], g.shape[1]) if k in row_sharded else _shard_cols(g)
            for k, g in grads.items()]


def _pair_reduce(grads, d_model, d_ff):
    full = _chip_major(grads, d_model, d_ff)
    others = _pair_exchange(full, next(iter(grads)))
    return [_pair_sum(g, o, "grad_pair_sum_" + k) for g, o, k in zip(full, others, grads)]


def _chip_reduce(pair, landed, names):
    halves = [_chip_sum(p[0], l, "grad_chip_sum_" + k) for p, l, k in zip(pair, landed, names)]
    return dict(zip(names, _halves_to_full(halves, names[0])))


def _fwd_bwd(x, loss_target, g_mix, g_ffn, g_fin, wf_in, late_slabs):
    b_sz, s_len, d_model = x.shape
    t = b_sz * s_len
    d_ff = late_slabs[-1].shape[1] * N_CHIPS
    x2d = x.reshape(t, d_model)
    tgt2d = loss_target.reshape(t, d_model)
    wf_qkv, wf_gate = wf_in[:, :QKV_WIDTH], wf_in[:, QKV_WIDTH:]

    (u,) = _rowwise(lambda xv, g: (_rms_stats(xv)[0] * g,), [x2d], [g_mix], [(d_model, BF16)], [], tm=512, name="norm_mix")
    qkv = _mm(u, wf_qkv, tm=1024, tn=768, tk=d_model, name="proj_qkv")
    gates = _mm(u, wf_gate, tm=1024, tn=512, tk=d_model, name="proj_gates")
    qkv3 = qkv.reshape(b_sz, s_len, QKV_WIDTH)
    o_sb, late_slabs = _sb_fwd(qkv3, b_sz, s_len, late_slabs)
    wf_sb_up, wf_dil_up, wf_out, wf_ffn_in, wf_ffn_out = _late_weights(late_slabs, d_model, d_ff)
    o_dl, lse = _dil_fwd(qkv3, b_sz, s_len)
    o_sb2, o_dl2 = o_sb.reshape(t, SB_WIDTH), o_dl.reshape(t, DIL_OUT_WIDTH)
    y_sb = _mm(o_sb2, wf_sb_up, tm=1024, tn=1024, tk=SB_WIDTH, name="sb_up")
    y_dl = _mm(o_dl2, wf_dil_up, tm=1024, tn=1024, tk=DIL_OUT_WIDTH, name="dil_up")

    def merge_fn(gt, ys, yd):
        return (_sigmoid(gt[:, :d_model]) * ys + _sigmoid(gt[:, d_model:]) * yd,)

    (merged,) = _rowwise(merge_fn, [gates, y_sb, y_dl], [], [(d_model, BF16)], [], tm=512, name="merge")
    x1 = _mm(merged, wf_out, add=x2d, tm=512, tn=1024, tk=d_model, name="mix_out")
    (u2,) = _rowwise(lambda xv, g: (_rms_stats(xv)[0] * g,), [x1], [g_ffn], [(d_model, BF16)], [], tm=512, name="norm_ffn")
    h = _mm(u2, wf_ffn_in, tm=1024, tn=512, tk=d_model, name="ffn_in")

    def act_fn(hv):
        gate = hv[:, :d_ff]
        return (gate * _sigmoid(gate) * hv[:, d_ff:],)

    (act,) = _rowwise(act_fn, [h], [], [(d_ff, BF16)], [], tm=256, name="ffn_act")
    x2 = _mm(act, wf_ffn_out, add=x1, tm=512, tn=1024, tk=d_ff, name="ffn_out")

    def head_fn(xv, tg, g):
        xhat, r = _rms_stats(xv)
        err = xhat * g - tg
        dy = err * (1.0 / d_model)
        dx, dg_rows = _rms_bwd(dy, xhat, r, g)
        loss_lanes = (0.5 / d_model) * jnp.sum(err * err, axis=0, keepdims=True)
        return dx, jnp.sum(dg_rows, axis=0, keepdims=True), loss_lanes

    dx2, dg_fin, loss_lanes = _rowwise(head_fn, [x2, tgt2d], [g_fin], [(d_model, F32)], [(1, d_model), (1, d_model)],
                                       tm=512, name="loss_head")

    dact = _mm(dx2, wf_ffn_out, tb=True, tm=512, tn=d_ff // 2, tk=d_model, name="ffn_out_dx")
    gw_ffn_out = _mm(act, dx2, ta=True, tm=d_ff // 2, tn=1024, tk=512, name="ffn_out_dw")

    def dact_fn(hv, da):
        gate, up = hv[:, :d_ff], hv[:, d_ff:]
        sg = _sigmoid(gate)
        dgate = da * up * (sg * (1.0 + gate * (1.0 - sg)))
        return (jnp.concatenate([dgate, da * (gate * sg)], axis=1),)

    (dh,) = _rowwise(dact_fn, [h, dact], [], [(2 * d_ff, BF16)], [], tm=256, name="ffn_act_bwd")
    du2 = _mm(dh, wf_ffn_in, tb=True, tm=1024, tn=1024, tk=512, name="ffn_in_dx")
    gw_ffn_in = _mm(u2, dh, ta=True, tm=1024, tn=512, tk=1024, name="ffn_in_dw")

    def norm_bwd_fn(dres, du_, xv, g):
        xhat, r = _rms_stats(xv)
        dx, dg_rows = _rms_bwd(du_, xhat, r, g)
        return dres + dx, jnp.sum(dg_rows, axis=0, keepdims=True)

    dx1, dg_ffn = _rowwise(norm_bwd_fn, [dx2, du2, x1], [g_ffn], [(d_model, F32)], [(1, d_model)], tm=512, name="norm_ffn_bwd")

    dmerged = _mm(dx1, wf_out, tb=True, tm=512, tn=1024, tk=d_model, name="mix_out_dx")
    gw_out = _mm(merged, dx1, ta=True, tm=1024, tn=512, tk=512, name="mix_out_dw")

    def merge_bwd_fn(gt, ys, yd, dm):
        s_sb, s_dl = _sigmoid(gt[:, :d_model]), _sigmoid(gt[:, d_model:])
        dgates = jnp.concatenate([dm * ys * s_sb * (1.0 - s_sb), dm * yd * s_dl * (1.0 - s_dl)], axis=1)
        return dgates, dm * s_sb, dm * s_dl

    dgates, dy_sb, dy_dl = _rowwise(merge_bwd_fn, [gates, y_sb, y_dl, dmerged], [],
                                    [(2 * d_model, BF16), (d_model, BF16), (d_model, BF16)], [], tm=256, name="merge_bwd")
    do_sb = _mm(dy_sb, wf_sb_up, tb=True, tm=1024, tn=SB_WIDTH, tk=d_model, name="sb_up_dx")
    gw_sb_up = _mm(o_sb2, dy_sb, ta=True, tm=SB_WIDTH, tn=1024, tk=512, name="sb_up_dw")
    do_dl = _mm(dy_dl, wf_dil_up, tb=True, tm=1024, tn=DIL_OUT_WIDTH, tk=d_model, name="dil_up_dx")
    gw_dil_up = _mm(o_dl2, dy_dl, ta=True, tm=DIL_OUT_WIDTH, tn=1024, tk=512, name="dil_up_dw")
    late_grads = {"w_sb_up": gw_sb_up, "w_dil_up": gw_dil_up, "w_out": gw_out, "w_ffn_in": gw_ffn_in, "w_ffn_out": gw_ffn_out}
    pair = _pair_reduce(late_grads, d_model, d_ff)
    (dq_sb, dk_sb, dv_sb), landed = _sb_bwd(qkv3, o_sb, do_sb.reshape(b_sz, s_len, SB_WIDTH), b_sz, s_len,
                                           [p[1] for p in pair])
    dq_dl, dk_dl, dv_dl = _dil_bwd(qkv3, o_dl, lse, do_dl.reshape(b_sz, s_len, DIL_OUT_WIDTH), b_sz, s_len)
    dproj = jnp.concatenate(
        [a.reshape(t, -1) for a in (dq_sb, dk_sb, dv_sb)]
        + [a.reshape(t, -1).astype(BF16) for a in (dq_dl, dk_dl, dv_dl)] + [dgates], axis=1)
    du = _mm(dproj, wf_in, tb=True, tm=512, tn=1024, tk=wf_in.shape[1] // 2, name="proj_dx")
    gw_in = _mm(u, dproj, ta=True, tm=512, tn=wf_in.shape[1] // 2, tk=512, name="proj_dw")
    dx, dg_mix = _rowwise(norm_bwd_fn, [dx1, du, x2d], [g_mix], [(d_model, F32)], [(1, d_model)], tm=512, name="norm_mix_bwd")

    grads = _chip_reduce(pair, landed, LATE)
    return dx, grads, gw_in, dg_mix, dg_ffn, dg_fin, loss_lanes


def kernel(x, norm_mix_g, w_in, w_sb_up, w_dil_up, w_out, norm_ffn_g, w_ffn_in, w_ffn_out, norm_final_g, loss_target, m_norm_mix_g, m_w_in, m_w_sb_up, m_w_dil_up, m_w_out, m_norm_ffn_g, m_w_ffn_in, m_w_ffn_out, m_norm_final_g, v_norm_mix_g, v_w_in, v_w_sb_up, v_w_dil_up, v_w_out, v_norm_ffn_g, v_w_ffn_in, v_w_ffn_out, v_norm_final_g):
    b_sz, s_len, d_model = x.shape
    d_ff = w_ffn_out.shape[1] * N_CHIPS
    g_mix, g_ffn, g_fin = norm_mix_g, norm_ffn_g, norm_final_g.reshape(1, d_model)

    names = ["w_in", "w_sb_up", "w_dil_up", "w_out", "w_ffn_in", "w_ffn_out"]
    shards = {"w_in": w_in[0], "w_sb_up": w_sb_up[0], "w_dil_up": w_dil_up[0], "w_out": w_out[0],
              "w_ffn_in": w_ffn_in[0], "w_ffn_out": w_ffn_out[0]}
    (slab_in,) = _gather_weights([_cast_to_slab(shards["w_in"], "cast_w_in")])
    late_slabs = [_cast_to_slab(shards[k], "cast_" + k) for k in LATE]

    dx, grads, gw_in, dg_mix, dg_ffn, dg_fin, loss_lanes = _fwd_bwd(
        x, loss_target, g_mix, g_ffn, g_fin, _unshard_cols(slab_in), late_slabs)

    pair = _pair_reduce({"w_in": gw_in}, d_model, d_ff)
    grads.update(_chip_reduce(pair, _chip_exchange([p[1] for p in pair]), ["w_in"]))

    small = jnp.concatenate([dg_mix, dg_ffn, dg_fin, loss_lanes, jnp.zeros((4, d_model), F32)], axis=0)
    small = _all_sum_small(small)
    loss = small[3, 0]
    gains = jnp.concatenate([g_mix, g_ffn, g_fin, jnp.zeros((5, d_model), F32)], axis=0)
    gains_m = jnp.concatenate([m_norm_mix_g, m_norm_ffn_g, m_norm_final_g.reshape(1, d_model), jnp.zeros((5, d_model), F32)], axis=0)
    gains_v = jnp.concatenate([v_norm_mix_g, v_norm_ffn_g, v_norm_final_g.reshape(1, d_model), jnp.ones((5, d_model), F32)], axis=0)
    gd, gm, gv = _rowwise(_adamw_math, [gains, small, gains_m, gains_v], [], [(d_model, F32)] * 3, [], tm=8, name="adamw_gains")

    moments = {"w_in": (m_w_in, v_w_in), "w_sb_up": (m_w_sb_up, v_w_sb_up), "w_dil_up": (m_w_dil_up, v_w_dil_up),
               "w_out": (m_w_out, v_w_out), "w_ffn_in": (m_w_ffn_in, v_w_ffn_in), "w_ffn_out": (m_w_ffn_out, v_w_ffn_out)}
    upd = {k: _adamw(shards[k], grads[k], moments[k][0][0], moments[k][1][0], "adamw_" + k) for k in names}

    def w_out_of(i):
        return [upd[k][i][None] for k in names]

    def ordered(mix, ws, ffn_g, fin):
        return [mix, ws[0], ws[1], ws[2], ws[3], ffn_g, ws[4], ws[5], fin]

    grad_ws = [grads[k][None] for k in names]
    outs = [loss, dx.reshape(b_sz, s_len, d_model)]
    outs += ordered(small[0:1], grad_ws, small[1:2], small[2])
    outs += ordered(gd[0:1], w_out_of(0), gd[1:2], gd[2])
    outs += ordered(gm[0:1], w_out_of(1), gm[1:2], gm[2])
    outs += ordered(gv[0:1], w_out_of(2), gv[1:2], gv[2])
    return tuple(outs)
```

```python
import functools
import math

import jax
import jax.numpy as jnp
from jax import lax
from jax.experimental import pallas as pl
from jax.experimental.pallas import tpu as pltpu

F32 = jnp.float32
BF16 = jnp.bfloat16
MESH = pl.DeviceIdType.MESH

HEAD_DIM = 64
SB_HEADS = 8
DIL_PAIRS = ((128, 1), (512, 4), (2048, 16))
DIL_HEADS_PER_GROUP = 4
DIL_HEADS = DIL_HEADS_PER_GROUP * len(DIL_PAIRS)
SB_WIDTH = SB_HEADS * HEAD_DIM
DIL_WIDTH = DIL_HEADS * HEAD_DIM
DIL_OUT_WIDTH = DIL_HEADS_PER_GROUP * HEAD_DIM
QKV_WIDTH = 3 * SB_WIDTH + 3 * DIL_WIDTH
RMS_EPS = 1e-6
ALIBI_MAX_BIAS = 8.0
ADAM_LR = 0.001
ADAM_B1 = 0.9
ADAM_B2 = 0.999
ADAM_EPS = 1e-08
ADAM_WD = 0.01
ADAM_STEP = 10

LANES = 128
BLK = 128
NEG = -1e30
SB_FWD_CHAINS = 4
SB_BWD_CHAINS = 4
N_CHIPS = 4
VMEM_CAP = 56 * 1024 * 1024


def _vmem_limit(tile_bytes):
    return int(min(VMEM_CAP, max(32 * 1024 * 1024, 3 * tile_bytes + 8 * 1024 * 1024)))


def _nbytes(shape, dtype):
    return math.prod(shape) * jnp.dtype(dtype).itemsize


def _dot(a, b):
    return jnp.dot(a, b, preferred_element_type=F32)


def _dot_nt(a, b):
    return lax.dot_general(a, b, (((1,), (1,)), ((), ())), preferred_element_type=F32)


def _dot_tn(a, b):
    return lax.dot_general(a, b, (((0,), (0,)), ((), ())), preferred_element_type=F32)


def _split2(x):
    hi = x.astype(BF16)
    lo = (x - hi.astype(F32)).astype(BF16)
    return hi, lo


def _sigmoid(x):
    return 1.0 / (1.0 + jnp.exp(-x))


def _mm(a, b, *, ta=False, tb=False, add=None, out_dtype=F32, tm, tn, tk, name, carried=()):
    n_car = len(carried)
    if ta:
        kdim, m = a.shape
    else:
        m, kdim = a.shape
    if tb:
        n, k2 = b.shape
    else:
        k2, n = b.shape
    assert kdim == k2 and m % tm == 0 and n % tn == 0 and kdim % tk == 0, (name, a.shape, b.shape)
    nk = kdim // tk
    a_spec = pl.BlockSpec((tk, tm), lambda i, j, k: (k, i)) if ta else pl.BlockSpec((tm, tk), lambda i, j, k: (i, k))
    b_spec = pl.BlockSpec((tn, tk), lambda i, j, k: (j, k)) if tb else pl.BlockSpec((tk, tn), lambda i, j, k: (k, j))
    o_spec = pl.BlockSpec((tm, tn), lambda i, j, k: (i, j))
    dims = ((((0,) if ta else (1,)), ((1,) if tb else (0,))), ((), ()))
    has_add = add is not None
    use_scratch = nk > 1 and jnp.dtype(out_dtype) != jnp.dtype(F32)

    n_in = 3 if has_add else 2
    grid = (m // tm, n // tn, nk)

    def compute(a_ref, b_ref, add_ref, o_ref, acc_ref):
        prod = lax.dot_general(a_ref[...].astype(BF16), b_ref[...].astype(BF16), dims, preferred_element_type=F32)
        if nk == 1:
            if has_add:
                prod = prod + add_ref[...]
            o_ref[...] = prod.astype(out_dtype)
            return
        k = pl.program_id(2)

        @pl.when(k == 0)
        def _():
            acc_ref[...] = prod + add_ref[...] if has_add else prod

        @pl.when(k > 0)
        def _():
            acc_ref[...] += prod

        if use_scratch:
            @pl.when(k == nk - 1)
            def _():
                o_ref[...] = acc_ref[...].astype(out_dtype)

    def body(*refs):
        add_ref = refs[2] if has_add else None
        o_ref = refs[n_in + n_car]
        acc_ref = refs[n_in + 2 * n_car + 1] if use_scratch else o_ref
        if not n_car:
            compute(refs[0], refs[1], add_ref, o_ref, acc_ref)
            return
        copies = _chip_copies(refs[n_in:n_in + n_car], refs[n_in + n_car + 1:n_in + 2 * n_car + 1], *refs[-2:])
        step = (pl.program_id(0) * grid[1] + pl.program_id(1)) * nk + pl.program_id(2)

        @pl.when(step == 0)
        def _():
            for cp in copies:
                cp.start()

        compute(refs[0], refs[1], add_ref, o_ref, acc_ref)

        @pl.when(step == grid[0] * grid[1] * nk - 1)
        def _():
            for cp in copies:
                cp.wait()

    tile_bytes = (_nbytes((tm, tk), a.dtype) + _nbytes((tk, tn), b.dtype) + _nbytes((tm, tn), out_dtype)
                  + (_nbytes((tm, tn), F32) if has_add else 0) + _nbytes((tm, tn), F32))
    in_specs = [a_spec, b_spec] + ([o_spec] if has_add else [])
    args = (a, b) + ((add,) if has_add else ())
    out_sd = jax.ShapeDtypeStruct((m, n), out_dtype)
    scratch = [pltpu.VMEM((tm, tn), F32)] if use_scratch else []
    if not n_car:
        return pl.pallas_call(
            body, name=name, grid=grid, in_specs=in_specs, out_specs=o_spec, out_shape=out_sd, scratch_shapes=scratch,
            compiler_params=pltpu.CompilerParams(
                dimension_semantics=("parallel", "parallel", "arbitrary"),
                vmem_limit_bytes=_vmem_limit(tile_bytes)),
        )(*args)
    res = pl.pallas_call(
        body, name=name, grid=grid,
        in_specs=in_specs + _hbm_specs(n_car), out_specs=[o_spec] + _hbm_specs(n_car),
        out_shape=[out_sd] + _chip_landing(carried), scratch_shapes=scratch + _chip_sems(n_car),
        compiler_params=pltpu.CompilerParams(
            dimension_semantics=("arbitrary", "arbitrary", "arbitrary"),
            vmem_limit_bytes=_vmem_limit(tile_bytes)),
    )(*args, *carried)
    return res[0], res[1:]


def _rowwise(fn, rows, params, outs, accs, *, tm, name):
    t = rows[0].shape[0]
    assert t % tm == 0, (name, t, tm)
    n_r, n_p, n_o = len(rows), len(params), len(outs)

    def body(*refs):
        vals = [r[...] for r in refs[:n_r + n_p]]
        res = fn(*vals)
        o_refs = refs[n_r + n_p:n_r + n_p + n_o]
        a_refs = refs[n_r + n_p + n_o:]
        for o_ref, v in zip(o_refs, res[:n_o]):
            o_ref[...] = v.astype(o_ref.dtype)
        if accs:
            @pl.when(pl.program_id(0) == 0)
            def _():
                for a_ref in a_refs:
                    a_ref[...] = jnp.zeros(a_ref.shape, F32)

            for a_ref, v in zip(a_refs, res[n_o:]):
                a_ref[...] += v

    in_specs = [pl.BlockSpec((tm, r.shape[1]), lambda i: (i, 0)) for r in rows]
    in_specs += [pl.BlockSpec(p.shape, lambda i: (0, 0)) for p in params]
    out_specs = [pl.BlockSpec((tm, w), lambda i: (i, 0)) for w, _ in outs]
    out_specs += [pl.BlockSpec(s, lambda i: (0, 0)) for s in accs]
    out_shape = [jax.ShapeDtypeStruct((t, w), d) for w, d in outs]
    out_shape += [jax.ShapeDtypeStruct(s, F32) for s in accs]
    tile_bytes = sum(_nbytes((tm, r.shape[1]), r.dtype) for r in rows) + sum(_nbytes((tm, w), F32) for w, _ in outs)
    res = pl.pallas_call(
        body, name=name, grid=(t // tm,), in_specs=in_specs, out_specs=out_specs, out_shape=out_shape,
        compiler_params=pltpu.CompilerParams(
            dimension_semantics=("arbitrary",) if accs else ("parallel",),
            vmem_limit_bytes=_vmem_limit(2 * tile_bytes)),
    )(*rows, *params)
    return res


def _rms_stats(x):
    r = lax.rsqrt(jnp.mean(x * x, axis=-1, keepdims=True) + RMS_EPS)
    return x * r, r


def _rms_bwd(dy, xhat, r, g):
    dxhat = dy * g
    dx = r * (dxhat - xhat * jnp.mean(dxhat * xhat, axis=-1, keepdims=True))
    return dx, dy * xhat


def _sb_consts():
    lane = lax.broadcasted_iota(jnp.int32, (BLK, LANES), 1)
    head0 = lane < HEAD_DIM
    row = lax.broadcasted_iota(jnp.int32, (2 * BLK, BLK), 0) % BLK
    col = lax.broadcasted_iota(jnp.int32, (2 * BLK, BLK), 1)
    causal = col < row
    jj = lax.broadcasted_iota(jnp.int32, (BLK, BLK), 0)
    ss = lax.broadcasted_iota(jnp.int32, (BLK, BLK), 1)
    suffix = jnp.where(jj > ss, 1.0, 0.0).astype(BF16)
    return head0, causal, suffix


def _stack_heads(x, head0):
    zero = jnp.zeros_like(x)
    return jnp.concatenate([jnp.where(head0, x, zero), jnp.where(head0, zero, x)], axis=0)


def _sb_logits(z, causal, masked):
    sp = jnp.log(1.0 + jnp.exp(-jnp.abs(z)))
    log_keep = -(jnp.maximum(z, 0.0) + sp)
    log_beta = jnp.minimum(z, 0.0) - sp
    if masked:
        log_keep = jnp.where(causal, log_keep, 0.0)
    return log_keep, log_beta


def _suffix_sums(x, suffix):
    hi, lo = _split2(x)
    after = _dot(hi, suffix) + _dot(lo, suffix)
    total = jnp.broadcast_to(after[:, 0:1] + x[:, 0:1], x.shape)
    return after, total


def _lane_blocks(x, n):
    return [x[:, p * LANES:(p + 1) * LANES] for p in range(n)]


def _sb_fwd(qkv, b_sz, s_len, slabs):
    n_slabs = len(slabs)
    nq = s_len // BLK
    n_pairs = SB_WIDTH // LANES
    ch = SB_FWD_CHAINS
    n_steps = n_pairs // ch
    scale = 1.0 / math.sqrt(HEAD_DIM)

    def body(q_ref, k_ref, v_ref, *rest):
        o_ref = rest[n_slabs]
        slab_refs = rest[n_slabs + 1:2 * n_slabs + 1]
        send_sems, recv_sems = rest[2 * n_slabs + 1:]
        step = pl.program_id(0) * n_steps + pl.program_id(1)
        head0, causal, suffix = _sb_consts()

        @pl.when(step == 0)
        def _():
            _gather_issue(slab_refs, send_sems, recv_sems)

        def q_block(i, _):
            qs = pl.multiple_of(i * BLK, BLK)
            q_all = (q_ref[pl.ds(qs, BLK), :] * scale).astype(BF16)
            q01 = [_stack_heads(q, head0) for q in _lane_blocks(q_all, ch)]

            def tile(j, state, masked):
                ks = pl.multiple_of(j * BLK, BLK)
                ks_ = _lane_blocks(k_ref[pl.ds(ks, BLK), :].astype(BF16), ch)
                vs_ = _lane_blocks(v_ref[pl.ds(ks, BLK), :].astype(BF16), ch)
                zs = [_dot_nt(q01[p], ks_[p]) for p in range(ch)]
                logits = [_sb_logits(z, causal, masked) for z in zs]
                sums = [_suffix_sums(lg[0], suffix) for lg in logits]
                out = []
                for p in range(ch):
                    carry, acc = state[2 * p], state[2 * p + 1]
                    after, total = sums[p]
                    a = jnp.exp(logits[p][1] + carry + after)
                    if masked:
                        a = jnp.where(causal, a, 0.0)
                    a_hi, a_lo = _split2(a)
                    a_cat = jnp.concatenate([a_hi[:BLK], a_hi[BLK:], a_lo[:BLK], a_lo[BLK:]], axis=1)
                    v01 = _stack_heads(vs_[p], head0)
                    out += [carry + total, acc + _dot(a_cat, jnp.concatenate([v01, v01], axis=0))]
                return tuple(out)

            state = (jnp.zeros((2 * BLK, BLK), F32), jnp.zeros((BLK, LANES), F32)) * ch
            state = tile(i, state, True)
            state = lax.fori_loop(0, i, lambda t, st: tile(i - 1 - t, st, False), state)
            o_ref[pl.ds(qs, BLK), :] = jnp.concatenate([state[2 * p + 1] for p in range(ch)], axis=1)
            return 0

        lax.fori_loop(0, nq, q_block, 0)

        @pl.when(step == b_sz * n_steps - 1)
        def _():
            _gather_complete(slab_refs, send_sems, recv_sems)

    blk = lambda off: pl.BlockSpec((None, s_len, ch * LANES), lambda b, p: (b, 0, off + p))
    res = pl.pallas_call(
        body, name="sb_fwd", grid=(b_sz, n_steps),
        in_specs=[blk(0), blk(n_steps), blk(2 * n_steps)] + _hbm_specs(n_slabs),
        out_specs=[blk(0)] + _hbm_specs(n_slabs),
        out_shape=[jax.ShapeDtypeStruct((b_sz, s_len, SB_WIDTH), F32)]
        + [jax.ShapeDtypeStruct(a.shape, a.dtype) for a in slabs],
        input_output_aliases={3 + k: 1 + k for k in range(n_slabs)},
        scratch_shapes=_gather_sems(n_slabs),
        compiler_params=pltpu.CompilerParams(dimension_semantics=("arbitrary", "arbitrary"),
                                             vmem_limit_bytes=VMEM_CAP),
    )(qkv, qkv, qkv, *slabs)
    return res[0], res[1:]


def _sb_bwd(qkv, o_sb, do_sb, b_sz, s_len, sums_bf16):
    n_sums = len(sums_bf16)
    nq = s_len // BLK
    n_pairs = SB_WIDTH // LANES
    ch = SB_BWD_CHAINS
    n_steps = n_pairs // ch
    scale = 1.0 / math.sqrt(HEAD_DIM)

    def body(q_ref, k_ref, v_ref, o_ref, do_ref, *rest):
        sum_refs = rest[:n_sums]
        dq_ref, dk_ref, dv_ref = rest[n_sums:n_sums + 3]
        land_refs = rest[n_sums + 3:2 * n_sums + 3]
        dk_acc, dv_acc, send_sems, recv_sems = rest[2 * n_sums + 3:]
        step = pl.program_id(0) * n_steps + pl.program_id(1)
        copies = _chip_copies(sum_refs, land_refs, send_sems, recv_sems)

        @pl.when(step == 0)
        def _():
            for cp in copies:
                cp.start()

        head0, causal, suffix = _sb_consts()
        lrow = lax.broadcasted_iota(jnp.int32, (LANES, LANES), 0)
        ones_h0 = jnp.where(lrow < HEAD_DIM, 1.0, 0.0).astype(BF16)
        ones_h1 = jnp.where(lrow >= HEAD_DIM, 1.0, 0.0).astype(BF16)
        dk_acc[...] = jnp.zeros(dk_acc.shape, F32)
        dv_acc[...] = jnp.zeros(dv_acc.shape, F32)

        def q_block(i, _):
            qs = pl.multiple_of(i * BLK, BLK)
            q_all = (q_ref[pl.ds(qs, BLK), :] * scale).astype(BF16)
            do_all = do_ref[pl.ds(qs, BLK), :].astype(BF16)
            dd_all = do_all.astype(F32) * o_ref[pl.ds(qs, BLK), :]
            q01 = [_stack_heads(q, head0) for q in _lane_blocks(q_all, ch)]
            do01 = [_stack_heads(d, head0) for d in _lane_blocks(do_all, ch)]
            tot = []
            for dd in _lane_blocks(dd_all, ch):
                dd_hi, dd_lo = _split2(dd)
                tot.append(jnp.concatenate([_dot(dd_hi, ones_h0) + _dot(dd_lo, ones_h0),
                                            _dot(dd_hi, ones_h1) + _dot(dd_lo, ones_h1)], axis=0))

            def tile(j, state, masked):
                ks = pl.multiple_of(j * BLK, BLK)
                ks_ = _lane_blocks(k_ref[pl.ds(ks, BLK), :].astype(BF16), ch)
                vs_ = _lane_blocks(v_ref[pl.ds(ks, BLK), :].astype(BF16), ch)
                zs = [_dot_nt(q01[p], ks_[p]) for p in range(ch)]
                das = [_dot_nt(do01[p], vs_[p]) for p in range(ch)]
                logits = [_sb_logits(z, causal, masked) for z in zs]
                sums = [_suffix_sums(lg[0], suffix) for lg in logits]
                a_s, e_s = [], []
                for p in range(ch):
                    a = jnp.exp(logits[p][1] + state[3 * p] + sums[p][0])
                    if masked:
                        a = jnp.where(causal, a, 0.0)
                    a_s.append(a)
                    e_s.append(a * das[p])
                e_sums = [_suffix_sums(e, suffix) for e in e_s]
                out, dks, dvs = [], [], []
                for p in range(ch):
                    carry, rcarry, dq = state[3 * p:3 * p + 3]
                    e = e_s[p]
                    before = tot[p] - (rcarry + e_sums[p][0] + e)
                    beta = jnp.exp(logits[p][1])
                    dz = e * (1.0 - beta) - beta * before
                    if masked:
                        dz = jnp.where(causal, dz, 0.0)
                    dz_b = dz.astype(BF16)
                    dks.append(_dot_tn(dz_b, q01[p]))
                    dvs.append(_dot_tn(a_s[p].astype(BF16), do01[p]))
                    out += [carry + sums[p][1], rcarry + e_sums[p][1], dq + _dot(dz_b, ks_[p])]
                dk_acc[pl.ds(ks, BLK), :] += jnp.concatenate(dks, axis=1)
                dv_acc[pl.ds(ks, BLK), :] += jnp.concatenate(dvs, axis=1)
                return tuple(out)

            state = (jnp.zeros((2 * BLK, BLK), F32),) * (3 * ch)
            state = tile(i, state, True)
            state = lax.fori_loop(0, i, lambda t, st: tile(i - 1 - t, st, False), state)
            dq = [jnp.where(head0, state[3 * p + 2][:BLK], state[3 * p + 2][BLK:]) for p in range(ch)]
            dq_ref[pl.ds(qs, BLK), :] = (jnp.concatenate(dq, axis=1) * scale).astype(dq_ref.dtype)
            return 0

        lax.fori_loop(0, nq, q_block, 0)
        dk_ref[...] = dk_acc[...].astype(dk_ref.dtype)
        dv_ref[...] = dv_acc[...].astype(dv_ref.dtype)

        @pl.when(step == b_sz * n_steps - 1)
        def _():
            for cp in copies:
                cp.wait()

    blk = lambda off: pl.BlockSpec((None, s_len, ch * LANES), lambda b, p: (b, 0, off + p))
    once = lambda off: pl.BlockSpec((None, s_len, ch * LANES), lambda b, p: (b, 0, off + p),
                                    pipeline_mode=pl.Buffered(1))
    out_sd = jax.ShapeDtypeStruct((b_sz, s_len, SB_WIDTH), BF16)
    res = pl.pallas_call(
        body, name="sb_bwd", grid=(b_sz, n_steps),
        in_specs=[once(0), once(n_steps), once(2 * n_steps), once(0), once(0)] + _hbm_specs(n_sums),
        out_specs=[blk(0), blk(0), blk(0)] + _hbm_specs(n_sums),
        out_shape=[out_sd, out_sd, out_sd] + _chip_landing(sums_bf16),
        scratch_shapes=[pltpu.VMEM((s_len, ch * LANES), F32), pltpu.VMEM((s_len, ch * LANES), F32)] + _chip_sems(n_sums),
        compiler_params=pltpu.CompilerParams(dimension_semantics=("arbitrary", "arbitrary"),
                                             vmem_limit_bytes=VMEM_CAP),
    )(qkv, qkv, qkv, o_sb, do_sb, *sums_bf16)
    return res[:3], res[3:]


def _dil_consts(group, pair_idx, dilation):
    lane = lax.broadcasted_iota(jnp.int32, (BLK, LANES), 1)
    head0 = lane < HEAD_DIM
    row = lax.broadcasted_iota(jnp.int32, (2 * BLK, BLK), 0)
    qa = row % BLK
    kb = lax.broadcasted_iota(jnp.int32, (2 * BLK, BLK), 1)
    head = (group * DIL_HEADS_PER_GROUP + 2 * pair_idx + row // BLK).astype(F32)
    slope = jnp.exp((-ALIBI_MAX_BIAS * math.log(2.0) / DIL_HEADS) * (head + 1.0))
    valid_cur = kb <= qa
    valid_prev = kb >= qa
    bias_cur = -slope * ((qa - kb) * dilation).astype(F32)
    bias_prev = -slope * ((BLK + qa - kb) * dilation).astype(F32)
    return head0, valid_cur, valid_prev, bias_cur, bias_prev


def _dil_units(s_len, dilation):
    nb = s_len // dilation // BLK
    return [(r, n) for r in range(dilation) for n in range(nb)]


def _dil_rows(n, r, dilation):
    if dilation == 1:
        return pl.ds(n * BLK, BLK)
    return pl.ds(n * BLK * dilation + r, BLK, stride=dilation)


def _dil_scores(q01, k, bias, valid):
    s = _dot_nt(q01, k) * (1.0 / math.sqrt(HEAD_DIM)) + bias
    return jnp.where(valid, s, NEG)


def _dil_fwd(qkv, b_sz, s_len):
    n_pairs = DIL_OUT_WIDTH // LANES
    q_off = 3 * SB_WIDTH // LANES
    per_kind = DIL_WIDTH // LANES

    def body(*refs):
        qkv_refs = refs[:9]
        o_ref, lse_ref, m_s, l_s = refs[9:]
        pair_idx = pl.program_id(1)
        m_s[...] = jnp.full(m_s.shape, NEG, F32)
        l_s[...] = jnp.zeros(l_s.shape, F32)
        o_ref[...] = jnp.zeros(o_ref.shape, F32)
        for g, (_, dilation) in enumerate(DIL_PAIRS):
            q_ref, k_ref, v_ref = qkv_refs[3 * g:3 * g + 3]
            head0, valid_cur, valid_prev, bias_cur, bias_prev = _dil_consts(g, pair_idx, dilation)
            for r, n in _dil_units(s_len, dilation):
                rows = _dil_rows(n, r, dilation)
                q01 = _stack_heads(q_ref[rows, :].astype(BF16), head0)
                k_c = k_ref[rows, :].astype(BF16)
                v_c = v_ref[rows, :].astype(BF16)
                scores = [_dil_scores(q01, k_c, bias_cur, valid_cur)]
                values = [_stack_heads(v_c, head0)]
                if n > 0:
                    prev = _dil_rows(n - 1, r, dilation)
                    scores.append(_dil_scores(q01, k_ref[prev, :].astype(BF16), bias_prev, valid_prev))
                    values.append(_stack_heads(v_ref[prev, :].astype(BF16), head0))
                m_blk = functools.reduce(jnp.maximum, [jnp.max(s, axis=-1, keepdims=True) for s in scores])
                m_old = jnp.concatenate([m_s.at[0][rows, :], m_s.at[1][rows, :]], axis=0)
                l_old = jnp.concatenate([l_s.at[0][rows, :], l_s.at[1][rows, :]], axis=0)
                m_new = jnp.maximum(m_old, m_blk)
                probs = [jnp.exp(s - m_new) for s in scores]
                l_blk = functools.reduce(jnp.add, [jnp.sum(p, axis=-1, keepdims=True) for p in probs])
                alpha = jnp.exp(m_old - m_new)
                l_new = alpha * l_old + l_blk
                alpha_tok = jnp.where(head0, alpha[:BLK], alpha[BLK:])
                p_cat = jnp.concatenate([h for p in probs for h in (p[:BLK].astype(BF16), p[BLK:].astype(BF16))], axis=1)
                o_ref[rows, :] = alpha_tok * o_ref[rows, :] + _dot(p_cat, jnp.concatenate(values, axis=0))
                m_s.at[0][rows, :] = m_new[:BLK]
                m_s.at[1][rows, :] = m_new[BLK:]
                l_s.at[0][rows, :] = l_new[:BLK]
                l_s.at[1][rows, :] = l_new[BLK:]
        lane = lax.broadcasted_iota(jnp.int32, (BLK, LANES), 1)
        for c in range(s_len // BLK):
            rows = pl.ds(c * BLK, BLK)
            l0, l1 = l_s.at[0][rows, :], l_s.at[1][rows, :]
            o_ref[rows, :] = o_ref[rows, :] / jnp.where(lane < HEAD_DIM, l0, l1)
            lse_ref.at[0][rows, :] = m_s.at[0][rows, :] + jnp.log(l0)
            lse_ref.at[1][rows, :] = m_s.at[1][rows, :] + jnp.log(l1)

    in_specs = []
    for g in range(len(DIL_PAIRS)):
        for kind in range(3):
            off = q_off + kind * per_kind + g * n_pairs
            in_specs.append(pl.BlockSpec((None, s_len, LANES), lambda b, p, off=off: (b, 0, off + p)))
    return pl.pallas_call(
        body, name="dil_fwd", grid=(b_sz, n_pairs),
        in_specs=in_specs,
        out_specs=[pl.BlockSpec((None, s_len, LANES), lambda b, p: (b, 0, p)),
                   pl.BlockSpec((None, None, 2, s_len, LANES), lambda b, p: (b, p, 0, 0, 0))],
        out_shape=[jax.ShapeDtypeStruct((b_sz, s_len, DIL_OUT_WIDTH), F32),
                   jax.ShapeDtypeStruct((b_sz, n_pairs, 2, s_len, LANES), F32)],
        scratch_shapes=[pltpu.VMEM((2, s_len, LANES), F32), pltpu.VMEM((2, s_len, LANES), F32)],
        compiler_params=pltpu.CompilerParams(dimension_semantics=("parallel", "parallel"),
                                             vmem_limit_bytes=VMEM_CAP),
    )(*([qkv] * 9))


def _dil_bwd(qkv, o_dl, lse, do_dl, b_sz, s_len):
    n_pairs = DIL_OUT_WIDTH // LANES
    n_groups = len(DIL_PAIRS)
    q_off = 3 * SB_WIDTH // LANES
    per_kind = DIL_WIDTH // LANES

    def body(q_ref, k_ref, v_ref, o_ref, lse_ref, do_ref, dq_ref, dk_ref, dv_ref, d_s, dq_s, dk_s, dv_s):
        pair_idx = pl.program_id(1)
        group = pl.program_id(2)
        lrow = lax.broadcasted_iota(jnp.int32, (LANES, LANES), 0)
        ones_h0 = jnp.where(lrow < HEAD_DIM, 1.0, 0.0).astype(BF16)
        ones_h1 = jnp.where(lrow >= HEAD_DIM, 1.0, 0.0).astype(BF16)
        for c in range(s_len // BLK):
            rows = pl.ds(c * BLK, BLK)
            dd_hi, dd_lo = _split2(do_ref[rows, :] * o_ref[rows, :])
            d_s.at[0][rows, :] = _dot(dd_hi, ones_h0) + _dot(dd_lo, ones_h0)
            d_s.at[1][rows, :] = _dot(dd_hi, ones_h1) + _dot(dd_lo, ones_h1)
        dk_s[...] = jnp.zeros(dk_s.shape, F32)
        dv_s[...] = jnp.zeros(dv_s.shape, F32)

        def one_group(g, dilation):
            head0, valid_cur, valid_prev, bias_cur, bias_prev = _dil_consts(g, pair_idx, dilation)
            for r, n in _dil_units(s_len, dilation):
                rows = _dil_rows(n, r, dilation)
                q01 = _stack_heads(q_ref[rows, :].astype(BF16), head0)
                do01 = _stack_heads(do_ref[rows, :].astype(BF16), head0)
                lse01 = jnp.concatenate([lse_ref.at[0][rows, :], lse_ref.at[1][rows, :]], axis=0)
                d01 = jnp.concatenate([d_s.at[0][rows, :], d_s.at[1][rows, :]], axis=0)
                dq = jnp.zeros((2 * BLK, LANES), F32)
                blocks = [(rows, bias_cur, valid_cur)]
                if n > 0:
                    blocks.append((_dil_rows(n - 1, r, dilation), bias_prev, valid_prev))
                for krows, bias, valid in blocks:
                    k = k_ref[krows, :].astype(BF16)
                    v = v_ref[krows, :].astype(BF16)
                    p = jnp.exp(_dil_scores(q01, k, bias, valid) - lse01)
                    ds = (p * (_dot_nt(do01, v) - d01) * (1.0 / math.sqrt(HEAD_DIM))).astype(BF16)
                    dq = dq + _dot(ds, k)
                    dk_s[krows, :] = dk_s[krows, :] + _dot_tn(ds, q01)
                    dv_s[krows, :] = dv_s[krows, :] + _dot_tn(p.astype(BF16), do01)
                dq_s[rows, :] = jnp.where(head0, dq[:BLK], dq[BLK:])

        for g, (_, dilation) in enumerate(DIL_PAIRS):
            pl.when(group == g)(functools.partial(one_group, g, dilation))
        dq_ref[...] = dq_s[...].astype(dq_ref.dtype)
        dk_ref[...] = dk_s[...].astype(dk_ref.dtype)
        dv_ref[...] = dv_s[...].astype(dv_ref.dtype)

    def qkv_spec(kind):
        return pl.BlockSpec((None, s_len, LANES),
                            lambda b, p, g: (b, 0, q_off + kind * per_kind + g * n_pairs + p))

    tok_spec = pl.BlockSpec((None, s_len, LANES), lambda b, p, g: (b, 0, p))
    out_spec = pl.BlockSpec((None, s_len, LANES), lambda b, p, g: (b, 0, g * n_pairs + p))
    out_sd = jax.ShapeDtypeStruct((b_sz, s_len, DIL_WIDTH), BF16)
    return pl.pallas_call(
        body, name="dil_bwd", grid=(b_sz, n_pairs, n_groups),
        in_specs=[qkv_spec(0), qkv_spec(1), qkv_spec(2), tok_spec,
                  pl.BlockSpec((None, None, 2, s_len, LANES), lambda b, p, g: (b, p, 0, 0, 0)), tok_spec],
        out_specs=[out_spec, out_spec, out_spec],
        out_shape=[out_sd, out_sd, out_sd],
        scratch_shapes=[pltpu.VMEM((2, s_len, LANES), F32)] + [pltpu.VMEM((s_len, LANES), F32)] * 3,
        compiler_params=pltpu.CompilerParams(dimension_semantics=("parallel", "parallel", "arbitrary"),
                                             vmem_limit_bytes=VMEM_CAP),
    )(qkv, qkv, qkv, o_dl, lse, do_dl)


def _mesh_pos():
    return lax.axis_index("x"), lax.axis_index("y"), lax.axis_index("c")


def _other_chips(x, y):
    return [(1 - x, y), (x, 1 - y), (1 - x, 1 - y)]


def _hbm_specs(n):
    return [pl.BlockSpec(memory_space=pl.ANY)] * n


def _cast_to_slab(w, name):
    rows, cols = w.shape
    mine = jnp.reshape(2 * lax.axis_index("x") + lax.axis_index("y"), (1,)).astype(jnp.int32)

    def body(idx_ref, w_ref, o_ref):
        o_ref[...] = w_ref[...].astype(BF16)

    return pl.pallas_call(
        body, name=name,
        grid_spec=pltpu.PrefetchScalarGridSpec(
            num_scalar_prefetch=1, grid=(1,),
            in_specs=[pl.BlockSpec((rows, cols), lambda i, idx: (0, 0))],
            out_specs=pl.BlockSpec((None, rows, cols), lambda i, idx: (idx[0], 0, 0))),
        out_shape=jax.ShapeDtypeStruct((N_CHIPS, rows, cols), BF16),
        compiler_params=pltpu.CompilerParams(vmem_limit_bytes=_vmem_limit(rows * cols * 6)),
    )(mine, w)


def _gather_issue(slabs, send_sems, recv_sems):
    x, y, c = _mesh_pos()
    for k, slab in enumerate(slabs):
        half = slab.shape[1] // 2
        rows = slab.at[2 * x + y, pl.ds(c * half, half), :]
        for r, (px, py) in enumerate(_other_chips(x, y)):
            pltpu.make_async_remote_copy(
                src_ref=rows, dst_ref=rows, send_sem=send_sems.at[6 * k + r], recv_sem=recv_sems.at[6 * k + r],
                device_id=(px, py, c), device_id_type=MESH).start()


def _gather_complete(slabs, send_sems, recv_sems):
    x, y, c = _mesh_pos()
    chips = _other_chips(x, y)

    def copy(k, sem, block, rows, to):
        ref = slabs[k].at[block, rows, :]
        return pltpu.make_async_remote_copy(
            src_ref=ref, dst_ref=ref, send_sem=send_sems.at[sem], recv_sem=recv_sems.at[sem],
            device_id=to, device_id_type=MESH)

    for k, slab in enumerate(slabs):
        half = slab.shape[1] // 2
        for r, (px, py) in enumerate(chips):
            copy(k, 6 * k + r, 2 * px + py, pl.ds(c * half, half), (px, py, c)).wait_recv()
            copy(k, 6 * k + 3 + r, 2 * px + py, pl.ds(c * half, half), (x, y, 1 - c)).start()
    for k, slab in enumerate(slabs):
        half = slab.shape[1] // 2
        for r, (px, py) in enumerate(chips):
            copy(k, 6 * k + 3 + r, 2 * px + py, pl.ds((1 - c) * half, half), (x, y, 1 - c)).wait_recv()
    for k, slab in enumerate(slabs):
        half = slab.shape[1] // 2
        for r, (px, py) in enumerate(chips):
            copy(k, 6 * k + r, 2 * x + y, pl.ds(c * half, half), (px, py, c)).wait_send()
            copy(k, 6 * k + 3 + r, 2 * px + py, pl.ds(c * half, half), (x, y, 1 - c)).wait_send()


def _gather_sems(n):
    return [pltpu.SemaphoreType.DMA((6 * n,)), pltpu.SemaphoreType.DMA((6 * n,))]


def _gather_weights(slabs):
    n = len(slabs)

    def body(*refs):
        outs = refs[n:2 * n]
        send_sems, recv_sems = refs[2 * n:]
        _gather_issue(outs, send_sems, recv_sems)
        _gather_complete(outs, send_sems, recv_sems)

    return pl.pallas_call(
        body, name="gather_weights",
        in_specs=_hbm_specs(n), out_specs=_hbm_specs(n),
        out_shape=[jax.ShapeDtypeStruct(s.shape, s.dtype) for s in slabs],
        input_output_aliases={k: k for k in range(n)},
        scratch_shapes=_gather_sems(n),
    )(*slabs)


def _pair_exchange(grads, tag):
    n = len(grads)

    def body(*refs):
        ins, outs = refs[:n], refs[n:2 * n]
        send_sems, recv_sems = refs[2 * n:]
        x, y, c = _mesh_pos()
        copies = []
        for k in range(n):
            half = grads[k].shape[1] // 2
            cp = pltpu.make_async_remote_copy(
                src_ref=ins[k].at[:, pl.ds((1 - c) * half, half), :], dst_ref=outs[k],
                send_sem=send_sems.at[k], recv_sem=recv_sems.at[k],
                device_id=(x, y, 1 - c), device_id_type=MESH)
            cp.start()
            copies.append(cp)
        for cp in copies:
            cp.wait()

    return pl.pallas_call(
        body, name="grad_pair_exchange_" + tag,
        in_specs=_hbm_specs(n), out_specs=_hbm_specs(n),
        out_shape=[jax.ShapeDtypeStruct((N_CHIPS, g.shape[1] // 2, g.shape[2]), F32) for g in grads],
        scratch_shapes=[pltpu.SemaphoreType.DMA((n,)), pltpu.SemaphoreType.DMA((n,))],
    )(*grads)


def _pair_sum(grad, other, name):
    _, rows, cols = grad.shape
    half = rows // 2
    core = jnp.reshape(lax.axis_index("c"), (1,)).astype(jnp.int32)

    def body(core_ref, g_ref, p_ref, s_ref, sb_ref):
        s = g_ref[...] + p_ref[...]
        s_ref[...] = s
        sb_ref[...] = s.astype(BF16)

    blk = pl.BlockSpec((None, half, cols), lambda p, core_ref: (p, 0, 0))
    return pl.pallas_call(
        body, name=name,
        grid_spec=pltpu.PrefetchScalarGridSpec(
            num_scalar_prefetch=1, grid=(N_CHIPS,),
            in_specs=[pl.BlockSpec((None, half, cols), lambda p, core_ref: (p, core_ref[0], 0)), blk],
            out_specs=[blk, blk]),
        out_shape=[jax.ShapeDtypeStruct((N_CHIPS, half, cols), F32),
                   jax.ShapeDtypeStruct((N_CHIPS, half, cols), BF16)],
        compiler_params=pltpu.CompilerParams(dimension_semantics=("parallel",),
                                             vmem_limit_bytes=_vmem_limit(4 * half * cols * 4)),
    )(core, grad, other)


def _chip_copies(sums_bf16, lands, send_sems, recv_sems):
    x, y, c = _mesh_pos()
    return [pltpu.make_async_remote_copy(
        src_ref=sums_bf16[k].at[2 * px + py], dst_ref=lands[k].at[r],
        send_sem=send_sems.at[3 * k + r], recv_sem=recv_sems.at[3 * k + r],
        device_id=(px, py, c), device_id_type=MESH)
        for k in range(len(sums_bf16)) for r, (px, py) in enumerate(_other_chips(x, y))]


def _chip_sems(n):
    return [pltpu.SemaphoreType.DMA((3 * n,)), pltpu.SemaphoreType.DMA((3 * n,))]


def _chip_landing(sums_bf16):
    return [jax.ShapeDtypeStruct((N_CHIPS - 1,) + s.shape[1:], BF16) for s in sums_bf16]


def _chip_sum(sums_f32, landed, name):
    _, rows, cols = sums_f32.shape
    x, y, c = _mesh_pos()
    idx = jnp.stack([2 * x + y, c]).astype(jnp.int32)

    def body(idx_ref, o_ref, l_ref, out_ref):
        out_ref[...] = ((o_ref[...] + l_ref[0].astype(F32)) + l_ref[1].astype(F32)) + l_ref[2].astype(F32)

    return pl.pallas_call(
        body, name=name,
        grid_spec=pltpu.PrefetchScalarGridSpec(
            num_scalar_prefetch=1, grid=(1,),
            in_specs=[pl.BlockSpec((None, rows, cols), lambda i, idx: (idx[0], 0, 0)),
                      pl.BlockSpec((N_CHIPS - 1, rows, cols), lambda i, idx: (0, 0, 0))],
            out_specs=pl.BlockSpec((rows, cols), lambda i, idx: (idx[1], 0))),
        out_shape=jax.ShapeDtypeStruct((2 * rows, cols), F32),
        compiler_params=pltpu.CompilerParams(vmem_limit_bytes=_vmem_limit(3 * rows * cols * 4)),
    )(idx, sums_f32, landed)


def _halves_to_full(fulls, tag):
    n = len(fulls)

    def body(*refs):
        outs = refs[n:2 * n]
        send_sems, recv_sems = refs[2 * n:]
        x, y, c = _mesh_pos()
        copies = []
        for k in range(n):
            half = fulls[k].shape[0] // 2
            rows = outs[k].at[pl.ds(c * half, half), :]
            cp = pltpu.make_async_remote_copy(
                src_ref=rows, dst_ref=rows, send_sem=send_sems.at[k], recv_sem=recv_sems.at[k],
                device_id=(x, y, 1 - c), device_id_type=MESH)
            cp.start()
            copies.append(cp)
        for k in range(n):
            half = fulls[k].shape[0] // 2
            theirs = outs[k].at[pl.ds((1 - c) * half, half), :]
            pltpu.make_async_remote_copy(
                src_ref=theirs, dst_ref=theirs, send_sem=send_sems.at[k], recv_sem=recv_sems.at[k],
                device_id=(x, y, 1 - c), device_id_type=MESH).wait_recv()
        for cp in copies:
            cp.wait_send()

    return pl.pallas_call(
        body, name="grad_halves_to_full_" + tag,
        in_specs=_hbm_specs(n), out_specs=_hbm_specs(n),
        out_shape=[jax.ShapeDtypeStruct(f.shape, F32) for f in fulls],
        input_output_aliases={k: k for k in range(n)},
        scratch_shapes=[pltpu.SemaphoreType.DMA((n,)), pltpu.SemaphoreType.DMA((n,))],
    )(*fulls)


def _all_sum_small(v):
    rows, cols = v.shape
    n_dev = 8

    def body(v_ref, out_ref, buf, send_sems, recv_sems):
        x, y, c = _mesh_pos()
        me = 4 * x + 2 * y + c
        buf[me] = v_ref[...]
        peers = []
        for r in range(1, n_dev):
            px = 1 - x if r & 4 else x
            py = 1 - y if r & 2 else y
            pc = 1 - c if r & 1 else c
            peers.append((px, py, pc))
        copies = []
        for r, peer in enumerate(peers):
            cp = pltpu.make_async_remote_copy(
                src_ref=v_ref, dst_ref=buf.at[me], send_sem=send_sems.at[r], recv_sem=recv_sems.at[r],
                device_id=peer, device_id_type=MESH)
            cp.start()
            copies.append(cp)
        for r, (px, py, pc) in enumerate(peers):
            pltpu.make_async_remote_copy(
                src_ref=v_ref, dst_ref=buf.at[4 * px + 2 * py + pc], send_sem=send_sems.at[r], recv_sem=recv_sems.at[r],
                device_id=(px, py, pc), device_id_type=MESH).wait_recv()
        for cp in copies:
            cp.wait_send()
        acc = buf[0]
        for d in range(1, n_dev):
            acc = acc + buf[d]
        out_ref[...] = acc
        out_ref[3:4, :] = jnp.broadcast_to(jnp.sum(acc[3:4, :], axis=1, keepdims=True), (1, cols))

    vm = pl.BlockSpec(memory_space=pltpu.VMEM)
    return pl.pallas_call(
        body, name="all_sum_small", in_specs=[vm], out_specs=vm,
        out_shape=jax.ShapeDtypeStruct((rows, cols), F32),
        scratch_shapes=[pltpu.VMEM((n_dev, rows, cols), F32),
                        pltpu.SemaphoreType.DMA((n_dev - 1,)), pltpu.SemaphoreType.DMA((n_dev - 1,))],
    )(v)


def _adamw_math(w, g, m, v):
    m = ADAM_B1 * m + (1.0 - ADAM_B1) * g
    v = ADAM_B2 * v + (1.0 - ADAM_B2) * (g * g)
    m_hat = m / (1.0 - ADAM_B1 ** ADAM_STEP)
    v_hat = v / (1.0 - ADAM_B2 ** ADAM_STEP)
    delta = -ADAM_LR * (m_hat / (jnp.sqrt(v_hat) + ADAM_EPS) + ADAM_WD * w)
    return delta, m, v


def _adamw(w, g, m, v, name):
    rows, cols = w.shape
    tm = rows // 2 if (rows // 2) % 8 == 0 else rows
    return _rowwise(_adamw_math, [w, g, m, v], [], [(cols, F32)] * 3, [], tm=tm, name=name)


def _unshard_cols(gathered):
    n, r, c = gathered.shape
    return jnp.transpose(gathered, (1, 0, 2)).reshape(r, n * c)


def _shard_cols(full):
    r, nc = full.shape
    return jnp.transpose(full.reshape(r, N_CHIPS, nc // N_CHIPS), (1, 0, 2))


LATE = ["w_sb_up", "w_dil_up", "w_out", "w_ffn_in", "w_ffn_out"]


def _late_weights(slabs, d_model, d_ff):
    g = dict(zip(LATE, slabs))
    return (_unshard_cols(g["w_sb_up"]), _unshard_cols(g["w_dil_up"]), g["w_out"].reshape(d_model, d_model),
            _unshard_cols(g["w_ffn_in"]), g["w_ffn_out"].reshape(d_ff, d_model))


def _chip_major(grads, d_model, d_ff):
    row_sharded = {"w_out": d_model // N_CHIPS, "w_ffn_out": d_ff // N_CHIPS}
    return [g.reshape(N_CHIPS, row_sharded[k], g.shape[1]) if k in row_sharded else _shard_cols(g)
            for k, g in grads.items()]


def _pair_reduce(grads, d_model, d_ff):
    full = _chip_major(grads, d_model, d_ff)
    others = _pair_exchange(full, next(iter(grads)))
    return [_pair_sum(g, o, "grad_pair_sum_" + k) for g, o, k in zip(full, others, grads)]


def _chip_reduce(pair, landed, names):
    halves = [_chip_sum(p[0], l, "grad_chip_sum_" + k) for p, l, k in zip(pair, landed, names)]
    return dict(zip(names, _halves_to_full(halves, names[0])))


def _fwd_bwd(x, loss_target, g_mix, g_ffn, g_fin, wf_in, late_slabs):
    b_sz, s_len, d_model = x.shape
    t = b_sz * s_len
    d_ff = late_slabs[-1].shape[1] * N_CHIPS
    x2d = x.reshape(t, d_model)
    tgt2d = loss_target.reshape(t, d_model)
    wf_qkv, wf_gate = wf_in[:, :QKV_WIDTH], wf_in[:, QKV_WIDTH:]

    (u,) = _rowwise(lambda xv, g: (_rms_stats(xv)[0] * g,), [x2d], [g_mix], [(d_model, BF16)], [], tm=512, name="norm_mix")
    qkv = _mm(u, wf_qkv, tm=1024, tn=768, tk=d_model, name="proj_qkv")
    gates = _mm(u, wf_gate, tm=1024, tn=512, tk=d_model, name="proj_gates")
    qkv3 = qkv.reshape(b_sz, s_len, QKV_WIDTH)
    o_sb, late_slabs = _sb_fwd(qkv3, b_sz, s_len, late_slabs)
    wf_sb_up, wf_dil_up, wf_out, wf_ffn_in, wf_ffn_out = _late_weights(late_slabs, d_model, d_ff)
    o_dl, lse = _dil_fwd(qkv3, b_sz, s_len)
    o_sb2, o_dl2 = o_sb.reshape(t, SB_WIDTH), o_dl.reshape(t, DIL_OUT_WIDTH)
    y_sb = _mm(o_sb2, wf_sb_up, tm=1024, tn=1024, tk=SB_WIDTH, name="sb_up")
    y_dl = _mm(o_dl2, wf_dil_up, tm=1024, tn=1024, tk=DIL_OUT_WIDTH, name="dil_up")

    def merge_fn(gt, ys, yd):
        return (_sigmoid(gt[:, :d_model]) * ys + _sigmoid(gt[:, d_model:]) * yd,)

    (merged,) = _rowwise(merge_fn, [gates, y_sb, y_dl], [], [(d_model, BF16)], [], tm=512, name="merge")
    x1 = _mm(merged, wf_out, add=x2d, tm=512, tn=1024, tk=d_model, name="mix_out")
    (u2,) = _rowwise(lambda xv, g: (_rms_stats(xv)[0] * g,), [x1], [g_ffn], [(d_model, BF16)], [], tm=512, name="norm_ffn")
    h = _mm(u2, wf_ffn_in, tm=1024, tn=512, tk=d_model, name="ffn_in")

    def act_fn(hv):
        gate = hv[:, :d_ff]
        return (gate * _sigmoid(gate) * hv[:, d_ff:],)

    (act,) = _rowwise(act_fn, [h], [], [(d_ff, BF16)], [], tm=256, name="ffn_act")
    x2 = _mm(act, wf_ffn_out, add=x1, tm=512, tn=1024, tk=d_ff, name="ffn_out")

    def head_fn(xv, tg, g):
        xhat, r = _rms_stats(xv)
        err = xhat * g - tg
        dy = err * (1.0 / d_model)
        dx, dg_rows = _rms_bwd(dy, xhat, r, g)
        loss_lanes = (0.5 / d_model) * jnp.sum(err * err, axis=0, keepdims=True)
        return dx, jnp.sum(dg_rows, axis=0, keepdims=True), loss_lanes

    dx2, dg_fin, loss_lanes = _rowwise(head_fn, [x2, tgt2d], [g_fin], [(d_model, F32)], [(1, d_model), (1, d_model)],
                                       tm=512, name="loss_head")

    dact = _mm(dx2, wf_ffn_out, tb=True, tm=512, tn=d_ff // 2, tk=d_model, name="ffn_out_dx")
    gw_ffn_out = _mm(act, dx2, ta=True, tm=d_ff // 2, tn=1024, tk=512, name="ffn_out_dw")

    def dact_fn(hv, da):
        gate, up = hv[:, :d_ff], hv[:, d_ff:]
        sg = _sigmoid(gate)
        dgate = da * up * (sg * (1.0 + gate * (1.0 - sg)))
        return (jnp.concatenate([dgate, da * (gate * sg)], axis=1),)

    (dh,) = _rowwise(dact_fn, [h, dact], [], [(2 * d_ff, BF16)], [], tm=256, name="ffn_act_bwd")
    du2 = _mm(dh, wf_ffn_in, tb=True, tm=1024, tn=1024, tk=512, name="ffn_in_dx")
    gw_ffn_in = _mm(u2, dh, ta=True, tm=1024, tn=512, tk=1024, name="ffn_in_dw")

    def norm_bwd_fn(dres, du_, xv, g):
        xhat, r = _rms_stats(xv)
        dx, dg_rows = _rms_bwd(du_, xhat, r, g)
        return dres + dx, jnp.sum(dg_rows, axis=0, keepdims=True)

    dx1, dg_ffn = _rowwise(norm_bwd_fn, [dx2, du2, x1], [g_ffn], [(d_model, F32)], [(1, d_model)], tm=512, name="norm_ffn_bwd")

    dmerged = _mm(dx1, wf_out, tb=True, tm=512, tn=1024, tk=d_model, name="mix_out_dx")
    gw_out = _mm(merged, dx1, ta=True, tm=1024, tn=512, tk=512, name="mix_out_dw")

    def merge_bwd_fn(gt, ys, yd, dm):
        s_sb, s_dl = _sigmoid(gt[:, :d_model]), _sigmoid(gt[:, d_model:])
        dgates = jnp.concatenate([dm * ys * s_sb * (1.0 - s_sb), dm * yd * s_dl * (1.0 - s_dl)], axis=1)
        return dgates, dm * s_sb, dm * s_dl

    dgates, dy_sb, dy_dl = _rowwise(merge_bwd_fn, [gates, y_sb, y_dl, dmerged], [],
                                    [(2 * d_model, BF16), (d_model, BF16), (d_model, BF16)], [], tm=256, name="merge_bwd")
    do_sb = _mm(dy_sb, wf_sb_up, tb=True, tm=1024, tn=SB_WIDTH, tk=d_model, name="sb_up_dx")
    gw_sb_up = _mm(o_sb2, dy_sb, ta=True, tm=SB_WIDTH, tn=1024, tk=512, name="sb_up_dw")
    do_dl = _mm(dy_dl, wf_dil_up, tb=True, tm=1024, tn=DIL_OUT_WIDTH, tk=d_model, name="dil_up_dx")
    gw_dil_up = _mm(o_dl2, dy_dl, ta=True, tm=DIL_OUT_WIDTH, tn=1024, tk=512, name="dil_up_dw")
    late_grads = {"w_sb_up": gw_sb_up, "w_dil_up": gw_dil_up, "w_out": gw_out, "w_ffn_in": gw_ffn_in, "w_ffn_out": gw_ffn_out}
    pair = _pair_reduce(late_grads, d_model, d_ff)
    (dq_sb, dk_sb, dv_sb), landed = _sb_bwd(qkv3, o_sb, do_sb.reshape(b_sz, s_len, SB_WIDTH), b_sz, s_len,
                                           [p[1] for p in pair])
    dq_dl, dk_dl, dv_dl = _dil_bwd(qkv3, o_dl, lse, do_dl.reshape(b_sz, s_len, DIL_OUT_WIDTH), b_sz, s_len)
    dproj = jnp.concatenate(
        [a.reshape(t, -1) for a in (dq_sb, dk_sb, dv_sb)]
        + [a.reshape(t, -1) for a in (dq_dl, dk_dl, dv_dl)] + [dgates], axis=1)
    gw_in = _mm(u, dproj, ta=True, tm=512, tn=wf_in.shape[1] // 2, tk=512, name="proj_dw")
    pair_in = _pair_reduce({"w_in": gw_in}, d_model, d_ff)
    du, landed_in = _mm(dproj, wf_in, tb=True, tm=512, tn=1024, tk=wf_in.shape[1] // 2, name="proj_dx",
                        carried=[p[1] for p in pair_in])
    dx, dg_mix = _rowwise(norm_bwd_fn, [dx1, du, x2d], [g_mix], [(d_model, F32)], [(1, d_model)], tm=512, name="norm_mix_bwd")

    grads = _chip_reduce(pair, landed, LATE)
    grads.update(_chip_reduce(pair_in, landed_in, ["w_in"]))
    return dx, grads, dg_mix, dg_ffn, dg_fin, loss_lanes


def kernel(x, norm_mix_g, w_in, w_sb_up, w_dil_up, w_out, norm_ffn_g, w_ffn_in, w_ffn_out, norm_final_g, loss_target, m_norm_mix_g, m_w_in, m_w_sb_up, m_w_dil_up, m_w_out, m_norm_ffn_g, m_w_ffn_in, m_w_ffn_out, m_norm_final_g, v_norm_mix_g, v_w_in, v_w_sb_up, v_w_dil_up, v_w_out, v_norm_ffn_g, v_w_ffn_in, v_w_ffn_out, v_norm_final_g):
    b_sz, s_len, d_model = x.shape
    d_ff = w_ffn_out.shape[1] * N_CHIPS
    g_mix, g_ffn, g_fin = norm_mix_g, norm_ffn_g, norm_final_g.reshape(1, d_model)

    names = ["w_in", "w_sb_up", "w_dil_up", "w_out", "w_ffn_in", "w_ffn_out"]
    shards = {"w_in": w_in[0], "w_sb_up": w_sb_up[0], "w_dil_up": w_dil_up[0], "w_out": w_out[0],
              "w_ffn_in": w_ffn_in[0], "w_ffn_out": w_ffn_out[0]}
    (slab_in,) = _gather_weights([_cast_to_slab(shards["w_in"], "cast_w_in")])
    late_slabs = [_cast_to_slab(shards[k], "cast_" + k) for k in LATE]

    dx, grads, dg_mix, dg_ffn, dg_fin, loss_lanes = _fwd_bwd(
        x, loss_target, g_mix, g_ffn, g_fin, _unshard_cols(slab_in), late_slabs)

    small = jnp.concatenate([dg_mix, dg_ffn, dg_fin, loss_lanes, jnp.zeros((4, d_model), F32)], axis=0)
    small = _all_sum_small(small)
    loss = small[3, 0]
    gains = jnp.concatenate([g_mix, g_ffn, g_fin, jnp.zeros((5, d_model), F32)], axis=0)
    gains_m = jnp.concatenate([m_norm_mix_g, m_norm_ffn_g, m_norm_final_g.reshape(1, d_model), jnp.zeros((5, d_model), F32)], axis=0)
    gains_v = jnp.concatenate([v_norm_mix_g, v_norm_ffn_g, v_norm_final_g.reshape(1, d_model), jnp.ones((5, d_model), F32)], axis=0)
    gd, gm, gv = _rowwise(_adamw_math, [gains, small, gains_m, gains_v], [], [(d_model, F32)] * 3, [], tm=8, name="adamw_gains")

    moments = {"w_in": (m_w_in, v_w_in), "w_sb_up": (m_w_sb_up, v_w_sb_up), "w_dil_up": (m_w_dil_up, v_w_dil_up),
               "w_out": (m_w_out, v_w_out), "w_ffn_in": (m_w_ffn_in, v_w_ffn_in), "w_ffn_out": (m_w_ffn_out, v_w_ffn_out)}
    upd = {k: _adamw(shards[k], grads[k], moments[k][0][0], moments[k][1][0], "adamw_" + k) for k in names}

    def w_out_of(i):
        return [upd[k][i][None] for k in names]

    def ordered(mix, ws, ffn_g, fin):
        return [mix, ws[0], ws[1], ws[2], ws[3], ffn_g, ws[4], ws[5], fin]

    grad_ws = [grads[k][None] for k in names]
    outs = [loss, dx.reshape(b_sz, s_len, d_model)]
    outs += ordered(small[0:1], grad_ws, small[1:2], small[2])
    outs += ordered(gd[0:1], w_out_of(0), gd[1:2], gd[2])
    outs += ordered(gm[0:1], w_out_of(1), gm[1:2], gm[2])
    outs += ordered(gv[0:1], w_out_of(2), gv[1:2], gv[2])
    return tuple(outs)
```

```python
import functools
import math

import jax
import jax.numpy as jnp
from jax import lax
from jax.experimental import pallas as pl
from jax.experimental.pallas import tpu as pltpu

F32 = jnp.float32
BF16 = jnp.bfloat16
MESH = pl.DeviceIdType.MESH

HEAD_DIM = 64
SB_HEADS = 8
DIL_PAIRS = ((128, 1), (512, 4), (2048, 16))
DIL_HEADS_PER_GROUP = 4
DIL_HEADS = DIL_HEADS_PER_GROUP * len(DIL_PAIRS)
SB_WIDTH = SB_HEADS * HEAD_DIM
DIL_WIDTH = DIL_HEADS * HEAD_DIM
DIL_OUT_WIDTH = DIL_HEADS_PER_GROUP * HEAD_DIM
QKV_WIDTH = 3 * SB_WIDTH + 3 * DIL_WIDTH
RMS_EPS = 1e-6
ALIBI_MAX_BIAS = 8.0
ADAM_LR = 0.001
ADAM_B1 = 0.9
ADAM_B2 = 0.999
ADAM_EPS = 1e-08
ADAM_WD = 0.01
ADAM_STEP = 10

LANES = 128
BLK = 128
NEG = -1e30
SB_FWD_CHAINS = 4
SB_BWD_CHAINS = 4
N_CHIPS = 4
VMEM_CAP = 56 * 1024 * 1024


def _vmem_limit(tile_bytes):
    return int(min(VMEM_CAP, max(32 * 1024 * 1024, 3 * tile_bytes + 8 * 1024 * 1024)))


def _nbytes(shape, dtype):
    return math.prod(shape) * jnp.dtype(dtype).itemsize


def _dot(a, b):
    return jnp.dot(a, b, preferred_element_type=F32)


def _dot_nt(a, b):
    return lax.dot_general(a, b, (((1,), (1,)), ((), ())), preferred_element_type=F32)


def _dot_tn(a, b):
    return lax.dot_general(a, b, (((0,), (0,)), ((), ())), preferred_element_type=F32)


def _split2(x):
    hi = x.astype(BF16)
    lo = (x - hi.astype(F32)).astype(BF16)
    return hi, lo


def _sigmoid(x):
    return 1.0 / (1.0 + jnp.exp(-x))


def _mm(a, b, *, ta=False, tb=False, add=None, out_dtype=F32, tm, tn, tk, name, carried=(), epilogue=None):
    n_car = len(carried)
    if ta:
        kdim, m = a.shape
    else:
        m, kdim = a.shape
    if tb:
        n, k2 = b.shape
    else:
        k2, n = b.shape
    assert kdim == k2 and m % tm == 0 and n % tn == 0 and kdim % tk == 0, (name, a.shape, b.shape)
    nk = kdim // tk
    grid = (m // tm, n // tn, nk)
    a_spec = pl.BlockSpec((tk, tm), lambda i, j, k: (k, i)) if ta else pl.BlockSpec((tm, tk), lambda i, j, k: (i, k))
    b_spec = pl.BlockSpec((tn, tk), lambda i, j, k: (j, k)) if tb else pl.BlockSpec((tk, tn), lambda i, j, k: (k, j))
    o_spec = pl.BlockSpec((tm, tn), lambda i, j, k: (i, j))
    dims = ((((0,) if ta else (1,)), ((1,) if tb else (0,))), ((), ()))
    has_add = add is not None
    if epilogue is None:
        ep_fn, ep_rows, ep_params, ep_outs, ep_accs = None, [], [], [], []
        out_sds, out_specs = [jax.ShapeDtypeStruct((m, n), out_dtype)], [o_spec]
    else:
        ep_fn, ep_rows, ep_params, ep_outs, ep_accs = epilogue
        assert grid[1] == 1, name
        out_sds = [jax.ShapeDtypeStruct((m, w), d) for w, d in ep_outs] + [jax.ShapeDtypeStruct(sh, F32) for sh in ep_accs]
        out_specs = [pl.BlockSpec((tm, w), lambda i, j, k: (i, 0)) for w, _ in ep_outs]
        out_specs += [pl.BlockSpec(sh, lambda i, j, k: (0, 0)) for sh in ep_accs]
    n_main = len(out_sds)
    use_scratch = nk > 1 and (ep_fn is not None or jnp.dtype(out_dtype) != jnp.dtype(F32))
    n_in = 2 + has_add + len(ep_rows) + len(ep_params)

    def finish(total, refs, pid):
        outs = refs[n_in + n_car:n_in + n_car + n_main]
        if ep_fn is None:
            outs[0][...] = total.astype(out_dtype)
            return
        first = 2 + has_add
        rows = [r[...].astype(F32) for r in refs[first:first + len(ep_rows)]]
        params = [p[...] for p in refs[first + len(ep_rows):n_in]]
        res = ep_fn(total, *rows, *params)
        for o_ref, v in zip(outs[:len(ep_outs)], res):
            o_ref[...] = v.astype(o_ref.dtype)
        acc_refs = outs[len(ep_outs):]
        if acc_refs:
            @pl.when(pid[0] == 0)
            def _():
                for r in acc_refs:
                    r[...] = jnp.zeros(r.shape, F32)

            for r, v in zip(acc_refs, res[len(ep_outs):]):
                r[...] += v

    def compute(refs, pid):
        a_ref, b_ref = refs[0], refs[1]
        add_ref = refs[2] if has_add else None
        prod = lax.dot_general(a_ref[...].astype(BF16), b_ref[...].astype(BF16), dims, preferred_element_type=F32)
        if nk == 1:
            finish(prod + add_ref[...] if has_add else prod, refs, pid)
            return
        acc_ref = refs[n_in + 2 * n_car + n_main] if use_scratch else refs[n_in + n_car]
        k = pid[2]

        @pl.when(k == 0)
        def _():
            acc_ref[...] = prod + add_ref[...] if has_add else prod

        @pl.when(k > 0)
        def _():
            acc_ref[...] += prod

        if use_scratch:
            @pl.when(k == nk - 1)
            def _():
                finish(acc_ref[...], refs, pid)

    def body(*refs):
        pid = (pl.program_id(0), pl.program_id(1), pl.program_id(2))
        if not n_car:
            compute(refs, pid)
            return
        lands = refs[n_in + n_car + n_main:n_in + 2 * n_car + n_main]
        copies = _chip_copies(refs[n_in:n_in + n_car], lands, *refs[-2:])
        step = (pid[0] * grid[1] + pid[1]) * nk + pid[2]

        @pl.when(step == 0)
        def _():
            for cp in copies:
                cp.start()

        compute(refs, pid)

        @pl.when(step == grid[0] * grid[1] * nk - 1)
        def _():
            for cp in copies:
                cp.wait()

    tile_bytes = (_nbytes((tm, tk), a.dtype) + _nbytes((tk, tn), b.dtype) + 2 * _nbytes((tm, tn), F32)
                  + (_nbytes((tm, tn), F32) if has_add else 0)
                  + sum(_nbytes((tm, r.shape[1]), r.dtype) for r in ep_rows) + sum(_nbytes((tm, w), d) for w, d in ep_outs))
    in_specs = [a_spec, b_spec] + ([o_spec] if has_add else [])
    in_specs += [pl.BlockSpec((tm, r.shape[1]), lambda i, j, k: (i, 0)) for r in ep_rows]
    in_specs += [pl.BlockSpec(p.shape, lambda i, j, k: (0, 0)) for p in ep_params]
    args = (a, b) + ((add,) if has_add else ()) + tuple(ep_rows) + tuple(ep_params)
    scratch = [pltpu.VMEM((tm, tn), F32)] if use_scratch else []
    serial = bool(n_car or ep_accs)
    res = pl.pallas_call(
        body, name=name, grid=grid,
        in_specs=in_specs + _hbm_specs(n_car), out_specs=out_specs + _hbm_specs(n_car),
        out_shape=out_sds + _chip_landing(carried), scratch_shapes=scratch + (_chip_sems(n_car) if n_car else []),
        compiler_params=pltpu.CompilerParams(
            dimension_semantics=("arbitrary",) * 3 if serial else ("parallel", "parallel", "arbitrary"),
            vmem_limit_bytes=_vmem_limit(tile_bytes)),
    )(*args, *carried)
    main = res[0] if ep_fn is None else list(res[:n_main])
    return (main, res[n_main:]) if n_car else main


def _rowwise(fn, rows, params, outs, accs, *, tm, name):
    t = rows[0].shape[0]
    assert t % tm == 0, (name, t, tm)
    n_r, n_p, n_o = len(rows), len(params), len(outs)

    def body(*refs):
        vals = [r[...].astype(F32) for r in refs[:n_r]] + [p[...] for p in refs[n_r:n_r + n_p]]
        res = fn(*vals)
        o_refs = refs[n_r + n_p:n_r + n_p + n_o]
        a_refs = refs[n_r + n_p + n_o:]
        for o_ref, v in zip(o_refs, res[:n_o]):
            o_ref[...] = v.astype(o_ref.dtype)
        if accs:
            @pl.when(pl.program_id(0) == 0)
            def _():
                for a_ref in a_refs:
                    a_ref[...] = jnp.zeros(a_ref.shape, F32)

            for a_ref, v in zip(a_refs, res[n_o:]):
                a_ref[...] += v

    in_specs = [pl.BlockSpec((tm, r.shape[1]), lambda i: (i, 0)) for r in rows]
    in_specs += [pl.BlockSpec(p.shape, lambda i: (0, 0)) for p in params]
    out_specs = [pl.BlockSpec((tm, w), lambda i: (i, 0)) for w, _ in outs]
    out_specs += [pl.BlockSpec(s, lambda i: (0, 0)) for s in accs]
    out_shape = [jax.ShapeDtypeStruct((t, w), d) for w, d in outs]
    out_shape += [jax.ShapeDtypeStruct(s, F32) for s in accs]
    tile_bytes = sum(_nbytes((tm, r.shape[1]), r.dtype) for r in rows) + sum(_nbytes((tm, w), F32) for w, _ in outs)
    res = pl.pallas_call(
        body, name=name, grid=(t // tm,), in_specs=in_specs, out_specs=out_specs, out_shape=out_shape,
        compiler_params=pltpu.CompilerParams(
            dimension_semantics=("arbitrary",) if accs else ("parallel",),
            vmem_limit_bytes=_vmem_limit(2 * tile_bytes)),
    )(*rows, *params)
    return res


def _rms_stats(x):
    r = lax.rsqrt(jnp.mean(x * x, axis=-1, keepdims=True) + RMS_EPS)
    return x * r, r


def _rms_bwd(dy, xhat, r, g):
    dxhat = dy * g
    dx = r * (dxhat - xhat * jnp.mean(dxhat * xhat, axis=-1, keepdims=True))
    return dx, dy * xhat


def _sb_consts():
    lane = lax.broadcasted_iota(jnp.int32, (BLK, LANES), 1)
    head0 = lane < HEAD_DIM
    row = lax.broadcasted_iota(jnp.int32, (2 * BLK, BLK), 0) % BLK
    col = lax.broadcasted_iota(jnp.int32, (2 * BLK, BLK), 1)
    causal = col < row
    jj = lax.broadcasted_iota(jnp.int32, (BLK, BLK), 0)
    ss = lax.broadcasted_iota(jnp.int32, (BLK, BLK), 1)
    suffix = jnp.where(jj > ss, 1.0, 0.0).astype(BF16)
    return head0, causal, suffix


def _stack_heads(x, head0):
    zero = jnp.zeros_like(x)
    return jnp.concatenate([jnp.where(head0, x, zero), jnp.where(head0, zero, x)], axis=0)


def _sb_logits(z, causal, masked):
    sp = jnp.log(1.0 + jnp.exp(-jnp.abs(z)))
    log_keep = -(jnp.maximum(z, 0.0) + sp)
    log_beta = jnp.minimum(z, 0.0) - sp
    if masked:
        log_keep = jnp.where(causal, log_keep, 0.0)
    return log_keep, log_beta


def _suffix_sums(x, suffix):
    hi, lo = _split2(x)
    after = _dot(hi, suffix) + _dot(lo, suffix)
    total = jnp.broadcast_to(after[:, 0:1] + x[:, 0:1], x.shape)
    return after, total


def _lane_blocks(x, n):
    return [x[:, p * LANES:(p + 1) * LANES] for p in range(n)]


def _sb_fwd(qkv, b_sz, s_len, slabs):
    n_slabs = len(slabs)
    nq = s_len // BLK
    n_pairs = SB_WIDTH // LANES
    ch = SB_FWD_CHAINS
    n_steps = n_pairs // ch
    scale = 1.0 / math.sqrt(HEAD_DIM)

    def body(q_ref, k_ref, v_ref, *rest):
        o_ref = rest[n_slabs]
        slab_refs = rest[n_slabs + 1:2 * n_slabs + 1]
        send_sems, recv_sems = rest[2 * n_slabs + 1:]
        step = pl.program_id(0) * n_steps + pl.program_id(1)
        head0, causal, suffix = _sb_consts()

        @pl.when(step == 0)
        def _():
            _gather_issue(slab_refs, send_sems, recv_sems)

        def q_block(i, _):
            qs = pl.multiple_of(i * BLK, BLK)
            q_all = (q_ref[pl.ds(qs, BLK), :] * scale).astype(BF16)
            q01 = [_stack_heads(q, head0) for q in _lane_blocks(q_all, ch)]

            def tile(j, state, masked):
                ks = pl.multiple_of(j * BLK, BLK)
                ks_ = _lane_blocks(k_ref[pl.ds(ks, BLK), :].astype(BF16), ch)
                vs_ = _lane_blocks(v_ref[pl.ds(ks, BLK), :].astype(BF16), ch)
                zs = [_dot_nt(q01[p], ks_[p]) for p in range(ch)]
                logits = [_sb_logits(z, causal, masked) for z in zs]
                sums = [_suffix_sums(lg[0], suffix) for lg in logits]
                out = []
                for p in range(ch):
                    carry, acc = state[2 * p], state[2 * p + 1]
                    after, total = sums[p]
                    a = jnp.exp(logits[p][1] + carry + after)
                    if masked:
                        a = jnp.where(causal, a, 0.0)
                    a_hi, a_lo = _split2(a)
                    a_cat = jnp.concatenate([a_hi[:BLK], a_hi[BLK:], a_lo[:BLK], a_lo[BLK:]], axis=1)
                    v01 = _stack_heads(vs_[p], head0)
                    out += [carry + total, acc + _dot(a_cat, jnp.concatenate([v01, v01], axis=0))]
                return tuple(out)

            state = (jnp.zeros((2 * BLK, BLK), F32), jnp.zeros((BLK, LANES), F32)) * ch
            state = tile(i, state, True)
            state = lax.fori_loop(0, i, lambda t, st: tile(i - 1 - t, st, False), state)
            o_ref[pl.ds(qs, BLK), :] = jnp.concatenate([state[2 * p + 1] for p in range(ch)], axis=1)
            return 0

        lax.fori_loop(0, nq, q_block, 0)

        @pl.when(step == b_sz * n_steps - 1)
        def _():
            _gather_complete(slab_refs, send_sems, recv_sems)

    blk = lambda off: pl.BlockSpec((None, s_len, ch * LANES), lambda b, p: (b, 0, off + p))
    res = pl.pallas_call(
        body, name="sb_fwd", grid=(b_sz, n_steps),
        in_specs=[blk(0), blk(n_steps), blk(2 * n_steps)] + _hbm_specs(n_slabs),
        out_specs=[blk(0)] + _hbm_specs(n_slabs),
        out_shape=[jax.ShapeDtypeStruct((b_sz, s_len, SB_WIDTH), F32)]
        + [jax.ShapeDtypeStruct(a.shape, a.dtype) for a in slabs],
        input_output_aliases={3 + k: 1 + k for k in range(n_slabs)},
        scratch_shapes=_gather_sems(n_slabs),
        compiler_params=pltpu.CompilerParams(dimension_semantics=("arbitrary", "arbitrary"),
                                             vmem_limit_bytes=VMEM_CAP),
    )(qkv, qkv, qkv, *slabs)
    return res[0], res[1:]


def _sb_bwd(qkv, o_sb, do_sb, b_sz, s_len, sums_bf16):
    n_sums = len(sums_bf16)
    nq = s_len // BLK
    n_pairs = SB_WIDTH // LANES
    ch = SB_BWD_CHAINS
    n_steps = n_pairs // ch
    scale = 1.0 / math.sqrt(HEAD_DIM)

    def body(q_ref, k_ref, v_ref, o_ref, do_ref, *rest):
        sum_refs = rest[:n_sums]
        dq_ref, dk_ref, dv_ref = rest[n_sums:n_sums + 3]
        land_refs = rest[n_sums + 3:2 * n_sums + 3]
        dk_acc, dv_acc, send_sems, recv_sems = rest[2 * n_sums + 3:]
        step = pl.program_id(0) * n_steps + pl.program_id(1)
        copies = _chip_copies(sum_refs, land_refs, send_sems, recv_sems)

        @pl.when(step == 0)
        def _():
            for cp in copies:
                cp.start()

        head0, causal, suffix = _sb_consts()
        lrow = lax.broadcasted_iota(jnp.int32, (LANES, LANES), 0)
        ones_h0 = jnp.where(lrow < HEAD_DIM, 1.0, 0.0).astype(BF16)
        ones_h1 = jnp.where(lrow >= HEAD_DIM, 1.0, 0.0).astype(BF16)
        dk_acc[...] = jnp.zeros(dk_acc.shape, F32)
        dv_acc[...] = jnp.zeros(dv_acc.shape, F32)

        def q_block(i, _):
            qs = pl.multiple_of(i * BLK, BLK)
            q_all = (q_ref[pl.ds(qs, BLK), :] * scale).astype(BF16)
            do_all = do_ref[pl.ds(qs, BLK), :].astype(BF16)
            dd_all = do_all.astype(F32) * o_ref[pl.ds(qs, BLK), :]
            q01 = [_stack_heads(q, head0) for q in _lane_blocks(q_all, ch)]
            do01 = [_stack_heads(d, head0) for d in _lane_blocks(do_all, ch)]
            tot = []
            for dd in _lane_blocks(dd_all, ch):
                dd_hi, dd_lo = _split2(dd)
                tot.append(jnp.concatenate([_dot(dd_hi, ones_h0) + _dot(dd_lo, ones_h0),
                                            _dot(dd_hi, ones_h1) + _dot(dd_lo, ones_h1)], axis=0))

            def tile(j, state, masked):
                ks = pl.multiple_of(j * BLK, BLK)
                ks_ = _lane_blocks(k_ref[pl.ds(ks, BLK), :].astype(BF16), ch)
                vs_ = _lane_blocks(v_ref[pl.ds(ks, BLK), :].astype(BF16), ch)
                zs = [_dot_nt(q01[p], ks_[p]) for p in range(ch)]
                das = [_dot_nt(do01[p], vs_[p]) for p in range(ch)]
                logits = [_sb_logits(z, causal, masked) for z in zs]
                sums = [_suffix_sums(lg[0], suffix) for lg in logits]
                a_s, e_s = [], []
                for p in range(ch):
                    a = jnp.exp(logits[p][1] + state[3 * p] + sums[p][0])
                    if masked:
                        a = jnp.where(causal, a, 0.0)
                    a_s.append(a)
                    e_s.append(a * das[p])
                e_sums = [_suffix_sums(e, suffix) for e in e_s]
                out, dks, dvs = [], [], []
                for p in range(ch):
                    carry, rcarry, dq = state[3 * p:3 * p + 3]
                    e = e_s[p]
                    before = tot[p] - (rcarry + e_sums[p][0] + e)
                    beta = jnp.exp(logits[p][1])
                    dz = e * (1.0 - beta) - beta * before
                    if masked:
                        dz = jnp.where(causal, dz, 0.0)
                    dz_b = dz.astype(BF16)
                    dks.append(_dot_tn(dz_b, q01[p]))
                    dvs.append(_dot_tn(a_s[p].astype(BF16), do01[p]))
                    out += [carry + sums[p][1], rcarry + e_sums[p][1], dq + _dot(dz_b, ks_[p])]
                dk_acc[pl.ds(ks, BLK), :] += jnp.concatenate(dks, axis=1)
                dv_acc[pl.ds(ks, BLK), :] += jnp.concatenate(dvs, axis=1)
                return tuple(out)

            state = (jnp.zeros((2 * BLK, BLK), F32),) * (3 * ch)
            state = tile(i, state, True)
            state = lax.fori_loop(0, i, lambda t, st: tile(i - 1 - t, st, False), state)
            dq = [jnp.where(head0, state[3 * p + 2][:BLK], state[3 * p + 2][BLK:]) for p in range(ch)]
            dq_ref[pl.ds(qs, BLK), :] = (jnp.concatenate(dq, axis=1) * scale).astype(dq_ref.dtype)
            return 0

        lax.fori_loop(0, nq, q_block, 0)
        dk_ref[...] = dk_acc[...].astype(dk_ref.dtype)
        dv_ref[...] = dv_acc[...].astype(dv_ref.dtype)

        @pl.when(step == b_sz * n_steps - 1)
        def _():
            for cp in copies:
                cp.wait()

    blk = lambda off: pl.BlockSpec((None, s_len, ch * LANES), lambda b, p: (b, 0, off + p))
    once = lambda off: pl.BlockSpec((None, s_len, ch * LANES), lambda b, p: (b, 0, off + p),
                                    pipeline_mode=pl.Buffered(1))
    out_sd = jax.ShapeDtypeStruct((b_sz, s_len, SB_WIDTH), BF16)
    res = pl.pallas_call(
        body, name="sb_bwd", grid=(b_sz, n_steps),
        in_specs=[once(0), once(n_steps), once(2 * n_steps), once(0), once(0)] + _hbm_specs(n_sums),
        out_specs=[blk(0), blk(0), blk(0)] + _hbm_specs(n_sums),
        out_shape=[out_sd, out_sd, out_sd] + _chip_landing(sums_bf16),
        scratch_shapes=[pltpu.VMEM((s_len, ch * LANES), F32), pltpu.VMEM((s_len, ch * LANES), F32)] + _chip_sems(n_sums),
        compiler_params=pltpu.CompilerParams(dimension_semantics=("arbitrary", "arbitrary"),
                                             vmem_limit_bytes=VMEM_CAP),
    )(qkv, qkv, qkv, o_sb, do_sb, *sums_bf16)
    return res[:3], res[3:]


def _dil_consts(group, pair_idx, dilation):
    lane = lax.broadcasted_iota(jnp.int32, (BLK, LANES), 1)
    head0 = lane < HEAD_DIM
    row = lax.broadcasted_iota(jnp.int32, (2 * BLK, BLK), 0)
    qa = row % BLK
    kb = lax.broadcasted_iota(jnp.int32, (2 * BLK, BLK), 1)
    head = (group * DIL_HEADS_PER_GROUP + 2 * pair_idx + row // BLK).astype(F32)
    slope = jnp.exp((-ALIBI_MAX_BIAS * math.log(2.0) / DIL_HEADS) * (head + 1.0))
    valid_cur = kb <= qa
    valid_prev = kb >= qa
    bias_cur = -slope * ((qa - kb) * dilation).astype(F32)
    bias_prev = -slope * ((BLK + qa - kb) * dilation).astype(F32)
    return head0, valid_cur, valid_prev, bias_cur, bias_prev


def _dil_units(s_len, dilation):
    nb = s_len // dilation // BLK
    return [(r, n) for r in range(dilation) for n in range(nb)]


def _dil_rows(n, r, dilation):
    if dilation == 1:
        return pl.ds(n * BLK, BLK)
    return pl.ds(n * BLK * dilation + r, BLK, stride=dilation)


def _dil_scores(q01, k, bias, valid):
    s = _dot_nt(q01, k) * (1.0 / math.sqrt(HEAD_DIM)) + bias
    return jnp.where(valid, s, NEG)


def _dil_fwd(qkv, b_sz, s_len):
    n_pairs = DIL_OUT_WIDTH // LANES
    q_off = 3 * SB_WIDTH // LANES
    per_kind = DIL_WIDTH // LANES

    def body(*refs):
        qkv_refs = refs[:9]
        o_ref, lse_ref, m_s, l_s = refs[9:]
        pair_idx = pl.program_id(1)
        m_s[...] = jnp.full(m_s.shape, NEG, F32)
        l_s[...] = jnp.zeros(l_s.shape, F32)
        o_ref[...] = jnp.zeros(o_ref.shape, F32)
        for g, (_, dilation) in enumerate(DIL_PAIRS):
            q_ref, k_ref, v_ref = qkv_refs[3 * g:3 * g + 3]
            head0, valid_cur, valid_prev, bias_cur, bias_prev = _dil_consts(g, pair_idx, dilation)
            for r, n in _dil_units(s_len, dilation):
                rows = _dil_rows(n, r, dilation)
                q01 = _stack_heads(q_ref[rows, :].astype(BF16), head0)
                k_c = k_ref[rows, :].astype(BF16)
                v_c = v_ref[rows, :].astype(BF16)
                scores = [_dil_scores(q01, k_c, bias_cur, valid_cur)]
                values = [_stack_heads(v_c, head0)]
                if n > 0:
                    prev = _dil_rows(n - 1, r, dilation)
                    scores.append(_dil_scores(q01, k_ref[prev, :].astype(BF16), bias_prev, valid_prev))
                    values.append(_stack_heads(v_ref[prev, :].astype(BF16), head0))
                m_blk = functools.reduce(jnp.maximum, [jnp.max(s, axis=-1, keepdims=True) for s in scores])
                m_old = jnp.concatenate([m_s.at[0][rows, :], m_s.at[1][rows, :]], axis=0)
                l_old = jnp.concatenate([l_s.at[0][rows, :], l_s.at[1][rows, :]], axis=0)
                m_new = jnp.maximum(m_old, m_blk)
                probs = [jnp.exp(s - m_new) for s in scores]
                l_blk = functools.reduce(jnp.add, [jnp.sum(p, axis=-1, keepdims=True) for p in probs])
                alpha = jnp.exp(m_old - m_new)
                l_new = alpha * l_old + l_blk
                alpha_tok = jnp.where(head0, alpha[:BLK], alpha[BLK:])
                p_cat = jnp.concatenate([h for p in probs for h in (p[:BLK].astype(BF16), p[BLK:].astype(BF16))], axis=1)
                o_ref[rows, :] = alpha_tok * o_ref[rows, :] + _dot(p_cat, jnp.concatenate(values, axis=0))
                m_s.at[0][rows, :] = m_new[:BLK]
                m_s.at[1][rows, :] = m_new[BLK:]
                l_s.at[0][rows, :] = l_new[:BLK]
                l_s.at[1][rows, :] = l_new[BLK:]
        lane = lax.broadcasted_iota(jnp.int32, (BLK, LANES), 1)
        for c in range(s_len // BLK):
            rows = pl.ds(c * BLK, BLK)
            l0, l1 = l_s.at[0][rows, :], l_s.at[1][rows, :]
            o_ref[rows, :] = o_ref[rows, :] / jnp.where(lane < HEAD_DIM, l0, l1)
            lse_ref.at[0][rows, :] = m_s.at[0][rows, :] + jnp.log(l0)
            lse_ref.at[1][rows, :] = m_s.at[1][rows, :] + jnp.log(l1)

    in_specs = []
    for g in range(len(DIL_PAIRS)):
        for kind in range(3):
            off = q_off + kind * per_kind + g * n_pairs
            in_specs.append(pl.BlockSpec((None, s_len, LANES), lambda b, p, off=off: (b, 0, off + p)))
    return pl.pallas_call(
        body, name="dil_fwd", grid=(b_sz, n_pairs),
        in_specs=in_specs,
        out_specs=[pl.BlockSpec((None, s_len, LANES), lambda b, p: (b, 0, p)),
                   pl.BlockSpec((None, None, 2, s_len, LANES), lambda b, p: (b, p, 0, 0, 0))],
        out_shape=[jax.ShapeDtypeStruct((b_sz, s_len, DIL_OUT_WIDTH), F32),
                   jax.ShapeDtypeStruct((b_sz, n_pairs, 2, s_len, LANES), F32)],
        scratch_shapes=[pltpu.VMEM((2, s_len, LANES), F32), pltpu.VMEM((2, s_len, LANES), F32)],
        compiler_params=pltpu.CompilerParams(dimension_semantics=("parallel", "parallel"),
                                             vmem_limit_bytes=VMEM_CAP),
    )(*([qkv] * 9))


def _dil_bwd(qkv, o_dl, lse, do_dl, b_sz, s_len):
    n_pairs = DIL_OUT_WIDTH // LANES
    n_groups = len(DIL_PAIRS)
    q_off = 3 * SB_WIDTH // LANES
    per_kind = DIL_WIDTH // LANES

    def body(q_ref, k_ref, v_ref, o_ref, lse_ref, do_ref, dq_ref, dk_ref, dv_ref, d_s, dq_s, dk_s, dv_s):
        pair_idx = pl.program_id(1)
        group = pl.program_id(2)
        lrow = lax.broadcasted_iota(jnp.int32, (LANES, LANES), 0)
        ones_h0 = jnp.where(lrow < HEAD_DIM, 1.0, 0.0).astype(BF16)
        ones_h1 = jnp.where(lrow >= HEAD_DIM, 1.0, 0.0).astype(BF16)
        for c in range(s_len // BLK):
            rows = pl.ds(c * BLK, BLK)
            dd_hi, dd_lo = _split2(do_ref[rows, :] * o_ref[rows, :])
            d_s.at[0][rows, :] = _dot(dd_hi, ones_h0) + _dot(dd_lo, ones_h0)
            d_s.at[1][rows, :] = _dot(dd_hi, ones_h1) + _dot(dd_lo, ones_h1)
        dk_s[...] = jnp.zeros(dk_s.shape, F32)
        dv_s[...] = jnp.zeros(dv_s.shape, F32)

        def one_group(g, dilation):
            head0, valid_cur, valid_prev, bias_cur, bias_prev = _dil_consts(g, pair_idx, dilation)
            for r, n in _dil_units(s_len, dilation):
                rows = _dil_rows(n, r, dilation)
                q01 = _stack_heads(q_ref[rows, :].astype(BF16), head0)
                do01 = _stack_heads(do_ref[rows, :].astype(BF16), head0)
                lse01 = jnp.concatenate([lse_ref.at[0][rows, :], lse_ref.at[1][rows, :]], axis=0)
                d01 = jnp.concatenate([d_s.at[0][rows, :], d_s.at[1][rows, :]], axis=0)
                dq = jnp.zeros((2 * BLK, LANES), F32)
                blocks = [(rows, bias_cur, valid_cur)]
                if n > 0:
                    blocks.append((_dil_rows(n - 1, r, dilation), bias_prev, valid_prev))
                for krows, bias, valid in blocks:
                    k = k_ref[krows, :].astype(BF16)
                    v = v_ref[krows, :].astype(BF16)
                    p = jnp.exp(_dil_scores(q01, k, bias, valid) - lse01)
                    ds = (p * (_dot_nt(do01, v) - d01) * (1.0 / math.sqrt(HEAD_DIM))).astype(BF16)
                    dq = dq + _dot(ds, k)
                    dk_s[krows, :] = dk_s[krows, :] + _dot_tn(ds, q01)
                    dv_s[krows, :] = dv_s[krows, :] + _dot_tn(p.astype(BF16), do01)
                dq_s[rows, :] = jnp.where(head0, dq[:BLK], dq[BLK:])

        for g, (_, dilation) in enumerate(DIL_PAIRS):
            pl.when(group == g)(functools.partial(one_group, g, dilation))
        dq_ref[...] = dq_s[...].astype(dq_ref.dtype)
        dk_ref[...] = dk_s[...].astype(dk_ref.dtype)
        dv_ref[...] = dv_s[...].astype(dv_ref.dtype)

    def qkv_spec(kind):
        return pl.BlockSpec((None, s_len, LANES),
                            lambda b, p, g: (b, 0, q_off + kind * per_kind + g * n_pairs + p))

    tok_spec = pl.BlockSpec((None, s_len, LANES), lambda b, p, g: (b, 0, p))
    out_spec = pl.BlockSpec((None, s_len, LANES), lambda b, p, g: (b, 0, g * n_pairs + p))
    out_sd = jax.ShapeDtypeStruct((b_sz, s_len, DIL_WIDTH), BF16)
    return pl.pallas_call(
        body, name="dil_bwd", grid=(b_sz, n_pairs, n_groups),
        in_specs=[qkv_spec(0), qkv_spec(1), qkv_spec(2), tok_spec,
                  pl.BlockSpec((None, None, 2, s_len, LANES), lambda b, p, g: (b, p, 0, 0, 0)), tok_spec],
        out_specs=[out_spec, out_spec, out_spec],
        out_shape=[out_sd, out_sd, out_sd],
        scratch_shapes=[pltpu.VMEM((2, s_len, LANES), F32)] + [pltpu.VMEM((s_len, LANES), F32)] * 3,
        compiler_params=pltpu.CompilerParams(dimension_semantics=("parallel", "parallel", "arbitrary"),
                                             vmem_limit_bytes=VMEM_CAP),
    )(qkv, qkv, qkv, o_dl, lse, do_dl)


def _mesh_pos():
    return lax.axis_index("x"), lax.axis_index("y"), lax.axis_index("c")


def _other_chips(x, y):
    return [(1 - x, y), (x, 1 - y), (1 - x, 1 - y)]


def _hbm_specs(n):
    return [pl.BlockSpec(memory_space=pl.ANY)] * n


def _cast_to_slab(w, name):
    rows, cols = w.shape
    mine = jnp.reshape(2 * lax.axis_index("x") + lax.axis_index("y"), (1,)).astype(jnp.int32)

    def body(idx_ref, w_ref, o_ref):
        o_ref[...] = w_ref[...].astype(BF16)

    return pl.pallas_call(
        body, name=name,
        grid_spec=pltpu.PrefetchScalarGridSpec(
            num_scalar_prefetch=1, grid=(1,),
            in_specs=[pl.BlockSpec((rows, cols), lambda i, idx: (0, 0))],
            out_specs=pl.BlockSpec((None, rows, cols), lambda i, idx: (idx[0], 0, 0))),
        out_shape=jax.ShapeDtypeStruct((N_CHIPS, rows, cols), BF16),
        compiler_params=pltpu.CompilerParams(vmem_limit_bytes=_vmem_limit(rows * cols * 6)),
    )(mine, w)


def _gather_issue(slabs, send_sems, recv_sems):
    x, y, c = _mesh_pos()
    for k, slab in enumerate(slabs):
        half = slab.shape[1] // 2
        rows = slab.at[2 * x + y, pl.ds(c * half, half), :]
        for r, (px, py) in enumerate(_other_chips(x, y)):
            pltpu.make_async_remote_copy(
                src_ref=rows, dst_ref=rows, send_sem=send_sems.at[6 * k + r], recv_sem=recv_sems.at[6 * k + r],
                device_id=(px, py, c), device_id_type=MESH).start()


def _gather_complete(slabs, send_sems, recv_sems):
    x, y, c = _mesh_pos()
    chips = _other_chips(x, y)

    def copy(k, sem, block, rows, to):
        ref = slabs[k].at[block, rows, :]
        return pltpu.make_async_remote_copy(
            src_ref=ref, dst_ref=ref, send_sem=send_sems.at[sem], recv_sem=recv_sems.at[sem],
            device_id=to, device_id_type=MESH)

    for k, slab in enumerate(slabs):
        half = slab.shape[1] // 2
        for r, (px, py) in enumerate(chips):
            copy(k, 6 * k + r, 2 * px + py, pl.ds(c * half, half), (px, py, c)).wait_recv()
            copy(k, 6 * k + 3 + r, 2 * px + py, pl.ds(c * half, half), (x, y, 1 - c)).start()
    for k, slab in enumerate(slabs):
        half = slab.shape[1] // 2
        for r, (px, py) in enumerate(chips):
            copy(k, 6 * k + 3 + r, 2 * px + py, pl.ds((1 - c) * half, half), (x, y, 1 - c)).wait_recv()
    for k, slab in enumerate(slabs):
        half = slab.shape[1] // 2
        for r, (px, py) in enumerate(chips):
            copy(k, 6 * k + r, 2 * x + y, pl.ds(c * half, half), (px, py, c)).wait_send()
            copy(k, 6 * k + 3 + r, 2 * px + py, pl.ds(c * half, half), (x, y, 1 - c)).wait_send()


def _gather_sems(n):
    return [pltpu.SemaphoreType.DMA((6 * n,)), pltpu.SemaphoreType.DMA((6 * n,))]


def _gather_weights(slabs):
    n = len(slabs)

    def body(*refs):
        outs = refs[n:2 * n]
        send_sems, recv_sems = refs[2 * n:]
        _gather_issue(outs, send_sems, recv_sems)
        _gather_complete(outs, send_sems, recv_sems)

    return pl.pallas_call(
        body, name="gather_weights",
        in_specs=_hbm_specs(n), out_specs=_hbm_specs(n),
        out_shape=[jax.ShapeDtypeStruct(s.shape, s.dtype) for s in slabs],
        input_output_aliases={k: k for k in range(n)},
        scratch_shapes=_gather_sems(n),
    )(*slabs)


def _pair_exchange(grads, tag):
    n = len(grads)

    def body(*refs):
        ins, outs = refs[:n], refs[n:2 * n]
        send_sems, recv_sems = refs[2 * n:]
        x, y, c = _mesh_pos()
        copies = []
        for k in range(n):
            half = grads[k].shape[1] // 2
            cp = pltpu.make_async_remote_copy(
                src_ref=ins[k].at[:, pl.ds((1 - c) * half, half), :], dst_ref=outs[k],
                send_sem=send_sems.at[k], recv_sem=recv_sems.at[k],
                device_id=(x, y, 1 - c), device_id_type=MESH)
            cp.start()
            copies.append(cp)
        for cp in copies:
            cp.wait()

    return pl.pallas_call(
        body, name="grad_pair_exchange_" + tag,
        in_specs=_hbm_specs(n), out_specs=_hbm_specs(n),
        out_shape=[jax.ShapeDtypeStruct((N_CHIPS, g.shape[1] // 2, g.shape[2]), F32) for g in grads],
        scratch_shapes=[pltpu.SemaphoreType.DMA((n,)), pltpu.SemaphoreType.DMA((n,))],
    )(*grads)


def _pair_sum(grad, other, name):
    _, rows, cols = grad.shape
    half = rows // 2
    core = jnp.reshape(lax.axis_index("c"), (1,)).astype(jnp.int32)

    def body(core_ref, g_ref, p_ref, s_ref, sb_ref):
        s = g_ref[...] + p_ref[...]
        s_ref[...] = s
        sb_ref[...] = s.astype(BF16)

    blk = pl.BlockSpec((None, half, cols), lambda p, core_ref: (p, 0, 0))
    return pl.pallas_call(
        body, name=name,
        grid_spec=pltpu.PrefetchScalarGridSpec(
            num_scalar_prefetch=1, grid=(N_CHIPS,),
            in_specs=[pl.BlockSpec((None, half, cols), lambda p, core_ref: (p, core_ref[0], 0)), blk],
            out_specs=[blk, blk]),
        out_shape=[jax.ShapeDtypeStruct((N_CHIPS, half, cols), F32),
                   jax.ShapeDtypeStruct((N_CHIPS, half, cols), BF16)],
        compiler_params=pltpu.CompilerParams(dimension_semantics=("parallel",),
                                             vmem_limit_bytes=_vmem_limit(4 * half * cols * 4)),
    )(core, grad, other)


def _chip_copies(sums_bf16, lands, send_sems, recv_sems):
    x, y, c = _mesh_pos()
    return [pltpu.make_async_remote_copy(
        src_ref=sums_bf16[k].at[2 * px + py], dst_ref=lands[k].at[r],
        send_sem=send_sems.at[3 * k + r], recv_sem=recv_sems.at[3 * k + r],
        device_id=(px, py, c), device_id_type=MESH)
        for k in range(len(sums_bf16)) for r, (px, py) in enumerate(_other_chips(x, y))]


def _chip_sems(n):
    return [pltpu.SemaphoreType.DMA((3 * n,)), pltpu.SemaphoreType.DMA((3 * n,))]


def _chip_landing(sums_bf16):
    return [jax.ShapeDtypeStruct((N_CHIPS - 1,) + s.shape[1:], BF16) for s in sums_bf16]


def _chip_sum(sums_f32, landed, name):
    _, rows, cols = sums_f32.shape
    x, y, c = _mesh_pos()
    idx = jnp.stack([2 * x + y, c]).astype(jnp.int32)

    def body(idx_ref, o_ref, l_ref, out_ref):
        out_ref[...] = ((o_ref[...] + l_ref[0].astype(F32)) + l_ref[1].astype(F32)) + l_ref[2].astype(F32)

    return pl.pallas_call(
        body, name=name,
        grid_spec=pltpu.PrefetchScalarGridSpec(
            num_scalar_prefetch=1, grid=(1,),
            in_specs=[pl.BlockSpec((None, rows, cols), lambda i, idx: (idx[0], 0, 0)),
                      pl.BlockSpec((N_CHIPS - 1, rows, cols), lambda i, idx: (0, 0, 0))],
            out_specs=pl.BlockSpec((rows, cols), lambda i, idx: (idx[1], 0))),
        out_shape=jax.ShapeDtypeStruct((2 * rows, cols), F32),
        compiler_params=pltpu.CompilerParams(vmem_limit_bytes=_vmem_limit(3 * rows * cols * 4)),
    )(idx, sums_f32, landed)


def _halves_to_full(fulls, tag):
    n = len(fulls)

    def body(*refs):
        outs = refs[n:2 * n]
        send_sems, recv_sems = refs[2 * n:]
        x, y, c = _mesh_pos()
        copies = []
        for k in range(n):
            half = fulls[k].shape[0] // 2
            rows = outs[k].at[pl.ds(c * half, half), :]
            cp = pltpu.make_async_remote_copy(
                src_ref=rows, dst_ref=rows, send_sem=send_sems.at[k], recv_sem=recv_sems.at[k],
                device_id=(x, y, 1 - c), device_id_type=MESH)
            cp.start()
            copies.append(cp)
        for k in range(n):
            half = fulls[k].shape[0] // 2
            theirs = outs[k].at[pl.ds((1 - c) * half, half), :]
            pltpu.make_async_remote_copy(
                src_ref=theirs, dst_ref=theirs, send_sem=send_sems.at[k], recv_sem=recv_sems.at[k],
                device_id=(x, y, 1 - c), device_id_type=MESH).wait_recv()
        for cp in copies:
            cp.wait_send()

    return pl.pallas_call(
        body, name="grad_halves_to_full_" + tag,
        in_specs=_hbm_specs(n), out_specs=_hbm_specs(n),
        out_shape=[jax.ShapeDtypeStruct(f.shape, F32) for f in fulls],
        input_output_aliases={k: k for k in range(n)},
        scratch_shapes=[pltpu.SemaphoreType.DMA((n,)), pltpu.SemaphoreType.DMA((n,))],
    )(*fulls)


def _all_sum_small(v):
    rows, cols = v.shape
    n_dev = 8

    def body(v_ref, out_ref, buf, send_sems, recv_sems):
        x, y, c = _mesh_pos()
        me = 4 * x + 2 * y + c
        buf[me] = v_ref[...]
        peers = []
        for r in range(1, n_dev):
            px = 1 - x if r & 4 else x
            py = 1 - y if r & 2 else y
            pc = 1 - c if r & 1 else c
            peers.append((px, py, pc))
        copies = []
        for r, peer in enumerate(peers):
            cp = pltpu.make_async_remote_copy(
                src_ref=v_ref, dst_ref=buf.at[me], send_sem=send_sems.at[r], recv_sem=recv_sems.at[r],
                device_id=peer, device_id_type=MESH)
            cp.start()
            copies.append(cp)
        for r, (px, py, pc) in enumerate(peers):
            pltpu.make_async_remote_copy(
                src_ref=v_ref, dst_ref=buf.at[4 * px + 2 * py + pc], send_sem=send_sems.at[r], recv_sem=recv_sems.at[r],
                device_id=(px, py, pc), device_id_type=MESH).wait_recv()
        for cp in copies:
            cp.wait_send()
        acc = buf[0]
        for d in range(1, n_dev):
            acc = acc + buf[d]
        out_ref[...] = acc
        out_ref[3:4, :] = jnp.broadcast_to(jnp.sum(acc[3:4, :], axis=1, keepdims=True), (1, cols))

    vm = pl.BlockSpec(memory_space=pltpu.VMEM)
    return pl.pallas_call(
        body, name="all_sum_small", in_specs=[vm], out_specs=vm,
        out_shape=jax.ShapeDtypeStruct((rows, cols), F32),
        scratch_shapes=[pltpu.VMEM((n_dev, rows, cols), F32),
                        pltpu.SemaphoreType.DMA((n_dev - 1,)), pltpu.SemaphoreType.DMA((n_dev - 1,))],
    )(v)


def _adamw_math(w, g, m, v):
    m = ADAM_B1 * m + (1.0 - ADAM_B1) * g
    v = ADAM_B2 * v + (1.0 - ADAM_B2) * (g * g)
    m_hat = m / (1.0 - ADAM_B1 ** ADAM_STEP)
    v_hat = v / (1.0 - ADAM_B2 ** ADAM_STEP)
    delta = -ADAM_LR * (m_hat / (jnp.sqrt(v_hat) + ADAM_EPS) + ADAM_WD * w)
    return delta, m, v


def _adamw(w, g, m, v, name):
    rows, cols = w.shape
    tm = rows // 2 if (rows // 2) % 8 == 0 else rows
    return _rowwise(_adamw_math, [w, g, m, v], [], [(cols, F32)] * 3, [], tm=tm, name=name)


def _unshard_cols(gathered):
    n, r, c = gathered.shape
    return jnp.transpose(gathered, (1, 0, 2)).reshape(r, n * c)


def _shard_cols(full):
    r, nc = full.shape
    return jnp.transpose(full.reshape(r, N_CHIPS, nc // N_CHIPS), (1, 0, 2))


LATE = ["w_sb_up", "w_dil_up", "w_out", "w_ffn_in", "w_ffn_out"]


def _late_weights(slabs, d_model, d_ff):
    g = dict(zip(LATE, slabs))
    return (_unshard_cols(g["w_sb_up"]), _unshard_cols(g["w_dil_up"]), g["w_out"].reshape(d_model, d_model),
            _unshard_cols(g["w_ffn_in"]), g["w_ffn_out"].reshape(d_ff, d_model))


def _chip_major(grads, d_model, d_ff):
    row_sharded = {"w_out": d_model // N_CHIPS, "w_ffn_out": d_ff // N_CHIPS}
    return [g.reshape(N_CHIPS, row_sharded[k], g.shape[1]) if k in row_sharded else _shard_cols(g)
            for k, g in grads.items()]


def _pair_reduce(grads, d_model, d_ff):
    full = _chip_major(grads, d_model, d_ff)
    others = _pair_exchange(full, next(iter(grads)))
    return [_pair_sum(g, o, "grad_pair_sum_" + k) for g, o, k in zip(full, others, grads)]


def _chip_reduce(pair, landed, names):
    halves = [_chip_sum(p[0], l, "grad_chip_sum_" + k) for p, l, k in zip(pair, landed, names)]
    return dict(zip(names, _halves_to_full(halves, names[0])))


def _fwd_bwd(x, loss_target, g_mix, g_ffn, g_fin, wf_in, late_slabs):
    b_sz, s_len, d_model = x.shape
    t = b_sz * s_len
    d_ff = late_slabs[-1].shape[1] * N_CHIPS
    x2d = x.reshape(t, d_model)
    tgt2d = loss_target.reshape(t, d_model)
    wf_qkv, wf_gate = wf_in[:, :QKV_WIDTH], wf_in[:, QKV_WIDTH:]

    (u,) = _rowwise(lambda xv, g: (_rms_stats(xv)[0] * g,), [x2d], [g_mix], [(d_model, BF16)], [], tm=512, name="norm_mix")
    qkv = _mm(u, wf_qkv, tm=1024, tn=768, tk=d_model, name="proj_qkv")
    gates = _mm(u, wf_gate, out_dtype=BF16, tm=1024, tn=512, tk=d_model, name="proj_gates")
    qkv3 = qkv.reshape(b_sz, s_len, QKV_WIDTH)
    o_sb, late_slabs = _sb_fwd(qkv3, b_sz, s_len, late_slabs)
    wf_sb_up, wf_dil_up, wf_out, wf_ffn_in, wf_ffn_out = _late_weights(late_slabs, d_model, d_ff)
    o_dl, lse = _dil_fwd(qkv3, b_sz, s_len)
    o_sb2, o_dl2 = o_sb.reshape(t, SB_WIDTH), o_dl.reshape(t, DIL_OUT_WIDTH)
    y_sb = _mm(o_sb2, wf_sb_up, out_dtype=BF16, tm=1024, tn=1024, tk=SB_WIDTH, name="sb_up")
    y_dl = _mm(o_dl2, wf_dil_up, out_dtype=BF16, tm=1024, tn=1024, tk=DIL_OUT_WIDTH, name="dil_up")

    def merge_fn(gt, ys, yd):
        return (_sigmoid(gt[:, :d_model]) * ys + _sigmoid(gt[:, d_model:]) * yd,)

    (merged,) = _rowwise(merge_fn, [gates, y_sb, y_dl], [], [(d_model, BF16)], [], tm=512, name="merge")
    x1 = _mm(merged, wf_out, add=x2d, tm=512, tn=1024, tk=d_model, name="mix_out")
    (u2,) = _rowwise(lambda xv, g: (_rms_stats(xv)[0] * g,), [x1], [g_ffn], [(d_model, BF16)], [], tm=512, name="norm_ffn")
    h = _mm(u2, wf_ffn_in, out_dtype=BF16, tm=1024, tn=512, tk=d_model, name="ffn_in")

    def act_fn(hv):
        gate = hv[:, :d_ff]
        return (gate * _sigmoid(gate) * hv[:, d_ff:],)

    (act,) = _rowwise(act_fn, [h], [], [(d_ff, BF16)], [], tm=256, name="ffn_act")
    def head_fn(xv, tg, g):
        xhat, r = _rms_stats(xv)
        err = xhat * g - tg
        dy = err * (1.0 / d_model)
        dx, dg_rows = _rms_bwd(dy, xhat, r, g)
        loss_lanes = (0.5 / d_model) * jnp.sum(err * err, axis=0, keepdims=True)
        return dx, jnp.sum(dg_rows, axis=0, keepdims=True), loss_lanes

    dx2, dg_fin, loss_lanes = _mm(act, wf_ffn_out, add=x1, tm=512, tn=1024, tk=d_ff, name="ffn_out",
                                  epilogue=(head_fn, [tgt2d], [g_fin], [(d_model, F32)], [(1, d_model), (1, d_model)]))

    dact = _mm(dx2, wf_ffn_out, tb=True, out_dtype=BF16, tm=512, tn=d_ff // 2, tk=d_model, name="ffn_out_dx")
    gw_ffn_out = _mm(act, dx2, ta=True, tm=d_ff // 2, tn=1024, tk=512, name="ffn_out_dw")

    def dact_fn(hv, da):
        gate, up = hv[:, :d_ff], hv[:, d_ff:]
        sg = _sigmoid(gate)
        dgate = da * up * (sg * (1.0 + gate * (1.0 - sg)))
        return (jnp.concatenate([dgate, da * (gate * sg)], axis=1),)

    (dh,) = _rowwise(dact_fn, [h, dact], [], [(2 * d_ff, BF16)], [], tm=256, name="ffn_act_bwd")
    def norm_bwd_fn(du_, dres, xv, g):
        xhat, r = _rms_stats(xv)
        dx, dg_rows = _rms_bwd(du_, xhat, r, g)
        return dres + dx, jnp.sum(dg_rows, axis=0, keepdims=True)

    dx1, dg_ffn = _mm(dh, wf_ffn_in, tb=True, tm=512, tn=1024, tk=512, name="ffn_in_dx",
                      epilogue=(norm_bwd_fn, [dx2, x1], [g_ffn], [(d_model, F32)], [(1, d_model)]))
    gw_ffn_in = _mm(u2, dh, ta=True, tm=1024, tn=512, tk=1024, name="ffn_in_dw")

    dmerged = _mm(dx1, wf_out, tb=True, out_dtype=BF16, tm=512, tn=1024, tk=d_model, name="mix_out_dx")
    gw_out = _mm(merged, dx1, ta=True, tm=1024, tn=512, tk=512, name="mix_out_dw")

    def merge_bwd_fn(gt, ys, yd, dm):
        s_sb, s_dl = _sigmoid(gt[:, :d_model]), _sigmoid(gt[:, d_model:])
        dgates = jnp.concatenate([dm * ys * s_sb * (1.0 - s_sb), dm * yd * s_dl * (1.0 - s_dl)], axis=1)
        return dgates, dm * s_sb, dm * s_dl

    dgates, dy_sb, dy_dl = _rowwise(merge_bwd_fn, [gates, y_sb, y_dl, dmerged], [],
                                    [(2 * d_model, BF16), (d_model, BF16), (d_model, BF16)], [], tm=256, name="merge_bwd")
    do_sb = _mm(dy_sb, wf_sb_up, tb=True, out_dtype=BF16, tm=1024, tn=SB_WIDTH, tk=d_model, name="sb_up_dx")
    gw_sb_up = _mm(o_sb2, dy_sb, ta=True, tm=SB_WIDTH, tn=1024, tk=512, name="sb_up_dw")
    do_dl = _mm(dy_dl, wf_dil_up, tb=True, tm=1024, tn=DIL_OUT_WIDTH, tk=d_model, name="dil_up_dx")
    gw_dil_up = _mm(o_dl2, dy_dl, ta=True, tm=DIL_OUT_WIDTH, tn=1024, tk=512, name="dil_up_dw")
    late_grads = {"w_sb_up": gw_sb_up, "w_dil_up": gw_dil_up, "w_out": gw_out, "w_ffn_in": gw_ffn_in, "w_ffn_out": gw_ffn_out}
    pair = _pair_reduce(late_grads, d_model, d_ff)
    (dq_sb, dk_sb, dv_sb), landed = _sb_bwd(qkv3, o_sb, do_sb.reshape(b_sz, s_len, SB_WIDTH), b_sz, s_len,
                                           [p[1] for p in pair])
    dq_dl, dk_dl, dv_dl = _dil_bwd(qkv3, o_dl, lse, do_dl.reshape(b_sz, s_len, DIL_OUT_WIDTH), b_sz, s_len)
    dproj = jnp.concatenate(
        [a.reshape(t, -1) for a in (dq_sb, dk_sb, dv_sb)]
        + [a.reshape(t, -1) for a in (dq_dl, dk_dl, dv_dl)] + [dgates], axis=1)
    gw_in = _mm(u, dproj, ta=True, tm=512, tn=wf_in.shape[1] // 2, tk=512, name="proj_dw")
    pair_in = _pair_reduce({"w_in": gw_in}, d_model, d_ff)
    (dx, dg_mix), landed_in = _mm(dproj, wf_in, tb=True, tm=512, tn=1024, tk=wf_in.shape[1] // 2, name="proj_dx",
                                  carried=[p[1] for p in pair_in],
                                  epilogue=(norm_bwd_fn, [dx1, x2d], [g_mix], [(d_model, F32)], [(1, d_model)]))

    grads = _chip_reduce(pair, landed, LATE)
    grads.update(_chip_reduce(pair_in, landed_in, ["w_in"]))
    return dx, grads, dg_mix, dg_ffn, dg_fin, loss_lanes


def kernel(x, norm_mix_g, w_in, w_sb_up, w_dil_up, w_out, norm_ffn_g, w_ffn_in, w_ffn_out, norm_final_g, loss_target, m_norm_mix_g, m_w_in, m_w_sb_up, m_w_dil_up, m_w_out, m_norm_ffn_g, m_w_ffn_in, m_w_ffn_out, m_norm_final_g, v_norm_mix_g, v_w_in, v_w_sb_up, v_w_dil_up, v_w_out, v_norm_ffn_g, v_w_ffn_in, v_w_ffn_out, v_norm_final_g):
    b_sz, s_len, d_model = x.shape
    d_ff = w_ffn_out.shape[1] * N_CHIPS
    g_mix, g_ffn, g_fin = norm_mix_g, norm_ffn_g, norm_final_g.reshape(1, d_model)

    names = ["w_in", "w_sb_up", "w_dil_up", "w_out", "w_ffn_in", "w_ffn_out"]
    shards = {"w_in": w_in[0], "w_sb_up": w_sb_up[0], "w_dil_up": w_dil_up[0], "w_out": w_out[0],
              "w_ffn_in": w_ffn_in[0], "w_ffn_out": w_ffn_out[0]}
    (slab_in,) = _gather_weights([_cast_to_slab(shards["w_in"], "cast_w_in")])
    late_slabs = [_cast_to_slab(shards[k], "cast_" + k) for k in LATE]

    dx, grads, dg_mix, dg_ffn, dg_fin, loss_lanes = _fwd_bwd(
        x, loss_target, g_mix, g_ffn, g_fin, _unshard_cols(slab_in), late_slabs)

    small = jnp.concatenate([dg_mix, dg_ffn, dg_fin, loss_lanes, jnp.zeros((4, d_model), F32)], axis=0)
    small = _all_sum_small(small)
    loss = small[3, 0]
    gains = jnp.concatenate([g_mix, g_ffn, g_fin, jnp.zeros((5, d_model), F32)], axis=0)
    gains_m = jnp.concatenate([m_norm_mix_g, m_norm_ffn_g, m_norm_final_g.reshape(1, d_model), jnp.zeros((5, d_model), F32)], axis=0)
    gains_v = jnp.concatenate([v_norm_mix_g, v_norm_ffn_g, v_norm_final_g.reshape(1, d_model), jnp.ones((5, d_model), F32)], axis=0)
    gd, gm, gv = _rowwise(_adamw_math, [gains, small, gains_m, gains_v], [], [(d_model, F32)] * 3, [], tm=8, name="adamw_gains")

    moments = {"w_in": (m_w_in, v_w_in), "w_sb_up": (m_w_sb_up, v_w_sb_up), "w_dil_up": (m_w_dil_up, v_w_dil_up),
               "w_out": (m_w_out, v_w_out), "w_ffn_in": (m_w_ffn_in, v_w_ffn_in), "w_ffn_out": (m_w_ffn_out, v_w_ffn_out)}
    upd = {k: _adamw(shards[k], grads[k], moments[k][0][0], moments[k][1][0], "adamw_" + k) for k in names}

    def w_out_of(i):
        return [upd[k][i][None] for k in names]

    def ordered(mix, ws, ffn_g, fin):
        return [mix, ws[0], ws[1], ws[2], ws[3], ffn_g, ws[4], ws[5], fin]

    grad_ws = [grads[k][None] for k in names]
    outs = [loss, dx.reshape(b_sz, s_len, d_model)]
    outs += ordered(small[0:1], grad_ws, small[1:2], small[2])
    outs += ordered(gd[0:1], w_out_of(0), gd[1:2], gd[2])
    outs += ordered(gm[0:1], w_out_of(1), gm[1:2], gm[2])
    outs += ordered(gv[0:1], w_out_of(2), gv[1:2], gv[2])
    return tuple(outs)
```

```python
import functools
import math

import jax
import jax.numpy as jnp
from jax import lax
from jax.experimental import pallas as pl
from jax.experimental.pallas import tpu as pltpu

F32 = jnp.float32
BF16 = jnp.bfloat16
MESH = pl.DeviceIdType.MESH

HEAD_DIM = 64
SB_HEADS = 8
DIL_PAIRS = ((128, 1), (512, 4), (2048, 16))
DIL_HEADS_PER_GROUP = 4
DIL_HEADS = DIL_HEADS_PER_GROUP * len(DIL_PAIRS)
SB_WIDTH = SB_HEADS * HEAD_DIM
DIL_WIDTH = DIL_HEADS * HEAD_DIM
DIL_OUT_WIDTH = DIL_HEADS_PER_GROUP * HEAD_DIM
QKV_WIDTH = 3 * SB_WIDTH + 3 * DIL_WIDTH
RMS_EPS = 1e-6
ALIBI_MAX_BIAS = 8.0
ADAM_LR = 0.001
ADAM_B1 = 0.9
ADAM_B2 = 0.999
ADAM_EPS = 1e-08
ADAM_WD = 0.01
ADAM_STEP = 10

LANES = 128
BLK = 128
NEG = -1e30
SB_FWD_CHAINS = 4
SB_BWD_CHAINS = 4
N_CHIPS = 4
VMEM_CAP = 56 * 1024 * 1024


def _vmem_limit(tile_bytes):
    return int(min(VMEM_CAP, max(32 * 1024 * 1024, 3 * tile_bytes + 8 * 1024 * 1024)))


def _nbytes(shape, dtype):
    return math.prod(shape) * jnp.dtype(dtype).itemsize


def _dot(a, b):
    return jnp.dot(a, b, preferred_element_type=F32)


def _dot_nt(a, b):
    return lax.dot_general(a, b, (((1,), (1,)), ((), ())), preferred_element_type=F32)


def _dot_tn(a, b):
    return lax.dot_general(a, b, (((0,), (0,)), ((), ())), preferred_element_type=F32)


def _split2(x):
    hi = x.astype(BF16)
    lo = (x - hi.astype(F32)).astype(BF16)
    return hi, lo


def _sigmoid(x):
    return 1.0 / (1.0 + jnp.exp(-x))


def _mm(a, b, *, ta=False, tb=False, add=None, out_dtype=F32, tm, tn, tk, name, carried=(), epilogue=None):
    n_car = len(carried)
    if ta:
        kdim, m = a.shape
    else:
        m, kdim = a.shape
    if tb:
        n, k2 = b.shape
    else:
        k2, n = b.shape
    assert kdim == k2 and m % tm == 0 and n % tn == 0 and kdim % tk == 0, (name, a.shape, b.shape)
    nk = kdim // tk
    grid = (m // tm, n // tn, nk)
    a_mode = dict(pipeline_mode=pl.Buffered(1)) if grid[0] == 1 and nk == 1 else {}
    b_mode = dict(pipeline_mode=pl.Buffered(1)) if grid[1] == 1 and nk == 1 else {}
    a_spec = (pl.BlockSpec((tk, tm), lambda i, j, k: (k, i), **a_mode) if ta
              else pl.BlockSpec((tm, tk), lambda i, j, k: (i, k), **a_mode))
    b_spec = (pl.BlockSpec((tn, tk), lambda i, j, k: (j, k), **b_mode) if tb
              else pl.BlockSpec((tk, tn), lambda i, j, k: (k, j), **b_mode))
    o_spec = pl.BlockSpec((tm, tn), lambda i, j, k: (i, j))
    dims = ((((0,) if ta else (1,)), ((1,) if tb else (0,))), ((), ()))
    has_add = add is not None
    if epilogue is None:
        ep_fn, ep_rows, ep_params, ep_outs, ep_accs = None, [], [], [], []
        out_sds, out_specs = [jax.ShapeDtypeStruct((m, n), out_dtype)], [o_spec]
    else:
        ep_fn, ep_rows, ep_params, ep_outs, ep_accs = epilogue
        assert grid[1] == 1, name
        out_sds = [jax.ShapeDtypeStruct((m, w), d) for w, d in ep_outs] + [jax.ShapeDtypeStruct(sh, F32) for sh in ep_accs]
        out_specs = [pl.BlockSpec((tm, w), lambda i, j, k: (i, 0)) for w, _ in ep_outs]
        out_specs += [pl.BlockSpec(sh, lambda i, j, k: (0, 0)) for sh in ep_accs]
    n_main = len(out_sds)
    use_scratch = nk > 1 and (ep_fn is not None or jnp.dtype(out_dtype) != jnp.dtype(F32))
    n_in = 2 + has_add + len(ep_rows) + len(ep_params)

    def finish(total, refs, pid):
        outs = refs[n_in + n_car:n_in + n_car + n_main]
        if ep_fn is None:
            outs[0][...] = total.astype(out_dtype)
            return
        first = 2 + has_add
        rows = [r[...].astype(F32) for r in refs[first:first + len(ep_rows)]]
        params = [p[...] for p in refs[first + len(ep_rows):n_in]]
        res = ep_fn(total, *rows, *params)
        for o_ref, v in zip(outs[:len(ep_outs)], res):
            o_ref[...] = v.astype(o_ref.dtype)
        acc_refs = outs[len(ep_outs):]
        if acc_refs:
            @pl.when(pid[0] == 0)
            def _():
                for r in acc_refs:
                    r[...] = jnp.zeros(r.shape, F32)

            for r, v in zip(acc_refs, res[len(ep_outs):]):
                r[...] += v

    def compute(refs, pid):
        a_ref, b_ref = refs[0], refs[1]
        add_ref = refs[2] if has_add else None
        prod = lax.dot_general(a_ref[...].astype(BF16), b_ref[...].astype(BF16), dims, preferred_element_type=F32)
        if nk == 1:
            finish(prod + add_ref[...] if has_add else prod, refs, pid)
            return
        acc_ref = refs[n_in + 2 * n_car + n_main] if use_scratch else refs[n_in + n_car]
        k = pid[2]

        @pl.when(k == 0)
        def _():
            acc_ref[...] = prod + add_ref[...] if has_add else prod

        @pl.when(k > 0)
        def _():
            acc_ref[...] += prod

        if use_scratch:
            @pl.when(k == nk - 1)
            def _():
                finish(acc_ref[...], refs, pid)

    def body(*refs):
        pid = (pl.program_id(0), pl.program_id(1), pl.program_id(2))
        if not n_car:
            compute(refs, pid)
            return
        lands = refs[n_in + n_car + n_main:n_in + 2 * n_car + n_main]
        copies = _chip_copies(refs[n_in:n_in + n_car], lands, *refs[-2:])
        step = (pid[0] * grid[1] + pid[1]) * nk + pid[2]

        @pl.when(step == 0)
        def _():
            for cp in copies:
                cp.start()

        compute(refs, pid)

        @pl.when(step == grid[0] * grid[1] * nk - 1)
        def _():
            for cp in copies:
                cp.wait()

    tile_bytes = (_nbytes((tm, tk), a.dtype) + _nbytes((tk, tn), b.dtype) + 2 * _nbytes((tm, tn), F32)
                  + (_nbytes((tm, tn), F32) if has_add else 0)
                  + sum(_nbytes((tm, r.shape[1]), r.dtype) for r in ep_rows) + sum(_nbytes((tm, w), d) for w, d in ep_outs))
    in_specs = [a_spec, b_spec] + ([o_spec] if has_add else [])
    in_specs += [pl.BlockSpec((tm, r.shape[1]), lambda i, j, k: (i, 0)) for r in ep_rows]
    in_specs += [pl.BlockSpec(p.shape, lambda i, j, k: (0, 0)) for p in ep_params]
    args = (a, b) + ((add,) if has_add else ()) + tuple(ep_rows) + tuple(ep_params)
    scratch = [pltpu.VMEM((tm, tn), F32)] if use_scratch else []
    serial = bool(n_car or ep_accs)
    res = pl.pallas_call(
        body, name=name, grid=grid,
        in_specs=in_specs + _hbm_specs(n_car), out_specs=out_specs + _hbm_specs(n_car),
        out_shape=out_sds + _chip_landing(carried), scratch_shapes=scratch + (_chip_sems(n_car) if n_car else []),
        compiler_params=pltpu.CompilerParams(
            dimension_semantics=("arbitrary",) * 3 if serial else ("parallel", "parallel", "arbitrary"),
            vmem_limit_bytes=_vmem_limit(tile_bytes)),
    )(*args, *carried)
    main = res[0] if ep_fn is None else list(res[:n_main])
    return (main, res[n_main:]) if n_car else main


def _rowwise(fn, rows, params, outs, accs, *, tm, name):
    t = rows[0].shape[0]
    assert t % tm == 0, (name, t, tm)
    n_r, n_p, n_o = len(rows), len(params), len(outs)

    def body(*refs):
        vals = [r[...].astype(F32) for r in refs[:n_r]] + [p[...] for p in refs[n_r:n_r + n_p]]
        res = fn(*vals)
        o_refs = refs[n_r + n_p:n_r + n_p + n_o]
        a_refs = refs[n_r + n_p + n_o:]
        for o_ref, v in zip(o_refs, res[:n_o]):
            o_ref[...] = v.astype(o_ref.dtype)
        if accs:
            @pl.when(pl.program_id(0) == 0)
            def _():
                for a_ref in a_refs:
                    a_ref[...] = jnp.zeros(a_ref.shape, F32)

            for a_ref, v in zip(a_refs, res[n_o:]):
                a_ref[...] += v

    in_specs = [pl.BlockSpec((tm, r.shape[1]), lambda i: (i, 0)) for r in rows]
    in_specs += [pl.BlockSpec(p.shape, lambda i: (0, 0)) for p in params]
    out_specs = [pl.BlockSpec((tm, w), lambda i: (i, 0)) for w, _ in outs]
    out_specs += [pl.BlockSpec(s, lambda i: (0, 0)) for s in accs]
    out_shape = [jax.ShapeDtypeStruct((t, w), d) for w, d in outs]
    out_shape += [jax.ShapeDtypeStruct(s, F32) for s in accs]
    tile_bytes = sum(_nbytes((tm, r.shape[1]), r.dtype) for r in rows) + sum(_nbytes((tm, w), F32) for w, _ in outs)
    res = pl.pallas_call(
        body, name=name, grid=(t // tm,), in_specs=in_specs, out_specs=out_specs, out_shape=out_shape,
        compiler_params=pltpu.CompilerParams(
            dimension_semantics=("arbitrary",) if accs else ("parallel",),
            vmem_limit_bytes=_vmem_limit(2 * tile_bytes)),
    )(*rows, *params)
    return res


def _rms_stats(x):
    r = lax.rsqrt(jnp.mean(x * x, axis=-1, keepdims=True) + RMS_EPS)
    return x * r, r


def _rms_bwd(dy, xhat, r, g):
    dxhat = dy * g
    dx = r * (dxhat - xhat * jnp.mean(dxhat * xhat, axis=-1, keepdims=True))
    return dx, dy * xhat


def _sb_consts():
    lane = lax.broadcasted_iota(jnp.int32, (BLK, LANES), 1)
    head0 = lane < HEAD_DIM
    row = lax.broadcasted_iota(jnp.int32, (2 * BLK, BLK), 0) % BLK
    col = lax.broadcasted_iota(jnp.int32, (2 * BLK, BLK), 1)
    causal = col < row
    jj = lax.broadcasted_iota(jnp.int32, (BLK, BLK), 0)
    ss = lax.broadcasted_iota(jnp.int32, (BLK, BLK), 1)
    suffix = jnp.where(jj > ss, 1.0, 0.0).astype(BF16)
    return head0, causal, suffix


def _stack_heads(x, head0):
    zero = jnp.zeros_like(x)
    return jnp.concatenate([jnp.where(head0, x, zero), jnp.where(head0, zero, x)], axis=0)


def _sb_logits(z, causal, masked):
    sp = jnp.log(1.0 + jnp.exp(-jnp.abs(z)))
    log_keep = -(jnp.maximum(z, 0.0) + sp)
    log_beta = jnp.minimum(z, 0.0) - sp
    if masked:
        log_keep = jnp.where(causal, log_keep, 0.0)
    return log_keep, log_beta


def _suffix_sums(x, suffix):
    hi, lo = _split2(x)
    after = _dot(hi, suffix) + _dot(lo, suffix)
    total = jnp.broadcast_to(after[:, 0:1] + x[:, 0:1], x.shape)
    return after, total


def _lane_blocks(x, n):
    return [x[:, p * LANES:(p + 1) * LANES] for p in range(n)]


def _sb_fwd(qkv, b_sz, s_len, slabs):
    n_slabs = len(slabs)
    nq = s_len // BLK
    n_pairs = SB_WIDTH // LANES
    ch = SB_FWD_CHAINS
    n_steps = n_pairs // ch
    scale = 1.0 / math.sqrt(HEAD_DIM)

    def body(q_ref, k_ref, v_ref, *rest):
        o_ref = rest[n_slabs]
        slab_refs = rest[n_slabs + 1:2 * n_slabs + 1]
        send_sems, recv_sems = rest[2 * n_slabs + 1:]
        step = pl.program_id(0) * n_steps + pl.program_id(1)
        head0, causal, suffix = _sb_consts()

        @pl.when(step == 0)
        def _():
            _gather_issue(slab_refs, send_sems, recv_sems)

        def q_block(i, _):
            qs = pl.multiple_of(i * BLK, BLK)
            q_all = (q_ref[pl.ds(qs, BLK), :] * scale).astype(BF16)
            q01 = [_stack_heads(q, head0) for q in _lane_blocks(q_all, ch)]

            def tile(j, state, masked):
                ks = pl.multiple_of(j * BLK, BLK)
                ks_ = _lane_blocks(k_ref[pl.ds(ks, BLK), :].astype(BF16), ch)
                vs_ = _lane_blocks(v_ref[pl.ds(ks, BLK), :].astype(BF16), ch)
                zs = [_dot_nt(q01[p], ks_[p]) for p in range(ch)]
                logits = [_sb_logits(z, causal, masked) for z in zs]
                sums = [_suffix_sums(lg[0], suffix) for lg in logits]
                out = []
                for p in range(ch):
                    carry, acc = state[2 * p], state[2 * p + 1]
                    after, total = sums[p]
                    a = jnp.exp(logits[p][1] + carry + after)
                    if masked:
                        a = jnp.where(causal, a, 0.0)
                    a_hi, a_lo = _split2(a)
                    a_cat = jnp.concatenate([a_hi[:BLK], a_hi[BLK:], a_lo[:BLK], a_lo[BLK:]], axis=1)
                    v01 = _stack_heads(vs_[p], head0)
                    out += [carry + total, acc + _dot(a_cat, jnp.concatenate([v01, v01], axis=0))]
                return tuple(out)

            state = (jnp.zeros((2 * BLK, BLK), F32), jnp.zeros((BLK, LANES), F32)) * ch
            state = tile(i, state, True)
            state = lax.fori_loop(0, i, lambda t, st: tile(i - 1 - t, st, False), state)
            o_ref[pl.ds(qs, BLK), :] = jnp.concatenate([state[2 * p + 1] for p in range(ch)], axis=1)
            return 0

        lax.fori_loop(0, nq, q_block, 0)

        @pl.when(step == b_sz * n_steps - 1)
        def _():
            _gather_complete(slab_refs, send_sems, recv_sems)

    blk = lambda off: pl.BlockSpec((None, s_len, ch * LANES), lambda b, p: (b, 0, off + p))
    res = pl.pallas_call(
        body, name="sb_fwd", grid=(b_sz, n_steps),
        in_specs=[blk(0), blk(n_steps), blk(2 * n_steps)] + _hbm_specs(n_slabs),
        out_specs=[blk(0)] + _hbm_specs(n_slabs),
        out_shape=[jax.ShapeDtypeStruct((b_sz, s_len, SB_WIDTH), F32)]
        + [jax.ShapeDtypeStruct(a.shape, a.dtype) for a in slabs],
        input_output_aliases={3 + k: 1 + k for k in range(n_slabs)},
        scratch_shapes=_gather_sems(n_slabs),
        compiler_params=pltpu.CompilerParams(dimension_semantics=("arbitrary", "arbitrary"),
                                             vmem_limit_bytes=VMEM_CAP),
    )(qkv, qkv, qkv, *slabs)
    return res[0], res[1:]


def _sb_bwd(qkv, o_sb, do_sb, b_sz, s_len, sums_bf16):
    n_sums = len(sums_bf16)
    nq = s_len // BLK
    n_pairs = SB_WIDTH // LANES
    ch = SB_BWD_CHAINS
    n_steps = n_pairs // ch
    scale = 1.0 / math.sqrt(HEAD_DIM)

    def body(q_ref, k_ref, v_ref, o_ref, do_ref, *rest):
        sum_refs = rest[:n_sums]
        dq_ref, dk_ref, dv_ref = rest[n_sums:n_sums + 3]
        land_refs = rest[n_sums + 3:2 * n_sums + 3]
        dk_acc, dv_acc, send_sems, recv_sems = rest[2 * n_sums + 3:]
        step = pl.program_id(0) * n_steps + pl.program_id(1)
        copies = _chip_copies(sum_refs, land_refs, send_sems, recv_sems)

        @pl.when(step == 0)
        def _():
            for cp in copies:
                cp.start()

        head0, causal, suffix = _sb_consts()
        lrow = lax.broadcasted_iota(jnp.int32, (LANES, LANES), 0)
        ones_h0 = jnp.where(lrow < HEAD_DIM, 1.0, 0.0).astype(BF16)
        ones_h1 = jnp.where(lrow >= HEAD_DIM, 1.0, 0.0).astype(BF16)
        dk_acc[...] = jnp.zeros(dk_acc.shape, F32)
        dv_acc[...] = jnp.zeros(dv_acc.shape, F32)

        def q_block(i, _):
            qs = pl.multiple_of(i * BLK, BLK)
            q_all = (q_ref[pl.ds(qs, BLK), :] * scale).astype(BF16)
            do_all = do_ref[pl.ds(qs, BLK), :].astype(BF16)
            dd_all = do_all.astype(F32) * o_ref[pl.ds(qs, BLK), :]
            q01 = [_stack_heads(q, head0) for q in _lane_blocks(q_all, ch)]
            do01 = [_stack_heads(d, head0) for d in _lane_blocks(do_all, ch)]
            tot = []
            for dd in _lane_blocks(dd_all, ch):
                dd_hi, dd_lo = _split2(dd)
                tot.append(jnp.concatenate([_dot(dd_hi, ones_h0) + _dot(dd_lo, ones_h0),
                                            _dot(dd_hi, ones_h1) + _dot(dd_lo, ones_h1)], axis=0))

            def tile(j, state, masked):
                ks = pl.multiple_of(j * BLK, BLK)
                ks_ = _lane_blocks(k_ref[pl.ds(ks, BLK), :].astype(BF16), ch)
                vs_ = _lane_blocks(v_ref[pl.ds(ks, BLK), :].astype(BF16), ch)
                zs = [_dot_nt(q01[p], ks_[p]) for p in range(ch)]
                das = [_dot_nt(do01[p], vs_[p]) for p in range(ch)]
                logits = [_sb_logits(z, causal, masked) for z in zs]
                sums = [_suffix_sums(lg[0], suffix) for lg in logits]
                a_s, e_s = [], []
                for p in range(ch):
                    a = jnp.exp(logits[p][1] + state[3 * p] + sums[p][0])
                    if masked:
                        a = jnp.where(causal, a, 0.0)
                    a_s.append(a)
                    e_s.append(a * das[p])
                e_sums = [_suffix_sums(e, suffix) for e in e_s]
                out, dks, dvs = [], [], []
                for p in range(ch):
                    carry, rcarry, dq = state[3 * p:3 * p + 3]
                    e = e_s[p]
                    before = tot[p] - (rcarry + e_sums[p][0] + e)
                    beta = jnp.exp(logits[p][1])
                    dz = e * (1.0 - beta) - beta * before
                    if masked:
                        dz = jnp.where(causal, dz, 0.0)
                    dz_b = dz.astype(BF16)
                    dks.append(_dot_tn(dz_b, q01[p]))
                    dvs.append(_dot_tn(a_s[p].astype(BF16), do01[p]))
                    out += [carry + sums[p][1], rcarry + e_sums[p][1], dq + _dot(dz_b, ks_[p])]
                dk_acc[pl.ds(ks, BLK), :] += jnp.concatenate(dks, axis=1)
                dv_acc[pl.ds(ks, BLK), :] += jnp.concatenate(dvs, axis=1)
                return tuple(out)

            state = (jnp.zeros((2 * BLK, BLK), F32),) * (3 * ch)
            state = tile(i, state, True)
            state = lax.fori_loop(0, i, lambda t, st: tile(i - 1 - t, st, False), state)
            dq = [jnp.where(head0, state[3 * p + 2][:BLK], state[3 * p + 2][BLK:]) for p in range(ch)]
            dq_ref[pl.ds(qs, BLK), :] = (jnp.concatenate(dq, axis=1) * scale).astype(dq_ref.dtype)
            return 0

        lax.fori_loop(0, nq, q_block, 0)
        dk_ref[...] = dk_acc[...].astype(dk_ref.dtype)
        dv_ref[...] = dv_acc[...].astype(dv_ref.dtype)

        @pl.when(step == b_sz * n_steps - 1)
        def _():
            for cp in copies:
                cp.wait()

    blk = lambda off: pl.BlockSpec((None, s_len, ch * LANES), lambda b, p: (b, 0, off + p))
    once = lambda off: pl.BlockSpec((None, s_len, ch * LANES), lambda b, p: (b, 0, off + p),
                                    pipeline_mode=pl.Buffered(1))
    out_sd = jax.ShapeDtypeStruct((b_sz, s_len, SB_WIDTH), BF16)
    res = pl.pallas_call(
        body, name="sb_bwd", grid=(b_sz, n_steps),
        in_specs=[once(0), once(n_steps), once(2 * n_steps), once(0), once(0)] + _hbm_specs(n_sums),
        out_specs=[blk(0), blk(0), blk(0)] + _hbm_specs(n_sums),
        out_shape=[out_sd, out_sd, out_sd] + _chip_landing(sums_bf16),
        scratch_shapes=[pltpu.VMEM((s_len, ch * LANES), F32), pltpu.VMEM((s_len, ch * LANES), F32)] + _chip_sems(n_sums),
        compiler_params=pltpu.CompilerParams(dimension_semantics=("arbitrary", "arbitrary"),
                                             vmem_limit_bytes=VMEM_CAP),
    )(qkv, qkv, qkv, o_sb, do_sb, *sums_bf16)
    return res[:3], res[3:]


def _dil_consts(group, pair_idx, dilation):
    lane = lax.broadcasted_iota(jnp.int32, (BLK, LANES), 1)
    head0 = lane < HEAD_DIM
    row = lax.broadcasted_iota(jnp.int32, (2 * BLK, BLK), 0)
    qa = row % BLK
    kb = lax.broadcasted_iota(jnp.int32, (2 * BLK, BLK), 1)
    head = (group * DIL_HEADS_PER_GROUP + 2 * pair_idx + row // BLK).astype(F32)
    slope = jnp.exp((-ALIBI_MAX_BIAS * math.log(2.0) / DIL_HEADS) * (head + 1.0))
    valid_cur = kb <= qa
    valid_prev = kb >= qa
    bias_cur = -slope * ((qa - kb) * dilation).astype(F32)
    bias_prev = -slope * ((BLK + qa - kb) * dilation).astype(F32)
    return head0, valid_cur, valid_prev, bias_cur, bias_prev


def _dil_units(s_len, dilation):
    nb = s_len // dilation // BLK
    return [(r, n) for r in range(dilation) for n in range(nb)]


def _dil_rows(n, r, dilation):
    if dilation == 1:
        return pl.ds(n * BLK, BLK)
    return pl.ds(n * BLK * dilation + r, BLK, stride=dilation)


def _dil_scores(q01, k, bias, valid):
    s = _dot_nt(q01, k) * (1.0 / math.sqrt(HEAD_DIM)) + bias
    return jnp.where(valid, s, NEG)


def _dil_fwd(qkv, b_sz, s_len):
    n_pairs = DIL_OUT_WIDTH // LANES
    q_off = 3 * SB_WIDTH // LANES
    per_kind = DIL_WIDTH // LANES

    def body(*refs):
        qkv_refs = refs[:9]
        o_ref, lse_ref, m_s, l_s = refs[9:]
        pair_idx = pl.program_id(1)
        m_s[...] = jnp.full(m_s.shape, NEG, F32)
        l_s[...] = jnp.zeros(l_s.shape, F32)
        o_ref[...] = jnp.zeros(o_ref.shape, F32)
        for g, (_, dilation) in enumerate(DIL_PAIRS):
            q_ref, k_ref, v_ref = qkv_refs[3 * g:3 * g + 3]
            head0, valid_cur, valid_prev, bias_cur, bias_prev = _dil_consts(g, pair_idx, dilation)
            for r, n in _dil_units(s_len, dilation):
                rows = _dil_rows(n, r, dilation)
                q01 = _stack_heads(q_ref[rows, :].astype(BF16), head0)
                k_c = k_ref[rows, :].astype(BF16)
                v_c = v_ref[rows, :].astype(BF16)
                scores = [_dil_scores(q01, k_c, bias_cur, valid_cur)]
                values = [_stack_heads(v_c, head0)]
                if n > 0:
                    prev = _dil_rows(n - 1, r, dilation)
                    scores.append(_dil_scores(q01, k_ref[prev, :].astype(BF16), bias_prev, valid_prev))
                    values.append(_stack_heads(v_ref[prev, :].astype(BF16), head0))
                m_blk = functools.reduce(jnp.maximum, [jnp.max(s, axis=-1, keepdims=True) for s in scores])
                m_old = jnp.concatenate([m_s.at[0][rows, :], m_s.at[1][rows, :]], axis=0)
                l_old = jnp.concatenate([l_s.at[0][rows, :], l_s.at[1][rows, :]], axis=0)
                m_new = jnp.maximum(m_old, m_blk)
                probs = [jnp.exp(s - m_new) for s in scores]
                l_blk = functools.reduce(jnp.add, [jnp.sum(p, axis=-1, keepdims=True) for p in probs])
                alpha = jnp.exp(m_old - m_new)
                l_new = alpha * l_old + l_blk
                alpha_tok = jnp.where(head0, alpha[:BLK], alpha[BLK:])
                p_cat = jnp.concatenate([h for p in probs for h in (p[:BLK].astype(BF16), p[BLK:].astype(BF16))], axis=1)
                o_ref[rows, :] = alpha_tok * o_ref[rows, :] + _dot(p_cat, jnp.concatenate(values, axis=0))
                m_s.at[0][rows, :] = m_new[:BLK]
                m_s.at[1][rows, :] = m_new[BLK:]
                l_s.at[0][rows, :] = l_new[:BLK]
                l_s.at[1][rows, :] = l_new[BLK:]
        lane = lax.broadcasted_iota(jnp.int32, (BLK, LANES), 1)
        for c in range(s_len // BLK):
            rows = pl.ds(c * BLK, BLK)
            l0, l1 = l_s.at[0][rows, :], l_s.at[1][rows, :]
            o_ref[rows, :] = o_ref[rows, :] / jnp.where(lane < HEAD_DIM, l0, l1)
            lse_ref.at[0][rows, :] = m_s.at[0][rows, :] + jnp.log(l0)
            lse_ref.at[1][rows, :] = m_s.at[1][rows, :] + jnp.log(l1)

    in_specs = []
    for g in range(len(DIL_PAIRS)):
        for kind in range(3):
            off = q_off + kind * per_kind + g * n_pairs
            in_specs.append(pl.BlockSpec((None, s_len, LANES), lambda b, p, off=off: (b, 0, off + p)))
    return pl.pallas_call(
        body, name="dil_fwd", grid=(b_sz, n_pairs),
        in_specs=in_specs,
        out_specs=[pl.BlockSpec((None, s_len, LANES), lambda b, p: (b, 0, p)),
                   pl.BlockSpec((None, None, 2, s_len, LANES), lambda b, p: (b, p, 0, 0, 0))],
        out_shape=[jax.ShapeDtypeStruct((b_sz, s_len, DIL_OUT_WIDTH), F32),
                   jax.ShapeDtypeStruct((b_sz, n_pairs, 2, s_len, LANES), F32)],
        scratch_shapes=[pltpu.VMEM((2, s_len, LANES), F32), pltpu.VMEM((2, s_len, LANES), F32)],
        compiler_params=pltpu.CompilerParams(dimension_semantics=("parallel", "parallel"),
                                             vmem_limit_bytes=VMEM_CAP),
    )(*([qkv] * 9))


def _dil_bwd(qkv, o_dl, lse, do_dl, b_sz, s_len):
    n_pairs = DIL_OUT_WIDTH // LANES
    n_groups = len(DIL_PAIRS)
    q_off = 3 * SB_WIDTH // LANES
    per_kind = DIL_WIDTH // LANES

    def body(q_ref, k_ref, v_ref, o_ref, lse_ref, do_ref, dq_ref, dk_ref, dv_ref, d_s, dq_s, dk_s, dv_s):
        pair_idx = pl.program_id(1)
        group = pl.program_id(2)
        lrow = lax.broadcasted_iota(jnp.int32, (LANES, LANES), 0)
        ones_h0 = jnp.where(lrow < HEAD_DIM, 1.0, 0.0).astype(BF16)
        ones_h1 = jnp.where(lrow >= HEAD_DIM, 1.0, 0.0).astype(BF16)
        for c in range(s_len // BLK):
            rows = pl.ds(c * BLK, BLK)
            dd_hi, dd_lo = _split2(do_ref[rows, :] * o_ref[rows, :])
            d_s.at[0][rows, :] = _dot(dd_hi, ones_h0) + _dot(dd_lo, ones_h0)
            d_s.at[1][rows, :] = _dot(dd_hi, ones_h1) + _dot(dd_lo, ones_h1)
        dk_s[...] = jnp.zeros(dk_s.shape, F32)
        dv_s[...] = jnp.zeros(dv_s.shape, F32)

        def one_group(g, dilation):
            head0, valid_cur, valid_prev, bias_cur, bias_prev = _dil_consts(g, pair_idx, dilation)
            for r, n in _dil_units(s_len, dilation):
                rows = _dil_rows(n, r, dilation)
                q01 = _stack_heads(q_ref[rows, :].astype(BF16), head0)
                do01 = _stack_heads(do_ref[rows, :].astype(BF16), head0)
                lse01 = jnp.concatenate([lse_ref.at[0][rows, :], lse_ref.at[1][rows, :]], axis=0)
                d01 = jnp.concatenate([d_s.at[0][rows, :], d_s.at[1][rows, :]], axis=0)
                dq = jnp.zeros((2 * BLK, LANES), F32)
                blocks = [(rows, bias_cur, valid_cur)]
                if n > 0:
                    blocks.append((_dil_rows(n - 1, r, dilation), bias_prev, valid_prev))
                for krows, bias, valid in blocks:
                    k = k_ref[krows, :].astype(BF16)
                    v = v_ref[krows, :].astype(BF16)
                    p = jnp.exp(_dil_scores(q01, k, bias, valid) - lse01)
                    ds = (p * (_dot_nt(do01, v) - d01) * (1.0 / math.sqrt(HEAD_DIM))).astype(BF16)
                    dq = dq + _dot(ds, k)
                    dk_s[krows, :] = dk_s[krows, :] + _dot_tn(ds, q01)
                    dv_s[krows, :] = dv_s[krows, :] + _dot_tn(p.astype(BF16), do01)
                dq_s[rows, :] = jnp.where(head0, dq[:BLK], dq[BLK:])

        for g, (_, dilation) in enumerate(DIL_PAIRS):
            pl.when(group == g)(functools.partial(one_group, g, dilation))
        dq_ref[...] = dq_s[...].astype(dq_ref.dtype)
        dk_ref[...] = dk_s[...].astype(dk_ref.dtype)
        dv_ref[...] = dv_s[...].astype(dv_ref.dtype)

    def qkv_spec(kind):
        return pl.BlockSpec((None, s_len, LANES),
                            lambda b, p, g: (b, 0, q_off + kind * per_kind + g * n_pairs + p))

    tok_spec = pl.BlockSpec((None, s_len, LANES), lambda b, p, g: (b, 0, p))
    out_spec = pl.BlockSpec((None, s_len, LANES), lambda b, p, g: (b, 0, g * n_pairs + p))
    out_sd = jax.ShapeDtypeStruct((b_sz, s_len, DIL_WIDTH), BF16)
    return pl.pallas_call(
        body, name="dil_bwd", grid=(b_sz, n_pairs, n_groups),
        in_specs=[qkv_spec(0), qkv_spec(1), qkv_spec(2), tok_spec,
                  pl.BlockSpec((None, None, 2, s_len, LANES), lambda b, p, g: (b, p, 0, 0, 0)), tok_spec],
        out_specs=[out_spec, out_spec, out_spec],
        out_shape=[out_sd, out_sd, out_sd],
        scratch_shapes=[pltpu.VMEM((2, s_len, LANES), F32)] + [pltpu.VMEM((s_len, LANES), F32)] * 3,
        compiler_params=pltpu.CompilerParams(dimension_semantics=("parallel", "parallel", "arbitrary"),
                                             vmem_limit_bytes=VMEM_CAP),
    )(qkv, qkv, qkv, o_dl, lse, do_dl)


def _mesh_pos():
    return lax.axis_index("x"), lax.axis_index("y"), lax.axis_index("c")


def _other_chips(x, y):
    return [(1 - x, y), (x, 1 - y), (1 - x, 1 - y)]


def _hbm_specs(n):
    return [pl.BlockSpec(memory_space=pl.ANY)] * n


def _cast_to_slab(w, name):
    rows, cols = w.shape
    mine = jnp.reshape(2 * lax.axis_index("x") + lax.axis_index("y"), (1,)).astype(jnp.int32)

    def body(idx_ref, w_ref, o_ref):
        o_ref[...] = w_ref[...].astype(BF16)

    return pl.pallas_call(
        body, name=name,
        grid_spec=pltpu.PrefetchScalarGridSpec(
            num_scalar_prefetch=1, grid=(1,),
            in_specs=[pl.BlockSpec((rows, cols), lambda i, idx: (0, 0))],
            out_specs=pl.BlockSpec((None, rows, cols), lambda i, idx: (idx[0], 0, 0))),
        out_shape=jax.ShapeDtypeStruct((N_CHIPS, rows, cols), BF16),
        compiler_params=pltpu.CompilerParams(vmem_limit_bytes=_vmem_limit(rows * cols * 6)),
    )(mine, w)


def _gather_issue(slabs, send_sems, recv_sems):
    x, y, c = _mesh_pos()
    for k, slab in enumerate(slabs):
        half = slab.shape[1] // 2
        rows = slab.at[2 * x + y, pl.ds(c * half, half), :]
        for r, (px, py) in enumerate(_other_chips(x, y)):
            pltpu.make_async_remote_copy(
                src_ref=rows, dst_ref=rows, send_sem=send_sems.at[6 * k + r], recv_sem=recv_sems.at[6 * k + r],
                device_id=(px, py, c), device_id_type=MESH).start()


def _gather_complete(slabs, send_sems, recv_sems):
    x, y, c = _mesh_pos()
    chips = _other_chips(x, y)

    def copy(k, sem, block, rows, to):
        ref = slabs[k].at[block, rows, :]
        return pltpu.make_async_remote_copy(
            src_ref=ref, dst_ref=ref, send_sem=send_sems.at[sem], recv_sem=recv_sems.at[sem],
            device_id=to, device_id_type=MESH)

    for k, slab in enumerate(slabs):
        half = slab.shape[1] // 2
        for r, (px, py) in enumerate(chips):
            copy(k, 6 * k + r, 2 * px + py, pl.ds(c * half, half), (px, py, c)).wait_recv()
            copy(k, 6 * k + 3 + r, 2 * px + py, pl.ds(c * half, half), (x, y, 1 - c)).start()
    for k, slab in enumerate(slabs):
        half = slab.shape[1] // 2
        for r, (px, py) in enumerate(chips):
            copy(k, 6 * k + 3 + r, 2 * px + py, pl.ds((1 - c) * half, half), (x, y, 1 - c)).wait_recv()
    for k, slab in enumerate(slabs):
        half = slab.shape[1] // 2
        for r, (px, py) in enumerate(chips):
            copy(k, 6 * k + r, 2 * x + y, pl.ds(c * half, half), (px, py, c)).wait_send()
            copy(k, 6 * k + 3 + r, 2 * px + py, pl.ds(c * half, half), (x, y, 1 - c)).wait_send()


def _gather_sems(n):
    return [pltpu.SemaphoreType.DMA((6 * n,)), pltpu.SemaphoreType.DMA((6 * n,))]


def _gather_weights(slabs):
    n = len(slabs)

    def body(*refs):
        outs = refs[n:2 * n]
        send_sems, recv_sems = refs[2 * n:]
        _gather_issue(outs, send_sems, recv_sems)
        _gather_complete(outs, send_sems, recv_sems)

    return pl.pallas_call(
        body, name="gather_weights",
        in_specs=_hbm_specs(n), out_specs=_hbm_specs(n),
        out_shape=[jax.ShapeDtypeStruct(s.shape, s.dtype) for s in slabs],
        input_output_aliases={k: k for k in range(n)},
        scratch_shapes=_gather_sems(n),
    )(*slabs)


def _pair_exchange(grads, tag):
    n = len(grads)

    def body(*refs):
        ins, outs = refs[:n], refs[n:2 * n]
        send_sems, recv_sems = refs[2 * n:]
        x, y, c = _mesh_pos()
        copies = []
        for k in range(n):
            half = grads[k].shape[1] // 2
            cp = pltpu.make_async_remote_copy(
                src_ref=ins[k].at[:, pl.ds((1 - c) * half, half), :], dst_ref=outs[k],
                send_sem=send_sems.at[k], recv_sem=recv_sems.at[k],
                device_id=(x, y, 1 - c), device_id_type=MESH)
            cp.start()
            copies.append(cp)
        for cp in copies:
            cp.wait()

    return pl.pallas_call(
        body, name="grad_pair_exchange_" + tag,
        in_specs=_hbm_specs(n), out_specs=_hbm_specs(n),
        out_shape=[jax.ShapeDtypeStruct((N_CHIPS, g.shape[1] // 2, g.shape[2]), F32) for g in grads],
        scratch_shapes=[pltpu.SemaphoreType.DMA((n,)), pltpu.SemaphoreType.DMA((n,))],
    )(*grads)


def _pair_sum(grad, other, name):
    _, rows, cols = grad.shape
    half = rows // 2
    core = jnp.reshape(lax.axis_index("c"), (1,)).astype(jnp.int32)

    def body(core_ref, g_ref, p_ref, s_ref, sb_ref):
        s = g_ref[...] + p_ref[...]
        s_ref[...] = s
        sb_ref[...] = s.astype(BF16)

    blk = pl.BlockSpec((None, half, cols), lambda p, core_ref: (p, 0, 0))
    return pl.pallas_call(
        body, name=name,
        grid_spec=pltpu.PrefetchScalarGridSpec(
            num_scalar_prefetch=1, grid=(N_CHIPS,),
            in_specs=[pl.BlockSpec((None, half, cols), lambda p, core_ref: (p, core_ref[0], 0)), blk],
            out_specs=[blk, blk]),
        out_shape=[jax.ShapeDtypeStruct((N_CHIPS, half, cols), F32),
                   jax.ShapeDtypeStruct((N_CHIPS, half, cols), BF16)],
        compiler_params=pltpu.CompilerParams(dimension_semantics=("parallel",),
                                             vmem_limit_bytes=_vmem_limit(4 * half * cols * 4)),
    )(core, grad, other)


def _chip_copies(sums_bf16, lands, send_sems, recv_sems):
    x, y, c = _mesh_pos()
    return [pltpu.make_async_remote_copy(
        src_ref=sums_bf16[k].at[2 * px + py], dst_ref=lands[k].at[r],
        send_sem=send_sems.at[3 * k + r], recv_sem=recv_sems.at[3 * k + r],
        device_id=(px, py, c), device_id_type=MESH)
        for k in range(len(sums_bf16)) for r, (px, py) in enumerate(_other_chips(x, y))]


def _chip_sems(n):
    return [pltpu.SemaphoreType.DMA((3 * n,)), pltpu.SemaphoreType.DMA((3 * n,))]


def _chip_landing(sums_bf16):
    return [jax.ShapeDtypeStruct((N_CHIPS - 1,) + s.shape[1:], BF16) for s in sums_bf16]


def _chip_sum(sums_f32, landed, name):
    _, rows, cols = sums_f32.shape
    x, y, c = _mesh_pos()
    idx = jnp.stack([2 * x + y, c]).astype(jnp.int32)

    def body(idx_ref, o_ref, l_ref, out_ref):
        out_ref[...] = ((o_ref[...] + l_ref[0].astype(F32)) + l_ref[1].astype(F32)) + l_ref[2].astype(F32)

    return pl.pallas_call(
        body, name=name,
        grid_spec=pltpu.PrefetchScalarGridSpec(
            num_scalar_prefetch=1, grid=(1,),
            in_specs=[pl.BlockSpec((None, rows, cols), lambda i, idx: (idx[0], 0, 0)),
                      pl.BlockSpec((N_CHIPS - 1, rows, cols), lambda i, idx: (0, 0, 0))],
            out_specs=pl.BlockSpec((rows, cols), lambda i, idx: (idx[1], 0))),
        out_shape=jax.ShapeDtypeStruct((2 * rows, cols), F32),
        compiler_params=pltpu.CompilerParams(vmem_limit_bytes=_vmem_limit(3 * rows * cols * 4)),
    )(idx, sums_f32, landed)


def _halves_to_full(fulls, tag):
    n = len(fulls)

    def body(*refs):
        outs = refs[n:2 * n]
        send_sems, recv_sems = refs[2 * n:]
        x, y, c = _mesh_pos()
        copies = []
        for k in range(n):
            half = fulls[k].shape[0] // 2
            rows = outs[k].at[pl.ds(c * half, half), :]
            cp = pltpu.make_async_remote_copy(
                src_ref=rows, dst_ref=rows, send_sem=send_sems.at[k], recv_sem=recv_sems.at[k],
                device_id=(x, y, 1 - c), device_id_type=MESH)
            cp.start()
            copies.append(cp)
        for k in range(n):
            half = fulls[k].shape[0] // 2
            theirs = outs[k].at[pl.ds((1 - c) * half, half), :]
            pltpu.make_async_remote_copy(
                src_ref=theirs, dst_ref=theirs, send_sem=send_sems.at[k], recv_sem=recv_sems.at[k],
                device_id=(x, y, 1 - c), device_id_type=MESH).wait_recv()
        for cp in copies:
            cp.wait_send()

    return pl.pallas_call(
        body, name="grad_halves_to_full_" + tag,
        in_specs=_hbm_specs(n), out_specs=_hbm_specs(n),
        out_shape=[jax.ShapeDtypeStruct(f.shape, F32) for f in fulls],
        input_output_aliases={k: k for k in range(n)},
        scratch_shapes=[pltpu.SemaphoreType.DMA((n,)), pltpu.SemaphoreType.DMA((n,))],
    )(*fulls)


def _all_sum_small(v):
    rows, cols = v.shape
    n_dev = 8

    def body(v_ref, out_ref, buf, send_sems, recv_sems):
        x, y, c = _mesh_pos()
        me = 4 * x + 2 * y + c
        buf[me] = v_ref[...]
        peers = []
        for r in range(1, n_dev):
            px = 1 - x if r & 4 else x
            py = 1 - y if r & 2 else y
            pc = 1 - c if r & 1 else c
            peers.append((px, py, pc))
        copies = []
        for r, peer in enumerate(peers):
            cp = pltpu.make_async_remote_copy(
                src_ref=v_ref, dst_ref=buf.at[me], send_sem=send_sems.at[r], recv_sem=recv_sems.at[r],
                device_id=peer, device_id_type=MESH)
            cp.start()
            copies.append(cp)
        for r, (px, py, pc) in enumerate(peers):
            pltpu.make_async_remote_copy(
                src_ref=v_ref, dst_ref=buf.at[4 * px + 2 * py + pc], send_sem=send_sems.at[r], recv_sem=recv_sems.at[r],
                device_id=(px, py, pc), device_id_type=MESH).wait_recv()
        for cp in copies:
            cp.wait_send()
        acc = buf[0]
        for d in range(1, n_dev):
            acc = acc + buf[d]
        out_ref[...] = acc
        out_ref[3:4, :] = jnp.broadcast_to(jnp.sum(acc[3:4, :], axis=1, keepdims=True), (1, cols))

    vm = pl.BlockSpec(memory_space=pltpu.VMEM)
    return pl.pallas_call(
        body, name="all_sum_small", in_specs=[vm], out_specs=vm,
        out_shape=jax.ShapeDtypeStruct((rows, cols), F32),
        scratch_shapes=[pltpu.VMEM((n_dev, rows, cols), F32),
                        pltpu.SemaphoreType.DMA((n_dev - 1,)), pltpu.SemaphoreType.DMA((n_dev - 1,))],
    )(v)


def _adamw_math(w, g, m, v):
    m = ADAM_B1 * m + (1.0 - ADAM_B1) * g
    v = ADAM_B2 * v + (1.0 - ADAM_B2) * (g * g)
    m_hat = m / (1.0 - ADAM_B1 ** ADAM_STEP)
    v_hat = v / (1.0 - ADAM_B2 ** ADAM_STEP)
    delta = -ADAM_LR * (m_hat / (jnp.sqrt(v_hat) + ADAM_EPS) + ADAM_WD * w)
    return delta, m, v


def _adamw(w, g, m, v, name):
    rows, cols = w.shape
    tm = rows // 2 if (rows // 2) % 8 == 0 else rows
    return _rowwise(_adamw_math, [w, g, m, v], [], [(cols, F32)] * 3, [], tm=tm, name=name)


def _unshard_cols(gathered):
    n, r, c = gathered.shape
    return jnp.transpose(gathered, (1, 0, 2)).reshape(r, n * c)


def _shard_cols(full):
    r, nc = full.shape
    return jnp.transpose(full.reshape(r, N_CHIPS, nc // N_CHIPS), (1, 0, 2))


LATE = ["w_sb_up", "w_dil_up", "w_out", "w_ffn_in", "w_ffn_out"]


def _late_weights(slabs, d_model, d_ff):
    g = dict(zip(LATE, slabs))
    return (_unshard_cols(g["w_sb_up"]), _unshard_cols(g["w_dil_up"]), g["w_out"].reshape(d_model, d_model),
            _unshard_cols(g["w_ffn_in"]), g["w_ffn_out"].reshape(d_ff, d_model))


def _chip_major(grads, d_model, d_ff):
    row_sharded = {"w_out": d_model // N_CHIPS, "w_ffn_out": d_ff // N_CHIPS}
    return [g.reshape(N_CHIPS, row_sharded[k], g.shape[1]) if k in row_sharded else _shard_cols(g)
            for k, g in grads.items()]


def _pair_reduce(grads, d_model, d_ff):
    full = _chip_major(grads, d_model, d_ff)
    others = _pair_exchange(full, next(iter(grads)))
    return [_pair_sum(g, o, "grad_pair_sum_" + k) for g, o, k in zip(full, others, grads)]


def _chip_reduce(pair, landed, names):
    halves = [_chip_sum(p[0], l, "grad_chip_sum_" + k) for p, l, k in zip(pair, landed, names)]
    return dict(zip(names, _halves_to_full(halves, names[0])))


def _fwd_bwd(x, loss_target, g_mix, g_ffn, g_fin, wf_in, late_slabs):
    b_sz, s_len, d_model = x.shape
    t = b_sz * s_len
    d_ff = late_slabs[-1].shape[1] * N_CHIPS
    x2d = x.reshape(t, d_model)
    tgt2d = loss_target.reshape(t, d_model)
    wf_qkv, wf_gate = wf_in[:, :QKV_WIDTH], wf_in[:, QKV_WIDTH:]

    (u,) = _rowwise(lambda xv, g: (_rms_stats(xv)[0] * g,), [x2d], [g_mix], [(d_model, BF16)], [], tm=512, name="norm_mix")
    qkv = _mm(u, wf_qkv, tm=1024, tn=768, tk=d_model, name="proj_qkv")
    gates = _mm(u, wf_gate, out_dtype=BF16, tm=1024, tn=512, tk=d_model, name="proj_gates")
    qkv3 = qkv.reshape(b_sz, s_len, QKV_WIDTH)
    o_sb, late_slabs = _sb_fwd(qkv3, b_sz, s_len, late_slabs)
    wf_sb_up, wf_dil_up, wf_out, wf_ffn_in, wf_ffn_out = _late_weights(late_slabs, d_model, d_ff)
    o_dl, lse = _dil_fwd(qkv3, b_sz, s_len)
    o_sb2, o_dl2 = o_sb.reshape(t, SB_WIDTH), o_dl.reshape(t, DIL_OUT_WIDTH)
    y_sb = _mm(o_sb2, wf_sb_up, out_dtype=BF16, tm=1024, tn=1024, tk=SB_WIDTH, name="sb_up")
    y_dl = _mm(o_dl2, wf_dil_up, out_dtype=BF16, tm=1024, tn=1024, tk=DIL_OUT_WIDTH, name="dil_up")

    def merge_fn(gt, ys, yd):
        return (_sigmoid(gt[:, :d_model]) * ys + _sigmoid(gt[:, d_model:]) * yd,)

    (merged,) = _rowwise(merge_fn, [gates, y_sb, y_dl], [], [(d_model, BF16)], [], tm=512, name="merge")
    x1 = _mm(merged, wf_out, add=x2d, tm=512, tn=1024, tk=d_model, name="mix_out")
    (u2,) = _rowwise(lambda xv, g: (_rms_stats(xv)[0] * g,), [x1], [g_ffn], [(d_model, BF16)], [], tm=512, name="norm_ffn")
    h = _mm(u2, wf_ffn_in, out_dtype=BF16, tm=1024, tn=512, tk=d_model, name="ffn_in")

    def act_fn(hv):
        gate = hv[:, :d_ff]
        return (gate * _sigmoid(gate) * hv[:, d_ff:],)

    (act,) = _rowwise(act_fn, [h], [], [(d_ff, BF16)], [], tm=256, name="ffn_act")
    def head_fn(xv, tg, g):
        xhat, r = _rms_stats(xv)
        err = xhat * g - tg
        dy = err * (1.0 / d_model)
        dx, dg_rows = _rms_bwd(dy, xhat, r, g)
        loss_lanes = (0.5 / d_model) * jnp.sum(err * err, axis=0, keepdims=True)
        return dx, dx, jnp.sum(dg_rows, axis=0, keepdims=True), loss_lanes

    dx2, dx2_b, dg_fin, loss_lanes = _mm(
        act, wf_ffn_out, add=x1, tm=512, tn=1024, tk=d_ff, name="ffn_out",
        epilogue=(head_fn, [tgt2d], [g_fin], [(d_model, F32), (d_model, BF16)], [(1, d_model), (1, d_model)]))

    dact = _mm(dx2_b, wf_ffn_out, tb=True, out_dtype=BF16, tm=512, tn=d_ff // 2, tk=d_model, name="ffn_out_dx")
    gw_ffn_out = _mm(act, dx2_b, ta=True, tm=256, tn=d_model, tk=t, name="ffn_out_dw")

    def dact_fn(hv, da):
        gate, up = hv[:, :d_ff], hv[:, d_ff:]
        sg = _sigmoid(gate)
        dgate = da * up * (sg * (1.0 + gate * (1.0 - sg)))
        return (jnp.concatenate([dgate, da * (gate * sg)], axis=1),)

    (dh,) = _rowwise(dact_fn, [h, dact], [], [(2 * d_ff, BF16)], [], tm=256, name="ffn_act_bwd")
    def norm_bwd_fn(du_, dres, xv, g):
        xhat, r = _rms_stats(xv)
        dx, dg_rows = _rms_bwd(du_, xhat, r, g)
        return dres + dx, jnp.sum(dg_rows, axis=0, keepdims=True)

    def norm_bwd_twice(*args):
        dx, dg = norm_bwd_fn(*args)
        return dx, dx, dg

    dx1, dx1_b, dg_ffn = _mm(dh, wf_ffn_in, tb=True, tm=512, tn=1024, tk=2 * d_ff, name="ffn_in_dx",
                             epilogue=(norm_bwd_twice, [dx2, x1], [g_ffn], [(d_model, F32), (d_model, BF16)], [(1, d_model)]))
    gw_ffn_in = _mm(u2, dh, ta=True, tm=d_model, tn=512, tk=t, name="ffn_in_dw")

    dmerged = _mm(dx1_b, wf_out, tb=True, out_dtype=BF16, tm=512, tn=1024, tk=d_model, name="mix_out_dx")
    gw_out = _mm(merged, dx1_b, ta=True, tm=256, tn=d_model, tk=t, name="mix_out_dw")

    def merge_bwd_fn(gt, ys, yd, dm):
        s_sb, s_dl = _sigmoid(gt[:, :d_model]), _sigmoid(gt[:, d_model:])
        dgates = jnp.concatenate([dm * ys * s_sb * (1.0 - s_sb), dm * yd * s_dl * (1.0 - s_dl)], axis=1)
        return dgates, dm * s_sb, dm * s_dl

    dgates, dy_sb, dy_dl = _rowwise(merge_bwd_fn, [gates, y_sb, y_dl, dmerged], [],
                                    [(2 * d_model, BF16), (d_model, BF16), (d_model, BF16)], [], tm=256, name="merge_bwd")
    do_sb = _mm(dy_sb, wf_sb_up, tb=True, out_dtype=BF16, tm=1024, tn=SB_WIDTH, tk=d_model, name="sb_up_dx")
    gw_sb_up = _mm(o_sb2, dy_sb, ta=True, tm=SB_WIDTH, tn=1024, tk=512, name="sb_up_dw")
    do_dl = _mm(dy_dl, wf_dil_up, tb=True, tm=1024, tn=DIL_OUT_WIDTH, tk=d_model, name="dil_up_dx")
    gw_dil_up = _mm(o_dl2, dy_dl, ta=True, tm=DIL_OUT_WIDTH, tn=1024, tk=512, name="dil_up_dw")
    late_grads = {"w_sb_up": gw_sb_up, "w_dil_up": gw_dil_up, "w_out": gw_out, "w_ffn_in": gw_ffn_in, "w_ffn_out": gw_ffn_out}
    pair = _pair_reduce(late_grads, d_model, d_ff)
    (dq_sb, dk_sb, dv_sb), landed = _sb_bwd(qkv3, o_sb, do_sb.reshape(b_sz, s_len, SB_WIDTH), b_sz, s_len,
                                           [p[1] for p in pair])
    dq_dl, dk_dl, dv_dl = _dil_bwd(qkv3, o_dl, lse, do_dl.reshape(b_sz, s_len, DIL_OUT_WIDTH), b_sz, s_len)
    dproj = jnp.concatenate(
        [a.reshape(t, -1) for a in (dq_sb, dk_sb, dv_sb)]
        + [a.reshape(t, -1) for a in (dq_dl, dk_dl, dv_dl)] + [dgates], axis=1)
    gw_in = _mm(u, dproj, ta=True, tm=d_model, tn=256, tk=t, name="proj_dw")
    pair_in = _pair_reduce({"w_in": gw_in}, d_model, d_ff)
    (dx, dg_mix), landed_in = _mm(
        dproj, wf_in, tb=True, tm=512, tn=1024, tk=wf_in.shape[1], name="proj_dx", carried=[p[1] for p in pair_in],
        epilogue=(norm_bwd_fn, [dx1, x2d], [g_mix], [(d_model, F32)], [(1, d_model)]))

    grads = _chip_reduce(pair, landed, LATE)
    grads.update(_chip_reduce(pair_in, landed_in, ["w_in"]))
    return dx, grads, dg_mix, dg_ffn, dg_fin, loss_lanes


def kernel(x, norm_mix_g, w_in, w_sb_up, w_dil_up, w_out, norm_ffn_g, w_ffn_in, w_ffn_out, norm_final_g, loss_target, m_norm_mix_g, m_w_in, m_w_sb_up, m_w_dil_up, m_w_out, m_norm_ffn_g, m_w_ffn_in, m_w_ffn_out, m_norm_final_g, v_norm_mix_g, v_w_in, v_w_sb_up, v_w_dil_up, v_w_out, v_norm_ffn_g, v_w_ffn_in, v_w_ffn_out, v_norm_final_g):
    b_sz, s_len, d_model = x.shape
    d_ff = w_ffn_out.shape[1] * N_CHIPS
    g_mix, g_ffn, g_fin = norm_mix_g, norm_ffn_g, norm_final_g.reshape(1, d_model)

    names = ["w_in", "w_sb_up", "w_dil_up", "w_out", "w_ffn_in", "w_ffn_out"]
    shards = {"w_in": w_in[0], "w_sb_up": w_sb_up[0], "w_dil_up": w_dil_up[0], "w_out": w_out[0],
              "w_ffn_in": w_ffn_in[0], "w_ffn_out": w_ffn_out[0]}
    (slab_in,) = _gather_weights([_cast_to_slab(shards["w_in"], "cast_w_in")])
    late_slabs = [_cast_to_slab(shards[k], "cast_" + k) for k in LATE]

    dx, grads, dg_mix, dg_ffn, dg_fin, loss_lanes = _fwd_bwd(
        x, loss_target, g_mix, g_ffn, g_fin, _unshard_cols(slab_in), late_slabs)

    small = jnp.concatenate([dg_mix, dg_ffn, dg_fin, loss_lanes, jnp.zeros((4, d_model), F32)], axis=0)
    small = _all_sum_small(small)
    loss = small[3, 0]
    gains = jnp.concatenate([g_mix, g_ffn, g_fin, jnp.zeros((5, d_model), F32)], axis=0)
    gains_m = jnp.concatenate([m_norm_mix_g, m_norm_ffn_g, m_norm_final_g.reshape(1, d_model), jnp.zeros((5, d_model), F32)], axis=0)
    gains_v = jnp.concatenate([v_norm_mix_g, v_norm_ffn_g, v_norm_final_g.reshape(1, d_model), jnp.ones((5, d_model), F32)], axis=0)
    gd, gm, gv = _rowwise(_adamw_math, [gains, small, gains_m, gains_v], [], [(d_model, F32)] * 3, [], tm=8, name="adamw_gains")

    moments = {"w_in": (m_w_in, v_w_in), "w_sb_up": (m_w_sb_up, v_w_sb_up), "w_dil_up": (m_w_dil_up, v_w_dil_up),
               "w_out": (m_w_out, v_w_out), "w_ffn_in": (m_w_ffn_in, v_w_ffn_in), "w_ffn_out": (m_w_ffn_out, v_w_ffn_out)}
    upd = {k: _adamw(shards[k], grads[k], moments[k][0][0], moments[k][1][0], "adamw_" + k) for k in names}

    def w_out_of(i):
        return [upd[k][i][None] for k in names]

    def ordered(mix, ws, ffn_g, fin):
        return [mix, ws[0], ws[1], ws[2], ws[3], ffn_g, ws[4], ws[5], fin]

    grad_ws = [grads[k][None] for k in names]
    outs = [loss, dx.reshape(b_sz, s_len, d_model)]
    outs += ordered(small[0:1], grad_ws, small[1:2], small[2])
    outs += ordered(gd[0:1], w_out_of(0), gd[1:2], gd[2])
    outs += ordered(gm[0:1], w_out_of(1), gm[1:2], gm[2])
    outs += ordered(gv[0:1], w_out_of(2), gv[1:2], gv[2])
    return tuple(outs)
```

```python
import functools
import math

import jax
import jax.numpy as jnp
from jax import lax
from jax.experimental import pallas as pl
from jax.experimental.pallas import tpu as pltpu

F32 = jnp.float32
BF16 = jnp.bfloat16
MESH = pl.DeviceIdType.MESH

HEAD_DIM = 64
SB_HEADS = 8
DIL_PAIRS = ((128, 1), (512, 4), (2048, 16))
DIL_HEADS_PER_GROUP = 4
DIL_HEADS = DIL_HEADS_PER_GROUP * len(DIL_PAIRS)
SB_WIDTH = SB_HEADS * HEAD_DIM
DIL_WIDTH = DIL_HEADS * HEAD_DIM
DIL_OUT_WIDTH = DIL_HEADS_PER_GROUP * HEAD_DIM
QKV_WIDTH = 3 * SB_WIDTH + 3 * DIL_WIDTH
RMS_EPS = 1e-6
ALIBI_MAX_BIAS = 8.0
ADAM_LR = 0.001
ADAM_B1 = 0.9
ADAM_B2 = 0.999
ADAM_EPS = 1e-08
ADAM_WD = 0.01
ADAM_STEP = 10

LANES = 128
BLK = 128
NEG = -1e30
SB_FWD_CHAINS = 4
SB_BWD_CHAINS = 4
N_CHIPS = 4
VMEM_CAP = 56 * 1024 * 1024


def _vmem_limit(tile_bytes):
    return int(min(VMEM_CAP, max(32 * 1024 * 1024, 3 * tile_bytes + 8 * 1024 * 1024)))


def _nbytes(shape, dtype):
    return math.prod(shape) * jnp.dtype(dtype).itemsize


def _dot(a, b):
    return jnp.dot(a, b, preferred_element_type=F32)


def _dot_nt(a, b):
    return lax.dot_general(a, b, (((1,), (1,)), ((), ())), preferred_element_type=F32)


def _dot_tn(a, b):
    return lax.dot_general(a, b, (((0,), (0,)), ((), ())), preferred_element_type=F32)


def _split2(x):
    hi = x.astype(BF16)
    lo = (x - hi.astype(F32)).astype(BF16)
    return hi, lo


def _sigmoid(x):
    return 1.0 / (1.0 + jnp.exp(-x))


def _mm(a, b, *, ta=False, tb=False, add=None, out_dtype=F32, tm, tn, tk, name, carried=(), epilogue=None,
        b_cols=None):
    n_car = len(carried)
    if ta:
        kdim, m = a.shape
    else:
        m, kdim = a.shape
    if tb:
        n, k2 = b.shape
    else:
        k2, n = b.shape
    col0 = 0
    if b_cols is not None:
        assert not tb and b_cols[0] % tn == 0, name
        col0, n = b_cols[0] // tn, b_cols[1]
    assert kdim == k2 and m % tm == 0 and n % tn == 0 and kdim % tk == 0, (name, a.shape, b.shape)
    nk = kdim // tk
    grid = (m // tm, n // tn, nk)
    a_mode = dict(pipeline_mode=pl.Buffered(1)) if grid[0] == 1 and nk == 1 else {}
    b_mode = dict(pipeline_mode=pl.Buffered(1)) if grid[1] == 1 and nk == 1 else {}
    a_spec = (pl.BlockSpec((tk, tm), lambda i, j, k: (k, i), **a_mode) if ta
              else pl.BlockSpec((tm, tk), lambda i, j, k: (i, k), **a_mode))
    b_spec = (pl.BlockSpec((tn, tk), lambda i, j, k: (j, k), **b_mode) if tb
              else pl.BlockSpec((tk, tn), lambda i, j, k: (k, j + col0), **b_mode))
    o_spec = pl.BlockSpec((tm, tn), lambda i, j, k: (i, j))
    dims = ((((0,) if ta else (1,)), ((1,) if tb else (0,))), ((), ()))
    has_add = add is not None
    if epilogue is None:
        ep_fn, ep_rows, ep_params, ep_outs, ep_accs = None, [], [], [], []
        out_sds, out_specs = [jax.ShapeDtypeStruct((m, n), out_dtype)], [o_spec]
    else:
        ep_fn, ep_rows, ep_params, ep_outs, ep_accs = epilogue
        assert grid[1] == 1 or not ep_accs, name
        out_sds = [jax.ShapeDtypeStruct((m, w * grid[1]), d) for w, d in ep_outs]
        out_sds += [jax.ShapeDtypeStruct(sh, F32) for sh in ep_accs]
        out_specs = [pl.BlockSpec((tm, w), lambda i, j, k: (i, j)) for w, _ in ep_outs]
        out_specs += [pl.BlockSpec(sh, lambda i, j, k: (0, 0)) for sh in ep_accs]
    n_main = len(out_sds)
    use_scratch = nk > 1 and (ep_fn is not None or jnp.dtype(out_dtype) != jnp.dtype(F32))
    n_in = 2 + has_add + len(ep_rows) + len(ep_params)

    def finish(total, refs, pid):
        outs = refs[n_in + n_car:n_in + n_car + n_main]
        if ep_fn is None:
            outs[0][...] = total.astype(out_dtype)
            return
        first = 2 + has_add
        rows = [r[...].astype(F32) for r in refs[first:first + len(ep_rows)]]
        params = [p[...] for p in refs[first + len(ep_rows):n_in]]
        res = ep_fn(total, *rows, *params)
        for o_ref, v in zip(outs[:len(ep_outs)], res):
            o_ref[...] = v.astype(o_ref.dtype)
        acc_refs = outs[len(ep_outs):]
        if acc_refs:
            @pl.when(pid[0] == 0)
            def _():
                for r in acc_refs:
                    r[...] = jnp.zeros(r.shape, F32)

            for r, v in zip(acc_refs, res[len(ep_outs):]):
                r[...] += v

    def compute(refs, pid):
        a_ref, b_ref = refs[0], refs[1]
        add_ref = refs[2] if has_add else None
        prod = lax.dot_general(a_ref[...].astype(BF16), b_ref[...].astype(BF16), dims, preferred_element_type=F32)
        if nk == 1:
            finish(prod + add_ref[...] if has_add else prod, refs, pid)
            return
        acc_ref = refs[n_in + 2 * n_car + n_main] if use_scratch else refs[n_in + n_car]
        k = pid[2]

        @pl.when(k == 0)
        def _():
            acc_ref[...] = prod + add_ref[...] if has_add else prod

        @pl.when(k > 0)
        def _():
            acc_ref[...] += prod

        if use_scratch:
            @pl.when(k == nk - 1)
            def _():
                finish(acc_ref[...], refs, pid)

    def body(*refs):
        pid = (pl.program_id(0), pl.program_id(1), pl.program_id(2))
        if not n_car:
            compute(refs, pid)
            return
        lands = refs[n_in + n_car + n_main:n_in + 2 * n_car + n_main]
        copies = _chip_copies(refs[n_in:n_in + n_car], lands, *refs[-2:])
        step = (pid[0] * grid[1] + pid[1]) * nk + pid[2]

        @pl.when(step == 0)
        def _():
            for cp in copies:
                cp.start()

        compute(refs, pid)

        @pl.when(step == grid[0] * grid[1] * nk - 1)
        def _():
            for cp in copies:
                cp.wait()

    tile_bytes = (_nbytes((tm, tk), a.dtype) + _nbytes((tk, tn), b.dtype) + 2 * _nbytes((tm, tn), F32)
                  + (_nbytes((tm, tn), F32) if has_add else 0)
                  + sum(_nbytes((tm, r.shape[1]), r.dtype) for r in ep_rows) + sum(_nbytes((tm, w), d) for w, d in ep_outs))
    in_specs = [a_spec, b_spec] + ([o_spec] if has_add else [])
    in_specs += [pl.BlockSpec((tm, r.shape[1] // grid[1]), lambda i, j, k: (i, j)) for r in ep_rows]
    in_specs += [pl.BlockSpec(p.shape, lambda i, j, k: (0, 0)) for p in ep_params]
    args = (a, b) + ((add,) if has_add else ()) + tuple(ep_rows) + tuple(ep_params)
    scratch = [pltpu.VMEM((tm, tn), F32)] if use_scratch else []
    serial = bool(n_car or ep_accs)
    res = pl.pallas_call(
        body, name=name, grid=grid,
        in_specs=in_specs + _hbm_specs(n_car), out_specs=out_specs + _hbm_specs(n_car),
        out_shape=out_sds + _chip_landing(carried), scratch_shapes=scratch + (_chip_sems(n_car) if n_car else []),
        compiler_params=pltpu.CompilerParams(
            dimension_semantics=("arbitrary",) * 3 if serial else ("parallel", "parallel", "arbitrary"),
            vmem_limit_bytes=_vmem_limit(tile_bytes)),
    )(*args, *carried)
    main = res[0] if ep_fn is None else list(res[:n_main])
    return (main, res[n_main:]) if n_car else main


def _rowwise(fn, rows, params, outs, accs, *, tm, name):
    t = rows[0].shape[0]
    assert t % tm == 0, (name, t, tm)
    n_r, n_p, n_o = len(rows), len(params), len(outs)

    def body(*refs):
        vals = [r[...].astype(F32) for r in refs[:n_r]] + [p[...] for p in refs[n_r:n_r + n_p]]
        res = fn(*vals)
        o_refs = refs[n_r + n_p:n_r + n_p + n_o]
        a_refs = refs[n_r + n_p + n_o:]
        for o_ref, v in zip(o_refs, res[:n_o]):
            o_ref[...] = v.astype(o_ref.dtype)
        if accs:
            @pl.when(pl.program_id(0) == 0)
            def _():
                for a_ref in a_refs:
                    a_ref[...] = jnp.zeros(a_ref.shape, F32)

            for a_ref, v in zip(a_refs, res[n_o:]):
                a_ref[...] += v

    in_specs = [pl.BlockSpec((tm, r.shape[1]), lambda i: (i, 0)) for r in rows]
    in_specs += [pl.BlockSpec(p.shape, lambda i: (0, 0)) for p in params]
    out_specs = [pl.BlockSpec((tm, w), lambda i: (i, 0)) for w, _ in outs]
    out_specs += [pl.BlockSpec(s, lambda i: (0, 0)) for s in accs]
    out_shape = [jax.ShapeDtypeStruct((t, w), d) for w, d in outs]
    out_shape += [jax.ShapeDtypeStruct(s, F32) for s in accs]
    tile_bytes = sum(_nbytes((tm, r.shape[1]), r.dtype) for r in rows) + sum(_nbytes((tm, w), F32) for w, _ in outs)
    res = pl.pallas_call(
        body, name=name, grid=(t // tm,), in_specs=in_specs, out_specs=out_specs, out_shape=out_shape,
        compiler_params=pltpu.CompilerParams(
            dimension_semantics=("arbitrary",) if accs else ("parallel",),
            vmem_limit_bytes=_vmem_limit(2 * tile_bytes)),
    )(*rows, *params)
    return res


def _rms_stats(x):
    r = lax.rsqrt(jnp.mean(x * x, axis=-1, keepdims=True) + RMS_EPS)
    return x * r, r


def _rms_bwd(dy, xhat, r, g):
    dxhat = dy * g
    dx = r * (dxhat - xhat * jnp.mean(dxhat * xhat, axis=-1, keepdims=True))
    return dx, dy * xhat


def _sb_consts():
    lane = lax.broadcasted_iota(jnp.int32, (BLK, LANES), 1)
    head0 = lane < HEAD_DIM
    row = lax.broadcasted_iota(jnp.int32, (2 * BLK, BLK), 0) % BLK
    col = lax.broadcasted_iota(jnp.int32, (2 * BLK, BLK), 1)
    causal = col < row
    jj = lax.broadcasted_iota(jnp.int32, (BLK, BLK), 0)
    ss = lax.broadcasted_iota(jnp.int32, (BLK, BLK), 1)
    suffix = jnp.where(jj > ss, 1.0, 0.0).astype(BF16)
    return head0, causal, suffix


def _stack_heads(x, head0):
    zero = jnp.zeros_like(x)
    return jnp.concatenate([jnp.where(head0, x, zero), jnp.where(head0, zero, x)], axis=0)


def _sb_logits(z, causal, masked):
    sp = jnp.log(1.0 + jnp.exp(-jnp.abs(z)))
    log_keep = -(jnp.maximum(z, 0.0) + sp)
    log_beta = jnp.minimum(z, 0.0) - sp
    if masked:
        log_keep = jnp.where(causal, log_keep, 0.0)
    return log_keep, log_beta


def _suffix_sums(x, suffix):
    hi, lo = _split2(x)
    after = _dot(hi, suffix) + _dot(lo, suffix)
    total = jnp.broadcast_to(after[:, 0:1] + x[:, 0:1], x.shape)
    return after, total


def _lane_blocks(x, n):
    return [x[:, p * LANES:(p + 1) * LANES] for p in range(n)]


def _sb_fwd(qkv, b_sz, s_len, slabs):
    n_slabs = len(slabs)
    nq = s_len // BLK
    n_pairs = SB_WIDTH // LANES
    ch = SB_FWD_CHAINS
    n_steps = n_pairs // ch
    scale = 1.0 / math.sqrt(HEAD_DIM)

    def body(q_ref, k_ref, v_ref, *rest):
        o_ref = rest[n_slabs]
        slab_refs = rest[n_slabs + 1:2 * n_slabs + 1]
        send_sems, recv_sems = rest[2 * n_slabs + 1:]
        step = pl.program_id(0) * n_steps + pl.program_id(1)
        head0, causal, suffix = _sb_consts()

        @pl.when(step == 0)
        def _():
            _gather_issue(slab_refs, send_sems, recv_sems)

        def q_block(i, _):
            qs = pl.multiple_of(i * BLK, BLK)
            q_all = (q_ref[pl.ds(qs, BLK), :] * scale).astype(BF16)
            q01 = [_stack_heads(q, head0) for q in _lane_blocks(q_all, ch)]

            def tile(j, state, masked):
                ks = pl.multiple_of(j * BLK, BLK)
                ks_ = _lane_blocks(k_ref[pl.ds(ks, BLK), :].astype(BF16), ch)
                vs_ = _lane_blocks(v_ref[pl.ds(ks, BLK), :].astype(BF16), ch)
                zs = [_dot_nt(q01[p], ks_[p]) for p in range(ch)]
                logits = [_sb_logits(z, causal, masked) for z in zs]
                sums = [_suffix_sums(lg[0], suffix) for lg in logits]
                out = []
                for p in range(ch):
                    carry, acc = state[2 * p], state[2 * p + 1]
                    after, total = sums[p]
                    a = jnp.exp(logits[p][1] + carry + after)
                    if masked:
                        a = jnp.where(causal, a, 0.0)
                    a_hi, a_lo = _split2(a)
                    a_cat = jnp.concatenate([a_hi[:BLK], a_hi[BLK:], a_lo[:BLK], a_lo[BLK:]], axis=1)
                    v01 = _stack_heads(vs_[p], head0)
                    out += [carry + total, acc + _dot(a_cat, jnp.concatenate([v01, v01], axis=0))]
                return tuple(out)

            state = (jnp.zeros((2 * BLK, BLK), F32), jnp.zeros((BLK, LANES), F32)) * ch
            state = tile(i, state, True)
            state = lax.fori_loop(0, i, lambda t, st: tile(i - 1 - t, st, False), state)
            o_ref[pl.ds(qs, BLK), :] = jnp.concatenate([state[2 * p + 1] for p in range(ch)], axis=1)
            return 0

        lax.fori_loop(0, nq, q_block, 0)

        @pl.when(step == b_sz * n_steps - 1)
        def _():
            _gather_complete(slab_refs, send_sems, recv_sems)

    blk = lambda off: pl.BlockSpec((None, s_len, ch * LANES), lambda b, p: (b, 0, off + p))
    res = pl.pallas_call(
        body, name="sb_fwd", grid=(b_sz, n_steps),
        in_specs=[blk(0), blk(n_steps), blk(2 * n_steps)] + _hbm_specs(n_slabs),
        out_specs=[blk(0)] + _hbm_specs(n_slabs),
        out_shape=[jax.ShapeDtypeStruct((b_sz, s_len, SB_WIDTH), F32)]
        + [jax.ShapeDtypeStruct(a.shape, a.dtype) for a in slabs],
        input_output_aliases={3 + k: 1 + k for k in range(n_slabs)},
        scratch_shapes=_gather_sems(n_slabs),
        compiler_params=pltpu.CompilerParams(dimension_semantics=("arbitrary", "arbitrary"),
                                             vmem_limit_bytes=VMEM_CAP),
    )(qkv, qkv, qkv, *slabs)
    return res[0], res[1:]


def _sb_bwd(qkv, o_sb, do_sb, b_sz, s_len, sums_bf16):
    n_sums = len(sums_bf16)
    nq = s_len // BLK
    n_pairs = SB_WIDTH // LANES
    ch = SB_BWD_CHAINS
    n_steps = n_pairs // ch
    scale = 1.0 / math.sqrt(HEAD_DIM)

    def body(q_ref, k_ref, v_ref, o_ref, do_ref, *rest):
        sum_refs = rest[:n_sums]
        dq_ref, dk_ref, dv_ref = rest[n_sums:n_sums + 3]
        land_refs = rest[n_sums + 3:2 * n_sums + 3]
        dk_acc, dv_acc, send_sems, recv_sems = rest[2 * n_sums + 3:]
        step = pl.program_id(0) * n_steps + pl.program_id(1)
        copies = _chip_copies(sum_refs, land_refs, send_sems, recv_sems)

        @pl.when(step == 0)
        def _():
            for cp in copies:
                cp.start()

        head0, causal, suffix = _sb_consts()
        lrow = lax.broadcasted_iota(jnp.int32, (LANES, LANES), 0)
        ones_h0 = jnp.where(lrow < HEAD_DIM, 1.0, 0.0).astype(BF16)
        ones_h1 = jnp.where(lrow >= HEAD_DIM, 1.0, 0.0).astype(BF16)
        dk_acc[...] = jnp.zeros(dk_acc.shape, F32)
        dv_acc[...] = jnp.zeros(dv_acc.shape, F32)

        def q_block(i, _):
            qs = pl.multiple_of(i * BLK, BLK)
            q_all = (q_ref[pl.ds(qs, BLK), :] * scale).astype(BF16)
            do_all = do_ref[pl.ds(qs, BLK), :].astype(BF16)
            dd_all = do_all.astype(F32) * o_ref[pl.ds(qs, BLK), :]
            q01 = [_stack_heads(q, head0) for q in _lane_blocks(q_all, ch)]
            do01 = [_stack_heads(d, head0) for d in _lane_blocks(do_all, ch)]
            tot = []
            for dd in _lane_blocks(dd_all, ch):
                dd_hi, dd_lo = _split2(dd)
                tot.append(jnp.concatenate([_dot(dd_hi, ones_h0) + _dot(dd_lo, ones_h0),
                                            _dot(dd_hi, ones_h1) + _dot(dd_lo, ones_h1)], axis=0))

            def tile(j, state, masked):
                ks = pl.multiple_of(j * BLK, BLK)
                ks_ = _lane_blocks(k_ref[pl.ds(ks, BLK), :].astype(BF16), ch)
                vs_ = _lane_blocks(v_ref[pl.ds(ks, BLK), :].astype(BF16), ch)
                zs = [_dot_nt(q01[p], ks_[p]) for p in range(ch)]
                das = [_dot_nt(do01[p], vs_[p]) for p in range(ch)]
                logits = [_sb_logits(z, causal, masked) for z in zs]
                sums = [_suffix_sums(lg[0], suffix) for lg in logits]
                a_s, e_s = [], []
                for p in range(ch):
                    a = jnp.exp(logits[p][1] + state[3 * p] + sums[p][0])
                    if masked:
                        a = jnp.where(causal, a, 0.0)
                    a_s.append(a)
                    e_s.append(a * das[p])
                e_sums = [_suffix_sums(e, suffix) for e in e_s]
                out, dks, dvs = [], [], []
                for p in range(ch):
                    carry, rcarry, dq = state[3 * p:3 * p + 3]
                    e = e_s[p]
                    before = tot[p] - (rcarry + e_sums[p][0] + e)
                    beta = jnp.exp(logits[p][1])
                    dz = e * (1.0 - beta) - beta * before
                    if masked:
                        dz = jnp.where(causal, dz, 0.0)
                    dz_b = dz.astype(BF16)
                    dks.append(_dot_tn(dz_b, q01[p]))
                    dvs.append(_dot_tn(a_s[p].astype(BF16), do01[p]))
                    out += [carry + sums[p][1], rcarry + e_sums[p][1], dq + _dot(dz_b, ks_[p])]
                dk_acc[pl.ds(ks, BLK), :] += jnp.concatenate(dks, axis=1)
                dv_acc[pl.ds(ks, BLK), :] += jnp.concatenate(dvs, axis=1)
                return tuple(out)

            state = (jnp.zeros((2 * BLK, BLK), F32),) * (3 * ch)
            state = tile(i, state, True)
            state = lax.fori_loop(0, i, lambda t, st: tile(i - 1 - t, st, False), state)
            dq = [jnp.where(head0, state[3 * p + 2][:BLK], state[3 * p + 2][BLK:]) for p in range(ch)]
            dq_ref[pl.ds(qs, BLK), :] = (jnp.concatenate(dq, axis=1) * scale).astype(dq_ref.dtype)
            return 0

        lax.fori_loop(0, nq, q_block, 0)
        dk_ref[...] = dk_acc[...].astype(dk_ref.dtype)
        dv_ref[...] = dv_acc[...].astype(dv_ref.dtype)

        @pl.when(step == b_sz * n_steps - 1)
        def _():
            for cp in copies:
                cp.wait()

    blk = lambda off: pl.BlockSpec((None, s_len, ch * LANES), lambda b, p: (b, 0, off + p))
    once = lambda off: pl.BlockSpec((None, s_len, ch * LANES), lambda b, p: (b, 0, off + p),
                                    pipeline_mode=pl.Buffered(1))
    out_sd = jax.ShapeDtypeStruct((b_sz, s_len, SB_WIDTH), BF16)
    res = pl.pallas_call(
        body, name="sb_bwd", grid=(b_sz, n_steps),
        in_specs=[once(0), once(n_steps), once(2 * n_steps), once(0), once(0)] + _hbm_specs(n_sums),
        out_specs=[blk(0), blk(0), blk(0)] + _hbm_specs(n_sums),
        out_shape=[out_sd, out_sd, out_sd] + _chip_landing(sums_bf16),
        scratch_shapes=[pltpu.VMEM((s_len, ch * LANES), F32), pltpu.VMEM((s_len, ch * LANES), F32)] + _chip_sems(n_sums),
        compiler_params=pltpu.CompilerParams(dimension_semantics=("arbitrary", "arbitrary"),
                                             vmem_limit_bytes=VMEM_CAP),
    )(qkv, qkv, qkv, o_sb, do_sb, *sums_bf16)
    return res[:3], res[3:]


def _dil_consts(group, pair_idx, dilation):
    lane = lax.broadcasted_iota(jnp.int32, (BLK, LANES), 1)
    head0 = lane < HEAD_DIM
    row = lax.broadcasted_iota(jnp.int32, (2 * BLK, BLK), 0)
    qa = row % BLK
    kb = lax.broadcasted_iota(jnp.int32, (2 * BLK, BLK), 1)
    head = (group * DIL_HEADS_PER_GROUP + 2 * pair_idx + row // BLK).astype(F32)
    slope = jnp.exp((-ALIBI_MAX_BIAS * math.log(2.0) / DIL_HEADS) * (head + 1.0))
    valid_cur = kb <= qa
    valid_prev = kb >= qa
    bias_cur = -slope * ((qa - kb) * dilation).astype(F32)
    bias_prev = -slope * ((BLK + qa - kb) * dilation).astype(F32)
    return head0, valid_cur, valid_prev, bias_cur, bias_prev


def _dil_units(s_len, dilation):
    nb = s_len // dilation // BLK
    return [(r, n) for r in range(dilation) for n in range(nb)]


def _dil_rows(n, r, dilation):
    if dilation == 1:
        return pl.ds(n * BLK, BLK)
    return pl.ds(n * BLK * dilation + r, BLK, stride=dilation)


def _dil_scores(q01, k, bias, valid):
    s = _dot_nt(q01, k) * (1.0 / math.sqrt(HEAD_DIM)) + bias
    return jnp.where(valid, s, NEG)


def _dil_fwd(qkv, b_sz, s_len):
    n_pairs = DIL_OUT_WIDTH // LANES
    q_off = 3 * SB_WIDTH // LANES
    per_kind = DIL_WIDTH // LANES

    def body(*refs):
        qkv_refs = refs[:9]
        o_ref, lse_ref, m_s, l_s = refs[9:]
        pair_idx = pl.program_id(1)
        m_s[...] = jnp.full(m_s.shape, NEG, F32)
        l_s[...] = jnp.zeros(l_s.shape, F32)
        o_ref[...] = jnp.zeros(o_ref.shape, F32)
        for g, (_, dilation) in enumerate(DIL_PAIRS):
            q_ref, k_ref, v_ref = qkv_refs[3 * g:3 * g + 3]
            head0, valid_cur, valid_prev, bias_cur, bias_prev = _dil_consts(g, pair_idx, dilation)
            for r, n in _dil_units(s_len, dilation):
                rows = _dil_rows(n, r, dilation)
                q01 = _stack_heads(q_ref[rows, :].astype(BF16), head0)
                k_c = k_ref[rows, :].astype(BF16)
                v_c = v_ref[rows, :].astype(BF16)
                scores = [_dil_scores(q01, k_c, bias_cur, valid_cur)]
                values = [_stack_heads(v_c, head0)]
                if n > 0:
                    prev = _dil_rows(n - 1, r, dilation)
                    scores.append(_dil_scores(q01, k_ref[prev, :].astype(BF16), bias_prev, valid_prev))
                    values.append(_stack_heads(v_ref[prev, :].astype(BF16), head0))
                m_blk = functools.reduce(jnp.maximum, [jnp.max(s, axis=-1, keepdims=True) for s in scores])
                m_old = jnp.concatenate([m_s.at[0][rows, :], m_s.at[1][rows, :]], axis=0)
                l_old = jnp.concatenate([l_s.at[0][rows, :], l_s.at[1][rows, :]], axis=0)
                m_new = jnp.maximum(m_old, m_blk)
                probs = [jnp.exp(s - m_new) for s in scores]
                l_blk = functools.reduce(jnp.add, [jnp.sum(p, axis=-1, keepdims=True) for p in probs])
                alpha = jnp.exp(m_old - m_new)
                l_new = alpha * l_old + l_blk
                alpha_tok = jnp.where(head0, alpha[:BLK], alpha[BLK:])
                p_cat = jnp.concatenate([h for p in probs for h in (p[:BLK].astype(BF16), p[BLK:].astype(BF16))], axis=1)
                o_ref[rows, :] = alpha_tok * o_ref[rows, :] + _dot(p_cat, jnp.concatenate(values, axis=0))
                m_s.at[0][rows, :] = m_new[:BLK]
                m_s.at[1][rows, :] = m_new[BLK:]
                l_s.at[0][rows, :] = l_new[:BLK]
                l_s.at[1][rows, :] = l_new[BLK:]
        lane = lax.broadcasted_iota(jnp.int32, (BLK, LANES), 1)
        for c in range(s_len // BLK):
            rows = pl.ds(c * BLK, BLK)
            l0, l1 = l_s.at[0][rows, :], l_s.at[1][rows, :]
            o_ref[rows, :] = o_ref[rows, :] / jnp.where(lane < HEAD_DIM, l0, l1)
            lse_ref.at[0][rows, :] = m_s.at[0][rows, :] + jnp.log(l0)
            lse_ref.at[1][rows, :] = m_s.at[1][rows, :] + jnp.log(l1)

    in_specs = []
    for g in range(len(DIL_PAIRS)):
        for kind in range(3):
            off = q_off + kind * per_kind + g * n_pairs
            in_specs.append(pl.BlockSpec((None, s_len, LANES), lambda b, p, off=off: (b, 0, off + p)))
    return pl.pallas_call(
        body, name="dil_fwd", grid=(b_sz, n_pairs),
        in_specs=in_specs,
        out_specs=[pl.BlockSpec((None, s_len, LANES), lambda b, p: (b, 0, p)),
                   pl.BlockSpec((None, None, 2, s_len, LANES), lambda b, p: (b, p, 0, 0, 0))],
        out_shape=[jax.ShapeDtypeStruct((b_sz, s_len, DIL_OUT_WIDTH), F32),
                   jax.ShapeDtypeStruct((b_sz, n_pairs, 2, s_len, LANES), F32)],
        scratch_shapes=[pltpu.VMEM((2, s_len, LANES), F32), pltpu.VMEM((2, s_len, LANES), F32)],
        compiler_params=pltpu.CompilerParams(dimension_semantics=("parallel", "parallel"),
                                             vmem_limit_bytes=VMEM_CAP),
    )(*([qkv] * 9))


def _dil_bwd(qkv, o_dl, lse, do_dl, b_sz, s_len):
    n_pairs = DIL_OUT_WIDTH // LANES
    n_groups = len(DIL_PAIRS)
    q_off = 3 * SB_WIDTH // LANES
    per_kind = DIL_WIDTH // LANES

    def body(q_ref, k_ref, v_ref, o_ref, lse_ref, do_ref, dq_ref, dk_ref, dv_ref, d_s, dq_s, dk_s, dv_s):
        pair_idx = pl.program_id(1)
        group = pl.program_id(2)
        lrow = lax.broadcasted_iota(jnp.int32, (LANES, LANES), 0)
        ones_h0 = jnp.where(lrow < HEAD_DIM, 1.0, 0.0).astype(BF16)
        ones_h1 = jnp.where(lrow >= HEAD_DIM, 1.0, 0.0).astype(BF16)
        for c in range(s_len // BLK):
            rows = pl.ds(c * BLK, BLK)
            dd_hi, dd_lo = _split2(do_ref[rows, :] * o_ref[rows, :])
            d_s.at[0][rows, :] = _dot(dd_hi, ones_h0) + _dot(dd_lo, ones_h0)
            d_s.at[1][rows, :] = _dot(dd_hi, ones_h1) + _dot(dd_lo, ones_h1)
        dk_s[...] = jnp.zeros(dk_s.shape, F32)
        dv_s[...] = jnp.zeros(dv_s.shape, F32)

        def one_group(g, dilation):
            head0, valid_cur, valid_prev, bias_cur, bias_prev = _dil_consts(g, pair_idx, dilation)
            for r, n in _dil_units(s_len, dilation):
                rows = _dil_rows(n, r, dilation)
                q01 = _stack_heads(q_ref[rows, :].astype(BF16), head0)
                do01 = _stack_heads(do_ref[rows, :].astype(BF16), head0)
                lse01 = jnp.concatenate([lse_ref.at[0][rows, :], lse_ref.at[1][rows, :]], axis=0)
                d01 = jnp.concatenate([d_s.at[0][rows, :], d_s.at[1][rows, :]], axis=0)
                dq = jnp.zeros((2 * BLK, LANES), F32)
                blocks = [(rows, bias_cur, valid_cur)]
                if n > 0:
                    blocks.append((_dil_rows(n - 1, r, dilation), bias_prev, valid_prev))
                for krows, bias, valid in blocks:
                    k = k_ref[krows, :].astype(BF16)
                    v = v_ref[krows, :].astype(BF16)
                    p = jnp.exp(_dil_scores(q01, k, bias, valid) - lse01)
                    ds = (p * (_dot_nt(do01, v) - d01) * (1.0 / math.sqrt(HEAD_DIM))).astype(BF16)
                    dq = dq + _dot(ds, k)
                    dk_s[krows, :] = dk_s[krows, :] + _dot_tn(ds, q01)
                    dv_s[krows, :] = dv_s[krows, :] + _dot_tn(p.astype(BF16), do01)
                dq_s[rows, :] = jnp.where(head0, dq[:BLK], dq[BLK:])

        for g, (_, dilation) in enumerate(DIL_PAIRS):
            pl.when(group == g)(functools.partial(one_group, g, dilation))
        dq_ref[...] = dq_s[...].astype(dq_ref.dtype)
        dk_ref[...] = dk_s[...].astype(dk_ref.dtype)
        dv_ref[...] = dv_s[...].astype(dv_ref.dtype)

    def qkv_spec(kind):
        return pl.BlockSpec((None, s_len, LANES),
                            lambda b, p, g: (b, 0, q_off + kind * per_kind + g * n_pairs + p))

    tok_spec = pl.BlockSpec((None, s_len, LANES), lambda b, p, g: (b, 0, p))
    out_spec = pl.BlockSpec((None, s_len, LANES), lambda b, p, g: (b, 0, g * n_pairs + p))
    out_sd = jax.ShapeDtypeStruct((b_sz, s_len, DIL_WIDTH), BF16)
    return pl.pallas_call(
        body, name="dil_bwd", grid=(b_sz, n_pairs, n_groups),
        in_specs=[qkv_spec(0), qkv_spec(1), qkv_spec(2), tok_spec,
                  pl.BlockSpec((None, None, 2, s_len, LANES), lambda b, p, g: (b, p, 0, 0, 0)), tok_spec],
        out_specs=[out_spec, out_spec, out_spec],
        out_shape=[out_sd, out_sd, out_sd],
        scratch_shapes=[pltpu.VMEM((2, s_len, LANES), F32)] + [pltpu.VMEM((s_len, LANES), F32)] * 3,
        compiler_params=pltpu.CompilerParams(dimension_semantics=("parallel", "parallel", "arbitrary"),
                                             vmem_limit_bytes=VMEM_CAP),
    )(qkv, qkv, qkv, o_dl, lse, do_dl)


def _mesh_pos():
    return lax.axis_index("x"), lax.axis_index("y"), lax.axis_index("c")


def _other_chips(x, y):
    return [(1 - x, y), (x, 1 - y), (1 - x, 1 - y)]


def _hbm_specs(n):
    return [pl.BlockSpec(memory_space=pl.ANY)] * n


def _cast_to_slab(w, name):
    rows, cols = w.shape
    mine = jnp.reshape(2 * lax.axis_index("x") + lax.axis_index("y"), (1,)).astype(jnp.int32)

    def body(idx_ref, w_ref, o_ref):
        o_ref[...] = w_ref[...].astype(BF16)

    return pl.pallas_call(
        body, name=name,
        grid_spec=pltpu.PrefetchScalarGridSpec(
            num_scalar_prefetch=1, grid=(1,),
            in_specs=[pl.BlockSpec((rows, cols), lambda i, idx: (0, 0))],
            out_specs=pl.BlockSpec((None, rows, cols), lambda i, idx: (idx[0], 0, 0))),
        out_shape=jax.ShapeDtypeStruct((N_CHIPS, rows, cols), BF16),
        compiler_params=pltpu.CompilerParams(vmem_limit_bytes=_vmem_limit(rows * cols * 6)),
    )(mine, w)


def _gather_issue(slabs, send_sems, recv_sems):
    x, y, c = _mesh_pos()
    for k, slab in enumerate(slabs):
        half = slab.shape[1] // 2
        rows = slab.at[2 * x + y, pl.ds(c * half, half), :]
        for r, (px, py) in enumerate(_other_chips(x, y)):
            pltpu.make_async_remote_copy(
                src_ref=rows, dst_ref=rows, send_sem=send_sems.at[6 * k + r], recv_sem=recv_sems.at[6 * k + r],
                device_id=(px, py, c), device_id_type=MESH).start()


def _gather_complete(slabs, send_sems, recv_sems):
    x, y, c = _mesh_pos()
    chips = _other_chips(x, y)

    def copy(k, sem, block, rows, to):
        ref = slabs[k].at[block, rows, :]
        return pltpu.make_async_remote_copy(
            src_ref=ref, dst_ref=ref, send_sem=send_sems.at[sem], recv_sem=recv_sems.at[sem],
            device_id=to, device_id_type=MESH)

    for k, slab in enumerate(slabs):
        half = slab.shape[1] // 2
        for r, (px, py) in enumerate(chips):
            copy(k, 6 * k + r, 2 * px + py, pl.ds(c * half, half), (px, py, c)).wait_recv()
            copy(k, 6 * k + 3 + r, 2 * px + py, pl.ds(c * half, half), (x, y, 1 - c)).start()
    for k, slab in enumerate(slabs):
        half = slab.shape[1] // 2
        for r, (px, py) in enumerate(chips):
            copy(k, 6 * k + 3 + r, 2 * px + py, pl.ds((1 - c) * half, half), (x, y, 1 - c)).wait_recv()
    for k, slab in enumerate(slabs):
        half = slab.shape[1] // 2
        for r, (px, py) in enumerate(chips):
            copy(k, 6 * k + r, 2 * x + y, pl.ds(c * half, half), (px, py, c)).wait_send()
            copy(k, 6 * k + 3 + r, 2 * px + py, pl.ds(c * half, half), (x, y, 1 - c)).wait_send()


def _gather_sems(n):
    return [pltpu.SemaphoreType.DMA((6 * n,)), pltpu.SemaphoreType.DMA((6 * n,))]


def _gather_weights(slabs):
    n = len(slabs)

    def body(*refs):
        outs = refs[n:2 * n]
        send_sems, recv_sems = refs[2 * n:]
        _gather_issue(outs, send_sems, recv_sems)
        _gather_complete(outs, send_sems, recv_sems)

    return pl.pallas_call(
        body, name="gather_weights",
        in_specs=_hbm_specs(n), out_specs=_hbm_specs(n),
        out_shape=[jax.ShapeDtypeStruct(s.shape, s.dtype) for s in slabs],
        input_output_aliases={k: k for k in range(n)},
        scratch_shapes=_gather_sems(n),
    )(*slabs)


def _pair_exchange(grads, tag):
    n = len(grads)

    def body(*refs):
        ins, outs = refs[:n], refs[n:2 * n]
        send_sems, recv_sems = refs[2 * n:]
        x, y, c = _mesh_pos()
        copies = []
        for k in range(n):
            half = grads[k].shape[1] // 2
            cp = pltpu.make_async_remote_copy(
                src_ref=ins[k].at[:, pl.ds((1 - c) * half, half), :], dst_ref=outs[k],
                send_sem=send_sems.at[k], recv_sem=recv_sems.at[k],
                device_id=(x, y, 1 - c), device_id_type=MESH)
            cp.start()
            copies.append(cp)
        for cp in copies:
            cp.wait()

    return pl.pallas_call(
        body, name="grad_pair_exchange_" + tag,
        in_specs=_hbm_specs(n), out_specs=_hbm_specs(n),
        out_shape=[jax.ShapeDtypeStruct((N_CHIPS, g.shape[1] // 2, g.shape[2]), F32) for g in grads],
        scratch_shapes=[pltpu.SemaphoreType.DMA((n,)), pltpu.SemaphoreType.DMA((n,))],
    )(*grads)


def _pair_sum(grad, other, name):
    _, rows, cols = grad.shape
    half = rows // 2
    core = jnp.reshape(lax.axis_index("c"), (1,)).astype(jnp.int32)

    def body(core_ref, g_ref, p_ref, s_ref, sb_ref):
        s = g_ref[...] + p_ref[...]
        s_ref[...] = s
        sb_ref[...] = s.astype(BF16)

    blk = pl.BlockSpec((None, half, cols), lambda p, core_ref: (p, 0, 0))
    return pl.pallas_call(
        body, name=name,
        grid_spec=pltpu.PrefetchScalarGridSpec(
            num_scalar_prefetch=1, grid=(N_CHIPS,),
            in_specs=[pl.BlockSpec((None, half, cols), lambda p, core_ref: (p, core_ref[0], 0)), blk],
            out_specs=[blk, blk]),
        out_shape=[jax.ShapeDtypeStruct((N_CHIPS, half, cols), F32),
                   jax.ShapeDtypeStruct((N_CHIPS, half, cols), BF16)],
        compiler_params=pltpu.CompilerParams(dimension_semantics=("parallel",),
                                             vmem_limit_bytes=_vmem_limit(4 * half * cols * 4)),
    )(core, grad, other)


def _chip_copies(sums_bf16, lands, send_sems, recv_sems):
    x, y, c = _mesh_pos()
    return [pltpu.make_async_remote_copy(
        src_ref=sums_bf16[k].at[2 * px + py], dst_ref=lands[k].at[r],
        send_sem=send_sems.at[3 * k + r], recv_sem=recv_sems.at[3 * k + r],
        device_id=(px, py, c), device_id_type=MESH)
        for k in range(len(sums_bf16)) for r, (px, py) in enumerate(_other_chips(x, y))]


def _chip_sems(n):
    return [pltpu.SemaphoreType.DMA((3 * n,)), pltpu.SemaphoreType.DMA((3 * n,))]


def _chip_landing(sums_bf16):
    return [jax.ShapeDtypeStruct((N_CHIPS - 1,) + s.shape[1:], BF16) for s in sums_bf16]


def _chip_sum(sums_f32, landed, name):
    _, rows, cols = sums_f32.shape
    x, y, c = _mesh_pos()
    idx = jnp.stack([2 * x + y, c]).astype(jnp.int32)

    def body(idx_ref, o_ref, l_ref, out_ref):
        out_ref[...] = ((o_ref[...] + l_ref[0].astype(F32)) + l_ref[1].astype(F32)) + l_ref[2].astype(F32)

    return pl.pallas_call(
        body, name=name,
        grid_spec=pltpu.PrefetchScalarGridSpec(
            num_scalar_prefetch=1, grid=(1,),
            in_specs=[pl.BlockSpec((None, rows, cols), lambda i, idx: (idx[0], 0, 0)),
                      pl.BlockSpec((N_CHIPS - 1, rows, cols), lambda i, idx: (0, 0, 0))],
            out_specs=pl.BlockSpec((rows, cols), lambda i, idx: (idx[1], 0))),
        out_shape=jax.ShapeDtypeStruct((2 * rows, cols), F32),
        compiler_params=pltpu.CompilerParams(vmem_limit_bytes=_vmem_limit(3 * rows * cols * 4)),
    )(idx, sums_f32, landed)


def _halves_to_full(fulls, tag):
    n = len(fulls)

    def body(*refs):
        outs = refs[n:2 * n]
        send_sems, recv_sems = refs[2 * n:]
        x, y, c = _mesh_pos()
        copies = []
        for k in range(n):
            half = fulls[k].shape[0] // 2
            rows = outs[k].at[pl.ds(c * half, half), :]
            cp = pltpu.make_async_remote_copy(
                src_ref=rows, dst_ref=rows, send_sem=send_sems.at[k], recv_sem=recv_sems.at[k],
                device_id=(x, y, 1 - c), device_id_type=MESH)
            cp.start()
            copies.append(cp)
        for k in range(n):
            half = fulls[k].shape[0] // 2
            theirs = outs[k].at[pl.ds((1 - c) * half, half), :]
            pltpu.make_async_remote_copy(
                src_ref=theirs, dst_ref=theirs, send_sem=send_sems.at[k], recv_sem=recv_sems.at[k],
                device_id=(x, y, 1 - c), device_id_type=MESH).wait_recv()
        for cp in copies:
            cp.wait_send()

    return pl.pallas_call(
        body, name="grad_halves_to_full_" + tag,
        in_specs=_hbm_specs(n), out_specs=_hbm_specs(n),
        out_shape=[jax.ShapeDtypeStruct(f.shape, F32) for f in fulls],
        input_output_aliases={k: k for k in range(n)},
        scratch_shapes=[pltpu.SemaphoreType.DMA((n,)), pltpu.SemaphoreType.DMA((n,))],
    )(*fulls)


def _all_sum_small(v):
    rows, cols = v.shape
    n_dev = 8

    def body(v_ref, out_ref, buf, send_sems, recv_sems):
        x, y, c = _mesh_pos()
        me = 4 * x + 2 * y + c
        buf[me] = v_ref[...]
        peers = []
        for r in range(1, n_dev):
            px = 1 - x if r & 4 else x
            py = 1 - y if r & 2 else y
            pc = 1 - c if r & 1 else c
            peers.append((px, py, pc))
        copies = []
        for r, peer in enumerate(peers):
            cp = pltpu.make_async_remote_copy(
                src_ref=v_ref, dst_ref=buf.at[me], send_sem=send_sems.at[r], recv_sem=recv_sems.at[r],
                device_id=peer, device_id_type=MESH)
            cp.start()
            copies.append(cp)
        for r, (px, py, pc) in enumerate(peers):
            pltpu.make_async_remote_copy(
                src_ref=v_ref, dst_ref=buf.at[4 * px + 2 * py + pc], send_sem=send_sems.at[r], recv_sem=recv_sems.at[r],
                device_id=(px, py, pc), device_id_type=MESH).wait_recv()
        for cp in copies:
            cp.wait_send()
        acc = buf[0]
        for d in range(1, n_dev):
            acc = acc + buf[d]
        out_ref[...] = acc
        out_ref[3:4, :] = jnp.broadcast_to(jnp.sum(acc[3:4, :], axis=1, keepdims=True), (1, cols))

    vm = pl.BlockSpec(memory_space=pltpu.VMEM)
    return pl.pallas_call(
        body, name="all_sum_small", in_specs=[vm], out_specs=vm,
        out_shape=jax.ShapeDtypeStruct((rows, cols), F32),
        scratch_shapes=[pltpu.VMEM((n_dev, rows, cols), F32),
                        pltpu.SemaphoreType.DMA((n_dev - 1,)), pltpu.SemaphoreType.DMA((n_dev - 1,))],
    )(v)


def _adamw_math(w, g, m, v):
    m = ADAM_B1 * m + (1.0 - ADAM_B1) * g
    v = ADAM_B2 * v + (1.0 - ADAM_B2) * (g * g)
    m_hat = m / (1.0 - ADAM_B1 ** ADAM_STEP)
    v_hat = v / (1.0 - ADAM_B2 ** ADAM_STEP)
    delta = -ADAM_LR * (m_hat / (jnp.sqrt(v_hat) + ADAM_EPS) + ADAM_WD * w)
    return delta, m, v


def _adamw(w, g, m, v, name):
    rows, cols = w.shape
    tm = rows // 2 if (rows // 2) % 8 == 0 else rows
    return _rowwise(_adamw_math, [w, g, m, v], [], [(cols, F32)] * 3, [], tm=tm, name=name)


def _unshard_cols(gathered):
    n, r, c = gathered.shape
    return jnp.transpose(gathered, (1, 0, 2)).reshape(r, n * c)


def _shard_cols(full):
    r, nc = full.shape
    return jnp.transpose(full.reshape(r, N_CHIPS, nc // N_CHIPS), (1, 0, 2))


LATE = ["w_sb_up", "w_dil_up", "w_out", "w_ffn_in", "w_ffn_out"]


def _swap_middle(slabs4):
    return jnp.stack([slabs4[0], slabs4[2], slabs4[1], slabs4[3]])


def _late_weights(slabs, d_model, d_ff):
    g = dict(zip(LATE, slabs))
    return (_unshard_cols(g["w_sb_up"]), _unshard_cols(g["w_dil_up"]), g["w_out"].reshape(d_model, d_model),
            _unshard_cols(_swap_middle(g["w_ffn_in"])), g["w_ffn_out"].reshape(d_ff, d_model))


def _chip_major(grads, d_model, d_ff):
    row_sharded = {"w_out": d_model // N_CHIPS, "w_ffn_out": d_ff // N_CHIPS}
    out = []
    for k, g in grads.items():
        if k in row_sharded:
            out.append(g.reshape(N_CHIPS, row_sharded[k], g.shape[1]))
        else:
            out.append(_swap_middle(_shard_cols(g)) if k == "w_ffn_in" else _shard_cols(g))
    return out


def _pair_reduce(grads, d_model, d_ff):
    full = _chip_major(grads, d_model, d_ff)
    others = _pair_exchange(full, next(iter(grads)))
    return [_pair_sum(g, o, "grad_pair_sum_" + k) for g, o, k in zip(full, others, grads)]


def _chip_reduce(pair, landed, names):
    halves = [_chip_sum(p[0], l, "grad_chip_sum_" + k) for p, l, k in zip(pair, landed, names)]
    return dict(zip(names, _halves_to_full(halves, names[0])))


def _fwd_bwd(x, loss_target, g_mix, g_ffn, g_fin, wf_in, late_slabs):
    b_sz, s_len, d_model = x.shape
    t = b_sz * s_len
    d_ff = late_slabs[-1].shape[1] * N_CHIPS
    x2d = x.reshape(t, d_model)
    tgt2d = loss_target.reshape(t, d_model)

    (u,) = _rowwise(lambda xv, g: (_rms_stats(xv)[0] * g,), [x2d], [g_mix], [(d_model, BF16)], [], tm=512, name="norm_mix")
    qkv = _mm(u, wf_in, b_cols=(0, QKV_WIDTH), tm=2048, tn=768, tk=d_model, name="proj_qkv")
    gates = _mm(u, wf_in, b_cols=(QKV_WIDTH, 2 * d_model), out_dtype=BF16, tm=t, tn=256, tk=d_model, name="proj_gates")
    qkv3 = qkv.reshape(b_sz, s_len, QKV_WIDTH)
    o_sb, late_slabs = _sb_fwd(qkv3, b_sz, s_len, late_slabs)
    wf_sb_up, wf_dil_up, wf_out, wf_ffn_in, wf_ffn_out = _late_weights(late_slabs, d_model, d_ff)
    o_dl, lse = _dil_fwd(qkv3, b_sz, s_len)
    o_sb2, o_dl2 = o_sb.reshape(t, SB_WIDTH), o_dl.reshape(t, DIL_OUT_WIDTH)
    y_sb = _mm(o_sb2, wf_sb_up, out_dtype=BF16, tm=1024, tn=1024, tk=SB_WIDTH, name="sb_up")
    y_dl = _mm(o_dl2, wf_dil_up, out_dtype=BF16, tm=1024, tn=1024, tk=DIL_OUT_WIDTH, name="dil_up")

    def merge_fn(gt, ys, yd):
        return (_sigmoid(gt[:, :d_model]) * ys + _sigmoid(gt[:, d_model:]) * yd,)

    (merged,) = _rowwise(merge_fn, [gates, y_sb, y_dl], [], [(d_model, BF16)], [], tm=512, name="merge")
    x1 = _mm(merged, wf_out, add=x2d, tm=512, tn=1024, tk=d_model, name="mix_out")
    (u2,) = _rowwise(lambda xv, g: (_rms_stats(xv)[0] * g,), [x1], [g_ffn], [(d_model, BF16)], [], tm=512, name="norm_ffn")
    half_ff = d_ff // 2

    def act_fn(hv):
        gate = hv[:, :half_ff]
        return hv, gate * _sigmoid(gate) * hv[:, half_ff:]

    h, act = _mm(u2, wf_ffn_in, tm=512, tn=d_ff, tk=d_model, name="ffn_in",
                 epilogue=(act_fn, [], [], [(d_ff, BF16), (half_ff, BF16)], []))
    def head_fn(xv, tg, g):
        xhat, r = _rms_stats(xv)
        err = xhat * g - tg
        dy = err * (1.0 / d_model)
        dx, dg_rows = _rms_bwd(dy, xhat, r, g)
        loss_lanes = (0.5 / d_model) * jnp.sum(err * err, axis=0, keepdims=True)
        return dx, dx, jnp.sum(dg_rows, axis=0, keepdims=True), loss_lanes

    dx2, dx2_b, dg_fin, loss_lanes = _mm(
        act, wf_ffn_out, add=x1, tm=512, tn=1024, tk=d_ff, name="ffn_out",
        epilogue=(head_fn, [tgt2d], [g_fin], [(d_model, F32), (d_model, BF16)], [(1, d_model), (1, d_model)]))

    def dact_fn(da, hv):
        gate, up = hv[:, :half_ff], hv[:, half_ff:]
        sg = _sigmoid(gate)
        dgate = da * up * (sg * (1.0 + gate * (1.0 - sg)))
        return (jnp.concatenate([dgate, da * (gate * sg)], axis=1),)

    (dh,) = _mm(dx2_b, wf_ffn_out, tb=True, tm=512, tn=half_ff, tk=d_model, name="ffn_out_dx",
                epilogue=(dact_fn, [h], [], [(d_ff, BF16)], []))
    gw_ffn_out = _mm(act, dx2_b, ta=True, tm=256, tn=d_model, tk=t, name="ffn_out_dw")
    def norm_bwd_fn(du_, dres, xv, g):
        xhat, r = _rms_stats(xv)
        dx, dg_rows = _rms_bwd(du_, xhat, r, g)
        return dres + dx, jnp.sum(dg_rows, axis=0, keepdims=True)

    def norm_bwd_twice(*args):
        dx, dg = norm_bwd_fn(*args)
        return dx, dx, dg

    dx1, dx1_b, dg_ffn = _mm(dh, wf_ffn_in, tb=True, tm=512, tn=1024, tk=2 * d_ff, name="ffn_in_dx",
                             epilogue=(norm_bwd_twice, [dx2, x1], [g_ffn], [(d_model, F32), (d_model, BF16)], [(1, d_model)]))
    gw_ffn_in = _mm(u2, dh, ta=True, tm=d_model, tn=512, tk=t, name="ffn_in_dw")

    dmerged = _mm(dx1_b, wf_out, tb=True, out_dtype=BF16, tm=512, tn=1024, tk=d_model, name="mix_out_dx")
    gw_out = _mm(merged, dx1_b, ta=True, tm=256, tn=d_model, tk=t, name="mix_out_dw")

    def merge_bwd_fn(gt, ys, yd, dm):
        s_sb, s_dl = _sigmoid(gt[:, :d_model]), _sigmoid(gt[:, d_model:])
        dgates = jnp.concatenate([dm * ys * s_sb * (1.0 - s_sb), dm * yd * s_dl * (1.0 - s_dl)], axis=1)
        return dgates, dm * s_sb, dm * s_dl

    dgates, dy_sb, dy_dl = _rowwise(merge_bwd_fn, [gates, y_sb, y_dl, dmerged], [],
                                    [(2 * d_model, BF16), (d_model, BF16), (d_model, BF16)], [], tm=256, name="merge_bwd")
    do_sb = _mm(dy_sb, wf_sb_up, tb=True, out_dtype=BF16, tm=1024, tn=SB_WIDTH, tk=d_model, name="sb_up_dx")
    gw_sb_up = _mm(o_sb2, dy_sb, ta=True, tm=SB_WIDTH, tn=1024, tk=512, name="sb_up_dw")
    do_dl = _mm(dy_dl, wf_dil_up, tb=True, tm=1024, tn=DIL_OUT_WIDTH, tk=d_model, name="dil_up_dx")
    gw_dil_up = _mm(o_dl2, dy_dl, ta=True, tm=DIL_OUT_WIDTH, tn=1024, tk=512, name="dil_up_dw")
    late_grads = {"w_sb_up": gw_sb_up, "w_dil_up": gw_dil_up, "w_out": gw_out, "w_ffn_in": gw_ffn_in, "w_ffn_out": gw_ffn_out}
    pair = _pair_reduce(late_grads, d_model, d_ff)
    (dq_sb, dk_sb, dv_sb), landed = _sb_bwd(qkv3, o_sb, do_sb.reshape(b_sz, s_len, SB_WIDTH), b_sz, s_len,
                                           [p[1] for p in pair])
    dq_dl, dk_dl, dv_dl = _dil_bwd(qkv3, o_dl, lse, do_dl.reshape(b_sz, s_len, DIL_OUT_WIDTH), b_sz, s_len)
    dproj = jnp.concatenate(
        [a.reshape(t, -1) for a in (dq_sb, dk_sb, dv_sb)]
        + [a.reshape(t, -1) for a in (dq_dl, dk_dl, dv_dl)] + [dgates], axis=1)
    gw_in = _mm(u, dproj, ta=True, tm=d_model, tn=256, tk=t, name="proj_dw")
    pair_in = _pair_reduce({"w_in": gw_in}, d_model, d_ff)
    (dx, dg_mix), landed_in = _mm(
        dproj, wf_in, tb=True, tm=512, tn=1024, tk=wf_in.shape[1], name="proj_dx", carried=[p[1] for p in pair_in],
        epilogue=(norm_bwd_fn, [dx1, x2d], [g_mix], [(d_model, F32)], [(1, d_model)]))

    grads = _chip_reduce(pair, landed, LATE)
    grads.update(_chip_reduce(pair_in, landed_in, ["w_in"]))
    return dx, grads, dg_mix, dg_ffn, dg_fin, loss_lanes


def kernel(x, norm_mix_g, w_in, w_sb_up, w_dil_up, w_out, norm_ffn_g, w_ffn_in, w_ffn_out, norm_final_g, loss_target, m_norm_mix_g, m_w_in, m_w_sb_up, m_w_dil_up, m_w_out, m_norm_ffn_g, m_w_ffn_in, m_w_ffn_out, m_norm_final_g, v_norm_mix_g, v_w_in, v_w_sb_up, v_w_dil_up, v_w_out, v_norm_ffn_g, v_w_ffn_in, v_w_ffn_out, v_norm_final_g):
    b_sz, s_len, d_model = x.shape
    d_ff = w_ffn_out.shape[1] * N_CHIPS
    g_mix, g_ffn, g_fin = norm_mix_g, norm_ffn_g, norm_final_g.reshape(1, d_model)

    names = ["w_in", "w_sb_up", "w_dil_up", "w_out", "w_ffn_in", "w_ffn_out"]
    shards = {"w_in": w_in[0], "w_sb_up": w_sb_up[0], "w_dil_up": w_dil_up[0], "w_out": w_out[0],
              "w_ffn_in": w_ffn_in[0], "w_ffn_out": w_ffn_out[0]}
    (slab_in,) = _gather_weights([_cast_to_slab(shards["w_in"], "cast_w_in")])
    late_slabs = [_cast_to_slab(shards[k], "cast_" + k) for k in LATE]

    dx, grads, dg_mix, dg_ffn, dg_fin, loss_lanes = _fwd_bwd(
        x, loss_target, g_mix, g_ffn, g_fin, _unshard_cols(slab_in), late_slabs)

    small = jnp.concatenate([dg_mix, dg_ffn, dg_fin, loss_lanes, jnp.zeros((4, d_model), F32)], axis=0)
    small = _all_sum_small(small)
    loss = small[3, 0]
    gains = jnp.concatenate([g_mix, g_ffn, g_fin, jnp.zeros((5, d_model), F32)], axis=0)
    gains_m = jnp.concatenate([m_norm_mix_g, m_norm_ffn_g, m_norm_final_g.reshape(1, d_model), jnp.zeros((5, d_model), F32)], axis=0)
    gains_v = jnp.concatenate([v_norm_mix_g, v_norm_ffn_g, v_norm_final_g.reshape(1, d_model), jnp.ones((5, d_model), F32)], axis=0)
    gd, gm, gv = _rowwise(_adamw_math, [gains, small, gains_m, gains_v], [], [(d_model, F32)] * 3, [], tm=8, name="adamw_gains")

    moments = {"w_in": (m_w_in, v_w_in), "w_sb_up": (m_w_sb_up, v_w_sb_up), "w_dil_up": (m_w_dil_up, v_w_dil_up),
               "w_out": (m_w_out, v_w_out), "w_ffn_in": (m_w_ffn_in, v_w_ffn_in), "w_ffn_out": (m_w_ffn_out, v_w_ffn_out)}
    upd = {k: _adamw(shards[k], grads[k], moments[k][0][0], moments[k][1][0], "adamw_" + k) for k in names}

    def w_out_of(i):
        return [upd[k][i][None] for k in names]

    def ordered(mix, ws, ffn_g, fin):
        return [mix, ws[0], ws[1], ws[2], ws[3], ffn_g, ws[4], ws[5], fin]

    grad_ws = [grads[k][None] for k in names]
    outs = [loss, dx.reshape(b_sz, s_len, d_model)]
    outs += ordered(small[0:1], grad_ws, small[1:2], small[2])
    outs += ordered(gd[0:1], w_out_of(0), gd[1:2], gd[2])
    outs += ordered(gm[0:1], w_out_of(1), gm[1:2], gm[2])
    outs += ordered(gv[0:1], w_out_of(2), gv[1:2], gv[2])
    return tuple(outs)
```

```python
import functools
import math

import jax
import jax.numpy as jnp
from jax import lax
from jax.experimental import pallas as pl
from jax.experimental.pallas import tpu as pltpu

F32 = jnp.float32
BF16 = jnp.bfloat16
MESH = pl.DeviceIdType.MESH

HEAD_DIM = 64
SB_HEADS = 8
DIL_PAIRS = ((128, 1), (512, 4), (2048, 16))
DIL_HEADS_PER_GROUP = 4
DIL_HEADS = DIL_HEADS_PER_GROUP * len(DIL_PAIRS)
SB_WIDTH = SB_HEADS * HEAD_DIM
DIL_WIDTH = DIL_HEADS * HEAD_DIM
DIL_OUT_WIDTH = DIL_HEADS_PER_GROUP * HEAD_DIM
QKV_WIDTH = 3 * SB_WIDTH + 3 * DIL_WIDTH
RMS_EPS = 1e-6
ALIBI_MAX_BIAS = 8.0
ADAM_LR = 0.001
ADAM_B1 = 0.9
ADAM_B2 = 0.999
ADAM_EPS = 1e-08
ADAM_WD = 0.01
ADAM_STEP = 10

LANES = 128
BLK = 128
NEG = -1e30
EXP_UNDERFLOW = -104.0
SB_FWD_CHAINS = 4
SB_BWD_CHAINS = 4
N_CHIPS = 4
VMEM_CAP = 56 * 1024 * 1024


def _vmem_limit(tile_bytes):
    return int(min(VMEM_CAP, max(32 * 1024 * 1024, 3 * tile_bytes + 8 * 1024 * 1024)))


def _nbytes(shape, dtype):
    return math.prod(shape) * jnp.dtype(dtype).itemsize


def _dot(a, b):
    return jnp.dot(a, b, preferred_element_type=F32)


def _dot_nt(a, b):
    return lax.dot_general(a, b, (((1,), (1,)), ((), ())), preferred_element_type=F32)


def _dot_tn(a, b):
    return lax.dot_general(a, b, (((0,), (0,)), ((), ())), preferred_element_type=F32)


def _split2(x):
    hi = x.astype(BF16)
    lo = (x - hi.astype(F32)).astype(BF16)
    return hi, lo


def _sigmoid(x):
    return 1.0 / (1.0 + jnp.exp(-x))


def _mm(a, b, *, ta=False, tb=False, add=None, out_dtype=F32, tm, tn, tk, name, carried=(), epilogue=None,
        b_cols=None):
    n_car = len(carried)
    if ta:
        kdim, m = a.shape
    else:
        m, kdim = a.shape
    if tb:
        n, k2 = b.shape
    else:
        k2, n = b.shape
    col0 = 0
    if b_cols is not None:
        assert not tb and b_cols[0] % tn == 0, name
        col0, n = b_cols[0] // tn, b_cols[1]
    assert kdim == k2 and m % tm == 0 and n % tn == 0 and kdim % tk == 0, (name, a.shape, b.shape)
    nk = kdim // tk
    grid = (m // tm, n // tn, nk)
    a_mode = dict(pipeline_mode=pl.Buffered(1)) if grid[0] == 1 and nk == 1 else {}
    b_mode = dict(pipeline_mode=pl.Buffered(1)) if grid[1] == 1 and nk == 1 else {}
    a_spec = (pl.BlockSpec((tk, tm), lambda i, j, k: (k, i), **a_mode) if ta
              else pl.BlockSpec((tm, tk), lambda i, j, k: (i, k), **a_mode))
    b_spec = (pl.BlockSpec((tn, tk), lambda i, j, k: (j, k), **b_mode) if tb
              else pl.BlockSpec((tk, tn), lambda i, j, k: (k, j + col0), **b_mode))
    o_spec = pl.BlockSpec((tm, tn), lambda i, j, k: (i, j))
    dims = ((((0,) if ta else (1,)), ((1,) if tb else (0,))), ((), ()))
    has_add = add is not None
    if epilogue is None:
        ep_fn, ep_rows, ep_params, ep_outs, ep_accs = None, [], [], [], []
        out_sds, out_specs = [jax.ShapeDtypeStruct((m, n), out_dtype)], [o_spec]
    else:
        ep_fn, ep_rows, ep_params, ep_outs, ep_accs = epilogue
        assert grid[1] == 1 or not ep_accs, name
        out_sds = [jax.ShapeDtypeStruct((m, w * grid[1]), d) for w, d in ep_outs]
        out_sds += [jax.ShapeDtypeStruct(sh, F32) for sh in ep_accs]
        out_specs = [pl.BlockSpec((tm, w), lambda i, j, k: (i, j)) for w, _ in ep_outs]
        out_specs += [pl.BlockSpec(sh, lambda i, j, k: (0, 0)) for sh in ep_accs]
    n_main = len(out_sds)
    use_scratch = nk > 1 and (ep_fn is not None or jnp.dtype(out_dtype) != jnp.dtype(F32))
    n_in = 2 + has_add + len(ep_rows) + len(ep_params)

    def finish(total, refs, pid):
        outs = refs[n_in + n_car:n_in + n_car + n_main]
        if ep_fn is None:
            outs[0][...] = total.astype(out_dtype)
            return
        first = 2 + has_add
        rows = [r[...].astype(F32) for r in refs[first:first + len(ep_rows)]]
        params = [p[...] for p in refs[first + len(ep_rows):n_in]]
        res = ep_fn(total, *rows, *params)
        for o_ref, v in zip(outs[:len(ep_outs)], res):
            o_ref[...] = v.astype(o_ref.dtype)
        acc_refs = outs[len(ep_outs):]
        if acc_refs:
            @pl.when(pid[0] == 0)
            def _():
                for r in acc_refs:
                    r[...] = jnp.zeros(r.shape, F32)

            for r, v in zip(acc_refs, res[len(ep_outs):]):
                r[...] += v

    def compute(refs, pid):
        a_ref, b_ref = refs[0], refs[1]
        add_ref = refs[2] if has_add else None
        prod = lax.dot_general(a_ref[...].astype(BF16), b_ref[...].astype(BF16), dims, preferred_element_type=F32)
        if nk == 1:
            finish(prod + add_ref[...] if has_add else prod, refs, pid)
            return
        acc_ref = refs[n_in + 2 * n_car + n_main] if use_scratch else refs[n_in + n_car]
        k = pid[2]

        @pl.when(k == 0)
        def _():
            acc_ref[...] = prod + add_ref[...] if has_add else prod

        @pl.when(k > 0)
        def _():
            acc_ref[...] += prod

        if use_scratch:
            @pl.when(k == nk - 1)
            def _():
                finish(acc_ref[...], refs, pid)

    def body(*refs):
        pid = (pl.program_id(0), pl.program_id(1), pl.program_id(2))
        if not n_car:
            compute(refs, pid)
            return
        lands = refs[n_in + n_car + n_main:n_in + 2 * n_car + n_main]
        copies = _chip_copies(refs[n_in:n_in + n_car], lands, *refs[-2:])
        step = (pid[0] * grid[1] + pid[1]) * nk + pid[2]

        @pl.when(step == 0)
        def _():
            for cp in copies:
                cp.start()

        compute(refs, pid)

        @pl.when(step == grid[0] * grid[1] * nk - 1)
        def _():
            for cp in copies:
                cp.wait()

    tile_bytes = (_nbytes((tm, tk), a.dtype) + _nbytes((tk, tn), b.dtype) + 2 * _nbytes((tm, tn), F32)
                  + (_nbytes((tm, tn), F32) if has_add else 0)
                  + sum(_nbytes((tm, r.shape[1]), r.dtype) for r in ep_rows) + sum(_nbytes((tm, w), d) for w, d in ep_outs))
    in_specs = [a_spec, b_spec] + ([o_spec] if has_add else [])
    in_specs += [pl.BlockSpec((tm, r.shape[1] // grid[1]), lambda i, j, k: (i, j)) for r in ep_rows]
    in_specs += [pl.BlockSpec(p.shape, lambda i, j, k: (0, 0)) for p in ep_params]
    args = (a, b) + ((add,) if has_add else ()) + tuple(ep_rows) + tuple(ep_params)
    scratch = [pltpu.VMEM((tm, tn), F32)] if use_scratch else []
    serial = bool(n_car or ep_accs)
    res = pl.pallas_call(
        body, name=name, grid=grid,
        in_specs=in_specs + _hbm_specs(n_car), out_specs=out_specs + _hbm_specs(n_car),
        out_shape=out_sds + _chip_landing(carried), scratch_shapes=scratch + (_chip_sems(n_car) if n_car else []),
        compiler_params=pltpu.CompilerParams(
            dimension_semantics=("arbitrary",) * 3 if serial else ("parallel", "parallel", "arbitrary"),
            vmem_limit_bytes=_vmem_limit(tile_bytes)),
    )(*args, *carried)
    main = res[0] if ep_fn is None else list(res[:n_main])
    return (main, res[n_main:]) if n_car else main


def _rowwise(fn, rows, params, outs, accs, *, tm, name):
    t = rows[0].shape[0]
    assert t % tm == 0, (name, t, tm)
    n_r, n_p, n_o = len(rows), len(params), len(outs)

    def body(*refs):
        vals = [r[...].astype(F32) for r in refs[:n_r]] + [p[...] for p in refs[n_r:n_r + n_p]]
        res = fn(*vals)
        o_refs = refs[n_r + n_p:n_r + n_p + n_o]
        a_refs = refs[n_r + n_p + n_o:]
        for o_ref, v in zip(o_refs, res[:n_o]):
            o_ref[...] = v.astype(o_ref.dtype)
        if accs:
            @pl.when(pl.program_id(0) == 0)
            def _():
                for a_ref in a_refs:
                    a_ref[...] = jnp.zeros(a_ref.shape, F32)

            for a_ref, v in zip(a_refs, res[n_o:]):
                a_ref[...] += v

    in_specs = [pl.BlockSpec((tm, r.shape[1]), lambda i: (i, 0)) for r in rows]
    in_specs += [pl.BlockSpec(p.shape, lambda i: (0, 0)) for p in params]
    out_specs = [pl.BlockSpec((tm, w), lambda i: (i, 0)) for w, _ in outs]
    out_specs += [pl.BlockSpec(s, lambda i: (0, 0)) for s in accs]
    out_shape = [jax.ShapeDtypeStruct((t, w), d) for w, d in outs]
    out_shape += [jax.ShapeDtypeStruct(s, F32) for s in accs]
    tile_bytes = sum(_nbytes((tm, r.shape[1]), r.dtype) for r in rows) + sum(_nbytes((tm, w), F32) for w, _ in outs)
    res = pl.pallas_call(
        body, name=name, grid=(t // tm,), in_specs=in_specs, out_specs=out_specs, out_shape=out_shape,
        compiler_params=pltpu.CompilerParams(
            dimension_semantics=("arbitrary",) if accs else ("parallel",),
            vmem_limit_bytes=_vmem_limit(2 * tile_bytes)),
    )(*rows, *params)
    return res


def _rms_stats(x):
    r = lax.rsqrt(jnp.mean(x * x, axis=-1, keepdims=True) + RMS_EPS)
    return x * r, r


def _rms_bwd(dy, xhat, r, g):
    dxhat = dy * g
    dx = r * (dxhat - xhat * jnp.mean(dxhat * xhat, axis=-1, keepdims=True))
    return dx, dy * xhat


def _sb_consts():
    lane = lax.broadcasted_iota(jnp.int32, (BLK, LANES), 1)
    head0 = lane < HEAD_DIM
    row = lax.broadcasted_iota(jnp.int32, (2 * BLK, BLK), 0) % BLK
    col = lax.broadcasted_iota(jnp.int32, (2 * BLK, BLK), 1)
    causal = col < row
    jj = lax.broadcasted_iota(jnp.int32, (BLK, BLK), 0)
    ss = lax.broadcasted_iota(jnp.int32, (BLK, BLK), 1)
    suffix = jnp.where(jj > ss, 1.0, 0.0).astype(BF16)
    return head0, causal, suffix


def _stack_heads(x, head0):
    zero = jnp.zeros_like(x)
    return jnp.concatenate([jnp.where(head0, x, zero), jnp.where(head0, zero, x)], axis=0)


def _sb_logits(z, causal, masked):
    sp = jnp.log(1.0 + jnp.exp(-jnp.abs(z)))
    log_keep = -(jnp.maximum(z, 0.0) + sp)
    log_beta = jnp.minimum(z, 0.0) - sp
    if masked:
        log_keep = jnp.where(causal, log_keep, 0.0)
    return log_keep, log_beta


def _suffix_sums(x, suffix):
    hi, lo = _split2(x)
    after = _dot(hi, suffix) + _dot(lo, suffix)
    total = jnp.broadcast_to(after[:, 0:1] + x[:, 0:1], x.shape)
    return after, total


def _sb_walk_back(i, state, per_chain, tile):
    def alive(st):
        worst = functools.reduce(jnp.maximum, [st[p][:, 0:1] for p in range(0, len(st), per_chain)])
        return jnp.max(worst) > EXP_UNDERFLOW

    def cond(c):
        return jnp.logical_and(c[0] < i, alive(c[1]))

    def body(c):
        return c[0] + 1, tile(i - 1 - c[0], c[1], False)

    return lax.while_loop(cond, body, (jnp.int32(0), state))[1]


def _lane_blocks(x, n):
    return [x[:, p * LANES:(p + 1) * LANES] for p in range(n)]


def _sb_fwd(qkv, b_sz, s_len, slabs):
    n_slabs = len(slabs)
    nq = s_len // BLK
    n_pairs = SB_WIDTH // LANES
    ch = SB_FWD_CHAINS
    n_steps = n_pairs // ch
    scale = 1.0 / math.sqrt(HEAD_DIM)

    def body(q_ref, k_ref, v_ref, *rest):
        o_ref = rest[n_slabs]
        slab_refs = rest[n_slabs + 1:2 * n_slabs + 1]
        send_sems, recv_sems = rest[2 * n_slabs + 1:]
        step = pl.program_id(0) * n_steps + pl.program_id(1)
        head0, causal, suffix = _sb_consts()

        @pl.when(step == 0)
        def _():
            _gather_issue(slab_refs, send_sems, recv_sems)

        def q_block(i, _):
            qs = pl.multiple_of(i * BLK, BLK)
            q_all = (q_ref[pl.ds(qs, BLK), :] * scale).astype(BF16)
            q01 = [_stack_heads(q, head0) for q in _lane_blocks(q_all, ch)]

            def tile(j, state, masked):
                ks = pl.multiple_of(j * BLK, BLK)
                ks_ = _lane_blocks(k_ref[pl.ds(ks, BLK), :].astype(BF16), ch)
                vs_ = _lane_blocks(v_ref[pl.ds(ks, BLK), :].astype(BF16), ch)
                zs = [_dot_nt(q01[p], ks_[p]) for p in range(ch)]
                logits = [_sb_logits(z, causal, masked) for z in zs]
                sums = [_suffix_sums(lg[0], suffix) for lg in logits]
                out = []
                for p in range(ch):
                    carry, acc = state[2 * p], state[2 * p + 1]
                    after, total = sums[p]
                    a = jnp.exp(logits[p][1] + carry + after)
                    if masked:
                        a = jnp.where(causal, a, 0.0)
                    a_hi, a_lo = _split2(a)
                    a_cat = jnp.concatenate([a_hi[:BLK], a_hi[BLK:], a_lo[:BLK], a_lo[BLK:]], axis=1)
                    v01 = _stack_heads(vs_[p], head0)
                    out += [carry + total, acc + _dot(a_cat, jnp.concatenate([v01, v01], axis=0))]
                return tuple(out)

            state = (jnp.zeros((2 * BLK, BLK), F32), jnp.zeros((BLK, LANES), F32)) * ch
            state = tile(i, state, True)
            state = _sb_walk_back(i, state, 2, tile)
            o_ref[pl.ds(qs, BLK), :] = jnp.concatenate([state[2 * p + 1] for p in range(ch)], axis=1)
            return 0

        lax.fori_loop(0, nq, q_block, 0)

        @pl.when(step == b_sz * n_steps - 1)
        def _():
            _gather_complete(slab_refs, send_sems, recv_sems)

    blk = lambda off: pl.BlockSpec((None, s_len, ch * LANES), lambda b, p: (b, 0, off + p))
    res = pl.pallas_call(
        body, name="sb_fwd", grid=(b_sz, n_steps),
        in_specs=[blk(0), blk(n_steps), blk(2 * n_steps)] + _hbm_specs(n_slabs),
        out_specs=[blk(0)] + _hbm_specs(n_slabs),
        out_shape=[jax.ShapeDtypeStruct((b_sz, s_len, SB_WIDTH), F32)]
        + [jax.ShapeDtypeStruct(a.shape, a.dtype) for a in slabs],
        input_output_aliases={3 + k: 1 + k for k in range(n_slabs)},
        scratch_shapes=_gather_sems(n_slabs),
        compiler_params=pltpu.CompilerParams(dimension_semantics=("arbitrary", "arbitrary"),
                                             vmem_limit_bytes=VMEM_CAP),
    )(qkv, qkv, qkv, *slabs)
    return res[0], res[1:]


def _sb_bwd(qkv, o_sb, do_sb, b_sz, s_len, sums_bf16):
    n_sums = len(sums_bf16)
    nq = s_len // BLK
    n_pairs = SB_WIDTH // LANES
    ch = SB_BWD_CHAINS
    n_steps = n_pairs // ch
    scale = 1.0 / math.sqrt(HEAD_DIM)

    def body(q_ref, k_ref, v_ref, o_ref, do_ref, *rest):
        sum_refs = rest[:n_sums]
        dq_ref, dk_ref, dv_ref = rest[n_sums:n_sums + 3]
        land_refs = rest[n_sums + 3:2 * n_sums + 3]
        dk_acc, dv_acc, send_sems, recv_sems = rest[2 * n_sums + 3:]
        step = pl.program_id(0) * n_steps + pl.program_id(1)
        copies = _chip_copies(sum_refs, land_refs, send_sems, recv_sems)

        @pl.when(step == 0)
        def _():
            for cp in copies:
                cp.start()

        head0, causal, suffix = _sb_consts()
        lrow = lax.broadcasted_iota(jnp.int32, (LANES, LANES), 0)
        ones_h0 = jnp.where(lrow < HEAD_DIM, 1.0, 0.0).astype(BF16)
        ones_h1 = jnp.where(lrow >= HEAD_DIM, 1.0, 0.0).astype(BF16)
        dk_acc[...] = jnp.zeros(dk_acc.shape, F32)
        dv_acc[...] = jnp.zeros(dv_acc.shape, F32)

        def q_block(i, _):
            qs = pl.multiple_of(i * BLK, BLK)
            q_all = (q_ref[pl.ds(qs, BLK), :] * scale).astype(BF16)
            do_all = do_ref[pl.ds(qs, BLK), :].astype(BF16)
            dd_all = do_all.astype(F32) * o_ref[pl.ds(qs, BLK), :]
            q01 = [_stack_heads(q, head0) for q in _lane_blocks(q_all, ch)]
            do01 = [_stack_heads(d, head0) for d in _lane_blocks(do_all, ch)]
            tot = []
            for dd in _lane_blocks(dd_all, ch):
                dd_hi, dd_lo = _split2(dd)
                tot.append(jnp.concatenate([_dot(dd_hi, ones_h0) + _dot(dd_lo, ones_h0),
                                            _dot(dd_hi, ones_h1) + _dot(dd_lo, ones_h1)], axis=0))

            def tile(j, state, masked):
                ks = pl.multiple_of(j * BLK, BLK)
                ks_ = _lane_blocks(k_ref[pl.ds(ks, BLK), :].astype(BF16), ch)
                vs_ = _lane_blocks(v_ref[pl.ds(ks, BLK), :].astype(BF16), ch)
                zs = [_dot_nt(q01[p], ks_[p]) for p in range(ch)]
                das = [_dot_nt(do01[p], vs_[p]) for p in range(ch)]
                logits = [_sb_logits(z, causal, masked) for z in zs]
                sums = [_suffix_sums(lg[0], suffix) for lg in logits]
                a_s, e_s = [], []
                for p in range(ch):
                    a = jnp.exp(logits[p][1] + state[3 * p] + sums[p][0])
                    if masked:
                        a = jnp.where(causal, a, 0.0)
                    a_s.append(a)
                    e_s.append(a * das[p])
                e_sums = [_suffix_sums(e, suffix) for e in e_s]
                out, dks, dvs = [], [], []
                for p in range(ch):
                    carry, rcarry, dq = state[3 * p:3 * p + 3]
                    e = e_s[p]
                    before = tot[p] - (rcarry + e_sums[p][0] + e)
                    beta = jnp.exp(logits[p][1])
                    dz = e * (1.0 - beta) - beta * before
                    if masked:
                        dz = jnp.where(causal, dz, 0.0)
                    dz_b = dz.astype(BF16)
                    dks.append(_dot_tn(dz_b, q01[p]))
                    dvs.append(_dot_tn(a_s[p].astype(BF16), do01[p]))
                    out += [carry + sums[p][1], rcarry + e_sums[p][1], dq + _dot(dz_b, ks_[p])]
                dk_acc[pl.ds(ks, BLK), :] += jnp.concatenate(dks, axis=1)
                dv_acc[pl.ds(ks, BLK), :] += jnp.concatenate(dvs, axis=1)
                return tuple(out)

            state = (jnp.zeros((2 * BLK, BLK), F32),) * (3 * ch)
            state = tile(i, state, True)
            state = _sb_walk_back(i, state, 3, tile)
            dq = [jnp.where(head0, state[3 * p + 2][:BLK], state[3 * p + 2][BLK:]) for p in range(ch)]
            dq_ref[pl.ds(qs, BLK), :] = (jnp.concatenate(dq, axis=1) * scale).astype(dq_ref.dtype)
            return 0

        lax.fori_loop(0, nq, q_block, 0)
        dk_ref[...] = dk_acc[...].astype(dk_ref.dtype)
        dv_ref[...] = dv_acc[...].astype(dv_ref.dtype)

        @pl.when(step == b_sz * n_steps - 1)
        def _():
            for cp in copies:
                cp.wait()

    blk = lambda off: pl.BlockSpec((None, s_len, ch * LANES), lambda b, p: (b, 0, off + p))
    once = lambda off: pl.BlockSpec((None, s_len, ch * LANES), lambda b, p: (b, 0, off + p),
                                    pipeline_mode=pl.Buffered(1))
    out_sd = jax.ShapeDtypeStruct((b_sz, s_len, SB_WIDTH), BF16)
    res = pl.pallas_call(
        body, name="sb_bwd", grid=(b_sz, n_steps),
        in_specs=[once(0), once(n_steps), once(2 * n_steps), once(0), once(0)] + _hbm_specs(n_sums),
        out_specs=[blk(0), blk(0), blk(0)] + _hbm_specs(n_sums),
        out_shape=[out_sd, out_sd, out_sd] + _chip_landing(sums_bf16),
        scratch_shapes=[pltpu.VMEM((s_len, ch * LANES), F32), pltpu.VMEM((s_len, ch * LANES), F32)] + _chip_sems(n_sums),
        compiler_params=pltpu.CompilerParams(dimension_semantics=("arbitrary", "arbitrary"),
                                             vmem_limit_bytes=VMEM_CAP),
    )(qkv, qkv, qkv, o_sb, do_sb, *sums_bf16)
    return res[:3], res[3:]


def _dil_consts(group, pair_idx, dilation):
    lane = lax.broadcasted_iota(jnp.int32, (BLK, LANES), 1)
    head0 = lane < HEAD_DIM
    row = lax.broadcasted_iota(jnp.int32, (2 * BLK, BLK), 0)
    qa = row % BLK
    kb = lax.broadcasted_iota(jnp.int32, (2 * BLK, BLK), 1)
    head = (group * DIL_HEADS_PER_GROUP + 2 * pair_idx + row // BLK).astype(F32)
    slope = jnp.exp((-ALIBI_MAX_BIAS * math.log(2.0) / DIL_HEADS) * (head + 1.0))
    valid_cur = kb <= qa
    valid_prev = kb >= qa
    bias_cur = -slope * ((qa - kb) * dilation).astype(F32)
    bias_prev = -slope * ((BLK + qa - kb) * dilation).astype(F32)
    return head0, valid_cur, valid_prev, bias_cur, bias_prev


def _dil_units(s_len, dilation):
    nb = s_len // dilation // BLK
    return [(r, n) for r in range(dilation) for n in range(nb)]


def _dil_rows(n, r, dilation):
    if dilation == 1:
        return pl.ds(n * BLK, BLK)
    return pl.ds(n * BLK * dilation + r, BLK, stride=dilation)


def _dil_scores(q01, k, bias, valid):
    s = _dot_nt(q01, k) * (1.0 / math.sqrt(HEAD_DIM)) + bias
    return jnp.where(valid, s, NEG)


def _dil_fwd(qkv, b_sz, s_len):
    n_pairs = DIL_OUT_WIDTH // LANES
    q_off = 3 * SB_WIDTH // LANES
    per_kind = DIL_WIDTH // LANES

    def body(*refs):
        qkv_refs = refs[:9]
        o_ref, lse_ref, m_s, l_s = refs[9:]
        pair_idx = pl.program_id(1)
        m_s[...] = jnp.full(m_s.shape, NEG, F32)
        l_s[...] = jnp.zeros(l_s.shape, F32)
        o_ref[...] = jnp.zeros(o_ref.shape, F32)
        for g, (_, dilation) in enumerate(DIL_PAIRS):
            q_ref, k_ref, v_ref = qkv_refs[3 * g:3 * g + 3]
            head0, valid_cur, valid_prev, bias_cur, bias_prev = _dil_consts(g, pair_idx, dilation)
            for r, n in _dil_units(s_len, dilation):
                rows = _dil_rows(n, r, dilation)
                q01 = _stack_heads(q_ref[rows, :].astype(BF16), head0)
                k_c = k_ref[rows, :].astype(BF16)
                v_c = v_ref[rows, :].astype(BF16)
                scores = [_dil_scores(q01, k_c, bias_cur, valid_cur)]
                values = [_stack_heads(v_c, head0)]
                if n > 0:
                    prev = _dil_rows(n - 1, r, dilation)
                    scores.append(_dil_scores(q01, k_ref[prev, :].astype(BF16), bias_prev, valid_prev))
                    values.append(_stack_heads(v_ref[prev, :].astype(BF16), head0))
                m_blk = functools.reduce(jnp.maximum, [jnp.max(s, axis=-1, keepdims=True) for s in scores])
                m_old = jnp.concatenate([m_s.at[0][rows, :], m_s.at[1][rows, :]], axis=0)
                l_old = jnp.concatenate([l_s.at[0][rows, :], l_s.at[1][rows, :]], axis=0)
                m_new = jnp.maximum(m_old, m_blk)
                probs = [jnp.exp(s - m_new) for s in scores]
                l_blk = functools.reduce(jnp.add, [jnp.sum(p, axis=-1, keepdims=True) for p in probs])
                alpha = jnp.exp(m_old - m_new)
                l_new = alpha * l_old + l_blk
                alpha_tok = jnp.where(head0, alpha[:BLK], alpha[BLK:])
                p_cat = jnp.concatenate([h for p in probs for h in (p[:BLK].astype(BF16), p[BLK:].astype(BF16))], axis=1)
                o_ref[rows, :] = alpha_tok * o_ref[rows, :] + _dot(p_cat, jnp.concatenate(values, axis=0))
                m_s.at[0][rows, :] = m_new[:BLK]
                m_s.at[1][rows, :] = m_new[BLK:]
                l_s.at[0][rows, :] = l_new[:BLK]
                l_s.at[1][rows, :] = l_new[BLK:]
        lane = lax.broadcasted_iota(jnp.int32, (BLK, LANES), 1)
        for c in range(s_len // BLK):
            rows = pl.ds(c * BLK, BLK)
            l0, l1 = l_s.at[0][rows, :], l_s.at[1][rows, :]
            o_ref[rows, :] = o_ref[rows, :] / jnp.where(lane < HEAD_DIM, l0, l1)
            lse_ref.at[0][rows, :] = m_s.at[0][rows, :] + jnp.log(l0)
            lse_ref.at[1][rows, :] = m_s.at[1][rows, :] + jnp.log(l1)

    in_specs = []
    for g in range(len(DIL_PAIRS)):
        for kind in range(3):
            off = q_off + kind * per_kind + g * n_pairs
            in_specs.append(pl.BlockSpec((None, s_len, LANES), lambda b, p, off=off: (b, 0, off + p)))
    return pl.pallas_call(
        body, name="dil_fwd", grid=(b_sz, n_pairs),
        in_specs=in_specs,
        out_specs=[pl.BlockSpec((None, s_len, LANES), lambda b, p: (b, 0, p)),
                   pl.BlockSpec((None, None, 2, s_len, LANES), lambda b, p: (b, p, 0, 0, 0))],
        out_shape=[jax.ShapeDtypeStruct((b_sz, s_len, DIL_OUT_WIDTH), F32),
                   jax.ShapeDtypeStruct((b_sz, n_pairs, 2, s_len, LANES), F32)],
        scratch_shapes=[pltpu.VMEM((2, s_len, LANES), F32), pltpu.VMEM((2, s_len, LANES), F32)],
        compiler_params=pltpu.CompilerParams(dimension_semantics=("parallel", "parallel"),
                                             vmem_limit_bytes=VMEM_CAP),
    )(*([qkv] * 9))


def _dil_bwd(qkv, o_dl, lse, do_dl, b_sz, s_len):
    n_pairs = DIL_OUT_WIDTH // LANES
    n_groups = len(DIL_PAIRS)
    q_off = 3 * SB_WIDTH // LANES
    per_kind = DIL_WIDTH // LANES

    def body(q_ref, k_ref, v_ref, o_ref, lse_ref, do_ref, dq_ref, dk_ref, dv_ref, d_s, dq_s, dk_s, dv_s):
        pair_idx = pl.program_id(1)
        group = pl.program_id(2)
        lrow = lax.broadcasted_iota(jnp.int32, (LANES, LANES), 0)
        ones_h0 = jnp.where(lrow < HEAD_DIM, 1.0, 0.0).astype(BF16)
        ones_h1 = jnp.where(lrow >= HEAD_DIM, 1.0, 0.0).astype(BF16)
        for c in range(s_len // BLK):
            rows = pl.ds(c * BLK, BLK)
            dd_hi, dd_lo = _split2(do_ref[rows, :] * o_ref[rows, :])
            d_s.at[0][rows, :] = _dot(dd_hi, ones_h0) + _dot(dd_lo, ones_h0)
            d_s.at[1][rows, :] = _dot(dd_hi, ones_h1) + _dot(dd_lo, ones_h1)
        dk_s[...] = jnp.zeros(dk_s.shape, F32)
        dv_s[...] = jnp.zeros(dv_s.shape, F32)

        def one_group(g, dilation):
            head0, valid_cur, valid_prev, bias_cur, bias_prev = _dil_consts(g, pair_idx, dilation)
            for r, n in _dil_units(s_len, dilation):
                rows = _dil_rows(n, r, dilation)
                q01 = _stack_heads(q_ref[rows, :].astype(BF16), head0)
                do01 = _stack_heads(do_ref[rows, :].astype(BF16), head0)
                lse01 = jnp.concatenate([lse_ref.at[0][rows, :], lse_ref.at[1][rows, :]], axis=0)
                d01 = jnp.concatenate([d_s.at[0][rows, :], d_s.at[1][rows, :]], axis=0)
                dq = jnp.zeros((2 * BLK, LANES), F32)
                blocks = [(rows, bias_cur, valid_cur)]
                if n > 0:
                    blocks.append((_dil_rows(n - 1, r, dilation), bias_prev, valid_prev))
                for krows, bias, valid in blocks:
                    k = k_ref[krows, :].astype(BF16)
                    v = v_ref[krows, :].astype(BF16)
                    p = jnp.exp(_dil_scores(q01, k, bias, valid) - lse01)
                    ds = (p * (_dot_nt(do01, v) - d01) * (1.0 / math.sqrt(HEAD_DIM))).astype(BF16)
                    dq = dq + _dot(ds, k)
                    dk_s[krows, :] = dk_s[krows, :] + _dot_tn(ds, q01)
                    dv_s[krows, :] = dv_s[krows, :] + _dot_tn(p.astype(BF16), do01)
                dq_s[rows, :] = jnp.where(head0, dq[:BLK], dq[BLK:])

        for g, (_, dilation) in enumerate(DIL_PAIRS):
            pl.when(group == g)(functools.partial(one_group, g, dilation))
        dq_ref[...] = dq_s[...].astype(dq_ref.dtype)
        dk_ref[...] = dk_s[...].astype(dk_ref.dtype)
        dv_ref[...] = dv_s[...].astype(dv_ref.dtype)

    def qkv_spec(kind):
        return pl.BlockSpec((None, s_len, LANES),
                            lambda b, p, g: (b, 0, q_off + kind * per_kind + g * n_pairs + p))

    tok_spec = pl.BlockSpec((None, s_len, LANES), lambda b, p, g: (b, 0, p))
    out_spec = pl.BlockSpec((None, s_len, LANES), lambda b, p, g: (b, 0, g * n_pairs + p))
    out_sd = jax.ShapeDtypeStruct((b_sz, s_len, DIL_WIDTH), BF16)
    return pl.pallas_call(
        body, name="dil_bwd", grid=(b_sz, n_pairs, n_groups),
        in_specs=[qkv_spec(0), qkv_spec(1), qkv_spec(2), tok_spec,
                  pl.BlockSpec((None, None, 2, s_len, LANES), lambda b, p, g: (b, p, 0, 0, 0)), tok_spec],
        out_specs=[out_spec, out_spec, out_spec],
        out_shape=[out_sd, out_sd, out_sd],
        scratch_shapes=[pltpu.VMEM((2, s_len, LANES), F32)] + [pltpu.VMEM((s_len, LANES), F32)] * 3,
        compiler_params=pltpu.CompilerParams(dimension_semantics=("parallel", "parallel", "arbitrary"),
                                             vmem_limit_bytes=VMEM_CAP),
    )(qkv, qkv, qkv, o_dl, lse, do_dl)


def _mesh_pos():
    return lax.axis_index("x"), lax.axis_index("y"), lax.axis_index("c")


def _other_chips(x, y):
    return [(1 - x, y), (x, 1 - y), (1 - x, 1 - y)]


def _hbm_specs(n):
    return [pl.BlockSpec(memory_space=pl.ANY)] * n


def _cast_to_slab(w, name):
    rows, cols = w.shape
    mine = jnp.reshape(2 * lax.axis_index("x") + lax.axis_index("y"), (1,)).astype(jnp.int32)

    def body(idx_ref, w_ref, o_ref):
        o_ref[...] = w_ref[...].astype(BF16)

    return pl.pallas_call(
        body, name=name,
        grid_spec=pltpu.PrefetchScalarGridSpec(
            num_scalar_prefetch=1, grid=(1,),
            in_specs=[pl.BlockSpec((rows, cols), lambda i, idx: (0, 0))],
            out_specs=pl.BlockSpec((None, rows, cols), lambda i, idx: (idx[0], 0, 0))),
        out_shape=jax.ShapeDtypeStruct((N_CHIPS, rows, cols), BF16),
        compiler_params=pltpu.CompilerParams(vmem_limit_bytes=_vmem_limit(rows * cols * 6)),
    )(mine, w)


def _gather_issue(slabs, send_sems, recv_sems):
    x, y, c = _mesh_pos()
    for k, slab in enumerate(slabs):
        half = slab.shape[1] // 2
        rows = slab.at[2 * x + y, pl.ds(c * half, half), :]
        for r, (px, py) in enumerate(_other_chips(x, y)):
            pltpu.make_async_remote_copy(
                src_ref=rows, dst_ref=rows, send_sem=send_sems.at[6 * k + r], recv_sem=recv_sems.at[6 * k + r],
                device_id=(px, py, c), device_id_type=MESH).start()


def _gather_complete(slabs, send_sems, recv_sems):
    x, y, c = _mesh_pos()
    chips = _other_chips(x, y)

    def copy(k, sem, block, rows, to):
        ref = slabs[k].at[block, rows, :]
        return pltpu.make_async_remote_copy(
            src_ref=ref, dst_ref=ref, send_sem=send_sems.at[sem], recv_sem=recv_sems.at[sem],
            device_id=to, device_id_type=MESH)

    for k, slab in enumerate(slabs):
        half = slab.shape[1] // 2
        for r, (px, py) in enumerate(chips):
            copy(k, 6 * k + r, 2 * px + py, pl.ds(c * half, half), (px, py, c)).wait_recv()
            copy(k, 6 * k + 3 + r, 2 * px + py, pl.ds(c * half, half), (x, y, 1 - c)).start()
    for k, slab in enumerate(slabs):
        half = slab.shape[1] // 2
        for r, (px, py) in enumerate(chips):
            copy(k, 6 * k + 3 + r, 2 * px + py, pl.ds((1 - c) * half, half), (x, y, 1 - c)).wait_recv()
    for k, slab in enumerate(slabs):
        half = slab.shape[1] // 2
        for r, (px, py) in enumerate(chips):
            copy(k, 6 * k + r, 2 * x + y, pl.ds(c * half, half), (px, py, c)).wait_send()
            copy(k, 6 * k + 3 + r, 2 * px + py, pl.ds(c * half, half), (x, y, 1 - c)).wait_send()


def _gather_sems(n):
    return [pltpu.SemaphoreType.DMA((6 * n,)), pltpu.SemaphoreType.DMA((6 * n,))]


def _gather_weights(slabs):
    n = len(slabs)

    def body(*refs):
        outs = refs[n:2 * n]
        send_sems, recv_sems = refs[2 * n:]
        _gather_issue(outs, send_sems, recv_sems)
        _gather_complete(outs, send_sems, recv_sems)

    return pl.pallas_call(
        body, name="gather_weights",
        in_specs=_hbm_specs(n), out_specs=_hbm_specs(n),
        out_shape=[jax.ShapeDtypeStruct(s.shape, s.dtype) for s in slabs],
        input_output_aliases={k: k for k in range(n)},
        scratch_shapes=_gather_sems(n),
    )(*slabs)


def _pair_exchange(grads, tag):
    n = len(grads)

    def body(*refs):
        ins, outs = refs[:n], refs[n:2 * n]
        send_sems, recv_sems = refs[2 * n:]
        x, y, c = _mesh_pos()
        copies = []
        for k in range(n):
            half = grads[k].shape[1] // 2
            cp = pltpu.make_async_remote_copy(
                src_ref=ins[k].at[:, pl.ds((1 - c) * half, half), :], dst_ref=outs[k],
                send_sem=send_sems.at[k], recv_sem=recv_sems.at[k],
                device_id=(x, y, 1 - c), device_id_type=MESH)
            cp.start()
            copies.append(cp)
        for cp in copies:
            cp.wait()

    return pl.pallas_call(
        body, name="grad_pair_exchange_" + tag,
        in_specs=_hbm_specs(n), out_specs=_hbm_specs(n),
        out_shape=[jax.ShapeDtypeStruct((N_CHIPS, g.shape[1] // 2, g.shape[2]), F32) for g in grads],
        scratch_shapes=[pltpu.SemaphoreType.DMA((n,)), pltpu.SemaphoreType.DMA((n,))],
    )(*grads)


def _pair_sum(grad, other, name):
    _, rows, cols = grad.shape
    half = rows // 2
    core = jnp.reshape(lax.axis_index("c"), (1,)).astype(jnp.int32)

    def body(core_ref, g_ref, p_ref, s_ref, sb_ref):
        s = g_ref[...] + p_ref[...]
        s_ref[...] = s
        sb_ref[...] = s.astype(BF16)

    blk = pl.BlockSpec((None, half, cols), lambda p, core_ref: (p, 0, 0))
    return pl.pallas_call(
        body, name=name,
        grid_spec=pltpu.PrefetchScalarGridSpec(
            num_scalar_prefetch=1, grid=(N_CHIPS,),
            in_specs=[pl.BlockSpec((None, half, cols), lambda p, core_ref: (p, core_ref[0], 0)), blk],
            out_specs=[blk, blk]),
        out_shape=[jax.ShapeDtypeStruct((N_CHIPS, half, cols), F32),
                   jax.ShapeDtypeStruct((N_CHIPS, half, cols), BF16)],
        compiler_params=pltpu.CompilerParams(dimension_semantics=("parallel",),
                                             vmem_limit_bytes=_vmem_limit(4 * half * cols * 4)),
    )(core, grad, other)


def _chip_copies(sums_bf16, lands, send_sems, recv_sems):
    x, y, c = _mesh_pos()
    return [pltpu.make_async_remote_copy(
        src_ref=sums_bf16[k].at[2 * px + py], dst_ref=lands[k].at[r],
        send_sem=send_sems.at[3 * k + r], recv_sem=recv_sems.at[3 * k + r],
        device_id=(px, py, c), device_id_type=MESH)
        for k in range(len(sums_bf16)) for r, (px, py) in enumerate(_other_chips(x, y))]


def _chip_sems(n):
    return [pltpu.SemaphoreType.DMA((3 * n,)), pltpu.SemaphoreType.DMA((3 * n,))]


def _chip_landing(sums_bf16):
    return [jax.ShapeDtypeStruct((N_CHIPS - 1,) + s.shape[1:], BF16) for s in sums_bf16]


def _chip_sum(sums_f32, landed, name):
    _, rows, cols = sums_f32.shape
    x, y, c = _mesh_pos()
    idx = jnp.stack([2 * x + y, c]).astype(jnp.int32)

    def body(idx_ref, o_ref, l_ref, out_ref):
        out_ref[...] = ((o_ref[...] + l_ref[0].astype(F32)) + l_ref[1].astype(F32)) + l_ref[2].astype(F32)

    return pl.pallas_call(
        body, name=name,
        grid_spec=pltpu.PrefetchScalarGridSpec(
            num_scalar_prefetch=1, grid=(1,),
            in_specs=[pl.BlockSpec((None, rows, cols), lambda i, idx: (idx[0], 0, 0)),
                      pl.BlockSpec((N_CHIPS - 1, rows, cols), lambda i, idx: (0, 0, 0))],
            out_specs=pl.BlockSpec((rows, cols), lambda i, idx: (idx[1], 0))),
        out_shape=jax.ShapeDtypeStruct((2 * rows, cols), F32),
        compiler_params=pltpu.CompilerParams(vmem_limit_bytes=_vmem_limit(3 * rows * cols * 4)),
    )(idx, sums_f32, landed)


def _halves_to_full(fulls, tag):
    n = len(fulls)

    def body(*refs):
        outs = refs[n:2 * n]
        send_sems, recv_sems = refs[2 * n:]
        x, y, c = _mesh_pos()
        copies = []
        for k in range(n):
            half = fulls[k].shape[0] // 2
            rows = outs[k].at[pl.ds(c * half, half), :]
            cp = pltpu.make_async_remote_copy(
                src_ref=rows, dst_ref=rows, send_sem=send_sems.at[k], recv_sem=recv_sems.at[k],
                device_id=(x, y, 1 - c), device_id_type=MESH)
            cp.start()
            copies.append(cp)
        for k in range(n):
            half = fulls[k].shape[0] // 2
            theirs = outs[k].at[pl.ds((1 - c) * half, half), :]
            pltpu.make_async_remote_copy(
                src_ref=theirs, dst_ref=theirs, send_sem=send_sems.at[k], recv_sem=recv_sems.at[k],
                device_id=(x, y, 1 - c), device_id_type=MESH).wait_recv()
        for cp in copies:
            cp.wait_send()

    return pl.pallas_call(
        body, name="grad_halves_to_full_" + tag,
        in_specs=_hbm_specs(n), out_specs=_hbm_specs(n),
        out_shape=[jax.ShapeDtypeStruct(f.shape, F32) for f in fulls],
        input_output_aliases={k: k for k in range(n)},
        scratch_shapes=[pltpu.SemaphoreType.DMA((n,)), pltpu.SemaphoreType.DMA((n,))],
    )(*fulls)


def _all_sum_small(v):
    rows, cols = v.shape
    n_dev = 8

    def body(v_ref, out_ref, buf, send_sems, recv_sems):
        x, y, c = _mesh_pos()
        me = 4 * x + 2 * y + c
        buf[me] = v_ref[...]
        peers = []
        for r in range(1, n_dev):
            px = 1 - x if r & 4 else x
            py = 1 - y if r & 2 else y
            pc = 1 - c if r & 1 else c
            peers.append((px, py, pc))
        copies = []
        for r, peer in enumerate(peers):
            cp = pltpu.make_async_remote_copy(
                src_ref=v_ref, dst_ref=buf.at[me], send_sem=send_sems.at[r], recv_sem=recv_sems.at[r],
                device_id=peer, device_id_type=MESH)
            cp.start()
            copies.append(cp)
        for r, (px, py, pc) in enumerate(peers):
            pltpu.make_async_remote_copy(
                src_ref=v_ref, dst_ref=buf.at[4 * px + 2 * py + pc], send_sem=send_sems.at[r], recv_sem=recv_sems.at[r],
                device_id=(px, py, pc), device_id_type=MESH).wait_recv()
        for cp in copies:
            cp.wait_send()
        acc = buf[0]
        for d in range(1, n_dev):
            acc = acc + buf[d]
        out_ref[...] = acc
        out_ref[3:4, :] = jnp.broadcast_to(jnp.sum(acc[3:4, :], axis=1, keepdims=True), (1, cols))

    vm = pl.BlockSpec(memory_space=pltpu.VMEM)
    return pl.pallas_call(
        body, name="all_sum_small", in_specs=[vm], out_specs=vm,
        out_shape=jax.ShapeDtypeStruct((rows, cols), F32),
        scratch_shapes=[pltpu.VMEM((n_dev, rows, cols), F32),
                        pltpu.SemaphoreType.DMA((n_dev - 1,)), pltpu.SemaphoreType.DMA((n_dev - 1,))],
    )(v)


def _adamw_math(w, g, m, v):
    m = ADAM_B1 * m + (1.0 - ADAM_B1) * g
    v = ADAM_B2 * v + (1.0 - ADAM_B2) * (g * g)
    m_hat = m / (1.0 - ADAM_B1 ** ADAM_STEP)
    v_hat = v / (1.0 - ADAM_B2 ** ADAM_STEP)
    delta = -ADAM_LR * (m_hat / (jnp.sqrt(v_hat) + ADAM_EPS) + ADAM_WD * w)
    return delta, m, v


def _adamw(w, g, m, v, name):
    rows, cols = w.shape
    tm = rows // 2 if (rows // 2) % 8 == 0 else rows
    return _rowwise(_adamw_math, [w, g, m, v], [], [(cols, F32)] * 3, [], tm=tm, name=name)


def _unshard_cols(gathered):
    n, r, c = gathered.shape
    return jnp.transpose(gathered, (1, 0, 2)).reshape(r, n * c)


def _shard_cols(full):
    r, nc = full.shape
    return jnp.transpose(full.reshape(r, N_CHIPS, nc // N_CHIPS), (1, 0, 2))


LATE = ["w_sb_up", "w_dil_up", "w_out", "w_ffn_in", "w_ffn_out"]


def _swap_middle(slabs4):
    return jnp.stack([slabs4[0], slabs4[2], slabs4[1], slabs4[3]])


def _late_weights(slabs, d_model, d_ff):
    g = dict(zip(LATE, slabs))
    return (_unshard_cols(g["w_sb_up"]), _unshard_cols(g["w_dil_up"]), g["w_out"].reshape(d_model, d_model),
            _unshard_cols(_swap_middle(g["w_ffn_in"])), g["w_ffn_out"].reshape(d_ff, d_model))


def _chip_major(grads, d_model, d_ff):
    row_sharded = {"w_out": d_model // N_CHIPS, "w_ffn_out": d_ff // N_CHIPS}
    out = []
    for k, g in grads.items():
        if k in row_sharded:
            out.append(g.reshape(N_CHIPS, row_sharded[k], g.shape[1]))
        else:
            out.append(_swap_middle(_shard_cols(g)) if k == "w_ffn_in" else _shard_cols(g))
    return out


def _pair_reduce(grads, d_model, d_ff):
    full = _chip_major(grads, d_model, d_ff)
    others = _pair_exchange(full, next(iter(grads)))
    return [_pair_sum(g, o, "grad_pair_sum_" + k) for g, o, k in zip(full, others, grads)]


def _chip_reduce(pair, landed, names):
    halves = [_chip_sum(p[0], l, "grad_chip_sum_" + k) for p, l, k in zip(pair, landed, names)]
    return dict(zip(names, _halves_to_full(halves, names[0])))


def _fwd_bwd(x, loss_target, g_mix, g_ffn, g_fin, wf_in, late_slabs):
    b_sz, s_len, d_model = x.shape
    t = b_sz * s_len
    d_ff = late_slabs[-1].shape[1] * N_CHIPS
    x2d = x.reshape(t, d_model)
    tgt2d = loss_target.reshape(t, d_model)

    (u,) = _rowwise(lambda xv, g: (_rms_stats(xv)[0] * g,), [x2d], [g_mix], [(d_model, BF16)], [], tm=512, name="norm_mix")
    qkv = _mm(u, wf_in, b_cols=(0, QKV_WIDTH), tm=2048, tn=768, tk=d_model, name="proj_qkv")
    gates = _mm(u, wf_in, b_cols=(QKV_WIDTH, 2 * d_model), out_dtype=BF16, tm=t, tn=256, tk=d_model, name="proj_gates")
    qkv3 = qkv.reshape(b_sz, s_len, QKV_WIDTH)
    o_sb, late_slabs = _sb_fwd(qkv3, b_sz, s_len, late_slabs)
    wf_sb_up, wf_dil_up, wf_out, wf_ffn_in, wf_ffn_out = _late_weights(late_slabs, d_model, d_ff)
    o_dl, lse = _dil_fwd(qkv3, b_sz, s_len)
    o_sb2, o_dl2 = o_sb.reshape(t, SB_WIDTH), o_dl.reshape(t, DIL_OUT_WIDTH)
    y_sb = _mm(o_sb2, wf_sb_up, out_dtype=BF16, tm=1024, tn=1024, tk=SB_WIDTH, name="sb_up")
    y_dl = _mm(o_dl2, wf_dil_up, out_dtype=BF16, tm=1024, tn=1024, tk=DIL_OUT_WIDTH, name="dil_up")

    def merge_fn(gt, ys, yd):
        return (_sigmoid(gt[:, :d_model]) * ys + _sigmoid(gt[:, d_model:]) * yd,)

    (merged,) = _rowwise(merge_fn, [gates, y_sb, y_dl], [], [(d_model, BF16)], [], tm=512, name="merge")
    x1 = _mm(merged, wf_out, add=x2d, tm=512, tn=1024, tk=d_model, name="mix_out")
    (u2,) = _rowwise(lambda xv, g: (_rms_stats(xv)[0] * g,), [x1], [g_ffn], [(d_model, BF16)], [], tm=512, name="norm_ffn")
    half_ff = d_ff // 2

    def act_fn(hv):
        gate = hv[:, :half_ff]
        return hv, gate * _sigmoid(gate) * hv[:, half_ff:]

    h, act = _mm(u2, wf_ffn_in, tm=512, tn=d_ff, tk=d_model, name="ffn_in",
                 epilogue=(act_fn, [], [], [(d_ff, BF16), (half_ff, BF16)], []))
    def head_fn(xv, tg, g):
        xhat, r = _rms_stats(xv)
        err = xhat * g - tg
        dy = err * (1.0 / d_model)
        dx, dg_rows = _rms_bwd(dy, xhat, r, g)
        loss_lanes = (0.5 / d_model) * jnp.sum(err * err, axis=0, keepdims=True)
        return dx, dx, jnp.sum(dg_rows, axis=0, keepdims=True), loss_lanes

    dx2, dx2_b, dg_fin, loss_lanes = _mm(
        act, wf_ffn_out, add=x1, tm=512, tn=1024, tk=d_ff, name="ffn_out",
        epilogue=(head_fn, [tgt2d], [g_fin], [(d_model, F32), (d_model, BF16)], [(1, d_model), (1, d_model)]))

    def dact_fn(da, hv):
        gate, up = hv[:, :half_ff], hv[:, half_ff:]
        sg = _sigmoid(gate)
        dgate = da * up * (sg * (1.0 + gate * (1.0 - sg)))
        return (jnp.concatenate([dgate, da * (gate * sg)], axis=1),)

    (dh,) = _mm(dx2_b, wf_ffn_out, tb=True, tm=512, tn=half_ff, tk=d_model, name="ffn_out_dx",
                epilogue=(dact_fn, [h], [], [(d_ff, BF16)], []))
    gw_ffn_out = _mm(act, dx2_b, ta=True, tm=256, tn=d_model, tk=t, name="ffn_out_dw")
    def norm_bwd_fn(du_, dres, xv, g):
        xhat, r = _rms_stats(xv)
        dx, dg_rows = _rms_bwd(du_, xhat, r, g)
        return dres + dx, jnp.sum(dg_rows, axis=0, keepdims=True)

    def norm_bwd_twice(*args):
        dx, dg = norm_bwd_fn(*args)
        return dx, dx, dg

    dx1, dx1_b, dg_ffn = _mm(dh, wf_ffn_in, tb=True, tm=512, tn=1024, tk=2 * d_ff, name="ffn_in_dx",
                             epilogue=(norm_bwd_twice, [dx2, x1], [g_ffn], [(d_model, F32), (d_model, BF16)], [(1, d_model)]))
    gw_ffn_in = _mm(u2, dh, ta=True, tm=d_model, tn=512, tk=t, name="ffn_in_dw")

    dmerged = _mm(dx1_b, wf_out, tb=True, out_dtype=BF16, tm=512, tn=1024, tk=d_model, name="mix_out_dx")
    gw_out = _mm(merged, dx1_b, ta=True, tm=256, tn=d_model, tk=t, name="mix_out_dw")

    def merge_bwd_fn(gt, ys, yd, dm):
        s_sb, s_dl = _sigmoid(gt[:, :d_model]), _sigmoid(gt[:, d_model:])
        dgates = jnp.concatenate([dm * ys * s_sb * (1.0 - s_sb), dm * yd * s_dl * (1.0 - s_dl)], axis=1)
        return dgates, dm * s_sb, dm * s_dl

    dgates, dy_sb, dy_dl = _rowwise(merge_bwd_fn, [gates, y_sb, y_dl, dmerged], [],
                                    [(2 * d_model, BF16), (d_model, BF16), (d_model, BF16)], [], tm=256, name="merge_bwd")
    do_sb = _mm(dy_sb, wf_sb_up, tb=True, out_dtype=BF16, tm=1024, tn=SB_WIDTH, tk=d_model, name="sb_up_dx")
    gw_sb_up = _mm(o_sb2, dy_sb, ta=True, tm=SB_WIDTH, tn=1024, tk=512, name="sb_up_dw")
    do_dl = _mm(dy_dl, wf_dil_up, tb=True, tm=1024, tn=DIL_OUT_WIDTH, tk=d_model, name="dil_up_dx")
    gw_dil_up = _mm(o_dl2, dy_dl, ta=True, tm=DIL_OUT_WIDTH, tn=1024, tk=512, name="dil_up_dw")
    late_grads = {"w_sb_up": gw_sb_up, "w_dil_up": gw_dil_up, "w_out": gw_out, "w_ffn_in": gw_ffn_in, "w_ffn_out": gw_ffn_out}
    pair = _pair_reduce(late_grads, d_model, d_ff)
    (dq_sb, dk_sb, dv_sb), landed = _sb_bwd(qkv3, o_sb, do_sb.reshape(b_sz, s_len, SB_WIDTH), b_sz, s_len,
                                           [p[1] for p in pair])
    dq_dl, dk_dl, dv_dl = _dil_bwd(qkv3, o_dl, lse, do_dl.reshape(b_sz, s_len, DIL_OUT_WIDTH), b_sz, s_len)
    dproj = jnp.concatenate(
        [a.reshape(t, -1) for a in (dq_sb, dk_sb, dv_sb)]
        + [a.reshape(t, -1) for a in (dq_dl, dk_dl, dv_dl)] + [dgates], axis=1)
    gw_in = _mm(u, dproj, ta=True, tm=d_model, tn=256, tk=t, name="proj_dw")
    pair_in = _pair_reduce({"w_in": gw_in}, d_model, d_ff)
    (dx, dg_mix), landed_in = _mm(
        dproj, wf_in, tb=True, tm=512, tn=1024, tk=wf_in.shape[1], name="proj_dx", carried=[p[1] for p in pair_in],
        epilogue=(norm_bwd_fn, [dx1, x2d], [g_mix], [(d_model, F32)], [(1, d_model)]))

    grads = _chip_reduce(pair, landed, LATE)
    grads.update(_chip_reduce(pair_in, landed_in, ["w_in"]))
    return dx, grads, dg_mix, dg_ffn, dg_fin, loss_lanes


def kernel(x, norm_mix_g, w_in, w_sb_up, w_dil_up, w_out, norm_ffn_g, w_ffn_in, w_ffn_out, norm_final_g, loss_target, m_norm_mix_g, m_w_in, m_w_sb_up, m_w_dil_up, m_w_out, m_norm_ffn_g, m_w_ffn_in, m_w_ffn_out, m_norm_final_g, v_norm_mix_g, v_w_in, v_w_sb_up, v_w_dil_up, v_w_out, v_norm_ffn_g, v_w_ffn_in, v_w_ffn_out, v_norm_final_g):
    b_sz, s_len, d_model = x.shape
    d_ff = w_ffn_out.shape[1] * N_CHIPS
    g_mix, g_ffn, g_fin = norm_mix_g, norm_ffn_g, norm_final_g.reshape(1, d_model)

    names = ["w_in", "w_sb_up", "w_dil_up", "w_out", "w_ffn_in", "w_ffn_out"]
    shards = {"w_in": w_in[0], "w_sb_up": w_sb_up[0], "w_dil_up": w_dil_up[0], "w_out": w_out[0],
              "w_ffn_in": w_ffn_in[0], "w_ffn_out": w_ffn_out[0]}
    (slab_in,) = _gather_weights([_cast_to_slab(shards["w_in"], "cast_w_in")])
    late_slabs = [_cast_to_slab(shards[k], "cast_" + k) for k in LATE]

    dx, grads, dg_mix, dg_ffn, dg_fin, loss_lanes = _fwd_bwd(
        x, loss_target, g_mix, g_ffn, g_fin, _unshard_cols(slab_in), late_slabs)

    small = jnp.concatenate([dg_mix, dg_ffn, dg_fin, loss_lanes, jnp.zeros((4, d_model), F32)], axis=0)
    small = _all_sum_small(small)
    loss = small[3, 0]
    gains = jnp.concatenate([g_mix, g_ffn, g_fin, jnp.zeros((5, d_model), F32)], axis=0)
    gains_m = jnp.concatenate([m_norm_mix_g, m_norm_ffn_g, m_norm_final_g.reshape(1, d_model), jnp.zeros((5, d_model), F32)], axis=0)
    gains_v = jnp.concatenate([v_norm_mix_g, v_norm_ffn_g, v_norm_final_g.reshape(1, d_model), jnp.ones((5, d_model), F32)], axis=0)
    gd, gm, gv = _rowwise(_adamw_math, [gains, small, gains_m, gains_v], [], [(d_model, F32)] * 3, [], tm=8, name="adamw_gains")

    moments = {"w_in": (m_w_in, v_w_in), "w_sb_up": (m_w_sb_up, v_w_sb_up), "w_dil_up": (m_w_dil_up, v_w_dil_up),
               "w_out": (m_w_out, v_w_out), "w_ffn_in": (m_w_ffn_in, v_w_ffn_in), "w_ffn_out": (m_w_ffn_out, v_w_ffn_out)}
    upd = {k: _adamw(shards[k], grads[k], moments[k][0][0], moments[k][1][0], "adamw_" + k) for k in names}

    def w_out_of(i):
        return [upd[k][i][None] for k in names]

    def ordered(mix, ws, ffn_g, fin):
        return [mix, ws[0], ws[1], ws[2], ws[3], ffn_g, ws[4], ws[5], fin]

    grad_ws = [grads[k][None] for k in names]
    outs = [loss, dx.reshape(b_sz, s_len, d_model)]
    outs += ordered(small[0:1], grad_ws, small[1:2], small[2])
    outs += ordered(gd[0:1], w_out_of(0), gd[1:2], gd[2])
    outs += ordered(gm[0:1], w_out_of(1), gm[1:2], gm[2])
    outs += ordered(gv[0:1], w_out_of(2), gv[1:2], gv[2])
    return tuple(outs)
```

```python
import functools
import math

import jax
import jax.numpy as jnp
from jax import lax
from jax.experimental import pallas as pl
from jax.experimental.pallas import tpu as pltpu

F32 = jnp.float32
BF16 = jnp.bfloat16
MESH = pl.DeviceIdType.MESH

HEAD_DIM = 64
SB_HEADS = 8
DIL_PAIRS = ((128, 1), (512, 4), (2048, 16))
DIL_HEADS_PER_GROUP = 4
DIL_HEADS = DIL_HEADS_PER_GROUP * len(DIL_PAIRS)
SB_WIDTH = SB_HEADS * HEAD_DIM
DIL_WIDTH = DIL_HEADS * HEAD_DIM
DIL_OUT_WIDTH = DIL_HEADS_PER_GROUP * HEAD_DIM
QKV_WIDTH = 3 * SB_WIDTH + 3 * DIL_WIDTH
RMS_EPS = 1e-6
ALIBI_MAX_BIAS = 8.0
ADAM_LR = 0.001
ADAM_B1 = 0.9
ADAM_B2 = 0.999
ADAM_EPS = 1e-08
ADAM_WD = 0.01
ADAM_STEP = 10

LANES = 128
BLK = 128
NEG = -1e30
EXP_UNDERFLOW = -104.0
SB_FWD_CHAINS = 4
SB_BWD_CHAINS = 4
N_CHIPS = 4
VMEM_CAP = 56 * 1024 * 1024


def _vmem_limit(tile_bytes):
    return int(min(VMEM_CAP, max(32 * 1024 * 1024, 3 * tile_bytes + 8 * 1024 * 1024)))


def _nbytes(shape, dtype):
    return math.prod(shape) * jnp.dtype(dtype).itemsize


def _dot(a, b):
    return jnp.dot(a, b, preferred_element_type=F32)


def _dot_nt(a, b):
    return lax.dot_general(a, b, (((1,), (1,)), ((), ())), preferred_element_type=F32)


def _dot_tn(a, b):
    return lax.dot_general(a, b, (((0,), (0,)), ((), ())), preferred_element_type=F32)


def _split2(x):
    hi = x.astype(BF16)
    lo = (x - hi.astype(F32)).astype(BF16)
    return hi, lo


def _sigmoid(x):
    return 1.0 / (1.0 + jnp.exp(-x))


class _Carry:
    def __init__(self, arrays=(), out_shapes=(), aliased=False, sems=(), start=None, finish=None):
        self.arrays, self.out_shapes, self.aliased = list(arrays), list(out_shapes), aliased
        self.sems, self.start, self.finish = list(sems), start, finish

    def __bool__(self):
        return bool(self.arrays)

    def call_args(self, n_in, n_out):
        aliases = {n_in + k: n_out + k for k in range(len(self.arrays))} if self.aliased else {}
        return _hbm_specs(len(self.arrays)), _hbm_specs(len(self.out_shapes)), self.out_shapes, aliases, self.sems

    def run(self, refs, n_in, n_out, step, n_steps, compute):
        if not self:
            compute()
            return
        n_c, n_o, n_s = len(self.arrays), len(self.out_shapes), len(self.sems)
        ins = refs[n_in:n_in + n_c]
        outs = refs[n_in + n_c + n_out:n_in + n_c + n_out + n_o]
        sems = refs[len(refs) - n_s:]

        @pl.when(step == 0)
        def _():
            self.start(ins, outs, sems)

        compute()

        @pl.when(step == n_steps - 1)
        def _():
            self.finish(ins, outs, sems)


def _mm(a, b, *, ta=False, tb=False, add=None, out_dtype=F32, tm, tn, tk, name, carry=None, epilogue=None,
        b_cols=None):
    carry = carry or _Carry()
    n_car = len(carry.arrays)
    if ta:
        kdim, m = a.shape
    else:
        m, kdim = a.shape
    if tb:
        n, k2 = b.shape
    else:
        k2, n = b.shape
    col0 = 0
    if b_cols is not None:
        assert not tb and b_cols[0] % tn == 0, name
        col0, n = b_cols[0] // tn, b_cols[1]
    assert kdim == k2 and m % tm == 0 and n % tn == 0 and kdim % tk == 0, (name, a.shape, b.shape)
    nk = kdim // tk
    grid = (m // tm, n // tn, nk)
    a_mode = dict(pipeline_mode=pl.Buffered(1)) if grid[0] == 1 and nk == 1 else {}
    b_mode = dict(pipeline_mode=pl.Buffered(1)) if grid[1] == 1 and nk == 1 else {}
    a_spec = (pl.BlockSpec((tk, tm), lambda i, j, k: (k, i), **a_mode) if ta
              else pl.BlockSpec((tm, tk), lambda i, j, k: (i, k), **a_mode))
    b_spec = (pl.BlockSpec((tn, tk), lambda i, j, k: (j, k), **b_mode) if tb
              else pl.BlockSpec((tk, tn), lambda i, j, k: (k, j + col0), **b_mode))
    o_spec = pl.BlockSpec((tm, tn), lambda i, j, k: (i, j))
    dims = ((((0,) if ta else (1,)), ((1,) if tb else (0,))), ((), ()))
    has_add = add is not None
    if epilogue is None:
        ep_fn, ep_rows, ep_params, ep_outs, ep_accs = None, [], [], [], []
        out_sds, out_specs = [jax.ShapeDtypeStruct((m, n), out_dtype)], [o_spec]
    else:
        ep_fn, ep_rows, ep_params, ep_outs, ep_accs = epilogue
        assert grid[1] == 1 or not ep_accs, name
        out_sds = [jax.ShapeDtypeStruct((m, w * grid[1]), d) for w, d in ep_outs]
        out_sds += [jax.ShapeDtypeStruct(sh, F32) for sh in ep_accs]
        out_specs = [pl.BlockSpec((tm, w), lambda i, j, k: (i, j)) for w, _ in ep_outs]
        out_specs += [pl.BlockSpec(sh, lambda i, j, k: (0, 0)) for sh in ep_accs]
    n_main = len(out_sds)
    use_scratch = nk > 1 and (ep_fn is not None or jnp.dtype(out_dtype) != jnp.dtype(F32))
    n_in = 2 + has_add + len(ep_rows) + len(ep_params)

    def finish(total, refs, pid):
        outs = refs[n_in + n_car:n_in + n_car + n_main]
        if ep_fn is None:
            outs[0][...] = total.astype(out_dtype)
            return
        first = 2 + has_add
        rows = [r[...].astype(F32) for r in refs[first:first + len(ep_rows)]]
        params = [p[...] for p in refs[first + len(ep_rows):n_in]]
        res = ep_fn(total, *rows, *params)
        for o_ref, v in zip(outs[:len(ep_outs)], res):
            o_ref[...] = v.astype(o_ref.dtype)
        acc_refs = outs[len(ep_outs):]
        if acc_refs:
            @pl.when(pid[0] == 0)
            def _():
                for r in acc_refs:
                    r[...] = jnp.zeros(r.shape, F32)

            for r, v in zip(acc_refs, res[len(ep_outs):]):
                r[...] += v

    def compute(refs, pid):
        a_ref, b_ref = refs[0], refs[1]
        add_ref = refs[2] if has_add else None
        prod = lax.dot_general(a_ref[...].astype(BF16), b_ref[...].astype(BF16), dims, preferred_element_type=F32)
        if nk == 1:
            finish(prod + add_ref[...] if has_add else prod, refs, pid)
            return
        acc_ref = refs[n_in + n_car + n_main + len(carry.out_shapes)] if use_scratch else refs[n_in + n_car]
        k = pid[2]

        @pl.when(k == 0)
        def _():
            acc_ref[...] = prod + add_ref[...] if has_add else prod

        @pl.when(k > 0)
        def _():
            acc_ref[...] += prod

        if use_scratch:
            @pl.when(k == nk - 1)
            def _():
                finish(acc_ref[...], refs, pid)

    def body(*refs):
        pid = (pl.program_id(0), pl.program_id(1), pl.program_id(2))
        step = (pid[0] * grid[1] + pid[1]) * nk + pid[2]
        carry.run(refs, n_in, n_main, step, grid[0] * grid[1] * nk, lambda: compute(refs, pid))

    tile_bytes = (_nbytes((tm, tk), a.dtype) + _nbytes((tk, tn), b.dtype) + 2 * _nbytes((tm, tn), F32)
                  + (_nbytes((tm, tn), F32) if has_add else 0)
                  + sum(_nbytes((tm, r.shape[1]), r.dtype) for r in ep_rows) + sum(_nbytes((tm, w), d) for w, d in ep_outs))
    in_specs = [a_spec, b_spec] + ([o_spec] if has_add else [])
    in_specs += [pl.BlockSpec((tm, r.shape[1] // grid[1]), lambda i, j, k: (i, j)) for r in ep_rows]
    in_specs += [pl.BlockSpec(p.shape, lambda i, j, k: (0, 0)) for p in ep_params]
    args = (a, b) + ((add,) if has_add else ()) + tuple(ep_rows) + tuple(ep_params)
    scratch = [pltpu.VMEM((tm, tn), F32)] if use_scratch else []
    serial = bool(carry) or bool(ep_accs)
    c_in, c_out, c_shapes, c_alias, c_sems = carry.call_args(n_in, n_main)
    res = pl.pallas_call(
        body, name=name, grid=grid,
        in_specs=in_specs + c_in, out_specs=out_specs + c_out, out_shape=out_sds + c_shapes,
        input_output_aliases=c_alias, scratch_shapes=scratch + c_sems,
        compiler_params=pltpu.CompilerParams(
            dimension_semantics=("arbitrary",) * 3 if serial else ("parallel", "parallel", "arbitrary"),
            vmem_limit_bytes=_vmem_limit(tile_bytes)),
    )(*args, *carry.arrays)
    main = res[0] if ep_fn is None else list(res[:n_main])
    return (main, res[n_main:]) if carry else main


def _rowwise(fn, rows, params, outs, accs, *, tm, name):
    t = rows[0].shape[0]
    assert t % tm == 0, (name, t, tm)
    n_r, n_p, n_o = len(rows), len(params), len(outs)

    def body(*refs):
        vals = [r[...].astype(F32) for r in refs[:n_r]] + [p[...] for p in refs[n_r:n_r + n_p]]
        res = fn(*vals)
        o_refs = refs[n_r + n_p:n_r + n_p + n_o]
        a_refs = refs[n_r + n_p + n_o:]
        for o_ref, v in zip(o_refs, res[:n_o]):
            o_ref[...] = v.astype(o_ref.dtype)
        if accs:
            @pl.when(pl.program_id(0) == 0)
            def _():
                for a_ref in a_refs:
                    a_ref[...] = jnp.zeros(a_ref.shape, F32)

            for a_ref, v in zip(a_refs, res[n_o:]):
                a_ref[...] += v

    in_specs = [pl.BlockSpec((tm, r.shape[1]), lambda i: (i, 0)) for r in rows]
    in_specs += [pl.BlockSpec(p.shape, lambda i: (0, 0)) for p in params]
    out_specs = [pl.BlockSpec((tm, w), lambda i: (i, 0)) for w, _ in outs]
    out_specs += [pl.BlockSpec(s, lambda i: (0, 0)) for s in accs]
    out_shape = [jax.ShapeDtypeStruct((t, w), d) for w, d in outs]
    out_shape += [jax.ShapeDtypeStruct(s, F32) for s in accs]
    tile_bytes = sum(_nbytes((tm, r.shape[1]), r.dtype) for r in rows) + sum(_nbytes((tm, w), F32) for w, _ in outs)
    res = pl.pallas_call(
        body, name=name, grid=(t // tm,), in_specs=in_specs, out_specs=out_specs, out_shape=out_shape,
        compiler_params=pltpu.CompilerParams(
            dimension_semantics=("arbitrary",) if accs else ("parallel",),
            vmem_limit_bytes=_vmem_limit(2 * tile_bytes)),
    )(*rows, *params)
    return res


def _rms_stats(x):
    r = lax.rsqrt(jnp.mean(x * x, axis=-1, keepdims=True) + RMS_EPS)
    return x * r, r


def _rms_bwd(dy, xhat, r, g):
    dxhat = dy * g
    dx = r * (dxhat - xhat * jnp.mean(dxhat * xhat, axis=-1, keepdims=True))
    return dx, dy * xhat


def _sb_consts():
    lane = lax.broadcasted_iota(jnp.int32, (BLK, LANES), 1)
    head0 = lane < HEAD_DIM
    row = lax.broadcasted_iota(jnp.int32, (2 * BLK, BLK), 0) % BLK
    col = lax.broadcasted_iota(jnp.int32, (2 * BLK, BLK), 1)
    causal = col < row
    jj = lax.broadcasted_iota(jnp.int32, (BLK, BLK), 0)
    ss = lax.broadcasted_iota(jnp.int32, (BLK, BLK), 1)
    suffix = jnp.where(jj > ss, 1.0, 0.0).astype(BF16)
    return head0, causal, suffix


def _stack_heads(x, head0):
    zero = jnp.zeros_like(x)
    return jnp.concatenate([jnp.where(head0, x, zero), jnp.where(head0, zero, x)], axis=0)


def _sb_logits(z, causal, masked):
    sp = jnp.log(1.0 + jnp.exp(-jnp.abs(z)))
    log_keep = -(jnp.maximum(z, 0.0) + sp)
    log_beta = jnp.minimum(z, 0.0) - sp
    if masked:
        log_keep = jnp.where(causal, log_keep, 0.0)
    return log_keep, log_beta


def _suffix_sums(x, suffix):
    hi, lo = _split2(x)
    after = _dot(hi, suffix) + _dot(lo, suffix)
    total = jnp.broadcast_to(after[:, 0:1] + x[:, 0:1], x.shape)
    return after, total


def _sb_walk_back(i, state, per_chain, tile):
    def alive(st):
        worst = functools.reduce(jnp.maximum, [st[p][:, 0:1] for p in range(0, len(st), per_chain)])
        return jnp.max(worst) > EXP_UNDERFLOW

    def cond(c):
        return jnp.logical_and(c[0] < i, alive(c[1]))

    def body(c):
        return c[0] + 1, tile(i - 1 - c[0], c[1], False)

    return lax.while_loop(cond, body, (jnp.int32(0), state))[1]


def _lane_blocks(x, n):
    return [x[:, p * LANES:(p + 1) * LANES] for p in range(n)]


def _sb_fwd(qkv, b_sz, s_len, carry):
    nq = s_len // BLK
    n_pairs = SB_WIDTH // LANES
    ch = SB_FWD_CHAINS
    n_steps = n_pairs // ch
    scale = 1.0 / math.sqrt(HEAD_DIM)

    def compute(q_ref, k_ref, v_ref, o_ref):
        head0, causal, suffix = _sb_consts()

        def q_block(i, _):
            qs = pl.multiple_of(i * BLK, BLK)
            q_all = (q_ref[pl.ds(qs, BLK), :] * scale).astype(BF16)
            q01 = [_stack_heads(q, head0) for q in _lane_blocks(q_all, ch)]

            def tile(j, state, masked):
                ks = pl.multiple_of(j * BLK, BLK)
                ks_ = _lane_blocks(k_ref[pl.ds(ks, BLK), :].astype(BF16), ch)
                vs_ = _lane_blocks(v_ref[pl.ds(ks, BLK), :].astype(BF16), ch)
                zs = [_dot_nt(q01[p], ks_[p]) for p in range(ch)]
                logits = [_sb_logits(z, causal, masked) for z in zs]
                sums = [_suffix_sums(lg[0], suffix) for lg in logits]
                out = []
                for p in range(ch):
                    carry, acc = state[2 * p], state[2 * p + 1]
                    after, total = sums[p]
                    a = jnp.exp(logits[p][1] + carry + after)
                    if masked:
                        a = jnp.where(causal, a, 0.0)
                    a_hi, a_lo = _split2(a)
                    a_cat = jnp.concatenate([a_hi[:BLK], a_hi[BLK:], a_lo[:BLK], a_lo[BLK:]], axis=1)
                    v01 = _stack_heads(vs_[p], head0)
                    out += [carry + total, acc + _dot(a_cat, jnp.concatenate([v01, v01], axis=0))]
                return tuple(out)

            state = (jnp.zeros((2 * BLK, BLK), F32), jnp.zeros((BLK, LANES), F32)) * ch
            state = tile(i, state, True)
            state = _sb_walk_back(i, state, 2, tile)
            o_ref[pl.ds(qs, BLK), :] = jnp.concatenate([state[2 * p + 1] for p in range(ch)], axis=1)
            return 0

        lax.fori_loop(0, nq, q_block, 0)

    def body(*refs):
        step = pl.program_id(0) * n_steps + pl.program_id(1)
        o_ref = refs[3 + len(carry.arrays)]
        carry.run(refs, 3, 1, step, b_sz * n_steps, lambda: compute(refs[0], refs[1], refs[2], o_ref))

    blk = lambda off: pl.BlockSpec((None, s_len, ch * LANES), lambda b, p: (b, 0, off + p))
    c_in, c_out, c_shapes, c_alias, c_sems = carry.call_args(3, 1)
    res = pl.pallas_call(
        body, name="sb_fwd", grid=(b_sz, n_steps),
        in_specs=[blk(0), blk(n_steps), blk(2 * n_steps)] + c_in, out_specs=[blk(0)] + c_out,
        out_shape=[jax.ShapeDtypeStruct((b_sz, s_len, SB_WIDTH), F32)] + c_shapes,
        input_output_aliases=c_alias, scratch_shapes=c_sems,
        compiler_params=pltpu.CompilerParams(dimension_semantics=("arbitrary", "arbitrary"),
                                             vmem_limit_bytes=VMEM_CAP),
    )(qkv, qkv, qkv, *carry.arrays)
    return res[0], res[1:]


def _sb_bwd(qkv, o_sb, do_sb, b_sz, s_len, carry):
    nq = s_len // BLK
    n_pairs = SB_WIDTH // LANES
    ch = SB_BWD_CHAINS
    n_steps = n_pairs // ch
    scale = 1.0 / math.sqrt(HEAD_DIM)

    def compute(q_ref, k_ref, v_ref, o_ref, do_ref, dq_ref, dk_ref, dv_ref, dk_acc, dv_acc):
        head0, causal, suffix = _sb_consts()
        lrow = lax.broadcasted_iota(jnp.int32, (LANES, LANES), 0)
        ones_h0 = jnp.where(lrow < HEAD_DIM, 1.0, 0.0).astype(BF16)
        ones_h1 = jnp.where(lrow >= HEAD_DIM, 1.0, 0.0).astype(BF16)
        dk_acc[...] = jnp.zeros(dk_acc.shape, F32)
        dv_acc[...] = jnp.zeros(dv_acc.shape, F32)

        def q_block(i, _):
            qs = pl.multiple_of(i * BLK, BLK)
            q_all = (q_ref[pl.ds(qs, BLK), :] * scale).astype(BF16)
            do_all = do_ref[pl.ds(qs, BLK), :].astype(BF16)
            dd_all = do_all.astype(F32) * o_ref[pl.ds(qs, BLK), :]
            q01 = [_stack_heads(q, head0) for q in _lane_blocks(q_all, ch)]
            do01 = [_stack_heads(d, head0) for d in _lane_blocks(do_all, ch)]
            tot = []
            for dd in _lane_blocks(dd_all, ch):
                dd_hi, dd_lo = _split2(dd)
                tot.append(jnp.concatenate([_dot(dd_hi, ones_h0) + _dot(dd_lo, ones_h0),
                                            _dot(dd_hi, ones_h1) + _dot(dd_lo, ones_h1)], axis=0))

            def tile(j, state, masked):
                ks = pl.multiple_of(j * BLK, BLK)
                ks_ = _lane_blocks(k_ref[pl.ds(ks, BLK), :].astype(BF16), ch)
                vs_ = _lane_blocks(v_ref[pl.ds(ks, BLK), :].astype(BF16), ch)
                zs = [_dot_nt(q01[p], ks_[p]) for p in range(ch)]
                das = [_dot_nt(do01[p], vs_[p]) for p in range(ch)]
                logits = [_sb_logits(z, causal, masked) for z in zs]
                sums = [_suffix_sums(lg[0], suffix) for lg in logits]
                a_s, e_s = [], []
                for p in range(ch):
                    a = jnp.exp(logits[p][1] + state[3 * p] + sums[p][0])
                    if masked:
                        a = jnp.where(causal, a, 0.0)
                    a_s.append(a)
                    e_s.append(a * das[p])
                e_sums = [_suffix_sums(e, suffix) for e in e_s]
                out, dks, dvs = [], [], []
                for p in range(ch):
                    carry, rcarry, dq = state[3 * p:3 * p + 3]
                    e = e_s[p]
                    before = tot[p] - (rcarry + e_sums[p][0] + e)
                    beta = jnp.exp(logits[p][1])
                    dz = e * (1.0 - beta) - beta * before
                    if masked:
                        dz = jnp.where(causal, dz, 0.0)
                    dz_b = dz.astype(BF16)
                    dks.append(_dot_tn(dz_b, q01[p]))
                    dvs.append(_dot_tn(a_s[p].astype(BF16), do01[p]))
                    out += [carry + sums[p][1], rcarry + e_sums[p][1], dq + _dot(dz_b, ks_[p])]
                dk_acc[pl.ds(ks, BLK), :] += jnp.concatenate(dks, axis=1)
                dv_acc[pl.ds(ks, BLK), :] += jnp.concatenate(dvs, axis=1)
                return tuple(out)

            state = (jnp.zeros((2 * BLK, BLK), F32),) * (3 * ch)
            state = tile(i, state, True)
            state = _sb_walk_back(i, state, 3, tile)
            dq = [jnp.where(head0, state[3 * p + 2][:BLK], state[3 * p + 2][BLK:]) for p in range(ch)]
            dq_ref[pl.ds(qs, BLK), :] = (jnp.concatenate(dq, axis=1) * scale).astype(dq_ref.dtype)
            return 0

        lax.fori_loop(0, nq, q_block, 0)
        dk_ref[...] = dk_acc[...].astype(dk_ref.dtype)
        dv_ref[...] = dv_acc[...].astype(dv_ref.dtype)

    def body(*refs):
        step = pl.program_id(0) * n_steps + pl.program_id(1)
        n_c, n_o = len(carry.arrays), len(carry.out_shapes)
        own = refs[:5] + refs[5 + n_c:8 + n_c] + refs[8 + n_c + n_o:10 + n_c + n_o]
        carry.run(refs, 5, 3, step, b_sz * n_steps, lambda: compute(*own))

    blk = lambda off: pl.BlockSpec((None, s_len, ch * LANES), lambda b, p: (b, 0, off + p))
    once = lambda off: pl.BlockSpec((None, s_len, ch * LANES), lambda b, p: (b, 0, off + p),
                                    pipeline_mode=pl.Buffered(1))
    out_sd = jax.ShapeDtypeStruct((b_sz, s_len, SB_WIDTH), BF16)
    c_in, c_out, c_shapes, c_alias, c_sems = carry.call_args(5, 3)
    res = pl.pallas_call(
        body, name="sb_bwd", grid=(b_sz, n_steps),
        in_specs=[once(0), once(n_steps), once(2 * n_steps), once(0), once(0)] + c_in,
        out_specs=[blk(0), blk(0), blk(0)] + c_out, out_shape=[out_sd, out_sd, out_sd] + c_shapes,
        input_output_aliases=c_alias,
        scratch_shapes=[pltpu.VMEM((s_len, ch * LANES), F32), pltpu.VMEM((s_len, ch * LANES), F32)] + c_sems,
        compiler_params=pltpu.CompilerParams(dimension_semantics=("arbitrary", "arbitrary"),
                                             vmem_limit_bytes=VMEM_CAP),
    )(qkv, qkv, qkv, o_sb, do_sb, *carry.arrays)
    return res[:3], res[3:]


def _dil_consts(group, pair_idx, dilation):
    lane = lax.broadcasted_iota(jnp.int32, (BLK, LANES), 1)
    head0 = lane < HEAD_DIM
    row = lax.broadcasted_iota(jnp.int32, (2 * BLK, BLK), 0)
    qa = row % BLK
    kb = lax.broadcasted_iota(jnp.int32, (2 * BLK, BLK), 1)
    head = (group * DIL_HEADS_PER_GROUP + 2 * pair_idx + row // BLK).astype(F32)
    slope = jnp.exp((-ALIBI_MAX_BIAS * math.log(2.0) / DIL_HEADS) * (head + 1.0))
    valid_cur = kb <= qa
    valid_prev = kb >= qa
    bias_cur = -slope * ((qa - kb) * dilation).astype(F32)
    bias_prev = -slope * ((BLK + qa - kb) * dilation).astype(F32)
    return head0, valid_cur, valid_prev, bias_cur, bias_prev


def _dil_units(s_len, dilation):
    nb = s_len // dilation // BLK
    return [(r, n) for r in range(dilation) for n in range(nb)]


def _dil_rows(n, r, dilation):
    if dilation == 1:
        return pl.ds(n * BLK, BLK)
    return pl.ds(n * BLK * dilation + r, BLK, stride=dilation)


def _dil_scores(q01, k, bias, valid):
    s = _dot_nt(q01, k) * (1.0 / math.sqrt(HEAD_DIM)) + bias
    return jnp.where(valid, s, NEG)


def _dil_fwd(qkv, b_sz, s_len, carry):
    n_pairs = DIL_OUT_WIDTH // LANES
    q_off = 3 * SB_WIDTH // LANES
    per_kind = DIL_WIDTH // LANES

    def compute(pair_idx, qkv_refs, o_ref, lse_ref, m_s, l_s):
        m_s[...] = jnp.full(m_s.shape, NEG, F32)
        l_s[...] = jnp.zeros(l_s.shape, F32)
        o_ref[...] = jnp.zeros(o_ref.shape, F32)
        for g, (_, dilation) in enumerate(DIL_PAIRS):
            q_ref, k_ref, v_ref = qkv_refs[3 * g:3 * g + 3]
            head0, valid_cur, valid_prev, bias_cur, bias_prev = _dil_consts(g, pair_idx, dilation)
            for r, n in _dil_units(s_len, dilation):
                rows = _dil_rows(n, r, dilation)
                q01 = _stack_heads(q_ref[rows, :].astype(BF16), head0)
                k_c = k_ref[rows, :].astype(BF16)
                v_c = v_ref[rows, :].astype(BF16)
                scores = [_dil_scores(q01, k_c, bias_cur, valid_cur)]
                values = [_stack_heads(v_c, head0)]
                if n > 0:
                    prev = _dil_rows(n - 1, r, dilation)
                    scores.append(_dil_scores(q01, k_ref[prev, :].astype(BF16), bias_prev, valid_prev))
                    values.append(_stack_heads(v_ref[prev, :].astype(BF16), head0))
                m_blk = functools.reduce(jnp.maximum, [jnp.max(s, axis=-1, keepdims=True) for s in scores])
                m_old = jnp.concatenate([m_s.at[0][rows, :], m_s.at[1][rows, :]], axis=0)
                l_old = jnp.concatenate([l_s.at[0][rows, :], l_s.at[1][rows, :]], axis=0)
                m_new = jnp.maximum(m_old, m_blk)
                probs = [jnp.exp(s - m_new) for s in scores]
                l_blk = functools.reduce(jnp.add, [jnp.sum(p, axis=-1, keepdims=True) for p in probs])
                alpha = jnp.exp(m_old - m_new)
                l_new = alpha * l_old + l_blk
                alpha_tok = jnp.where(head0, alpha[:BLK], alpha[BLK:])
                p_cat = jnp.concatenate([h for p in probs for h in (p[:BLK].astype(BF16), p[BLK:].astype(BF16))], axis=1)
                o_ref[rows, :] = alpha_tok * o_ref[rows, :] + _dot(p_cat, jnp.concatenate(values, axis=0))
                m_s.at[0][rows, :] = m_new[:BLK]
                m_s.at[1][rows, :] = m_new[BLK:]
                l_s.at[0][rows, :] = l_new[:BLK]
                l_s.at[1][rows, :] = l_new[BLK:]
        lane = lax.broadcasted_iota(jnp.int32, (BLK, LANES), 1)
        for c in range(s_len // BLK):
            rows = pl.ds(c * BLK, BLK)
            l0, l1 = l_s.at[0][rows, :], l_s.at[1][rows, :]
            o_ref[rows, :] = o_ref[rows, :] / jnp.where(lane < HEAD_DIM, l0, l1)
            lse_ref.at[0][rows, :] = m_s.at[0][rows, :] + jnp.log(l0)
            lse_ref.at[1][rows, :] = m_s.at[1][rows, :] + jnp.log(l1)

    def body(*refs):
        pair_idx = pl.program_id(1)
        step = pl.program_id(0) * n_pairs + pair_idx
        n_c, n_o = len(carry.arrays), len(carry.out_shapes)
        o_ref, lse_ref = refs[9 + n_c:11 + n_c]
        m_s, l_s = refs[11 + n_c + n_o:13 + n_c + n_o]
        carry.run(refs, 9, 2, step, b_sz * n_pairs, lambda: compute(pair_idx, refs[:9], o_ref, lse_ref, m_s, l_s))

    in_specs = []
    for g in range(len(DIL_PAIRS)):
        for kind in range(3):
            off = q_off + kind * per_kind + g * n_pairs
            in_specs.append(pl.BlockSpec((None, s_len, LANES), lambda b, p, off=off: (b, 0, off + p)))
    c_in, c_out, c_shapes, c_alias, c_sems = carry.call_args(9, 2)
    res = pl.pallas_call(
        body, name="dil_fwd", grid=(b_sz, n_pairs),
        in_specs=in_specs + c_in,
        out_specs=[pl.BlockSpec((None, s_len, LANES), lambda b, p: (b, 0, p)),
                   pl.BlockSpec((None, None, 2, s_len, LANES), lambda b, p: (b, p, 0, 0, 0))] + c_out,
        out_shape=[jax.ShapeDtypeStruct((b_sz, s_len, DIL_OUT_WIDTH), F32),
                   jax.ShapeDtypeStruct((b_sz, n_pairs, 2, s_len, LANES), F32)] + c_shapes,
        input_output_aliases=c_alias,
        scratch_shapes=[pltpu.VMEM((2, s_len, LANES), F32), pltpu.VMEM((2, s_len, LANES), F32)] + c_sems,
        compiler_params=pltpu.CompilerParams(dimension_semantics=("arbitrary", "arbitrary"),
                                             vmem_limit_bytes=VMEM_CAP),
    )(*([qkv] * 9), *carry.arrays)
    return res[0], res[1], res[2:]


def _dil_bwd(qkv, o_dl, lse, do_dl, b_sz, s_len, carry):
    n_pairs = DIL_OUT_WIDTH // LANES
    n_groups = len(DIL_PAIRS)
    q_off = 3 * SB_WIDTH // LANES
    per_kind = DIL_WIDTH // LANES

    def compute(pair_idx, group, q_ref, k_ref, v_ref, o_ref, lse_ref, do_ref, dq_ref, dk_ref, dv_ref, d_s, dq_s, dk_s, dv_s):
        lrow = lax.broadcasted_iota(jnp.int32, (LANES, LANES), 0)
        ones_h0 = jnp.where(lrow < HEAD_DIM, 1.0, 0.0).astype(BF16)
        ones_h1 = jnp.where(lrow >= HEAD_DIM, 1.0, 0.0).astype(BF16)
        for c in range(s_len // BLK):
            rows = pl.ds(c * BLK, BLK)
            dd_hi, dd_lo = _split2(do_ref[rows, :] * o_ref[rows, :])
            d_s.at[0][rows, :] = _dot(dd_hi, ones_h0) + _dot(dd_lo, ones_h0)
            d_s.at[1][rows, :] = _dot(dd_hi, ones_h1) + _dot(dd_lo, ones_h1)
        dk_s[...] = jnp.zeros(dk_s.shape, F32)
        dv_s[...] = jnp.zeros(dv_s.shape, F32)

        def one_group(g, dilation):
            head0, valid_cur, valid_prev, bias_cur, bias_prev = _dil_consts(g, pair_idx, dilation)
            for r, n in _dil_units(s_len, dilation):
                rows = _dil_rows(n, r, dilation)
                q01 = _stack_heads(q_ref[rows, :].astype(BF16), head0)
                do01 = _stack_heads(do_ref[rows, :].astype(BF16), head0)
                lse01 = jnp.concatenate([lse_ref.at[0][rows, :], lse_ref.at[1][rows, :]], axis=0)
                d01 = jnp.concatenate([d_s.at[0][rows, :], d_s.at[1][rows, :]], axis=0)
                dq = jnp.zeros((2 * BLK, LANES), F32)
                blocks = [(rows, bias_cur, valid_cur)]
                if n > 0:
                    blocks.append((_dil_rows(n - 1, r, dilation), bias_prev, valid_prev))
                for krows, bias, valid in blocks:
                    k = k_ref[krows, :].astype(BF16)
                    v = v_ref[krows, :].astype(BF16)
                    p = jnp.exp(_dil_scores(q01, k, bias, valid) - lse01)
                    ds = (p * (_dot_nt(do01, v) - d01) * (1.0 / math.sqrt(HEAD_DIM))).astype(BF16)
                    dq = dq + _dot(ds, k)
                    dk_s[krows, :] = dk_s[krows, :] + _dot_tn(ds, q01)
                    dv_s[krows, :] = dv_s[krows, :] + _dot_tn(p.astype(BF16), do01)
                dq_s[rows, :] = jnp.where(head0, dq[:BLK], dq[BLK:])

        for g, (_, dilation) in enumerate(DIL_PAIRS):
            pl.when(group == g)(functools.partial(one_group, g, dilation))
        dq_ref[...] = dq_s[...].astype(dq_ref.dtype)
        dk_ref[...] = dk_s[...].astype(dk_ref.dtype)
        dv_ref[...] = dv_s[...].astype(dv_ref.dtype)

    def body(*refs):
        pair_idx, group = pl.program_id(1), pl.program_id(2)
        step = (pl.program_id(0) * n_pairs + pair_idx) * n_groups + group
        n_c, n_o = len(carry.arrays), len(carry.out_shapes)
        own = refs[:6] + refs[6 + n_c:9 + n_c] + refs[9 + n_c + n_o:13 + n_c + n_o]
        carry.run(refs, 6, 3, step, b_sz * n_pairs * n_groups, lambda: compute(pair_idx, group, *own))

    def qkv_spec(kind):
        return pl.BlockSpec((None, s_len, LANES),
                            lambda b, p, g: (b, 0, q_off + kind * per_kind + g * n_pairs + p))

    tok_spec = pl.BlockSpec((None, s_len, LANES), lambda b, p, g: (b, 0, p))
    out_spec = pl.BlockSpec((None, s_len, LANES), lambda b, p, g: (b, 0, g * n_pairs + p))
    out_sd = jax.ShapeDtypeStruct((b_sz, s_len, DIL_WIDTH), BF16)
    c_in, c_out, c_shapes, c_alias, c_sems = carry.call_args(6, 3)
    res = pl.pallas_call(
        body, name="dil_bwd", grid=(b_sz, n_pairs, n_groups),
        in_specs=[qkv_spec(0), qkv_spec(1), qkv_spec(2), tok_spec,
                  pl.BlockSpec((None, None, 2, s_len, LANES), lambda b, p, g: (b, p, 0, 0, 0)), tok_spec] + c_in,
        out_specs=[out_spec, out_spec, out_spec] + c_out,
        out_shape=[out_sd, out_sd, out_sd] + c_shapes,
        input_output_aliases=c_alias,
        scratch_shapes=[pltpu.VMEM((2, s_len, LANES), F32)] + [pltpu.VMEM((s_len, LANES), F32)] * 3 + c_sems,
        compiler_params=pltpu.CompilerParams(dimension_semantics=("arbitrary", "arbitrary", "arbitrary"),
                                             vmem_limit_bytes=VMEM_CAP),
    )(qkv, qkv, qkv, o_dl, lse, do_dl, *carry.arrays)
    return res[:3], res[3:]


def _mesh_pos():
    return lax.axis_index("x"), lax.axis_index("y"), lax.axis_index("c")


def _other_chips(x, y):
    return [(1 - x, y), (x, 1 - y), (1 - x, 1 - y)]


def _hbm_specs(n):
    return [pl.BlockSpec(memory_space=pl.ANY)] * n


def _cast_to_slab(w, name):
    rows, cols = w.shape
    mine = jnp.reshape(2 * lax.axis_index("x") + lax.axis_index("y"), (1,)).astype(jnp.int32)

    def body(idx_ref, w_ref, o_ref):
        o_ref[...] = w_ref[...].astype(BF16)

    return pl.pallas_call(
        body, name=name,
        grid_spec=pltpu.PrefetchScalarGridSpec(
            num_scalar_prefetch=1, grid=(1,),
            in_specs=[pl.BlockSpec((rows, cols), lambda i, idx: (0, 0))],
            out_specs=pl.BlockSpec((None, rows, cols), lambda i, idx: (idx[0], 0, 0))),
        out_shape=jax.ShapeDtypeStruct((N_CHIPS, rows, cols), BF16),
        compiler_params=pltpu.CompilerParams(vmem_limit_bytes=_vmem_limit(rows * cols * 6)),
    )(mine, w)


def _gather_issue(slabs, send_sems, recv_sems):
    x, y, c = _mesh_pos()
    for k, slab in enumerate(slabs):
        half = slab.shape[1] // 2
        rows = slab.at[2 * x + y, pl.ds(c * half, half), :]
        for r, (px, py) in enumerate(_other_chips(x, y)):
            pltpu.make_async_remote_copy(
                src_ref=rows, dst_ref=rows, send_sem=send_sems.at[6 * k + r], recv_sem=recv_sems.at[6 * k + r],
                device_id=(px, py, c), device_id_type=MESH).start()


def _gather_complete(slabs, send_sems, recv_sems):
    x, y, c = _mesh_pos()
    chips = _other_chips(x, y)

    def copy(k, sem, block, rows, to):
        ref = slabs[k].at[block, rows, :]
        return pltpu.make_async_remote_copy(
            src_ref=ref, dst_ref=ref, send_sem=send_sems.at[sem], recv_sem=recv_sems.at[sem],
            device_id=to, device_id_type=MESH)

    for k, slab in enumerate(slabs):
        half = slab.shape[1] // 2
        for r, (px, py) in enumerate(chips):
            copy(k, 6 * k + r, 2 * px + py, pl.ds(c * half, half), (px, py, c)).wait_recv()
            copy(k, 6 * k + 3 + r, 2 * px + py, pl.ds(c * half, half), (x, y, 1 - c)).start()
    for k, slab in enumerate(slabs):
        half = slab.shape[1] // 2
        for r, (px, py) in enumerate(chips):
            copy(k, 6 * k + 3 + r, 2 * px + py, pl.ds((1 - c) * half, half), (x, y, 1 - c)).wait_recv()
    for k, slab in enumerate(slabs):
        half = slab.shape[1] // 2
        for r, (px, py) in enumerate(chips):
            copy(k, 6 * k + r, 2 * x + y, pl.ds(c * half, half), (px, py, c)).wait_send()
            copy(k, 6 * k + 3 + r, 2 * px + py, pl.ds(c * half, half), (x, y, 1 - c)).wait_send()


def _gather_sems(n):
    return [pltpu.SemaphoreType.DMA((6 * n,)), pltpu.SemaphoreType.DMA((6 * n,))]


def _gather_carry(slabs):
    return _Carry(slabs, [jax.ShapeDtypeStruct(a.shape, a.dtype) for a in slabs], True, _gather_sems(len(slabs)),
                  lambda ins, outs, sems: _gather_issue(outs, *sems),
                  lambda ins, outs, sems: _gather_complete(outs, *sems))


def _gather_weights(slabs):
    n = len(slabs)

    def body(*refs):
        outs = refs[n:2 * n]
        send_sems, recv_sems = refs[2 * n:]
        _gather_issue(outs, send_sems, recv_sems)
        _gather_complete(outs, send_sems, recv_sems)

    return pl.pallas_call(
        body, name="gather_weights",
        in_specs=_hbm_specs(n), out_specs=_hbm_specs(n),
        out_shape=[jax.ShapeDtypeStruct(s.shape, s.dtype) for s in slabs],
        input_output_aliases={k: k for k in range(n)},
        scratch_shapes=_gather_sems(n),
    )(*slabs)


def _pair_exchange(grads, tag):
    n = len(grads)

    def body(*refs):
        ins, outs = refs[:n], refs[n:2 * n]
        send_sems, recv_sems = refs[2 * n:]
        x, y, c = _mesh_pos()
        copies = []
        for k in range(n):
            half = grads[k].shape[1] // 2
            cp = pltpu.make_async_remote_copy(
                src_ref=ins[k].at[:, pl.ds((1 - c) * half, half), :], dst_ref=outs[k],
                send_sem=send_sems.at[k], recv_sem=recv_sems.at[k],
                device_id=(x, y, 1 - c), device_id_type=MESH)
            cp.start()
            copies.append(cp)
        for cp in copies:
            cp.wait()

    return pl.pallas_call(
        body, name="grad_pair_exchange_" + tag,
        in_specs=_hbm_specs(n), out_specs=_hbm_specs(n),
        out_shape=[jax.ShapeDtypeStruct((N_CHIPS, g.shape[1] // 2, g.shape[2]), F32) for g in grads],
        scratch_shapes=[pltpu.SemaphoreType.DMA((n,)), pltpu.SemaphoreType.DMA((n,))],
    )(*grads)


def _pair_sum(grad, other, name):
    _, rows, cols = grad.shape
    half = rows // 2
    core = jnp.reshape(lax.axis_index("c"), (1,)).astype(jnp.int32)

    def body(core_ref, g_ref, p_ref, s_ref, sb_ref):
        s = g_ref[...] + p_ref[...]
        s_ref[...] = s
        sb_ref[...] = s.astype(BF16)

    blk = pl.BlockSpec((None, half, cols), lambda p, core_ref: (p, 0, 0))
    return pl.pallas_call(
        body, name=name,
        grid_spec=pltpu.PrefetchScalarGridSpec(
            num_scalar_prefetch=1, grid=(N_CHIPS,),
            in_specs=[pl.BlockSpec((None, half, cols), lambda p, core_ref: (p, core_ref[0], 0)), blk],
            out_specs=[blk, blk]),
        out_shape=[jax.ShapeDtypeStruct((N_CHIPS, half, cols), F32),
                   jax.ShapeDtypeStruct((N_CHIPS, half, cols), BF16)],
        compiler_params=pltpu.CompilerParams(dimension_semantics=("parallel",),
                                             vmem_limit_bytes=_vmem_limit(4 * half * cols * 4)),
    )(core, grad, other)


def _chip_copies(sums_bf16, lands, send_sems, recv_sems):
    x, y, c = _mesh_pos()
    return [pltpu.make_async_remote_copy(
        src_ref=sums_bf16[k].at[2 * px + py], dst_ref=lands[k].at[r],
        send_sem=send_sems.at[3 * k + r], recv_sem=recv_sems.at[3 * k + r],
        device_id=(px, py, c), device_id_type=MESH)
        for k in range(len(sums_bf16)) for r, (px, py) in enumerate(_other_chips(x, y))]


def _chip_carry(sums_bf16):
    def start(ins, outs, sems):
        for cp in _chip_copies(ins, outs, *sems):
            cp.start()

    def finish(ins, outs, sems):
        for cp in _chip_copies(ins, outs, *sems):
            cp.wait()

    return _Carry(sums_bf16, _chip_landing(sums_bf16), False, _chip_sems(len(sums_bf16)), start, finish)


def _chip_sems(n):
    return [pltpu.SemaphoreType.DMA((3 * n,)), pltpu.SemaphoreType.DMA((3 * n,))]


def _chip_landing(sums_bf16):
    return [jax.ShapeDtypeStruct((N_CHIPS - 1,) + s.shape[1:], BF16) for s in sums_bf16]


def _chip_sum(sums_f32, landed, name):
    _, rows, cols = sums_f32.shape
    x, y, c = _mesh_pos()
    idx = jnp.stack([2 * x + y, c]).astype(jnp.int32)

    def body(idx_ref, o_ref, l_ref, out_ref):
        out_ref[...] = ((o_ref[...] + l_ref[0].astype(F32)) + l_ref[1].astype(F32)) + l_ref[2].astype(F32)

    return pl.pallas_call(
        body, name=name,
        grid_spec=pltpu.PrefetchScalarGridSpec(
            num_scalar_prefetch=1, grid=(1,),
            in_specs=[pl.BlockSpec((None, rows, cols), lambda i, idx: (idx[0], 0, 0)),
                      pl.BlockSpec((N_CHIPS - 1, rows, cols), lambda i, idx: (0, 0, 0))],
            out_specs=pl.BlockSpec((rows, cols), lambda i, idx: (idx[1], 0))),
        out_shape=jax.ShapeDtypeStruct((2 * rows, cols), F32),
        compiler_params=pltpu.CompilerParams(vmem_limit_bytes=_vmem_limit(3 * rows * cols * 4)),
    )(idx, sums_f32, landed)


def _halves_to_full(fulls, tag):
    n = len(fulls)

    def body(*refs):
        outs = refs[n:2 * n]
        send_sems, recv_sems = refs[2 * n:]
        x, y, c = _mesh_pos()
        copies = []
        for k in range(n):
            half = fulls[k].shape[0] // 2
            rows = outs[k].at[pl.ds(c * half, half), :]
            cp = pltpu.make_async_remote_copy(
                src_ref=rows, dst_ref=rows, send_sem=send_sems.at[k], recv_sem=recv_sems.at[k],
                device_id=(x, y, 1 - c), device_id_type=MESH)
            cp.start()
            copies.append(cp)
        for k in range(n):
            half = fulls[k].shape[0] // 2
            theirs = outs[k].at[pl.ds((1 - c) * half, half), :]
            pltpu.make_async_remote_copy(
                src_ref=theirs, dst_ref=theirs, send_sem=send_sems.at[k], recv_sem=recv_sems.at[k],
                device_id=(x, y, 1 - c), device_id_type=MESH).wait_recv()
        for cp in copies:
            cp.wait_send()

    return pl.pallas_call(
        body, name="grad_halves_to_full_" + tag,
        in_specs=_hbm_specs(n), out_specs=_hbm_specs(n),
        out_shape=[jax.ShapeDtypeStruct(f.shape, F32) for f in fulls],
        input_output_aliases={k: k for k in range(n)},
        scratch_shapes=[pltpu.SemaphoreType.DMA((n,)), pltpu.SemaphoreType.DMA((n,))],
    )(*fulls)


def _all_sum_small(v):
    rows, cols = v.shape
    n_dev = 8

    def body(v_ref, out_ref, buf, send_sems, recv_sems):
        x, y, c = _mesh_pos()
        me = 4 * x + 2 * y + c
        buf[me] = v_ref[...]
        peers = []
        for r in range(1, n_dev):
            px = 1 - x if r & 4 else x
            py = 1 - y if r & 2 else y
            pc = 1 - c if r & 1 else c
            peers.append((px, py, pc))
        copies = []
        for r, peer in enumerate(peers):
            cp = pltpu.make_async_remote_copy(
                src_ref=v_ref, dst_ref=buf.at[me], send_sem=send_sems.at[r], recv_sem=recv_sems.at[r],
                device_id=peer, device_id_type=MESH)
            cp.start()
            copies.append(cp)
        for r, (px, py, pc) in enumerate(peers):
            pltpu.make_async_remote_copy(
                src_ref=v_ref, dst_ref=buf.at[4 * px + 2 * py + pc], send_sem=send_sems.at[r], recv_sem=recv_sems.at[r],
                device_id=(px, py, pc), device_id_type=MESH).wait_recv()
        for cp in copies:
            cp.wait_send()
        acc = buf[0]
        for d in range(1, n_dev):
            acc = acc + buf[d]
        out_ref[...] = acc
        out_ref[3:4, :] = jnp.broadcast_to(jnp.sum(acc[3:4, :], axis=1, keepdims=True), (1, cols))

    vm = pl.BlockSpec(memory_space=pltpu.VMEM)
    return pl.pallas_call(
        body, name="all_sum_small", in_specs=[vm], out_specs=vm,
        out_shape=jax.ShapeDtypeStruct((rows, cols), F32),
        scratch_shapes=[pltpu.VMEM((n_dev, rows, cols), F32),
                        pltpu.SemaphoreType.DMA((n_dev - 1,)), pltpu.SemaphoreType.DMA((n_dev - 1,))],
    )(v)


def _adamw_math(w, g, m, v):
    m = ADAM_B1 * m + (1.0 - ADAM_B1) * g
    v = ADAM_B2 * v + (1.0 - ADAM_B2) * (g * g)
    m_hat = m / (1.0 - ADAM_B1 ** ADAM_STEP)
    v_hat = v / (1.0 - ADAM_B2 ** ADAM_STEP)
    delta = -ADAM_LR * (m_hat / (jnp.sqrt(v_hat) + ADAM_EPS) + ADAM_WD * w)
    return delta, m, v


def _adamw(w, g, m, v, name):
    rows, cols = w.shape
    tm = rows // 2 if (rows // 2) % 8 == 0 else rows
    return _rowwise(_adamw_math, [w, g, m, v], [], [(cols, F32)] * 3, [], tm=tm, name=name)


def _unshard_cols(gathered):
    n, r, c = gathered.shape
    return jnp.transpose(gathered, (1, 0, 2)).reshape(r, n * c)


def _shard_cols(full):
    r, nc = full.shape
    return jnp.transpose(full.reshape(r, N_CHIPS, nc // N_CHIPS), (1, 0, 2))


LATE = ["w_sb_up", "w_dil_up", "w_out", "w_ffn_in", "w_ffn_out"]


def _swap_middle(slabs4):
    return jnp.stack([slabs4[0], slabs4[2], slabs4[1], slabs4[3]])


def _late_weights(slabs, d_model, d_ff):
    g = dict(zip(LATE, slabs))
    return (_unshard_cols(g["w_sb_up"]), _unshard_cols(g["w_dil_up"]), g["w_out"].reshape(d_model, d_model),
            _unshard_cols(_swap_middle(g["w_ffn_in"])), g["w_ffn_out"].reshape(d_ff, d_model))


def _chip_major(grads, d_model, d_ff):
    row_sharded = {"w_out": d_model // N_CHIPS, "w_ffn_out": d_ff // N_CHIPS}
    out = []
    for k, g in grads.items():
        if k in row_sharded:
            out.append(g.reshape(N_CHIPS, row_sharded[k], g.shape[1]))
        else:
            out.append(_swap_middle(_shard_cols(g)) if k == "w_ffn_in" else _shard_cols(g))
    return out


def _pair_reduce(grads, d_model, d_ff):
    full = _chip_major(grads, d_model, d_ff)
    others = _pair_exchange(full, next(iter(grads)))
    return [_pair_sum(g, o, "grad_pair_sum_" + k) for g, o, k in zip(full, others, grads)]


def _chip_reduce(pair, landed, names):
    halves = [_chip_sum(p[0], l, "grad_chip_sum_" + k) for p, l, k in zip(pair, landed, names)]
    return dict(zip(names, _halves_to_full(halves, names[0])))


def _fwd_bwd(x, loss_target, g_mix, g_ffn, g_fin, wf_in, late_slabs):
    b_sz, s_len, d_model = x.shape
    t = b_sz * s_len
    d_ff = late_slabs[-1].shape[1] * N_CHIPS
    x2d = x.reshape(t, d_model)
    tgt2d = loss_target.reshape(t, d_model)

    (u,) = _rowwise(lambda xv, g: (_rms_stats(xv)[0] * g,), [x2d], [g_mix], [(d_model, BF16)], [], tm=512, name="norm_mix")
    qkv, (slab_ffn_out,) = _mm(u, wf_in, b_cols=(0, QKV_WIDTH), tm=2048, tn=768, tk=d_model, name="proj_qkv",
                               carry=_gather_carry(late_slabs[4:]))
    gates = _mm(u, wf_in, b_cols=(QKV_WIDTH, 2 * d_model), out_dtype=BF16, tm=t, tn=256, tk=d_model, name="proj_gates")
    qkv3 = qkv.reshape(b_sz, s_len, QKV_WIDTH)
    o_sb, small_slabs = _sb_fwd(qkv3, b_sz, s_len, _gather_carry(late_slabs[:3]))
    o_dl, lse, (slab_ffn_in,) = _dil_fwd(qkv3, b_sz, s_len, _gather_carry(late_slabs[3:4]))
    wf_sb_up, wf_dil_up, wf_out, wf_ffn_in, wf_ffn_out = _late_weights(
        list(small_slabs) + [slab_ffn_in, slab_ffn_out], d_model, d_ff)
    o_sb2, o_dl2 = o_sb.reshape(t, SB_WIDTH), o_dl.reshape(t, DIL_OUT_WIDTH)
    y_sb = _mm(o_sb2, wf_sb_up, out_dtype=BF16, tm=1024, tn=1024, tk=SB_WIDTH, name="sb_up")
    y_dl = _mm(o_dl2, wf_dil_up, out_dtype=BF16, tm=1024, tn=1024, tk=DIL_OUT_WIDTH, name="dil_up")

    def merge_fn(gt, ys, yd):
        return (_sigmoid(gt[:, :d_model]) * ys + _sigmoid(gt[:, d_model:]) * yd,)

    (merged,) = _rowwise(merge_fn, [gates, y_sb, y_dl], [], [(d_model, BF16)], [], tm=512, name="merge")
    x1 = _mm(merged, wf_out, add=x2d, tm=512, tn=1024, tk=d_model, name="mix_out")
    (u2,) = _rowwise(lambda xv, g: (_rms_stats(xv)[0] * g,), [x1], [g_ffn], [(d_model, BF16)], [], tm=512, name="norm_ffn")
    half_ff = d_ff // 2

    def act_fn(hv):
        gate = hv[:, :half_ff]
        return hv, gate * _sigmoid(gate) * hv[:, half_ff:]

    h, act = _mm(u2, wf_ffn_in, tm=512, tn=d_ff, tk=d_model, name="ffn_in",
                 epilogue=(act_fn, [], [], [(d_ff, BF16), (half_ff, BF16)], []))
    def head_fn(xv, tg, g):
        xhat, r = _rms_stats(xv)
        err = xhat * g - tg
        dy = err * (1.0 / d_model)
        dx, dg_rows = _rms_bwd(dy, xhat, r, g)
        loss_lanes = (0.5 / d_model) * jnp.sum(err * err, axis=0, keepdims=True)
        return dx, dx, jnp.sum(dg_rows, axis=0, keepdims=True), loss_lanes

    dx2, dx2_b, dg_fin, loss_lanes = _mm(
        act, wf_ffn_out, add=x1, tm=512, tn=1024, tk=d_ff, name="ffn_out",
        epilogue=(head_fn, [tgt2d], [g_fin], [(d_model, F32), (d_model, BF16)], [(1, d_model), (1, d_model)]))

    def dact_fn(da, hv):
        gate, up = hv[:, :half_ff], hv[:, half_ff:]
        sg = _sigmoid(gate)
        dgate = da * up * (sg * (1.0 + gate * (1.0 - sg)))
        return (jnp.concatenate([dgate, da * (gate * sg)], axis=1),)

    (dh,) = _mm(dx2_b, wf_ffn_out, tb=True, tm=512, tn=half_ff, tk=d_model, name="ffn_out_dx",
                epilogue=(dact_fn, [h], [], [(d_ff, BF16)], []))
    gw_ffn_out = _mm(act, dx2_b, ta=True, tm=256, tn=d_model, tk=t, name="ffn_out_dw")
    def norm_bwd_fn(du_, dres, xv, g):
        xhat, r = _rms_stats(xv)
        dx, dg_rows = _rms_bwd(du_, xhat, r, g)
        return dres + dx, jnp.sum(dg_rows, axis=0, keepdims=True)

    def norm_bwd_twice(*args):
        dx, dg = norm_bwd_fn(*args)
        return dx, dx, dg

    dx1, dx1_b, dg_ffn = _mm(dh, wf_ffn_in, tb=True, tm=512, tn=1024, tk=2 * d_ff, name="ffn_in_dx",
                             epilogue=(norm_bwd_twice, [dx2, x1], [g_ffn], [(d_model, F32), (d_model, BF16)], [(1, d_model)]))
    gw_ffn_in = _mm(u2, dh, ta=True, tm=d_model, tn=512, tk=t, name="ffn_in_dw")

    dmerged = _mm(dx1_b, wf_out, tb=True, out_dtype=BF16, tm=512, tn=1024, tk=d_model, name="mix_out_dx")
    gw_out = _mm(merged, dx1_b, ta=True, tm=256, tn=d_model, tk=t, name="mix_out_dw")

    def merge_bwd_fn(gt, ys, yd, dm):
        s_sb, s_dl = _sigmoid(gt[:, :d_model]), _sigmoid(gt[:, d_model:])
        dgates = jnp.concatenate([dm * ys * s_sb * (1.0 - s_sb), dm * yd * s_dl * (1.0 - s_dl)], axis=1)
        return dgates, dm * s_sb, dm * s_dl

    dgates, dy_sb, dy_dl = _rowwise(merge_bwd_fn, [gates, y_sb, y_dl, dmerged], [],
                                    [(2 * d_model, BF16), (d_model, BF16), (d_model, BF16)], [], tm=256, name="merge_bwd")
    do_sb = _mm(dy_sb, wf_sb_up, tb=True, out_dtype=BF16, tm=1024, tn=SB_WIDTH, tk=d_model, name="sb_up_dx")
    gw_sb_up = _mm(o_sb2, dy_sb, ta=True, tm=SB_WIDTH, tn=1024, tk=512, name="sb_up_dw")
    do_dl = _mm(dy_dl, wf_dil_up, tb=True, tm=1024, tn=DIL_OUT_WIDTH, tk=d_model, name="dil_up_dx")
    gw_dil_up = _mm(o_dl2, dy_dl, ta=True, tm=DIL_OUT_WIDTH, tn=1024, tk=512, name="dil_up_dw")
    late_grads = {"w_sb_up": gw_sb_up, "w_dil_up": gw_dil_up, "w_out": gw_out, "w_ffn_in": gw_ffn_in, "w_ffn_out": gw_ffn_out}
    pair = _pair_reduce(late_grads, d_model, d_ff)
    (dq_sb, dk_sb, dv_sb), landed_a = _sb_bwd(qkv3, o_sb, do_sb.reshape(b_sz, s_len, SB_WIDTH), b_sz, s_len,
                                             _chip_carry([p[1] for p in pair[:4]]))
    (dq_dl, dk_dl, dv_dl), landed_b = _dil_bwd(qkv3, o_dl, lse, do_dl.reshape(b_sz, s_len, DIL_OUT_WIDTH), b_sz, s_len,
                                               _chip_carry([p[1] for p in pair[4:]]))
    landed = list(landed_a) + list(landed_b)
    dproj = jnp.concatenate(
        [a.reshape(t, -1) for a in (dq_sb, dk_sb, dv_sb)]
        + [a.reshape(t, -1) for a in (dq_dl, dk_dl, dv_dl)] + [dgates], axis=1)
    gw_in = _mm(u, dproj, ta=True, tm=d_model, tn=256, tk=t, name="proj_dw")
    pair_in = _pair_reduce({"w_in": gw_in}, d_model, d_ff)
    (dx, dg_mix), landed_in = _mm(
        dproj, wf_in, tb=True, tm=512, tn=1024, tk=wf_in.shape[1], name="proj_dx",
        carry=_chip_carry([p[1] for p in pair_in]),
        epilogue=(norm_bwd_fn, [dx1, x2d], [g_mix], [(d_model, F32)], [(1, d_model)]))

    grads = _chip_reduce(pair, landed, LATE)
    grads.update(_chip_reduce(pair_in, landed_in, ["w_in"]))
    return dx, grads, dg_mix, dg_ffn, dg_fin, loss_lanes


def kernel(x, norm_mix_g, w_in, w_sb_up, w_dil_up, w_out, norm_ffn_g, w_ffn_in, w_ffn_out, norm_final_g, loss_target, m_norm_mix_g, m_w_in, m_w_sb_up, m_w_dil_up, m_w_out, m_norm_ffn_g, m_w_ffn_in, m_w_ffn_out, m_norm_final_g, v_norm_mix_g, v_w_in, v_w_sb_up, v_w_dil_up, v_w_out, v_norm_ffn_g, v_w_ffn_in, v_w_ffn_out, v_norm_final_g):
    b_sz, s_len, d_model = x.shape
    d_ff = w_ffn_out.shape[1] * N_CHIPS
    g_mix, g_ffn, g_fin = norm_mix_g, norm_ffn_g, norm_final_g.reshape(1, d_model)

    names = ["w_in", "w_sb_up", "w_dil_up", "w_out", "w_ffn_in", "w_ffn_out"]
    shards = {"w_in": w_in[0], "w_sb_up": w_sb_up[0], "w_dil_up": w_dil_up[0], "w_out": w_out[0],
              "w_ffn_in": w_ffn_in[0], "w_ffn_out": w_ffn_out[0]}
    (slab_in,) = _gather_weights([_cast_to_slab(shards["w_in"], "cast_w_in")])
    late_slabs = [_cast_to_slab(shards[k], "cast_" + k) for k in LATE]

    dx, grads, dg_mix, dg_ffn, dg_fin, loss_lanes = _fwd_bwd(
        x, loss_target, g_mix, g_ffn, g_fin, _unshard_cols(slab_in), late_slabs)

    small = jnp.concatenate([dg_mix, dg_ffn, dg_fin, loss_lanes, jnp.zeros((4, d_model), F32)], axis=0)
    small = _all_sum_small(small)
    loss = small[3, 0]
    gains = jnp.concatenate([g_mix, g_ffn, g_fin, jnp.zeros((5, d_model), F32)], axis=0)
    gains_m = jnp.concatenate([m_norm_mix_g, m_norm_ffn_g, m_norm_final_g.reshape(1, d_model), jnp.zeros((5, d_model), F32)], axis=0)
    gains_v = jnp.concatenate([v_norm_mix_g, v_norm_ffn_g, v_norm_final_g.reshape(1, d_model), jnp.ones((5, d_model), F32)], axis=0)
    gd, gm, gv = _rowwise(_adamw_math, [gains, small, gains_m, gains_v], [], [(d_model, F32)] * 3, [], tm=8, name="adamw_gains")

    moments = {"w_in": (m_w_in, v_w_in), "w_sb_up": (m_w_sb_up, v_w_sb_up), "w_dil_up": (m_w_dil_up, v_w_dil_up),
               "w_out": (m_w_out, v_w_out), "w_ffn_in": (m_w_ffn_in, v_w_ffn_in), "w_ffn_out": (m_w_ffn_out, v_w_ffn_out)}
    upd = {k: _adamw(shards[k], grads[k], moments[k][0][0], moments[k][1][0], "adamw_" + k) for k in names}

    def w_out_of(i):
        return [upd[k][i][None] for k in names]

    def ordered(mix, ws, ffn_g, fin):
        return [mix, ws[0], ws[1], ws[2], ws[3], ffn_g, ws[4], ws[5], fin]

    grad_ws = [grads[k][None] for k in names]
    outs = [loss, dx.reshape(b_sz, s_len, d_model)]
    outs += ordered(small[0:1], grad_ws, small[1:2], small[2])
    outs += ordered(gd[0:1], w_out_of(0), gd[1:2], gd[2])
    outs += ordered(gm[0:1], w_out_of(1), gm[1:2], gm[2])
    outs += ordered(gv[0:1], w_out_of(2), gv[1:2], gv[2])
    return tuple(outs)
```

```python
import functools
import math

import jax
import jax.numpy as jnp
from jax import lax
from jax.experimental import pallas as pl
from jax.experimental.pallas import tpu as pltpu

F32 = jnp.float32
BF16 = jnp.bfloat16
MESH = pl.DeviceIdType.MESH

HEAD_DIM = 64
SB_HEADS = 8
DIL_PAIRS = ((128, 1), (512, 4), (2048, 16))
DIL_HEADS_PER_GROUP = 4
DIL_HEADS = DIL_HEADS_PER_GROUP * len(DIL_PAIRS)
SB_WIDTH = SB_HEADS * HEAD_DIM
DIL_WIDTH = DIL_HEADS * HEAD_DIM
DIL_OUT_WIDTH = DIL_HEADS_PER_GROUP * HEAD_DIM
QKV_WIDTH = 3 * SB_WIDTH + 3 * DIL_WIDTH
RMS_EPS = 1e-6
ALIBI_MAX_BIAS = 8.0
ADAM_LR = 0.001
ADAM_B1 = 0.9
ADAM_B2 = 0.999
ADAM_EPS = 1e-08
ADAM_WD = 0.01
ADAM_STEP = 10

LANES = 128
BLK = 128
NEG = -1e30
EXP_UNDERFLOW = -104.0
SB_FWD_CHAINS = 4
SB_BWD_CHAINS = 4
N_CHIPS = 4
VMEM_CAP = 56 * 1024 * 1024


def _vmem_limit(tile_bytes):
    return int(min(VMEM_CAP, max(32 * 1024 * 1024, 3 * tile_bytes + 8 * 1024 * 1024)))


def _nbytes(shape, dtype):
    return math.prod(shape) * jnp.dtype(dtype).itemsize


def _dot(a, b):
    return jnp.dot(a, b, preferred_element_type=F32)


def _dot_nt(a, b):
    return lax.dot_general(a, b, (((1,), (1,)), ((), ())), preferred_element_type=F32)


def _dot_tn(a, b):
    return lax.dot_general(a, b, (((0,), (0,)), ((), ())), preferred_element_type=F32)


def _split2(x):
    hi = x.astype(BF16)
    lo = (x - hi.astype(F32)).astype(BF16)
    return hi, lo


def _sigmoid(x):
    return 1.0 / (1.0 + jnp.exp(-x))


def _hbm(x):
    return pltpu.with_memory_space_constraint(x, pltpu.HBM)


class _Carry:
    def __init__(self, arrays=(), out_shapes=(), aliased=False, sems=(), start=None, finish=None):
        self.arrays, self.out_shapes, self.aliased = [_hbm(a) for a in arrays], list(out_shapes), aliased
        self.sems, self.start, self.finish = list(sems), start, finish

    def __bool__(self):
        return bool(self.arrays)

    def call_args(self, n_in, n_out):
        aliases = {n_in + k: n_out + k for k in range(len(self.arrays))} if self.aliased else {}
        return _hbm_specs(len(self.arrays)), _hbm_specs(len(self.out_shapes)), self.out_shapes, aliases, self.sems

    def run(self, refs, n_in, n_out, step, n_steps, compute):
        if not self:
            compute()
            return
        n_c, n_o, n_s = len(self.arrays), len(self.out_shapes), len(self.sems)
        ins = refs[n_in:n_in + n_c]
        outs = refs[n_in + n_c + n_out:n_in + n_c + n_out + n_o]
        sems = refs[len(refs) - n_s:]

        @pl.when(step == 0)
        def _():
            self.start(ins, outs, sems)

        compute()

        @pl.when(step == n_steps - 1)
        def _():
            self.finish(ins, outs, sems)


def _mm(a, b, *, ta=False, tb=False, add=None, out_dtype=F32, tm, tn, tk, name, carry=None, epilogue=None,
        b_cols=None):
    carry = carry or _Carry()
    n_car = len(carry.arrays)
    if ta:
        kdim, m = a.shape
    else:
        m, kdim = a.shape
    if tb:
        n, k2 = b.shape
    else:
        k2, n = b.shape
    col0 = 0
    if b_cols is not None:
        assert not tb and b_cols[0] % tn == 0, name
        col0, n = b_cols[0] // tn, b_cols[1]
    assert kdim == k2 and m % tm == 0 and n % tn == 0 and kdim % tk == 0, (name, a.shape, b.shape)
    nk = kdim // tk
    grid = (m // tm, n // tn, nk)
    a_mode = dict(pipeline_mode=pl.Buffered(1)) if grid[0] == 1 and nk == 1 else {}
    b_mode = dict(pipeline_mode=pl.Buffered(1)) if grid[1] == 1 and nk == 1 else {}
    a_spec = (pl.BlockSpec((tk, tm), lambda i, j, k: (k, i), **a_mode) if ta
              else pl.BlockSpec((tm, tk), lambda i, j, k: (i, k), **a_mode))
    b_spec = (pl.BlockSpec((tn, tk), lambda i, j, k: (j, k), **b_mode) if tb
              else pl.BlockSpec((tk, tn), lambda i, j, k: (k, j + col0), **b_mode))
    o_spec = pl.BlockSpec((tm, tn), lambda i, j, k: (i, j))
    dims = ((((0,) if ta else (1,)), ((1,) if tb else (0,))), ((), ()))
    has_add = add is not None
    if epilogue is None:
        ep_fn, ep_rows, ep_params, ep_outs, ep_accs = None, [], [], [], []
        out_sds, out_specs = [jax.ShapeDtypeStruct((m, n), out_dtype)], [o_spec]
    else:
        ep_fn, ep_rows, ep_params, ep_outs, ep_accs = epilogue
        assert grid[1] == 1 or not ep_accs, name
        out_sds = [jax.ShapeDtypeStruct((m, w * grid[1]), d) for w, d in ep_outs]
        out_sds += [jax.ShapeDtypeStruct(sh, F32) for sh in ep_accs]
        out_specs = [pl.BlockSpec((tm, w), lambda i, j, k: (i, j)) for w, _ in ep_outs]
        out_specs += [pl.BlockSpec(sh, lambda i, j, k: (0, 0)) for sh in ep_accs]
    n_main = len(out_sds)
    use_scratch = nk > 1 and (ep_fn is not None or jnp.dtype(out_dtype) != jnp.dtype(F32))
    n_in = 2 + has_add + len(ep_rows) + len(ep_params)

    def finish(total, refs, pid):
        outs = refs[n_in + n_car:n_in + n_car + n_main]
        if ep_fn is None:
            outs[0][...] = total.astype(out_dtype)
            return
        first = 2 + has_add
        rows = [r[...].astype(F32) for r in refs[first:first + len(ep_rows)]]
        params = [p[...] for p in refs[first + len(ep_rows):n_in]]
        res = ep_fn(total, *rows, *params)
        for o_ref, v in zip(outs[:len(ep_outs)], res):
            o_ref[...] = v.astype(o_ref.dtype)
        acc_refs = outs[len(ep_outs):]
        if acc_refs:
            @pl.when(pid[0] == 0)
            def _():
                for r in acc_refs:
                    r[...] = jnp.zeros(r.shape, F32)

            for r, v in zip(acc_refs, res[len(ep_outs):]):
                r[...] += v

    def compute(refs, pid):
        a_ref, b_ref = refs[0], refs[1]
        add_ref = refs[2] if has_add else None
        prod = lax.dot_general(a_ref[...].astype(BF16), b_ref[...].astype(BF16), dims, preferred_element_type=F32)
        if nk == 1:
            finish(prod + add_ref[...] if has_add else prod, refs, pid)
            return
        acc_ref = refs[n_in + n_car + n_main + len(carry.out_shapes)] if use_scratch else refs[n_in + n_car]
        k = pid[2]

        @pl.when(k == 0)
        def _():
            acc_ref[...] = prod + add_ref[...] if has_add else prod

        @pl.when(k > 0)
        def _():
            acc_ref[...] += prod

        if use_scratch:
            @pl.when(k == nk - 1)
            def _():
                finish(acc_ref[...], refs, pid)

    def body(*refs):
        pid = (pl.program_id(0), pl.program_id(1), pl.program_id(2))
        step = (pid[0] * grid[1] + pid[1]) * nk + pid[2]
        carry.run(refs, n_in, n_main, step, grid[0] * grid[1] * nk, lambda: compute(refs, pid))

    tile_bytes = (_nbytes((tm, tk), a.dtype) + _nbytes((tk, tn), b.dtype) + 2 * _nbytes((tm, tn), F32)
                  + (_nbytes((tm, tn), F32) if has_add else 0)
                  + sum(_nbytes((tm, r.shape[1]), r.dtype) for r in ep_rows) + sum(_nbytes((tm, w), d) for w, d in ep_outs))
    in_specs = [a_spec, b_spec] + ([o_spec] if has_add else [])
    in_specs += [pl.BlockSpec((tm, r.shape[1] // grid[1]), lambda i, j, k: (i, j)) for r in ep_rows]
    in_specs += [pl.BlockSpec(p.shape, lambda i, j, k: (0, 0)) for p in ep_params]
    args = tuple(_hbm(x) for x in (a, b) + ((add,) if has_add else ()) + tuple(ep_rows)) + tuple(ep_params)
    scratch = [pltpu.VMEM((tm, tn), F32)] if use_scratch else []
    serial = bool(carry) or bool(ep_accs)
    c_in, c_out, c_shapes, c_alias, c_sems = carry.call_args(n_in, n_main)
    res = pl.pallas_call(
        body, name=name, grid=grid,
        in_specs=in_specs + c_in, out_specs=out_specs + c_out, out_shape=out_sds + c_shapes,
        input_output_aliases=c_alias, scratch_shapes=scratch + c_sems,
        compiler_params=pltpu.CompilerParams(
            dimension_semantics=("arbitrary",) * 3 if serial else ("parallel", "parallel", "arbitrary"),
            vmem_limit_bytes=_vmem_limit(tile_bytes)),
    )(*args, *carry.arrays)
    main = res[0] if ep_fn is None else list(res[:n_main])
    return (main, res[n_main:]) if carry else main


def _rowwise(fn, rows, params, outs, accs, *, tm, name):
    t = rows[0].shape[0]
    assert t % tm == 0, (name, t, tm)
    n_r, n_p, n_o = len(rows), len(params), len(outs)

    def body(*refs):
        vals = [r[...].astype(F32) for r in refs[:n_r]] + [p[...] for p in refs[n_r:n_r + n_p]]
        res = fn(*vals)
        o_refs = refs[n_r + n_p:n_r + n_p + n_o]
        a_refs = refs[n_r + n_p + n_o:]
        for o_ref, v in zip(o_refs, res[:n_o]):
            o_ref[...] = v.astype(o_ref.dtype)
        if accs:
            @pl.when(pl.program_id(0) == 0)
            def _():
                for a_ref in a_refs:
                    a_ref[...] = jnp.zeros(a_ref.shape, F32)

            for a_ref, v in zip(a_refs, res[n_o:]):
                a_ref[...] += v

    in_specs = [pl.BlockSpec((tm, r.shape[1]), lambda i: (i, 0)) for r in rows]
    in_specs += [pl.BlockSpec(p.shape, lambda i: (0, 0)) for p in params]
    out_specs = [pl.BlockSpec((tm, w), lambda i: (i, 0)) for w, _ in outs]
    out_specs += [pl.BlockSpec(s, lambda i: (0, 0)) for s in accs]
    out_shape = [jax.ShapeDtypeStruct((t, w), d) for w, d in outs]
    out_shape += [jax.ShapeDtypeStruct(s, F32) for s in accs]
    tile_bytes = sum(_nbytes((tm, r.shape[1]), r.dtype) for r in rows) + sum(_nbytes((tm, w), F32) for w, _ in outs)
    res = pl.pallas_call(
        body, name=name, grid=(t // tm,), in_specs=in_specs, out_specs=out_specs, out_shape=out_shape,
        compiler_params=pltpu.CompilerParams(
            dimension_semantics=("arbitrary",) if accs else ("parallel",),
            vmem_limit_bytes=_vmem_limit(2 * tile_bytes)),
    )(*[_hbm(r) for r in rows], *params)
    return res


def _rms_stats(x):
    r = lax.rsqrt(jnp.mean(x * x, axis=-1, keepdims=True) + RMS_EPS)
    return x * r, r


def _rms_bwd(dy, xhat, r, g):
    dxhat = dy * g
    dx = r * (dxhat - xhat * jnp.mean(dxhat * xhat, axis=-1, keepdims=True))
    return dx, dy * xhat


def _sb_consts():
    lane = lax.broadcasted_iota(jnp.int32, (BLK, LANES), 1)
    head0 = lane < HEAD_DIM
    row = lax.broadcasted_iota(jnp.int32, (2 * BLK, BLK), 0) % BLK
    col = lax.broadcasted_iota(jnp.int32, (2 * BLK, BLK), 1)
    causal = col < row
    jj = lax.broadcasted_iota(jnp.int32, (BLK, BLK), 0)
    ss = lax.broadcasted_iota(jnp.int32, (BLK, BLK), 1)
    suffix = jnp.where(jj > ss, 1.0, 0.0).astype(BF16)
    return head0, causal, suffix


def _stack_heads(x, head0):
    zero = jnp.zeros_like(x)
    return jnp.concatenate([jnp.where(head0, x, zero), jnp.where(head0, zero, x)], axis=0)


def _sb_logits(z, causal, masked):
    sp = jnp.log(1.0 + jnp.exp(-jnp.abs(z)))
    log_keep = -(jnp.maximum(z, 0.0) + sp)
    log_beta = jnp.minimum(z, 0.0) - sp
    if masked:
        log_keep = jnp.where(causal, log_keep, 0.0)
    return log_keep, log_beta


def _suffix_sums(x, suffix):
    hi, lo = _split2(x)
    after = _dot(hi, suffix) + _dot(lo, suffix)
    total = jnp.broadcast_to(after[:, 0:1] + x[:, 0:1], x.shape)
    return after, total


def _sb_walk_back(i, state, per_chain, tile):
    def alive(st):
        worst = functools.reduce(jnp.maximum, [st[p][:, 0:1] for p in range(0, len(st), per_chain)])
        return jnp.max(worst) > EXP_UNDERFLOW

    def cond(c):
        return jnp.logical_and(c[0] < i, alive(c[1]))

    def body(c):
        return c[0] + 1, tile(i - 1 - c[0], c[1], False)

    return lax.while_loop(cond, body, (jnp.int32(0), state))[1]


def _lane_blocks(x, n):
    return [x[:, p * LANES:(p + 1) * LANES] for p in range(n)]


def _sb_fwd(qkv, b_sz, s_len, carry):
    nq = s_len // BLK
    n_pairs = SB_WIDTH // LANES
    ch = SB_FWD_CHAINS
    n_steps = n_pairs // ch
    scale = 1.0 / math.sqrt(HEAD_DIM)

    def compute(q_ref, k_ref, v_ref, o_ref):
        head0, causal, suffix = _sb_consts()

        def q_block(i, _):
            qs = pl.multiple_of(i * BLK, BLK)
            q_all = (q_ref[pl.ds(qs, BLK), :] * scale).astype(BF16)
            q01 = [_stack_heads(q, head0) for q in _lane_blocks(q_all, ch)]

            def tile(j, state, masked):
                ks = pl.multiple_of(j * BLK, BLK)
                ks_ = _lane_blocks(k_ref[pl.ds(ks, BLK), :].astype(BF16), ch)
                vs_ = _lane_blocks(v_ref[pl.ds(ks, BLK), :].astype(BF16), ch)
                zs = [_dot_nt(q01[p], ks_[p]) for p in range(ch)]
                logits = [_sb_logits(z, causal, masked) for z in zs]
                sums = [_suffix_sums(lg[0], suffix) for lg in logits]
                out = []
                for p in range(ch):
                    carry, acc = state[2 * p], state[2 * p + 1]
                    after, total = sums[p]
                    a = jnp.exp(logits[p][1] + carry + after)
                    if masked:
                        a = jnp.where(causal, a, 0.0)
                    a_hi, a_lo = _split2(a)
                    a_cat = jnp.concatenate([a_hi[:BLK], a_hi[BLK:], a_lo[:BLK], a_lo[BLK:]], axis=1)
                    v01 = _stack_heads(vs_[p], head0)
                    out += [carry + total, acc + _dot(a_cat, jnp.concatenate([v01, v01], axis=0))]
                return tuple(out)

            state = (jnp.zeros((2 * BLK, BLK), F32), jnp.zeros((BLK, LANES), F32)) * ch
            state = tile(i, state, True)
            state = _sb_walk_back(i, state, 2, tile)
            o_ref[pl.ds(qs, BLK), :] = jnp.concatenate([state[2 * p + 1] for p in range(ch)], axis=1)
            return 0

        lax.fori_loop(0, nq, q_block, 0)

    def body(*refs):
        step = pl.program_id(0) * n_steps + pl.program_id(1)
        o_ref = refs[3 + len(carry.arrays)]
        carry.run(refs, 3, 1, step, b_sz * n_steps, lambda: compute(refs[0], refs[1], refs[2], o_ref))

    blk = lambda off: pl.BlockSpec((None, s_len, ch * LANES), lambda b, p: (b, 0, off + p))
    c_in, c_out, c_shapes, c_alias, c_sems = carry.call_args(3, 1)
    res = pl.pallas_call(
        body, name="sb_fwd", grid=(b_sz, n_steps),
        in_specs=[blk(0), blk(n_steps), blk(2 * n_steps)] + c_in, out_specs=[blk(0)] + c_out,
        out_shape=[jax.ShapeDtypeStruct((b_sz, s_len, SB_WIDTH), F32)] + c_shapes,
        input_output_aliases=c_alias, scratch_shapes=c_sems,
        compiler_params=pltpu.CompilerParams(dimension_semantics=("arbitrary", "arbitrary"),
                                             vmem_limit_bytes=VMEM_CAP),
    )(*[_hbm(qkv)] * 3, *carry.arrays)
    return res[0], res[1:]


def _sb_bwd(qkv, o_sb, do_sb, b_sz, s_len, carry):
    nq = s_len // BLK
    n_pairs = SB_WIDTH // LANES
    ch = SB_BWD_CHAINS
    n_steps = n_pairs // ch
    scale = 1.0 / math.sqrt(HEAD_DIM)

    def compute(q_ref, k_ref, v_ref, o_ref, do_ref, dq_ref, dk_ref, dv_ref, dk_acc, dv_acc):
        head0, causal, suffix = _sb_consts()
        lrow = lax.broadcasted_iota(jnp.int32, (LANES, LANES), 0)
        ones_h0 = jnp.where(lrow < HEAD_DIM, 1.0, 0.0).astype(BF16)
        ones_h1 = jnp.where(lrow >= HEAD_DIM, 1.0, 0.0).astype(BF16)
        dk_acc[...] = jnp.zeros(dk_acc.shape, F32)
        dv_acc[...] = jnp.zeros(dv_acc.shape, F32)

        def q_block(i, _):
            qs = pl.multiple_of(i * BLK, BLK)
            q_all = (q_ref[pl.ds(qs, BLK), :] * scale).astype(BF16)
            do_all = do_ref[pl.ds(qs, BLK), :].astype(BF16)
            dd_all = do_all.astype(F32) * o_ref[pl.ds(qs, BLK), :]
            q01 = [_stack_heads(q, head0) for q in _lane_blocks(q_all, ch)]
            do01 = [_stack_heads(d, head0) for d in _lane_blocks(do_all, ch)]
            tot = []
            for dd in _lane_blocks(dd_all, ch):
                dd_hi, dd_lo = _split2(dd)
                tot.append(jnp.concatenate([_dot(dd_hi, ones_h0) + _dot(dd_lo, ones_h0),
                                            _dot(dd_hi, ones_h1) + _dot(dd_lo, ones_h1)], axis=0))

            def tile(j, state, masked):
                ks = pl.multiple_of(j * BLK, BLK)
                ks_ = _lane_blocks(k_ref[pl.ds(ks, BLK), :].astype(BF16), ch)
                vs_ = _lane_blocks(v_ref[pl.ds(ks, BLK), :].astype(BF16), ch)
                zs = [_dot_nt(q01[p], ks_[p]) for p in range(ch)]
                das = [_dot_nt(do01[p], vs_[p]) for p in range(ch)]
                logits = [_sb_logits(z, causal, masked) for z in zs]
                sums = [_suffix_sums(lg[0], suffix) for lg in logits]
                a_s, e_s = [], []
                for p in range(ch):
                    a = jnp.exp(logits[p][1] + state[3 * p] + sums[p][0])
                    if masked:
                        a = jnp.where(causal, a, 0.0)
                    a_s.append(a)
                    e_s.append(a * das[p])
                e_sums = [_suffix_sums(e, suffix) for e in e_s]
                out, dks, dvs = [], [], []
                for p in range(ch):
                    carry, rcarry, dq = state[3 * p:3 * p + 3]
                    e = e_s[p]
                    before = tot[p] - (rcarry + e_sums[p][0] + e)
                    beta = jnp.exp(logits[p][1])
                    dz = e * (1.0 - beta) - beta * before
                    if masked:
                        dz = jnp.where(causal, dz, 0.0)
                    dz_b = dz.astype(BF16)
                    dks.append(_dot_tn(dz_b, q01[p]))
                    dvs.append(_dot_tn(a_s[p].astype(BF16), do01[p]))
                    out += [carry + sums[p][1], rcarry + e_sums[p][1], dq + _dot(dz_b, ks_[p])]
                dk_acc[pl.ds(ks, BLK), :] += jnp.concatenate(dks, axis=1)
                dv_acc[pl.ds(ks, BLK), :] += jnp.concatenate(dvs, axis=1)
                return tuple(out)

            state = (jnp.zeros((2 * BLK, BLK), F32),) * (3 * ch)
            state = tile(i, state, True)
            state = _sb_walk_back(i, state, 3, tile)
            dq = [jnp.where(head0, state[3 * p + 2][:BLK], state[3 * p + 2][BLK:]) for p in range(ch)]
            dq_ref[pl.ds(qs, BLK), :] = (jnp.concatenate(dq, axis=1) * scale).astype(dq_ref.dtype)
            return 0

        lax.fori_loop(0, nq, q_block, 0)
        dk_ref[...] = dk_acc[...].astype(dk_ref.dtype)
        dv_ref[...] = dv_acc[...].astype(dv_ref.dtype)

    def body(*refs):
        step = pl.program_id(0) * n_steps + pl.program_id(1)
        n_c, n_o = len(carry.arrays), len(carry.out_shapes)
        own = refs[:5] + refs[5 + n_c:8 + n_c] + refs[8 + n_c + n_o:10 + n_c + n_o]
        carry.run(refs, 5, 3, step, b_sz * n_steps, lambda: compute(*own))

    blk = lambda off: pl.BlockSpec((None, s_len, ch * LANES), lambda b, p: (b, 0, off + p))
    once = lambda off: pl.BlockSpec((None, s_len, ch * LANES), lambda b, p: (b, 0, off + p),
                                    pipeline_mode=pl.Buffered(1))
    out_sd = jax.ShapeDtypeStruct((b_sz, s_len, SB_WIDTH), BF16)
    c_in, c_out, c_shapes, c_alias, c_sems = carry.call_args(5, 3)
    res = pl.pallas_call(
        body, name="sb_bwd", grid=(b_sz, n_steps),
        in_specs=[once(0), once(n_steps), once(2 * n_steps), once(0), once(0)] + c_in,
        out_specs=[blk(0), blk(0), blk(0)] + c_out, out_shape=[out_sd, out_sd, out_sd] + c_shapes,
        input_output_aliases=c_alias,
        scratch_shapes=[pltpu.VMEM((s_len, ch * LANES), F32), pltpu.VMEM((s_len, ch * LANES), F32)] + c_sems,
        compiler_params=pltpu.CompilerParams(dimension_semantics=("arbitrary", "arbitrary"),
                                             vmem_limit_bytes=VMEM_CAP),
    )(*[_hbm(x) for x in (qkv, qkv, qkv, o_sb, do_sb)], *carry.arrays)
    return res[:3], res[3:]


def _dil_consts(group, pair_idx, dilation):
    lane = lax.broadcasted_iota(jnp.int32, (BLK, LANES), 1)
    head0 = lane < HEAD_DIM
    row = lax.broadcasted_iota(jnp.int32, (2 * BLK, BLK), 0)
    qa = row % BLK
    kb = lax.broadcasted_iota(jnp.int32, (2 * BLK, BLK), 1)
    head = (group * DIL_HEADS_PER_GROUP + 2 * pair_idx + row // BLK).astype(F32)
    slope = jnp.exp((-ALIBI_MAX_BIAS * math.log(2.0) / DIL_HEADS) * (head + 1.0))
    valid_cur = kb <= qa
    valid_prev = kb >= qa
    bias_cur = -slope * ((qa - kb) * dilation).astype(F32)
    bias_prev = -slope * ((BLK + qa - kb) * dilation).astype(F32)
    return head0, valid_cur, valid_prev, bias_cur, bias_prev


def _dil_units(s_len, dilation):
    nb = s_len // dilation // BLK
    return [(r, n) for r in range(dilation) for n in range(nb)]


def _dil_rows(n, r, dilation):
    if dilation == 1:
        return pl.ds(n * BLK, BLK)
    return pl.ds(n * BLK * dilation + r, BLK, stride=dilation)


def _dil_scores(q01, k, bias, valid):
    s = _dot_nt(q01, k) * (1.0 / math.sqrt(HEAD_DIM)) + bias
    return jnp.where(valid, s, NEG)


def _dil_fwd(qkv, b_sz, s_len, carry):
    n_pairs = DIL_OUT_WIDTH // LANES
    q_off = 3 * SB_WIDTH // LANES
    per_kind = DIL_WIDTH // LANES

    def compute(pair_idx, qkv_refs, o_ref, lse_ref, m_s, l_s):
        m_s[...] = jnp.full(m_s.shape, NEG, F32)
        l_s[...] = jnp.zeros(l_s.shape, F32)
        o_ref[...] = jnp.zeros(o_ref.shape, F32)
        for g, (_, dilation) in enumerate(DIL_PAIRS):
            q_ref, k_ref, v_ref = qkv_refs[3 * g:3 * g + 3]
            head0, valid_cur, valid_prev, bias_cur, bias_prev = _dil_consts(g, pair_idx, dilation)
            for r, n in _dil_units(s_len, dilation):
                rows = _dil_rows(n, r, dilation)
                q01 = _stack_heads(q_ref[rows, :].astype(BF16), head0)
                k_c = k_ref[rows, :].astype(BF16)
                v_c = v_ref[rows, :].astype(BF16)
                scores = [_dil_scores(q01, k_c, bias_cur, valid_cur)]
                values = [_stack_heads(v_c, head0)]
                if n > 0:
                    prev = _dil_rows(n - 1, r, dilation)
                    scores.append(_dil_scores(q01, k_ref[prev, :].astype(BF16), bias_prev, valid_prev))
                    values.append(_stack_heads(v_ref[prev, :].astype(BF16), head0))
                m_blk = functools.reduce(jnp.maximum, [jnp.max(s, axis=-1, keepdims=True) for s in scores])
                m_old = jnp.concatenate([m_s.at[0][rows, :], m_s.at[1][rows, :]], axis=0)
                l_old = jnp.concatenate([l_s.at[0][rows, :], l_s.at[1][rows, :]], axis=0)
                m_new = jnp.maximum(m_old, m_blk)
                probs = [jnp.exp(s - m_new) for s in scores]
                l_blk = functools.reduce(jnp.add, [jnp.sum(p, axis=-1, keepdims=True) for p in probs])
                alpha = jnp.exp(m_old - m_new)
                l_new = alpha * l_old + l_blk
                alpha_tok = jnp.where(head0, alpha[:BLK], alpha[BLK:])
                p_cat = jnp.concatenate([h for p in probs for h in (p[:BLK].astype(BF16), p[BLK:].astype(BF16))], axis=1)
                o_ref[rows, :] = alpha_tok * o_ref[rows, :] + _dot(p_cat, jnp.concatenate(values, axis=0))
                m_s.at[0][rows, :] = m_new[:BLK]
                m_s.at[1][rows, :] = m_new[BLK:]
                l_s.at[0][rows, :] = l_new[:BLK]
                l_s.at[1][rows, :] = l_new[BLK:]
        lane = lax.broadcasted_iota(jnp.int32, (BLK, LANES), 1)
        for c in range(s_len // BLK):
            rows = pl.ds(c * BLK, BLK)
            l0, l1 = l_s.at[0][rows, :], l_s.at[1][rows, :]
            o_ref[rows, :] = o_ref[rows, :] / jnp.where(lane < HEAD_DIM, l0, l1)
            lse_ref.at[0][rows, :] = m_s.at[0][rows, :] + jnp.log(l0)
            lse_ref.at[1][rows, :] = m_s.at[1][rows, :] + jnp.log(l1)

    def body(*refs):
        pair_idx = pl.program_id(1)
        step = pl.program_id(0) * n_pairs + pair_idx
        n_c, n_o = len(carry.arrays), len(carry.out_shapes)
        o_ref, lse_ref = refs[9 + n_c:11 + n_c]
        m_s, l_s = refs[11 + n_c + n_o:13 + n_c + n_o]
        carry.run(refs, 9, 2, step, b_sz * n_pairs, lambda: compute(pair_idx, refs[:9], o_ref, lse_ref, m_s, l_s))

    in_specs = []
    for g in range(len(DIL_PAIRS)):
        for kind in range(3):
            off = q_off + kind * per_kind + g * n_pairs
            in_specs.append(pl.BlockSpec((None, s_len, LANES), lambda b, p, off=off: (b, 0, off + p)))
    c_in, c_out, c_shapes, c_alias, c_sems = carry.call_args(9, 2)
    res = pl.pallas_call(
        body, name="dil_fwd", grid=(b_sz, n_pairs),
        in_specs=in_specs + c_in,
        out_specs=[pl.BlockSpec((None, s_len, LANES), lambda b, p: (b, 0, p)),
                   pl.BlockSpec((None, None, 2, s_len, LANES), lambda b, p: (b, p, 0, 0, 0))] + c_out,
        out_shape=[jax.ShapeDtypeStruct((b_sz, s_len, DIL_OUT_WIDTH), F32),
                   jax.ShapeDtypeStruct((b_sz, n_pairs, 2, s_len, LANES), F32)] + c_shapes,
        input_output_aliases=c_alias,
        scratch_shapes=[pltpu.VMEM((2, s_len, LANES), F32), pltpu.VMEM((2, s_len, LANES), F32)] + c_sems,
        compiler_params=pltpu.CompilerParams(dimension_semantics=("arbitrary", "arbitrary"),
                                             vmem_limit_bytes=VMEM_CAP),
    )(*[_hbm(qkv)] * 9, *carry.arrays)
    return res[0], res[1], res[2:]


def _dil_bwd(qkv, o_dl, lse, do_dl, b_sz, s_len, carry):
    n_pairs = DIL_OUT_WIDTH // LANES
    n_groups = len(DIL_PAIRS)
    q_off = 3 * SB_WIDTH // LANES
    per_kind = DIL_WIDTH // LANES

    def compute(pair_idx, group, q_ref, k_ref, v_ref, o_ref, lse_ref, do_ref, dq_ref, dk_ref, dv_ref, d_s, dq_s, dk_s, dv_s):
        lrow = lax.broadcasted_iota(jnp.int32, (LANES, LANES), 0)
        ones_h0 = jnp.where(lrow < HEAD_DIM, 1.0, 0.0).astype(BF16)
        ones_h1 = jnp.where(lrow >= HEAD_DIM, 1.0, 0.0).astype(BF16)
        for c in range(s_len // BLK):
            rows = pl.ds(c * BLK, BLK)
            dd_hi, dd_lo = _split2(do_ref[rows, :] * o_ref[rows, :])
            d_s.at[0][rows, :] = _dot(dd_hi, ones_h0) + _dot(dd_lo, ones_h0)
            d_s.at[1][rows, :] = _dot(dd_hi, ones_h1) + _dot(dd_lo, ones_h1)
        dk_s[...] = jnp.zeros(dk_s.shape, F32)
        dv_s[...] = jnp.zeros(dv_s.shape, F32)

        def one_group(g, dilation):
            head0, valid_cur, valid_prev, bias_cur, bias_prev = _dil_consts(g, pair_idx, dilation)
            for r, n in _dil_units(s_len, dilation):
                rows = _dil_rows(n, r, dilation)
                q01 = _stack_heads(q_ref[rows, :].astype(BF16), head0)
                do01 = _stack_heads(do_ref[rows, :].astype(BF16), head0)
                lse01 = jnp.concatenate([lse_ref.at[0][rows, :], lse_ref.at[1][rows, :]], axis=0)
                d01 = jnp.concatenate([d_s.at[0][rows, :], d_s.at[1][rows, :]], axis=0)
                dq = jnp.zeros((2 * BLK, LANES), F32)
                blocks = [(rows, bias_cur, valid_cur)]
                if n > 0:
                    blocks.append((_dil_rows(n - 1, r, dilation), bias_prev, valid_prev))
                for krows, bias, valid in blocks:
                    k = k_ref[krows, :].astype(BF16)
                    v = v_ref[krows, :].astype(BF16)
                    p = jnp.exp(_dil_scores(q01, k, bias, valid) - lse01)
                    ds = (p * (_dot_nt(do01, v) - d01) * (1.0 / math.sqrt(HEAD_DIM))).astype(BF16)
                    dq = dq + _dot(ds, k)
                    dk_s[krows, :] = dk_s[krows, :] + _dot_tn(ds, q01)
                    dv_s[krows, :] = dv_s[krows, :] + _dot_tn(p.astype(BF16), do01)
                dq_s[rows, :] = jnp.where(head0, dq[:BLK], dq[BLK:])

        for g, (_, dilation) in enumerate(DIL_PAIRS):
            pl.when(group == g)(functools.partial(one_group, g, dilation))
        dq_ref[...] = dq_s[...].astype(dq_ref.dtype)
        dk_ref[...] = dk_s[...].astype(dk_ref.dtype)
        dv_ref[...] = dv_s[...].astype(dv_ref.dtype)

    def body(*refs):
        pair_idx, group = pl.program_id(1), pl.program_id(2)
        step = (pl.program_id(0) * n_pairs + pair_idx) * n_groups + group
        n_c, n_o = len(carry.arrays), len(carry.out_shapes)
        own = refs[:6] + refs[6 + n_c:9 + n_c] + refs[9 + n_c + n_o:13 + n_c + n_o]
        carry.run(refs, 6, 3, step, b_sz * n_pairs * n_groups, lambda: compute(pair_idx, group, *own))

    def qkv_spec(kind):
        return pl.BlockSpec((None, s_len, LANES),
                            lambda b, p, g: (b, 0, q_off + kind * per_kind + g * n_pairs + p))

    tok_spec = pl.BlockSpec((None, s_len, LANES), lambda b, p, g: (b, 0, p))
    out_spec = pl.BlockSpec((None, s_len, LANES), lambda b, p, g: (b, 0, g * n_pairs + p))
    out_sd = jax.ShapeDtypeStruct((b_sz, s_len, DIL_WIDTH), BF16)
    c_in, c_out, c_shapes, c_alias, c_sems = carry.call_args(6, 3)
    res = pl.pallas_call(
        body, name="dil_bwd", grid=(b_sz, n_pairs, n_groups),
        in_specs=[qkv_spec(0), qkv_spec(1), qkv_spec(2), tok_spec,
                  pl.BlockSpec((None, None, 2, s_len, LANES), lambda b, p, g: (b, p, 0, 0, 0)), tok_spec] + c_in,
        out_specs=[out_spec, out_spec, out_spec] + c_out,
        out_shape=[out_sd, out_sd, out_sd] + c_shapes,
        input_output_aliases=c_alias,
        scratch_shapes=[pltpu.VMEM((2, s_len, LANES), F32)] + [pltpu.VMEM((s_len, LANES), F32)] * 3 + c_sems,
        compiler_params=pltpu.CompilerParams(dimension_semantics=("arbitrary", "arbitrary", "arbitrary"),
                                             vmem_limit_bytes=VMEM_CAP),
    )(*[_hbm(x) for x in (qkv, qkv, qkv, o_dl, lse, do_dl)], *carry.arrays)
    return res[:3], res[3:]


def _mesh_pos():
    return lax.axis_index("x"), lax.axis_index("y"), lax.axis_index("c")


def _other_chips(x, y):
    return [(1 - x, y), (x, 1 - y), (1 - x, 1 - y)]


def _hbm_specs(n):
    return [pl.BlockSpec(memory_space=pl.ANY)] * n


def _cast_to_slab(w, name):
    rows, cols = w.shape
    mine = jnp.reshape(2 * lax.axis_index("x") + lax.axis_index("y"), (1,)).astype(jnp.int32)

    def body(idx_ref, w_ref, o_ref):
        o_ref[...] = w_ref[...].astype(BF16)

    return pl.pallas_call(
        body, name=name,
        grid_spec=pltpu.PrefetchScalarGridSpec(
            num_scalar_prefetch=1, grid=(1,),
            in_specs=[pl.BlockSpec((rows, cols), lambda i, idx: (0, 0))],
            out_specs=pl.BlockSpec((None, rows, cols), lambda i, idx: (idx[0], 0, 0))),
        out_shape=jax.ShapeDtypeStruct((N_CHIPS, rows, cols), BF16),
        compiler_params=pltpu.CompilerParams(vmem_limit_bytes=_vmem_limit(rows * cols * 6)),
    )(mine, _hbm(w))


def _gather_issue(slabs, send_sems, recv_sems):
    x, y, c = _mesh_pos()
    for k, slab in enumerate(slabs):
        half = slab.shape[1] // 2
        rows = slab.at[2 * x + y, pl.ds(c * half, half), :]
        for r, (px, py) in enumerate(_other_chips(x, y)):
            pltpu.make_async_remote_copy(
                src_ref=rows, dst_ref=rows, send_sem=send_sems.at[6 * k + r], recv_sem=recv_sems.at[6 * k + r],
                device_id=(px, py, c), device_id_type=MESH).start()


def _gather_complete(slabs, send_sems, recv_sems):
    x, y, c = _mesh_pos()
    chips = _other_chips(x, y)

    def copy(k, sem, block, rows, to):
        ref = slabs[k].at[block, rows, :]
        return pltpu.make_async_remote_copy(
            src_ref=ref, dst_ref=ref, send_sem=send_sems.at[sem], recv_sem=recv_sems.at[sem],
            device_id=to, device_id_type=MESH)

    for k, slab in enumerate(slabs):
        half = slab.shape[1] // 2
        for r, (px, py) in enumerate(chips):
            copy(k, 6 * k + r, 2 * px + py, pl.ds(c * half, half), (px, py, c)).wait_recv()
            copy(k, 6 * k + 3 + r, 2 * px + py, pl.ds(c * half, half), (x, y, 1 - c)).start()
    for k, slab in enumerate(slabs):
        half = slab.shape[1] // 2
        for r, (px, py) in enumerate(chips):
            copy(k, 6 * k + 3 + r, 2 * px + py, pl.ds((1 - c) * half, half), (x, y, 1 - c)).wait_recv()
    for k, slab in enumerate(slabs):
        half = slab.shape[1] // 2
        for r, (px, py) in enumerate(chips):
            copy(k, 6 * k + r, 2 * x + y, pl.ds(c * half, half), (px, py, c)).wait_send()
            copy(k, 6 * k + 3 + r, 2 * px + py, pl.ds(c * half, half), (x, y, 1 - c)).wait_send()


def _gather_sems(n):
    return [pltpu.SemaphoreType.DMA((6 * n,)), pltpu.SemaphoreType.DMA((6 * n,))]


def _gather_carry(slabs):
    return _Carry(slabs, [jax.ShapeDtypeStruct(a.shape, a.dtype) for a in slabs], True, _gather_sems(len(slabs)),
                  lambda ins, outs, sems: _gather_issue(outs, *sems),
                  lambda ins, outs, sems: _gather_complete(outs, *sems))


def _gather_weights(slabs):
    n = len(slabs)

    def body(*refs):
        outs = refs[n:2 * n]
        send_sems, recv_sems = refs[2 * n:]
        _gather_issue(outs, send_sems, recv_sems)
        _gather_complete(outs, send_sems, recv_sems)

    return pl.pallas_call(
        body, name="gather_weights",
        in_specs=_hbm_specs(n), out_specs=_hbm_specs(n),
        out_shape=[jax.ShapeDtypeStruct(s.shape, s.dtype) for s in slabs],
        input_output_aliases={k: k for k in range(n)},
        scratch_shapes=_gather_sems(n),
    )(*[_hbm(a) for a in slabs])


def _pair_exchange(grads, tag):
    n = len(grads)

    def body(*refs):
        ins, outs = refs[:n], refs[n:2 * n]
        send_sems, recv_sems = refs[2 * n:]
        x, y, c = _mesh_pos()
        copies = []
        for k in range(n):
            half = grads[k].shape[1] // 2
            cp = pltpu.make_async_remote_copy(
                src_ref=ins[k].at[:, pl.ds((1 - c) * half, half), :], dst_ref=outs[k],
                send_sem=send_sems.at[k], recv_sem=recv_sems.at[k],
                device_id=(x, y, 1 - c), device_id_type=MESH)
            cp.start()
            copies.append(cp)
        for cp in copies:
            cp.wait()

    return pl.pallas_call(
        body, name="grad_pair_exchange_" + tag,
        in_specs=_hbm_specs(n), out_specs=_hbm_specs(n),
        out_shape=[jax.ShapeDtypeStruct((N_CHIPS, g.shape[1] // 2, g.shape[2]), F32) for g in grads],
        scratch_shapes=[pltpu.SemaphoreType.DMA((n,)), pltpu.SemaphoreType.DMA((n,))],
    )(*[_hbm(g) for g in grads])


def _pair_sum(grad, other, name):
    _, rows, cols = grad.shape
    half = rows // 2
    core = jnp.reshape(lax.axis_index("c"), (1,)).astype(jnp.int32)

    def body(core_ref, g_ref, p_ref, s_ref, sb_ref):
        s = g_ref[...] + p_ref[...]
        s_ref[...] = s
        sb_ref[...] = s.astype(BF16)

    blk = pl.BlockSpec((None, half, cols), lambda p, core_ref: (p, 0, 0))
    return pl.pallas_call(
        body, name=name,
        grid_spec=pltpu.PrefetchScalarGridSpec(
            num_scalar_prefetch=1, grid=(N_CHIPS,),
            in_specs=[pl.BlockSpec((None, half, cols), lambda p, core_ref: (p, core_ref[0], 0)), blk],
            out_specs=[blk, blk]),
        out_shape=[jax.ShapeDtypeStruct((N_CHIPS, half, cols), F32),
                   jax.ShapeDtypeStruct((N_CHIPS, half, cols), BF16)],
        compiler_params=pltpu.CompilerParams(dimension_semantics=("parallel",),
                                             vmem_limit_bytes=_vmem_limit(4 * half * cols * 4)),
    )(core, _hbm(grad), _hbm(other))


def _chip_copies(sums_bf16, lands, send_sems, recv_sems):
    x, y, c = _mesh_pos()
    return [pltpu.make_async_remote_copy(
        src_ref=sums_bf16[k].at[2 * px + py], dst_ref=lands[k].at[r],
        send_sem=send_sems.at[3 * k + r], recv_sem=recv_sems.at[3 * k + r],
        device_id=(px, py, c), device_id_type=MESH)
        for k in range(len(sums_bf16)) for r, (px, py) in enumerate(_other_chips(x, y))]


def _chip_carry(sums_bf16):
    def start(ins, outs, sems):
        for cp in _chip_copies(ins, outs, *sems):
            cp.start()

    def finish(ins, outs, sems):
        for cp in _chip_copies(ins, outs, *sems):
            cp.wait()

    return _Carry(sums_bf16, _chip_landing(sums_bf16), False, _chip_sems(len(sums_bf16)), start, finish)


def _chip_sems(n):
    return [pltpu.SemaphoreType.DMA((3 * n,)), pltpu.SemaphoreType.DMA((3 * n,))]


def _chip_landing(sums_bf16):
    return [jax.ShapeDtypeStruct((N_CHIPS - 1,) + s.shape[1:], BF16) for s in sums_bf16]


def _chip_sum(sums_f32, landed, name):
    _, rows, cols = sums_f32.shape
    x, y, c = _mesh_pos()
    idx = jnp.stack([2 * x + y, c]).astype(jnp.int32)

    def body(idx_ref, o_ref, l_ref, out_ref):
        out_ref[...] = ((o_ref[...] + l_ref[0].astype(F32)) + l_ref[1].astype(F32)) + l_ref[2].astype(F32)

    return pl.pallas_call(
        body, name=name,
        grid_spec=pltpu.PrefetchScalarGridSpec(
            num_scalar_prefetch=1, grid=(1,),
            in_specs=[pl.BlockSpec((None, rows, cols), lambda i, idx: (idx[0], 0, 0)),
                      pl.BlockSpec((N_CHIPS - 1, rows, cols), lambda i, idx: (0, 0, 0))],
            out_specs=pl.BlockSpec((rows, cols), lambda i, idx: (idx[1], 0))),
        out_shape=jax.ShapeDtypeStruct((2 * rows, cols), F32),
        compiler_params=pltpu.CompilerParams(vmem_limit_bytes=_vmem_limit(3 * rows * cols * 4)),
    )(idx, _hbm(sums_f32), _hbm(landed))


def _halves_to_full(fulls, tag):
    n = len(fulls)

    def body(*refs):
        outs = refs[n:2 * n]
        send_sems, recv_sems = refs[2 * n:]
        x, y, c = _mesh_pos()
        copies = []
        for k in range(n):
            half = fulls[k].shape[0] // 2
            rows = outs[k].at[pl.ds(c * half, half), :]
            cp = pltpu.make_async_remote_copy(
                src_ref=rows, dst_ref=rows, send_sem=send_sems.at[k], recv_sem=recv_sems.at[k],
                device_id=(x, y, 1 - c), device_id_type=MESH)
            cp.start()
            copies.append(cp)
        for k in range(n):
            half = fulls[k].shape[0] // 2
            theirs = outs[k].at[pl.ds((1 - c) * half, half), :]
            pltpu.make_async_remote_copy(
                src_ref=theirs, dst_ref=theirs, send_sem=send_sems.at[k], recv_sem=recv_sems.at[k],
                device_id=(x, y, 1 - c), device_id_type=MESH).wait_recv()
        for cp in copies:
            cp.wait_send()

    return pl.pallas_call(
        body, name="grad_halves_to_full_" + tag,
        in_specs=_hbm_specs(n), out_specs=_hbm_specs(n),
        out_shape=[jax.ShapeDtypeStruct(f.shape, F32) for f in fulls],
        input_output_aliases={k: k for k in range(n)},
        scratch_shapes=[pltpu.SemaphoreType.DMA((n,)), pltpu.SemaphoreType.DMA((n,))],
    )(*[_hbm(f) for f in fulls])


def _all_sum_small(v):
    rows, cols = v.shape
    n_dev = 8

    def body(v_ref, out_ref, buf, send_sems, recv_sems):
        x, y, c = _mesh_pos()
        me = 4 * x + 2 * y + c
        buf[me] = v_ref[...]
        peers = []
        for r in range(1, n_dev):
            px = 1 - x if r & 4 else x
            py = 1 - y if r & 2 else y
            pc = 1 - c if r & 1 else c
            peers.append((px, py, pc))
        copies = []
        for r, peer in enumerate(peers):
            cp = pltpu.make_async_remote_copy(
                src_ref=v_ref, dst_ref=buf.at[me], send_sem=send_sems.at[r], recv_sem=recv_sems.at[r],
                device_id=peer, device_id_type=MESH)
            cp.start()
            copies.append(cp)
        for r, (px, py, pc) in enumerate(peers):
            pltpu.make_async_remote_copy(
                src_ref=v_ref, dst_ref=buf.at[4 * px + 2 * py + pc], send_sem=send_sems.at[r], recv_sem=recv_sems.at[r],
                device_id=(px, py, pc), device_id_type=MESH).wait_recv()
        for cp in copies:
            cp.wait_send()
        acc = buf[0]
        for d in range(1, n_dev):
            acc = acc + buf[d]
        out_ref[...] = acc
        out_ref[3:4, :] = jnp.broadcast_to(jnp.sum(acc[3:4, :], axis=1, keepdims=True), (1, cols))

    vm = pl.BlockSpec(memory_space=pltpu.VMEM)
    return pl.pallas_call(
        body, name="all_sum_small", in_specs=[vm], out_specs=vm,
        out_shape=jax.ShapeDtypeStruct((rows, cols), F32),
        scratch_shapes=[pltpu.VMEM((n_dev, rows, cols), F32),
                        pltpu.SemaphoreType.DMA((n_dev - 1,)), pltpu.SemaphoreType.DMA((n_dev - 1,))],
    )(v)


def _adamw_math(w, g, m, v):
    m = ADAM_B1 * m + (1.0 - ADAM_B1) * g
    v = ADAM_B2 * v + (1.0 - ADAM_B2) * (g * g)
    m_hat = m / (1.0 - ADAM_B1 ** ADAM_STEP)
    v_hat = v / (1.0 - ADAM_B2 ** ADAM_STEP)
    delta = -ADAM_LR * (m_hat / (jnp.sqrt(v_hat) + ADAM_EPS) + ADAM_WD * w)
    return delta, m, v


def _adamw(w, g, m, v, name):
    rows, cols = w.shape
    tm = rows // 2 if (rows // 2) % 8 == 0 else rows
    return _rowwise(_adamw_math, [w, g, m, v], [], [(cols, F32)] * 3, [], tm=tm, name=name)


def _unshard_cols(gathered):
    n, r, c = gathered.shape
    return jnp.transpose(gathered, (1, 0, 2)).reshape(r, n * c)


def _shard_cols(full):
    r, nc = full.shape
    return jnp.transpose(full.reshape(r, N_CHIPS, nc // N_CHIPS), (1, 0, 2))


LATE = ["w_sb_up", "w_dil_up", "w_out", "w_ffn_in", "w_ffn_out"]


def _swap_middle(slabs4):
    return jnp.stack([slabs4[0], slabs4[2], slabs4[1], slabs4[3]])


def _late_weights(slabs, d_model, d_ff):
    g = dict(zip(LATE, slabs))
    return (_unshard_cols(g["w_sb_up"]), _unshard_cols(g["w_dil_up"]), g["w_out"].reshape(d_model, d_model),
            _unshard_cols(_swap_middle(g["w_ffn_in"])), g["w_ffn_out"].reshape(d_ff, d_model))


def _chip_major(grads, d_model, d_ff):
    row_sharded = {"w_out": d_model // N_CHIPS, "w_ffn_out": d_ff // N_CHIPS}
    out = []
    for k, g in grads.items():
        if k in row_sharded:
            out.append(g.reshape(N_CHIPS, row_sharded[k], g.shape[1]))
        else:
            out.append(_swap_middle(_shard_cols(g)) if k == "w_ffn_in" else _shard_cols(g))
    return out


def _pair_reduce(grads, d_model, d_ff):
    full = _chip_major(grads, d_model, d_ff)
    others = _pair_exchange(full, next(iter(grads)))
    return [_pair_sum(g, o, "grad_pair_sum_" + k) for g, o, k in zip(full, others, grads)]


def _chip_reduce(pair, landed, names):
    halves = [_chip_sum(p[0], l, "grad_chip_sum_" + k) for p, l, k in zip(pair, landed, names)]
    return dict(zip(names, _halves_to_full(halves, names[0])))


def _fwd_bwd(x, loss_target, g_mix, g_ffn, g_fin, wf_in, late_slabs):
    b_sz, s_len, d_model = x.shape
    t = b_sz * s_len
    d_ff = late_slabs[-1].shape[1] * N_CHIPS
    x2d = x.reshape(t, d_model)
    tgt2d = loss_target.reshape(t, d_model)

    (u,) = _rowwise(lambda xv, g: (_rms_stats(xv)[0] * g,), [x2d], [g_mix], [(d_model, BF16)], [], tm=512, name="norm_mix")
    qkv, (slab_ffn_out,) = _mm(u, wf_in, b_cols=(0, QKV_WIDTH), tm=2048, tn=768, tk=d_model, name="proj_qkv",
                               carry=_gather_carry(late_slabs[4:]))
    gates = _mm(u, wf_in, b_cols=(QKV_WIDTH, 2 * d_model), out_dtype=BF16, tm=t, tn=256, tk=d_model, name="proj_gates")
    qkv3 = qkv.reshape(b_sz, s_len, QKV_WIDTH)
    o_sb, small_slabs = _sb_fwd(qkv3, b_sz, s_len, _gather_carry(late_slabs[:3]))
    o_dl, lse, (slab_ffn_in,) = _dil_fwd(qkv3, b_sz, s_len, _gather_carry(late_slabs[3:4]))
    wf_sb_up, wf_dil_up, wf_out, wf_ffn_in, wf_ffn_out = _late_weights(
        list(small_slabs) + [slab_ffn_in, slab_ffn_out], d_model, d_ff)
    o_sb2, o_dl2 = o_sb.reshape(t, SB_WIDTH), o_dl.reshape(t, DIL_OUT_WIDTH)
    y_sb = _mm(o_sb2, wf_sb_up, out_dtype=BF16, tm=1024, tn=1024, tk=SB_WIDTH, name="sb_up")
    y_dl = _mm(o_dl2, wf_dil_up, out_dtype=BF16, tm=1024, tn=1024, tk=DIL_OUT_WIDTH, name="dil_up")

    def merge_fn(gt, ys, yd):
        return (_sigmoid(gt[:, :d_model]) * ys + _sigmoid(gt[:, d_model:]) * yd,)

    (merged,) = _rowwise(merge_fn, [gates, y_sb, y_dl], [], [(d_model, BF16)], [], tm=512, name="merge")
    x1 = _mm(merged, wf_out, add=x2d, tm=512, tn=1024, tk=d_model, name="mix_out")
    (u2,) = _rowwise(lambda xv, g: (_rms_stats(xv)[0] * g,), [x1], [g_ffn], [(d_model, BF16)], [], tm=512, name="norm_ffn")
    half_ff = d_ff // 2

    def act_fn(hv):
        gate = hv[:, :half_ff]
        return hv, gate * _sigmoid(gate) * hv[:, half_ff:]

    h, act = _mm(u2, wf_ffn_in, tm=512, tn=d_ff, tk=d_model, name="ffn_in",
                 epilogue=(act_fn, [], [], [(d_ff, BF16), (half_ff, BF16)], []))
    def head_fn(xv, tg, g):
        xhat, r = _rms_stats(xv)
        err = xhat * g - tg
        dy = err * (1.0 / d_model)
        dx, dg_rows = _rms_bwd(dy, xhat, r, g)
        loss_lanes = (0.5 / d_model) * jnp.sum(err * err, axis=0, keepdims=True)
        return dx, dx, jnp.sum(dg_rows, axis=0, keepdims=True), loss_lanes

    dx2, dx2_b, dg_fin, loss_lanes = _mm(
        act, wf_ffn_out, add=x1, tm=512, tn=1024, tk=d_ff, name="ffn_out",
        epilogue=(head_fn, [tgt2d], [g_fin], [(d_model, F32), (d_model, BF16)], [(1, d_model), (1, d_model)]))

    def dact_fn(da, hv):
        gate, up = hv[:, :half_ff], hv[:, half_ff:]
        sg = _sigmoid(gate)
        dgate = da * up * (sg * (1.0 + gate * (1.0 - sg)))
        return (jnp.concatenate([dgate, da * (gate * sg)], axis=1),)

    (dh,) = _mm(dx2_b, wf_ffn_out, tb=True, tm=512, tn=half_ff, tk=d_model, name="ffn_out_dx",
                epilogue=(dact_fn, [h], [], [(d_ff, BF16)], []))
    gw_ffn_out = _mm(act, dx2_b, ta=True, tm=256, tn=d_model, tk=t, name="ffn_out_dw")
    def norm_bwd_fn(du_, dres, xv, g):
        xhat, r = _rms_stats(xv)
        dx, dg_rows = _rms_bwd(du_, xhat, r, g)
        return dres + dx, jnp.sum(dg_rows, axis=0, keepdims=True)

    def norm_bwd_twice(*args):
        dx, dg = norm_bwd_fn(*args)
        return dx, dx, dg

    dx1, dx1_b, dg_ffn = _mm(dh, wf_ffn_in, tb=True, tm=512, tn=1024, tk=2 * d_ff, name="ffn_in_dx",
                             epilogue=(norm_bwd_twice, [dx2, x1], [g_ffn], [(d_model, F32), (d_model, BF16)], [(1, d_model)]))
    gw_ffn_in = _mm(u2, dh, ta=True, tm=d_model, tn=512, tk=t, name="ffn_in_dw")

    dmerged = _mm(dx1_b, wf_out, tb=True, out_dtype=BF16, tm=512, tn=1024, tk=d_model, name="mix_out_dx")
    gw_out = _mm(merged, dx1_b, ta=True, tm=256, tn=d_model, tk=t, name="mix_out_dw")

    def merge_bwd_fn(gt, ys, yd, dm):
        s_sb, s_dl = _sigmoid(gt[:, :d_model]), _sigmoid(gt[:, d_model:])
        dgates = jnp.concatenate([dm * ys * s_sb * (1.0 - s_sb), dm * yd * s_dl * (1.0 - s_dl)], axis=1)
        return dgates, dm * s_sb, dm * s_dl

    dgates, dy_sb, dy_dl = _rowwise(merge_bwd_fn, [gates, y_sb, y_dl, dmerged], [],
                                    [(2 * d_model, BF16), (d_model, BF16), (d_model, BF16)], [], tm=256, name="merge_bwd")
    do_sb = _mm(dy_sb, wf_sb_up, tb=True, out_dtype=BF16, tm=1024, tn=SB_WIDTH, tk=d_model, name="sb_up_dx")
    gw_sb_up = _mm(o_sb2, dy_sb, ta=True, tm=SB_WIDTH, tn=1024, tk=512, name="sb_up_dw")
    do_dl = _mm(dy_dl, wf_dil_up, tb=True, tm=1024, tn=DIL_OUT_WIDTH, tk=d_model, name="dil_up_dx")
    gw_dil_up = _mm(o_dl2, dy_dl, ta=True, tm=DIL_OUT_WIDTH, tn=1024, tk=512, name="dil_up_dw")
    late_grads = {"w_sb_up": gw_sb_up, "w_dil_up": gw_dil_up, "w_out": gw_out, "w_ffn_in": gw_ffn_in, "w_ffn_out": gw_ffn_out}
    pair = _pair_reduce(late_grads, d_model, d_ff)
    (dq_sb, dk_sb, dv_sb), landed_a = _sb_bwd(qkv3, o_sb, do_sb.reshape(b_sz, s_len, SB_WIDTH), b_sz, s_len,
                                             _chip_carry([p[1] for p in pair[:4]]))
    (dq_dl, dk_dl, dv_dl), landed_b = _dil_bwd(qkv3, o_dl, lse, do_dl.reshape(b_sz, s_len, DIL_OUT_WIDTH), b_sz, s_len,
                                               _chip_carry([p[1] for p in pair[4:]]))
    landed = list(landed_a) + list(landed_b)
    dproj = jnp.concatenate(
        [a.reshape(t, -1) for a in (dq_sb, dk_sb, dv_sb)]
        + [a.reshape(t, -1) for a in (dq_dl, dk_dl, dv_dl)] + [dgates], axis=1)
    gw_in = _mm(u, dproj, ta=True, tm=d_model, tn=256, tk=t, name="proj_dw")
    pair_in = _pair_reduce({"w_in": gw_in}, d_model, d_ff)
    (dx, dg_mix), landed_in = _mm(
        dproj, wf_in, tb=True, tm=512, tn=1024, tk=wf_in.shape[1], name="proj_dx",
        carry=_chip_carry([p[1] for p in pair_in]),
        epilogue=(norm_bwd_fn, [dx1, x2d], [g_mix], [(d_model, F32)], [(1, d_model)]))

    grads = _chip_reduce(pair, landed, LATE)
    grads.update(_chip_reduce(pair_in, landed_in, ["w_in"]))
    return dx, grads, dg_mix, dg_ffn, dg_fin, loss_lanes


def kernel(x, norm_mix_g, w_in, w_sb_up, w_dil_up, w_out, norm_ffn_g, w_ffn_in, w_ffn_out, norm_final_g, loss_target, m_norm_mix_g, m_w_in, m_w_sb_up, m_w_dil_up, m_w_out, m_norm_ffn_g, m_w_ffn_in, m_w_ffn_out, m_norm_final_g, v_norm_mix_g, v_w_in, v_w_sb_up, v_w_dil_up, v_w_out, v_norm_ffn_g, v_w_ffn_in, v_w_ffn_out, v_norm_final_g):
    b_sz, s_len, d_model = x.shape
    d_ff = w_ffn_out.shape[1] * N_CHIPS
    g_mix, g_ffn, g_fin = norm_mix_g, norm_ffn_g, norm_final_g.reshape(1, d_model)

    names = ["w_in", "w_sb_up", "w_dil_up", "w_out", "w_ffn_in", "w_ffn_out"]
    shards = {"w_in": w_in[0], "w_sb_up": w_sb_up[0], "w_dil_up": w_dil_up[0], "w_out": w_out[0],
              "w_ffn_in": w_ffn_in[0], "w_ffn_out": w_ffn_out[0]}
    (slab_in,) = _gather_weights([_cast_to_slab(shards["w_in"], "cast_w_in")])
    late_slabs = [_cast_to_slab(shards[k], "cast_" + k) for k in LATE]

    dx, grads, dg_mix, dg_ffn, dg_fin, loss_lanes = _fwd_bwd(
        x, loss_target, g_mix, g_ffn, g_fin, _unshard_cols(slab_in), late_slabs)

    small = jnp.concatenate([dg_mix, dg_ffn, dg_fin, loss_lanes, jnp.zeros((4, d_model), F32)], axis=0)
    small = _all_sum_small(small)
    loss = small[3, 0]
    gains = jnp.concatenate([g_mix, g_ffn, g_fin, jnp.zeros((5, d_model), F32)], axis=0)
    gains_m = jnp.concatenate([m_norm_mix_g, m_norm_ffn_g, m_norm_final_g.reshape(1, d_model), jnp.zeros((5, d_model), F32)], axis=0)
    gains_v = jnp.concatenate([v_norm_mix_g, v_norm_ffn_g, v_norm_final_g.reshape(1, d_model), jnp.ones((5, d_model), F32)], axis=0)
    gd, gm, gv = _rowwise(_adamw_math, [gains, small, gains_m, gains_v], [], [(d_model, F32)] * 3, [], tm=8, name="adamw_gains")

    moments = {"w_in": (m_w_in, v_w_in), "w_sb_up": (m_w_sb_up, v_w_sb_up), "w_dil_up": (m_w_dil_up, v_w_dil_up),
               "w_out": (m_w_out, v_w_out), "w_ffn_in": (m_w_ffn_in, v_w_ffn_in), "w_ffn_out": (m_w_ffn_out, v_w_ffn_out)}
    upd = {k: _adamw(shards[k], grads[k], moments[k][0][0], moments[k][1][0], "adamw_" + k) for k in names}

    def w_out_of(i):
        return [upd[k][i][None] for k in names]

    def ordered(mix, ws, ffn_g, fin):
        return [mix, ws[0], ws[1], ws[2], ws[3], ffn_g, ws[4], ws[5], fin]

    grad_ws = [grads[k][None] for k in names]
    outs = [loss, dx.reshape(b_sz, s_len, d_model)]
    outs += ordered(small[0:1], grad_ws, small[1:2], small[2])
    outs += ordered(gd[0:1], w_out_of(0), gd[1:2], gd[2])
    outs += ordered(gm[0:1], w_out_of(1), gm[1:2], gm[2])
    outs += ordered(gv[0:1], w_out_of(2), gv[1:2], gv[2])
    return tuple(outs)
```

```python
import functools
import math

import jax
import jax.numpy as jnp
from jax import lax
from jax.experimental import pallas as pl
from jax.experimental.pallas import tpu as pltpu

F32 = jnp.float32
BF16 = jnp.bfloat16
MESH = pl.DeviceIdType.MESH

HEAD_DIM = 64
SB_HEADS = 8
DIL_PAIRS = ((128, 1), (512, 4), (2048, 16))
DIL_HEADS_PER_GROUP = 4
DIL_HEADS = DIL_HEADS_PER_GROUP * len(DIL_PAIRS)
SB_WIDTH = SB_HEADS * HEAD_DIM
DIL_WIDTH = DIL_HEADS * HEAD_DIM
DIL_OUT_WIDTH = DIL_HEADS_PER_GROUP * HEAD_DIM
QKV_WIDTH = 3 * SB_WIDTH + 3 * DIL_WIDTH
RMS_EPS = 1e-6
ALIBI_MAX_BIAS = 8.0
ADAM_LR = 0.001
ADAM_B1 = 0.9
ADAM_B2 = 0.999
ADAM_EPS = 1e-08
ADAM_WD = 0.01
ADAM_STEP = 10

LANES = 128
BLK = 128
NEG = -1e30
EXP_UNDERFLOW = -104.0
SB_FWD_CHAINS = 4
SB_BWD_CHAINS = 4
DIL_CHAINS = 4
N_CHIPS = 4
VMEM_CAP = 56 * 1024 * 1024


def _vmem_limit(tile_bytes):
    return int(min(VMEM_CAP, max(32 * 1024 * 1024, 3 * tile_bytes + 8 * 1024 * 1024)))


def _nbytes(shape, dtype):
    return math.prod(shape) * jnp.dtype(dtype).itemsize


def _dot(a, b):
    return jnp.dot(a, b, preferred_element_type=F32)


def _dot_nt(a, b):
    return lax.dot_general(a, b, (((1,), (1,)), ((), ())), preferred_element_type=F32)


def _dot_tn(a, b):
    return lax.dot_general(a, b, (((0,), (0,)), ((), ())), preferred_element_type=F32)


def _split2(x):
    hi = x.astype(BF16)
    lo = (x - hi.astype(F32)).astype(BF16)
    return hi, lo


def _sigmoid(x):
    return 1.0 / (1.0 + jnp.exp(-x))


class _Carry:
    def __init__(self, arrays=(), out_shapes=(), aliased=False, sems=(), start=None, finish=None):
        self.arrays, self.out_shapes, self.aliased = list(arrays), list(out_shapes), aliased
        self.sems, self.start, self.finish = list(sems), start, finish

    def __bool__(self):
        return bool(self.arrays)

    def call_args(self, n_in, n_out):
        aliases = {n_in + k: n_out + k for k in range(len(self.arrays))} if self.aliased else {}
        return _hbm_specs(len(self.arrays)), _hbm_specs(len(self.out_shapes)), self.out_shapes, aliases, self.sems

    def run(self, refs, n_in, n_out, step, n_steps, compute):
        if not self:
            compute()
            return
        n_c, n_o, n_s = len(self.arrays), len(self.out_shapes), len(self.sems)
        ins = refs[n_in:n_in + n_c]
        outs = refs[n_in + n_c + n_out:n_in + n_c + n_out + n_o]
        sems = refs[len(refs) - n_s:]

        @pl.when(step == 0)
        def _():
            self.start(ins, outs, sems)

        compute()

        @pl.when(step == n_steps - 1)
        def _():
            self.finish(ins, outs, sems)


def _mm(a, b, *, ta=False, tb=False, add=None, out_dtype=F32, tm, tn, tk, name, carry=None, epilogue=None,
        b_cols=None):
    carry = carry or _Carry()
    n_car = len(carry.arrays)
    if ta:
        kdim, m = a.shape
    else:
        m, kdim = a.shape
    if tb:
        n, k2 = b.shape
    else:
        k2, n = b.shape
    col0 = 0
    if b_cols is not None:
        assert not tb and b_cols[0] % tn == 0, name
        col0, n = b_cols[0] // tn, b_cols[1]
    assert kdim == k2 and m % tm == 0 and n % tn == 0 and kdim % tk == 0, (name, a.shape, b.shape)
    nk = kdim // tk
    grid = (m // tm, n // tn, nk)
    a_mode = dict(pipeline_mode=pl.Buffered(1)) if grid[0] == 1 and nk == 1 else {}
    b_mode = dict(pipeline_mode=pl.Buffered(1)) if grid[1] == 1 and nk == 1 else {}
    a_spec = (pl.BlockSpec((tk, tm), lambda i, j, k: (k, i), **a_mode) if ta
              else pl.BlockSpec((tm, tk), lambda i, j, k: (i, k), **a_mode))
    b_spec = (pl.BlockSpec((tn, tk), lambda i, j, k: (j, k), **b_mode) if tb
              else pl.BlockSpec((tk, tn), lambda i, j, k: (k, j + col0), **b_mode))
    o_spec = pl.BlockSpec((tm, tn), lambda i, j, k: (i, j))
    dims = ((((0,) if ta else (1,)), ((1,) if tb else (0,))), ((), ()))
    has_add = add is not None
    if epilogue is None:
        ep_fn, ep_rows, ep_params, ep_outs, ep_accs = None, [], [], [], []
        out_sds, out_specs = [jax.ShapeDtypeStruct((m, n), out_dtype)], [o_spec]
    else:
        ep_fn, ep_rows, ep_params, ep_outs, ep_accs = epilogue
        assert grid[1] == 1 or not ep_accs, name
        out_sds = [jax.ShapeDtypeStruct((m, w * grid[1]), d) for w, d in ep_outs]
        out_sds += [jax.ShapeDtypeStruct(sh, F32) for sh in ep_accs]
        out_specs = [pl.BlockSpec((tm, w), lambda i, j, k: (i, j)) for w, _ in ep_outs]
        out_specs += [pl.BlockSpec(sh, lambda i, j, k: (0, 0)) for sh in ep_accs]
    n_main = len(out_sds)
    use_scratch = nk > 1 and (ep_fn is not None or jnp.dtype(out_dtype) != jnp.dtype(F32))
    n_in = 2 + has_add + len(ep_rows) + len(ep_params)

    def finish(total, refs, pid):
        outs = refs[n_in + n_car:n_in + n_car + n_main]
        if ep_fn is None:
            outs[0][...] = total.astype(out_dtype)
            return
        first = 2 + has_add
        rows = [r[...].astype(F32) for r in refs[first:first + len(ep_rows)]]
        params = [p[...] for p in refs[first + len(ep_rows):n_in]]
        res = ep_fn(total, *rows, *params)
        for o_ref, v in zip(outs[:len(ep_outs)], res):
            o_ref[...] = v.astype(o_ref.dtype)
        acc_refs = outs[len(ep_outs):]
        if acc_refs:
            @pl.when(pid[0] == 0)
            def _():
                for r in acc_refs:
                    r[...] = jnp.zeros(r.shape, F32)

            for r, v in zip(acc_refs, res[len(ep_outs):]):
                r[...] += v

    def compute(refs, pid):
        a_ref, b_ref = refs[0], refs[1]
        add_ref = refs[2] if has_add else None
        prod = lax.dot_general(a_ref[...].astype(BF16), b_ref[...].astype(BF16), dims, preferred_element_type=F32)
        if nk == 1:
            finish(prod + add_ref[...] if has_add else prod, refs, pid)
            return
        acc_ref = refs[n_in + n_car + n_main + len(carry.out_shapes)] if use_scratch else refs[n_in + n_car]
        k = pid[2]

        @pl.when(k == 0)
        def _():
            acc_ref[...] = prod + add_ref[...] if has_add else prod

        @pl.when(k > 0)
        def _():
            acc_ref[...] += prod

        if use_scratch:
            @pl.when(k == nk - 1)
            def _():
                finish(acc_ref[...], refs, pid)

    def body(*refs):
        pid = (pl.program_id(0), pl.program_id(1), pl.program_id(2))
        step = (pid[0] * grid[1] + pid[1]) * nk + pid[2]
        carry.run(refs, n_in, n_main, step, grid[0] * grid[1] * nk, lambda: compute(refs, pid))

    tile_bytes = (_nbytes((tm, tk), a.dtype) + _nbytes((tk, tn), b.dtype) + 2 * _nbytes((tm, tn), F32)
                  + (_nbytes((tm, tn), F32) if has_add else 0)
                  + sum(_nbytes((tm, r.shape[1]), r.dtype) for r in ep_rows) + sum(_nbytes((tm, w), d) for w, d in ep_outs))
    in_specs = [a_spec, b_spec] + ([o_spec] if has_add else [])
    in_specs += [pl.BlockSpec((tm, r.shape[1] // grid[1]), lambda i, j, k: (i, j)) for r in ep_rows]
    in_specs += [pl.BlockSpec(p.shape, lambda i, j, k: (0, 0)) for p in ep_params]
    args = (a, b) + ((add,) if has_add else ()) + tuple(ep_rows) + tuple(ep_params)
    scratch = [pltpu.VMEM((tm, tn), F32)] if use_scratch else []
    serial = bool(carry) or bool(ep_accs)
    c_in, c_out, c_shapes, c_alias, c_sems = carry.call_args(n_in, n_main)
    res = pl.pallas_call(
        body, name=name, grid=grid,
        in_specs=in_specs + c_in, out_specs=out_specs + c_out, out_shape=out_sds + c_shapes,
        input_output_aliases=c_alias, scratch_shapes=scratch + c_sems,
        compiler_params=pltpu.CompilerParams(
            dimension_semantics=("arbitrary",) * 3 if serial else ("parallel", "parallel", "arbitrary"),
            vmem_limit_bytes=_vmem_limit(tile_bytes)),
    )(*args, *carry.arrays)
    main = res[0] if ep_fn is None else list(res[:n_main])
    return (main, res[n_main:]) if carry else main


def _rowwise(fn, rows, params, outs, accs, *, tm, name):
    t = rows[0].shape[0]
    assert t % tm == 0, (name, t, tm)
    n_r, n_p, n_o = len(rows), len(params), len(outs)

    def body(*refs):
        vals = [r[...].astype(F32) for r in refs[:n_r]] + [p[...] for p in refs[n_r:n_r + n_p]]
        res = fn(*vals)
        o_refs = refs[n_r + n_p:n_r + n_p + n_o]
        a_refs = refs[n_r + n_p + n_o:]
        for o_ref, v in zip(o_refs, res[:n_o]):
            o_ref[...] = v.astype(o_ref.dtype)
        if accs:
            @pl.when(pl.program_id(0) == 0)
            def _():
                for a_ref in a_refs:
                    a_ref[...] = jnp.zeros(a_ref.shape, F32)

            for a_ref, v in zip(a_refs, res[n_o:]):
                a_ref[...] += v

    in_specs = [pl.BlockSpec((tm, r.shape[1]), lambda i: (i, 0)) for r in rows]
    in_specs += [pl.BlockSpec(p.shape, lambda i: (0, 0)) for p in params]
    out_specs = [pl.BlockSpec((tm, w), lambda i: (i, 0)) for w, _ in outs]
    out_specs += [pl.BlockSpec(s, lambda i: (0, 0)) for s in accs]
    out_shape = [jax.ShapeDtypeStruct((t, w), d) for w, d in outs]
    out_shape += [jax.ShapeDtypeStruct(s, F32) for s in accs]
    tile_bytes = sum(_nbytes((tm, r.shape[1]), r.dtype) for r in rows) + sum(_nbytes((tm, w), F32) for w, _ in outs)
    res = pl.pallas_call(
        body, name=name, grid=(t // tm,), in_specs=in_specs, out_specs=out_specs, out_shape=out_shape,
        compiler_params=pltpu.CompilerParams(
            dimension_semantics=("arbitrary",) if accs else ("parallel",),
            vmem_limit_bytes=_vmem_limit(2 * tile_bytes)),
    )(*rows, *params)
    return res


def _rms_stats(x):
    r = lax.rsqrt(jnp.mean(x * x, axis=-1, keepdims=True) + RMS_EPS)
    return x * r, r


def _rms_bwd(dy, xhat, r, g):
    dxhat = dy * g
    dx = r * (dxhat - xhat * jnp.mean(dxhat * xhat, axis=-1, keepdims=True))
    return dx, dy * xhat


def _sb_consts():
    lane = lax.broadcasted_iota(jnp.int32, (BLK, LANES), 1)
    head0 = lane < HEAD_DIM
    row = lax.broadcasted_iota(jnp.int32, (2 * BLK, BLK), 0) % BLK
    col = lax.broadcasted_iota(jnp.int32, (2 * BLK, BLK), 1)
    causal = col < row
    jj = lax.broadcasted_iota(jnp.int32, (BLK, BLK), 0)
    ss = lax.broadcasted_iota(jnp.int32, (BLK, BLK), 1)
    suffix = jnp.where(jj > ss, 1.0, 0.0).astype(BF16)
    return head0, causal, suffix


def _stack_heads(x, head0):
    zero = jnp.zeros_like(x)
    return jnp.concatenate([jnp.where(head0, x, zero), jnp.where(head0, zero, x)], axis=0)


def _sb_logits(z, causal, masked):
    sp = jnp.log(1.0 + jnp.exp(-jnp.abs(z)))
    log_keep = -(jnp.maximum(z, 0.0) + sp)
    log_beta = jnp.minimum(z, 0.0) - sp
    if masked:
        log_keep = jnp.where(causal, log_keep, 0.0)
    return log_keep, log_beta


def _suffix_sums(x, suffix):
    hi, lo = _split2(x)
    after = _dot(hi, suffix) + _dot(lo, suffix)
    total = jnp.broadcast_to(after[:, 0:1] + x[:, 0:1], x.shape)
    return after, total


def _sb_walk_back(i, state, per_chain, tile):
    def alive(st):
        worst = functools.reduce(jnp.maximum, [st[p][:, 0:1] for p in range(0, len(st), per_chain)])
        return jnp.max(worst) > EXP_UNDERFLOW

    def cond(c):
        return jnp.logical_and(c[0] < i, alive(c[1]))

    def body(c):
        return c[0] + 1, tile(i - 1 - c[0], c[1], False)

    return lax.while_loop(cond, body, (jnp.int32(0), state))[1]


def _lane_blocks(x, n):
    return [x[:, p * LANES:(p + 1) * LANES] for p in range(n)]


def _sb_fwd(qkv, b_sz, s_len, carry):
    nq = s_len // BLK
    n_pairs = SB_WIDTH // LANES
    ch = SB_FWD_CHAINS
    n_steps = n_pairs // ch
    scale = 1.0 / math.sqrt(HEAD_DIM)

    def compute(q_ref, k_ref, v_ref, o_ref):
        head0, causal, suffix = _sb_consts()

        def q_block(i, _):
            qs = pl.multiple_of(i * BLK, BLK)
            q_all = (q_ref[pl.ds(qs, BLK), :] * scale).astype(BF16)
            q01 = [_stack_heads(q, head0) for q in _lane_blocks(q_all, ch)]

            def tile(j, state, masked):
                ks = pl.multiple_of(j * BLK, BLK)
                ks_ = _lane_blocks(k_ref[pl.ds(ks, BLK), :].astype(BF16), ch)
                vs_ = _lane_blocks(v_ref[pl.ds(ks, BLK), :].astype(BF16), ch)
                zs = [_dot_nt(q01[p], ks_[p]) for p in range(ch)]
                logits = [_sb_logits(z, causal, masked) for z in zs]
                sums = [_suffix_sums(lg[0], suffix) for lg in logits]
                out = []
                for p in range(ch):
                    carry, acc = state[2 * p], state[2 * p + 1]
                    after, total = sums[p]
                    a = jnp.exp(logits[p][1] + carry + after)
                    if masked:
                        a = jnp.where(causal, a, 0.0)
                    a_hi, a_lo = _split2(a)
                    a_cat = jnp.concatenate([a_hi[:BLK], a_hi[BLK:], a_lo[:BLK], a_lo[BLK:]], axis=1)
                    v01 = _stack_heads(vs_[p], head0)
                    out += [carry + total, acc + _dot(a_cat, jnp.concatenate([v01, v01], axis=0))]
                return tuple(out)

            state = (jnp.zeros((2 * BLK, BLK), F32), jnp.zeros((BLK, LANES), F32)) * ch
            state = tile(i, state, True)
            state = _sb_walk_back(i, state, 2, tile)
            o_ref[pl.ds(qs, BLK), :] = jnp.concatenate([state[2 * p + 1] for p in range(ch)], axis=1)
            return 0

        lax.fori_loop(0, nq, q_block, 0)

    def body(*refs):
        step = pl.program_id(0) * n_steps + pl.program_id(1)
        o_ref = refs[3 + len(carry.arrays)]
        carry.run(refs, 3, 1, step, b_sz * n_steps, lambda: compute(refs[0], refs[1], refs[2], o_ref))

    blk = lambda off: pl.BlockSpec((None, s_len, ch * LANES), lambda b, p: (b, 0, off + p))
    c_in, c_out, c_shapes, c_alias, c_sems = carry.call_args(3, 1)
    res = pl.pallas_call(
        body, name="sb_fwd", grid=(b_sz, n_steps),
        in_specs=[blk(0), blk(n_steps), blk(2 * n_steps)] + c_in, out_specs=[blk(0)] + c_out,
        out_shape=[jax.ShapeDtypeStruct((b_sz, s_len, SB_WIDTH), F32)] + c_shapes,
        input_output_aliases=c_alias, scratch_shapes=c_sems,
        compiler_params=pltpu.CompilerParams(dimension_semantics=("arbitrary", "arbitrary"),
                                             vmem_limit_bytes=VMEM_CAP),
    )(qkv, qkv, qkv, *carry.arrays)
    return res[0], res[1:]


def _sb_bwd(qkv, o_sb, do_sb, b_sz, s_len, carry):
    nq = s_len // BLK
    n_pairs = SB_WIDTH // LANES
    ch = SB_BWD_CHAINS
    n_steps = n_pairs // ch
    scale = 1.0 / math.sqrt(HEAD_DIM)

    def compute(q_ref, k_ref, v_ref, o_ref, do_ref, dq_ref, dk_ref, dv_ref, dk_acc, dv_acc):
        head0, causal, suffix = _sb_consts()
        lrow = lax.broadcasted_iota(jnp.int32, (LANES, LANES), 0)
        ones_h0 = jnp.where(lrow < HEAD_DIM, 1.0, 0.0).astype(BF16)
        ones_h1 = jnp.where(lrow >= HEAD_DIM, 1.0, 0.0).astype(BF16)
        dk_acc[...] = jnp.zeros(dk_acc.shape, F32)
        dv_acc[...] = jnp.zeros(dv_acc.shape, F32)

        def q_block(i, _):
            qs = pl.multiple_of(i * BLK, BLK)
            q_all = (q_ref[pl.ds(qs, BLK), :] * scale).astype(BF16)
            do_all = do_ref[pl.ds(qs, BLK), :].astype(BF16)
            dd_all = do_all.astype(F32) * o_ref[pl.ds(qs, BLK), :]
            q01 = [_stack_heads(q, head0) for q in _lane_blocks(q_all, ch)]
            do01 = [_stack_heads(d, head0) for d in _lane_blocks(do_all, ch)]
            tot = []
            for dd in _lane_blocks(dd_all, ch):
                dd_hi, dd_lo = _split2(dd)
                tot.append(jnp.concatenate([_dot(dd_hi, ones_h0) + _dot(dd_lo, ones_h0),
                                            _dot(dd_hi, ones_h1) + _dot(dd_lo, ones_h1)], axis=0))

            def tile(j, state, masked):
                ks = pl.multiple_of(j * BLK, BLK)
                ks_ = _lane_blocks(k_ref[pl.ds(ks, BLK), :].astype(BF16), ch)
                vs_ = _lane_blocks(v_ref[pl.ds(ks, BLK), :].astype(BF16), ch)
                zs = [_dot_nt(q01[p], ks_[p]) for p in range(ch)]
                das = [_dot_nt(do01[p], vs_[p]) for p in range(ch)]
                logits = [_sb_logits(z, causal, masked) for z in zs]
                sums = [_suffix_sums(lg[0], suffix) for lg in logits]
                a_s, e_s = [], []
                for p in range(ch):
                    a = jnp.exp(logits[p][1] + state[3 * p] + sums[p][0])
                    if masked:
                        a = jnp.where(causal, a, 0.0)
                    a_s.append(a)
                    e_s.append(a * das[p])
                e_sums = [_suffix_sums(e, suffix) for e in e_s]
                out, dks, dvs = [], [], []
                for p in range(ch):
                    carry, rcarry, dq = state[3 * p:3 * p + 3]
                    e = e_s[p]
                    before = tot[p] - (rcarry + e_sums[p][0] + e)
                    beta = jnp.exp(logits[p][1])
                    dz = e * (1.0 - beta) - beta * before
                    if masked:
                        dz = jnp.where(causal, dz, 0.0)
                    dz_b = dz.astype(BF16)
                    dks.append(_dot_tn(dz_b, q01[p]))
                    dvs.append(_dot_tn(a_s[p].astype(BF16), do01[p]))
                    out += [carry + sums[p][1], rcarry + e_sums[p][1], dq + _dot(dz_b, ks_[p])]
                dk_acc[pl.ds(ks, BLK), :] += jnp.concatenate(dks, axis=1)
                dv_acc[pl.ds(ks, BLK), :] += jnp.concatenate(dvs, axis=1)
                return tuple(out)

            state = (jnp.zeros((2 * BLK, BLK), F32),) * (3 * ch)
            state = tile(i, state, True)
            state = _sb_walk_back(i, state, 3, tile)
            dq = [jnp.where(head0, state[3 * p + 2][:BLK], state[3 * p + 2][BLK:]) for p in range(ch)]
            dq_ref[pl.ds(qs, BLK), :] = (jnp.concatenate(dq, axis=1) * scale).astype(dq_ref.dtype)
            return 0

        lax.fori_loop(0, nq, q_block, 0)
        dk_ref[...] = dk_acc[...].astype(dk_ref.dtype)
        dv_ref[...] = dv_acc[...].astype(dv_ref.dtype)

    def body(*refs):
        step = pl.program_id(0) * n_steps + pl.program_id(1)
        n_c, n_o = len(carry.arrays), len(carry.out_shapes)
        own = refs[:5] + refs[5 + n_c:8 + n_c] + refs[8 + n_c + n_o:10 + n_c + n_o]
        carry.run(refs, 5, 3, step, b_sz * n_steps, lambda: compute(*own))

    blk = lambda off: pl.BlockSpec((None, s_len, ch * LANES), lambda b, p: (b, 0, off + p))
    once = lambda off: pl.BlockSpec((None, s_len, ch * LANES), lambda b, p: (b, 0, off + p),
                                    pipeline_mode=pl.Buffered(1))
    out_sd = jax.ShapeDtypeStruct((b_sz, s_len, SB_WIDTH), BF16)
    c_in, c_out, c_shapes, c_alias, c_sems = carry.call_args(5, 3)
    res = pl.pallas_call(
        body, name="sb_bwd", grid=(b_sz, n_steps),
        in_specs=[once(0), once(n_steps), once(2 * n_steps), once(0), once(0)] + c_in,
        out_specs=[blk(0), blk(0), blk(0)] + c_out, out_shape=[out_sd, out_sd, out_sd] + c_shapes,
        input_output_aliases=c_alias,
        scratch_shapes=[pltpu.VMEM((s_len, ch * LANES), F32), pltpu.VMEM((s_len, ch * LANES), F32)] + c_sems,
        compiler_params=pltpu.CompilerParams(dimension_semantics=("arbitrary", "arbitrary"),
                                             vmem_limit_bytes=VMEM_CAP),
    )(qkv, qkv, qkv, o_sb, do_sb, *carry.arrays)
    return res[:3], res[3:]


def _dil_consts(group, pair_idx, dilation):
    lane = lax.broadcasted_iota(jnp.int32, (BLK, LANES), 1)
    head0 = lane < HEAD_DIM
    row = lax.broadcasted_iota(jnp.int32, (2 * BLK, BLK), 0)
    qa = row % BLK
    kb = lax.broadcasted_iota(jnp.int32, (2 * BLK, BLK), 1)
    head = (group * DIL_HEADS_PER_GROUP + 2 * pair_idx + row // BLK).astype(F32)
    slope = jnp.exp((-ALIBI_MAX_BIAS * math.log(2.0) / DIL_HEADS) * (head + 1.0))
    valid_cur = kb <= qa
    valid_prev = kb >= qa
    bias_cur = -slope * ((qa - kb) * dilation).astype(F32)
    bias_prev = -slope * ((BLK + qa - kb) * dilation).astype(F32)
    return head0, valid_cur, valid_prev, bias_cur, bias_prev


def _dil_units(s_len, dilation):
    nb = s_len // dilation // BLK
    return [(r, n) for r in range(dilation) for n in range(nb)]


def _dil_rows(n, r, dilation):
    if dilation == 1:
        return pl.ds(n * BLK, BLK)
    return pl.ds(n * BLK * dilation + r, BLK, stride=dilation)


def _dil_scores(q01, k, bias, valid):
    s = _dot_nt(q01, k) * (1.0 / math.sqrt(HEAD_DIM)) + bias
    return jnp.where(valid, s, NEG)


def _dil_fwd(qkv, b_sz, s_len, carry):
    n_pairs = DIL_OUT_WIDTH // LANES
    q_off = 3 * SB_WIDTH // LANES
    per_kind = DIL_WIDTH // LANES

    def compute(pair_idx, qkv_refs, o_ref, lse_ref, m_s, l_s):
        m_s[...] = jnp.full(m_s.shape, NEG, F32)
        l_s[...] = jnp.zeros(l_s.shape, F32)
        o_ref[...] = jnp.zeros(o_ref.shape, F32)
        for g, (_, dilation) in enumerate(DIL_PAIRS):
            q_ref, k_ref, v_ref = qkv_refs[3 * g:3 * g + 3]
            head0, valid_cur, valid_prev, bias_cur, bias_prev = _dil_consts(g, pair_idx, dilation)
            units = _dil_units(s_len, dilation)
            for u0 in range(0, len(units), DIL_CHAINS):
                group = units[u0:u0 + DIL_CHAINS]
                rows_of = [_dil_rows(n, r, dilation) for r, n in group]
                scores, values = [], []
                for (r, n), rows in zip(group, rows_of):
                    q01 = _stack_heads(q_ref[rows, :].astype(BF16), head0)
                    sc = [_dil_scores(q01, k_ref[rows, :].astype(BF16), bias_cur, valid_cur)]
                    vals = [_stack_heads(v_ref[rows, :].astype(BF16), head0)]
                    if n > 0:
                        prev = _dil_rows(n - 1, r, dilation)
                        sc.append(_dil_scores(q01, k_ref[prev, :].astype(BF16), bias_prev, valid_prev))
                        vals.append(_stack_heads(v_ref[prev, :].astype(BF16), head0))
                    scores.append(sc)
                    values.append(vals)
                stats = []
                for sc, rows in zip(scores, rows_of):
                    m_blk = functools.reduce(jnp.maximum, [jnp.max(x, axis=-1, keepdims=True) for x in sc])
                    m_old = jnp.concatenate([m_s.at[0][rows, :], m_s.at[1][rows, :]], axis=0)
                    l_old = jnp.concatenate([l_s.at[0][rows, :], l_s.at[1][rows, :]], axis=0)
                    m_new = jnp.maximum(m_old, m_blk)
                    probs = [jnp.exp(x - m_new) for x in sc]
                    l_blk = functools.reduce(jnp.add, [jnp.sum(p, axis=-1, keepdims=True) for p in probs])
                    alpha = jnp.exp(m_old - m_new)
                    stats.append((m_new, alpha * l_old + l_blk, alpha, probs))
                for (m_new, l_new, alpha, probs), vals, rows in zip(stats, values, rows_of):
                    alpha_tok = jnp.where(head0, alpha[:BLK], alpha[BLK:])
                    p_cat = jnp.concatenate(
                        [h for p in probs for h in (p[:BLK].astype(BF16), p[BLK:].astype(BF16))], axis=1)
                    o_ref[rows, :] = alpha_tok * o_ref[rows, :] + _dot(p_cat, jnp.concatenate(vals, axis=0))
                    m_s.at[0][rows, :] = m_new[:BLK]
                    m_s.at[1][rows, :] = m_new[BLK:]
                    l_s.at[0][rows, :] = l_new[:BLK]
                    l_s.at[1][rows, :] = l_new[BLK:]
        lane = lax.broadcasted_iota(jnp.int32, (BLK, LANES), 1)
        for c in range(s_len // BLK):
            rows = pl.ds(c * BLK, BLK)
            l0, l1 = l_s.at[0][rows, :], l_s.at[1][rows, :]
            o_ref[rows, :] = o_ref[rows, :] / jnp.where(lane < HEAD_DIM, l0, l1)
            lse_ref.at[0][rows, :] = m_s.at[0][rows, :] + jnp.log(l0)
            lse_ref.at[1][rows, :] = m_s.at[1][rows, :] + jnp.log(l1)

    def body(*refs):
        pair_idx = pl.program_id(1)
        step = pl.program_id(0) * n_pairs + pair_idx
        n_c, n_o = len(carry.arrays), len(carry.out_shapes)
        o_ref, lse_ref = refs[9 + n_c:11 + n_c]
        m_s, l_s = refs[11 + n_c + n_o:13 + n_c + n_o]
        carry.run(refs, 9, 2, step, b_sz * n_pairs, lambda: compute(pair_idx, refs[:9], o_ref, lse_ref, m_s, l_s))

    in_specs = []
    for g in range(len(DIL_PAIRS)):
        for kind in range(3):
            off = q_off + kind * per_kind + g * n_pairs
            in_specs.append(pl.BlockSpec((None, s_len, LANES), lambda b, p, off=off: (b, 0, off + p)))
    c_in, c_out, c_shapes, c_alias, c_sems = carry.call_args(9, 2)
    res = pl.pallas_call(
        body, name="dil_fwd", grid=(b_sz, n_pairs),
        in_specs=in_specs + c_in,
        out_specs=[pl.BlockSpec((None, s_len, LANES), lambda b, p: (b, 0, p)),
                   pl.BlockSpec((None, None, 2, s_len, LANES), lambda b, p: (b, p, 0, 0, 0))] + c_out,
        out_shape=[jax.ShapeDtypeStruct((b_sz, s_len, DIL_OUT_WIDTH), F32),
                   jax.ShapeDtypeStruct((b_sz, n_pairs, 2, s_len, LANES), F32)] + c_shapes,
        input_output_aliases=c_alias,
        scratch_shapes=[pltpu.VMEM((2, s_len, LANES), F32), pltpu.VMEM((2, s_len, LANES), F32)] + c_sems,
        compiler_params=pltpu.CompilerParams(dimension_semantics=("arbitrary", "arbitrary"),
                                             vmem_limit_bytes=VMEM_CAP),
    )(*([qkv] * 9), *carry.arrays)
    return res[0], res[1], res[2:]


def _dil_bwd(qkv, o_dl, lse, do_dl, b_sz, s_len, carry):
    n_pairs = DIL_OUT_WIDTH // LANES
    n_groups = len(DIL_PAIRS)
    q_off = 3 * SB_WIDTH // LANES
    per_kind = DIL_WIDTH // LANES

    def compute(pair_idx, group, q_ref, k_ref, v_ref, o_ref, lse_ref, do_ref, dq_ref, dk_ref, dv_ref, d_s, dq_s, dk_s, dv_s):
        lrow = lax.broadcasted_iota(jnp.int32, (LANES, LANES), 0)
        ones_h0 = jnp.where(lrow < HEAD_DIM, 1.0, 0.0).astype(BF16)
        ones_h1 = jnp.where(lrow >= HEAD_DIM, 1.0, 0.0).astype(BF16)
        for c in range(s_len // BLK):
            rows = pl.ds(c * BLK, BLK)
            dd_hi, dd_lo = _split2(do_ref[rows, :] * o_ref[rows, :])
            d_s.at[0][rows, :] = _dot(dd_hi, ones_h0) + _dot(dd_lo, ones_h0)
            d_s.at[1][rows, :] = _dot(dd_hi, ones_h1) + _dot(dd_lo, ones_h1)
        dk_s[...] = jnp.zeros(dk_s.shape, F32)
        dv_s[...] = jnp.zeros(dv_s.shape, F32)

        def one_group(g, dilation):
            head0, valid_cur, valid_prev, bias_cur, bias_prev = _dil_consts(g, pair_idx, dilation)
            units = _dil_units(s_len, dilation)
            scale = 1.0 / math.sqrt(HEAD_DIM)
            for u0 in range(0, len(units), DIL_CHAINS):
                chunk = units[u0:u0 + DIL_CHAINS]
                loaded = []
                for r, n in chunk:
                    rows = _dil_rows(n, r, dilation)
                    q01 = _stack_heads(q_ref[rows, :].astype(BF16), head0)
                    do01 = _stack_heads(do_ref[rows, :].astype(BF16), head0)
                    lse01 = jnp.concatenate([lse_ref.at[0][rows, :], lse_ref.at[1][rows, :]], axis=0)
                    d01 = jnp.concatenate([d_s.at[0][rows, :], d_s.at[1][rows, :]], axis=0)
                    blocks = [(rows, bias_cur, valid_cur)]
                    if n > 0:
                        blocks.append((_dil_rows(n - 1, r, dilation), bias_prev, valid_prev))
                    parts = []
                    for krows, bias, valid in blocks:
                        k = k_ref[krows, :].astype(BF16)
                        v = v_ref[krows, :].astype(BF16)
                        parts.append((krows, k, _dil_scores(q01, k, bias, valid), _dot_nt(do01, v)))
                    loaded.append((rows, q01, do01, lse01, d01, parts))
                grads = []
                for rows, q01, do01, lse01, d01, parts in loaded:
                    for krows, k, sc, dp in parts:
                        p = jnp.exp(sc - lse01)
                        grads.append((p.astype(BF16), (p * (dp - d01) * scale).astype(BF16)))
                it = iter(grads)
                updates = []
                for rows, q01, do01, lse01, d01, parts in loaded:
                    dq = jnp.zeros((2 * BLK, LANES), F32)
                    for krows, k, sc, dp in parts:
                        p_b, ds = next(it)
                        dq = dq + _dot(ds, k)
                        updates.append((krows, _dot_tn(ds, q01), _dot_tn(p_b, do01)))
                    dq_s[rows, :] = jnp.where(head0, dq[:BLK], dq[BLK:])
                for krows, dk, dv in updates:
                    dk_s[krows, :] = dk_s[krows, :] + dk
                    dv_s[krows, :] = dv_s[krows, :] + dv

        for g, (_, dilation) in enumerate(DIL_PAIRS):
            pl.when(group == g)(functools.partial(one_group, g, dilation))
        dq_ref[...] = dq_s[...].astype(dq_ref.dtype)
        dk_ref[...] = dk_s[...].astype(dk_ref.dtype)
        dv_ref[...] = dv_s[...].astype(dv_ref.dtype)

    def body(*refs):
        pair_idx, group = pl.program_id(1), pl.program_id(2)
        step = (pl.program_id(0) * n_pairs + pair_idx) * n_groups + group
        n_c, n_o = len(carry.arrays), len(carry.out_shapes)
        own = refs[:6] + refs[6 + n_c:9 + n_c] + refs[9 + n_c + n_o:13 + n_c + n_o]
        carry.run(refs, 6, 3, step, b_sz * n_pairs * n_groups, lambda: compute(pair_idx, group, *own))

    def qkv_spec(kind):
        return pl.BlockSpec((None, s_len, LANES),
                            lambda b, p, g: (b, 0, q_off + kind * per_kind + g * n_pairs + p))

    tok_spec = pl.BlockSpec((None, s_len, LANES), lambda b, p, g: (b, 0, p))
    out_spec = pl.BlockSpec((None, s_len, LANES), lambda b, p, g: (b, 0, g * n_pairs + p))
    out_sd = jax.ShapeDtypeStruct((b_sz, s_len, DIL_WIDTH), BF16)
    c_in, c_out, c_shapes, c_alias, c_sems = carry.call_args(6, 3)
    res = pl.pallas_call(
        body, name="dil_bwd", grid=(b_sz, n_pairs, n_groups),
        in_specs=[qkv_spec(0), qkv_spec(1), qkv_spec(2), tok_spec,
                  pl.BlockSpec((None, None, 2, s_len, LANES), lambda b, p, g: (b, p, 0, 0, 0)), tok_spec] + c_in,
        out_specs=[out_spec, out_spec, out_spec] + c_out,
        out_shape=[out_sd, out_sd, out_sd] + c_shapes,
        input_output_aliases=c_alias,
        scratch_shapes=[pltpu.VMEM((2, s_len, LANES), F32)] + [pltpu.VMEM((s_len, LANES), F32)] * 3 + c_sems,
        compiler_params=pltpu.CompilerParams(dimension_semantics=("arbitrary", "arbitrary", "arbitrary"),
                                             vmem_limit_bytes=VMEM_CAP),
    )(qkv, qkv, qkv, o_dl, lse, do_dl, *carry.arrays)
    return res[:3], res[3:]


def _mesh_pos():
    return lax.axis_index("x"), lax.axis_index("y"), lax.axis_index("c")


def _other_chips(x, y):
    return [(1 - x, y), (x, 1 - y), (1 - x, 1 - y)]


def _hbm_specs(n):
    return [pl.BlockSpec(memory_space=pl.ANY)] * n


def _cast_to_slab(w, name):
    rows, cols = w.shape
    mine = jnp.reshape(2 * lax.axis_index("x") + lax.axis_index("y"), (1,)).astype(jnp.int32)

    def body(idx_ref, w_ref, o_ref):
        o_ref[...] = w_ref[...].astype(BF16)

    return pl.pallas_call(
        body, name=name,
        grid_spec=pltpu.PrefetchScalarGridSpec(
            num_scalar_prefetch=1, grid=(1,),
            in_specs=[pl.BlockSpec((rows, cols), lambda i, idx: (0, 0))],
            out_specs=pl.BlockSpec((None, rows, cols), lambda i, idx: (idx[0], 0, 0))),
        out_shape=jax.ShapeDtypeStruct((N_CHIPS, rows, cols), BF16),
        compiler_params=pltpu.CompilerParams(vmem_limit_bytes=_vmem_limit(rows * cols * 6)),
    )(mine, w)


def _gather_issue(slabs, send_sems, recv_sems):
    x, y, c = _mesh_pos()
    for k, slab in enumerate(slabs):
        half = slab.shape[1] // 2
        rows = slab.at[2 * x + y, pl.ds(c * half, half), :]
        for r, (px, py) in enumerate(_other_chips(x, y)):
            pltpu.make_async_remote_copy(
                src_ref=rows, dst_ref=rows, send_sem=send_sems.at[6 * k + r], recv_sem=recv_sems.at[6 * k + r],
                device_id=(px, py, c), device_id_type=MESH).start()


def _gather_complete(slabs, send_sems, recv_sems):
    x, y, c = _mesh_pos()
    chips = _other_chips(x, y)

    def copy(k, sem, block, rows, to):
        ref = slabs[k].at[block, rows, :]
        return pltpu.make_async_remote_copy(
            src_ref=ref, dst_ref=ref, send_sem=send_sems.at[sem], recv_sem=recv_sems.at[sem],
            device_id=to, device_id_type=MESH)

    for k, slab in enumerate(slabs):
        half = slab.shape[1] // 2
        for r, (px, py) in enumerate(chips):
            copy(k, 6 * k + r, 2 * px + py, pl.ds(c * half, half), (px, py, c)).wait_recv()
            copy(k, 6 * k + 3 + r, 2 * px + py, pl.ds(c * half, half), (x, y, 1 - c)).start()
    for k, slab in enumerate(slabs):
        half = slab.shape[1] // 2
        for r, (px, py) in enumerate(chips):
            copy(k, 6 * k + 3 + r, 2 * px + py, pl.ds((1 - c) * half, half), (x, y, 1 - c)).wait_recv()
    for k, slab in enumerate(slabs):
        half = slab.shape[1] // 2
        for r, (px, py) in enumerate(chips):
            copy(k, 6 * k + r, 2 * x + y, pl.ds(c * half, half), (px, py, c)).wait_send()
            copy(k, 6 * k + 3 + r, 2 * px + py, pl.ds(c * half, half), (x, y, 1 - c)).wait_send()


def _gather_sems(n):
    return [pltpu.SemaphoreType.DMA((6 * n,)), pltpu.SemaphoreType.DMA((6 * n,))]


def _gather_carry(slabs):
    return _Carry(slabs, [jax.ShapeDtypeStruct(a.shape, a.dtype) for a in slabs], True, _gather_sems(len(slabs)),
                  lambda ins, outs, sems: _gather_issue(outs, *sems),
                  lambda ins, outs, sems: _gather_complete(outs, *sems))


def _gather_weights(slabs):
    n = len(slabs)

    def body(*refs):
        outs = refs[n:2 * n]
        send_sems, recv_sems = refs[2 * n:]
        _gather_issue(outs, send_sems, recv_sems)
        _gather_complete(outs, send_sems, recv_sems)

    return pl.pallas_call(
        body, name="gather_weights",
        in_specs=_hbm_specs(n), out_specs=_hbm_specs(n),
        out_shape=[jax.ShapeDtypeStruct(s.shape, s.dtype) for s in slabs],
        input_output_aliases={k: k for k in range(n)},
        scratch_shapes=_gather_sems(n),
    )(*slabs)


def _pair_exchange(grads, tag):
    n = len(grads)

    def body(*refs):
        ins, outs = refs[:n], refs[n:2 * n]
        send_sems, recv_sems = refs[2 * n:]
        x, y, c = _mesh_pos()
        copies = []
        for k in range(n):
            half = grads[k].shape[1] // 2
            cp = pltpu.make_async_remote_copy(
                src_ref=ins[k].at[:, pl.ds((1 - c) * half, half), :], dst_ref=outs[k],
                send_sem=send_sems.at[k], recv_sem=recv_sems.at[k],
                device_id=(x, y, 1 - c), device_id_type=MESH)
            cp.start()
            copies.append(cp)
        for cp in copies:
            cp.wait()

    return pl.pallas_call(
        body, name="grad_pair_exchange_" + tag,
        in_specs=_hbm_specs(n), out_specs=_hbm_specs(n),
        out_shape=[jax.ShapeDtypeStruct((N_CHIPS, g.shape[1] // 2, g.shape[2]), F32) for g in grads],
        scratch_shapes=[pltpu.SemaphoreType.DMA((n,)), pltpu.SemaphoreType.DMA((n,))],
    )(*grads)


def _pair_sum(grad, other, name):
    _, rows, cols = grad.shape
    half = rows // 2
    core = jnp.reshape(lax.axis_index("c"), (1,)).astype(jnp.int32)

    def body(core_ref, g_ref, p_ref, s_ref, sb_ref):
        s = g_ref[...] + p_ref[...]
        s_ref[...] = s
        sb_ref[...] = s.astype(BF16)

    blk = pl.BlockSpec((None, half, cols), lambda p, core_ref: (p, 0, 0))
    return pl.pallas_call(
        body, name=name,
        grid_spec=pltpu.PrefetchScalarGridSpec(
            num_scalar_prefetch=1, grid=(N_CHIPS,),
            in_specs=[pl.BlockSpec((None, half, cols), lambda p, core_ref: (p, core_ref[0], 0)), blk],
            out_specs=[blk, blk]),
        out_shape=[jax.ShapeDtypeStruct((N_CHIPS, half, cols), F32),
                   jax.ShapeDtypeStruct((N_CHIPS, half, cols), BF16)],
        compiler_params=pltpu.CompilerParams(dimension_semantics=("parallel",),
                                             vmem_limit_bytes=_vmem_limit(4 * half * cols * 4)),
    )(core, grad, other)


def _chip_copies(sums_bf16, lands, send_sems, recv_sems):
    x, y, c = _mesh_pos()
    return [pltpu.make_async_remote_copy(
        src_ref=sums_bf16[k].at[2 * px + py], dst_ref=lands[k].at[r],
        send_sem=send_sems.at[3 * k + r], recv_sem=recv_sems.at[3 * k + r],
        device_id=(px, py, c), device_id_type=MESH)
        for k in range(len(sums_bf16)) for r, (px, py) in enumerate(_other_chips(x, y))]


def _chip_carry(sums_bf16):
    def start(ins, outs, sems):
        for cp in _chip_copies(ins, outs, *sems):
            cp.start()

    def finish(ins, outs, sems):
        for cp in _chip_copies(ins, outs, *sems):
            cp.wait()

    return _Carry(sums_bf16, _chip_landing(sums_bf16), False, _chip_sems(len(sums_bf16)), start, finish)


def _chip_sems(n):
    return [pltpu.SemaphoreType.DMA((3 * n,)), pltpu.SemaphoreType.DMA((3 * n,))]


def _chip_landing(sums_bf16):
    return [jax.ShapeDtypeStruct((N_CHIPS - 1,) + s.shape[1:], BF16) for s in sums_bf16]


def _chip_sum(sums_f32, landed, name):
    _, rows, cols = sums_f32.shape
    x, y, c = _mesh_pos()
    idx = jnp.stack([2 * x + y, c]).astype(jnp.int32)

    def body(idx_ref, o_ref, l_ref, out_ref):
        out_ref[...] = ((o_ref[...] + l_ref[0].astype(F32)) + l_ref[1].astype(F32)) + l_ref[2].astype(F32)

    return pl.pallas_call(
        body, name=name,
        grid_spec=pltpu.PrefetchScalarGridSpec(
            num_scalar_prefetch=1, grid=(1,),
            in_specs=[pl.BlockSpec((None, rows, cols), lambda i, idx: (idx[0], 0, 0)),
                      pl.BlockSpec((N_CHIPS - 1, rows, cols), lambda i, idx: (0, 0, 0))],
            out_specs=pl.BlockSpec((rows, cols), lambda i, idx: (idx[1], 0))),
        out_shape=jax.ShapeDtypeStruct((2 * rows, cols), F32),
        compiler_params=pltpu.CompilerParams(vmem_limit_bytes=_vmem_limit(3 * rows * cols * 4)),
    )(idx, sums_f32, landed)


def _halves_to_full(fulls, tag):
    n = len(fulls)

    def body(*refs):
        outs = refs[n:2 * n]
        send_sems, recv_sems = refs[2 * n:]
        x, y, c = _mesh_pos()
        copies = []
        for k in range(n):
            half = fulls[k].shape[0] // 2
            rows = outs[k].at[pl.ds(c * half, half), :]
            cp = pltpu.make_async_remote_copy(
                src_ref=rows, dst_ref=rows, send_sem=send_sems.at[k], recv_sem=recv_sems.at[k],
                device_id=(x, y, 1 - c), device_id_type=MESH)
            cp.start()
            copies.append(cp)
        for k in range(n):
            half = fulls[k].shape[0] // 2
            theirs = outs[k].at[pl.ds((1 - c) * half, half), :]
            pltpu.make_async_remote_copy(
                src_ref=theirs, dst_ref=theirs, send_sem=send_sems.at[k], recv_sem=recv_sems.at[k],
                device_id=(x, y, 1 - c), device_id_type=MESH).wait_recv()
        for cp in copies:
            cp.wait_send()

    return pl.pallas_call(
        body, name="grad_halves_to_full_" + tag,
        in_specs=_hbm_specs(n), out_specs=_hbm_specs(n),
        out_shape=[jax.ShapeDtypeStruct(f.shape, F32) for f in fulls],
        input_output_aliases={k: k for k in range(n)},
        scratch_shapes=[pltpu.SemaphoreType.DMA((n,)), pltpu.SemaphoreType.DMA((n,))],
    )(*fulls)


def _all_sum_small(v):
    rows, cols = v.shape
    n_dev = 8

    def body(v_ref, out_ref, buf, send_sems, recv_sems):
        x, y, c = _mesh_pos()
        me = 4 * x + 2 * y + c
        buf[me] = v_ref[...]
        peers = []
        for r in range(1, n_dev):
            px = 1 - x if r & 4 else x
            py = 1 - y if r & 2 else y
            pc = 1 - c if r & 1 else c
            peers.append((px, py, pc))
        copies = []
        for r, peer in enumerate(peers):
            cp = pltpu.make_async_remote_copy(
                src_ref=v_ref, dst_ref=buf.at[me], send_sem=send_sems.at[r], recv_sem=recv_sems.at[r],
                device_id=peer, device_id_type=MESH)
            cp.start()
            copies.append(cp)
        for r, (px, py, pc) in enumerate(peers):
            pltpu.make_async_remote_copy(
                src_ref=v_ref, dst_ref=buf.at[4 * px + 2 * py + pc], send_sem=send_sems.at[r], recv_sem=recv_sems.at[r],
                device_id=(px, py, pc), device_id_type=MESH).wait_recv()
        for cp in copies:
            cp.wait_send()
        acc = buf[0]
        for d in range(1, n_dev):
            acc = acc + buf[d]
        out_ref[...] = acc
        out_ref[3:4, :] = jnp.broadcast_to(jnp.sum(acc[3:4, :], axis=1, keepdims=True), (1, cols))

    vm = pl.BlockSpec(memory_space=pltpu.VMEM)
    return pl.pallas_call(
        body, name="all_sum_small", in_specs=[vm], out_specs=vm,
        out_shape=jax.ShapeDtypeStruct((rows, cols), F32),
        scratch_shapes=[pltpu.VMEM((n_dev, rows, cols), F32),
                        pltpu.SemaphoreType.DMA((n_dev - 1,)), pltpu.SemaphoreType.DMA((n_dev - 1,))],
    )(v)


def _adamw_math(w, g, m, v):
    m = ADAM_B1 * m + (1.0 - ADAM_B1) * g
    v = ADAM_B2 * v + (1.0 - ADAM_B2) * (g * g)
    m_hat = m / (1.0 - ADAM_B1 ** ADAM_STEP)
    v_hat = v / (1.0 - ADAM_B2 ** ADAM_STEP)
    delta = -ADAM_LR * (m_hat / (jnp.sqrt(v_hat) + ADAM_EPS) + ADAM_WD * w)
    return delta, m, v


def _adamw(w, g, m, v, name):
    rows, cols = w.shape
    tm = rows // 2 if (rows // 2) % 8 == 0 else rows
    return _rowwise(_adamw_math, [w, g, m, v], [], [(cols, F32)] * 3, [], tm=tm, name=name)


def _unshard_cols(gathered):
    n, r, c = gathered.shape
    return jnp.transpose(gathered, (1, 0, 2)).reshape(r, n * c)


def _shard_cols(full):
    r, nc = full.shape
    return jnp.transpose(full.reshape(r, N_CHIPS, nc // N_CHIPS), (1, 0, 2))


LATE = ["w_sb_up", "w_dil_up", "w_out", "w_ffn_in", "w_ffn_out"]


def _swap_middle(slabs4):
    return jnp.stack([slabs4[0], slabs4[2], slabs4[1], slabs4[3]])


def _late_weights(slabs, d_model, d_ff):
    g = dict(zip(LATE, slabs))
    return (_unshard_cols(g["w_sb_up"]), _unshard_cols(g["w_dil_up"]), g["w_out"].reshape(d_model, d_model),
            _unshard_cols(_swap_middle(g["w_ffn_in"])), g["w_ffn_out"].reshape(d_ff, d_model))


def _chip_major(grads, d_model, d_ff):
    row_sharded = {"w_out": d_model // N_CHIPS, "w_ffn_out": d_ff // N_CHIPS}
    out = []
    for k, g in grads.items():
        if k in row_sharded:
            out.append(g.reshape(N_CHIPS, row_sharded[k], g.shape[1]))
        else:
            out.append(_swap_middle(_shard_cols(g)) if k == "w_ffn_in" else _shard_cols(g))
    return out


def _pair_reduce(grads, d_model, d_ff):
    full = _chip_major(grads, d_model, d_ff)
    others = _pair_exchange(full, next(iter(grads)))
    return [_pair_sum(g, o, "grad_pair_sum_" + k) for g, o, k in zip(full, others, grads)]


def _chip_reduce(pair, landed, names):
    halves = [_chip_sum(p[0], l, "grad_chip_sum_" + k) for p, l, k in zip(pair, landed, names)]
    return dict(zip(names, _halves_to_full(halves, names[0])))


def _fwd_bwd(x, loss_target, g_mix, g_ffn, g_fin, wf_in, late_slabs):
    b_sz, s_len, d_model = x.shape
    t = b_sz * s_len
    d_ff = late_slabs[-1].shape[1] * N_CHIPS
    x2d = x.reshape(t, d_model)
    tgt2d = loss_target.reshape(t, d_model)

    (u,) = _rowwise(lambda xv, g: (_rms_stats(xv)[0] * g,), [x2d], [g_mix], [(d_model, BF16)], [], tm=512, name="norm_mix")
    qkv, (slab_ffn_out,) = _mm(u, wf_in, b_cols=(0, QKV_WIDTH), tm=2048, tn=768, tk=d_model, name="proj_qkv",
                               carry=_gather_carry(late_slabs[4:]))
    gates = _mm(u, wf_in, b_cols=(QKV_WIDTH, 2 * d_model), out_dtype=BF16, tm=t, tn=256, tk=d_model, name="proj_gates")
    qkv3 = qkv.reshape(b_sz, s_len, QKV_WIDTH)
    o_sb, small_slabs = _sb_fwd(qkv3, b_sz, s_len, _gather_carry(late_slabs[:3]))
    o_dl, lse, (slab_ffn_in,) = _dil_fwd(qkv3, b_sz, s_len, _gather_carry(late_slabs[3:4]))
    wf_sb_up, wf_dil_up, wf_out, wf_ffn_in, wf_ffn_out = _late_weights(
        list(small_slabs) + [slab_ffn_in, slab_ffn_out], d_model, d_ff)
    o_sb2, o_dl2 = o_sb.reshape(t, SB_WIDTH), o_dl.reshape(t, DIL_OUT_WIDTH)
    y_sb = _mm(o_sb2, wf_sb_up, out_dtype=BF16, tm=1024, tn=1024, tk=SB_WIDTH, name="sb_up")
    y_dl = _mm(o_dl2, wf_dil_up, out_dtype=BF16, tm=1024, tn=1024, tk=DIL_OUT_WIDTH, name="dil_up")

    def merge_fn(gt, ys, yd):
        return (_sigmoid(gt[:, :d_model]) * ys + _sigmoid(gt[:, d_model:]) * yd,)

    (merged,) = _rowwise(merge_fn, [gates, y_sb, y_dl], [], [(d_model, BF16)], [], tm=512, name="merge")
    x1 = _mm(merged, wf_out, add=x2d, tm=512, tn=1024, tk=d_model, name="mix_out")
    (u2,) = _rowwise(lambda xv, g: (_rms_stats(xv)[0] * g,), [x1], [g_ffn], [(d_model, BF16)], [], tm=512, name="norm_ffn")
    half_ff = d_ff // 2

    def act_fn(hv):
        gate = hv[:, :half_ff]
        return hv, gate * _sigmoid(gate) * hv[:, half_ff:]

    h, act = _mm(u2, wf_ffn_in, tm=512, tn=d_ff, tk=d_model, name="ffn_in",
                 epilogue=(act_fn, [], [], [(d_ff, BF16), (half_ff, BF16)], []))
    def head_fn(xv, tg, g):
        xhat, r = _rms_stats(xv)
        err = xhat * g - tg
        dy = err * (1.0 / d_model)
        dx, dg_rows = _rms_bwd(dy, xhat, r, g)
        loss_lanes = (0.5 / d_model) * jnp.sum(err * err, axis=0, keepdims=True)
        return dx, dx, jnp.sum(dg_rows, axis=0, keepdims=True), loss_lanes

    dx2, dx2_b, dg_fin, loss_lanes = _mm(
        act, wf_ffn_out, add=x1, tm=512, tn=1024, tk=d_ff, name="ffn_out",
        epilogue=(head_fn, [tgt2d], [g_fin], [(d_model, F32), (d_model, BF16)], [(1, d_model), (1, d_model)]))

    def dact_fn(da, hv):
        gate, up = hv[:, :half_ff], hv[:, half_ff:]
        sg = _sigmoid(gate)
        dgate = da * up * (sg * (1.0 + gate * (1.0 - sg)))
        return (jnp.concatenate([dgate, da * (gate * sg)], axis=1),)

    (dh,) = _mm(dx2_b, wf_ffn_out, tb=True, tm=512, tn=half_ff, tk=d_model, name="ffn_out_dx",
                epilogue=(dact_fn, [h], [], [(d_ff, BF16)], []))
    gw_ffn_out = _mm(act, dx2_b, ta=True, tm=256, tn=d_model, tk=t, name="ffn_out_dw")
    def norm_bwd_fn(du_, dres, xv, g):
        xhat, r = _rms_stats(xv)
        dx, dg_rows = _rms_bwd(du_, xhat, r, g)
        return dres + dx, jnp.sum(dg_rows, axis=0, keepdims=True)

    def norm_bwd_twice(*args):
        dx, dg = norm_bwd_fn(*args)
        return dx, dx, dg

    dx1, dx1_b, dg_ffn = _mm(dh, wf_ffn_in, tb=True, tm=512, tn=1024, tk=2 * d_ff, name="ffn_in_dx",
                             epilogue=(norm_bwd_twice, [dx2, x1], [g_ffn], [(d_model, F32), (d_model, BF16)], [(1, d_model)]))
    gw_ffn_in = _mm(u2, dh, ta=True, tm=d_model, tn=512, tk=t, name="ffn_in_dw")

    dmerged = _mm(dx1_b, wf_out, tb=True, out_dtype=BF16, tm=512, tn=1024, tk=d_model, name="mix_out_dx")
    gw_out = _mm(merged, dx1_b, ta=True, tm=256, tn=d_model, tk=t, name="mix_out_dw")

    def merge_bwd_fn(gt, ys, yd, dm):
        s_sb, s_dl = _sigmoid(gt[:, :d_model]), _sigmoid(gt[:, d_model:])
        dgates = jnp.concatenate([dm * ys * s_sb * (1.0 - s_sb), dm * yd * s_dl * (1.0 - s_dl)], axis=1)
        return dgates, dm * s_sb, dm * s_dl

    dgates, dy_sb, dy_dl = _rowwise(merge_bwd_fn, [gates, y_sb, y_dl, dmerged], [],
                                    [(2 * d_model, BF16), (d_model, BF16), (d_model, BF16)], [], tm=256, name="merge_bwd")
    do_sb = _mm(dy_sb, wf_sb_up, tb=True, out_dtype=BF16, tm=1024, tn=SB_WIDTH, tk=d_model, name="sb_up_dx")
    gw_sb_up = _mm(o_sb2, dy_sb, ta=True, tm=SB_WIDTH, tn=1024, tk=512, name="sb_up_dw")
    do_dl = _mm(dy_dl, wf_dil_up, tb=True, tm=1024, tn=DIL_OUT_WIDTH, tk=d_model, name="dil_up_dx")
    gw_dil_up = _mm(o_dl2, dy_dl, ta=True, tm=DIL_OUT_WIDTH, tn=1024, tk=512, name="dil_up_dw")
    late_grads = {"w_sb_up": gw_sb_up, "w_dil_up": gw_dil_up, "w_out": gw_out, "w_ffn_in": gw_ffn_in, "w_ffn_out": gw_ffn_out}
    pair = _pair_reduce(late_grads, d_model, d_ff)
    (dq_sb, dk_sb, dv_sb), landed_a = _sb_bwd(qkv3, o_sb, do_sb.reshape(b_sz, s_len, SB_WIDTH), b_sz, s_len,
                                             _chip_carry([p[1] for p in pair[:4]]))
    (dq_dl, dk_dl, dv_dl), landed_b = _dil_bwd(qkv3, o_dl, lse, do_dl.reshape(b_sz, s_len, DIL_OUT_WIDTH), b_sz, s_len,
                                               _chip_carry([p[1] for p in pair[4:]]))
    landed = list(landed_a) + list(landed_b)
    dproj = jnp.concatenate(
        [a.reshape(t, -1) for a in (dq_sb, dk_sb, dv_sb)]
        + [a.reshape(t, -1) for a in (dq_dl, dk_dl, dv_dl)] + [dgates], axis=1)
    gw_in = _mm(u, dproj, ta=True, tm=d_model, tn=256, tk=t, name="proj_dw")
    pair_in = _pair_reduce({"w_in": gw_in}, d_model, d_ff)
    (dx, dg_mix), landed_in = _mm(
        dproj, wf_in, tb=True, tm=512, tn=1024, tk=wf_in.shape[1], name="proj_dx",
        carry=_chip_carry([p[1] for p in pair_in]),
        epilogue=(norm_bwd_fn, [dx1, x2d], [g_mix], [(d_model, F32)], [(1, d_model)]))

    grads = _chip_reduce(pair, landed, LATE)
    grads.update(_chip_reduce(pair_in, landed_in, ["w_in"]))
    return dx, grads, dg_mix, dg_ffn, dg_fin, loss_lanes


def kernel(x, norm_mix_g, w_in, w_sb_up, w_dil_up, w_out, norm_ffn_g, w_ffn_in, w_ffn_out, norm_final_g, loss_target, m_norm_mix_g, m_w_in, m_w_sb_up, m_w_dil_up, m_w_out, m_norm_ffn_g, m_w_ffn_in, m_w_ffn_out, m_norm_final_g, v_norm_mix_g, v_w_in, v_w_sb_up, v_w_dil_up, v_w_out, v_norm_ffn_g, v_w_ffn_in, v_w_ffn_out, v_norm_final_g):
    b_sz, s_len, d_model = x.shape
    d_ff = w_ffn_out.shape[1] * N_CHIPS
    g_mix, g_ffn, g_fin = norm_mix_g, norm_ffn_g, norm_final_g.reshape(1, d_model)

    names = ["w_in", "w_sb_up", "w_dil_up", "w_out", "w_ffn_in", "w_ffn_out"]
    shards = {"w_in": w_in[0], "w_sb_up": w_sb_up[0], "w_dil_up": w_dil_up[0], "w_out": w_out[0],
              "w_ffn_in": w_ffn_in[0], "w_ffn_out": w_ffn_out[0]}
    (slab_in,) = _gather_weights([_cast_to_slab(shards["w_in"], "cast_w_in")])
    late_slabs = [_cast_to_slab(shards[k], "cast_" + k) for k in LATE]

    dx, grads, dg_mix, dg_ffn, dg_fin, loss_lanes = _fwd_bwd(
        x, loss_target, g_mix, g_ffn, g_fin, _unshard_cols(slab_in), late_slabs)

    small = jnp.concatenate([dg_mix, dg_ffn, dg_fin, loss_lanes, jnp.zeros((4, d_model), F32)], axis=0)
    small = _all_sum_small(small)
    loss = small[3, 0]
    gains = jnp.concatenate([g_mix, g_ffn, g_fin, jnp.zeros((5, d_model), F32)], axis=0)
    gains_m = jnp.concatenate([m_norm_mix_g, m_norm_ffn_g, m_norm_final_g.reshape(1, d_model), jnp.zeros((5, d_model), F32)], axis=0)
    gains_v = jnp.concatenate([v_norm_mix_g, v_norm_ffn_g, v_norm_final_g.reshape(1, d_model), jnp.ones((5, d_model), F32)], axis=0)
    gd, gm, gv = _rowwise(_adamw_math, [gains, small, gains_m, gains_v], [], [(d_model, F32)] * 3, [], tm=8, name="adamw_gains")

    moments = {"w_in": (m_w_in, v_w_in), "w_sb_up": (m_w_sb_up, v_w_sb_up), "w_dil_up": (m_w_dil_up, v_w_dil_up),
               "w_out": (m_w_out, v_w_out), "w_ffn_in": (m_w_ffn_in, v_w_ffn_in), "w_ffn_out": (m_w_ffn_out, v_w_ffn_out)}
    upd = {k: _adamw(shards[k], grads[k], moments[k][0][0], moments[k][1][0], "adamw_" + k) for k in names}

    def w_out_of(i):
        return [upd[k][i][None] for k in names]

    def ordered(mix, ws, ffn_g, fin):
        return [mix, ws[0], ws[1], ws[2], ws[3], ffn_g, ws[4], ws[5], fin]

    grad_ws = [grads[k][None] for k in names]
    outs = [loss, dx.reshape(b_sz, s_len, d_model)]
    outs += ordered(small[0:1], grad_ws, small[1:2], small[2])
    outs += ordered(gd[0:1], w_out_of(0), gd[1:2], gd[2])
    outs += ordered(gm[0:1], w_out_of(1), gm[1:2], gm[2])
    outs += ordered(gv[0:1], w_out_of(2), gv[1:2], gv[2])
    return tuple(outs)
```

```python
import functools
import math

import jax
import jax.numpy as jnp
from jax import lax
from jax.experimental import pallas as pl
from jax.experimental.pallas import tpu as pltpu

F32 = jnp.float32
BF16 = jnp.bfloat16
MESH = pl.DeviceIdType.MESH

HEAD_DIM = 64
SB_HEADS = 8
DIL_PAIRS = ((128, 1), (512, 4), (2048, 16))
DIL_HEADS_PER_GROUP = 4
DIL_HEADS = DIL_HEADS_PER_GROUP * len(DIL_PAIRS)
SB_WIDTH = SB_HEADS * HEAD_DIM
DIL_WIDTH = DIL_HEADS * HEAD_DIM
DIL_OUT_WIDTH = DIL_HEADS_PER_GROUP * HEAD_DIM
QKV_WIDTH = 3 * SB_WIDTH + 3 * DIL_WIDTH
RMS_EPS = 1e-6
ALIBI_MAX_BIAS = 8.0
ADAM_LR = 0.001
ADAM_B1 = 0.9
ADAM_B2 = 0.999
ADAM_EPS = 1e-08
ADAM_WD = 0.01
ADAM_STEP = 10

LANES = 128
BLK = 128
NEG = -1e30
EXP_UNDERFLOW = -104.0
SB_FWD_CHAINS = 4
SB_BWD_CHAINS = 4
DIL_CHAINS = 4
N_CHIPS = 4
VMEM_CAP = 56 * 1024 * 1024


def _vmem_limit(tile_bytes):
    return int(min(VMEM_CAP, max(32 * 1024 * 1024, 3 * tile_bytes + 8 * 1024 * 1024)))


def _nbytes(shape, dtype):
    return math.prod(shape) * jnp.dtype(dtype).itemsize


def _dot(a, b):
    return jnp.dot(a, b, preferred_element_type=F32)


def _dot_nt(a, b):
    return lax.dot_general(a, b, (((1,), (1,)), ((), ())), preferred_element_type=F32)


def _dot_tn(a, b):
    return lax.dot_general(a, b, (((0,), (0,)), ((), ())), preferred_element_type=F32)


def _split2(x):
    hi = x.astype(BF16)
    lo = (x - hi.astype(F32)).astype(BF16)
    return hi, lo


def _sigmoid(x):
    return 1.0 / (1.0 + jnp.exp(-x))


class _Carry:
    def __init__(self, arrays=(), out_shapes=(), aliased=False, sems=(), start=None, finish=None):
        self.arrays, self.out_shapes, self.aliased = list(arrays), list(out_shapes), aliased
        self.sems, self.start, self.finish = list(sems), start, finish

    def __bool__(self):
        return bool(self.arrays)

    def call_args(self, n_in, n_out):
        aliases = {n_in + k: n_out + k for k in range(len(self.arrays))} if self.aliased else {}
        return _hbm_specs(len(self.arrays)), _hbm_specs(len(self.out_shapes)), self.out_shapes, aliases, self.sems

    def run(self, refs, n_in, n_out, step, n_steps, compute):
        if not self:
            compute()
            return
        n_c, n_o, n_s = len(self.arrays), len(self.out_shapes), len(self.sems)
        ins = refs[n_in:n_in + n_c]
        outs = refs[n_in + n_c + n_out:n_in + n_c + n_out + n_o]
        sems = refs[len(refs) - n_s:]

        @pl.when(step == 0)
        def _():
            self.start(ins, outs, sems)

        compute()

        @pl.when(step == n_steps - 1)
        def _():
            self.finish(ins, outs, sems)


def _mm(a, b, *, ta=False, tb=False, add=None, out_dtype=F32, tm, tn, tk, name, carry=None, epilogue=None,
        b_cols=None):
    carry = carry or _Carry()
    n_car = len(carry.arrays)
    if ta:
        kdim, m = a.shape
    else:
        m, kdim = a.shape
    if tb:
        n, k2 = b.shape
    else:
        k2, n = b.shape
    col0 = 0
    if b_cols is not None:
        assert b_cols[0] % tn == 0, name
        col0, n = b_cols[0] // tn, b_cols[1]
    assert kdim == k2 and m % tm == 0 and n % tn == 0 and kdim % tk == 0, (name, a.shape, b.shape)
    nk = kdim // tk
    grid = (m // tm, n // tn, nk)
    a_mode = dict(pipeline_mode=pl.Buffered(1)) if grid[0] == 1 and nk == 1 else {}
    b_mode = dict(pipeline_mode=pl.Buffered(1)) if grid[1] == 1 and nk == 1 else {}
    a_spec = (pl.BlockSpec((tk, tm), lambda i, j, k: (k, i), **a_mode) if ta
              else pl.BlockSpec((tm, tk), lambda i, j, k: (i, k), **a_mode))
    b_spec = (pl.BlockSpec((tn, tk), lambda i, j, k: (j + col0, k), **b_mode) if tb
              else pl.BlockSpec((tk, tn), lambda i, j, k: (k, j + col0), **b_mode))
    o_spec = pl.BlockSpec((tm, tn), lambda i, j, k: (i, j))
    dims = ((((0,) if ta else (1,)), ((1,) if tb else (0,))), ((), ()))
    has_add = add is not None
    if epilogue is None:
        ep_fn, ep_rows, ep_params, ep_outs, ep_accs = None, [], [], [], []
        out_sds, out_specs = [jax.ShapeDtypeStruct((m, n), out_dtype)], [o_spec]
    else:
        ep_fn, ep_rows, ep_params, ep_outs, ep_accs = epilogue
        assert grid[1] == 1 or not ep_accs, name
        out_sds = [jax.ShapeDtypeStruct((m, w * grid[1]), d) for w, d in ep_outs]
        out_sds += [jax.ShapeDtypeStruct(sh, F32) for sh in ep_accs]
        out_specs = [pl.BlockSpec((tm, w), lambda i, j, k: (i, j)) for w, _ in ep_outs]
        out_specs += [pl.BlockSpec(sh, lambda i, j, k: (0, 0)) for sh in ep_accs]
    n_main = len(out_sds)
    use_scratch = nk > 1 and (ep_fn is not None or jnp.dtype(out_dtype) != jnp.dtype(F32))
    n_in = 2 + has_add + len(ep_rows) + len(ep_params)

    def finish(total, refs, pid):
        outs = refs[n_in + n_car:n_in + n_car + n_main]
        if ep_fn is None:
            outs[0][...] = total.astype(out_dtype)
            return
        first = 2 + has_add
        rows = [r[...].astype(F32) for r in refs[first:first + len(ep_rows)]]
        params = [p[...] for p in refs[first + len(ep_rows):n_in]]
        res = ep_fn(total, *rows, *params)
        for o_ref, v in zip(outs[:len(ep_outs)], res):
            o_ref[...] = v.astype(o_ref.dtype)
        acc_refs = outs[len(ep_outs):]
        if acc_refs:
            @pl.when(pid[0] == 0)
            def _():
                for r in acc_refs:
                    r[...] = jnp.zeros(r.shape, F32)

            for r, v in zip(acc_refs, res[len(ep_outs):]):
                r[...] += v

    def compute(refs, pid):
        a_ref, b_ref = refs[0], refs[1]
        add_ref = refs[2] if has_add else None
        prod = lax.dot_general(a_ref[...].astype(BF16), b_ref[...].astype(BF16), dims, preferred_element_type=F32)
        if nk == 1:
            finish(prod + add_ref[...] if has_add else prod, refs, pid)
            return
        acc_ref = refs[n_in + n_car + n_main + len(carry.out_shapes)] if use_scratch else refs[n_in + n_car]
        k = pid[2]

        @pl.when(k == 0)
        def _():
            acc_ref[...] = prod + add_ref[...] if has_add else prod

        @pl.when(k > 0)
        def _():
            acc_ref[...] += prod

        if use_scratch:
            @pl.when(k == nk - 1)
            def _():
                finish(acc_ref[...], refs, pid)

    def body(*refs):
        pid = (pl.program_id(0), pl.program_id(1), pl.program_id(2))
        step = (pid[0] * grid[1] + pid[1]) * nk + pid[2]
        carry.run(refs, n_in, n_main, step, grid[0] * grid[1] * nk, lambda: compute(refs, pid))

    tile_bytes = (_nbytes((tm, tk), a.dtype) + _nbytes((tk, tn), b.dtype) + 2 * _nbytes((tm, tn), F32)
                  + (_nbytes((tm, tn), F32) if has_add else 0)
                  + sum(_nbytes((tm, r.shape[1]), r.dtype) for r in ep_rows) + sum(_nbytes((tm, w), d) for w, d in ep_outs))
    in_specs = [a_spec, b_spec] + ([o_spec] if has_add else [])
    in_specs += [pl.BlockSpec((tm, r.shape[1] // grid[1]), lambda i, j, k: (i, j)) for r in ep_rows]
    in_specs += [pl.BlockSpec(p.shape, lambda i, j, k: (0, 0)) for p in ep_params]
    args = (a, b) + ((add,) if has_add else ()) + tuple(ep_rows) + tuple(ep_params)
    scratch = [pltpu.VMEM((tm, tn), F32)] if use_scratch else []
    serial = bool(carry) or bool(ep_accs)
    c_in, c_out, c_shapes, c_alias, c_sems = carry.call_args(n_in, n_main)
    res = pl.pallas_call(
        body, name=name, grid=grid,
        in_specs=in_specs + c_in, out_specs=out_specs + c_out, out_shape=out_sds + c_shapes,
        input_output_aliases=c_alias, scratch_shapes=scratch + c_sems,
        compiler_params=pltpu.CompilerParams(
            dimension_semantics=("arbitrary",) * 3 if serial else ("parallel", "parallel", "arbitrary"),
            vmem_limit_bytes=_vmem_limit(tile_bytes)),
    )(*args, *carry.arrays)
    main = res[0] if ep_fn is None else list(res[:n_main])
    return (main, res[n_main:]) if carry else main


def _rowwise(fn, rows, params, outs, accs, *, tm, name):
    t = rows[0].shape[0]
    assert t % tm == 0, (name, t, tm)
    n_r, n_p, n_o = len(rows), len(params), len(outs)

    def body(*refs):
        vals = [r[...].astype(F32) for r in refs[:n_r]] + [p[...] for p in refs[n_r:n_r + n_p]]
        res = fn(*vals)
        o_refs = refs[n_r + n_p:n_r + n_p + n_o]
        a_refs = refs[n_r + n_p + n_o:]
        for o_ref, v in zip(o_refs, res[:n_o]):
            o_ref[...] = v.astype(o_ref.dtype)
        if accs:
            @pl.when(pl.program_id(0) == 0)
            def _():
                for a_ref in a_refs:
                    a_ref[...] = jnp.zeros(a_ref.shape, F32)

            for a_ref, v in zip(a_refs, res[n_o:]):
                a_ref[...] += v

    in_specs = [pl.BlockSpec((tm, r.shape[1]), lambda i: (i, 0)) for r in rows]
    in_specs += [pl.BlockSpec(p.shape, lambda i: (0, 0)) for p in params]
    out_specs = [pl.BlockSpec((tm, w), lambda i: (i, 0)) for w, _ in outs]
    out_specs += [pl.BlockSpec(s, lambda i: (0, 0)) for s in accs]
    out_shape = [jax.ShapeDtypeStruct((t, w), d) for w, d in outs]
    out_shape += [jax.ShapeDtypeStruct(s, F32) for s in accs]
    tile_bytes = sum(_nbytes((tm, r.shape[1]), r.dtype) for r in rows) + sum(_nbytes((tm, w), F32) for w, _ in outs)
    res = pl.pallas_call(
        body, name=name, grid=(t // tm,), in_specs=in_specs, out_specs=out_specs, out_shape=out_shape,
        compiler_params=pltpu.CompilerParams(
            dimension_semantics=("arbitrary",) if accs else ("parallel",),
            vmem_limit_bytes=_vmem_limit(2 * tile_bytes)),
    )(*rows, *params)
    return res


def _rms_stats(x):
    r = lax.rsqrt(jnp.mean(x * x, axis=-1, keepdims=True) + RMS_EPS)
    return x * r, r


def _rms_bwd(dy, xhat, r, g):
    dxhat = dy * g
    dx = r * (dxhat - xhat * jnp.mean(dxhat * xhat, axis=-1, keepdims=True))
    return dx, dy * xhat


def _sb_consts():
    lane = lax.broadcasted_iota(jnp.int32, (BLK, LANES), 1)
    head0 = lane < HEAD_DIM
    row = lax.broadcasted_iota(jnp.int32, (2 * BLK, BLK), 0) % BLK
    col = lax.broadcasted_iota(jnp.int32, (2 * BLK, BLK), 1)
    causal = col < row
    jj = lax.broadcasted_iota(jnp.int32, (BLK, BLK), 0)
    ss = lax.broadcasted_iota(jnp.int32, (BLK, BLK), 1)
    suffix = jnp.where(jj > ss, 1.0, 0.0).astype(BF16)
    return head0, causal, suffix


def _stack_heads(x, head0):
    zero = jnp.zeros_like(x)
    return jnp.concatenate([jnp.where(head0, x, zero), jnp.where(head0, zero, x)], axis=0)


def _sb_logits(z, causal, masked):
    sp = jnp.log(1.0 + jnp.exp(-jnp.abs(z)))
    log_keep = -(jnp.maximum(z, 0.0) + sp)
    log_beta = jnp.minimum(z, 0.0) - sp
    if masked:
        log_keep = jnp.where(causal, log_keep, 0.0)
    return log_keep, log_beta


def _suffix_sums(x, suffix):
    hi, lo = _split2(x)
    after = _dot(hi, suffix) + _dot(lo, suffix)
    total = jnp.broadcast_to(after[:, 0:1] + x[:, 0:1], x.shape)
    return after, total


def _sb_walk_back(i, state, per_chain, tile):
    def alive(st):
        worst = functools.reduce(jnp.maximum, [st[p][:, 0:1] for p in range(0, len(st), per_chain)])
        return jnp.max(worst) > EXP_UNDERFLOW

    def cond(c):
        return jnp.logical_and(c[0] < i, alive(c[1]))

    def body(c):
        return c[0] + 1, tile(i - 1 - c[0], c[1], False)

    return lax.while_loop(cond, body, (jnp.int32(0), state))[1]


def _lane_blocks(x, n):
    return [x[:, p * LANES:(p + 1) * LANES] for p in range(n)]


def _sb_fwd(qkv, b_sz, s_len, carry):
    nq = s_len // BLK
    n_pairs = SB_WIDTH // LANES
    ch = SB_FWD_CHAINS
    n_steps = n_pairs // ch
    scale = 1.0 / math.sqrt(HEAD_DIM)

    def compute(q_ref, k_ref, v_ref, o_ref):
        head0, causal, suffix = _sb_consts()

        def q_block(i, _):
            qs = pl.multiple_of(i * BLK, BLK)
            q_all = (q_ref[pl.ds(qs, BLK), :] * scale).astype(BF16)
            q01 = [_stack_heads(q, head0) for q in _lane_blocks(q_all, ch)]

            def tile(j, state, masked):
                ks = pl.multiple_of(j * BLK, BLK)
                ks_ = _lane_blocks(k_ref[pl.ds(ks, BLK), :].astype(BF16), ch)
                vs_ = _lane_blocks(v_ref[pl.ds(ks, BLK), :].astype(BF16), ch)
                zs = [_dot_nt(q01[p], ks_[p]) for p in range(ch)]
                logits = [_sb_logits(z, causal, masked) for z in zs]
                sums = [_suffix_sums(lg[0], suffix) for lg in logits]
                out = []
                for p in range(ch):
                    carry, acc = state[2 * p], state[2 * p + 1]
                    after, total = sums[p]
                    a = jnp.exp(logits[p][1] + carry + after)
                    if masked:
                        a = jnp.where(causal, a, 0.0)
                    a_hi, a_lo = _split2(a)
                    a_cat = jnp.concatenate([a_hi[:BLK], a_hi[BLK:], a_lo[:BLK], a_lo[BLK:]], axis=1)
                    v01 = _stack_heads(vs_[p], head0)
                    out += [carry + total, acc + _dot(a_cat, jnp.concatenate([v01, v01], axis=0))]
                return tuple(out)

            state = (jnp.zeros((2 * BLK, BLK), F32), jnp.zeros((BLK, LANES), F32)) * ch
            state = tile(i, state, True)
            state = _sb_walk_back(i, state, 2, tile)
            o_ref[pl.ds(qs, BLK), :] = jnp.concatenate([state[2 * p + 1] for p in range(ch)], axis=1)
            return 0

        lax.fori_loop(0, nq, q_block, 0)

    def body(*refs):
        step = pl.program_id(0) * n_steps + pl.program_id(1)
        o_ref = refs[3 + len(carry.arrays)]
        carry.run(refs, 3, 1, step, b_sz * n_steps, lambda: compute(refs[0], refs[1], refs[2], o_ref))

    blk = lambda off: pl.BlockSpec((None, s_len, ch * LANES), lambda b, p: (b, 0, off + p))
    c_in, c_out, c_shapes, c_alias, c_sems = carry.call_args(3, 1)
    res = pl.pallas_call(
        body, name="sb_fwd", grid=(b_sz, n_steps),
        in_specs=[blk(0), blk(n_steps), blk(2 * n_steps)] + c_in, out_specs=[blk(0)] + c_out,
        out_shape=[jax.ShapeDtypeStruct((b_sz, s_len, SB_WIDTH), F32)] + c_shapes,
        input_output_aliases=c_alias, scratch_shapes=c_sems,
        compiler_params=pltpu.CompilerParams(dimension_semantics=("arbitrary", "arbitrary"),
                                             vmem_limit_bytes=VMEM_CAP),
    )(qkv, qkv, qkv, *carry.arrays)
    return res[0], res[1:]


def _sb_bwd(qkv, o_sb, do_sb, b_sz, s_len, carry):
    nq = s_len // BLK
    n_pairs = SB_WIDTH // LANES
    ch = SB_BWD_CHAINS
    n_steps = n_pairs // ch
    scale = 1.0 / math.sqrt(HEAD_DIM)

    def compute(q_ref, k_ref, v_ref, o_ref, do_ref, dq_ref, dk_ref, dv_ref, dk_acc, dv_acc):
        head0, causal, suffix = _sb_consts()
        lrow = lax.broadcasted_iota(jnp.int32, (LANES, LANES), 0)
        ones_h0 = jnp.where(lrow < HEAD_DIM, 1.0, 0.0).astype(BF16)
        ones_h1 = jnp.where(lrow >= HEAD_DIM, 1.0, 0.0).astype(BF16)
        dk_acc[...] = jnp.zeros(dk_acc.shape, F32)
        dv_acc[...] = jnp.zeros(dv_acc.shape, F32)

        def q_block(i, _):
            qs = pl.multiple_of(i * BLK, BLK)
            q_all = (q_ref[pl.ds(qs, BLK), :] * scale).astype(BF16)
            do_all = do_ref[pl.ds(qs, BLK), :].astype(BF16)
            dd_all = do_all.astype(F32) * o_ref[pl.ds(qs, BLK), :]
            q01 = [_stack_heads(q, head0) for q in _lane_blocks(q_all, ch)]
            do01 = [_stack_heads(d, head0) for d in _lane_blocks(do_all, ch)]
            tot = []
            for dd in _lane_blocks(dd_all, ch):
                dd_hi, dd_lo = _split2(dd)
                tot.append(jnp.concatenate([_dot(dd_hi, ones_h0) + _dot(dd_lo, ones_h0),
                                            _dot(dd_hi, ones_h1) + _dot(dd_lo, ones_h1)], axis=0))

            def tile(j, state, masked):
                ks = pl.multiple_of(j * BLK, BLK)
                ks_ = _lane_blocks(k_ref[pl.ds(ks, BLK), :].astype(BF16), ch)
                vs_ = _lane_blocks(v_ref[pl.ds(ks, BLK), :].astype(BF16), ch)
                zs = [_dot_nt(q01[p], ks_[p]) for p in range(ch)]
                das = [_dot_nt(do01[p], vs_[p]) for p in range(ch)]
                logits = [_sb_logits(z, causal, masked) for z in zs]
                sums = [_suffix_sums(lg[0], suffix) for lg in logits]
                a_s, e_s = [], []
                for p in range(ch):
                    a = jnp.exp(logits[p][1] + state[3 * p] + sums[p][0])
                    if masked:
                        a = jnp.where(causal, a, 0.0)
                    a_s.append(a)
                    e_s.append(a * das[p])
                e_sums = [_suffix_sums(e, suffix) for e in e_s]
                out, dks, dvs = [], [], []
                for p in range(ch):
                    carry, rcarry, dq = state[3 * p:3 * p + 3]
                    e = e_s[p]
                    before = tot[p] - (rcarry + e_sums[p][0] + e)
                    beta = jnp.exp(logits[p][1])
                    dz = e * (1.0 - beta) - beta * before
                    if masked:
                        dz = jnp.where(causal, dz, 0.0)
                    dz_b = dz.astype(BF16)
                    dks.append(_dot_tn(dz_b, q01[p]))
                    dvs.append(_dot_tn(a_s[p].astype(BF16), do01[p]))
                    out += [carry + sums[p][1], rcarry + e_sums[p][1], dq + _dot(dz_b, ks_[p])]
                dk_acc[pl.ds(ks, BLK), :] += jnp.concatenate(dks, axis=1)
                dv_acc[pl.ds(ks, BLK), :] += jnp.concatenate(dvs, axis=1)
                return tuple(out)

            state = (jnp.zeros((2 * BLK, BLK), F32),) * (3 * ch)
            state = tile(i, state, True)
            state = _sb_walk_back(i, state, 3, tile)
            dq = [jnp.where(head0, state[3 * p + 2][:BLK], state[3 * p + 2][BLK:]) for p in range(ch)]
            dq_ref[pl.ds(qs, BLK), :] = (jnp.concatenate(dq, axis=1) * scale).astype(dq_ref.dtype)
            return 0

        lax.fori_loop(0, nq, q_block, 0)
        dk_ref[...] = dk_acc[...].astype(dk_ref.dtype)
        dv_ref[...] = dv_acc[...].astype(dv_ref.dtype)

    def body(*refs):
        step = pl.program_id(0) * n_steps + pl.program_id(1)
        n_c, n_o = len(carry.arrays), len(carry.out_shapes)
        own = refs[:5] + refs[5 + n_c:8 + n_c] + refs[8 + n_c + n_o:10 + n_c + n_o]
        carry.run(refs, 5, 3, step, b_sz * n_steps, lambda: compute(*own))

    blk = lambda off: pl.BlockSpec((None, s_len, ch * LANES), lambda b, p: (b, 0, off + p))
    once = lambda off: pl.BlockSpec((None, s_len, ch * LANES), lambda b, p: (b, 0, off + p),
                                    pipeline_mode=pl.Buffered(1))
    out_sd = jax.ShapeDtypeStruct((b_sz, s_len, SB_WIDTH), BF16)
    c_in, c_out, c_shapes, c_alias, c_sems = carry.call_args(5, 3)
    res = pl.pallas_call(
        body, name="sb_bwd", grid=(b_sz, n_steps),
        in_specs=[once(0), once(n_steps), once(2 * n_steps), once(0), once(0)] + c_in,
        out_specs=[blk(0), blk(0), blk(0)] + c_out, out_shape=[out_sd, out_sd, out_sd] + c_shapes,
        input_output_aliases=c_alias,
        scratch_shapes=[pltpu.VMEM((s_len, ch * LANES), F32), pltpu.VMEM((s_len, ch * LANES), F32)] + c_sems,
        compiler_params=pltpu.CompilerParams(dimension_semantics=("arbitrary", "arbitrary"),
                                             vmem_limit_bytes=VMEM_CAP),
    )(qkv, qkv, qkv, o_sb, do_sb, *carry.arrays)
    return res[:3], res[3:]


def _dil_consts(group, pair_idx, dilation):
    lane = lax.broadcasted_iota(jnp.int32, (BLK, LANES), 1)
    head0 = lane < HEAD_DIM
    row = lax.broadcasted_iota(jnp.int32, (2 * BLK, BLK), 0)
    qa = row % BLK
    kb = lax.broadcasted_iota(jnp.int32, (2 * BLK, BLK), 1)
    head = (group * DIL_HEADS_PER_GROUP + 2 * pair_idx + row // BLK).astype(F32)
    slope = jnp.exp((-ALIBI_MAX_BIAS * math.log(2.0) / DIL_HEADS) * (head + 1.0))
    valid_cur = kb <= qa
    valid_prev = kb >= qa
    bias_cur = -slope * ((qa - kb) * dilation).astype(F32)
    bias_prev = -slope * ((BLK + qa - kb) * dilation).astype(F32)
    return head0, valid_cur, valid_prev, bias_cur, bias_prev


def _dil_units(s_len, dilation):
    nb = s_len // dilation // BLK
    return [(r, n) for r in range(dilation) for n in range(nb)]


def _dil_rows(n, r, dilation):
    if dilation == 1:
        return pl.ds(n * BLK, BLK)
    return pl.ds(n * BLK * dilation + r, BLK, stride=dilation)


def _dil_scores(q01, k, bias, valid):
    s = _dot_nt(q01, k) * (1.0 / math.sqrt(HEAD_DIM)) + bias
    return jnp.where(valid, s, NEG)


def _dil_fwd(qkv, b_sz, s_len, carry):
    n_pairs = DIL_OUT_WIDTH // LANES
    q_off = 3 * SB_WIDTH // LANES
    per_kind = DIL_WIDTH // LANES

    def compute(pair_idx, qkv_refs, o_ref, lse_ref, m_s, l_s):
        m_s[...] = jnp.full(m_s.shape, NEG, F32)
        l_s[...] = jnp.zeros(l_s.shape, F32)
        o_ref[...] = jnp.zeros(o_ref.shape, F32)
        for g, (_, dilation) in enumerate(DIL_PAIRS):
            q_ref, k_ref, v_ref = qkv_refs[3 * g:3 * g + 3]
            head0, valid_cur, valid_prev, bias_cur, bias_prev = _dil_consts(g, pair_idx, dilation)
            units = _dil_units(s_len, dilation)
            for u0 in range(0, len(units), DIL_CHAINS):
                group = units[u0:u0 + DIL_CHAINS]
                rows_of = [_dil_rows(n, r, dilation) for r, n in group]
                scores, values = [], []
                for (r, n), rows in zip(group, rows_of):
                    q01 = _stack_heads(q_ref[rows, :].astype(BF16), head0)
                    sc = [_dil_scores(q01, k_ref[rows, :].astype(BF16), bias_cur, valid_cur)]
                    vals = [_stack_heads(v_ref[rows, :].astype(BF16), head0)]
                    if n > 0:
                        prev = _dil_rows(n - 1, r, dilation)
                        sc.append(_dil_scores(q01, k_ref[prev, :].astype(BF16), bias_prev, valid_prev))
                        vals.append(_stack_heads(v_ref[prev, :].astype(BF16), head0))
                    scores.append(sc)
                    values.append(vals)
                stats = []
                for sc, rows in zip(scores, rows_of):
                    m_blk = functools.reduce(jnp.maximum, [jnp.max(x, axis=-1, keepdims=True) for x in sc])
                    m_old = jnp.concatenate([m_s.at[0][rows, :], m_s.at[1][rows, :]], axis=0)
                    l_old = jnp.concatenate([l_s.at[0][rows, :], l_s.at[1][rows, :]], axis=0)
                    m_new = jnp.maximum(m_old, m_blk)
                    probs = [jnp.exp(x - m_new) for x in sc]
                    l_blk = functools.reduce(jnp.add, [jnp.sum(p, axis=-1, keepdims=True) for p in probs])
                    alpha = jnp.exp(m_old - m_new)
                    stats.append((m_new, alpha * l_old + l_blk, alpha, probs))
                for (m_new, l_new, alpha, probs), vals, rows in zip(stats, values, rows_of):
                    alpha_tok = jnp.where(head0, alpha[:BLK], alpha[BLK:])
                    p_cat = jnp.concatenate(
                        [h for p in probs for h in (p[:BLK].astype(BF16), p[BLK:].astype(BF16))], axis=1)
                    o_ref[rows, :] = alpha_tok * o_ref[rows, :] + _dot(p_cat, jnp.concatenate(vals, axis=0))
                    m_s.at[0][rows, :] = m_new[:BLK]
                    m_s.at[1][rows, :] = m_new[BLK:]
                    l_s.at[0][rows, :] = l_new[:BLK]
                    l_s.at[1][rows, :] = l_new[BLK:]
        lane = lax.broadcasted_iota(jnp.int32, (BLK, LANES), 1)
        for c in range(s_len // BLK):
            rows = pl.ds(c * BLK, BLK)
            l0, l1 = l_s.at[0][rows, :], l_s.at[1][rows, :]
            o_ref[rows, :] = o_ref[rows, :] / jnp.where(lane < HEAD_DIM, l0, l1)
            lse_ref.at[0][rows, :] = m_s.at[0][rows, :] + jnp.log(l0)
            lse_ref.at[1][rows, :] = m_s.at[1][rows, :] + jnp.log(l1)

    def body(*refs):
        pair_idx = pl.program_id(1)
        step = pl.program_id(0) * n_pairs + pair_idx
        n_c, n_o = len(carry.arrays), len(carry.out_shapes)
        o_ref, lse_ref = refs[9 + n_c:11 + n_c]
        m_s, l_s = refs[11 + n_c + n_o:13 + n_c + n_o]
        carry.run(refs, 9, 2, step, b_sz * n_pairs, lambda: compute(pair_idx, refs[:9], o_ref, lse_ref, m_s, l_s))

    in_specs = []
    for g in range(len(DIL_PAIRS)):
        for kind in range(3):
            off = q_off + kind * per_kind + g * n_pairs
            in_specs.append(pl.BlockSpec((None, s_len, LANES), lambda b, p, off=off: (b, 0, off + p)))
    c_in, c_out, c_shapes, c_alias, c_sems = carry.call_args(9, 2)
    res = pl.pallas_call(
        body, name="dil_fwd", grid=(b_sz, n_pairs),
        in_specs=in_specs + c_in,
        out_specs=[pl.BlockSpec((None, s_len, LANES), lambda b, p: (b, 0, p)),
                   pl.BlockSpec((None, None, 2, s_len, LANES), lambda b, p: (b, p, 0, 0, 0))] + c_out,
        out_shape=[jax.ShapeDtypeStruct((b_sz, s_len, DIL_OUT_WIDTH), F32),
                   jax.ShapeDtypeStruct((b_sz, n_pairs, 2, s_len, LANES), F32)] + c_shapes,
        input_output_aliases=c_alias,
        scratch_shapes=[pltpu.VMEM((2, s_len, LANES), F32), pltpu.VMEM((2, s_len, LANES), F32)] + c_sems,
        compiler_params=pltpu.CompilerParams(dimension_semantics=("arbitrary", "arbitrary"),
                                             vmem_limit_bytes=VMEM_CAP),
    )(*([qkv] * 9), *carry.arrays)
    return res[0], res[1], res[2:]


def _dil_bwd(qkv, o_dl, lse, do_dl, b_sz, s_len, carry):
    n_pairs = DIL_OUT_WIDTH // LANES
    n_groups = len(DIL_PAIRS)
    q_off = 3 * SB_WIDTH // LANES
    per_kind = DIL_WIDTH // LANES

    def compute(pair_idx, group, q_ref, k_ref, v_ref, o_ref, lse_ref, do_ref, dq_ref, dk_ref, dv_ref, d_s, dq_s, dk_s, dv_s):
        lrow = lax.broadcasted_iota(jnp.int32, (LANES, LANES), 0)
        ones_h0 = jnp.where(lrow < HEAD_DIM, 1.0, 0.0).astype(BF16)
        ones_h1 = jnp.where(lrow >= HEAD_DIM, 1.0, 0.0).astype(BF16)
        for c in range(s_len // BLK):
            rows = pl.ds(c * BLK, BLK)
            dd_hi, dd_lo = _split2(do_ref[rows, :] * o_ref[rows, :])
            d_s.at[0][rows, :] = _dot(dd_hi, ones_h0) + _dot(dd_lo, ones_h0)
            d_s.at[1][rows, :] = _dot(dd_hi, ones_h1) + _dot(dd_lo, ones_h1)
        dk_s[...] = jnp.zeros(dk_s.shape, F32)
        dv_s[...] = jnp.zeros(dv_s.shape, F32)

        def one_group(g, dilation):
            head0, valid_cur, valid_prev, bias_cur, bias_prev = _dil_consts(g, pair_idx, dilation)
            units = _dil_units(s_len, dilation)
            scale = 1.0 / math.sqrt(HEAD_DIM)
            for u0 in range(0, len(units), DIL_CHAINS):
                chunk = units[u0:u0 + DIL_CHAINS]
                loaded = []
                for r, n in chunk:
                    rows = _dil_rows(n, r, dilation)
                    q01 = _stack_heads(q_ref[rows, :].astype(BF16), head0)
                    do01 = _stack_heads(do_ref[rows, :].astype(BF16), head0)
                    lse01 = jnp.concatenate([lse_ref.at[0][rows, :], lse_ref.at[1][rows, :]], axis=0)
                    d01 = jnp.concatenate([d_s.at[0][rows, :], d_s.at[1][rows, :]], axis=0)
                    blocks = [(rows, bias_cur, valid_cur)]
                    if n > 0:
                        blocks.append((_dil_rows(n - 1, r, dilation), bias_prev, valid_prev))
                    parts = []
                    for krows, bias, valid in blocks:
                        k = k_ref[krows, :].astype(BF16)
                        v = v_ref[krows, :].astype(BF16)
                        parts.append((krows, k, _dil_scores(q01, k, bias, valid), _dot_nt(do01, v)))
                    loaded.append((rows, q01, do01, lse01, d01, parts))
                grads = []
                for rows, q01, do01, lse01, d01, parts in loaded:
                    for krows, k, sc, dp in parts:
                        p = jnp.exp(sc - lse01)
                        grads.append((p.astype(BF16), (p * (dp - d01) * scale).astype(BF16)))
                it = iter(grads)
                updates = []
                for rows, q01, do01, lse01, d01, parts in loaded:
                    dq = jnp.zeros((2 * BLK, LANES), F32)
                    for krows, k, sc, dp in parts:
                        p_b, ds = next(it)
                        dq = dq + _dot(ds, k)
                        updates.append((krows, _dot_tn(ds, q01), _dot_tn(p_b, do01)))
                    dq_s[rows, :] = jnp.where(head0, dq[:BLK], dq[BLK:])
                for krows, dk, dv in updates:
                    dk_s[krows, :] = dk_s[krows, :] + dk
                    dv_s[krows, :] = dv_s[krows, :] + dv

        for g, (_, dilation) in enumerate(DIL_PAIRS):
            pl.when(group == g)(functools.partial(one_group, g, dilation))
        dq_ref[...] = dq_s[...].astype(dq_ref.dtype)
        dk_ref[...] = dk_s[...].astype(dk_ref.dtype)
        dv_ref[...] = dv_s[...].astype(dv_ref.dtype)

    def body(*refs):
        pair_idx, group = pl.program_id(1), pl.program_id(2)
        step = (pl.program_id(0) * n_pairs + pair_idx) * n_groups + group
        n_c, n_o = len(carry.arrays), len(carry.out_shapes)
        own = refs[:6] + refs[6 + n_c:9 + n_c] + refs[9 + n_c + n_o:13 + n_c + n_o]
        carry.run(refs, 6, 3, step, b_sz * n_pairs * n_groups, lambda: compute(pair_idx, group, *own))

    def qkv_spec(kind):
        return pl.BlockSpec((None, s_len, LANES),
                            lambda b, p, g: (b, 0, q_off + kind * per_kind + g * n_pairs + p))

    tok_spec = pl.BlockSpec((None, s_len, LANES), lambda b, p, g: (b, 0, p))
    out_spec = pl.BlockSpec((None, s_len, LANES), lambda b, p, g: (b, 0, g * n_pairs + p))
    out_sd = jax.ShapeDtypeStruct((b_sz, s_len, DIL_WIDTH), BF16)
    c_in, c_out, c_shapes, c_alias, c_sems = carry.call_args(6, 3)
    res = pl.pallas_call(
        body, name="dil_bwd", grid=(b_sz, n_pairs, n_groups),
        in_specs=[qkv_spec(0), qkv_spec(1), qkv_spec(2), tok_spec,
                  pl.BlockSpec((None, None, 2, s_len, LANES), lambda b, p, g: (b, p, 0, 0, 0)), tok_spec] + c_in,
        out_specs=[out_spec, out_spec, out_spec] + c_out,
        out_shape=[out_sd, out_sd, out_sd] + c_shapes,
        input_output_aliases=c_alias,
        scratch_shapes=[pltpu.VMEM((2, s_len, LANES), F32)] + [pltpu.VMEM((s_len, LANES), F32)] * 3 + c_sems,
        compiler_params=pltpu.CompilerParams(dimension_semantics=("arbitrary", "arbitrary", "arbitrary"),
                                             vmem_limit_bytes=VMEM_CAP),
    )(qkv, qkv, qkv, o_dl, lse, do_dl, *carry.arrays)
    return res[:3], res[3:]


def _mesh_pos():
    return lax.axis_index("x"), lax.axis_index("y"), lax.axis_index("c")


def _other_chips(x, y):
    return [(1 - x, y), (x, 1 - y), (1 - x, 1 - y)]


def _hbm_specs(n):
    return [pl.BlockSpec(memory_space=pl.ANY)] * n


def _cast_to_slab(w, name):
    rows, cols = w.shape
    mine = jnp.reshape(2 * lax.axis_index("x") + lax.axis_index("y"), (1,)).astype(jnp.int32)

    def body(idx_ref, w_ref, o_ref):
        o_ref[...] = w_ref[...].astype(BF16)

    return pl.pallas_call(
        body, name=name,
        grid_spec=pltpu.PrefetchScalarGridSpec(
            num_scalar_prefetch=1, grid=(1,),
            in_specs=[pl.BlockSpec((rows, cols), lambda i, idx: (0, 0))],
            out_specs=pl.BlockSpec((None, rows, cols), lambda i, idx: (idx[0], 0, 0))),
        out_shape=jax.ShapeDtypeStruct((N_CHIPS, rows, cols), BF16),
        compiler_params=pltpu.CompilerParams(vmem_limit_bytes=_vmem_limit(rows * cols * 6)),
    )(mine, w)


def _gather_issue(slabs, send_sems, recv_sems):
    x, y, c = _mesh_pos()
    for k, slab in enumerate(slabs):
        half = slab.shape[1] // 2
        rows = slab.at[2 * x + y, pl.ds(c * half, half), :]
        for r, (px, py) in enumerate(_other_chips(x, y)):
            pltpu.make_async_remote_copy(
                src_ref=rows, dst_ref=rows, send_sem=send_sems.at[6 * k + r], recv_sem=recv_sems.at[6 * k + r],
                device_id=(px, py, c), device_id_type=MESH).start()


def _gather_complete(slabs, send_sems, recv_sems):
    x, y, c = _mesh_pos()
    chips = _other_chips(x, y)

    def copy(k, sem, block, rows, to):
        ref = slabs[k].at[block, rows, :]
        return pltpu.make_async_remote_copy(
            src_ref=ref, dst_ref=ref, send_sem=send_sems.at[sem], recv_sem=recv_sems.at[sem],
            device_id=to, device_id_type=MESH)

    for k, slab in enumerate(slabs):
        half = slab.shape[1] // 2
        for r, (px, py) in enumerate(chips):
            copy(k, 6 * k + r, 2 * px + py, pl.ds(c * half, half), (px, py, c)).wait_recv()
            copy(k, 6 * k + 3 + r, 2 * px + py, pl.ds(c * half, half), (x, y, 1 - c)).start()
    for k, slab in enumerate(slabs):
        half = slab.shape[1] // 2
        for r, (px, py) in enumerate(chips):
            copy(k, 6 * k + 3 + r, 2 * px + py, pl.ds((1 - c) * half, half), (x, y, 1 - c)).wait_recv()
    for k, slab in enumerate(slabs):
        half = slab.shape[1] // 2
        for r, (px, py) in enumerate(chips):
            copy(k, 6 * k + r, 2 * x + y, pl.ds(c * half, half), (px, py, c)).wait_send()
            copy(k, 6 * k + 3 + r, 2 * px + py, pl.ds(c * half, half), (x, y, 1 - c)).wait_send()


def _gather_sems(n):
    return [pltpu.SemaphoreType.DMA((6 * n,)), pltpu.SemaphoreType.DMA((6 * n,))]


def _gather_carry(slabs):
    return _Carry(slabs, [jax.ShapeDtypeStruct(a.shape, a.dtype) for a in slabs], True, _gather_sems(len(slabs)),
                  lambda ins, outs, sems: _gather_issue(outs, *sems),
                  lambda ins, outs, sems: _gather_complete(outs, *sems))


def _gather_weights(slabs):
    n = len(slabs)

    def body(*refs):
        outs = refs[n:2 * n]
        send_sems, recv_sems = refs[2 * n:]
        _gather_issue(outs, send_sems, recv_sems)
        _gather_complete(outs, send_sems, recv_sems)

    return pl.pallas_call(
        body, name="gather_weights",
        in_specs=_hbm_specs(n), out_specs=_hbm_specs(n),
        out_shape=[jax.ShapeDtypeStruct(s.shape, s.dtype) for s in slabs],
        input_output_aliases={k: k for k in range(n)},
        scratch_shapes=_gather_sems(n),
    )(*slabs)


def _pair_exchange(grads, tag):
    n = len(grads)

    def body(*refs):
        ins, outs = refs[:n], refs[n:2 * n]
        send_sems, recv_sems = refs[2 * n:]
        x, y, c = _mesh_pos()
        copies = []
        for k in range(n):
            half = grads[k].shape[1] // 2
            cp = pltpu.make_async_remote_copy(
                src_ref=ins[k].at[:, pl.ds((1 - c) * half, half), :], dst_ref=outs[k],
                send_sem=send_sems.at[k], recv_sem=recv_sems.at[k],
                device_id=(x, y, 1 - c), device_id_type=MESH)
            cp.start()
            copies.append(cp)
        for cp in copies:
            cp.wait()

    return pl.pallas_call(
        body, name="grad_pair_exchange_" + tag,
        in_specs=_hbm_specs(n), out_specs=_hbm_specs(n),
        out_shape=[jax.ShapeDtypeStruct((N_CHIPS, g.shape[1] // 2, g.shape[2]), F32) for g in grads],
        scratch_shapes=[pltpu.SemaphoreType.DMA((n,)), pltpu.SemaphoreType.DMA((n,))],
    )(*grads)


def _pair_sum(grad, other, name):
    _, rows, cols = grad.shape
    half = rows // 2
    core = jnp.reshape(lax.axis_index("c"), (1,)).astype(jnp.int32)

    def body(core_ref, g_ref, p_ref, s_ref, sb_ref):
        s = g_ref[...] + p_ref[...]
        s_ref[...] = s
        sb_ref[...] = s.astype(BF16)

    blk = pl.BlockSpec((None, half, cols), lambda p, core_ref: (p, 0, 0))
    return pl.pallas_call(
        body, name=name,
        grid_spec=pltpu.PrefetchScalarGridSpec(
            num_scalar_prefetch=1, grid=(N_CHIPS,),
            in_specs=[pl.BlockSpec((None, half, cols), lambda p, core_ref: (p, core_ref[0], 0)), blk],
            out_specs=[blk, blk]),
        out_shape=[jax.ShapeDtypeStruct((N_CHIPS, half, cols), F32),
                   jax.ShapeDtypeStruct((N_CHIPS, half, cols), BF16)],
        compiler_params=pltpu.CompilerParams(dimension_semantics=("parallel",),
                                             vmem_limit_bytes=_vmem_limit(4 * half * cols * 4)),
    )(core, grad, other)


def _chip_copies(sums_bf16, lands, send_sems, recv_sems):
    x, y, c = _mesh_pos()
    return [pltpu.make_async_remote_copy(
        src_ref=sums_bf16[k].at[2 * px + py], dst_ref=lands[k].at[r],
        send_sem=send_sems.at[3 * k + r], recv_sem=recv_sems.at[3 * k + r],
        device_id=(px, py, c), device_id_type=MESH)
        for k in range(len(sums_bf16)) for r, (px, py) in enumerate(_other_chips(x, y))]


def _chip_carry(sums_bf16):
    def start(ins, outs, sems):
        for cp in _chip_copies(ins, outs, *sems):
            cp.start()

    def finish(ins, outs, sems):
        for cp in _chip_copies(ins, outs, *sems):
            cp.wait()

    return _Carry(sums_bf16, _chip_landing(sums_bf16), False, _chip_sems(len(sums_bf16)), start, finish)


def _chip_sems(n):
    return [pltpu.SemaphoreType.DMA((3 * n,)), pltpu.SemaphoreType.DMA((3 * n,))]


def _chip_landing(sums_bf16):
    return [jax.ShapeDtypeStruct((N_CHIPS - 1,) + s.shape[1:], BF16) for s in sums_bf16]


def _chip_sum(sums_f32, landed, name):
    _, rows, cols = sums_f32.shape
    x, y, c = _mesh_pos()
    idx = jnp.stack([2 * x + y, c]).astype(jnp.int32)

    def body(idx_ref, o_ref, l_ref, out_ref):
        out_ref[...] = ((o_ref[...] + l_ref[0].astype(F32)) + l_ref[1].astype(F32)) + l_ref[2].astype(F32)

    return pl.pallas_call(
        body, name=name,
        grid_spec=pltpu.PrefetchScalarGridSpec(
            num_scalar_prefetch=1, grid=(1,),
            in_specs=[pl.BlockSpec((None, rows, cols), lambda i, idx: (idx[0], 0, 0)),
                      pl.BlockSpec((N_CHIPS - 1, rows, cols), lambda i, idx: (0, 0, 0))],
            out_specs=pl.BlockSpec((rows, cols), lambda i, idx: (idx[1], 0))),
        out_shape=jax.ShapeDtypeStruct((2 * rows, cols), F32),
        compiler_params=pltpu.CompilerParams(vmem_limit_bytes=_vmem_limit(3 * rows * cols * 4)),
    )(idx, sums_f32, landed)


def _halves_to_full(fulls, tag):
    n = len(fulls)

    def body(*refs):
        outs = refs[n:2 * n]
        send_sems, recv_sems = refs[2 * n:]
        x, y, c = _mesh_pos()
        copies = []
        for k in range(n):
            half = fulls[k].shape[0] // 2
            rows = outs[k].at[pl.ds(c * half, half), :]
            cp = pltpu.make_async_remote_copy(
                src_ref=rows, dst_ref=rows, send_sem=send_sems.at[k], recv_sem=recv_sems.at[k],
                device_id=(x, y, 1 - c), device_id_type=MESH)
            cp.start()
            copies.append(cp)
        for k in range(n):
            half = fulls[k].shape[0] // 2
            theirs = outs[k].at[pl.ds((1 - c) * half, half), :]
            pltpu.make_async_remote_copy(
                src_ref=theirs, dst_ref=theirs, send_sem=send_sems.at[k], recv_sem=recv_sems.at[k],
                device_id=(x, y, 1 - c), device_id_type=MESH).wait_recv()
        for cp in copies:
            cp.wait_send()

    return pl.pallas_call(
        body, name="grad_halves_to_full_" + tag,
        in_specs=_hbm_specs(n), out_specs=_hbm_specs(n),
        out_shape=[jax.ShapeDtypeStruct(f.shape, F32) for f in fulls],
        input_output_aliases={k: k for k in range(n)},
        scratch_shapes=[pltpu.SemaphoreType.DMA((n,)), pltpu.SemaphoreType.DMA((n,))],
    )(*fulls)


def _all_sum_small(v):
    rows, cols = v.shape
    n_dev = 8

    def body(v_ref, out_ref, buf, send_sems, recv_sems):
        x, y, c = _mesh_pos()
        me = 4 * x + 2 * y + c
        buf[me] = v_ref[...]
        peers = []
        for r in range(1, n_dev):
            px = 1 - x if r & 4 else x
            py = 1 - y if r & 2 else y
            pc = 1 - c if r & 1 else c
            peers.append((px, py, pc))
        copies = []
        for r, peer in enumerate(peers):
            cp = pltpu.make_async_remote_copy(
                src_ref=v_ref, dst_ref=buf.at[me], send_sem=send_sems.at[r], recv_sem=recv_sems.at[r],
                device_id=peer, device_id_type=MESH)
            cp.start()
            copies.append(cp)
        for r, (px, py, pc) in enumerate(peers):
            pltpu.make_async_remote_copy(
                src_ref=v_ref, dst_ref=buf.at[4 * px + 2 * py + pc], send_sem=send_sems.at[r], recv_sem=recv_sems.at[r],
                device_id=(px, py, pc), device_id_type=MESH).wait_recv()
        for cp in copies:
            cp.wait_send()
        acc = buf[0]
        for d in range(1, n_dev):
            acc = acc + buf[d]
        out_ref[...] = acc
        out_ref[3:4, :] = jnp.broadcast_to(jnp.sum(acc[3:4, :], axis=1, keepdims=True), (1, cols))

    vm = pl.BlockSpec(memory_space=pltpu.VMEM)
    return pl.pallas_call(
        body, name="all_sum_small", in_specs=[vm], out_specs=vm,
        out_shape=jax.ShapeDtypeStruct((rows, cols), F32),
        scratch_shapes=[pltpu.VMEM((n_dev, rows, cols), F32),
                        pltpu.SemaphoreType.DMA((n_dev - 1,)), pltpu.SemaphoreType.DMA((n_dev - 1,))],
    )(v)


def _adamw_math(w, g, m, v):
    m = ADAM_B1 * m + (1.0 - ADAM_B1) * g
    v = ADAM_B2 * v + (1.0 - ADAM_B2) * (g * g)
    m_hat = m / (1.0 - ADAM_B1 ** ADAM_STEP)
    v_hat = v / (1.0 - ADAM_B2 ** ADAM_STEP)
    delta = -ADAM_LR * (m_hat / (jnp.sqrt(v_hat) + ADAM_EPS) + ADAM_WD * w)
    return delta, m, v


def _adamw(w, g, m, v, name):
    rows, cols = w.shape
    tm = rows // 2 if (rows // 2) % 8 == 0 else rows
    return _rowwise(_adamw_math, [w, g, m, v], [], [(cols, F32)] * 3, [], tm=tm, name=name)


def _unshard_cols(gathered):
    n, r, c = gathered.shape
    return jnp.transpose(gathered, (1, 0, 2)).reshape(r, n * c)


def _shard_cols(full):
    r, nc = full.shape
    return jnp.transpose(full.reshape(r, N_CHIPS, nc // N_CHIPS), (1, 0, 2))


LATE = ["w_sb_up", "w_dil_up", "w_out", "w_ffn_in", "w_ffn_out"]


def _swap_middle(slabs4):
    return jnp.stack([slabs4[0], slabs4[2], slabs4[1], slabs4[3]])


def _late_weights(slabs, d_model, d_ff):
    g = dict(zip(LATE, slabs))
    return (_unshard_cols(g["w_sb_up"]), _unshard_cols(g["w_dil_up"]), g["w_out"].reshape(d_model, d_model),
            _unshard_cols(_swap_middle(g["w_ffn_in"])), g["w_ffn_out"].reshape(d_ff, d_model))


ROW_SHARDED = ("w_in", "w_out", "w_ffn_out")


def _chip_major(grads):
    out = []
    for k, g in grads.items():
        if k in ROW_SHARDED:
            out.append(g.reshape(N_CHIPS, g.shape[0] // N_CHIPS, g.shape[1]))
        else:
            out.append(_swap_middle(_shard_cols(g)) if k == "w_ffn_in" else _shard_cols(g))
    return out


def _pair_reduce(grads):
    full = _chip_major(grads)
    others = _pair_exchange(full, next(iter(grads)))
    return [_pair_sum(g, o, "grad_pair_sum_" + k) for g, o, k in zip(full, others, grads)]


def _chip_reduce(pair, landed, names):
    halves = [_chip_sum(p[0], l, "grad_chip_sum_" + k) for p, l, k in zip(pair, landed, names)]
    return dict(zip(names, _halves_to_full(halves, names[0])))


def _fwd_bwd(x, loss_target, g_mix, g_ffn, g_fin, wt_in, late_slabs):
    b_sz, s_len, d_model = x.shape
    t = b_sz * s_len
    d_ff = late_slabs[-1].shape[1] * N_CHIPS
    x2d = x.reshape(t, d_model)
    tgt2d = loss_target.reshape(t, d_model)

    (u,) = _rowwise(lambda xv, g: (_rms_stats(xv)[0] * g,), [x2d], [g_mix], [(d_model, BF16)], [], tm=512, name="norm_mix")
    qkv, (slab_ffn_out,) = _mm(u, wt_in, tb=True, b_cols=(0, QKV_WIDTH), tm=2048, tn=768, tk=d_model, name="proj_qkv",
                               carry=_gather_carry(late_slabs[4:]))
    gates = _mm(u, wt_in, tb=True, b_cols=(QKV_WIDTH, 2 * d_model), out_dtype=BF16, tm=t, tn=256, tk=d_model,
                name="proj_gates")
    qkv3 = qkv.reshape(b_sz, s_len, QKV_WIDTH)
    o_sb, small_slabs = _sb_fwd(qkv3, b_sz, s_len, _gather_carry(late_slabs[:3]))
    o_dl, lse, (slab_ffn_in,) = _dil_fwd(qkv3, b_sz, s_len, _gather_carry(late_slabs[3:4]))
    wf_sb_up, wf_dil_up, wf_out, wf_ffn_in, wf_ffn_out = _late_weights(
        list(small_slabs) + [slab_ffn_in, slab_ffn_out], d_model, d_ff)
    o_sb2, o_dl2 = o_sb.reshape(t, SB_WIDTH), o_dl.reshape(t, DIL_OUT_WIDTH)
    y_sb = _mm(o_sb2, wf_sb_up, out_dtype=BF16, tm=1024, tn=1024, tk=SB_WIDTH, name="sb_up")
    y_dl = _mm(o_dl2, wf_dil_up, out_dtype=BF16, tm=1024, tn=1024, tk=DIL_OUT_WIDTH, name="dil_up")

    def merge_fn(gt, ys, yd):
        return (_sigmoid(gt[:, :d_model]) * ys + _sigmoid(gt[:, d_model:]) * yd,)

    (merged,) = _rowwise(merge_fn, [gates, y_sb, y_dl], [], [(d_model, BF16)], [], tm=512, name="merge")
    x1 = _mm(merged, wf_out, add=x2d, tm=512, tn=1024, tk=d_model, name="mix_out")
    (u2,) = _rowwise(lambda xv, g: (_rms_stats(xv)[0] * g,), [x1], [g_ffn], [(d_model, BF16)], [], tm=512, name="norm_ffn")
    half_ff = d_ff // 2

    def act_fn(hv):
        gate = hv[:, :half_ff]
        return hv, gate * _sigmoid(gate) * hv[:, half_ff:]

    h, act = _mm(u2, wf_ffn_in, tm=512, tn=d_ff, tk=d_model, name="ffn_in",
                 epilogue=(act_fn, [], [], [(d_ff, BF16), (half_ff, BF16)], []))
    def head_fn(xv, tg, g):
        xhat, r = _rms_stats(xv)
        err = xhat * g - tg
        dy = err * (1.0 / d_model)
        dx, dg_rows = _rms_bwd(dy, xhat, r, g)
        loss_lanes = (0.5 / d_model) * jnp.sum(err * err, axis=0, keepdims=True)
        return dx, dx, jnp.sum(dg_rows, axis=0, keepdims=True), loss_lanes

    dx2, dx2_b, dg_fin, loss_lanes = _mm(
        act, wf_ffn_out, add=x1, tm=512, tn=1024, tk=d_ff, name="ffn_out",
        epilogue=(head_fn, [tgt2d], [g_fin], [(d_model, F32), (d_model, BF16)], [(1, d_model), (1, d_model)]))

    def dact_fn(da, hv):
        gate, up = hv[:, :half_ff], hv[:, half_ff:]
        sg = _sigmoid(gate)
        dgate = da * up * (sg * (1.0 + gate * (1.0 - sg)))
        return (jnp.concatenate([dgate, da * (gate * sg)], axis=1),)

    (dh,) = _mm(dx2_b, wf_ffn_out, tb=True, tm=512, tn=half_ff, tk=d_model, name="ffn_out_dx",
                epilogue=(dact_fn, [h], [], [(d_ff, BF16)], []))
    gw_ffn_out = _mm(act, dx2_b, ta=True, tm=256, tn=d_model, tk=t, name="ffn_out_dw")
    def norm_bwd_fn(du_, dres, xv, g):
        xhat, r = _rms_stats(xv)
        dx, dg_rows = _rms_bwd(du_, xhat, r, g)
        return dres + dx, jnp.sum(dg_rows, axis=0, keepdims=True)

    def norm_bwd_twice(*args):
        dx, dg = norm_bwd_fn(*args)
        return dx, dx, dg

    dx1, dx1_b, dg_ffn = _mm(dh, wf_ffn_in, tb=True, tm=512, tn=1024, tk=2 * d_ff, name="ffn_in_dx",
                             epilogue=(norm_bwd_twice, [dx2, x1], [g_ffn], [(d_model, F32), (d_model, BF16)], [(1, d_model)]))
    gw_ffn_in = _mm(u2, dh, ta=True, tm=d_model, tn=512, tk=t, name="ffn_in_dw")

    dmerged = _mm(dx1_b, wf_out, tb=True, out_dtype=BF16, tm=512, tn=1024, tk=d_model, name="mix_out_dx")
    gw_out = _mm(merged, dx1_b, ta=True, tm=256, tn=d_model, tk=t, name="mix_out_dw")

    def merge_bwd_fn(gt, ys, yd, dm):
        s_sb, s_dl = _sigmoid(gt[:, :d_model]), _sigmoid(gt[:, d_model:])
        dgates = jnp.concatenate([dm * ys * s_sb * (1.0 - s_sb), dm * yd * s_dl * (1.0 - s_dl)], axis=1)
        return dgates, dm * s_sb, dm * s_dl

    dgates, dy_sb, dy_dl = _rowwise(merge_bwd_fn, [gates, y_sb, y_dl, dmerged], [],
                                    [(2 * d_model, BF16), (d_model, BF16), (d_model, BF16)], [], tm=256, name="merge_bwd")
    do_sb = _mm(dy_sb, wf_sb_up, tb=True, out_dtype=BF16, tm=1024, tn=SB_WIDTH, tk=d_model, name="sb_up_dx")
    gw_sb_up = _mm(o_sb2, dy_sb, ta=True, tm=SB_WIDTH, tn=1024, tk=512, name="sb_up_dw")
    do_dl = _mm(dy_dl, wf_dil_up, tb=True, tm=1024, tn=DIL_OUT_WIDTH, tk=d_model, name="dil_up_dx")
    gw_dil_up = _mm(o_dl2, dy_dl, ta=True, tm=DIL_OUT_WIDTH, tn=1024, tk=512, name="dil_up_dw")
    late_grads = {"w_sb_up": gw_sb_up, "w_dil_up": gw_dil_up, "w_out": gw_out, "w_ffn_in": gw_ffn_in, "w_ffn_out": gw_ffn_out}
    pair = _pair_reduce(late_grads)
    (dq_sb, dk_sb, dv_sb), landed_a = _sb_bwd(qkv3, o_sb, do_sb.reshape(b_sz, s_len, SB_WIDTH), b_sz, s_len,
                                             _chip_carry([p[1] for p in pair[:4]]))
    (dq_dl, dk_dl, dv_dl), landed_b = _dil_bwd(qkv3, o_dl, lse, do_dl.reshape(b_sz, s_len, DIL_OUT_WIDTH), b_sz, s_len,
                                               _chip_carry([p[1] for p in pair[4:]]))
    landed = list(landed_a) + list(landed_b)
    dproj = jnp.concatenate(
        [a.reshape(t, -1) for a in (dq_sb, dk_sb, dv_sb)]
        + [a.reshape(t, -1) for a in (dq_dl, dk_dl, dv_dl)] + [dgates], axis=1)
    gwt_in = _mm(dproj, u, ta=True, tm=256, tn=d_model, tk=t, name="proj_dw")
    pair_in = _pair_reduce({"w_in": gwt_in})
    (dx, dg_mix), landed_in = _mm(
        dproj, wt_in, tm=512, tn=1024, tk=wt_in.shape[0], name="proj_dx",
        carry=_chip_carry([p[1] for p in pair_in]),
        epilogue=(norm_bwd_fn, [dx1, x2d], [g_mix], [(d_model, F32)], [(1, d_model)]))

    grads = _chip_reduce(pair, landed, LATE)
    grads.update(_chip_reduce(pair_in, landed_in, ["w_in"]))
    return dx, grads, dg_mix, dg_ffn, dg_fin, loss_lanes


def kernel(x, norm_mix_g, w_in, w_sb_up, w_dil_up, w_out, norm_ffn_g, w_ffn_in, w_ffn_out, norm_final_g, loss_target, m_norm_mix_g, m_w_in, m_w_sb_up, m_w_dil_up, m_w_out, m_norm_ffn_g, m_w_ffn_in, m_w_ffn_out, m_norm_final_g, v_norm_mix_g, v_w_in, v_w_sb_up, v_w_dil_up, v_w_out, v_norm_ffn_g, v_w_ffn_in, v_w_ffn_out, v_norm_final_g):
    b_sz, s_len, d_model = x.shape
    d_ff = w_ffn_out.shape[1] * N_CHIPS
    g_mix, g_ffn, g_fin = norm_mix_g, norm_ffn_g, norm_final_g.reshape(1, d_model)

    names = ["w_in", "w_sb_up", "w_dil_up", "w_out", "w_ffn_in", "w_ffn_out"]
    shards = {"w_in": jnp.swapaxes(w_in[0], 0, 1), "w_sb_up": w_sb_up[0], "w_dil_up": w_dil_up[0], "w_out": w_out[0],
              "w_ffn_in": w_ffn_in[0], "w_ffn_out": w_ffn_out[0]}
    (slab_in,) = _gather_weights([_cast_to_slab(shards["w_in"], "cast_w_in")])
    late_slabs = [_cast_to_slab(shards[k], "cast_" + k) for k in LATE]

    dx, grads, dg_mix, dg_ffn, dg_fin, loss_lanes = _fwd_bwd(
        x, loss_target, g_mix, g_ffn, g_fin, slab_in.reshape(-1, d_model), late_slabs)

    small = jnp.concatenate([dg_mix, dg_ffn, dg_fin, loss_lanes, jnp.zeros((4, d_model), F32)], axis=0)
    small = _all_sum_small(small)
    loss = small[3, 0]
    gains = jnp.concatenate([g_mix, g_ffn, g_fin, jnp.zeros((5, d_model), F32)], axis=0)
    gains_m = jnp.concatenate([m_norm_mix_g, m_norm_ffn_g, m_norm_final_g.reshape(1, d_model), jnp.zeros((5, d_model), F32)], axis=0)
    gains_v = jnp.concatenate([v_norm_mix_g, v_norm_ffn_g, v_norm_final_g.reshape(1, d_model), jnp.ones((5, d_model), F32)], axis=0)
    gd, gm, gv = _rowwise(_adamw_math, [gains, small, gains_m, gains_v], [], [(d_model, F32)] * 3, [], tm=8, name="adamw_gains")

    moments = {"w_in": (jnp.swapaxes(m_w_in[0], 0, 1), jnp.swapaxes(v_w_in[0], 0, 1)),
               "w_sb_up": (m_w_sb_up[0], v_w_sb_up[0]), "w_dil_up": (m_w_dil_up[0], v_w_dil_up[0]),
               "w_out": (m_w_out[0], v_w_out[0]), "w_ffn_in": (m_w_ffn_in[0], v_w_ffn_in[0]),
               "w_ffn_out": (m_w_ffn_out[0], v_w_ffn_out[0])}
    upd = {k: _adamw(shards[k], grads[k], moments[k][0], moments[k][1], "adamw_" + k) for k in names}

    def as_output(k, a):
        return (jnp.swapaxes(a, 0, 1) if k == "w_in" else a)[None]

    def w_out_of(i):
        return [as_output(k, upd[k][i]) for k in names]

    def ordered(mix, ws, ffn_g, fin):
        return [mix, ws[0], ws[1], ws[2], ws[3], ffn_g, ws[4], ws[5], fin]

    grad_ws = [as_output(k, grads[k]) for k in names]
    outs = [loss, dx.reshape(b_sz, s_len, d_model)]
    outs += ordered(small[0:1], grad_ws, small[1:2], small[2])
    outs += ordered(gd[0:1], w_out_of(0), gd[1:2], gd[2])
    outs += ordered(gm[0:1], w_out_of(1), gm[1:2], gm[2])
    outs += ordered(gv[0:1], w_out_of(2), gv[1:2], gv[2])
    return tuple(outs)
```

```python
import functools
import math

import jax
import jax.numpy as jnp
from jax import lax
from jax.experimental import pallas as pl
from jax.experimental.pallas import tpu as pltpu

F32 = jnp.float32
BF16 = jnp.bfloat16
MESH = pl.DeviceIdType.MESH

HEAD_DIM = 64
SB_HEADS = 8
DIL_PAIRS = ((128, 1), (512, 4), (2048, 16))
DIL_HEADS_PER_GROUP = 4
DIL_HEADS = DIL_HEADS_PER_GROUP * len(DIL_PAIRS)
SB_WIDTH = SB_HEADS * HEAD_DIM
DIL_WIDTH = DIL_HEADS * HEAD_DIM
DIL_OUT_WIDTH = DIL_HEADS_PER_GROUP * HEAD_DIM
QKV_WIDTH = 3 * SB_WIDTH + 3 * DIL_WIDTH
RMS_EPS = 1e-6
ALIBI_MAX_BIAS = 8.0
ADAM_LR = 0.001
ADAM_B1 = 0.9
ADAM_B2 = 0.999
ADAM_EPS = 1e-08
ADAM_WD = 0.01
ADAM_STEP = 10

LANES = 128
BLK = 128
NEG = -1e30
EXP_UNDERFLOW = -104.0
SB_FWD_CHAINS = 4
SB_BWD_CHAINS = 4
DIL_CHAINS = 4
N_CHIPS = 4
VMEM_CAP = 56 * 1024 * 1024


def _vmem_limit(tile_bytes):
    return int(min(VMEM_CAP, max(32 * 1024 * 1024, 3 * tile_bytes + 8 * 1024 * 1024)))


def _nbytes(shape, dtype):
    return math.prod(shape) * jnp.dtype(dtype).itemsize


def _dot(a, b):
    return jnp.dot(a, b, preferred_element_type=F32)


def _dot_nt(a, b):
    return lax.dot_general(a, b, (((1,), (1,)), ((), ())), preferred_element_type=F32)


def _dot_tn(a, b):
    return lax.dot_general(a, b, (((0,), (0,)), ((), ())), preferred_element_type=F32)


def _split2(x):
    hi = x.astype(BF16)
    lo = (x - hi.astype(F32)).astype(BF16)
    return hi, lo


def _sigmoid(x):
    return pl.reciprocal(1.0 + jnp.exp(-x), approx=True)


class _Carry:
    def __init__(self, arrays=(), out_shapes=(), aliased=False, sems=(), start=None, finish=None):
        self.arrays, self.out_shapes, self.aliased = list(arrays), list(out_shapes), aliased
        self.sems, self.start, self.finish = list(sems), start, finish

    def __bool__(self):
        return bool(self.arrays)

    def call_args(self, n_in, n_out):
        aliases = {n_in + k: n_out + k for k in range(len(self.arrays))} if self.aliased else {}
        return _hbm_specs(len(self.arrays)), _hbm_specs(len(self.out_shapes)), self.out_shapes, aliases, self.sems

    def run(self, refs, n_in, n_out, step, n_steps, compute):
        if not self:
            compute()
            return
        n_c, n_o, n_s = len(self.arrays), len(self.out_shapes), len(self.sems)
        ins = refs[n_in:n_in + n_c]
        outs = refs[n_in + n_c + n_out:n_in + n_c + n_out + n_o]
        sems = refs[len(refs) - n_s:]

        @pl.when(step == 0)
        def _():
            self.start(ins, outs, sems)

        compute()

        @pl.when(step == n_steps - 1)
        def _():
            self.finish(ins, outs, sems)


def _mm(a, b, *, ta=False, tb=False, add=None, out_dtype=F32, tm, tn, tk, name, carry=None, epilogue=None,
        b_cols=None):
    carry = carry or _Carry()
    n_car = len(carry.arrays)
    if ta:
        kdim, m = a.shape
    else:
        m, kdim = a.shape
    if tb:
        n, k2 = b.shape
    else:
        k2, n = b.shape
    col0 = 0
    if b_cols is not None:
        assert b_cols[0] % tn == 0, name
        col0, n = b_cols[0] // tn, b_cols[1]
    assert kdim == k2 and m % tm == 0 and n % tn == 0 and kdim % tk == 0, (name, a.shape, b.shape)
    nk = kdim // tk
    grid = (m // tm, n // tn, nk)
    a_mode = dict(pipeline_mode=pl.Buffered(1)) if grid[0] == 1 and nk == 1 else {}
    b_mode = dict(pipeline_mode=pl.Buffered(1)) if grid[1] == 1 and nk == 1 else {}
    a_spec = (pl.BlockSpec((tk, tm), lambda i, j, k: (k, i), **a_mode) if ta
              else pl.BlockSpec((tm, tk), lambda i, j, k: (i, k), **a_mode))
    b_spec = (pl.BlockSpec((tn, tk), lambda i, j, k: (j + col0, k), **b_mode) if tb
              else pl.BlockSpec((tk, tn), lambda i, j, k: (k, j + col0), **b_mode))
    o_spec = pl.BlockSpec((tm, tn), lambda i, j, k: (i, j))
    dims = ((((0,) if ta else (1,)), ((1,) if tb else (0,))), ((), ()))
    has_add = add is not None
    if epilogue is None:
        ep_fn, ep_rows, ep_params, ep_outs, ep_accs = None, [], [], [], []
        out_sds, out_specs = [jax.ShapeDtypeStruct((m, n), out_dtype)], [o_spec]
    else:
        ep_fn, ep_rows, ep_params, ep_outs, ep_accs = epilogue
        assert grid[1] == 1 or not ep_accs, name
        out_sds = [jax.ShapeDtypeStruct((m, w * grid[1]), d) for w, d in ep_outs]
        out_sds += [jax.ShapeDtypeStruct(sh, F32) for sh in ep_accs]
        out_specs = [pl.BlockSpec((tm, w), lambda i, j, k: (i, j)) for w, _ in ep_outs]
        out_specs += [pl.BlockSpec(sh, lambda i, j, k: (0, 0)) for sh in ep_accs]
    n_main = len(out_sds)
    use_scratch = nk > 1 and (ep_fn is not None or jnp.dtype(out_dtype) != jnp.dtype(F32))
    n_in = 2 + has_add + len(ep_rows) + len(ep_params)

    def finish(total, refs, pid):
        outs = refs[n_in + n_car:n_in + n_car + n_main]
        if ep_fn is None:
            outs[0][...] = total.astype(out_dtype)
            return
        first = 2 + has_add
        rows = [r[...].astype(F32) for r in refs[first:first + len(ep_rows)]]
        params = [p[...] for p in refs[first + len(ep_rows):n_in]]
        res = ep_fn(total, *rows, *params)
        for o_ref, v in zip(outs[:len(ep_outs)], res):
            o_ref[...] = v.astype(o_ref.dtype)
        acc_refs = outs[len(ep_outs):]
        if acc_refs:
            @pl.when(pid[0] == 0)
            def _():
                for r in acc_refs:
                    r[...] = jnp.zeros(r.shape, F32)

            for r, v in zip(acc_refs, res[len(ep_outs):]):
                r[...] += v

    def compute(refs, pid):
        a_ref, b_ref = refs[0], refs[1]
        add_ref = refs[2] if has_add else None
        prod = lax.dot_general(a_ref[...].astype(BF16), b_ref[...].astype(BF16), dims, preferred_element_type=F32)
        if nk == 1:
            finish(prod + add_ref[...] if has_add else prod, refs, pid)
            return
        acc_ref = refs[n_in + n_car + n_main + len(carry.out_shapes)] if use_scratch else refs[n_in + n_car]
        k = pid[2]

        @pl.when(k == 0)
        def _():
            acc_ref[...] = prod + add_ref[...] if has_add else prod

        @pl.when(k > 0)
        def _():
            acc_ref[...] += prod

        if use_scratch:
            @pl.when(k == nk - 1)
            def _():
                finish(acc_ref[...], refs, pid)

    def body(*refs):
        pid = (pl.program_id(0), pl.program_id(1), pl.program_id(2))
        step = (pid[0] * grid[1] + pid[1]) * nk + pid[2]
        carry.run(refs, n_in, n_main, step, grid[0] * grid[1] * nk, lambda: compute(refs, pid))

    tile_bytes = (_nbytes((tm, tk), a.dtype) + _nbytes((tk, tn), b.dtype) + 2 * _nbytes((tm, tn), F32)
                  + (_nbytes((tm, tn), F32) if has_add else 0)
                  + sum(_nbytes((tm, r.shape[1]), r.dtype) for r in ep_rows) + sum(_nbytes((tm, w), d) for w, d in ep_outs))
    in_specs = [a_spec, b_spec] + ([o_spec] if has_add else [])
    in_specs += [pl.BlockSpec((tm, r.shape[1] // grid[1]), lambda i, j, k: (i, j)) for r in ep_rows]
    in_specs += [pl.BlockSpec(p.shape, lambda i, j, k: (0, 0)) for p in ep_params]
    args = (a, b) + ((add,) if has_add else ()) + tuple(ep_rows) + tuple(ep_params)
    scratch = [pltpu.VMEM((tm, tn), F32)] if use_scratch else []
    serial = bool(carry) or bool(ep_accs)
    c_in, c_out, c_shapes, c_alias, c_sems = carry.call_args(n_in, n_main)
    res = pl.pallas_call(
        body, name=name, grid=grid,
        in_specs=in_specs + c_in, out_specs=out_specs + c_out, out_shape=out_sds + c_shapes,
        input_output_aliases=c_alias, scratch_shapes=scratch + c_sems,
        compiler_params=pltpu.CompilerParams(
            dimension_semantics=("arbitrary",) * 3 if serial else ("parallel", "parallel", "arbitrary"),
            vmem_limit_bytes=_vmem_limit(tile_bytes)),
    )(*args, *carry.arrays)
    main = res[0] if ep_fn is None else list(res[:n_main])
    return (main, res[n_main:]) if carry else main


def _rowwise(fn, rows, params, outs, accs, *, tm, name):
    t = rows[0].shape[0]
    assert t % tm == 0, (name, t, tm)
    n_r, n_p, n_o = len(rows), len(params), len(outs)

    def body(*refs):
        vals = [r[...].astype(F32) for r in refs[:n_r]] + [p[...] for p in refs[n_r:n_r + n_p]]
        res = fn(*vals)
        o_refs = refs[n_r + n_p:n_r + n_p + n_o]
        a_refs = refs[n_r + n_p + n_o:]
        for o_ref, v in zip(o_refs, res[:n_o]):
            o_ref[...] = v.astype(o_ref.dtype)
        if accs:
            @pl.when(pl.program_id(0) == 0)
            def _():
                for a_ref in a_refs:
                    a_ref[...] = jnp.zeros(a_ref.shape, F32)

            for a_ref, v in zip(a_refs, res[n_o:]):
                a_ref[...] += v

    in_specs = [pl.BlockSpec((tm, r.shape[1]), lambda i: (i, 0)) for r in rows]
    in_specs += [pl.BlockSpec(p.shape, lambda i: (0, 0)) for p in params]
    out_specs = [pl.BlockSpec((tm, w), lambda i: (i, 0)) for w, _ in outs]
    out_specs += [pl.BlockSpec(s, lambda i: (0, 0)) for s in accs]
    out_shape = [jax.ShapeDtypeStruct((t, w), d) for w, d in outs]
    out_shape += [jax.ShapeDtypeStruct(s, F32) for s in accs]
    tile_bytes = sum(_nbytes((tm, r.shape[1]), r.dtype) for r in rows) + sum(_nbytes((tm, w), F32) for w, _ in outs)
    res = pl.pallas_call(
        body, name=name, grid=(t // tm,), in_specs=in_specs, out_specs=out_specs, out_shape=out_shape,
        compiler_params=pltpu.CompilerParams(
            dimension_semantics=("arbitrary",) if accs else ("parallel",),
            vmem_limit_bytes=_vmem_limit(2 * tile_bytes)),
    )(*rows, *params)
    return res


def _rms_stats(x):
    r = lax.rsqrt(jnp.mean(x * x, axis=-1, keepdims=True) + RMS_EPS)
    return x * r, r


def _rms_bwd(dy, xhat, r, g):
    dxhat = dy * g
    dx = r * (dxhat - xhat * jnp.mean(dxhat * xhat, axis=-1, keepdims=True))
    return dx, dy * xhat


def _sb_consts():
    lane = lax.broadcasted_iota(jnp.int32, (BLK, LANES), 1)
    head0 = lane < HEAD_DIM
    row = lax.broadcasted_iota(jnp.int32, (2 * BLK, BLK), 0) % BLK
    col = lax.broadcasted_iota(jnp.int32, (2 * BLK, BLK), 1)
    causal = col < row
    jj = lax.broadcasted_iota(jnp.int32, (BLK, BLK), 0)
    ss = lax.broadcasted_iota(jnp.int32, (BLK, BLK), 1)
    suffix = jnp.where(jj > ss, 1.0, 0.0).astype(BF16)
    return head0, causal, suffix


def _stack_heads(x, head0):
    zero = jnp.zeros_like(x)
    return jnp.concatenate([jnp.where(head0, x, zero), jnp.where(head0, zero, x)], axis=0)


def _sb_logits(z, causal, masked):
    sp = jnp.log(1.0 + jnp.exp(-jnp.abs(z)))
    log_keep = -(jnp.maximum(z, 0.0) + sp)
    log_beta = jnp.minimum(z, 0.0) - sp
    if masked:
        log_keep = jnp.where(causal, log_keep, 0.0)
    return log_keep, log_beta


def _suffix_sums(x, suffix):
    hi, lo = _split2(x)
    after = _dot(hi, suffix) + _dot(lo, suffix)
    total = jnp.broadcast_to(after[:, 0:1] + x[:, 0:1], x.shape)
    return after, total


def _sb_walk_back(i, state, per_chain, tile):
    def alive(st):
        worst = functools.reduce(jnp.maximum, [st[p][:, 0:1] for p in range(0, len(st), per_chain)])
        return jnp.max(worst) > EXP_UNDERFLOW

    def cond(c):
        return jnp.logical_and(c[0] < i, alive(c[1]))

    def body(c):
        return c[0] + 1, tile(i - 1 - c[0], c[1], False)

    return lax.while_loop(cond, body, (jnp.int32(0), state))[1]


def _lane_blocks(x, n):
    return [x[:, p * LANES:(p + 1) * LANES] for p in range(n)]


def _sb_fwd(qkv, b_sz, s_len, carry):
    nq = s_len // BLK
    n_pairs = SB_WIDTH // LANES
    ch = SB_FWD_CHAINS
    n_steps = n_pairs // ch
    scale = 1.0 / math.sqrt(HEAD_DIM)

    def compute(q_ref, k_ref, v_ref, o_ref):
        head0, causal, suffix = _sb_consts()

        def q_block(i, _):
            qs = pl.multiple_of(i * BLK, BLK)
            q_all = (q_ref[pl.ds(qs, BLK), :] * scale).astype(BF16)
            q01 = [_stack_heads(q, head0) for q in _lane_blocks(q_all, ch)]

            def tile(j, state, masked):
                ks = pl.multiple_of(j * BLK, BLK)
                ks_ = _lane_blocks(k_ref[pl.ds(ks, BLK), :].astype(BF16), ch)
                vs_ = _lane_blocks(v_ref[pl.ds(ks, BLK), :].astype(BF16), ch)
                zs = [_dot_nt(q01[p], ks_[p]) for p in range(ch)]
                logits = [_sb_logits(z, causal, masked) for z in zs]
                sums = [_suffix_sums(lg[0], suffix) for lg in logits]
                out = []
                for p in range(ch):
                    carry, acc = state[2 * p], state[2 * p + 1]
                    after, total = sums[p]
                    a = jnp.exp(logits[p][1] + carry + after)
                    if masked:
                        a = jnp.where(causal, a, 0.0)
                    a_hi, a_lo = _split2(a)
                    a_cat = jnp.concatenate([a_hi[:BLK], a_hi[BLK:], a_lo[:BLK], a_lo[BLK:]], axis=1)
                    v01 = _stack_heads(vs_[p], head0)
                    out += [carry + total, acc + _dot(a_cat, jnp.concatenate([v01, v01], axis=0))]
                return tuple(out)

            state = (jnp.zeros((2 * BLK, BLK), F32), jnp.zeros((BLK, LANES), F32)) * ch
            state = tile(i, state, True)
            state = _sb_walk_back(i, state, 2, tile)
            o_ref[pl.ds(qs, BLK), :] = jnp.concatenate([state[2 * p + 1] for p in range(ch)], axis=1)
            return 0

        lax.fori_loop(0, nq, q_block, 0)

    def body(*refs):
        step = pl.program_id(0) * n_steps + pl.program_id(1)
        o_ref = refs[3 + len(carry.arrays)]
        carry.run(refs, 3, 1, step, b_sz * n_steps, lambda: compute(refs[0], refs[1], refs[2], o_ref))

    blk = lambda off: pl.BlockSpec((None, s_len, ch * LANES), lambda b, p: (b, 0, off + p))
    c_in, c_out, c_shapes, c_alias, c_sems = carry.call_args(3, 1)
    res = pl.pallas_call(
        body, name="sb_fwd", grid=(b_sz, n_steps),
        in_specs=[blk(0), blk(n_steps), blk(2 * n_steps)] + c_in, out_specs=[blk(0)] + c_out,
        out_shape=[jax.ShapeDtypeStruct((b_sz, s_len, SB_WIDTH), F32)] + c_shapes,
        input_output_aliases=c_alias, scratch_shapes=c_sems,
        compiler_params=pltpu.CompilerParams(dimension_semantics=("arbitrary", "arbitrary"),
                                             vmem_limit_bytes=VMEM_CAP),
    )(qkv, qkv, qkv, *carry.arrays)
    return res[0], res[1:]


def _sb_bwd(qkv, o_sb, do_sb, b_sz, s_len, carry):
    nq = s_len // BLK
    n_pairs = SB_WIDTH // LANES
    ch = SB_BWD_CHAINS
    n_steps = n_pairs // ch
    scale = 1.0 / math.sqrt(HEAD_DIM)

    def compute(q_ref, k_ref, v_ref, o_ref, do_ref, dq_ref, dk_ref, dv_ref, dk_acc, dv_acc):
        head0, causal, suffix = _sb_consts()
        lrow = lax.broadcasted_iota(jnp.int32, (LANES, LANES), 0)
        ones_h0 = jnp.where(lrow < HEAD_DIM, 1.0, 0.0).astype(BF16)
        ones_h1 = jnp.where(lrow >= HEAD_DIM, 1.0, 0.0).astype(BF16)
        dk_acc[...] = jnp.zeros(dk_acc.shape, F32)
        dv_acc[...] = jnp.zeros(dv_acc.shape, F32)

        def q_block(i, _):
            qs = pl.multiple_of(i * BLK, BLK)
            q_all = (q_ref[pl.ds(qs, BLK), :] * scale).astype(BF16)
            do_all = do_ref[pl.ds(qs, BLK), :].astype(BF16)
            dd_all = do_all.astype(F32) * o_ref[pl.ds(qs, BLK), :]
            q01 = [_stack_heads(q, head0) for q in _lane_blocks(q_all, ch)]
            do01 = [_stack_heads(d, head0) for d in _lane_blocks(do_all, ch)]
            tot = []
            for dd in _lane_blocks(dd_all, ch):
                dd_hi, dd_lo = _split2(dd)
                tot.append(jnp.concatenate([_dot(dd_hi, ones_h0) + _dot(dd_lo, ones_h0),
                                            _dot(dd_hi, ones_h1) + _dot(dd_lo, ones_h1)], axis=0))

            def tile(j, state, masked):
                ks = pl.multiple_of(j * BLK, BLK)
                ks_ = _lane_blocks(k_ref[pl.ds(ks, BLK), :].astype(BF16), ch)
                vs_ = _lane_blocks(v_ref[pl.ds(ks, BLK), :].astype(BF16), ch)
                zs = [_dot_nt(q01[p], ks_[p]) for p in range(ch)]
                das = [_dot_nt(do01[p], vs_[p]) for p in range(ch)]
                logits = [_sb_logits(z, causal, masked) for z in zs]
                sums = [_suffix_sums(lg[0], suffix) for lg in logits]
                a_s, e_s = [], []
                for p in range(ch):
                    a = jnp.exp(logits[p][1] + state[3 * p] + sums[p][0])
                    if masked:
                        a = jnp.where(causal, a, 0.0)
                    a_s.append(a)
                    e_s.append(a * das[p])
                e_sums = [_suffix_sums(e, suffix) for e in e_s]
                out, dks, dvs = [], [], []
                for p in range(ch):
                    carry, rcarry, dq = state[3 * p:3 * p + 3]
                    e = e_s[p]
                    before = tot[p] - (rcarry + e_sums[p][0] + e)
                    beta = jnp.exp(logits[p][1])
                    dz = e * (1.0 - beta) - beta * before
                    if masked:
                        dz = jnp.where(causal, dz, 0.0)
                    dz_b = dz.astype(BF16)
                    dks.append(_dot_tn(dz_b, q01[p]))
                    dvs.append(_dot_tn(a_s[p].astype(BF16), do01[p]))
                    out += [carry + sums[p][1], rcarry + e_sums[p][1], dq + _dot(dz_b, ks_[p])]
                dk_acc[pl.ds(ks, BLK), :] += jnp.concatenate(dks, axis=1)
                dv_acc[pl.ds(ks, BLK), :] += jnp.concatenate(dvs, axis=1)
                return tuple(out)

            state = (jnp.zeros((2 * BLK, BLK), F32),) * (3 * ch)
            state = tile(i, state, True)
            state = _sb_walk_back(i, state, 3, tile)
            dq = [jnp.where(head0, state[3 * p + 2][:BLK], state[3 * p + 2][BLK:]) for p in range(ch)]
            dq_ref[pl.ds(qs, BLK), :] = (jnp.concatenate(dq, axis=1) * scale).astype(dq_ref.dtype)
            return 0

        lax.fori_loop(0, nq, q_block, 0)
        dk_ref[...] = dk_acc[...].astype(dk_ref.dtype)
        dv_ref[...] = dv_acc[...].astype(dv_ref.dtype)

    def body(*refs):
        step = pl.program_id(0) * n_steps + pl.program_id(1)
        n_c, n_o = len(carry.arrays), len(carry.out_shapes)
        own = refs[:5] + refs[5 + n_c:8 + n_c] + refs[8 + n_c + n_o:10 + n_c + n_o]
        carry.run(refs, 5, 3, step, b_sz * n_steps, lambda: compute(*own))

    blk = lambda off: pl.BlockSpec((None, s_len, ch * LANES), lambda b, p: (b, 0, off + p))
    once = lambda off: pl.BlockSpec((None, s_len, ch * LANES), lambda b, p: (b, 0, off + p),
                                    pipeline_mode=pl.Buffered(1))
    out_sd = jax.ShapeDtypeStruct((b_sz, s_len, SB_WIDTH), BF16)
    c_in, c_out, c_shapes, c_alias, c_sems = carry.call_args(5, 3)
    res = pl.pallas_call(
        body, name="sb_bwd", grid=(b_sz, n_steps),
        in_specs=[once(0), once(n_steps), once(2 * n_steps), once(0), once(0)] + c_in,
        out_specs=[blk(0), blk(0), blk(0)] + c_out, out_shape=[out_sd, out_sd, out_sd] + c_shapes,
        input_output_aliases=c_alias,
        scratch_shapes=[pltpu.VMEM((s_len, ch * LANES), F32), pltpu.VMEM((s_len, ch * LANES), F32)] + c_sems,
        compiler_params=pltpu.CompilerParams(dimension_semantics=("arbitrary", "arbitrary"),
                                             vmem_limit_bytes=VMEM_CAP),
    )(qkv, qkv, qkv, o_sb, do_sb, *carry.arrays)
    return res[:3], res[3:]


def _dil_consts(group, pair_idx, dilation):
    lane = lax.broadcasted_iota(jnp.int32, (BLK, LANES), 1)
    head0 = lane < HEAD_DIM
    row = lax.broadcasted_iota(jnp.int32, (2 * BLK, BLK), 0)
    qa = row % BLK
    kb = lax.broadcasted_iota(jnp.int32, (2 * BLK, BLK), 1)
    head = (group * DIL_HEADS_PER_GROUP + 2 * pair_idx + row // BLK).astype(F32)
    slope = jnp.exp((-ALIBI_MAX_BIAS * math.log(2.0) / DIL_HEADS) * (head + 1.0))
    valid_cur = kb <= qa
    valid_prev = kb >= qa
    bias_cur = -slope * ((qa - kb) * dilation).astype(F32)
    bias_prev = -slope * ((BLK + qa - kb) * dilation).astype(F32)
    return head0, valid_cur, valid_prev, bias_cur, bias_prev


def _dil_units(s_len, dilation):
    nb = s_len // dilation // BLK
    return [(r, n) for r in range(dilation) for n in range(nb)]


def _dil_rows(n, r, dilation):
    if dilation == 1:
        return pl.ds(n * BLK, BLK)
    return pl.ds(n * BLK * dilation + r, BLK, stride=dilation)


def _dil_scores(q01, k, bias, valid):
    s = _dot_nt(q01, k) * (1.0 / math.sqrt(HEAD_DIM)) + bias
    return jnp.where(valid, s, NEG)


def _dil_fwd(qkv, b_sz, s_len, carry):
    n_pairs = DIL_OUT_WIDTH // LANES
    q_off = 3 * SB_WIDTH // LANES
    per_kind = DIL_WIDTH // LANES

    def compute(pair_idx, qkv_refs, o_ref, lse_ref, m_s, l_s):
        m_s[...] = jnp.full(m_s.shape, NEG, F32)
        l_s[...] = jnp.zeros(l_s.shape, F32)
        o_ref[...] = jnp.zeros(o_ref.shape, F32)
        for g, (_, dilation) in enumerate(DIL_PAIRS):
            q_ref, k_ref, v_ref = qkv_refs[3 * g:3 * g + 3]
            head0, valid_cur, valid_prev, bias_cur, bias_prev = _dil_consts(g, pair_idx, dilation)
            units = _dil_units(s_len, dilation)
            for u0 in range(0, len(units), DIL_CHAINS):
                group = units[u0:u0 + DIL_CHAINS]
                rows_of = [_dil_rows(n, r, dilation) for r, n in group]
                scores, values = [], []
                for (r, n), rows in zip(group, rows_of):
                    q01 = _stack_heads(q_ref[rows, :].astype(BF16), head0)
                    sc = [_dil_scores(q01, k_ref[rows, :].astype(BF16), bias_cur, valid_cur)]
                    vals = [_stack_heads(v_ref[rows, :].astype(BF16), head0)]
                    if n > 0:
                        prev = _dil_rows(n - 1, r, dilation)
                        sc.append(_dil_scores(q01, k_ref[prev, :].astype(BF16), bias_prev, valid_prev))
                        vals.append(_stack_heads(v_ref[prev, :].astype(BF16), head0))
                    scores.append(sc)
                    values.append(vals)
                stats = []
                for sc, rows in zip(scores, rows_of):
                    m_blk = functools.reduce(jnp.maximum, [jnp.max(x, axis=-1, keepdims=True) for x in sc])
                    m_old = jnp.concatenate([m_s.at[0][rows, :], m_s.at[1][rows, :]], axis=0)
                    l_old = jnp.concatenate([l_s.at[0][rows, :], l_s.at[1][rows, :]], axis=0)
                    m_new = jnp.maximum(m_old, m_blk)
                    probs = [jnp.exp(x - m_new) for x in sc]
                    l_blk = functools.reduce(jnp.add, [jnp.sum(p, axis=-1, keepdims=True) for p in probs])
                    alpha = jnp.exp(m_old - m_new)
                    stats.append((m_new, alpha * l_old + l_blk, alpha, probs))
                for (m_new, l_new, alpha, probs), vals, rows in zip(stats, values, rows_of):
                    alpha_tok = jnp.where(head0, alpha[:BLK], alpha[BLK:])
                    p_cat = jnp.concatenate(
                        [h for p in probs for h in (p[:BLK].astype(BF16), p[BLK:].astype(BF16))], axis=1)
                    o_ref[rows, :] = alpha_tok * o_ref[rows, :] + _dot(p_cat, jnp.concatenate(vals, axis=0))
                    m_s.at[0][rows, :] = m_new[:BLK]
                    m_s.at[1][rows, :] = m_new[BLK:]
                    l_s.at[0][rows, :] = l_new[:BLK]
                    l_s.at[1][rows, :] = l_new[BLK:]
        lane = lax.broadcasted_iota(jnp.int32, (BLK, LANES), 1)
        for c in range(s_len // BLK):
            rows = pl.ds(c * BLK, BLK)
            l0, l1 = l_s.at[0][rows, :], l_s.at[1][rows, :]
            o_ref[rows, :] = o_ref[rows, :] / jnp.where(lane < HEAD_DIM, l0, l1)
            lse_ref.at[0][rows, :] = m_s.at[0][rows, :] + jnp.log(l0)
            lse_ref.at[1][rows, :] = m_s.at[1][rows, :] + jnp.log(l1)

    def body(*refs):
        pair_idx = pl.program_id(1)
        step = pl.program_id(0) * n_pairs + pair_idx
        n_c, n_o = len(carry.arrays), len(carry.out_shapes)
        o_ref, lse_ref = refs[9 + n_c:11 + n_c]
        m_s, l_s = refs[11 + n_c + n_o:13 + n_c + n_o]
        carry.run(refs, 9, 2, step, b_sz * n_pairs, lambda: compute(pair_idx, refs[:9], o_ref, lse_ref, m_s, l_s))

    in_specs = []
    for g in range(len(DIL_PAIRS)):
        for kind in range(3):
            off = q_off + kind * per_kind + g * n_pairs
            in_specs.append(pl.BlockSpec((None, s_len, LANES), lambda b, p, off=off: (b, 0, off + p)))
    c_in, c_out, c_shapes, c_alias, c_sems = carry.call_args(9, 2)
    res = pl.pallas_call(
        body, name="dil_fwd", grid=(b_sz, n_pairs),
        in_specs=in_specs + c_in,
        out_specs=[pl.BlockSpec((None, s_len, LANES), lambda b, p: (b, 0, p)),
                   pl.BlockSpec((None, None, 2, s_len, LANES), lambda b, p: (b, p, 0, 0, 0))] + c_out,
        out_shape=[jax.ShapeDtypeStruct((b_sz, s_len, DIL_OUT_WIDTH), F32),
                   jax.ShapeDtypeStruct((b_sz, n_pairs, 2, s_len, LANES), F32)] + c_shapes,
        input_output_aliases=c_alias,
        scratch_shapes=[pltpu.VMEM((2, s_len, LANES), F32), pltpu.VMEM((2, s_len, LANES), F32)] + c_sems,
        compiler_params=pltpu.CompilerParams(dimension_semantics=("arbitrary", "arbitrary"),
                                             vmem_limit_bytes=VMEM_CAP),
    )(*([qkv] * 9), *carry.arrays)
    return res[0], res[1], res[2:]


def _dil_bwd(qkv, o_dl, lse, do_dl, b_sz, s_len, carry):
    n_pairs = DIL_OUT_WIDTH // LANES
    n_groups = len(DIL_PAIRS)
    q_off = 3 * SB_WIDTH // LANES
    per_kind = DIL_WIDTH // LANES

    def compute(pair_idx, group, q_ref, k_ref, v_ref, o_ref, lse_ref, do_ref, dq_ref, dk_ref, dv_ref, d_s, dq_s, dk_s, dv_s):
        lrow = lax.broadcasted_iota(jnp.int32, (LANES, LANES), 0)
        ones_h0 = jnp.where(lrow < HEAD_DIM, 1.0, 0.0).astype(BF16)
        ones_h1 = jnp.where(lrow >= HEAD_DIM, 1.0, 0.0).astype(BF16)
        for c in range(s_len // BLK):
            rows = pl.ds(c * BLK, BLK)
            dd_hi, dd_lo = _split2(do_ref[rows, :] * o_ref[rows, :])
            d_s.at[0][rows, :] = _dot(dd_hi, ones_h0) + _dot(dd_lo, ones_h0)
            d_s.at[1][rows, :] = _dot(dd_hi, ones_h1) + _dot(dd_lo, ones_h1)
        dk_s[...] = jnp.zeros(dk_s.shape, F32)
        dv_s[...] = jnp.zeros(dv_s.shape, F32)

        def one_group(g, dilation):
            head0, valid_cur, valid_prev, bias_cur, bias_prev = _dil_consts(g, pair_idx, dilation)
            units = _dil_units(s_len, dilation)
            scale = 1.0 / math.sqrt(HEAD_DIM)
            for u0 in range(0, len(units), DIL_CHAINS):
                chunk = units[u0:u0 + DIL_CHAINS]
                loaded = []
                for r, n in chunk:
                    rows = _dil_rows(n, r, dilation)
                    q01 = _stack_heads(q_ref[rows, :].astype(BF16), head0)
                    do01 = _stack_heads(do_ref[rows, :].astype(BF16), head0)
                    lse01 = jnp.concatenate([lse_ref.at[0][rows, :], lse_ref.at[1][rows, :]], axis=0)
                    d01 = jnp.concatenate([d_s.at[0][rows, :], d_s.at[1][rows, :]], axis=0)
                    blocks = [(rows, bias_cur, valid_cur)]
                    if n > 0:
                        blocks.append((_dil_rows(n - 1, r, dilation), bias_prev, valid_prev))
                    parts = []
                    for krows, bias, valid in blocks:
                        k = k_ref[krows, :].astype(BF16)
                        v = v_ref[krows, :].astype(BF16)
                        parts.append((krows, k, _dil_scores(q01, k, bias, valid), _dot_nt(do01, v)))
                    loaded.append((rows, q01, do01, lse01, d01, parts))
                grads = []
                for rows, q01, do01, lse01, d01, parts in loaded:
                    for krows, k, sc, dp in parts:
                        p = jnp.exp(sc - lse01)
                        grads.append((p.astype(BF16), (p * (dp - d01) * scale).astype(BF16)))
                it = iter(grads)
                updates = []
                for rows, q01, do01, lse01, d01, parts in loaded:
                    dq = jnp.zeros((2 * BLK, LANES), F32)
                    for krows, k, sc, dp in parts:
                        p_b, ds = next(it)
                        dq = dq + _dot(ds, k)
                        updates.append((krows, _dot_tn(ds, q01), _dot_tn(p_b, do01)))
                    dq_s[rows, :] = jnp.where(head0, dq[:BLK], dq[BLK:])
                for krows, dk, dv in updates:
                    dk_s[krows, :] = dk_s[krows, :] + dk
                    dv_s[krows, :] = dv_s[krows, :] + dv

        for g, (_, dilation) in enumerate(DIL_PAIRS):
            pl.when(group == g)(functools.partial(one_group, g, dilation))
        dq_ref[...] = dq_s[...].astype(dq_ref.dtype)
        dk_ref[...] = dk_s[...].astype(dk_ref.dtype)
        dv_ref[...] = dv_s[...].astype(dv_ref.dtype)

    def body(*refs):
        pair_idx, group = pl.program_id(1), pl.program_id(2)
        step = (pl.program_id(0) * n_pairs + pair_idx) * n_groups + group
        n_c, n_o = len(carry.arrays), len(carry.out_shapes)
        own = refs[:6] + refs[6 + n_c:9 + n_c] + refs[9 + n_c + n_o:13 + n_c + n_o]
        carry.run(refs, 6, 3, step, b_sz * n_pairs * n_groups, lambda: compute(pair_idx, group, *own))

    def qkv_spec(kind):
        return pl.BlockSpec((None, s_len, LANES),
                            lambda b, p, g: (b, 0, q_off + kind * per_kind + g * n_pairs + p))

    tok_spec = pl.BlockSpec((None, s_len, LANES), lambda b, p, g: (b, 0, p))
    out_spec = pl.BlockSpec((None, s_len, LANES), lambda b, p, g: (b, 0, g * n_pairs + p))
    out_sd = jax.ShapeDtypeStruct((b_sz, s_len, DIL_WIDTH), BF16)
    c_in, c_out, c_shapes, c_alias, c_sems = carry.call_args(6, 3)
    res = pl.pallas_call(
        body, name="dil_bwd", grid=(b_sz, n_pairs, n_groups),
        in_specs=[qkv_spec(0), qkv_spec(1), qkv_spec(2), tok_spec,
                  pl.BlockSpec((None, None, 2, s_len, LANES), lambda b, p, g: (b, p, 0, 0, 0)), tok_spec] + c_in,
        out_specs=[out_spec, out_spec, out_spec] + c_out,
        out_shape=[out_sd, out_sd, out_sd] + c_shapes,
        input_output_aliases=c_alias,
        scratch_shapes=[pltpu.VMEM((2, s_len, LANES), F32)] + [pltpu.VMEM((s_len, LANES), F32)] * 3 + c_sems,
        compiler_params=pltpu.CompilerParams(dimension_semantics=("arbitrary", "arbitrary", "arbitrary"),
                                             vmem_limit_bytes=VMEM_CAP),
    )(qkv, qkv, qkv, o_dl, lse, do_dl, *carry.arrays)
    return res[:3], res[3:]


def _mesh_pos():
    return lax.axis_index("x"), lax.axis_index("y"), lax.axis_index("c")


def _other_chips(x, y):
    return [(1 - x, y), (x, 1 - y), (1 - x, 1 - y)]


def _hbm_specs(n):
    return [pl.BlockSpec(memory_space=pl.ANY)] * n


SWAPPED = ("w_ffn_in",)


def _slot(x, y, swapped):
    return 2 * y + x if swapped else 2 * x + y


def _cast_to_slab(w, name, swapped=False):
    rows, cols = w.shape
    mine = jnp.reshape(_slot(lax.axis_index("x"), lax.axis_index("y"), swapped), (1,)).astype(jnp.int32)

    def body(idx_ref, w_ref, o_ref):
        o_ref[...] = w_ref[...].astype(BF16)

    return pl.pallas_call(
        body, name=name,
        grid_spec=pltpu.PrefetchScalarGridSpec(
            num_scalar_prefetch=1, grid=(1,),
            in_specs=[pl.BlockSpec((rows, cols), lambda i, idx: (0, 0))],
            out_specs=pl.BlockSpec((None, rows, cols), lambda i, idx: (idx[0], 0, 0))),
        out_shape=jax.ShapeDtypeStruct((N_CHIPS, rows, cols), BF16),
        compiler_params=pltpu.CompilerParams(vmem_limit_bytes=_vmem_limit(rows * cols * 6)),
    )(mine, w)


def _gather_issue(slabs, send_sems, recv_sems, swapped):
    x, y, c = _mesh_pos()
    for k, slab in enumerate(slabs):
        half = slab.shape[1] // 2
        rows = slab.at[_slot(x, y, swapped[k]), pl.ds(c * half, half), :]
        for r, (px, py) in enumerate(_other_chips(x, y)):
            pltpu.make_async_remote_copy(
                src_ref=rows, dst_ref=rows, send_sem=send_sems.at[6 * k + r], recv_sem=recv_sems.at[6 * k + r],
                device_id=(px, py, c), device_id_type=MESH).start()


def _gather_complete(slabs, send_sems, recv_sems, swapped):
    x, y, c = _mesh_pos()
    chips = _other_chips(x, y)

    def copy(k, sem, block, rows, to):
        ref = slabs[k].at[block, rows, :]
        return pltpu.make_async_remote_copy(
            src_ref=ref, dst_ref=ref, send_sem=send_sems.at[sem], recv_sem=recv_sems.at[sem],
            device_id=to, device_id_type=MESH)

    for k, slab in enumerate(slabs):
        half = slab.shape[1] // 2
        for r, (px, py) in enumerate(chips):
            theirs = _slot(px, py, swapped[k])
            copy(k, 6 * k + r, theirs, pl.ds(c * half, half), (px, py, c)).wait_recv()
            copy(k, 6 * k + 3 + r, theirs, pl.ds(c * half, half), (x, y, 1 - c)).start()
    for k, slab in enumerate(slabs):
        half = slab.shape[1] // 2
        for r, (px, py) in enumerate(chips):
            copy(k, 6 * k + 3 + r, _slot(px, py, swapped[k]), pl.ds((1 - c) * half, half), (x, y, 1 - c)).wait_recv()
    for k, slab in enumerate(slabs):
        half = slab.shape[1] // 2
        for r, (px, py) in enumerate(chips):
            copy(k, 6 * k + r, _slot(x, y, swapped[k]), pl.ds(c * half, half), (px, py, c)).wait_send()
            copy(k, 6 * k + 3 + r, _slot(px, py, swapped[k]), pl.ds(c * half, half), (x, y, 1 - c)).wait_send()


def _gather_sems(n):
    return [pltpu.SemaphoreType.DMA((6 * n,)), pltpu.SemaphoreType.DMA((6 * n,))]


def _gather_carry(slabs, names):
    swapped = [k in SWAPPED for k in names]
    return _Carry(slabs, [jax.ShapeDtypeStruct(a.shape, a.dtype) for a in slabs], True, _gather_sems(len(slabs)),
                  lambda ins, outs, sems: _gather_issue(outs, *sems, swapped),
                  lambda ins, outs, sems: _gather_complete(outs, *sems, swapped))


def _gather_weights(slabs):
    n = len(slabs)

    def body(*refs):
        outs = refs[n:2 * n]
        send_sems, recv_sems = refs[2 * n:]
        _gather_issue(outs, send_sems, recv_sems, [False] * n)
        _gather_complete(outs, send_sems, recv_sems, [False] * n)

    return pl.pallas_call(
        body, name="gather_weights",
        in_specs=_hbm_specs(n), out_specs=_hbm_specs(n),
        out_shape=[jax.ShapeDtypeStruct(s.shape, s.dtype) for s in slabs],
        input_output_aliases={k: k for k in range(n)},
        scratch_shapes=_gather_sems(n),
    )(*slabs)


def _pair_exchange(grads, tag):
    n = len(grads)

    def body(*refs):
        ins, outs = refs[:n], refs[n:2 * n]
        send_sems, recv_sems = refs[2 * n:]
        x, y, c = _mesh_pos()
        copies = []
        for k in range(n):
            half = grads[k].shape[1] // 2
            cp = pltpu.make_async_remote_copy(
                src_ref=ins[k].at[:, pl.ds((1 - c) * half, half), :], dst_ref=outs[k],
                send_sem=send_sems.at[k], recv_sem=recv_sems.at[k],
                device_id=(x, y, 1 - c), device_id_type=MESH)
            cp.start()
            copies.append(cp)
        for cp in copies:
            cp.wait()

    return pl.pallas_call(
        body, name="grad_pair_exchange_" + tag,
        in_specs=_hbm_specs(n), out_specs=_hbm_specs(n),
        out_shape=[jax.ShapeDtypeStruct((N_CHIPS, g.shape[1] // 2, g.shape[2]), F32) for g in grads],
        scratch_shapes=[pltpu.SemaphoreType.DMA((n,)), pltpu.SemaphoreType.DMA((n,))],
    )(*grads)


def _pair_sum(grad, other, name):
    _, rows, cols = grad.shape
    half = rows // 2
    core = jnp.reshape(lax.axis_index("c"), (1,)).astype(jnp.int32)

    def body(core_ref, g_ref, p_ref, s_ref, sb_ref):
        s = g_ref[...] + p_ref[...]
        s_ref[...] = s
        sb_ref[...] = s.astype(BF16)

    blk = pl.BlockSpec((None, half, cols), lambda p, core_ref: (p, 0, 0))
    return pl.pallas_call(
        body, name=name,
        grid_spec=pltpu.PrefetchScalarGridSpec(
            num_scalar_prefetch=1, grid=(N_CHIPS,),
            in_specs=[pl.BlockSpec((None, half, cols), lambda p, core_ref: (p, core_ref[0], 0)), blk],
            out_specs=[blk, blk]),
        out_shape=[jax.ShapeDtypeStruct((N_CHIPS, half, cols), F32),
                   jax.ShapeDtypeStruct((N_CHIPS, half, cols), BF16)],
        compiler_params=pltpu.CompilerParams(dimension_semantics=("parallel",),
                                             vmem_limit_bytes=_vmem_limit(4 * half * cols * 4)),
    )(core, grad, other)


def _chip_copies(sums_bf16, lands, send_sems, recv_sems, swapped):
    x, y, c = _mesh_pos()
    return [pltpu.make_async_remote_copy(
        src_ref=sums_bf16[k].at[_slot(px, py, swapped[k])], dst_ref=lands[k].at[r],
        send_sem=send_sems.at[3 * k + r], recv_sem=recv_sems.at[3 * k + r],
        device_id=(px, py, c), device_id_type=MESH)
        for k in range(len(sums_bf16)) for r, (px, py) in enumerate(_other_chips(x, y))]


def _chip_carry(sums_bf16, names):
    swapped = [k in SWAPPED for k in names]

    def start(ins, outs, sems):
        for cp in _chip_copies(ins, outs, *sems, swapped):
            cp.start()

    def finish(ins, outs, sems):
        for cp in _chip_copies(ins, outs, *sems, swapped):
            cp.wait()

    return _Carry(sums_bf16, _chip_landing(sums_bf16), False, _chip_sems(len(sums_bf16)), start, finish)


def _chip_sems(n):
    return [pltpu.SemaphoreType.DMA((3 * n,)), pltpu.SemaphoreType.DMA((3 * n,))]


def _chip_landing(sums_bf16):
    return [jax.ShapeDtypeStruct((N_CHIPS - 1,) + s.shape[1:], BF16) for s in sums_bf16]


def _chip_sum(sums_f32, landed, name, swapped):
    _, rows, cols = sums_f32.shape
    x, y, c = _mesh_pos()
    idx = jnp.stack([_slot(x, y, swapped), c]).astype(jnp.int32)

    def body(idx_ref, o_ref, l_ref, out_ref):
        out_ref[...] = ((o_ref[...] + l_ref[0].astype(F32)) + l_ref[1].astype(F32)) + l_ref[2].astype(F32)

    return pl.pallas_call(
        body, name=name,
        grid_spec=pltpu.PrefetchScalarGridSpec(
            num_scalar_prefetch=1, grid=(1,),
            in_specs=[pl.BlockSpec((None, rows, cols), lambda i, idx: (idx[0], 0, 0)),
                      pl.BlockSpec((N_CHIPS - 1, rows, cols), lambda i, idx: (0, 0, 0))],
            out_specs=pl.BlockSpec((rows, cols), lambda i, idx: (idx[1], 0))),
        out_shape=jax.ShapeDtypeStruct((2 * rows, cols), F32),
        compiler_params=pltpu.CompilerParams(vmem_limit_bytes=_vmem_limit(3 * rows * cols * 4)),
    )(idx, sums_f32, landed)


def _halves_to_full(fulls, tag):
    n = len(fulls)

    def body(*refs):
        outs = refs[n:2 * n]
        send_sems, recv_sems = refs[2 * n:]
        x, y, c = _mesh_pos()
        copies = []
        for k in range(n):
            half = fulls[k].shape[0] // 2
            rows = outs[k].at[pl.ds(c * half, half), :]
            cp = pltpu.make_async_remote_copy(
                src_ref=rows, dst_ref=rows, send_sem=send_sems.at[k], recv_sem=recv_sems.at[k],
                device_id=(x, y, 1 - c), device_id_type=MESH)
            cp.start()
            copies.append(cp)
        for k in range(n):
            half = fulls[k].shape[0] // 2
            theirs = outs[k].at[pl.ds((1 - c) * half, half), :]
            pltpu.make_async_remote_copy(
                src_ref=theirs, dst_ref=theirs, send_sem=send_sems.at[k], recv_sem=recv_sems.at[k],
                device_id=(x, y, 1 - c), device_id_type=MESH).wait_recv()
        for cp in copies:
            cp.wait_send()

    return pl.pallas_call(
        body, name="grad_halves_to_full_" + tag,
        in_specs=_hbm_specs(n), out_specs=_hbm_specs(n),
        out_shape=[jax.ShapeDtypeStruct(f.shape, F32) for f in fulls],
        input_output_aliases={k: k for k in range(n)},
        scratch_shapes=[pltpu.SemaphoreType.DMA((n,)), pltpu.SemaphoreType.DMA((n,))],
    )(*fulls)


def _all_sum_small(v):
    rows, cols = v.shape
    n_dev = 8

    def body(v_ref, out_ref, buf, send_sems, recv_sems):
        x, y, c = _mesh_pos()
        me = 4 * x + 2 * y + c
        buf[me] = v_ref[...]
        peers = []
        for r in range(1, n_dev):
            px = 1 - x if r & 4 else x
            py = 1 - y if r & 2 else y
            pc = 1 - c if r & 1 else c
            peers.append((px, py, pc))
        copies = []
        for r, peer in enumerate(peers):
            cp = pltpu.make_async_remote_copy(
                src_ref=v_ref, dst_ref=buf.at[me], send_sem=send_sems.at[r], recv_sem=recv_sems.at[r],
                device_id=peer, device_id_type=MESH)
            cp.start()
            copies.append(cp)
        for r, (px, py, pc) in enumerate(peers):
            pltpu.make_async_remote_copy(
                src_ref=v_ref, dst_ref=buf.at[4 * px + 2 * py + pc], send_sem=send_sems.at[r], recv_sem=recv_sems.at[r],
                device_id=(px, py, pc), device_id_type=MESH).wait_recv()
        for cp in copies:
            cp.wait_send()
        acc = buf[0]
        for d in range(1, n_dev):
            acc = acc + buf[d]
        out_ref[...] = acc
        out_ref[3:4, :] = jnp.broadcast_to(jnp.sum(acc[3:4, :], axis=1, keepdims=True), (1, cols))

    vm = pl.BlockSpec(memory_space=pltpu.VMEM)
    return pl.pallas_call(
        body, name="all_sum_small", in_specs=[vm], out_specs=vm,
        out_shape=jax.ShapeDtypeStruct((rows, cols), F32),
        scratch_shapes=[pltpu.VMEM((n_dev, rows, cols), F32),
                        pltpu.SemaphoreType.DMA((n_dev - 1,)), pltpu.SemaphoreType.DMA((n_dev - 1,))],
    )(v)


def _adamw_math(w, g, m, v):
    m = ADAM_B1 * m + (1.0 - ADAM_B1) * g
    v = ADAM_B2 * v + (1.0 - ADAM_B2) * (g * g)
    m_hat = m / (1.0 - ADAM_B1 ** ADAM_STEP)
    v_hat = v / (1.0 - ADAM_B2 ** ADAM_STEP)
    delta = -ADAM_LR * (m_hat / (jnp.sqrt(v_hat) + ADAM_EPS) + ADAM_WD * w)
    return delta, m, v


def _adamw(w, g, m, v, name):
    rows, cols = w.shape
    tm = rows // 2 if (rows // 2) % 8 == 0 else rows
    return _rowwise(_adamw_math, [w, g, m, v], [], [(cols, F32)] * 3, [], tm=tm, name=name)


def _unshard_cols(gathered):
    n, r, c = gathered.shape
    return jnp.transpose(gathered, (1, 0, 2)).reshape(r, n * c)


def _shard_cols(full):
    r, nc = full.shape
    return jnp.transpose(full.reshape(r, N_CHIPS, nc // N_CHIPS), (1, 0, 2))


LATE = ["w_sb_up", "w_dil_up", "w_out", "w_ffn_in", "w_ffn_out"]


def _late_weights(slabs, d_model, d_ff):
    g = dict(zip(LATE, slabs))
    return (_unshard_cols(g["w_sb_up"]), _unshard_cols(g["w_dil_up"]), g["w_out"].reshape(d_model, d_model),
            _unshard_cols(g["w_ffn_in"]), g["w_ffn_out"].reshape(d_ff, d_model))


ROW_SHARDED = ("w_in", "w_out", "w_ffn_out")


def _chip_major(grads):
    out = []
    for k, g in grads.items():
        if k in ROW_SHARDED:
            out.append(g.reshape(N_CHIPS, g.shape[0] // N_CHIPS, g.shape[1]))
        else:
            out.append(_shard_cols(g))
    return out


def _pair_reduce(grads):
    full = _chip_major(grads)
    others = _pair_exchange(full, next(iter(grads)))
    return [_pair_sum(g, o, "grad_pair_sum_" + k) for g, o, k in zip(full, others, grads)]


def _chip_reduce(pair, landed, names):
    halves = [_chip_sum(p[0], l, "grad_chip_sum_" + k, k in SWAPPED) for p, l, k in zip(pair, landed, names)]
    return dict(zip(names, _halves_to_full(halves, names[0])))


def _fwd_bwd(x, loss_target, g_mix, g_ffn, g_fin, wt_in, late_slabs):
    b_sz, s_len, d_model = x.shape
    t = b_sz * s_len
    d_ff = late_slabs[-1].shape[1] * N_CHIPS
    x2d = x.reshape(t, d_model)
    tgt2d = loss_target.reshape(t, d_model)

    (u,) = _rowwise(lambda xv, g: (_rms_stats(xv)[0] * g,), [x2d], [g_mix], [(d_model, BF16)], [], tm=512, name="norm_mix")
    qkv, (slab_ffn_out,) = _mm(u, wt_in, tb=True, b_cols=(0, QKV_WIDTH), tm=2048, tn=768, tk=d_model, name="proj_qkv",
                               carry=_gather_carry(late_slabs[4:], LATE[4:]))
    gates = _mm(u, wt_in, tb=True, b_cols=(QKV_WIDTH, 2 * d_model), out_dtype=BF16, tm=t, tn=256, tk=d_model,
                name="proj_gates")
    qkv3 = qkv.reshape(b_sz, s_len, QKV_WIDTH)
    o_sb, (slab_ffn_in,) = _sb_fwd(qkv3, b_sz, s_len, _gather_carry(late_slabs[3:4], LATE[3:4]))
    o_dl, lse, small_slabs = _dil_fwd(qkv3, b_sz, s_len, _gather_carry(late_slabs[:3], LATE[:3]))
    wf_sb_up, wf_dil_up, wf_out, wf_ffn_in, wf_ffn_out = _late_weights(
        list(small_slabs) + [slab_ffn_in, slab_ffn_out], d_model, d_ff)
    o_sb2, o_dl2 = o_sb.reshape(t, SB_WIDTH), o_dl.reshape(t, DIL_OUT_WIDTH)
    y_sb = _mm(o_sb2, wf_sb_up, out_dtype=BF16, tm=1024, tn=1024, tk=SB_WIDTH, name="sb_up")
    y_dl = _mm(o_dl2, wf_dil_up, out_dtype=BF16, tm=1024, tn=1024, tk=DIL_OUT_WIDTH, name="dil_up")

    def merge_fn(gt, ys, yd):
        return (_sigmoid(gt[:, :d_model]) * ys + _sigmoid(gt[:, d_model:]) * yd,)

    (merged,) = _rowwise(merge_fn, [gates, y_sb, y_dl], [], [(d_model, BF16)], [], tm=512, name="merge")
    x1 = _mm(merged, wf_out, add=x2d, tm=512, tn=1024, tk=d_model, name="mix_out")
    (u2,) = _rowwise(lambda xv, g: (_rms_stats(xv)[0] * g,), [x1], [g_ffn], [(d_model, BF16)], [], tm=512, name="norm_ffn")
    half_ff = d_ff // 2

    def act_fn(hv):
        gate = hv[:, :half_ff]
        return hv, gate * _sigmoid(gate) * hv[:, half_ff:]

    h, act = _mm(u2, wf_ffn_in, tm=512, tn=d_ff, tk=d_model, name="ffn_in",
                 epilogue=(act_fn, [], [], [(d_ff, BF16), (half_ff, BF16)], []))
    def head_fn(xv, tg, g):
        xhat, r = _rms_stats(xv)
        err = xhat * g - tg
        dy = err * (1.0 / d_model)
        dx, dg_rows = _rms_bwd(dy, xhat, r, g)
        loss_lanes = (0.5 / d_model) * jnp.sum(err * err, axis=0, keepdims=True)
        return dx, dx, jnp.sum(dg_rows, axis=0, keepdims=True), loss_lanes

    dx2, dx2_b, dg_fin, loss_lanes = _mm(
        act, wf_ffn_out, add=x1, tm=512, tn=1024, tk=d_ff, name="ffn_out",
        epilogue=(head_fn, [tgt2d], [g_fin], [(d_model, F32), (d_model, BF16)], [(1, d_model), (1, d_model)]))

    def dact_fn(da, hv):
        gate, up = hv[:, :half_ff], hv[:, half_ff:]
        sg = _sigmoid(gate)
        dgate = da * up * (sg * (1.0 + gate * (1.0 - sg)))
        return (jnp.concatenate([dgate, da * (gate * sg)], axis=1),)

    (dh,) = _mm(dx2_b, wf_ffn_out, tb=True, tm=512, tn=half_ff, tk=d_model, name="ffn_out_dx",
                epilogue=(dact_fn, [h], [], [(d_ff, BF16)], []))
    gw_ffn_out = _mm(act, dx2_b, ta=True, tm=256, tn=d_model, tk=t, name="ffn_out_dw")
    def norm_bwd_fn(du_, dres, xv, g):
        xhat, r = _rms_stats(xv)
        dx, dg_rows = _rms_bwd(du_, xhat, r, g)
        return dres + dx, jnp.sum(dg_rows, axis=0, keepdims=True)

    def norm_bwd_twice(*args):
        dx, dg = norm_bwd_fn(*args)
        return dx, dx, dg

    dx1, dx1_b, dg_ffn = _mm(dh, wf_ffn_in, tb=True, tm=512, tn=1024, tk=2 * d_ff, name="ffn_in_dx",
                             epilogue=(norm_bwd_twice, [dx2, x1], [g_ffn], [(d_model, F32), (d_model, BF16)], [(1, d_model)]))
    gw_ffn_in = _mm(u2, dh, ta=True, tm=d_model, tn=512, tk=t, name="ffn_in_dw")

    dmerged = _mm(dx1_b, wf_out, tb=True, out_dtype=BF16, tm=512, tn=1024, tk=d_model, name="mix_out_dx")
    gw_out = _mm(merged, dx1_b, ta=True, tm=256, tn=d_model, tk=t, name="mix_out_dw")

    def merge_bwd_fn(gt, ys, yd, dm):
        s_sb, s_dl = _sigmoid(gt[:, :d_model]), _sigmoid(gt[:, d_model:])
        dgates = jnp.concatenate([dm * ys * s_sb * (1.0 - s_sb), dm * yd * s_dl * (1.0 - s_dl)], axis=1)
        return dgates, dm * s_sb, dm * s_dl

    dgates, dy_sb, dy_dl = _rowwise(merge_bwd_fn, [gates, y_sb, y_dl, dmerged], [],
                                    [(2 * d_model, BF16), (d_model, BF16), (d_model, BF16)], [], tm=256, name="merge_bwd")
    do_sb = _mm(dy_sb, wf_sb_up, tb=True, out_dtype=BF16, tm=1024, tn=SB_WIDTH, tk=d_model, name="sb_up_dx")
    gw_sb_up = _mm(o_sb2, dy_sb, ta=True, tm=SB_WIDTH, tn=1024, tk=512, name="sb_up_dw")
    do_dl = _mm(dy_dl, wf_dil_up, tb=True, tm=1024, tn=DIL_OUT_WIDTH, tk=d_model, name="dil_up_dx")
    gw_dil_up = _mm(o_dl2, dy_dl, ta=True, tm=DIL_OUT_WIDTH, tn=1024, tk=512, name="dil_up_dw")
    late_grads = {"w_sb_up": gw_sb_up, "w_dil_up": gw_dil_up, "w_out": gw_out, "w_ffn_in": gw_ffn_in, "w_ffn_out": gw_ffn_out}
    pair = _pair_reduce(late_grads)
    (dq_sb, dk_sb, dv_sb), landed_a = _sb_bwd(qkv3, o_sb, do_sb.reshape(b_sz, s_len, SB_WIDTH), b_sz, s_len,
                                             _chip_carry([p[1] for p in pair[:4]], LATE[:4]))
    (dq_dl, dk_dl, dv_dl), landed_b = _dil_bwd(qkv3, o_dl, lse, do_dl.reshape(b_sz, s_len, DIL_OUT_WIDTH), b_sz, s_len,
                                               _chip_carry([p[1] for p in pair[4:]], LATE[4:]))
    landed = list(landed_a) + list(landed_b)
    dproj = jnp.concatenate(
        [a.reshape(t, -1) for a in (dq_sb, dk_sb, dv_sb)]
        + [a.reshape(t, -1) for a in (dq_dl, dk_dl, dv_dl)] + [dgates], axis=1)
    gwt_in = _mm(dproj, u, ta=True, tm=256, tn=d_model, tk=t, name="proj_dw")
    pair_in = _pair_reduce({"w_in": gwt_in})
    (dx, dg_mix), landed_in = _mm(
        dproj, wt_in, tm=512, tn=1024, tk=wt_in.shape[0], name="proj_dx",
        carry=_chip_carry([p[1] for p in pair_in], ["w_in"]),
        epilogue=(norm_bwd_fn, [dx1, x2d], [g_mix], [(d_model, F32)], [(1, d_model)]))

    grads = _chip_reduce(pair, landed, LATE)
    grads.update(_chip_reduce(pair_in, landed_in, ["w_in"]))
    return dx, grads, dg_mix, dg_ffn, dg_fin, loss_lanes


def kernel(x, norm_mix_g, w_in, w_sb_up, w_dil_up, w_out, norm_ffn_g, w_ffn_in, w_ffn_out, norm_final_g, loss_target, m_norm_mix_g, m_w_in, m_w_sb_up, m_w_dil_up, m_w_out, m_norm_ffn_g, m_w_ffn_in, m_w_ffn_out, m_norm_final_g, v_norm_mix_g, v_w_in, v_w_sb_up, v_w_dil_up, v_w_out, v_norm_ffn_g, v_w_ffn_in, v_w_ffn_out, v_norm_final_g):
    b_sz, s_len, d_model = x.shape
    d_ff = w_ffn_out.shape[1] * N_CHIPS
    g_mix, g_ffn, g_fin = norm_mix_g, norm_ffn_g, norm_final_g.reshape(1, d_model)

    names = ["w_in", "w_sb_up", "w_dil_up", "w_out", "w_ffn_in", "w_ffn_out"]
    shards = {"w_in": jnp.swapaxes(w_in[0], 0, 1), "w_sb_up": w_sb_up[0], "w_dil_up": w_dil_up[0], "w_out": w_out[0],
              "w_ffn_in": w_ffn_in[0], "w_ffn_out": w_ffn_out[0]}
    (slab_in,) = _gather_weights([_cast_to_slab(shards["w_in"], "cast_w_in")])
    late_slabs = [_cast_to_slab(shards[k], "cast_" + k, k in SWAPPED) for k in LATE]

    dx, grads, dg_mix, dg_ffn, dg_fin, loss_lanes = _fwd_bwd(
        x, loss_target, g_mix, g_ffn, g_fin, slab_in.reshape(-1, d_model), late_slabs)

    small = jnp.concatenate([dg_mix, dg_ffn, dg_fin, loss_lanes, jnp.zeros((4, d_model), F32)], axis=0)
    small = _all_sum_small(small)
    loss = small[3, 0]
    gains = jnp.concatenate([g_mix, g_ffn, g_fin, jnp.zeros((5, d_model), F32)], axis=0)
    gains_m = jnp.concatenate([m_norm_mix_g, m_norm_ffn_g, m_norm_final_g.reshape(1, d_model), jnp.zeros((5, d_model), F32)], axis=0)
    gains_v = jnp.concatenate([v_norm_mix_g, v_norm_ffn_g, v_norm_final_g.reshape(1, d_model), jnp.ones((5, d_model), F32)], axis=0)
    gd, gm, gv = _rowwise(_adamw_math, [gains, small, gains_m, gains_v], [], [(d_model, F32)] * 3, [], tm=8, name="adamw_gains")

    moments = {"w_in": (jnp.swapaxes(m_w_in[0], 0, 1), jnp.swapaxes(v_w_in[0], 0, 1)),
               "w_sb_up": (m_w_sb_up[0], v_w_sb_up[0]), "w_dil_up": (m_w_dil_up[0], v_w_dil_up[0]),
               "w_out": (m_w_out[0], v_w_out[0]), "w_ffn_in": (m_w_ffn_in[0], v_w_ffn_in[0]),
               "w_ffn_out": (m_w_ffn_out[0], v_w_ffn_out[0])}
    upd = {k: _adamw(shards[k], grads[k], moments[k][0], moments[k][1], "adamw_" + k) for k in names}

    def as_output(k, a):
        return (jnp.swapaxes(a, 0, 1) if k == "w_in" else a)[None]

    def w_out_of(i):
        return [as_output(k, upd[k][i]) for k in names]

    def ordered(mix, ws, ffn_g, fin):
        return [mix, ws[0], ws[1], ws[2], ws[3], ffn_g, ws[4], ws[5], fin]

    grad_ws = [as_output(k, grads[k]) for k in names]
    outs = [loss, dx.reshape(b_sz, s_len, d_model)]
    outs += ordered(small[0:1], grad_ws, small[1:2], small[2])
    outs += ordered(gd[0:1], w_out_of(0), gd[1:2], gd[2])
    outs += ordered(gm[0:1], w_out_of(1), gm[1:2], gm[2])
    outs += ordered(gv[0:1], w_out_of(2), gv[1:2], gv[2])
    return tuple(outs)
```

```python
import functools
import math

import jax
import jax.numpy as jnp
from jax import lax
from jax.experimental import pallas as pl
from jax.experimental.pallas import tpu as pltpu

F32 = jnp.float32
BF16 = jnp.bfloat16
MESH = pl.DeviceIdType.MESH

HEAD_DIM = 64
SB_HEADS = 8
DIL_PAIRS = ((128, 1), (512, 4), (2048, 16))
DIL_HEADS_PER_GROUP = 4
DIL_HEADS = DIL_HEADS_PER_GROUP * len(DIL_PAIRS)
SB_WIDTH = SB_HEADS * HEAD_DIM
DIL_WIDTH = DIL_HEADS * HEAD_DIM
DIL_OUT_WIDTH = DIL_HEADS_PER_GROUP * HEAD_DIM
QKV_WIDTH = 3 * SB_WIDTH + 3 * DIL_WIDTH
RMS_EPS = 1e-6
ALIBI_MAX_BIAS = 8.0
ADAM_LR = 0.001
ADAM_B1 = 0.9
ADAM_B2 = 0.999
ADAM_EPS = 1e-08
ADAM_WD = 0.01
ADAM_STEP = 10

LANES = 128
BLK = 128
NEG = -1e30
EXP_UNDERFLOW = -104.0
SB_FWD_CHAINS = 4
SB_BWD_CHAINS = 4
DIL_CHAINS = 4
N_CHIPS = 4
VMEM_CAP = 56 * 1024 * 1024


def _vmem_limit(tile_bytes):
    return int(min(VMEM_CAP, max(32 * 1024 * 1024, 3 * tile_bytes + 8 * 1024 * 1024)))


def _nbytes(shape, dtype):
    return math.prod(shape) * jnp.dtype(dtype).itemsize


def _dot(a, b):
    return jnp.dot(a, b, preferred_element_type=F32)


def _dot_nt(a, b):
    return lax.dot_general(a, b, (((1,), (1,)), ((), ())), preferred_element_type=F32)


def _dot_tn(a, b):
    return lax.dot_general(a, b, (((0,), (0,)), ((), ())), preferred_element_type=F32)


def _split2(x):
    hi = x.astype(BF16)
    lo = (x - hi.astype(F32)).astype(BF16)
    return hi, lo


def _sigmoid(x):
    return pl.reciprocal(1.0 + jnp.exp(-x), approx=True)


class _Carry:
    def __init__(self, arrays=(), out_shapes=(), aliased=False, sems=(), start=None, finish=None):
        self.arrays, self.out_shapes, self.aliased = list(arrays), list(out_shapes), aliased
        self.sems, self.start, self.finish = list(sems), start, finish

    def __bool__(self):
        return bool(self.arrays)

    def call_args(self, n_in, n_out):
        aliases = {n_in + k: n_out + k for k in range(len(self.arrays))} if self.aliased else {}
        return _hbm_specs(len(self.arrays)), _hbm_specs(len(self.out_shapes)), self.out_shapes, aliases, self.sems

    def run(self, refs, n_in, n_out, step, n_steps, compute):
        if not self:
            compute()
            return
        n_c, n_o, n_s = len(self.arrays), len(self.out_shapes), len(self.sems)
        ins = refs[n_in:n_in + n_c]
        outs = refs[n_in + n_c + n_out:n_in + n_c + n_out + n_o]
        sems = refs[len(refs) - n_s:]

        @pl.when(step == 0)
        def _():
            self.start(ins, outs, sems)

        compute()

        @pl.when(step == n_steps - 1)
        def _():
            self.finish(ins, outs, sems)


def _mm(a, b, *, ta=False, tb=False, add=None, out_dtype=F32, tm, tn, tk, name, carry=None, epilogue=None,
        b_cols=None):
    carry = carry or _Carry()
    n_car = len(carry.arrays)
    if ta:
        kdim, m = a.shape
    else:
        m, kdim = a.shape
    if tb:
        n, k2 = b.shape
    else:
        k2, n = b.shape
    col0 = 0
    if b_cols is not None:
        assert b_cols[0] % tn == 0, name
        col0, n = b_cols[0] // tn, b_cols[1]
    assert kdim == k2 and m % tm == 0 and n % tn == 0 and kdim % tk == 0, (name, a.shape, b.shape)
    nk = kdim // tk
    grid = (m // tm, n // tn, nk)
    a_mode = dict(pipeline_mode=pl.Buffered(1)) if grid[0] == 1 and nk == 1 else {}
    b_mode = dict(pipeline_mode=pl.Buffered(1)) if grid[1] == 1 and nk == 1 else {}
    a_spec = (pl.BlockSpec((tk, tm), lambda i, j, k: (k, i), **a_mode) if ta
              else pl.BlockSpec((tm, tk), lambda i, j, k: (i, k), **a_mode))
    b_spec = (pl.BlockSpec((tn, tk), lambda i, j, k: (j + col0, k), **b_mode) if tb
              else pl.BlockSpec((tk, tn), lambda i, j, k: (k, j + col0), **b_mode))
    o_spec = pl.BlockSpec((tm, tn), lambda i, j, k: (i, j))
    dims = ((((0,) if ta else (1,)), ((1,) if tb else (0,))), ((), ()))
    has_add = add is not None
    if epilogue is None:
        ep_fn, ep_rows, ep_params, ep_outs, ep_accs = None, [], [], [], []
        out_sds, out_specs = [jax.ShapeDtypeStruct((m, n), out_dtype)], [o_spec]
    else:
        ep_fn, ep_rows, ep_params, ep_outs, ep_accs = epilogue
        assert grid[1] == 1 or not ep_accs, name
        out_sds = [jax.ShapeDtypeStruct((m, w * grid[1]), d) for w, d in ep_outs]
        out_sds += [jax.ShapeDtypeStruct(sh, F32) for sh in ep_accs]
        out_specs = [pl.BlockSpec((tm, w), lambda i, j, k: (i, j)) for w, _ in ep_outs]
        out_specs += [pl.BlockSpec(sh, lambda i, j, k: (0, 0)) for sh in ep_accs]
    n_main = len(out_sds)
    use_scratch = nk > 1 and (ep_fn is not None or jnp.dtype(out_dtype) != jnp.dtype(F32))
    n_in = 2 + has_add + len(ep_rows) + len(ep_params)

    def finish(total, refs, pid):
        outs = refs[n_in + n_car:n_in + n_car + n_main]
        if ep_fn is None:
            outs[0][...] = total.astype(out_dtype)
            return
        first = 2 + has_add
        rows = [r[...].astype(F32) for r in refs[first:first + len(ep_rows)]]
        params = [p[...] for p in refs[first + len(ep_rows):n_in]]
        res = ep_fn(total, *rows, *params)
        for o_ref, v in zip(outs[:len(ep_outs)], res):
            o_ref[...] = v.astype(o_ref.dtype)
        acc_refs = outs[len(ep_outs):]
        if acc_refs:
            @pl.when(pid[0] == 0)
            def _():
                for r in acc_refs:
                    r[...] = jnp.zeros(r.shape, F32)

            for r, v in zip(acc_refs, res[len(ep_outs):]):
                r[...] += v

    def compute(refs, pid):
        a_ref, b_ref = refs[0], refs[1]
        add_ref = refs[2] if has_add else None
        prod = lax.dot_general(a_ref[...].astype(BF16), b_ref[...].astype(BF16), dims, preferred_element_type=F32)
        if nk == 1:
            finish(prod + add_ref[...] if has_add else prod, refs, pid)
            return
        acc_ref = refs[n_in + n_car + n_main + len(carry.out_shapes)] if use_scratch else refs[n_in + n_car]
        k = pid[2]

        @pl.when(k == 0)
        def _():
            acc_ref[...] = prod + add_ref[...] if has_add else prod

        @pl.when(k > 0)
        def _():
            acc_ref[...] += prod

        if use_scratch:
            @pl.when(k == nk - 1)
            def _():
                finish(acc_ref[...], refs, pid)

    def body(*refs):
        pid = (pl.program_id(0), pl.program_id(1), pl.program_id(2))
        step = (pid[0] * grid[1] + pid[1]) * nk + pid[2]
        carry.run(refs, n_in, n_main, step, grid[0] * grid[1] * nk, lambda: compute(refs, pid))

    tile_bytes = (_nbytes((tm, tk), a.dtype) + _nbytes((tk, tn), b.dtype) + 2 * _nbytes((tm, tn), F32)
                  + (_nbytes((tm, tn), F32) if has_add else 0)
                  + sum(_nbytes((tm, r.shape[1]), r.dtype) for r in ep_rows) + sum(_nbytes((tm, w), d) for w, d in ep_outs))
    in_specs = [a_spec, b_spec] + ([o_spec] if has_add else [])
    in_specs += [pl.BlockSpec((tm, r.shape[1] // grid[1]), lambda i, j, k: (i, j)) for r in ep_rows]
    in_specs += [pl.BlockSpec(p.shape, lambda i, j, k: (0, 0)) for p in ep_params]
    args = (a, b) + ((add,) if has_add else ()) + tuple(ep_rows) + tuple(ep_params)
    scratch = [pltpu.VMEM((tm, tn), F32)] if use_scratch else []
    serial = bool(carry) or bool(ep_accs)
    c_in, c_out, c_shapes, c_alias, c_sems = carry.call_args(n_in, n_main)
    res = pl.pallas_call(
        body, name=name, grid=grid,
        in_specs=in_specs + c_in, out_specs=out_specs + c_out, out_shape=out_sds + c_shapes,
        input_output_aliases=c_alias, scratch_shapes=scratch + c_sems,
        compiler_params=pltpu.CompilerParams(
            dimension_semantics=("arbitrary",) * 3 if serial else ("parallel", "parallel", "arbitrary"),
            vmem_limit_bytes=_vmem_limit(tile_bytes)),
    )(*args, *carry.arrays)
    main = res[0] if ep_fn is None else list(res[:n_main])
    return (main, res[n_main:]) if carry else main


def _rowwise(fn, rows, params, outs, accs, *, tm, name):
    t = rows[0].shape[0]
    assert t % tm == 0, (name, t, tm)
    n_r, n_p, n_o = len(rows), len(params), len(outs)

    def body(*refs):
        vals = [r[...].astype(F32) for r in refs[:n_r]] + [p[...] for p in refs[n_r:n_r + n_p]]
        res = fn(*vals)
        o_refs = refs[n_r + n_p:n_r + n_p + n_o]
        a_refs = refs[n_r + n_p + n_o:]
        for o_ref, v in zip(o_refs, res[:n_o]):
            o_ref[...] = v.astype(o_ref.dtype)
        if accs:
            @pl.when(pl.program_id(0) == 0)
            def _():
                for a_ref in a_refs:
                    a_ref[...] = jnp.zeros(a_ref.shape, F32)

            for a_ref, v in zip(a_refs, res[n_o:]):
                a_ref[...] += v

    in_specs = [pl.BlockSpec((tm, r.shape[1]), lambda i: (i, 0)) for r in rows]
    in_specs += [pl.BlockSpec(p.shape, lambda i: (0, 0)) for p in params]
    out_specs = [pl.BlockSpec((tm, w), lambda i: (i, 0)) for w, _ in outs]
    out_specs += [pl.BlockSpec(s, lambda i: (0, 0)) for s in accs]
    out_shape = [jax.ShapeDtypeStruct((t, w), d) for w, d in outs]
    out_shape += [jax.ShapeDtypeStruct(s, F32) for s in accs]
    tile_bytes = sum(_nbytes((tm, r.shape[1]), r.dtype) for r in rows) + sum(_nbytes((tm, w), F32) for w, _ in outs)
    res = pl.pallas_call(
        body, name=name, grid=(t // tm,), in_specs=in_specs, out_specs=out_specs, out_shape=out_shape,
        compiler_params=pltpu.CompilerParams(
            dimension_semantics=("arbitrary",) if accs else ("parallel",),
            vmem_limit_bytes=_vmem_limit(2 * tile_bytes)),
    )(*rows, *params)
    return res


def _rms_stats(x):
    r = lax.rsqrt(jnp.mean(x * x, axis=-1, keepdims=True) + RMS_EPS)
    return x * r, r


def _rms_bwd(dy, xhat, r, g):
    dxhat = dy * g
    dx = r * (dxhat - xhat * jnp.mean(dxhat * xhat, axis=-1, keepdims=True))
    return dx, dy * xhat


def _sb_consts():
    lane = lax.broadcasted_iota(jnp.int32, (BLK, LANES), 1)
    head0 = lane < HEAD_DIM
    row = lax.broadcasted_iota(jnp.int32, (2 * BLK, BLK), 0) % BLK
    col = lax.broadcasted_iota(jnp.int32, (2 * BLK, BLK), 1)
    causal = col < row
    jj = lax.broadcasted_iota(jnp.int32, (BLK, BLK), 0)
    ss = lax.broadcasted_iota(jnp.int32, (BLK, BLK), 1)
    suffix = jnp.where(jj > ss, 1.0, 0.0).astype(BF16)
    return head0, causal, suffix


def _stack_heads(x, head0):
    zero = jnp.zeros_like(x)
    return jnp.concatenate([jnp.where(head0, x, zero), jnp.where(head0, zero, x)], axis=0)


def _sb_logits(z, causal, masked):
    sp = jnp.log(1.0 + jnp.exp(-jnp.abs(z)))
    log_keep = -(jnp.maximum(z, 0.0) + sp)
    log_beta = jnp.minimum(z, 0.0) - sp
    if masked:
        log_keep = jnp.where(causal, log_keep, 0.0)
    return log_keep, log_beta


def _suffix_sums(x, suffix):
    hi, lo = _split2(x)
    after = _dot(hi, suffix) + _dot(lo, suffix)
    total = jnp.broadcast_to(after[:, 0:1] + x[:, 0:1], x.shape)
    return after, total


def _sb_walk_back(i, state, per_chain, tile):
    def alive(st):
        worst = functools.reduce(jnp.maximum, [st[p][:, 0:1] for p in range(0, len(st), per_chain)])
        return jnp.max(worst) > EXP_UNDERFLOW

    def cond(c):
        return jnp.logical_and(c[0] < i, alive(c[1]))

    def body(c):
        return c[0] + 1, tile(i - 1 - c[0], c[1], False)

    return lax.while_loop(cond, body, (jnp.int32(0), state))[1]


def _lane_blocks(x, n):
    return [x[:, p * LANES:(p + 1) * LANES] for p in range(n)]


def _sb_fwd(qkv, b_sz, s_len, carry):
    nq = s_len // BLK
    n_pairs = SB_WIDTH // LANES
    ch = SB_FWD_CHAINS
    n_steps = n_pairs // ch
    scale = 1.0 / math.sqrt(HEAD_DIM)

    def compute(q_ref, k_ref, v_ref, o_ref):
        head0, causal, suffix = _sb_consts()

        def q_block(i, _):
            qs = pl.multiple_of(i * BLK, BLK)
            q_all = (q_ref[pl.ds(qs, BLK), :] * scale).astype(BF16)
            q01 = [_stack_heads(q, head0) for q in _lane_blocks(q_all, ch)]

            def tile(j, state, masked):
                ks = pl.multiple_of(j * BLK, BLK)
                ks_ = _lane_blocks(k_ref[pl.ds(ks, BLK), :].astype(BF16), ch)
                vs_ = _lane_blocks(v_ref[pl.ds(ks, BLK), :].astype(BF16), ch)
                zs = [_dot_nt(q01[p], ks_[p]) for p in range(ch)]
                logits = [_sb_logits(z, causal, masked) for z in zs]
                sums = [_suffix_sums(lg[0], suffix) for lg in logits]
                out = []
                for p in range(ch):
                    carry, acc = state[2 * p], state[2 * p + 1]
                    after, total = sums[p]
                    a = jnp.exp(logits[p][1] + carry + after)
                    if masked:
                        a = jnp.where(causal, a, 0.0)
                    a_hi, a_lo = _split2(a)
                    a_cat = jnp.concatenate([a_hi[:BLK], a_hi[BLK:], a_lo[:BLK], a_lo[BLK:]], axis=1)
                    v01 = _stack_heads(vs_[p], head0)
                    out += [carry + total, acc + _dot(a_cat, jnp.concatenate([v01, v01], axis=0))]
                return tuple(out)

            state = (jnp.zeros((2 * BLK, BLK), F32), jnp.zeros((BLK, LANES), F32)) * ch
            state = tile(i, state, True)
            state = _sb_walk_back(i, state, 2, tile)
            o_ref[pl.ds(qs, BLK), :] = jnp.concatenate([state[2 * p + 1] for p in range(ch)], axis=1)
            return 0

        lax.fori_loop(0, nq, q_block, 0)

    def body(*refs):
        step = pl.program_id(0) * n_steps + pl.program_id(1)
        o_ref = refs[3 + len(carry.arrays)]
        carry.run(refs, 3, 1, step, b_sz * n_steps, lambda: compute(refs[0], refs[1], refs[2], o_ref))

    blk = lambda off: pl.BlockSpec((None, s_len, ch * LANES), lambda b, p: (b, 0, off + p))
    c_in, c_out, c_shapes, c_alias, c_sems = carry.call_args(3, 1)
    res = pl.pallas_call(
        body, name="sb_fwd", grid=(b_sz, n_steps),
        in_specs=[blk(0), blk(n_steps), blk(2 * n_steps)] + c_in, out_specs=[blk(0)] + c_out,
        out_shape=[jax.ShapeDtypeStruct((b_sz, s_len, SB_WIDTH), F32)] + c_shapes,
        input_output_aliases=c_alias, scratch_shapes=c_sems,
        compiler_params=pltpu.CompilerParams(dimension_semantics=("arbitrary", "arbitrary"),
                                             vmem_limit_bytes=VMEM_CAP),
    )(qkv, qkv, qkv, *carry.arrays)
    return res[0], res[1:]


def _sb_bwd(qkv, o_sb, do_sb, b_sz, s_len, carry):
    nq = s_len // BLK
    n_pairs = SB_WIDTH // LANES
    ch = SB_BWD_CHAINS
    n_steps = n_pairs // ch
    scale = 1.0 / math.sqrt(HEAD_DIM)

    def compute(q_ref, k_ref, v_ref, o_ref, do_ref, dq_ref, dk_ref, dv_ref, dk_acc, dv_acc):
        head0, causal, suffix = _sb_consts()
        lrow = lax.broadcasted_iota(jnp.int32, (LANES, LANES), 0)
        ones_h0 = jnp.where(lrow < HEAD_DIM, 1.0, 0.0).astype(BF16)
        ones_h1 = jnp.where(lrow >= HEAD_DIM, 1.0, 0.0).astype(BF16)
        dk_acc[...] = jnp.zeros(dk_acc.shape, F32)
        dv_acc[...] = jnp.zeros(dv_acc.shape, F32)

        def q_block(i, _):
            qs = pl.multiple_of(i * BLK, BLK)
            q_all = (q_ref[pl.ds(qs, BLK), :] * scale).astype(BF16)
            do_all = do_ref[pl.ds(qs, BLK), :].astype(BF16)
            dd_all = do_all.astype(F32) * o_ref[pl.ds(qs, BLK), :]
            q01 = [_stack_heads(q, head0) for q in _lane_blocks(q_all, ch)]
            do01 = [_stack_heads(d, head0) for d in _lane_blocks(do_all, ch)]
            tot = []
            for dd in _lane_blocks(dd_all, ch):
                dd_hi, dd_lo = _split2(dd)
                tot.append(jnp.concatenate([_dot(dd_hi, ones_h0) + _dot(dd_lo, ones_h0),
                                            _dot(dd_hi, ones_h1) + _dot(dd_lo, ones_h1)], axis=0))

            def tile(j, state, masked):
                ks = pl.multiple_of(j * BLK, BLK)
                ks_ = _lane_blocks(k_ref[pl.ds(ks, BLK), :].astype(BF16), ch)
                vs_ = _lane_blocks(v_ref[pl.ds(ks, BLK), :].astype(BF16), ch)
                zs = [_dot_nt(q01[p], ks_[p]) for p in range(ch)]
                das = [_dot_nt(do01[p], vs_[p]) for p in range(ch)]
                logits = [_sb_logits(z, causal, masked) for z in zs]
                sums = [_suffix_sums(lg[0], suffix) for lg in logits]
                a_s, e_s = [], []
                for p in range(ch):
                    a = jnp.exp(logits[p][1] + state[3 * p] + sums[p][0])
                    if masked:
                        a = jnp.where(causal, a, 0.0)
                    a_s.append(a)
                    e_s.append(a * das[p])
                e_sums = [_suffix_sums(e, suffix) for e in e_s]
                out, dks, dvs = [], [], []
                for p in range(ch):
                    carry, rcarry, dq = state[3 * p:3 * p + 3]
                    e = e_s[p]
                    before = tot[p] - (rcarry + e_sums[p][0] + e)
                    beta = jnp.exp(logits[p][1])
                    dz = e * (1.0 - beta) - beta * before
                    if masked:
                        dz = jnp.where(causal, dz, 0.0)
                    dz_b = dz.astype(BF16)
                    dks.append(_dot_tn(dz_b, q01[p]))
                    dvs.append(_dot_tn(a_s[p].astype(BF16), do01[p]))
                    out += [carry + sums[p][1], rcarry + e_sums[p][1], dq + _dot(dz_b, ks_[p])]
                dk_acc[pl.ds(ks, BLK), :] += jnp.concatenate(dks, axis=1)
                dv_acc[pl.ds(ks, BLK), :] += jnp.concatenate(dvs, axis=1)
                return tuple(out)

            state = (jnp.zeros((2 * BLK, BLK), F32),) * (3 * ch)
            state = tile(i, state, True)
            state = _sb_walk_back(i, state, 3, tile)
            dq = [jnp.where(head0, state[3 * p + 2][:BLK], state[3 * p + 2][BLK:]) for p in range(ch)]
            dq_ref[pl.ds(qs, BLK), :] = (jnp.concatenate(dq, axis=1) * scale).astype(dq_ref.dtype)
            return 0

        lax.fori_loop(0, nq, q_block, 0)
        dk_ref[...] = dk_acc[...].astype(dk_ref.dtype)
        dv_ref[...] = dv_acc[...].astype(dv_ref.dtype)

    def body(*refs):
        step = pl.program_id(0) * n_steps + pl.program_id(1)
        n_c, n_o = len(carry.arrays), len(carry.out_shapes)
        own = refs[:5] + refs[5 + n_c:8 + n_c] + refs[8 + n_c + n_o:10 + n_c + n_o]
        carry.run(refs, 5, 3, step, b_sz * n_steps, lambda: compute(*own))

    blk = lambda off: pl.BlockSpec((None, s_len, ch * LANES), lambda b, p: (b, 0, off + p))
    once = lambda off: pl.BlockSpec((None, s_len, ch * LANES), lambda b, p: (b, 0, off + p),
                                    pipeline_mode=pl.Buffered(1))
    out_sd = jax.ShapeDtypeStruct((b_sz, s_len, SB_WIDTH), BF16)
    c_in, c_out, c_shapes, c_alias, c_sems = carry.call_args(5, 3)
    res = pl.pallas_call(
        body, name="sb_bwd", grid=(b_sz, n_steps),
        in_specs=[once(0), once(n_steps), once(2 * n_steps), once(0), once(0)] + c_in,
        out_specs=[blk(0), blk(0), blk(0)] + c_out, out_shape=[out_sd, out_sd, out_sd] + c_shapes,
        input_output_aliases=c_alias,
        scratch_shapes=[pltpu.VMEM((s_len, ch * LANES), F32), pltpu.VMEM((s_len, ch * LANES), F32)] + c_sems,
        compiler_params=pltpu.CompilerParams(dimension_semantics=("arbitrary", "arbitrary"),
                                             vmem_limit_bytes=VMEM_CAP),
    )(qkv, qkv, qkv, o_sb, do_sb, *carry.arrays)
    return res[:3], res[3:]


def _dil_consts(group, pair_idx, dilation):
    lane = lax.broadcasted_iota(jnp.int32, (BLK, LANES), 1)
    head0 = lane < HEAD_DIM
    row = lax.broadcasted_iota(jnp.int32, (2 * BLK, BLK), 0)
    qa = row % BLK
    kb = lax.broadcasted_iota(jnp.int32, (2 * BLK, BLK), 1)
    head = (group * DIL_HEADS_PER_GROUP + 2 * pair_idx + row // BLK).astype(F32)
    slope = jnp.exp((-ALIBI_MAX_BIAS * math.log(2.0) / DIL_HEADS) * (head + 1.0))
    valid_cur = kb <= qa
    valid_prev = kb >= qa
    bias_cur = -slope * ((qa - kb) * dilation).astype(F32)
    bias_prev = -slope * ((BLK + qa - kb) * dilation).astype(F32)
    return head0, valid_cur, valid_prev, bias_cur, bias_prev


def _dil_units(s_len, dilation):
    nb = s_len // dilation // BLK
    return [(r, n) for r in range(dilation) for n in range(nb)]


def _dil_rows(n, r, dilation):
    if dilation == 1:
        return pl.ds(n * BLK, BLK)
    return pl.ds(n * BLK * dilation + r, BLK, stride=dilation)


def _dil_scores(q01, k, bias, valid):
    s = _dot_nt(q01, k) * (1.0 / math.sqrt(HEAD_DIM)) + bias
    return jnp.where(valid, s, NEG)


def _dil_fwd(qkv, b_sz, s_len, carry):
    n_pairs = DIL_OUT_WIDTH // LANES
    q_off = 3 * SB_WIDTH // LANES
    per_kind = DIL_WIDTH // LANES

    def compute(pair_idx, qkv_refs, o_ref, lse_ref, m_s, l_s):
        m_s[...] = jnp.full(m_s.shape, NEG, F32)
        l_s[...] = jnp.zeros(l_s.shape, F32)
        o_ref[...] = jnp.zeros(o_ref.shape, F32)
        for g, (_, dilation) in enumerate(DIL_PAIRS):
            q_ref, k_ref, v_ref = qkv_refs[3 * g:3 * g + 3]
            head0, valid_cur, valid_prev, bias_cur, bias_prev = _dil_consts(g, pair_idx, dilation)
            units = _dil_units(s_len, dilation)
            for u0 in range(0, len(units), DIL_CHAINS):
                group = units[u0:u0 + DIL_CHAINS]
                rows_of = [_dil_rows(n, r, dilation) for r, n in group]
                scores, values = [], []
                for (r, n), rows in zip(group, rows_of):
                    q01 = _stack_heads(q_ref[rows, :].astype(BF16), head0)
                    sc = [_dil_scores(q01, k_ref[rows, :].astype(BF16), bias_cur, valid_cur)]
                    vals = [_stack_heads(v_ref[rows, :].astype(BF16), head0)]
                    if n > 0:
                        prev = _dil_rows(n - 1, r, dilation)
                        sc.append(_dil_scores(q01, k_ref[prev, :].astype(BF16), bias_prev, valid_prev))
                        vals.append(_stack_heads(v_ref[prev, :].astype(BF16), head0))
                    scores.append(sc)
                    values.append(vals)
                stats = []
                for sc, rows in zip(scores, rows_of):
                    m_blk = functools.reduce(jnp.maximum, [jnp.max(x, axis=-1, keepdims=True) for x in sc])
                    m_old = jnp.concatenate([m_s.at[0][rows, :], m_s.at[1][rows, :]], axis=0)
                    l_old = jnp.concatenate([l_s.at[0][rows, :], l_s.at[1][rows, :]], axis=0)
                    m_new = jnp.maximum(m_old, m_blk)
                    probs = [jnp.exp(x - m_new) for x in sc]
                    l_blk = functools.reduce(jnp.add, [jnp.sum(p, axis=-1, keepdims=True) for p in probs])
                    alpha = jnp.exp(m_old - m_new)
                    stats.append((m_new, alpha * l_old + l_blk, alpha, probs))
                for (m_new, l_new, alpha, probs), vals, rows in zip(stats, values, rows_of):
                    alpha_tok = jnp.where(head0, alpha[:BLK], alpha[BLK:])
                    p_cat = jnp.concatenate(
                        [h for p in probs for h in (p[:BLK].astype(BF16), p[BLK:].astype(BF16))], axis=1)
                    o_ref[rows, :] = alpha_tok * o_ref[rows, :] + _dot(p_cat, jnp.concatenate(vals, axis=0))
                    m_s.at[0][rows, :] = m_new[:BLK]
                    m_s.at[1][rows, :] = m_new[BLK:]
                    l_s.at[0][rows, :] = l_new[:BLK]
                    l_s.at[1][rows, :] = l_new[BLK:]
        lane = lax.broadcasted_iota(jnp.int32, (BLK, LANES), 1)
        for c in range(s_len // BLK):
            rows = pl.ds(c * BLK, BLK)
            l0, l1 = l_s.at[0][rows, :], l_s.at[1][rows, :]
            o_ref[rows, :] = o_ref[rows, :] / jnp.where(lane < HEAD_DIM, l0, l1)
            lse_ref.at[0][rows, :] = m_s.at[0][rows, :] + jnp.log(l0)
            lse_ref.at[1][rows, :] = m_s.at[1][rows, :] + jnp.log(l1)

    def body(*refs):
        pair_idx = pl.program_id(1)
        step = pl.program_id(0) * n_pairs + pair_idx
        n_c, n_o = len(carry.arrays), len(carry.out_shapes)
        o_ref, lse_ref = refs[9 + n_c:11 + n_c]
        m_s, l_s = refs[11 + n_c + n_o:13 + n_c + n_o]
        carry.run(refs, 9, 2, step, b_sz * n_pairs, lambda: compute(pair_idx, refs[:9], o_ref, lse_ref, m_s, l_s))

    in_specs = []
    for g in range(len(DIL_PAIRS)):
        for kind in range(3):
            off = q_off + kind * per_kind + g * n_pairs
            in_specs.append(pl.BlockSpec((None, s_len, LANES), lambda b, p, off=off: (b, 0, off + p)))
    c_in, c_out, c_shapes, c_alias, c_sems = carry.call_args(9, 2)
    res = pl.pallas_call(
        body, name="dil_fwd", grid=(b_sz, n_pairs),
        in_specs=in_specs + c_in,
        out_specs=[pl.BlockSpec((None, s_len, LANES), lambda b, p: (b, 0, p)),
                   pl.BlockSpec((None, None, 2, s_len, LANES), lambda b, p: (b, p, 0, 0, 0))] + c_out,
        out_shape=[jax.ShapeDtypeStruct((b_sz, s_len, DIL_OUT_WIDTH), F32),
                   jax.ShapeDtypeStruct((b_sz, n_pairs, 2, s_len, LANES), F32)] + c_shapes,
        input_output_aliases=c_alias,
        scratch_shapes=[pltpu.VMEM((2, s_len, LANES), F32), pltpu.VMEM((2, s_len, LANES), F32)] + c_sems,
        compiler_params=pltpu.CompilerParams(dimension_semantics=("arbitrary", "arbitrary"),
                                             vmem_limit_bytes=VMEM_CAP),
    )(*([qkv] * 9), *carry.arrays)
    return res[0], res[1], res[2:]


def _dil_bwd(qkv, o_dl, lse, do_dl, b_sz, s_len, carry):
    n_pairs = DIL_OUT_WIDTH // LANES
    n_groups = len(DIL_PAIRS)
    q_off = 3 * SB_WIDTH // LANES
    per_kind = DIL_WIDTH // LANES

    def compute(pair_idx, group, q_ref, k_ref, v_ref, o_ref, lse_ref, do_ref, dq_ref, dk_ref, dv_ref, d_s, dq_s, dk_s, dv_s):
        lrow = lax.broadcasted_iota(jnp.int32, (LANES, LANES), 0)
        ones_h0 = jnp.where(lrow < HEAD_DIM, 1.0, 0.0).astype(BF16)
        ones_h1 = jnp.where(lrow >= HEAD_DIM, 1.0, 0.0).astype(BF16)
        for c in range(s_len // BLK):
            rows = pl.ds(c * BLK, BLK)
            dd_hi, dd_lo = _split2(do_ref[rows, :] * o_ref[rows, :])
            d_s.at[0][rows, :] = _dot(dd_hi, ones_h0) + _dot(dd_lo, ones_h0)
            d_s.at[1][rows, :] = _dot(dd_hi, ones_h1) + _dot(dd_lo, ones_h1)
        dk_s[...] = jnp.zeros(dk_s.shape, F32)
        dv_s[...] = jnp.zeros(dv_s.shape, F32)

        def one_group(g, dilation):
            head0, valid_cur, valid_prev, bias_cur, bias_prev = _dil_consts(g, pair_idx, dilation)
            units = _dil_units(s_len, dilation)
            scale = 1.0 / math.sqrt(HEAD_DIM)
            for u0 in range(0, len(units), DIL_CHAINS):
                chunk = units[u0:u0 + DIL_CHAINS]
                loaded = []
                for r, n in chunk:
                    rows = _dil_rows(n, r, dilation)
                    q01 = _stack_heads(q_ref[rows, :].astype(BF16), head0)
                    do01 = _stack_heads(do_ref[rows, :].astype(BF16), head0)
                    lse01 = jnp.concatenate([lse_ref.at[0][rows, :], lse_ref.at[1][rows, :]], axis=0)
                    d01 = jnp.concatenate([d_s.at[0][rows, :], d_s.at[1][rows, :]], axis=0)
                    blocks = [(rows, bias_cur, valid_cur)]
                    if n > 0:
                        blocks.append((_dil_rows(n - 1, r, dilation), bias_prev, valid_prev))
                    parts = []
                    for krows, bias, valid in blocks:
                        k = k_ref[krows, :].astype(BF16)
                        v = v_ref[krows, :].astype(BF16)
                        parts.append((krows, k, _dil_scores(q01, k, bias, valid), _dot_nt(do01, v)))
                    loaded.append((rows, q01, do01, lse01, d01, parts))
                grads = []
                for rows, q01, do01, lse01, d01, parts in loaded:
                    for krows, k, sc, dp in parts:
                        p = jnp.exp(sc - lse01)
                        grads.append((p.astype(BF16), (p * (dp - d01) * scale).astype(BF16)))
                it = iter(grads)
                updates = []
                for rows, q01, do01, lse01, d01, parts in loaded:
                    dq = jnp.zeros((2 * BLK, LANES), F32)
                    for krows, k, sc, dp in parts:
                        p_b, ds = next(it)
                        dq = dq + _dot(ds, k)
                        updates.append((krows, _dot_tn(ds, q01), _dot_tn(p_b, do01)))
                    dq_s[rows, :] = jnp.where(head0, dq[:BLK], dq[BLK:])
                for krows, dk, dv in updates:
                    dk_s[krows, :] = dk_s[krows, :] + dk
                    dv_s[krows, :] = dv_s[krows, :] + dv

        for g, (_, dilation) in enumerate(DIL_PAIRS):
            pl.when(group == g)(functools.partial(one_group, g, dilation))
        dq_ref[...] = dq_s[...].astype(dq_ref.dtype)
        dk_ref[...] = dk_s[...].astype(dk_ref.dtype)
        dv_ref[...] = dv_s[...].astype(dv_ref.dtype)

    def body(*refs):
        pair_idx, group = pl.program_id(1), pl.program_id(2)
        step = (pl.program_id(0) * n_pairs + pair_idx) * n_groups + group
        n_c, n_o = len(carry.arrays), len(carry.out_shapes)
        own = refs[:6] + refs[6 + n_c:9 + n_c] + refs[9 + n_c + n_o:13 + n_c + n_o]
        carry.run(refs, 6, 3, step, b_sz * n_pairs * n_groups, lambda: compute(pair_idx, group, *own))

    def qkv_spec(kind):
        return pl.BlockSpec((None, s_len, LANES),
                            lambda b, p, g: (b, 0, q_off + kind * per_kind + g * n_pairs + p))

    tok_spec = pl.BlockSpec((None, s_len, LANES), lambda b, p, g: (b, 0, p))
    out_spec = pl.BlockSpec((None, s_len, LANES), lambda b, p, g: (b, 0, g * n_pairs + p))
    out_sd = jax.ShapeDtypeStruct((b_sz, s_len, DIL_WIDTH), BF16)
    c_in, c_out, c_shapes, c_alias, c_sems = carry.call_args(6, 3)
    res = pl.pallas_call(
        body, name="dil_bwd", grid=(b_sz, n_pairs, n_groups),
        in_specs=[qkv_spec(0), qkv_spec(1), qkv_spec(2), tok_spec,
                  pl.BlockSpec((None, None, 2, s_len, LANES), lambda b, p, g: (b, p, 0, 0, 0)), tok_spec] + c_in,
        out_specs=[out_spec, out_spec, out_spec] + c_out,
        out_shape=[out_sd, out_sd, out_sd] + c_shapes,
        input_output_aliases=c_alias,
        scratch_shapes=[pltpu.VMEM((2, s_len, LANES), F32)] + [pltpu.VMEM((s_len, LANES), F32)] * 3 + c_sems,
        compiler_params=pltpu.CompilerParams(dimension_semantics=("arbitrary", "arbitrary", "arbitrary"),
                                             vmem_limit_bytes=VMEM_CAP),
    )(qkv, qkv, qkv, o_dl, lse, do_dl, *carry.arrays)
    return res[:3], res[3:]


def _mesh_pos():
    return lax.axis_index("x"), lax.axis_index("y"), lax.axis_index("c")


def _other_chips(x, y):
    return [(1 - x, y), (x, 1 - y), (1 - x, 1 - y)]


def _hbm_specs(n):
    return [pl.BlockSpec(memory_space=pl.ANY)] * n


SWAPPED = ("w_ffn_in",)


def _slot(x, y, swapped):
    return 2 * y + x if swapped else 2 * x + y


def _cast_to_slab(w, name, swapped=False):
    rows, cols = w.shape
    mine = jnp.reshape(_slot(lax.axis_index("x"), lax.axis_index("y"), swapped), (1,)).astype(jnp.int32)

    def body(idx_ref, w_ref, o_ref):
        o_ref[...] = w_ref[...].astype(BF16)

    return pl.pallas_call(
        body, name=name,
        grid_spec=pltpu.PrefetchScalarGridSpec(
            num_scalar_prefetch=1, grid=(1,),
            in_specs=[pl.BlockSpec((rows, cols), lambda i, idx: (0, 0))],
            out_specs=pl.BlockSpec((None, rows, cols), lambda i, idx: (idx[0], 0, 0))),
        out_shape=jax.ShapeDtypeStruct((N_CHIPS, rows, cols), BF16),
        compiler_params=pltpu.CompilerParams(vmem_limit_bytes=_vmem_limit(rows * cols * 6)),
    )(mine, w)


def _gather_issue(slabs, send_sems, recv_sems, swapped):
    x, y, c = _mesh_pos()
    for k, slab in enumerate(slabs):
        half = slab.shape[1] // 2
        rows = slab.at[_slot(x, y, swapped[k]), pl.ds(c * half, half), :]
        for r, (px, py) in enumerate(_other_chips(x, y)):
            pltpu.make_async_remote_copy(
                src_ref=rows, dst_ref=rows, send_sem=send_sems.at[6 * k + r], recv_sem=recv_sems.at[6 * k + r],
                device_id=(px, py, c), device_id_type=MESH).start()


def _gather_complete(slabs, send_sems, recv_sems, swapped):
    x, y, c = _mesh_pos()
    chips = _other_chips(x, y)

    def copy(k, sem, block, rows, to):
        ref = slabs[k].at[block, rows, :]
        return pltpu.make_async_remote_copy(
            src_ref=ref, dst_ref=ref, send_sem=send_sems.at[sem], recv_sem=recv_sems.at[sem],
            device_id=to, device_id_type=MESH)

    for k, slab in enumerate(slabs):
        half = slab.shape[1] // 2
        for r, (px, py) in enumerate(chips):
            theirs = _slot(px, py, swapped[k])
            copy(k, 6 * k + r, theirs, pl.ds(c * half, half), (px, py, c)).wait_recv()
            copy(k, 6 * k + 3 + r, theirs, pl.ds(c * half, half), (x, y, 1 - c)).start()
    for k, slab in enumerate(slabs):
        half = slab.shape[1] // 2
        for r, (px, py) in enumerate(chips):
            copy(k, 6 * k + 3 + r, _slot(px, py, swapped[k]), pl.ds((1 - c) * half, half), (x, y, 1 - c)).wait_recv()
    for k, slab in enumerate(slabs):
        half = slab.shape[1] // 2
        for r, (px, py) in enumerate(chips):
            copy(k, 6 * k + r, _slot(x, y, swapped[k]), pl.ds(c * half, half), (px, py, c)).wait_send()
            copy(k, 6 * k + 3 + r, _slot(px, py, swapped[k]), pl.ds(c * half, half), (x, y, 1 - c)).wait_send()


def _gather_sems(n):
    return [pltpu.SemaphoreType.DMA((6 * n,)), pltpu.SemaphoreType.DMA((6 * n,))]


def _gather_carry(slabs, names):
    swapped = [k in SWAPPED for k in names]
    return _Carry(slabs, [jax.ShapeDtypeStruct(a.shape, a.dtype) for a in slabs], True, _gather_sems(len(slabs)),
                  lambda ins, outs, sems: _gather_issue(outs, *sems, swapped),
                  lambda ins, outs, sems: _gather_complete(outs, *sems, swapped))


def _gather_weights(slabs):
    n = len(slabs)

    def body(*refs):
        outs = refs[n:2 * n]
        send_sems, recv_sems = refs[2 * n:]
        _gather_issue(outs, send_sems, recv_sems, [False] * n)
        _gather_complete(outs, send_sems, recv_sems, [False] * n)

    return pl.pallas_call(
        body, name="gather_weights",
        in_specs=_hbm_specs(n), out_specs=_hbm_specs(n),
        out_shape=[jax.ShapeDtypeStruct(s.shape, s.dtype) for s in slabs],
        input_output_aliases={k: k for k in range(n)},
        scratch_shapes=_gather_sems(n),
    )(*slabs)


def _pair_exchange(grads, tag):
    n = len(grads)

    def body(*refs):
        ins, outs = refs[:n], refs[n:2 * n]
        send_sems, recv_sems = refs[2 * n:]
        x, y, c = _mesh_pos()
        copies = []
        for k in range(n):
            half = grads[k].shape[1] // 2
            cp = pltpu.make_async_remote_copy(
                src_ref=ins[k].at[:, pl.ds((1 - c) * half, half), :], dst_ref=outs[k],
                send_sem=send_sems.at[k], recv_sem=recv_sems.at[k],
                device_id=(x, y, 1 - c), device_id_type=MESH)
            cp.start()
            copies.append(cp)
        for cp in copies:
            cp.wait()

    return pl.pallas_call(
        body, name="grad_pair_exchange_" + tag,
        in_specs=_hbm_specs(n), out_specs=_hbm_specs(n),
        out_shape=[jax.ShapeDtypeStruct((N_CHIPS, g.shape[1] // 2, g.shape[2]), F32) for g in grads],
        scratch_shapes=[pltpu.SemaphoreType.DMA((n,)), pltpu.SemaphoreType.DMA((n,))],
    )(*grads)


def _pair_sum(grad, other, name):
    _, rows, cols = grad.shape
    half = rows // 2
    core = jnp.reshape(lax.axis_index("c"), (1,)).astype(jnp.int32)

    def body(core_ref, g_ref, p_ref, s_ref, sb_ref):
        s = g_ref[...] + p_ref[...]
        s_ref[...] = s
        sb_ref[...] = s.astype(BF16)

    blk = pl.BlockSpec((None, half, cols), lambda p, core_ref: (p, 0, 0))
    return pl.pallas_call(
        body, name=name,
        grid_spec=pltpu.PrefetchScalarGridSpec(
            num_scalar_prefetch=1, grid=(N_CHIPS,),
            in_specs=[pl.BlockSpec((None, half, cols), lambda p, core_ref: (p, core_ref[0], 0)), blk],
            out_specs=[blk, blk]),
        out_shape=[jax.ShapeDtypeStruct((N_CHIPS, half, cols), F32),
                   jax.ShapeDtypeStruct((N_CHIPS, half, cols), BF16)],
        compiler_params=pltpu.CompilerParams(dimension_semantics=("parallel",),
                                             vmem_limit_bytes=_vmem_limit(4 * half * cols * 4)),
    )(core, grad, other)


def _chip_copies(sums_bf16, lands, send_sems, recv_sems, swapped):
    x, y, c = _mesh_pos()
    return [pltpu.make_async_remote_copy(
        src_ref=sums_bf16[k].at[_slot(px, py, swapped[k])], dst_ref=lands[k].at[r],
        send_sem=send_sems.at[3 * k + r], recv_sem=recv_sems.at[3 * k + r],
        device_id=(px, py, c), device_id_type=MESH)
        for k in range(len(sums_bf16)) for r, (px, py) in enumerate(_other_chips(x, y))]


def _chip_carry(sums_bf16, names):
    swapped = [k in SWAPPED for k in names]

    def start(ins, outs, sems):
        for cp in _chip_copies(ins, outs, *sems, swapped):
            cp.start()

    def finish(ins, outs, sems):
        for cp in _chip_copies(ins, outs, *sems, swapped):
            cp.wait()

    return _Carry(sums_bf16, _chip_landing(sums_bf16), False, _chip_sems(len(sums_bf16)), start, finish)


def _chip_sems(n):
    return [pltpu.SemaphoreType.DMA((3 * n,)), pltpu.SemaphoreType.DMA((3 * n,))]


def _chip_landing(sums_bf16):
    return [jax.ShapeDtypeStruct((N_CHIPS - 1,) + s.shape[1:], BF16) for s in sums_bf16]


def _chip_sum(sums_f32, landed, name, swapped):
    _, rows, cols = sums_f32.shape
    x, y, c = _mesh_pos()
    idx = jnp.stack([_slot(x, y, swapped), c]).astype(jnp.int32)

    def body(idx_ref, o_ref, l_ref, out_ref):
        out_ref[...] = ((o_ref[...] + l_ref[0].astype(F32)) + l_ref[1].astype(F32)) + l_ref[2].astype(F32)

    return pl.pallas_call(
        body, name=name,
        grid_spec=pltpu.PrefetchScalarGridSpec(
            num_scalar_prefetch=1, grid=(1,),
            in_specs=[pl.BlockSpec((None, rows, cols), lambda i, idx: (idx[0], 0, 0)),
                      pl.BlockSpec((N_CHIPS - 1, rows, cols), lambda i, idx: (0, 0, 0))],
            out_specs=pl.BlockSpec((rows, cols), lambda i, idx: (idx[1], 0))),
        out_shape=jax.ShapeDtypeStruct((2 * rows, cols), F32),
        compiler_params=pltpu.CompilerParams(vmem_limit_bytes=_vmem_limit(3 * rows * cols * 4)),
    )(idx, sums_f32, landed)


def _final_exchange(fulls, v):
    n = len(fulls)
    rows, cols = v.shape
    n_dev = 8

    def body(*refs):
        v_ref, out_ref = refs[0], refs[1 + 2 * n]
        outs = refs[1 + n:1 + 2 * n]
        buf, v_send, v_recv, h_send, h_recv = refs[2 + 2 * n:]
        x, y, c = _mesh_pos()
        me = 4 * x + 2 * y + c
        buf[me] = v_ref[...]
        peers = [(1 - x if r & 4 else x, 1 - y if r & 2 else y, 1 - c if r & 1 else c) for r in range(1, n_dev)]
        copies = []
        for r, peer in enumerate(peers):
            copies.append(pltpu.make_async_remote_copy(
                src_ref=v_ref, dst_ref=buf.at[me], send_sem=v_send.at[r], recv_sem=v_recv.at[r],
                device_id=peer, device_id_type=MESH))
        for k in range(n):
            half = fulls[k].shape[0] // 2
            mine = outs[k].at[pl.ds(c * half, half), :]
            copies.append(pltpu.make_async_remote_copy(
                src_ref=mine, dst_ref=mine, send_sem=h_send.at[k], recv_sem=h_recv.at[k],
                device_id=(x, y, 1 - c), device_id_type=MESH))
        for cp in copies:
            cp.start()
        for r, (px, py, pc) in enumerate(peers):
            pltpu.make_async_remote_copy(
                src_ref=v_ref, dst_ref=buf.at[4 * px + 2 * py + pc], send_sem=v_send.at[r], recv_sem=v_recv.at[r],
                device_id=(px, py, pc), device_id_type=MESH).wait_recv()
        for k in range(n):
            half = fulls[k].shape[0] // 2
            theirs = outs[k].at[pl.ds((1 - c) * half, half), :]
            pltpu.make_async_remote_copy(
                src_ref=theirs, dst_ref=theirs, send_sem=h_send.at[k], recv_sem=h_recv.at[k],
                device_id=(x, y, 1 - c), device_id_type=MESH).wait_recv()
        for cp in copies:
            cp.wait_send()
        acc = buf[0]
        for d in range(1, n_dev):
            acc = acc + buf[d]
        out_ref[...] = acc
        out_ref[3:4, :] = jnp.broadcast_to(jnp.sum(acc[3:4, :], axis=1, keepdims=True), (1, cols))

    vm = pl.BlockSpec(memory_space=pltpu.VMEM)
    res = pl.pallas_call(
        body, name="final_exchange",
        in_specs=[vm] + _hbm_specs(n), out_specs=_hbm_specs(n) + [vm],
        out_shape=[jax.ShapeDtypeStruct(f.shape, F32) for f in fulls] + [jax.ShapeDtypeStruct((rows, cols), F32)],
        input_output_aliases={1 + k: k for k in range(n)},
        scratch_shapes=[pltpu.VMEM((n_dev, rows, cols), F32),
                        pltpu.SemaphoreType.DMA((n_dev - 1,)), pltpu.SemaphoreType.DMA((n_dev - 1,)),
                        pltpu.SemaphoreType.DMA((n,)), pltpu.SemaphoreType.DMA((n,))],
    )(v, *fulls)
    return res[:n], res[n]


def _adamw_math(w, g, m, v):
    m = ADAM_B1 * m + (1.0 - ADAM_B1) * g
    v = ADAM_B2 * v + (1.0 - ADAM_B2) * (g * g)
    m_hat = m / (1.0 - ADAM_B1 ** ADAM_STEP)
    v_hat = v / (1.0 - ADAM_B2 ** ADAM_STEP)
    delta = -ADAM_LR * (m_hat / (jnp.sqrt(v_hat) + ADAM_EPS) + ADAM_WD * w)
    return delta, m, v


def _adamw(w, g, m, v, name):
    rows, cols = w.shape
    tm = rows // 2 if (rows // 2) % 8 == 0 else rows
    return _rowwise(_adamw_math, [w, g, m, v], [], [(cols, F32)] * 3, [], tm=tm, name=name)


def _unshard_cols(gathered):
    n, r, c = gathered.shape
    return jnp.transpose(gathered, (1, 0, 2)).reshape(r, n * c)


def _shard_cols(full):
    r, nc = full.shape
    return jnp.transpose(full.reshape(r, N_CHIPS, nc // N_CHIPS), (1, 0, 2))


LATE = ["w_sb_up", "w_dil_up", "w_out", "w_ffn_in", "w_ffn_out"]


def _late_weights(slabs, d_model, d_ff):
    g = dict(zip(LATE, slabs))
    return (_unshard_cols(g["w_sb_up"]), _unshard_cols(g["w_dil_up"]), g["w_out"].reshape(d_model, d_model),
            _unshard_cols(g["w_ffn_in"]), g["w_ffn_out"].reshape(d_ff, d_model))


ROW_SHARDED = ("w_in", "w_out", "w_ffn_out")


def _chip_major(grads):
    out = []
    for k, g in grads.items():
        if k in ROW_SHARDED:
            out.append(g.reshape(N_CHIPS, g.shape[0] // N_CHIPS, g.shape[1]))
        else:
            out.append(_shard_cols(g))
    return out


def _pair_reduce(grads):
    full = _chip_major(grads)
    others = _pair_exchange(full, next(iter(grads)))
    return [_pair_sum(g, o, "grad_pair_sum_" + k) for g, o, k in zip(full, others, grads)]


def _chip_sums(pair, landed, names):
    return {k: _chip_sum(p[0], l, "grad_chip_sum_" + k, k in SWAPPED) for p, l, k in zip(pair, landed, names)}


def _fwd_bwd(x, loss_target, g_mix, g_ffn, g_fin, wt_in, late_slabs):
    b_sz, s_len, d_model = x.shape
    t = b_sz * s_len
    d_ff = late_slabs[-1].shape[1] * N_CHIPS
    x2d = x.reshape(t, d_model)
    tgt2d = loss_target.reshape(t, d_model)

    (u,) = _rowwise(lambda xv, g: (_rms_stats(xv)[0] * g,), [x2d], [g_mix], [(d_model, BF16)], [], tm=512, name="norm_mix")
    qkv, (slab_ffn_out,) = _mm(u, wt_in, tb=True, b_cols=(0, QKV_WIDTH), tm=2048, tn=768, tk=d_model, name="proj_qkv",
                               carry=_gather_carry(late_slabs[4:], LATE[4:]))
    gates = _mm(u, wt_in, tb=True, b_cols=(QKV_WIDTH, 2 * d_model), out_dtype=BF16, tm=t, tn=256, tk=d_model,
                name="proj_gates")
    qkv3 = qkv.reshape(b_sz, s_len, QKV_WIDTH)
    o_sb, (slab_ffn_in,) = _sb_fwd(qkv3, b_sz, s_len, _gather_carry(late_slabs[3:4], LATE[3:4]))
    o_dl, lse, small_slabs = _dil_fwd(qkv3, b_sz, s_len, _gather_carry(late_slabs[:3], LATE[:3]))
    wf_sb_up, wf_dil_up, wf_out, wf_ffn_in, wf_ffn_out = _late_weights(
        list(small_slabs) + [slab_ffn_in, slab_ffn_out], d_model, d_ff)
    o_sb2, o_dl2 = o_sb.reshape(t, SB_WIDTH), o_dl.reshape(t, DIL_OUT_WIDTH)
    y_sb = _mm(o_sb2, wf_sb_up, out_dtype=BF16, tm=1024, tn=1024, tk=SB_WIDTH, name="sb_up")
    y_dl = _mm(o_dl2, wf_dil_up, out_dtype=BF16, tm=1024, tn=1024, tk=DIL_OUT_WIDTH, name="dil_up")

    def merge_fn(gt, ys, yd):
        return (_sigmoid(gt[:, :d_model]) * ys + _sigmoid(gt[:, d_model:]) * yd,)

    (merged,) = _rowwise(merge_fn, [gates, y_sb, y_dl], [], [(d_model, BF16)], [], tm=512, name="merge")
    x1 = _mm(merged, wf_out, add=x2d, tm=512, tn=1024, tk=d_model, name="mix_out")
    (u2,) = _rowwise(lambda xv, g: (_rms_stats(xv)[0] * g,), [x1], [g_ffn], [(d_model, BF16)], [], tm=512, name="norm_ffn")
    half_ff = d_ff // 2

    def act_fn(hv):
        gate = hv[:, :half_ff]
        return hv, gate * _sigmoid(gate) * hv[:, half_ff:]

    h, act = _mm(u2, wf_ffn_in, tm=512, tn=d_ff, tk=d_model, name="ffn_in",
                 epilogue=(act_fn, [], [], [(d_ff, BF16), (half_ff, BF16)], []))
    def head_fn(xv, tg, g):
        xhat, r = _rms_stats(xv)
        err = xhat * g - tg
        dy = err * (1.0 / d_model)
        dx, dg_rows = _rms_bwd(dy, xhat, r, g)
        loss_lanes = (0.5 / d_model) * jnp.sum(err * err, axis=0, keepdims=True)
        return dx, dx, jnp.sum(dg_rows, axis=0, keepdims=True), loss_lanes

    dx2, dx2_b, dg_fin, loss_lanes = _mm(
        act, wf_ffn_out, add=x1, tm=512, tn=1024, tk=d_ff, name="ffn_out",
        epilogue=(head_fn, [tgt2d], [g_fin], [(d_model, F32), (d_model, BF16)], [(1, d_model), (1, d_model)]))

    def dact_fn(da, hv):
        gate, up = hv[:, :half_ff], hv[:, half_ff:]
        sg = _sigmoid(gate)
        dgate = da * up * (sg * (1.0 + gate * (1.0 - sg)))
        return (jnp.concatenate([dgate, da * (gate * sg)], axis=1),)

    (dh,) = _mm(dx2_b, wf_ffn_out, tb=True, tm=512, tn=half_ff, tk=d_model, name="ffn_out_dx",
                epilogue=(dact_fn, [h], [], [(d_ff, BF16)], []))
    gw_ffn_out = _mm(act, dx2_b, ta=True, tm=256, tn=d_model, tk=t, name="ffn_out_dw")
    def norm_bwd_fn(du_, dres, xv, g):
        xhat, r = _rms_stats(xv)
        dx, dg_rows = _rms_bwd(du_, xhat, r, g)
        return dres + dx, jnp.sum(dg_rows, axis=0, keepdims=True)

    def norm_bwd_twice(*args):
        dx, dg = norm_bwd_fn(*args)
        return dx, dx, dg

    dx1, dx1_b, dg_ffn = _mm(dh, wf_ffn_in, tb=True, tm=512, tn=1024, tk=2 * d_ff, name="ffn_in_dx",
                             epilogue=(norm_bwd_twice, [dx2, x1], [g_ffn], [(d_model, F32), (d_model, BF16)], [(1, d_model)]))
    gw_ffn_in = _mm(u2, dh, ta=True, tm=d_model, tn=512, tk=t, name="ffn_in_dw")

    dmerged = _mm(dx1_b, wf_out, tb=True, out_dtype=BF16, tm=512, tn=1024, tk=d_model, name="mix_out_dx")
    gw_out = _mm(merged, dx1_b, ta=True, tm=256, tn=d_model, tk=t, name="mix_out_dw")

    def merge_bwd_fn(gt, ys, yd, dm):
        s_sb, s_dl = _sigmoid(gt[:, :d_model]), _sigmoid(gt[:, d_model:])
        dgates = jnp.concatenate([dm * ys * s_sb * (1.0 - s_sb), dm * yd * s_dl * (1.0 - s_dl)], axis=1)
        return dgates, dm * s_sb, dm * s_dl

    dgates, dy_sb, dy_dl = _rowwise(merge_bwd_fn, [gates, y_sb, y_dl, dmerged], [],
                                    [(2 * d_model, BF16), (d_model, BF16), (d_model, BF16)], [], tm=256, name="merge_bwd")
    do_sb = _mm(dy_sb, wf_sb_up, tb=True, out_dtype=BF16, tm=1024, tn=SB_WIDTH, tk=d_model, name="sb_up_dx")
    gw_sb_up = _mm(o_sb2, dy_sb, ta=True, tm=SB_WIDTH, tn=1024, tk=512, name="sb_up_dw")
    do_dl = _mm(dy_dl, wf_dil_up, tb=True, tm=1024, tn=DIL_OUT_WIDTH, tk=d_model, name="dil_up_dx")
    gw_dil_up = _mm(o_dl2, dy_dl, ta=True, tm=DIL_OUT_WIDTH, tn=1024, tk=512, name="dil_up_dw")
    late_grads = {"w_sb_up": gw_sb_up, "w_dil_up": gw_dil_up, "w_out": gw_out, "w_ffn_in": gw_ffn_in, "w_ffn_out": gw_ffn_out}
    pair = _pair_reduce(late_grads)
    (dq_sb, dk_sb, dv_sb), landed_a = _sb_bwd(qkv3, o_sb, do_sb.reshape(b_sz, s_len, SB_WIDTH), b_sz, s_len,
                                             _chip_carry([p[1] for p in pair[:4]], LATE[:4]))
    (dq_dl, dk_dl, dv_dl), landed_b = _dil_bwd(qkv3, o_dl, lse, do_dl.reshape(b_sz, s_len, DIL_OUT_WIDTH), b_sz, s_len,
                                               _chip_carry([p[1] for p in pair[4:]], LATE[4:]))
    landed = list(landed_a) + list(landed_b)
    dproj = jnp.concatenate(
        [a.reshape(t, -1) for a in (dq_sb, dk_sb, dv_sb)]
        + [a.reshape(t, -1) for a in (dq_dl, dk_dl, dv_dl)] + [dgates], axis=1)
    gwt_in = _mm(dproj, u, ta=True, tm=256, tn=d_model, tk=t, name="proj_dw")
    pair_in = _pair_reduce({"w_in": gwt_in})
    (dx, dg_mix), landed_in = _mm(
        dproj, wt_in, tm=512, tn=1024, tk=wt_in.shape[0], name="proj_dx",
        carry=_chip_carry([p[1] for p in pair_in], ["w_in"]),
        epilogue=(norm_bwd_fn, [dx1, x2d], [g_mix], [(d_model, F32)], [(1, d_model)]))

    grads = _chip_sums(pair, landed, LATE)
    grads.update(_chip_sums(pair_in, landed_in, ["w_in"]))
    return dx, grads, dg_mix, dg_ffn, dg_fin, loss_lanes


def kernel(x, norm_mix_g, w_in, w_sb_up, w_dil_up, w_out, norm_ffn_g, w_ffn_in, w_ffn_out, norm_final_g, loss_target, m_norm_mix_g, m_w_in, m_w_sb_up, m_w_dil_up, m_w_out, m_norm_ffn_g, m_w_ffn_in, m_w_ffn_out, m_norm_final_g, v_norm_mix_g, v_w_in, v_w_sb_up, v_w_dil_up, v_w_out, v_norm_ffn_g, v_w_ffn_in, v_w_ffn_out, v_norm_final_g):
    b_sz, s_len, d_model = x.shape
    d_ff = w_ffn_out.shape[1] * N_CHIPS
    g_mix, g_ffn, g_fin = norm_mix_g, norm_ffn_g, norm_final_g.reshape(1, d_model)

    names = ["w_in", "w_sb_up", "w_dil_up", "w_out", "w_ffn_in", "w_ffn_out"]
    shards = {"w_in": jnp.swapaxes(w_in[0], 0, 1), "w_sb_up": w_sb_up[0], "w_dil_up": w_dil_up[0], "w_out": w_out[0],
              "w_ffn_in": w_ffn_in[0], "w_ffn_out": w_ffn_out[0]}
    (slab_in,) = _gather_weights([_cast_to_slab(shards["w_in"], "cast_w_in")])
    late_slabs = [_cast_to_slab(shards[k], "cast_" + k, k in SWAPPED) for k in LATE]

    dx, grads, dg_mix, dg_ffn, dg_fin, loss_lanes = _fwd_bwd(
        x, loss_target, g_mix, g_ffn, g_fin, slab_in.reshape(-1, d_model), late_slabs)

    small = jnp.concatenate([dg_mix, dg_ffn, dg_fin, loss_lanes, jnp.zeros((4, d_model), F32)], axis=0)
    full_grads, small = _final_exchange([grads[k] for k in names], small)
    grads = dict(zip(names, full_grads))
    loss = small[3, 0]
    gains = jnp.concatenate([g_mix, g_ffn, g_fin, jnp.zeros((5, d_model), F32)], axis=0)
    gains_m = jnp.concatenate([m_norm_mix_g, m_norm_ffn_g, m_norm_final_g.reshape(1, d_model), jnp.zeros((5, d_model), F32)], axis=0)
    gains_v = jnp.concatenate([v_norm_mix_g, v_norm_ffn_g, v_norm_final_g.reshape(1, d_model), jnp.ones((5, d_model), F32)], axis=0)
    gd, gm, gv = _rowwise(_adamw_math, [gains, small, gains_m, gains_v], [], [(d_model, F32)] * 3, [], tm=8, name="adamw_gains")

    moments = {"w_in": (jnp.swapaxes(m_w_in[0], 0, 1), jnp.swapaxes(v_w_in[0], 0, 1)),
               "w_sb_up": (m_w_sb_up[0], v_w_sb_up[0]), "w_dil_up": (m_w_dil_up[0], v_w_dil_up[0]),
               "w_out": (m_w_out[0], v_w_out[0]), "w_ffn_in": (m_w_ffn_in[0], v_w_ffn_in[0]),
               "w_ffn_out": (m_w_ffn_out[0], v_w_ffn_out[0])}
    upd = {k: _adamw(shards[k], grads[k], moments[k][0], moments[k][1], "adamw_" + k) for k in names}

    def as_output(k, a):
        return (jnp.swapaxes(a, 0, 1) if k == "w_in" else a)[None]

    def w_out_of(i):
        return [as_output(k, upd[k][i]) for k in names]

    def ordered(mix, ws, ffn_g, fin):
        return [mix, ws[0], ws[1], ws[2], ws[3], ffn_g, ws[4], ws[5], fin]

    grad_ws = [as_output(k, grads[k]) for k in names]
    outs = [loss, dx.reshape(b_sz, s_len, d_model)]
    outs += ordered(small[0:1], grad_ws, small[1:2], small[2])
    outs += ordered(gd[0:1], w_out_of(0), gd[1:2], gd[2])
    outs += ordered(gm[0:1], w_out_of(1), gm[1:2], gm[2])
    outs += ordered(gv[0:1], w_out_of(2), gv[1:2], gv[2])
    return tuple(outs)
```

```python
import functools
import math

import jax
import jax.numpy as jnp
from jax import lax
from jax.experimental import pallas as pl
from jax.experimental.pallas import tpu as pltpu

F32 = jnp.float32
BF16 = jnp.bfloat16
MESH = pl.DeviceIdType.MESH

HEAD_DIM = 64
SB_HEADS = 8
DIL_PAIRS = ((128, 1), (512, 4), (2048, 16))
DIL_HEADS_PER_GROUP = 4
DIL_HEADS = DIL_HEADS_PER_GROUP * len(DIL_PAIRS)
SB_WIDTH = SB_HEADS * HEAD_DIM
DIL_WIDTH = DIL_HEADS * HEAD_DIM
DIL_OUT_WIDTH = DIL_HEADS_PER_GROUP * HEAD_DIM
QKV_WIDTH = 3 * SB_WIDTH + 3 * DIL_WIDTH
RMS_EPS = 1e-6
ALIBI_MAX_BIAS = 8.0
ADAM_LR = 0.001
ADAM_B1 = 0.9
ADAM_B2 = 0.999
ADAM_EPS = 1e-08
ADAM_WD = 0.01
ADAM_STEP = 10

LANES = 128
BLK = 128
NEG = -1e30
EXP_UNDERFLOW = -104.0
SB_FWD_CHAINS = 4
SB_BWD_CHAINS = 4
DIL_CHAINS = 4
N_CHIPS = 4
VMEM_CAP = 56 * 1024 * 1024


def _vmem_limit(tile_bytes):
    return int(min(VMEM_CAP, max(32 * 1024 * 1024, 3 * tile_bytes + 8 * 1024 * 1024)))


def _nbytes(shape, dtype):
    return math.prod(shape) * jnp.dtype(dtype).itemsize


def _dot(a, b):
    return jnp.dot(a, b, preferred_element_type=F32)


def _dot_nt(a, b):
    return lax.dot_general(a, b, (((1,), (1,)), ((), ())), preferred_element_type=F32)


def _dot_tn(a, b):
    return lax.dot_general(a, b, (((0,), (0,)), ((), ())), preferred_element_type=F32)


def _split2(x):
    hi = x.astype(BF16)
    lo = (x - hi.astype(F32)).astype(BF16)
    return hi, lo


def _sigmoid(x):
    return pl.reciprocal(1.0 + jnp.exp(-x), approx=True)


class _Carry:
    def __init__(self, arrays=(), out_shapes=(), aliased=False, sems=(), start=None, finish=None):
        self.arrays, self.out_shapes, self.aliased = list(arrays), list(out_shapes), aliased
        self.sems, self.start, self.finish = list(sems), start, finish

    def __bool__(self):
        return bool(self.arrays)

    def __add__(self, other):
        assert not self.aliased and not other.aliased
        n_a, n_o, n_s = len(self.arrays), len(self.out_shapes), len(self.sems)
        return _Carry(
            self.arrays + other.arrays, self.out_shapes + other.out_shapes, False, self.sems + other.sems,
            lambda i, o, s: (self.start(i[:n_a], o[:n_o], s[:n_s]), other.start(i[n_a:], o[n_o:], s[n_s:])),
            lambda i, o, s: (self.finish(i[:n_a], o[:n_o], s[:n_s]), other.finish(i[n_a:], o[n_o:], s[n_s:])))

    def call_args(self, n_in, n_out):
        aliases = {n_in + k: n_out + k for k in range(len(self.arrays))} if self.aliased else {}
        return _hbm_specs(len(self.arrays)), _hbm_specs(len(self.out_shapes)), self.out_shapes, aliases, self.sems

    def run(self, refs, n_in, n_out, step, n_steps, compute):
        if not self:
            compute()
            return
        n_c, n_o, n_s = len(self.arrays), len(self.out_shapes), len(self.sems)
        ins = refs[n_in:n_in + n_c]
        outs = refs[n_in + n_c + n_out:n_in + n_c + n_out + n_o]
        sems = refs[len(refs) - n_s:]

        @pl.when(step == 0)
        def _():
            self.start(ins, outs, sems)

        compute()

        @pl.when(step == n_steps - 1)
        def _():
            self.finish(ins, outs, sems)


def _mm(a, b, *, ta=False, tb=False, add=None, out_dtype=F32, tm, tn, tk, name, carry=None, epilogue=None,
        b_cols=None):
    carry = carry or _Carry()
    n_car = len(carry.arrays)
    if ta:
        kdim, m = a.shape
    else:
        m, kdim = a.shape
    if tb:
        n, k2 = b.shape
    else:
        k2, n = b.shape
    col0 = 0
    if b_cols is not None:
        assert b_cols[0] % tn == 0, name
        col0, n = b_cols[0] // tn, b_cols[1]
    assert kdim == k2 and m % tm == 0 and n % tn == 0 and kdim % tk == 0, (name, a.shape, b.shape)
    nk = kdim // tk
    grid = (m // tm, n // tn, nk)
    a_mode = dict(pipeline_mode=pl.Buffered(1)) if grid[0] == 1 and nk == 1 else {}
    b_mode = dict(pipeline_mode=pl.Buffered(1)) if grid[1] == 1 and nk == 1 else {}
    a_spec = (pl.BlockSpec((tk, tm), lambda i, j, k: (k, i), **a_mode) if ta
              else pl.BlockSpec((tm, tk), lambda i, j, k: (i, k), **a_mode))
    b_spec = (pl.BlockSpec((tn, tk), lambda i, j, k: (j + col0, k), **b_mode) if tb
              else pl.BlockSpec((tk, tn), lambda i, j, k: (k, j + col0), **b_mode))
    o_spec = pl.BlockSpec((tm, tn), lambda i, j, k: (i, j))
    dims = ((((0,) if ta else (1,)), ((1,) if tb else (0,))), ((), ()))
    has_add = add is not None
    if epilogue is None:
        ep_fn, ep_rows, ep_params, ep_outs, ep_accs = None, [], [], [], []
        out_sds, out_specs = [jax.ShapeDtypeStruct((m, n), out_dtype)], [o_spec]
    else:
        ep_fn, ep_rows, ep_params, ep_outs, ep_accs = epilogue
        assert grid[1] == 1 or not ep_accs, name
        out_sds = [jax.ShapeDtypeStruct((m, w * grid[1]), d) for w, d in ep_outs]
        out_sds += [jax.ShapeDtypeStruct(sh, F32) for sh in ep_accs]
        out_specs = [pl.BlockSpec((tm, w), lambda i, j, k: (i, j)) for w, _ in ep_outs]
        out_specs += [pl.BlockSpec(sh, lambda i, j, k: (0, 0)) for sh in ep_accs]
    n_main = len(out_sds)
    use_scratch = nk > 1 and (ep_fn is not None or jnp.dtype(out_dtype) != jnp.dtype(F32))
    n_in = 2 + has_add + len(ep_rows) + len(ep_params)

    def finish(total, refs, pid):
        outs = refs[n_in + n_car:n_in + n_car + n_main]
        if ep_fn is None:
            outs[0][...] = total.astype(out_dtype)
            return
        first = 2 + has_add
        rows = [r[...].astype(F32) for r in refs[first:first + len(ep_rows)]]
        params = [p[...] for p in refs[first + len(ep_rows):n_in]]
        res = ep_fn(total, *rows, *params)
        for o_ref, v in zip(outs[:len(ep_outs)], res):
            o_ref[...] = v.astype(o_ref.dtype)
        acc_refs = outs[len(ep_outs):]
        if acc_refs:
            @pl.when(pid[0] == 0)
            def _():
                for r in acc_refs:
                    r[...] = jnp.zeros(r.shape, F32)

            for r, v in zip(acc_refs, res[len(ep_outs):]):
                r[...] += v

    def compute(refs, pid):
        a_ref, b_ref = refs[0], refs[1]
        add_ref = refs[2] if has_add else None
        prod = lax.dot_general(a_ref[...].astype(BF16), b_ref[...].astype(BF16), dims, preferred_element_type=F32)
        if nk == 1:
            finish(prod + add_ref[...] if has_add else prod, refs, pid)
            return
        acc_ref = refs[n_in + n_car + n_main + len(carry.out_shapes)] if use_scratch else refs[n_in + n_car]
        k = pid[2]

        @pl.when(k == 0)
        def _():
            acc_ref[...] = prod + add_ref[...] if has_add else prod

        @pl.when(k > 0)
        def _():
            acc_ref[...] += prod

        if use_scratch:
            @pl.when(k == nk - 1)
            def _():
                finish(acc_ref[...], refs, pid)

    def body(*refs):
        pid = (pl.program_id(0), pl.program_id(1), pl.program_id(2))
        step = (pid[0] * grid[1] + pid[1]) * nk + pid[2]
        carry.run(refs, n_in, n_main, step, grid[0] * grid[1] * nk, lambda: compute(refs, pid))

    tile_bytes = (_nbytes((tm, tk), a.dtype) + _nbytes((tk, tn), b.dtype) + 2 * _nbytes((tm, tn), F32)
                  + (_nbytes((tm, tn), F32) if has_add else 0)
                  + sum(_nbytes((tm, r.shape[1]), r.dtype) for r in ep_rows) + sum(_nbytes((tm, w), d) for w, d in ep_outs))
    in_specs = [a_spec, b_spec] + ([o_spec] if has_add else [])
    in_specs += [pl.BlockSpec((tm, r.shape[1] // grid[1]), lambda i, j, k: (i, j)) for r in ep_rows]
    in_specs += [pl.BlockSpec(p.shape, lambda i, j, k: (0, 0)) for p in ep_params]
    args = (a, b) + ((add,) if has_add else ()) + tuple(ep_rows) + tuple(ep_params)
    scratch = [pltpu.VMEM((tm, tn), F32)] if use_scratch else []
    serial = bool(carry) or bool(ep_accs)
    c_in, c_out, c_shapes, c_alias, c_sems = carry.call_args(n_in, n_main)
    res = pl.pallas_call(
        body, name=name, grid=grid,
        in_specs=in_specs + c_in, out_specs=out_specs + c_out, out_shape=out_sds + c_shapes,
        input_output_aliases=c_alias, scratch_shapes=scratch + c_sems,
        compiler_params=pltpu.CompilerParams(
            dimension_semantics=("arbitrary",) * 3 if serial else ("parallel", "parallel", "arbitrary"),
            vmem_limit_bytes=_vmem_limit(tile_bytes)),
    )(*args, *carry.arrays)
    main = res[0] if ep_fn is None else list(res[:n_main])
    return (main, res[n_main:]) if carry else main


def _rowwise(fn, rows, params, outs, accs, *, tm, name, carry=None):
    carry = carry or _Carry()
    t = rows[0].shape[0]
    assert t % tm == 0, (name, t, tm)
    n_r, n_p, n_o, n_c = len(rows), len(params), len(outs), len(carry.arrays)

    def compute(refs, first):
        vals = [r[...].astype(F32) for r in refs[:n_r]] + [p[...] for p in refs[n_r:n_r + n_p]]
        res = fn(*vals)
        o_refs = refs[n_r + n_p + n_c:n_r + n_p + n_c + n_o]
        a_refs = refs[n_r + n_p + n_c + n_o:n_r + n_p + n_c + n_o + len(accs)]
        for o_ref, v in zip(o_refs, res[:n_o]):
            o_ref[...] = v.astype(o_ref.dtype)
        if accs:
            @pl.when(first)
            def _():
                for a_ref in a_refs:
                    a_ref[...] = jnp.zeros(a_ref.shape, F32)

            for a_ref, v in zip(a_refs, res[n_o:]):
                a_ref[...] += v

    def body(*refs):
        step = pl.program_id(0)
        carry.run(refs, n_r + n_p, n_o + len(accs), step, t // tm, lambda: compute(refs, step == 0))

    in_specs = [pl.BlockSpec((tm, r.shape[1]), lambda i: (i, 0)) for r in rows]
    in_specs += [pl.BlockSpec(p.shape, lambda i: (0, 0)) for p in params]
    out_specs = [pl.BlockSpec((tm, w), lambda i: (i, 0)) for w, _ in outs]
    out_specs += [pl.BlockSpec(s, lambda i: (0, 0)) for s in accs]
    out_shape = [jax.ShapeDtypeStruct((t, w), d) for w, d in outs]
    out_shape += [jax.ShapeDtypeStruct(s, F32) for s in accs]
    tile_bytes = sum(_nbytes((tm, r.shape[1]), r.dtype) for r in rows) + sum(_nbytes((tm, w), F32) for w, _ in outs)
    c_in, c_out, c_shapes, c_alias, c_sems = carry.call_args(n_r + n_p, n_o + len(accs))
    res = pl.pallas_call(
        body, name=name, grid=(t // tm,), in_specs=in_specs + c_in, out_specs=out_specs + c_out,
        out_shape=out_shape + c_shapes, input_output_aliases=c_alias, scratch_shapes=c_sems,
        compiler_params=pltpu.CompilerParams(
            dimension_semantics=("arbitrary",) if accs or carry else ("parallel",),
            vmem_limit_bytes=_vmem_limit(2 * tile_bytes)),
    )(*rows, *params, *carry.arrays)
    own = n_o + len(accs)
    return (list(res[:own]) + [res[own:]]) if carry else res


def _rms_stats(x):
    r = lax.rsqrt(jnp.mean(x * x, axis=-1, keepdims=True) + RMS_EPS)
    return x * r, r


def _rms_bwd(dy, xhat, r, g):
    dxhat = dy * g
    dx = r * (dxhat - xhat * jnp.mean(dxhat * xhat, axis=-1, keepdims=True))
    return dx, dy * xhat


def _sb_consts():
    lane = lax.broadcasted_iota(jnp.int32, (BLK, LANES), 1)
    head0 = lane < HEAD_DIM
    row = lax.broadcasted_iota(jnp.int32, (2 * BLK, BLK), 0) % BLK
    col = lax.broadcasted_iota(jnp.int32, (2 * BLK, BLK), 1)
    causal = col < row
    jj = lax.broadcasted_iota(jnp.int32, (BLK, BLK), 0)
    ss = lax.broadcasted_iota(jnp.int32, (BLK, BLK), 1)
    suffix = jnp.where(jj > ss, 1.0, 0.0).astype(BF16)
    return head0, causal, suffix


def _stack_heads(x, head0):
    zero = jnp.zeros_like(x)
    return jnp.concatenate([jnp.where(head0, x, zero), jnp.where(head0, zero, x)], axis=0)


def _sb_logits(z, causal, masked):
    sp = jnp.log(1.0 + jnp.exp(-jnp.abs(z)))
    log_keep = -(jnp.maximum(z, 0.0) + sp)
    log_beta = jnp.minimum(z, 0.0) - sp
    if masked:
        log_keep = jnp.where(causal, log_keep, 0.0)
    return log_keep, log_beta


def _suffix_sums(x, suffix):
    hi, lo = _split2(x)
    after = _dot(hi, suffix) + _dot(lo, suffix)
    total = jnp.broadcast_to(after[:, 0:1] + x[:, 0:1], x.shape)
    return after, total


def _sb_walk_back(i, state, per_chain, tile):
    def alive(st):
        worst = functools.reduce(jnp.maximum, [st[p][:, 0:1] for p in range(0, len(st), per_chain)])
        return jnp.max(worst) > EXP_UNDERFLOW

    def cond(c):
        return jnp.logical_and(c[0] < i, alive(c[1]))

    def body(c):
        return c[0] + 1, tile(i - 1 - c[0], c[1], False)

    return lax.while_loop(cond, body, (jnp.int32(0), state))[1]


def _lane_blocks(x, n):
    return [x[:, p * LANES:(p + 1) * LANES] for p in range(n)]


def _sb_fwd(qkv, b_sz, s_len, carry):
    nq = s_len // BLK
    n_pairs = SB_WIDTH // LANES
    ch = SB_FWD_CHAINS
    n_steps = n_pairs // ch
    scale = 1.0 / math.sqrt(HEAD_DIM)

    def compute(q_ref, k_ref, v_ref, o_ref):
        head0, causal, suffix = _sb_consts()

        def q_block(i, _):
            qs = pl.multiple_of(i * BLK, BLK)
            q_all = (q_ref[pl.ds(qs, BLK), :] * scale).astype(BF16)
            q01 = [_stack_heads(q, head0) for q in _lane_blocks(q_all, ch)]

            def tile(j, state, masked):
                ks = pl.multiple_of(j * BLK, BLK)
                ks_ = _lane_blocks(k_ref[pl.ds(ks, BLK), :].astype(BF16), ch)
                vs_ = _lane_blocks(v_ref[pl.ds(ks, BLK), :].astype(BF16), ch)
                zs = [_dot_nt(q01[p], ks_[p]) for p in range(ch)]
                logits = [_sb_logits(z, causal, masked) for z in zs]
                sums = [_suffix_sums(lg[0], suffix) for lg in logits]
                out = []
                for p in range(ch):
                    carry, acc = state[2 * p], state[2 * p + 1]
                    after, total = sums[p]
                    a = jnp.exp(logits[p][1] + carry + after)
                    if masked:
                        a = jnp.where(causal, a, 0.0)
                    a_hi, a_lo = _split2(a)
                    a_cat = jnp.concatenate([a_hi[:BLK], a_hi[BLK:], a_lo[:BLK], a_lo[BLK:]], axis=1)
                    v01 = _stack_heads(vs_[p], head0)
                    out += [carry + total, acc + _dot(a_cat, jnp.concatenate([v01, v01], axis=0))]
                return tuple(out)

            state = (jnp.zeros((2 * BLK, BLK), F32), jnp.zeros((BLK, LANES), F32)) * ch
            state = tile(i, state, True)
            state = _sb_walk_back(i, state, 2, tile)
            o_ref[pl.ds(qs, BLK), :] = jnp.concatenate([state[2 * p + 1] for p in range(ch)], axis=1)
            return 0

        lax.fori_loop(0, nq, q_block, 0)

    def body(*refs):
        step = pl.program_id(0) * n_steps + pl.program_id(1)
        o_ref = refs[3 + len(carry.arrays)]
        carry.run(refs, 3, 1, step, b_sz * n_steps, lambda: compute(refs[0], refs[1], refs[2], o_ref))

    blk = lambda off: pl.BlockSpec((None, s_len, ch * LANES), lambda b, p: (b, 0, off + p))
    c_in, c_out, c_shapes, c_alias, c_sems = carry.call_args(3, 1)
    res = pl.pallas_call(
        body, name="sb_fwd", grid=(b_sz, n_steps),
        in_specs=[blk(0), blk(n_steps), blk(2 * n_steps)] + c_in, out_specs=[blk(0)] + c_out,
        out_shape=[jax.ShapeDtypeStruct((b_sz, s_len, SB_WIDTH), F32)] + c_shapes,
        input_output_aliases=c_alias, scratch_shapes=c_sems,
        compiler_params=pltpu.CompilerParams(dimension_semantics=("arbitrary", "arbitrary"),
                                             vmem_limit_bytes=VMEM_CAP),
    )(qkv, qkv, qkv, *carry.arrays)
    return res[0], res[1:]


def _sb_bwd(qkv, o_sb, do_sb, b_sz, s_len, carry):
    nq = s_len // BLK
    n_pairs = SB_WIDTH // LANES
    ch = SB_BWD_CHAINS
    n_steps = n_pairs // ch
    scale = 1.0 / math.sqrt(HEAD_DIM)

    def compute(q_ref, k_ref, v_ref, o_ref, do_ref, dq_ref, dk_ref, dv_ref, dk_acc, dv_acc):
        head0, causal, suffix = _sb_consts()
        lrow = lax.broadcasted_iota(jnp.int32, (LANES, LANES), 0)
        ones_h0 = jnp.where(lrow < HEAD_DIM, 1.0, 0.0).astype(BF16)
        ones_h1 = jnp.where(lrow >= HEAD_DIM, 1.0, 0.0).astype(BF16)
        dk_acc[...] = jnp.zeros(dk_acc.shape, F32)
        dv_acc[...] = jnp.zeros(dv_acc.shape, F32)

        def q_block(i, _):
            qs = pl.multiple_of(i * BLK, BLK)
            q_all = (q_ref[pl.ds(qs, BLK), :] * scale).astype(BF16)
            do_all = do_ref[pl.ds(qs, BLK), :].astype(BF16)
            dd_all = do_all.astype(F32) * o_ref[pl.ds(qs, BLK), :]
            q01 = [_stack_heads(q, head0) for q in _lane_blocks(q_all, ch)]
            do01 = [_stack_heads(d, head0) for d in _lane_blocks(do_all, ch)]
            tot = []
            for dd in _lane_blocks(dd_all, ch):
                dd_hi, dd_lo = _split2(dd)
                tot.append(jnp.concatenate([_dot(dd_hi, ones_h0) + _dot(dd_lo, ones_h0),
                                            _dot(dd_hi, ones_h1) + _dot(dd_lo, ones_h1)], axis=0))

            def tile(j, state, masked):
                ks = pl.multiple_of(j * BLK, BLK)
                ks_ = _lane_blocks(k_ref[pl.ds(ks, BLK), :].astype(BF16), ch)
                vs_ = _lane_blocks(v_ref[pl.ds(ks, BLK), :].astype(BF16), ch)
                zs = [_dot_nt(q01[p], ks_[p]) for p in range(ch)]
                das = [_dot_nt(do01[p], vs_[p]) for p in range(ch)]
                logits = [_sb_logits(z, causal, masked) for z in zs]
                sums = [_suffix_sums(lg[0], suffix) for lg in logits]
                a_s, e_s = [], []
                for p in range(ch):
                    a = jnp.exp(logits[p][1] + state[3 * p] + sums[p][0])
                    if masked:
                        a = jnp.where(causal, a, 0.0)
                    a_s.append(a)
                    e_s.append(a * das[p])
                e_sums = [_suffix_sums(e, suffix) for e in e_s]
                out, dks, dvs = [], [], []
                for p in range(ch):
                    carry, rcarry, dq = state[3 * p:3 * p + 3]
                    e = e_s[p]
                    before = tot[p] - (rcarry + e_sums[p][0] + e)
                    beta = jnp.exp(logits[p][1])
                    dz = e * (1.0 - beta) - beta * before
                    if masked:
                        dz = jnp.where(causal, dz, 0.0)
                    dz_b = dz.astype(BF16)
                    dks.append(_dot_tn(dz_b, q01[p]))
                    dvs.append(_dot_tn(a_s[p].astype(BF16), do01[p]))
                    out += [carry + sums[p][1], rcarry + e_sums[p][1], dq + _dot(dz_b, ks_[p])]
                dk_acc[pl.ds(ks, BLK), :] += jnp.concatenate(dks, axis=1)
                dv_acc[pl.ds(ks, BLK), :] += jnp.concatenate(dvs, axis=1)
                return tuple(out)

            state = (jnp.zeros((2 * BLK, BLK), F32),) * (3 * ch)
            state = tile(i, state, True)
            state = _sb_walk_back(i, state, 3, tile)
            dq = [jnp.where(head0, state[3 * p + 2][:BLK], state[3 * p + 2][BLK:]) for p in range(ch)]
            dq_ref[pl.ds(qs, BLK), :] = (jnp.concatenate(dq, axis=1) * scale).astype(dq_ref.dtype)
            return 0

        lax.fori_loop(0, nq, q_block, 0)
        dk_ref[...] = dk_acc[...].astype(dk_ref.dtype)
        dv_ref[...] = dv_acc[...].astype(dv_ref.dtype)

    def body(*refs):
        step = pl.program_id(0) * n_steps + pl.program_id(1)
        n_c, n_o = len(carry.arrays), len(carry.out_shapes)
        own = refs[:5] + refs[5 + n_c:8 + n_c] + refs[8 + n_c + n_o:10 + n_c + n_o]
        carry.run(refs, 5, 3, step, b_sz * n_steps, lambda: compute(*own))

    blk = lambda off: pl.BlockSpec((None, s_len, ch * LANES), lambda b, p: (b, 0, off + p))
    once = lambda off: pl.BlockSpec((None, s_len, ch * LANES), lambda b, p: (b, 0, off + p),
                                    pipeline_mode=pl.Buffered(1))
    out_sd = jax.ShapeDtypeStruct((b_sz, s_len, SB_WIDTH), BF16)
    c_in, c_out, c_shapes, c_alias, c_sems = carry.call_args(5, 3)
    res = pl.pallas_call(
        body, name="sb_bwd", grid=(b_sz, n_steps),
        in_specs=[once(0), once(n_steps), once(2 * n_steps), once(0), once(0)] + c_in,
        out_specs=[blk(0), blk(0), blk(0)] + c_out, out_shape=[out_sd, out_sd, out_sd] + c_shapes,
        input_output_aliases=c_alias,
        scratch_shapes=[pltpu.VMEM((s_len, ch * LANES), F32), pltpu.VMEM((s_len, ch * LANES), F32)] + c_sems,
        compiler_params=pltpu.CompilerParams(dimension_semantics=("arbitrary", "arbitrary"),
                                             vmem_limit_bytes=VMEM_CAP),
    )(qkv, qkv, qkv, o_sb, do_sb, *carry.arrays)
    return res[:3], res[3:]


def _dil_consts(group, pair_idx, dilation):
    lane = lax.broadcasted_iota(jnp.int32, (BLK, LANES), 1)
    head0 = lane < HEAD_DIM
    row = lax.broadcasted_iota(jnp.int32, (2 * BLK, BLK), 0)
    qa = row % BLK
    kb = lax.broadcasted_iota(jnp.int32, (2 * BLK, BLK), 1)
    head = (group * DIL_HEADS_PER_GROUP + 2 * pair_idx + row // BLK).astype(F32)
    slope = jnp.exp((-ALIBI_MAX_BIAS * math.log(2.0) / DIL_HEADS) * (head + 1.0))
    valid_cur = kb <= qa
    valid_prev = kb >= qa
    bias_cur = -slope * ((qa - kb) * dilation).astype(F32)
    bias_prev = -slope * ((BLK + qa - kb) * dilation).astype(F32)
    return head0, valid_cur, valid_prev, bias_cur, bias_prev


def _dil_units(s_len, dilation):
    nb = s_len // dilation // BLK
    return [(r, n) for r in range(dilation) for n in range(nb)]


def _dil_rows(n, r, dilation):
    if dilation == 1:
        return pl.ds(n * BLK, BLK)
    return pl.ds(n * BLK * dilation + r, BLK, stride=dilation)


def _dil_scores(q01, k, bias, valid):
    s = _dot_nt(q01, k) * (1.0 / math.sqrt(HEAD_DIM)) + bias
    return jnp.where(valid, s, NEG)


def _dil_fwd(qkv, b_sz, s_len, carry):
    n_pairs = DIL_OUT_WIDTH // LANES
    q_off = 3 * SB_WIDTH // LANES
    per_kind = DIL_WIDTH // LANES

    def compute(pair_idx, qkv_refs, o_ref, lse_ref, m_s, l_s):
        m_s[...] = jnp.full(m_s.shape, NEG, F32)
        l_s[...] = jnp.zeros(l_s.shape, F32)
        o_ref[...] = jnp.zeros(o_ref.shape, F32)
        for g, (_, dilation) in enumerate(DIL_PAIRS):
            q_ref, k_ref, v_ref = qkv_refs[3 * g:3 * g + 3]
            head0, valid_cur, valid_prev, bias_cur, bias_prev = _dil_consts(g, pair_idx, dilation)
            units = _dil_units(s_len, dilation)
            for u0 in range(0, len(units), DIL_CHAINS):
                group = units[u0:u0 + DIL_CHAINS]
                rows_of = [_dil_rows(n, r, dilation) for r, n in group]
                scores, values = [], []
                for (r, n), rows in zip(group, rows_of):
                    q01 = _stack_heads(q_ref[rows, :].astype(BF16), head0)
                    sc = [_dil_scores(q01, k_ref[rows, :].astype(BF16), bias_cur, valid_cur)]
                    vals = [_stack_heads(v_ref[rows, :].astype(BF16), head0)]
                    if n > 0:
                        prev = _dil_rows(n - 1, r, dilation)
                        sc.append(_dil_scores(q01, k_ref[prev, :].astype(BF16), bias_prev, valid_prev))
                        vals.append(_stack_heads(v_ref[prev, :].astype(BF16), head0))
                    scores.append(sc)
                    values.append(vals)
                stats = []
                for sc, rows in zip(scores, rows_of):
                    m_blk = functools.reduce(jnp.maximum, [jnp.max(x, axis=-1, keepdims=True) for x in sc])
                    m_old = jnp.concatenate([m_s.at[0][rows, :], m_s.at[1][rows, :]], axis=0)
                    l_old = jnp.concatenate([l_s.at[0][rows, :], l_s.at[1][rows, :]], axis=0)
                    m_new = jnp.maximum(m_old, m_blk)
                    probs = [jnp.exp(x - m_new) for x in sc]
                    l_blk = functools.reduce(jnp.add, [jnp.sum(p, axis=-1, keepdims=True) for p in probs])
                    alpha = jnp.exp(m_old - m_new)
                    stats.append((m_new, alpha * l_old + l_blk, alpha, probs))
                for (m_new, l_new, alpha, probs), vals, rows in zip(stats, values, rows_of):
                    alpha_tok = jnp.where(head0, alpha[:BLK], alpha[BLK:])
                    p_cat = jnp.concatenate(
                        [h for p in probs for h in (p[:BLK].astype(BF16), p[BLK:].astype(BF16))], axis=1)
                    o_ref[rows, :] = alpha_tok * o_ref[rows, :] + _dot(p_cat, jnp.concatenate(vals, axis=0))
                    m_s.at[0][rows, :] = m_new[:BLK]
                    m_s.at[1][rows, :] = m_new[BLK:]
                    l_s.at[0][rows, :] = l_new[:BLK]
                    l_s.at[1][rows, :] = l_new[BLK:]
        lane = lax.broadcasted_iota(jnp.int32, (BLK, LANES), 1)
        for c in range(s_len // BLK):
            rows = pl.ds(c * BLK, BLK)
            l0, l1 = l_s.at[0][rows, :], l_s.at[1][rows, :]
            o_ref[rows, :] = o_ref[rows, :] / jnp.where(lane < HEAD_DIM, l0, l1)
            lse_ref.at[0][rows, :] = m_s.at[0][rows, :] + jnp.log(l0)
            lse_ref.at[1][rows, :] = m_s.at[1][rows, :] + jnp.log(l1)

    def body(*refs):
        pair_idx = pl.program_id(1)
        step = pl.program_id(0) * n_pairs + pair_idx
        n_c, n_o = len(carry.arrays), len(carry.out_shapes)
        o_ref, lse_ref = refs[9 + n_c:11 + n_c]
        m_s, l_s = refs[11 + n_c + n_o:13 + n_c + n_o]
        carry.run(refs, 9, 2, step, b_sz * n_pairs, lambda: compute(pair_idx, refs[:9], o_ref, lse_ref, m_s, l_s))

    in_specs = []
    for g in range(len(DIL_PAIRS)):
        for kind in range(3):
            off = q_off + kind * per_kind + g * n_pairs
            in_specs.append(pl.BlockSpec((None, s_len, LANES), lambda b, p, off=off: (b, 0, off + p)))
    c_in, c_out, c_shapes, c_alias, c_sems = carry.call_args(9, 2)
    res = pl.pallas_call(
        body, name="dil_fwd", grid=(b_sz, n_pairs),
        in_specs=in_specs + c_in,
        out_specs=[pl.BlockSpec((None, s_len, LANES), lambda b, p: (b, 0, p)),
                   pl.BlockSpec((None, None, 2, s_len, LANES), lambda b, p: (b, p, 0, 0, 0))] + c_out,
        out_shape=[jax.ShapeDtypeStruct((b_sz, s_len, DIL_OUT_WIDTH), F32),
                   jax.ShapeDtypeStruct((b_sz, n_pairs, 2, s_len, LANES), F32)] + c_shapes,
        input_output_aliases=c_alias,
        scratch_shapes=[pltpu.VMEM((2, s_len, LANES), F32), pltpu.VMEM((2, s_len, LANES), F32)] + c_sems,
        compiler_params=pltpu.CompilerParams(dimension_semantics=("arbitrary", "arbitrary"),
                                             vmem_limit_bytes=VMEM_CAP),
    )(*([qkv] * 9), *carry.arrays)
    return res[0], res[1], res[2:]


def _dil_bwd(qkv, o_dl, lse, do_dl, b_sz, s_len, carry):
    n_pairs = DIL_OUT_WIDTH // LANES
    n_groups = len(DIL_PAIRS)
    q_off = 3 * SB_WIDTH // LANES
    per_kind = DIL_WIDTH // LANES

    def compute(pair_idx, group, q_ref, k_ref, v_ref, o_ref, lse_ref, do_ref, dq_ref, dk_ref, dv_ref, d_s, dq_s, dk_s, dv_s):
        lrow = lax.broadcasted_iota(jnp.int32, (LANES, LANES), 0)
        ones_h0 = jnp.where(lrow < HEAD_DIM, 1.0, 0.0).astype(BF16)
        ones_h1 = jnp.where(lrow >= HEAD_DIM, 1.0, 0.0).astype(BF16)
        for c in range(s_len // BLK):
            rows = pl.ds(c * BLK, BLK)
            dd_hi, dd_lo = _split2(do_ref[rows, :] * o_ref[rows, :])
            d_s.at[0][rows, :] = _dot(dd_hi, ones_h0) + _dot(dd_lo, ones_h0)
            d_s.at[1][rows, :] = _dot(dd_hi, ones_h1) + _dot(dd_lo, ones_h1)
        dk_s[...] = jnp.zeros(dk_s.shape, F32)
        dv_s[...] = jnp.zeros(dv_s.shape, F32)

        def one_group(g, dilation):
            head0, valid_cur, valid_prev, bias_cur, bias_prev = _dil_consts(g, pair_idx, dilation)
            units = _dil_units(s_len, dilation)
            scale = 1.0 / math.sqrt(HEAD_DIM)
            for u0 in range(0, len(units), DIL_CHAINS):
                chunk = units[u0:u0 + DIL_CHAINS]
                loaded = []
                for r, n in chunk:
                    rows = _dil_rows(n, r, dilation)
                    q01 = _stack_heads(q_ref[rows, :].astype(BF16), head0)
                    do01 = _stack_heads(do_ref[rows, :].astype(BF16), head0)
                    lse01 = jnp.concatenate([lse_ref.at[0][rows, :], lse_ref.at[1][rows, :]], axis=0)
                    d01 = jnp.concatenate([d_s.at[0][rows, :], d_s.at[1][rows, :]], axis=0)
                    blocks = [(rows, bias_cur, valid_cur)]
                    if n > 0:
                        blocks.append((_dil_rows(n - 1, r, dilation), bias_prev, valid_prev))
                    parts = []
                    for krows, bias, valid in blocks:
                        k = k_ref[krows, :].astype(BF16)
                        v = v_ref[krows, :].astype(BF16)
                        parts.append((krows, k, _dil_scores(q01, k, bias, valid), _dot_nt(do01, v)))
                    loaded.append((rows, q01, do01, lse01, d01, parts))
                grads = []
                for rows, q01, do01, lse01, d01, parts in loaded:
                    for krows, k, sc, dp in parts:
                        p = jnp.exp(sc - lse01)
                        grads.append((p.astype(BF16), (p * (dp - d01) * scale).astype(BF16)))
                it = iter(grads)
                updates = []
                for rows, q01, do01, lse01, d01, parts in loaded:
                    dq = jnp.zeros((2 * BLK, LANES), F32)
                    for krows, k, sc, dp in parts:
                        p_b, ds = next(it)
                        dq = dq + _dot(ds, k)
                        updates.append((krows, _dot_tn(ds, q01), _dot_tn(p_b, do01)))
                    dq_s[rows, :] = jnp.where(head0, dq[:BLK], dq[BLK:])
                for krows, dk, dv in updates:
                    dk_s[krows, :] = dk_s[krows, :] + dk
                    dv_s[krows, :] = dv_s[krows, :] + dv

        for g, (_, dilation) in enumerate(DIL_PAIRS):
            pl.when(group == g)(functools.partial(one_group, g, dilation))
        dq_ref[...] = dq_s[...].astype(dq_ref.dtype)
        dk_ref[...] = dk_s[...].astype(dk_ref.dtype)
        dv_ref[...] = dv_s[...].astype(dv_ref.dtype)

    def body(*refs):
        pair_idx, group = pl.program_id(1), pl.program_id(2)
        step = (pl.program_id(0) * n_pairs + pair_idx) * n_groups + group
        n_c, n_o = len(carry.arrays), len(carry.out_shapes)
        own = refs[:6] + refs[6 + n_c:9 + n_c] + refs[9 + n_c + n_o:13 + n_c + n_o]
        carry.run(refs, 6, 3, step, b_sz * n_pairs * n_groups, lambda: compute(pair_idx, group, *own))

    def qkv_spec(kind):
        return pl.BlockSpec((None, s_len, LANES),
                            lambda b, p, g: (b, 0, q_off + kind * per_kind + g * n_pairs + p))

    tok_spec = pl.BlockSpec((None, s_len, LANES), lambda b, p, g: (b, 0, p))
    out_spec = pl.BlockSpec((None, s_len, LANES), lambda b, p, g: (b, 0, g * n_pairs + p))
    out_sd = jax.ShapeDtypeStruct((b_sz, s_len, DIL_WIDTH), BF16)
    c_in, c_out, c_shapes, c_alias, c_sems = carry.call_args(6, 3)
    res = pl.pallas_call(
        body, name="dil_bwd", grid=(b_sz, n_pairs, n_groups),
        in_specs=[qkv_spec(0), qkv_spec(1), qkv_spec(2), tok_spec,
                  pl.BlockSpec((None, None, 2, s_len, LANES), lambda b, p, g: (b, p, 0, 0, 0)), tok_spec] + c_in,
        out_specs=[out_spec, out_spec, out_spec] + c_out,
        out_shape=[out_sd, out_sd, out_sd] + c_shapes,
        input_output_aliases=c_alias,
        scratch_shapes=[pltpu.VMEM((2, s_len, LANES), F32)] + [pltpu.VMEM((s_len, LANES), F32)] * 3 + c_sems,
        compiler_params=pltpu.CompilerParams(dimension_semantics=("arbitrary", "arbitrary", "arbitrary"),
                                             vmem_limit_bytes=VMEM_CAP),
    )(qkv, qkv, qkv, o_dl, lse, do_dl, *carry.arrays)
    return res[:3], res[3:]


def _mesh_pos():
    return lax.axis_index("x"), lax.axis_index("y"), lax.axis_index("c")


def _other_chips(x, y):
    return [(1 - x, y), (x, 1 - y), (1 - x, 1 - y)]


def _hbm_specs(n):
    return [pl.BlockSpec(memory_space=pl.ANY)] * n


SWAPPED = ("w_ffn_in",)


def _slot(x, y, swapped):
    return 2 * y + x if swapped else 2 * x + y


def _cast_to_slab(w, name, swapped=False):
    rows, cols = w.shape
    mine = jnp.reshape(_slot(lax.axis_index("x"), lax.axis_index("y"), swapped), (1,)).astype(jnp.int32)

    def body(idx_ref, w_ref, o_ref):
        o_ref[...] = w_ref[...].astype(BF16)

    return pl.pallas_call(
        body, name=name,
        grid_spec=pltpu.PrefetchScalarGridSpec(
            num_scalar_prefetch=1, grid=(1,),
            in_specs=[pl.BlockSpec((rows, cols), lambda i, idx: (0, 0))],
            out_specs=pl.BlockSpec((None, rows, cols), lambda i, idx: (idx[0], 0, 0))),
        out_shape=jax.ShapeDtypeStruct((N_CHIPS, rows, cols), BF16),
        compiler_params=pltpu.CompilerParams(vmem_limit_bytes=_vmem_limit(rows * cols * 6)),
    )(mine, w)


def _gather_issue(slabs, send_sems, recv_sems, swapped):
    x, y, c = _mesh_pos()
    for k, slab in enumerate(slabs):
        half = slab.shape[1] // 2
        rows = slab.at[_slot(x, y, swapped[k]), pl.ds(c * half, half), :]
        for r, (px, py) in enumerate(_other_chips(x, y)):
            pltpu.make_async_remote_copy(
                src_ref=rows, dst_ref=rows, send_sem=send_sems.at[6 * k + r], recv_sem=recv_sems.at[6 * k + r],
                device_id=(px, py, c), device_id_type=MESH).start()


def _gather_complete(slabs, send_sems, recv_sems, swapped):
    x, y, c = _mesh_pos()
    chips = _other_chips(x, y)

    def copy(k, sem, block, rows, to):
        ref = slabs[k].at[block, rows, :]
        return pltpu.make_async_remote_copy(
            src_ref=ref, dst_ref=ref, send_sem=send_sems.at[sem], recv_sem=recv_sems.at[sem],
            device_id=to, device_id_type=MESH)

    for k, slab in enumerate(slabs):
        half = slab.shape[1] // 2
        for r, (px, py) in enumerate(chips):
            theirs = _slot(px, py, swapped[k])
            copy(k, 6 * k + r, theirs, pl.ds(c * half, half), (px, py, c)).wait_recv()
            copy(k, 6 * k + 3 + r, theirs, pl.ds(c * half, half), (x, y, 1 - c)).start()
    for k, slab in enumerate(slabs):
        half = slab.shape[1] // 2
        for r, (px, py) in enumerate(chips):
            copy(k, 6 * k + 3 + r, _slot(px, py, swapped[k]), pl.ds((1 - c) * half, half), (x, y, 1 - c)).wait_recv()
    for k, slab in enumerate(slabs):
        half = slab.shape[1] // 2
        for r, (px, py) in enumerate(chips):
            copy(k, 6 * k + r, _slot(x, y, swapped[k]), pl.ds(c * half, half), (px, py, c)).wait_send()
            copy(k, 6 * k + 3 + r, _slot(px, py, swapped[k]), pl.ds(c * half, half), (x, y, 1 - c)).wait_send()


def _gather_sems(n):
    return [pltpu.SemaphoreType.DMA((6 * n,)), pltpu.SemaphoreType.DMA((6 * n,))]


def _gather_carry(slabs, names):
    swapped = [k in SWAPPED for k in names]
    return _Carry(slabs, [jax.ShapeDtypeStruct(a.shape, a.dtype) for a in slabs], True, _gather_sems(len(slabs)),
                  lambda ins, outs, sems: _gather_issue(outs, *sems, swapped),
                  lambda ins, outs, sems: _gather_complete(outs, *sems, swapped))


def _pair_copies(ins, outs, send_sems, recv_sems):
    x, y, c = _mesh_pos()
    copies = []
    for k, g in enumerate(ins):
        half = g.shape[1] // 2
        copies.append(pltpu.make_async_remote_copy(
            src_ref=g.at[:, pl.ds((1 - c) * half, half), :], dst_ref=outs[k],
            send_sem=send_sems.at[k], recv_sem=recv_sems.at[k],
            device_id=(x, y, 1 - c), device_id_type=MESH))
    return copies


def _pair_carry(grads):
    n = len(grads)

    def start(ins, outs, sems):
        for cp in _pair_copies(ins, outs, *sems):
            cp.start()

    def finish(ins, outs, sems):
        for cp in _pair_copies(ins, outs, *sems):
            cp.wait()

    return _Carry(grads, [jax.ShapeDtypeStruct((N_CHIPS, g.shape[1] // 2, g.shape[2]), F32) for g in grads], False,
                  [pltpu.SemaphoreType.DMA((n,)), pltpu.SemaphoreType.DMA((n,))], start, finish)


def _pair_exchange(grads, tag):
    carry = _pair_carry(grads)
    n = len(grads)

    def body(*refs):
        carry.start(refs[:n], refs[n:2 * n], refs[2 * n:])
        carry.finish(refs[:n], refs[n:2 * n], refs[2 * n:])

    return pl.pallas_call(
        body, name="grad_pair_exchange_" + tag, in_specs=_hbm_specs(n), out_specs=_hbm_specs(n),
        out_shape=carry.out_shapes, scratch_shapes=carry.sems,
    )(*grads)


def _pair_sum(grad, other, name):
    _, rows, cols = grad.shape
    half = rows // 2
    core = jnp.reshape(lax.axis_index("c"), (1,)).astype(jnp.int32)

    def body(core_ref, g_ref, p_ref, s_ref, sb_ref):
        s = g_ref[...] + p_ref[...]
        s_ref[...] = s
        sb_ref[...] = s.astype(BF16)

    blk = pl.BlockSpec((None, half, cols), lambda p, core_ref: (p, 0, 0))
    return pl.pallas_call(
        body, name=name,
        grid_spec=pltpu.PrefetchScalarGridSpec(
            num_scalar_prefetch=1, grid=(N_CHIPS,),
            in_specs=[pl.BlockSpec((None, half, cols), lambda p, core_ref: (p, core_ref[0], 0)), blk],
            out_specs=[blk, blk]),
        out_shape=[jax.ShapeDtypeStruct((N_CHIPS, half, cols), F32),
                   jax.ShapeDtypeStruct((N_CHIPS, half, cols), BF16)],
        compiler_params=pltpu.CompilerParams(dimension_semantics=("parallel",),
                                             vmem_limit_bytes=_vmem_limit(4 * half * cols * 4)),
    )(core, grad, other)


def _chip_copies(sums_bf16, lands, send_sems, recv_sems, swapped):
    x, y, c = _mesh_pos()
    return [pltpu.make_async_remote_copy(
        src_ref=sums_bf16[k].at[_slot(px, py, swapped[k])], dst_ref=lands[k].at[r],
        send_sem=send_sems.at[3 * k + r], recv_sem=recv_sems.at[3 * k + r],
        device_id=(px, py, c), device_id_type=MESH)
        for k in range(len(sums_bf16)) for r, (px, py) in enumerate(_other_chips(x, y))]


def _chip_carry(sums_bf16, names):
    swapped = [k in SWAPPED for k in names]

    def start(ins, outs, sems):
        for cp in _chip_copies(ins, outs, *sems, swapped):
            cp.start()

    def finish(ins, outs, sems):
        for cp in _chip_copies(ins, outs, *sems, swapped):
            cp.wait()

    return _Carry(sums_bf16, _chip_landing(sums_bf16), False, _chip_sems(len(sums_bf16)), start, finish)


def _chip_sems(n):
    return [pltpu.SemaphoreType.DMA((3 * n,)), pltpu.SemaphoreType.DMA((3 * n,))]


def _chip_landing(sums_bf16):
    return [jax.ShapeDtypeStruct((N_CHIPS - 1,) + s.shape[1:], BF16) for s in sums_bf16]


def _chip_sum(sums_f32, landed, name, swapped):
    _, rows, cols = sums_f32.shape
    x, y, c = _mesh_pos()
    idx = jnp.stack([_slot(x, y, swapped), c]).astype(jnp.int32)

    def body(idx_ref, o_ref, l_ref, out_ref):
        out_ref[...] = ((o_ref[...] + l_ref[0].astype(F32)) + l_ref[1].astype(F32)) + l_ref[2].astype(F32)

    return pl.pallas_call(
        body, name=name,
        grid_spec=pltpu.PrefetchScalarGridSpec(
            num_scalar_prefetch=1, grid=(1,),
            in_specs=[pl.BlockSpec((None, rows, cols), lambda i, idx: (idx[0], 0, 0)),
                      pl.BlockSpec((N_CHIPS - 1, rows, cols), lambda i, idx: (0, 0, 0))],
            out_specs=pl.BlockSpec((rows, cols), lambda i, idx: (idx[1], 0))),
        out_shape=jax.ShapeDtypeStruct((2 * rows, cols), F32),
        compiler_params=pltpu.CompilerParams(vmem_limit_bytes=_vmem_limit(3 * rows * cols * 4)),
    )(idx, sums_f32, landed)


def _final_exchange(fulls, v):
    n = len(fulls)
    rows, cols = v.shape
    n_dev = 8

    def body(*refs):
        v_ref, out_ref = refs[0], refs[1 + 2 * n]
        outs = refs[1 + n:1 + 2 * n]
        buf, v_send, v_recv, h_send, h_recv = refs[2 + 2 * n:]
        x, y, c = _mesh_pos()
        me = 4 * x + 2 * y + c
        buf[me] = v_ref[...]
        peers = [(1 - x if r & 4 else x, 1 - y if r & 2 else y, 1 - c if r & 1 else c) for r in range(1, n_dev)]
        copies = []
        for r, peer in enumerate(peers):
            copies.append(pltpu.make_async_remote_copy(
                src_ref=v_ref, dst_ref=buf.at[me], send_sem=v_send.at[r], recv_sem=v_recv.at[r],
                device_id=peer, device_id_type=MESH))
        for k in range(n):
            half = fulls[k].shape[0] // 2
            mine = outs[k].at[pl.ds(c * half, half), :]
            copies.append(pltpu.make_async_remote_copy(
                src_ref=mine, dst_ref=mine, send_sem=h_send.at[k], recv_sem=h_recv.at[k],
                device_id=(x, y, 1 - c), device_id_type=MESH))
        for cp in copies:
            cp.start()
        for r, (px, py, pc) in enumerate(peers):
            pltpu.make_async_remote_copy(
                src_ref=v_ref, dst_ref=buf.at[4 * px + 2 * py + pc], send_sem=v_send.at[r], recv_sem=v_recv.at[r],
                device_id=(px, py, pc), device_id_type=MESH).wait_recv()
        for k in range(n):
            half = fulls[k].shape[0] // 2
            theirs = outs[k].at[pl.ds((1 - c) * half, half), :]
            pltpu.make_async_remote_copy(
                src_ref=theirs, dst_ref=theirs, send_sem=h_send.at[k], recv_sem=h_recv.at[k],
                device_id=(x, y, 1 - c), device_id_type=MESH).wait_recv()
        for cp in copies:
            cp.wait_send()
        acc = buf[0]
        for d in range(1, n_dev):
            acc = acc + buf[d]
        out_ref[...] = acc
        out_ref[3:4, :] = jnp.broadcast_to(jnp.sum(acc[3:4, :], axis=1, keepdims=True), (1, cols))

    vm = pl.BlockSpec(memory_space=pltpu.VMEM)
    res = pl.pallas_call(
        body, name="final_exchange",
        in_specs=[vm] + _hbm_specs(n), out_specs=_hbm_specs(n) + [vm],
        out_shape=[jax.ShapeDtypeStruct(f.shape, F32) for f in fulls] + [jax.ShapeDtypeStruct((rows, cols), F32)],
        input_output_aliases={1 + k: k for k in range(n)},
        scratch_shapes=[pltpu.VMEM((n_dev, rows, cols), F32),
                        pltpu.SemaphoreType.DMA((n_dev - 1,)), pltpu.SemaphoreType.DMA((n_dev - 1,)),
                        pltpu.SemaphoreType.DMA((n,)), pltpu.SemaphoreType.DMA((n,))],
    )(v, *fulls)
    return res[:n], res[n]


def _adamw_math(w, g, m, v):
    m = ADAM_B1 * m + (1.0 - ADAM_B1) * g
    v = ADAM_B2 * v + (1.0 - ADAM_B2) * (g * g)
    m_hat = m / (1.0 - ADAM_B1 ** ADAM_STEP)
    v_hat = v / (1.0 - ADAM_B2 ** ADAM_STEP)
    delta = -ADAM_LR * (m_hat / (jnp.sqrt(v_hat) + ADAM_EPS) + ADAM_WD * w)
    return delta, m, v


def _adamw(w, g, m, v, name):
    rows, cols = w.shape
    tm = rows // 2 if (rows // 2) % 8 == 0 else rows
    return _rowwise(_adamw_math, [w, g, m, v], [], [(cols, F32)] * 3, [], tm=tm, name=name)


def _unshard_cols(gathered):
    n, r, c = gathered.shape
    return jnp.transpose(gathered, (1, 0, 2)).reshape(r, n * c)


def _shard_cols(full):
    r, nc = full.shape
    return jnp.transpose(full.reshape(r, N_CHIPS, nc // N_CHIPS), (1, 0, 2))


LATE = ["w_sb_up", "w_dil_up", "w_out", "w_ffn_in", "w_ffn_out"]


def _late_weights(slabs, d_model, d_ff):
    g = dict(zip(LATE, slabs))
    return (_unshard_cols(g["w_sb_up"]), _unshard_cols(g["w_dil_up"]), g["w_out"].reshape(d_model, d_model),
            _unshard_cols(g["w_ffn_in"]), g["w_ffn_out"].reshape(d_ff, d_model))


ROW_SHARDED = ("w_in", "w_out", "w_ffn_out")


def _chip_major(grads):
    out = []
    for k, g in grads.items():
        if k in ROW_SHARDED:
            out.append(g.reshape(N_CHIPS, g.shape[0] // N_CHIPS, g.shape[1]))
        else:
            out.append(_shard_cols(g))
    return out


def _pair_sums(full, others, names):
    return [_pair_sum(g, o, "grad_pair_sum_" + k) for g, o, k in zip(full, others, names)]


def _pair_reduce(grads):
    full = _chip_major(grads)
    return _pair_sums(full, _pair_exchange(full, next(iter(grads))), list(grads))


def _chip_sums(pair, landed, names):
    return {k: _chip_sum(p[0], l, "grad_chip_sum_" + k, k in SWAPPED) for p, l, k in zip(pair, landed, names)}


def _fwd_bwd(x, loss_target, g_mix, g_ffn, g_fin, slab_in, late_slabs):
    b_sz, s_len, d_model = x.shape
    t = b_sz * s_len
    d_ff = late_slabs[-1].shape[1] * N_CHIPS
    x2d = x.reshape(t, d_model)
    tgt2d = loss_target.reshape(t, d_model)

    u, (slab_in,) = _rowwise(lambda xv, g: (_rms_stats(xv)[0] * g,), [x2d], [g_mix], [(d_model, BF16)], [], tm=512,
                             name="norm_mix", carry=_gather_carry([slab_in], ["w_in"]))
    wt_in = slab_in.reshape(-1, d_model)
    qkv, (slab_ffn_out,) = _mm(u, wt_in, tb=True, b_cols=(0, QKV_WIDTH), tm=2048, tn=768, tk=d_model, name="proj_qkv",
                               carry=_gather_carry(late_slabs[4:], LATE[4:]))
    gates = _mm(u, wt_in, tb=True, b_cols=(QKV_WIDTH, 2 * d_model), out_dtype=BF16, tm=t, tn=256, tk=d_model,
                name="proj_gates")
    qkv3 = qkv.reshape(b_sz, s_len, QKV_WIDTH)
    o_sb, (slab_ffn_in,) = _sb_fwd(qkv3, b_sz, s_len, _gather_carry(late_slabs[3:4], LATE[3:4]))
    o_dl, lse, small_slabs = _dil_fwd(qkv3, b_sz, s_len, _gather_carry(late_slabs[:3], LATE[:3]))
    wf_sb_up, wf_dil_up, wf_out, wf_ffn_in, wf_ffn_out = _late_weights(
        list(small_slabs) + [slab_ffn_in, slab_ffn_out], d_model, d_ff)
    o_sb2, o_dl2 = o_sb.reshape(t, SB_WIDTH), o_dl.reshape(t, DIL_OUT_WIDTH)
    y_sb = _mm(o_sb2, wf_sb_up, out_dtype=BF16, tm=1024, tn=1024, tk=SB_WIDTH, name="sb_up")
    y_dl = _mm(o_dl2, wf_dil_up, out_dtype=BF16, tm=1024, tn=1024, tk=DIL_OUT_WIDTH, name="dil_up")

    def merge_fn(gt, ys, yd):
        return (_sigmoid(gt[:, :d_model]) * ys + _sigmoid(gt[:, d_model:]) * yd,)

    (merged,) = _rowwise(merge_fn, [gates, y_sb, y_dl], [], [(d_model, BF16)], [], tm=512, name="merge")
    x1 = _mm(merged, wf_out, add=x2d, tm=512, tn=1024, tk=d_model, name="mix_out")
    (u2,) = _rowwise(lambda xv, g: (_rms_stats(xv)[0] * g,), [x1], [g_ffn], [(d_model, BF16)], [], tm=512, name="norm_ffn")
    half_ff = d_ff // 2

    def act_fn(hv):
        gate = hv[:, :half_ff]
        return hv, gate * _sigmoid(gate) * hv[:, half_ff:]

    h, act = _mm(u2, wf_ffn_in, tm=512, tn=d_ff, tk=d_model, name="ffn_in",
                 epilogue=(act_fn, [], [], [(d_ff, BF16), (half_ff, BF16)], []))
    def head_fn(xv, tg, g):
        xhat, r = _rms_stats(xv)
        err = xhat * g - tg
        dy = err * (1.0 / d_model)
        dx, dg_rows = _rms_bwd(dy, xhat, r, g)
        loss_lanes = (0.5 / d_model) * jnp.sum(err * err, axis=0, keepdims=True)
        return dx, dx, jnp.sum(dg_rows, axis=0, keepdims=True), loss_lanes

    dx2, dx2_b, dg_fin, loss_lanes = _mm(
        act, wf_ffn_out, add=x1, tm=512, tn=1024, tk=d_ff, name="ffn_out",
        epilogue=(head_fn, [tgt2d], [g_fin], [(d_model, F32), (d_model, BF16)], [(1, d_model), (1, d_model)]))

    def dact_fn(da, hv):
        gate, up = hv[:, :half_ff], hv[:, half_ff:]
        sg = _sigmoid(gate)
        dgate = da * up * (sg * (1.0 + gate * (1.0 - sg)))
        return (jnp.concatenate([dgate, da * (gate * sg)], axis=1),)

    (dh,) = _mm(dx2_b, wf_ffn_out, tb=True, tm=512, tn=half_ff, tk=d_model, name="ffn_out_dx",
                epilogue=(dact_fn, [h], [], [(d_ff, BF16)], []))
    gw_ffn_out = _mm(act, dx2_b, ta=True, tm=256, tn=d_model, tk=t, name="ffn_out_dw")
    def norm_bwd_fn(du_, dres, xv, g):
        xhat, r = _rms_stats(xv)
        dx, dg_rows = _rms_bwd(du_, xhat, r, g)
        return dres + dx, jnp.sum(dg_rows, axis=0, keepdims=True)

    def norm_bwd_twice(*args):
        dx, dg = norm_bwd_fn(*args)
        return dx, dx, dg

    dx1, dx1_b, dg_ffn = _mm(dh, wf_ffn_in, tb=True, tm=512, tn=1024, tk=2 * d_ff, name="ffn_in_dx",
                             epilogue=(norm_bwd_twice, [dx2, x1], [g_ffn], [(d_model, F32), (d_model, BF16)], [(1, d_model)]))
    gw_ffn_in = _mm(u2, dh, ta=True, tm=d_model, tn=512, tk=t, name="ffn_in_dw")

    dmerged = _mm(dx1_b, wf_out, tb=True, out_dtype=BF16, tm=512, tn=1024, tk=d_model, name="mix_out_dx")
    gw_out = _mm(merged, dx1_b, ta=True, tm=256, tn=d_model, tk=t, name="mix_out_dw")

    def merge_bwd_fn(gt, ys, yd, dm):
        s_sb, s_dl = _sigmoid(gt[:, :d_model]), _sigmoid(gt[:, d_model:])
        dgates = jnp.concatenate([dm * ys * s_sb * (1.0 - s_sb), dm * yd * s_dl * (1.0 - s_dl)], axis=1)
        return dgates, dm * s_sb, dm * s_dl

    full_big = _chip_major({"w_out": gw_out, "w_ffn_in": gw_ffn_in, "w_ffn_out": gw_ffn_out})
    dgates, dy_sb, dy_dl, others_big = _rowwise(
        merge_bwd_fn, [gates, y_sb, y_dl, dmerged], [], [(2 * d_model, BF16), (d_model, BF16), (d_model, BF16)], [],
        tm=256, name="merge_bwd", carry=_pair_carry(full_big))
    pair_big = _pair_sums(full_big, others_big, LATE[2:])
    do_sb = _mm(dy_sb, wf_sb_up, tb=True, out_dtype=BF16, tm=1024, tn=SB_WIDTH, tk=d_model, name="sb_up_dx")
    gw_sb_up = _mm(o_sb2, dy_sb, ta=True, tm=SB_WIDTH, tn=1024, tk=512, name="sb_up_dw")
    do_dl = _mm(dy_dl, wf_dil_up, tb=True, tm=1024, tn=DIL_OUT_WIDTH, tk=d_model, name="dil_up_dx")
    gw_dil_up = _mm(o_dl2, dy_dl, ta=True, tm=DIL_OUT_WIDTH, tn=1024, tk=512, name="dil_up_dw")
    full_small = _chip_major({"w_sb_up": gw_sb_up, "w_dil_up": gw_dil_up})
    (dq_sb, dk_sb, dv_sb), brought = _sb_bwd(
        qkv3, o_sb, do_sb.reshape(b_sz, s_len, SB_WIDTH), b_sz, s_len,
        _chip_carry([p[1] for p in pair_big[:2]], LATE[2:4]) + _pair_carry(full_small))
    pair_small = _pair_sums(full_small, brought[2:], LATE[:2])
    (dq_dl, dk_dl, dv_dl), landed_b = _dil_bwd(
        qkv3, o_dl, lse, do_dl.reshape(b_sz, s_len, DIL_OUT_WIDTH), b_sz, s_len,
        _chip_carry([pair_big[2][1], pair_small[0][1], pair_small[1][1]], [LATE[4], LATE[0], LATE[1]]))
    pair = pair_small + pair_big
    landed = [landed_b[1], landed_b[2], brought[0], brought[1], landed_b[0]]
    dproj = jnp.concatenate(
        [a.reshape(t, -1) for a in (dq_sb, dk_sb, dv_sb)]
        + [a.reshape(t, -1) for a in (dq_dl, dk_dl, dv_dl)] + [dgates], axis=1)
    gwt_in = _mm(dproj, u, ta=True, tm=256, tn=d_model, tk=t, name="proj_dw")
    pair_in = _pair_reduce({"w_in": gwt_in})
    (dx, dg_mix), landed_in = _mm(
        dproj, wt_in, tm=512, tn=1024, tk=wt_in.shape[0], name="proj_dx",
        carry=_chip_carry([p[1] for p in pair_in], ["w_in"]),
        epilogue=(norm_bwd_fn, [dx1, x2d], [g_mix], [(d_model, F32)], [(1, d_model)]))

    grads = _chip_sums(pair, landed, LATE)
    grads.update(_chip_sums(pair_in, landed_in, ["w_in"]))
    return dx, grads, dg_mix, dg_ffn, dg_fin, loss_lanes


def kernel(x, norm_mix_g, w_in, w_sb_up, w_dil_up, w_out, norm_ffn_g, w_ffn_in, w_ffn_out, norm_final_g, loss_target, m_norm_mix_g, m_w_in, m_w_sb_up, m_w_dil_up, m_w_out, m_norm_ffn_g, m_w_ffn_in, m_w_ffn_out, m_norm_final_g, v_norm_mix_g, v_w_in, v_w_sb_up, v_w_dil_up, v_w_out, v_norm_ffn_g, v_w_ffn_in, v_w_ffn_out, v_norm_final_g):
    b_sz, s_len, d_model = x.shape
    d_ff = w_ffn_out.shape[1] * N_CHIPS
    g_mix, g_ffn, g_fin = norm_mix_g, norm_ffn_g, norm_final_g.reshape(1, d_model)

    names = ["w_in", "w_sb_up", "w_dil_up", "w_out", "w_ffn_in", "w_ffn_out"]
    shards = {"w_in": jnp.swapaxes(w_in[0], 0, 1), "w_sb_up": w_sb_up[0], "w_dil_up": w_dil_up[0], "w_out": w_out[0],
              "w_ffn_in": w_ffn_in[0], "w_ffn_out": w_ffn_out[0]}
    slab_in = _cast_to_slab(shards["w_in"], "cast_w_in")
    late_slabs = [_cast_to_slab(shards[k], "cast_" + k, k in SWAPPED) for k in LATE]

    dx, grads, dg_mix, dg_ffn, dg_fin, loss_lanes = _fwd_bwd(
        x, loss_target, g_mix, g_ffn, g_fin, slab_in, late_slabs)

    small = jnp.concatenate([dg_mix, dg_ffn, dg_fin, loss_lanes, jnp.zeros((4, d_model), F32)], axis=0)
    full_grads, small = _final_exchange([grads[k] for k in names], small)
    grads = dict(zip(names, full_grads))
    loss = small[3, 0]
    gains = jnp.concatenate([g_mix, g_ffn, g_fin, jnp.zeros((5, d_model), F32)], axis=0)
    gains_m = jnp.concatenate([m_norm_mix_g, m_norm_ffn_g, m_norm_final_g.reshape(1, d_model), jnp.zeros((5, d_model), F32)], axis=0)
    gains_v = jnp.concatenate([v_norm_mix_g, v_norm_ffn_g, v_norm_final_g.reshape(1, d_model), jnp.ones((5, d_model), F32)], axis=0)
    gd, gm, gv = _rowwise(_adamw_math, [gains, small, gains_m, gains_v], [], [(d_model, F32)] * 3, [], tm=8, name="adamw_gains")

    moments = {"w_in": (jnp.swapaxes(m_w_in[0], 0, 1), jnp.swapaxes(v_w_in[0], 0, 1)),
               "w_sb_up": (m_w_sb_up[0], v_w_sb_up[0]), "w_dil_up": (m_w_dil_up[0], v_w_dil_up[0]),
               "w_out": (m_w_out[0], v_w_out[0]), "w_ffn_in": (m_w_ffn_in[0], v_w_ffn_in[0]),
               "w_ffn_out": (m_w_ffn_out[0], v_w_ffn_out[0])}
    upd = {k: _adamw(shards[k], grads[k], moments[k][0], moments[k][1], "adamw_" + k) for k in names}

    def as_output(k, a):
        return (jnp.swapaxes(a, 0, 1) if k == "w_in" else a)[None]

    def w_out_of(i):
        return [as_output(k, upd[k][i]) for k in names]

    def ordered(mix, ws, ffn_g, fin):
        return [mix, ws[0], ws[1], ws[2], ws[3], ffn_g, ws[4], ws[5], fin]

    grad_ws = [as_output(k, grads[k]) for k in names]
    outs = [loss, dx.reshape(b_sz, s_len, d_model)]
    outs += ordered(small[0:1], grad_ws, small[1:2], small[2])
    outs += ordered(gd[0:1], w_out_of(0), gd[1:2], gd[2])
    outs += ordered(gm[0:1], w_out_of(1), gm[1:2], gm[2])
    outs += ordered(gv[0:1], w_out_of(2), gv[1:2], gv[2])
    return tuple(outs)
```

```python
import functools
import math

import jax
import jax.numpy as jnp
from jax import lax
from jax.experimental import pallas as pl
from jax.experimental.pallas import tpu as pltpu

F32 = jnp.float32
BF16 = jnp.bfloat16
MESH = pl.DeviceIdType.MESH

HEAD_DIM = 64
SB_HEADS = 8
DIL_PAIRS = ((128, 1), (512, 4), (2048, 16))
DIL_HEADS_PER_GROUP = 4
DIL_HEADS = DIL_HEADS_PER_GROUP * len(DIL_PAIRS)
SB_WIDTH = SB_HEADS * HEAD_DIM
DIL_WIDTH = DIL_HEADS * HEAD_DIM
DIL_OUT_WIDTH = DIL_HEADS_PER_GROUP * HEAD_DIM
QKV_WIDTH = 3 * SB_WIDTH + 3 * DIL_WIDTH
RMS_EPS = 1e-6
ALIBI_MAX_BIAS = 8.0
ADAM_LR = 0.001
ADAM_B1 = 0.9
ADAM_B2 = 0.999
ADAM_EPS = 1e-08
ADAM_WD = 0.01
ADAM_STEP = 10

LANES = 128
BLK = 128
NEG = -1e30
EXP_UNDERFLOW = -104.0
SB_FWD_CHAINS = 4
SB_BWD_CHAINS = 4
DIL_CHAINS = 4
N_CHIPS = 4
VMEM_CAP = 56 * 1024 * 1024


def _vmem_limit(tile_bytes):
    return int(min(VMEM_CAP, max(32 * 1024 * 1024, 3 * tile_bytes + 8 * 1024 * 1024)))


def _nbytes(shape, dtype):
    return math.prod(shape) * jnp.dtype(dtype).itemsize


def _dot(a, b):
    return jnp.dot(a, b, preferred_element_type=F32)


def _dot_nt(a, b):
    return lax.dot_general(a, b, (((1,), (1,)), ((), ())), preferred_element_type=F32)


def _dot_tn(a, b):
    return lax.dot_general(a, b, (((0,), (0,)), ((), ())), preferred_element_type=F32)


def _split2(x):
    hi = x.astype(BF16)
    lo = (x - hi.astype(F32)).astype(BF16)
    return hi, lo


def _sigmoid(x):
    return pl.reciprocal(1.0 + jnp.exp(-x), approx=True)


class _Carry:
    def __init__(self, arrays=(), out_shapes=(), aliased=False, sems=(), start=None, finish=None):
        self.arrays, self.out_shapes, self.aliased = list(arrays), list(out_shapes), aliased
        self.sems, self.start, self.finish = list(sems), start, finish

    def __bool__(self):
        return bool(self.arrays)

    def __add__(self, other):
        assert not self.aliased and not other.aliased
        n_a, n_o, n_s = len(self.arrays), len(self.out_shapes), len(self.sems)
        return _Carry(
            self.arrays + other.arrays, self.out_shapes + other.out_shapes, False, self.sems + other.sems,
            lambda i, o, s: (self.start(i[:n_a], o[:n_o], s[:n_s]), other.start(i[n_a:], o[n_o:], s[n_s:])),
            lambda i, o, s: (self.finish(i[:n_a], o[:n_o], s[:n_s]), other.finish(i[n_a:], o[n_o:], s[n_s:])))

    def call_args(self, n_in, n_out):
        aliases = {n_in + k: n_out + k for k in range(len(self.arrays))} if self.aliased else {}
        return _hbm_specs(len(self.arrays)), _hbm_specs(len(self.out_shapes)), self.out_shapes, aliases, self.sems

    def run(self, refs, n_in, n_out, step, n_steps, compute):
        if not self:
            compute()
            return
        n_c, n_o, n_s = len(self.arrays), len(self.out_shapes), len(self.sems)
        ins = refs[n_in:n_in + n_c]
        outs = refs[n_in + n_c + n_out:n_in + n_c + n_out + n_o]
        sems = refs[len(refs) - n_s:]

        @pl.when(step == 0)
        def _():
            self.start(ins, outs, sems)

        compute()

        @pl.when(step == n_steps - 1)
        def _():
            self.finish(ins, outs, sems)


def _mm(a, b, *, ta=False, tb=False, add=None, out_dtype=F32, tm, tn, tk, name, carry=None, epilogue=None,
        b_cols=None):
    carry = carry or _Carry()
    n_car = len(carry.arrays)
    if ta:
        kdim, m = a.shape
    else:
        m, kdim = a.shape
    if tb:
        n, k2 = b.shape
    else:
        k2, n = b.shape
    col0 = 0
    if b_cols is not None:
        assert b_cols[0] % tn == 0, name
        col0, n = b_cols[0] // tn, b_cols[1]
    assert kdim == k2 and m % tm == 0 and n % tn == 0 and kdim % tk == 0, (name, a.shape, b.shape)
    nk = kdim // tk
    grid = (m // tm, n // tn, nk)
    a_mode = dict(pipeline_mode=pl.Buffered(1)) if grid[0] == 1 and nk == 1 else {}
    b_mode = dict(pipeline_mode=pl.Buffered(1)) if grid[1] == 1 and nk == 1 else {}
    a_spec = (pl.BlockSpec((tk, tm), lambda i, j, k: (k, i), **a_mode) if ta
              else pl.BlockSpec((tm, tk), lambda i, j, k: (i, k), **a_mode))
    b_spec = (pl.BlockSpec((tn, tk), lambda i, j, k: (j + col0, k), **b_mode) if tb
              else pl.BlockSpec((tk, tn), lambda i, j, k: (k, j + col0), **b_mode))
    o_spec = pl.BlockSpec((tm, tn), lambda i, j, k: (i, j))
    dims = ((((0,) if ta else (1,)), ((1,) if tb else (0,))), ((), ()))
    has_add = add is not None
    if epilogue is None:
        ep_fn, ep_rows, ep_params, ep_outs, ep_accs = None, [], [], [], []
        out_sds, out_specs = [jax.ShapeDtypeStruct((m, n), out_dtype)], [o_spec]
    else:
        ep_fn, ep_rows, ep_params, ep_outs, ep_accs = epilogue
        assert grid[1] == 1 or not ep_accs, name
        out_sds = [jax.ShapeDtypeStruct((m, w * grid[1]), d) for w, d in ep_outs]
        out_sds += [jax.ShapeDtypeStruct(sh, F32) for sh in ep_accs]
        out_specs = [pl.BlockSpec((tm, w), lambda i, j, k: (i, j)) for w, _ in ep_outs]
        out_specs += [pl.BlockSpec(sh, lambda i, j, k: (0, 0)) for sh in ep_accs]
    n_main = len(out_sds)
    use_scratch = nk > 1 and (ep_fn is not None or jnp.dtype(out_dtype) != jnp.dtype(F32))
    n_in = 2 + has_add + len(ep_rows) + len(ep_params)

    def finish(total, refs, pid):
        outs = refs[n_in + n_car:n_in + n_car + n_main]
        if ep_fn is None:
            outs[0][...] = total.astype(out_dtype)
            return
        first = 2 + has_add
        rows = [r[...].astype(F32) for r in refs[first:first + len(ep_rows)]]
        params = [p[...] for p in refs[first + len(ep_rows):n_in]]
        res = ep_fn(total, *rows, *params)
        for o_ref, v in zip(outs[:len(ep_outs)], res):
            o_ref[...] = v.astype(o_ref.dtype)
        acc_refs = outs[len(ep_outs):]
        if acc_refs:
            @pl.when(pid[0] == 0)
            def _():
                for r in acc_refs:
                    r[...] = jnp.zeros(r.shape, F32)

            for r, v in zip(acc_refs, res[len(ep_outs):]):
                r[...] += v

    def compute(refs, pid):
        a_ref, b_ref = refs[0], refs[1]
        add_ref = refs[2] if has_add else None
        prod = lax.dot_general(a_ref[...].astype(BF16), b_ref[...].astype(BF16), dims, preferred_element_type=F32)
        if nk == 1:
            finish(prod + add_ref[...] if has_add else prod, refs, pid)
            return
        acc_ref = refs[n_in + n_car + n_main + len(carry.out_shapes)] if use_scratch else refs[n_in + n_car]
        k = pid[2]

        @pl.when(k == 0)
        def _():
            acc_ref[...] = prod + add_ref[...] if has_add else prod

        @pl.when(k > 0)
        def _():
            acc_ref[...] += prod

        if use_scratch:
            @pl.when(k == nk - 1)
            def _():
                finish(acc_ref[...], refs, pid)

    def body(*refs):
        pid = (pl.program_id(0), pl.program_id(1), pl.program_id(2))
        step = (pid[0] * grid[1] + pid[1]) * nk + pid[2]
        carry.run(refs, n_in, n_main, step, grid[0] * grid[1] * nk, lambda: compute(refs, pid))

    tile_bytes = (_nbytes((tm, tk), a.dtype) + _nbytes((tk, tn), b.dtype) + 2 * _nbytes((tm, tn), F32)
                  + (_nbytes((tm, tn), F32) if has_add else 0)
                  + sum(_nbytes((tm, r.shape[1]), r.dtype) for r in ep_rows) + sum(_nbytes((tm, w), d) for w, d in ep_outs))
    in_specs = [a_spec, b_spec] + ([o_spec] if has_add else [])
    in_specs += [pl.BlockSpec((tm, r.shape[1] // grid[1]), lambda i, j, k: (i, j)) for r in ep_rows]
    in_specs += [pl.BlockSpec(p.shape, lambda i, j, k: (0, 0)) for p in ep_params]
    args = (a, b) + ((add,) if has_add else ()) + tuple(ep_rows) + tuple(ep_params)
    scratch = [pltpu.VMEM((tm, tn), F32)] if use_scratch else []
    serial = bool(carry) or bool(ep_accs)
    c_in, c_out, c_shapes, c_alias, c_sems = carry.call_args(n_in, n_main)
    res = pl.pallas_call(
        body, name=name, grid=grid,
        in_specs=in_specs + c_in, out_specs=out_specs + c_out, out_shape=out_sds + c_shapes,
        input_output_aliases=c_alias, scratch_shapes=scratch + c_sems,
        compiler_params=pltpu.CompilerParams(
            dimension_semantics=("arbitrary",) * 3 if serial else ("parallel", "parallel", "arbitrary"),
            vmem_limit_bytes=_vmem_limit(tile_bytes)),
    )(*args, *carry.arrays)
    main = res[0] if ep_fn is None else list(res[:n_main])
    return (main, res[n_main:]) if carry else main


def _rowwise(fn, rows, params, outs, accs, *, tm, name, carry=None):
    carry = carry or _Carry()
    t = rows[0].shape[0]
    assert t % tm == 0, (name, t, tm)
    n_r, n_p, n_o, n_c = len(rows), len(params), len(outs), len(carry.arrays)

    def compute(refs, first):
        vals = [r[...].astype(F32) for r in refs[:n_r]] + [p[...] for p in refs[n_r:n_r + n_p]]
        res = fn(*vals)
        o_refs = refs[n_r + n_p + n_c:n_r + n_p + n_c + n_o]
        a_refs = refs[n_r + n_p + n_c + n_o:n_r + n_p + n_c + n_o + len(accs)]
        for o_ref, v in zip(o_refs, res[:n_o]):
            o_ref[...] = v.astype(o_ref.dtype)
        if accs:
            @pl.when(first)
            def _():
                for a_ref in a_refs:
                    a_ref[...] = jnp.zeros(a_ref.shape, F32)

            for a_ref, v in zip(a_refs, res[n_o:]):
                a_ref[...] += v

    def body(*refs):
        step = pl.program_id(0)
        carry.run(refs, n_r + n_p, n_o + len(accs), step, t // tm, lambda: compute(refs, step == 0))

    in_specs = [pl.BlockSpec((tm, r.shape[1]), lambda i: (i, 0)) for r in rows]
    in_specs += [pl.BlockSpec(p.shape, lambda i: (0, 0)) for p in params]
    out_specs = [pl.BlockSpec((tm, w), lambda i: (i, 0)) for w, _ in outs]
    out_specs += [pl.BlockSpec(s, lambda i: (0, 0)) for s in accs]
    out_shape = [jax.ShapeDtypeStruct((t, w), d) for w, d in outs]
    out_shape += [jax.ShapeDtypeStruct(s, F32) for s in accs]
    tile_bytes = sum(_nbytes((tm, r.shape[1]), r.dtype) for r in rows) + sum(_nbytes((tm, w), F32) for w, _ in outs)
    c_in, c_out, c_shapes, c_alias, c_sems = carry.call_args(n_r + n_p, n_o + len(accs))
    res = pl.pallas_call(
        body, name=name, grid=(t // tm,), in_specs=in_specs + c_in, out_specs=out_specs + c_out,
        out_shape=out_shape + c_shapes, input_output_aliases=c_alias, scratch_shapes=c_sems,
        compiler_params=pltpu.CompilerParams(
            dimension_semantics=("arbitrary",) if accs or carry else ("parallel",),
            vmem_limit_bytes=_vmem_limit(2 * tile_bytes)),
    )(*rows, *params, *carry.arrays)
    own = n_o + len(accs)
    return (list(res[:own]) + [res[own:]]) if carry else res


def _rms_stats(x):
    r = lax.rsqrt(jnp.mean(x * x, axis=-1, keepdims=True) + RMS_EPS)
    return x * r, r


def _rms_bwd(dy, xhat, r, g):
    dxhat = dy * g
    dx = r * (dxhat - xhat * jnp.mean(dxhat * xhat, axis=-1, keepdims=True))
    return dx, dy * xhat


def _sb_consts():
    lane = lax.broadcasted_iota(jnp.int32, (BLK, LANES), 1)
    head0 = lane < HEAD_DIM
    row = lax.broadcasted_iota(jnp.int32, (2 * BLK, BLK), 0) % BLK
    col = lax.broadcasted_iota(jnp.int32, (2 * BLK, BLK), 1)
    causal = col < row
    jj = lax.broadcasted_iota(jnp.int32, (BLK, BLK), 0)
    ss = lax.broadcasted_iota(jnp.int32, (BLK, BLK), 1)
    suffix = jnp.where(jj > ss, 1.0, 0.0).astype(BF16)
    return head0, causal, suffix


def _stack_heads(x, head0):
    zero = jnp.zeros_like(x)
    return jnp.concatenate([jnp.where(head0, x, zero), jnp.where(head0, zero, x)], axis=0)


def _sb_logits(z, causal, masked):
    sp = jnp.log(1.0 + jnp.exp(-jnp.abs(z)))
    log_keep = -(jnp.maximum(z, 0.0) + sp)
    log_beta = jnp.minimum(z, 0.0) - sp
    if masked:
        log_keep = jnp.where(causal, log_keep, 0.0)
    return log_keep, log_beta


def _suffix_sums(x, suffix):
    hi, lo = _split2(x)
    after = _dot(hi, suffix) + _dot(lo, suffix)
    total = jnp.broadcast_to(after[:, 0:1] + x[:, 0:1], x.shape)
    return after, total


def _sb_walk_back(i, state, per_chain, tile):
    def alive(st):
        worst = functools.reduce(jnp.maximum, [st[p][:, 0:1] for p in range(0, len(st), per_chain)])
        return jnp.max(worst) > EXP_UNDERFLOW

    def cond(c):
        return jnp.logical_and(c[0] < i, alive(c[1]))

    def body(c):
        return c[0] + 1, tile(i - 1 - c[0], c[1], False)

    return lax.while_loop(cond, body, (jnp.int32(0), state))[1]


def _lane_blocks(x, n):
    return [x[:, p * LANES:(p + 1) * LANES] for p in range(n)]


def _sb_fwd(qkv, b_sz, s_len, carry):
    nq = s_len // BLK
    n_pairs = SB_WIDTH // LANES
    ch = SB_FWD_CHAINS
    n_steps = n_pairs // ch
    scale = 1.0 / math.sqrt(HEAD_DIM)

    def compute(q_ref, k_ref, v_ref, o_ref):
        head0, causal, suffix = _sb_consts()

        def q_block(i, _):
            qs = pl.multiple_of(i * BLK, BLK)
            q_all = (q_ref[pl.ds(qs, BLK), :] * scale).astype(BF16)
            q01 = [_stack_heads(q, head0) for q in _lane_blocks(q_all, ch)]

            def tile(j, state, masked):
                ks = pl.multiple_of(j * BLK, BLK)
                ks_ = _lane_blocks(k_ref[pl.ds(ks, BLK), :].astype(BF16), ch)
                vs_ = _lane_blocks(v_ref[pl.ds(ks, BLK), :].astype(BF16), ch)
                zs = [_dot_nt(q01[p], ks_[p]) for p in range(ch)]
                logits = [_sb_logits(z, causal, masked) for z in zs]
                sums = [_suffix_sums(lg[0], suffix) for lg in logits]
                out = []
                for p in range(ch):
                    carry, acc = state[2 * p], state[2 * p + 1]
                    after, total = sums[p]
                    a = jnp.exp(logits[p][1] + carry + after)
                    if masked:
                        a = jnp.where(causal, a, 0.0)
                    a_hi, a_lo = _split2(a)
                    a_cat = jnp.concatenate([a_hi[:BLK], a_hi[BLK:], a_lo[:BLK], a_lo[BLK:]], axis=1)
                    v01 = _stack_heads(vs_[p], head0)
                    out += [carry + total, acc + _dot(a_cat, jnp.concatenate([v01, v01], axis=0))]
                return tuple(out)

            state = (jnp.zeros((2 * BLK, BLK), F32), jnp.zeros((BLK, LANES), F32)) * ch
            state = tile(i, state, True)
            state = _sb_walk_back(i, state, 2, tile)
            o_ref[pl.ds(qs, BLK), :] = jnp.concatenate([state[2 * p + 1] for p in range(ch)], axis=1)
            return 0

        lax.fori_loop(0, nq, q_block, 0)

    def body(*refs):
        step = pl.program_id(0) * n_steps + pl.program_id(1)
        o_ref = refs[3 + len(carry.arrays)]
        carry.run(refs, 3, 1, step, b_sz * n_steps, lambda: compute(refs[0], refs[1], refs[2], o_ref))

    blk = lambda off: pl.BlockSpec((None, s_len, ch * LANES), lambda b, p: (b, 0, off + p))
    c_in, c_out, c_shapes, c_alias, c_sems = carry.call_args(3, 1)
    res = pl.pallas_call(
        body, name="sb_fwd", grid=(b_sz, n_steps),
        in_specs=[blk(0), blk(n_steps), blk(2 * n_steps)] + c_in, out_specs=[blk(0)] + c_out,
        out_shape=[jax.ShapeDtypeStruct((b_sz, s_len, SB_WIDTH), F32)] + c_shapes,
        input_output_aliases=c_alias, scratch_shapes=c_sems,
        compiler_params=pltpu.CompilerParams(dimension_semantics=("arbitrary", "arbitrary"),
                                             vmem_limit_bytes=VMEM_CAP),
    )(qkv, qkv, qkv, *carry.arrays)
    return res[0], res[1:]


def _sb_bwd(qkv, o_sb, do_sb, b_sz, s_len, carry):
    nq = s_len // BLK
    n_pairs = SB_WIDTH // LANES
    ch = SB_BWD_CHAINS
    n_steps = n_pairs // ch
    scale = 1.0 / math.sqrt(HEAD_DIM)

    def compute(q_ref, k_ref, v_ref, o_ref, do_ref, dq_ref, dk_ref, dv_ref, dk_acc, dv_acc):
        head0, causal, suffix = _sb_consts()
        lrow = lax.broadcasted_iota(jnp.int32, (LANES, LANES), 0)
        ones_h0 = jnp.where(lrow < HEAD_DIM, 1.0, 0.0).astype(BF16)
        ones_h1 = jnp.where(lrow >= HEAD_DIM, 1.0, 0.0).astype(BF16)
        dk_acc[...] = jnp.zeros(dk_acc.shape, F32)
        dv_acc[...] = jnp.zeros(dv_acc.shape, F32)

        def q_block(i, _):
            qs = pl.multiple_of(i * BLK, BLK)
            q_all = (q_ref[pl.ds(qs, BLK), :] * scale).astype(BF16)
            do_all = do_ref[pl.ds(qs, BLK), :].astype(BF16)
            dd_all = do_all.astype(F32) * o_ref[pl.ds(qs, BLK), :]
            q01 = [_stack_heads(q, head0) for q in _lane_blocks(q_all, ch)]
            do01 = [_stack_heads(d, head0) for d in _lane_blocks(do_all, ch)]
            tot = []
            for dd in _lane_blocks(dd_all, ch):
                dd_hi, dd_lo = _split2(dd)
                tot.append(jnp.concatenate([_dot(dd_hi, ones_h0) + _dot(dd_lo, ones_h0),
                                            _dot(dd_hi, ones_h1) + _dot(dd_lo, ones_h1)], axis=0))

            def tile(j, state, masked):
                ks = pl.multiple_of(j * BLK, BLK)
                ks_ = _lane_blocks(k_ref[pl.ds(ks, BLK), :].astype(BF16), ch)
                vs_ = _lane_blocks(v_ref[pl.ds(ks, BLK), :].astype(BF16), ch)
                zs = [_dot_nt(q01[p], ks_[p]) for p in range(ch)]
                das = [_dot_nt(do01[p], vs_[p]) for p in range(ch)]
                logits = [_sb_logits(z, causal, masked) for z in zs]
                sums = [_suffix_sums(lg[0], suffix) for lg in logits]
                a_s, e_s = [], []
                for p in range(ch):
                    a = jnp.exp(logits[p][1] + state[3 * p] + sums[p][0])
                    if masked:
                        a = jnp.where(causal, a, 0.0)
                    a_s.append(a)
                    e_s.append(a * das[p])
                e_sums = [_suffix_sums(e, suffix) for e in e_s]
                out, dks, dvs = [], [], []
                for p in range(ch):
                    carry, rcarry, dq = state[3 * p:3 * p + 3]
                    e = e_s[p]
                    before = tot[p] - (rcarry + e_sums[p][0] + e)
                    beta = jnp.exp(logits[p][1])
                    dz = e * (1.0 - beta) - beta * before
                    if masked:
                        dz = jnp.where(causal, dz, 0.0)
                    dz_b = dz.astype(BF16)
                    dks.append(_dot_tn(dz_b, q01[p]))
                    dvs.append(_dot_tn(a_s[p].astype(BF16), do01[p]))
                    out += [carry + sums[p][1], rcarry + e_sums[p][1], dq + _dot(dz_b, ks_[p])]
                dk_acc[pl.ds(ks, BLK), :] += jnp.concatenate(dks, axis=1)
                dv_acc[pl.ds(ks, BLK), :] += jnp.concatenate(dvs, axis=1)
                return tuple(out)

            state = (jnp.zeros((2 * BLK, BLK), F32),) * (3 * ch)
            state = tile(i, state, True)
            state = _sb_walk_back(i, state, 3, tile)
            dq = [jnp.where(head0, state[3 * p + 2][:BLK], state[3 * p + 2][BLK:]) for p in range(ch)]
            dq_ref[pl.ds(qs, BLK), :] = (jnp.concatenate(dq, axis=1) * scale).astype(dq_ref.dtype)
            return 0

        lax.fori_loop(0, nq, q_block, 0)
        dk_ref[...] = dk_acc[...].astype(dk_ref.dtype)
        dv_ref[...] = dv_acc[...].astype(dv_ref.dtype)

    def body(*refs):
        step = pl.program_id(0) * n_steps + pl.program_id(1)
        n_c, n_o = len(carry.arrays), len(carry.out_shapes)
        own = refs[:5] + refs[5 + n_c:8 + n_c] + refs[8 + n_c + n_o:10 + n_c + n_o]
        carry.run(refs, 5, 3, step, b_sz * n_steps, lambda: compute(*own))

    blk = lambda off: pl.BlockSpec((None, s_len, ch * LANES), lambda b, p: (b, 0, off + p))
    once = lambda off: pl.BlockSpec((None, s_len, ch * LANES), lambda b, p: (b, 0, off + p),
                                    pipeline_mode=pl.Buffered(1))
    out_sd = jax.ShapeDtypeStruct((b_sz, s_len, SB_WIDTH), BF16)
    c_in, c_out, c_shapes, c_alias, c_sems = carry.call_args(5, 3)
    res = pl.pallas_call(
        body, name="sb_bwd", grid=(b_sz, n_steps),
        in_specs=[once(0), once(n_steps), once(2 * n_steps), once(0), once(0)] + c_in,
        out_specs=[blk(0), blk(0), blk(0)] + c_out, out_shape=[out_sd, out_sd, out_sd] + c_shapes,
        input_output_aliases=c_alias,
        scratch_shapes=[pltpu.VMEM((s_len, ch * LANES), F32), pltpu.VMEM((s_len, ch * LANES), F32)] + c_sems,
        compiler_params=pltpu.CompilerParams(dimension_semantics=("arbitrary", "arbitrary"),
                                             vmem_limit_bytes=VMEM_CAP),
    )(qkv, qkv, qkv, o_sb, do_sb, *carry.arrays)
    return res[:3], res[3:]


def _dil_consts(group, pair_idx, dilation):
    lane = lax.broadcasted_iota(jnp.int32, (BLK, LANES), 1)
    head0 = lane < HEAD_DIM
    row = lax.broadcasted_iota(jnp.int32, (2 * BLK, BLK), 0)
    qa = row % BLK
    kb = lax.broadcasted_iota(jnp.int32, (2 * BLK, BLK), 1)
    head = (group * DIL_HEADS_PER_GROUP + 2 * pair_idx + row // BLK).astype(F32)
    slope = jnp.exp((-ALIBI_MAX_BIAS * math.log(2.0) / DIL_HEADS) * (head + 1.0))
    valid_cur = kb <= qa
    valid_prev = kb >= qa
    bias_cur = -slope * ((qa - kb) * dilation).astype(F32)
    bias_prev = -slope * ((BLK + qa - kb) * dilation).astype(F32)
    return head0, valid_cur, valid_prev, bias_cur, bias_prev


def _dil_units(s_len, dilation):
    nb = s_len // dilation // BLK
    return [(r, n) for r in range(dilation) for n in range(nb)]


def _dil_rows(n, r, dilation):
    if dilation == 1:
        return pl.ds(n * BLK, BLK)
    return pl.ds(n * BLK * dilation + r, BLK, stride=dilation)


def _dil_scores(q01, k, bias, valid):
    s = _dot_nt(q01, k) * (1.0 / math.sqrt(HEAD_DIM)) + bias
    return jnp.where(valid, s, NEG)


def _dil_fwd(qkv, b_sz, s_len, carry):
    n_pairs = DIL_OUT_WIDTH // LANES
    q_off = 3 * SB_WIDTH // LANES
    per_kind = DIL_WIDTH // LANES

    def compute(pair_idx, qkv_refs, o_ref, lse_ref, m_s, l_s):
        m_s[...] = jnp.full(m_s.shape, NEG, F32)
        l_s[...] = jnp.zeros(l_s.shape, F32)
        o_ref[...] = jnp.zeros(o_ref.shape, F32)
        for g, (_, dilation) in enumerate(DIL_PAIRS):
            q_ref, k_ref, v_ref = qkv_refs[3 * g:3 * g + 3]
            head0, valid_cur, valid_prev, bias_cur, bias_prev = _dil_consts(g, pair_idx, dilation)
            units = _dil_units(s_len, dilation)
            for u0 in range(0, len(units), DIL_CHAINS):
                group = units[u0:u0 + DIL_CHAINS]
                rows_of = [_dil_rows(n, r, dilation) for r, n in group]
                scores, values = [], []
                for (r, n), rows in zip(group, rows_of):
                    q01 = _stack_heads(q_ref[rows, :].astype(BF16), head0)
                    sc = [_dil_scores(q01, k_ref[rows, :].astype(BF16), bias_cur, valid_cur)]
                    vals = [_stack_heads(v_ref[rows, :].astype(BF16), head0)]
                    if n > 0:
                        prev = _dil_rows(n - 1, r, dilation)
                        sc.append(_dil_scores(q01, k_ref[prev, :].astype(BF16), bias_prev, valid_prev))
                        vals.append(_stack_heads(v_ref[prev, :].astype(BF16), head0))
                    scores.append(sc)
                    values.append(vals)
                stats = []
                for sc, rows in zip(scores, rows_of):
                    m_blk = functools.reduce(jnp.maximum, [jnp.max(x, axis=-1, keepdims=True) for x in sc])
                    m_old = jnp.concatenate([m_s.at[0][rows, :], m_s.at[1][rows, :]], axis=0)
                    l_old = jnp.concatenate([l_s.at[0][rows, :], l_s.at[1][rows, :]], axis=0)
                    m_new = jnp.maximum(m_old, m_blk)
                    probs = [jnp.exp(x - m_new) for x in sc]
                    l_blk = functools.reduce(jnp.add, [jnp.sum(p, axis=-1, keepdims=True) for p in probs])
                    alpha = jnp.exp(m_old - m_new)
                    stats.append((m_new, alpha * l_old + l_blk, alpha, probs))
                for (m_new, l_new, alpha, probs), vals, rows in zip(stats, values, rows_of):
                    alpha_tok = jnp.where(head0, alpha[:BLK], alpha[BLK:])
                    p_cat = jnp.concatenate(
                        [h for p in probs for h in (p[:BLK].astype(BF16), p[BLK:].astype(BF16))], axis=1)
                    o_ref[rows, :] = alpha_tok * o_ref[rows, :] + _dot(p_cat, jnp.concatenate(vals, axis=0))
                    m_s.at[0][rows, :] = m_new[:BLK]
                    m_s.at[1][rows, :] = m_new[BLK:]
                    l_s.at[0][rows, :] = l_new[:BLK]
                    l_s.at[1][rows, :] = l_new[BLK:]
        lane = lax.broadcasted_iota(jnp.int32, (BLK, LANES), 1)
        for c in range(s_len // BLK):
            rows = pl.ds(c * BLK, BLK)
            l0, l1 = l_s.at[0][rows, :], l_s.at[1][rows, :]
            o_ref[rows, :] = o_ref[rows, :] / jnp.where(lane < HEAD_DIM, l0, l1)
            lse_ref.at[0][rows, :] = m_s.at[0][rows, :] + jnp.log(l0)
            lse_ref.at[1][rows, :] = m_s.at[1][rows, :] + jnp.log(l1)

    def body(*refs):
        pair_idx = pl.program_id(1)
        step = pl.program_id(0) * n_pairs + pair_idx
        n_c, n_o = len(carry.arrays), len(carry.out_shapes)
        o_ref, lse_ref = refs[9 + n_c:11 + n_c]
        m_s, l_s = refs[11 + n_c + n_o:13 + n_c + n_o]
        carry.run(refs, 9, 2, step, b_sz * n_pairs, lambda: compute(pair_idx, refs[:9], o_ref, lse_ref, m_s, l_s))

    in_specs = []
    for g in range(len(DIL_PAIRS)):
        for kind in range(3):
            off = q_off + kind * per_kind + g * n_pairs
            in_specs.append(pl.BlockSpec((None, s_len, LANES), lambda b, p, off=off: (b, 0, off + p)))
    c_in, c_out, c_shapes, c_alias, c_sems = carry.call_args(9, 2)
    res = pl.pallas_call(
        body, name="dil_fwd", grid=(b_sz, n_pairs),
        in_specs=in_specs + c_in,
        out_specs=[pl.BlockSpec((None, s_len, LANES), lambda b, p: (b, 0, p)),
                   pl.BlockSpec((None, None, 2, s_len, LANES), lambda b, p: (b, p, 0, 0, 0))] + c_out,
        out_shape=[jax.ShapeDtypeStruct((b_sz, s_len, DIL_OUT_WIDTH), F32),
                   jax.ShapeDtypeStruct((b_sz, n_pairs, 2, s_len, LANES), F32)] + c_shapes,
        input_output_aliases=c_alias,
        scratch_shapes=[pltpu.VMEM((2, s_len, LANES), F32), pltpu.VMEM((2, s_len, LANES), F32)] + c_sems,
        compiler_params=pltpu.CompilerParams(dimension_semantics=("arbitrary", "arbitrary"),
                                             vmem_limit_bytes=VMEM_CAP),
    )(*([qkv] * 9), *carry.arrays)
    return res[0], res[1], res[2:]


def _dil_bwd(qkv, o_dl, lse, do_dl, b_sz, s_len, carry):
    n_pairs = DIL_OUT_WIDTH // LANES
    n_groups = len(DIL_PAIRS)
    q_off = 3 * SB_WIDTH // LANES
    per_kind = DIL_WIDTH // LANES

    def compute(pair_idx, group, q_ref, k_ref, v_ref, o_ref, lse_ref, do_ref, dq_ref, dk_ref, dv_ref, d_s, dq_s, dk_s, dv_s):
        lrow = lax.broadcasted_iota(jnp.int32, (LANES, LANES), 0)
        ones_h0 = jnp.where(lrow < HEAD_DIM, 1.0, 0.0).astype(BF16)
        ones_h1 = jnp.where(lrow >= HEAD_DIM, 1.0, 0.0).astype(BF16)
        for c in range(s_len // BLK):
            rows = pl.ds(c * BLK, BLK)
            dd_hi, dd_lo = _split2(do_ref[rows, :] * o_ref[rows, :])
            d_s.at[0][rows, :] = _dot(dd_hi, ones_h0) + _dot(dd_lo, ones_h0)
            d_s.at[1][rows, :] = _dot(dd_hi, ones_h1) + _dot(dd_lo, ones_h1)
        dk_s[...] = jnp.zeros(dk_s.shape, F32)
        dv_s[...] = jnp.zeros(dv_s.shape, F32)

        def one_group(g, dilation):
            head0, valid_cur, valid_prev, bias_cur, bias_prev = _dil_consts(g, pair_idx, dilation)
            units = _dil_units(s_len, dilation)
            scale = 1.0 / math.sqrt(HEAD_DIM)
            for u0 in range(0, len(units), DIL_CHAINS):
                chunk = units[u0:u0 + DIL_CHAINS]
                loaded = []
                for r, n in chunk:
                    rows = _dil_rows(n, r, dilation)
                    q01 = _stack_heads(q_ref[rows, :].astype(BF16), head0)
                    do01 = _stack_heads(do_ref[rows, :].astype(BF16), head0)
                    lse01 = jnp.concatenate([lse_ref.at[0][rows, :], lse_ref.at[1][rows, :]], axis=0)
                    d01 = jnp.concatenate([d_s.at[0][rows, :], d_s.at[1][rows, :]], axis=0)
                    blocks = [(rows, bias_cur, valid_cur)]
                    if n > 0:
                        blocks.append((_dil_rows(n - 1, r, dilation), bias_prev, valid_prev))
                    parts = []
                    for krows, bias, valid in blocks:
                        k = k_ref[krows, :].astype(BF16)
                        v = v_ref[krows, :].astype(BF16)
                        parts.append((krows, k, _dil_scores(q01, k, bias, valid), _dot_nt(do01, v)))
                    loaded.append((rows, q01, do01, lse01, d01, parts))
                grads = []
                for rows, q01, do01, lse01, d01, parts in loaded:
                    for krows, k, sc, dp in parts:
                        p = jnp.exp(sc - lse01)
                        grads.append((p.astype(BF16), (p * (dp - d01) * scale).astype(BF16)))
                it = iter(grads)
                updates = []
                for rows, q01, do01, lse01, d01, parts in loaded:
                    dq = jnp.zeros((2 * BLK, LANES), F32)
                    for krows, k, sc, dp in parts:
                        p_b, ds = next(it)
                        dq = dq + _dot(ds, k)
                        updates.append((krows, _dot_tn(ds, q01), _dot_tn(p_b, do01)))
                    dq_s[rows, :] = jnp.where(head0, dq[:BLK], dq[BLK:])
                for krows, dk, dv in updates:
                    dk_s[krows, :] = dk_s[krows, :] + dk
                    dv_s[krows, :] = dv_s[krows, :] + dv

        for g, (_, dilation) in enumerate(DIL_PAIRS):
            pl.when(group == g)(functools.partial(one_group, g, dilation))
        dq_ref[...] = dq_s[...].astype(dq_ref.dtype)
        dk_ref[...] = dk_s[...].astype(dk_ref.dtype)
        dv_ref[...] = dv_s[...].astype(dv_ref.dtype)

    def body(*refs):
        pair_idx, group = pl.program_id(1), pl.program_id(2)
        step = (pl.program_id(0) * n_pairs + pair_idx) * n_groups + group
        n_c, n_o = len(carry.arrays), len(carry.out_shapes)
        own = refs[:6] + refs[6 + n_c:9 + n_c] + refs[9 + n_c + n_o:13 + n_c + n_o]
        carry.run(refs, 6, 3, step, b_sz * n_pairs * n_groups, lambda: compute(pair_idx, group, *own))

    def qkv_spec(kind):
        return pl.BlockSpec((None, s_len, LANES),
                            lambda b, p, g: (b, 0, q_off + kind * per_kind + g * n_pairs + p))

    tok_spec = pl.BlockSpec((None, s_len, LANES), lambda b, p, g: (b, 0, p))
    out_spec = pl.BlockSpec((None, s_len, LANES), lambda b, p, g: (b, 0, g * n_pairs + p))
    out_sd = jax.ShapeDtypeStruct((b_sz, s_len, DIL_WIDTH), BF16)
    c_in, c_out, c_shapes, c_alias, c_sems = carry.call_args(6, 3)
    res = pl.pallas_call(
        body, name="dil_bwd", grid=(b_sz, n_pairs, n_groups),
        in_specs=[qkv_spec(0), qkv_spec(1), qkv_spec(2), tok_spec,
                  pl.BlockSpec((None, None, 2, s_len, LANES), lambda b, p, g: (b, p, 0, 0, 0)), tok_spec] + c_in,
        out_specs=[out_spec, out_spec, out_spec] + c_out,
        out_shape=[out_sd, out_sd, out_sd] + c_shapes,
        input_output_aliases=c_alias,
        scratch_shapes=[pltpu.VMEM((2, s_len, LANES), F32)] + [pltpu.VMEM((s_len, LANES), F32)] * 3 + c_sems,
        compiler_params=pltpu.CompilerParams(dimension_semantics=("arbitrary", "arbitrary", "arbitrary"),
                                             vmem_limit_bytes=VMEM_CAP),
    )(qkv, qkv, qkv, o_dl, lse, do_dl, *carry.arrays)
    return res[:3], res[3:]


def _mesh_pos():
    return lax.axis_index("x"), lax.axis_index("y"), lax.axis_index("c")


def _other_chips(x, y):
    return [(1 - x, y), (x, 1 - y), (1 - x, 1 - y)]


def _hbm_specs(n):
    return [pl.BlockSpec(memory_space=pl.ANY)] * n


SWAPPED = ("w_ffn_in",)


def _slot(x, y, swapped):
    return 2 * y + x if swapped else 2 * x + y


def _cast_to_slab(w, name, swapped=False):
    rows, cols = w.shape
    mine = jnp.reshape(_slot(lax.axis_index("x"), lax.axis_index("y"), swapped), (1,)).astype(jnp.int32)

    def body(idx_ref, w_ref, o_ref):
        o_ref[...] = w_ref[...].astype(BF16)

    return pl.pallas_call(
        body, name=name,
        grid_spec=pltpu.PrefetchScalarGridSpec(
            num_scalar_prefetch=1, grid=(1,),
            in_specs=[pl.BlockSpec((rows, cols), lambda i, idx: (0, 0))],
            out_specs=pl.BlockSpec((None, rows, cols), lambda i, idx: (idx[0], 0, 0))),
        out_shape=jax.ShapeDtypeStruct((N_CHIPS, rows, cols), BF16),
        compiler_params=pltpu.CompilerParams(vmem_limit_bytes=_vmem_limit(rows * cols * 6)),
    )(mine, w)


def _gather_issue(slabs, send_sems, recv_sems, swapped):
    x, y, c = _mesh_pos()
    for k, slab in enumerate(slabs):
        half = slab.shape[1] // 2
        rows = slab.at[_slot(x, y, swapped[k]), pl.ds(c * half, half), :]
        for r, (px, py) in enumerate(_other_chips(x, y)):
            pltpu.make_async_remote_copy(
                src_ref=rows, dst_ref=rows, send_sem=send_sems.at[6 * k + r], recv_sem=recv_sems.at[6 * k + r],
                device_id=(px, py, c), device_id_type=MESH).start()


def _gather_complete(slabs, send_sems, recv_sems, swapped):
    x, y, c = _mesh_pos()
    chips = _other_chips(x, y)

    def copy(k, sem, block, rows, to):
        ref = slabs[k].at[block, rows, :]
        return pltpu.make_async_remote_copy(
            src_ref=ref, dst_ref=ref, send_sem=send_sems.at[sem], recv_sem=recv_sems.at[sem],
            device_id=to, device_id_type=MESH)

    for k, slab in enumerate(slabs):
        half = slab.shape[1] // 2
        for r, (px, py) in enumerate(chips):
            theirs = _slot(px, py, swapped[k])
            copy(k, 6 * k + r, theirs, pl.ds(c * half, half), (px, py, c)).wait_recv()
            copy(k, 6 * k + 3 + r, theirs, pl.ds(c * half, half), (x, y, 1 - c)).start()
    for k, slab in enumerate(slabs):
        half = slab.shape[1] // 2
        for r, (px, py) in enumerate(chips):
            copy(k, 6 * k + 3 + r, _slot(px, py, swapped[k]), pl.ds((1 - c) * half, half), (x, y, 1 - c)).wait_recv()
    for k, slab in enumerate(slabs):
        half = slab.shape[1] // 2
        for r, (px, py) in enumerate(chips):
            copy(k, 6 * k + r, _slot(x, y, swapped[k]), pl.ds(c * half, half), (px, py, c)).wait_send()
            copy(k, 6 * k + 3 + r, _slot(px, py, swapped[k]), pl.ds(c * half, half), (x, y, 1 - c)).wait_send()


def _gather_sems(n):
    return [pltpu.SemaphoreType.DMA((6 * n,)), pltpu.SemaphoreType.DMA((6 * n,))]


def _gather_carry(slabs, names):
    swapped = [k in SWAPPED for k in names]
    return _Carry(slabs, [jax.ShapeDtypeStruct(a.shape, a.dtype) for a in slabs], True, _gather_sems(len(slabs)),
                  lambda ins, outs, sems: _gather_issue(outs, *sems, swapped),
                  lambda ins, outs, sems: _gather_complete(outs, *sems, swapped))


def _pair_copies(ins, outs, send_sems, recv_sems):
    x, y, c = _mesh_pos()
    copies = []
    for k, g in enumerate(ins):
        half = g.shape[1] // 2
        copies.append(pltpu.make_async_remote_copy(
            src_ref=g.at[:, pl.ds((1 - c) * half, half), :], dst_ref=outs[k],
            send_sem=send_sems.at[k], recv_sem=recv_sems.at[k],
            device_id=(x, y, 1 - c), device_id_type=MESH))
    return copies


def _pair_carry(grads):
    n = len(grads)

    def start(ins, outs, sems):
        for cp in _pair_copies(ins, outs, *sems):
            cp.start()

    def finish(ins, outs, sems):
        for cp in _pair_copies(ins, outs, *sems):
            cp.wait()

    return _Carry(grads, [jax.ShapeDtypeStruct((N_CHIPS, g.shape[1] // 2, g.shape[2]), g.dtype) for g in grads], False,
                  [pltpu.SemaphoreType.DMA((n,)), pltpu.SemaphoreType.DMA((n,))], start, finish)


def _pair_exchange(grads, tag):
    carry = _pair_carry(grads)
    n = len(grads)

    def body(*refs):
        carry.start(refs[:n], refs[n:2 * n], refs[2 * n:])
        carry.finish(refs[:n], refs[n:2 * n], refs[2 * n:])

    return pl.pallas_call(
        body, name="grad_pair_exchange_" + tag, in_specs=_hbm_specs(n), out_specs=_hbm_specs(n),
        out_shape=carry.out_shapes, scratch_shapes=carry.sems,
    )(*grads)


def _pair_sum(grad, other, name, swapped):
    _, rows, cols = grad.shape
    half = rows // 2
    x, y, c = _mesh_pos()
    idx = jnp.stack([c, _slot(x, y, swapped)]).astype(jnp.int32)

    def body(idx_ref, g_ref, p_ref, own_ref, sb_ref):
        s = g_ref[...] + p_ref[...].astype(F32)
        sb_ref[...] = s.astype(BF16)

        @pl.when(pl.program_id(0) == idx_ref[1])
        def _():
            own_ref[...] = s

    blk = pl.BlockSpec((None, half, cols), lambda p, idx: (p, 0, 0))
    return pl.pallas_call(
        body, name=name,
        grid_spec=pltpu.PrefetchScalarGridSpec(
            num_scalar_prefetch=1, grid=(N_CHIPS,),
            in_specs=[pl.BlockSpec((None, half, cols), lambda p, idx: (p, idx[0], 0)), blk],
            out_specs=[pl.BlockSpec((half, cols), lambda p, idx: (0, 0)), blk]),
        out_shape=[jax.ShapeDtypeStruct((half, cols), F32), jax.ShapeDtypeStruct((N_CHIPS, half, cols), BF16)],
        compiler_params=pltpu.CompilerParams(dimension_semantics=("arbitrary",),
                                             vmem_limit_bytes=_vmem_limit(4 * half * cols * 4)),
    )(idx, grad, other)


def _chip_copies(sums_bf16, lands, send_sems, recv_sems, swapped):
    x, y, c = _mesh_pos()
    return [pltpu.make_async_remote_copy(
        src_ref=sums_bf16[k].at[_slot(px, py, swapped[k])], dst_ref=lands[k].at[r],
        send_sem=send_sems.at[3 * k + r], recv_sem=recv_sems.at[3 * k + r],
        device_id=(px, py, c), device_id_type=MESH)
        for k in range(len(sums_bf16)) for r, (px, py) in enumerate(_other_chips(x, y))]


def _chip_carry(sums_bf16, names):
    swapped = [k in SWAPPED for k in names]

    def start(ins, outs, sems):
        for cp in _chip_copies(ins, outs, *sems, swapped):
            cp.start()

    def finish(ins, outs, sems):
        for cp in _chip_copies(ins, outs, *sems, swapped):
            cp.wait()

    return _Carry(sums_bf16, _chip_landing(sums_bf16), False, _chip_sems(len(sums_bf16)), start, finish)


def _chip_sems(n):
    return [pltpu.SemaphoreType.DMA((3 * n,)), pltpu.SemaphoreType.DMA((3 * n,))]


def _chip_landing(sums_bf16):
    return [jax.ShapeDtypeStruct((N_CHIPS - 1,) + s.shape[1:], BF16) for s in sums_bf16]


def _chip_sum(own, landed, name):
    rows, cols = own.shape
    core = jnp.reshape(lax.axis_index("c"), (1,)).astype(jnp.int32)

    def body(core_ref, o_ref, l_ref, out_ref):
        out_ref[...] = ((o_ref[...] + l_ref[0].astype(F32)) + l_ref[1].astype(F32)) + l_ref[2].astype(F32)

    return pl.pallas_call(
        body, name=name,
        grid_spec=pltpu.PrefetchScalarGridSpec(
            num_scalar_prefetch=1, grid=(1,),
            in_specs=[pl.BlockSpec((rows, cols), lambda i, core_ref: (0, 0)),
                      pl.BlockSpec((N_CHIPS - 1, rows, cols), lambda i, core_ref: (0, 0, 0))],
            out_specs=pl.BlockSpec((rows, cols), lambda i, core_ref: (core_ref[0], 0))),
        out_shape=jax.ShapeDtypeStruct((2 * rows, cols), F32),
        compiler_params=pltpu.CompilerParams(vmem_limit_bytes=_vmem_limit(3 * rows * cols * 4)),
    )(core, own, landed)


def _final_exchange(fulls, v):
    n = len(fulls)
    rows, cols = v.shape
    n_dev = 8

    def body(*refs):
        v_ref, out_ref = refs[0], refs[1 + 2 * n]
        outs = refs[1 + n:1 + 2 * n]
        buf, v_send, v_recv, h_send, h_recv = refs[2 + 2 * n:]
        x, y, c = _mesh_pos()
        me = 4 * x + 2 * y + c
        buf[me] = v_ref[...]
        peers = [(1 - x if r & 4 else x, 1 - y if r & 2 else y, 1 - c if r & 1 else c) for r in range(1, n_dev)]
        copies = []
        for r, peer in enumerate(peers):
            copies.append(pltpu.make_async_remote_copy(
                src_ref=v_ref, dst_ref=buf.at[me], send_sem=v_send.at[r], recv_sem=v_recv.at[r],
                device_id=peer, device_id_type=MESH))
        for k in range(n):
            half = fulls[k].shape[0] // 2
            mine = outs[k].at[pl.ds(c * half, half), :]
            copies.append(pltpu.make_async_remote_copy(
                src_ref=mine, dst_ref=mine, send_sem=h_send.at[k], recv_sem=h_recv.at[k],
                device_id=(x, y, 1 - c), device_id_type=MESH))
        for cp in copies:
            cp.start()
        for r, (px, py, pc) in enumerate(peers):
            pltpu.make_async_remote_copy(
                src_ref=v_ref, dst_ref=buf.at[4 * px + 2 * py + pc], send_sem=v_send.at[r], recv_sem=v_recv.at[r],
                device_id=(px, py, pc), device_id_type=MESH).wait_recv()
        for k in range(n):
            half = fulls[k].shape[0] // 2
            theirs = outs[k].at[pl.ds((1 - c) * half, half), :]
            pltpu.make_async_remote_copy(
                src_ref=theirs, dst_ref=theirs, send_sem=h_send.at[k], recv_sem=h_recv.at[k],
                device_id=(x, y, 1 - c), device_id_type=MESH).wait_recv()
        for cp in copies:
            cp.wait_send()
        acc = buf[0]
        for d in range(1, n_dev):
            acc = acc + buf[d]
        out_ref[...] = acc
        out_ref[3:4, :] = jnp.broadcast_to(jnp.sum(acc[3:4, :], axis=1, keepdims=True), (1, cols))

    vm = pl.BlockSpec(memory_space=pltpu.VMEM)
    res = pl.pallas_call(
        body, name="final_exchange",
        in_specs=[vm] + _hbm_specs(n), out_specs=_hbm_specs(n) + [vm],
        out_shape=[jax.ShapeDtypeStruct(f.shape, F32) for f in fulls] + [jax.ShapeDtypeStruct((rows, cols), F32)],
        input_output_aliases={1 + k: k for k in range(n)},
        scratch_shapes=[pltpu.VMEM((n_dev, rows, cols), F32),
                        pltpu.SemaphoreType.DMA((n_dev - 1,)), pltpu.SemaphoreType.DMA((n_dev - 1,)),
                        pltpu.SemaphoreType.DMA((n,)), pltpu.SemaphoreType.DMA((n,))],
    )(v, *fulls)
    return res[:n], res[n]


def _adamw_math(w, g, m, v):
    m = ADAM_B1 * m + (1.0 - ADAM_B1) * g
    v = ADAM_B2 * v + (1.0 - ADAM_B2) * (g * g)
    m_hat = m / (1.0 - ADAM_B1 ** ADAM_STEP)
    v_hat = v / (1.0 - ADAM_B2 ** ADAM_STEP)
    delta = -ADAM_LR * (m_hat / (jnp.sqrt(v_hat) + ADAM_EPS) + ADAM_WD * w)
    return delta, m, v


def _adamw(w, g, m, v, name):
    rows, cols = w.shape
    tm = rows // 2 if (rows // 2) % 8 == 0 else rows
    return _rowwise(_adamw_math, [w, g, m, v], [], [(cols, F32)] * 3, [], tm=tm, name=name)


def _unshard_cols(gathered):
    n, r, c = gathered.shape
    return jnp.transpose(gathered, (1, 0, 2)).reshape(r, n * c)


def _shard_cols(full):
    r, nc = full.shape
    return jnp.transpose(full.reshape(r, N_CHIPS, nc // N_CHIPS), (1, 0, 2))


LATE = ["w_sb_up", "w_dil_up", "w_out", "w_ffn_in", "w_ffn_out"]


def _late_weights(slabs, d_model, d_ff):
    g = dict(zip(LATE, slabs))
    return (_unshard_cols(g["w_sb_up"]), _unshard_cols(g["w_dil_up"]), g["w_out"].reshape(d_model, d_model),
            _unshard_cols(g["w_ffn_in"]), g["w_ffn_out"].reshape(d_ff, d_model))


ROW_SHARDED = ("w_in", "w_out", "w_ffn_in", "w_ffn_out")


def _chip_major(grads):
    out = []
    for k, g in grads.items():
        if k in ROW_SHARDED:
            out.append(g.reshape(N_CHIPS, g.shape[0] // N_CHIPS, g.shape[1]))
        else:
            out.append(_shard_cols(g))
    return out


def _pair_sums(full, others, names):
    return [_pair_sum(g, o, "grad_pair_sum_" + k, k in SWAPPED) for g, o, k in zip(full, others, names)]


def _chip_sums(pair, landed, names):
    return {k: _chip_sum(p[0], l, "grad_chip_sum_" + k) for p, l, k in zip(pair, landed, names)}


def _fwd_bwd(x, loss_target, g_mix, g_ffn, g_fin, slab_in, late_slabs):
    b_sz, s_len, d_model = x.shape
    t = b_sz * s_len
    d_ff = late_slabs[-1].shape[1] * N_CHIPS
    x2d = x.reshape(t, d_model)
    tgt2d = loss_target.reshape(t, d_model)

    u, (slab_in,) = _rowwise(lambda xv, g: (_rms_stats(xv)[0] * g,), [x2d], [g_mix], [(d_model, BF16)], [], tm=512,
                             name="norm_mix", carry=_gather_carry([slab_in], ["w_in"]))
    wt_in = slab_in.reshape(-1, d_model)
    qkv, (slab_ffn_out,) = _mm(u, wt_in, tb=True, b_cols=(0, QKV_WIDTH), tm=2048, tn=768, tk=d_model, name="proj_qkv",
                               carry=_gather_carry(late_slabs[4:], LATE[4:]))
    gates = _mm(u, wt_in, tb=True, b_cols=(QKV_WIDTH, 2 * d_model), out_dtype=BF16, tm=t, tn=256, tk=d_model,
                name="proj_gates")
    qkv3 = qkv.reshape(b_sz, s_len, QKV_WIDTH)
    o_sb, (slab_ffn_in,) = _sb_fwd(qkv3, b_sz, s_len, _gather_carry(late_slabs[3:4], LATE[3:4]))
    o_dl, lse, small_slabs = _dil_fwd(qkv3, b_sz, s_len, _gather_carry(late_slabs[:3], LATE[:3]))
    wf_sb_up, wf_dil_up, wf_out, wf_ffn_in, wf_ffn_out = _late_weights(
        list(small_slabs) + [slab_ffn_in, slab_ffn_out], d_model, d_ff)
    o_sb2, o_dl2 = o_sb.reshape(t, SB_WIDTH), o_dl.reshape(t, DIL_OUT_WIDTH)
    y_sb = _mm(o_sb2, wf_sb_up, out_dtype=BF16, tm=1024, tn=1024, tk=SB_WIDTH, name="sb_up")
    y_dl = _mm(o_dl2, wf_dil_up, out_dtype=BF16, tm=1024, tn=1024, tk=DIL_OUT_WIDTH, name="dil_up")

    def merge_fn(gt, ys, yd):
        return (_sigmoid(gt[:, :d_model]) * ys + _sigmoid(gt[:, d_model:]) * yd,)

    (merged,) = _rowwise(merge_fn, [gates, y_sb, y_dl], [], [(d_model, BF16)], [], tm=512, name="merge")
    x1 = _mm(merged, wf_out, add=x2d, tm=512, tn=1024, tk=d_model, name="mix_out")
    (u2,) = _rowwise(lambda xv, g: (_rms_stats(xv)[0] * g,), [x1], [g_ffn], [(d_model, BF16)], [], tm=512, name="norm_ffn")
    half_ff = d_ff // 2

    def act_fn(hv):
        gate = hv[:, :half_ff]
        return hv, gate * _sigmoid(gate) * hv[:, half_ff:]

    h, act = _mm(u2, wf_ffn_in, tm=512, tn=d_ff, tk=d_model, name="ffn_in",
                 epilogue=(act_fn, [], [], [(d_ff, BF16), (half_ff, BF16)], []))
    def head_fn(xv, tg, g):
        xhat, r = _rms_stats(xv)
        err = xhat * g - tg
        dy = err * (1.0 / d_model)
        dx, dg_rows = _rms_bwd(dy, xhat, r, g)
        loss_lanes = (0.5 / d_model) * jnp.sum(err * err, axis=0, keepdims=True)
        return dx, dx, jnp.sum(dg_rows, axis=0, keepdims=True), loss_lanes

    dx2, dx2_b, dg_fin, loss_lanes = _mm(
        act, wf_ffn_out, add=x1, tm=512, tn=1024, tk=d_ff, name="ffn_out",
        epilogue=(head_fn, [tgt2d], [g_fin], [(d_model, F32), (d_model, BF16)], [(1, d_model), (1, d_model)]))

    def dact_fn(da, hv):
        gate, up = hv[:, :half_ff], hv[:, half_ff:]
        sg = _sigmoid(gate)
        dgate = da * up * (sg * (1.0 + gate * (1.0 - sg)))
        return (jnp.concatenate([dgate, da * (gate * sg)], axis=1),)

    (dh,) = _mm(dx2_b, wf_ffn_out, tb=True, tm=512, tn=half_ff, tk=d_model, name="ffn_out_dx",
                epilogue=(dact_fn, [h], [], [(d_ff, BF16)], []))
    gw_ffn_out = _mm(act, dx2_b, ta=True, tm=256, tn=d_model, tk=t, name="ffn_out_dw")
    def norm_bwd_fn(du_, dres, xv, g):
        xhat, r = _rms_stats(xv)
        dx, dg_rows = _rms_bwd(du_, xhat, r, g)
        return dres + dx, jnp.sum(dg_rows, axis=0, keepdims=True)

    def norm_bwd_twice(*args):
        dx, dg = norm_bwd_fn(*args)
        return dx, dx, dg

    dx1, dx1_b, dg_ffn = _mm(dh, wf_ffn_in, tb=True, tm=512, tn=1024, tk=2 * d_ff, name="ffn_in_dx",
                             epilogue=(norm_bwd_twice, [dx2, x1], [g_ffn], [(d_model, F32), (d_model, BF16)], [(1, d_model)]))
    gwt_ffn_in = _mm(dh, u2, ta=True, tm=512, tn=d_model, tk=t, name="ffn_in_dw")

    dmerged = _mm(dx1_b, wf_out, tb=True, out_dtype=BF16, tm=512, tn=1024, tk=d_model, name="mix_out_dx")
    gw_out = _mm(merged, dx1_b, ta=True, tm=256, tn=d_model, tk=t, name="mix_out_dw")

    def merge_bwd_fn(gt, ys, yd, dm):
        s_sb, s_dl = _sigmoid(gt[:, :d_model]), _sigmoid(gt[:, d_model:])
        dgates = jnp.concatenate([dm * ys * s_sb * (1.0 - s_sb), dm * yd * s_dl * (1.0 - s_dl)], axis=1)
        return dgates, dm * s_sb, dm * s_dl

    full_big = _chip_major({"w_out": gw_out, "w_ffn_in": gwt_ffn_in, "w_ffn_out": gw_ffn_out})
    dgates, dy_sb, dy_dl, others_big = _rowwise(
        merge_bwd_fn, [gates, y_sb, y_dl, dmerged], [], [(2 * d_model, BF16), (d_model, BF16), (d_model, BF16)], [],
        tm=256, name="merge_bwd", carry=_pair_carry(full_big))
    pair_big = _pair_sums(full_big, others_big, LATE[2:])
    do_sb = _mm(dy_sb, wf_sb_up, tb=True, out_dtype=BF16, tm=1024, tn=SB_WIDTH, tk=d_model, name="sb_up_dx")
    gw_sb_up = _mm(o_sb2, dy_sb, ta=True, tm=SB_WIDTH, tn=1024, tk=512, name="sb_up_dw")
    do_dl = _mm(dy_dl, wf_dil_up, tb=True, tm=1024, tn=DIL_OUT_WIDTH, tk=d_model, name="dil_up_dx")
    gw_dil_up = _mm(o_dl2, dy_dl, ta=True, tm=DIL_OUT_WIDTH, tn=1024, tk=512, name="dil_up_dw")
    full_small = _chip_major({"w_sb_up": gw_sb_up, "w_dil_up": gw_dil_up})
    (dq_sb, dk_sb, dv_sb), brought = _sb_bwd(
        qkv3, o_sb, do_sb.reshape(b_sz, s_len, SB_WIDTH), b_sz, s_len,
        _chip_carry([p[1] for p in pair_big[:2]], LATE[2:4]) + _pair_carry(full_small))
    pair_small = _pair_sums(full_small, brought[2:], LATE[:2])
    (dq_dl, dk_dl, dv_dl), landed_b = _dil_bwd(
        qkv3, o_dl, lse, do_dl.reshape(b_sz, s_len, DIL_OUT_WIDTH), b_sz, s_len,
        _chip_carry([pair_big[2][1], pair_small[0][1], pair_small[1][1]], [LATE[4], LATE[0], LATE[1]]))
    pair = pair_small + pair_big
    landed = [landed_b[1], landed_b[2], brought[0], brought[1], landed_b[0]]
    dproj = jnp.concatenate(
        [a.reshape(t, -1) for a in (dq_sb, dk_sb, dv_sb)]
        + [a.reshape(t, -1) for a in (dq_dl, dk_dl, dv_dl)] + [dgates], axis=1)
    gwt_in, gwt_in_b = _mm(dproj, u, ta=True, tm=256, tn=d_model, tk=t, name="proj_dw",
                           epilogue=(lambda tile: (tile, tile), [], [], [(d_model, F32), (d_model, BF16)], []))
    full_in = _chip_major({"w_in": gwt_in})
    pair_in = _pair_sums(full_in, _pair_exchange(_chip_major({"w_in": gwt_in_b}), "w_in"), ["w_in"])
    (dx, dg_mix), landed_in = _mm(
        dproj, wt_in, tm=512, tn=1024, tk=wt_in.shape[0], name="proj_dx",
        carry=_chip_carry([p[1] for p in pair_in], ["w_in"]),
        epilogue=(norm_bwd_fn, [dx1, x2d], [g_mix], [(d_model, F32)], [(1, d_model)]))

    grads = _chip_sums(pair, landed, LATE)
    grads.update(_chip_sums(pair_in, landed_in, ["w_in"]))
    return dx, grads, dg_mix, dg_ffn, dg_fin, loss_lanes


def kernel(x, norm_mix_g, w_in, w_sb_up, w_dil_up, w_out, norm_ffn_g, w_ffn_in, w_ffn_out, norm_final_g, loss_target, m_norm_mix_g, m_w_in, m_w_sb_up, m_w_dil_up, m_w_out, m_norm_ffn_g, m_w_ffn_in, m_w_ffn_out, m_norm_final_g, v_norm_mix_g, v_w_in, v_w_sb_up, v_w_dil_up, v_w_out, v_norm_ffn_g, v_w_ffn_in, v_w_ffn_out, v_norm_final_g):
    b_sz, s_len, d_model = x.shape
    d_ff = w_ffn_out.shape[1] * N_CHIPS
    g_mix, g_ffn, g_fin = norm_mix_g, norm_ffn_g, norm_final_g.reshape(1, d_model)

    names = ["w_in", "w_sb_up", "w_dil_up", "w_out", "w_ffn_in", "w_ffn_out"]
    shards = {"w_in": jnp.swapaxes(w_in[0], 0, 1), "w_sb_up": w_sb_up[0], "w_dil_up": w_dil_up[0], "w_out": w_out[0],
              "w_ffn_in": w_ffn_in[0], "w_ffn_out": w_ffn_out[0]}
    slab_in = _cast_to_slab(shards["w_in"], "cast_w_in")
    late_slabs = [_cast_to_slab(shards[k], "cast_" + k, k in SWAPPED) for k in LATE]

    dx, grads, dg_mix, dg_ffn, dg_fin, loss_lanes = _fwd_bwd(
        x, loss_target, g_mix, g_ffn, g_fin, slab_in, late_slabs)

    small = jnp.concatenate([dg_mix, dg_ffn, dg_fin, loss_lanes, jnp.zeros((4, d_model), F32)], axis=0)
    full_grads, small = _final_exchange([grads[k] for k in names], small)
    grads = dict(zip(names, full_grads))
    grads["w_ffn_in"] = jnp.swapaxes(grads["w_ffn_in"], 0, 1)
    loss = small[3, 0]
    gains = jnp.concatenate([g_mix, g_ffn, g_fin, jnp.zeros((5, d_model), F32)], axis=0)
    gains_m = jnp.concatenate([m_norm_mix_g, m_norm_ffn_g, m_norm_final_g.reshape(1, d_model), jnp.zeros((5, d_model), F32)], axis=0)
    gains_v = jnp.concatenate([v_norm_mix_g, v_norm_ffn_g, v_norm_final_g.reshape(1, d_model), jnp.ones((5, d_model), F32)], axis=0)
    gd, gm, gv = _rowwise(_adamw_math, [gains, small, gains_m, gains_v], [], [(d_model, F32)] * 3, [], tm=8, name="adamw_gains")

    moments = {"w_in": (jnp.swapaxes(m_w_in[0], 0, 1), jnp.swapaxes(v_w_in[0], 0, 1)),
               "w_sb_up": (m_w_sb_up[0], v_w_sb_up[0]), "w_dil_up": (m_w_dil_up[0], v_w_dil_up[0]),
               "w_out": (m_w_out[0], v_w_out[0]), "w_ffn_in": (m_w_ffn_in[0], v_w_ffn_in[0]),
               "w_ffn_out": (m_w_ffn_out[0], v_w_ffn_out[0])}
    upd = {k: _adamw(shards[k], grads[k], moments[k][0], moments[k][1], "adamw_" + k) for k in names}

    def as_output(k, a):
        return (jnp.swapaxes(a, 0, 1) if k == "w_in" else a)[None]

    def w_out_of(i):
        return [as_output(k, upd[k][i]) for k in names]

    def ordered(mix, ws, ffn_g, fin):
        return [mix, ws[0], ws[1], ws[2], ws[3], ffn_g, ws[4], ws[5], fin]

    grad_ws = [as_output(k, grads[k]) for k in names]
    outs = [loss, dx.reshape(b_sz, s_len, d_model)]
    outs += ordered(small[0:1], grad_ws, small[1:2], small[2])
    outs += ordered(gd[0:1], w_out_of(0), gd[1:2], gd[2])
    outs += ordered(gm[0:1], w_out_of(1), gm[1:2], gm[2])
    outs += ordered(gv[0:1], w_out_of(2), gv[1:2], gv[2])
    return tuple(outs)
```

```python
import functools
import math

import jax
import jax.numpy as jnp
from jax import lax
from jax.experimental import pallas as pl
from jax.experimental.pallas import tpu as pltpu

F32 = jnp.float32
BF16 = jnp.bfloat16
MESH = pl.DeviceIdType.MESH

HEAD_DIM = 64
SB_HEADS = 8
DIL_PAIRS = ((128, 1), (512, 4), (2048, 16))
DIL_HEADS_PER_GROUP = 4
DIL_HEADS = DIL_HEADS_PER_GROUP * len(DIL_PAIRS)
SB_WIDTH = SB_HEADS * HEAD_DIM
DIL_WIDTH = DIL_HEADS * HEAD_DIM
DIL_OUT_WIDTH = DIL_HEADS_PER_GROUP * HEAD_DIM
QKV_WIDTH = 3 * SB_WIDTH + 3 * DIL_WIDTH
RMS_EPS = 1e-6
ALIBI_MAX_BIAS = 8.0
ADAM_LR = 0.001
ADAM_B1 = 0.9
ADAM_B2 = 0.999
ADAM_EPS = 1e-08
ADAM_WD = 0.01
ADAM_STEP = 10

LANES = 128
BLK = 128
NEG = -1e30
EXP_UNDERFLOW = -104.0
SB_FWD_CHAINS = 4
SB_BWD_CHAINS = 4
DIL_CHAINS = 4
N_CHIPS = 4
VMEM_CAP = 56 * 1024 * 1024


def _vmem_limit(tile_bytes):
    return int(min(VMEM_CAP, max(32 * 1024 * 1024, 3 * tile_bytes + 8 * 1024 * 1024)))


def _nbytes(shape, dtype):
    return math.prod(shape) * jnp.dtype(dtype).itemsize


def _dot(a, b):
    return jnp.dot(a, b, preferred_element_type=F32)


def _dot_nt(a, b):
    return lax.dot_general(a, b, (((1,), (1,)), ((), ())), preferred_element_type=F32)


def _dot_tn(a, b):
    return lax.dot_general(a, b, (((0,), (0,)), ((), ())), preferred_element_type=F32)


def _split2(x):
    hi = x.astype(BF16)
    lo = (x - hi.astype(F32)).astype(BF16)
    return hi, lo


def _sigmoid(x):
    return pl.reciprocal(1.0 + jnp.exp(-x), approx=True)


class _Carry:
    def __init__(self, arrays=(), out_shapes=(), aliased=False, sems=(), start=None, finish=None):
        self.arrays, self.out_shapes, self.aliased = list(arrays), list(out_shapes), aliased
        self.sems, self.start, self.finish = list(sems), start, finish

    def __bool__(self):
        return bool(self.arrays)

    def __add__(self, other):
        assert not self.aliased and not other.aliased
        n_a, n_o, n_s = len(self.arrays), len(self.out_shapes), len(self.sems)
        return _Carry(
            self.arrays + other.arrays, self.out_shapes + other.out_shapes, False, self.sems + other.sems,
            lambda i, o, s: (self.start(i[:n_a], o[:n_o], s[:n_s]), other.start(i[n_a:], o[n_o:], s[n_s:])),
            lambda i, o, s: (self.finish(i[:n_a], o[:n_o], s[:n_s]), other.finish(i[n_a:], o[n_o:], s[n_s:])))

    def call_args(self, n_in, n_out):
        aliases = {n_in + k: n_out + k for k in range(len(self.arrays))} if self.aliased else {}
        return _hbm_specs(len(self.arrays)), _hbm_specs(len(self.out_shapes)), self.out_shapes, aliases, self.sems

    def run(self, refs, n_in, n_out, step, n_steps, compute):
        if not self:
            compute()
            return
        n_c, n_o, n_s = len(self.arrays), len(self.out_shapes), len(self.sems)
        ins = refs[n_in:n_in + n_c]
        outs = refs[n_in + n_c + n_out:n_in + n_c + n_out + n_o]
        sems = refs[len(refs) - n_s:]

        @pl.when(step == 0)
        def _():
            self.start(ins, outs, sems)

        compute()

        @pl.when(step == n_steps - 1)
        def _():
            self.finish(ins, outs, sems)


def _mm(a, b, *, ta=False, tb=False, add=None, out_dtype=F32, tm, tn, tk, name, carry=None, epilogue=None,
        b_cols=None):
    carry = carry or _Carry()
    n_car = len(carry.arrays)
    pieces = list(a) if isinstance(a, (list, tuple)) else [a]
    n_a = len(pieces)
    widths = [p.shape[1] for p in pieces]
    starts = [sum(widths[:p]) for p in range(n_a)]
    if ta:
        kdim, m = pieces[0].shape[0], sum(widths)
    else:
        m, kdim = pieces[0].shape[0], sum(widths)
    if tb:
        n, k2 = b.shape
    else:
        k2, n = b.shape
    col0 = 0
    if b_cols is not None:
        assert b_cols[0] % tn == 0, name
        col0, n = b_cols[0] // tn, b_cols[1]
    assert kdim == k2 and m % tm == 0 and n % tn == 0 and kdim % tk == 0, (name, a.shape, b.shape)
    nk = kdim // tk
    assert n_a == 1 or (nk == 1 and not tb and (not ta or all(w % tm == 0 for w in widths))), name
    grid = (m // tm, n // tn, nk)
    a_mode = dict(pipeline_mode=pl.Buffered(1)) if grid[0] == 1 and nk == 1 else {}
    b_mode = dict(pipeline_mode=pl.Buffered(1)) if grid[1] == 1 and nk == 1 else {}
    if n_a == 1:
        a_specs = [pl.BlockSpec((tk, tm), lambda i, j, k: (k, i), **a_mode) if ta
                   else pl.BlockSpec((tm, tk), lambda i, j, k: (i, k), **a_mode)]
    elif ta:
        a_specs = [pl.BlockSpec((tk, tm), lambda i, j, k, s=s // tm, w=w // tm: (0, jnp.clip(i - s, 0, w - 1)))
                   for s, w in zip(starts, widths)]
    else:
        a_specs = [pl.BlockSpec((tm, w), lambda i, j, k: (i, 0)) for w in widths]
    b_spec = (pl.BlockSpec((tn, tk), lambda i, j, k: (j + col0, k), **b_mode) if tb
              else pl.BlockSpec((tk, tn), lambda i, j, k: (k, j + col0), **b_mode))
    o_spec = pl.BlockSpec((tm, tn), lambda i, j, k: (i, j))
    dims = ((((0,) if ta else (1,)), ((1,) if tb else (0,))), ((), ()))
    has_add = add is not None
    if epilogue is None:
        ep_fn, ep_rows, ep_params, ep_outs, ep_accs = None, [], [], [], []
        out_sds, out_specs = [jax.ShapeDtypeStruct((m, n), out_dtype)], [o_spec]
    else:
        ep_fn, ep_rows, ep_params, ep_outs, ep_accs = epilogue
        assert grid[1] == 1 or not ep_accs, name
        out_sds = [jax.ShapeDtypeStruct((m, w * grid[1]), d) for w, d in ep_outs]
        out_sds += [jax.ShapeDtypeStruct(sh, F32) for sh in ep_accs]
        out_specs = [pl.BlockSpec((tm, w), lambda i, j, k: (i, j)) for w, _ in ep_outs]
        out_specs += [pl.BlockSpec(sh, lambda i, j, k: (0, 0)) for sh in ep_accs]
    n_main = len(out_sds)
    use_scratch = nk > 1 and (ep_fn is not None or jnp.dtype(out_dtype) != jnp.dtype(F32))
    n_in = n_a + 1 + has_add + len(ep_rows) + len(ep_params)

    def finish(total, refs, pid):
        outs = refs[n_in + n_car:n_in + n_car + n_main]
        if ep_fn is None:
            outs[0][...] = total.astype(out_dtype)
            return
        first = n_a + 1 + has_add
        rows = [r[...].astype(F32) for r in refs[first:first + len(ep_rows)]]
        params = [p[...] for p in refs[first + len(ep_rows):n_in]]
        res = ep_fn(total, *rows, *params)
        for o_ref, v in zip(outs[:len(ep_outs)], res):
            o_ref[...] = v.astype(o_ref.dtype)
        acc_refs = outs[len(ep_outs):]
        if acc_refs:
            @pl.when(pid[0] == 0)
            def _():
                for r in acc_refs:
                    r[...] = jnp.zeros(r.shape, F32)

            for r, v in zip(acc_refs, res[len(ep_outs):]):
                r[...] += v

    def compute(refs, pid):
        a_ref, b_ref = refs[0], refs[n_a]
        add_ref = refs[n_a + 1] if has_add else None

        def dot(x, y):
            return lax.dot_general(x.astype(BF16), y.astype(BF16), dims, preferred_element_type=F32)

        if n_a > 1 and ta:
            for p_ref, s, w in zip(refs[:n_a], starts, widths):
                @pl.when((pid[0] >= s // tm) & (pid[0] < (s + w) // tm))
                def _(p_ref=p_ref):
                    prod = dot(p_ref[...], b_ref[...])
                    finish(prod + add_ref[...] if has_add else prod, refs, pid)
            return
        if n_a > 1:
            prod = dot(a_ref[...], b_ref[:widths[0], :])
            for p_ref, s, w in zip(refs[1:n_a], starts[1:], widths[1:]):
                prod += dot(p_ref[...], b_ref[s:s + w, :])
        else:
            prod = dot(a_ref[...], b_ref[...])
        if nk == 1:
            finish(prod + add_ref[...] if has_add else prod, refs, pid)
            return
        acc_ref = refs[n_in + n_car + n_main + len(carry.out_shapes)] if use_scratch else refs[n_in + n_car]
        k = pid[2]

        @pl.when(k == 0)
        def _():
            acc_ref[...] = prod + add_ref[...] if has_add else prod

        @pl.when(k > 0)
        def _():
            acc_ref[...] += prod

        if use_scratch:
            @pl.when(k == nk - 1)
            def _():
                finish(acc_ref[...], refs, pid)

    def body(*refs):
        pid = (pl.program_id(0), pl.program_id(1), pl.program_id(2))
        step = (pid[0] * grid[1] + pid[1]) * nk + pid[2]
        carry.run(refs, n_in, n_main, step, grid[0] * grid[1] * nk, lambda: compute(refs, pid))

    tile_bytes = ((n_a if ta else 1) * _nbytes((tm, tk), pieces[0].dtype)
                  + _nbytes((tk, tn), b.dtype) + 2 * _nbytes((tm, tn), F32)
                  + (_nbytes((tm, tn), F32) if has_add else 0)
                  + sum(_nbytes((tm, r.shape[1]), r.dtype) for r in ep_rows) + sum(_nbytes((tm, w), d) for w, d in ep_outs))
    in_specs = a_specs + [b_spec] + ([o_spec] if has_add else [])
    in_specs += [pl.BlockSpec((tm, r.shape[1] // grid[1]), lambda i, j, k: (i, j)) for r in ep_rows]
    in_specs += [pl.BlockSpec(p.shape, lambda i, j, k: (0, 0)) for p in ep_params]
    args = tuple(pieces) + (b,) + ((add,) if has_add else ()) + tuple(ep_rows) + tuple(ep_params)
    scratch = [pltpu.VMEM((tm, tn), F32)] if use_scratch else []
    serial = bool(carry) or bool(ep_accs)
    c_in, c_out, c_shapes, c_alias, c_sems = carry.call_args(n_in, n_main)
    res = pl.pallas_call(
        body, name=name, grid=grid,
        in_specs=in_specs + c_in, out_specs=out_specs + c_out, out_shape=out_sds + c_shapes,
        input_output_aliases=c_alias, scratch_shapes=scratch + c_sems,
        compiler_params=pltpu.CompilerParams(
            dimension_semantics=("arbitrary",) * 3 if serial else ("parallel", "parallel", "arbitrary"),
            vmem_limit_bytes=_vmem_limit(tile_bytes)),
    )(*args, *carry.arrays)
    main = res[0] if ep_fn is None else list(res[:n_main])
    return (main, res[n_main:]) if carry else main


def _rowwise(fn, rows, params, outs, accs, *, tm, name, carry=None):
    carry = carry or _Carry()
    t = rows[0].shape[0]
    assert t % tm == 0, (name, t, tm)
    n_r, n_p, n_o, n_c = len(rows), len(params), len(outs), len(carry.arrays)

    def compute(refs, first):
        vals = [r[...].astype(F32) for r in refs[:n_r]] + [p[...] for p in refs[n_r:n_r + n_p]]
        res = fn(*vals)
        o_refs = refs[n_r + n_p + n_c:n_r + n_p + n_c + n_o]
        a_refs = refs[n_r + n_p + n_c + n_o:n_r + n_p + n_c + n_o + len(accs)]
        for o_ref, v in zip(o_refs, res[:n_o]):
            o_ref[...] = v.astype(o_ref.dtype)
        if accs:
            @pl.when(first)
            def _():
                for a_ref in a_refs:
                    a_ref[...] = jnp.zeros(a_ref.shape, F32)

            for a_ref, v in zip(a_refs, res[n_o:]):
                a_ref[...] += v

    def body(*refs):
        step = pl.program_id(0)
        carry.run(refs, n_r + n_p, n_o + len(accs), step, t // tm, lambda: compute(refs, step == 0))

    in_specs = [pl.BlockSpec((tm, r.shape[1]), lambda i: (i, 0)) for r in rows]
    in_specs += [pl.BlockSpec(p.shape, lambda i: (0, 0)) for p in params]
    out_specs = [pl.BlockSpec((tm, w), lambda i: (i, 0)) for w, _ in outs]
    out_specs += [pl.BlockSpec(s, lambda i: (0, 0)) for s in accs]
    out_shape = [jax.ShapeDtypeStruct((t, w), d) for w, d in outs]
    out_shape += [jax.ShapeDtypeStruct(s, F32) for s in accs]
    tile_bytes = sum(_nbytes((tm, r.shape[1]), r.dtype) for r in rows) + sum(_nbytes((tm, w), F32) for w, _ in outs)
    c_in, c_out, c_shapes, c_alias, c_sems = carry.call_args(n_r + n_p, n_o + len(accs))
    res = pl.pallas_call(
        body, name=name, grid=(t // tm,), in_specs=in_specs + c_in, out_specs=out_specs + c_out,
        out_shape=out_shape + c_shapes, input_output_aliases=c_alias, scratch_shapes=c_sems,
        compiler_params=pltpu.CompilerParams(
            dimension_semantics=("arbitrary",) if accs or carry else ("parallel",),
            vmem_limit_bytes=_vmem_limit(2 * tile_bytes)),
    )(*rows, *params, *carry.arrays)
    own = n_o + len(accs)
    return (list(res[:own]) + [res[own:]]) if carry else res


def _rms_stats(x):
    r = lax.rsqrt(jnp.mean(x * x, axis=-1, keepdims=True) + RMS_EPS)
    return x * r, r


def _rms_bwd(dy, xhat, r, g):
    dxhat = dy * g
    dx = r * (dxhat - xhat * jnp.mean(dxhat * xhat, axis=-1, keepdims=True))
    return dx, dy * xhat


def _sb_consts():
    lane = lax.broadcasted_iota(jnp.int32, (BLK, LANES), 1)
    head0 = lane < HEAD_DIM
    row = lax.broadcasted_iota(jnp.int32, (2 * BLK, BLK), 0) % BLK
    col = lax.broadcasted_iota(jnp.int32, (2 * BLK, BLK), 1)
    causal = col < row
    jj = lax.broadcasted_iota(jnp.int32, (BLK, BLK), 0)
    ss = lax.broadcasted_iota(jnp.int32, (BLK, BLK), 1)
    suffix = jnp.where(jj > ss, 1.0, 0.0).astype(BF16)
    return head0, causal, suffix


def _stack_heads(x, head0):
    zero = jnp.zeros_like(x)
    return jnp.concatenate([jnp.where(head0, x, zero), jnp.where(head0, zero, x)], axis=0)


def _sb_logits(z, causal, masked):
    sp = jnp.log(1.0 + jnp.exp(-jnp.abs(z)))
    log_keep = -(jnp.maximum(z, 0.0) + sp)
    log_beta = jnp.minimum(z, 0.0) - sp
    if masked:
        log_keep = jnp.where(causal, log_keep, 0.0)
    return log_keep, log_beta


def _suffix_sums(x, suffix):
    hi, lo = _split2(x)
    after = _dot(hi, suffix) + _dot(lo, suffix)
    total = jnp.broadcast_to(after[:, 0:1] + x[:, 0:1], x.shape)
    return after, total


def _sb_walk_back(i, state, per_chain, tile):
    def alive(st):
        worst = functools.reduce(jnp.maximum, [st[p][:, 0:1] for p in range(0, len(st), per_chain)])
        return jnp.max(worst) > EXP_UNDERFLOW

    def cond(c):
        return jnp.logical_and(c[0] < i, alive(c[1]))

    def body(c):
        return c[0] + 1, tile(i - 1 - c[0], c[1], False)

    return lax.while_loop(cond, body, (jnp.int32(0), state))[1]


def _lane_blocks(x, n):
    return [x[:, p * LANES:(p + 1) * LANES] for p in range(n)]


def _sb_fwd(qkv, b_sz, s_len, carry):
    nq = s_len // BLK
    n_pairs = SB_WIDTH // LANES
    ch = SB_FWD_CHAINS
    n_steps = n_pairs // ch
    scale = 1.0 / math.sqrt(HEAD_DIM)

    def compute(q_ref, k_ref, v_ref, o_ref):
        head0, causal, suffix = _sb_consts()

        def q_block(i, _):
            qs = pl.multiple_of(i * BLK, BLK)
            q_all = (q_ref[pl.ds(qs, BLK), :] * scale).astype(BF16)
            q01 = [_stack_heads(q, head0) for q in _lane_blocks(q_all, ch)]

            def tile(j, state, masked):
                ks = pl.multiple_of(j * BLK, BLK)
                ks_ = _lane_blocks(k_ref[pl.ds(ks, BLK), :].astype(BF16), ch)
                vs_ = _lane_blocks(v_ref[pl.ds(ks, BLK), :].astype(BF16), ch)
                zs = [_dot_nt(q01[p], ks_[p]) for p in range(ch)]
                logits = [_sb_logits(z, causal, masked) for z in zs]
                sums = [_suffix_sums(lg[0], suffix) for lg in logits]
                out = []
                for p in range(ch):
                    carry, acc = state[2 * p], state[2 * p + 1]
                    after, total = sums[p]
                    a = jnp.exp(logits[p][1] + carry + after)
                    if masked:
                        a = jnp.where(causal, a, 0.0)
                    a_hi, a_lo = _split2(a)
                    a_cat = jnp.concatenate([a_hi[:BLK], a_hi[BLK:], a_lo[:BLK], a_lo[BLK:]], axis=1)
                    v01 = _stack_heads(vs_[p], head0)
                    out += [carry + total, acc + _dot(a_cat, jnp.concatenate([v01, v01], axis=0))]
                return tuple(out)

            state = (jnp.zeros((2 * BLK, BLK), F32), jnp.zeros((BLK, LANES), F32)) * ch
            state = tile(i, state, True)
            state = _sb_walk_back(i, state, 2, tile)
            o_ref[pl.ds(qs, BLK), :] = jnp.concatenate([state[2 * p + 1] for p in range(ch)], axis=1)
            return 0

        lax.fori_loop(0, nq, q_block, 0)

    def body(*refs):
        step = pl.program_id(0) * n_steps + pl.program_id(1)
        o_ref = refs[3 + len(carry.arrays)]
        carry.run(refs, 3, 1, step, b_sz * n_steps, lambda: compute(refs[0], refs[1], refs[2], o_ref))

    blk = lambda off: pl.BlockSpec((None, s_len, ch * LANES), lambda b, p: (b, 0, off + p))
    c_in, c_out, c_shapes, c_alias, c_sems = carry.call_args(3, 1)
    res = pl.pallas_call(
        body, name="sb_fwd", grid=(b_sz, n_steps),
        in_specs=[blk(0), blk(n_steps), blk(2 * n_steps)] + c_in, out_specs=[blk(0)] + c_out,
        out_shape=[jax.ShapeDtypeStruct((b_sz, s_len, SB_WIDTH), F32)] + c_shapes,
        input_output_aliases=c_alias, scratch_shapes=c_sems,
        compiler_params=pltpu.CompilerParams(dimension_semantics=("arbitrary", "arbitrary"),
                                             vmem_limit_bytes=VMEM_CAP),
    )(qkv, qkv, qkv, *carry.arrays)
    return res[0], res[1:]


def _sb_bwd(qkv, o_sb, do_sb, b_sz, s_len, carry):
    nq = s_len // BLK
    n_pairs = SB_WIDTH // LANES
    ch = SB_BWD_CHAINS
    n_steps = n_pairs // ch
    scale = 1.0 / math.sqrt(HEAD_DIM)

    def compute(q_ref, k_ref, v_ref, o_ref, do_ref, dq_ref, dk_ref, dv_ref, dk_acc, dv_acc):
        head0, causal, suffix = _sb_consts()
        lrow = lax.broadcasted_iota(jnp.int32, (LANES, LANES), 0)
        ones_h0 = jnp.where(lrow < HEAD_DIM, 1.0, 0.0).astype(BF16)
        ones_h1 = jnp.where(lrow >= HEAD_DIM, 1.0, 0.0).astype(BF16)
        dk_acc[...] = jnp.zeros(dk_acc.shape, F32)
        dv_acc[...] = jnp.zeros(dv_acc.shape, F32)

        def q_block(i, _):
            qs = pl.multiple_of(i * BLK, BLK)
            q_all = (q_ref[pl.ds(qs, BLK), :] * scale).astype(BF16)
            do_all = do_ref[pl.ds(qs, BLK), :].astype(BF16)
            dd_all = do_all.astype(F32) * o_ref[pl.ds(qs, BLK), :]
            q01 = [_stack_heads(q, head0) for q in _lane_blocks(q_all, ch)]
            do01 = [_stack_heads(d, head0) for d in _lane_blocks(do_all, ch)]
            tot = []
            for dd in _lane_blocks(dd_all, ch):
                dd_hi, dd_lo = _split2(dd)
                tot.append(jnp.concatenate([_dot(dd_hi, ones_h0) + _dot(dd_lo, ones_h0),
                                            _dot(dd_hi, ones_h1) + _dot(dd_lo, ones_h1)], axis=0))

            def tile(j, state, masked):
                ks = pl.multiple_of(j * BLK, BLK)
                ks_ = _lane_blocks(k_ref[pl.ds(ks, BLK), :].astype(BF16), ch)
                vs_ = _lane_blocks(v_ref[pl.ds(ks, BLK), :].astype(BF16), ch)
                zs = [_dot_nt(q01[p], ks_[p]) for p in range(ch)]
                das = [_dot_nt(do01[p], vs_[p]) for p in range(ch)]
                logits = [_sb_logits(z, causal, masked) for z in zs]
                sums = [_suffix_sums(lg[0], suffix) for lg in logits]
                a_s, e_s = [], []
                for p in range(ch):
                    a = jnp.exp(logits[p][1] + state[3 * p] + sums[p][0])
                    if masked:
                        a = jnp.where(causal, a, 0.0)
                    a_s.append(a)
                    e_s.append(a * das[p])
                e_sums = [_suffix_sums(e, suffix) for e in e_s]
                out, dks, dvs = [], [], []
                for p in range(ch):
                    carry, rcarry, dq = state[3 * p:3 * p + 3]
                    e = e_s[p]
                    before = tot[p] - (rcarry + e_sums[p][0] + e)
                    beta = jnp.exp(logits[p][1])
                    dz = e * (1.0 - beta) - beta * before
                    if masked:
                        dz = jnp.where(causal, dz, 0.0)
                    dz_b = dz.astype(BF16)
                    dks.append(_dot_tn(dz_b, q01[p]))
                    dvs.append(_dot_tn(a_s[p].astype(BF16), do01[p]))
                    out += [carry + sums[p][1], rcarry + e_sums[p][1], dq + _dot(dz_b, ks_[p])]
                dk_acc[pl.ds(ks, BLK), :] += jnp.concatenate(dks, axis=1)
                dv_acc[pl.ds(ks, BLK), :] += jnp.concatenate(dvs, axis=1)
                return tuple(out)

            state = (jnp.zeros((2 * BLK, BLK), F32),) * (3 * ch)
            state = tile(i, state, True)
            state = _sb_walk_back(i, state, 3, tile)
            dq = [jnp.where(head0, state[3 * p + 2][:BLK], state[3 * p + 2][BLK:]) for p in range(ch)]
            dq_ref[pl.ds(qs, BLK), :] = (jnp.concatenate(dq, axis=1) * scale).astype(dq_ref.dtype)
            return 0

        lax.fori_loop(0, nq, q_block, 0)
        dk_ref[...] = dk_acc[...].astype(dk_ref.dtype)
        dv_ref[...] = dv_acc[...].astype(dv_ref.dtype)

    def body(*refs):
        step = pl.program_id(0) * n_steps + pl.program_id(1)
        n_c, n_o = len(carry.arrays), len(carry.out_shapes)
        own = refs[:5] + refs[5 + n_c:8 + n_c] + refs[8 + n_c + n_o:10 + n_c + n_o]
        carry.run(refs, 5, 3, step, b_sz * n_steps, lambda: compute(*own))

    blk = lambda off: pl.BlockSpec((None, s_len, ch * LANES), lambda b, p: (b, 0, off + p))
    once = lambda off: pl.BlockSpec((None, s_len, ch * LANES), lambda b, p: (b, 0, off + p),
                                    pipeline_mode=pl.Buffered(1))
    out_sd = jax.ShapeDtypeStruct((b_sz, s_len, SB_WIDTH), BF16)
    c_in, c_out, c_shapes, c_alias, c_sems = carry.call_args(5, 3)
    res = pl.pallas_call(
        body, name="sb_bwd", grid=(b_sz, n_steps),
        in_specs=[once(0), once(n_steps), once(2 * n_steps), once(0), once(0)] + c_in,
        out_specs=[blk(0), blk(0), blk(0)] + c_out, out_shape=[out_sd, out_sd, out_sd] + c_shapes,
        input_output_aliases=c_alias,
        scratch_shapes=[pltpu.VMEM((s_len, ch * LANES), F32), pltpu.VMEM((s_len, ch * LANES), F32)] + c_sems,
        compiler_params=pltpu.CompilerParams(dimension_semantics=("arbitrary", "arbitrary"),
                                             vmem_limit_bytes=VMEM_CAP),
    )(qkv, qkv, qkv, o_sb, do_sb, *carry.arrays)
    return res[:3], res[3:]


def _dil_consts(group, pair_idx, dilation):
    lane = lax.broadcasted_iota(jnp.int32, (BLK, LANES), 1)
    head0 = lane < HEAD_DIM
    row = lax.broadcasted_iota(jnp.int32, (2 * BLK, BLK), 0)
    qa = row % BLK
    kb = lax.broadcasted_iota(jnp.int32, (2 * BLK, BLK), 1)
    head = (group * DIL_HEADS_PER_GROUP + 2 * pair_idx + row // BLK).astype(F32)
    slope = jnp.exp((-ALIBI_MAX_BIAS * math.log(2.0) / DIL_HEADS) * (head + 1.0))
    valid_cur = kb <= qa
    valid_prev = kb >= qa
    bias_cur = -slope * ((qa - kb) * dilation).astype(F32)
    bias_prev = -slope * ((BLK + qa - kb) * dilation).astype(F32)
    return head0, valid_cur, valid_prev, bias_cur, bias_prev


def _dil_units(s_len, dilation):
    nb = s_len // dilation // BLK
    return [(r, n) for r in range(dilation) for n in range(nb)]


def _dil_rows(n, r, dilation):
    if dilation == 1:
        return pl.ds(n * BLK, BLK)
    return pl.ds(n * BLK * dilation + r, BLK, stride=dilation)


def _dil_scores(q01, k, bias, valid):
    s = _dot_nt(q01, k) * (1.0 / math.sqrt(HEAD_DIM)) + bias
    return jnp.where(valid, s, NEG)


def _dil_fwd(qkv, b_sz, s_len, carry):
    n_pairs = DIL_OUT_WIDTH // LANES
    q_off = 3 * SB_WIDTH // LANES
    per_kind = DIL_WIDTH // LANES

    def compute(pair_idx, qkv_refs, o_ref, lse_ref, m_s, l_s):
        m_s[...] = jnp.full(m_s.shape, NEG, F32)
        l_s[...] = jnp.zeros(l_s.shape, F32)
        o_ref[...] = jnp.zeros(o_ref.shape, F32)
        for g, (_, dilation) in enumerate(DIL_PAIRS):
            q_ref, k_ref, v_ref = qkv_refs[3 * g:3 * g + 3]
            head0, valid_cur, valid_prev, bias_cur, bias_prev = _dil_consts(g, pair_idx, dilation)
            units = _dil_units(s_len, dilation)
            for u0 in range(0, len(units), DIL_CHAINS):
                group = units[u0:u0 + DIL_CHAINS]
                rows_of = [_dil_rows(n, r, dilation) for r, n in group]
                scores, values = [], []
                for (r, n), rows in zip(group, rows_of):
                    q01 = _stack_heads(q_ref[rows, :].astype(BF16), head0)
                    sc = [_dil_scores(q01, k_ref[rows, :].astype(BF16), bias_cur, valid_cur)]
                    vals = [_stack_heads(v_ref[rows, :].astype(BF16), head0)]
                    if n > 0:
                        prev = _dil_rows(n - 1, r, dilation)
                        sc.append(_dil_scores(q01, k_ref[prev, :].astype(BF16), bias_prev, valid_prev))
                        vals.append(_stack_heads(v_ref[prev, :].astype(BF16), head0))
                    scores.append(sc)
                    values.append(vals)
                stats = []
                for sc, rows in zip(scores, rows_of):
                    m_blk = functools.reduce(jnp.maximum, [jnp.max(x, axis=-1, keepdims=True) for x in sc])
                    m_old = jnp.concatenate([m_s.at[0][rows, :], m_s.at[1][rows, :]], axis=0)
                    l_old = jnp.concatenate([l_s.at[0][rows, :], l_s.at[1][rows, :]], axis=0)
                    m_new = jnp.maximum(m_old, m_blk)
                    probs = [jnp.exp(x - m_new) for x in sc]
                    l_blk = functools.reduce(jnp.add, [jnp.sum(p, axis=-1, keepdims=True) for p in probs])
                    alpha = jnp.exp(m_old - m_new)
                    stats.append((m_new, alpha * l_old + l_blk, alpha, probs))
                for (m_new, l_new, alpha, probs), vals, rows in zip(stats, values, rows_of):
                    alpha_tok = jnp.where(head0, alpha[:BLK], alpha[BLK:])
                    p_cat = jnp.concatenate(
                        [h for p in probs for h in (p[:BLK].astype(BF16), p[BLK:].astype(BF16))], axis=1)
                    o_ref[rows, :] = alpha_tok * o_ref[rows, :] + _dot(p_cat, jnp.concatenate(vals, axis=0))
                    m_s.at[0][rows, :] = m_new[:BLK]
                    m_s.at[1][rows, :] = m_new[BLK:]
                    l_s.at[0][rows, :] = l_new[:BLK]
                    l_s.at[1][rows, :] = l_new[BLK:]
        lane = lax.broadcasted_iota(jnp.int32, (BLK, LANES), 1)
        for c in range(s_len // BLK):
            rows = pl.ds(c * BLK, BLK)
            l0, l1 = l_s.at[0][rows, :], l_s.at[1][rows, :]
            o_ref[rows, :] = o_ref[rows, :] / jnp.where(lane < HEAD_DIM, l0, l1)
            lse_ref.at[0][rows, :] = m_s.at[0][rows, :] + jnp.log(l0)
            lse_ref.at[1][rows, :] = m_s.at[1][rows, :] + jnp.log(l1)

    def body(*refs):
        pair_idx = pl.program_id(1)
        step = pl.program_id(0) * n_pairs + pair_idx
        n_c, n_o = len(carry.arrays), len(carry.out_shapes)
        o_ref, lse_ref = refs[9 + n_c:11 + n_c]
        m_s, l_s = refs[11 + n_c + n_o:13 + n_c + n_o]
        carry.run(refs, 9, 2, step, b_sz * n_pairs, lambda: compute(pair_idx, refs[:9], o_ref, lse_ref, m_s, l_s))

    in_specs = []
    for g in range(len(DIL_PAIRS)):
        for kind in range(3):
            off = q_off + kind * per_kind + g * n_pairs
            in_specs.append(pl.BlockSpec((None, s_len, LANES), lambda b, p, off=off: (b, 0, off + p)))
    c_in, c_out, c_shapes, c_alias, c_sems = carry.call_args(9, 2)
    res = pl.pallas_call(
        body, name="dil_fwd", grid=(b_sz, n_pairs),
        in_specs=in_specs + c_in,
        out_specs=[pl.BlockSpec((None, s_len, LANES), lambda b, p: (b, 0, p)),
                   pl.BlockSpec((None, None, 2, s_len, LANES), lambda b, p: (b, p, 0, 0, 0))] + c_out,
        out_shape=[jax.ShapeDtypeStruct((b_sz, s_len, DIL_OUT_WIDTH), F32),
                   jax.ShapeDtypeStruct((b_sz, n_pairs, 2, s_len, LANES), F32)] + c_shapes,
        input_output_aliases=c_alias,
        scratch_shapes=[pltpu.VMEM((2, s_len, LANES), F32), pltpu.VMEM((2, s_len, LANES), F32)] + c_sems,
        compiler_params=pltpu.CompilerParams(dimension_semantics=("arbitrary", "arbitrary"),
                                             vmem_limit_bytes=VMEM_CAP),
    )(*([qkv] * 9), *carry.arrays)
    return res[0], res[1], res[2:]


def _dil_bwd(qkv, o_dl, lse, do_dl, b_sz, s_len, carry):
    n_pairs = DIL_OUT_WIDTH // LANES
    n_groups = len(DIL_PAIRS)
    q_off = 3 * SB_WIDTH // LANES
    per_kind = DIL_WIDTH // LANES

    def compute(pair_idx, group, q_ref, k_ref, v_ref, o_ref, lse_ref, do_ref, dq_ref, dk_ref, dv_ref, d_s, dq_s, dk_s, dv_s):
        lrow = lax.broadcasted_iota(jnp.int32, (LANES, LANES), 0)
        ones_h0 = jnp.where(lrow < HEAD_DIM, 1.0, 0.0).astype(BF16)
        ones_h1 = jnp.where(lrow >= HEAD_DIM, 1.0, 0.0).astype(BF16)
        for c in range(s_len // BLK):
            rows = pl.ds(c * BLK, BLK)
            dd_hi, dd_lo = _split2(do_ref[rows, :] * o_ref[rows, :])
            d_s.at[0][rows, :] = _dot(dd_hi, ones_h0) + _dot(dd_lo, ones_h0)
            d_s.at[1][rows, :] = _dot(dd_hi, ones_h1) + _dot(dd_lo, ones_h1)
        dk_s[...] = jnp.zeros(dk_s.shape, F32)
        dv_s[...] = jnp.zeros(dv_s.shape, F32)

        def one_group(g, dilation):
            head0, valid_cur, valid_prev, bias_cur, bias_prev = _dil_consts(g, pair_idx, dilation)
            units = _dil_units(s_len, dilation)
            scale = 1.0 / math.sqrt(HEAD_DIM)
            for u0 in range(0, len(units), DIL_CHAINS):
                chunk = units[u0:u0 + DIL_CHAINS]
                loaded = []
                for r, n in chunk:
                    rows = _dil_rows(n, r, dilation)
                    q01 = _stack_heads(q_ref[rows, :].astype(BF16), head0)
                    do01 = _stack_heads(do_ref[rows, :].astype(BF16), head0)
                    lse01 = jnp.concatenate([lse_ref.at[0][rows, :], lse_ref.at[1][rows, :]], axis=0)
                    d01 = jnp.concatenate([d_s.at[0][rows, :], d_s.at[1][rows, :]], axis=0)
                    blocks = [(rows, bias_cur, valid_cur)]
                    if n > 0:
                        blocks.append((_dil_rows(n - 1, r, dilation), bias_prev, valid_prev))
                    parts = []
                    for krows, bias, valid in blocks:
                        k = k_ref[krows, :].astype(BF16)
                        v = v_ref[krows, :].astype(BF16)
                        parts.append((krows, k, _dil_scores(q01, k, bias, valid), _dot_nt(do01, v)))
                    loaded.append((rows, q01, do01, lse01, d01, parts))
                grads = []
                for rows, q01, do01, lse01, d01, parts in loaded:
                    for krows, k, sc, dp in parts:
                        p = jnp.exp(sc - lse01)
                        grads.append((p.astype(BF16), (p * (dp - d01) * scale).astype(BF16)))
                it = iter(grads)
                updates = []
                for rows, q01, do01, lse01, d01, parts in loaded:
                    dq = jnp.zeros((2 * BLK, LANES), F32)
                    for krows, k, sc, dp in parts:
                        p_b, ds = next(it)
                        dq = dq + _dot(ds, k)
                        updates.append((krows, _dot_tn(ds, q01), _dot_tn(p_b, do01)))
                    dq_s[rows, :] = jnp.where(head0, dq[:BLK], dq[BLK:])
                for krows, dk, dv in updates:
                    dk_s[krows, :] = dk_s[krows, :] + dk
                    dv_s[krows, :] = dv_s[krows, :] + dv

        for g, (_, dilation) in enumerate(DIL_PAIRS):
            pl.when(group == g)(functools.partial(one_group, g, dilation))
        dq_ref[...] = dq_s[...].astype(dq_ref.dtype)
        dk_ref[...] = dk_s[...].astype(dk_ref.dtype)
        dv_ref[...] = dv_s[...].astype(dv_ref.dtype)

    def body(*refs):
        pair_idx, group = pl.program_id(1), pl.program_id(2)
        step = (pl.program_id(0) * n_pairs + pair_idx) * n_groups + group
        n_c, n_o = len(carry.arrays), len(carry.out_shapes)
        own = refs[:6] + refs[6 + n_c:9 + n_c] + refs[9 + n_c + n_o:13 + n_c + n_o]
        carry.run(refs, 6, 3, step, b_sz * n_pairs * n_groups, lambda: compute(pair_idx, group, *own))

    def qkv_spec(kind):
        return pl.BlockSpec((None, s_len, LANES),
                            lambda b, p, g: (b, 0, q_off + kind * per_kind + g * n_pairs + p))

    tok_spec = pl.BlockSpec((None, s_len, LANES), lambda b, p, g: (b, 0, p))
    out_spec = pl.BlockSpec((None, s_len, LANES), lambda b, p, g: (b, 0, g * n_pairs + p))
    out_sd = jax.ShapeDtypeStruct((b_sz, s_len, DIL_WIDTH), BF16)
    c_in, c_out, c_shapes, c_alias, c_sems = carry.call_args(6, 3)
    res = pl.pallas_call(
        body, name="dil_bwd", grid=(b_sz, n_pairs, n_groups),
        in_specs=[qkv_spec(0), qkv_spec(1), qkv_spec(2), tok_spec,
                  pl.BlockSpec((None, None, 2, s_len, LANES), lambda b, p, g: (b, p, 0, 0, 0)), tok_spec] + c_in,
        out_specs=[out_spec, out_spec, out_spec] + c_out,
        out_shape=[out_sd, out_sd, out_sd] + c_shapes,
        input_output_aliases=c_alias,
        scratch_shapes=[pltpu.VMEM((2, s_len, LANES), F32)] + [pltpu.VMEM((s_len, LANES), F32)] * 3 + c_sems,
        compiler_params=pltpu.CompilerParams(dimension_semantics=("arbitrary", "arbitrary", "arbitrary"),
                                             vmem_limit_bytes=VMEM_CAP),
    )(qkv, qkv, qkv, o_dl, lse, do_dl, *carry.arrays)
    return res[:3], res[3:]


def _mesh_pos():
    return lax.axis_index("x"), lax.axis_index("y"), lax.axis_index("c")


def _other_chips(x, y):
    return [(1 - x, y), (x, 1 - y), (1 - x, 1 - y)]


def _hbm_specs(n):
    return [pl.BlockSpec(memory_space=pl.ANY)] * n


SWAPPED = ("w_ffn_in",)


def _slot(x, y, swapped):
    return 2 * y + x if swapped else 2 * x + y


def _cast_to_slab(w, name, swapped=False):
    rows, cols = w.shape
    mine = jnp.reshape(_slot(lax.axis_index("x"), lax.axis_index("y"), swapped), (1,)).astype(jnp.int32)

    def body(idx_ref, w_ref, o_ref):
        o_ref[...] = w_ref[...].astype(BF16)

    return pl.pallas_call(
        body, name=name,
        grid_spec=pltpu.PrefetchScalarGridSpec(
            num_scalar_prefetch=1, grid=(1,),
            in_specs=[pl.BlockSpec((rows, cols), lambda i, idx: (0, 0))],
            out_specs=pl.BlockSpec((None, rows, cols), lambda i, idx: (idx[0], 0, 0))),
        out_shape=jax.ShapeDtypeStruct((N_CHIPS, rows, cols), BF16),
        compiler_params=pltpu.CompilerParams(vmem_limit_bytes=_vmem_limit(rows * cols * 6)),
    )(mine, w)


def _gather_issue(slabs, send_sems, recv_sems, swapped):
    x, y, c = _mesh_pos()
    for k, slab in enumerate(slabs):
        half = slab.shape[1] // 2
        rows = slab.at[_slot(x, y, swapped[k]), pl.ds(c * half, half), :]
        for r, (px, py) in enumerate(_other_chips(x, y)):
            pltpu.make_async_remote_copy(
                src_ref=rows, dst_ref=rows, send_sem=send_sems.at[6 * k + r], recv_sem=recv_sems.at[6 * k + r],
                device_id=(px, py, c), device_id_type=MESH).start()


def _gather_complete(slabs, send_sems, recv_sems, swapped):
    x, y, c = _mesh_pos()
    chips = _other_chips(x, y)

    def copy(k, sem, block, rows, to):
        ref = slabs[k].at[block, rows, :]
        return pltpu.make_async_remote_copy(
            src_ref=ref, dst_ref=ref, send_sem=send_sems.at[sem], recv_sem=recv_sems.at[sem],
            device_id=to, device_id_type=MESH)

    for k, slab in enumerate(slabs):
        half = slab.shape[1] // 2
        for r, (px, py) in enumerate(chips):
            theirs = _slot(px, py, swapped[k])
            copy(k, 6 * k + r, theirs, pl.ds(c * half, half), (px, py, c)).wait_recv()
            copy(k, 6 * k + 3 + r, theirs, pl.ds(c * half, half), (x, y, 1 - c)).start()
    for k, slab in enumerate(slabs):
        half = slab.shape[1] // 2
        for r, (px, py) in enumerate(chips):
            copy(k, 6 * k + 3 + r, _slot(px, py, swapped[k]), pl.ds((1 - c) * half, half), (x, y, 1 - c)).wait_recv()
    for k, slab in enumerate(slabs):
        half = slab.shape[1] // 2
        for r, (px, py) in enumerate(chips):
            copy(k, 6 * k + r, _slot(x, y, swapped[k]), pl.ds(c * half, half), (px, py, c)).wait_send()
            copy(k, 6 * k + 3 + r, _slot(px, py, swapped[k]), pl.ds(c * half, half), (x, y, 1 - c)).wait_send()


def _gather_sems(n):
    return [pltpu.SemaphoreType.DMA((6 * n,)), pltpu.SemaphoreType.DMA((6 * n,))]


def _gather_carry(slabs, names):
    swapped = [k in SWAPPED for k in names]
    return _Carry(slabs, [jax.ShapeDtypeStruct(a.shape, a.dtype) for a in slabs], True, _gather_sems(len(slabs)),
                  lambda ins, outs, sems: _gather_issue(outs, *sems, swapped),
                  lambda ins, outs, sems: _gather_complete(outs, *sems, swapped))


def _pair_copies(ins, outs, send_sems, recv_sems):
    x, y, c = _mesh_pos()
    copies = []
    for k, g in enumerate(ins):
        half = g.shape[1] // 2
        copies.append(pltpu.make_async_remote_copy(
            src_ref=g.at[:, pl.ds((1 - c) * half, half), :], dst_ref=outs[k],
            send_sem=send_sems.at[k], recv_sem=recv_sems.at[k],
            device_id=(x, y, 1 - c), device_id_type=MESH))
    return copies


def _pair_carry(grads):
    n = len(grads)

    def start(ins, outs, sems):
        for cp in _pair_copies(ins, outs, *sems):
            cp.start()

    def finish(ins, outs, sems):
        for cp in _pair_copies(ins, outs, *sems):
            cp.wait()

    return _Carry(grads, [jax.ShapeDtypeStruct((N_CHIPS, g.shape[1] // 2, g.shape[2]), g.dtype) for g in grads], False,
                  [pltpu.SemaphoreType.DMA((n,)), pltpu.SemaphoreType.DMA((n,))], start, finish)


def _pair_exchange(grads, tag):
    carry = _pair_carry(grads)
    n = len(grads)

    def body(*refs):
        carry.start(refs[:n], refs[n:2 * n], refs[2 * n:])
        carry.finish(refs[:n], refs[n:2 * n], refs[2 * n:])

    return pl.pallas_call(
        body, name="grad_pair_exchange_" + tag, in_specs=_hbm_specs(n), out_specs=_hbm_specs(n),
        out_shape=carry.out_shapes, scratch_shapes=carry.sems,
    )(*grads)


def _pair_sum(grad, other, name, swapped):
    _, rows, cols = grad.shape
    half = rows // 2
    x, y, c = _mesh_pos()
    idx = jnp.stack([c, _slot(x, y, swapped)]).astype(jnp.int32)

    def body(idx_ref, g_ref, p_ref, own_ref, sb_ref):
        s = g_ref[...] + p_ref[...].astype(F32)
        sb_ref[...] = s.astype(BF16)

        @pl.when(pl.program_id(0) == idx_ref[1])
        def _():
            own_ref[...] = s

    blk = pl.BlockSpec((None, half, cols), lambda p, idx: (p, 0, 0))
    return pl.pallas_call(
        body, name=name,
        grid_spec=pltpu.PrefetchScalarGridSpec(
            num_scalar_prefetch=1, grid=(N_CHIPS,),
            in_specs=[pl.BlockSpec((None, half, cols), lambda p, idx: (p, idx[0], 0)), blk],
            out_specs=[pl.BlockSpec((half, cols), lambda p, idx: (0, 0)), blk]),
        out_shape=[jax.ShapeDtypeStruct((half, cols), F32), jax.ShapeDtypeStruct((N_CHIPS, half, cols), BF16)],
        compiler_params=pltpu.CompilerParams(dimension_semantics=("arbitrary",),
                                             vmem_limit_bytes=_vmem_limit(4 * half * cols * 4)),
    )(idx, grad, other)


def _chip_copies(sums_bf16, lands, send_sems, recv_sems, swapped):
    x, y, c = _mesh_pos()
    return [pltpu.make_async_remote_copy(
        src_ref=sums_bf16[k].at[_slot(px, py, swapped[k])], dst_ref=lands[k].at[r],
        send_sem=send_sems.at[3 * k + r], recv_sem=recv_sems.at[3 * k + r],
        device_id=(px, py, c), device_id_type=MESH)
        for k in range(len(sums_bf16)) for r, (px, py) in enumerate(_other_chips(x, y))]


def _chip_carry(sums_bf16, names):
    swapped = [k in SWAPPED for k in names]

    def start(ins, outs, sems):
        for cp in _chip_copies(ins, outs, *sems, swapped):
            cp.start()

    def finish(ins, outs, sems):
        for cp in _chip_copies(ins, outs, *sems, swapped):
            cp.wait()

    return _Carry(sums_bf16, _chip_landing(sums_bf16), False, _chip_sems(len(sums_bf16)), start, finish)


def _chip_sems(n):
    return [pltpu.SemaphoreType.DMA((3 * n,)), pltpu.SemaphoreType.DMA((3 * n,))]


def _chip_landing(sums_bf16):
    return [jax.ShapeDtypeStruct((N_CHIPS - 1,) + s.shape[1:], BF16) for s in sums_bf16]


def _chip_sum(own, landed, name):
    rows, cols = own.shape
    core = jnp.reshape(lax.axis_index("c"), (1,)).astype(jnp.int32)

    def body(core_ref, o_ref, l_ref, out_ref):
        out_ref[...] = ((o_ref[...] + l_ref[0].astype(F32)) + l_ref[1].astype(F32)) + l_ref[2].astype(F32)

    return pl.pallas_call(
        body, name=name,
        grid_spec=pltpu.PrefetchScalarGridSpec(
            num_scalar_prefetch=1, grid=(1,),
            in_specs=[pl.BlockSpec((rows, cols), lambda i, core_ref: (0, 0)),
                      pl.BlockSpec((N_CHIPS - 1, rows, cols), lambda i, core_ref: (0, 0, 0))],
            out_specs=pl.BlockSpec((rows, cols), lambda i, core_ref: (core_ref[0], 0))),
        out_shape=jax.ShapeDtypeStruct((2 * rows, cols), F32),
        compiler_params=pltpu.CompilerParams(vmem_limit_bytes=_vmem_limit(3 * rows * cols * 4)),
    )(core, own, landed)


def _final_exchange(fulls, v):
    n = len(fulls)
    rows, cols = v.shape
    n_dev = 8

    def body(*refs):
        v_ref, out_ref = refs[0], refs[1 + 2 * n]
        outs = refs[1 + n:1 + 2 * n]
        buf, v_send, v_recv, h_send, h_recv = refs[2 + 2 * n:]
        x, y, c = _mesh_pos()
        me = 4 * x + 2 * y + c
        buf[me] = v_ref[...]
        peers = [(1 - x if r & 4 else x, 1 - y if r & 2 else y, 1 - c if r & 1 else c) for r in range(1, n_dev)]
        copies = []
        for r, peer in enumerate(peers):
            copies.append(pltpu.make_async_remote_copy(
                src_ref=v_ref, dst_ref=buf.at[me], send_sem=v_send.at[r], recv_sem=v_recv.at[r],
                device_id=peer, device_id_type=MESH))
        for k in range(n):
            half = fulls[k].shape[0] // 2
            mine = outs[k].at[pl.ds(c * half, half), :]
            copies.append(pltpu.make_async_remote_copy(
                src_ref=mine, dst_ref=mine, send_sem=h_send.at[k], recv_sem=h_recv.at[k],
                device_id=(x, y, 1 - c), device_id_type=MESH))
        for cp in copies:
            cp.start()
        for r, (px, py, pc) in enumerate(peers):
            pltpu.make_async_remote_copy(
                src_ref=v_ref, dst_ref=buf.at[4 * px + 2 * py + pc], send_sem=v_send.at[r], recv_sem=v_recv.at[r],
                device_id=(px, py, pc), device_id_type=MESH).wait_recv()
        for k in range(n):
            half = fulls[k].shape[0] // 2
            theirs = outs[k].at[pl.ds((1 - c) * half, half), :]
            pltpu.make_async_remote_copy(
                src_ref=theirs, dst_ref=theirs, send_sem=h_send.at[k], recv_sem=h_recv.at[k],
                device_id=(x, y, 1 - c), device_id_type=MESH).wait_recv()
        for cp in copies:
            cp.wait_send()
        acc = buf[0]
        for d in range(1, n_dev):
            acc = acc + buf[d]
        out_ref[...] = acc
        out_ref[3:4, :] = jnp.broadcast_to(jnp.sum(acc[3:4, :], axis=1, keepdims=True), (1, cols))

    vm = pl.BlockSpec(memory_space=pltpu.VMEM)
    res = pl.pallas_call(
        body, name="final_exchange",
        in_specs=[vm] + _hbm_specs(n), out_specs=_hbm_specs(n) + [vm],
        out_shape=[jax.ShapeDtypeStruct(f.shape, F32) for f in fulls] + [jax.ShapeDtypeStruct((rows, cols), F32)],
        input_output_aliases={1 + k: k for k in range(n)},
        scratch_shapes=[pltpu.VMEM((n_dev, rows, cols), F32),
                        pltpu.SemaphoreType.DMA((n_dev - 1,)), pltpu.SemaphoreType.DMA((n_dev - 1,)),
                        pltpu.SemaphoreType.DMA((n,)), pltpu.SemaphoreType.DMA((n,))],
    )(v, *fulls)
    return res[:n], res[n]


def _adamw_math(w, g, m, v):
    m = ADAM_B1 * m + (1.0 - ADAM_B1) * g
    v = ADAM_B2 * v + (1.0 - ADAM_B2) * (g * g)
    m_hat = m / (1.0 - ADAM_B1 ** ADAM_STEP)
    v_hat = v / (1.0 - ADAM_B2 ** ADAM_STEP)
    delta = -ADAM_LR * (m_hat / (jnp.sqrt(v_hat) + ADAM_EPS) + ADAM_WD * w)
    return delta, m, v


def _adamw(w, g, m, v, name):
    rows, cols = w.shape
    tm = rows // 2 if (rows // 2) % 8 == 0 else rows
    return _rowwise(_adamw_math, [w, g, m, v], [], [(cols, F32)] * 3, [], tm=tm, name=name)


def _unshard_cols(gathered):
    n, r, c = gathered.shape
    return jnp.transpose(gathered, (1, 0, 2)).reshape(r, n * c)


def _shard_cols(full):
    r, nc = full.shape
    return jnp.transpose(full.reshape(r, N_CHIPS, nc // N_CHIPS), (1, 0, 2))


LATE = ["w_sb_up", "w_dil_up", "w_out", "w_ffn_in", "w_ffn_out"]


def _late_weights(slabs, d_model, d_ff):
    g = dict(zip(LATE, slabs))
    return (_unshard_cols(g["w_sb_up"]), _unshard_cols(g["w_dil_up"]), g["w_out"].reshape(d_model, d_model),
            _unshard_cols(g["w_ffn_in"]), g["w_ffn_out"].reshape(d_ff, d_model))


ROW_SHARDED = ("w_in", "w_out", "w_ffn_in", "w_ffn_out")


def _chip_major(grads):
    out = []
    for k, g in grads.items():
        if k in ROW_SHARDED:
            out.append(g.reshape(N_CHIPS, g.shape[0] // N_CHIPS, g.shape[1]))
        else:
            out.append(_shard_cols(g))
    return out


def _pair_sums(full, others, names):
    return [_pair_sum(g, o, "grad_pair_sum_" + k, k in SWAPPED) for g, o, k in zip(full, others, names)]


def _chip_sums(pair, landed, names):
    return {k: _chip_sum(p[0], l, "grad_chip_sum_" + k) for p, l, k in zip(pair, landed, names)}


def _fwd_bwd(x, loss_target, g_mix, g_ffn, g_fin, slab_in, late_slabs):
    b_sz, s_len, d_model = x.shape
    t = b_sz * s_len
    d_ff = late_slabs[-1].shape[1] * N_CHIPS
    x2d = x.reshape(t, d_model)
    tgt2d = loss_target.reshape(t, d_model)

    u, (slab_in,) = _rowwise(lambda xv, g: (_rms_stats(xv)[0] * g,), [x2d], [g_mix], [(d_model, BF16)], [], tm=512,
                             name="norm_mix", carry=_gather_carry([slab_in], ["w_in"]))
    wt_in = slab_in.reshape(-1, d_model)
    qkv, (slab_ffn_out,) = _mm(u, wt_in, tb=True, b_cols=(0, QKV_WIDTH), tm=2048, tn=768, tk=d_model, name="proj_qkv",
                               carry=_gather_carry(late_slabs[4:], LATE[4:]))
    gates = _mm(u, wt_in, tb=True, b_cols=(QKV_WIDTH, 2 * d_model), out_dtype=BF16, tm=t, tn=256, tk=d_model,
                name="proj_gates")
    qkv3 = qkv.reshape(b_sz, s_len, QKV_WIDTH)
    o_sb, (slab_ffn_in,) = _sb_fwd(qkv3, b_sz, s_len, _gather_carry(late_slabs[3:4], LATE[3:4]))
    o_dl, lse, small_slabs = _dil_fwd(qkv3, b_sz, s_len, _gather_carry(late_slabs[:3], LATE[:3]))
    wf_sb_up, wf_dil_up, wf_out, wf_ffn_in, wf_ffn_out = _late_weights(
        list(small_slabs) + [slab_ffn_in, slab_ffn_out], d_model, d_ff)
    o_sb2, o_dl2 = o_sb.reshape(t, SB_WIDTH), o_dl.reshape(t, DIL_OUT_WIDTH)
    y_sb = _mm(o_sb2, wf_sb_up, out_dtype=BF16, tm=1024, tn=1024, tk=SB_WIDTH, name="sb_up")
    y_dl = _mm(o_dl2, wf_dil_up, out_dtype=BF16, tm=1024, tn=1024, tk=DIL_OUT_WIDTH, name="dil_up")

    def merge_fn(gt, ys, yd):
        return (_sigmoid(gt[:, :d_model]) * ys + _sigmoid(gt[:, d_model:]) * yd,)

    (merged,) = _rowwise(merge_fn, [gates, y_sb, y_dl], [], [(d_model, BF16)], [], tm=512, name="merge")
    x1 = _mm(merged, wf_out, add=x2d, tm=512, tn=1024, tk=d_model, name="mix_out")
    (u2,) = _rowwise(lambda xv, g: (_rms_stats(xv)[0] * g,), [x1], [g_ffn], [(d_model, BF16)], [], tm=512, name="norm_ffn")
    half_ff = d_ff // 2

    def act_fn(hv):
        gate = hv[:, :half_ff]
        return hv, gate * _sigmoid(gate) * hv[:, half_ff:]

    h, act = _mm(u2, wf_ffn_in, tm=512, tn=d_ff, tk=d_model, name="ffn_in",
                 epilogue=(act_fn, [], [], [(d_ff, BF16), (half_ff, BF16)], []))
    def head_fn(xv, tg, g):
        xhat, r = _rms_stats(xv)
        err = xhat * g - tg
        dy = err * (1.0 / d_model)
        dx, dg_rows = _rms_bwd(dy, xhat, r, g)
        loss_lanes = (0.5 / d_model) * jnp.sum(err * err, axis=0, keepdims=True)
        return dx, dx, jnp.sum(dg_rows, axis=0, keepdims=True), loss_lanes

    dx2, dx2_b, dg_fin, loss_lanes = _mm(
        act, wf_ffn_out, add=x1, tm=512, tn=1024, tk=d_ff, name="ffn_out",
        epilogue=(head_fn, [tgt2d], [g_fin], [(d_model, F32), (d_model, BF16)], [(1, d_model), (1, d_model)]))

    def dact_fn(da, hv):
        gate, up = hv[:, :half_ff], hv[:, half_ff:]
        sg = _sigmoid(gate)
        dgate = da * up * (sg * (1.0 + gate * (1.0 - sg)))
        return (jnp.concatenate([dgate, da * (gate * sg)], axis=1),)

    (dh,) = _mm(dx2_b, wf_ffn_out, tb=True, tm=512, tn=half_ff, tk=d_model, name="ffn_out_dx",
                epilogue=(dact_fn, [h], [], [(d_ff, BF16)], []))
    gw_ffn_out = _mm(act, dx2_b, ta=True, tm=256, tn=d_model, tk=t, name="ffn_out_dw")
    def norm_bwd_fn(du_, dres, xv, g):
        xhat, r = _rms_stats(xv)
        dx, dg_rows = _rms_bwd(du_, xhat, r, g)
        return dres + dx, jnp.sum(dg_rows, axis=0, keepdims=True)

    def norm_bwd_twice(*args):
        dx, dg = norm_bwd_fn(*args)
        return dx, dx, dg

    dx1, dx1_b, dg_ffn = _mm(dh, wf_ffn_in, tb=True, tm=512, tn=1024, tk=2 * d_ff, name="ffn_in_dx",
                             epilogue=(norm_bwd_twice, [dx2, x1], [g_ffn], [(d_model, F32), (d_model, BF16)], [(1, d_model)]))
    gwt_ffn_in = _mm(dh, u2, ta=True, tm=512, tn=d_model, tk=t, name="ffn_in_dw")

    dmerged = _mm(dx1_b, wf_out, tb=True, out_dtype=BF16, tm=512, tn=1024, tk=d_model, name="mix_out_dx")
    gw_out = _mm(merged, dx1_b, ta=True, tm=256, tn=d_model, tk=t, name="mix_out_dw")

    def merge_bwd_fn(gt, ys, yd, dm):
        s_sb, s_dl = _sigmoid(gt[:, :d_model]), _sigmoid(gt[:, d_model:])
        dgates = jnp.concatenate([dm * ys * s_sb * (1.0 - s_sb), dm * yd * s_dl * (1.0 - s_dl)], axis=1)
        return dgates, dm * s_sb, dm * s_dl

    full_big = _chip_major({"w_out": gw_out, "w_ffn_in": gwt_ffn_in, "w_ffn_out": gw_ffn_out})
    dgates, dy_sb, dy_dl, others_big = _rowwise(
        merge_bwd_fn, [gates, y_sb, y_dl, dmerged], [], [(2 * d_model, BF16), (d_model, BF16), (d_model, BF16)], [],
        tm=256, name="merge_bwd", carry=_pair_carry(full_big))
    pair_big = _pair_sums(full_big, others_big, LATE[2:])
    do_sb = _mm(dy_sb, wf_sb_up, tb=True, out_dtype=BF16, tm=1024, tn=SB_WIDTH, tk=d_model, name="sb_up_dx")
    gw_sb_up = _mm(o_sb2, dy_sb, ta=True, tm=SB_WIDTH, tn=1024, tk=512, name="sb_up_dw")
    do_dl = _mm(dy_dl, wf_dil_up, tb=True, tm=1024, tn=DIL_OUT_WIDTH, tk=d_model, name="dil_up_dx")
    gw_dil_up = _mm(o_dl2, dy_dl, ta=True, tm=DIL_OUT_WIDTH, tn=1024, tk=512, name="dil_up_dw")
    full_small = _chip_major({"w_sb_up": gw_sb_up, "w_dil_up": gw_dil_up})
    (dq_sb, dk_sb, dv_sb), brought = _sb_bwd(
        qkv3, o_sb, do_sb.reshape(b_sz, s_len, SB_WIDTH), b_sz, s_len,
        _chip_carry([p[1] for p in pair_big[:2]], LATE[2:4]) + _pair_carry(full_small))
    pair_small = _pair_sums(full_small, brought[2:], LATE[:2])
    (dq_dl, dk_dl, dv_dl), landed_b = _dil_bwd(
        qkv3, o_dl, lse, do_dl.reshape(b_sz, s_len, DIL_OUT_WIDTH), b_sz, s_len,
        _chip_carry([pair_big[2][1], pair_small[0][1], pair_small[1][1]], [LATE[4], LATE[0], LATE[1]]))
    pair = pair_small + pair_big
    landed = [landed_b[1], landed_b[2], brought[0], brought[1], landed_b[0]]
    dproj = [a.reshape(t, -1) for a in (dq_sb, dk_sb, dv_sb, dq_dl, dk_dl, dv_dl)] + [dgates]
    gwt_in, gwt_in_b = _mm(dproj, u, ta=True, tm=256, tn=d_model, tk=t, name="proj_dw",
                           epilogue=(lambda tile: (tile, tile), [], [], [(d_model, F32), (d_model, BF16)], []))
    full_in = _chip_major({"w_in": gwt_in})
    pair_in = _pair_sums(full_in, _pair_exchange(_chip_major({"w_in": gwt_in_b}), "w_in"), ["w_in"])
    (dx, dg_mix), landed_in = _mm(
        dproj, wt_in, tm=512, tn=1024, tk=wt_in.shape[0], name="proj_dx",
        carry=_chip_carry([p[1] for p in pair_in], ["w_in"]),
        epilogue=(norm_bwd_fn, [dx1, x2d], [g_mix], [(d_model, F32)], [(1, d_model)]))

    grads = _chip_sums(pair, landed, LATE)
    grads.update(_chip_sums(pair_in, landed_in, ["w_in"]))
    return dx, grads, dg_mix, dg_ffn, dg_fin, loss_lanes


def kernel(x, norm_mix_g, w_in, w_sb_up, w_dil_up, w_out, norm_ffn_g, w_ffn_in, w_ffn_out, norm_final_g, loss_target, m_norm_mix_g, m_w_in, m_w_sb_up, m_w_dil_up, m_w_out, m_norm_ffn_g, m_w_ffn_in, m_w_ffn_out, m_norm_final_g, v_norm_mix_g, v_w_in, v_w_sb_up, v_w_dil_up, v_w_out, v_norm_ffn_g, v_w_ffn_in, v_w_ffn_out, v_norm_final_g):
    b_sz, s_len, d_model = x.shape
    d_ff = w_ffn_out.shape[1] * N_CHIPS
    g_mix, g_ffn, g_fin = norm_mix_g, norm_ffn_g, norm_final_g.reshape(1, d_model)

    names = ["w_in", "w_sb_up", "w_dil_up", "w_out", "w_ffn_in", "w_ffn_out"]
    shards = {"w_in": jnp.swapaxes(w_in[0], 0, 1), "w_sb_up": w_sb_up[0], "w_dil_up": w_dil_up[0], "w_out": w_out[0],
              "w_ffn_in": w_ffn_in[0], "w_ffn_out": w_ffn_out[0]}
    slab_in = _cast_to_slab(shards["w_in"], "cast_w_in")
    late_slabs = [_cast_to_slab(shards[k], "cast_" + k, k in SWAPPED) for k in LATE]

    dx, grads, dg_mix, dg_ffn, dg_fin, loss_lanes = _fwd_bwd(
        x, loss_target, g_mix, g_ffn, g_fin, slab_in, late_slabs)

    small = jnp.concatenate([dg_mix, dg_ffn, dg_fin, loss_lanes, jnp.zeros((4, d_model), F32)], axis=0)
    full_grads, small = _final_exchange([grads[k] for k in names], small)
    grads = dict(zip(names, full_grads))
    grads["w_ffn_in"] = jnp.swapaxes(grads["w_ffn_in"], 0, 1)
    loss = small[3, 0]
    gains = jnp.concatenate([g_mix, g_ffn, g_fin, jnp.zeros((5, d_model), F32)], axis=0)
    gains_m = jnp.concatenate([m_norm_mix_g, m_norm_ffn_g, m_norm_final_g.reshape(1, d_model), jnp.zeros((5, d_model), F32)], axis=0)
    gains_v = jnp.concatenate([v_norm_mix_g, v_norm_ffn_g, v_norm_final_g.reshape(1, d_model), jnp.ones((5, d_model), F32)], axis=0)
    gd, gm, gv = _rowwise(_adamw_math, [gains, small, gains_m, gains_v], [], [(d_model, F32)] * 3, [], tm=8, name="adamw_gains")

    moments = {"w_in": (jnp.swapaxes(m_w_in[0], 0, 1), jnp.swapaxes(v_w_in[0], 0, 1)),
               "w_sb_up": (m_w_sb_up[0], v_w_sb_up[0]), "w_dil_up": (m_w_dil_up[0], v_w_dil_up[0]),
               "w_out": (m_w_out[0], v_w_out[0]), "w_ffn_in": (m_w_ffn_in[0], v_w_ffn_in[0]),
               "w_ffn_out": (m_w_ffn_out[0], v_w_ffn_out[0])}
    upd = {k: _adamw(shards[k], grads[k], moments[k][0], moments[k][1], "adamw_" + k) for k in names}

    def as_output(k, a):
        return (jnp.swapaxes(a, 0, 1) if k == "w_in" else a)[None]

    def w_out_of(i):
        return [as_output(k, upd[k][i]) for k in names]

    def ordered(mix, ws, ffn_g, fin):
        return [mix, ws[0], ws[1], ws[2], ws[3], ffn_g, ws[4], ws[5], fin]

    grad_ws = [as_output(k, grads[k]) for k in names]
    outs = [loss, dx.reshape(b_sz, s_len, d_model)]
    outs += ordered(small[0:1], grad_ws, small[1:2], small[2])
    outs += ordered(gd[0:1], w_out_of(0), gd[1:2], gd[2])
    outs += ordered(gm[0:1], w_out_of(1), gm[1:2], gm[2])
    outs += ordered(gv[0:1], w_out_of(2), gv[1:2], gv[2])
    return tuple(outs)
```

```python
import functools
import math

import jax
import jax.numpy as jnp
from jax import lax
from jax.experimental import pallas as pl
from jax.experimental.pallas import tpu as pltpu

F32 = jnp.float32
BF16 = jnp.bfloat16
MESH = pl.DeviceIdType.MESH

HEAD_DIM = 64
SB_HEADS = 8
DIL_PAIRS = ((128, 1), (512, 4), (2048, 16))
DIL_HEADS_PER_GROUP = 4
DIL_HEADS = DIL_HEADS_PER_GROUP * len(DIL_PAIRS)
SB_WIDTH = SB_HEADS * HEAD_DIM
DIL_WIDTH = DIL_HEADS * HEAD_DIM
DIL_OUT_WIDTH = DIL_HEADS_PER_GROUP * HEAD_DIM
QKV_WIDTH = 3 * SB_WIDTH + 3 * DIL_WIDTH
RMS_EPS = 1e-6
ALIBI_MAX_BIAS = 8.0
ADAM_LR = 0.001
ADAM_B1 = 0.9
ADAM_B2 = 0.999
ADAM_EPS = 1e-08
ADAM_WD = 0.01
ADAM_STEP = 10

LANES = 128
BLK = 128
NEG = -1e30
EXP_UNDERFLOW = -104.0
SB_FWD_CHAINS = 4
SB_BWD_CHAINS = 4
DIL_CHAINS = 4
N_CHIPS = 4
VMEM_CAP = 56 * 1024 * 1024


def _vmem_limit(tile_bytes):
    return int(min(VMEM_CAP, max(32 * 1024 * 1024, 3 * tile_bytes + 8 * 1024 * 1024)))


def _nbytes(shape, dtype):
    return math.prod(shape) * jnp.dtype(dtype).itemsize


def _dot(a, b):
    return jnp.dot(a, b, preferred_element_type=F32)


def _dot_nt(a, b):
    return lax.dot_general(a, b, (((1,), (1,)), ((), ())), preferred_element_type=F32)


def _dot_tn(a, b):
    return lax.dot_general(a, b, (((0,), (0,)), ((), ())), preferred_element_type=F32)


def _split2(x):
    hi = x.astype(BF16)
    lo = (x - hi.astype(F32)).astype(BF16)
    return hi, lo


def _sigmoid(x):
    return pl.reciprocal(1.0 + jnp.exp(-x), approx=True)


class _Carry:
    def __init__(self, arrays=(), out_shapes=(), aliased=False, sems=(), start=None, finish=None):
        self.arrays, self.out_shapes, self.aliased = list(arrays), list(out_shapes), aliased
        self.sems, self.start, self.finish = list(sems), start, finish

    def __bool__(self):
        return bool(self.arrays)

    def __add__(self, other):
        assert not self.aliased and not other.aliased
        n_a, n_o, n_s = len(self.arrays), len(self.out_shapes), len(self.sems)
        return _Carry(
            self.arrays + other.arrays, self.out_shapes + other.out_shapes, False, self.sems + other.sems,
            lambda i, o, s: (self.start(i[:n_a], o[:n_o], s[:n_s]), other.start(i[n_a:], o[n_o:], s[n_s:])),
            lambda i, o, s: (self.finish(i[:n_a], o[:n_o], s[:n_s]), other.finish(i[n_a:], o[n_o:], s[n_s:])))

    def call_args(self, n_in, n_out):
        aliases = {n_in + k: n_out + k for k in range(len(self.arrays))} if self.aliased else {}
        return _hbm_specs(len(self.arrays)), _hbm_specs(len(self.out_shapes)), self.out_shapes, aliases, self.sems

    def run(self, refs, n_in, n_out, step, n_steps, compute):
        if not self:
            compute()
            return
        n_c, n_o, n_s = len(self.arrays), len(self.out_shapes), len(self.sems)
        ins = refs[n_in:n_in + n_c]
        outs = refs[n_in + n_c + n_out:n_in + n_c + n_out + n_o]
        sems = refs[len(refs) - n_s:]

        @pl.when(step == 0)
        def _():
            self.start(ins, outs, sems)

        compute()

        @pl.when(step == n_steps - 1)
        def _():
            self.finish(ins, outs, sems)


def _mm(a, b, *, ta=False, tb=False, add=None, out_dtype=F32, tm, tn, tk, name, carry=None, epilogue=None,
        b_cols=None, sub=1):
    carry = carry or _Carry()
    n_car = len(carry.arrays)
    pieces = list(a) if isinstance(a, (list, tuple)) else [a]
    n_a = len(pieces)
    widths = [p.shape[1] for p in pieces]
    starts = [sum(widths[:p]) for p in range(n_a)]
    if ta:
        kdim, m = pieces[0].shape[0], sum(widths)
    else:
        m, kdim = pieces[0].shape[0], sum(widths)
    if tb:
        n, k2 = b.shape
    else:
        k2, n = b.shape
    col0 = 0
    if b_cols is not None:
        assert b_cols[0] % tn == 0, name
        col0, n = b_cols[0] // tn, b_cols[1]
    assert kdim == k2 and m % tm == 0 and n % tn == 0 and kdim % tk == 0, (name, a.shape, b.shape)
    nk = kdim // tk
    assert n_a == 1 or (nk == 1 and not tb and (not ta or all(w % tm == 0 for w in widths))), name
    assert sub == 1 or (n_a == 1 and nk == 1 and not ta and tm % sub == 0), name
    grid = (m // tm, n // tn, nk)
    a_mode = dict(pipeline_mode=pl.Buffered(1)) if grid[0] == 1 and nk == 1 else {}
    b_mode = dict(pipeline_mode=pl.Buffered(1)) if grid[1] == 1 and nk == 1 else {}
    if n_a == 1:
        a_specs = [pl.BlockSpec((tk, tm), lambda i, j, k: (k, i), **a_mode) if ta
                   else pl.BlockSpec((tm, tk), lambda i, j, k: (i, k), **a_mode)]
    elif ta:
        a_specs = [pl.BlockSpec((tk, tm), lambda i, j, k, s=s // tm, w=w // tm: (0, jnp.clip(i - s, 0, w - 1)))
                   for s, w in zip(starts, widths)]
    else:
        a_specs = [pl.BlockSpec((tm, w), lambda i, j, k: (i, 0)) for w in widths]
    b_spec = (pl.BlockSpec((tn, tk), lambda i, j, k: (j + col0, k), **b_mode) if tb
              else pl.BlockSpec((tk, tn), lambda i, j, k: (k, j + col0), **b_mode))
    o_spec = pl.BlockSpec((tm, tn), lambda i, j, k: (i, j))
    dims = ((((0,) if ta else (1,)), ((1,) if tb else (0,))), ((), ()))
    has_add = add is not None
    if epilogue is None:
        ep_fn, ep_rows, ep_params, ep_outs, ep_accs = None, [], [], [], []
        out_sds, out_specs = [jax.ShapeDtypeStruct((m, n), out_dtype)], [o_spec]
    else:
        ep_fn, ep_rows, ep_params, ep_outs, ep_accs = epilogue
        assert grid[1] == 1 or not ep_accs, name
        out_sds = [jax.ShapeDtypeStruct((m, w * grid[1]), d) for w, d in ep_outs]
        out_sds += [jax.ShapeDtypeStruct(sh, F32) for sh in ep_accs]
        out_specs = [pl.BlockSpec((tm, w), lambda i, j, k: (i, j)) for w, _ in ep_outs]
        out_specs += [pl.BlockSpec(sh, lambda i, j, k: (0, 0)) for sh in ep_accs]
    n_main = len(out_sds)
    use_scratch = nk > 1 and (ep_fn is not None or jnp.dtype(out_dtype) != jnp.dtype(F32))
    n_in = n_a + 1 + has_add + len(ep_rows) + len(ep_params)

    def finish(total, refs, pid, blk=slice(None)):
        outs = refs[n_in + n_car:n_in + n_car + n_main]
        if ep_fn is None:
            outs[0][blk, :] = total.astype(out_dtype)
            return
        first = n_a + 1 + has_add
        rows = [r[blk, :].astype(F32) for r in refs[first:first + len(ep_rows)]]
        params = [p[...] for p in refs[first + len(ep_rows):n_in]]
        res = ep_fn(total, *rows, *params)
        for o_ref, v in zip(outs[:len(ep_outs)], res):
            o_ref[blk, :] = v.astype(o_ref.dtype)
        acc_refs = outs[len(ep_outs):]
        if acc_refs:
            if blk in (slice(None), slice(0, tm // sub)):
                @pl.when(pid[0] == 0)
                def _():
                    for r in acc_refs:
                        r[...] = jnp.zeros(r.shape, F32)

            for r, v in zip(acc_refs, res[len(ep_outs):]):
                r[...] += v

    def compute(refs, pid):
        a_ref, b_ref = refs[0], refs[n_a]
        add_ref = refs[n_a + 1] if has_add else None

        def dot(x, y):
            return lax.dot_general(x.astype(BF16), y.astype(BF16), dims, preferred_element_type=F32)

        if n_a > 1 and ta:
            for p_ref, s, w in zip(refs[:n_a], starts, widths):
                @pl.when((pid[0] >= s // tm) & (pid[0] < (s + w) // tm))
                def _(p_ref=p_ref):
                    prod = dot(p_ref[...], b_ref[...])
                    finish(prod + add_ref[...] if has_add else prod, refs, pid)
            return
        if sub > 1:
            for blk in [slice(r * (tm // sub), (r + 1) * (tm // sub)) for r in range(sub)]:
                prod = dot(a_ref[blk, :], b_ref[...])
                finish(prod + add_ref[blk, :] if has_add else prod, refs, pid, blk)
            return
        if n_a > 1:
            prod = dot(a_ref[...], b_ref[:widths[0], :])
            for p_ref, s, w in zip(refs[1:n_a], starts[1:], widths[1:]):
                prod += dot(p_ref[...], b_ref[s:s + w, :])
        else:
            prod = dot(a_ref[...], b_ref[...])
        if nk == 1:
            finish(prod + add_ref[...] if has_add else prod, refs, pid)
            return
        acc_ref = refs[n_in + n_car + n_main + len(carry.out_shapes)] if use_scratch else refs[n_in + n_car]
        k = pid[2]

        @pl.when(k == 0)
        def _():
            acc_ref[...] = prod + add_ref[...] if has_add else prod

        @pl.when(k > 0)
        def _():
            acc_ref[...] += prod

        if use_scratch:
            @pl.when(k == nk - 1)
            def _():
                finish(acc_ref[...], refs, pid)

    def body(*refs):
        pid = (pl.program_id(0), pl.program_id(1), pl.program_id(2))
        step = (pid[0] * grid[1] + pid[1]) * nk + pid[2]
        carry.run(refs, n_in, n_main, step, grid[0] * grid[1] * nk, lambda: compute(refs, pid))

    tile_bytes = ((n_a if ta else 1) * _nbytes((tm, tk), pieces[0].dtype)
                  + _nbytes((tk, tn), b.dtype) + 2 * _nbytes((tm, tn), F32)
                  + (_nbytes((tm, tn), F32) if has_add else 0)
                  + sum(_nbytes((tm, r.shape[1]), r.dtype) for r in ep_rows) + sum(_nbytes((tm, w), d) for w, d in ep_outs))
    in_specs = a_specs + [b_spec] + ([o_spec] if has_add else [])
    in_specs += [pl.BlockSpec((tm, r.shape[1] // grid[1]), lambda i, j, k: (i, j)) for r in ep_rows]
    in_specs += [pl.BlockSpec(p.shape, lambda i, j, k: (0, 0)) for p in ep_params]
    args = tuple(pieces) + (b,) + ((add,) if has_add else ()) + tuple(ep_rows) + tuple(ep_params)
    scratch = [pltpu.VMEM((tm, tn), F32)] if use_scratch else []
    serial = bool(carry) or bool(ep_accs)
    c_in, c_out, c_shapes, c_alias, c_sems = carry.call_args(n_in, n_main)
    res = pl.pallas_call(
        body, name=name, grid=grid,
        in_specs=in_specs + c_in, out_specs=out_specs + c_out, out_shape=out_sds + c_shapes,
        input_output_aliases=c_alias, scratch_shapes=scratch + c_sems,
        compiler_params=pltpu.CompilerParams(
            dimension_semantics=("arbitrary",) * 3 if serial else ("parallel", "parallel", "arbitrary"),
            vmem_limit_bytes=_vmem_limit(tile_bytes)),
    )(*args, *carry.arrays)
    main = res[0] if ep_fn is None else list(res[:n_main])
    return (main, res[n_main:]) if carry else main


def _rowwise(fn, rows, params, outs, accs, *, tm, name, carry=None):
    carry = carry or _Carry()
    t = rows[0].shape[0]
    assert t % tm == 0, (name, t, tm)
    n_r, n_p, n_o, n_c = len(rows), len(params), len(outs), len(carry.arrays)

    def compute(refs, first):
        vals = [r[...].astype(F32) for r in refs[:n_r]] + [p[...] for p in refs[n_r:n_r + n_p]]
        res = fn(*vals)
        o_refs = refs[n_r + n_p + n_c:n_r + n_p + n_c + n_o]
        a_refs = refs[n_r + n_p + n_c + n_o:n_r + n_p + n_c + n_o + len(accs)]
        for o_ref, v in zip(o_refs, res[:n_o]):
            o_ref[...] = v.astype(o_ref.dtype)
        if accs:
            @pl.when(first)
            def _():
                for a_ref in a_refs:
                    a_ref[...] = jnp.zeros(a_ref.shape, F32)

            for a_ref, v in zip(a_refs, res[n_o:]):
                a_ref[...] += v

    def body(*refs):
        step = pl.program_id(0)
        carry.run(refs, n_r + n_p, n_o + len(accs), step, t // tm, lambda: compute(refs, step == 0))

    in_specs = [pl.BlockSpec((tm, r.shape[1]), lambda i: (i, 0)) for r in rows]
    in_specs += [pl.BlockSpec(p.shape, lambda i: (0, 0)) for p in params]
    out_specs = [pl.BlockSpec((tm, w), lambda i: (i, 0)) for w, _ in outs]
    out_specs += [pl.BlockSpec(s, lambda i: (0, 0)) for s in accs]
    out_shape = [jax.ShapeDtypeStruct((t, w), d) for w, d in outs]
    out_shape += [jax.ShapeDtypeStruct(s, F32) for s in accs]
    tile_bytes = sum(_nbytes((tm, r.shape[1]), r.dtype) for r in rows) + sum(_nbytes((tm, w), F32) for w, _ in outs)
    c_in, c_out, c_shapes, c_alias, c_sems = carry.call_args(n_r + n_p, n_o + len(accs))
    res = pl.pallas_call(
        body, name=name, grid=(t // tm,), in_specs=in_specs + c_in, out_specs=out_specs + c_out,
        out_shape=out_shape + c_shapes, input_output_aliases=c_alias, scratch_shapes=c_sems,
        compiler_params=pltpu.CompilerParams(
            dimension_semantics=("arbitrary",) if accs or carry else ("parallel",),
            vmem_limit_bytes=_vmem_limit(2 * tile_bytes)),
    )(*rows, *params, *carry.arrays)
    own = n_o + len(accs)
    return (list(res[:own]) + [res[own:]]) if carry else res


def _rms_stats(x):
    r = lax.rsqrt(jnp.mean(x * x, axis=-1, keepdims=True) + RMS_EPS)
    return x * r, r


def _rms_bwd(dy, xhat, r, g):
    dxhat = dy * g
    dx = r * (dxhat - xhat * jnp.mean(dxhat * xhat, axis=-1, keepdims=True))
    return dx, dy * xhat


def _sb_consts():
    lane = lax.broadcasted_iota(jnp.int32, (BLK, LANES), 1)
    head0 = lane < HEAD_DIM
    row = lax.broadcasted_iota(jnp.int32, (2 * BLK, BLK), 0) % BLK
    col = lax.broadcasted_iota(jnp.int32, (2 * BLK, BLK), 1)
    causal = col < row
    jj = lax.broadcasted_iota(jnp.int32, (BLK, BLK), 0)
    ss = lax.broadcasted_iota(jnp.int32, (BLK, BLK), 1)
    suffix = jnp.where(jj > ss, 1.0, 0.0).astype(BF16)
    return head0, causal, suffix


def _stack_heads(x, head0):
    zero = jnp.zeros_like(x)
    return jnp.concatenate([jnp.where(head0, x, zero), jnp.where(head0, zero, x)], axis=0)


def _sb_logits(z, causal, masked):
    sp = jnp.log(1.0 + jnp.exp(-jnp.abs(z)))
    log_keep = -(jnp.maximum(z, 0.0) + sp)
    log_beta = jnp.minimum(z, 0.0) - sp
    if masked:
        log_keep = jnp.where(causal, log_keep, 0.0)
    return log_keep, log_beta


def _suffix_sums(x, suffix):
    hi, lo = _split2(x)
    after = _dot(hi, suffix) + _dot(lo, suffix)
    total = jnp.broadcast_to(after[:, 0:1] + x[:, 0:1], x.shape)
    return after, total


def _sb_walk_back(i, state, per_chain, tile):
    def alive(st):
        worst = functools.reduce(jnp.maximum, [st[p][:, 0:1] for p in range(0, len(st), per_chain)])
        return jnp.max(worst) > EXP_UNDERFLOW

    def cond(c):
        return jnp.logical_and(c[0] < i, alive(c[1]))

    def body(c):
        return c[0] + 1, tile(i - 1 - c[0], c[1], False)

    return lax.while_loop(cond, body, (jnp.int32(0), state))[1]


def _lane_blocks(x, n):
    return [x[:, p * LANES:(p + 1) * LANES] for p in range(n)]


def _sb_fwd(qkv, b_sz, s_len, carry):
    nq = s_len // BLK
    n_pairs = SB_WIDTH // LANES
    ch = SB_FWD_CHAINS
    n_steps = n_pairs // ch
    scale = 1.0 / math.sqrt(HEAD_DIM)

    def compute(q_ref, k_ref, v_ref, o_ref):
        head0, causal, suffix = _sb_consts()

        def q_block(i, _):
            qs = pl.multiple_of(i * BLK, BLK)
            q_all = (q_ref[pl.ds(qs, BLK), :] * scale).astype(BF16)
            q01 = [_stack_heads(q, head0) for q in _lane_blocks(q_all, ch)]

            def tile(j, state, masked):
                ks = pl.multiple_of(j * BLK, BLK)
                ks_ = _lane_blocks(k_ref[pl.ds(ks, BLK), :].astype(BF16), ch)
                vs_ = _lane_blocks(v_ref[pl.ds(ks, BLK), :].astype(BF16), ch)
                zs = [_dot_nt(q01[p], ks_[p]) for p in range(ch)]
                logits = [_sb_logits(z, causal, masked) for z in zs]
                sums = [_suffix_sums(lg[0], suffix) for lg in logits]
                out = []
                for p in range(ch):
                    carry, acc = state[2 * p], state[2 * p + 1]
                    after, total = sums[p]
                    a = jnp.exp(logits[p][1] + carry + after)
                    if masked:
                        a = jnp.where(causal, a, 0.0)
                    a_hi, a_lo = _split2(a)
                    a_cat = jnp.concatenate([a_hi[:BLK], a_hi[BLK:], a_lo[:BLK], a_lo[BLK:]], axis=1)
                    v01 = _stack_heads(vs_[p], head0)
                    out += [carry + total, acc + _dot(a_cat, jnp.concatenate([v01, v01], axis=0))]
                return tuple(out)

            state = (jnp.zeros((2 * BLK, BLK), F32), jnp.zeros((BLK, LANES), F32)) * ch
            state = tile(i, state, True)
            state = _sb_walk_back(i, state, 2, tile)
            o_ref[pl.ds(qs, BLK), :] = jnp.concatenate([state[2 * p + 1] for p in range(ch)], axis=1)
            return 0

        lax.fori_loop(0, nq, q_block, 0)

    def body(*refs):
        step = pl.program_id(0) * n_steps + pl.program_id(1)
        o_ref = refs[3 + len(carry.arrays)]
        carry.run(refs, 3, 1, step, b_sz * n_steps, lambda: compute(refs[0], refs[1], refs[2], o_ref))

    blk = lambda off: pl.BlockSpec((None, s_len, ch * LANES), lambda b, p: (b, 0, off + p))
    c_in, c_out, c_shapes, c_alias, c_sems = carry.call_args(3, 1)
    res = pl.pallas_call(
        body, name="sb_fwd", grid=(b_sz, n_steps),
        in_specs=[blk(0), blk(n_steps), blk(2 * n_steps)] + c_in, out_specs=[blk(0)] + c_out,
        out_shape=[jax.ShapeDtypeStruct((b_sz, s_len, SB_WIDTH), F32)] + c_shapes,
        input_output_aliases=c_alias, scratch_shapes=c_sems,
        compiler_params=pltpu.CompilerParams(dimension_semantics=("arbitrary", "arbitrary"),
                                             vmem_limit_bytes=VMEM_CAP),
    )(qkv, qkv, qkv, *carry.arrays)
    return res[0], res[1:]


def _sb_bwd(qkv, o_sb, do_sb, b_sz, s_len, carry):
    nq = s_len // BLK
    n_pairs = SB_WIDTH // LANES
    ch = SB_BWD_CHAINS
    n_steps = n_pairs // ch
    scale = 1.0 / math.sqrt(HEAD_DIM)

    def compute(q_ref, k_ref, v_ref, o_ref, do_ref, dq_ref, dk_ref, dv_ref, dk_acc, dv_acc):
        head0, causal, suffix = _sb_consts()
        lrow = lax.broadcasted_iota(jnp.int32, (LANES, LANES), 0)
        ones_h0 = jnp.where(lrow < HEAD_DIM, 1.0, 0.0).astype(BF16)
        ones_h1 = jnp.where(lrow >= HEAD_DIM, 1.0, 0.0).astype(BF16)
        dk_acc[...] = jnp.zeros(dk_acc.shape, F32)
        dv_acc[...] = jnp.zeros(dv_acc.shape, F32)

        def q_block(i, _):
            qs = pl.multiple_of(i * BLK, BLK)
            q_all = (q_ref[pl.ds(qs, BLK), :] * scale).astype(BF16)
            do_all = do_ref[pl.ds(qs, BLK), :].astype(BF16)
            dd_all = do_all.astype(F32) * o_ref[pl.ds(qs, BLK), :]
            q01 = [_stack_heads(q, head0) for q in _lane_blocks(q_all, ch)]
            do01 = [_stack_heads(d, head0) for d in _lane_blocks(do_all, ch)]
            tot = []
            for dd in _lane_blocks(dd_all, ch):
                dd_hi, dd_lo = _split2(dd)
                tot.append(jnp.concatenate([_dot(dd_hi, ones_h0) + _dot(dd_lo, ones_h0),
                                            _dot(dd_hi, ones_h1) + _dot(dd_lo, ones_h1)], axis=0))

            def tile(j, state, masked):
                ks = pl.multiple_of(j * BLK, BLK)
                ks_ = _lane_blocks(k_ref[pl.ds(ks, BLK), :].astype(BF16), ch)
                vs_ = _lane_blocks(v_ref[pl.ds(ks, BLK), :].astype(BF16), ch)
                zs = [_dot_nt(q01[p], ks_[p]) for p in range(ch)]
                das = [_dot_nt(do01[p], vs_[p]) for p in range(ch)]
                logits = [_sb_logits(z, causal, masked) for z in zs]
                sums = [_suffix_sums(lg[0], suffix) for lg in logits]
                a_s, e_s = [], []
                for p in range(ch):
                    a = jnp.exp(logits[p][1] + state[3 * p] + sums[p][0])
                    if masked:
                        a = jnp.where(causal, a, 0.0)
                    a_s.append(a)
                    e_s.append(a * das[p])
                e_sums = [_suffix_sums(e, suffix) for e in e_s]
                out, dks, dvs = [], [], []
                for p in range(ch):
                    carry, rcarry, dq = state[3 * p:3 * p + 3]
                    e = e_s[p]
                    before = tot[p] - (rcarry + e_sums[p][0] + e)
                    beta = jnp.exp(logits[p][1])
                    dz = e * (1.0 - beta) - beta * before
                    if masked:
                        dz = jnp.where(causal, dz, 0.0)
                    dz_b = dz.astype(BF16)
                    dks.append(_dot_tn(dz_b, q01[p]))
                    dvs.append(_dot_tn(a_s[p].astype(BF16), do01[p]))
                    out += [carry + sums[p][1], rcarry + e_sums[p][1], dq + _dot(dz_b, ks_[p])]
                dk_acc[pl.ds(ks, BLK), :] += jnp.concatenate(dks, axis=1)
                dv_acc[pl.ds(ks, BLK), :] += jnp.concatenate(dvs, axis=1)
                return tuple(out)

            state = (jnp.zeros((2 * BLK, BLK), F32),) * (3 * ch)
            state = tile(i, state, True)
            state = _sb_walk_back(i, state, 3, tile)
            dq = [jnp.where(head0, state[3 * p + 2][:BLK], state[3 * p + 2][BLK:]) for p in range(ch)]
            dq_ref[pl.ds(qs, BLK), :] = (jnp.concatenate(dq, axis=1) * scale).astype(dq_ref.dtype)
            return 0

        lax.fori_loop(0, nq, q_block, 0)
        dk_ref[...] = dk_acc[...].astype(dk_ref.dtype)
        dv_ref[...] = dv_acc[...].astype(dv_ref.dtype)

    def body(*refs):
        step = pl.program_id(0) * n_steps + pl.program_id(1)
        n_c, n_o = len(carry.arrays), len(carry.out_shapes)
        own = refs[:5] + refs[5 + n_c:8 + n_c] + refs[8 + n_c + n_o:10 + n_c + n_o]
        carry.run(refs, 5, 3, step, b_sz * n_steps, lambda: compute(*own))

    blk = lambda off: pl.BlockSpec((None, s_len, ch * LANES), lambda b, p: (b, 0, off + p))
    once = lambda off: pl.BlockSpec((None, s_len, ch * LANES), lambda b, p: (b, 0, off + p),
                                    pipeline_mode=pl.Buffered(1))
    out_sd = jax.ShapeDtypeStruct((b_sz, s_len, SB_WIDTH), BF16)
    c_in, c_out, c_shapes, c_alias, c_sems = carry.call_args(5, 3)
    res = pl.pallas_call(
        body, name="sb_bwd", grid=(b_sz, n_steps),
        in_specs=[once(0), once(n_steps), once(2 * n_steps), once(0), once(0)] + c_in,
        out_specs=[blk(0), blk(0), blk(0)] + c_out, out_shape=[out_sd, out_sd, out_sd] + c_shapes,
        input_output_aliases=c_alias,
        scratch_shapes=[pltpu.VMEM((s_len, ch * LANES), F32), pltpu.VMEM((s_len, ch * LANES), F32)] + c_sems,
        compiler_params=pltpu.CompilerParams(dimension_semantics=("arbitrary", "arbitrary"),
                                             vmem_limit_bytes=VMEM_CAP),
    )(qkv, qkv, qkv, o_sb, do_sb, *carry.arrays)
    return res[:3], res[3:]


def _dil_consts(group, pair_idx, dilation):
    lane = lax.broadcasted_iota(jnp.int32, (BLK, LANES), 1)
    head0 = lane < HEAD_DIM
    row = lax.broadcasted_iota(jnp.int32, (2 * BLK, BLK), 0)
    qa = row % BLK
    kb = lax.broadcasted_iota(jnp.int32, (2 * BLK, BLK), 1)
    head = (group * DIL_HEADS_PER_GROUP + 2 * pair_idx + row // BLK).astype(F32)
    slope = jnp.exp((-ALIBI_MAX_BIAS * math.log(2.0) / DIL_HEADS) * (head + 1.0))
    valid_cur = kb <= qa
    valid_prev = kb >= qa
    bias_cur = -slope * ((qa - kb) * dilation).astype(F32)
    bias_prev = -slope * ((BLK + qa - kb) * dilation).astype(F32)
    return head0, valid_cur, valid_prev, bias_cur, bias_prev


def _dil_units(s_len, dilation):
    nb = s_len // dilation // BLK
    return [(r, n) for r in range(dilation) for n in range(nb)]


def _dil_rows(n, r, dilation):
    if dilation == 1:
        return pl.ds(n * BLK, BLK)
    return pl.ds(n * BLK * dilation + r, BLK, stride=dilation)


def _dil_scores(q01, k, bias, valid):
    s = _dot_nt(q01, k) * (1.0 / math.sqrt(HEAD_DIM)) + bias
    return jnp.where(valid, s, NEG)


def _dil_fwd(qkv, b_sz, s_len, carry):
    n_pairs = DIL_OUT_WIDTH // LANES
    q_off = 3 * SB_WIDTH // LANES
    per_kind = DIL_WIDTH // LANES

    def compute(pair_idx, qkv_refs, o_ref, lse_ref, m_s, l_s):
        m_s[...] = jnp.full(m_s.shape, NEG, F32)
        l_s[...] = jnp.zeros(l_s.shape, F32)
        o_ref[...] = jnp.zeros(o_ref.shape, F32)
        for g, (_, dilation) in enumerate(DIL_PAIRS):
            q_ref, k_ref, v_ref = qkv_refs[3 * g:3 * g + 3]
            head0, valid_cur, valid_prev, bias_cur, bias_prev = _dil_consts(g, pair_idx, dilation)
            units = _dil_units(s_len, dilation)
            for u0 in range(0, len(units), DIL_CHAINS):
                group = units[u0:u0 + DIL_CHAINS]
                rows_of = [_dil_rows(n, r, dilation) for r, n in group]
                scores, values = [], []
                for (r, n), rows in zip(group, rows_of):
                    q01 = _stack_heads(q_ref[rows, :].astype(BF16), head0)
                    sc = [_dil_scores(q01, k_ref[rows, :].astype(BF16), bias_cur, valid_cur)]
                    vals = [_stack_heads(v_ref[rows, :].astype(BF16), head0)]
                    if n > 0:
                        prev = _dil_rows(n - 1, r, dilation)
                        sc.append(_dil_scores(q01, k_ref[prev, :].astype(BF16), bias_prev, valid_prev))
                        vals.append(_stack_heads(v_ref[prev, :].astype(BF16), head0))
                    scores.append(sc)
                    values.append(vals)
                stats = []
                for sc, rows in zip(scores, rows_of):
                    m_blk = functools.reduce(jnp.maximum, [jnp.max(x, axis=-1, keepdims=True) for x in sc])
                    m_old = jnp.concatenate([m_s.at[0][rows, :], m_s.at[1][rows, :]], axis=0)
                    l_old = jnp.concatenate([l_s.at[0][rows, :], l_s.at[1][rows, :]], axis=0)
                    m_new = jnp.maximum(m_old, m_blk)
                    probs = [jnp.exp(x - m_new) for x in sc]
                    l_blk = functools.reduce(jnp.add, [jnp.sum(p, axis=-1, keepdims=True) for p in probs])
                    alpha = jnp.exp(m_old - m_new)
                    stats.append((m_new, alpha * l_old + l_blk, alpha, probs))
                for (m_new, l_new, alpha, probs), vals, rows in zip(stats, values, rows_of):
                    alpha_tok = jnp.where(head0, alpha[:BLK], alpha[BLK:])
                    p_cat = jnp.concatenate(
                        [h for p in probs for h in (p[:BLK].astype(BF16), p[BLK:].astype(BF16))], axis=1)
                    o_ref[rows, :] = alpha_tok * o_ref[rows, :] + _dot(p_cat, jnp.concatenate(vals, axis=0))
                    m_s.at[0][rows, :] = m_new[:BLK]
                    m_s.at[1][rows, :] = m_new[BLK:]
                    l_s.at[0][rows, :] = l_new[:BLK]
                    l_s.at[1][rows, :] = l_new[BLK:]
        lane = lax.broadcasted_iota(jnp.int32, (BLK, LANES), 1)
        for c in range(s_len // BLK):
            rows = pl.ds(c * BLK, BLK)
            l0, l1 = l_s.at[0][rows, :], l_s.at[1][rows, :]
            o_ref[rows, :] = o_ref[rows, :] / jnp.where(lane < HEAD_DIM, l0, l1)
            lse_ref.at[0][rows, :] = m_s.at[0][rows, :] + jnp.log(l0)
            lse_ref.at[1][rows, :] = m_s.at[1][rows, :] + jnp.log(l1)

    def body(*refs):
        pair_idx = pl.program_id(1)
        step = pl.program_id(0) * n_pairs + pair_idx
        n_c, n_o = len(carry.arrays), len(carry.out_shapes)
        o_ref, lse_ref = refs[9 + n_c:11 + n_c]
        m_s, l_s = refs[11 + n_c + n_o:13 + n_c + n_o]
        carry.run(refs, 9, 2, step, b_sz * n_pairs, lambda: compute(pair_idx, refs[:9], o_ref, lse_ref, m_s, l_s))

    in_specs = []
    for g in range(len(DIL_PAIRS)):
        for kind in range(3):
            off = q_off + kind * per_kind + g * n_pairs
            in_specs.append(pl.BlockSpec((None, s_len, LANES), lambda b, p, off=off: (b, 0, off + p)))
    c_in, c_out, c_shapes, c_alias, c_sems = carry.call_args(9, 2)
    res = pl.pallas_call(
        body, name="dil_fwd", grid=(b_sz, n_pairs),
        in_specs=in_specs + c_in,
        out_specs=[pl.BlockSpec((None, s_len, LANES), lambda b, p: (b, 0, p)),
                   pl.BlockSpec((None, None, 2, s_len, LANES), lambda b, p: (b, p, 0, 0, 0))] + c_out,
        out_shape=[jax.ShapeDtypeStruct((b_sz, s_len, DIL_OUT_WIDTH), F32),
                   jax.ShapeDtypeStruct((b_sz, n_pairs, 2, s_len, LANES), F32)] + c_shapes,
        input_output_aliases=c_alias,
        scratch_shapes=[pltpu.VMEM((2, s_len, LANES), F32), pltpu.VMEM((2, s_len, LANES), F32)] + c_sems,
        compiler_params=pltpu.CompilerParams(dimension_semantics=("arbitrary", "arbitrary"),
                                             vmem_limit_bytes=VMEM_CAP),
    )(*([qkv] * 9), *carry.arrays)
    return res[0], res[1], res[2:]


def _dil_bwd(qkv, o_dl, lse, do_dl, b_sz, s_len, carry):
    n_pairs = DIL_OUT_WIDTH // LANES
    n_groups = len(DIL_PAIRS)
    q_off = 3 * SB_WIDTH // LANES
    per_kind = DIL_WIDTH // LANES

    def compute(pair_idx, group, q_ref, k_ref, v_ref, o_ref, lse_ref, do_ref, dq_ref, dk_ref, dv_ref, d_s, dq_s, dk_s, dv_s):
        lrow = lax.broadcasted_iota(jnp.int32, (LANES, LANES), 0)
        ones_h0 = jnp.where(lrow < HEAD_DIM, 1.0, 0.0).astype(BF16)
        ones_h1 = jnp.where(lrow >= HEAD_DIM, 1.0, 0.0).astype(BF16)
        for c in range(s_len // BLK):
            rows = pl.ds(c * BLK, BLK)
            dd_hi, dd_lo = _split2(do_ref[rows, :] * o_ref[rows, :])
            d_s.at[0][rows, :] = _dot(dd_hi, ones_h0) + _dot(dd_lo, ones_h0)
            d_s.at[1][rows, :] = _dot(dd_hi, ones_h1) + _dot(dd_lo, ones_h1)
        dk_s[...] = jnp.zeros(dk_s.shape, F32)
        dv_s[...] = jnp.zeros(dv_s.shape, F32)

        def one_group(g, dilation):
            head0, valid_cur, valid_prev, bias_cur, bias_prev = _dil_consts(g, pair_idx, dilation)
            units = _dil_units(s_len, dilation)
            scale = 1.0 / math.sqrt(HEAD_DIM)
            for u0 in range(0, len(units), DIL_CHAINS):
                chunk = units[u0:u0 + DIL_CHAINS]
                loaded = []
                for r, n in chunk:
                    rows = _dil_rows(n, r, dilation)
                    q01 = _stack_heads(q_ref[rows, :].astype(BF16), head0)
                    do01 = _stack_heads(do_ref[rows, :].astype(BF16), head0)
                    lse01 = jnp.concatenate([lse_ref.at[0][rows, :], lse_ref.at[1][rows, :]], axis=0)
                    d01 = jnp.concatenate([d_s.at[0][rows, :], d_s.at[1][rows, :]], axis=0)
                    blocks = [(rows, bias_cur, valid_cur)]
                    if n > 0:
                        blocks.append((_dil_rows(n - 1, r, dilation), bias_prev, valid_prev))
                    parts = []
                    for krows, bias, valid in blocks:
                        k = k_ref[krows, :].astype(BF16)
                        v = v_ref[krows, :].astype(BF16)
                        parts.append((krows, k, _dil_scores(q01, k, bias, valid), _dot_nt(do01, v)))
                    loaded.append((rows, q01, do01, lse01, d01, parts))
                grads = []
                for rows, q01, do01, lse01, d01, parts in loaded:
                    for krows, k, sc, dp in parts:
                        p = jnp.exp(sc - lse01)
                        grads.append((p.astype(BF16), (p * (dp - d01) * scale).astype(BF16)))
                it = iter(grads)
                updates = []
                for rows, q01, do01, lse01, d01, parts in loaded:
                    dq = jnp.zeros((2 * BLK, LANES), F32)
                    for krows, k, sc, dp in parts:
                        p_b, ds = next(it)
                        dq = dq + _dot(ds, k)
                        updates.append((krows, _dot_tn(ds, q01), _dot_tn(p_b, do01)))
                    dq_s[rows, :] = jnp.where(head0, dq[:BLK], dq[BLK:])
                for krows, dk, dv in updates:
                    dk_s[krows, :] = dk_s[krows, :] + dk
                    dv_s[krows, :] = dv_s[krows, :] + dv

        for g, (_, dilation) in enumerate(DIL_PAIRS):
            pl.when(group == g)(functools.partial(one_group, g, dilation))
        dq_ref[...] = dq_s[...].astype(dq_ref.dtype)
        dk_ref[...] = dk_s[...].astype(dk_ref.dtype)
        dv_ref[...] = dv_s[...].astype(dv_ref.dtype)

    def body(*refs):
        pair_idx, group = pl.program_id(1), pl.program_id(2)
        step = (pl.program_id(0) * n_pairs + pair_idx) * n_groups + group
        n_c, n_o = len(carry.arrays), len(carry.out_shapes)
        own = refs[:6] + refs[6 + n_c:9 + n_c] + refs[9 + n_c + n_o:13 + n_c + n_o]
        carry.run(refs, 6, 3, step, b_sz * n_pairs * n_groups, lambda: compute(pair_idx, group, *own))

    def qkv_spec(kind):
        return pl.BlockSpec((None, s_len, LANES),
                            lambda b, p, g: (b, 0, q_off + kind * per_kind + g * n_pairs + p))

    tok_spec = pl.BlockSpec((None, s_len, LANES), lambda b, p, g: (b, 0, p))
    out_spec = pl.BlockSpec((None, s_len, LANES), lambda b, p, g: (b, 0, g * n_pairs + p))
    out_sd = jax.ShapeDtypeStruct((b_sz, s_len, DIL_WIDTH), BF16)
    c_in, c_out, c_shapes, c_alias, c_sems = carry.call_args(6, 3)
    res = pl.pallas_call(
        body, name="dil_bwd", grid=(b_sz, n_pairs, n_groups),
        in_specs=[qkv_spec(0), qkv_spec(1), qkv_spec(2), tok_spec,
                  pl.BlockSpec((None, None, 2, s_len, LANES), lambda b, p, g: (b, p, 0, 0, 0)), tok_spec] + c_in,
        out_specs=[out_spec, out_spec, out_spec] + c_out,
        out_shape=[out_sd, out_sd, out_sd] + c_shapes,
        input_output_aliases=c_alias,
        scratch_shapes=[pltpu.VMEM((2, s_len, LANES), F32)] + [pltpu.VMEM((s_len, LANES), F32)] * 3 + c_sems,
        compiler_params=pltpu.CompilerParams(dimension_semantics=("arbitrary", "arbitrary", "arbitrary"),
                                             vmem_limit_bytes=VMEM_CAP),
    )(qkv, qkv, qkv, o_dl, lse, do_dl, *carry.arrays)
    return res[:3], res[3:]


def _mesh_pos():
    return lax.axis_index("x"), lax.axis_index("y"), lax.axis_index("c")


def _other_chips(x, y):
    return [(1 - x, y), (x, 1 - y), (1 - x, 1 - y)]


def _hbm_specs(n):
    return [pl.BlockSpec(memory_space=pl.ANY)] * n


SWAPPED = ("w_ffn_in",)


def _slot(x, y, swapped):
    return 2 * y + x if swapped else 2 * x + y


def _cast_to_slab(w, name, swapped=False):
    rows, cols = w.shape
    mine = jnp.reshape(_slot(lax.axis_index("x"), lax.axis_index("y"), swapped), (1,)).astype(jnp.int32)

    def body(idx_ref, w_ref, o_ref):
        o_ref[...] = w_ref[...].astype(BF16)

    return pl.pallas_call(
        body, name=name,
        grid_spec=pltpu.PrefetchScalarGridSpec(
            num_scalar_prefetch=1, grid=(1,),
            in_specs=[pl.BlockSpec((rows, cols), lambda i, idx: (0, 0))],
            out_specs=pl.BlockSpec((None, rows, cols), lambda i, idx: (idx[0], 0, 0))),
        out_shape=jax.ShapeDtypeStruct((N_CHIPS, rows, cols), BF16),
        compiler_params=pltpu.CompilerParams(vmem_limit_bytes=_vmem_limit(rows * cols * 6)),
    )(mine, w)


def _gather_issue(slabs, send_sems, recv_sems, swapped):
    x, y, c = _mesh_pos()
    for k, slab in enumerate(slabs):
        half = slab.shape[1] // 2
        rows = slab.at[_slot(x, y, swapped[k]), pl.ds(c * half, half), :]
        for r, (px, py) in enumerate(_other_chips(x, y)):
            pltpu.make_async_remote_copy(
                src_ref=rows, dst_ref=rows, send_sem=send_sems.at[6 * k + r], recv_sem=recv_sems.at[6 * k + r],
                device_id=(px, py, c), device_id_type=MESH).start()


def _gather_complete(slabs, send_sems, recv_sems, swapped):
    x, y, c = _mesh_pos()
    chips = _other_chips(x, y)

    def copy(k, sem, block, rows, to):
        ref = slabs[k].at[block, rows, :]
        return pltpu.make_async_remote_copy(
            src_ref=ref, dst_ref=ref, send_sem=send_sems.at[sem], recv_sem=recv_sems.at[sem],
            device_id=to, device_id_type=MESH)

    for k, slab in enumerate(slabs):
        half = slab.shape[1] // 2
        for r, (px, py) in enumerate(chips):
            theirs = _slot(px, py, swapped[k])
            copy(k, 6 * k + r, theirs, pl.ds(c * half, half), (px, py, c)).wait_recv()
            copy(k, 6 * k + 3 + r, theirs, pl.ds(c * half, half), (x, y, 1 - c)).start()
    for k, slab in enumerate(slabs):
        half = slab.shape[1] // 2
        for r, (px, py) in enumerate(chips):
            copy(k, 6 * k + 3 + r, _slot(px, py, swapped[k]), pl.ds((1 - c) * half, half), (x, y, 1 - c)).wait_recv()
    for k, slab in enumerate(slabs):
        half = slab.shape[1] // 2
        for r, (px, py) in enumerate(chips):
            copy(k, 6 * k + r, _slot(x, y, swapped[k]), pl.ds(c * half, half), (px, py, c)).wait_send()
            copy(k, 6 * k + 3 + r, _slot(px, py, swapped[k]), pl.ds(c * half, half), (x, y, 1 - c)).wait_send()


def _gather_sems(n):
    return [pltpu.SemaphoreType.DMA((6 * n,)), pltpu.SemaphoreType.DMA((6 * n,))]


def _gather_carry(slabs, names):
    swapped = [k in SWAPPED for k in names]
    return _Carry(slabs, [jax.ShapeDtypeStruct(a.shape, a.dtype) for a in slabs], True, _gather_sems(len(slabs)),
                  lambda ins, outs, sems: _gather_issue(outs, *sems, swapped),
                  lambda ins, outs, sems: _gather_complete(outs, *sems, swapped))


def _pair_copies(ins, outs, send_sems, recv_sems):
    x, y, c = _mesh_pos()
    copies = []
    for k, g in enumerate(ins):
        half = g.shape[1] // 2
        copies.append(pltpu.make_async_remote_copy(
            src_ref=g.at[:, pl.ds((1 - c) * half, half), :], dst_ref=outs[k],
            send_sem=send_sems.at[k], recv_sem=recv_sems.at[k],
            device_id=(x, y, 1 - c), device_id_type=MESH))
    return copies


def _pair_carry(grads):
    n = len(grads)

    def start(ins, outs, sems):
        for cp in _pair_copies(ins, outs, *sems):
            cp.start()

    def finish(ins, outs, sems):
        for cp in _pair_copies(ins, outs, *sems):
            cp.wait()

    return _Carry(grads, [jax.ShapeDtypeStruct((N_CHIPS, g.shape[1] // 2, g.shape[2]), g.dtype) for g in grads], False,
                  [pltpu.SemaphoreType.DMA((n,)), pltpu.SemaphoreType.DMA((n,))], start, finish)


def _pair_exchange(grads, tag):
    carry = _pair_carry(grads)
    n = len(grads)

    def body(*refs):
        carry.start(refs[:n], refs[n:2 * n], refs[2 * n:])
        carry.finish(refs[:n], refs[n:2 * n], refs[2 * n:])

    return pl.pallas_call(
        body, name="grad_pair_exchange_" + tag, in_specs=_hbm_specs(n), out_specs=_hbm_specs(n),
        out_shape=carry.out_shapes, scratch_shapes=carry.sems,
    )(*grads)


def _pair_sum(grad, other, name, swapped):
    _, rows, cols = grad.shape
    half = rows // 2
    x, y, c = _mesh_pos()
    idx = jnp.stack([c, _slot(x, y, swapped)]).astype(jnp.int32)

    def body(idx_ref, g_ref, p_ref, own_ref, sb_ref):
        s = g_ref[...] + p_ref[...].astype(F32)
        sb_ref[...] = s.astype(BF16)

        @pl.when(pl.program_id(0) == idx_ref[1])
        def _():
            own_ref[...] = s

    blk = pl.BlockSpec((None, half, cols), lambda p, idx: (p, 0, 0))
    return pl.pallas_call(
        body, name=name,
        grid_spec=pltpu.PrefetchScalarGridSpec(
            num_scalar_prefetch=1, grid=(N_CHIPS,),
            in_specs=[pl.BlockSpec((None, half, cols), lambda p, idx: (p, idx[0], 0)), blk],
            out_specs=[pl.BlockSpec((half, cols), lambda p, idx: (0, 0)), blk]),
        out_shape=[jax.ShapeDtypeStruct((half, cols), F32), jax.ShapeDtypeStruct((N_CHIPS, half, cols), BF16)],
        compiler_params=pltpu.CompilerParams(dimension_semantics=("arbitrary",),
                                             vmem_limit_bytes=_vmem_limit(4 * half * cols * 4)),
    )(idx, grad, other)


def _chip_copies(sums_bf16, lands, send_sems, recv_sems, swapped):
    x, y, c = _mesh_pos()
    return [pltpu.make_async_remote_copy(
        src_ref=sums_bf16[k].at[_slot(px, py, swapped[k])], dst_ref=lands[k].at[r],
        send_sem=send_sems.at[3 * k + r], recv_sem=recv_sems.at[3 * k + r],
        device_id=(px, py, c), device_id_type=MESH)
        for k in range(len(sums_bf16)) for r, (px, py) in enumerate(_other_chips(x, y))]


def _chip_carry(sums_bf16, names):
    swapped = [k in SWAPPED for k in names]

    def start(ins, outs, sems):
        for cp in _chip_copies(ins, outs, *sems, swapped):
            cp.start()

    def finish(ins, outs, sems):
        for cp in _chip_copies(ins, outs, *sems, swapped):
            cp.wait()

    return _Carry(sums_bf16, _chip_landing(sums_bf16), False, _chip_sems(len(sums_bf16)), start, finish)


def _chip_sems(n):
    return [pltpu.SemaphoreType.DMA((3 * n,)), pltpu.SemaphoreType.DMA((3 * n,))]


def _chip_landing(sums_bf16):
    return [jax.ShapeDtypeStruct((N_CHIPS - 1,) + s.shape[1:], BF16) for s in sums_bf16]


def _chip_sum(own, landed, name):
    rows, cols = own.shape
    core = jnp.reshape(lax.axis_index("c"), (1,)).astype(jnp.int32)

    def body(core_ref, o_ref, l_ref, out_ref):
        out_ref[...] = ((o_ref[...] + l_ref[0].astype(F32)) + l_ref[1].astype(F32)) + l_ref[2].astype(F32)

    return pl.pallas_call(
        body, name=name,
        grid_spec=pltpu.PrefetchScalarGridSpec(
            num_scalar_prefetch=1, grid=(1,),
            in_specs=[pl.BlockSpec((rows, cols), lambda i, core_ref: (0, 0)),
                      pl.BlockSpec((N_CHIPS - 1, rows, cols), lambda i, core_ref: (0, 0, 0))],
            out_specs=pl.BlockSpec((rows, cols), lambda i, core_ref: (core_ref[0], 0))),
        out_shape=jax.ShapeDtypeStruct((2 * rows, cols), F32),
        compiler_params=pltpu.CompilerParams(vmem_limit_bytes=_vmem_limit(3 * rows * cols * 4)),
    )(core, own, landed)


def _final_exchange(fulls, v):
    n = len(fulls)
    rows, cols = v.shape
    n_dev = 8

    def body(*refs):
        v_ref, out_ref = refs[0], refs[1 + 2 * n]
        outs = refs[1 + n:1 + 2 * n]
        buf, v_send, v_recv, h_send, h_recv = refs[2 + 2 * n:]
        x, y, c = _mesh_pos()
        me = 4 * x + 2 * y + c
        buf[me] = v_ref[...]
        peers = [(1 - x if r & 4 else x, 1 - y if r & 2 else y, 1 - c if r & 1 else c) for r in range(1, n_dev)]
        copies = []
        for r, peer in enumerate(peers):
            copies.append(pltpu.make_async_remote_copy(
                src_ref=v_ref, dst_ref=buf.at[me], send_sem=v_send.at[r], recv_sem=v_recv.at[r],
                device_id=peer, device_id_type=MESH))
        for k in range(n):
            half = fulls[k].shape[0] // 2
            mine = outs[k].at[pl.ds(c * half, half), :]
            copies.append(pltpu.make_async_remote_copy(
                src_ref=mine, dst_ref=mine, send_sem=h_send.at[k], recv_sem=h_recv.at[k],
                device_id=(x, y, 1 - c), device_id_type=MESH))
        for cp in copies:
            cp.start()
        for r, (px, py, pc) in enumerate(peers):
            pltpu.make_async_remote_copy(
                src_ref=v_ref, dst_ref=buf.at[4 * px + 2 * py + pc], send_sem=v_send.at[r], recv_sem=v_recv.at[r],
                device_id=(px, py, pc), device_id_type=MESH).wait_recv()
        for k in range(n):
            half = fulls[k].shape[0] // 2
            theirs = outs[k].at[pl.ds((1 - c) * half, half), :]
            pltpu.make_async_remote_copy(
                src_ref=theirs, dst_ref=theirs, send_sem=h_send.at[k], recv_sem=h_recv.at[k],
                device_id=(x, y, 1 - c), device_id_type=MESH).wait_recv()
        for cp in copies:
            cp.wait_send()
        acc = buf[0]
        for d in range(1, n_dev):
            acc = acc + buf[d]
        out_ref[...] = acc
        out_ref[3:4, :] = jnp.broadcast_to(jnp.sum(acc[3:4, :], axis=1, keepdims=True), (1, cols))

    vm = pl.BlockSpec(memory_space=pltpu.VMEM)
    res = pl.pallas_call(
        body, name="final_exchange",
        in_specs=[vm] + _hbm_specs(n), out_specs=_hbm_specs(n) + [vm],
        out_shape=[jax.ShapeDtypeStruct(f.shape, F32) for f in fulls] + [jax.ShapeDtypeStruct((rows, cols), F32)],
        input_output_aliases={1 + k: k for k in range(n)},
        scratch_shapes=[pltpu.VMEM((n_dev, rows, cols), F32),
                        pltpu.SemaphoreType.DMA((n_dev - 1,)), pltpu.SemaphoreType.DMA((n_dev - 1,)),
                        pltpu.SemaphoreType.DMA((n,)), pltpu.SemaphoreType.DMA((n,))],
    )(v, *fulls)
    return res[:n], res[n]


def _adamw_math(w, g, m, v):
    m = ADAM_B1 * m + (1.0 - ADAM_B1) * g
    v = ADAM_B2 * v + (1.0 - ADAM_B2) * (g * g)
    m_hat = m / (1.0 - ADAM_B1 ** ADAM_STEP)
    v_hat = v / (1.0 - ADAM_B2 ** ADAM_STEP)
    delta = -ADAM_LR * (m_hat / (jnp.sqrt(v_hat) + ADAM_EPS) + ADAM_WD * w)
    return delta, m, v


def _adamw(w, g, m, v, name):
    rows, cols = w.shape
    tm = rows // 2 if (rows // 2) % 8 == 0 else rows
    return _rowwise(_adamw_math, [w, g, m, v], [], [(cols, F32)] * 3, [], tm=tm, name=name)


def _unshard_cols(gathered):
    n, r, c = gathered.shape
    return jnp.transpose(gathered, (1, 0, 2)).reshape(r, n * c)


def _shard_cols(full):
    r, nc = full.shape
    return jnp.transpose(full.reshape(r, N_CHIPS, nc // N_CHIPS), (1, 0, 2))


LATE = ["w_sb_up", "w_dil_up", "w_out", "w_ffn_in", "w_ffn_out"]


def _late_weights(slabs, d_model, d_ff):
    g = dict(zip(LATE, slabs))
    return (_unshard_cols(g["w_sb_up"]), _unshard_cols(g["w_dil_up"]), g["w_out"].reshape(d_model, d_model),
            _unshard_cols(g["w_ffn_in"]), g["w_ffn_out"].reshape(d_ff, d_model))


ROW_SHARDED = ("w_in", "w_out", "w_ffn_in", "w_ffn_out")


def _chip_major(grads):
    out = []
    for k, g in grads.items():
        if k in ROW_SHARDED:
            out.append(g.reshape(N_CHIPS, g.shape[0] // N_CHIPS, g.shape[1]))
        else:
            out.append(_shard_cols(g))
    return out


def _pair_sums(full, others, names):
    return [_pair_sum(g, o, "grad_pair_sum_" + k, k in SWAPPED) for g, o, k in zip(full, others, names)]


def _chip_sums(pair, landed, names):
    return {k: _chip_sum(p[0], l, "grad_chip_sum_" + k) for p, l, k in zip(pair, landed, names)}


def _fwd_bwd(x, loss_target, g_mix, g_ffn, g_fin, slab_in, late_slabs):
    b_sz, s_len, d_model = x.shape
    t = b_sz * s_len
    d_ff = late_slabs[-1].shape[1] * N_CHIPS
    x2d = x.reshape(t, d_model)
    tgt2d = loss_target.reshape(t, d_model)

    u, (slab_in,) = _rowwise(lambda xv, g: (_rms_stats(xv)[0] * g,), [x2d], [g_mix], [(d_model, BF16)], [], tm=512,
                             name="norm_mix", carry=_gather_carry([slab_in], ["w_in"]))
    wt_in = slab_in.reshape(-1, d_model)
    qkv, (slab_ffn_out,) = _mm(u, wt_in, tb=True, b_cols=(0, QKV_WIDTH), tm=2048, tn=768, tk=d_model, name="proj_qkv",
                               carry=_gather_carry(late_slabs[4:], LATE[4:]))
    gates = _mm(u, wt_in, tb=True, b_cols=(QKV_WIDTH, 2 * d_model), out_dtype=BF16, tm=t, tn=256, tk=d_model,
                name="proj_gates")
    qkv3 = qkv.reshape(b_sz, s_len, QKV_WIDTH)
    o_sb, (slab_ffn_in,) = _sb_fwd(qkv3, b_sz, s_len, _gather_carry(late_slabs[3:4], LATE[3:4]))
    o_dl, lse, small_slabs = _dil_fwd(qkv3, b_sz, s_len, _gather_carry(late_slabs[:3], LATE[:3]))
    wf_sb_up, wf_dil_up, wf_out, wf_ffn_in, wf_ffn_out = _late_weights(
        list(small_slabs) + [slab_ffn_in, slab_ffn_out], d_model, d_ff)
    o_sb2, o_dl2 = o_sb.reshape(t, SB_WIDTH), o_dl.reshape(t, DIL_OUT_WIDTH)
    y_sb = _mm(o_sb2, wf_sb_up, out_dtype=BF16, tm=1024, tn=1024, tk=SB_WIDTH, name="sb_up")
    y_dl = _mm(o_dl2, wf_dil_up, out_dtype=BF16, tm=1024, tn=1024, tk=DIL_OUT_WIDTH, name="dil_up")

    def merge_fn(gt, ys, yd):
        return (_sigmoid(gt[:, :d_model]) * ys + _sigmoid(gt[:, d_model:]) * yd,)

    (merged,) = _rowwise(merge_fn, [gates, y_sb, y_dl], [], [(d_model, BF16)], [], tm=512, name="merge")
    x1 = _mm(merged, wf_out, add=x2d, tm=512, tn=1024, tk=d_model, name="mix_out")
    (u2,) = _rowwise(lambda xv, g: (_rms_stats(xv)[0] * g,), [x1], [g_ffn], [(d_model, BF16)], [], tm=512, name="norm_ffn")
    half_ff = d_ff // 2

    def act_fn(hv):
        gate = hv[:, :half_ff]
        return hv, gate * _sigmoid(gate) * hv[:, half_ff:]

    h, act = _mm(u2, wf_ffn_in, tm=512, tn=d_ff, tk=d_model, name="ffn_in", sub=2,
                 epilogue=(act_fn, [], [], [(d_ff, BF16), (half_ff, BF16)], []))
    def head_fn(xv, tg, g):
        xhat, r = _rms_stats(xv)
        err = xhat * g - tg
        dy = err * (1.0 / d_model)
        dx, dg_rows = _rms_bwd(dy, xhat, r, g)
        loss_lanes = (0.5 / d_model) * jnp.sum(err * err, axis=0, keepdims=True)
        return dx, dx, jnp.sum(dg_rows, axis=0, keepdims=True), loss_lanes

    dx2, dx2_b, dg_fin, loss_lanes = _mm(
        act, wf_ffn_out, add=x1, tm=512, tn=1024, tk=d_ff, name="ffn_out",
        epilogue=(head_fn, [tgt2d], [g_fin], [(d_model, F32), (d_model, BF16)], [(1, d_model), (1, d_model)]))

    def dact_fn(da, hv):
        gate, up = hv[:, :half_ff], hv[:, half_ff:]
        sg = _sigmoid(gate)
        dgate = da * up * (sg * (1.0 + gate * (1.0 - sg)))
        return (jnp.concatenate([dgate, da * (gate * sg)], axis=1),)

    (dh,) = _mm(dx2_b, wf_ffn_out, tb=True, tm=512, tn=half_ff, tk=d_model, name="ffn_out_dx", sub=2,
                epilogue=(dact_fn, [h], [], [(d_ff, BF16)], []))
    gw_ffn_out = _mm(act, dx2_b, ta=True, tm=256, tn=d_model, tk=t, name="ffn_out_dw")
    def norm_bwd_fn(du_, dres, xv, g):
        xhat, r = _rms_stats(xv)
        dx, dg_rows = _rms_bwd(du_, xhat, r, g)
        return dres + dx, jnp.sum(dg_rows, axis=0, keepdims=True)

    def norm_bwd_twice(*args):
        dx, dg = norm_bwd_fn(*args)
        return dx, dx, dg

    dx1, dx1_b, dg_ffn = _mm(dh, wf_ffn_in, tb=True, tm=512, tn=1024, tk=2 * d_ff, name="ffn_in_dx",
                             epilogue=(norm_bwd_twice, [dx2, x1], [g_ffn], [(d_model, F32), (d_model, BF16)], [(1, d_model)]))
    gwt_ffn_in = _mm(dh, u2, ta=True, tm=512, tn=d_model, tk=t, name="ffn_in_dw")

    dmerged = _mm(dx1_b, wf_out, tb=True, out_dtype=BF16, tm=512, tn=1024, tk=d_model, name="mix_out_dx")
    gw_out = _mm(merged, dx1_b, ta=True, tm=256, tn=d_model, tk=t, name="mix_out_dw")

    def merge_bwd_fn(gt, ys, yd, dm):
        s_sb, s_dl = _sigmoid(gt[:, :d_model]), _sigmoid(gt[:, d_model:])
        dgates = jnp.concatenate([dm * ys * s_sb * (1.0 - s_sb), dm * yd * s_dl * (1.0 - s_dl)], axis=1)
        return dgates, dm * s_sb, dm * s_dl

    full_big = _chip_major({"w_out": gw_out, "w_ffn_in": gwt_ffn_in, "w_ffn_out": gw_ffn_out})
    dgates, dy_sb, dy_dl, others_big = _rowwise(
        merge_bwd_fn, [gates, y_sb, y_dl, dmerged], [], [(2 * d_model, BF16), (d_model, BF16), (d_model, BF16)], [],
        tm=256, name="merge_bwd", carry=_pair_carry(full_big))
    pair_big = _pair_sums(full_big, others_big, LATE[2:])
    do_sb = _mm(dy_sb, wf_sb_up, tb=True, out_dtype=BF16, tm=1024, tn=SB_WIDTH, tk=d_model, name="sb_up_dx")
    gw_sb_up = _mm(o_sb2, dy_sb, ta=True, tm=SB_WIDTH, tn=1024, tk=512, name="sb_up_dw")
    do_dl = _mm(dy_dl, wf_dil_up, tb=True, tm=1024, tn=DIL_OUT_WIDTH, tk=d_model, name="dil_up_dx")
    gw_dil_up = _mm(o_dl2, dy_dl, ta=True, tm=DIL_OUT_WIDTH, tn=1024, tk=512, name="dil_up_dw")
    full_small = _chip_major({"w_sb_up": gw_sb_up, "w_dil_up": gw_dil_up})
    (dq_sb, dk_sb, dv_sb), brought = _sb_bwd(
        qkv3, o_sb, do_sb.reshape(b_sz, s_len, SB_WIDTH), b_sz, s_len,
        _chip_carry([p[1] for p in pair_big[:2]], LATE[2:4]) + _pair_carry(full_small))
    pair_small = _pair_sums(full_small, brought[2:], LATE[:2])
    (dq_dl, dk_dl, dv_dl), landed_b = _dil_bwd(
        qkv3, o_dl, lse, do_dl.reshape(b_sz, s_len, DIL_OUT_WIDTH), b_sz, s_len,
        _chip_carry([pair_big[2][1], pair_small[0][1], pair_small[1][1]], [LATE[4], LATE[0], LATE[1]]))
    pair = pair_small + pair_big
    landed = [landed_b[1], landed_b[2], brought[0], brought[1], landed_b[0]]
    dproj = [a.reshape(t, -1) for a in (dq_sb, dk_sb, dv_sb, dq_dl, dk_dl, dv_dl)] + [dgates]
    gwt_in, gwt_in_b = _mm(dproj, u, ta=True, tm=256, tn=d_model, tk=t, name="proj_dw",
                           epilogue=(lambda tile: (tile, tile), [], [], [(d_model, F32), (d_model, BF16)], []))
    full_in = _chip_major({"w_in": gwt_in})
    pair_in = _pair_sums(full_in, _pair_exchange(_chip_major({"w_in": gwt_in_b}), "w_in"), ["w_in"])
    (dx, dg_mix), landed_in = _mm(
        dproj, wt_in, tm=512, tn=1024, tk=wt_in.shape[0], name="proj_dx",
        carry=_chip_carry([p[1] for p in pair_in], ["w_in"]),
        epilogue=(norm_bwd_fn, [dx1, x2d], [g_mix], [(d_model, F32)], [(1, d_model)]))

    grads = _chip_sums(pair, landed, LATE)
    grads.update(_chip_sums(pair_in, landed_in, ["w_in"]))
    return dx, grads, dg_mix, dg_ffn, dg_fin, loss_lanes


def kernel(x, norm_mix_g, w_in, w_sb_up, w_dil_up, w_out, norm_ffn_g, w_ffn_in, w_ffn_out, norm_final_g, loss_target, m_norm_mix_g, m_w_in, m_w_sb_up, m_w_dil_up, m_w_out, m_norm_ffn_g, m_w_ffn_in, m_w_ffn_out, m_norm_final_g, v_norm_mix_g, v_w_in, v_w_sb_up, v_w_dil_up, v_w_out, v_norm_ffn_g, v_w_ffn_in, v_w_ffn_out, v_norm_final_g):
    b_sz, s_len, d_model = x.shape
    d_ff = w_ffn_out.shape[1] * N_CHIPS
    g_mix, g_ffn, g_fin = norm_mix_g, norm_ffn_g, norm_final_g.reshape(1, d_model)

    names = ["w_in", "w_sb_up", "w_dil_up", "w_out", "w_ffn_in", "w_ffn_out"]
    shards = {"w_in": jnp.swapaxes(w_in[0], 0, 1), "w_sb_up": w_sb_up[0], "w_dil_up": w_dil_up[0], "w_out": w_out[0],
              "w_ffn_in": w_ffn_in[0], "w_ffn_out": w_ffn_out[0]}
    slab_in = _cast_to_slab(shards["w_in"], "cast_w_in")
    late_slabs = [_cast_to_slab(shards[k], "cast_" + k, k in SWAPPED) for k in LATE]

    dx, grads, dg_mix, dg_ffn, dg_fin, loss_lanes = _fwd_bwd(
        x, loss_target, g_mix, g_ffn, g_fin, slab_in, late_slabs)

    small = jnp.concatenate([dg_mix, dg_ffn, dg_fin, loss_lanes, jnp.zeros((4, d_model), F32)], axis=0)
    full_grads, small = _final_exchange([grads[k] for k in names], small)
    grads = dict(zip(names, full_grads))
    grads["w_ffn_in"] = jnp.swapaxes(grads["w_ffn_in"], 0, 1)
    loss = small[3, 0]
    gains = jnp.concatenate([g_mix, g_ffn, g_fin, jnp.zeros((5, d_model), F32)], axis=0)
    gains_m = jnp.concatenate([m_norm_mix_g, m_norm_ffn_g, m_norm_final_g.reshape(1, d_model), jnp.zeros((5, d_model), F32)], axis=0)
    gains_v = jnp.concatenate([v_norm_mix_g, v_norm_ffn_g, v_norm_final_g.reshape(1, d_model), jnp.ones((5, d_model), F32)], axis=0)
    gd, gm, gv = _rowwise(_adamw_math, [gains, small, gains_m, gains_v], [], [(d_model, F32)] * 3, [], tm=8, name="adamw_gains")

    moments = {"w_in": (jnp.swapaxes(m_w_in[0], 0, 1), jnp.swapaxes(v_w_in[0], 0, 1)),
               "w_sb_up": (m_w_sb_up[0], v_w_sb_up[0]), "w_dil_up": (m_w_dil_up[0], v_w_dil_up[0]),
               "w_out": (m_w_out[0], v_w_out[0]), "w_ffn_in": (m_w_ffn_in[0], v_w_ffn_in[0]),
               "w_ffn_out": (m_w_ffn_out[0], v_w_ffn_out[0])}
    upd = {k: _adamw(shards[k], grads[k], moments[k][0], moments[k][1], "adamw_" + k) for k in names}

    def as_output(k, a):
        return (jnp.swapaxes(a, 0, 1) if k == "w_in" else a)[None]

    def w_out_of(i):
        return [as_output(k, upd[k][i]) for k in names]

    def ordered(mix, ws, ffn_g, fin):
        return [mix, ws[0], ws[1], ws[2], ws[3], ffn_g, ws[4], ws[5], fin]

    grad_ws = [as_output(k, grads[k]) for k in names]
    outs = [loss, dx.reshape(b_sz, s_len, d_model)]
    outs += ordered(small[0:1], grad_ws, small[1:2], small[2])
    outs += ordered(gd[0:1], w_out_of(0), gd[1:2], gd[2])
    outs += ordered(gm[0:1], w_out_of(1), gm[1:2], gm[2])
    outs += ordered(gv[0:1], w_out_of(2), gv[1:2], gv[2])
    return tuple(outs)
```

```python
import functools
import math

import jax
import jax.numpy as jnp
from jax import lax
from jax.experimental import pallas as pl
from jax.experimental.pallas import tpu as pltpu

F32 = jnp.float32
BF16 = jnp.bfloat16
MESH = pl.DeviceIdType.MESH

HEAD_DIM = 64
SB_HEADS = 8
DIL_PAIRS = ((128, 1), (512, 4), (2048, 16))
DIL_HEADS_PER_GROUP = 4
DIL_HEADS = DIL_HEADS_PER_GROUP * len(DIL_PAIRS)
SB_WIDTH = SB_HEADS * HEAD_DIM
DIL_WIDTH = DIL_HEADS * HEAD_DIM
DIL_OUT_WIDTH = DIL_HEADS_PER_GROUP * HEAD_DIM
QKV_WIDTH = 3 * SB_WIDTH + 3 * DIL_WIDTH
RMS_EPS = 1e-6
ALIBI_MAX_BIAS = 8.0
ADAM_LR = 0.001
ADAM_B1 = 0.9
ADAM_B2 = 0.999
ADAM_EPS = 1e-08
ADAM_WD = 0.01
ADAM_STEP = 10

LANES = 128
BLK = 128
NEG = -1e30
EXP_UNDERFLOW = -104.0
SB_FWD_CHAINS = 4
SB_BWD_CHAINS = 4
DIL_CHAINS = 4
N_CHIPS = 4
VMEM_CAP = 56 * 1024 * 1024


def _vmem_limit(tile_bytes):
    return int(min(VMEM_CAP, max(32 * 1024 * 1024, 3 * tile_bytes + 8 * 1024 * 1024)))


def _nbytes(shape, dtype):
    return math.prod(shape) * jnp.dtype(dtype).itemsize


def _dot(a, b):
    return jnp.dot(a, b, preferred_element_type=F32)


def _dot_nt(a, b):
    return lax.dot_general(a, b, (((1,), (1,)), ((), ())), preferred_element_type=F32)


def _dot_tn(a, b):
    return lax.dot_general(a, b, (((0,), (0,)), ((), ())), preferred_element_type=F32)


def _split2(x):
    hi = x.astype(BF16)
    lo = (x - hi.astype(F32)).astype(BF16)
    return hi, lo


def _sigmoid(x):
    return pl.reciprocal(1.0 + jnp.exp(-x), approx=True)


class _Carry:
    def __init__(self, arrays=(), out_shapes=(), aliased=False, sems=(), start=None, finish=None):
        self.arrays, self.out_shapes, self.aliased = list(arrays), list(out_shapes), aliased
        self.sems, self.start, self.finish = list(sems), start, finish

    def __bool__(self):
        return bool(self.arrays)

    def __add__(self, other):
        assert not self.aliased and not other.aliased
        n_a, n_o, n_s = len(self.arrays), len(self.out_shapes), len(self.sems)
        return _Carry(
            self.arrays + other.arrays, self.out_shapes + other.out_shapes, False, self.sems + other.sems,
            lambda i, o, s: (self.start(i[:n_a], o[:n_o], s[:n_s]), other.start(i[n_a:], o[n_o:], s[n_s:])),
            lambda i, o, s: (self.finish(i[:n_a], o[:n_o], s[:n_s]), other.finish(i[n_a:], o[n_o:], s[n_s:])))

    def call_args(self, n_in, n_out):
        aliases = {n_in + k: n_out + k for k in range(len(self.arrays))} if self.aliased else {}
        return _hbm_specs(len(self.arrays)), _hbm_specs(len(self.out_shapes)), self.out_shapes, aliases, self.sems

    def run(self, refs, n_in, n_out, step, n_steps, compute):
        if not self:
            compute()
            return
        n_c, n_o, n_s = len(self.arrays), len(self.out_shapes), len(self.sems)
        ins = refs[n_in:n_in + n_c]
        outs = refs[n_in + n_c + n_out:n_in + n_c + n_out + n_o]
        sems = refs[len(refs) - n_s:]

        @pl.when(step == 0)
        def _():
            self.start(ins, outs, sems)

        compute()

        @pl.when(step == n_steps - 1)
        def _():
            self.finish(ins, outs, sems)


def _mm(a, b, *, ta=False, tb=False, add=None, out_dtype=F32, tm, tn, tk, name, carry=None, epilogue=None,
        b_cols=None):
    carry = carry or _Carry()
    n_car = len(carry.arrays)
    pieces = list(a) if isinstance(a, (list, tuple)) else [a]
    n_a = len(pieces)
    widths = [p.shape[1] for p in pieces]
    starts = [sum(widths[:p]) for p in range(n_a)]
    if ta:
        kdim, m = pieces[0].shape[0], sum(widths)
    else:
        m, kdim = pieces[0].shape[0], sum(widths)
    if tb:
        n, k2 = b.shape
    else:
        k2, n = b.shape
    col0 = 0
    if b_cols is not None:
        assert b_cols[0] % tn == 0, name
        col0, n = b_cols[0] // tn, b_cols[1]
    assert kdim == k2 and m % tm == 0 and n % tn == 0 and kdim % tk == 0, (name, a.shape, b.shape)
    nk = kdim // tk
    assert n_a == 1 or (nk == 1 and not tb and (not ta or all(w % tm == 0 for w in widths))), name
    grid = (m // tm, n // tn, nk)
    a_mode = dict(pipeline_mode=pl.Buffered(1)) if grid[0] == 1 and nk == 1 else {}
    b_mode = dict(pipeline_mode=pl.Buffered(1)) if grid[1] == 1 and nk == 1 else {}
    if n_a == 1:
        a_specs = [pl.BlockSpec((tk, tm), lambda i, j, k: (k, i), **a_mode) if ta
                   else pl.BlockSpec((tm, tk), lambda i, j, k: (i, k), **a_mode)]
    elif ta:
        a_specs = [pl.BlockSpec((tk, tm), lambda i, j, k, s=s // tm, w=w // tm: (0, jnp.clip(i - s, 0, w - 1)))
                   for s, w in zip(starts, widths)]
    else:
        a_specs = [pl.BlockSpec((tm, w), lambda i, j, k: (i, 0)) for w in widths]
    b_spec = (pl.BlockSpec((tn, tk), lambda i, j, k: (j + col0, k), **b_mode) if tb
              else pl.BlockSpec((tk, tn), lambda i, j, k: (k, j + col0), **b_mode))
    o_spec = pl.BlockSpec((tm, tn), lambda i, j, k: (i, j))
    dims = ((((0,) if ta else (1,)), ((1,) if tb else (0,))), ((), ()))
    has_add = add is not None
    if epilogue is None:
        ep_fn, ep_rows, ep_params, ep_outs, ep_accs = None, [], [], [], []
        out_sds, out_specs = [jax.ShapeDtypeStruct((m, n), out_dtype)], [o_spec]
    else:
        ep_fn, ep_rows, ep_params, ep_outs, ep_accs = epilogue
        assert grid[1] == 1 or not ep_accs, name
        out_sds = [jax.ShapeDtypeStruct((m, w * grid[1]), d) for w, d in ep_outs]
        out_sds += [jax.ShapeDtypeStruct(sh, F32) for sh in ep_accs]
        out_specs = [pl.BlockSpec((tm, w), lambda i, j, k: (i, j)) for w, _ in ep_outs]
        out_specs += [pl.BlockSpec(sh, lambda i, j, k: (0, 0)) for sh in ep_accs]
    n_main = len(out_sds)
    use_scratch = nk > 1 and (ep_fn is not None or jnp.dtype(out_dtype) != jnp.dtype(F32))
    n_in = n_a + 1 + has_add + len(ep_rows) + len(ep_params)

    def finish(total, refs, pid):
        outs = refs[n_in + n_car:n_in + n_car + n_main]
        if ep_fn is None:
            outs[0][...] = total.astype(out_dtype)
            return
        first = n_a + 1 + has_add
        rows = [r[...].astype(F32) for r in refs[first:first + len(ep_rows)]]
        params = [p[...] for p in refs[first + len(ep_rows):n_in]]
        res = ep_fn(total, *rows, *params)
        for o_ref, v in zip(outs[:len(ep_outs)], res):
            o_ref[...] = v.astype(o_ref.dtype)
        acc_refs = outs[len(ep_outs):]
        if acc_refs:
            @pl.when(pid[0] == 0)
            def _():
                for r in acc_refs:
                    r[...] = jnp.zeros(r.shape, F32)

            for r, v in zip(acc_refs, res[len(ep_outs):]):
                r[...] += v

    def compute(refs, pid):
        a_ref, b_ref = refs[0], refs[n_a]
        add_ref = refs[n_a + 1] if has_add else None

        def dot(x, y):
            return lax.dot_general(x.astype(BF16), y.astype(BF16), dims, preferred_element_type=F32)

        if n_a > 1 and ta:
            for p_ref, s, w in zip(refs[:n_a], starts, widths):
                @pl.when((pid[0] >= s // tm) & (pid[0] < (s + w) // tm))
                def _(p_ref=p_ref):
                    prod = dot(p_ref[...], b_ref[...])
                    finish(prod + add_ref[...] if has_add else prod, refs, pid)
            return
        if n_a > 1:
            prod = dot(a_ref[...], b_ref[:widths[0], :])
            for p_ref, s, w in zip(refs[1:n_a], starts[1:], widths[1:]):
                prod += dot(p_ref[...], b_ref[s:s + w, :])
        else:
            prod = dot(a_ref[...], b_ref[...])
        if nk == 1:
            finish(prod + add_ref[...] if has_add else prod, refs, pid)
            return
        acc_ref = refs[n_in + n_car + n_main + len(carry.out_shapes)] if use_scratch else refs[n_in + n_car]
        k = pid[2]

        @pl.when(k == 0)
        def _():
            acc_ref[...] = prod + add_ref[...] if has_add else prod

        @pl.when(k > 0)
        def _():
            acc_ref[...] += prod

        if use_scratch:
            @pl.when(k == nk - 1)
            def _():
                finish(acc_ref[...], refs, pid)

    def body(*refs):
        pid = (pl.program_id(0), pl.program_id(1), pl.program_id(2))
        step = (pid[0] * grid[1] + pid[1]) * nk + pid[2]
        carry.run(refs, n_in, n_main, step, grid[0] * grid[1] * nk, lambda: compute(refs, pid))

    tile_bytes = ((n_a if ta else 1) * _nbytes((tm, tk), pieces[0].dtype)
                  + _nbytes((tk, tn), b.dtype) + 2 * _nbytes((tm, tn), F32)
                  + (_nbytes((tm, tn), F32) if has_add else 0)
                  + sum(_nbytes((tm, r.shape[1]), r.dtype) for r in ep_rows) + sum(_nbytes((tm, w), d) for w, d in ep_outs))
    in_specs = a_specs + [b_spec] + ([o_spec] if has_add else [])
    in_specs += [pl.BlockSpec((tm, r.shape[1] // grid[1]), lambda i, j, k: (i, j)) for r in ep_rows]
    in_specs += [pl.BlockSpec(p.shape, lambda i, j, k: (0, 0)) for p in ep_params]
    args = tuple(pieces) + (b,) + ((add,) if has_add else ()) + tuple(ep_rows) + tuple(ep_params)
    scratch = [pltpu.VMEM((tm, tn), F32)] if use_scratch else []
    serial = bool(carry) or bool(ep_accs)
    c_in, c_out, c_shapes, c_alias, c_sems = carry.call_args(n_in, n_main)
    res = pl.pallas_call(
        body, name=name, grid=grid,
        in_specs=in_specs + c_in, out_specs=out_specs + c_out, out_shape=out_sds + c_shapes,
        input_output_aliases=c_alias, scratch_shapes=scratch + c_sems,
        compiler_params=pltpu.CompilerParams(
            dimension_semantics=("arbitrary",) * 3 if serial else ("parallel", "parallel", "arbitrary"),
            vmem_limit_bytes=_vmem_limit(tile_bytes)),
    )(*args, *carry.arrays)
    main = res[0] if ep_fn is None else list(res[:n_main])
    return (main, res[n_main:]) if carry else main


def _rowwise(fn, rows, params, outs, accs, *, tm, name, carry=None):
    carry = carry or _Carry()
    t = rows[0].shape[0]
    assert t % tm == 0, (name, t, tm)
    n_r, n_p, n_o, n_c = len(rows), len(params), len(outs), len(carry.arrays)

    def compute(refs, first):
        vals = [r[...].astype(F32) for r in refs[:n_r]] + [p[...] for p in refs[n_r:n_r + n_p]]
        res = fn(*vals)
        o_refs = refs[n_r + n_p + n_c:n_r + n_p + n_c + n_o]
        a_refs = refs[n_r + n_p + n_c + n_o:n_r + n_p + n_c + n_o + len(accs)]
        for o_ref, v in zip(o_refs, res[:n_o]):
            o_ref[...] = v.astype(o_ref.dtype)
        if accs:
            @pl.when(first)
            def _():
                for a_ref in a_refs:
                    a_ref[...] = jnp.zeros(a_ref.shape, F32)

            for a_ref, v in zip(a_refs, res[n_o:]):
                a_ref[...] += v

    def body(*refs):
        step = pl.program_id(0)
        carry.run(refs, n_r + n_p, n_o + len(accs), step, t // tm, lambda: compute(refs, step == 0))

    in_specs = [pl.BlockSpec((tm, r.shape[1]), lambda i: (i, 0)) for r in rows]
    in_specs += [pl.BlockSpec(p.shape, lambda i: (0, 0)) for p in params]
    out_specs = [pl.BlockSpec((tm, w), lambda i: (i, 0)) for w, _ in outs]
    out_specs += [pl.BlockSpec(s, lambda i: (0, 0)) for s in accs]
    out_shape = [jax.ShapeDtypeStruct((t, w), d) for w, d in outs]
    out_shape += [jax.ShapeDtypeStruct(s, F32) for s in accs]
    tile_bytes = sum(_nbytes((tm, r.shape[1]), r.dtype) for r in rows) + sum(_nbytes((tm, w), F32) for w, _ in outs)
    c_in, c_out, c_shapes, c_alias, c_sems = carry.call_args(n_r + n_p, n_o + len(accs))
    res = pl.pallas_call(
        body, name=name, grid=(t // tm,), in_specs=in_specs + c_in, out_specs=out_specs + c_out,
        out_shape=out_shape + c_shapes, input_output_aliases=c_alias, scratch_shapes=c_sems,
        compiler_params=pltpu.CompilerParams(
            dimension_semantics=("arbitrary",) if accs or carry else ("parallel",),
            vmem_limit_bytes=_vmem_limit(2 * tile_bytes)),
    )(*rows, *params, *carry.arrays)
    own = n_o + len(accs)
    return (list(res[:own]) + [res[own:]]) if carry else res


def _rms_stats(x):
    r = lax.rsqrt(jnp.mean(x * x, axis=-1, keepdims=True) + RMS_EPS)
    return x * r, r


def _rms_bwd(dy, xhat, r, g):
    dxhat = dy * g
    dx = r * (dxhat - xhat * jnp.mean(dxhat * xhat, axis=-1, keepdims=True))
    return dx, dy * xhat


def _sb_consts():
    lane = lax.broadcasted_iota(jnp.int32, (BLK, LANES), 1)
    head0 = lane < HEAD_DIM
    row = lax.broadcasted_iota(jnp.int32, (2 * BLK, BLK), 0) % BLK
    col = lax.broadcasted_iota(jnp.int32, (2 * BLK, BLK), 1)
    causal = col < row
    jj = lax.broadcasted_iota(jnp.int32, (BLK, BLK), 0)
    ss = lax.broadcasted_iota(jnp.int32, (BLK, BLK), 1)
    suffix = jnp.where(jj > ss, 1.0, 0.0).astype(BF16)
    return head0, causal, suffix


def _stack_heads(x, head0):
    zero = jnp.zeros_like(x)
    return jnp.concatenate([jnp.where(head0, x, zero), jnp.where(head0, zero, x)], axis=0)


def _sb_logits(z, causal, masked):
    sp = jnp.log(1.0 + jnp.exp(-jnp.abs(z)))
    log_keep = -(jnp.maximum(z, 0.0) + sp)
    log_beta = jnp.minimum(z, 0.0) - sp
    if masked:
        log_keep = jnp.where(causal, log_keep, 0.0)
    return log_keep, log_beta


def _suffix_sums(x, suffix):
    hi, lo = _split2(x)
    after = _dot(hi, suffix) + _dot(lo, suffix)
    total = jnp.broadcast_to(after[:, 0:1] + x[:, 0:1], x.shape)
    return after, total


def _sb_walk_back(i, state, per_chain, tile):
    def alive(st):
        worst = functools.reduce(jnp.maximum, [st[p][:, 0:1] for p in range(0, len(st), per_chain)])
        return jnp.max(worst) > EXP_UNDERFLOW

    def cond(c):
        return jnp.logical_and(c[0] < i, alive(c[1]))

    def body(c):
        return c[0] + 1, tile(i - 1 - c[0], c[1], False)

    return lax.while_loop(cond, body, (jnp.int32(0), state))[1]


def _lane_blocks(x, n):
    return [x[:, p * LANES:(p + 1) * LANES] for p in range(n)]


def _sb_fwd(qkv, b_sz, s_len, carry):
    nq = s_len // BLK
    n_pairs = SB_WIDTH // LANES
    ch = SB_FWD_CHAINS
    n_steps = n_pairs // ch
    scale = 1.0 / math.sqrt(HEAD_DIM)

    def compute(q_ref, k_ref, v_ref, o_ref):
        head0, causal, suffix = _sb_consts()

        def q_block(i, _):
            qs = pl.multiple_of(i * BLK, BLK)
            q_all = (q_ref[pl.ds(qs, BLK), :] * scale).astype(BF16)
            q01 = [_stack_heads(q, head0) for q in _lane_blocks(q_all, ch)]

            def tile(j, state, masked):
                ks = pl.multiple_of(j * BLK, BLK)
                ks_ = _lane_blocks(k_ref[pl.ds(ks, BLK), :].astype(BF16), ch)
                vs_ = _lane_blocks(v_ref[pl.ds(ks, BLK), :].astype(BF16), ch)
                zs = [_dot_nt(q01[p], ks_[p]) for p in range(ch)]
                logits = [_sb_logits(z, causal, masked) for z in zs]
                sums = [_suffix_sums(lg[0], suffix) for lg in logits]
                out = []
                for p in range(ch):
                    carry, acc = state[2 * p], state[2 * p + 1]
                    after, total = sums[p]
                    a = jnp.exp(logits[p][1] + carry + after)
                    if masked:
                        a = jnp.where(causal, a, 0.0)
                    a_hi, a_lo = _split2(a)
                    a_cat = jnp.concatenate([a_hi[:BLK], a_hi[BLK:], a_lo[:BLK], a_lo[BLK:]], axis=1)
                    v01 = _stack_heads(vs_[p], head0)
                    out += [carry + total, acc + _dot(a_cat, jnp.concatenate([v01, v01], axis=0))]
                return tuple(out)

            state = (jnp.zeros((2 * BLK, BLK), F32), jnp.zeros((BLK, LANES), F32)) * ch
            state = tile(i, state, True)
            state = _sb_walk_back(i, state, 2, tile)
            o_ref[pl.ds(qs, BLK), :] = jnp.concatenate([state[2 * p + 1] for p in range(ch)], axis=1)
            return 0

        lax.fori_loop(0, nq, q_block, 0)

    def body(*refs):
        step = pl.program_id(0) * n_steps + pl.program_id(1)
        o_ref = refs[3 + len(carry.arrays)]
        carry.run(refs, 3, 1, step, b_sz * n_steps, lambda: compute(refs[0], refs[1], refs[2], o_ref))

    blk = lambda off: pl.BlockSpec((None, s_len, ch * LANES), lambda b, p: (b, 0, off + p))
    c_in, c_out, c_shapes, c_alias, c_sems = carry.call_args(3, 1)
    res = pl.pallas_call(
        body, name="sb_fwd", grid=(b_sz, n_steps),
        in_specs=[blk(0), blk(n_steps), blk(2 * n_steps)] + c_in, out_specs=[blk(0)] + c_out,
        out_shape=[jax.ShapeDtypeStruct((b_sz, s_len, SB_WIDTH), F32)] + c_shapes,
        input_output_aliases=c_alias, scratch_shapes=c_sems,
        compiler_params=pltpu.CompilerParams(dimension_semantics=("arbitrary", "arbitrary"),
                                             vmem_limit_bytes=VMEM_CAP),
    )(qkv, qkv, qkv, *carry.arrays)
    return res[0], res[1:]


def _sb_bwd(qkv, o_sb, do_sb, b_sz, s_len, carry):
    nq = s_len // BLK
    n_pairs = SB_WIDTH // LANES
    ch = SB_BWD_CHAINS
    n_steps = n_pairs // ch
    scale = 1.0 / math.sqrt(HEAD_DIM)

    def compute(q_ref, k_ref, v_ref, o_ref, do_ref, dq_ref, dk_ref, dv_ref, dk_acc, dv_acc):
        head0, causal, suffix = _sb_consts()
        lrow = lax.broadcasted_iota(jnp.int32, (LANES, LANES), 0)
        ones_h0 = jnp.where(lrow < HEAD_DIM, 1.0, 0.0).astype(BF16)
        ones_h1 = jnp.where(lrow >= HEAD_DIM, 1.0, 0.0).astype(BF16)
        dk_acc[...] = jnp.zeros(dk_acc.shape, F32)
        dv_acc[...] = jnp.zeros(dv_acc.shape, F32)

        def q_block(i, _):
            qs = pl.multiple_of(i * BLK, BLK)
            q_all = (q_ref[pl.ds(qs, BLK), :] * scale).astype(BF16)
            do_all = do_ref[pl.ds(qs, BLK), :].astype(BF16)
            dd_all = do_all.astype(F32) * o_ref[pl.ds(qs, BLK), :]
            q01 = [_stack_heads(q, head0) for q in _lane_blocks(q_all, ch)]
            do01 = [_stack_heads(d, head0) for d in _lane_blocks(do_all, ch)]
            tot = []
            for dd in _lane_blocks(dd_all, ch):
                dd_hi, dd_lo = _split2(dd)
                tot.append(jnp.concatenate([_dot(dd_hi, ones_h0) + _dot(dd_lo, ones_h0),
                                            _dot(dd_hi, ones_h1) + _dot(dd_lo, ones_h1)], axis=0))

            def tile(j, state, masked):
                ks = pl.multiple_of(j * BLK, BLK)
                ks_ = _lane_blocks(k_ref[pl.ds(ks, BLK), :].astype(BF16), ch)
                vs_ = _lane_blocks(v_ref[pl.ds(ks, BLK), :].astype(BF16), ch)
                zs = [_dot_nt(q01[p], ks_[p]) for p in range(ch)]
                das = [_dot_nt(do01[p], vs_[p]) for p in range(ch)]
                logits = [_sb_logits(z, causal, masked) for z in zs]
                sums = [_suffix_sums(lg[0], suffix) for lg in logits]
                a_s, e_s = [], []
                for p in range(ch):
                    a = jnp.exp(logits[p][1] + state[3 * p] + sums[p][0])
                    if masked:
                        a = jnp.where(causal, a, 0.0)
                    a_s.append(a)
                    e_s.append(a * das[p])
                e_sums = [_suffix_sums(e, suffix) for e in e_s]
                out, dks, dvs = [], [], []
                for p in range(ch):
                    carry, rcarry, dq = state[3 * p:3 * p + 3]
                    e = e_s[p]
                    before = tot[p] - (rcarry + e_sums[p][0] + e)
                    beta = jnp.exp(logits[p][1])
                    dz = e * (1.0 - beta) - beta * before
                    if masked:
                        dz = jnp.where(causal, dz, 0.0)
                    dz_b = dz.astype(BF16)
                    dks.append(_dot_tn(dz_b, q01[p]))
                    dvs.append(_dot_tn(a_s[p].astype(BF16), do01[p]))
                    out += [carry + sums[p][1], rcarry + e_sums[p][1], dq + _dot(dz_b, ks_[p])]
                dk_acc[pl.ds(ks, BLK), :] += jnp.concatenate(dks, axis=1)
                dv_acc[pl.ds(ks, BLK), :] += jnp.concatenate(dvs, axis=1)
                return tuple(out)

            state = (jnp.zeros((2 * BLK, BLK), F32),) * (3 * ch)
            state = tile(i, state, True)
            state = _sb_walk_back(i, state, 3, tile)
            dq = [jnp.where(head0, state[3 * p + 2][:BLK], state[3 * p + 2][BLK:]) for p in range(ch)]
            dq_ref[pl.ds(qs, BLK), :] = (jnp.concatenate(dq, axis=1) * scale).astype(dq_ref.dtype)
            return 0

        lax.fori_loop(0, nq, q_block, 0)
        dk_ref[...] = dk_acc[...].astype(dk_ref.dtype)
        dv_ref[...] = dv_acc[...].astype(dv_ref.dtype)

    def body(*refs):
        step = pl.program_id(0) * n_steps + pl.program_id(1)
        n_c, n_o = len(carry.arrays), len(carry.out_shapes)
        own = refs[:5] + refs[5 + n_c:8 + n_c] + refs[8 + n_c + n_o:10 + n_c + n_o]
        carry.run(refs, 5, 3, step, b_sz * n_steps, lambda: compute(*own))

    blk = lambda off: pl.BlockSpec((None, s_len, ch * LANES), lambda b, p: (b, 0, off + p))
    once = lambda off: pl.BlockSpec((None, s_len, ch * LANES), lambda b, p: (b, 0, off + p),
                                    pipeline_mode=pl.Buffered(1))
    out_sd = jax.ShapeDtypeStruct((b_sz, s_len, SB_WIDTH), BF16)
    c_in, c_out, c_shapes, c_alias, c_sems = carry.call_args(5, 3)
    res = pl.pallas_call(
        body, name="sb_bwd", grid=(b_sz, n_steps),
        in_specs=[once(0), once(n_steps), once(2 * n_steps), once(0), once(0)] + c_in,
        out_specs=[blk(0), blk(0), blk(0)] + c_out, out_shape=[out_sd, out_sd, out_sd] + c_shapes,
        input_output_aliases=c_alias,
        scratch_shapes=[pltpu.VMEM((s_len, ch * LANES), F32), pltpu.VMEM((s_len, ch * LANES), F32)] + c_sems,
        compiler_params=pltpu.CompilerParams(dimension_semantics=("arbitrary", "arbitrary"),
                                             vmem_limit_bytes=VMEM_CAP),
    )(qkv, qkv, qkv, o_sb, do_sb, *carry.arrays)
    return res[:3], res[3:]


def _dil_consts(group, pair_idx, dilation):
    lane = lax.broadcasted_iota(jnp.int32, (BLK, LANES), 1)
    head0 = lane < HEAD_DIM
    row = lax.broadcasted_iota(jnp.int32, (2 * BLK, BLK), 0)
    qa = row % BLK
    kb = lax.broadcasted_iota(jnp.int32, (2 * BLK, BLK), 1)
    head = (group * DIL_HEADS_PER_GROUP + 2 * pair_idx + row // BLK).astype(F32)
    slope = jnp.exp((-ALIBI_MAX_BIAS * math.log(2.0) / DIL_HEADS) * (head + 1.0))
    valid_cur = kb <= qa
    valid_prev = kb >= qa
    bias_cur = -slope * ((qa - kb) * dilation).astype(F32)
    bias_prev = -slope * ((BLK + qa - kb) * dilation).astype(F32)
    return head0, valid_cur, valid_prev, bias_cur, bias_prev


def _dil_units(s_len, dilation):
    nb = s_len // dilation // BLK
    return [(r, n) for r in range(dilation) for n in range(nb)]


def _dil_rows(n, r, dilation):
    if dilation == 1:
        return pl.ds(n * BLK, BLK)
    return pl.ds(n * BLK * dilation + r, BLK, stride=dilation)


def _dil_scores(q01, k, bias, valid):
    s = _dot_nt(q01, k) * (1.0 / math.sqrt(HEAD_DIM)) + bias
    return jnp.where(valid, s, NEG)


def _dil_fwd(qkv, b_sz, s_len, carry):
    n_pairs = DIL_OUT_WIDTH // LANES
    q_off = 3 * SB_WIDTH // LANES
    per_kind = DIL_WIDTH // LANES

    def compute(pair_idx, qkv_refs, o_ref, lse_ref, m_s, l_s):
        m_s[...] = jnp.full(m_s.shape, NEG, F32)
        l_s[...] = jnp.zeros(l_s.shape, F32)
        o_ref[...] = jnp.zeros(o_ref.shape, F32)
        for g, (_, dilation) in enumerate(DIL_PAIRS):
            q_ref, k_ref, v_ref = qkv_refs[3 * g:3 * g + 3]
            head0, valid_cur, valid_prev, bias_cur, bias_prev = _dil_consts(g, pair_idx, dilation)
            units = _dil_units(s_len, dilation)
            for u0 in range(0, len(units), DIL_CHAINS):
                group = units[u0:u0 + DIL_CHAINS]
                rows_of = [_dil_rows(n, r, dilation) for r, n in group]
                scores, values = [], []
                for (r, n), rows in zip(group, rows_of):
                    q01 = _stack_heads(q_ref[rows, :].astype(BF16), head0)
                    sc = [_dil_scores(q01, k_ref[rows, :].astype(BF16), bias_cur, valid_cur)]
                    vals = [_stack_heads(v_ref[rows, :].astype(BF16), head0)]
                    if n > 0:
                        prev = _dil_rows(n - 1, r, dilation)
                        sc.append(_dil_scores(q01, k_ref[prev, :].astype(BF16), bias_prev, valid_prev))
                        vals.append(_stack_heads(v_ref[prev, :].astype(BF16), head0))
                    scores.append(sc)
                    values.append(vals)
                stats = []
                for sc, rows in zip(scores, rows_of):
                    m_blk = functools.reduce(jnp.maximum, [jnp.max(x, axis=-1, keepdims=True) for x in sc])
                    m_old = jnp.concatenate([m_s.at[0][rows, :], m_s.at[1][rows, :]], axis=0)
                    l_old = jnp.concatenate([l_s.at[0][rows, :], l_s.at[1][rows, :]], axis=0)
                    m_new = jnp.maximum(m_old, m_blk)
                    probs = [jnp.exp(x - m_new) for x in sc]
                    l_blk = functools.reduce(jnp.add, [jnp.sum(p, axis=-1, keepdims=True) for p in probs])
                    alpha = jnp.exp(m_old - m_new)
                    stats.append((m_new, alpha * l_old + l_blk, alpha, probs))
                for (m_new, l_new, alpha, probs), vals, rows in zip(stats, values, rows_of):
                    alpha_tok = jnp.where(head0, alpha[:BLK], alpha[BLK:])
                    p_cat = jnp.concatenate(
                        [h for p in probs for h in (p[:BLK].astype(BF16), p[BLK:].astype(BF16))], axis=1)
                    o_ref[rows, :] = alpha_tok * o_ref[rows, :] + _dot(p_cat, jnp.concatenate(vals, axis=0))
                    m_s.at[0][rows, :] = m_new[:BLK]
                    m_s.at[1][rows, :] = m_new[BLK:]
                    l_s.at[0][rows, :] = l_new[:BLK]
                    l_s.at[1][rows, :] = l_new[BLK:]
        lane = lax.broadcasted_iota(jnp.int32, (BLK, LANES), 1)
        for c in range(s_len // BLK):
            rows = pl.ds(c * BLK, BLK)
            l0, l1 = l_s.at[0][rows, :], l_s.at[1][rows, :]
            o_ref[rows, :] = o_ref[rows, :] / jnp.where(lane < HEAD_DIM, l0, l1)
            lse_ref.at[0][rows, :] = m_s.at[0][rows, :] + jnp.log(l0)
            lse_ref.at[1][rows, :] = m_s.at[1][rows, :] + jnp.log(l1)

    def body(*refs):
        pair_idx = pl.program_id(1)
        step = pl.program_id(0) * n_pairs + pair_idx
        n_c, n_o = len(carry.arrays), len(carry.out_shapes)
        o_ref, lse_ref = refs[9 + n_c:11 + n_c]
        m_s, l_s = refs[11 + n_c + n_o:13 + n_c + n_o]
        carry.run(refs, 9, 2, step, b_sz * n_pairs, lambda: compute(pair_idx, refs[:9], o_ref, lse_ref, m_s, l_s))

    in_specs = []
    for g in range(len(DIL_PAIRS)):
        for kind in range(3):
            off = q_off + kind * per_kind + g * n_pairs
            in_specs.append(pl.BlockSpec((None, s_len, LANES), lambda b, p, off=off: (b, 0, off + p)))
    c_in, c_out, c_shapes, c_alias, c_sems = carry.call_args(9, 2)
    res = pl.pallas_call(
        body, name="dil_fwd", grid=(b_sz, n_pairs),
        in_specs=in_specs + c_in,
        out_specs=[pl.BlockSpec((None, s_len, LANES), lambda b, p: (b, 0, p)),
                   pl.BlockSpec((None, None, 2, s_len, LANES), lambda b, p: (b, p, 0, 0, 0))] + c_out,
        out_shape=[jax.ShapeDtypeStruct((b_sz, s_len, DIL_OUT_WIDTH), F32),
                   jax.ShapeDtypeStruct((b_sz, n_pairs, 2, s_len, LANES), F32)] + c_shapes,
        input_output_aliases=c_alias,
        scratch_shapes=[pltpu.VMEM((2, s_len, LANES), F32), pltpu.VMEM((2, s_len, LANES), F32)] + c_sems,
        compiler_params=pltpu.CompilerParams(dimension_semantics=("arbitrary", "arbitrary"),
                                             vmem_limit_bytes=VMEM_CAP),
    )(*([qkv] * 9), *carry.arrays)
    return res[0], res[1], res[2:]


def _dil_bwd(qkv, o_dl, lse, do_dl, b_sz, s_len, carry):
    n_pairs = DIL_OUT_WIDTH // LANES
    n_groups = len(DIL_PAIRS)
    q_off = 3 * SB_WIDTH // LANES
    per_kind = DIL_WIDTH // LANES

    def compute(pair_idx, group, q_ref, k_ref, v_ref, o_ref, lse_ref, do_ref, dq_ref, dk_ref, dv_ref, d_s, dq_s, dk_s, dv_s):
        lrow = lax.broadcasted_iota(jnp.int32, (LANES, LANES), 0)
        ones_h0 = jnp.where(lrow < HEAD_DIM, 1.0, 0.0).astype(BF16)
        ones_h1 = jnp.where(lrow >= HEAD_DIM, 1.0, 0.0).astype(BF16)
        for c in range(s_len // BLK):
            rows = pl.ds(c * BLK, BLK)
            dd_hi, dd_lo = _split2(do_ref[rows, :] * o_ref[rows, :])
            d_s.at[0][rows, :] = _dot(dd_hi, ones_h0) + _dot(dd_lo, ones_h0)
            d_s.at[1][rows, :] = _dot(dd_hi, ones_h1) + _dot(dd_lo, ones_h1)
        dk_s[...] = jnp.zeros(dk_s.shape, F32)
        dv_s[...] = jnp.zeros(dv_s.shape, F32)

        def one_group(g, dilation):
            head0, valid_cur, valid_prev, bias_cur, bias_prev = _dil_consts(g, pair_idx, dilation)
            units = _dil_units(s_len, dilation)
            scale = 1.0 / math.sqrt(HEAD_DIM)
            for u0 in range(0, len(units), DIL_CHAINS):
                chunk = units[u0:u0 + DIL_CHAINS]
                loaded = []
                for r, n in chunk:
                    rows = _dil_rows(n, r, dilation)
                    q01 = _stack_heads(q_ref[rows, :].astype(BF16), head0)
                    do01 = _stack_heads(do_ref[rows, :].astype(BF16), head0)
                    lse01 = jnp.concatenate([lse_ref.at[0][rows, :], lse_ref.at[1][rows, :]], axis=0)
                    d01 = jnp.concatenate([d_s.at[0][rows, :], d_s.at[1][rows, :]], axis=0)
                    blocks = [(rows, bias_cur, valid_cur)]
                    if n > 0:
                        blocks.append((_dil_rows(n - 1, r, dilation), bias_prev, valid_prev))
                    parts = []
                    for krows, bias, valid in blocks:
                        k = k_ref[krows, :].astype(BF16)
                        v = v_ref[krows, :].astype(BF16)
                        parts.append((krows, k, _dil_scores(q01, k, bias, valid), _dot_nt(do01, v)))
                    loaded.append((rows, q01, do01, lse01, d01, parts))
                grads = []
                for rows, q01, do01, lse01, d01, parts in loaded:
                    for krows, k, sc, dp in parts:
                        p = jnp.exp(sc - lse01)
                        grads.append((p.astype(BF16), (p * (dp - d01) * scale).astype(BF16)))
                it = iter(grads)
                updates = []
                for rows, q01, do01, lse01, d01, parts in loaded:
                    dq = jnp.zeros((2 * BLK, LANES), F32)
                    for krows, k, sc, dp in parts:
                        p_b, ds = next(it)
                        dq = dq + _dot(ds, k)
                        updates.append((krows, _dot_tn(ds, q01), _dot_tn(p_b, do01)))
                    dq_s[rows, :] = jnp.where(head0, dq[:BLK], dq[BLK:])
                for krows, dk, dv in updates:
                    dk_s[krows, :] = dk_s[krows, :] + dk
                    dv_s[krows, :] = dv_s[krows, :] + dv

        for g, (_, dilation) in enumerate(DIL_PAIRS):
            pl.when(group == g)(functools.partial(one_group, g, dilation))
        dq_ref[...] = dq_s[...].astype(dq_ref.dtype)
        dk_ref[...] = dk_s[...].astype(dk_ref.dtype)
        dv_ref[...] = dv_s[...].astype(dv_ref.dtype)

    def body(*refs):
        pair_idx, group = pl.program_id(1), pl.program_id(2)
        step = (pl.program_id(0) * n_pairs + pair_idx) * n_groups + group
        n_c, n_o = len(carry.arrays), len(carry.out_shapes)
        own = refs[:6] + refs[6 + n_c:9 + n_c] + refs[9 + n_c + n_o:13 + n_c + n_o]
        carry.run(refs, 6, 3, step, b_sz * n_pairs * n_groups, lambda: compute(pair_idx, group, *own))

    def qkv_spec(kind):
        return pl.BlockSpec((None, s_len, LANES),
                            lambda b, p, g: (b, 0, q_off + kind * per_kind + g * n_pairs + p))

    tok_spec = pl.BlockSpec((None, s_len, LANES), lambda b, p, g: (b, 0, p))
    out_spec = pl.BlockSpec((None, s_len, LANES), lambda b, p, g: (b, 0, g * n_pairs + p))
    out_sd = jax.ShapeDtypeStruct((b_sz, s_len, DIL_WIDTH), BF16)
    c_in, c_out, c_shapes, c_alias, c_sems = carry.call_args(6, 3)
    res = pl.pallas_call(
        body, name="dil_bwd", grid=(b_sz, n_pairs, n_groups),
        in_specs=[qkv_spec(0), qkv_spec(1), qkv_spec(2), tok_spec,
                  pl.BlockSpec((None, None, 2, s_len, LANES), lambda b, p, g: (b, p, 0, 0, 0)), tok_spec] + c_in,
        out_specs=[out_spec, out_spec, out_spec] + c_out,
        out_shape=[out_sd, out_sd, out_sd] + c_shapes,
        input_output_aliases=c_alias,
        scratch_shapes=[pltpu.VMEM((2, s_len, LANES), F32)] + [pltpu.VMEM((s_len, LANES), F32)] * 3 + c_sems,
        compiler_params=pltpu.CompilerParams(dimension_semantics=("arbitrary", "arbitrary", "arbitrary"),
                                             vmem_limit_bytes=VMEM_CAP),
    )(qkv, qkv, qkv, o_dl, lse, do_dl, *carry.arrays)
    return res[:3], res[3:]


def _mesh_pos():
    return lax.axis_index("x"), lax.axis_index("y"), lax.axis_index("c")


def _other_chips(x, y):
    return [(1 - x, y), (x, 1 - y), (1 - x, 1 - y)]


def _hbm_specs(n):
    return [pl.BlockSpec(memory_space=pl.ANY)] * n


SWAPPED = ("w_ffn_in",)


def _slot(x, y, swapped):
    return 2 * y + x if swapped else 2 * x + y


def _cast_to_slab(w, name, swapped=False):
    rows, cols = w.shape
    mine = jnp.reshape(_slot(lax.axis_index("x"), lax.axis_index("y"), swapped), (1,)).astype(jnp.int32)

    def body(idx_ref, w_ref, o_ref):
        o_ref[...] = w_ref[...].astype(BF16)

    return pl.pallas_call(
        body, name=name,
        grid_spec=pltpu.PrefetchScalarGridSpec(
            num_scalar_prefetch=1, grid=(1,),
            in_specs=[pl.BlockSpec((rows, cols), lambda i, idx: (0, 0))],
            out_specs=pl.BlockSpec((None, rows, cols), lambda i, idx: (idx[0], 0, 0))),
        out_shape=pltpu.HBM((N_CHIPS, rows, cols), BF16),
        compiler_params=pltpu.CompilerParams(vmem_limit_bytes=_vmem_limit(rows * cols * 6)),
    )(mine, w)


def _gather_issue(slabs, send_sems, recv_sems, swapped):
    x, y, c = _mesh_pos()
    for k, slab in enumerate(slabs):
        half = slab.shape[1] // 2
        rows = slab.at[_slot(x, y, swapped[k]), pl.ds(c * half, half), :]
        for r, (px, py) in enumerate(_other_chips(x, y)):
            pltpu.make_async_remote_copy(
                src_ref=rows, dst_ref=rows, send_sem=send_sems.at[6 * k + r], recv_sem=recv_sems.at[6 * k + r],
                device_id=(px, py, c), device_id_type=MESH).start()


def _gather_complete(slabs, send_sems, recv_sems, swapped):
    x, y, c = _mesh_pos()
    chips = _other_chips(x, y)

    def copy(k, sem, block, rows, to):
        ref = slabs[k].at[block, rows, :]
        return pltpu.make_async_remote_copy(
            src_ref=ref, dst_ref=ref, send_sem=send_sems.at[sem], recv_sem=recv_sems.at[sem],
            device_id=to, device_id_type=MESH)

    for k, slab in enumerate(slabs):
        half = slab.shape[1] // 2
        for r, (px, py) in enumerate(chips):
            theirs = _slot(px, py, swapped[k])
            copy(k, 6 * k + r, theirs, pl.ds(c * half, half), (px, py, c)).wait_recv()
            copy(k, 6 * k + 3 + r, theirs, pl.ds(c * half, half), (x, y, 1 - c)).start()
    for k, slab in enumerate(slabs):
        half = slab.shape[1] // 2
        for r, (px, py) in enumerate(chips):
            copy(k, 6 * k + 3 + r, _slot(px, py, swapped[k]), pl.ds((1 - c) * half, half), (x, y, 1 - c)).wait_recv()
    for k, slab in enumerate(slabs):
        half = slab.shape[1] // 2
        for r, (px, py) in enumerate(chips):
            copy(k, 6 * k + r, _slot(x, y, swapped[k]), pl.ds(c * half, half), (px, py, c)).wait_send()
            copy(k, 6 * k + 3 + r, _slot(px, py, swapped[k]), pl.ds(c * half, half), (x, y, 1 - c)).wait_send()


def _gather_sems(n):
    return [pltpu.SemaphoreType.DMA((6 * n,)), pltpu.SemaphoreType.DMA((6 * n,))]


def _gather_carry(slabs, names):
    swapped = [k in SWAPPED for k in names]
    return _Carry(slabs, [jax.ShapeDtypeStruct(a.shape, a.dtype) for a in slabs], True, _gather_sems(len(slabs)),
                  lambda ins, outs, sems: _gather_issue(outs, *sems, swapped),
                  lambda ins, outs, sems: _gather_complete(outs, *sems, swapped))


def _pair_copies(ins, outs, send_sems, recv_sems):
    x, y, c = _mesh_pos()
    copies = []
    for k, g in enumerate(ins):
        half = g.shape[1] // 2
        copies.append(pltpu.make_async_remote_copy(
            src_ref=g.at[:, pl.ds((1 - c) * half, half), :], dst_ref=outs[k],
            send_sem=send_sems.at[k], recv_sem=recv_sems.at[k],
            device_id=(x, y, 1 - c), device_id_type=MESH))
    return copies


def _pair_carry(grads):
    n = len(grads)

    def start(ins, outs, sems):
        for cp in _pair_copies(ins, outs, *sems):
            cp.start()

    def finish(ins, outs, sems):
        for cp in _pair_copies(ins, outs, *sems):
            cp.wait()

    return _Carry(grads, [jax.ShapeDtypeStruct((N_CHIPS, g.shape[1] // 2, g.shape[2]), g.dtype) for g in grads], False,
                  [pltpu.SemaphoreType.DMA((n,)), pltpu.SemaphoreType.DMA((n,))], start, finish)


def _pair_exchange(grads, tag):
    carry = _pair_carry(grads)
    n = len(grads)

    def body(*refs):
        carry.start(refs[:n], refs[n:2 * n], refs[2 * n:])
        carry.finish(refs[:n], refs[n:2 * n], refs[2 * n:])

    return pl.pallas_call(
        body, name="grad_pair_exchange_" + tag, in_specs=_hbm_specs(n), out_specs=_hbm_specs(n),
        out_shape=carry.out_shapes, scratch_shapes=carry.sems,
    )(*grads)


def _pair_sum(grad, other, name, swapped):
    _, rows, cols = grad.shape
    half = rows // 2
    x, y, c = _mesh_pos()
    idx = jnp.stack([c, _slot(x, y, swapped)]).astype(jnp.int32)

    def body(idx_ref, g_ref, p_ref, own_ref, sb_ref):
        s = g_ref[...] + p_ref[...].astype(F32)
        sb_ref[...] = s.astype(BF16)

        @pl.when(pl.program_id(0) == idx_ref[1])
        def _():
            own_ref[...] = s

    blk = pl.BlockSpec((None, half, cols), lambda p, idx: (p, 0, 0))
    return pl.pallas_call(
        body, name=name,
        grid_spec=pltpu.PrefetchScalarGridSpec(
            num_scalar_prefetch=1, grid=(N_CHIPS,),
            in_specs=[pl.BlockSpec((None, half, cols), lambda p, idx: (p, idx[0], 0)), blk],
            out_specs=[pl.BlockSpec((half, cols), lambda p, idx: (0, 0)), blk]),
        out_shape=[jax.ShapeDtypeStruct((half, cols), F32), jax.ShapeDtypeStruct((N_CHIPS, half, cols), BF16)],
        compiler_params=pltpu.CompilerParams(dimension_semantics=("arbitrary",),
                                             vmem_limit_bytes=_vmem_limit(4 * half * cols * 4)),
    )(idx, grad, other)


def _chip_copies(sums_bf16, lands, send_sems, recv_sems, swapped):
    x, y, c = _mesh_pos()
    return [pltpu.make_async_remote_copy(
        src_ref=sums_bf16[k].at[_slot(px, py, swapped[k])], dst_ref=lands[k].at[r],
        send_sem=send_sems.at[3 * k + r], recv_sem=recv_sems.at[3 * k + r],
        device_id=(px, py, c), device_id_type=MESH)
        for k in range(len(sums_bf16)) for r, (px, py) in enumerate(_other_chips(x, y))]


def _chip_carry(sums_bf16, names):
    swapped = [k in SWAPPED for k in names]

    def start(ins, outs, sems):
        for cp in _chip_copies(ins, outs, *sems, swapped):
            cp.start()

    def finish(ins, outs, sems):
        for cp in _chip_copies(ins, outs, *sems, swapped):
            cp.wait()

    return _Carry(sums_bf16, _chip_landing(sums_bf16), False, _chip_sems(len(sums_bf16)), start, finish)


def _chip_sems(n):
    return [pltpu.SemaphoreType.DMA((3 * n,)), pltpu.SemaphoreType.DMA((3 * n,))]


def _chip_landing(sums_bf16):
    return [jax.ShapeDtypeStruct((N_CHIPS - 1,) + s.shape[1:], BF16) for s in sums_bf16]


def _chip_sum(own, landed, name):
    rows, cols = own.shape
    core = jnp.reshape(lax.axis_index("c"), (1,)).astype(jnp.int32)

    def body(core_ref, o_ref, l_ref, out_ref):
        out_ref[...] = ((o_ref[...] + l_ref[0].astype(F32)) + l_ref[1].astype(F32)) + l_ref[2].astype(F32)

    return pl.pallas_call(
        body, name=name,
        grid_spec=pltpu.PrefetchScalarGridSpec(
            num_scalar_prefetch=1, grid=(1,),
            in_specs=[pl.BlockSpec((rows, cols), lambda i, core_ref: (0, 0)),
                      pl.BlockSpec((N_CHIPS - 1, rows, cols), lambda i, core_ref: (0, 0, 0))],
            out_specs=pl.BlockSpec((rows, cols), lambda i, core_ref: (core_ref[0], 0))),
        out_shape=jax.ShapeDtypeStruct((2 * rows, cols), F32),
        compiler_params=pltpu.CompilerParams(vmem_limit_bytes=_vmem_limit(3 * rows * cols * 4)),
    )(core, own, landed)


def _final_exchange(fulls, v):
    n = len(fulls)
    rows, cols = v.shape
    n_dev = 8

    def body(*refs):
        v_ref, out_ref = refs[0], refs[1 + 2 * n]
        outs = refs[1 + n:1 + 2 * n]
        buf, v_send, v_recv, h_send, h_recv = refs[2 + 2 * n:]
        x, y, c = _mesh_pos()
        me = 4 * x + 2 * y + c
        buf[me] = v_ref[...]
        peers = [(1 - x if r & 4 else x, 1 - y if r & 2 else y, 1 - c if r & 1 else c) for r in range(1, n_dev)]
        copies = []
        for r, peer in enumerate(peers):
            copies.append(pltpu.make_async_remote_copy(
                src_ref=v_ref, dst_ref=buf.at[me], send_sem=v_send.at[r], recv_sem=v_recv.at[r],
                device_id=peer, device_id_type=MESH))
        for k in range(n):
            half = fulls[k].shape[0] // 2
            mine = outs[k].at[pl.ds(c * half, half), :]
            copies.append(pltpu.make_async_remote_copy(
                src_ref=mine, dst_ref=mine, send_sem=h_send.at[k], recv_sem=h_recv.at[k],
                device_id=(x, y, 1 - c), device_id_type=MESH))
        for cp in copies:
            cp.start()
        for r, (px, py, pc) in enumerate(peers):
            pltpu.make_async_remote_copy(
                src_ref=v_ref, dst_ref=buf.at[4 * px + 2 * py + pc], send_sem=v_send.at[r], recv_sem=v_recv.at[r],
                device_id=(px, py, pc), device_id_type=MESH).wait_recv()
        for k in range(n):
            half = fulls[k].shape[0] // 2
            theirs = outs[k].at[pl.ds((1 - c) * half, half), :]
            pltpu.make_async_remote_copy(
                src_ref=theirs, dst_ref=theirs, send_sem=h_send.at[k], recv_sem=h_recv.at[k],
                device_id=(x, y, 1 - c), device_id_type=MESH).wait_recv()
        for cp in copies:
            cp.wait_send()
        acc = buf[0]
        for d in range(1, n_dev):
            acc = acc + buf[d]
        out_ref[...] = acc
        out_ref[3:4, :] = jnp.broadcast_to(jnp.sum(acc[3:4, :], axis=1, keepdims=True), (1, cols))

    vm = pl.BlockSpec(memory_space=pltpu.VMEM)
    res = pl.pallas_call(
        body, name="final_exchange",
        in_specs=[vm] + _hbm_specs(n), out_specs=_hbm_specs(n) + [vm],
        out_shape=[jax.ShapeDtypeStruct(f.shape, F32) for f in fulls] + [jax.ShapeDtypeStruct((rows, cols), F32)],
        input_output_aliases={1 + k: k for k in range(n)},
        scratch_shapes=[pltpu.VMEM((n_dev, rows, cols), F32),
                        pltpu.SemaphoreType.DMA((n_dev - 1,)), pltpu.SemaphoreType.DMA((n_dev - 1,)),
                        pltpu.SemaphoreType.DMA((n,)), pltpu.SemaphoreType.DMA((n,))],
    )(v, *fulls)
    return res[:n], res[n]


def _adamw_math(w, g, m, v):
    m = ADAM_B1 * m + (1.0 - ADAM_B1) * g
    v = ADAM_B2 * v + (1.0 - ADAM_B2) * (g * g)
    m_hat = m / (1.0 - ADAM_B1 ** ADAM_STEP)
    v_hat = v / (1.0 - ADAM_B2 ** ADAM_STEP)
    delta = -ADAM_LR * (m_hat / (jnp.sqrt(v_hat) + ADAM_EPS) + ADAM_WD * w)
    return delta, m, v


def _adamw(w, g, m, v, name):
    rows, cols = w.shape
    tm = rows // 2 if (rows // 2) % 8 == 0 else rows
    return _rowwise(_adamw_math, [w, g, m, v], [], [(cols, F32)] * 3, [], tm=tm, name=name)


def _unshard_cols(gathered):
    n, r, c = gathered.shape
    return jnp.transpose(gathered, (1, 0, 2)).reshape(r, n * c)


def _shard_cols(full):
    r, nc = full.shape
    return jnp.transpose(full.reshape(r, N_CHIPS, nc // N_CHIPS), (1, 0, 2))


LATE = ["w_sb_up", "w_dil_up", "w_out", "w_ffn_in", "w_ffn_out"]


def _late_weights(slabs, d_model, d_ff):
    g = dict(zip(LATE, slabs))
    return (_unshard_cols(g["w_sb_up"]), _unshard_cols(g["w_dil_up"]), g["w_out"].reshape(d_model, d_model),
            _unshard_cols(g["w_ffn_in"]), g["w_ffn_out"].reshape(d_ff, d_model))


ROW_SHARDED = ("w_in", "w_out", "w_ffn_in", "w_ffn_out")


def _chip_major(grads):
    out = []
    for k, g in grads.items():
        if k in ROW_SHARDED:
            out.append(g.reshape(N_CHIPS, g.shape[0] // N_CHIPS, g.shape[1]))
        else:
            out.append(_shard_cols(g))
    return out


def _pair_sums(full, others, names):
    return [_pair_sum(g, o, "grad_pair_sum_" + k, k in SWAPPED) for g, o, k in zip(full, others, names)]


def _chip_sums(pair, landed, names):
    return {k: _chip_sum(p[0], l, "grad_chip_sum_" + k) for p, l, k in zip(pair, landed, names)}


def _fwd_bwd(x, loss_target, g_mix, g_ffn, g_fin, slab_in, late_slabs):
    b_sz, s_len, d_model = x.shape
    t = b_sz * s_len
    d_ff = late_slabs[-1].shape[1] * N_CHIPS
    x2d = x.reshape(t, d_model)
    tgt2d = loss_target.reshape(t, d_model)

    u, (slab_in,) = _rowwise(lambda xv, g: (_rms_stats(xv)[0] * g,), [x2d], [g_mix], [(d_model, BF16)], [], tm=512,
                             name="norm_mix", carry=_gather_carry([slab_in], ["w_in"]))
    wt_in = slab_in.reshape(-1, d_model)
    qkv, (slab_ffn_out,) = _mm(u, wt_in, tb=True, b_cols=(0, QKV_WIDTH), tm=2048, tn=768, tk=d_model, name="proj_qkv",
                               carry=_gather_carry(late_slabs[4:], LATE[4:]))
    gates = _mm(u, wt_in, tb=True, b_cols=(QKV_WIDTH, 2 * d_model), out_dtype=BF16, tm=t, tn=256, tk=d_model,
                name="proj_gates")
    qkv3 = qkv.reshape(b_sz, s_len, QKV_WIDTH)
    o_sb, (slab_ffn_in,) = _sb_fwd(qkv3, b_sz, s_len, _gather_carry(late_slabs[3:4], LATE[3:4]))
    o_dl, lse, small_slabs = _dil_fwd(qkv3, b_sz, s_len, _gather_carry(late_slabs[:3], LATE[:3]))
    wf_sb_up, wf_dil_up, wf_out, wf_ffn_in, wf_ffn_out = _late_weights(
        list(small_slabs) + [slab_ffn_in, slab_ffn_out], d_model, d_ff)
    o_sb2, o_dl2 = o_sb.reshape(t, SB_WIDTH), o_dl.reshape(t, DIL_OUT_WIDTH)
    y_sb = _mm(o_sb2, wf_sb_up, out_dtype=BF16, tm=1024, tn=1024, tk=SB_WIDTH, name="sb_up")
    y_dl = _mm(o_dl2, wf_dil_up, out_dtype=BF16, tm=1024, tn=1024, tk=DIL_OUT_WIDTH, name="dil_up")

    def merge_fn(gt, ys, yd):
        return (_sigmoid(gt[:, :d_model]) * ys + _sigmoid(gt[:, d_model:]) * yd,)

    (merged,) = _rowwise(merge_fn, [gates, y_sb, y_dl], [], [(d_model, BF16)], [], tm=512, name="merge")
    x1 = _mm(merged, wf_out, add=x2d, tm=512, tn=1024, tk=d_model, name="mix_out")
    (u2,) = _rowwise(lambda xv, g: (_rms_stats(xv)[0] * g,), [x1], [g_ffn], [(d_model, BF16)], [], tm=512, name="norm_ffn")
    half_ff = d_ff // 2

    def act_fn(hv):
        gate = hv[:, :half_ff]
        return hv, gate * _sigmoid(gate) * hv[:, half_ff:]

    h, act = _mm(u2, wf_ffn_in, tm=512, tn=d_ff, tk=d_model, name="ffn_in",
                 epilogue=(act_fn, [], [], [(d_ff, BF16), (half_ff, BF16)], []))
    def head_fn(xv, tg, g):
        xhat, r = _rms_stats(xv)
        err = xhat * g - tg
        dy = err * (1.0 / d_model)
        dx, dg_rows = _rms_bwd(dy, xhat, r, g)
        loss_lanes = (0.5 / d_model) * jnp.sum(err * err, axis=0, keepdims=True)
        return dx, dx, jnp.sum(dg_rows, axis=0, keepdims=True), loss_lanes

    dx2, dx2_b, dg_fin, loss_lanes = _mm(
        act, wf_ffn_out, add=x1, tm=512, tn=1024, tk=d_ff, name="ffn_out",
        epilogue=(head_fn, [tgt2d], [g_fin], [(d_model, F32), (d_model, BF16)], [(1, d_model), (1, d_model)]))

    def dact_fn(da, hv):
        gate, up = hv[:, :half_ff], hv[:, half_ff:]
        sg = _sigmoid(gate)
        dgate = da * up * (sg * (1.0 + gate * (1.0 - sg)))
        return (jnp.concatenate([dgate, da * (gate * sg)], axis=1),)

    (dh,) = _mm(dx2_b, wf_ffn_out, tb=True, tm=512, tn=half_ff, tk=d_model, name="ffn_out_dx",
                epilogue=(dact_fn, [h], [], [(d_ff, BF16)], []))
    gw_ffn_out = _mm(act, dx2_b, ta=True, tm=256, tn=d_model, tk=t, name="ffn_out_dw")
    def norm_bwd_fn(du_, dres, xv, g):
        xhat, r = _rms_stats(xv)
        dx, dg_rows = _rms_bwd(du_, xhat, r, g)
        return dres + dx, jnp.sum(dg_rows, axis=0, keepdims=True)

    def norm_bwd_twice(*args):
        dx, dg = norm_bwd_fn(*args)
        return dx, dx, dg

    dx1, dx1_b, dg_ffn = _mm(dh, wf_ffn_in, tb=True, tm=512, tn=1024, tk=2 * d_ff, name="ffn_in_dx",
                             epilogue=(norm_bwd_twice, [dx2, x1], [g_ffn], [(d_model, F32), (d_model, BF16)], [(1, d_model)]))
    gwt_ffn_in = _mm(dh, u2, ta=True, tm=512, tn=d_model, tk=t, name="ffn_in_dw")

    dmerged = _mm(dx1_b, wf_out, tb=True, out_dtype=BF16, tm=512, tn=1024, tk=d_model, name="mix_out_dx")
    gw_out = _mm(merged, dx1_b, ta=True, tm=256, tn=d_model, tk=t, name="mix_out_dw")

    def merge_bwd_fn(gt, ys, yd, dm):
        s_sb, s_dl = _sigmoid(gt[:, :d_model]), _sigmoid(gt[:, d_model:])
        dgates = jnp.concatenate([dm * ys * s_sb * (1.0 - s_sb), dm * yd * s_dl * (1.0 - s_dl)], axis=1)
        return dgates, dm * s_sb, dm * s_dl

    full_big = _chip_major({"w_out": gw_out, "w_ffn_in": gwt_ffn_in, "w_ffn_out": gw_ffn_out})
    dgates, dy_sb, dy_dl, others_big = _rowwise(
        merge_bwd_fn, [gates, y_sb, y_dl, dmerged], [], [(2 * d_model, BF16), (d_model, BF16), (d_model, BF16)], [],
        tm=256, name="merge_bwd", carry=_pair_carry(full_big))
    pair_big = _pair_sums(full_big, others_big, LATE[2:])
    do_sb = _mm(dy_sb, wf_sb_up, tb=True, out_dtype=BF16, tm=1024, tn=SB_WIDTH, tk=d_model, name="sb_up_dx")
    gw_sb_up = _mm(o_sb2, dy_sb, ta=True, tm=SB_WIDTH, tn=1024, tk=512, name="sb_up_dw")
    do_dl = _mm(dy_dl, wf_dil_up, tb=True, tm=1024, tn=DIL_OUT_WIDTH, tk=d_model, name="dil_up_dx")
    gw_dil_up = _mm(o_dl2, dy_dl, ta=True, tm=DIL_OUT_WIDTH, tn=1024, tk=512, name="dil_up_dw")
    full_small = _chip_major({"w_sb_up": gw_sb_up, "w_dil_up": gw_dil_up})
    (dq_sb, dk_sb, dv_sb), brought = _sb_bwd(
        qkv3, o_sb, do_sb.reshape(b_sz, s_len, SB_WIDTH), b_sz, s_len,
        _chip_carry([p[1] for p in pair_big[:2]], LATE[2:4]) + _pair_carry(full_small))
    pair_small = _pair_sums(full_small, brought[2:], LATE[:2])
    (dq_dl, dk_dl, dv_dl), landed_b = _dil_bwd(
        qkv3, o_dl, lse, do_dl.reshape(b_sz, s_len, DIL_OUT_WIDTH), b_sz, s_len,
        _chip_carry([pair_big[2][1], pair_small[0][1], pair_small[1][1]], [LATE[4], LATE[0], LATE[1]]))
    pair = pair_small + pair_big
    landed = [landed_b[1], landed_b[2], brought[0], brought[1], landed_b[0]]
    dproj = [a.reshape(t, -1) for a in (dq_sb, dk_sb, dv_sb, dq_dl, dk_dl, dv_dl)] + [dgates]
    gwt_in, gwt_in_b = _mm(dproj, u, ta=True, tm=256, tn=d_model, tk=t, name="proj_dw",
                           epilogue=(lambda tile: (tile, tile), [], [], [(d_model, F32), (d_model, BF16)], []))
    full_in = _chip_major({"w_in": gwt_in})
    pair_in = _pair_sums(full_in, _pair_exchange(_chip_major({"w_in": gwt_in_b}), "w_in"), ["w_in"])
    (dx, dg_mix), landed_in = _mm(
        dproj, wt_in, tm=512, tn=1024, tk=wt_in.shape[0], name="proj_dx",
        carry=_chip_carry([p[1] for p in pair_in], ["w_in"]),
        epilogue=(norm_bwd_fn, [dx1, x2d], [g_mix], [(d_model, F32)], [(1, d_model)]))

    grads = _chip_sums(pair, landed, LATE)
    grads.update(_chip_sums(pair_in, landed_in, ["w_in"]))
    return dx, grads, dg_mix, dg_ffn, dg_fin, loss_lanes


def kernel(x, norm_mix_g, w_in, w_sb_up, w_dil_up, w_out, norm_ffn_g, w_ffn_in, w_ffn_out, norm_final_g, loss_target, m_norm_mix_g, m_w_in, m_w_sb_up, m_w_dil_up, m_w_out, m_norm_ffn_g, m_w_ffn_in, m_w_ffn_out, m_norm_final_g, v_norm_mix_g, v_w_in, v_w_sb_up, v_w_dil_up, v_w_out, v_norm_ffn_g, v_w_ffn_in, v_w_ffn_out, v_norm_final_g):
    b_sz, s_len, d_model = x.shape
    d_ff = w_ffn_out.shape[1] * N_CHIPS
    g_mix, g_ffn, g_fin = norm_mix_g, norm_ffn_g, norm_final_g.reshape(1, d_model)

    names = ["w_in", "w_sb_up", "w_dil_up", "w_out", "w_ffn_in", "w_ffn_out"]
    shards = {"w_in": jnp.swapaxes(w_in[0], 0, 1), "w_sb_up": w_sb_up[0], "w_dil_up": w_dil_up[0], "w_out": w_out[0],
              "w_ffn_in": w_ffn_in[0], "w_ffn_out": w_ffn_out[0]}
    slab_in = _cast_to_slab(shards["w_in"], "cast_w_in")
    late_slabs = [_cast_to_slab(shards[k], "cast_" + k, k in SWAPPED) for k in LATE]

    dx, grads, dg_mix, dg_ffn, dg_fin, loss_lanes = _fwd_bwd(
        x, loss_target, g_mix, g_ffn, g_fin, slab_in, late_slabs)

    small = jnp.concatenate([dg_mix, dg_ffn, dg_fin, loss_lanes, jnp.zeros((4, d_model), F32)], axis=0)
    full_grads, small = _final_exchange([grads[k] for k in names], small)
    grads = dict(zip(names, full_grads))
    grads["w_ffn_in"] = jnp.swapaxes(grads["w_ffn_in"], 0, 1)
    loss = small[3, 0]
    gains = jnp.concatenate([g_mix, g_ffn, g_fin, jnp.zeros((5, d_model), F32)], axis=0)
    gains_m = jnp.concatenate([m_norm_mix_g, m_norm_ffn_g, m_norm_final_g.reshape(1, d_model), jnp.zeros((5, d_model), F32)], axis=0)
    gains_v = jnp.concatenate([v_norm_mix_g, v_norm_ffn_g, v_norm_final_g.reshape(1, d_model), jnp.ones((5, d_model), F32)], axis=0)
    gd, gm, gv = _rowwise(_adamw_math, [gains, small, gains_m, gains_v], [], [(d_model, F32)] * 3, [], tm=8, name="adamw_gains")

    moments = {"w_in": (jnp.swapaxes(m_w_in[0], 0, 1), jnp.swapaxes(v_w_in[0], 0, 1)),
               "w_sb_up": (m_w_sb_up[0], v_w_sb_up[0]), "w_dil_up": (m_w_dil_up[0], v_w_dil_up[0]),
               "w_out": (m_w_out[0], v_w_out[0]), "w_ffn_in": (m_w_ffn_in[0], v_w_ffn_in[0]),
               "w_ffn_out": (m_w_ffn_out[0], v_w_ffn_out[0])}
    upd = {k: _adamw(shards[k], grads[k], moments[k][0], moments[k][1], "adamw_" + k) for k in names}

    def as_output(k, a):
        return (jnp.swapaxes(a, 0, 1) if k == "w_in" else a)[None]

    def w_out_of(i):
        return [as_output(k, upd[k][i]) for k in names]

    def ordered(mix, ws, ffn_g, fin):
        return [mix, ws[0], ws[1], ws[2], ws[3], ffn_g, ws[4], ws[5], fin]

    grad_ws = [as_output(k, grads[k]) for k in names]
    outs = [loss, dx.reshape(b_sz, s_len, d_model)]
    outs += ordered(small[0:1], grad_ws, small[1:2], small[2])
    outs += ordered(gd[0:1], w_out_of(0), gd[1:2], gd[2])
    outs += ordered(gm[0:1], w_out_of(1), gm[1:2], gm[2])
    outs += ordered(gv[0:1], w_out_of(2), gv[1:2], gv[2])
    return tuple(outs)
```

```python
import functools
import math

import jax
import jax.numpy as jnp
from jax import lax
from jax.experimental import pallas as pl
from jax.experimental.pallas import tpu as pltpu

F32 = jnp.float32
BF16 = jnp.bfloat16
MESH = pl.DeviceIdType.MESH

HEAD_DIM = 64
SB_HEADS = 8
DIL_PAIRS = ((128, 1), (512, 4), (2048, 16))
DIL_HEADS_PER_GROUP = 4
DIL_HEADS = DIL_HEADS_PER_GROUP * len(DIL_PAIRS)
SB_WIDTH = SB_HEADS * HEAD_DIM
DIL_WIDTH = DIL_HEADS * HEAD_DIM
DIL_OUT_WIDTH = DIL_HEADS_PER_GROUP * HEAD_DIM
QKV_WIDTH = 3 * SB_WIDTH + 3 * DIL_WIDTH
RMS_EPS = 1e-6
ALIBI_MAX_BIAS = 8.0
ADAM_LR = 0.001
ADAM_B1 = 0.9
ADAM_B2 = 0.999
ADAM_EPS = 1e-08
ADAM_WD = 0.01
ADAM_STEP = 10

LANES = 128
BLK = 128
NEG = -1e30
EXP_UNDERFLOW = -104.0
SB_FWD_CHAINS = 4
SB_BWD_CHAINS = 4
DIL_CHAINS = 4
N_CHIPS = 4
VMEM_CAP = 56 * 1024 * 1024


def _vmem_limit(tile_bytes):
    return int(min(VMEM_CAP, max(32 * 1024 * 1024, 3 * tile_bytes + 8 * 1024 * 1024)))


def _hbm_array(shape, dtype):
    return pltpu.HBM(shape, dtype)


def _nbytes(shape, dtype):
    return math.prod(shape) * jnp.dtype(dtype).itemsize


def _dot(a, b):
    return jnp.dot(a, b, preferred_element_type=F32)


def _dot_nt(a, b):
    return lax.dot_general(a, b, (((1,), (1,)), ((), ())), preferred_element_type=F32)


def _dot_tn(a, b):
    return lax.dot_general(a, b, (((0,), (0,)), ((), ())), preferred_element_type=F32)


def _split2(x):
    hi = x.astype(BF16)
    lo = (x - hi.astype(F32)).astype(BF16)
    return hi, lo


def _sigmoid(x):
    return pl.reciprocal(1.0 + jnp.exp(-x), approx=True)


class _Carry:
    def __init__(self, arrays=(), out_shapes=(), aliased=False, sems=(), start=None, finish=None):
        self.arrays, self.out_shapes, self.aliased = list(arrays), list(out_shapes), aliased
        self.sems, self.start, self.finish = list(sems), start, finish

    def __bool__(self):
        return bool(self.arrays)

    def __add__(self, other):
        assert not self.aliased and not other.aliased
        n_a, n_o, n_s = len(self.arrays), len(self.out_shapes), len(self.sems)
        return _Carry(
            self.arrays + other.arrays, self.out_shapes + other.out_shapes, False, self.sems + other.sems,
            lambda i, o, s: (self.start(i[:n_a], o[:n_o], s[:n_s]), other.start(i[n_a:], o[n_o:], s[n_s:])),
            lambda i, o, s: (self.finish(i[:n_a], o[:n_o], s[:n_s]), other.finish(i[n_a:], o[n_o:], s[n_s:])))

    def call_args(self, n_in, n_out):
        aliases = {n_in + k: n_out + k for k in range(len(self.arrays))} if self.aliased else {}
        return _hbm_specs(len(self.arrays)), _hbm_specs(len(self.out_shapes)), self.out_shapes, aliases, self.sems

    def run(self, refs, n_in, n_out, step, n_steps, compute):
        if not self:
            compute()
            return
        n_c, n_o, n_s = len(self.arrays), len(self.out_shapes), len(self.sems)
        ins = refs[n_in:n_in + n_c]
        outs = refs[n_in + n_c + n_out:n_in + n_c + n_out + n_o]
        sems = refs[len(refs) - n_s:]

        @pl.when(step == 0)
        def _():
            self.start(ins, outs, sems)

        compute()

        @pl.when(step == n_steps - 1)
        def _():
            self.finish(ins, outs, sems)


def _mm(a, b, *, ta=False, tb=False, add=None, out_dtype=F32, tm, tn, tk, name, carry=None, epilogue=None,
        b_cols=None):
    carry = carry or _Carry()
    n_car = len(carry.arrays)
    pieces = list(a) if isinstance(a, (list, tuple)) else [a]
    n_a = len(pieces)
    widths = [p.shape[1] for p in pieces]
    starts = [sum(widths[:p]) for p in range(n_a)]
    if ta:
        kdim, m = pieces[0].shape[0], sum(widths)
    else:
        m, kdim = pieces[0].shape[0], sum(widths)
    if tb:
        n, k2 = b.shape
    else:
        k2, n = b.shape
    col0 = 0
    if b_cols is not None:
        assert b_cols[0] % tn == 0, name
        col0, n = b_cols[0] // tn, b_cols[1]
    assert kdim == k2 and m % tm == 0 and n % tn == 0 and kdim % tk == 0, (name, a.shape, b.shape)
    nk = kdim // tk
    assert n_a == 1 or (nk == 1 and not tb and (not ta or all(w % tm == 0 for w in widths))), name
    grid = (m // tm, n // tn, nk)
    a_mode = dict(pipeline_mode=pl.Buffered(1)) if grid[0] == 1 and nk == 1 else {}
    b_mode = dict(pipeline_mode=pl.Buffered(1)) if grid[1] == 1 and nk == 1 else {}
    if n_a == 1:
        a_specs = [pl.BlockSpec((tk, tm), lambda i, j, k: (k, i), **a_mode) if ta
                   else pl.BlockSpec((tm, tk), lambda i, j, k: (i, k), **a_mode)]
    elif ta:
        a_specs = [pl.BlockSpec((tk, tm), lambda i, j, k, s=s // tm, w=w // tm: (0, jnp.clip(i - s, 0, w - 1)))
                   for s, w in zip(starts, widths)]
    else:
        a_specs = [pl.BlockSpec((tm, w), lambda i, j, k: (i, 0)) for w in widths]
    b_spec = (pl.BlockSpec((tn, tk), lambda i, j, k: (j + col0, k), **b_mode) if tb
              else pl.BlockSpec((tk, tn), lambda i, j, k: (k, j + col0), **b_mode))
    o_spec = pl.BlockSpec((tm, tn), lambda i, j, k: (i, j))
    dims = ((((0,) if ta else (1,)), ((1,) if tb else (0,))), ((), ()))
    has_add = add is not None
    if epilogue is None:
        ep_fn, ep_rows, ep_params, ep_outs, ep_accs = None, [], [], [], []
        out_sds, out_specs = [_hbm_array((m, n), out_dtype)], [o_spec]
    else:
        ep_fn, ep_rows, ep_params, ep_outs, ep_accs = epilogue
        assert grid[1] == 1 or not ep_accs, name
        out_sds = [_hbm_array((m, w * grid[1]), d) for w, d in ep_outs]
        out_sds += [_hbm_array(sh, F32) for sh in ep_accs]
        out_specs = [pl.BlockSpec((tm, w), lambda i, j, k: (i, j)) for w, _ in ep_outs]
        out_specs += [pl.BlockSpec(sh, lambda i, j, k: (0, 0)) for sh in ep_accs]
    n_main = len(out_sds)
    use_scratch = nk > 1 and (ep_fn is not None or jnp.dtype(out_dtype) != jnp.dtype(F32))
    n_in = n_a + 1 + has_add + len(ep_rows) + len(ep_params)

    def finish(total, refs, pid):
        outs = refs[n_in + n_car:n_in + n_car + n_main]
        if ep_fn is None:
            outs[0][...] = total.astype(out_dtype)
            return
        first = n_a + 1 + has_add
        rows = [r[...].astype(F32) for r in refs[first:first + len(ep_rows)]]
        params = [p[...] for p in refs[first + len(ep_rows):n_in]]
        res = ep_fn(total, *rows, *params)
        for o_ref, v in zip(outs[:len(ep_outs)], res):
            o_ref[...] = v.astype(o_ref.dtype)
        acc_refs = outs[len(ep_outs):]
        if acc_refs:
            @pl.when(pid[0] == 0)
            def _():
                for r in acc_refs:
                    r[...] = jnp.zeros(r.shape, F32)

            for r, v in zip(acc_refs, res[len(ep_outs):]):
                r[...] += v

    def compute(refs, pid):
        a_ref, b_ref = refs[0], refs[n_a]
        add_ref = refs[n_a + 1] if has_add else None

        def dot(x, y):
            return lax.dot_general(x.astype(BF16), y.astype(BF16), dims, preferred_element_type=F32)

        if n_a > 1 and ta:
            for p_ref, s, w in zip(refs[:n_a], starts, widths):
                @pl.when((pid[0] >= s // tm) & (pid[0] < (s + w) // tm))
                def _(p_ref=p_ref):
                    prod = dot(p_ref[...], b_ref[...])
                    finish(prod + add_ref[...] if has_add else prod, refs, pid)
            return
        if n_a > 1:
            prod = dot(a_ref[...], b_ref[:widths[0], :])
            for p_ref, s, w in zip(refs[1:n_a], starts[1:], widths[1:]):
                prod += dot(p_ref[...], b_ref[s:s + w, :])
        else:
            prod = dot(a_ref[...], b_ref[...])
        if nk == 1:
            finish(prod + add_ref[...] if has_add else prod, refs, pid)
            return
        acc_ref = refs[n_in + n_car + n_main + len(carry.out_shapes)] if use_scratch else refs[n_in + n_car]
        k = pid[2]

        @pl.when(k == 0)
        def _():
            acc_ref[...] = prod + add_ref[...] if has_add else prod

        @pl.when(k > 0)
        def _():
            acc_ref[...] += prod

        if use_scratch:
            @pl.when(k == nk - 1)
            def _():
                finish(acc_ref[...], refs, pid)

    def body(*refs):
        pid = (pl.program_id(0), pl.program_id(1), pl.program_id(2))
        step = (pid[0] * grid[1] + pid[1]) * nk + pid[2]
        carry.run(refs, n_in, n_main, step, grid[0] * grid[1] * nk, lambda: compute(refs, pid))

    tile_bytes = ((n_a if ta else 1) * _nbytes((tm, tk), pieces[0].dtype)
                  + _nbytes((tk, tn), b.dtype) + 2 * _nbytes((tm, tn), F32)
                  + (_nbytes((tm, tn), F32) if has_add else 0)
                  + sum(_nbytes((tm, r.shape[1]), r.dtype) for r in ep_rows) + sum(_nbytes((tm, w), d) for w, d in ep_outs))
    in_specs = a_specs + [b_spec] + ([o_spec] if has_add else [])
    in_specs += [pl.BlockSpec((tm, r.shape[1] // grid[1]), lambda i, j, k: (i, j)) for r in ep_rows]
    in_specs += [pl.BlockSpec(p.shape, lambda i, j, k: (0, 0)) for p in ep_params]
    args = tuple(pieces) + (b,) + ((add,) if has_add else ()) + tuple(ep_rows) + tuple(ep_params)
    scratch = [pltpu.VMEM((tm, tn), F32)] if use_scratch else []
    serial = bool(carry) or bool(ep_accs)
    c_in, c_out, c_shapes, c_alias, c_sems = carry.call_args(n_in, n_main)
    res = pl.pallas_call(
        body, name=name, grid=grid,
        in_specs=in_specs + c_in, out_specs=out_specs + c_out, out_shape=out_sds + c_shapes,
        input_output_aliases=c_alias, scratch_shapes=scratch + c_sems,
        compiler_params=pltpu.CompilerParams(
            dimension_semantics=("arbitrary",) * 3 if serial else ("parallel", "parallel", "arbitrary"),
            vmem_limit_bytes=_vmem_limit(tile_bytes)),
    )(*args, *carry.arrays)
    main = res[0] if ep_fn is None else list(res[:n_main])
    return (main, res[n_main:]) if carry else main


def _rowwise(fn, rows, params, outs, accs, *, tm, name, carry=None):
    carry = carry or _Carry()
    t = rows[0].shape[0]
    assert t % tm == 0, (name, t, tm)
    n_r, n_p, n_o, n_c = len(rows), len(params), len(outs), len(carry.arrays)

    def compute(refs, first):
        vals = [r[...].astype(F32) for r in refs[:n_r]] + [p[...] for p in refs[n_r:n_r + n_p]]
        res = fn(*vals)
        o_refs = refs[n_r + n_p + n_c:n_r + n_p + n_c + n_o]
        a_refs = refs[n_r + n_p + n_c + n_o:n_r + n_p + n_c + n_o + len(accs)]
        for o_ref, v in zip(o_refs, res[:n_o]):
            o_ref[...] = v.astype(o_ref.dtype)
        if accs:
            @pl.when(first)
            def _():
                for a_ref in a_refs:
                    a_ref[...] = jnp.zeros(a_ref.shape, F32)

            for a_ref, v in zip(a_refs, res[n_o:]):
                a_ref[...] += v

    def body(*refs):
        step = pl.program_id(0)
        carry.run(refs, n_r + n_p, n_o + len(accs), step, t // tm, lambda: compute(refs, step == 0))

    in_specs = [pl.BlockSpec((tm, r.shape[1]), lambda i: (i, 0)) for r in rows]
    in_specs += [pl.BlockSpec(p.shape, lambda i: (0, 0)) for p in params]
    out_specs = [pl.BlockSpec((tm, w), lambda i: (i, 0)) for w, _ in outs]
    out_specs += [pl.BlockSpec(s, lambda i: (0, 0)) for s in accs]
    out_shape = [_hbm_array((t, w), d) for w, d in outs]
    out_shape += [_hbm_array(s, F32) for s in accs]
    tile_bytes = sum(_nbytes((tm, r.shape[1]), r.dtype) for r in rows) + sum(_nbytes((tm, w), F32) for w, _ in outs)
    c_in, c_out, c_shapes, c_alias, c_sems = carry.call_args(n_r + n_p, n_o + len(accs))
    res = pl.pallas_call(
        body, name=name, grid=(t // tm,), in_specs=in_specs + c_in, out_specs=out_specs + c_out,
        out_shape=out_shape + c_shapes, input_output_aliases=c_alias, scratch_shapes=c_sems,
        compiler_params=pltpu.CompilerParams(
            dimension_semantics=("arbitrary",) if accs or carry else ("parallel",),
            vmem_limit_bytes=_vmem_limit(2 * tile_bytes)),
    )(*rows, *params, *carry.arrays)
    own = n_o + len(accs)
    return (list(res[:own]) + [res[own:]]) if carry else res


def _rms_stats(x):
    r = lax.rsqrt(jnp.mean(x * x, axis=-1, keepdims=True) + RMS_EPS)
    return x * r, r


def _rms_bwd(dy, xhat, r, g):
    dxhat = dy * g
    dx = r * (dxhat - xhat * jnp.mean(dxhat * xhat, axis=-1, keepdims=True))
    return dx, dy * xhat


def _sb_consts():
    lane = lax.broadcasted_iota(jnp.int32, (BLK, LANES), 1)
    head0 = lane < HEAD_DIM
    row = lax.broadcasted_iota(jnp.int32, (2 * BLK, BLK), 0) % BLK
    col = lax.broadcasted_iota(jnp.int32, (2 * BLK, BLK), 1)
    causal = col < row
    jj = lax.broadcasted_iota(jnp.int32, (BLK, BLK), 0)
    ss = lax.broadcasted_iota(jnp.int32, (BLK, BLK), 1)
    suffix = jnp.where(jj > ss, 1.0, 0.0).astype(BF16)
    return head0, causal, suffix


def _stack_heads(x, head0):
    zero = jnp.zeros_like(x)
    return jnp.concatenate([jnp.where(head0, x, zero), jnp.where(head0, zero, x)], axis=0)


def _sb_logits(z, causal, masked):
    sp = jnp.log(1.0 + jnp.exp(-jnp.abs(z)))
    log_keep = -(jnp.maximum(z, 0.0) + sp)
    log_beta = jnp.minimum(z, 0.0) - sp
    if masked:
        log_keep = jnp.where(causal, log_keep, 0.0)
    return log_keep, log_beta


def _suffix_sums(x, suffix):
    hi, lo = _split2(x)
    after = _dot(hi, suffix) + _dot(lo, suffix)
    total = jnp.broadcast_to(after[:, 0:1] + x[:, 0:1], x.shape)
    return after, total


def _sb_walk_back(i, state, per_chain, tile):
    def alive(st):
        worst = functools.reduce(jnp.maximum, [st[p][:, 0:1] for p in range(0, len(st), per_chain)])
        return jnp.max(worst) > EXP_UNDERFLOW

    def cond(c):
        return jnp.logical_and(c[0] < i, alive(c[1]))

    def body(c):
        return c[0] + 1, tile(i - 1 - c[0], c[1], False)

    return lax.while_loop(cond, body, (jnp.int32(0), state))[1]


def _lane_blocks(x, n):
    return [x[:, p * LANES:(p + 1) * LANES] for p in range(n)]


def _sb_fwd(qkv, b_sz, s_len, carry):
    nq = s_len // BLK
    n_pairs = SB_WIDTH // LANES
    ch = SB_FWD_CHAINS
    n_steps = n_pairs // ch
    scale = 1.0 / math.sqrt(HEAD_DIM)

    def compute(q_ref, k_ref, v_ref, o_ref):
        head0, causal, suffix = _sb_consts()

        def q_block(i, _):
            qs = pl.multiple_of(i * BLK, BLK)
            q_all = (q_ref[pl.ds(qs, BLK), :] * scale).astype(BF16)
            q01 = [_stack_heads(q, head0) for q in _lane_blocks(q_all, ch)]

            def tile(j, state, masked):
                ks = pl.multiple_of(j * BLK, BLK)
                ks_ = _lane_blocks(k_ref[pl.ds(ks, BLK), :].astype(BF16), ch)
                vs_ = _lane_blocks(v_ref[pl.ds(ks, BLK), :].astype(BF16), ch)
                zs = [_dot_nt(q01[p], ks_[p]) for p in range(ch)]
                logits = [_sb_logits(z, causal, masked) for z in zs]
                sums = [_suffix_sums(lg[0], suffix) for lg in logits]
                out = []
                for p in range(ch):
                    carry, acc = state[2 * p], state[2 * p + 1]
                    after, total = sums[p]
                    a = jnp.exp(logits[p][1] + carry + after)
                    if masked:
                        a = jnp.where(causal, a, 0.0)
                    a_hi, a_lo = _split2(a)
                    a_cat = jnp.concatenate([a_hi[:BLK], a_hi[BLK:], a_lo[:BLK], a_lo[BLK:]], axis=1)
                    v01 = _stack_heads(vs_[p], head0)
                    out += [carry + total, acc + _dot(a_cat, jnp.concatenate([v01, v01], axis=0))]
                return tuple(out)

            state = (jnp.zeros((2 * BLK, BLK), F32), jnp.zeros((BLK, LANES), F32)) * ch
            state = tile(i, state, True)
            state = _sb_walk_back(i, state, 2, tile)
            o_ref[pl.ds(qs, BLK), :] = jnp.concatenate([state[2 * p + 1] for p in range(ch)], axis=1)
            return 0

        lax.fori_loop(0, nq, q_block, 0)

    def body(*refs):
        step = pl.program_id(0) * n_steps + pl.program_id(1)
        o_ref = refs[3 + len(carry.arrays)]
        carry.run(refs, 3, 1, step, b_sz * n_steps, lambda: compute(refs[0], refs[1], refs[2], o_ref))

    blk = lambda off: pl.BlockSpec((None, s_len, ch * LANES), lambda b, p: (b, 0, off + p))
    c_in, c_out, c_shapes, c_alias, c_sems = carry.call_args(3, 1)
    res = pl.pallas_call(
        body, name="sb_fwd", grid=(b_sz, n_steps),
        in_specs=[blk(0), blk(n_steps), blk(2 * n_steps)] + c_in, out_specs=[blk(0)] + c_out,
        out_shape=[_hbm_array((b_sz, s_len, SB_WIDTH), F32)] + c_shapes,
        input_output_aliases=c_alias, scratch_shapes=c_sems,
        compiler_params=pltpu.CompilerParams(dimension_semantics=("arbitrary", "arbitrary"),
                                             vmem_limit_bytes=VMEM_CAP),
    )(qkv, qkv, qkv, *carry.arrays)
    return res[0], res[1:]


def _sb_bwd(qkv, o_sb, do_sb, b_sz, s_len, carry):
    nq = s_len // BLK
    n_pairs = SB_WIDTH // LANES
    ch = SB_BWD_CHAINS
    n_steps = n_pairs // ch
    scale = 1.0 / math.sqrt(HEAD_DIM)

    def compute(q_ref, k_ref, v_ref, o_ref, do_ref, dq_ref, dk_ref, dv_ref, dk_acc, dv_acc):
        head0, causal, suffix = _sb_consts()
        lrow = lax.broadcasted_iota(jnp.int32, (LANES, LANES), 0)
        ones_h0 = jnp.where(lrow < HEAD_DIM, 1.0, 0.0).astype(BF16)
        ones_h1 = jnp.where(lrow >= HEAD_DIM, 1.0, 0.0).astype(BF16)
        dk_acc[...] = jnp.zeros(dk_acc.shape, F32)
        dv_acc[...] = jnp.zeros(dv_acc.shape, F32)

        def q_block(i, _):
            qs = pl.multiple_of(i * BLK, BLK)
            q_all = (q_ref[pl.ds(qs, BLK), :] * scale).astype(BF16)
            do_all = do_ref[pl.ds(qs, BLK), :].astype(BF16)
            dd_all = do_all.astype(F32) * o_ref[pl.ds(qs, BLK), :]
            q01 = [_stack_heads(q, head0) for q in _lane_blocks(q_all, ch)]
            do01 = [_stack_heads(d, head0) for d in _lane_blocks(do_all, ch)]
            tot = []
            for dd in _lane_blocks(dd_all, ch):
                dd_hi, dd_lo = _split2(dd)
                tot.append(jnp.concatenate([_dot(dd_hi, ones_h0) + _dot(dd_lo, ones_h0),
                                            _dot(dd_hi, ones_h1) + _dot(dd_lo, ones_h1)], axis=0))

            def tile(j, state, masked):
                ks = pl.multiple_of(j * BLK, BLK)
                ks_ = _lane_blocks(k_ref[pl.ds(ks, BLK), :].astype(BF16), ch)
                vs_ = _lane_blocks(v_ref[pl.ds(ks, BLK), :].astype(BF16), ch)
                zs = [_dot_nt(q01[p], ks_[p]) for p in range(ch)]
                das = [_dot_nt(do01[p], vs_[p]) for p in range(ch)]
                logits = [_sb_logits(z, causal, masked) for z in zs]
                sums = [_suffix_sums(lg[0], suffix) for lg in logits]
                a_s, e_s = [], []
                for p in range(ch):
                    a = jnp.exp(logits[p][1] + state[3 * p] + sums[p][0])
                    if masked:
                        a = jnp.where(causal, a, 0.0)
                    a_s.append(a)
                    e_s.append(a * das[p])
                e_sums = [_suffix_sums(e, suffix) for e in e_s]
                out, dks, dvs = [], [], []
                for p in range(ch):
                    carry, rcarry, dq = state[3 * p:3 * p + 3]
                    e = e_s[p]
                    before = tot[p] - (rcarry + e_sums[p][0] + e)
                    beta = jnp.exp(logits[p][1])
                    dz = e * (1.0 - beta) - beta * before
                    if masked:
                        dz = jnp.where(causal, dz, 0.0)
                    dz_b = dz.astype(BF16)
                    dks.append(_dot_tn(dz_b, q01[p]))
                    dvs.append(_dot_tn(a_s[p].astype(BF16), do01[p]))
                    out += [carry + sums[p][1], rcarry + e_sums[p][1], dq + _dot(dz_b, ks_[p])]
                dk_acc[pl.ds(ks, BLK), :] += jnp.concatenate(dks, axis=1)
                dv_acc[pl.ds(ks, BLK), :] += jnp.concatenate(dvs, axis=1)
                return tuple(out)

            state = (jnp.zeros((2 * BLK, BLK), F32),) * (3 * ch)
            state = tile(i, state, True)
            state = _sb_walk_back(i, state, 3, tile)
            dq = [jnp.where(head0, state[3 * p + 2][:BLK], state[3 * p + 2][BLK:]) for p in range(ch)]
            dq_ref[pl.ds(qs, BLK), :] = (jnp.concatenate(dq, axis=1) * scale).astype(dq_ref.dtype)
            return 0

        lax.fori_loop(0, nq, q_block, 0)
        dk_ref[...] = dk_acc[...].astype(dk_ref.dtype)
        dv_ref[...] = dv_acc[...].astype(dv_ref.dtype)

    def body(*refs):
        step = pl.program_id(0) * n_steps + pl.program_id(1)
        n_c, n_o = len(carry.arrays), len(carry.out_shapes)
        own = refs[:5] + refs[5 + n_c:8 + n_c] + refs[8 + n_c + n_o:10 + n_c + n_o]
        carry.run(refs, 5, 3, step, b_sz * n_steps, lambda: compute(*own))

    blk = lambda off: pl.BlockSpec((None, s_len, ch * LANES), lambda b, p: (b, 0, off + p))
    once = lambda off: pl.BlockSpec((None, s_len, ch * LANES), lambda b, p: (b, 0, off + p),
                                    pipeline_mode=pl.Buffered(1))
    out_sd = _hbm_array((b_sz, s_len, SB_WIDTH), BF16)
    c_in, c_out, c_shapes, c_alias, c_sems = carry.call_args(5, 3)
    res = pl.pallas_call(
        body, name="sb_bwd", grid=(b_sz, n_steps),
        in_specs=[once(0), once(n_steps), once(2 * n_steps), once(0), once(0)] + c_in,
        out_specs=[blk(0), blk(0), blk(0)] + c_out, out_shape=[out_sd, out_sd, out_sd] + c_shapes,
        input_output_aliases=c_alias,
        scratch_shapes=[pltpu.VMEM((s_len, ch * LANES), F32), pltpu.VMEM((s_len, ch * LANES), F32)] + c_sems,
        compiler_params=pltpu.CompilerParams(dimension_semantics=("arbitrary", "arbitrary"),
                                             vmem_limit_bytes=VMEM_CAP),
    )(qkv, qkv, qkv, o_sb, do_sb, *carry.arrays)
    return res[:3], res[3:]


def _dil_consts(group, pair_idx, dilation):
    lane = lax.broadcasted_iota(jnp.int32, (BLK, LANES), 1)
    head0 = lane < HEAD_DIM
    row = lax.broadcasted_iota(jnp.int32, (2 * BLK, BLK), 0)
    qa = row % BLK
    kb = lax.broadcasted_iota(jnp.int32, (2 * BLK, BLK), 1)
    head = (group * DIL_HEADS_PER_GROUP + 2 * pair_idx + row // BLK).astype(F32)
    slope = jnp.exp((-ALIBI_MAX_BIAS * math.log(2.0) / DIL_HEADS) * (head + 1.0))
    valid_cur = kb <= qa
    valid_prev = kb >= qa
    bias_cur = -slope * ((qa - kb) * dilation).astype(F32)
    bias_prev = -slope * ((BLK + qa - kb) * dilation).astype(F32)
    return head0, valid_cur, valid_prev, bias_cur, bias_prev


def _dil_units(s_len, dilation):
    nb = s_len // dilation // BLK
    return [(r, n) for r in range(dilation) for n in range(nb)]


def _dil_rows(n, r, dilation):
    if dilation == 1:
        return pl.ds(n * BLK, BLK)
    return pl.ds(n * BLK * dilation + r, BLK, stride=dilation)


def _dil_scores(q01, k, bias, valid):
    s = _dot_nt(q01, k) * (1.0 / math.sqrt(HEAD_DIM)) + bias
    return jnp.where(valid, s, NEG)


def _dil_fwd(qkv, b_sz, s_len, carry):
    n_pairs = DIL_OUT_WIDTH // LANES
    q_off = 3 * SB_WIDTH // LANES
    per_kind = DIL_WIDTH // LANES

    def compute(pair_idx, qkv_refs, o_ref, lse_ref, m_s, l_s):
        m_s[...] = jnp.full(m_s.shape, NEG, F32)
        l_s[...] = jnp.zeros(l_s.shape, F32)
        o_ref[...] = jnp.zeros(o_ref.shape, F32)
        for g, (_, dilation) in enumerate(DIL_PAIRS):
            q_ref, k_ref, v_ref = qkv_refs[3 * g:3 * g + 3]
            head0, valid_cur, valid_prev, bias_cur, bias_prev = _dil_consts(g, pair_idx, dilation)
            units = _dil_units(s_len, dilation)
            for u0 in range(0, len(units), DIL_CHAINS):
                group = units[u0:u0 + DIL_CHAINS]
                rows_of = [_dil_rows(n, r, dilation) for r, n in group]
                scores, values = [], []
                for (r, n), rows in zip(group, rows_of):
                    q01 = _stack_heads(q_ref[rows, :].astype(BF16), head0)
                    sc = [_dil_scores(q01, k_ref[rows, :].astype(BF16), bias_cur, valid_cur)]
                    vals = [_stack_heads(v_ref[rows, :].astype(BF16), head0)]
                    if n > 0:
                        prev = _dil_rows(n - 1, r, dilation)
                        sc.append(_dil_scores(q01, k_ref[prev, :].astype(BF16), bias_prev, valid_prev))
                        vals.append(_stack_heads(v_ref[prev, :].astype(BF16), head0))
                    scores.append(sc)
                    values.append(vals)
                stats = []
                for sc, rows in zip(scores, rows_of):
                    m_blk = functools.reduce(jnp.maximum, [jnp.max(x, axis=-1, keepdims=True) for x in sc])
                    m_old = jnp.concatenate([m_s.at[0][rows, :], m_s.at[1][rows, :]], axis=0)
                    l_old = jnp.concatenate([l_s.at[0][rows, :], l_s.at[1][rows, :]], axis=0)
                    m_new = jnp.maximum(m_old, m_blk)
                    probs = [jnp.exp(x - m_new) for x in sc]
                    l_blk = functools.reduce(jnp.add, [jnp.sum(p, axis=-1, keepdims=True) for p in probs])
                    alpha = jnp.exp(m_old - m_new)
                    stats.append((m_new, alpha * l_old + l_blk, alpha, probs))
                for (m_new, l_new, alpha, probs), vals, rows in zip(stats, values, rows_of):
                    alpha_tok = jnp.where(head0, alpha[:BLK], alpha[BLK:])
                    p_cat = jnp.concatenate(
                        [h for p in probs for h in (p[:BLK].astype(BF16), p[BLK:].astype(BF16))], axis=1)
                    o_ref[rows, :] = alpha_tok * o_ref[rows, :] + _dot(p_cat, jnp.concatenate(vals, axis=0))
                    m_s.at[0][rows, :] = m_new[:BLK]
                    m_s.at[1][rows, :] = m_new[BLK:]
                    l_s.at[0][rows, :] = l_new[:BLK]
                    l_s.at[1][rows, :] = l_new[BLK:]
        lane = lax.broadcasted_iota(jnp.int32, (BLK, LANES), 1)
        for c in range(s_len // BLK):
            rows = pl.ds(c * BLK, BLK)
            l0, l1 = l_s.at[0][rows, :], l_s.at[1][rows, :]
            o_ref[rows, :] = o_ref[rows, :] / jnp.where(lane < HEAD_DIM, l0, l1)
            lse_ref.at[0][rows, :] = m_s.at[0][rows, :] + jnp.log(l0)
            lse_ref.at[1][rows, :] = m_s.at[1][rows, :] + jnp.log(l1)

    def body(*refs):
        pair_idx = pl.program_id(1)
        step = pl.program_id(0) * n_pairs + pair_idx
        n_c, n_o = len(carry.arrays), len(carry.out_shapes)
        o_ref, lse_ref = refs[9 + n_c:11 + n_c]
        m_s, l_s = refs[11 + n_c + n_o:13 + n_c + n_o]
        carry.run(refs, 9, 2, step, b_sz * n_pairs, lambda: compute(pair_idx, refs[:9], o_ref, lse_ref, m_s, l_s))

    in_specs = []
    for g in range(len(DIL_PAIRS)):
        for kind in range(3):
            off = q_off + kind * per_kind + g * n_pairs
            in_specs.append(pl.BlockSpec((None, s_len, LANES), lambda b, p, off=off: (b, 0, off + p)))
    c_in, c_out, c_shapes, c_alias, c_sems = carry.call_args(9, 2)
    res = pl.pallas_call(
        body, name="dil_fwd", grid=(b_sz, n_pairs),
        in_specs=in_specs + c_in,
        out_specs=[pl.BlockSpec((None, s_len, LANES), lambda b, p: (b, 0, p)),
                   pl.BlockSpec((None, None, 2, s_len, LANES), lambda b, p: (b, p, 0, 0, 0))] + c_out,
        out_shape=[_hbm_array((b_sz, s_len, DIL_OUT_WIDTH), F32),
                   _hbm_array((b_sz, n_pairs, 2, s_len, LANES), F32)] + c_shapes,
        input_output_aliases=c_alias,
        scratch_shapes=[pltpu.VMEM((2, s_len, LANES), F32), pltpu.VMEM((2, s_len, LANES), F32)] + c_sems,
        compiler_params=pltpu.CompilerParams(dimension_semantics=("arbitrary", "arbitrary"),
                                             vmem_limit_bytes=VMEM_CAP),
    )(*([qkv] * 9), *carry.arrays)
    return res[0], res[1], res[2:]


def _dil_bwd(qkv, o_dl, lse, do_dl, b_sz, s_len, carry):
    n_pairs = DIL_OUT_WIDTH // LANES
    n_groups = len(DIL_PAIRS)
    q_off = 3 * SB_WIDTH // LANES
    per_kind = DIL_WIDTH // LANES

    def compute(pair_idx, group, q_ref, k_ref, v_ref, o_ref, lse_ref, do_ref, dq_ref, dk_ref, dv_ref, d_s, dq_s, dk_s, dv_s):
        lrow = lax.broadcasted_iota(jnp.int32, (LANES, LANES), 0)
        ones_h0 = jnp.where(lrow < HEAD_DIM, 1.0, 0.0).astype(BF16)
        ones_h1 = jnp.where(lrow >= HEAD_DIM, 1.0, 0.0).astype(BF16)
        for c in range(s_len // BLK):
            rows = pl.ds(c * BLK, BLK)
            dd_hi, dd_lo = _split2(do_ref[rows, :] * o_ref[rows, :])
            d_s.at[0][rows, :] = _dot(dd_hi, ones_h0) + _dot(dd_lo, ones_h0)
            d_s.at[1][rows, :] = _dot(dd_hi, ones_h1) + _dot(dd_lo, ones_h1)
        dk_s[...] = jnp.zeros(dk_s.shape, F32)
        dv_s[...] = jnp.zeros(dv_s.shape, F32)

        def one_group(g, dilation):
            head0, valid_cur, valid_prev, bias_cur, bias_prev = _dil_consts(g, pair_idx, dilation)
            units = _dil_units(s_len, dilation)
            scale = 1.0 / math.sqrt(HEAD_DIM)
            for u0 in range(0, len(units), DIL_CHAINS):
                chunk = units[u0:u0 + DIL_CHAINS]
                loaded = []
                for r, n in chunk:
                    rows = _dil_rows(n, r, dilation)
                    q01 = _stack_heads(q_ref[rows, :].astype(BF16), head0)
                    do01 = _stack_heads(do_ref[rows, :].astype(BF16), head0)
                    lse01 = jnp.concatenate([lse_ref.at[0][rows, :], lse_ref.at[1][rows, :]], axis=0)
                    d01 = jnp.concatenate([d_s.at[0][rows, :], d_s.at[1][rows, :]], axis=0)
                    blocks = [(rows, bias_cur, valid_cur)]
                    if n > 0:
                        blocks.append((_dil_rows(n - 1, r, dilation), bias_prev, valid_prev))
                    parts = []
                    for krows, bias, valid in blocks:
                        k = k_ref[krows, :].astype(BF16)
                        v = v_ref[krows, :].astype(BF16)
                        parts.append((krows, k, _dil_scores(q01, k, bias, valid), _dot_nt(do01, v)))
                    loaded.append((rows, q01, do01, lse01, d01, parts))
                grads = []
                for rows, q01, do01, lse01, d01, parts in loaded:
                    for krows, k, sc, dp in parts:
                        p = jnp.exp(sc - lse01)
                        grads.append((p.astype(BF16), (p * (dp - d01) * scale).astype(BF16)))
                it = iter(grads)
                updates = []
                for rows, q01, do01, lse01, d01, parts in loaded:
                    dq = jnp.zeros((2 * BLK, LANES), F32)
                    for krows, k, sc, dp in parts:
                        p_b, ds = next(it)
                        dq = dq + _dot(ds, k)
                        updates.append((krows, _dot_tn(ds, q01), _dot_tn(p_b, do01)))
                    dq_s[rows, :] = jnp.where(head0, dq[:BLK], dq[BLK:])
                for krows, dk, dv in updates:
                    dk_s[krows, :] = dk_s[krows, :] + dk
                    dv_s[krows, :] = dv_s[krows, :] + dv

        for g, (_, dilation) in enumerate(DIL_PAIRS):
            pl.when(group == g)(functools.partial(one_group, g, dilation))
        dq_ref[...] = dq_s[...].astype(dq_ref.dtype)
        dk_ref[...] = dk_s[...].astype(dk_ref.dtype)
        dv_ref[...] = dv_s[...].astype(dv_ref.dtype)

    def body(*refs):
        pair_idx, group = pl.program_id(1), pl.program_id(2)
        step = (pl.program_id(0) * n_pairs + pair_idx) * n_groups + group
        n_c, n_o = len(carry.arrays), len(carry.out_shapes)
        own = refs[:6] + refs[6 + n_c:9 + n_c] + refs[9 + n_c + n_o:13 + n_c + n_o]
        carry.run(refs, 6, 3, step, b_sz * n_pairs * n_groups, lambda: compute(pair_idx, group, *own))

    def qkv_spec(kind):
        return pl.BlockSpec((None, s_len, LANES),
                            lambda b, p, g: (b, 0, q_off + kind * per_kind + g * n_pairs + p))

    tok_spec = pl.BlockSpec((None, s_len, LANES), lambda b, p, g: (b, 0, p))
    out_spec = pl.BlockSpec((None, s_len, LANES), lambda b, p, g: (b, 0, g * n_pairs + p))
    out_sd = _hbm_array((b_sz, s_len, DIL_WIDTH), BF16)
    c_in, c_out, c_shapes, c_alias, c_sems = carry.call_args(6, 3)
    res = pl.pallas_call(
        body, name="dil_bwd", grid=(b_sz, n_pairs, n_groups),
        in_specs=[qkv_spec(0), qkv_spec(1), qkv_spec(2), tok_spec,
                  pl.BlockSpec((None, None, 2, s_len, LANES), lambda b, p, g: (b, p, 0, 0, 0)), tok_spec] + c_in,
        out_specs=[out_spec, out_spec, out_spec] + c_out,
        out_shape=[out_sd, out_sd, out_sd] + c_shapes,
        input_output_aliases=c_alias,
        scratch_shapes=[pltpu.VMEM((2, s_len, LANES), F32)] + [pltpu.VMEM((s_len, LANES), F32)] * 3 + c_sems,
        compiler_params=pltpu.CompilerParams(dimension_semantics=("arbitrary", "arbitrary", "arbitrary"),
                                             vmem_limit_bytes=VMEM_CAP),
    )(qkv, qkv, qkv, o_dl, lse, do_dl, *carry.arrays)
    return res[:3], res[3:]


def _mesh_pos():
    return lax.axis_index("x"), lax.axis_index("y"), lax.axis_index("c")


def _other_chips(x, y):
    return [(1 - x, y), (x, 1 - y), (1 - x, 1 - y)]


def _hbm_specs(n):
    return [pl.BlockSpec(memory_space=pl.ANY)] * n


SWAPPED = ("w_ffn_in",)


def _slot(x, y, swapped):
    return 2 * y + x if swapped else 2 * x + y


def _cast_to_slab(w, name, swapped=False):
    rows, cols = w.shape
    mine = jnp.reshape(_slot(lax.axis_index("x"), lax.axis_index("y"), swapped), (1,)).astype(jnp.int32)

    def body(idx_ref, w_ref, o_ref):
        o_ref[...] = w_ref[...].astype(BF16)

    return pl.pallas_call(
        body, name=name,
        grid_spec=pltpu.PrefetchScalarGridSpec(
            num_scalar_prefetch=1, grid=(1,),
            in_specs=[pl.BlockSpec((rows, cols), lambda i, idx: (0, 0))],
            out_specs=pl.BlockSpec((None, rows, cols), lambda i, idx: (idx[0], 0, 0))),
        out_shape=_hbm_array((N_CHIPS, rows, cols), BF16),
        compiler_params=pltpu.CompilerParams(vmem_limit_bytes=_vmem_limit(rows * cols * 6)),
    )(mine, w)


def _gather_issue(slabs, send_sems, recv_sems, swapped):
    x, y, c = _mesh_pos()
    for k, slab in enumerate(slabs):
        half = slab.shape[1] // 2
        rows = slab.at[_slot(x, y, swapped[k]), pl.ds(c * half, half), :]
        for r, (px, py) in enumerate(_other_chips(x, y)):
            pltpu.make_async_remote_copy(
                src_ref=rows, dst_ref=rows, send_sem=send_sems.at[6 * k + r], recv_sem=recv_sems.at[6 * k + r],
                device_id=(px, py, c), device_id_type=MESH).start()


def _gather_complete(slabs, send_sems, recv_sems, swapped):
    x, y, c = _mesh_pos()
    chips = _other_chips(x, y)

    def copy(k, sem, block, rows, to):
        ref = slabs[k].at[block, rows, :]
        return pltpu.make_async_remote_copy(
            src_ref=ref, dst_ref=ref, send_sem=send_sems.at[sem], recv_sem=recv_sems.at[sem],
            device_id=to, device_id_type=MESH)

    for k, slab in enumerate(slabs):
        half = slab.shape[1] // 2
        for r, (px, py) in enumerate(chips):
            theirs = _slot(px, py, swapped[k])
            copy(k, 6 * k + r, theirs, pl.ds(c * half, half), (px, py, c)).wait_recv()
            copy(k, 6 * k + 3 + r, theirs, pl.ds(c * half, half), (x, y, 1 - c)).start()
    for k, slab in enumerate(slabs):
        half = slab.shape[1] // 2
        for r, (px, py) in enumerate(chips):
            copy(k, 6 * k + 3 + r, _slot(px, py, swapped[k]), pl.ds((1 - c) * half, half), (x, y, 1 - c)).wait_recv()
    for k, slab in enumerate(slabs):
        half = slab.shape[1] // 2
        for r, (px, py) in enumerate(chips):
            copy(k, 6 * k + r, _slot(x, y, swapped[k]), pl.ds(c * half, half), (px, py, c)).wait_send()
            copy(k, 6 * k + 3 + r, _slot(px, py, swapped[k]), pl.ds(c * half, half), (x, y, 1 - c)).wait_send()


def _gather_sems(n):
    return [pltpu.SemaphoreType.DMA((6 * n,)), pltpu.SemaphoreType.DMA((6 * n,))]


def _gather_carry(slabs, names):
    swapped = [k in SWAPPED for k in names]
    return _Carry(slabs, [_hbm_array(a.shape, a.dtype) for a in slabs], True, _gather_sems(len(slabs)),
                  lambda ins, outs, sems: _gather_issue(outs, *sems, swapped),
                  lambda ins, outs, sems: _gather_complete(outs, *sems, swapped))


def _pair_copies(ins, outs, send_sems, recv_sems):
    x, y, c = _mesh_pos()
    copies = []
    for k, g in enumerate(ins):
        half = g.shape[1] // 2
        copies.append(pltpu.make_async_remote_copy(
            src_ref=g.at[:, pl.ds((1 - c) * half, half), :], dst_ref=outs[k],
            send_sem=send_sems.at[k], recv_sem=recv_sems.at[k],
            device_id=(x, y, 1 - c), device_id_type=MESH))
    return copies


def _pair_carry(grads):
    n = len(grads)

    def start(ins, outs, sems):
        for cp in _pair_copies(ins, outs, *sems):
            cp.start()

    def finish(ins, outs, sems):
        for cp in _pair_copies(ins, outs, *sems):
            cp.wait()

    return _Carry(grads, [_hbm_array((N_CHIPS, g.shape[1] // 2, g.shape[2]), g.dtype) for g in grads], False,
                  [pltpu.SemaphoreType.DMA((n,)), pltpu.SemaphoreType.DMA((n,))], start, finish)


def _pair_exchange(grads, tag):
    carry = _pair_carry(grads)
    n = len(grads)

    def body(*refs):
        carry.start(refs[:n], refs[n:2 * n], refs[2 * n:])
        carry.finish(refs[:n], refs[n:2 * n], refs[2 * n:])

    return pl.pallas_call(
        body, name="grad_pair_exchange_" + tag, in_specs=_hbm_specs(n), out_specs=_hbm_specs(n),
        out_shape=carry.out_shapes, scratch_shapes=carry.sems,
    )(*grads)


def _pair_sum(grad, other, name, swapped):
    _, rows, cols = grad.shape
    half = rows // 2
    x, y, c = _mesh_pos()
    idx = jnp.stack([c, _slot(x, y, swapped)]).astype(jnp.int32)

    def body(idx_ref, g_ref, p_ref, own_ref, sb_ref):
        s = g_ref[...] + p_ref[...].astype(F32)
        sb_ref[...] = s.astype(BF16)

        @pl.when(pl.program_id(0) == idx_ref[1])
        def _():
            own_ref[...] = s

    blk = pl.BlockSpec((None, half, cols), lambda p, idx: (p, 0, 0))
    return pl.pallas_call(
        body, name=name,
        grid_spec=pltpu.PrefetchScalarGridSpec(
            num_scalar_prefetch=1, grid=(N_CHIPS,),
            in_specs=[pl.BlockSpec((None, half, cols), lambda p, idx: (p, idx[0], 0)), blk],
            out_specs=[pl.BlockSpec((half, cols), lambda p, idx: (0, 0)), blk]),
        out_shape=[_hbm_array((half, cols), F32), _hbm_array((N_CHIPS, half, cols), BF16)],
        compiler_params=pltpu.CompilerParams(dimension_semantics=("arbitrary",),
                                             vmem_limit_bytes=_vmem_limit(4 * half * cols * 4)),
    )(idx, grad, other)


def _chip_copies(sums_bf16, lands, send_sems, recv_sems, swapped):
    x, y, c = _mesh_pos()
    return [pltpu.make_async_remote_copy(
        src_ref=sums_bf16[k].at[_slot(px, py, swapped[k])], dst_ref=lands[k].at[r],
        send_sem=send_sems.at[3 * k + r], recv_sem=recv_sems.at[3 * k + r],
        device_id=(px, py, c), device_id_type=MESH)
        for k in range(len(sums_bf16)) for r, (px, py) in enumerate(_other_chips(x, y))]


def _chip_carry(sums_bf16, names):
    swapped = [k in SWAPPED for k in names]

    def start(ins, outs, sems):
        for cp in _chip_copies(ins, outs, *sems, swapped):
            cp.start()

    def finish(ins, outs, sems):
        for cp in _chip_copies(ins, outs, *sems, swapped):
            cp.wait()

    return _Carry(sums_bf16, _chip_landing(sums_bf16), False, _chip_sems(len(sums_bf16)), start, finish)


def _chip_sems(n):
    return [pltpu.SemaphoreType.DMA((3 * n,)), pltpu.SemaphoreType.DMA((3 * n,))]


def _chip_landing(sums_bf16):
    return [_hbm_array((N_CHIPS - 1,) + s.shape[1:], BF16) for s in sums_bf16]


def _chip_sum(own, landed, name):
    rows, cols = own.shape
    core = jnp.reshape(lax.axis_index("c"), (1,)).astype(jnp.int32)

    def body(core_ref, o_ref, l_ref, out_ref):
        out_ref[...] = ((o_ref[...] + l_ref[0].astype(F32)) + l_ref[1].astype(F32)) + l_ref[2].astype(F32)

    return pl.pallas_call(
        body, name=name,
        grid_spec=pltpu.PrefetchScalarGridSpec(
            num_scalar_prefetch=1, grid=(1,),
            in_specs=[pl.BlockSpec((rows, cols), lambda i, core_ref: (0, 0)),
                      pl.BlockSpec((N_CHIPS - 1, rows, cols), lambda i, core_ref: (0, 0, 0))],
            out_specs=pl.BlockSpec((rows, cols), lambda i, core_ref: (core_ref[0], 0))),
        out_shape=_hbm_array((2 * rows, cols), F32),
        compiler_params=pltpu.CompilerParams(vmem_limit_bytes=_vmem_limit(3 * rows * cols * 4)),
    )(core, own, landed)


def _final_exchange(fulls, v):
    n = len(fulls)
    rows, cols = v.shape
    n_dev = 8

    def body(*refs):
        v_ref, out_ref = refs[0], refs[1 + 2 * n]
        outs = refs[1 + n:1 + 2 * n]
        buf, v_send, v_recv, h_send, h_recv = refs[2 + 2 * n:]
        x, y, c = _mesh_pos()
        me = 4 * x + 2 * y + c
        buf[me] = v_ref[...]
        peers = [(1 - x if r & 4 else x, 1 - y if r & 2 else y, 1 - c if r & 1 else c) for r in range(1, n_dev)]
        copies = []
        for r, peer in enumerate(peers):
            copies.append(pltpu.make_async_remote_copy(
                src_ref=v_ref, dst_ref=buf.at[me], send_sem=v_send.at[r], recv_sem=v_recv.at[r],
                device_id=peer, device_id_type=MESH))
        for k in range(n):
            half = fulls[k].shape[0] // 2
            mine = outs[k].at[pl.ds(c * half, half), :]
            copies.append(pltpu.make_async_remote_copy(
                src_ref=mine, dst_ref=mine, send_sem=h_send.at[k], recv_sem=h_recv.at[k],
                device_id=(x, y, 1 - c), device_id_type=MESH))
        for cp in copies:
            cp.start()
        for r, (px, py, pc) in enumerate(peers):
            pltpu.make_async_remote_copy(
                src_ref=v_ref, dst_ref=buf.at[4 * px + 2 * py + pc], send_sem=v_send.at[r], recv_sem=v_recv.at[r],
                device_id=(px, py, pc), device_id_type=MESH).wait_recv()
        for k in range(n):
            half = fulls[k].shape[0] // 2
            theirs = outs[k].at[pl.ds((1 - c) * half, half), :]
            pltpu.make_async_remote_copy(
                src_ref=theirs, dst_ref=theirs, send_sem=h_send.at[k], recv_sem=h_recv.at[k],
                device_id=(x, y, 1 - c), device_id_type=MESH).wait_recv()
        for cp in copies:
            cp.wait_send()
        acc = buf[0]
        for d in range(1, n_dev):
            acc = acc + buf[d]
        out_ref[...] = acc
        out_ref[3:4, :] = jnp.broadcast_to(jnp.sum(acc[3:4, :], axis=1, keepdims=True), (1, cols))

    vm = pl.BlockSpec(memory_space=pltpu.VMEM)
    res = pl.pallas_call(
        body, name="final_exchange",
        in_specs=[vm] + _hbm_specs(n), out_specs=_hbm_specs(n) + [vm],
        out_shape=[_hbm_array(f.shape, F32) for f in fulls] + [jax.ShapeDtypeStruct((rows, cols), F32)],
        input_output_aliases={1 + k: k for k in range(n)},
        scratch_shapes=[pltpu.VMEM((n_dev, rows, cols), F32),
                        pltpu.SemaphoreType.DMA((n_dev - 1,)), pltpu.SemaphoreType.DMA((n_dev - 1,)),
                        pltpu.SemaphoreType.DMA((n,)), pltpu.SemaphoreType.DMA((n,))],
    )(v, *fulls)
    return res[:n], res[n]


def _adamw_math(w, g, m, v):
    m = ADAM_B1 * m + (1.0 - ADAM_B1) * g
    v = ADAM_B2 * v + (1.0 - ADAM_B2) * (g * g)
    m_hat = m / (1.0 - ADAM_B1 ** ADAM_STEP)
    v_hat = v / (1.0 - ADAM_B2 ** ADAM_STEP)
    delta = -ADAM_LR * (m_hat / (jnp.sqrt(v_hat) + ADAM_EPS) + ADAM_WD * w)
    return delta, m, v


def _adamw(w, g, m, v, name):
    rows, cols = w.shape
    tm = rows // 2 if (rows // 2) % 8 == 0 else rows
    return _rowwise(_adamw_math, [w, g, m, v], [], [(cols, F32)] * 3, [], tm=tm, name=name)


def _unshard_cols(gathered):
    n, r, c = gathered.shape
    return jnp.transpose(gathered, (1, 0, 2)).reshape(r, n * c)


def _shard_cols(full):
    r, nc = full.shape
    return jnp.transpose(full.reshape(r, N_CHIPS, nc // N_CHIPS), (1, 0, 2))


LATE = ["w_sb_up", "w_dil_up", "w_out", "w_ffn_in", "w_ffn_out"]


def _late_weights(slabs, d_model, d_ff):
    g = dict(zip(LATE, slabs))
    return (_unshard_cols(g["w_sb_up"]), _unshard_cols(g["w_dil_up"]), g["w_out"].reshape(d_model, d_model),
            _unshard_cols(g["w_ffn_in"]), g["w_ffn_out"].reshape(d_ff, d_model))


ROW_SHARDED = ("w_in", "w_out", "w_ffn_in", "w_ffn_out")


def _chip_major(grads):
    out = []
    for k, g in grads.items():
        if k in ROW_SHARDED:
            out.append(g.reshape(N_CHIPS, g.shape[0] // N_CHIPS, g.shape[1]))
        else:
            out.append(_shard_cols(g))
    return out


def _pair_sums(full, others, names):
    return [_pair_sum(g, o, "grad_pair_sum_" + k, k in SWAPPED) for g, o, k in zip(full, others, names)]


def _chip_sums(pair, landed, names):
    return {k: _chip_sum(p[0], l, "grad_chip_sum_" + k) for p, l, k in zip(pair, landed, names)}


def _fwd_bwd(x, loss_target, g_mix, g_ffn, g_fin, slab_in, late_slabs):
    b_sz, s_len, d_model = x.shape
    t = b_sz * s_len
    d_ff = late_slabs[-1].shape[1] * N_CHIPS
    x2d = x.reshape(t, d_model)
    tgt2d = loss_target.reshape(t, d_model)

    u, (slab_in,) = _rowwise(lambda xv, g: (_rms_stats(xv)[0] * g,), [x2d], [g_mix], [(d_model, BF16)], [], tm=512,
                             name="norm_mix", carry=_gather_carry([slab_in], ["w_in"]))
    wt_in = slab_in.reshape(-1, d_model)
    qkv, (slab_ffn_out,) = _mm(u, wt_in, tb=True, b_cols=(0, QKV_WIDTH), tm=2048, tn=768, tk=d_model, name="proj_qkv",
                               carry=_gather_carry(late_slabs[4:], LATE[4:]))
    gates = _mm(u, wt_in, tb=True, b_cols=(QKV_WIDTH, 2 * d_model), out_dtype=BF16, tm=t, tn=256, tk=d_model,
                name="proj_gates")
    qkv3 = qkv.reshape(b_sz, s_len, QKV_WIDTH)
    o_sb, (slab_ffn_in,) = _sb_fwd(qkv3, b_sz, s_len, _gather_carry(late_slabs[3:4], LATE[3:4]))
    o_dl, lse, small_slabs = _dil_fwd(qkv3, b_sz, s_len, _gather_carry(late_slabs[:3], LATE[:3]))
    wf_sb_up, wf_dil_up, wf_out, wf_ffn_in, wf_ffn_out = _late_weights(
        list(small_slabs) + [slab_ffn_in, slab_ffn_out], d_model, d_ff)
    o_sb2, o_dl2 = o_sb.reshape(t, SB_WIDTH), o_dl.reshape(t, DIL_OUT_WIDTH)
    y_sb = _mm(o_sb2, wf_sb_up, out_dtype=BF16, tm=1024, tn=1024, tk=SB_WIDTH, name="sb_up")
    y_dl = _mm(o_dl2, wf_dil_up, out_dtype=BF16, tm=1024, tn=1024, tk=DIL_OUT_WIDTH, name="dil_up")

    def merge_fn(gt, ys, yd):
        return (_sigmoid(gt[:, :d_model]) * ys + _sigmoid(gt[:, d_model:]) * yd,)

    (merged,) = _rowwise(merge_fn, [gates, y_sb, y_dl], [], [(d_model, BF16)], [], tm=512, name="merge")
    x1 = _mm(merged, wf_out, add=x2d, tm=512, tn=1024, tk=d_model, name="mix_out")
    (u2,) = _rowwise(lambda xv, g: (_rms_stats(xv)[0] * g,), [x1], [g_ffn], [(d_model, BF16)], [], tm=512, name="norm_ffn")
    half_ff = d_ff // 2

    def act_fn(hv):
        gate = hv[:, :half_ff]
        return hv, gate * _sigmoid(gate) * hv[:, half_ff:]

    h, act = _mm(u2, wf_ffn_in, tm=512, tn=d_ff, tk=d_model, name="ffn_in",
                 epilogue=(act_fn, [], [], [(d_ff, BF16), (half_ff, BF16)], []))
    def head_fn(xv, tg, g):
        xhat, r = _rms_stats(xv)
        err = xhat * g - tg
        dy = err * (1.0 / d_model)
        dx, dg_rows = _rms_bwd(dy, xhat, r, g)
        loss_lanes = (0.5 / d_model) * jnp.sum(err * err, axis=0, keepdims=True)
        return dx, dx, jnp.sum(dg_rows, axis=0, keepdims=True), loss_lanes

    dx2, dx2_b, dg_fin, loss_lanes = _mm(
        act, wf_ffn_out, add=x1, tm=512, tn=1024, tk=d_ff, name="ffn_out",
        epilogue=(head_fn, [tgt2d], [g_fin], [(d_model, F32), (d_model, BF16)], [(1, d_model), (1, d_model)]))

    def dact_fn(da, hv):
        gate, up = hv[:, :half_ff], hv[:, half_ff:]
        sg = _sigmoid(gate)
        dgate = da * up * (sg * (1.0 + gate * (1.0 - sg)))
        return (jnp.concatenate([dgate, da * (gate * sg)], axis=1),)

    (dh,) = _mm(dx2_b, wf_ffn_out, tb=True, tm=512, tn=half_ff, tk=d_model, name="ffn_out_dx",
                epilogue=(dact_fn, [h], [], [(d_ff, BF16)], []))
    gw_ffn_out = _mm(act, dx2_b, ta=True, tm=256, tn=d_model, tk=t, name="ffn_out_dw")
    def norm_bwd_fn(du_, dres, xv, g):
        xhat, r = _rms_stats(xv)
        dx, dg_rows = _rms_bwd(du_, xhat, r, g)
        return dres + dx, jnp.sum(dg_rows, axis=0, keepdims=True)

    def norm_bwd_twice(*args):
        dx, dg = norm_bwd_fn(*args)
        return dx, dx, dg

    dx1, dx1_b, dg_ffn = _mm(dh, wf_ffn_in, tb=True, tm=512, tn=1024, tk=2 * d_ff, name="ffn_in_dx",
                             epilogue=(norm_bwd_twice, [dx2, x1], [g_ffn], [(d_model, F32), (d_model, BF16)], [(1, d_model)]))
    gwt_ffn_in = _mm(dh, u2, ta=True, tm=512, tn=d_model, tk=t, name="ffn_in_dw")

    dmerged = _mm(dx1_b, wf_out, tb=True, out_dtype=BF16, tm=512, tn=1024, tk=d_model, name="mix_out_dx")
    gw_out = _mm(merged, dx1_b, ta=True, tm=256, tn=d_model, tk=t, name="mix_out_dw")

    def merge_bwd_fn(gt, ys, yd, dm):
        s_sb, s_dl = _sigmoid(gt[:, :d_model]), _sigmoid(gt[:, d_model:])
        dgates = jnp.concatenate([dm * ys * s_sb * (1.0 - s_sb), dm * yd * s_dl * (1.0 - s_dl)], axis=1)
        return dgates, dm * s_sb, dm * s_dl

    full_big = _chip_major({"w_out": gw_out, "w_ffn_in": gwt_ffn_in, "w_ffn_out": gw_ffn_out})
    dgates, dy_sb, dy_dl, others_big = _rowwise(
        merge_bwd_fn, [gates, y_sb, y_dl, dmerged], [], [(2 * d_model, BF16), (d_model, BF16), (d_model, BF16)], [],
        tm=256, name="merge_bwd", carry=_pair_carry(full_big))
    pair_big = _pair_sums(full_big, others_big, LATE[2:])
    do_sb = _mm(dy_sb, wf_sb_up, tb=True, out_dtype=BF16, tm=1024, tn=SB_WIDTH, tk=d_model, name="sb_up_dx")
    gw_sb_up = _mm(o_sb2, dy_sb, ta=True, tm=SB_WIDTH, tn=1024, tk=512, name="sb_up_dw")
    do_dl = _mm(dy_dl, wf_dil_up, tb=True, tm=1024, tn=DIL_OUT_WIDTH, tk=d_model, name="dil_up_dx")
    gw_dil_up = _mm(o_dl2, dy_dl, ta=True, tm=DIL_OUT_WIDTH, tn=1024, tk=512, name="dil_up_dw")
    full_small = _chip_major({"w_sb_up": gw_sb_up, "w_dil_up": gw_dil_up})
    (dq_sb, dk_sb, dv_sb), brought = _sb_bwd(
        qkv3, o_sb, do_sb.reshape(b_sz, s_len, SB_WIDTH), b_sz, s_len,
        _chip_carry([p[1] for p in pair_big[:2]], LATE[2:4]) + _pair_carry(full_small))
    pair_small = _pair_sums(full_small, brought[2:], LATE[:2])
    (dq_dl, dk_dl, dv_dl), landed_b = _dil_bwd(
        qkv3, o_dl, lse, do_dl.reshape(b_sz, s_len, DIL_OUT_WIDTH), b_sz, s_len,
        _chip_carry([pair_big[2][1], pair_small[0][1], pair_small[1][1]], [LATE[4], LATE[0], LATE[1]]))
    pair = pair_small + pair_big
    landed = [landed_b[1], landed_b[2], brought[0], brought[1], landed_b[0]]
    dproj = [a.reshape(t, -1) for a in (dq_sb, dk_sb, dv_sb, dq_dl, dk_dl, dv_dl)] + [dgates]
    gwt_in, gwt_in_b = _mm(dproj, u, ta=True, tm=256, tn=d_model, tk=t, name="proj_dw",
                           epilogue=(lambda tile: (tile, tile), [], [], [(d_model, F32), (d_model, BF16)], []))
    full_in = _chip_major({"w_in": gwt_in})
    pair_in = _pair_sums(full_in, _pair_exchange(_chip_major({"w_in": gwt_in_b}), "w_in"), ["w_in"])
    (dx, dg_mix), landed_in = _mm(
        dproj, wt_in, tm=512, tn=1024, tk=wt_in.shape[0], name="proj_dx",
        carry=_chip_carry([p[1] for p in pair_in], ["w_in"]),
        epilogue=(norm_bwd_fn, [dx1, x2d], [g_mix], [(d_model, F32)], [(1, d_model)]))

    grads = _chip_sums(pair, landed, LATE)
    grads.update(_chip_sums(pair_in, landed_in, ["w_in"]))
    return dx, grads, dg_mix, dg_ffn, dg_fin, loss_lanes


def kernel(x, norm_mix_g, w_in, w_sb_up, w_dil_up, w_out, norm_ffn_g, w_ffn_in, w_ffn_out, norm_final_g, loss_target, m_norm_mix_g, m_w_in, m_w_sb_up, m_w_dil_up, m_w_out, m_norm_ffn_g, m_w_ffn_in, m_w_ffn_out, m_norm_final_g, v_norm_mix_g, v_w_in, v_w_sb_up, v_w_dil_up, v_w_out, v_norm_ffn_g, v_w_ffn_in, v_w_ffn_out, v_norm_final_g):
    b_sz, s_len, d_model = x.shape
    d_ff = w_ffn_out.shape[1] * N_CHIPS
    g_mix, g_ffn, g_fin = norm_mix_g, norm_ffn_g, norm_final_g.reshape(1, d_model)

    names = ["w_in", "w_sb_up", "w_dil_up", "w_out", "w_ffn_in", "w_ffn_out"]
    shards = {"w_in": jnp.swapaxes(w_in[0], 0, 1), "w_sb_up": w_sb_up[0], "w_dil_up": w_dil_up[0], "w_out": w_out[0],
              "w_ffn_in": w_ffn_in[0], "w_ffn_out": w_ffn_out[0]}
    slab_in = _cast_to_slab(shards["w_in"], "cast_w_in")
    late_slabs = [_cast_to_slab(shards[k], "cast_" + k, k in SWAPPED) for k in LATE]

    dx, grads, dg_mix, dg_ffn, dg_fin, loss_lanes = _fwd_bwd(
        x, loss_target, g_mix, g_ffn, g_fin, slab_in, late_slabs)

    small = jnp.concatenate([dg_mix, dg_ffn, dg_fin, loss_lanes, jnp.zeros((4, d_model), F32)], axis=0)
    full_grads, small = _final_exchange([grads[k] for k in names], small)
    grads = dict(zip(names, full_grads))
    grads["w_ffn_in"] = jnp.swapaxes(grads["w_ffn_in"], 0, 1)
    loss = small[3, 0]
    gains = jnp.concatenate([g_mix, g_ffn, g_fin, jnp.zeros((5, d_model), F32)], axis=0)
    gains_m = jnp.concatenate([m_norm_mix_g, m_norm_ffn_g, m_norm_final_g.reshape(1, d_model), jnp.zeros((5, d_model), F32)], axis=0)
    gains_v = jnp.concatenate([v_norm_mix_g, v_norm_ffn_g, v_norm_final_g.reshape(1, d_model), jnp.ones((5, d_model), F32)], axis=0)
    gd, gm, gv = _rowwise(_adamw_math, [gains, small, gains_m, gains_v], [], [(d_model, F32)] * 3, [], tm=8, name="adamw_gains")

    moments = {"w_in": (jnp.swapaxes(m_w_in[0], 0, 1), jnp.swapaxes(v_w_in[0], 0, 1)),
               "w_sb_up": (m_w_sb_up[0], v_w_sb_up[0]), "w_dil_up": (m_w_dil_up[0], v_w_dil_up[0]),
               "w_out": (m_w_out[0], v_w_out[0]), "w_ffn_in": (m_w_ffn_in[0], v_w_ffn_in[0]),
               "w_ffn_out": (m_w_ffn_out[0], v_w_ffn_out[0])}
    upd = {k: _adamw(shards[k], grads[k], moments[k][0], moments[k][1], "adamw_" + k) for k in names}

    def as_output(k, a):
        return (jnp.swapaxes(a, 0, 1) if k == "w_in" else a)[None]

    def w_out_of(i):
        return [as_output(k, upd[k][i]) for k in names]

    def ordered(mix, ws, ffn_g, fin):
        return [mix, ws[0], ws[1], ws[2], ws[3], ffn_g, ws[4], ws[5], fin]

    grad_ws = [as_output(k, grads[k]) for k in names]
    outs = [loss, dx.reshape(b_sz, s_len, d_model)]
    outs += ordered(small[0:1], grad_ws, small[1:2], small[2])
    outs += ordered(gd[0:1], w_out_of(0), gd[1:2], gd[2])
    outs += ordered(gm[0:1], w_out_of(1), gm[1:2], gm[2])
    outs += ordered(gv[0:1], w_out_of(2), gv[1:2], gv[2])
    return tuple(outs)
```

```python
import functools
import math

import jax
import jax.numpy as jnp
from jax import lax
from jax.experimental import pallas as pl
from jax.experimental.pallas import tpu as pltpu

F32 = jnp.float32
BF16 = jnp.bfloat16
MESH = pl.DeviceIdType.MESH

HEAD_DIM = 64
SB_HEADS = 8
DIL_PAIRS = ((128, 1), (512, 4), (2048, 16))
DIL_HEADS_PER_GROUP = 4
DIL_HEADS = DIL_HEADS_PER_GROUP * len(DIL_PAIRS)
SB_WIDTH = SB_HEADS * HEAD_DIM
DIL_WIDTH = DIL_HEADS * HEAD_DIM
DIL_OUT_WIDTH = DIL_HEADS_PER_GROUP * HEAD_DIM
QKV_WIDTH = 3 * SB_WIDTH + 3 * DIL_WIDTH
RMS_EPS = 1e-6
ALIBI_MAX_BIAS = 8.0
ADAM_LR = 0.001
ADAM_B1 = 0.9
ADAM_B2 = 0.999
ADAM_EPS = 1e-08
ADAM_WD = 0.01
ADAM_STEP = 10

LANES = 128
BLK = 128
NEG = -1e30
EXP_UNDERFLOW = -104.0
SB_FWD_CHAINS = 4
SB_BWD_CHAINS = 4
DIL_CHAINS = 4
N_CHIPS = 4
VMEM_CAP = 56 * 1024 * 1024
SMALL_OPERAND_BYTES = 64 * 1024


def _vmem_limit(tile_bytes):
    return int(min(VMEM_CAP, max(32 * 1024 * 1024, 3 * tile_bytes + 8 * 1024 * 1024)))


def _hbm_array(shape, dtype):
    return pltpu.HBM(shape, dtype)


def _nbytes(shape, dtype):
    return math.prod(shape) * jnp.dtype(dtype).itemsize


def _pallas_call(body, **kwargs):
    call = pl.pallas_call(body, **kwargs)

    def run(*operands):
        return call(*[pltpu.with_memory_space_constraint(x, pltpu.HBM)
                      if _nbytes(x.shape, x.dtype) >= SMALL_OPERAND_BYTES else x for x in operands])

    return run


def _dot(a, b):
    return jnp.dot(a, b, preferred_element_type=F32)


def _dot_nt(a, b):
    return lax.dot_general(a, b, (((1,), (1,)), ((), ())), preferred_element_type=F32)


def _dot_tn(a, b):
    return lax.dot_general(a, b, (((0,), (0,)), ((), ())), preferred_element_type=F32)


def _split2(x):
    hi = x.astype(BF16)
    lo = (x - hi.astype(F32)).astype(BF16)
    return hi, lo


def _sigmoid(x):
    return pl.reciprocal(1.0 + jnp.exp(-x), approx=True)


class _Carry:
    def __init__(self, arrays=(), out_shapes=(), aliased=False, sems=(), start=None, finish=None):
        self.arrays, self.out_shapes, self.aliased = list(arrays), list(out_shapes), aliased
        self.sems, self.start, self.finish = list(sems), start, finish

    def __bool__(self):
        return bool(self.arrays)

    def __add__(self, other):
        assert not self.aliased and not other.aliased
        n_a, n_o, n_s = len(self.arrays), len(self.out_shapes), len(self.sems)
        return _Carry(
            self.arrays + other.arrays, self.out_shapes + other.out_shapes, False, self.sems + other.sems,
            lambda i, o, s: (self.start(i[:n_a], o[:n_o], s[:n_s]), other.start(i[n_a:], o[n_o:], s[n_s:])),
            lambda i, o, s: (self.finish(i[:n_a], o[:n_o], s[:n_s]), other.finish(i[n_a:], o[n_o:], s[n_s:])))

    def call_args(self, n_in, n_out):
        aliases = {n_in + k: n_out + k for k in range(len(self.arrays))} if self.aliased else {}
        return _hbm_specs(len(self.arrays)), _hbm_specs(len(self.out_shapes)), self.out_shapes, aliases, self.sems

    def run(self, refs, n_in, n_out, step, n_steps, compute):
        if not self:
            compute()
            return
        n_c, n_o, n_s = len(self.arrays), len(self.out_shapes), len(self.sems)
        ins = refs[n_in:n_in + n_c]
        outs = refs[n_in + n_c + n_out:n_in + n_c + n_out + n_o]
        sems = refs[len(refs) - n_s:]

        @pl.when(step == 0)
        def _():
            self.start(ins, outs, sems)

        compute()

        @pl.when(step == n_steps - 1)
        def _():
            self.finish(ins, outs, sems)


def _mm(a, b, *, ta=False, tb=False, add=None, out_dtype=F32, tm, tn, tk, name, carry=None, epilogue=None,
        b_cols=None):
    carry = carry or _Carry()
    n_car = len(carry.arrays)
    pieces = list(a) if isinstance(a, (list, tuple)) else [a]
    n_a = len(pieces)
    widths = [p.shape[1] for p in pieces]
    starts = [sum(widths[:p]) for p in range(n_a)]
    if ta:
        kdim, m = pieces[0].shape[0], sum(widths)
    else:
        m, kdim = pieces[0].shape[0], sum(widths)
    if tb:
        n, k2 = b.shape
    else:
        k2, n = b.shape
    col0 = 0
    if b_cols is not None:
        assert b_cols[0] % tn == 0, name
        col0, n = b_cols[0] // tn, b_cols[1]
    assert kdim == k2 and m % tm == 0 and n % tn == 0 and kdim % tk == 0, (name, a.shape, b.shape)
    nk = kdim // tk
    assert n_a == 1 or (nk == 1 and not tb and (not ta or all(w % tm == 0 for w in widths))), name
    grid = (m // tm, n // tn, nk)
    a_mode = dict(pipeline_mode=pl.Buffered(1)) if grid[0] == 1 and nk == 1 else {}
    b_mode = dict(pipeline_mode=pl.Buffered(1)) if grid[1] == 1 and nk == 1 else {}
    if n_a == 1:
        a_specs = [pl.BlockSpec((tk, tm), lambda i, j, k: (k, i), **a_mode) if ta
                   else pl.BlockSpec((tm, tk), lambda i, j, k: (i, k), **a_mode)]
    elif ta:
        a_specs = [pl.BlockSpec((tk, tm), lambda i, j, k, s=s // tm, w=w // tm: (0, jnp.clip(i - s, 0, w - 1)))
                   for s, w in zip(starts, widths)]
    else:
        a_specs = [pl.BlockSpec((tm, w), lambda i, j, k: (i, 0)) for w in widths]
    b_spec = (pl.BlockSpec((tn, tk), lambda i, j, k: (j + col0, k), **b_mode) if tb
              else pl.BlockSpec((tk, tn), lambda i, j, k: (k, j + col0), **b_mode))
    o_spec = pl.BlockSpec((tm, tn), lambda i, j, k: (i, j))
    dims = ((((0,) if ta else (1,)), ((1,) if tb else (0,))), ((), ()))
    has_add = add is not None
    if epilogue is None:
        ep_fn, ep_rows, ep_params, ep_outs, ep_accs = None, [], [], [], []
        out_sds, out_specs = [_hbm_array((m, n), out_dtype)], [o_spec]
    else:
        ep_fn, ep_rows, ep_params, ep_outs, ep_accs = epilogue
        assert grid[1] == 1 or not ep_accs, name
        out_sds = [_hbm_array((m, w * grid[1]), d) for w, d in ep_outs]
        out_sds += [_hbm_array(sh, F32) for sh in ep_accs]
        out_specs = [pl.BlockSpec((tm, w), lambda i, j, k: (i, j)) for w, _ in ep_outs]
        out_specs += [pl.BlockSpec(sh, lambda i, j, k: (0, 0)) for sh in ep_accs]
    n_main = len(out_sds)
    use_scratch = nk > 1 and (ep_fn is not None or jnp.dtype(out_dtype) != jnp.dtype(F32))
    n_in = n_a + 1 + has_add + len(ep_rows) + len(ep_params)

    def finish(total, refs, pid):
        outs = refs[n_in + n_car:n_in + n_car + n_main]
        if ep_fn is None:
            outs[0][...] = total.astype(out_dtype)
            return
        first = n_a + 1 + has_add
        rows = [r[...].astype(F32) for r in refs[first:first + len(ep_rows)]]
        params = [p[...] for p in refs[first + len(ep_rows):n_in]]
        res = ep_fn(total, *rows, *params)
        for o_ref, v in zip(outs[:len(ep_outs)], res):
            o_ref[...] = v.astype(o_ref.dtype)
        acc_refs = outs[len(ep_outs):]
        if acc_refs:
            @pl.when(pid[0] == 0)
            def _():
                for r in acc_refs:
                    r[...] = jnp.zeros(r.shape, F32)

            for r, v in zip(acc_refs, res[len(ep_outs):]):
                r[...] += v

    def compute(refs, pid):
        a_ref, b_ref = refs[0], refs[n_a]
        add_ref = refs[n_a + 1] if has_add else None

        def dot(x, y):
            return lax.dot_general(x.astype(BF16), y.astype(BF16), dims, preferred_element_type=F32)

        if n_a > 1 and ta:
            for p_ref, s, w in zip(refs[:n_a], starts, widths):
                @pl.when((pid[0] >= s // tm) & (pid[0] < (s + w) // tm))
                def _(p_ref=p_ref):
                    prod = dot(p_ref[...], b_ref[...])
                    finish(prod + add_ref[...] if has_add else prod, refs, pid)
            return
        if n_a > 1:
            prod = dot(a_ref[...], b_ref[:widths[0], :])
            for p_ref, s, w in zip(refs[1:n_a], starts[1:], widths[1:]):
                prod += dot(p_ref[...], b_ref[s:s + w, :])
        else:
            prod = dot(a_ref[...], b_ref[...])
        if nk == 1:
            finish(prod + add_ref[...] if has_add else prod, refs, pid)
            return
        acc_ref = refs[n_in + n_car + n_main + len(carry.out_shapes)] if use_scratch else refs[n_in + n_car]
        k = pid[2]

        @pl.when(k == 0)
        def _():
            acc_ref[...] = prod + add_ref[...] if has_add else prod

        @pl.when(k > 0)
        def _():
            acc_ref[...] += prod

        if use_scratch:
            @pl.when(k == nk - 1)
            def _():
                finish(acc_ref[...], refs, pid)

    def body(*refs):
        pid = (pl.program_id(0), pl.program_id(1), pl.program_id(2))
        step = (pid[0] * grid[1] + pid[1]) * nk + pid[2]
        carry.run(refs, n_in, n_main, step, grid[0] * grid[1] * nk, lambda: compute(refs, pid))

    tile_bytes = ((n_a if ta else 1) * _nbytes((tm, tk), pieces[0].dtype)
                  + _nbytes((tk, tn), b.dtype) + 2 * _nbytes((tm, tn), F32)
                  + (_nbytes((tm, tn), F32) if has_add else 0)
                  + sum(_nbytes((tm, r.shape[1]), r.dtype) for r in ep_rows) + sum(_nbytes((tm, w), d) for w, d in ep_outs))
    in_specs = a_specs + [b_spec] + ([o_spec] if has_add else [])
    in_specs += [pl.BlockSpec((tm, r.shape[1] // grid[1]), lambda i, j, k: (i, j)) for r in ep_rows]
    in_specs += [pl.BlockSpec(p.shape, lambda i, j, k: (0, 0)) for p in ep_params]
    args = tuple(pieces) + (b,) + ((add,) if has_add else ()) + tuple(ep_rows) + tuple(ep_params)
    scratch = [pltpu.VMEM((tm, tn), F32)] if use_scratch else []
    serial = bool(carry) or bool(ep_accs)
    c_in, c_out, c_shapes, c_alias, c_sems = carry.call_args(n_in, n_main)
    res = _pallas_call(
        body, name=name, grid=grid,
        in_specs=in_specs + c_in, out_specs=out_specs + c_out, out_shape=out_sds + c_shapes,
        input_output_aliases=c_alias, scratch_shapes=scratch + c_sems,
        compiler_params=pltpu.CompilerParams(
            dimension_semantics=("arbitrary",) * 3 if serial else ("parallel", "parallel", "arbitrary"),
            vmem_limit_bytes=_vmem_limit(tile_bytes)),
    )(*args, *carry.arrays)
    main = res[0] if ep_fn is None else list(res[:n_main])
    return (main, res[n_main:]) if carry else main


def _rowwise(fn, rows, params, outs, accs, *, tm, name, carry=None):
    carry = carry or _Carry()
    t = rows[0].shape[0]
    assert t % tm == 0, (name, t, tm)
    n_r, n_p, n_o, n_c = len(rows), len(params), len(outs), len(carry.arrays)

    def compute(refs, first):
        vals = [r[...].astype(F32) for r in refs[:n_r]] + [p[...] for p in refs[n_r:n_r + n_p]]
        res = fn(*vals)
        o_refs = refs[n_r + n_p + n_c:n_r + n_p + n_c + n_o]
        a_refs = refs[n_r + n_p + n_c + n_o:n_r + n_p + n_c + n_o + len(accs)]
        for o_ref, v in zip(o_refs, res[:n_o]):
            o_ref[...] = v.astype(o_ref.dtype)
        if accs:
            @pl.when(first)
            def _():
                for a_ref in a_refs:
                    a_ref[...] = jnp.zeros(a_ref.shape, F32)

            for a_ref, v in zip(a_refs, res[n_o:]):
                a_ref[...] += v

    def body(*refs):
        step = pl.program_id(0)
        carry.run(refs, n_r + n_p, n_o + len(accs), step, t // tm, lambda: compute(refs, step == 0))

    in_specs = [pl.BlockSpec((tm, r.shape[1]), lambda i: (i, 0)) for r in rows]
    in_specs += [pl.BlockSpec(p.shape, lambda i: (0, 0)) for p in params]
    out_specs = [pl.BlockSpec((tm, w), lambda i: (i, 0)) for w, _ in outs]
    out_specs += [pl.BlockSpec(s, lambda i: (0, 0)) for s in accs]
    out_shape = [_hbm_array((t, w), d) for w, d in outs]
    out_shape += [_hbm_array(s, F32) for s in accs]
    tile_bytes = sum(_nbytes((tm, r.shape[1]), r.dtype) for r in rows) + sum(_nbytes((tm, w), F32) for w, _ in outs)
    c_in, c_out, c_shapes, c_alias, c_sems = carry.call_args(n_r + n_p, n_o + len(accs))
    res = _pallas_call(
        body, name=name, grid=(t // tm,), in_specs=in_specs + c_in, out_specs=out_specs + c_out,
        out_shape=out_shape + c_shapes, input_output_aliases=c_alias, scratch_shapes=c_sems,
        compiler_params=pltpu.CompilerParams(
            dimension_semantics=("arbitrary",) if accs or carry else ("parallel",),
            vmem_limit_bytes=_vmem_limit(2 * tile_bytes)),
    )(*rows, *params, *carry.arrays)
    own = n_o + len(accs)
    return (list(res[:own]) + [res[own:]]) if carry else res


def _rms_stats(x):
    r = lax.rsqrt(jnp.mean(x * x, axis=-1, keepdims=True) + RMS_EPS)
    return x * r, r


def _rms_bwd(dy, xhat, r, g):
    dxhat = dy * g
    dx = r * (dxhat - xhat * jnp.mean(dxhat * xhat, axis=-1, keepdims=True))
    return dx, dy * xhat


def _sb_consts():
    lane = lax.broadcasted_iota(jnp.int32, (BLK, LANES), 1)
    head0 = lane < HEAD_DIM
    row = lax.broadcasted_iota(jnp.int32, (2 * BLK, BLK), 0) % BLK
    col = lax.broadcasted_iota(jnp.int32, (2 * BLK, BLK), 1)
    causal = col < row
    jj = lax.broadcasted_iota(jnp.int32, (BLK, BLK), 0)
    ss = lax.broadcasted_iota(jnp.int32, (BLK, BLK), 1)
    suffix = jnp.where(jj > ss, 1.0, 0.0).astype(BF16)
    return head0, causal, suffix


def _stack_heads(x, head0):
    zero = jnp.zeros_like(x)
    return jnp.concatenate([jnp.where(head0, x, zero), jnp.where(head0, zero, x)], axis=0)


def _sb_logits(z, causal, masked):
    sp = jnp.log(1.0 + jnp.exp(-jnp.abs(z)))
    log_keep = -(jnp.maximum(z, 0.0) + sp)
    log_beta = jnp.minimum(z, 0.0) - sp
    if masked:
        log_keep = jnp.where(causal, log_keep, 0.0)
    return log_keep, log_beta


def _suffix_sums(x, suffix):
    hi, lo = _split2(x)
    after = _dot(hi, suffix) + _dot(lo, suffix)
    total = jnp.broadcast_to(after[:, 0:1] + x[:, 0:1], x.shape)
    return after, total


def _sb_walk_back(i, state, per_chain, tile):
    def alive(st):
        worst = functools.reduce(jnp.maximum, [st[p][:, 0:1] for p in range(0, len(st), per_chain)])
        return jnp.max(worst) > EXP_UNDERFLOW

    def cond(c):
        return jnp.logical_and(c[0] < i, alive(c[1]))

    def body(c):
        return c[0] + 1, tile(i - 1 - c[0], c[1], False)

    return lax.while_loop(cond, body, (jnp.int32(0), state))[1]


def _lane_blocks(x, n):
    return [x[:, p * LANES:(p + 1) * LANES] for p in range(n)]


def _sb_fwd(qkv, b_sz, s_len, carry):
    nq = s_len // BLK
    n_pairs = SB_WIDTH // LANES
    ch = SB_FWD_CHAINS
    n_steps = n_pairs // ch
    scale = 1.0 / math.sqrt(HEAD_DIM)

    def compute(q_ref, k_ref, v_ref, o_ref):
        head0, causal, suffix = _sb_consts()

        def q_block(i, _):
            qs = pl.multiple_of(i * BLK, BLK)
            q_all = (q_ref[pl.ds(qs, BLK), :] * scale).astype(BF16)
            q01 = [_stack_heads(q, head0) for q in _lane_blocks(q_all, ch)]

            def tile(j, state, masked):
                ks = pl.multiple_of(j * BLK, BLK)
                ks_ = _lane_blocks(k_ref[pl.ds(ks, BLK), :].astype(BF16), ch)
                vs_ = _lane_blocks(v_ref[pl.ds(ks, BLK), :].astype(BF16), ch)
                zs = [_dot_nt(q01[p], ks_[p]) for p in range(ch)]
                logits = [_sb_logits(z, causal, masked) for z in zs]
                sums = [_suffix_sums(lg[0], suffix) for lg in logits]
                out = []
                for p in range(ch):
                    carry, acc = state[2 * p], state[2 * p + 1]
                    after, total = sums[p]
                    a = jnp.exp(logits[p][1] + carry + after)
                    if masked:
                        a = jnp.where(causal, a, 0.0)
                    a_hi, a_lo = _split2(a)
                    a_cat = jnp.concatenate([a_hi[:BLK], a_hi[BLK:], a_lo[:BLK], a_lo[BLK:]], axis=1)
                    v01 = _stack_heads(vs_[p], head0)
                    out += [carry + total, acc + _dot(a_cat, jnp.concatenate([v01, v01], axis=0))]
                return tuple(out)

            state = (jnp.zeros((2 * BLK, BLK), F32), jnp.zeros((BLK, LANES), F32)) * ch
            state = tile(i, state, True)
            state = _sb_walk_back(i, state, 2, tile)
            o_ref[pl.ds(qs, BLK), :] = jnp.concatenate([state[2 * p + 1] for p in range(ch)], axis=1)
            return 0

        lax.fori_loop(0, nq, q_block, 0)

    def body(*refs):
        step = pl.program_id(0) * n_steps + pl.program_id(1)
        o_ref = refs[3 + len(carry.arrays)]
        carry.run(refs, 3, 1, step, b_sz * n_steps, lambda: compute(refs[0], refs[1], refs[2], o_ref))

    blk = lambda off: pl.BlockSpec((None, s_len, ch * LANES), lambda b, p: (b, 0, off + p))
    c_in, c_out, c_shapes, c_alias, c_sems = carry.call_args(3, 1)
    res = _pallas_call(
        body, name="sb_fwd", grid=(b_sz, n_steps),
        in_specs=[blk(0), blk(n_steps), blk(2 * n_steps)] + c_in, out_specs=[blk(0)] + c_out,
        out_shape=[_hbm_array((b_sz, s_len, SB_WIDTH), F32)] + c_shapes,
        input_output_aliases=c_alias, scratch_shapes=c_sems,
        compiler_params=pltpu.CompilerParams(dimension_semantics=("arbitrary", "arbitrary"),
                                             vmem_limit_bytes=VMEM_CAP),
    )(qkv, qkv, qkv, *carry.arrays)
    return res[0], res[1:]


def _sb_bwd(qkv, o_sb, do_sb, b_sz, s_len, carry):
    nq = s_len // BLK
    n_pairs = SB_WIDTH // LANES
    ch = SB_BWD_CHAINS
    n_steps = n_pairs // ch
    scale = 1.0 / math.sqrt(HEAD_DIM)

    def compute(q_ref, k_ref, v_ref, o_ref, do_ref, dq_ref, dk_ref, dv_ref, dk_acc, dv_acc):
        head0, causal, suffix = _sb_consts()
        lrow = lax.broadcasted_iota(jnp.int32, (LANES, LANES), 0)
        ones_h0 = jnp.where(lrow < HEAD_DIM, 1.0, 0.0).astype(BF16)
        ones_h1 = jnp.where(lrow >= HEAD_DIM, 1.0, 0.0).astype(BF16)
        dk_acc[...] = jnp.zeros(dk_acc.shape, F32)
        dv_acc[...] = jnp.zeros(dv_acc.shape, F32)

        def q_block(i, _):
            qs = pl.multiple_of(i * BLK, BLK)
            q_all = (q_ref[pl.ds(qs, BLK), :] * scale).astype(BF16)
            do_all = do_ref[pl.ds(qs, BLK), :].astype(BF16)
            dd_all = do_all.astype(F32) * o_ref[pl.ds(qs, BLK), :]
            q01 = [_stack_heads(q, head0) for q in _lane_blocks(q_all, ch)]
            do01 = [_stack_heads(d, head0) for d in _lane_blocks(do_all, ch)]
            tot = []
            for dd in _lane_blocks(dd_all, ch):
                dd_hi, dd_lo = _split2(dd)
                tot.append(jnp.concatenate([_dot(dd_hi, ones_h0) + _dot(dd_lo, ones_h0),
                                            _dot(dd_hi, ones_h1) + _dot(dd_lo, ones_h1)], axis=0))

            def tile(j, state, masked):
                ks = pl.multiple_of(j * BLK, BLK)
                ks_ = _lane_blocks(k_ref[pl.ds(ks, BLK), :].astype(BF16), ch)
                vs_ = _lane_blocks(v_ref[pl.ds(ks, BLK), :].astype(BF16), ch)
                zs = [_dot_nt(q01[p], ks_[p]) for p in range(ch)]
                das = [_dot_nt(do01[p], vs_[p]) for p in range(ch)]
                logits = [_sb_logits(z, causal, masked) for z in zs]
                sums = [_suffix_sums(lg[0], suffix) for lg in logits]
                a_s, e_s = [], []
                for p in range(ch):
                    a = jnp.exp(logits[p][1] + state[3 * p] + sums[p][0])
                    if masked:
                        a = jnp.where(causal, a, 0.0)
                    a_s.append(a)
                    e_s.append(a * das[p])
                e_sums = [_suffix_sums(e, suffix) for e in e_s]
                out, dks, dvs = [], [], []
                for p in range(ch):
                    carry, rcarry, dq = state[3 * p:3 * p + 3]
                    e = e_s[p]
                    before = tot[p] - (rcarry + e_sums[p][0] + e)
                    beta = jnp.exp(logits[p][1])
                    dz = e * (1.0 - beta) - beta * before
                    if masked:
                        dz = jnp.where(causal, dz, 0.0)
                    dz_b = dz.astype(BF16)
                    dks.append(_dot_tn(dz_b, q01[p]))
                    dvs.append(_dot_tn(a_s[p].astype(BF16), do01[p]))
                    out += [carry + sums[p][1], rcarry + e_sums[p][1], dq + _dot(dz_b, ks_[p])]
                dk_acc[pl.ds(ks, BLK), :] += jnp.concatenate(dks, axis=1)
                dv_acc[pl.ds(ks, BLK), :] += jnp.concatenate(dvs, axis=1)
                return tuple(out)

            state = (jnp.zeros((2 * BLK, BLK), F32),) * (3 * ch)
            state = tile(i, state, True)
            state = _sb_walk_back(i, state, 3, tile)
            dq = [jnp.where(head0, state[3 * p + 2][:BLK], state[3 * p + 2][BLK:]) for p in range(ch)]
            dq_ref[pl.ds(qs, BLK), :] = (jnp.concatenate(dq, axis=1) * scale).astype(dq_ref.dtype)
            return 0

        lax.fori_loop(0, nq, q_block, 0)
        dk_ref[...] = dk_acc[...].astype(dk_ref.dtype)
        dv_ref[...] = dv_acc[...].astype(dv_ref.dtype)

    def body(*refs):
        step = pl.program_id(0) * n_steps + pl.program_id(1)
        n_c, n_o = len(carry.arrays), len(carry.out_shapes)
        own = refs[:5] + refs[5 + n_c:8 + n_c] + refs[8 + n_c + n_o:10 + n_c + n_o]
        carry.run(refs, 5, 3, step, b_sz * n_steps, lambda: compute(*own))

    blk = lambda off: pl.BlockSpec((None, s_len, ch * LANES), lambda b, p: (b, 0, off + p))
    once = lambda off: pl.BlockSpec((None, s_len, ch * LANES), lambda b, p: (b, 0, off + p),
                                    pipeline_mode=pl.Buffered(1))
    out_sd = _hbm_array((b_sz, s_len, SB_WIDTH), BF16)
    c_in, c_out, c_shapes, c_alias, c_sems = carry.call_args(5, 3)
    res = _pallas_call(
        body, name="sb_bwd", grid=(b_sz, n_steps),
        in_specs=[once(0), once(n_steps), once(2 * n_steps), once(0), once(0)] + c_in,
        out_specs=[blk(0), blk(0), blk(0)] + c_out, out_shape=[out_sd, out_sd, out_sd] + c_shapes,
        input_output_aliases=c_alias,
        scratch_shapes=[pltpu.VMEM((s_len, ch * LANES), F32), pltpu.VMEM((s_len, ch * LANES), F32)] + c_sems,
        compiler_params=pltpu.CompilerParams(dimension_semantics=("arbitrary", "arbitrary"),
                                             vmem_limit_bytes=VMEM_CAP),
    )(qkv, qkv, qkv, o_sb, do_sb, *carry.arrays)
    return res[:3], res[3:]


def _dil_consts(group, pair_idx, dilation):
    lane = lax.broadcasted_iota(jnp.int32, (BLK, LANES), 1)
    head0 = lane < HEAD_DIM
    row = lax.broadcasted_iota(jnp.int32, (2 * BLK, BLK), 0)
    qa = row % BLK
    kb = lax.broadcasted_iota(jnp.int32, (2 * BLK, BLK), 1)
    head = (group * DIL_HEADS_PER_GROUP + 2 * pair_idx + row // BLK).astype(F32)
    slope = jnp.exp((-ALIBI_MAX_BIAS * math.log(2.0) / DIL_HEADS) * (head + 1.0))
    valid_cur = kb <= qa
    valid_prev = kb >= qa
    bias_cur = -slope * ((qa - kb) * dilation).astype(F32)
    bias_prev = -slope * ((BLK + qa - kb) * dilation).astype(F32)
    return head0, valid_cur, valid_prev, bias_cur, bias_prev


def _dil_units(s_len, dilation):
    nb = s_len // dilation // BLK
    return [(r, n) for r in range(dilation) for n in range(nb)]


def _dil_rows(n, r, dilation):
    if dilation == 1:
        return pl.ds(n * BLK, BLK)
    return pl.ds(n * BLK * dilation + r, BLK, stride=dilation)


def _dil_scores(q01, k, bias, valid):
    s = _dot_nt(q01, k) * (1.0 / math.sqrt(HEAD_DIM)) + bias
    return jnp.where(valid, s, NEG)


def _dil_fwd(qkv, b_sz, s_len, carry):
    n_pairs = DIL_OUT_WIDTH // LANES
    q_off = 3 * SB_WIDTH // LANES
    per_kind = DIL_WIDTH // LANES

    def compute(pair_idx, qkv_refs, o_ref, lse_ref, m_s, l_s):
        m_s[...] = jnp.full(m_s.shape, NEG, F32)
        l_s[...] = jnp.zeros(l_s.shape, F32)
        o_ref[...] = jnp.zeros(o_ref.shape, F32)
        for g, (_, dilation) in enumerate(DIL_PAIRS):
            q_ref, k_ref, v_ref = qkv_refs[3 * g:3 * g + 3]
            head0, valid_cur, valid_prev, bias_cur, bias_prev = _dil_consts(g, pair_idx, dilation)
            units = _dil_units(s_len, dilation)
            for u0 in range(0, len(units), DIL_CHAINS):
                group = units[u0:u0 + DIL_CHAINS]
                rows_of = [_dil_rows(n, r, dilation) for r, n in group]
                scores, values = [], []
                for (r, n), rows in zip(group, rows_of):
                    q01 = _stack_heads(q_ref[rows, :].astype(BF16), head0)
                    sc = [_dil_scores(q01, k_ref[rows, :].astype(BF16), bias_cur, valid_cur)]
                    vals = [_stack_heads(v_ref[rows, :].astype(BF16), head0)]
                    if n > 0:
                        prev = _dil_rows(n - 1, r, dilation)
                        sc.append(_dil_scores(q01, k_ref[prev, :].astype(BF16), bias_prev, valid_prev))
                        vals.append(_stack_heads(v_ref[prev, :].astype(BF16), head0))
                    scores.append(sc)
                    values.append(vals)
                stats = []
                for sc, rows in zip(scores, rows_of):
                    m_blk = functools.reduce(jnp.maximum, [jnp.max(x, axis=-1, keepdims=True) for x in sc])
                    m_old = jnp.concatenate([m_s.at[0][rows, :], m_s.at[1][rows, :]], axis=0)
                    l_old = jnp.concatenate([l_s.at[0][rows, :], l_s.at[1][rows, :]], axis=0)
                    m_new = jnp.maximum(m_old, m_blk)
                    probs = [jnp.exp(x - m_new) for x in sc]
                    l_blk = functools.reduce(jnp.add, [jnp.sum(p, axis=-1, keepdims=True) for p in probs])
                    alpha = jnp.exp(m_old - m_new)
                    stats.append((m_new, alpha * l_old + l_blk, alpha, probs))
                for (m_new, l_new, alpha, probs), vals, rows in zip(stats, values, rows_of):
                    alpha_tok = jnp.where(head0, alpha[:BLK], alpha[BLK:])
                    p_cat = jnp.concatenate(
                        [h for p in probs for h in (p[:BLK].astype(BF16), p[BLK:].astype(BF16))], axis=1)
                    o_ref[rows, :] = alpha_tok * o_ref[rows, :] + _dot(p_cat, jnp.concatenate(vals, axis=0))
                    m_s.at[0][rows, :] = m_new[:BLK]
                    m_s.at[1][rows, :] = m_new[BLK:]
                    l_s.at[0][rows, :] = l_new[:BLK]
                    l_s.at[1][rows, :] = l_new[BLK:]
        lane = lax.broadcasted_iota(jnp.int32, (BLK, LANES), 1)
        for c in range(s_len // BLK):
            rows = pl.ds(c * BLK, BLK)
            l0, l1 = l_s.at[0][rows, :], l_s.at[1][rows, :]
            o_ref[rows, :] = o_ref[rows, :] / jnp.where(lane < HEAD_DIM, l0, l1)
            lse_ref.at[0][rows, :] = m_s.at[0][rows, :] + jnp.log(l0)
            lse_ref.at[1][rows, :] = m_s.at[1][rows, :] + jnp.log(l1)

    def body(*refs):
        pair_idx = pl.program_id(1)
        step = pl.program_id(0) * n_pairs + pair_idx
        n_c, n_o = len(carry.arrays), len(carry.out_shapes)
        o_ref, lse_ref = refs[9 + n_c:11 + n_c]
        m_s, l_s = refs[11 + n_c + n_o:13 + n_c + n_o]
        carry.run(refs, 9, 2, step, b_sz * n_pairs, lambda: compute(pair_idx, refs[:9], o_ref, lse_ref, m_s, l_s))

    in_specs = []
    for g in range(len(DIL_PAIRS)):
        for kind in range(3):
            off = q_off + kind * per_kind + g * n_pairs
            in_specs.append(pl.BlockSpec((None, s_len, LANES), lambda b, p, off=off: (b, 0, off + p)))
    c_in, c_out, c_shapes, c_alias, c_sems = carry.call_args(9, 2)
    res = _pallas_call(
        body, name="dil_fwd", grid=(b_sz, n_pairs),
        in_specs=in_specs + c_in,
        out_specs=[pl.BlockSpec((None, s_len, LANES), lambda b, p: (b, 0, p)),
                   pl.BlockSpec((None, None, 2, s_len, LANES), lambda b, p: (b, p, 0, 0, 0))] + c_out,
        out_shape=[_hbm_array((b_sz, s_len, DIL_OUT_WIDTH), F32),
                   _hbm_array((b_sz, n_pairs, 2, s_len, LANES), F32)] + c_shapes,
        input_output_aliases=c_alias,
        scratch_shapes=[pltpu.VMEM((2, s_len, LANES), F32), pltpu.VMEM((2, s_len, LANES), F32)] + c_sems,
        compiler_params=pltpu.CompilerParams(dimension_semantics=("arbitrary", "arbitrary"),
                                             vmem_limit_bytes=VMEM_CAP),
    )(*([qkv] * 9), *carry.arrays)
    return res[0], res[1], res[2:]


def _dil_bwd(qkv, o_dl, lse, do_dl, b_sz, s_len, carry):
    n_pairs = DIL_OUT_WIDTH // LANES
    n_groups = len(DIL_PAIRS)
    q_off = 3 * SB_WIDTH // LANES
    per_kind = DIL_WIDTH // LANES

    def compute(pair_idx, group, q_ref, k_ref, v_ref, o_ref, lse_ref, do_ref, dq_ref, dk_ref, dv_ref, d_s, dq_s, dk_s, dv_s):
        lrow = lax.broadcasted_iota(jnp.int32, (LANES, LANES), 0)
        ones_h0 = jnp.where(lrow < HEAD_DIM, 1.0, 0.0).astype(BF16)
        ones_h1 = jnp.where(lrow >= HEAD_DIM, 1.0, 0.0).astype(BF16)
        for c in range(s_len // BLK):
            rows = pl.ds(c * BLK, BLK)
            dd_hi, dd_lo = _split2(do_ref[rows, :] * o_ref[rows, :])
            d_s.at[0][rows, :] = _dot(dd_hi, ones_h0) + _dot(dd_lo, ones_h0)
            d_s.at[1][rows, :] = _dot(dd_hi, ones_h1) + _dot(dd_lo, ones_h1)
        dk_s[...] = jnp.zeros(dk_s.shape, F32)
        dv_s[...] = jnp.zeros(dv_s.shape, F32)

        def one_group(g, dilation):
            head0, valid_cur, valid_prev, bias_cur, bias_prev = _dil_consts(g, pair_idx, dilation)
            units = _dil_units(s_len, dilation)
            scale = 1.0 / math.sqrt(HEAD_DIM)
            for u0 in range(0, len(units), DIL_CHAINS):
                chunk = units[u0:u0 + DIL_CHAINS]
                loaded = []
                for r, n in chunk:
                    rows = _dil_rows(n, r, dilation)
                    q01 = _stack_heads(q_ref[rows, :].astype(BF16), head0)
                    do01 = _stack_heads(do_ref[rows, :].astype(BF16), head0)
                    lse01 = jnp.concatenate([lse_ref.at[0][rows, :], lse_ref.at[1][rows, :]], axis=0)
                    d01 = jnp.concatenate([d_s.at[0][rows, :], d_s.at[1][rows, :]], axis=0)
                    blocks = [(rows, bias_cur, valid_cur)]
                    if n > 0:
                        blocks.append((_dil_rows(n - 1, r, dilation), bias_prev, valid_prev))
                    parts = []
                    for krows, bias, valid in blocks:
                        k = k_ref[krows, :].astype(BF16)
                        v = v_ref[krows, :].astype(BF16)
                        parts.append((krows, k, _dil_scores(q01, k, bias, valid), _dot_nt(do01, v)))
                    loaded.append((rows, q01, do01, lse01, d01, parts))
                grads = []
                for rows, q01, do01, lse01, d01, parts in loaded:
                    for krows, k, sc, dp in parts:
                        p = jnp.exp(sc - lse01)
                        grads.append((p.astype(BF16), (p * (dp - d01) * scale).astype(BF16)))
                it = iter(grads)
                updates = []
                for rows, q01, do01, lse01, d01, parts in loaded:
                    dq = jnp.zeros((2 * BLK, LANES), F32)
                    for krows, k, sc, dp in parts:
                        p_b, ds = next(it)
                        dq = dq + _dot(ds, k)
                        updates.append((krows, _dot_tn(ds, q01), _dot_tn(p_b, do01)))
                    dq_s[rows, :] = jnp.where(head0, dq[:BLK], dq[BLK:])
                for krows, dk, dv in updates:
                    dk_s[krows, :] = dk_s[krows, :] + dk
                    dv_s[krows, :] = dv_s[krows, :] + dv

        for g, (_, dilation) in enumerate(DIL_PAIRS):
            pl.when(group == g)(functools.partial(one_group, g, dilation))
        dq_ref[...] = dq_s[...].astype(dq_ref.dtype)
        dk_ref[...] = dk_s[...].astype(dk_ref.dtype)
        dv_ref[...] = dv_s[...].astype(dv_ref.dtype)

    def body(*refs):
        pair_idx, group = pl.program_id(1), pl.program_id(2)
        step = (pl.program_id(0) * n_pairs + pair_idx) * n_groups + group
        n_c, n_o = len(carry.arrays), len(carry.out_shapes)
        own = refs[:6] + refs[6 + n_c:9 + n_c] + refs[9 + n_c + n_o:13 + n_c + n_o]
        carry.run(refs, 6, 3, step, b_sz * n_pairs * n_groups, lambda: compute(pair_idx, group, *own))

    def qkv_spec(kind):
        return pl.BlockSpec((None, s_len, LANES),
                            lambda b, p, g: (b, 0, q_off + kind * per_kind + g * n_pairs + p))

    tok_spec = pl.BlockSpec((None, s_len, LANES), lambda b, p, g: (b, 0, p))
    out_spec = pl.BlockSpec((None, s_len, LANES), lambda b, p, g: (b, 0, g * n_pairs + p))
    out_sd = _hbm_array((b_sz, s_len, DIL_WIDTH), BF16)
    c_in, c_out, c_shapes, c_alias, c_sems = carry.call_args(6, 3)
    res = _pallas_call(
        body, name="dil_bwd", grid=(b_sz, n_pairs, n_groups),
        in_specs=[qkv_spec(0), qkv_spec(1), qkv_spec(2), tok_spec,
                  pl.BlockSpec((None, None, 2, s_len, LANES), lambda b, p, g: (b, p, 0, 0, 0)), tok_spec] + c_in,
        out_specs=[out_spec, out_spec, out_spec] + c_out,
        out_shape=[out_sd, out_sd, out_sd] + c_shapes,
        input_output_aliases=c_alias,
        scratch_shapes=[pltpu.VMEM((2, s_len, LANES), F32)] + [pltpu.VMEM((s_len, LANES), F32)] * 3 + c_sems,
        compiler_params=pltpu.CompilerParams(dimension_semantics=("arbitrary", "arbitrary", "arbitrary"),
                                             vmem_limit_bytes=VMEM_CAP),
    )(qkv, qkv, qkv, o_dl, lse, do_dl, *carry.arrays)
    return res[:3], res[3:]


def _mesh_pos():
    return lax.axis_index("x"), lax.axis_index("y"), lax.axis_index("c")


def _other_chips(x, y):
    return [(1 - x, y), (x, 1 - y), (1 - x, 1 - y)]


def _hbm_specs(n):
    return [pl.BlockSpec(memory_space=pl.ANY)] * n


SWAPPED = ("w_ffn_in",)


def _slot(x, y, swapped):
    return 2 * y + x if swapped else 2 * x + y


def _cast_to_slab(w, name, swapped=False):
    rows, cols = w.shape
    mine = jnp.reshape(_slot(lax.axis_index("x"), lax.axis_index("y"), swapped), (1,)).astype(jnp.int32)

    def body(idx_ref, w_ref, o_ref):
        o_ref[...] = w_ref[...].astype(BF16)

    return _pallas_call(
        body, name=name,
        grid_spec=pltpu.PrefetchScalarGridSpec(
            num_scalar_prefetch=1, grid=(1,),
            in_specs=[pl.BlockSpec((rows, cols), lambda i, idx: (0, 0))],
            out_specs=pl.BlockSpec((None, rows, cols), lambda i, idx: (idx[0], 0, 0))),
        out_shape=_hbm_array((N_CHIPS, rows, cols), BF16),
        compiler_params=pltpu.CompilerParams(vmem_limit_bytes=_vmem_limit(rows * cols * 6)),
    )(mine, w)


def _gather_issue(slabs, send_sems, recv_sems, swapped):
    x, y, c = _mesh_pos()
    for k, slab in enumerate(slabs):
        half = slab.shape[1] // 2
        rows = slab.at[_slot(x, y, swapped[k]), pl.ds(c * half, half), :]
        for r, (px, py) in enumerate(_other_chips(x, y)):
            pltpu.make_async_remote_copy(
                src_ref=rows, dst_ref=rows, send_sem=send_sems.at[6 * k + r], recv_sem=recv_sems.at[6 * k + r],
                device_id=(px, py, c), device_id_type=MESH).start()


def _gather_complete(slabs, send_sems, recv_sems, swapped):
    x, y, c = _mesh_pos()
    chips = _other_chips(x, y)

    def copy(k, sem, block, rows, to):
        ref = slabs[k].at[block, rows, :]
        return pltpu.make_async_remote_copy(
            src_ref=ref, dst_ref=ref, send_sem=send_sems.at[sem], recv_sem=recv_sems.at[sem],
            device_id=to, device_id_type=MESH)

    for k, slab in enumerate(slabs):
        half = slab.shape[1] // 2
        for r, (px, py) in enumerate(chips):
            theirs = _slot(px, py, swapped[k])
            copy(k, 6 * k + r, theirs, pl.ds(c * half, half), (px, py, c)).wait_recv()
            copy(k, 6 * k + 3 + r, theirs, pl.ds(c * half, half), (x, y, 1 - c)).start()
    for k, slab in enumerate(slabs):
        half = slab.shape[1] // 2
        for r, (px, py) in enumerate(chips):
            copy(k, 6 * k + 3 + r, _slot(px, py, swapped[k]), pl.ds((1 - c) * half, half), (x, y, 1 - c)).wait_recv()
    for k, slab in enumerate(slabs):
        half = slab.shape[1] // 2
        for r, (px, py) in enumerate(chips):
            copy(k, 6 * k + r, _slot(x, y, swapped[k]), pl.ds(c * half, half), (px, py, c)).wait_send()
            copy(k, 6 * k + 3 + r, _slot(px, py, swapped[k]), pl.ds(c * half, half), (x, y, 1 - c)).wait_send()


def _gather_sems(n):
    return [pltpu.SemaphoreType.DMA((6 * n,)), pltpu.SemaphoreType.DMA((6 * n,))]


def _gather_carry(slabs, names):
    swapped = [k in SWAPPED for k in names]
    return _Carry(slabs, [_hbm_array(a.shape, a.dtype) for a in slabs], True, _gather_sems(len(slabs)),
                  lambda ins, outs, sems: _gather_issue(outs, *sems, swapped),
                  lambda ins, outs, sems: _gather_complete(outs, *sems, swapped))


def _pair_copies(ins, outs, send_sems, recv_sems):
    x, y, c = _mesh_pos()
    copies = []
    for k, g in enumerate(ins):
        half = g.shape[1] // 2
        copies.append(pltpu.make_async_remote_copy(
            src_ref=g.at[:, pl.ds((1 - c) * half, half), :], dst_ref=outs[k],
            send_sem=send_sems.at[k], recv_sem=recv_sems.at[k],
            device_id=(x, y, 1 - c), device_id_type=MESH))
    return copies


def _pair_carry(grads):
    n = len(grads)

    def start(ins, outs, sems):
        for cp in _pair_copies(ins, outs, *sems):
            cp.start()

    def finish(ins, outs, sems):
        for cp in _pair_copies(ins, outs, *sems):
            cp.wait()

    return _Carry(grads, [_hbm_array((N_CHIPS, g.shape[1] // 2, g.shape[2]), g.dtype) for g in grads], False,
                  [pltpu.SemaphoreType.DMA((n,)), pltpu.SemaphoreType.DMA((n,))], start, finish)


def _pair_exchange(grads, tag):
    carry = _pair_carry(grads)
    n = len(grads)

    def body(*refs):
        carry.start(refs[:n], refs[n:2 * n], refs[2 * n:])
        carry.finish(refs[:n], refs[n:2 * n], refs[2 * n:])

    return _pallas_call(
        body, name="grad_pair_exchange_" + tag, in_specs=_hbm_specs(n), out_specs=_hbm_specs(n),
        out_shape=carry.out_shapes, scratch_shapes=carry.sems,
    )(*grads)


def _pair_sum(grad, other, name, swapped):
    _, rows, cols = grad.shape
    half = rows // 2
    x, y, c = _mesh_pos()
    idx = jnp.stack([c, _slot(x, y, swapped)]).astype(jnp.int32)

    def body(idx_ref, g_ref, p_ref, own_ref, sb_ref):
        s = g_ref[...] + p_ref[...].astype(F32)
        sb_ref[...] = s.astype(BF16)

        @pl.when(pl.program_id(0) == idx_ref[1])
        def _():
            own_ref[...] = s

    blk = pl.BlockSpec((None, half, cols), lambda p, idx: (p, 0, 0))
    return _pallas_call(
        body, name=name,
        grid_spec=pltpu.PrefetchScalarGridSpec(
            num_scalar_prefetch=1, grid=(N_CHIPS,),
            in_specs=[pl.BlockSpec((None, half, cols), lambda p, idx: (p, idx[0], 0)), blk],
            out_specs=[pl.BlockSpec((half, cols), lambda p, idx: (0, 0)), blk]),
        out_shape=[_hbm_array((half, cols), F32), _hbm_array((N_CHIPS, half, cols), BF16)],
        compiler_params=pltpu.CompilerParams(dimension_semantics=("arbitrary",),
                                             vmem_limit_bytes=_vmem_limit(4 * half * cols * 4)),
    )(idx, grad, other)


def _chip_copies(sums_bf16, lands, send_sems, recv_sems, swapped):
    x, y, c = _mesh_pos()
    return [pltpu.make_async_remote_copy(
        src_ref=sums_bf16[k].at[_slot(px, py, swapped[k])], dst_ref=lands[k].at[r],
        send_sem=send_sems.at[3 * k + r], recv_sem=recv_sems.at[3 * k + r],
        device_id=(px, py, c), device_id_type=MESH)
        for k in range(len(sums_bf16)) for r, (px, py) in enumerate(_other_chips(x, y))]


def _chip_carry(sums_bf16, names):
    swapped = [k in SWAPPED for k in names]

    def start(ins, outs, sems):
        for cp in _chip_copies(ins, outs, *sems, swapped):
            cp.start()

    def finish(ins, outs, sems):
        for cp in _chip_copies(ins, outs, *sems, swapped):
            cp.wait()

    return _Carry(sums_bf16, _chip_landing(sums_bf16), False, _chip_sems(len(sums_bf16)), start, finish)


def _chip_sems(n):
    return [pltpu.SemaphoreType.DMA((3 * n,)), pltpu.SemaphoreType.DMA((3 * n,))]


def _chip_landing(sums_bf16):
    return [_hbm_array((N_CHIPS - 1,) + s.shape[1:], BF16) for s in sums_bf16]


def _chip_sum(own, landed, name):
    rows, cols = own.shape
    core = jnp.reshape(lax.axis_index("c"), (1,)).astype(jnp.int32)

    def body(core_ref, o_ref, l_ref, out_ref):
        out_ref[...] = ((o_ref[...] + l_ref[0].astype(F32)) + l_ref[1].astype(F32)) + l_ref[2].astype(F32)

    return _pallas_call(
        body, name=name,
        grid_spec=pltpu.PrefetchScalarGridSpec(
            num_scalar_prefetch=1, grid=(1,),
            in_specs=[pl.BlockSpec((rows, cols), lambda i, core_ref: (0, 0)),
                      pl.BlockSpec((N_CHIPS - 1, rows, cols), lambda i, core_ref: (0, 0, 0))],
            out_specs=pl.BlockSpec((rows, cols), lambda i, core_ref: (core_ref[0], 0))),
        out_shape=_hbm_array((2 * rows, cols), F32),
        compiler_params=pltpu.CompilerParams(vmem_limit_bytes=_vmem_limit(3 * rows * cols * 4)),
    )(core, own, landed)


def _final_exchange(fulls, v):
    n = len(fulls)
    rows, cols = v.shape
    n_dev = 8

    def body(*refs):
        v_ref, out_ref = refs[0], refs[1 + 2 * n]
        outs = refs[1 + n:1 + 2 * n]
        buf, v_send, v_recv, h_send, h_recv = refs[2 + 2 * n:]
        x, y, c = _mesh_pos()
        me = 4 * x + 2 * y + c
        buf[me] = v_ref[...]
        peers = [(1 - x if r & 4 else x, 1 - y if r & 2 else y, 1 - c if r & 1 else c) for r in range(1, n_dev)]
        copies = []
        for r, peer in enumerate(peers):
            copies.append(pltpu.make_async_remote_copy(
                src_ref=v_ref, dst_ref=buf.at[me], send_sem=v_send.at[r], recv_sem=v_recv.at[r],
                device_id=peer, device_id_type=MESH))
        for k in range(n):
            half = fulls[k].shape[0] // 2
            mine = outs[k].at[pl.ds(c * half, half), :]
            copies.append(pltpu.make_async_remote_copy(
                src_ref=mine, dst_ref=mine, send_sem=h_send.at[k], recv_sem=h_recv.at[k],
                device_id=(x, y, 1 - c), device_id_type=MESH))
        for cp in copies:
            cp.start()
        for r, (px, py, pc) in enumerate(peers):
            pltpu.make_async_remote_copy(
                src_ref=v_ref, dst_ref=buf.at[4 * px + 2 * py + pc], send_sem=v_send.at[r], recv_sem=v_recv.at[r],
                device_id=(px, py, pc), device_id_type=MESH).wait_recv()
        for k in range(n):
            half = fulls[k].shape[0] // 2
            theirs = outs[k].at[pl.ds((1 - c) * half, half), :]
            pltpu.make_async_remote_copy(
                src_ref=theirs, dst_ref=theirs, send_sem=h_send.at[k], recv_sem=h_recv.at[k],
                device_id=(x, y, 1 - c), device_id_type=MESH).wait_recv()
        for cp in copies:
            cp.wait_send()
        acc = buf[0]
        for d in range(1, n_dev):
            acc = acc + buf[d]
        out_ref[...] = acc
        out_ref[3:4, :] = jnp.broadcast_to(jnp.sum(acc[3:4, :], axis=1, keepdims=True), (1, cols))

    vm = pl.BlockSpec(memory_space=pltpu.VMEM)
    res = _pallas_call(
        body, name="final_exchange",
        in_specs=[vm] + _hbm_specs(n), out_specs=_hbm_specs(n) + [vm],
        out_shape=[_hbm_array(f.shape, F32) for f in fulls] + [jax.ShapeDtypeStruct((rows, cols), F32)],
        input_output_aliases={1 + k: k for k in range(n)},
        scratch_shapes=[pltpu.VMEM((n_dev, rows, cols), F32),
                        pltpu.SemaphoreType.DMA((n_dev - 1,)), pltpu.SemaphoreType.DMA((n_dev - 1,)),
                        pltpu.SemaphoreType.DMA((n,)), pltpu.SemaphoreType.DMA((n,))],
    )(v, *fulls)
    return res[:n], res[n]


def _adamw_math(w, g, m, v):
    m = ADAM_B1 * m + (1.0 - ADAM_B1) * g
    v = ADAM_B2 * v + (1.0 - ADAM_B2) * (g * g)
    m_hat = m / (1.0 - ADAM_B1 ** ADAM_STEP)
    v_hat = v / (1.0 - ADAM_B2 ** ADAM_STEP)
    delta = -ADAM_LR * (m_hat / (jnp.sqrt(v_hat) + ADAM_EPS) + ADAM_WD * w)
    return delta, m, v


def _adamw(w, g, m, v, name):
    rows, cols = w.shape
    tm = rows // 2 if (rows // 2) % 8 == 0 else rows
    return _rowwise(_adamw_math, [w, g, m, v], [], [(cols, F32)] * 3, [], tm=tm, name=name)


def _unshard_cols(gathered):
    n, r, c = gathered.shape
    return jnp.transpose(gathered, (1, 0, 2)).reshape(r, n * c)


def _shard_cols(full):
    r, nc = full.shape
    return jnp.transpose(full.reshape(r, N_CHIPS, nc // N_CHIPS), (1, 0, 2))


LATE = ["w_sb_up", "w_dil_up", "w_out", "w_ffn_in", "w_ffn_out"]


def _late_weights(slabs, d_model, d_ff):
    g = dict(zip(LATE, slabs))
    return (_unshard_cols(g["w_sb_up"]), _unshard_cols(g["w_dil_up"]), g["w_out"].reshape(d_model, d_model),
            _unshard_cols(g["w_ffn_in"]), g["w_ffn_out"].reshape(d_ff, d_model))


ROW_SHARDED = ("w_in", "w_out", "w_ffn_in", "w_ffn_out")


def _chip_major(grads):
    out = []
    for k, g in grads.items():
        if k in ROW_SHARDED:
            out.append(g.reshape(N_CHIPS, g.shape[0] // N_CHIPS, g.shape[1]))
        else:
            out.append(_shard_cols(g))
    return out


def _pair_sums(full, others, names):
    return [_pair_sum(g, o, "grad_pair_sum_" + k, k in SWAPPED) for g, o, k in zip(full, others, names)]


def _chip_sums(pair, landed, names):
    return {k: _chip_sum(p[0], l, "grad_chip_sum_" + k) for p, l, k in zip(pair, landed, names)}


def _fwd_bwd(x, loss_target, g_mix, g_ffn, g_fin, slab_in, late_slabs):
    b_sz, s_len, d_model = x.shape
    t = b_sz * s_len
    d_ff = late_slabs[-1].shape[1] * N_CHIPS
    x2d = x.reshape(t, d_model)
    tgt2d = loss_target.reshape(t, d_model)

    u, (slab_in,) = _rowwise(lambda xv, g: (_rms_stats(xv)[0] * g,), [x2d], [g_mix], [(d_model, BF16)], [], tm=512,
                             name="norm_mix", carry=_gather_carry([slab_in], ["w_in"]))
    wt_in = slab_in.reshape(-1, d_model)
    qkv, (slab_ffn_out,) = _mm(u, wt_in, tb=True, b_cols=(0, QKV_WIDTH), tm=2048, tn=768, tk=d_model, name="proj_qkv",
                               carry=_gather_carry(late_slabs[4:], LATE[4:]))
    gates = _mm(u, wt_in, tb=True, b_cols=(QKV_WIDTH, 2 * d_model), out_dtype=BF16, tm=t, tn=256, tk=d_model,
                name="proj_gates")
    qkv3 = qkv.reshape(b_sz, s_len, QKV_WIDTH)
    o_sb, (slab_ffn_in,) = _sb_fwd(qkv3, b_sz, s_len, _gather_carry(late_slabs[3:4], LATE[3:4]))
    o_dl, lse, small_slabs = _dil_fwd(qkv3, b_sz, s_len, _gather_carry(late_slabs[:3], LATE[:3]))
    wf_sb_up, wf_dil_up, wf_out, wf_ffn_in, wf_ffn_out = _late_weights(
        list(small_slabs) + [slab_ffn_in, slab_ffn_out], d_model, d_ff)
    o_sb2, o_dl2 = o_sb.reshape(t, SB_WIDTH), o_dl.reshape(t, DIL_OUT_WIDTH)
    y_sb = _mm(o_sb2, wf_sb_up, out_dtype=BF16, tm=1024, tn=1024, tk=SB_WIDTH, name="sb_up")
    y_dl = _mm(o_dl2, wf_dil_up, out_dtype=BF16, tm=1024, tn=1024, tk=DIL_OUT_WIDTH, name="dil_up")

    def merge_fn(gt, ys, yd):
        return (_sigmoid(gt[:, :d_model]) * ys + _sigmoid(gt[:, d_model:]) * yd,)

    (merged,) = _rowwise(merge_fn, [gates, y_sb, y_dl], [], [(d_model, BF16)], [], tm=512, name="merge")
    x1 = _mm(merged, wf_out, add=x2d, tm=512, tn=1024, tk=d_model, name="mix_out")
    (u2,) = _rowwise(lambda xv, g: (_rms_stats(xv)[0] * g,), [x1], [g_ffn], [(d_model, BF16)], [], tm=512, name="norm_ffn")
    half_ff = d_ff // 2

    def act_fn(hv):
        gate = hv[:, :half_ff]
        return hv, gate * _sigmoid(gate) * hv[:, half_ff:]

    h, act = _mm(u2, wf_ffn_in, tm=512, tn=d_ff, tk=d_model, name="ffn_in",
                 epilogue=(act_fn, [], [], [(d_ff, BF16), (half_ff, BF16)], []))
    def head_fn(xv, tg, g):
        xhat, r = _rms_stats(xv)
        err = xhat * g - tg
        dy = err * (1.0 / d_model)
        dx, dg_rows = _rms_bwd(dy, xhat, r, g)
        loss_lanes = (0.5 / d_model) * jnp.sum(err * err, axis=0, keepdims=True)
        return dx, dx, jnp.sum(dg_rows, axis=0, keepdims=True), loss_lanes

    dx2, dx2_b, dg_fin, loss_lanes = _mm(
        act, wf_ffn_out, add=x1, tm=512, tn=1024, tk=d_ff, name="ffn_out",
        epilogue=(head_fn, [tgt2d], [g_fin], [(d_model, F32), (d_model, BF16)], [(1, d_model), (1, d_model)]))

    def dact_fn(da, hv):
        gate, up = hv[:, :half_ff], hv[:, half_ff:]
        sg = _sigmoid(gate)
        dgate = da * up * (sg * (1.0 + gate * (1.0 - sg)))
        return (jnp.concatenate([dgate, da * (gate * sg)], axis=1),)

    (dh,) = _mm(dx2_b, wf_ffn_out, tb=True, tm=512, tn=half_ff, tk=d_model, name="ffn_out_dx",
                epilogue=(dact_fn, [h], [], [(d_ff, BF16)], []))
    gw_ffn_out = _mm(act, dx2_b, ta=True, tm=256, tn=d_model, tk=t, name="ffn_out_dw")
    def norm_bwd_fn(du_, dres, xv, g):
        xhat, r = _rms_stats(xv)
        dx, dg_rows = _rms_bwd(du_, xhat, r, g)
        return dres + dx, jnp.sum(dg_rows, axis=0, keepdims=True)

    def norm_bwd_twice(*args):
        dx, dg = norm_bwd_fn(*args)
        return dx, dx, dg

    dx1, dx1_b, dg_ffn = _mm(dh, wf_ffn_in, tb=True, tm=512, tn=1024, tk=2 * d_ff, name="ffn_in_dx",
                             epilogue=(norm_bwd_twice, [dx2, x1], [g_ffn], [(d_model, F32), (d_model, BF16)], [(1, d_model)]))
    gwt_ffn_in = _mm(dh, u2, ta=True, tm=512, tn=d_model, tk=t, name="ffn_in_dw")

    dmerged = _mm(dx1_b, wf_out, tb=True, out_dtype=BF16, tm=512, tn=1024, tk=d_model, name="mix_out_dx")
    gw_out = _mm(merged, dx1_b, ta=True, tm=256, tn=d_model, tk=t, name="mix_out_dw")

    def merge_bwd_fn(gt, ys, yd, dm):
        s_sb, s_dl = _sigmoid(gt[:, :d_model]), _sigmoid(gt[:, d_model:])
        dgates = jnp.concatenate([dm * ys * s_sb * (1.0 - s_sb), dm * yd * s_dl * (1.0 - s_dl)], axis=1)
        return dgates, dm * s_sb, dm * s_dl

    full_big = _chip_major({"w_out": gw_out, "w_ffn_in": gwt_ffn_in, "w_ffn_out": gw_ffn_out})
    dgates, dy_sb, dy_dl, others_big = _rowwise(
        merge_bwd_fn, [gates, y_sb, y_dl, dmerged], [], [(2 * d_model, BF16), (d_model, BF16), (d_model, BF16)], [],
        tm=256, name="merge_bwd", carry=_pair_carry(full_big))
    pair_big = _pair_sums(full_big, others_big, LATE[2:])
    do_sb = _mm(dy_sb, wf_sb_up, tb=True, out_dtype=BF16, tm=1024, tn=SB_WIDTH, tk=d_model, name="sb_up_dx")
    gw_sb_up = _mm(o_sb2, dy_sb, ta=True, tm=SB_WIDTH, tn=1024, tk=512, name="sb_up_dw")
    do_dl = _mm(dy_dl, wf_dil_up, tb=True, tm=1024, tn=DIL_OUT_WIDTH, tk=d_model, name="dil_up_dx")
    gw_dil_up = _mm(o_dl2, dy_dl, ta=True, tm=DIL_OUT_WIDTH, tn=1024, tk=512, name="dil_up_dw")
    full_small = _chip_major({"w_sb_up": gw_sb_up, "w_dil_up": gw_dil_up})
    (dq_sb, dk_sb, dv_sb), brought = _sb_bwd(
        qkv3, o_sb, do_sb.reshape(b_sz, s_len, SB_WIDTH), b_sz, s_len,
        _chip_carry([p[1] for p in pair_big[:2]], LATE[2:4]) + _pair_carry(full_small))
    pair_small = _pair_sums(full_small, brought[2:], LATE[:2])
    (dq_dl, dk_dl, dv_dl), landed_b = _dil_bwd(
        qkv3, o_dl, lse, do_dl.reshape(b_sz, s_len, DIL_OUT_WIDTH), b_sz, s_len,
        _chip_carry([pair_big[2][1], pair_small[0][1], pair_small[1][1]], [LATE[4], LATE[0], LATE[1]]))
    pair = pair_small + pair_big
    landed = [landed_b[1], landed_b[2], brought[0], brought[1], landed_b[0]]
    dproj = [a.reshape(t, -1) for a in (dq_sb, dk_sb, dv_sb, dq_dl, dk_dl, dv_dl)] + [dgates]
    gwt_in, gwt_in_b = _mm(dproj, u, ta=True, tm=256, tn=d_model, tk=t, name="proj_dw",
                           epilogue=(lambda tile: (tile, tile), [], [], [(d_model, F32), (d_model, BF16)], []))
    full_in = _chip_major({"w_in": gwt_in})
    pair_in = _pair_sums(full_in, _pair_exchange(_chip_major({"w_in": gwt_in_b}), "w_in"), ["w_in"])
    (dx, dg_mix), landed_in = _mm(
        dproj, wt_in, tm=512, tn=1024, tk=wt_in.shape[0], name="proj_dx",
        carry=_chip_carry([p[1] for p in pair_in], ["w_in"]),
        epilogue=(norm_bwd_fn, [dx1, x2d], [g_mix], [(d_model, F32)], [(1, d_model)]))

    grads = _chip_sums(pair, landed, LATE)
    grads.update(_chip_sums(pair_in, landed_in, ["w_in"]))
    return dx, grads, dg_mix, dg_ffn, dg_fin, loss_lanes


def kernel(x, norm_mix_g, w_in, w_sb_up, w_dil_up, w_out, norm_ffn_g, w_ffn_in, w_ffn_out, norm_final_g, loss_target, m_norm_mix_g, m_w_in, m_w_sb_up, m_w_dil_up, m_w_out, m_norm_ffn_g, m_w_ffn_in, m_w_ffn_out, m_norm_final_g, v_norm_mix_g, v_w_in, v_w_sb_up, v_w_dil_up, v_w_out, v_norm_ffn_g, v_w_ffn_in, v_w_ffn_out, v_norm_final_g):
    b_sz, s_len, d_model = x.shape
    d_ff = w_ffn_out.shape[1] * N_CHIPS
    g_mix, g_ffn, g_fin = norm_mix_g, norm_ffn_g, norm_final_g.reshape(1, d_model)

    names = ["w_in", "w_sb_up", "w_dil_up", "w_out", "w_ffn_in", "w_ffn_out"]
    shards = {"w_in": jnp.swapaxes(w_in[0], 0, 1), "w_sb_up": w_sb_up[0], "w_dil_up": w_dil_up[0], "w_out": w_out[0],
              "w_ffn_in": w_ffn_in[0], "w_ffn_out": w_ffn_out[0]}
    slab_in = _cast_to_slab(shards["w_in"], "cast_w_in")
    late_slabs = [_cast_to_slab(shards[k], "cast_" + k, k in SWAPPED) for k in LATE]

    dx, grads, dg_mix, dg_ffn, dg_fin, loss_lanes = _fwd_bwd(
        x, loss_target, g_mix, g_ffn, g_fin, slab_in, late_slabs)

    small = jnp.concatenate([dg_mix, dg_ffn, dg_fin, loss_lanes, jnp.zeros((4, d_model), F32)], axis=0)
    full_grads, small = _final_exchange([grads[k] for k in names], small)
    grads = dict(zip(names, full_grads))
    grads["w_ffn_in"] = jnp.swapaxes(grads["w_ffn_in"], 0, 1)
    loss = small[3, 0]
    gains = jnp.concatenate([g_mix, g_ffn, g_fin, jnp.zeros((5, d_model), F32)], axis=0)
    gains_m = jnp.concatenate([m_norm_mix_g, m_norm_ffn_g, m_norm_final_g.reshape(1, d_model), jnp.zeros((5, d_model), F32)], axis=0)
    gains_v = jnp.concatenate([v_norm_mix_g, v_norm_ffn_g, v_norm_final_g.reshape(1, d_model), jnp.ones((5, d_model), F32)], axis=0)
    gd, gm, gv = _rowwise(_adamw_math, [gains, small, gains_m, gains_v], [], [(d_model, F32)] * 3, [], tm=8, name="adamw_gains")

    moments = {"w_in": (jnp.swapaxes(m_w_in[0], 0, 1), jnp.swapaxes(v_w_in[0], 0, 1)),
               "w_sb_up": (m_w_sb_up[0], v_w_sb_up[0]), "w_dil_up": (m_w_dil_up[0], v_w_dil_up[0]),
               "w_out": (m_w_out[0], v_w_out[0]), "w_ffn_in": (m_w_ffn_in[0], v_w_ffn_in[0]),
               "w_ffn_out": (m_w_ffn_out[0], v_w_ffn_out[0])}
    upd = {k: _adamw(shards[k], grads[k], moments[k][0], moments[k][1], "adamw_" + k) for k in names}

    def as_output(k, a):
        return (jnp.swapaxes(a, 0, 1) if k == "w_in" else a)[None]

    def w_out_of(i):
        return [as_output(k, upd[k][i]) for k in names]

    def ordered(mix, ws, ffn_g, fin):
        return [mix, ws[0], ws[1], ws[2], ws[3], ffn_g, ws[4], ws[5], fin]

    grad_ws = [as_output(k, grads[k]) for k in names]
    outs = [loss, dx.reshape(b_sz, s_len, d_model)]
    outs += ordered(small[0:1], grad_ws, small[1:2], small[2])
    outs += ordered(gd[0:1], w_out_of(0), gd[1:2], gd[2])
    outs += ordered(gm[0:1], w_out_of(1), gm[1:2], gm[2])
    outs += ordered(gv[0:1], w_out_of(2), gv[1:2], gv[2])
    return tuple(outs)
```

```python
import functools
import math

import jax
import jax.numpy as jnp
from jax import lax
from jax.experimental import pallas as pl
from jax.experimental.pallas import tpu as pltpu

F32 = jnp.float32
BF16 = jnp.bfloat16
MESH = pl.DeviceIdType.MESH

HEAD_DIM = 64
SB_HEADS = 8
DIL_PAIRS = ((128, 1), (512, 4), (2048, 16))
DIL_HEADS_PER_GROUP = 4
DIL_HEADS = DIL_HEADS_PER_GROUP * len(DIL_PAIRS)
SB_WIDTH = SB_HEADS * HEAD_DIM
DIL_WIDTH = DIL_HEADS * HEAD_DIM
DIL_OUT_WIDTH = DIL_HEADS_PER_GROUP * HEAD_DIM
QKV_WIDTH = 3 * SB_WIDTH + 3 * DIL_WIDTH
RMS_EPS = 1e-6
ALIBI_MAX_BIAS = 8.0
ADAM_LR = 0.001
ADAM_B1 = 0.9
ADAM_B2 = 0.999
ADAM_EPS = 1e-08
ADAM_WD = 0.01
ADAM_STEP = 10

LANES = 128
BLK = 128
NEG = -1e30
EXP_UNDERFLOW = -104.0
SB_FWD_CHAINS = 4
SB_BWD_CHAINS = 4
DIL_CHAINS = 4
N_CHIPS = 4
VMEM_CAP = 56 * 1024 * 1024


def _vmem_limit(tile_bytes):
    return int(min(VMEM_CAP, max(32 * 1024 * 1024, 3 * tile_bytes + 8 * 1024 * 1024)))


def _hbm_array(shape, dtype):
    return pltpu.HBM(shape, dtype)


def _nbytes(shape, dtype):
    return math.prod(shape) * jnp.dtype(dtype).itemsize


def _in_hbm(x):
    return pltpu.with_memory_space_constraint(x, pltpu.HBM)


def _dot(a, b):
    return jnp.dot(a, b, preferred_element_type=F32)


def _dot_nt(a, b):
    return lax.dot_general(a, b, (((1,), (1,)), ((), ())), preferred_element_type=F32)


def _dot_tn(a, b):
    return lax.dot_general(a, b, (((0,), (0,)), ((), ())), preferred_element_type=F32)


def _split2(x):
    hi = x.astype(BF16)
    lo = (x - hi.astype(F32)).astype(BF16)
    return hi, lo


def _sigmoid(x):
    return pl.reciprocal(1.0 + jnp.exp(-x), approx=True)


class _Carry:
    def __init__(self, arrays=(), out_shapes=(), aliased=False, sems=(), start=None, finish=None):
        self.arrays, self.out_shapes, self.aliased = list(arrays), list(out_shapes), aliased
        self.sems, self.start, self.finish = list(sems), start, finish

    def __bool__(self):
        return bool(self.arrays)

    def __add__(self, other):
        assert not self.aliased and not other.aliased
        n_a, n_o, n_s = len(self.arrays), len(self.out_shapes), len(self.sems)
        return _Carry(
            self.arrays + other.arrays, self.out_shapes + other.out_shapes, False, self.sems + other.sems,
            lambda i, o, s: (self.start(i[:n_a], o[:n_o], s[:n_s]), other.start(i[n_a:], o[n_o:], s[n_s:])),
            lambda i, o, s: (self.finish(i[:n_a], o[:n_o], s[:n_s]), other.finish(i[n_a:], o[n_o:], s[n_s:])))

    def call_args(self, n_in, n_out):
        aliases = {n_in + k: n_out + k for k in range(len(self.arrays))} if self.aliased else {}
        return _hbm_specs(len(self.arrays)), _hbm_specs(len(self.out_shapes)), self.out_shapes, aliases, self.sems

    def run(self, refs, n_in, n_out, step, n_steps, compute):
        if not self:
            compute()
            return
        n_c, n_o, n_s = len(self.arrays), len(self.out_shapes), len(self.sems)
        ins = refs[n_in:n_in + n_c]
        outs = refs[n_in + n_c + n_out:n_in + n_c + n_out + n_o]
        sems = refs[len(refs) - n_s:]

        @pl.when(step == 0)
        def _():
            self.start(ins, outs, sems)

        compute()

        @pl.when(step == n_steps - 1)
        def _():
            self.finish(ins, outs, sems)


def _mm(a, b, *, ta=False, tb=False, add=None, out_dtype=F32, tm, tn, tk, name, carry=None, epilogue=None,
        b_cols=None):
    carry = carry or _Carry()
    n_car = len(carry.arrays)
    pieces = list(a) if isinstance(a, (list, tuple)) else [a]
    n_a = len(pieces)
    widths = [p.shape[1] for p in pieces]
    starts = [sum(widths[:p]) for p in range(n_a)]
    if ta:
        kdim, m = pieces[0].shape[0], sum(widths)
    else:
        m, kdim = pieces[0].shape[0], sum(widths)
    if tb:
        n, k2 = b.shape
    else:
        k2, n = b.shape
    col0 = 0
    if b_cols is not None:
        assert b_cols[0] % tn == 0, name
        col0, n = b_cols[0] // tn, b_cols[1]
    assert kdim == k2 and m % tm == 0 and n % tn == 0 and kdim % tk == 0, (name, a.shape, b.shape)
    nk = kdim // tk
    assert n_a == 1 or (nk == 1 and not tb and (not ta or all(w % tm == 0 for w in widths))), name
    grid = (m // tm, n // tn, nk)
    a_mode = dict(pipeline_mode=pl.Buffered(1)) if grid[0] == 1 and nk == 1 else {}
    b_mode = dict(pipeline_mode=pl.Buffered(1)) if grid[1] == 1 and nk == 1 else {}
    if n_a == 1:
        a_specs = [pl.BlockSpec((tk, tm), lambda i, j, k: (k, i), **a_mode) if ta
                   else pl.BlockSpec((tm, tk), lambda i, j, k: (i, k), **a_mode)]
    elif ta:
        a_specs = [pl.BlockSpec((tk, tm), lambda i, j, k, s=s // tm, w=w // tm: (0, jnp.clip(i - s, 0, w - 1)))
                   for s, w in zip(starts, widths)]
    else:
        a_specs = [pl.BlockSpec((tm, w), lambda i, j, k: (i, 0)) for w in widths]
    b_spec = (pl.BlockSpec((tn, tk), lambda i, j, k: (j + col0, k), **b_mode) if tb
              else pl.BlockSpec((tk, tn), lambda i, j, k: (k, j + col0), **b_mode))
    o_spec = pl.BlockSpec((tm, tn), lambda i, j, k: (i, j))
    dims = ((((0,) if ta else (1,)), ((1,) if tb else (0,))), ((), ()))
    has_add = add is not None
    if epilogue is None:
        ep_fn, ep_rows, ep_params, ep_outs, ep_accs = None, [], [], [], []
        out_sds, out_specs = [jax.ShapeDtypeStruct((m, n), out_dtype)], [o_spec]
    else:
        ep_fn, ep_rows, ep_params, ep_outs, ep_accs = epilogue
        assert grid[1] == 1 or not ep_accs, name
        out_sds = [jax.ShapeDtypeStruct((m, w * grid[1]), d) for w, d in ep_outs]
        out_sds += [jax.ShapeDtypeStruct(sh, F32) for sh in ep_accs]
        out_specs = [pl.BlockSpec((tm, w), lambda i, j, k: (i, j)) for w, _ in ep_outs]
        out_specs += [pl.BlockSpec(sh, lambda i, j, k: (0, 0)) for sh in ep_accs]
    n_main = len(out_sds)
    use_scratch = nk > 1 and (ep_fn is not None or jnp.dtype(out_dtype) != jnp.dtype(F32))
    n_in = n_a + 1 + has_add + len(ep_rows) + len(ep_params)

    def finish(total, refs, pid):
        outs = refs[n_in + n_car:n_in + n_car + n_main]
        if ep_fn is None:
            outs[0][...] = total.astype(out_dtype)
            return
        first = n_a + 1 + has_add
        rows = [r[...].astype(F32) for r in refs[first:first + len(ep_rows)]]
        params = [p[...] for p in refs[first + len(ep_rows):n_in]]
        res = ep_fn(total, *rows, *params)
        for o_ref, v in zip(outs[:len(ep_outs)], res):
            o_ref[...] = v.astype(o_ref.dtype)
        acc_refs = outs[len(ep_outs):]
        if acc_refs:
            @pl.when(pid[0] == 0)
            def _():
                for r in acc_refs:
                    r[...] = jnp.zeros(r.shape, F32)

            for r, v in zip(acc_refs, res[len(ep_outs):]):
                r[...] += v

    def compute(refs, pid):
        a_ref, b_ref = refs[0], refs[n_a]
        add_ref = refs[n_a + 1] if has_add else None

        def dot(x, y):
            return lax.dot_general(x.astype(BF16), y.astype(BF16), dims, preferred_element_type=F32)

        if n_a > 1 and ta:
            for p_ref, s, w in zip(refs[:n_a], starts, widths):
                @pl.when((pid[0] >= s // tm) & (pid[0] < (s + w) // tm))
                def _(p_ref=p_ref):
                    prod = dot(p_ref[...], b_ref[...])
                    finish(prod + add_ref[...] if has_add else prod, refs, pid)
            return
        if n_a > 1:
            prod = dot(a_ref[...], b_ref[:widths[0], :])
            for p_ref, s, w in zip(refs[1:n_a], starts[1:], widths[1:]):
                prod += dot(p_ref[...], b_ref[s:s + w, :])
        else:
            prod = dot(a_ref[...], b_ref[...])
        if nk == 1:
            finish(prod + add_ref[...] if has_add else prod, refs, pid)
            return
        acc_ref = refs[n_in + n_car + n_main + len(carry.out_shapes)] if use_scratch else refs[n_in + n_car]
        k = pid[2]

        @pl.when(k == 0)
        def _():
            acc_ref[...] = prod + add_ref[...] if has_add else prod

        @pl.when(k > 0)
        def _():
            acc_ref[...] += prod

        if use_scratch:
            @pl.when(k == nk - 1)
            def _():
                finish(acc_ref[...], refs, pid)

    def body(*refs):
        pid = (pl.program_id(0), pl.program_id(1), pl.program_id(2))
        step = (pid[0] * grid[1] + pid[1]) * nk + pid[2]
        carry.run(refs, n_in, n_main, step, grid[0] * grid[1] * nk, lambda: compute(refs, pid))

    tile_bytes = ((n_a if ta else 1) * _nbytes((tm, tk), pieces[0].dtype)
                  + _nbytes((tk, tn), b.dtype) + 2 * _nbytes((tm, tn), F32)
                  + (_nbytes((tm, tn), F32) if has_add else 0)
                  + sum(_nbytes((tm, r.shape[1]), r.dtype) for r in ep_rows) + sum(_nbytes((tm, w), d) for w, d in ep_outs))
    in_specs = a_specs + [b_spec] + ([o_spec] if has_add else [])
    in_specs += [pl.BlockSpec((tm, r.shape[1] // grid[1]), lambda i, j, k: (i, j)) for r in ep_rows]
    in_specs += [pl.BlockSpec(p.shape, lambda i, j, k: (0, 0)) for p in ep_params]
    args = tuple(pieces) + (b,) + ((add,) if has_add else ()) + tuple(ep_rows) + tuple(ep_params)
    scratch = [pltpu.VMEM((tm, tn), F32)] if use_scratch else []
    serial = bool(carry) or bool(ep_accs)
    c_in, c_out, c_shapes, c_alias, c_sems = carry.call_args(n_in, n_main)
    res = pl.pallas_call(
        body, name=name, grid=grid,
        in_specs=in_specs + c_in, out_specs=out_specs + c_out, out_shape=out_sds + c_shapes,
        input_output_aliases=c_alias, scratch_shapes=scratch + c_sems,
        compiler_params=pltpu.CompilerParams(
            dimension_semantics=("arbitrary",) * 3 if serial else ("parallel", "parallel", "arbitrary"),
            vmem_limit_bytes=_vmem_limit(tile_bytes)),
    )(*args, *carry.arrays)
    main = res[0] if ep_fn is None else list(res[:n_main])
    return (main, res[n_main:]) if carry else main


def _rowwise(fn, rows, params, outs, accs, *, tm, name, carry=None, out_type=jax.ShapeDtypeStruct):
    carry = carry or _Carry()
    t = rows[0].shape[0]
    assert t % tm == 0, (name, t, tm)
    n_r, n_p, n_o, n_c = len(rows), len(params), len(outs), len(carry.arrays)

    def compute(refs, first):
        vals = [r[...].astype(F32) for r in refs[:n_r]] + [p[...] for p in refs[n_r:n_r + n_p]]
        res = fn(*vals)
        o_refs = refs[n_r + n_p + n_c:n_r + n_p + n_c + n_o]
        a_refs = refs[n_r + n_p + n_c + n_o:n_r + n_p + n_c + n_o + len(accs)]
        for o_ref, v in zip(o_refs, res[:n_o]):
            o_ref[...] = v.astype(o_ref.dtype)
        if accs:
            @pl.when(first)
            def _():
                for a_ref in a_refs:
                    a_ref[...] = jnp.zeros(a_ref.shape, F32)

            for a_ref, v in zip(a_refs, res[n_o:]):
                a_ref[...] += v

    def body(*refs):
        step = pl.program_id(0)
        carry.run(refs, n_r + n_p, n_o + len(accs), step, t // tm, lambda: compute(refs, step == 0))

    in_specs = [pl.BlockSpec((tm, r.shape[1]), lambda i: (i, 0)) for r in rows]
    in_specs += [pl.BlockSpec(p.shape, lambda i: (0, 0)) for p in params]
    out_specs = [pl.BlockSpec((tm, w), lambda i: (i, 0)) for w, _ in outs]
    out_specs += [pl.BlockSpec(s, lambda i: (0, 0)) for s in accs]
    out_shape = [out_type((t, w), d) for w, d in outs]
    out_shape += [jax.ShapeDtypeStruct(s, F32) for s in accs]
    tile_bytes = sum(_nbytes((tm, r.shape[1]), r.dtype) for r in rows) + sum(_nbytes((tm, w), F32) for w, _ in outs)
    c_in, c_out, c_shapes, c_alias, c_sems = carry.call_args(n_r + n_p, n_o + len(accs))
    res = pl.pallas_call(
        body, name=name, grid=(t // tm,), in_specs=in_specs + c_in, out_specs=out_specs + c_out,
        out_shape=out_shape + c_shapes, input_output_aliases=c_alias, scratch_shapes=c_sems,
        compiler_params=pltpu.CompilerParams(
            dimension_semantics=("arbitrary",) if accs or carry else ("parallel",),
            vmem_limit_bytes=_vmem_limit(2 * tile_bytes)),
    )(*rows, *params, *carry.arrays)
    own = n_o + len(accs)
    return (list(res[:own]) + [res[own:]]) if carry else res


def _rms_stats(x):
    r = lax.rsqrt(jnp.mean(x * x, axis=-1, keepdims=True) + RMS_EPS)
    return x * r, r


def _rms_bwd(dy, xhat, r, g):
    dxhat = dy * g
    dx = r * (dxhat - xhat * jnp.mean(dxhat * xhat, axis=-1, keepdims=True))
    return dx, dy * xhat


def _sb_consts():
    lane = lax.broadcasted_iota(jnp.int32, (BLK, LANES), 1)
    head0 = lane < HEAD_DIM
    row = lax.broadcasted_iota(jnp.int32, (2 * BLK, BLK), 0) % BLK
    col = lax.broadcasted_iota(jnp.int32, (2 * BLK, BLK), 1)
    causal = col < row
    jj = lax.broadcasted_iota(jnp.int32, (BLK, BLK), 0)
    ss = lax.broadcasted_iota(jnp.int32, (BLK, BLK), 1)
    suffix = jnp.where(jj > ss, 1.0, 0.0).astype(BF16)
    return head0, causal, suffix


def _stack_heads(x, head0):
    zero = jnp.zeros_like(x)
    return jnp.concatenate([jnp.where(head0, x, zero), jnp.where(head0, zero, x)], axis=0)


def _sb_logits(z, causal, masked):
    sp = jnp.log(1.0 + jnp.exp(-jnp.abs(z)))
    log_keep = -(jnp.maximum(z, 0.0) + sp)
    log_beta = jnp.minimum(z, 0.0) - sp
    if masked:
        log_keep = jnp.where(causal, log_keep, 0.0)
    return log_keep, log_beta


def _suffix_sums(x, suffix):
    hi, lo = _split2(x)
    after = _dot(hi, suffix) + _dot(lo, suffix)
    total = jnp.broadcast_to(after[:, 0:1] + x[:, 0:1], x.shape)
    return after, total


def _sb_walk_back(i, state, per_chain, tile):
    def alive(st):
        worst = functools.reduce(jnp.maximum, [st[p][:, 0:1] for p in range(0, len(st), per_chain)])
        return jnp.max(worst) > EXP_UNDERFLOW

    def cond(c):
        return jnp.logical_and(c[0] < i, alive(c[1]))

    def body(c):
        return c[0] + 1, tile(i - 1 - c[0], c[1], False)

    return lax.while_loop(cond, body, (jnp.int32(0), state))[1]


def _lane_blocks(x, n):
    return [x[:, p * LANES:(p + 1) * LANES] for p in range(n)]


def _sb_fwd(qkv, b_sz, s_len, carry):
    nq = s_len // BLK
    n_pairs = SB_WIDTH // LANES
    ch = SB_FWD_CHAINS
    n_steps = n_pairs // ch
    scale = 1.0 / math.sqrt(HEAD_DIM)

    def compute(q_ref, k_ref, v_ref, o_ref):
        head0, causal, suffix = _sb_consts()

        def q_block(i, _):
            qs = pl.multiple_of(i * BLK, BLK)
            q_all = (q_ref[pl.ds(qs, BLK), :] * scale).astype(BF16)
            q01 = [_stack_heads(q, head0) for q in _lane_blocks(q_all, ch)]

            def tile(j, state, masked):
                ks = pl.multiple_of(j * BLK, BLK)
                ks_ = _lane_blocks(k_ref[pl.ds(ks, BLK), :].astype(BF16), ch)
                vs_ = _lane_blocks(v_ref[pl.ds(ks, BLK), :].astype(BF16), ch)
                zs = [_dot_nt(q01[p], ks_[p]) for p in range(ch)]
                logits = [_sb_logits(z, causal, masked) for z in zs]
                sums = [_suffix_sums(lg[0], suffix) for lg in logits]
                out = []
                for p in range(ch):
                    carry, acc = state[2 * p], state[2 * p + 1]
                    after, total = sums[p]
                    a = jnp.exp(logits[p][1] + carry + after)
                    if masked:
                        a = jnp.where(causal, a, 0.0)
                    a_hi, a_lo = _split2(a)
                    a_cat = jnp.concatenate([a_hi[:BLK], a_hi[BLK:], a_lo[:BLK], a_lo[BLK:]], axis=1)
                    v01 = _stack_heads(vs_[p], head0)
                    out += [carry + total, acc + _dot(a_cat, jnp.concatenate([v01, v01], axis=0))]
                return tuple(out)

            state = (jnp.zeros((2 * BLK, BLK), F32), jnp.zeros((BLK, LANES), F32)) * ch
            state = tile(i, state, True)
            state = _sb_walk_back(i, state, 2, tile)
            o_ref[pl.ds(qs, BLK), :] = jnp.concatenate([state[2 * p + 1] for p in range(ch)], axis=1)
            return 0

        lax.fori_loop(0, nq, q_block, 0)

    def body(*refs):
        step = pl.program_id(0) * n_steps + pl.program_id(1)
        o_ref = refs[3 + len(carry.arrays)]
        carry.run(refs, 3, 1, step, b_sz * n_steps, lambda: compute(refs[0], refs[1], refs[2], o_ref))

    blk = lambda off: pl.BlockSpec((None, s_len, ch * LANES), lambda b, p: (b, 0, off + p))
    c_in, c_out, c_shapes, c_alias, c_sems = carry.call_args(3, 1)
    res = pl.pallas_call(
        body, name="sb_fwd", grid=(b_sz, n_steps),
        in_specs=[blk(0), blk(n_steps), blk(2 * n_steps)] + c_in, out_specs=[blk(0)] + c_out,
        out_shape=[jax.ShapeDtypeStruct((b_sz, s_len, SB_WIDTH), F32)] + c_shapes,
        input_output_aliases=c_alias, scratch_shapes=c_sems,
        compiler_params=pltpu.CompilerParams(dimension_semantics=("arbitrary", "arbitrary"),
                                             vmem_limit_bytes=VMEM_CAP),
    )(qkv, qkv, qkv, *carry.arrays)
    return res[0], res[1:]


def _sb_bwd(qkv, o_sb, do_sb, b_sz, s_len, carry):
    nq = s_len // BLK
    n_pairs = SB_WIDTH // LANES
    ch = SB_BWD_CHAINS
    n_steps = n_pairs // ch
    scale = 1.0 / math.sqrt(HEAD_DIM)

    def compute(q_ref, k_ref, v_ref, o_ref, do_ref, dq_ref, dk_ref, dv_ref, dk_acc, dv_acc):
        head0, causal, suffix = _sb_consts()
        lrow = lax.broadcasted_iota(jnp.int32, (LANES, LANES), 0)
        ones_h0 = jnp.where(lrow < HEAD_DIM, 1.0, 0.0).astype(BF16)
        ones_h1 = jnp.where(lrow >= HEAD_DIM, 1.0, 0.0).astype(BF16)
        dk_acc[...] = jnp.zeros(dk_acc.shape, F32)
        dv_acc[...] = jnp.zeros(dv_acc.shape, F32)

        def q_block(i, _):
            qs = pl.multiple_of(i * BLK, BLK)
            q_all = (q_ref[pl.ds(qs, BLK), :] * scale).astype(BF16)
            do_all = do_ref[pl.ds(qs, BLK), :].astype(BF16)
            dd_all = do_all.astype(F32) * o_ref[pl.ds(qs, BLK), :]
            q01 = [_stack_heads(q, head0) for q in _lane_blocks(q_all, ch)]
            do01 = [_stack_heads(d, head0) for d in _lane_blocks(do_all, ch)]
            tot = []
            for dd in _lane_blocks(dd_all, ch):
                dd_hi, dd_lo = _split2(dd)
                tot.append(jnp.concatenate([_dot(dd_hi, ones_h0) + _dot(dd_lo, ones_h0),
                                            _dot(dd_hi, ones_h1) + _dot(dd_lo, ones_h1)], axis=0))

            def tile(j, state, masked):
                ks = pl.multiple_of(j * BLK, BLK)
                ks_ = _lane_blocks(k_ref[pl.ds(ks, BLK), :].astype(BF16), ch)
                vs_ = _lane_blocks(v_ref[pl.ds(ks, BLK), :].astype(BF16), ch)
                zs = [_dot_nt(q01[p], ks_[p]) for p in range(ch)]
                das = [_dot_nt(do01[p], vs_[p]) for p in range(ch)]
                logits = [_sb_logits(z, causal, masked) for z in zs]
                sums = [_suffix_sums(lg[0], suffix) for lg in logits]
                a_s, e_s = [], []
                for p in range(ch):
                    a = jnp.exp(logits[p][1] + state[3 * p] + sums[p][0])
                    if masked:
                        a = jnp.where(causal, a, 0.0)
                    a_s.append(a)
                    e_s.append(a * das[p])
                e_sums = [_suffix_sums(e, suffix) for e in e_s]
                out, dks, dvs = [], [], []
                for p in range(ch):
                    carry, rcarry, dq = state[3 * p:3 * p + 3]
                    e = e_s[p]
                    before = tot[p] - (rcarry + e_sums[p][0] + e)
                    beta = jnp.exp(logits[p][1])
                    dz = e * (1.0 - beta) - beta * before
                    if masked:
                        dz = jnp.where(causal, dz, 0.0)
                    dz_b = dz.astype(BF16)
                    dks.append(_dot_tn(dz_b, q01[p]))
                    dvs.append(_dot_tn(a_s[p].astype(BF16), do01[p]))
                    out += [carry + sums[p][1], rcarry + e_sums[p][1], dq + _dot(dz_b, ks_[p])]
                dk_acc[pl.ds(ks, BLK), :] += jnp.concatenate(dks, axis=1)
                dv_acc[pl.ds(ks, BLK), :] += jnp.concatenate(dvs, axis=1)
                return tuple(out)

            state = (jnp.zeros((2 * BLK, BLK), F32),) * (3 * ch)
            state = tile(i, state, True)
            state = _sb_walk_back(i, state, 3, tile)
            dq = [jnp.where(head0, state[3 * p + 2][:BLK], state[3 * p + 2][BLK:]) for p in range(ch)]
            dq_ref[pl.ds(qs, BLK), :] = (jnp.concatenate(dq, axis=1) * scale).astype(dq_ref.dtype)
            return 0

        lax.fori_loop(0, nq, q_block, 0)
        dk_ref[...] = dk_acc[...].astype(dk_ref.dtype)
        dv_ref[...] = dv_acc[...].astype(dv_ref.dtype)

    def body(*refs):
        step = pl.program_id(0) * n_steps + pl.program_id(1)
        n_c, n_o = len(carry.arrays), len(carry.out_shapes)
        own = refs[:5] + refs[5 + n_c:8 + n_c] + refs[8 + n_c + n_o:10 + n_c + n_o]
        carry.run(refs, 5, 3, step, b_sz * n_steps, lambda: compute(*own))

    blk = lambda off: pl.BlockSpec((None, s_len, ch * LANES), lambda b, p: (b, 0, off + p))
    once = lambda off: pl.BlockSpec((None, s_len, ch * LANES), lambda b, p: (b, 0, off + p),
                                    pipeline_mode=pl.Buffered(1))
    out_sd = jax.ShapeDtypeStruct((b_sz, s_len, SB_WIDTH), BF16)
    c_in, c_out, c_shapes, c_alias, c_sems = carry.call_args(5, 3)
    res = pl.pallas_call(
        body, name="sb_bwd", grid=(b_sz, n_steps),
        in_specs=[once(0), once(n_steps), once(2 * n_steps), once(0), once(0)] + c_in,
        out_specs=[blk(0), blk(0), blk(0)] + c_out, out_shape=[out_sd, out_sd, out_sd] + c_shapes,
        input_output_aliases=c_alias,
        scratch_shapes=[pltpu.VMEM((s_len, ch * LANES), F32), pltpu.VMEM((s_len, ch * LANES), F32)] + c_sems,
        compiler_params=pltpu.CompilerParams(dimension_semantics=("arbitrary", "arbitrary"),
                                             vmem_limit_bytes=VMEM_CAP),
    )(qkv, qkv, qkv, o_sb, do_sb, *carry.arrays)
    return res[:3], res[3:]


def _dil_consts(group, pair_idx, dilation):
    lane = lax.broadcasted_iota(jnp.int32, (BLK, LANES), 1)
    head0 = lane < HEAD_DIM
    row = lax.broadcasted_iota(jnp.int32, (2 * BLK, BLK), 0)
    qa = row % BLK
    kb = lax.broadcasted_iota(jnp.int32, (2 * BLK, BLK), 1)
    head = (group * DIL_HEADS_PER_GROUP + 2 * pair_idx + row // BLK).astype(F32)
    slope = jnp.exp((-ALIBI_MAX_BIAS * math.log(2.0) / DIL_HEADS) * (head + 1.0))
    valid_cur = kb <= qa
    valid_prev = kb >= qa
    bias_cur = -slope * ((qa - kb) * dilation).astype(F32)
    bias_prev = -slope * ((BLK + qa - kb) * dilation).astype(F32)
    return head0, valid_cur, valid_prev, bias_cur, bias_prev


def _dil_units(s_len, dilation):
    nb = s_len // dilation // BLK
    return [(r, n) for r in range(dilation) for n in range(nb)]


def _dil_rows(n, r, dilation):
    if dilation == 1:
        return pl.ds(n * BLK, BLK)
    return pl.ds(n * BLK * dilation + r, BLK, stride=dilation)


def _dil_scores(q01, k, bias, valid):
    s = _dot_nt(q01, k) * (1.0 / math.sqrt(HEAD_DIM)) + bias
    return jnp.where(valid, s, NEG)


def _dil_fwd(qkv, b_sz, s_len, carry):
    n_pairs = DIL_OUT_WIDTH // LANES
    q_off = 3 * SB_WIDTH // LANES
    per_kind = DIL_WIDTH // LANES

    def compute(pair_idx, qkv_refs, o_ref, lse_ref, m_s, l_s):
        m_s[...] = jnp.full(m_s.shape, NEG, F32)
        l_s[...] = jnp.zeros(l_s.shape, F32)
        o_ref[...] = jnp.zeros(o_ref.shape, F32)
        for g, (_, dilation) in enumerate(DIL_PAIRS):
            q_ref, k_ref, v_ref = qkv_refs[3 * g:3 * g + 3]
            head0, valid_cur, valid_prev, bias_cur, bias_prev = _dil_consts(g, pair_idx, dilation)
            units = _dil_units(s_len, dilation)
            for u0 in range(0, len(units), DIL_CHAINS):
                group = units[u0:u0 + DIL_CHAINS]
                rows_of = [_dil_rows(n, r, dilation) for r, n in group]
                scores, values = [], []
                for (r, n), rows in zip(group, rows_of):
                    q01 = _stack_heads(q_ref[rows, :].astype(BF16), head0)
                    sc = [_dil_scores(q01, k_ref[rows, :].astype(BF16), bias_cur, valid_cur)]
                    vals = [_stack_heads(v_ref[rows, :].astype(BF16), head0)]
                    if n > 0:
                        prev = _dil_rows(n - 1, r, dilation)
                        sc.append(_dil_scores(q01, k_ref[prev, :].astype(BF16), bias_prev, valid_prev))
                        vals.append(_stack_heads(v_ref[prev, :].astype(BF16), head0))
                    scores.append(sc)
                    values.append(vals)
                stats = []
                for sc, rows in zip(scores, rows_of):
                    m_blk = functools.reduce(jnp.maximum, [jnp.max(x, axis=-1, keepdims=True) for x in sc])
                    m_old = jnp.concatenate([m_s.at[0][rows, :], m_s.at[1][rows, :]], axis=0)
                    l_old = jnp.concatenate([l_s.at[0][rows, :], l_s.at[1][rows, :]], axis=0)
                    m_new = jnp.maximum(m_old, m_blk)
                    probs = [jnp.exp(x - m_new) for x in sc]
                    l_blk = functools.reduce(jnp.add, [jnp.sum(p, axis=-1, keepdims=True) for p in probs])
                    alpha = jnp.exp(m_old - m_new)
                    stats.append((m_new, alpha * l_old + l_blk, alpha, probs))
                for (m_new, l_new, alpha, probs), vals, rows in zip(stats, values, rows_of):
                    alpha_tok = jnp.where(head0, alpha[:BLK], alpha[BLK:])
                    p_cat = jnp.concatenate(
                        [h for p in probs for h in (p[:BLK].astype(BF16), p[BLK:].astype(BF16))], axis=1)
                    o_ref[rows, :] = alpha_tok * o_ref[rows, :] + _dot(p_cat, jnp.concatenate(vals, axis=0))
                    m_s.at[0][rows, :] = m_new[:BLK]
                    m_s.at[1][rows, :] = m_new[BLK:]
                    l_s.at[0][rows, :] = l_new[:BLK]
                    l_s.at[1][rows, :] = l_new[BLK:]
        lane = lax.broadcasted_iota(jnp.int32, (BLK, LANES), 1)
        for c in range(s_len // BLK):
            rows = pl.ds(c * BLK, BLK)
            l0, l1 = l_s.at[0][rows, :], l_s.at[1][rows, :]
            o_ref[rows, :] = o_ref[rows, :] / jnp.where(lane < HEAD_DIM, l0, l1)
            lse_ref.at[0][rows, :] = m_s.at[0][rows, :] + jnp.log(l0)
            lse_ref.at[1][rows, :] = m_s.at[1][rows, :] + jnp.log(l1)

    def body(*refs):
        pair_idx = pl.program_id(1)
        step = pl.program_id(0) * n_pairs + pair_idx
        n_c, n_o = len(carry.arrays), len(carry.out_shapes)
        o_ref, lse_ref = refs[9 + n_c:11 + n_c]
        m_s, l_s = refs[11 + n_c + n_o:13 + n_c + n_o]
        carry.run(refs, 9, 2, step, b_sz * n_pairs, lambda: compute(pair_idx, refs[:9], o_ref, lse_ref, m_s, l_s))

    in_specs = []
    for g in range(len(DIL_PAIRS)):
        for kind in range(3):
            off = q_off + kind * per_kind + g * n_pairs
            in_specs.append(pl.BlockSpec((None, s_len, LANES), lambda b, p, off=off: (b, 0, off + p)))
    c_in, c_out, c_shapes, c_alias, c_sems = carry.call_args(9, 2)
    res = pl.pallas_call(
        body, name="dil_fwd", grid=(b_sz, n_pairs),
        in_specs=in_specs + c_in,
        out_specs=[pl.BlockSpec((None, s_len, LANES), lambda b, p: (b, 0, p)),
                   pl.BlockSpec((None, None, 2, s_len, LANES), lambda b, p: (b, p, 0, 0, 0))] + c_out,
        out_shape=[jax.ShapeDtypeStruct((b_sz, s_len, DIL_OUT_WIDTH), F32),
                   jax.ShapeDtypeStruct((b_sz, n_pairs, 2, s_len, LANES), F32)] + c_shapes,
        input_output_aliases=c_alias,
        scratch_shapes=[pltpu.VMEM((2, s_len, LANES), F32), pltpu.VMEM((2, s_len, LANES), F32)] + c_sems,
        compiler_params=pltpu.CompilerParams(dimension_semantics=("arbitrary", "arbitrary"),
                                             vmem_limit_bytes=VMEM_CAP),
    )(*([qkv] * 9), *carry.arrays)
    return res[0], res[1], res[2:]


def _dil_bwd(qkv, o_dl, lse, do_dl, b_sz, s_len, carry):
    n_pairs = DIL_OUT_WIDTH // LANES
    n_groups = len(DIL_PAIRS)
    q_off = 3 * SB_WIDTH // LANES
    per_kind = DIL_WIDTH // LANES

    def compute(pair_idx, group, q_ref, k_ref, v_ref, o_ref, lse_ref, do_ref, dq_ref, dk_ref, dv_ref, d_s, dq_s, dk_s, dv_s):
        lrow = lax.broadcasted_iota(jnp.int32, (LANES, LANES), 0)
        ones_h0 = jnp.where(lrow < HEAD_DIM, 1.0, 0.0).astype(BF16)
        ones_h1 = jnp.where(lrow >= HEAD_DIM, 1.0, 0.0).astype(BF16)
        for c in range(s_len // BLK):
            rows = pl.ds(c * BLK, BLK)
            dd_hi, dd_lo = _split2(do_ref[rows, :] * o_ref[rows, :])
            d_s.at[0][rows, :] = _dot(dd_hi, ones_h0) + _dot(dd_lo, ones_h0)
            d_s.at[1][rows, :] = _dot(dd_hi, ones_h1) + _dot(dd_lo, ones_h1)
        dk_s[...] = jnp.zeros(dk_s.shape, F32)
        dv_s[...] = jnp.zeros(dv_s.shape, F32)

        def one_group(g, dilation):
            head0, valid_cur, valid_prev, bias_cur, bias_prev = _dil_consts(g, pair_idx, dilation)
            units = _dil_units(s_len, dilation)
            scale = 1.0 / math.sqrt(HEAD_DIM)
            for u0 in range(0, len(units), DIL_CHAINS):
                chunk = units[u0:u0 + DIL_CHAINS]
                loaded = []
                for r, n in chunk:
                    rows = _dil_rows(n, r, dilation)
                    q01 = _stack_heads(q_ref[rows, :].astype(BF16), head0)
                    do01 = _stack_heads(do_ref[rows, :].astype(BF16), head0)
                    lse01 = jnp.concatenate([lse_ref.at[0][rows, :], lse_ref.at[1][rows, :]], axis=0)
                    d01 = jnp.concatenate([d_s.at[0][rows, :], d_s.at[1][rows, :]], axis=0)
                    blocks = [(rows, bias_cur, valid_cur)]
                    if n > 0:
                        blocks.append((_dil_rows(n - 1, r, dilation), bias_prev, valid_prev))
                    parts = []
                    for krows, bias, valid in blocks:
                        k = k_ref[krows, :].astype(BF16)
                        v = v_ref[krows, :].astype(BF16)
                        parts.append((krows, k, _dil_scores(q01, k, bias, valid), _dot_nt(do01, v)))
                    loaded.append((rows, q01, do01, lse01, d01, parts))
                grads = []
                for rows, q01, do01, lse01, d01, parts in loaded:
                    for krows, k, sc, dp in parts:
                        p = jnp.exp(sc - lse01)
                        grads.append((p.astype(BF16), (p * (dp - d01) * scale).astype(BF16)))
                it = iter(grads)
                updates = []
                for rows, q01, do01, lse01, d01, parts in loaded:
                    dq = jnp.zeros((2 * BLK, LANES), F32)
                    for krows, k, sc, dp in parts:
                        p_b, ds = next(it)
                        dq = dq + _dot(ds, k)
                        updates.append((krows, _dot_tn(ds, q01), _dot_tn(p_b, do01)))
                    dq_s[rows, :] = jnp.where(head0, dq[:BLK], dq[BLK:])
                for krows, dk, dv in updates:
                    dk_s[krows, :] = dk_s[krows, :] + dk
                    dv_s[krows, :] = dv_s[krows, :] + dv

        for g, (_, dilation) in enumerate(DIL_PAIRS):
            pl.when(group == g)(functools.partial(one_group, g, dilation))
        dq_ref[...] = dq_s[...].astype(dq_ref.dtype)
        dk_ref[...] = dk_s[...].astype(dk_ref.dtype)
        dv_ref[...] = dv_s[...].astype(dv_ref.dtype)

    def body(*refs):
        pair_idx, group = pl.program_id(1), pl.program_id(2)
        step = (pl.program_id(0) * n_pairs + pair_idx) * n_groups + group
        n_c, n_o = len(carry.arrays), len(carry.out_shapes)
        own = refs[:6] + refs[6 + n_c:9 + n_c] + refs[9 + n_c + n_o:13 + n_c + n_o]
        carry.run(refs, 6, 3, step, b_sz * n_pairs * n_groups, lambda: compute(pair_idx, group, *own))

    def qkv_spec(kind):
        return pl.BlockSpec((None, s_len, LANES),
                            lambda b, p, g: (b, 0, q_off + kind * per_kind + g * n_pairs + p))

    tok_spec = pl.BlockSpec((None, s_len, LANES), lambda b, p, g: (b, 0, p))
    out_spec = pl.BlockSpec((None, s_len, LANES), lambda b, p, g: (b, 0, g * n_pairs + p))
    out_sd = jax.ShapeDtypeStruct((b_sz, s_len, DIL_WIDTH), BF16)
    c_in, c_out, c_shapes, c_alias, c_sems = carry.call_args(6, 3)
    res = pl.pallas_call(
        body, name="dil_bwd", grid=(b_sz, n_pairs, n_groups),
        in_specs=[qkv_spec(0), qkv_spec(1), qkv_spec(2), tok_spec,
                  pl.BlockSpec((None, None, 2, s_len, LANES), lambda b, p, g: (b, p, 0, 0, 0)), tok_spec] + c_in,
        out_specs=[out_spec, out_spec, out_spec] + c_out,
        out_shape=[out_sd, out_sd, out_sd] + c_shapes,
        input_output_aliases=c_alias,
        scratch_shapes=[pltpu.VMEM((2, s_len, LANES), F32)] + [pltpu.VMEM((s_len, LANES), F32)] * 3 + c_sems,
        compiler_params=pltpu.CompilerParams(dimension_semantics=("arbitrary", "arbitrary", "arbitrary"),
                                             vmem_limit_bytes=VMEM_CAP),
    )(qkv, qkv, qkv, o_dl, lse, do_dl, *carry.arrays)
    return res[:3], res[3:]


def _mesh_pos():
    return lax.axis_index("x"), lax.axis_index("y"), lax.axis_index("c")


def _other_chips(x, y):
    return [(1 - x, y), (x, 1 - y), (1 - x, 1 - y)]


def _hbm_specs(n):
    return [pl.BlockSpec(memory_space=pl.ANY)] * n


SWAPPED = ("w_ffn_in",)


def _slot(x, y, swapped):
    return 2 * y + x if swapped else 2 * x + y


def _cast_to_slab(w, name, swapped=False):
    rows, cols = w.shape
    mine = jnp.reshape(_slot(lax.axis_index("x"), lax.axis_index("y"), swapped), (1,)).astype(jnp.int32)

    def body(idx_ref, w_ref, o_ref):
        o_ref[...] = w_ref[...].astype(BF16)

    return pl.pallas_call(
        body, name=name,
        grid_spec=pltpu.PrefetchScalarGridSpec(
            num_scalar_prefetch=1, grid=(1,),
            in_specs=[pl.BlockSpec((rows, cols), lambda i, idx: (0, 0))],
            out_specs=pl.BlockSpec((None, rows, cols), lambda i, idx: (idx[0], 0, 0))),
        out_shape=_hbm_array((N_CHIPS, rows, cols), BF16),
        compiler_params=pltpu.CompilerParams(vmem_limit_bytes=_vmem_limit(rows * cols * 6)),
    )(mine, w)


def _gather_issue(slabs, send_sems, recv_sems, swapped):
    x, y, c = _mesh_pos()
    for k, slab in enumerate(slabs):
        half = slab.shape[1] // 2
        rows = slab.at[_slot(x, y, swapped[k]), pl.ds(c * half, half), :]
        for r, (px, py) in enumerate(_other_chips(x, y)):
            pltpu.make_async_remote_copy(
                src_ref=rows, dst_ref=rows, send_sem=send_sems.at[6 * k + r], recv_sem=recv_sems.at[6 * k + r],
                device_id=(px, py, c), device_id_type=MESH).start()


def _gather_complete(slabs, send_sems, recv_sems, swapped):
    x, y, c = _mesh_pos()
    chips = _other_chips(x, y)

    def copy(k, sem, block, rows, to):
        ref = slabs[k].at[block, rows, :]
        return pltpu.make_async_remote_copy(
            src_ref=ref, dst_ref=ref, send_sem=send_sems.at[sem], recv_sem=recv_sems.at[sem],
            device_id=to, device_id_type=MESH)

    for k, slab in enumerate(slabs):
        half = slab.shape[1] // 2
        for r, (px, py) in enumerate(chips):
            theirs = _slot(px, py, swapped[k])
            copy(k, 6 * k + r, theirs, pl.ds(c * half, half), (px, py, c)).wait_recv()
            copy(k, 6 * k + 3 + r, theirs, pl.ds(c * half, half), (x, y, 1 - c)).start()
    for k, slab in enumerate(slabs):
        half = slab.shape[1] // 2
        for r, (px, py) in enumerate(chips):
            copy(k, 6 * k + 3 + r, _slot(px, py, swapped[k]), pl.ds((1 - c) * half, half), (x, y, 1 - c)).wait_recv()
    for k, slab in enumerate(slabs):
        half = slab.shape[1] // 2
        for r, (px, py) in enumerate(chips):
            copy(k, 6 * k + r, _slot(x, y, swapped[k]), pl.ds(c * half, half), (px, py, c)).wait_send()
            copy(k, 6 * k + 3 + r, _slot(px, py, swapped[k]), pl.ds(c * half, half), (x, y, 1 - c)).wait_send()


def _gather_sems(n):
    return [pltpu.SemaphoreType.DMA((6 * n,)), pltpu.SemaphoreType.DMA((6 * n,))]


def _gather_carry(slabs, names):
    swapped = [k in SWAPPED for k in names]
    return _Carry(slabs, [_hbm_array(a.shape, a.dtype) for a in slabs], True, _gather_sems(len(slabs)),
                  lambda ins, outs, sems: _gather_issue(outs, *sems, swapped),
                  lambda ins, outs, sems: _gather_complete(outs, *sems, swapped))


def _pair_copies(ins, outs, send_sems, recv_sems):
    x, y, c = _mesh_pos()
    copies = []
    for k, g in enumerate(ins):
        half = g.shape[1] // 2
        copies.append(pltpu.make_async_remote_copy(
            src_ref=g.at[:, pl.ds((1 - c) * half, half), :], dst_ref=outs[k],
            send_sem=send_sems.at[k], recv_sem=recv_sems.at[k],
            device_id=(x, y, 1 - c), device_id_type=MESH))
    return copies


def _pair_carry(grads):
    n = len(grads)

    def start(ins, outs, sems):
        for cp in _pair_copies(ins, outs, *sems):
            cp.start()

    def finish(ins, outs, sems):
        for cp in _pair_copies(ins, outs, *sems):
            cp.wait()

    return _Carry(grads, [jax.ShapeDtypeStruct((N_CHIPS, g.shape[1] // 2, g.shape[2]), g.dtype) for g in grads], False,
                  [pltpu.SemaphoreType.DMA((n,)), pltpu.SemaphoreType.DMA((n,))], start, finish)


def _pair_exchange(grads, tag):
    carry = _pair_carry(grads)
    n = len(grads)

    def body(*refs):
        carry.start(refs[:n], refs[n:2 * n], refs[2 * n:])
        carry.finish(refs[:n], refs[n:2 * n], refs[2 * n:])

    return pl.pallas_call(
        body, name="grad_pair_exchange_" + tag, in_specs=_hbm_specs(n), out_specs=_hbm_specs(n),
        out_shape=carry.out_shapes, scratch_shapes=carry.sems,
    )(*grads)


def _pair_sum(grad, other, name, swapped):
    _, rows, cols = grad.shape
    half = rows // 2
    x, y, c = _mesh_pos()
    idx = jnp.stack([c, _slot(x, y, swapped)]).astype(jnp.int32)

    def body(idx_ref, g_ref, p_ref, own_ref, sb_ref):
        s = g_ref[...] + p_ref[...].astype(F32)
        sb_ref[...] = s.astype(BF16)

        @pl.when(pl.program_id(0) == idx_ref[1])
        def _():
            own_ref[...] = s

    blk = pl.BlockSpec((None, half, cols), lambda p, idx: (p, 0, 0))
    return pl.pallas_call(
        body, name=name,
        grid_spec=pltpu.PrefetchScalarGridSpec(
            num_scalar_prefetch=1, grid=(N_CHIPS,),
            in_specs=[pl.BlockSpec((None, half, cols), lambda p, idx: (p, idx[0], 0)), blk],
            out_specs=[pl.BlockSpec((half, cols), lambda p, idx: (0, 0)), blk]),
        out_shape=[jax.ShapeDtypeStruct((half, cols), F32), jax.ShapeDtypeStruct((N_CHIPS, half, cols), BF16)],
        compiler_params=pltpu.CompilerParams(dimension_semantics=("arbitrary",),
                                             vmem_limit_bytes=_vmem_limit(4 * half * cols * 4)),
    )(idx, grad, other)


def _chip_copies(sums_bf16, lands, send_sems, recv_sems, swapped):
    x, y, c = _mesh_pos()
    return [pltpu.make_async_remote_copy(
        src_ref=sums_bf16[k].at[_slot(px, py, swapped[k])], dst_ref=lands[k].at[r],
        send_sem=send_sems.at[3 * k + r], recv_sem=recv_sems.at[3 * k + r],
        device_id=(px, py, c), device_id_type=MESH)
        for k in range(len(sums_bf16)) for r, (px, py) in enumerate(_other_chips(x, y))]


def _chip_carry(sums_bf16, names):
    swapped = [k in SWAPPED for k in names]

    def start(ins, outs, sems):
        for cp in _chip_copies(ins, outs, *sems, swapped):
            cp.start()

    def finish(ins, outs, sems):
        for cp in _chip_copies(ins, outs, *sems, swapped):
            cp.wait()

    return _Carry(sums_bf16, _chip_landing(sums_bf16), False, _chip_sems(len(sums_bf16)), start, finish)


def _chip_sems(n):
    return [pltpu.SemaphoreType.DMA((3 * n,)), pltpu.SemaphoreType.DMA((3 * n,))]


def _chip_landing(sums_bf16):
    return [jax.ShapeDtypeStruct((N_CHIPS - 1,) + s.shape[1:], BF16) for s in sums_bf16]


def _chip_sum(own, landed, name):
    rows, cols = own.shape
    core = jnp.reshape(lax.axis_index("c"), (1,)).astype(jnp.int32)

    def body(core_ref, o_ref, l_ref, out_ref):
        out_ref[...] = ((o_ref[...] + l_ref[0].astype(F32)) + l_ref[1].astype(F32)) + l_ref[2].astype(F32)

    return pl.pallas_call(
        body, name=name,
        grid_spec=pltpu.PrefetchScalarGridSpec(
            num_scalar_prefetch=1, grid=(1,),
            in_specs=[pl.BlockSpec((rows, cols), lambda i, core_ref: (0, 0)),
                      pl.BlockSpec((N_CHIPS - 1, rows, cols), lambda i, core_ref: (0, 0, 0))],
            out_specs=pl.BlockSpec((rows, cols), lambda i, core_ref: (core_ref[0], 0))),
        out_shape=jax.ShapeDtypeStruct((2 * rows, cols), F32),
        compiler_params=pltpu.CompilerParams(vmem_limit_bytes=_vmem_limit(3 * rows * cols * 4)),
    )(core, own, landed)


def _final_exchange(fulls, v):
    n = len(fulls)
    rows, cols = v.shape
    n_dev = 8

    def body(*refs):
        v_ref, out_ref = refs[0], refs[1 + 2 * n]
        outs = refs[1 + n:1 + 2 * n]
        buf, v_send, v_recv, h_send, h_recv = refs[2 + 2 * n:]
        x, y, c = _mesh_pos()
        me = 4 * x + 2 * y + c
        buf[me] = v_ref[...]
        peers = [(1 - x if r & 4 else x, 1 - y if r & 2 else y, 1 - c if r & 1 else c) for r in range(1, n_dev)]
        copies = []
        for r, peer in enumerate(peers):
            copies.append(pltpu.make_async_remote_copy(
                src_ref=v_ref, dst_ref=buf.at[me], send_sem=v_send.at[r], recv_sem=v_recv.at[r],
                device_id=peer, device_id_type=MESH))
        for k in range(n):
            half = fulls[k].shape[0] // 2
            mine = outs[k].at[pl.ds(c * half, half), :]
            copies.append(pltpu.make_async_remote_copy(
                src_ref=mine, dst_ref=mine, send_sem=h_send.at[k], recv_sem=h_recv.at[k],
                device_id=(x, y, 1 - c), device_id_type=MESH))
        for cp in copies:
            cp.start()
        for r, (px, py, pc) in enumerate(peers):
            pltpu.make_async_remote_copy(
                src_ref=v_ref, dst_ref=buf.at[4 * px + 2 * py + pc], send_sem=v_send.at[r], recv_sem=v_recv.at[r],
                device_id=(px, py, pc), device_id_type=MESH).wait_recv()
        for k in range(n):
            half = fulls[k].shape[0] // 2
            theirs = outs[k].at[pl.ds((1 - c) * half, half), :]
            pltpu.make_async_remote_copy(
                src_ref=theirs, dst_ref=theirs, send_sem=h_send.at[k], recv_sem=h_recv.at[k],
                device_id=(x, y, 1 - c), device_id_type=MESH).wait_recv()
        for cp in copies:
            cp.wait_send()
        acc = buf[0]
        for d in range(1, n_dev):
            acc = acc + buf[d]
        out_ref[...] = acc
        out_ref[3:4, :] = jnp.broadcast_to(jnp.sum(acc[3:4, :], axis=1, keepdims=True), (1, cols))

    vm = pl.BlockSpec(memory_space=pltpu.VMEM)
    res = pl.pallas_call(
        body, name="final_exchange",
        in_specs=[vm] + _hbm_specs(n), out_specs=_hbm_specs(n) + [vm],
        out_shape=[jax.ShapeDtypeStruct(f.shape, F32) for f in fulls] + [jax.ShapeDtypeStruct((rows, cols), F32)],
        input_output_aliases={1 + k: k for k in range(n)},
        scratch_shapes=[pltpu.VMEM((n_dev, rows, cols), F32),
                        pltpu.SemaphoreType.DMA((n_dev - 1,)), pltpu.SemaphoreType.DMA((n_dev - 1,)),
                        pltpu.SemaphoreType.DMA((n,)), pltpu.SemaphoreType.DMA((n,))],
    )(v, *fulls)
    return res[:n], res[n]


def _adamw_math(w, g, m, v):
    m = ADAM_B1 * m + (1.0 - ADAM_B1) * g
    v = ADAM_B2 * v + (1.0 - ADAM_B2) * (g * g)
    m_hat = m / (1.0 - ADAM_B1 ** ADAM_STEP)
    v_hat = v / (1.0 - ADAM_B2 ** ADAM_STEP)
    delta = -ADAM_LR * (m_hat / (jnp.sqrt(v_hat) + ADAM_EPS) + ADAM_WD * w)
    return delta, m, v


def _adamw(w, g, m, v, name):
    rows, cols = w.shape
    tm = rows // 2 if (rows // 2) % 8 == 0 else rows
    return _rowwise(_adamw_math, [w, g, m, v], [], [(cols, F32)] * 3, [], tm=tm, name=name)


def _unshard_cols(gathered):
    n, r, c = gathered.shape
    return jnp.transpose(gathered, (1, 0, 2)).reshape(r, n * c)


def _shard_cols(full):
    r, nc = full.shape
    return jnp.transpose(full.reshape(r, N_CHIPS, nc // N_CHIPS), (1, 0, 2))


LATE = ["w_sb_up", "w_dil_up", "w_out", "w_ffn_in", "w_ffn_out"]


def _late_weights(slabs, d_model, d_ff):
    g = dict(zip(LATE, slabs))
    return (_unshard_cols(g["w_sb_up"]), _unshard_cols(g["w_dil_up"]), g["w_out"].reshape(d_model, d_model),
            _unshard_cols(g["w_ffn_in"]), g["w_ffn_out"].reshape(d_ff, d_model))


ROW_SHARDED = ("w_in", "w_out", "w_ffn_in", "w_ffn_out")


def _chip_major(grads):
    out = []
    for k, g in grads.items():
        if k in ROW_SHARDED:
            out.append(g.reshape(N_CHIPS, g.shape[0] // N_CHIPS, g.shape[1]))
        else:
            out.append(_shard_cols(g))
    return out


def _pair_sums(full, others, names):
    return [_pair_sum(g, o, "grad_pair_sum_" + k, k in SWAPPED) for g, o, k in zip(full, others, names)]


def _chip_sums(pair, landed, names):
    return {k: _chip_sum(p[0], l, "grad_chip_sum_" + k) for p, l, k in zip(pair, landed, names)}


def _fwd_bwd(x, loss_target, g_mix, g_ffn, g_fin, slab_in, late_slabs):
    b_sz, s_len, d_model = x.shape
    t = b_sz * s_len
    d_ff = late_slabs[-1].shape[1] * N_CHIPS
    x2d = x.reshape(t, d_model)
    tgt2d = loss_target.reshape(t, d_model)

    u, (slab_in,) = _rowwise(lambda xv, g: (_rms_stats(xv)[0] * g,), [x2d], [g_mix], [(d_model, BF16)], [], tm=512,
                             name="norm_mix", carry=_gather_carry([slab_in], ["w_in"]), out_type=_hbm_array)
    u, wt_in = _in_hbm(u), _in_hbm(slab_in.reshape(-1, d_model))
    qkv, (slab_ffn_out,) = _mm(u, wt_in, tb=True, b_cols=(0, QKV_WIDTH), tm=2048, tn=768, tk=d_model, name="proj_qkv",
                               carry=_gather_carry(late_slabs[4:], LATE[4:]))
    gates = _mm(u, wt_in, tb=True, b_cols=(QKV_WIDTH, 2 * d_model), out_dtype=BF16, tm=t, tn=256, tk=d_model,
                name="proj_gates")
    qkv3 = qkv.reshape(b_sz, s_len, QKV_WIDTH)
    o_sb, (slab_ffn_in,) = _sb_fwd(qkv3, b_sz, s_len, _gather_carry(late_slabs[3:4], LATE[3:4]))
    o_dl, lse, small_slabs = _dil_fwd(qkv3, b_sz, s_len, _gather_carry(late_slabs[:3], LATE[:3]))
    wf_sb_up, wf_dil_up, wf_out, wf_ffn_in, wf_ffn_out = _late_weights(
        list(small_slabs) + [slab_ffn_in, slab_ffn_out], d_model, d_ff)
    o_sb2, o_dl2 = o_sb.reshape(t, SB_WIDTH), o_dl.reshape(t, DIL_OUT_WIDTH)
    y_sb = _mm(o_sb2, wf_sb_up, out_dtype=BF16, tm=1024, tn=1024, tk=SB_WIDTH, name="sb_up")
    y_dl = _mm(o_dl2, wf_dil_up, out_dtype=BF16, tm=1024, tn=1024, tk=DIL_OUT_WIDTH, name="dil_up")

    def merge_fn(gt, ys, yd):
        return (_sigmoid(gt[:, :d_model]) * ys + _sigmoid(gt[:, d_model:]) * yd,)

    (merged,) = _rowwise(merge_fn, [gates, y_sb, y_dl], [], [(d_model, BF16)], [], tm=512, name="merge")
    x1 = _mm(merged, wf_out, add=x2d, tm=512, tn=1024, tk=d_model, name="mix_out")
    (u2,) = _rowwise(lambda xv, g: (_rms_stats(xv)[0] * g,), [x1], [g_ffn], [(d_model, BF16)], [], tm=512, name="norm_ffn",
                     out_type=_hbm_array)
    u2 = _in_hbm(u2)
    half_ff = d_ff // 2

    def act_fn(hv):
        gate = hv[:, :half_ff]
        return hv, gate * _sigmoid(gate) * hv[:, half_ff:]

    h, act = _mm(u2, wf_ffn_in, tm=512, tn=d_ff, tk=d_model, name="ffn_in",
                 epilogue=(act_fn, [], [], [(d_ff, BF16), (half_ff, BF16)], []))
    def head_fn(xv, tg, g):
        xhat, r = _rms_stats(xv)
        err = xhat * g - tg
        dy = err * (1.0 / d_model)
        dx, dg_rows = _rms_bwd(dy, xhat, r, g)
        loss_lanes = (0.5 / d_model) * jnp.sum(err * err, axis=0, keepdims=True)
        return dx, dx, jnp.sum(dg_rows, axis=0, keepdims=True), loss_lanes

    dx2, dx2_b, dg_fin, loss_lanes = _mm(
        act, wf_ffn_out, add=x1, tm=512, tn=1024, tk=d_ff, name="ffn_out",
        epilogue=(head_fn, [tgt2d], [g_fin], [(d_model, F32), (d_model, BF16)], [(1, d_model), (1, d_model)]))

    def dact_fn(da, hv):
        gate, up = hv[:, :half_ff], hv[:, half_ff:]
        sg = _sigmoid(gate)
        dgate = da * up * (sg * (1.0 + gate * (1.0 - sg)))
        return (jnp.concatenate([dgate, da * (gate * sg)], axis=1),)

    (dh,) = _mm(dx2_b, wf_ffn_out, tb=True, tm=512, tn=half_ff, tk=d_model, name="ffn_out_dx",
                epilogue=(dact_fn, [h], [], [(d_ff, BF16)], []))
    gw_ffn_out = _mm(act, dx2_b, ta=True, tm=256, tn=d_model, tk=t, name="ffn_out_dw")
    def norm_bwd_fn(du_, dres, xv, g):
        xhat, r = _rms_stats(xv)
        dx, dg_rows = _rms_bwd(du_, xhat, r, g)
        return dres + dx, jnp.sum(dg_rows, axis=0, keepdims=True)

    def norm_bwd_twice(*args):
        dx, dg = norm_bwd_fn(*args)
        return dx, dx, dg

    dx1, dx1_b, dg_ffn = _mm(dh, wf_ffn_in, tb=True, tm=512, tn=1024, tk=2 * d_ff, name="ffn_in_dx",
                             epilogue=(norm_bwd_twice, [dx2, x1], [g_ffn], [(d_model, F32), (d_model, BF16)], [(1, d_model)]))
    gwt_ffn_in = _mm(dh, u2, ta=True, tm=512, tn=d_model, tk=t, name="ffn_in_dw")

    dmerged = _mm(dx1_b, wf_out, tb=True, out_dtype=BF16, tm=512, tn=1024, tk=d_model, name="mix_out_dx")
    gw_out = _mm(merged, dx1_b, ta=True, tm=256, tn=d_model, tk=t, name="mix_out_dw")

    def merge_bwd_fn(gt, ys, yd, dm):
        s_sb, s_dl = _sigmoid(gt[:, :d_model]), _sigmoid(gt[:, d_model:])
        dgates = jnp.concatenate([dm * ys * s_sb * (1.0 - s_sb), dm * yd * s_dl * (1.0 - s_dl)], axis=1)
        return dgates, dm * s_sb, dm * s_dl

    full_big = _chip_major({"w_out": gw_out, "w_ffn_in": gwt_ffn_in, "w_ffn_out": gw_ffn_out})
    dgates, dy_sb, dy_dl, others_big = _rowwise(
        merge_bwd_fn, [gates, y_sb, y_dl, dmerged], [], [(2 * d_model, BF16), (d_model, BF16), (d_model, BF16)], [],
        tm=256, name="merge_bwd", carry=_pair_carry(full_big))
    pair_big = _pair_sums(full_big, others_big, LATE[2:])
    do_sb = _mm(dy_sb, wf_sb_up, tb=True, out_dtype=BF16, tm=1024, tn=SB_WIDTH, tk=d_model, name="sb_up_dx")
    gw_sb_up = _mm(o_sb2, dy_sb, ta=True, tm=SB_WIDTH, tn=1024, tk=512, name="sb_up_dw")
    do_dl = _mm(dy_dl, wf_dil_up, tb=True, tm=1024, tn=DIL_OUT_WIDTH, tk=d_model, name="dil_up_dx")
    gw_dil_up = _mm(o_dl2, dy_dl, ta=True, tm=DIL_OUT_WIDTH, tn=1024, tk=512, name="dil_up_dw")
    full_small = _chip_major({"w_sb_up": gw_sb_up, "w_dil_up": gw_dil_up})
    (dq_sb, dk_sb, dv_sb), brought = _sb_bwd(
        qkv3, o_sb, do_sb.reshape(b_sz, s_len, SB_WIDTH), b_sz, s_len,
        _chip_carry([p[1] for p in pair_big[:2]], LATE[2:4]) + _pair_carry(full_small))
    pair_small = _pair_sums(full_small, brought[2:], LATE[:2])
    (dq_dl, dk_dl, dv_dl), landed_b = _dil_bwd(
        qkv3, o_dl, lse, do_dl.reshape(b_sz, s_len, DIL_OUT_WIDTH), b_sz, s_len,
        _chip_carry([pair_big[2][1], pair_small[0][1], pair_small[1][1]], [LATE[4], LATE[0], LATE[1]]))
    pair = pair_small + pair_big
    landed = [landed_b[1], landed_b[2], brought[0], brought[1], landed_b[0]]
    dproj = [a.reshape(t, -1) for a in (dq_sb, dk_sb, dv_sb, dq_dl, dk_dl, dv_dl)] + [dgates]
    gwt_in, gwt_in_b = _mm(dproj, u, ta=True, tm=256, tn=d_model, tk=t, name="proj_dw",
                           epilogue=(lambda tile: (tile, tile), [], [], [(d_model, F32), (d_model, BF16)], []))
    full_in = _chip_major({"w_in": gwt_in})
    pair_in = _pair_sums(full_in, _pair_exchange(_chip_major({"w_in": gwt_in_b}), "w_in"), ["w_in"])
    (dx, dg_mix), landed_in = _mm(
        dproj, wt_in, tm=512, tn=1024, tk=wt_in.shape[0], name="proj_dx",
        carry=_chip_carry([p[1] for p in pair_in], ["w_in"]),
        epilogue=(norm_bwd_fn, [dx1, x2d], [g_mix], [(d_model, F32)], [(1, d_model)]))

    grads = _chip_sums(pair, landed, LATE)
    grads.update(_chip_sums(pair_in, landed_in, ["w_in"]))
    return dx, grads, dg_mix, dg_ffn, dg_fin, loss_lanes


def kernel(x, norm_mix_g, w_in, w_sb_up, w_dil_up, w_out, norm_ffn_g, w_ffn_in, w_ffn_out, norm_final_g, loss_target, m_norm_mix_g, m_w_in, m_w_sb_up, m_w_dil_up, m_w_out, m_norm_ffn_g, m_w_ffn_in, m_w_ffn_out, m_norm_final_g, v_norm_mix_g, v_w_in, v_w_sb_up, v_w_dil_up, v_w_out, v_norm_ffn_g, v_w_ffn_in, v_w_ffn_out, v_norm_final_g):
    b_sz, s_len, d_model = x.shape
    d_ff = w_ffn_out.shape[1] * N_CHIPS
    g_mix, g_ffn, g_fin = norm_mix_g, norm_ffn_g, norm_final_g.reshape(1, d_model)

    names = ["w_in", "w_sb_up", "w_dil_up", "w_out", "w_ffn_in", "w_ffn_out"]
    shards = {"w_in": jnp.swapaxes(w_in[0], 0, 1), "w_sb_up": w_sb_up[0], "w_dil_up": w_dil_up[0], "w_out": w_out[0],
              "w_ffn_in": w_ffn_in[0], "w_ffn_out": w_ffn_out[0]}
    slab_in = _cast_to_slab(shards["w_in"], "cast_w_in")
    late_slabs = [_cast_to_slab(shards[k], "cast_" + k, k in SWAPPED) for k in LATE]

    dx, grads, dg_mix, dg_ffn, dg_fin, loss_lanes = _fwd_bwd(
        x, loss_target, g_mix, g_ffn, g_fin, slab_in, late_slabs)

    small = jnp.concatenate([dg_mix, dg_ffn, dg_fin, loss_lanes, jnp.zeros((4, d_model), F32)], axis=0)
    full_grads, small = _final_exchange([grads[k] for k in names], small)
    grads = dict(zip(names, full_grads))
    grads["w_ffn_in"] = jnp.swapaxes(grads["w_ffn_in"], 0, 1)
    loss = small[3, 0]
    gains = jnp.concatenate([g_mix, g_ffn, g_fin, jnp.zeros((5, d_model), F32)], axis=0)
    gains_m = jnp.concatenate([m_norm_mix_g, m_norm_ffn_g, m_norm_final_g.reshape(1, d_model), jnp.zeros((5, d_model), F32)], axis=0)
    gains_v = jnp.concatenate([v_norm_mix_g, v_norm_ffn_g, v_norm_final_g.reshape(1, d_model), jnp.ones((5, d_model), F32)], axis=0)
    gd, gm, gv = _rowwise(_adamw_math, [gains, small, gains_m, gains_v], [], [(d_model, F32)] * 3, [], tm=8, name="adamw_gains")

    moments = {"w_in": (jnp.swapaxes(m_w_in[0], 0, 1), jnp.swapaxes(v_w_in[0], 0, 1)),
               "w_sb_up": (m_w_sb_up[0], v_w_sb_up[0]), "w_dil_up": (m_w_dil_up[0], v_w_dil_up[0]),
               "w_out": (m_w_out[0], v_w_out[0]), "w_ffn_in": (m_w_ffn_in[0], v_w_ffn_in[0]),
               "w_ffn_out": (m_w_ffn_out[0], v_w_ffn_out[0])}
    upd = {k: _adamw(shards[k], grads[k], moments[k][0], moments[k][1], "adamw_" + k) for k in names}

    def as_output(k, a):
        return (jnp.swapaxes(a, 0, 1) if k == "w_in" else a)[None]

    def w_out_of(i):
        return [as_output(k, upd[k][i]) for k in names]

    def ordered(mix, ws, ffn_g, fin):
        return [mix, ws[0], ws[1], ws[2], ws[3], ffn_g, ws[4], ws[5], fin]

    grad_ws = [as_output(k, grads[k]) for k in names]
    outs = [loss, dx.reshape(b_sz, s_len, d_model)]
    outs += ordered(small[0:1], grad_ws, small[1:2], small[2])
    outs += ordered(gd[0:1], w_out_of(0), gd[1:2], gd[2])
    outs += ordered(gm[0:1], w_out_of(1), gm[1:2], gm[2])
    outs += ordered(gv[0:1], w_out_of(2), gv[1:2], gv[2])
    return tuple(outs)
```

```python
import functools
import math

import jax
import jax.numpy as jnp
from jax import lax
from jax.experimental import pallas as pl
from jax.experimental.pallas import tpu as pltpu

F32 = jnp.float32
BF16 = jnp.bfloat16
MESH = pl.DeviceIdType.MESH

HEAD_DIM = 64
SB_HEADS = 8
DIL_PAIRS = ((128, 1), (512, 4), (2048, 16))
DIL_HEADS_PER_GROUP = 4
DIL_HEADS = DIL_HEADS_PER_GROUP * len(DIL_PAIRS)
SB_WIDTH = SB_HEADS * HEAD_DIM
DIL_WIDTH = DIL_HEADS * HEAD_DIM
DIL_OUT_WIDTH = DIL_HEADS_PER_GROUP * HEAD_DIM
QKV_WIDTH = 3 * SB_WIDTH + 3 * DIL_WIDTH
RMS_EPS = 1e-6
ALIBI_MAX_BIAS = 8.0
ADAM_LR = 0.001
ADAM_B1 = 0.9
ADAM_B2 = 0.999
ADAM_EPS = 1e-08
ADAM_WD = 0.01
ADAM_STEP = 10

LANES = 128
BLK = 128
NEG = -1e30
EXP_UNDERFLOW = -104.0
SB_FWD_CHAINS = 4
SB_BWD_CHAINS = 4
DIL_CHAINS = 4
N_CHIPS = 4
VMEM_CAP = 56 * 1024 * 1024


def _vmem_limit(tile_bytes):
    return int(min(VMEM_CAP, max(32 * 1024 * 1024, 3 * tile_bytes + 8 * 1024 * 1024)))


def _hbm_array(shape, dtype):
    return pltpu.HBM(shape, dtype)


def _nbytes(shape, dtype):
    return math.prod(shape) * jnp.dtype(dtype).itemsize


def _in_hbm(x):
    return pltpu.with_memory_space_constraint(x, pltpu.HBM)


def _dot(a, b):
    return jnp.dot(a, b, preferred_element_type=F32)


def _dot_nt(a, b):
    return lax.dot_general(a, b, (((1,), (1,)), ((), ())), preferred_element_type=F32)


def _dot_tn(a, b):
    return lax.dot_general(a, b, (((0,), (0,)), ((), ())), preferred_element_type=F32)


def _split2(x):
    hi = x.astype(BF16)
    lo = (x - hi.astype(F32)).astype(BF16)
    return hi, lo


def _sigmoid(x):
    return pl.reciprocal(1.0 + jnp.exp(-x), approx=True)


class _Carry:
    def __init__(self, arrays=(), out_shapes=(), aliased=False, sems=(), start=None, finish=None):
        self.arrays, self.out_shapes, self.aliased = list(arrays), list(out_shapes), aliased
        self.sems, self.start, self.finish = list(sems), start, finish

    def __bool__(self):
        return bool(self.arrays)

    def __add__(self, other):
        assert not self.aliased and not other.aliased
        n_a, n_o, n_s = len(self.arrays), len(self.out_shapes), len(self.sems)
        return _Carry(
            self.arrays + other.arrays, self.out_shapes + other.out_shapes, False, self.sems + other.sems,
            lambda i, o, s: (self.start(i[:n_a], o[:n_o], s[:n_s]), other.start(i[n_a:], o[n_o:], s[n_s:])),
            lambda i, o, s: (self.finish(i[:n_a], o[:n_o], s[:n_s]), other.finish(i[n_a:], o[n_o:], s[n_s:])))

    def call_args(self, n_in, n_out):
        aliases = {n_in + k: n_out + k for k in range(len(self.arrays))} if self.aliased else {}
        return _hbm_specs(len(self.arrays)), _hbm_specs(len(self.out_shapes)), self.out_shapes, aliases, self.sems

    def run(self, refs, n_in, n_out, step, n_steps, compute):
        if not self:
            compute()
            return
        n_c, n_o, n_s = len(self.arrays), len(self.out_shapes), len(self.sems)
        ins = refs[n_in:n_in + n_c]
        outs = refs[n_in + n_c + n_out:n_in + n_c + n_out + n_o]
        sems = refs[len(refs) - n_s:]

        @pl.when(step == 0)
        def _():
            self.start(ins, outs, sems)

        compute()

        @pl.when(step == n_steps - 1)
        def _():
            self.finish(ins, outs, sems)


def _mm(a, b, *, ta=False, tb=False, add=None, out_dtype=F32, tm, tn, tk, name, carry=None, epilogue=None,
        b_cols=None, out_type=jax.ShapeDtypeStruct):
    carry = carry or _Carry()
    n_car = len(carry.arrays)
    pieces = list(a) if isinstance(a, (list, tuple)) else [a]
    n_a = len(pieces)
    widths = [p.shape[1] for p in pieces]
    starts = [sum(widths[:p]) for p in range(n_a)]
    if ta:
        kdim, m = pieces[0].shape[0], sum(widths)
    else:
        m, kdim = pieces[0].shape[0], sum(widths)
    if tb:
        n, k2 = b.shape
    else:
        k2, n = b.shape
    col0 = 0
    if b_cols is not None:
        assert b_cols[0] % tn == 0, name
        col0, n = b_cols[0] // tn, b_cols[1]
    assert kdim == k2 and m % tm == 0 and n % tn == 0 and kdim % tk == 0, (name, a.shape, b.shape)
    nk = kdim // tk
    assert n_a == 1 or (nk == 1 and not tb and (not ta or all(w % tm == 0 for w in widths))), name
    grid = (m // tm, n // tn, nk)
    a_mode = dict(pipeline_mode=pl.Buffered(1)) if grid[0] == 1 and nk == 1 else {}
    b_mode = dict(pipeline_mode=pl.Buffered(1)) if grid[1] == 1 and nk == 1 else {}
    if n_a == 1:
        a_specs = [pl.BlockSpec((tk, tm), lambda i, j, k: (k, i), **a_mode) if ta
                   else pl.BlockSpec((tm, tk), lambda i, j, k: (i, k), **a_mode)]
    elif ta:
        a_specs = [pl.BlockSpec((tk, tm), lambda i, j, k, s=s // tm, w=w // tm: (0, jnp.clip(i - s, 0, w - 1)))
                   for s, w in zip(starts, widths)]
    else:
        a_specs = [pl.BlockSpec((tm, w), lambda i, j, k: (i, 0)) for w in widths]
    b_spec = (pl.BlockSpec((tn, tk), lambda i, j, k: (j + col0, k), **b_mode) if tb
              else pl.BlockSpec((tk, tn), lambda i, j, k: (k, j + col0), **b_mode))
    o_spec = pl.BlockSpec((tm, tn), lambda i, j, k: (i, j))
    dims = ((((0,) if ta else (1,)), ((1,) if tb else (0,))), ((), ()))
    has_add = add is not None
    if epilogue is None:
        ep_fn, ep_rows, ep_params, ep_outs, ep_accs = None, [], [], [], []
        out_sds, out_specs = [out_type((m, n), out_dtype)], [o_spec]
    else:
        ep_fn, ep_rows, ep_params, ep_outs, ep_accs = epilogue
        assert grid[1] == 1 or not ep_accs, name
        out_sds = [out_type((m, w * grid[1]), d) for w, d in ep_outs]
        out_sds += [jax.ShapeDtypeStruct(sh, F32) for sh in ep_accs]
        out_specs = [pl.BlockSpec((tm, w), lambda i, j, k: (i, j)) for w, _ in ep_outs]
        out_specs += [pl.BlockSpec(sh, lambda i, j, k: (0, 0)) for sh in ep_accs]
    n_main = len(out_sds)
    use_scratch = nk > 1 and (ep_fn is not None or jnp.dtype(out_dtype) != jnp.dtype(F32))
    n_in = n_a + 1 + has_add + len(ep_rows) + len(ep_params)

    def finish(total, refs, pid):
        outs = refs[n_in + n_car:n_in + n_car + n_main]
        if ep_fn is None:
            outs[0][...] = total.astype(out_dtype)
            return
        first = n_a + 1 + has_add
        rows = [r[...].astype(F32) for r in refs[first:first + len(ep_rows)]]
        params = [p[...] for p in refs[first + len(ep_rows):n_in]]
        res = ep_fn(total, *rows, *params)
        for o_ref, v in zip(outs[:len(ep_outs)], res):
            o_ref[...] = v.astype(o_ref.dtype)
        acc_refs = outs[len(ep_outs):]
        if acc_refs:
            @pl.when(pid[0] == 0)
            def _():
                for r in acc_refs:
                    r[...] = jnp.zeros(r.shape, F32)

            for r, v in zip(acc_refs, res[len(ep_outs):]):
                r[...] += v

    def compute(refs, pid):
        a_ref, b_ref = refs[0], refs[n_a]
        add_ref = refs[n_a + 1] if has_add else None

        def dot(x, y):
            return lax.dot_general(x.astype(BF16), y.astype(BF16), dims, preferred_element_type=F32)

        if n_a > 1 and ta:
            for p_ref, s, w in zip(refs[:n_a], starts, widths):
                @pl.when((pid[0] >= s // tm) & (pid[0] < (s + w) // tm))
                def _(p_ref=p_ref):
                    prod = dot(p_ref[...], b_ref[...])
                    finish(prod + add_ref[...] if has_add else prod, refs, pid)
            return
        if n_a > 1:
            prod = dot(a_ref[...], b_ref[:widths[0], :])
            for p_ref, s, w in zip(refs[1:n_a], starts[1:], widths[1:]):
                prod += dot(p_ref[...], b_ref[s:s + w, :])
        else:
            prod = dot(a_ref[...], b_ref[...])
        if nk == 1:
            finish(prod + add_ref[...] if has_add else prod, refs, pid)
            return
        acc_ref = refs[n_in + n_car + n_main + len(carry.out_shapes)] if use_scratch else refs[n_in + n_car]
        k = pid[2]

        @pl.when(k == 0)
        def _():
            acc_ref[...] = prod + add_ref[...] if has_add else prod

        @pl.when(k > 0)
        def _():
            acc_ref[...] += prod

        if use_scratch:
            @pl.when(k == nk - 1)
            def _():
                finish(acc_ref[...], refs, pid)

    def body(*refs):
        pid = (pl.program_id(0), pl.program_id(1), pl.program_id(2))
        step = (pid[0] * grid[1] + pid[1]) * nk + pid[2]
        carry.run(refs, n_in, n_main, step, grid[0] * grid[1] * nk, lambda: compute(refs, pid))

    tile_bytes = ((n_a if ta else 1) * _nbytes((tm, tk), pieces[0].dtype)
                  + _nbytes((tk, tn), b.dtype) + 2 * _nbytes((tm, tn), F32)
                  + (_nbytes((tm, tn), F32) if has_add else 0)
                  + sum(_nbytes((tm, r.shape[1]), r.dtype) for r in ep_rows) + sum(_nbytes((tm, w), d) for w, d in ep_outs))
    in_specs = a_specs + [b_spec] + ([o_spec] if has_add else [])
    in_specs += [pl.BlockSpec((tm, r.shape[1] // grid[1]), lambda i, j, k: (i, j)) for r in ep_rows]
    in_specs += [pl.BlockSpec(p.shape, lambda i, j, k: (0, 0)) for p in ep_params]
    args = tuple(pieces) + (b,) + ((add,) if has_add else ()) + tuple(ep_rows) + tuple(ep_params)
    scratch = [pltpu.VMEM((tm, tn), F32)] if use_scratch else []
    serial = bool(carry) or bool(ep_accs)
    c_in, c_out, c_shapes, c_alias, c_sems = carry.call_args(n_in, n_main)
    res = pl.pallas_call(
        body, name=name, grid=grid,
        in_specs=in_specs + c_in, out_specs=out_specs + c_out, out_shape=out_sds + c_shapes,
        input_output_aliases=c_alias, scratch_shapes=scratch + c_sems,
        compiler_params=pltpu.CompilerParams(
            dimension_semantics=("arbitrary",) * 3 if serial else ("parallel", "parallel", "arbitrary"),
            vmem_limit_bytes=_vmem_limit(tile_bytes)),
    )(*args, *carry.arrays)
    main = res[0] if ep_fn is None else list(res[:n_main])
    return (main, res[n_main:]) if carry else main


def _rowwise(fn, rows, params, outs, accs, *, tm, name, carry=None, out_type=jax.ShapeDtypeStruct):
    carry = carry or _Carry()
    t = rows[0].shape[0]
    assert t % tm == 0, (name, t, tm)
    n_r, n_p, n_o, n_c = len(rows), len(params), len(outs), len(carry.arrays)

    def compute(refs, first):
        vals = [r[...].astype(F32) for r in refs[:n_r]] + [p[...] for p in refs[n_r:n_r + n_p]]
        res = fn(*vals)
        o_refs = refs[n_r + n_p + n_c:n_r + n_p + n_c + n_o]
        a_refs = refs[n_r + n_p + n_c + n_o:n_r + n_p + n_c + n_o + len(accs)]
        for o_ref, v in zip(o_refs, res[:n_o]):
            o_ref[...] = v.astype(o_ref.dtype)
        if accs:
            @pl.when(first)
            def _():
                for a_ref in a_refs:
                    a_ref[...] = jnp.zeros(a_ref.shape, F32)

            for a_ref, v in zip(a_refs, res[n_o:]):
                a_ref[...] += v

    def body(*refs):
        step = pl.program_id(0)
        carry.run(refs, n_r + n_p, n_o + len(accs), step, t // tm, lambda: compute(refs, step == 0))

    in_specs = [pl.BlockSpec((tm, r.shape[1]), lambda i: (i, 0)) for r in rows]
    in_specs += [pl.BlockSpec(p.shape, lambda i: (0, 0)) for p in params]
    out_specs = [pl.BlockSpec((tm, w), lambda i: (i, 0)) for w, _ in outs]
    out_specs += [pl.BlockSpec(s, lambda i: (0, 0)) for s in accs]
    out_shape = [out_type((t, w), d) for w, d in outs]
    out_shape += [jax.ShapeDtypeStruct(s, F32) for s in accs]
    tile_bytes = sum(_nbytes((tm, r.shape[1]), r.dtype) for r in rows) + sum(_nbytes((tm, w), F32) for w, _ in outs)
    c_in, c_out, c_shapes, c_alias, c_sems = carry.call_args(n_r + n_p, n_o + len(accs))
    res = pl.pallas_call(
        body, name=name, grid=(t // tm,), in_specs=in_specs + c_in, out_specs=out_specs + c_out,
        out_shape=out_shape + c_shapes, input_output_aliases=c_alias, scratch_shapes=c_sems,
        compiler_params=pltpu.CompilerParams(
            dimension_semantics=("arbitrary",) if accs or carry else ("parallel",),
            vmem_limit_bytes=_vmem_limit(2 * tile_bytes)),
    )(*rows, *params, *carry.arrays)
    own = n_o + len(accs)
    return (list(res[:own]) + [res[own:]]) if carry else res


def _rms_stats(x):
    r = lax.rsqrt(jnp.mean(x * x, axis=-1, keepdims=True) + RMS_EPS)
    return x * r, r


def _rms_bwd(dy, xhat, r, g):
    dxhat = dy * g
    dx = r * (dxhat - xhat * jnp.mean(dxhat * xhat, axis=-1, keepdims=True))
    return dx, dy * xhat


def _sb_consts():
    lane = lax.broadcasted_iota(jnp.int32, (BLK, LANES), 1)
    head0 = lane < HEAD_DIM
    row = lax.broadcasted_iota(jnp.int32, (2 * BLK, BLK), 0) % BLK
    col = lax.broadcasted_iota(jnp.int32, (2 * BLK, BLK), 1)
    causal = col < row
    jj = lax.broadcasted_iota(jnp.int32, (BLK, BLK), 0)
    ss = lax.broadcasted_iota(jnp.int32, (BLK, BLK), 1)
    suffix = jnp.where(jj > ss, 1.0, 0.0).astype(BF16)
    return head0, causal, suffix


def _stack_heads(x, head0):
    zero = jnp.zeros_like(x)
    return jnp.concatenate([jnp.where(head0, x, zero), jnp.where(head0, zero, x)], axis=0)


def _sb_logits(z, causal, masked):
    sp = jnp.log(1.0 + jnp.exp(-jnp.abs(z)))
    log_keep = -(jnp.maximum(z, 0.0) + sp)
    log_beta = jnp.minimum(z, 0.0) - sp
    if masked:
        log_keep = jnp.where(causal, log_keep, 0.0)
    return log_keep, log_beta


def _suffix_sums(x, suffix):
    hi, lo = _split2(x)
    after = _dot(hi, suffix) + _dot(lo, suffix)
    total = jnp.broadcast_to(after[:, 0:1] + x[:, 0:1], x.shape)
    return after, total


def _sb_walk_back(i, state, per_chain, tile):
    def alive(st):
        worst = functools.reduce(jnp.maximum, [st[p][:, 0:1] for p in range(0, len(st), per_chain)])
        return jnp.max(worst) > EXP_UNDERFLOW

    def cond(c):
        return jnp.logical_and(c[0] < i, alive(c[1]))

    def body(c):
        return c[0] + 1, tile(i - 1 - c[0], c[1], False)

    return lax.while_loop(cond, body, (jnp.int32(0), state))[1]


def _lane_blocks(x, n):
    return [x[:, p * LANES:(p + 1) * LANES] for p in range(n)]


def _sb_fwd(qkv, b_sz, s_len, carry):
    nq = s_len // BLK
    n_pairs = SB_WIDTH // LANES
    ch = SB_FWD_CHAINS
    n_steps = n_pairs // ch
    scale = 1.0 / math.sqrt(HEAD_DIM)

    def compute(q_ref, k_ref, v_ref, o_ref):
        head0, causal, suffix = _sb_consts()

        def q_block(i, _):
            qs = pl.multiple_of(i * BLK, BLK)
            q_all = (q_ref[pl.ds(qs, BLK), :] * scale).astype(BF16)
            q01 = [_stack_heads(q, head0) for q in _lane_blocks(q_all, ch)]

            def tile(j, state, masked):
                ks = pl.multiple_of(j * BLK, BLK)
                ks_ = _lane_blocks(k_ref[pl.ds(ks, BLK), :].astype(BF16), ch)
                vs_ = _lane_blocks(v_ref[pl.ds(ks, BLK), :].astype(BF16), ch)
                zs = [_dot_nt(q01[p], ks_[p]) for p in range(ch)]
                logits = [_sb_logits(z, causal, masked) for z in zs]
                sums = [_suffix_sums(lg[0], suffix) for lg in logits]
                out = []
                for p in range(ch):
                    carry, acc = state[2 * p], state[2 * p + 1]
                    after, total = sums[p]
                    a = jnp.exp(logits[p][1] + carry + after)
                    if masked:
                        a = jnp.where(causal, a, 0.0)
                    a_hi, a_lo = _split2(a)
                    a_cat = jnp.concatenate([a_hi[:BLK], a_hi[BLK:], a_lo[:BLK], a_lo[BLK:]], axis=1)
                    v01 = _stack_heads(vs_[p], head0)
                    out += [carry + total, acc + _dot(a_cat, jnp.concatenate([v01, v01], axis=0))]
                return tuple(out)

            state = (jnp.zeros((2 * BLK, BLK), F32), jnp.zeros((BLK, LANES), F32)) * ch
            state = tile(i, state, True)
            state = _sb_walk_back(i, state, 2, tile)
            o_ref[pl.ds(qs, BLK), :] = jnp.concatenate([state[2 * p + 1] for p in range(ch)], axis=1)
            return 0

        lax.fori_loop(0, nq, q_block, 0)

    def body(*refs):
        step = pl.program_id(0) * n_steps + pl.program_id(1)
        o_ref = refs[3 + len(carry.arrays)]
        carry.run(refs, 3, 1, step, b_sz * n_steps, lambda: compute(refs[0], refs[1], refs[2], o_ref))

    blk = lambda off: pl.BlockSpec((None, s_len, ch * LANES), lambda b, p: (b, 0, off + p))
    c_in, c_out, c_shapes, c_alias, c_sems = carry.call_args(3, 1)
    res = pl.pallas_call(
        body, name="sb_fwd", grid=(b_sz, n_steps),
        in_specs=[blk(0), blk(n_steps), blk(2 * n_steps)] + c_in, out_specs=[blk(0)] + c_out,
        out_shape=[jax.ShapeDtypeStruct((b_sz, s_len, SB_WIDTH), F32)] + c_shapes,
        input_output_aliases=c_alias, scratch_shapes=c_sems,
        compiler_params=pltpu.CompilerParams(dimension_semantics=("arbitrary", "arbitrary"),
                                             vmem_limit_bytes=VMEM_CAP),
    )(qkv, qkv, qkv, *carry.arrays)
    return res[0], res[1:]


def _sb_bwd(qkv, o_sb, do_sb, b_sz, s_len, carry):
    nq = s_len // BLK
    n_pairs = SB_WIDTH // LANES
    ch = SB_BWD_CHAINS
    n_steps = n_pairs // ch
    scale = 1.0 / math.sqrt(HEAD_DIM)

    def compute(q_ref, k_ref, v_ref, o_ref, do_ref, dq_ref, dk_ref, dv_ref, dk_acc, dv_acc):
        head0, causal, suffix = _sb_consts()
        lrow = lax.broadcasted_iota(jnp.int32, (LANES, LANES), 0)
        ones_h0 = jnp.where(lrow < HEAD_DIM, 1.0, 0.0).astype(BF16)
        ones_h1 = jnp.where(lrow >= HEAD_DIM, 1.0, 0.0).astype(BF16)
        dk_acc[...] = jnp.zeros(dk_acc.shape, F32)
        dv_acc[...] = jnp.zeros(dv_acc.shape, F32)

        def q_block(i, _):
            qs = pl.multiple_of(i * BLK, BLK)
            q_all = (q_ref[pl.ds(qs, BLK), :] * scale).astype(BF16)
            do_all = do_ref[pl.ds(qs, BLK), :].astype(BF16)
            dd_all = do_all.astype(F32) * o_ref[pl.ds(qs, BLK), :]
            q01 = [_stack_heads(q, head0) for q in _lane_blocks(q_all, ch)]
            do01 = [_stack_heads(d, head0) for d in _lane_blocks(do_all, ch)]
            tot = []
            for dd in _lane_blocks(dd_all, ch):
                dd_hi, dd_lo = _split2(dd)
                tot.append(jnp.concatenate([_dot(dd_hi, ones_h0) + _dot(dd_lo, ones_h0),
                                            _dot(dd_hi, ones_h1) + _dot(dd_lo, ones_h1)], axis=0))

            def tile(j, state, masked):
                ks = pl.multiple_of(j * BLK, BLK)
                ks_ = _lane_blocks(k_ref[pl.ds(ks, BLK), :].astype(BF16), ch)
                vs_ = _lane_blocks(v_ref[pl.ds(ks, BLK), :].astype(BF16), ch)
                zs = [_dot_nt(q01[p], ks_[p]) for p in range(ch)]
                das = [_dot_nt(do01[p], vs_[p]) for p in range(ch)]
                logits = [_sb_logits(z, causal, masked) for z in zs]
                sums = [_suffix_sums(lg[0], suffix) for lg in logits]
                a_s, e_s = [], []
                for p in range(ch):
                    a = jnp.exp(logits[p][1] + state[3 * p] + sums[p][0])
                    if masked:
                        a = jnp.where(causal, a, 0.0)
                    a_s.append(a)
                    e_s.append(a * das[p])
                e_sums = [_suffix_sums(e, suffix) for e in e_s]
                out, dks, dvs = [], [], []
                for p in range(ch):
                    carry, rcarry, dq = state[3 * p:3 * p + 3]
                    e = e_s[p]
                    before = tot[p] - (rcarry + e_sums[p][0] + e)
                    beta = jnp.exp(logits[p][1])
                    dz = e * (1.0 - beta) - beta * before
                    if masked:
                        dz = jnp.where(causal, dz, 0.0)
                    dz_b = dz.astype(BF16)
                    dks.append(_dot_tn(dz_b, q01[p]))
                    dvs.append(_dot_tn(a_s[p].astype(BF16), do01[p]))
                    out += [carry + sums[p][1], rcarry + e_sums[p][1], dq + _dot(dz_b, ks_[p])]
                dk_acc[pl.ds(ks, BLK), :] += jnp.concatenate(dks, axis=1)
                dv_acc[pl.ds(ks, BLK), :] += jnp.concatenate(dvs, axis=1)
                return tuple(out)

            state = (jnp.zeros((2 * BLK, BLK), F32),) * (3 * ch)
            state = tile(i, state, True)
            state = _sb_walk_back(i, state, 3, tile)
            dq = [jnp.where(head0, state[3 * p + 2][:BLK], state[3 * p + 2][BLK:]) for p in range(ch)]
            dq_ref[pl.ds(qs, BLK), :] = (jnp.concatenate(dq, axis=1) * scale).astype(dq_ref.dtype)
            return 0

        lax.fori_loop(0, nq, q_block, 0)
        dk_ref[...] = dk_acc[...].astype(dk_ref.dtype)
        dv_ref[...] = dv_acc[...].astype(dv_ref.dtype)

    def body(*refs):
        step = pl.program_id(0) * n_steps + pl.program_id(1)
        n_c, n_o = len(carry.arrays), len(carry.out_shapes)
        own = refs[:5] + refs[5 + n_c:8 + n_c] + refs[8 + n_c + n_o:10 + n_c + n_o]
        carry.run(refs, 5, 3, step, b_sz * n_steps, lambda: compute(*own))

    blk = lambda off: pl.BlockSpec((None, s_len, ch * LANES), lambda b, p: (b, 0, off + p))
    once = lambda off: pl.BlockSpec((None, s_len, ch * LANES), lambda b, p: (b, 0, off + p),
                                    pipeline_mode=pl.Buffered(1))
    out_sd = jax.ShapeDtypeStruct((b_sz, s_len, SB_WIDTH), BF16)
    c_in, c_out, c_shapes, c_alias, c_sems = carry.call_args(5, 3)
    res = pl.pallas_call(
        body, name="sb_bwd", grid=(b_sz, n_steps),
        in_specs=[once(0), once(n_steps), once(2 * n_steps), once(0), once(0)] + c_in,
        out_specs=[blk(0), blk(0), blk(0)] + c_out, out_shape=[out_sd, out_sd, out_sd] + c_shapes,
        input_output_aliases=c_alias,
        scratch_shapes=[pltpu.VMEM((s_len, ch * LANES), F32), pltpu.VMEM((s_len, ch * LANES), F32)] + c_sems,
        compiler_params=pltpu.CompilerParams(dimension_semantics=("arbitrary", "arbitrary"),
                                             vmem_limit_bytes=VMEM_CAP),
    )(qkv, qkv, qkv, o_sb, do_sb, *carry.arrays)
    return res[:3], res[3:]


def _dil_consts(group, pair_idx, dilation):
    lane = lax.broadcasted_iota(jnp.int32, (BLK, LANES), 1)
    head0 = lane < HEAD_DIM
    row = lax.broadcasted_iota(jnp.int32, (2 * BLK, BLK), 0)
    qa = row % BLK
    kb = lax.broadcasted_iota(jnp.int32, (2 * BLK, BLK), 1)
    head = (group * DIL_HEADS_PER_GROUP + 2 * pair_idx + row // BLK).astype(F32)
    slope = jnp.exp((-ALIBI_MAX_BIAS * math.log(2.0) / DIL_HEADS) * (head + 1.0))
    valid_cur = kb <= qa
    valid_prev = kb >= qa
    bias_cur = -slope * ((qa - kb) * dilation).astype(F32)
    bias_prev = -slope * ((BLK + qa - kb) * dilation).astype(F32)
    return head0, valid_cur, valid_prev, bias_cur, bias_prev


def _dil_units(s_len, dilation):
    nb = s_len // dilation // BLK
    return [(r, n) for r in range(dilation) for n in range(nb)]


def _dil_rows(n, r, dilation):
    if dilation == 1:
        return pl.ds(n * BLK, BLK)
    return pl.ds(n * BLK * dilation + r, BLK, stride=dilation)


def _dil_scores(q01, k, bias, valid):
    s = _dot_nt(q01, k) * (1.0 / math.sqrt(HEAD_DIM)) + bias
    return jnp.where(valid, s, NEG)


def _dil_fwd(qkv, b_sz, s_len, carry):
    n_pairs = DIL_OUT_WIDTH // LANES
    q_off = 3 * SB_WIDTH // LANES
    per_kind = DIL_WIDTH // LANES

    def compute(pair_idx, qkv_refs, o_ref, lse_ref, m_s, l_s):
        m_s[...] = jnp.full(m_s.shape, NEG, F32)
        l_s[...] = jnp.zeros(l_s.shape, F32)
        o_ref[...] = jnp.zeros(o_ref.shape, F32)
        for g, (_, dilation) in enumerate(DIL_PAIRS):
            q_ref, k_ref, v_ref = qkv_refs[3 * g:3 * g + 3]
            head0, valid_cur, valid_prev, bias_cur, bias_prev = _dil_consts(g, pair_idx, dilation)
            units = _dil_units(s_len, dilation)
            for u0 in range(0, len(units), DIL_CHAINS):
                group = units[u0:u0 + DIL_CHAINS]
                rows_of = [_dil_rows(n, r, dilation) for r, n in group]
                scores, values = [], []
                for (r, n), rows in zip(group, rows_of):
                    q01 = _stack_heads(q_ref[rows, :].astype(BF16), head0)
                    sc = [_dil_scores(q01, k_ref[rows, :].astype(BF16), bias_cur, valid_cur)]
                    vals = [_stack_heads(v_ref[rows, :].astype(BF16), head0)]
                    if n > 0:
                        prev = _dil_rows(n - 1, r, dilation)
                        sc.append(_dil_scores(q01, k_ref[prev, :].astype(BF16), bias_prev, valid_prev))
                        vals.append(_stack_heads(v_ref[prev, :].astype(BF16), head0))
                    scores.append(sc)
                    values.append(vals)
                stats = []
                for sc, rows in zip(scores, rows_of):
                    m_blk = functools.reduce(jnp.maximum, [jnp.max(x, axis=-1, keepdims=True) for x in sc])
                    m_old = jnp.concatenate([m_s.at[0][rows, :], m_s.at[1][rows, :]], axis=0)
                    l_old = jnp.concatenate([l_s.at[0][rows, :], l_s.at[1][rows, :]], axis=0)
                    m_new = jnp.maximum(m_old, m_blk)
                    probs = [jnp.exp(x - m_new) for x in sc]
                    l_blk = functools.reduce(jnp.add, [jnp.sum(p, axis=-1, keepdims=True) for p in probs])
                    alpha = jnp.exp(m_old - m_new)
                    stats.append((m_new, alpha * l_old + l_blk, alpha, probs))
                for (m_new, l_new, alpha, probs), vals, rows in zip(stats, values, rows_of):
                    alpha_tok = jnp.where(head0, alpha[:BLK], alpha[BLK:])
                    p_cat = jnp.concatenate(
                        [h for p in probs for h in (p[:BLK].astype(BF16), p[BLK:].astype(BF16))], axis=1)
                    o_ref[rows, :] = alpha_tok * o_ref[rows, :] + _dot(p_cat, jnp.concatenate(vals, axis=0))
                    m_s.at[0][rows, :] = m_new[:BLK]
                    m_s.at[1][rows, :] = m_new[BLK:]
                    l_s.at[0][rows, :] = l_new[:BLK]
                    l_s.at[1][rows, :] = l_new[BLK:]
        lane = lax.broadcasted_iota(jnp.int32, (BLK, LANES), 1)
        for c in range(s_len // BLK):
            rows = pl.ds(c * BLK, BLK)
            l0, l1 = l_s.at[0][rows, :], l_s.at[1][rows, :]
            o_ref[rows, :] = o_ref[rows, :] / jnp.where(lane < HEAD_DIM, l0, l1)
            lse_ref.at[0][rows, :] = m_s.at[0][rows, :] + jnp.log(l0)
            lse_ref.at[1][rows, :] = m_s.at[1][rows, :] + jnp.log(l1)

    def body(*refs):
        pair_idx = pl.program_id(1)
        step = pl.program_id(0) * n_pairs + pair_idx
        n_c, n_o = len(carry.arrays), len(carry.out_shapes)
        o_ref, lse_ref = refs[9 + n_c:11 + n_c]
        m_s, l_s = refs[11 + n_c + n_o:13 + n_c + n_o]
        carry.run(refs, 9, 2, step, b_sz * n_pairs, lambda: compute(pair_idx, refs[:9], o_ref, lse_ref, m_s, l_s))

    in_specs = []
    for g in range(len(DIL_PAIRS)):
        for kind in range(3):
            off = q_off + kind * per_kind + g * n_pairs
            in_specs.append(pl.BlockSpec((None, s_len, LANES), lambda b, p, off=off: (b, 0, off + p)))
    c_in, c_out, c_shapes, c_alias, c_sems = carry.call_args(9, 2)
    res = pl.pallas_call(
        body, name="dil_fwd", grid=(b_sz, n_pairs),
        in_specs=in_specs + c_in,
        out_specs=[pl.BlockSpec((None, s_len, LANES), lambda b, p: (b, 0, p)),
                   pl.BlockSpec((None, None, 2, s_len, LANES), lambda b, p: (b, p, 0, 0, 0))] + c_out,
        out_shape=[jax.ShapeDtypeStruct((b_sz, s_len, DIL_OUT_WIDTH), F32),
                   jax.ShapeDtypeStruct((b_sz, n_pairs, 2, s_len, LANES), F32)] + c_shapes,
        input_output_aliases=c_alias,
        scratch_shapes=[pltpu.VMEM((2, s_len, LANES), F32), pltpu.VMEM((2, s_len, LANES), F32)] + c_sems,
        compiler_params=pltpu.CompilerParams(dimension_semantics=("arbitrary", "arbitrary"),
                                             vmem_limit_bytes=VMEM_CAP),
    )(*([qkv] * 9), *carry.arrays)
    return res[0], res[1], res[2:]


def _dil_bwd(qkv, o_dl, lse, do_dl, b_sz, s_len, carry):
    n_pairs = DIL_OUT_WIDTH // LANES
    n_groups = len(DIL_PAIRS)
    q_off = 3 * SB_WIDTH // LANES
    per_kind = DIL_WIDTH // LANES

    def compute(pair_idx, group, q_ref, k_ref, v_ref, o_ref, lse_ref, do_ref, dq_ref, dk_ref, dv_ref, d_s, dq_s, dk_s, dv_s):
        lrow = lax.broadcasted_iota(jnp.int32, (LANES, LANES), 0)
        ones_h0 = jnp.where(lrow < HEAD_DIM, 1.0, 0.0).astype(BF16)
        ones_h1 = jnp.where(lrow >= HEAD_DIM, 1.0, 0.0).astype(BF16)
        for c in range(s_len // BLK):
            rows = pl.ds(c * BLK, BLK)
            dd_hi, dd_lo = _split2(do_ref[rows, :] * o_ref[rows, :])
            d_s.at[0][rows, :] = _dot(dd_hi, ones_h0) + _dot(dd_lo, ones_h0)
            d_s.at[1][rows, :] = _dot(dd_hi, ones_h1) + _dot(dd_lo, ones_h1)
        dk_s[...] = jnp.zeros(dk_s.shape, F32)
        dv_s[...] = jnp.zeros(dv_s.shape, F32)

        def one_group(g, dilation):
            head0, valid_cur, valid_prev, bias_cur, bias_prev = _dil_consts(g, pair_idx, dilation)
            units = _dil_units(s_len, dilation)
            scale = 1.0 / math.sqrt(HEAD_DIM)
            for u0 in range(0, len(units), DIL_CHAINS):
                chunk = units[u0:u0 + DIL_CHAINS]
                loaded = []
                for r, n in chunk:
                    rows = _dil_rows(n, r, dilation)
                    q01 = _stack_heads(q_ref[rows, :].astype(BF16), head0)
                    do01 = _stack_heads(do_ref[rows, :].astype(BF16), head0)
                    lse01 = jnp.concatenate([lse_ref.at[0][rows, :], lse_ref.at[1][rows, :]], axis=0)
                    d01 = jnp.concatenate([d_s.at[0][rows, :], d_s.at[1][rows, :]], axis=0)
                    blocks = [(rows, bias_cur, valid_cur)]
                    if n > 0:
                        blocks.append((_dil_rows(n - 1, r, dilation), bias_prev, valid_prev))
                    parts = []
                    for krows, bias, valid in blocks:
                        k = k_ref[krows, :].astype(BF16)
                        v = v_ref[krows, :].astype(BF16)
                        parts.append((krows, k, _dil_scores(q01, k, bias, valid), _dot_nt(do01, v)))
                    loaded.append((rows, q01, do01, lse01, d01, parts))
                grads = []
                for rows, q01, do01, lse01, d01, parts in loaded:
                    for krows, k, sc, dp in parts:
                        p = jnp.exp(sc - lse01)
                        grads.append((p.astype(BF16), (p * (dp - d01) * scale).astype(BF16)))
                it = iter(grads)
                updates = []
                for rows, q01, do01, lse01, d01, parts in loaded:
                    dq = jnp.zeros((2 * BLK, LANES), F32)
                    for krows, k, sc, dp in parts:
                        p_b, ds = next(it)
                        dq = dq + _dot(ds, k)
                        updates.append((krows, _dot_tn(ds, q01), _dot_tn(p_b, do01)))
                    dq_s[rows, :] = jnp.where(head0, dq[:BLK], dq[BLK:])
                for krows, dk, dv in updates:
                    dk_s[krows, :] = dk_s[krows, :] + dk
                    dv_s[krows, :] = dv_s[krows, :] + dv

        for g, (_, dilation) in enumerate(DIL_PAIRS):
            pl.when(group == g)(functools.partial(one_group, g, dilation))
        dq_ref[...] = dq_s[...].astype(dq_ref.dtype)
        dk_ref[...] = dk_s[...].astype(dk_ref.dtype)
        dv_ref[...] = dv_s[...].astype(dv_ref.dtype)

    def body(*refs):
        pair_idx, group = pl.program_id(1), pl.program_id(2)
        step = (pl.program_id(0) * n_pairs + pair_idx) * n_groups + group
        n_c, n_o = len(carry.arrays), len(carry.out_shapes)
        own = refs[:6] + refs[6 + n_c:9 + n_c] + refs[9 + n_c + n_o:13 + n_c + n_o]
        carry.run(refs, 6, 3, step, b_sz * n_pairs * n_groups, lambda: compute(pair_idx, group, *own))

    def qkv_spec(kind):
        return pl.BlockSpec((None, s_len, LANES),
                            lambda b, p, g: (b, 0, q_off + kind * per_kind + g * n_pairs + p))

    tok_spec = pl.BlockSpec((None, s_len, LANES), lambda b, p, g: (b, 0, p))
    out_spec = pl.BlockSpec((None, s_len, LANES), lambda b, p, g: (b, 0, g * n_pairs + p))
    out_sd = jax.ShapeDtypeStruct((b_sz, s_len, DIL_WIDTH), BF16)
    c_in, c_out, c_shapes, c_alias, c_sems = carry.call_args(6, 3)
    res = pl.pallas_call(
        body, name="dil_bwd", grid=(b_sz, n_pairs, n_groups),
        in_specs=[qkv_spec(0), qkv_spec(1), qkv_spec(2), tok_spec,
                  pl.BlockSpec((None, None, 2, s_len, LANES), lambda b, p, g: (b, p, 0, 0, 0)), tok_spec] + c_in,
        out_specs=[out_spec, out_spec, out_spec] + c_out,
        out_shape=[out_sd, out_sd, out_sd] + c_shapes,
        input_output_aliases=c_alias,
        scratch_shapes=[pltpu.VMEM((2, s_len, LANES), F32)] + [pltpu.VMEM((s_len, LANES), F32)] * 3 + c_sems,
        compiler_params=pltpu.CompilerParams(dimension_semantics=("arbitrary", "arbitrary", "arbitrary"),
                                             vmem_limit_bytes=VMEM_CAP),
    )(qkv, qkv, qkv, o_dl, lse, do_dl, *carry.arrays)
    return res[:3], res[3:]


def _mesh_pos():
    return lax.axis_index("x"), lax.axis_index("y"), lax.axis_index("c")


def _other_chips(x, y):
    return [(1 - x, y), (x, 1 - y), (1 - x, 1 - y)]


def _hbm_specs(n):
    return [pl.BlockSpec(memory_space=pl.ANY)] * n


SWAPPED = ("w_ffn_in",)


def _slot(x, y, swapped):
    return 2 * y + x if swapped else 2 * x + y


def _cast_to_slab(w, name, swapped=False):
    rows, cols = w.shape
    mine = jnp.reshape(_slot(lax.axis_index("x"), lax.axis_index("y"), swapped), (1,)).astype(jnp.int32)

    def body(idx_ref, w_ref, o_ref):
        o_ref[...] = w_ref[...].astype(BF16)

    return pl.pallas_call(
        body, name=name,
        grid_spec=pltpu.PrefetchScalarGridSpec(
            num_scalar_prefetch=1, grid=(1,),
            in_specs=[pl.BlockSpec((rows, cols), lambda i, idx: (0, 0))],
            out_specs=pl.BlockSpec((None, rows, cols), lambda i, idx: (idx[0], 0, 0))),
        out_shape=_hbm_array((N_CHIPS, rows, cols), BF16),
        compiler_params=pltpu.CompilerParams(vmem_limit_bytes=_vmem_limit(rows * cols * 6)),
    )(mine, w)


def _gather_issue(slabs, send_sems, recv_sems, swapped):
    x, y, c = _mesh_pos()
    for k, slab in enumerate(slabs):
        half = slab.shape[1] // 2
        rows = slab.at[_slot(x, y, swapped[k]), pl.ds(c * half, half), :]
        for r, (px, py) in enumerate(_other_chips(x, y)):
            pltpu.make_async_remote_copy(
                src_ref=rows, dst_ref=rows, send_sem=send_sems.at[6 * k + r], recv_sem=recv_sems.at[6 * k + r],
                device_id=(px, py, c), device_id_type=MESH).start()


def _gather_complete(slabs, send_sems, recv_sems, swapped):
    x, y, c = _mesh_pos()
    chips = _other_chips(x, y)

    def copy(k, sem, block, rows, to):
        ref = slabs[k].at[block, rows, :]
        return pltpu.make_async_remote_copy(
            src_ref=ref, dst_ref=ref, send_sem=send_sems.at[sem], recv_sem=recv_sems.at[sem],
            device_id=to, device_id_type=MESH)

    for k, slab in enumerate(slabs):
        half = slab.shape[1] // 2
        for r, (px, py) in enumerate(chips):
            theirs = _slot(px, py, swapped[k])
            copy(k, 6 * k + r, theirs, pl.ds(c * half, half), (px, py, c)).wait_recv()
            copy(k, 6 * k + 3 + r, theirs, pl.ds(c * half, half), (x, y, 1 - c)).start()
    for k, slab in enumerate(slabs):
        half = slab.shape[1] // 2
        for r, (px, py) in enumerate(chips):
            copy(k, 6 * k + 3 + r, _slot(px, py, swapped[k]), pl.ds((1 - c) * half, half), (x, y, 1 - c)).wait_recv()
    for k, slab in enumerate(slabs):
        half = slab.shape[1] // 2
        for r, (px, py) in enumerate(chips):
            copy(k, 6 * k + r, _slot(x, y, swapped[k]), pl.ds(c * half, half), (px, py, c)).wait_send()
            copy(k, 6 * k + 3 + r, _slot(px, py, swapped[k]), pl.ds(c * half, half), (x, y, 1 - c)).wait_send()


def _gather_sems(n):
    return [pltpu.SemaphoreType.DMA((6 * n,)), pltpu.SemaphoreType.DMA((6 * n,))]


def _gather_carry(slabs, names):
    swapped = [k in SWAPPED for k in names]
    return _Carry(slabs, [_hbm_array(a.shape, a.dtype) for a in slabs], True, _gather_sems(len(slabs)),
                  lambda ins, outs, sems: _gather_issue(outs, *sems, swapped),
                  lambda ins, outs, sems: _gather_complete(outs, *sems, swapped))


def _pair_copies(ins, outs, send_sems, recv_sems):
    x, y, c = _mesh_pos()
    copies = []
    for k, g in enumerate(ins):
        half = g.shape[1] // 2
        copies.append(pltpu.make_async_remote_copy(
            src_ref=g.at[:, pl.ds((1 - c) * half, half), :], dst_ref=outs[k],
            send_sem=send_sems.at[k], recv_sem=recv_sems.at[k],
            device_id=(x, y, 1 - c), device_id_type=MESH))
    return copies


def _pair_carry(grads):
    n = len(grads)

    def start(ins, outs, sems):
        for cp in _pair_copies(ins, outs, *sems):
            cp.start()

    def finish(ins, outs, sems):
        for cp in _pair_copies(ins, outs, *sems):
            cp.wait()

    return _Carry(grads, [jax.ShapeDtypeStruct((N_CHIPS, g.shape[1] // 2, g.shape[2]), g.dtype) for g in grads], False,
                  [pltpu.SemaphoreType.DMA((n,)), pltpu.SemaphoreType.DMA((n,))], start, finish)


def _pair_exchange(grads, tag):
    carry = _pair_carry(grads)
    n = len(grads)

    def body(*refs):
        carry.start(refs[:n], refs[n:2 * n], refs[2 * n:])
        carry.finish(refs[:n], refs[n:2 * n], refs[2 * n:])

    return pl.pallas_call(
        body, name="grad_pair_exchange_" + tag, in_specs=_hbm_specs(n), out_specs=_hbm_specs(n),
        out_shape=carry.out_shapes, scratch_shapes=carry.sems,
    )(*grads)


def _pair_sum(grad, other, name, swapped):
    _, rows, cols = grad.shape
    half = rows // 2
    x, y, c = _mesh_pos()
    idx = jnp.stack([c, _slot(x, y, swapped)]).astype(jnp.int32)

    def body(idx_ref, g_ref, p_ref, own_ref, sb_ref):
        s = g_ref[...] + p_ref[...].astype(F32)
        sb_ref[...] = s.astype(BF16)

        @pl.when(pl.program_id(0) == idx_ref[1])
        def _():
            own_ref[...] = s

    blk = pl.BlockSpec((None, half, cols), lambda p, idx: (p, 0, 0))
    return pl.pallas_call(
        body, name=name,
        grid_spec=pltpu.PrefetchScalarGridSpec(
            num_scalar_prefetch=1, grid=(N_CHIPS,),
            in_specs=[pl.BlockSpec((None, half, cols), lambda p, idx: (p, idx[0], 0)), blk],
            out_specs=[pl.BlockSpec((half, cols), lambda p, idx: (0, 0)), blk]),
        out_shape=[jax.ShapeDtypeStruct((half, cols), F32), jax.ShapeDtypeStruct((N_CHIPS, half, cols), BF16)],
        compiler_params=pltpu.CompilerParams(dimension_semantics=("arbitrary",),
                                             vmem_limit_bytes=_vmem_limit(4 * half * cols * 4)),
    )(idx, grad, other)


def _chip_copies(sums_bf16, lands, send_sems, recv_sems, swapped):
    x, y, c = _mesh_pos()
    return [pltpu.make_async_remote_copy(
        src_ref=sums_bf16[k].at[_slot(px, py, swapped[k])], dst_ref=lands[k].at[r],
        send_sem=send_sems.at[3 * k + r], recv_sem=recv_sems.at[3 * k + r],
        device_id=(px, py, c), device_id_type=MESH)
        for k in range(len(sums_bf16)) for r, (px, py) in enumerate(_other_chips(x, y))]


def _chip_carry(sums_bf16, names):
    swapped = [k in SWAPPED for k in names]

    def start(ins, outs, sems):
        for cp in _chip_copies(ins, outs, *sems, swapped):
            cp.start()

    def finish(ins, outs, sems):
        for cp in _chip_copies(ins, outs, *sems, swapped):
            cp.wait()

    return _Carry(sums_bf16, _chip_landing(sums_bf16), False, _chip_sems(len(sums_bf16)), start, finish)


def _chip_sems(n):
    return [pltpu.SemaphoreType.DMA((3 * n,)), pltpu.SemaphoreType.DMA((3 * n,))]


def _chip_landing(sums_bf16):
    return [jax.ShapeDtypeStruct((N_CHIPS - 1,) + s.shape[1:], BF16) for s in sums_bf16]


def _chip_sum(own, landed, name):
    rows, cols = own.shape
    core = jnp.reshape(lax.axis_index("c"), (1,)).astype(jnp.int32)

    def body(core_ref, o_ref, l_ref, out_ref):
        out_ref[...] = ((o_ref[...] + l_ref[0].astype(F32)) + l_ref[1].astype(F32)) + l_ref[2].astype(F32)

    return pl.pallas_call(
        body, name=name,
        grid_spec=pltpu.PrefetchScalarGridSpec(
            num_scalar_prefetch=1, grid=(1,),
            in_specs=[pl.BlockSpec((rows, cols), lambda i, core_ref: (0, 0)),
                      pl.BlockSpec((N_CHIPS - 1, rows, cols), lambda i, core_ref: (0, 0, 0))],
            out_specs=pl.BlockSpec((rows, cols), lambda i, core_ref: (core_ref[0], 0))),
        out_shape=jax.ShapeDtypeStruct((2 * rows, cols), F32),
        compiler_params=pltpu.CompilerParams(vmem_limit_bytes=_vmem_limit(3 * rows * cols * 4)),
    )(core, own, landed)


def _final_exchange(fulls, v):
    n = len(fulls)
    rows, cols = v.shape
    n_dev = 8

    def body(*refs):
        v_ref, out_ref = refs[0], refs[1 + 2 * n]
        outs = refs[1 + n:1 + 2 * n]
        buf, v_send, v_recv, h_send, h_recv = refs[2 + 2 * n:]
        x, y, c = _mesh_pos()
        me = 4 * x + 2 * y + c
        buf[me] = v_ref[...]
        peers = [(1 - x if r & 4 else x, 1 - y if r & 2 else y, 1 - c if r & 1 else c) for r in range(1, n_dev)]
        copies = []
        for r, peer in enumerate(peers):
            copies.append(pltpu.make_async_remote_copy(
                src_ref=v_ref, dst_ref=buf.at[me], send_sem=v_send.at[r], recv_sem=v_recv.at[r],
                device_id=peer, device_id_type=MESH))
        for k in range(n):
            half = fulls[k].shape[0] // 2
            mine = outs[k].at[pl.ds(c * half, half), :]
            copies.append(pltpu.make_async_remote_copy(
                src_ref=mine, dst_ref=mine, send_sem=h_send.at[k], recv_sem=h_recv.at[k],
                device_id=(x, y, 1 - c), device_id_type=MESH))
        for cp in copies:
            cp.start()
        for r, (px, py, pc) in enumerate(peers):
            pltpu.make_async_remote_copy(
                src_ref=v_ref, dst_ref=buf.at[4 * px + 2 * py + pc], send_sem=v_send.at[r], recv_sem=v_recv.at[r],
                device_id=(px, py, pc), device_id_type=MESH).wait_recv()
        for k in range(n):
            half = fulls[k].shape[0] // 2
            theirs = outs[k].at[pl.ds((1 - c) * half, half), :]
            pltpu.make_async_remote_copy(
                src_ref=theirs, dst_ref=theirs, send_sem=h_send.at[k], recv_sem=h_recv.at[k],
                device_id=(x, y, 1 - c), device_id_type=MESH).wait_recv()
        for cp in copies:
            cp.wait_send()
        acc = buf[0]
        for d in range(1, n_dev):
            acc = acc + buf[d]
        out_ref[...] = acc
        out_ref[3:4, :] = jnp.broadcast_to(jnp.sum(acc[3:4, :], axis=1, keepdims=True), (1, cols))

    vm = pl.BlockSpec(memory_space=pltpu.VMEM)
    res = pl.pallas_call(
        body, name="final_exchange",
        in_specs=[vm] + _hbm_specs(n), out_specs=_hbm_specs(n) + [vm],
        out_shape=[jax.ShapeDtypeStruct(f.shape, F32) for f in fulls] + [jax.ShapeDtypeStruct((rows, cols), F32)],
        input_output_aliases={1 + k: k for k in range(n)},
        scratch_shapes=[pltpu.VMEM((n_dev, rows, cols), F32),
                        pltpu.SemaphoreType.DMA((n_dev - 1,)), pltpu.SemaphoreType.DMA((n_dev - 1,)),
                        pltpu.SemaphoreType.DMA((n,)), pltpu.SemaphoreType.DMA((n,))],
    )(v, *fulls)
    return res[:n], res[n]


def _adamw_math(w, g, m, v):
    m = ADAM_B1 * m + (1.0 - ADAM_B1) * g
    v = ADAM_B2 * v + (1.0 - ADAM_B2) * (g * g)
    m_hat = m / (1.0 - ADAM_B1 ** ADAM_STEP)
    v_hat = v / (1.0 - ADAM_B2 ** ADAM_STEP)
    delta = -ADAM_LR * (m_hat / (jnp.sqrt(v_hat) + ADAM_EPS) + ADAM_WD * w)
    return delta, m, v


def _adamw(w, g, m, v, name):
    rows, cols = w.shape
    tm = rows // 2 if (rows // 2) % 8 == 0 else rows
    return _rowwise(_adamw_math, [w, g, m, v], [], [(cols, F32)] * 3, [], tm=tm, name=name)


def _unshard_cols(gathered):
    n, r, c = gathered.shape
    return jnp.transpose(gathered, (1, 0, 2)).reshape(r, n * c)


def _shard_cols(full):
    r, nc = full.shape
    return jnp.transpose(full.reshape(r, N_CHIPS, nc // N_CHIPS), (1, 0, 2))


LATE = ["w_sb_up", "w_dil_up", "w_out", "w_ffn_in", "w_ffn_out"]


def _late_weights(slabs, d_model, d_ff):
    g = dict(zip(LATE, slabs))
    return (_unshard_cols(g["w_sb_up"]), _unshard_cols(g["w_dil_up"]), g["w_out"].reshape(d_model, d_model),
            _unshard_cols(g["w_ffn_in"]), g["w_ffn_out"].reshape(d_ff, d_model))


ROW_SHARDED = ("w_in", "w_out", "w_ffn_in", "w_ffn_out")


def _chip_major(grads):
    out = []
    for k, g in grads.items():
        if k in ROW_SHARDED:
            out.append(g.reshape(N_CHIPS, g.shape[0] // N_CHIPS, g.shape[1]))
        else:
            out.append(_shard_cols(g))
    return out


def _pair_sums(full, others, names):
    return [_pair_sum(g, o, "grad_pair_sum_" + k, k in SWAPPED) for g, o, k in zip(full, others, names)]


def _chip_sums(pair, landed, names):
    return {k: _chip_sum(p[0], l, "grad_chip_sum_" + k) for p, l, k in zip(pair, landed, names)}


def _fwd_bwd(x, loss_target, g_mix, g_ffn, g_fin, slab_in, late_slabs):
    b_sz, s_len, d_model = x.shape
    t = b_sz * s_len
    d_ff = late_slabs[-1].shape[1] * N_CHIPS
    x2d = x.reshape(t, d_model)
    tgt2d = loss_target.reshape(t, d_model)

    u, (slab_in,) = _rowwise(lambda xv, g: (_rms_stats(xv)[0] * g,), [_in_hbm(x2d)], [g_mix], [(d_model, BF16)], [], tm=512,
                             name="norm_mix", carry=_gather_carry([slab_in], ["w_in"]), out_type=_hbm_array)
    u, wt_in = _in_hbm(u), _in_hbm(slab_in.reshape(-1, d_model))
    qkv, (slab_ffn_out,) = _mm(u, wt_in, tb=True, b_cols=(0, QKV_WIDTH), tm=2048, tn=768, tk=d_model, name="proj_qkv",
                               carry=_gather_carry(late_slabs[4:], LATE[4:]))
    gates = _mm(u, wt_in, tb=True, b_cols=(QKV_WIDTH, 2 * d_model), out_dtype=BF16, tm=t, tn=256, tk=d_model,
                name="proj_gates")
    qkv3 = qkv.reshape(b_sz, s_len, QKV_WIDTH)
    o_sb, (slab_ffn_in,) = _sb_fwd(qkv3, b_sz, s_len, _gather_carry(late_slabs[3:4], LATE[3:4]))
    o_dl, lse, small_slabs = _dil_fwd(qkv3, b_sz, s_len, _gather_carry(late_slabs[:3], LATE[:3]))
    wf_sb_up, wf_dil_up, wf_out, wf_ffn_in, wf_ffn_out = _late_weights(
        list(small_slabs) + [slab_ffn_in, slab_ffn_out], d_model, d_ff)
    o_sb2, o_dl2 = o_sb.reshape(t, SB_WIDTH), o_dl.reshape(t, DIL_OUT_WIDTH)
    y_sb = _mm(o_sb2, wf_sb_up, out_dtype=BF16, tm=1024, tn=1024, tk=SB_WIDTH, name="sb_up", out_type=_hbm_array)
    y_dl = _mm(o_dl2, wf_dil_up, out_dtype=BF16, tm=1024, tn=1024, tk=DIL_OUT_WIDTH, name="dil_up", out_type=_hbm_array)

    def merge_fn(gt, ys, yd):
        return (_sigmoid(gt[:, :d_model]) * ys + _sigmoid(gt[:, d_model:]) * yd,)

    (merged,) = _rowwise(merge_fn, [gates, y_sb, y_dl], [], [(d_model, BF16)], [], tm=512, name="merge")
    x1 = _mm(merged, wf_out, add=x2d, tm=512, tn=1024, tk=d_model, name="mix_out")
    (u2,) = _rowwise(lambda xv, g: (_rms_stats(xv)[0] * g,), [_in_hbm(x1)], [g_ffn], [(d_model, BF16)], [], tm=512, name="norm_ffn",
                     out_type=_hbm_array)
    u2 = _in_hbm(u2)
    half_ff = d_ff // 2

    def act_fn(hv):
        gate = hv[:, :half_ff]
        return hv, gate * _sigmoid(gate) * hv[:, half_ff:]

    h, act = _mm(u2, wf_ffn_in, tm=512, tn=d_ff, tk=d_model, name="ffn_in",
                 epilogue=(act_fn, [], [], [(d_ff, BF16), (half_ff, BF16)], []))
    def head_fn(xv, tg, g):
        xhat, r = _rms_stats(xv)
        err = xhat * g - tg
        dy = err * (1.0 / d_model)
        dx, dg_rows = _rms_bwd(dy, xhat, r, g)
        loss_lanes = (0.5 / d_model) * jnp.sum(err * err, axis=0, keepdims=True)
        return dx, dx, jnp.sum(dg_rows, axis=0, keepdims=True), loss_lanes

    dx2, dx2_b, dg_fin, loss_lanes = _mm(
        act, wf_ffn_out, add=x1, tm=512, tn=1024, tk=d_ff, name="ffn_out",
        epilogue=(head_fn, [tgt2d], [g_fin], [(d_model, F32), (d_model, BF16)], [(1, d_model), (1, d_model)]))

    def dact_fn(da, hv):
        gate, up = hv[:, :half_ff], hv[:, half_ff:]
        sg = _sigmoid(gate)
        dgate = da * up * (sg * (1.0 + gate * (1.0 - sg)))
        return (jnp.concatenate([dgate, da * (gate * sg)], axis=1),)

    dx2_b = _in_hbm(dx2_b)
    (dh,) = _mm(dx2_b, wf_ffn_out, tb=True, tm=512, tn=half_ff, tk=d_model, name="ffn_out_dx",
                epilogue=(dact_fn, [h], [], [(d_ff, BF16)], []))
    gw_ffn_out = _mm(act, dx2_b, ta=True, tm=256, tn=d_model, tk=t, name="ffn_out_dw")
    def norm_bwd_fn(du_, dres, xv, g):
        xhat, r = _rms_stats(xv)
        dx, dg_rows = _rms_bwd(du_, xhat, r, g)
        return dres + dx, jnp.sum(dg_rows, axis=0, keepdims=True)

    def norm_bwd_twice(*args):
        dx, dg = norm_bwd_fn(*args)
        return dx, dx, dg

    dx1, dx1_b, dg_ffn = _mm(dh, wf_ffn_in, tb=True, tm=512, tn=1024, tk=2 * d_ff, name="ffn_in_dx",
                             epilogue=(norm_bwd_twice, [dx2, x1], [g_ffn], [(d_model, F32), (d_model, BF16)], [(1, d_model)]))
    gwt_ffn_in = _mm(dh, u2, ta=True, tm=512, tn=d_model, tk=t, name="ffn_in_dw")

    dx1_b = _in_hbm(dx1_b)
    dmerged = _mm(dx1_b, wf_out, tb=True, out_dtype=BF16, tm=512, tn=1024, tk=d_model, name="mix_out_dx",
                  out_type=_hbm_array)
    gw_out = _mm(merged, dx1_b, ta=True, tm=256, tn=d_model, tk=t, name="mix_out_dw")

    def merge_bwd_fn(gt, ys, yd, dm):
        s_sb, s_dl = _sigmoid(gt[:, :d_model]), _sigmoid(gt[:, d_model:])
        dgates = jnp.concatenate([dm * ys * s_sb * (1.0 - s_sb), dm * yd * s_dl * (1.0 - s_dl)], axis=1)
        return dgates, dm * s_sb, dm * s_dl

    full_big = _chip_major({"w_out": gw_out, "w_ffn_in": gwt_ffn_in, "w_ffn_out": gw_ffn_out})
    dgates, dy_sb, dy_dl, others_big = _rowwise(
        merge_bwd_fn, [gates, y_sb, y_dl, dmerged], [], [(2 * d_model, BF16), (d_model, BF16), (d_model, BF16)], [],
        tm=256, name="merge_bwd", carry=_pair_carry(full_big))
    pair_big = _pair_sums(full_big, others_big, LATE[2:])
    do_sb = _mm(dy_sb, wf_sb_up, tb=True, out_dtype=BF16, tm=1024, tn=SB_WIDTH, tk=d_model, name="sb_up_dx")
    gw_sb_up = _mm(o_sb2, dy_sb, ta=True, tm=SB_WIDTH, tn=1024, tk=512, name="sb_up_dw")
    do_dl = _mm(dy_dl, wf_dil_up, tb=True, tm=1024, tn=DIL_OUT_WIDTH, tk=d_model, name="dil_up_dx")
    gw_dil_up = _mm(o_dl2, dy_dl, ta=True, tm=DIL_OUT_WIDTH, tn=1024, tk=512, name="dil_up_dw")
    full_small = _chip_major({"w_sb_up": gw_sb_up, "w_dil_up": gw_dil_up})
    (dq_sb, dk_sb, dv_sb), brought = _sb_bwd(
        qkv3, o_sb, do_sb.reshape(b_sz, s_len, SB_WIDTH), b_sz, s_len,
        _chip_carry([p[1] for p in pair_big[:2]], LATE[2:4]) + _pair_carry(full_small))
    pair_small = _pair_sums(full_small, brought[2:], LATE[:2])
    (dq_dl, dk_dl, dv_dl), landed_b = _dil_bwd(
        qkv3, o_dl, lse, do_dl.reshape(b_sz, s_len, DIL_OUT_WIDTH), b_sz, s_len,
        _chip_carry([pair_big[2][1], pair_small[0][1], pair_small[1][1]], [LATE[4], LATE[0], LATE[1]]))
    pair = pair_small + pair_big
    landed = [landed_b[1], landed_b[2], brought[0], brought[1], landed_b[0]]
    dproj = [a.reshape(t, -1) for a in (dq_sb, dk_sb, dv_sb, dq_dl, dk_dl, dv_dl)] + [dgates]
    gwt_in, gwt_in_b = _mm(dproj, u, ta=True, tm=256, tn=d_model, tk=t, name="proj_dw",
                           epilogue=(lambda tile: (tile, tile), [], [], [(d_model, F32), (d_model, BF16)], []))
    full_in = _chip_major({"w_in": gwt_in})
    pair_in = _pair_sums(full_in, _pair_exchange(_chip_major({"w_in": gwt_in_b}), "w_in"), ["w_in"])
    (dx, dg_mix), landed_in = _mm(
        dproj, wt_in, tm=512, tn=1024, tk=wt_in.shape[0], name="proj_dx",
        carry=_chip_carry([p[1] for p in pair_in], ["w_in"]),
        epilogue=(norm_bwd_fn, [dx1, x2d], [g_mix], [(d_model, F32)], [(1, d_model)]))

    grads = _chip_sums(pair, landed, LATE)
    grads.update(_chip_sums(pair_in, landed_in, ["w_in"]))
    return dx, grads, dg_mix, dg_ffn, dg_fin, loss_lanes


def kernel(x, norm_mix_g, w_in, w_sb_up, w_dil_up, w_out, norm_ffn_g, w_ffn_in, w_ffn_out, norm_final_g, loss_target, m_norm_mix_g, m_w_in, m_w_sb_up, m_w_dil_up, m_w_out, m_norm_ffn_g, m_w_ffn_in, m_w_ffn_out, m_norm_final_g, v_norm_mix_g, v_w_in, v_w_sb_up, v_w_dil_up, v_w_out, v_norm_ffn_g, v_w_ffn_in, v_w_ffn_out, v_norm_final_g):
    b_sz, s_len, d_model = x.shape
    d_ff = w_ffn_out.shape[1] * N_CHIPS
    g_mix, g_ffn, g_fin = norm_mix_g, norm_ffn_g, norm_final_g.reshape(1, d_model)

    names = ["w_in", "w_sb_up", "w_dil_up", "w_out", "w_ffn_in", "w_ffn_out"]
    shards = {"w_in": jnp.swapaxes(w_in[0], 0, 1), "w_sb_up": w_sb_up[0], "w_dil_up": w_dil_up[0], "w_out": w_out[0],
              "w_ffn_in": w_ffn_in[0], "w_ffn_out": w_ffn_out[0]}
    slab_in = _cast_to_slab(shards["w_in"], "cast_w_in")
    late_slabs = [_cast_to_slab(shards[k], "cast_" + k, k in SWAPPED) for k in LATE]

    dx, grads, dg_mix, dg_ffn, dg_fin, loss_lanes = _fwd_bwd(
        x, loss_target, g_mix, g_ffn, g_fin, slab_in, late_slabs)

    small = jnp.concatenate([dg_mix, dg_ffn, dg_fin, loss_lanes, jnp.zeros((4, d_model), F32)], axis=0)
    full_grads, small = _final_exchange([grads[k] for k in names], small)
    grads = dict(zip(names, full_grads))
    grads["w_ffn_in"] = jnp.swapaxes(grads["w_ffn_in"], 0, 1)
    loss = small[3, 0]
    gains = jnp.concatenate([g_mix, g_ffn, g_fin, jnp.zeros((5, d_model), F32)], axis=0)
    gains_m = jnp.concatenate([m_norm_mix_g, m_norm_ffn_g, m_norm_final_g.reshape(1, d_model), jnp.zeros((5, d_model), F32)], axis=0)
    gains_v = jnp.concatenate([v_norm_mix_g, v_norm_ffn_g, v_norm_final_g.reshape(1, d_model), jnp.ones((5, d_model), F32)], axis=0)
    gd, gm, gv = _rowwise(_adamw_math, [gains, small, gains_m, gains_v], [], [(d_model, F32)] * 3, [], tm=8, name="adamw_gains")

    moments = {"w_in": (jnp.swapaxes(m_w_in[0], 0, 1), jnp.swapaxes(v_w_in[0], 0, 1)),
               "w_sb_up": (m_w_sb_up[0], v_w_sb_up[0]), "w_dil_up": (m_w_dil_up[0], v_w_dil_up[0]),
               "w_out": (m_w_out[0], v_w_out[0]), "w_ffn_in": (m_w_ffn_in[0], v_w_ffn_in[0]),
               "w_ffn_out": (m_w_ffn_out[0], v_w_ffn_out[0])}
    upd = {k: _adamw(shards[k], grads[k], moments[k][0], moments[k][1], "adamw_" + k) for k in names}

    def as_output(k, a):
        return (jnp.swapaxes(a, 0, 1) if k == "w_in" else a)[None]

    def w_out_of(i):
        return [as_output(k, upd[k][i]) for k in names]

    def ordered(mix, ws, ffn_g, fin):
        return [mix, ws[0], ws[1], ws[2], ws[3], ffn_g, ws[4], ws[5], fin]

    grad_ws = [as_output(k, grads[k]) for k in names]
    outs = [loss, dx.reshape(b_sz, s_len, d_model)]
    outs += ordered(small[0:1], grad_ws, small[1:2], small[2])
    outs += ordered(gd[0:1], w_out_of(0), gd[1:2], gd[2])
    outs += ordered(gm[0:1], w_out_of(1), gm[1:2], gm[2])
    outs += ordered(gv[0:1], w_out_of(2), gv[1:2], gv[2])
    return tuple(outs)
```

```python
import functools
import math

import jax
import jax.numpy as jnp
from jax import lax
from jax.experimental import pallas as pl
from jax.experimental.pallas import tpu as pltpu

F32 = jnp.float32
BF16 = jnp.bfloat16
MESH = pl.DeviceIdType.MESH

HEAD_DIM = 64
SB_HEADS = 8
DIL_PAIRS = ((128, 1), (512, 4), (2048, 16))
DIL_HEADS_PER_GROUP = 4
DIL_HEADS = DIL_HEADS_PER_GROUP * len(DIL_PAIRS)
SB_WIDTH = SB_HEADS * HEAD_DIM
DIL_WIDTH = DIL_HEADS * HEAD_DIM
DIL_OUT_WIDTH = DIL_HEADS_PER_GROUP * HEAD_DIM
QKV_WIDTH = 3 * SB_WIDTH + 3 * DIL_WIDTH
RMS_EPS = 1e-6
ALIBI_MAX_BIAS = 8.0
ADAM_LR = 0.001
ADAM_B1 = 0.9
ADAM_B2 = 0.999
ADAM_EPS = 1e-08
ADAM_WD = 0.01
ADAM_STEP = 10

LANES = 128
BLK = 128
NEG = -1e30
EXP_UNDERFLOW = -104.0
SB_FWD_CHAINS = 4
SB_BWD_CHAINS = 4
DIL_CHAINS = 4
N_CHIPS = 4
VMEM_CAP = 56 * 1024 * 1024


def _vmem_limit(tile_bytes):
    return int(min(VMEM_CAP, max(32 * 1024 * 1024, 3 * tile_bytes + 8 * 1024 * 1024)))


def _hbm_array(shape, dtype):
    return pltpu.HBM(shape, dtype)


def _nbytes(shape, dtype):
    return math.prod(shape) * jnp.dtype(dtype).itemsize


def _in_hbm(x):
    return pltpu.with_memory_space_constraint(x, pltpu.HBM)


def _dot(a, b):
    return jnp.dot(a, b, preferred_element_type=F32)


def _dot_nt(a, b):
    return lax.dot_general(a, b, (((1,), (1,)), ((), ())), preferred_element_type=F32)


def _dot_tn(a, b):
    return lax.dot_general(a, b, (((0,), (0,)), ((), ())), preferred_element_type=F32)


def _split2(x):
    hi = x.astype(BF16)
    lo = (x - hi.astype(F32)).astype(BF16)
    return hi, lo


def _sigmoid(x):
    return pl.reciprocal(1.0 + jnp.exp(-x), approx=True)


class _Carry:
    def __init__(self, arrays=(), out_shapes=(), aliased=False, sems=(), start=None, finish=None):
        self.arrays, self.out_shapes = list(arrays), list(out_shapes)
        self.n_aliased = len(self.arrays) if aliased is True else int(aliased)
        self.sems, self.start, self.finish = list(sems), start, finish

    def __bool__(self):
        return bool(self.arrays)

    def __add__(self, other):
        assert not other.n_aliased and self.n_aliased in (0, len(self.out_shapes))
        n_a, n_o, n_s = len(self.arrays), len(self.out_shapes), len(self.sems)
        return _Carry(
            self.arrays + other.arrays, self.out_shapes + other.out_shapes, self.n_aliased, self.sems + other.sems,
            lambda i, o, s: (self.start(i[:n_a], o[:n_o], s[:n_s]), other.start(i[n_a:], o[n_o:], s[n_s:])),
            lambda i, o, s: (self.finish(i[:n_a], o[:n_o], s[:n_s]), other.finish(i[n_a:], o[n_o:], s[n_s:])))

    def call_args(self, n_in, n_out):
        aliases = {n_in + k: n_out + k for k in range(self.n_aliased)}
        return _hbm_specs(len(self.arrays)), _hbm_specs(len(self.out_shapes)), self.out_shapes, aliases, self.sems

    def run(self, refs, n_in, n_out, step, n_steps, compute):
        if not self:
            compute()
            return
        n_c, n_o, n_s = len(self.arrays), len(self.out_shapes), len(self.sems)
        ins = refs[n_in:n_in + n_c]
        outs = refs[n_in + n_c + n_out:n_in + n_c + n_out + n_o]
        sems = refs[len(refs) - n_s:]

        @pl.when(step == 0)
        def _():
            self.start(ins, outs, sems)

        compute()

        @pl.when(step == n_steps - 1)
        def _():
            self.finish(ins, outs, sems)


def _mm(a, b, *, ta=False, tb=False, add=None, out_dtype=F32, tm, tn, tk, name, carry=None, epilogue=None,
        b_cols=None, out_type=jax.ShapeDtypeStruct):
    carry = carry or _Carry()
    n_car = len(carry.arrays)
    pieces = list(a) if isinstance(a, (list, tuple)) else [a]
    n_a = len(pieces)
    widths = [p.shape[1] for p in pieces]
    starts = [sum(widths[:p]) for p in range(n_a)]
    if ta:
        kdim, m = pieces[0].shape[0], sum(widths)
    else:
        m, kdim = pieces[0].shape[0], sum(widths)
    if tb:
        n, k2 = b.shape
    else:
        k2, n = b.shape
    col0 = 0
    if b_cols is not None:
        assert b_cols[0] % tn == 0, name
        col0, n = b_cols[0] // tn, b_cols[1]
    assert kdim == k2 and m % tm == 0 and n % tn == 0 and kdim % tk == 0, (name, a.shape, b.shape)
    nk = kdim // tk
    assert n_a == 1 or (nk == 1 and not tb and (not ta or all(w % tm == 0 for w in widths))), name
    grid = (m // tm, n // tn, nk)
    a_mode = dict(pipeline_mode=pl.Buffered(1)) if grid[0] == 1 and nk == 1 else {}
    b_mode = dict(pipeline_mode=pl.Buffered(1)) if grid[1] == 1 and nk == 1 else {}
    if n_a == 1:
        a_specs = [pl.BlockSpec((tk, tm), lambda i, j, k: (k, i), **a_mode) if ta
                   else pl.BlockSpec((tm, tk), lambda i, j, k: (i, k), **a_mode)]
    elif ta:
        a_specs = [pl.BlockSpec((tk, tm), lambda i, j, k, s=s // tm, w=w // tm: (0, jnp.clip(i - s, 0, w - 1)))
                   for s, w in zip(starts, widths)]
    else:
        a_specs = [pl.BlockSpec((tm, w), lambda i, j, k: (i, 0)) for w in widths]
    b_spec = (pl.BlockSpec((tn, tk), lambda i, j, k: (j + col0, k), **b_mode) if tb
              else pl.BlockSpec((tk, tn), lambda i, j, k: (k, j + col0), **b_mode))
    o_spec = pl.BlockSpec((tm, tn), lambda i, j, k: (i, j))
    dims = ((((0,) if ta else (1,)), ((1,) if tb else (0,))), ((), ()))
    has_add = add is not None
    if epilogue is None:
        ep_fn, ep_rows, ep_params, ep_outs, ep_accs = None, [], [], [], []
        out_sds, out_specs = [out_type((m, n), out_dtype)], [o_spec]
    else:
        ep_fn, ep_rows, ep_params, ep_outs, ep_accs = epilogue
        assert grid[1] == 1 or not ep_accs, name
        out_sds = [out_type((m, w * grid[1]), d) for w, d in ep_outs]
        out_sds += [jax.ShapeDtypeStruct(sh, F32) for sh in ep_accs]
        out_specs = [pl.BlockSpec((tm, w), lambda i, j, k: (i, j)) for w, _ in ep_outs]
        out_specs += [pl.BlockSpec(sh, lambda i, j, k: (0, 0)) for sh in ep_accs]
    n_main = len(out_sds)
    use_scratch = nk > 1 and (ep_fn is not None or jnp.dtype(out_dtype) != jnp.dtype(F32))
    n_in = n_a + 1 + has_add + len(ep_rows) + len(ep_params)

    def finish(total, refs, pid):
        outs = refs[n_in + n_car:n_in + n_car + n_main]
        if ep_fn is None:
            outs[0][...] = total.astype(out_dtype)
            return
        first = n_a + 1 + has_add
        rows = [r[...].astype(F32) for r in refs[first:first + len(ep_rows)]]
        params = [p[...] for p in refs[first + len(ep_rows):n_in]]
        res = ep_fn(total, *rows, *params)
        for o_ref, v in zip(outs[:len(ep_outs)], res):
            o_ref[...] = v.astype(o_ref.dtype)
        acc_refs = outs[len(ep_outs):]
        if acc_refs:
            @pl.when(pid[0] == 0)
            def _():
                for r in acc_refs:
                    r[...] = jnp.zeros(r.shape, F32)

            for r, v in zip(acc_refs, res[len(ep_outs):]):
                r[...] += v

    def compute(refs, pid):
        a_ref, b_ref = refs[0], refs[n_a]
        add_ref = refs[n_a + 1] if has_add else None

        def dot(x, y):
            return lax.dot_general(x.astype(BF16), y.astype(BF16), dims, preferred_element_type=F32)

        if n_a > 1 and ta:
            for p_ref, s, w in zip(refs[:n_a], starts, widths):
                @pl.when((pid[0] >= s // tm) & (pid[0] < (s + w) // tm))
                def _(p_ref=p_ref):
                    prod = dot(p_ref[...], b_ref[...])
                    finish(prod + add_ref[...] if has_add else prod, refs, pid)
            return
        if n_a > 1:
            prod = dot(a_ref[...], b_ref[:widths[0], :])
            for p_ref, s, w in zip(refs[1:n_a], starts[1:], widths[1:]):
                prod += dot(p_ref[...], b_ref[s:s + w, :])
        else:
            prod = dot(a_ref[...], b_ref[...])
        if nk == 1:
            finish(prod + add_ref[...] if has_add else prod, refs, pid)
            return
        acc_ref = refs[n_in + n_car + n_main + len(carry.out_shapes)] if use_scratch else refs[n_in + n_car]
        k = pid[2]

        @pl.when(k == 0)
        def _():
            acc_ref[...] = prod + add_ref[...] if has_add else prod

        @pl.when(k > 0)
        def _():
            acc_ref[...] += prod

        if use_scratch:
            @pl.when(k == nk - 1)
            def _():
                finish(acc_ref[...], refs, pid)

    def body(*refs):
        pid = (pl.program_id(0), pl.program_id(1), pl.program_id(2))
        step = (pid[0] * grid[1] + pid[1]) * nk + pid[2]
        carry.run(refs, n_in, n_main, step, grid[0] * grid[1] * nk, lambda: compute(refs, pid))

    tile_bytes = ((n_a if ta else 1) * _nbytes((tm, tk), pieces[0].dtype)
                  + _nbytes((tk, tn), b.dtype) + 2 * _nbytes((tm, tn), F32)
                  + (_nbytes((tm, tn), F32) if has_add else 0)
                  + sum(_nbytes((tm, r.shape[1]), r.dtype) for r in ep_rows) + sum(_nbytes((tm, w), d) for w, d in ep_outs))
    in_specs = a_specs + [b_spec] + ([o_spec] if has_add else [])
    in_specs += [pl.BlockSpec((tm, r.shape[1] // grid[1]), lambda i, j, k: (i, j)) for r in ep_rows]
    in_specs += [pl.BlockSpec(p.shape, lambda i, j, k: (0, 0)) for p in ep_params]
    args = tuple(pieces) + (b,) + ((add,) if has_add else ()) + tuple(ep_rows) + tuple(ep_params)
    scratch = [pltpu.VMEM((tm, tn), F32)] if use_scratch else []
    serial = bool(carry) or bool(ep_accs)
    c_in, c_out, c_shapes, c_alias, c_sems = carry.call_args(n_in, n_main)
    res = pl.pallas_call(
        body, name=name, grid=grid,
        in_specs=in_specs + c_in, out_specs=out_specs + c_out, out_shape=out_sds + c_shapes,
        input_output_aliases=c_alias, scratch_shapes=scratch + c_sems,
        compiler_params=pltpu.CompilerParams(
            dimension_semantics=("arbitrary",) * 3 if serial else ("parallel", "parallel", "arbitrary"),
            vmem_limit_bytes=_vmem_limit(tile_bytes)),
    )(*args, *carry.arrays)
    main = res[0] if ep_fn is None else list(res[:n_main])
    return (main, res[n_main:]) if carry else main


def _rowwise(fn, rows, params, outs, accs, *, tm, name, carry=None, out_type=jax.ShapeDtypeStruct):
    carry = carry or _Carry()
    t = rows[0].shape[0]
    assert t % tm == 0, (name, t, tm)
    n_r, n_p, n_o, n_c = len(rows), len(params), len(outs), len(carry.arrays)

    def compute(refs, first):
        vals = [r[...].astype(F32) for r in refs[:n_r]] + [p[...] for p in refs[n_r:n_r + n_p]]
        res = fn(*vals)
        o_refs = refs[n_r + n_p + n_c:n_r + n_p + n_c + n_o]
        a_refs = refs[n_r + n_p + n_c + n_o:n_r + n_p + n_c + n_o + len(accs)]
        for o_ref, v in zip(o_refs, res[:n_o]):
            o_ref[...] = v.astype(o_ref.dtype)
        if accs:
            @pl.when(first)
            def _():
                for a_ref in a_refs:
                    a_ref[...] = jnp.zeros(a_ref.shape, F32)

            for a_ref, v in zip(a_refs, res[n_o:]):
                a_ref[...] += v

    def body(*refs):
        step = pl.program_id(0)
        carry.run(refs, n_r + n_p, n_o + len(accs), step, t // tm, lambda: compute(refs, step == 0))

    in_specs = [pl.BlockSpec((tm, r.shape[1]), lambda i: (i, 0)) for r in rows]
    in_specs += [pl.BlockSpec(p.shape, lambda i: (0, 0)) for p in params]
    out_specs = [pl.BlockSpec((tm, w), lambda i: (i, 0)) for w, _ in outs]
    out_specs += [pl.BlockSpec(s, lambda i: (0, 0)) for s in accs]
    out_shape = [out_type((t, w), d) for w, d in outs]
    out_shape += [jax.ShapeDtypeStruct(s, F32) for s in accs]
    tile_bytes = sum(_nbytes((tm, r.shape[1]), r.dtype) for r in rows) + sum(_nbytes((tm, w), F32) for w, _ in outs)
    c_in, c_out, c_shapes, c_alias, c_sems = carry.call_args(n_r + n_p, n_o + len(accs))
    res = pl.pallas_call(
        body, name=name, grid=(t // tm,), in_specs=in_specs + c_in, out_specs=out_specs + c_out,
        out_shape=out_shape + c_shapes, input_output_aliases=c_alias, scratch_shapes=c_sems,
        compiler_params=pltpu.CompilerParams(
            dimension_semantics=("arbitrary",) if accs or carry else ("parallel",),
            vmem_limit_bytes=_vmem_limit(2 * tile_bytes)),
    )(*rows, *params, *carry.arrays)
    own = n_o + len(accs)
    return (list(res[:own]) + [res[own:]]) if carry else res


def _rms_stats(x):
    r = lax.rsqrt(jnp.mean(x * x, axis=-1, keepdims=True) + RMS_EPS)
    return x * r, r


def _rms_bwd(dy, xhat, r, g):
    dxhat = dy * g
    dx = r * (dxhat - xhat * jnp.mean(dxhat * xhat, axis=-1, keepdims=True))
    return dx, dy * xhat


def _sb_consts():
    lane = lax.broadcasted_iota(jnp.int32, (BLK, LANES), 1)
    head0 = lane < HEAD_DIM
    row = lax.broadcasted_iota(jnp.int32, (2 * BLK, BLK), 0) % BLK
    col = lax.broadcasted_iota(jnp.int32, (2 * BLK, BLK), 1)
    causal = col < row
    jj = lax.broadcasted_iota(jnp.int32, (BLK, BLK), 0)
    ss = lax.broadcasted_iota(jnp.int32, (BLK, BLK), 1)
    suffix = jnp.where(jj > ss, 1.0, 0.0).astype(BF16)
    return head0, causal, suffix


def _stack_heads(x, head0):
    zero = jnp.zeros_like(x)
    return jnp.concatenate([jnp.where(head0, x, zero), jnp.where(head0, zero, x)], axis=0)


def _sb_logits(z, causal, masked):
    sp = jnp.log(1.0 + jnp.exp(-jnp.abs(z)))
    log_keep = -(jnp.maximum(z, 0.0) + sp)
    log_beta = jnp.minimum(z, 0.0) - sp
    if masked:
        log_keep = jnp.where(causal, log_keep, 0.0)
    return log_keep, log_beta


def _suffix_sums(x, suffix):
    hi, lo = _split2(x)
    after = _dot(hi, suffix) + _dot(lo, suffix)
    total = jnp.broadcast_to(after[:, 0:1] + x[:, 0:1], x.shape)
    return after, total


def _sb_walk_back(i, state, per_chain, tile):
    def alive(st):
        worst = functools.reduce(jnp.maximum, [st[p][:, 0:1] for p in range(0, len(st), per_chain)])
        return jnp.max(worst) > EXP_UNDERFLOW

    def cond(c):
        return jnp.logical_and(c[0] < i, alive(c[1]))

    def body(c):
        return c[0] + 1, tile(i - 1 - c[0], c[1], False)

    return lax.while_loop(cond, body, (jnp.int32(0), state))[1]


def _lane_blocks(x, n):
    return [x[:, p * LANES:(p + 1) * LANES] for p in range(n)]


def _sb_fwd(qkv, b_sz, s_len, carry):
    nq = s_len // BLK
    n_pairs = SB_WIDTH // LANES
    ch = SB_FWD_CHAINS
    n_steps = n_pairs // ch
    scale = 1.0 / math.sqrt(HEAD_DIM)

    def compute(q_ref, k_ref, v_ref, o_ref):
        head0, causal, suffix = _sb_consts()

        def q_block(i, _):
            qs = pl.multiple_of(i * BLK, BLK)
            q_all = (q_ref[pl.ds(qs, BLK), :] * scale).astype(BF16)
            q01 = [_stack_heads(q, head0) for q in _lane_blocks(q_all, ch)]

            def tile(j, state, masked):
                ks = pl.multiple_of(j * BLK, BLK)
                ks_ = _lane_blocks(k_ref[pl.ds(ks, BLK), :].astype(BF16), ch)
                vs_ = _lane_blocks(v_ref[pl.ds(ks, BLK), :].astype(BF16), ch)
                zs = [_dot_nt(q01[p], ks_[p]) for p in range(ch)]
                logits = [_sb_logits(z, causal, masked) for z in zs]
                sums = [_suffix_sums(lg[0], suffix) for lg in logits]
                out = []
                for p in range(ch):
                    carry, acc = state[2 * p], state[2 * p + 1]
                    after, total = sums[p]
                    a = jnp.exp(logits[p][1] + carry + after)
                    if masked:
                        a = jnp.where(causal, a, 0.0)
                    a_hi, a_lo = _split2(a)
                    a_cat = jnp.concatenate([a_hi[:BLK], a_hi[BLK:], a_lo[:BLK], a_lo[BLK:]], axis=1)
                    v01 = _stack_heads(vs_[p], head0)
                    out += [carry + total, acc + _dot(a_cat, jnp.concatenate([v01, v01], axis=0))]
                return tuple(out)

            state = (jnp.zeros((2 * BLK, BLK), F32), jnp.zeros((BLK, LANES), F32)) * ch
            state = tile(i, state, True)
            state = _sb_walk_back(i, state, 2, tile)
            o_ref[pl.ds(qs, BLK), :] = jnp.concatenate([state[2 * p + 1] for p in range(ch)], axis=1)
            return 0

        lax.fori_loop(0, nq, q_block, 0)

    def body(*refs):
        step = pl.program_id(0) * n_steps + pl.program_id(1)
        o_ref = refs[3 + len(carry.arrays)]
        carry.run(refs, 3, 1, step, b_sz * n_steps, lambda: compute(refs[0], refs[1], refs[2], o_ref))

    blk = lambda off: pl.BlockSpec((None, s_len, ch * LANES), lambda b, p: (b, 0, off + p))
    c_in, c_out, c_shapes, c_alias, c_sems = carry.call_args(3, 1)
    res = pl.pallas_call(
        body, name="sb_fwd", grid=(b_sz, n_steps),
        in_specs=[blk(0), blk(n_steps), blk(2 * n_steps)] + c_in, out_specs=[blk(0)] + c_out,
        out_shape=[jax.ShapeDtypeStruct((b_sz, s_len, SB_WIDTH), F32)] + c_shapes,
        input_output_aliases=c_alias, scratch_shapes=c_sems,
        compiler_params=pltpu.CompilerParams(dimension_semantics=("arbitrary", "arbitrary"),
                                             vmem_limit_bytes=VMEM_CAP),
    )(qkv, qkv, qkv, *carry.arrays)
    return res[0], res[1:]


def _sb_bwd(qkv, o_sb, do_sb, b_sz, s_len, carry):
    nq = s_len // BLK
    n_pairs = SB_WIDTH // LANES
    ch = SB_BWD_CHAINS
    n_steps = n_pairs // ch
    scale = 1.0 / math.sqrt(HEAD_DIM)

    def compute(q_ref, k_ref, v_ref, o_ref, do_ref, dq_ref, dk_ref, dv_ref, dk_acc, dv_acc):
        head0, causal, suffix = _sb_consts()
        lrow = lax.broadcasted_iota(jnp.int32, (LANES, LANES), 0)
        ones_h0 = jnp.where(lrow < HEAD_DIM, 1.0, 0.0).astype(BF16)
        ones_h1 = jnp.where(lrow >= HEAD_DIM, 1.0, 0.0).astype(BF16)
        dk_acc[...] = jnp.zeros(dk_acc.shape, F32)
        dv_acc[...] = jnp.zeros(dv_acc.shape, F32)

        def q_block(i, _):
            qs = pl.multiple_of(i * BLK, BLK)
            q_all = (q_ref[pl.ds(qs, BLK), :] * scale).astype(BF16)
            do_all = do_ref[pl.ds(qs, BLK), :].astype(BF16)
            dd_all = do_all.astype(F32) * o_ref[pl.ds(qs, BLK), :]
            q01 = [_stack_heads(q, head0) for q in _lane_blocks(q_all, ch)]
            do01 = [_stack_heads(d, head0) for d in _lane_blocks(do_all, ch)]
            tot = []
            for dd in _lane_blocks(dd_all, ch):
                dd_hi, dd_lo = _split2(dd)
                tot.append(jnp.concatenate([_dot(dd_hi, ones_h0) + _dot(dd_lo, ones_h0),
                                            _dot(dd_hi, ones_h1) + _dot(dd_lo, ones_h1)], axis=0))

            def tile(j, state, masked):
                ks = pl.multiple_of(j * BLK, BLK)
                ks_ = _lane_blocks(k_ref[pl.ds(ks, BLK), :].astype(BF16), ch)
                vs_ = _lane_blocks(v_ref[pl.ds(ks, BLK), :].astype(BF16), ch)
                zs = [_dot_nt(q01[p], ks_[p]) for p in range(ch)]
                das = [_dot_nt(do01[p], vs_[p]) for p in range(ch)]
                logits = [_sb_logits(z, causal, masked) for z in zs]
                sums = [_suffix_sums(lg[0], suffix) for lg in logits]
                a_s, e_s = [], []
                for p in range(ch):
                    a = jnp.exp(logits[p][1] + state[3 * p] + sums[p][0])
                    if masked:
                        a = jnp.where(causal, a, 0.0)
                    a_s.append(a)
                    e_s.append(a * das[p])
                e_sums = [_suffix_sums(e, suffix) for e in e_s]
                out, dks, dvs = [], [], []
                for p in range(ch):
                    carry, rcarry, dq = state[3 * p:3 * p + 3]
                    e = e_s[p]
                    before = tot[p] - (rcarry + e_sums[p][0] + e)
                    beta = jnp.exp(logits[p][1])
                    dz = e * (1.0 - beta) - beta * before
                    if masked:
                        dz = jnp.where(causal, dz, 0.0)
                    dz_b = dz.astype(BF16)
                    dks.append(_dot_tn(dz_b, q01[p]))
                    dvs.append(_dot_tn(a_s[p].astype(BF16), do01[p]))
                    out += [carry + sums[p][1], rcarry + e_sums[p][1], dq + _dot(dz_b, ks_[p])]
                dk_acc[pl.ds(ks, BLK), :] += jnp.concatenate(dks, axis=1)
                dv_acc[pl.ds(ks, BLK), :] += jnp.concatenate(dvs, axis=1)
                return tuple(out)

            state = (jnp.zeros((2 * BLK, BLK), F32),) * (3 * ch)
            state = tile(i, state, True)
            state = _sb_walk_back(i, state, 3, tile)
            dq = [jnp.where(head0, state[3 * p + 2][:BLK], state[3 * p + 2][BLK:]) for p in range(ch)]
            dq_ref[pl.ds(qs, BLK), :] = (jnp.concatenate(dq, axis=1) * scale).astype(dq_ref.dtype)
            return 0

        lax.fori_loop(0, nq, q_block, 0)
        dk_ref[...] = dk_acc[...].astype(dk_ref.dtype)
        dv_ref[...] = dv_acc[...].astype(dv_ref.dtype)

    def body(*refs):
        step = pl.program_id(0) * n_steps + pl.program_id(1)
        n_c, n_o = len(carry.arrays), len(carry.out_shapes)
        own = refs[:5] + refs[5 + n_c:8 + n_c] + refs[8 + n_c + n_o:10 + n_c + n_o]
        carry.run(refs, 5, 3, step, b_sz * n_steps, lambda: compute(*own))

    blk = lambda off: pl.BlockSpec((None, s_len, ch * LANES), lambda b, p: (b, 0, off + p))
    once = lambda off: pl.BlockSpec((None, s_len, ch * LANES), lambda b, p: (b, 0, off + p),
                                    pipeline_mode=pl.Buffered(1))
    out_sd = jax.ShapeDtypeStruct((b_sz, s_len, SB_WIDTH), BF16)
    c_in, c_out, c_shapes, c_alias, c_sems = carry.call_args(5, 3)
    res = pl.pallas_call(
        body, name="sb_bwd", grid=(b_sz, n_steps),
        in_specs=[once(0), once(n_steps), once(2 * n_steps), once(0), once(0)] + c_in,
        out_specs=[blk(0), blk(0), blk(0)] + c_out, out_shape=[out_sd, out_sd, out_sd] + c_shapes,
        input_output_aliases=c_alias,
        scratch_shapes=[pltpu.VMEM((s_len, ch * LANES), F32), pltpu.VMEM((s_len, ch * LANES), F32)] + c_sems,
        compiler_params=pltpu.CompilerParams(dimension_semantics=("arbitrary", "arbitrary"),
                                             vmem_limit_bytes=VMEM_CAP),
    )(qkv, qkv, qkv, o_sb, do_sb, *carry.arrays)
    return res[:3], res[3:]


def _dil_consts(group, pair_idx, dilation):
    lane = lax.broadcasted_iota(jnp.int32, (BLK, LANES), 1)
    head0 = lane < HEAD_DIM
    row = lax.broadcasted_iota(jnp.int32, (2 * BLK, BLK), 0)
    qa = row % BLK
    kb = lax.broadcasted_iota(jnp.int32, (2 * BLK, BLK), 1)
    head = (group * DIL_HEADS_PER_GROUP + 2 * pair_idx + row // BLK).astype(F32)
    slope = jnp.exp((-ALIBI_MAX_BIAS * math.log(2.0) / DIL_HEADS) * (head + 1.0))
    valid_cur = kb <= qa
    valid_prev = kb >= qa
    bias_cur = -slope * ((qa - kb) * dilation).astype(F32)
    bias_prev = -slope * ((BLK + qa - kb) * dilation).astype(F32)
    return head0, valid_cur, valid_prev, bias_cur, bias_prev


def _dil_units(s_len, dilation):
    nb = s_len // dilation // BLK
    return [(r, n) for r in range(dilation) for n in range(nb)]


def _dil_rows(n, r, dilation):
    if dilation == 1:
        return pl.ds(n * BLK, BLK)
    return pl.ds(n * BLK * dilation + r, BLK, stride=dilation)


def _dil_scores(q01, k, bias, valid):
    s = _dot_nt(q01, k) * (1.0 / math.sqrt(HEAD_DIM)) + bias
    return jnp.where(valid, s, NEG)


def _dil_fwd(qkv, b_sz, s_len, carry):
    n_pairs = DIL_OUT_WIDTH // LANES
    q_off = 3 * SB_WIDTH // LANES
    per_kind = DIL_WIDTH // LANES

    def compute(pair_idx, qkv_refs, o_ref, lse_ref, m_s, l_s):
        m_s[...] = jnp.full(m_s.shape, NEG, F32)
        l_s[...] = jnp.zeros(l_s.shape, F32)
        o_ref[...] = jnp.zeros(o_ref.shape, F32)
        for g, (_, dilation) in enumerate(DIL_PAIRS):
            q_ref, k_ref, v_ref = qkv_refs[3 * g:3 * g + 3]
            head0, valid_cur, valid_prev, bias_cur, bias_prev = _dil_consts(g, pair_idx, dilation)
            units = _dil_units(s_len, dilation)
            for u0 in range(0, len(units), DIL_CHAINS):
                group = units[u0:u0 + DIL_CHAINS]
                rows_of = [_dil_rows(n, r, dilation) for r, n in group]
                scores, values = [], []
                for (r, n), rows in zip(group, rows_of):
                    q01 = _stack_heads(q_ref[rows, :].astype(BF16), head0)
                    sc = [_dil_scores(q01, k_ref[rows, :].astype(BF16), bias_cur, valid_cur)]
                    vals = [_stack_heads(v_ref[rows, :].astype(BF16), head0)]
                    if n > 0:
                        prev = _dil_rows(n - 1, r, dilation)
                        sc.append(_dil_scores(q01, k_ref[prev, :].astype(BF16), bias_prev, valid_prev))
                        vals.append(_stack_heads(v_ref[prev, :].astype(BF16), head0))
                    scores.append(sc)
                    values.append(vals)
                stats = []
                for sc, rows in zip(scores, rows_of):
                    m_blk = functools.reduce(jnp.maximum, [jnp.max(x, axis=-1, keepdims=True) for x in sc])
                    m_old = jnp.concatenate([m_s.at[0][rows, :], m_s.at[1][rows, :]], axis=0)
                    l_old = jnp.concatenate([l_s.at[0][rows, :], l_s.at[1][rows, :]], axis=0)
                    m_new = jnp.maximum(m_old, m_blk)
                    probs = [jnp.exp(x - m_new) for x in sc]
                    l_blk = functools.reduce(jnp.add, [jnp.sum(p, axis=-1, keepdims=True) for p in probs])
                    alpha = jnp.exp(m_old - m_new)
                    stats.append((m_new, alpha * l_old + l_blk, alpha, probs))
                for (m_new, l_new, alpha, probs), vals, rows in zip(stats, values, rows_of):
                    alpha_tok = jnp.where(head0, alpha[:BLK], alpha[BLK:])
                    p_cat = jnp.concatenate(
                        [h for p in probs for h in (p[:BLK].astype(BF16), p[BLK:].astype(BF16))], axis=1)
                    o_ref[rows, :] = alpha_tok * o_ref[rows, :] + _dot(p_cat, jnp.concatenate(vals, axis=0))
                    m_s.at[0][rows, :] = m_new[:BLK]
                    m_s.at[1][rows, :] = m_new[BLK:]
                    l_s.at[0][rows, :] = l_new[:BLK]
                    l_s.at[1][rows, :] = l_new[BLK:]
        lane = lax.broadcasted_iota(jnp.int32, (BLK, LANES), 1)
        for c in range(s_len // BLK):
            rows = pl.ds(c * BLK, BLK)
            l0, l1 = l_s.at[0][rows, :], l_s.at[1][rows, :]
            o_ref[rows, :] = o_ref[rows, :] / jnp.where(lane < HEAD_DIM, l0, l1)
            lse_ref.at[0][rows, :] = m_s.at[0][rows, :] + jnp.log(l0)
            lse_ref.at[1][rows, :] = m_s.at[1][rows, :] + jnp.log(l1)

    def body(*refs):
        pair_idx = pl.program_id(1)
        step = pl.program_id(0) * n_pairs + pair_idx
        n_c, n_o = len(carry.arrays), len(carry.out_shapes)
        o_ref, lse_ref = refs[9 + n_c:11 + n_c]
        m_s, l_s = refs[11 + n_c + n_o:13 + n_c + n_o]
        carry.run(refs, 9, 2, step, b_sz * n_pairs, lambda: compute(pair_idx, refs[:9], o_ref, lse_ref, m_s, l_s))

    in_specs = []
    for g in range(len(DIL_PAIRS)):
        for kind in range(3):
            off = q_off + kind * per_kind + g * n_pairs
            in_specs.append(pl.BlockSpec((None, s_len, LANES), lambda b, p, off=off: (b, 0, off + p)))
    c_in, c_out, c_shapes, c_alias, c_sems = carry.call_args(9, 2)
    res = pl.pallas_call(
        body, name="dil_fwd", grid=(b_sz, n_pairs),
        in_specs=in_specs + c_in,
        out_specs=[pl.BlockSpec((None, s_len, LANES), lambda b, p: (b, 0, p)),
                   pl.BlockSpec((None, None, 2, s_len, LANES), lambda b, p: (b, p, 0, 0, 0))] + c_out,
        out_shape=[jax.ShapeDtypeStruct((b_sz, s_len, DIL_OUT_WIDTH), F32),
                   jax.ShapeDtypeStruct((b_sz, n_pairs, 2, s_len, LANES), F32)] + c_shapes,
        input_output_aliases=c_alias,
        scratch_shapes=[pltpu.VMEM((2, s_len, LANES), F32), pltpu.VMEM((2, s_len, LANES), F32)] + c_sems,
        compiler_params=pltpu.CompilerParams(dimension_semantics=("arbitrary", "arbitrary"),
                                             vmem_limit_bytes=VMEM_CAP),
    )(*([qkv] * 9), *carry.arrays)
    return res[0], res[1], res[2:]


def _dil_bwd(qkv, o_dl, lse, do_dl, b_sz, s_len, carry):
    n_pairs = DIL_OUT_WIDTH // LANES
    n_groups = len(DIL_PAIRS)
    q_off = 3 * SB_WIDTH // LANES
    per_kind = DIL_WIDTH // LANES

    def compute(pair_idx, group, q_ref, k_ref, v_ref, o_ref, lse_ref, do_ref, dq_ref, dk_ref, dv_ref, d_s, dq_s, dk_s, dv_s):
        lrow = lax.broadcasted_iota(jnp.int32, (LANES, LANES), 0)
        ones_h0 = jnp.where(lrow < HEAD_DIM, 1.0, 0.0).astype(BF16)
        ones_h1 = jnp.where(lrow >= HEAD_DIM, 1.0, 0.0).astype(BF16)
        for c in range(s_len // BLK):
            rows = pl.ds(c * BLK, BLK)
            dd_hi, dd_lo = _split2(do_ref[rows, :] * o_ref[rows, :])
            d_s.at[0][rows, :] = _dot(dd_hi, ones_h0) + _dot(dd_lo, ones_h0)
            d_s.at[1][rows, :] = _dot(dd_hi, ones_h1) + _dot(dd_lo, ones_h1)
        dk_s[...] = jnp.zeros(dk_s.shape, F32)
        dv_s[...] = jnp.zeros(dv_s.shape, F32)

        def one_group(g, dilation):
            head0, valid_cur, valid_prev, bias_cur, bias_prev = _dil_consts(g, pair_idx, dilation)
            units = _dil_units(s_len, dilation)
            scale = 1.0 / math.sqrt(HEAD_DIM)
            for u0 in range(0, len(units), DIL_CHAINS):
                chunk = units[u0:u0 + DIL_CHAINS]
                loaded = []
                for r, n in chunk:
                    rows = _dil_rows(n, r, dilation)
                    q01 = _stack_heads(q_ref[rows, :].astype(BF16), head0)
                    do01 = _stack_heads(do_ref[rows, :].astype(BF16), head0)
                    lse01 = jnp.concatenate([lse_ref.at[0][rows, :], lse_ref.at[1][rows, :]], axis=0)
                    d01 = jnp.concatenate([d_s.at[0][rows, :], d_s.at[1][rows, :]], axis=0)
                    blocks = [(rows, bias_cur, valid_cur)]
                    if n > 0:
                        blocks.append((_dil_rows(n - 1, r, dilation), bias_prev, valid_prev))
                    parts = []
                    for krows, bias, valid in blocks:
                        k = k_ref[krows, :].astype(BF16)
                        v = v_ref[krows, :].astype(BF16)
                        parts.append((krows, k, _dil_scores(q01, k, bias, valid), _dot_nt(do01, v)))
                    loaded.append((rows, q01, do01, lse01, d01, parts))
                grads = []
                for rows, q01, do01, lse01, d01, parts in loaded:
                    for krows, k, sc, dp in parts:
                        p = jnp.exp(sc - lse01)
                        grads.append((p.astype(BF16), (p * (dp - d01) * scale).astype(BF16)))
                it = iter(grads)
                updates = []
                for rows, q01, do01, lse01, d01, parts in loaded:
                    dq = jnp.zeros((2 * BLK, LANES), F32)
                    for krows, k, sc, dp in parts:
                        p_b, ds = next(it)
                        dq = dq + _dot(ds, k)
                        updates.append((krows, _dot_tn(ds, q01), _dot_tn(p_b, do01)))
                    dq_s[rows, :] = jnp.where(head0, dq[:BLK], dq[BLK:])
                for krows, dk, dv in updates:
                    dk_s[krows, :] = dk_s[krows, :] + dk
                    dv_s[krows, :] = dv_s[krows, :] + dv

        for g, (_, dilation) in enumerate(DIL_PAIRS):
            pl.when(group == g)(functools.partial(one_group, g, dilation))
        dq_ref[...] = dq_s[...].astype(dq_ref.dtype)
        dk_ref[...] = dk_s[...].astype(dk_ref.dtype)
        dv_ref[...] = dv_s[...].astype(dv_ref.dtype)

    def body(*refs):
        pair_idx, group = pl.program_id(1), pl.program_id(2)
        step = (pl.program_id(0) * n_pairs + pair_idx) * n_groups + group
        n_c, n_o = len(carry.arrays), len(carry.out_shapes)
        own = refs[:6] + refs[6 + n_c:9 + n_c] + refs[9 + n_c + n_o:13 + n_c + n_o]
        carry.run(refs, 6, 3, step, b_sz * n_pairs * n_groups, lambda: compute(pair_idx, group, *own))

    def qkv_spec(kind):
        return pl.BlockSpec((None, s_len, LANES),
                            lambda b, p, g: (b, 0, q_off + kind * per_kind + g * n_pairs + p))

    tok_spec = pl.BlockSpec((None, s_len, LANES), lambda b, p, g: (b, 0, p))
    out_spec = pl.BlockSpec((None, s_len, LANES), lambda b, p, g: (b, 0, g * n_pairs + p))
    out_sd = jax.ShapeDtypeStruct((b_sz, s_len, DIL_WIDTH), BF16)
    c_in, c_out, c_shapes, c_alias, c_sems = carry.call_args(6, 3)
    res = pl.pallas_call(
        body, name="dil_bwd", grid=(b_sz, n_pairs, n_groups),
        in_specs=[qkv_spec(0), qkv_spec(1), qkv_spec(2), tok_spec,
                  pl.BlockSpec((None, None, 2, s_len, LANES), lambda b, p, g: (b, p, 0, 0, 0)), tok_spec] + c_in,
        out_specs=[out_spec, out_spec, out_spec] + c_out,
        out_shape=[out_sd, out_sd, out_sd] + c_shapes,
        input_output_aliases=c_alias,
        scratch_shapes=[pltpu.VMEM((2, s_len, LANES), F32)] + [pltpu.VMEM((s_len, LANES), F32)] * 3 + c_sems,
        compiler_params=pltpu.CompilerParams(dimension_semantics=("arbitrary", "arbitrary", "arbitrary"),
                                             vmem_limit_bytes=VMEM_CAP),
    )(qkv, qkv, qkv, o_dl, lse, do_dl, *carry.arrays)
    return res[:3], res[3:]


def _mesh_pos():
    return lax.axis_index("x"), lax.axis_index("y"), lax.axis_index("c")


def _other_chips(x, y):
    return [(1 - x, y), (x, 1 - y), (1 - x, 1 - y)]


def _hbm_specs(n):
    return [pl.BlockSpec(memory_space=pl.ANY)] * n


SWAPPED = ("w_ffn_in",)


def _slot(x, y, swapped):
    return 2 * y + x if swapped else 2 * x + y


def _cast_to_slab(w, name):
    rows, cols = w.shape
    mine = jnp.reshape(_slot(lax.axis_index("x"), lax.axis_index("y"), False), (1,)).astype(jnp.int32)

    def body(idx_ref, w_ref, o_ref):
        o_ref[...] = w_ref[...].astype(BF16)

    return pl.pallas_call(
        body, name=name,
        grid_spec=pltpu.PrefetchScalarGridSpec(
            num_scalar_prefetch=1, grid=(1,),
            in_specs=[pl.BlockSpec((rows, cols), lambda i, idx: (0, 0))],
            out_specs=pl.BlockSpec((None, rows, cols), lambda i, idx: (idx[0], 0, 0))),
        out_shape=_hbm_array((N_CHIPS, rows, cols), BF16),
        compiler_params=pltpu.CompilerParams(vmem_limit_bytes=_vmem_limit(rows * cols * 6)),
    )(mine, w)


def _gather_issue(slabs, send_sems, recv_sems, swapped):
    x, y, c = _mesh_pos()
    for k, slab in enumerate(slabs):
        half = slab.shape[1] // 2
        rows = slab.at[_slot(x, y, swapped[k]), pl.ds(c * half, half), :]
        for r, (px, py) in enumerate(_other_chips(x, y)):
            pltpu.make_async_remote_copy(
                src_ref=rows, dst_ref=rows, send_sem=send_sems.at[6 * k + r], recv_sem=recv_sems.at[6 * k + r],
                device_id=(px, py, c), device_id_type=MESH).start()


def _gather_complete(slabs, send_sems, recv_sems, swapped):
    x, y, c = _mesh_pos()
    chips = _other_chips(x, y)

    def copy(k, sem, block, rows, to):
        ref = slabs[k].at[block, rows, :]
        return pltpu.make_async_remote_copy(
            src_ref=ref, dst_ref=ref, send_sem=send_sems.at[sem], recv_sem=recv_sems.at[sem],
            device_id=to, device_id_type=MESH)

    for k, slab in enumerate(slabs):
        half = slab.shape[1] // 2
        for r, (px, py) in enumerate(chips):
            theirs = _slot(px, py, swapped[k])
            copy(k, 6 * k + r, theirs, pl.ds(c * half, half), (px, py, c)).wait_recv()
            copy(k, 6 * k + 3 + r, theirs, pl.ds(c * half, half), (x, y, 1 - c)).start()
    for k, slab in enumerate(slabs):
        half = slab.shape[1] // 2
        for r, (px, py) in enumerate(chips):
            copy(k, 6 * k + 3 + r, _slot(px, py, swapped[k]), pl.ds((1 - c) * half, half), (x, y, 1 - c)).wait_recv()
    for k, slab in enumerate(slabs):
        half = slab.shape[1] // 2
        for r, (px, py) in enumerate(chips):
            copy(k, 6 * k + r, _slot(x, y, swapped[k]), pl.ds(c * half, half), (px, py, c)).wait_send()
            copy(k, 6 * k + 3 + r, _slot(px, py, swapped[k]), pl.ds(c * half, half), (x, y, 1 - c)).wait_send()


def _gather_sems(n):
    return [pltpu.SemaphoreType.DMA((6 * n,)), pltpu.SemaphoreType.DMA((6 * n,))]


def _gather_carry(slabs, names):
    swapped = [k in SWAPPED for k in names]
    return _Carry(slabs, [_hbm_array(a.shape, a.dtype) for a in slabs], True, _gather_sems(len(slabs)),
                  lambda ins, outs, sems: _gather_issue(outs, *sems, swapped),
                  lambda ins, outs, sems: _gather_complete(outs, *sems, swapped))


def _cast_carry(shards, names):
    swapped = [k in SWAPPED for k in names]

    def start(ins, outs, sems):
        x, y, _ = _mesh_pos()
        for w, slab, sw in zip(ins, outs, swapped):
            def cast(f32_buf, bf16_buf, sem, w=w, slab=slab, sw=sw):
                load = pltpu.make_async_copy(w, f32_buf, sem)
                load.start()
                load.wait()
                bf16_buf[...] = f32_buf[...].astype(BF16)
                store = pltpu.make_async_copy(bf16_buf, slab.at[_slot(x, y, sw)], sem)
                store.start()
                store.wait()

            pl.run_scoped(cast, pltpu.VMEM(w.shape, F32), pltpu.VMEM(w.shape, BF16), pltpu.SemaphoreType.DMA)

    return _Carry(shards, [_hbm_array((N_CHIPS,) + w.shape, BF16) for w in shards], False, [], start,
                  lambda ins, outs, sems: None)


def _pair_copies(ins, outs, send_sems, recv_sems):
    x, y, c = _mesh_pos()
    copies = []
    for k, g in enumerate(ins):
        half = g.shape[1] // 2
        copies.append(pltpu.make_async_remote_copy(
            src_ref=g.at[:, pl.ds((1 - c) * half, half), :], dst_ref=outs[k],
            send_sem=send_sems.at[k], recv_sem=recv_sems.at[k],
            device_id=(x, y, 1 - c), device_id_type=MESH))
    return copies


def _pair_carry(grads):
    n = len(grads)

    def start(ins, outs, sems):
        for cp in _pair_copies(ins, outs, *sems):
            cp.start()

    def finish(ins, outs, sems):
        for cp in _pair_copies(ins, outs, *sems):
            cp.wait()

    return _Carry(grads, [jax.ShapeDtypeStruct((N_CHIPS, g.shape[1] // 2, g.shape[2]), g.dtype) for g in grads], False,
                  [pltpu.SemaphoreType.DMA((n,)), pltpu.SemaphoreType.DMA((n,))], start, finish)


def _pair_exchange(grads, tag):
    carry = _pair_carry(grads)
    n = len(grads)

    def body(*refs):
        carry.start(refs[:n], refs[n:2 * n], refs[2 * n:])
        carry.finish(refs[:n], refs[n:2 * n], refs[2 * n:])

    return pl.pallas_call(
        body, name="grad_pair_exchange_" + tag, in_specs=_hbm_specs(n), out_specs=_hbm_specs(n),
        out_shape=carry.out_shapes, scratch_shapes=carry.sems,
    )(*grads)


def _pair_sum(grad, other, name, swapped):
    _, rows, cols = grad.shape
    half = rows // 2
    x, y, c = _mesh_pos()
    idx = jnp.stack([c, _slot(x, y, swapped)]).astype(jnp.int32)

    def body(idx_ref, g_ref, p_ref, own_ref, sb_ref):
        s = g_ref[...] + p_ref[...].astype(F32)
        sb_ref[...] = s.astype(BF16)

        @pl.when(pl.program_id(0) == idx_ref[1])
        def _():
            own_ref[...] = s

    blk = pl.BlockSpec((None, half, cols), lambda p, idx: (p, 0, 0))
    return pl.pallas_call(
        body, name=name,
        grid_spec=pltpu.PrefetchScalarGridSpec(
            num_scalar_prefetch=1, grid=(N_CHIPS,),
            in_specs=[pl.BlockSpec((None, half, cols), lambda p, idx: (p, idx[0], 0)), blk],
            out_specs=[pl.BlockSpec((half, cols), lambda p, idx: (0, 0)), blk]),
        out_shape=[jax.ShapeDtypeStruct((half, cols), F32), jax.ShapeDtypeStruct((N_CHIPS, half, cols), BF16)],
        compiler_params=pltpu.CompilerParams(dimension_semantics=("arbitrary",),
                                             vmem_limit_bytes=_vmem_limit(4 * half * cols * 4)),
    )(idx, grad, other)


def _chip_copies(sums_bf16, lands, send_sems, recv_sems, swapped):
    x, y, c = _mesh_pos()
    return [pltpu.make_async_remote_copy(
        src_ref=sums_bf16[k].at[_slot(px, py, swapped[k])], dst_ref=lands[k].at[r],
        send_sem=send_sems.at[3 * k + r], recv_sem=recv_sems.at[3 * k + r],
        device_id=(px, py, c), device_id_type=MESH)
        for k in range(len(sums_bf16)) for r, (px, py) in enumerate(_other_chips(x, y))]


def _chip_carry(sums_bf16, names):
    swapped = [k in SWAPPED for k in names]

    def start(ins, outs, sems):
        for cp in _chip_copies(ins, outs, *sems, swapped):
            cp.start()

    def finish(ins, outs, sems):
        for cp in _chip_copies(ins, outs, *sems, swapped):
            cp.wait()

    return _Carry(sums_bf16, _chip_landing(sums_bf16), False, _chip_sems(len(sums_bf16)), start, finish)


def _chip_sems(n):
    return [pltpu.SemaphoreType.DMA((3 * n,)), pltpu.SemaphoreType.DMA((3 * n,))]


def _chip_landing(sums_bf16):
    return [jax.ShapeDtypeStruct((N_CHIPS - 1,) + s.shape[1:], BF16) for s in sums_bf16]


def _chip_sum(own, landed, name):
    rows, cols = own.shape
    core = jnp.reshape(lax.axis_index("c"), (1,)).astype(jnp.int32)

    def body(core_ref, o_ref, l_ref, out_ref):
        out_ref[...] = ((o_ref[...] + l_ref[0].astype(F32)) + l_ref[1].astype(F32)) + l_ref[2].astype(F32)

    return pl.pallas_call(
        body, name=name,
        grid_spec=pltpu.PrefetchScalarGridSpec(
            num_scalar_prefetch=1, grid=(1,),
            in_specs=[pl.BlockSpec((rows, cols), lambda i, core_ref: (0, 0)),
                      pl.BlockSpec((N_CHIPS - 1, rows, cols), lambda i, core_ref: (0, 0, 0))],
            out_specs=pl.BlockSpec((rows, cols), lambda i, core_ref: (core_ref[0], 0))),
        out_shape=jax.ShapeDtypeStruct((2 * rows, cols), F32),
        compiler_params=pltpu.CompilerParams(vmem_limit_bytes=_vmem_limit(3 * rows * cols * 4)),
    )(core, own, landed)


def _final_exchange(fulls, v):
    n = len(fulls)
    rows, cols = v.shape
    n_dev = 8

    def body(*refs):
        v_ref, out_ref = refs[0], refs[1 + 2 * n]
        outs = refs[1 + n:1 + 2 * n]
        buf, v_send, v_recv, h_send, h_recv = refs[2 + 2 * n:]
        x, y, c = _mesh_pos()
        me = 4 * x + 2 * y + c
        buf[me] = v_ref[...]
        peers = [(1 - x if r & 4 else x, 1 - y if r & 2 else y, 1 - c if r & 1 else c) for r in range(1, n_dev)]
        copies = []
        for r, peer in enumerate(peers):
            copies.append(pltpu.make_async_remote_copy(
                src_ref=v_ref, dst_ref=buf.at[me], send_sem=v_send.at[r], recv_sem=v_recv.at[r],
                device_id=peer, device_id_type=MESH))
        for k in range(n):
            half = fulls[k].shape[0] // 2
            mine = outs[k].at[pl.ds(c * half, half), :]
            copies.append(pltpu.make_async_remote_copy(
                src_ref=mine, dst_ref=mine, send_sem=h_send.at[k], recv_sem=h_recv.at[k],
                device_id=(x, y, 1 - c), device_id_type=MESH))
        for cp in copies:
            cp.start()
        for r, (px, py, pc) in enumerate(peers):
            pltpu.make_async_remote_copy(
                src_ref=v_ref, dst_ref=buf.at[4 * px + 2 * py + pc], send_sem=v_send.at[r], recv_sem=v_recv.at[r],
                device_id=(px, py, pc), device_id_type=MESH).wait_recv()
        for k in range(n):
            half = fulls[k].shape[0] // 2
            theirs = outs[k].at[pl.ds((1 - c) * half, half), :]
            pltpu.make_async_remote_copy(
                src_ref=theirs, dst_ref=theirs, send_sem=h_send.at[k], recv_sem=h_recv.at[k],
                device_id=(x, y, 1 - c), device_id_type=MESH).wait_recv()
        for cp in copies:
            cp.wait_send()
        acc = buf[0]
        for d in range(1, n_dev):
            acc = acc + buf[d]
        out_ref[...] = acc
        out_ref[3:4, :] = jnp.broadcast_to(jnp.sum(acc[3:4, :], axis=1, keepdims=True), (1, cols))

    vm = pl.BlockSpec(memory_space=pltpu.VMEM)
    res = pl.pallas_call(
        body, name="final_exchange",
        in_specs=[vm] + _hbm_specs(n), out_specs=_hbm_specs(n) + [vm],
        out_shape=[jax.ShapeDtypeStruct(f.shape, F32) for f in fulls] + [jax.ShapeDtypeStruct((rows, cols), F32)],
        input_output_aliases={1 + k: k for k in range(n)},
        scratch_shapes=[pltpu.VMEM((n_dev, rows, cols), F32),
                        pltpu.SemaphoreType.DMA((n_dev - 1,)), pltpu.SemaphoreType.DMA((n_dev - 1,)),
                        pltpu.SemaphoreType.DMA((n,)), pltpu.SemaphoreType.DMA((n,))],
    )(v, *fulls)
    return res[:n], res[n]


def _adamw_math(w, g, m, v):
    m = ADAM_B1 * m + (1.0 - ADAM_B1) * g
    v = ADAM_B2 * v + (1.0 - ADAM_B2) * (g * g)
    m_hat = m / (1.0 - ADAM_B1 ** ADAM_STEP)
    v_hat = v / (1.0 - ADAM_B2 ** ADAM_STEP)
    delta = -ADAM_LR * (m_hat / (jnp.sqrt(v_hat) + ADAM_EPS) + ADAM_WD * w)
    return delta, m, v


def _adamw(w, g, m, v, name):
    rows, cols = w.shape
    tm = rows // 2 if (rows // 2) % 8 == 0 else rows
    return _rowwise(_adamw_math, [w, g, m, v], [], [(cols, F32)] * 3, [], tm=tm, name=name)


def _unshard_cols(gathered):
    n, r, c = gathered.shape
    return jnp.transpose(gathered, (1, 0, 2)).reshape(r, n * c)


def _shard_cols(full):
    r, nc = full.shape
    return jnp.transpose(full.reshape(r, N_CHIPS, nc // N_CHIPS), (1, 0, 2))


LATE = ["w_sb_up", "w_dil_up", "w_out", "w_ffn_in", "w_ffn_out"]


def _late_weights(slabs, d_model, d_ff):
    g = dict(zip(LATE, slabs))
    return (_unshard_cols(g["w_sb_up"]), _unshard_cols(g["w_dil_up"]), g["w_out"].reshape(d_model, d_model),
            _unshard_cols(g["w_ffn_in"]), g["w_ffn_out"].reshape(d_ff, d_model))


ROW_SHARDED = ("w_in", "w_out", "w_ffn_in", "w_ffn_out")


def _chip_major(grads):
    out = []
    for k, g in grads.items():
        if k in ROW_SHARDED:
            out.append(g.reshape(N_CHIPS, g.shape[0] // N_CHIPS, g.shape[1]))
        else:
            out.append(_shard_cols(g))
    return out


def _pair_sums(full, others, names):
    return [_pair_sum(g, o, "grad_pair_sum_" + k, k in SWAPPED) for g, o, k in zip(full, others, names)]


def _chip_sums(pair, landed, names):
    return {k: _chip_sum(p[0], l, "grad_chip_sum_" + k) for p, l, k in zip(pair, landed, names)}


def _fwd_bwd(x, loss_target, g_mix, g_ffn, g_fin, slab_in, late_shards):
    b_sz, s_len, d_model = x.shape
    t = b_sz * s_len
    d_ff = late_shards[-1].shape[0] * N_CHIPS
    x2d = x.reshape(t, d_model)
    tgt2d = loss_target.reshape(t, d_model)

    u, (slab_in, *late_slabs) = _rowwise(
        lambda xv, g: (_rms_stats(xv)[0] * g,), [_in_hbm(x2d)], [g_mix], [(d_model, BF16)], [], tm=512, name="norm_mix",
        carry=_gather_carry([slab_in], ["w_in"]) + _cast_carry(late_shards, LATE), out_type=_hbm_array)
    u, wt_in = _in_hbm(u), _in_hbm(slab_in.reshape(-1, d_model))
    qkv, (slab_ffn_out,) = _mm(u, wt_in, tb=True, b_cols=(0, QKV_WIDTH), tm=2048, tn=768, tk=d_model, name="proj_qkv",
                               carry=_gather_carry(late_slabs[4:], LATE[4:]))
    gates = _mm(u, wt_in, tb=True, b_cols=(QKV_WIDTH, 2 * d_model), out_dtype=BF16, tm=t, tn=256, tk=d_model,
                name="proj_gates")
    qkv3 = qkv.reshape(b_sz, s_len, QKV_WIDTH)
    o_sb, (slab_ffn_in,) = _sb_fwd(qkv3, b_sz, s_len, _gather_carry(late_slabs[3:4], LATE[3:4]))
    o_dl, lse, small_slabs = _dil_fwd(qkv3, b_sz, s_len, _gather_carry(late_slabs[:3], LATE[:3]))
    wf_sb_up, wf_dil_up, wf_out, wf_ffn_in, wf_ffn_out = _late_weights(
        list(small_slabs) + [slab_ffn_in, slab_ffn_out], d_model, d_ff)
    o_sb2, o_dl2 = o_sb.reshape(t, SB_WIDTH), o_dl.reshape(t, DIL_OUT_WIDTH)
    y_sb = _mm(o_sb2, wf_sb_up, out_dtype=BF16, tm=1024, tn=1024, tk=SB_WIDTH, name="sb_up", out_type=_hbm_array)
    y_dl = _mm(o_dl2, wf_dil_up, out_dtype=BF16, tm=1024, tn=1024, tk=DIL_OUT_WIDTH, name="dil_up", out_type=_hbm_array)

    def merge_fn(gt, ys, yd):
        return (_sigmoid(gt[:, :d_model]) * ys + _sigmoid(gt[:, d_model:]) * yd,)

    (merged,) = _rowwise(merge_fn, [gates, y_sb, y_dl], [], [(d_model, BF16)], [], tm=512, name="merge")
    x1 = _mm(merged, wf_out, add=x2d, tm=512, tn=1024, tk=d_model, name="mix_out")
    (u2,) = _rowwise(lambda xv, g: (_rms_stats(xv)[0] * g,), [_in_hbm(x1)], [g_ffn], [(d_model, BF16)], [], tm=512, name="norm_ffn",
                     out_type=_hbm_array)
    u2 = _in_hbm(u2)
    half_ff = d_ff // 2

    def act_fn(hv):
        gate = hv[:, :half_ff]
        return hv, gate * _sigmoid(gate) * hv[:, half_ff:]

    h, act = _mm(u2, wf_ffn_in, tm=512, tn=d_ff, tk=d_model, name="ffn_in",
                 epilogue=(act_fn, [], [], [(d_ff, BF16), (half_ff, BF16)], []))
    def head_fn(xv, tg, g):
        xhat, r = _rms_stats(xv)
        err = xhat * g - tg
        dy = err * (1.0 / d_model)
        dx, dg_rows = _rms_bwd(dy, xhat, r, g)
        loss_lanes = (0.5 / d_model) * jnp.sum(err * err, axis=0, keepdims=True)
        return dx, dx, jnp.sum(dg_rows, axis=0, keepdims=True), loss_lanes

    dx2, dx2_b, dg_fin, loss_lanes = _mm(
        act, wf_ffn_out, add=x1, tm=512, tn=1024, tk=d_ff, name="ffn_out",
        epilogue=(head_fn, [tgt2d], [g_fin], [(d_model, F32), (d_model, BF16)], [(1, d_model), (1, d_model)]))

    def dact_fn(da, hv):
        gate, up = hv[:, :half_ff], hv[:, half_ff:]
        sg = _sigmoid(gate)
        dgate = da * up * (sg * (1.0 + gate * (1.0 - sg)))
        return (jnp.concatenate([dgate, da * (gate * sg)], axis=1),)

    dx2_b = _in_hbm(dx2_b)
    (dh,) = _mm(dx2_b, wf_ffn_out, tb=True, tm=512, tn=half_ff, tk=d_model, name="ffn_out_dx",
                epilogue=(dact_fn, [h], [], [(d_ff, BF16)], []))
    gw_ffn_out = _mm(act, dx2_b, ta=True, tm=256, tn=d_model, tk=t, name="ffn_out_dw")
    def norm_bwd_fn(du_, dres, xv, g):
        xhat, r = _rms_stats(xv)
        dx, dg_rows = _rms_bwd(du_, xhat, r, g)
        return dres + dx, jnp.sum(dg_rows, axis=0, keepdims=True)

    def norm_bwd_twice(*args):
        dx, dg = norm_bwd_fn(*args)
        return dx, dx, dg

    dx1, dx1_b, dg_ffn = _mm(dh, wf_ffn_in, tb=True, tm=512, tn=1024, tk=2 * d_ff, name="ffn_in_dx",
                             epilogue=(norm_bwd_twice, [dx2, x1], [g_ffn], [(d_model, F32), (d_model, BF16)], [(1, d_model)]))
    gwt_ffn_in = _mm(dh, u2, ta=True, tm=512, tn=d_model, tk=t, name="ffn_in_dw")

    dx1_b = _in_hbm(dx1_b)
    dmerged = _mm(dx1_b, wf_out, tb=True, out_dtype=BF16, tm=512, tn=1024, tk=d_model, name="mix_out_dx",
                  out_type=_hbm_array)
    gw_out = _mm(merged, dx1_b, ta=True, tm=256, tn=d_model, tk=t, name="mix_out_dw")

    def merge_bwd_fn(gt, ys, yd, dm):
        s_sb, s_dl = _sigmoid(gt[:, :d_model]), _sigmoid(gt[:, d_model:])
        dgates = jnp.concatenate([dm * ys * s_sb * (1.0 - s_sb), dm * yd * s_dl * (1.0 - s_dl)], axis=1)
        return dgates, dm * s_sb, dm * s_dl

    full_big = _chip_major({"w_out": gw_out, "w_ffn_in": gwt_ffn_in, "w_ffn_out": gw_ffn_out})
    dgates, dy_sb, dy_dl, others_big = _rowwise(
        merge_bwd_fn, [gates, y_sb, y_dl, dmerged], [], [(2 * d_model, BF16), (d_model, BF16), (d_model, BF16)], [],
        tm=256, name="merge_bwd", carry=_pair_carry(full_big))
    pair_big = _pair_sums(full_big, others_big, LATE[2:])
    do_sb = _mm(dy_sb, wf_sb_up, tb=True, out_dtype=BF16, tm=1024, tn=SB_WIDTH, tk=d_model, name="sb_up_dx")
    gw_sb_up = _mm(o_sb2, dy_sb, ta=True, tm=SB_WIDTH, tn=1024, tk=512, name="sb_up_dw")
    do_dl = _mm(dy_dl, wf_dil_up, tb=True, tm=1024, tn=DIL_OUT_WIDTH, tk=d_model, name="dil_up_dx")
    gw_dil_up = _mm(o_dl2, dy_dl, ta=True, tm=DIL_OUT_WIDTH, tn=1024, tk=512, name="dil_up_dw")
    full_small = _chip_major({"w_sb_up": gw_sb_up, "w_dil_up": gw_dil_up})
    (dq_sb, dk_sb, dv_sb), brought = _sb_bwd(
        qkv3, o_sb, do_sb.reshape(b_sz, s_len, SB_WIDTH), b_sz, s_len,
        _chip_carry([p[1] for p in pair_big[:2]], LATE[2:4]) + _pair_carry(full_small))
    pair_small = _pair_sums(full_small, brought[2:], LATE[:2])
    (dq_dl, dk_dl, dv_dl), landed_b = _dil_bwd(
        qkv3, o_dl, lse, do_dl.reshape(b_sz, s_len, DIL_OUT_WIDTH), b_sz, s_len,
        _chip_carry([pair_big[2][1], pair_small[0][1], pair_small[1][1]], [LATE[4], LATE[0], LATE[1]]))
    pair = pair_small + pair_big
    landed = [landed_b[1], landed_b[2], brought[0], brought[1], landed_b[0]]
    dproj = [a.reshape(t, -1) for a in (dq_sb, dk_sb, dv_sb, dq_dl, dk_dl, dv_dl)] + [dgates]
    gwt_in, gwt_in_b = _mm(dproj, u, ta=True, tm=256, tn=d_model, tk=t, name="proj_dw",
                           epilogue=(lambda tile: (tile, tile), [], [], [(d_model, F32), (d_model, BF16)], []))
    full_in = _chip_major({"w_in": gwt_in})
    pair_in = _pair_sums(full_in, _pair_exchange(_chip_major({"w_in": gwt_in_b}), "w_in"), ["w_in"])
    (dx, dg_mix), landed_in = _mm(
        dproj, wt_in, tm=512, tn=1024, tk=wt_in.shape[0], name="proj_dx",
        carry=_chip_carry([p[1] for p in pair_in], ["w_in"]),
        epilogue=(norm_bwd_fn, [dx1, x2d], [g_mix], [(d_model, F32)], [(1, d_model)]))

    grads = _chip_sums(pair, landed, LATE)
    grads.update(_chip_sums(pair_in, landed_in, ["w_in"]))
    return dx, grads, dg_mix, dg_ffn, dg_fin, loss_lanes


def kernel(x, norm_mix_g, w_in, w_sb_up, w_dil_up, w_out, norm_ffn_g, w_ffn_in, w_ffn_out, norm_final_g, loss_target, m_norm_mix_g, m_w_in, m_w_sb_up, m_w_dil_up, m_w_out, m_norm_ffn_g, m_w_ffn_in, m_w_ffn_out, m_norm_final_g, v_norm_mix_g, v_w_in, v_w_sb_up, v_w_dil_up, v_w_out, v_norm_ffn_g, v_w_ffn_in, v_w_ffn_out, v_norm_final_g):
    b_sz, s_len, d_model = x.shape
    d_ff = w_ffn_out.shape[1] * N_CHIPS
    g_mix, g_ffn, g_fin = norm_mix_g, norm_ffn_g, norm_final_g.reshape(1, d_model)

    names = ["w_in", "w_sb_up", "w_dil_up", "w_out", "w_ffn_in", "w_ffn_out"]
    shards = {"w_in": jnp.swapaxes(w_in[0], 0, 1), "w_sb_up": w_sb_up[0], "w_dil_up": w_dil_up[0], "w_out": w_out[0],
              "w_ffn_in": w_ffn_in[0], "w_ffn_out": w_ffn_out[0]}
    slab_in = _cast_to_slab(shards["w_in"], "cast_w_in")

    dx, grads, dg_mix, dg_ffn, dg_fin, loss_lanes = _fwd_bwd(
        x, loss_target, g_mix, g_ffn, g_fin, slab_in, [shards[k] for k in LATE])

    small = jnp.concatenate([dg_mix, dg_ffn, dg_fin, loss_lanes, jnp.zeros((4, d_model), F32)], axis=0)
    full_grads, small = _final_exchange([grads[k] for k in names], small)
    grads = dict(zip(names, full_grads))
    grads["w_ffn_in"] = jnp.swapaxes(grads["w_ffn_in"], 0, 1)
    loss = small[3, 0]
    gains = jnp.concatenate([g_mix, g_ffn, g_fin, jnp.zeros((5, d_model), F32)], axis=0)
    gains_m = jnp.concatenate([m_norm_mix_g, m_norm_ffn_g, m_norm_final_g.reshape(1, d_model), jnp.zeros((5, d_model), F32)], axis=0)
    gains_v = jnp.concatenate([v_norm_mix_g, v_norm_ffn_g, v_norm_final_g.reshape(1, d_model), jnp.ones((5, d_model), F32)], axis=0)
    gd, gm, gv = _rowwise(_adamw_math, [gains, small, gains_m, gains_v], [], [(d_model, F32)] * 3, [], tm=8, name="adamw_gains")

    moments = {"w_in": (jnp.swapaxes(m_w_in[0], 0, 1), jnp.swapaxes(v_w_in[0], 0, 1)),
               "w_sb_up": (m_w_sb_up[0], v_w_sb_up[0]), "w_dil_up": (m_w_dil_up[0], v_w_dil_up[0]),
               "w_out": (m_w_out[0], v_w_out[0]), "w_ffn_in": (m_w_ffn_in[0], v_w_ffn_in[0]),
               "w_ffn_out": (m_w_ffn_out[0], v_w_ffn_out[0])}
    upd = {k: _adamw(shards[k], grads[k], moments[k][0], moments[k][1], "adamw_" + k) for k in names}

    def as_output(k, a):
        return (jnp.swapaxes(a, 0, 1) if k == "w_in" else a)[None]

    def w_out_of(i):
        return [as_output(k, upd[k][i]) for k in names]

    def ordered(mix, ws, ffn_g, fin):
        return [mix, ws[0], ws[1], ws[2], ws[3], ffn_g, ws[4], ws[5], fin]

    grad_ws = [as_output(k, grads[k]) for k in names]
    outs = [loss, dx.reshape(b_sz, s_len, d_model)]
    outs += ordered(small[0:1], grad_ws, small[1:2], small[2])
    outs += ordered(gd[0:1], w_out_of(0), gd[1:2], gd[2])
    outs += ordered(gm[0:1], w_out_of(1), gm[1:2], gm[2])
    outs += ordered(gv[0:1], w_out_of(2), gv[1:2], gv[2])
    return tuple(outs)
```

```python
import functools
import math

import jax
import jax.numpy as jnp
from jax import lax
from jax.experimental import pallas as pl
from jax.experimental.pallas import tpu as pltpu

F32 = jnp.float32
BF16 = jnp.bfloat16
MESH = pl.DeviceIdType.MESH

HEAD_DIM = 64
SB_HEADS = 8
DIL_PAIRS = ((128, 1), (512, 4), (2048, 16))
DIL_HEADS_PER_GROUP = 4
DIL_HEADS = DIL_HEADS_PER_GROUP * len(DIL_PAIRS)
SB_WIDTH = SB_HEADS * HEAD_DIM
DIL_WIDTH = DIL_HEADS * HEAD_DIM
DIL_OUT_WIDTH = DIL_HEADS_PER_GROUP * HEAD_DIM
QKV_WIDTH = 3 * SB_WIDTH + 3 * DIL_WIDTH
RMS_EPS = 1e-6
ALIBI_MAX_BIAS = 8.0
ADAM_LR = 0.001
ADAM_B1 = 0.9
ADAM_B2 = 0.999
ADAM_EPS = 1e-08
ADAM_WD = 0.01
ADAM_STEP = 10

LANES = 128
BLK = 128
NEG = -1e30
EXP_UNDERFLOW = -104.0
SB_FWD_CHAINS = 4
SB_BWD_CHAINS = 4
DIL_CHAINS = 4
N_CHIPS = 4
VMEM_CAP = 56 * 1024 * 1024


def _vmem_limit(tile_bytes):
    return int(min(VMEM_CAP, max(32 * 1024 * 1024, 3 * tile_bytes + 8 * 1024 * 1024)))


def _hbm_array(shape, dtype):
    return pltpu.HBM(shape, dtype)


def _nbytes(shape, dtype):
    return math.prod(shape) * jnp.dtype(dtype).itemsize


def _in_hbm(x):
    return pltpu.with_memory_space_constraint(x, pltpu.HBM)


def _dot(a, b):
    return jnp.dot(a, b, preferred_element_type=F32)


def _dot_nt(a, b):
    return lax.dot_general(a, b, (((1,), (1,)), ((), ())), preferred_element_type=F32)


def _dot_tn(a, b):
    return lax.dot_general(a, b, (((0,), (0,)), ((), ())), preferred_element_type=F32)


def _split2(x):
    hi = x.astype(BF16)
    lo = (x - hi.astype(F32)).astype(BF16)
    return hi, lo


def _sigmoid(x):
    return pl.reciprocal(1.0 + jnp.exp(-x), approx=True)


class _Carry:
    def __init__(self, arrays=(), out_shapes=(), aliased=False, sems=(), start=None, finish=None):
        self.arrays, self.out_shapes = list(arrays), list(out_shapes)
        self.n_aliased = len(self.arrays) if aliased is True else int(aliased)
        self.sems, self.start, self.finish = list(sems), start, finish

    def __bool__(self):
        return bool(self.arrays)

    def __add__(self, other):
        assert not other.n_aliased and self.n_aliased in (0, len(self.out_shapes))
        n_a, n_o, n_s = len(self.arrays), len(self.out_shapes), len(self.sems)
        return _Carry(
            self.arrays + other.arrays, self.out_shapes + other.out_shapes, self.n_aliased, self.sems + other.sems,
            lambda i, o, s: (self.start(i[:n_a], o[:n_o], s[:n_s]), other.start(i[n_a:], o[n_o:], s[n_s:])),
            lambda i, o, s: (self.finish(i[:n_a], o[:n_o], s[:n_s]), other.finish(i[n_a:], o[n_o:], s[n_s:])))

    def call_args(self, n_in, n_out):
        aliases = {n_in + k: n_out + k for k in range(self.n_aliased)}
        return _hbm_specs(len(self.arrays)), _hbm_specs(len(self.out_shapes)), self.out_shapes, aliases, self.sems

    def run(self, refs, n_in, n_out, step, n_steps, compute):
        if not self:
            compute()
            return
        n_c, n_o, n_s = len(self.arrays), len(self.out_shapes), len(self.sems)
        ins = refs[n_in:n_in + n_c]
        outs = refs[n_in + n_c + n_out:n_in + n_c + n_out + n_o]
        sems = refs[len(refs) - n_s:]

        @pl.when(step == 0)
        def _():
            self.start(ins, outs, sems)

        compute()

        @pl.when(step == n_steps - 1)
        def _():
            self.finish(ins, outs, sems)


def _mm(a, b, *, ta=False, tb=False, add=None, out_dtype=F32, tm, tn, tk, name, carry=None, epilogue=None,
        b_cols=None, out_type=jax.ShapeDtypeStruct):
    carry = carry or _Carry()
    n_car = len(carry.arrays)
    pieces = list(a) if isinstance(a, (list, tuple)) else [a]
    n_a = len(pieces)
    widths = [p.shape[1] for p in pieces]
    starts = [sum(widths[:p]) for p in range(n_a)]
    if ta:
        kdim, m = pieces[0].shape[0], sum(widths)
    else:
        m, kdim = pieces[0].shape[0], sum(widths)
    if tb:
        n, k2 = b.shape
    else:
        k2, n = b.shape
    col0 = 0
    if b_cols is not None:
        assert b_cols[0] % tn == 0, name
        col0, n = b_cols[0] // tn, b_cols[1]
    assert kdim == k2 and m % tm == 0 and n % tn == 0 and kdim % tk == 0, (name, a.shape, b.shape)
    nk = kdim // tk
    assert n_a == 1 or (nk == 1 and not tb and (not ta or all(w % tm == 0 for w in widths))), name
    grid = (m // tm, n // tn, nk)
    a_mode = dict(pipeline_mode=pl.Buffered(1)) if grid[0] == 1 and nk == 1 else {}
    b_mode = dict(pipeline_mode=pl.Buffered(1)) if grid[1] == 1 and nk == 1 else {}
    if n_a == 1:
        a_specs = [pl.BlockSpec((tk, tm), lambda i, j, k: (k, i), **a_mode) if ta
                   else pl.BlockSpec((tm, tk), lambda i, j, k: (i, k), **a_mode)]
    elif ta:
        a_specs = [pl.BlockSpec((tk, tm), lambda i, j, k, s=s // tm, w=w // tm: (0, jnp.clip(i - s, 0, w - 1)))
                   for s, w in zip(starts, widths)]
    else:
        a_specs = [pl.BlockSpec((tm, w), lambda i, j, k: (i, 0)) for w in widths]
    b_spec = (pl.BlockSpec((tn, tk), lambda i, j, k: (j + col0, k), **b_mode) if tb
              else pl.BlockSpec((tk, tn), lambda i, j, k: (k, j + col0), **b_mode))
    o_spec = pl.BlockSpec((tm, tn), lambda i, j, k: (i, j))
    dims = ((((0,) if ta else (1,)), ((1,) if tb else (0,))), ((), ()))
    has_add = add is not None
    if epilogue is None:
        ep_fn, ep_rows, ep_params, ep_outs, ep_accs = None, [], [], [], []
        out_sds, out_specs = [out_type((m, n), out_dtype)], [o_spec]
    else:
        ep_fn, ep_rows, ep_params, ep_outs, ep_accs = epilogue
        assert grid[1] == 1 or not ep_accs, name
        out_sds = [out_type((m, w * grid[1]), d) for w, d in ep_outs]
        out_sds += [jax.ShapeDtypeStruct(sh, F32) for sh in ep_accs]
        out_specs = [pl.BlockSpec((tm, w), lambda i, j, k: (i, j)) for w, _ in ep_outs]
        out_specs += [pl.BlockSpec(sh, lambda i, j, k: (0, 0)) for sh in ep_accs]
    n_main = len(out_sds)
    use_scratch = nk > 1 and (ep_fn is not None or jnp.dtype(out_dtype) != jnp.dtype(F32))
    n_in = n_a + 1 + has_add + len(ep_rows) + len(ep_params)

    def finish(total, refs, pid):
        outs = refs[n_in + n_car:n_in + n_car + n_main]
        if ep_fn is None:
            outs[0][...] = total.astype(out_dtype)
            return
        first = n_a + 1 + has_add
        rows = [r[...].astype(F32) for r in refs[first:first + len(ep_rows)]]
        params = [p[...] for p in refs[first + len(ep_rows):n_in]]
        res = ep_fn(total, *rows, *params)
        for o_ref, v in zip(outs[:len(ep_outs)], res):
            o_ref[...] = v.astype(o_ref.dtype)
        acc_refs = outs[len(ep_outs):]
        if acc_refs:
            @pl.when(pid[0] == 0)
            def _():
                for r in acc_refs:
                    r[...] = jnp.zeros(r.shape, F32)

            for r, v in zip(acc_refs, res[len(ep_outs):]):
                r[...] += v

    def compute(refs, pid):
        a_ref, b_ref = refs[0], refs[n_a]
        add_ref = refs[n_a + 1] if has_add else None

        def dot(x, y):
            return lax.dot_general(x.astype(BF16), y.astype(BF16), dims, preferred_element_type=F32)

        if n_a > 1 and ta:
            for p_ref, s, w in zip(refs[:n_a], starts, widths):
                @pl.when((pid[0] >= s // tm) & (pid[0] < (s + w) // tm))
                def _(p_ref=p_ref):
                    prod = dot(p_ref[...], b_ref[...])
                    finish(prod + add_ref[...] if has_add else prod, refs, pid)
            return
        if n_a > 1:
            prod = dot(a_ref[...], b_ref[:widths[0], :])
            for p_ref, s, w in zip(refs[1:n_a], starts[1:], widths[1:]):
                prod += dot(p_ref[...], b_ref[s:s + w, :])
        else:
            prod = dot(a_ref[...], b_ref[...])
        if nk == 1:
            finish(prod + add_ref[...] if has_add else prod, refs, pid)
            return
        acc_ref = refs[n_in + n_car + n_main + len(carry.out_shapes)] if use_scratch else refs[n_in + n_car]
        k = pid[2]

        @pl.when(k == 0)
        def _():
            acc_ref[...] = prod + add_ref[...] if has_add else prod

        @pl.when(k > 0)
        def _():
            acc_ref[...] += prod

        if use_scratch:
            @pl.when(k == nk - 1)
            def _():
                finish(acc_ref[...], refs, pid)

    def body(*refs):
        pid = (pl.program_id(0), pl.program_id(1), pl.program_id(2))
        step = (pid[0] * grid[1] + pid[1]) * nk + pid[2]
        carry.run(refs, n_in, n_main, step, grid[0] * grid[1] * nk, lambda: compute(refs, pid))

    tile_bytes = ((n_a if ta else 1) * _nbytes((tm, tk), pieces[0].dtype)
                  + _nbytes((tk, tn), b.dtype) + 2 * _nbytes((tm, tn), F32)
                  + (_nbytes((tm, tn), F32) if has_add else 0)
                  + sum(_nbytes((tm, r.shape[1]), r.dtype) for r in ep_rows) + sum(_nbytes((tm, w), d) for w, d in ep_outs))
    in_specs = a_specs + [b_spec] + ([o_spec] if has_add else [])
    in_specs += [pl.BlockSpec((tm, r.shape[1] // grid[1]), lambda i, j, k: (i, j)) for r in ep_rows]
    in_specs += [pl.BlockSpec(p.shape, lambda i, j, k: (0, 0)) for p in ep_params]
    args = tuple(pieces) + (b,) + ((add,) if has_add else ()) + tuple(ep_rows) + tuple(ep_params)
    scratch = [pltpu.VMEM((tm, tn), F32)] if use_scratch else []
    serial = bool(carry) or bool(ep_accs)
    c_in, c_out, c_shapes, c_alias, c_sems = carry.call_args(n_in, n_main)
    res = pl.pallas_call(
        body, name=name, grid=grid,
        in_specs=in_specs + c_in, out_specs=out_specs + c_out, out_shape=out_sds + c_shapes,
        input_output_aliases=c_alias, scratch_shapes=scratch + c_sems,
        compiler_params=pltpu.CompilerParams(
            dimension_semantics=("arbitrary",) * 3 if serial else ("parallel", "parallel", "arbitrary"),
            vmem_limit_bytes=_vmem_limit(tile_bytes)),
    )(*args, *carry.arrays)
    main = res[0] if ep_fn is None else list(res[:n_main])
    return (main, res[n_main:]) if carry else main


def _rowwise(fn, rows, params, outs, accs, *, tm, name, carry=None, out_type=jax.ShapeDtypeStruct):
    carry = carry or _Carry()
    t = rows[0].shape[0]
    assert t % tm == 0, (name, t, tm)
    n_r, n_p, n_o, n_c = len(rows), len(params), len(outs), len(carry.arrays)

    def compute(refs, first):
        vals = [r[...].astype(F32) for r in refs[:n_r]] + [p[...] for p in refs[n_r:n_r + n_p]]
        res = fn(*vals)
        o_refs = refs[n_r + n_p + n_c:n_r + n_p + n_c + n_o]
        a_refs = refs[n_r + n_p + n_c + n_o:n_r + n_p + n_c + n_o + len(accs)]
        for o_ref, v in zip(o_refs, res[:n_o]):
            o_ref[...] = v.astype(o_ref.dtype)
        if accs:
            @pl.when(first)
            def _():
                for a_ref in a_refs:
                    a_ref[...] = jnp.zeros(a_ref.shape, F32)

            for a_ref, v in zip(a_refs, res[n_o:]):
                a_ref[...] += v

    def body(*refs):
        step = pl.program_id(0)
        carry.run(refs, n_r + n_p, n_o + len(accs), step, t // tm, lambda: compute(refs, step == 0))

    in_specs = [pl.BlockSpec((tm, r.shape[1]), lambda i: (i, 0)) for r in rows]
    in_specs += [pl.BlockSpec(p.shape, lambda i: (0, 0)) for p in params]
    out_specs = [pl.BlockSpec((tm, w), lambda i: (i, 0)) for w, _ in outs]
    out_specs += [pl.BlockSpec(s, lambda i: (0, 0)) for s in accs]
    out_shape = [out_type((t, w), d) for w, d in outs]
    out_shape += [jax.ShapeDtypeStruct(s, F32) for s in accs]
    tile_bytes = sum(_nbytes((tm, r.shape[1]), r.dtype) for r in rows) + sum(_nbytes((tm, w), F32) for w, _ in outs)
    c_in, c_out, c_shapes, c_alias, c_sems = carry.call_args(n_r + n_p, n_o + len(accs))
    res = pl.pallas_call(
        body, name=name, grid=(t // tm,), in_specs=in_specs + c_in, out_specs=out_specs + c_out,
        out_shape=out_shape + c_shapes, input_output_aliases=c_alias, scratch_shapes=c_sems,
        compiler_params=pltpu.CompilerParams(
            dimension_semantics=("arbitrary",) if accs or carry else ("parallel",),
            vmem_limit_bytes=_vmem_limit(2 * tile_bytes)),
    )(*rows, *params, *carry.arrays)
    own = n_o + len(accs)
    return (list(res[:own]) + [res[own:]]) if carry else res


def _rms_stats(x):
    r = lax.rsqrt(jnp.mean(x * x, axis=-1, keepdims=True) + RMS_EPS)
    return x * r, r


def _rms_bwd(dy, xhat, r, g):
    dxhat = dy * g
    dx = r * (dxhat - xhat * jnp.mean(dxhat * xhat, axis=-1, keepdims=True))
    return dx, dy * xhat


def _sb_consts():
    lane = lax.broadcasted_iota(jnp.int32, (BLK, LANES), 1)
    head0 = lane < HEAD_DIM
    row = lax.broadcasted_iota(jnp.int32, (2 * BLK, BLK), 0) % BLK
    col = lax.broadcasted_iota(jnp.int32, (2 * BLK, BLK), 1)
    causal = col < row
    jj = lax.broadcasted_iota(jnp.int32, (BLK, BLK), 0)
    ss = lax.broadcasted_iota(jnp.int32, (BLK, BLK), 1)
    suffix = jnp.where(jj > ss, 1.0, 0.0).astype(BF16)
    return head0, causal, suffix


def _stack_heads(x, head0):
    zero = jnp.zeros_like(x)
    return jnp.concatenate([jnp.where(head0, x, zero), jnp.where(head0, zero, x)], axis=0)


def _sb_logits(z, causal, masked):
    sp = jnp.log(1.0 + jnp.exp(-jnp.abs(z)))
    log_keep = -(jnp.maximum(z, 0.0) + sp)
    log_beta = jnp.minimum(z, 0.0) - sp
    if masked:
        log_keep = jnp.where(causal, log_keep, 0.0)
    return log_keep, log_beta


def _suffix_sums(x, suffix):
    hi, lo = _split2(x)
    after = _dot(hi, suffix) + _dot(lo, suffix)
    total = jnp.broadcast_to(after[:, 0:1] + x[:, 0:1], x.shape)
    return after, total


def _sb_walk_back(i, state, per_chain, tile):
    def alive(st):
        worst = functools.reduce(jnp.maximum, [st[p][:, 0:1] for p in range(0, len(st), per_chain)])
        return jnp.max(worst) > EXP_UNDERFLOW

    def cond(c):
        return jnp.logical_and(c[0] < i, alive(c[1]))

    def body(c):
        return c[0] + 1, tile(i - 1 - c[0], c[1], False)

    return lax.while_loop(cond, body, (jnp.int32(0), state))[1]


def _lane_blocks(x, n):
    return [x[:, p * LANES:(p + 1) * LANES] for p in range(n)]


def _sb_fwd(qkv, b_sz, s_len, carry):
    nq = s_len // BLK
    n_pairs = SB_WIDTH // LANES
    ch = SB_FWD_CHAINS
    n_steps = n_pairs // ch
    scale = 1.0 / math.sqrt(HEAD_DIM)

    def compute(q_ref, k_ref, v_ref, o_ref):
        head0, causal, suffix = _sb_consts()

        def q_block(i, _):
            qs = pl.multiple_of(i * BLK, BLK)
            q_all = (q_ref[pl.ds(qs, BLK), :] * scale).astype(BF16)
            q01 = [_stack_heads(q, head0) for q in _lane_blocks(q_all, ch)]

            def tile(j, state, masked):
                ks = pl.multiple_of(j * BLK, BLK)
                ks_ = _lane_blocks(k_ref[pl.ds(ks, BLK), :].astype(BF16), ch)
                vs_ = _lane_blocks(v_ref[pl.ds(ks, BLK), :].astype(BF16), ch)
                zs = [_dot_nt(q01[p], ks_[p]) for p in range(ch)]
                logits = [_sb_logits(z, causal, masked) for z in zs]
                sums = [_suffix_sums(lg[0], suffix) for lg in logits]
                out = []
                for p in range(ch):
                    carry, acc = state[2 * p], state[2 * p + 1]
                    after, total = sums[p]
                    a = jnp.exp(logits[p][1] + carry + after)
                    if masked:
                        a = jnp.where(causal, a, 0.0)
                    a_hi, a_lo = _split2(a)
                    a_cat = jnp.concatenate([a_hi[:BLK], a_hi[BLK:], a_lo[:BLK], a_lo[BLK:]], axis=1)
                    v01 = _stack_heads(vs_[p], head0)
                    out += [carry + total, acc + _dot(a_cat, jnp.concatenate([v01, v01], axis=0))]
                return tuple(out)

            state = (jnp.zeros((2 * BLK, BLK), F32), jnp.zeros((BLK, LANES), F32)) * ch
            state = tile(i, state, True)
            state = _sb_walk_back(i, state, 2, tile)
            o_ref[pl.ds(qs, BLK), :] = jnp.concatenate([state[2 * p + 1] for p in range(ch)], axis=1)
            return 0

        lax.fori_loop(0, nq, q_block, 0)

    def body(*refs):
        step = pl.program_id(0) * n_steps + pl.program_id(1)
        o_ref = refs[3 + len(carry.arrays)]
        carry.run(refs, 3, 1, step, b_sz * n_steps, lambda: compute(refs[0], refs[1], refs[2], o_ref))

    blk = lambda off: pl.BlockSpec((None, s_len, ch * LANES), lambda b, p: (b, 0, off + p))
    c_in, c_out, c_shapes, c_alias, c_sems = carry.call_args(3, 1)
    res = pl.pallas_call(
        body, name="sb_fwd", grid=(b_sz, n_steps),
        in_specs=[blk(0), blk(n_steps), blk(2 * n_steps)] + c_in, out_specs=[blk(0)] + c_out,
        out_shape=[jax.ShapeDtypeStruct((b_sz, s_len, SB_WIDTH), F32)] + c_shapes,
        input_output_aliases=c_alias, scratch_shapes=c_sems,
        compiler_params=pltpu.CompilerParams(dimension_semantics=("arbitrary", "arbitrary"),
                                             vmem_limit_bytes=VMEM_CAP),
    )(qkv, qkv, qkv, *carry.arrays)
    return res[0], res[1:]


def _sb_bwd(qkv, o_sb, do_sb, b_sz, s_len, carry):
    nq = s_len // BLK
    n_pairs = SB_WIDTH // LANES
    ch = SB_BWD_CHAINS
    n_steps = n_pairs // ch
    scale = 1.0 / math.sqrt(HEAD_DIM)

    def compute(q_ref, k_ref, v_ref, o_ref, do_ref, dq_ref, dk_ref, dv_ref, dk_acc, dv_acc):
        head0, causal, suffix = _sb_consts()
        lrow = lax.broadcasted_iota(jnp.int32, (LANES, LANES), 0)
        ones_h0 = jnp.where(lrow < HEAD_DIM, 1.0, 0.0).astype(BF16)
        ones_h1 = jnp.where(lrow >= HEAD_DIM, 1.0, 0.0).astype(BF16)
        dk_acc[...] = jnp.zeros(dk_acc.shape, F32)
        dv_acc[...] = jnp.zeros(dv_acc.shape, F32)

        def q_block(i, _):
            qs = pl.multiple_of(i * BLK, BLK)
            q_all = (q_ref[pl.ds(qs, BLK), :] * scale).astype(BF16)
            do_all = do_ref[pl.ds(qs, BLK), :].astype(BF16)
            dd_all = do_all.astype(F32) * o_ref[pl.ds(qs, BLK), :]
            q01 = [_stack_heads(q, head0) for q in _lane_blocks(q_all, ch)]
            do01 = [_stack_heads(d, head0) for d in _lane_blocks(do_all, ch)]
            tot = []
            for dd in _lane_blocks(dd_all, ch):
                dd_hi, dd_lo = _split2(dd)
                tot.append(jnp.concatenate([_dot(dd_hi, ones_h0) + _dot(dd_lo, ones_h0),
                                            _dot(dd_hi, ones_h1) + _dot(dd_lo, ones_h1)], axis=0))

            def tile(j, state, masked):
                ks = pl.multiple_of(j * BLK, BLK)
                ks_ = _lane_blocks(k_ref[pl.ds(ks, BLK), :].astype(BF16), ch)
                vs_ = _lane_blocks(v_ref[pl.ds(ks, BLK), :].astype(BF16), ch)
                zs = [_dot_nt(q01[p], ks_[p]) for p in range(ch)]
                das = [_dot_nt(do01[p], vs_[p]) for p in range(ch)]
                logits = [_sb_logits(z, causal, masked) for z in zs]
                sums = [_suffix_sums(lg[0], suffix) for lg in logits]
                a_s, e_s = [], []
                for p in range(ch):
                    a = jnp.exp(logits[p][1] + state[3 * p] + sums[p][0])
                    if masked:
                        a = jnp.where(causal, a, 0.0)
                    a_s.append(a)
                    e_s.append(a * das[p])
                e_sums = [_suffix_sums(e, suffix) for e in e_s]
                out, dks, dvs = [], [], []
                for p in range(ch):
                    carry, rcarry, dq = state[3 * p:3 * p + 3]
                    e = e_s[p]
                    before = tot[p] - (rcarry + e_sums[p][0] + e)
                    beta = jnp.exp(logits[p][1])
                    dz = e * (1.0 - beta) - beta * before
                    if masked:
                        dz = jnp.where(causal, dz, 0.0)
                    dz_b = dz.astype(BF16)
                    dks.append(_dot_tn(dz_b, q01[p]))
                    dvs.append(_dot_tn(a_s[p].astype(BF16), do01[p]))
                    out += [carry + sums[p][1], rcarry + e_sums[p][1], dq + _dot(dz_b, ks_[p])]
                dk_acc[pl.ds(ks, BLK), :] += jnp.concatenate(dks, axis=1)
                dv_acc[pl.ds(ks, BLK), :] += jnp.concatenate(dvs, axis=1)
                return tuple(out)

            state = (jnp.zeros((2 * BLK, BLK), F32),) * (3 * ch)
            state = tile(i, state, True)
            state = _sb_walk_back(i, state, 3, tile)
            dq = [jnp.where(head0, state[3 * p + 2][:BLK], state[3 * p + 2][BLK:]) for p in range(ch)]
            dq_ref[pl.ds(qs, BLK), :] = (jnp.concatenate(dq, axis=1) * scale).astype(dq_ref.dtype)
            return 0

        lax.fori_loop(0, nq, q_block, 0)
        dk_ref[...] = dk_acc[...].astype(dk_ref.dtype)
        dv_ref[...] = dv_acc[...].astype(dv_ref.dtype)

    def body(*refs):
        step = pl.program_id(0) * n_steps + pl.program_id(1)
        n_c, n_o = len(carry.arrays), len(carry.out_shapes)
        own = refs[:5] + refs[5 + n_c:8 + n_c] + refs[8 + n_c + n_o:10 + n_c + n_o]
        carry.run(refs, 5, 3, step, b_sz * n_steps, lambda: compute(*own))

    blk = lambda off: pl.BlockSpec((None, s_len, ch * LANES), lambda b, p: (b, 0, off + p))
    once = lambda off: pl.BlockSpec((None, s_len, ch * LANES), lambda b, p: (b, 0, off + p),
                                    pipeline_mode=pl.Buffered(1))
    out_sd = jax.ShapeDtypeStruct((b_sz, s_len, SB_WIDTH), BF16)
    c_in, c_out, c_shapes, c_alias, c_sems = carry.call_args(5, 3)
    res = pl.pallas_call(
        body, name="sb_bwd", grid=(b_sz, n_steps),
        in_specs=[once(0), once(n_steps), once(2 * n_steps), once(0), once(0)] + c_in,
        out_specs=[blk(0), blk(0), blk(0)] + c_out, out_shape=[out_sd, out_sd, out_sd] + c_shapes,
        input_output_aliases=c_alias,
        scratch_shapes=[pltpu.VMEM((s_len, ch * LANES), F32), pltpu.VMEM((s_len, ch * LANES), F32)] + c_sems,
        compiler_params=pltpu.CompilerParams(dimension_semantics=("arbitrary", "arbitrary"),
                                             vmem_limit_bytes=VMEM_CAP),
    )(qkv, qkv, qkv, o_sb, do_sb, *carry.arrays)
    return res[:3], res[3:]


def _dil_consts(group, pair_idx, dilation):
    lane = lax.broadcasted_iota(jnp.int32, (BLK, LANES), 1)
    head0 = lane < HEAD_DIM
    row = lax.broadcasted_iota(jnp.int32, (2 * BLK, BLK), 0)
    qa = row % BLK
    kb = lax.broadcasted_iota(jnp.int32, (2 * BLK, BLK), 1)
    head = (group * DIL_HEADS_PER_GROUP + 2 * pair_idx + row // BLK).astype(F32)
    slope = jnp.exp((-ALIBI_MAX_BIAS * math.log(2.0) / DIL_HEADS) * (head + 1.0))
    valid_cur = kb <= qa
    valid_prev = kb >= qa
    bias_cur = -slope * ((qa - kb) * dilation).astype(F32)
    bias_prev = -slope * ((BLK + qa - kb) * dilation).astype(F32)
    return head0, valid_cur, valid_prev, bias_cur, bias_prev


def _dil_units(s_len, dilation):
    nb = s_len // dilation // BLK
    return [(r, n) for r in range(dilation) for n in range(nb)]


def _dil_rows(n, r, dilation):
    if dilation == 1:
        return pl.ds(n * BLK, BLK)
    return pl.ds(n * BLK * dilation + r, BLK, stride=dilation)


def _dil_scores(q01, k, bias, valid):
    s = _dot_nt(q01, k) * (1.0 / math.sqrt(HEAD_DIM)) + bias
    return jnp.where(valid, s, NEG)


def _dil_fwd(qkv, b_sz, s_len, carry):
    n_pairs = DIL_OUT_WIDTH // LANES
    q_off = 3 * SB_WIDTH // LANES
    per_kind = DIL_WIDTH // LANES

    def compute(pair_idx, qkv_refs, o_ref, lse_ref, m_s, l_s):
        m_s[...] = jnp.full(m_s.shape, NEG, F32)
        l_s[...] = jnp.zeros(l_s.shape, F32)
        o_ref[...] = jnp.zeros(o_ref.shape, F32)
        for g, (_, dilation) in enumerate(DIL_PAIRS):
            q_ref, k_ref, v_ref = qkv_refs[3 * g:3 * g + 3]
            head0, valid_cur, valid_prev, bias_cur, bias_prev = _dil_consts(g, pair_idx, dilation)
            units = _dil_units(s_len, dilation)
            for u0 in range(0, len(units), DIL_CHAINS):
                group = units[u0:u0 + DIL_CHAINS]
                rows_of = [_dil_rows(n, r, dilation) for r, n in group]
                scores, values = [], []
                for (r, n), rows in zip(group, rows_of):
                    q01 = _stack_heads(q_ref[rows, :].astype(BF16), head0)
                    sc = [_dil_scores(q01, k_ref[rows, :].astype(BF16), bias_cur, valid_cur)]
                    vals = [_stack_heads(v_ref[rows, :].astype(BF16), head0)]
                    if n > 0:
                        prev = _dil_rows(n - 1, r, dilation)
                        sc.append(_dil_scores(q01, k_ref[prev, :].astype(BF16), bias_prev, valid_prev))
                        vals.append(_stack_heads(v_ref[prev, :].astype(BF16), head0))
                    scores.append(sc)
                    values.append(vals)
                stats = []
                for sc, rows in zip(scores, rows_of):
                    m_blk = functools.reduce(jnp.maximum, [jnp.max(x, axis=-1, keepdims=True) for x in sc])
                    m_old = jnp.concatenate([m_s.at[0][rows, :], m_s.at[1][rows, :]], axis=0)
                    l_old = jnp.concatenate([l_s.at[0][rows, :], l_s.at[1][rows, :]], axis=0)
                    m_new = jnp.maximum(m_old, m_blk)
                    probs = [jnp.exp(x - m_new) for x in sc]
                    l_blk = functools.reduce(jnp.add, [jnp.sum(p, axis=-1, keepdims=True) for p in probs])
                    alpha = jnp.exp(m_old - m_new)
                    stats.append((m_new, alpha * l_old + l_blk, alpha, probs))
                for (m_new, l_new, alpha, probs), vals, rows in zip(stats, values, rows_of):
                    alpha_tok = jnp.where(head0, alpha[:BLK], alpha[BLK:])
                    p_cat = jnp.concatenate(
                        [h for p in probs for h in (p[:BLK].astype(BF16), p[BLK:].astype(BF16))], axis=1)
                    o_ref[rows, :] = alpha_tok * o_ref[rows, :] + _dot(p_cat, jnp.concatenate(vals, axis=0))
                    m_s.at[0][rows, :] = m_new[:BLK]
                    m_s.at[1][rows, :] = m_new[BLK:]
                    l_s.at[0][rows, :] = l_new[:BLK]
                    l_s.at[1][rows, :] = l_new[BLK:]
        lane = lax.broadcasted_iota(jnp.int32, (BLK, LANES), 1)
        for c in range(s_len // BLK):
            rows = pl.ds(c * BLK, BLK)
            l0, l1 = l_s.at[0][rows, :], l_s.at[1][rows, :]
            o_ref[rows, :] = o_ref[rows, :] / jnp.where(lane < HEAD_DIM, l0, l1)
            lse_ref.at[0][rows, :] = m_s.at[0][rows, :] + jnp.log(l0)
            lse_ref.at[1][rows, :] = m_s.at[1][rows, :] + jnp.log(l1)

    def body(*refs):
        pair_idx = pl.program_id(1)
        step = pl.program_id(0) * n_pairs + pair_idx
        n_c, n_o = len(carry.arrays), len(carry.out_shapes)
        o_ref, lse_ref = refs[9 + n_c:11 + n_c]
        m_s, l_s = refs[11 + n_c + n_o:13 + n_c + n_o]
        carry.run(refs, 9, 2, step, b_sz * n_pairs, lambda: compute(pair_idx, refs[:9], o_ref, lse_ref, m_s, l_s))

    in_specs = []
    for g in range(len(DIL_PAIRS)):
        for kind in range(3):
            off = q_off + kind * per_kind + g * n_pairs
            in_specs.append(pl.BlockSpec((None, s_len, LANES), lambda b, p, off=off: (b, 0, off + p)))
    c_in, c_out, c_shapes, c_alias, c_sems = carry.call_args(9, 2)
    res = pl.pallas_call(
        body, name="dil_fwd", grid=(b_sz, n_pairs),
        in_specs=in_specs + c_in,
        out_specs=[pl.BlockSpec((None, s_len, LANES), lambda b, p: (b, 0, p)),
                   pl.BlockSpec((None, None, 2, s_len, LANES), lambda b, p: (b, p, 0, 0, 0))] + c_out,
        out_shape=[jax.ShapeDtypeStruct((b_sz, s_len, DIL_OUT_WIDTH), F32),
                   jax.ShapeDtypeStruct((b_sz, n_pairs, 2, s_len, LANES), F32)] + c_shapes,
        input_output_aliases=c_alias,
        scratch_shapes=[pltpu.VMEM((2, s_len, LANES), F32), pltpu.VMEM((2, s_len, LANES), F32)] + c_sems,
        compiler_params=pltpu.CompilerParams(dimension_semantics=("arbitrary", "arbitrary"),
                                             vmem_limit_bytes=VMEM_CAP),
    )(*([qkv] * 9), *carry.arrays)
    return res[0], res[1], res[2:]


def _dil_bwd(qkv, o_dl, lse, do_dl, b_sz, s_len, carry):
    n_pairs = DIL_OUT_WIDTH // LANES
    n_groups = len(DIL_PAIRS)
    q_off = 3 * SB_WIDTH // LANES
    per_kind = DIL_WIDTH // LANES

    def compute(pair_idx, group, q_ref, k_ref, v_ref, o_ref, lse_ref, do_ref, dq_ref, dk_ref, dv_ref, d_s, dq_s, dk_s, dv_s):
        lrow = lax.broadcasted_iota(jnp.int32, (LANES, LANES), 0)
        ones_h0 = jnp.where(lrow < HEAD_DIM, 1.0, 0.0).astype(BF16)
        ones_h1 = jnp.where(lrow >= HEAD_DIM, 1.0, 0.0).astype(BF16)
        for c in range(s_len // BLK):
            rows = pl.ds(c * BLK, BLK)
            dd_hi, dd_lo = _split2(do_ref[rows, :] * o_ref[rows, :])
            d_s.at[0][rows, :] = _dot(dd_hi, ones_h0) + _dot(dd_lo, ones_h0)
            d_s.at[1][rows, :] = _dot(dd_hi, ones_h1) + _dot(dd_lo, ones_h1)
        dk_s[...] = jnp.zeros(dk_s.shape, F32)
        dv_s[...] = jnp.zeros(dv_s.shape, F32)

        def one_group(g, dilation):
            head0, valid_cur, valid_prev, bias_cur, bias_prev = _dil_consts(g, pair_idx, dilation)
            units = _dil_units(s_len, dilation)
            scale = 1.0 / math.sqrt(HEAD_DIM)
            for u0 in range(0, len(units), DIL_CHAINS):
                chunk = units[u0:u0 + DIL_CHAINS]
                loaded = []
                for r, n in chunk:
                    rows = _dil_rows(n, r, dilation)
                    q01 = _stack_heads(q_ref[rows, :].astype(BF16), head0)
                    do01 = _stack_heads(do_ref[rows, :].astype(BF16), head0)
                    lse01 = jnp.concatenate([lse_ref.at[0][rows, :], lse_ref.at[1][rows, :]], axis=0)
                    d01 = jnp.concatenate([d_s.at[0][rows, :], d_s.at[1][rows, :]], axis=0)
                    blocks = [(rows, bias_cur, valid_cur)]
                    if n > 0:
                        blocks.append((_dil_rows(n - 1, r, dilation), bias_prev, valid_prev))
                    parts = []
                    for krows, bias, valid in blocks:
                        k = k_ref[krows, :].astype(BF16)
                        v = v_ref[krows, :].astype(BF16)
                        parts.append((krows, k, _dil_scores(q01, k, bias, valid), _dot_nt(do01, v)))
                    loaded.append((rows, q01, do01, lse01, d01, parts))
                grads = []
                for rows, q01, do01, lse01, d01, parts in loaded:
                    for krows, k, sc, dp in parts:
                        p = jnp.exp(sc - lse01)
                        grads.append((p.astype(BF16), (p * (dp - d01) * scale).astype(BF16)))
                it = iter(grads)
                updates = []
                for rows, q01, do01, lse01, d01, parts in loaded:
                    dq = jnp.zeros((2 * BLK, LANES), F32)
                    for krows, k, sc, dp in parts:
                        p_b, ds = next(it)
                        dq = dq + _dot(ds, k)
                        updates.append((krows, _dot_tn(ds, q01), _dot_tn(p_b, do01)))
                    dq_s[rows, :] = jnp.where(head0, dq[:BLK], dq[BLK:])
                for krows, dk, dv in updates:
                    dk_s[krows, :] = dk_s[krows, :] + dk
                    dv_s[krows, :] = dv_s[krows, :] + dv

        for g, (_, dilation) in enumerate(DIL_PAIRS):
            pl.when(group == g)(functools.partial(one_group, g, dilation))
        dq_ref[...] = dq_s[...].astype(dq_ref.dtype)
        dk_ref[...] = dk_s[...].astype(dk_ref.dtype)
        dv_ref[...] = dv_s[...].astype(dv_ref.dtype)

    def body(*refs):
        pair_idx, group = pl.program_id(1), pl.program_id(2)
        step = (pl.program_id(0) * n_pairs + pair_idx) * n_groups + group
        n_c, n_o = len(carry.arrays), len(carry.out_shapes)
        own = refs[:6] + refs[6 + n_c:9 + n_c] + refs[9 + n_c + n_o:13 + n_c + n_o]
        carry.run(refs, 6, 3, step, b_sz * n_pairs * n_groups, lambda: compute(pair_idx, group, *own))

    def qkv_spec(kind):
        return pl.BlockSpec((None, s_len, LANES),
                            lambda b, p, g: (b, 0, q_off + kind * per_kind + g * n_pairs + p))

    tok_spec = pl.BlockSpec((None, s_len, LANES), lambda b, p, g: (b, 0, p))
    out_spec = pl.BlockSpec((None, s_len, LANES), lambda b, p, g: (b, 0, g * n_pairs + p))
    out_sd = jax.ShapeDtypeStruct((b_sz, s_len, DIL_WIDTH), BF16)
    c_in, c_out, c_shapes, c_alias, c_sems = carry.call_args(6, 3)
    res = pl.pallas_call(
        body, name="dil_bwd", grid=(b_sz, n_pairs, n_groups),
        in_specs=[qkv_spec(0), qkv_spec(1), qkv_spec(2), tok_spec,
                  pl.BlockSpec((None, None, 2, s_len, LANES), lambda b, p, g: (b, p, 0, 0, 0)), tok_spec] + c_in,
        out_specs=[out_spec, out_spec, out_spec] + c_out,
        out_shape=[out_sd, out_sd, out_sd] + c_shapes,
        input_output_aliases=c_alias,
        scratch_shapes=[pltpu.VMEM((2, s_len, LANES), F32)] + [pltpu.VMEM((s_len, LANES), F32)] * 3 + c_sems,
        compiler_params=pltpu.CompilerParams(dimension_semantics=("arbitrary", "arbitrary", "arbitrary"),
                                             vmem_limit_bytes=VMEM_CAP),
    )(qkv, qkv, qkv, o_dl, lse, do_dl, *carry.arrays)
    return res[:3], res[3:]


def _mesh_pos():
    return lax.axis_index("x"), lax.axis_index("y"), lax.axis_index("c")


def _other_chips(x, y):
    return [(1 - x, y), (x, 1 - y), (1 - x, 1 - y)]


def _hbm_specs(n):
    return [pl.BlockSpec(memory_space=pl.ANY)] * n


SWAPPED = ("w_ffn_in",)


def _slot(x, y, swapped):
    return 2 * y + x if swapped else 2 * x + y


def _cast_to_slab(w, name):
    rows, cols = w.shape
    mine = jnp.reshape(_slot(lax.axis_index("x"), lax.axis_index("y"), False), (1,)).astype(jnp.int32)

    def body(idx_ref, w_ref, o_ref):
        o_ref[...] = w_ref[...].astype(BF16)

    return pl.pallas_call(
        body, name=name,
        grid_spec=pltpu.PrefetchScalarGridSpec(
            num_scalar_prefetch=1, grid=(1,),
            in_specs=[pl.BlockSpec((rows, cols), lambda i, idx: (0, 0))],
            out_specs=pl.BlockSpec((None, rows, cols), lambda i, idx: (idx[0], 0, 0))),
        out_shape=_hbm_array((N_CHIPS, rows, cols), BF16),
        compiler_params=pltpu.CompilerParams(vmem_limit_bytes=_vmem_limit(rows * cols * 6)),
    )(mine, w)


def _gather_issue(slabs, send_sems, recv_sems, swapped):
    x, y, c = _mesh_pos()
    for k, slab in enumerate(slabs):
        half = slab.shape[1] // 2
        rows = slab.at[_slot(x, y, swapped[k]), pl.ds(c * half, half), :]
        for r, (px, py) in enumerate(_other_chips(x, y)):
            pltpu.make_async_remote_copy(
                src_ref=rows, dst_ref=rows, send_sem=send_sems.at[6 * k + r], recv_sem=recv_sems.at[6 * k + r],
                device_id=(px, py, c), device_id_type=MESH).start()


def _gather_complete(slabs, send_sems, recv_sems, swapped):
    x, y, c = _mesh_pos()
    chips = _other_chips(x, y)

    def copy(k, sem, block, rows, to):
        ref = slabs[k].at[block, rows, :]
        return pltpu.make_async_remote_copy(
            src_ref=ref, dst_ref=ref, send_sem=send_sems.at[sem], recv_sem=recv_sems.at[sem],
            device_id=to, device_id_type=MESH)

    for k, slab in enumerate(slabs):
        half = slab.shape[1] // 2
        for r, (px, py) in enumerate(chips):
            theirs = _slot(px, py, swapped[k])
            copy(k, 6 * k + r, theirs, pl.ds(c * half, half), (px, py, c)).wait_recv()
            copy(k, 6 * k + 3 + r, theirs, pl.ds(c * half, half), (x, y, 1 - c)).start()
    for k, slab in enumerate(slabs):
        half = slab.shape[1] // 2
        for r, (px, py) in enumerate(chips):
            copy(k, 6 * k + 3 + r, _slot(px, py, swapped[k]), pl.ds((1 - c) * half, half), (x, y, 1 - c)).wait_recv()
    for k, slab in enumerate(slabs):
        half = slab.shape[1] // 2
        for r, (px, py) in enumerate(chips):
            copy(k, 6 * k + r, _slot(x, y, swapped[k]), pl.ds(c * half, half), (px, py, c)).wait_send()
            copy(k, 6 * k + 3 + r, _slot(px, py, swapped[k]), pl.ds(c * half, half), (x, y, 1 - c)).wait_send()


def _gather_sems(n):
    return [pltpu.SemaphoreType.DMA((6 * n,)), pltpu.SemaphoreType.DMA((6 * n,))]


def _gather_carry(slabs, names):
    swapped = [k in SWAPPED for k in names]
    return _Carry(slabs, [_hbm_array(a.shape, a.dtype) for a in slabs], True, _gather_sems(len(slabs)),
                  lambda ins, outs, sems: _gather_issue(outs, *sems, swapped),
                  lambda ins, outs, sems: _gather_complete(outs, *sems, swapped))


def _cast_carry(shards, names):
    swapped = [k in SWAPPED for k in names]

    def start(ins, outs, sems):
        x, y, _ = _mesh_pos()
        for w, slab, sw in zip(ins, outs, swapped):
            def cast(f32_buf, bf16_buf, sem, w=w, slab=slab, sw=sw):
                load = pltpu.make_async_copy(w, f32_buf, sem)
                load.start()
                load.wait()
                bf16_buf[...] = f32_buf[...].astype(BF16)
                store = pltpu.make_async_copy(bf16_buf, slab.at[_slot(x, y, sw)], sem)
                store.start()
                store.wait()

            pl.run_scoped(cast, pltpu.VMEM(w.shape, F32), pltpu.VMEM(w.shape, BF16), pltpu.SemaphoreType.DMA)

    return _Carry(shards, [_hbm_array((N_CHIPS,) + w.shape, BF16) for w in shards], False, [], start,
                  lambda ins, outs, sems: None)


def _pair_copies(ins, outs, send_sems, recv_sems):
    x, y, c = _mesh_pos()
    copies = []
    for k, g in enumerate(ins):
        half = g.shape[1] // 2
        copies.append(pltpu.make_async_remote_copy(
            src_ref=g.at[:, pl.ds((1 - c) * half, half), :], dst_ref=outs[k],
            send_sem=send_sems.at[k], recv_sem=recv_sems.at[k],
            device_id=(x, y, 1 - c), device_id_type=MESH))
    return copies


def _pair_carry(grads):
    n = len(grads)

    def start(ins, outs, sems):
        for cp in _pair_copies(ins, outs, *sems):
            cp.start()

    def finish(ins, outs, sems):
        for cp in _pair_copies(ins, outs, *sems):
            cp.wait()

    return _Carry(grads, [jax.ShapeDtypeStruct((N_CHIPS, g.shape[1] // 2, g.shape[2]), g.dtype) for g in grads], False,
                  [pltpu.SemaphoreType.DMA((n,)), pltpu.SemaphoreType.DMA((n,))], start, finish)


def _pair_exchange(grads, tag):
    carry = _pair_carry(grads)
    n = len(grads)

    def body(*refs):
        carry.start(refs[:n], refs[n:2 * n], refs[2 * n:])
        carry.finish(refs[:n], refs[n:2 * n], refs[2 * n:])

    return pl.pallas_call(
        body, name="grad_pair_exchange_" + tag, in_specs=_hbm_specs(n), out_specs=_hbm_specs(n),
        out_shape=carry.out_shapes, scratch_shapes=carry.sems,
    )(*grads)


def _pair_sum(grad, other, name, swapped):
    _, rows, cols = grad.shape
    half = rows // 2
    x, y, c = _mesh_pos()
    idx = jnp.stack([c, _slot(x, y, swapped)]).astype(jnp.int32)

    def body(idx_ref, g_ref, p_ref, own_ref, sb_ref):
        s = g_ref[...] + p_ref[...].astype(F32)
        sb_ref[...] = s.astype(BF16)

        @pl.when(pl.program_id(0) == idx_ref[1])
        def _():
            own_ref[...] = s

    blk = pl.BlockSpec((None, half, cols), lambda p, idx: (p, 0, 0))
    return pl.pallas_call(
        body, name=name,
        grid_spec=pltpu.PrefetchScalarGridSpec(
            num_scalar_prefetch=1, grid=(N_CHIPS,),
            in_specs=[pl.BlockSpec((None, half, cols), lambda p, idx: (p, idx[0], 0)), blk],
            out_specs=[pl.BlockSpec((half, cols), lambda p, idx: (0, 0)), blk]),
        out_shape=[jax.ShapeDtypeStruct((half, cols), F32), jax.ShapeDtypeStruct((N_CHIPS, half, cols), BF16)],
        compiler_params=pltpu.CompilerParams(dimension_semantics=("arbitrary",),
                                             vmem_limit_bytes=_vmem_limit(4 * half * cols * 4)),
    )(idx, grad, other)


def _chip_copies(sums_bf16, lands, send_sems, recv_sems, swapped):
    x, y, c = _mesh_pos()
    return [pltpu.make_async_remote_copy(
        src_ref=sums_bf16[k].at[_slot(px, py, swapped[k])], dst_ref=lands[k].at[r],
        send_sem=send_sems.at[3 * k + r], recv_sem=recv_sems.at[3 * k + r],
        device_id=(px, py, c), device_id_type=MESH)
        for k in range(len(sums_bf16)) for r, (px, py) in enumerate(_other_chips(x, y))]


def _chip_carry(sums_bf16, names):
    swapped = [k in SWAPPED for k in names]

    def start(ins, outs, sems):
        for cp in _chip_copies(ins, outs, *sems, swapped):
            cp.start()

    def finish(ins, outs, sems):
        for cp in _chip_copies(ins, outs, *sems, swapped):
            cp.wait()

    return _Carry(sums_bf16, _chip_landing(sums_bf16), False, _chip_sems(len(sums_bf16)), start, finish)


def _chip_sems(n):
    return [pltpu.SemaphoreType.DMA((3 * n,)), pltpu.SemaphoreType.DMA((3 * n,))]


def _chip_landing(sums_bf16):
    return [jax.ShapeDtypeStruct((N_CHIPS - 1,) + s.shape[1:], BF16) for s in sums_bf16]


def _chip_sum(own, landed, name):
    rows, cols = own.shape
    core = jnp.reshape(lax.axis_index("c"), (1,)).astype(jnp.int32)

    def body(core_ref, o_ref, l_ref, out_ref):
        out_ref[...] = ((o_ref[...] + l_ref[0].astype(F32)) + l_ref[1].astype(F32)) + l_ref[2].astype(F32)

    return pl.pallas_call(
        body, name=name,
        grid_spec=pltpu.PrefetchScalarGridSpec(
            num_scalar_prefetch=1, grid=(1,),
            in_specs=[pl.BlockSpec((rows, cols), lambda i, core_ref: (0, 0)),
                      pl.BlockSpec((N_CHIPS - 1, rows, cols), lambda i, core_ref: (0, 0, 0))],
            out_specs=pl.BlockSpec((rows, cols), lambda i, core_ref: (core_ref[0], 0))),
        out_shape=jax.ShapeDtypeStruct((2 * rows, cols), F32),
        compiler_params=pltpu.CompilerParams(vmem_limit_bytes=_vmem_limit(3 * rows * cols * 4)),
    )(core, own, landed)


def _halves_carry(fulls):
    n = len(fulls)

    def copies(outs, send_sems, recv_sems, own):
        x, y, c = _mesh_pos()
        res = []
        for k, out in enumerate(outs):
            half = out.shape[0] // 2
            rows = out.at[pl.ds((c if own else 1 - c) * half, half), :]
            res.append(pltpu.make_async_remote_copy(
                src_ref=rows, dst_ref=rows, send_sem=send_sems.at[k], recv_sem=recv_sems.at[k],
                device_id=(x, y, 1 - c), device_id_type=MESH))
        return res

    def start(ins, outs, sems):
        for cp in copies(outs, *sems, True):
            cp.start()

    def finish(ins, outs, sems):
        for cp in copies(outs, *sems, False):
            cp.wait_recv()
        for cp in copies(outs, *sems, True):
            cp.wait_send()

    return _Carry(fulls, [jax.ShapeDtypeStruct(f.shape, F32) for f in fulls], True,
                  [pltpu.SemaphoreType.DMA((n,)), pltpu.SemaphoreType.DMA((n,))], start, finish)


def _final_exchange(fulls, v):
    n = len(fulls)
    rows, cols = v.shape
    n_dev = 8

    def body(*refs):
        v_ref, out_ref = refs[0], refs[1 + 2 * n]
        outs = refs[1 + n:1 + 2 * n]
        buf, v_send, v_recv, h_send, h_recv = refs[2 + 2 * n:]
        x, y, c = _mesh_pos()
        me = 4 * x + 2 * y + c
        buf[me] = v_ref[...]
        peers = [(1 - x if r & 4 else x, 1 - y if r & 2 else y, 1 - c if r & 1 else c) for r in range(1, n_dev)]
        copies = []
        for r, peer in enumerate(peers):
            copies.append(pltpu.make_async_remote_copy(
                src_ref=v_ref, dst_ref=buf.at[me], send_sem=v_send.at[r], recv_sem=v_recv.at[r],
                device_id=peer, device_id_type=MESH))
        for k in range(n):
            half = fulls[k].shape[0] // 2
            mine = outs[k].at[pl.ds(c * half, half), :]
            copies.append(pltpu.make_async_remote_copy(
                src_ref=mine, dst_ref=mine, send_sem=h_send.at[k], recv_sem=h_recv.at[k],
                device_id=(x, y, 1 - c), device_id_type=MESH))
        for cp in copies:
            cp.start()
        for r, (px, py, pc) in enumerate(peers):
            pltpu.make_async_remote_copy(
                src_ref=v_ref, dst_ref=buf.at[4 * px + 2 * py + pc], send_sem=v_send.at[r], recv_sem=v_recv.at[r],
                device_id=(px, py, pc), device_id_type=MESH).wait_recv()
        for k in range(n):
            half = fulls[k].shape[0] // 2
            theirs = outs[k].at[pl.ds((1 - c) * half, half), :]
            pltpu.make_async_remote_copy(
                src_ref=theirs, dst_ref=theirs, send_sem=h_send.at[k], recv_sem=h_recv.at[k],
                device_id=(x, y, 1 - c), device_id_type=MESH).wait_recv()
        for cp in copies:
            cp.wait_send()
        acc = buf[0]
        for d in range(1, n_dev):
            acc = acc + buf[d]
        out_ref[...] = acc
        out_ref[3:4, :] = jnp.broadcast_to(jnp.sum(acc[3:4, :], axis=1, keepdims=True), (1, cols))

    vm = pl.BlockSpec(memory_space=pltpu.VMEM)
    res = pl.pallas_call(
        body, name="final_exchange",
        in_specs=[vm] + _hbm_specs(n), out_specs=_hbm_specs(n) + [vm],
        out_shape=[jax.ShapeDtypeStruct(f.shape, F32) for f in fulls] + [jax.ShapeDtypeStruct((rows, cols), F32)],
        input_output_aliases={1 + k: k for k in range(n)},
        scratch_shapes=[pltpu.VMEM((n_dev, rows, cols), F32),
                        pltpu.SemaphoreType.DMA((n_dev - 1,)), pltpu.SemaphoreType.DMA((n_dev - 1,)),
                        pltpu.SemaphoreType.DMA((n,)), pltpu.SemaphoreType.DMA((n,))],
    )(v, *fulls)
    return res[:n], res[n]


def _adamw_math(w, g, m, v):
    m = ADAM_B1 * m + (1.0 - ADAM_B1) * g
    v = ADAM_B2 * v + (1.0 - ADAM_B2) * (g * g)
    m_hat = m / (1.0 - ADAM_B1 ** ADAM_STEP)
    v_hat = v / (1.0 - ADAM_B2 ** ADAM_STEP)
    delta = -ADAM_LR * (m_hat / (jnp.sqrt(v_hat) + ADAM_EPS) + ADAM_WD * w)
    return delta, m, v


def _adamw(w, g, m, v, name):
    rows, cols = w.shape
    tm = rows // 2 if (rows // 2) % 8 == 0 else rows
    return _rowwise(_adamw_math, [w, g, m, v], [], [(cols, F32)] * 3, [], tm=tm, name=name)


def _unshard_cols(gathered):
    n, r, c = gathered.shape
    return jnp.transpose(gathered, (1, 0, 2)).reshape(r, n * c)


def _shard_cols(full):
    r, nc = full.shape
    return jnp.transpose(full.reshape(r, N_CHIPS, nc // N_CHIPS), (1, 0, 2))


LATE = ["w_sb_up", "w_dil_up", "w_out", "w_ffn_in", "w_ffn_out"]


def _late_weights(slabs, d_model, d_ff):
    g = dict(zip(LATE, slabs))
    return (_unshard_cols(g["w_sb_up"]), _unshard_cols(g["w_dil_up"]), g["w_out"].reshape(d_model, d_model),
            _unshard_cols(g["w_ffn_in"]), g["w_ffn_out"].reshape(d_ff, d_model))


ROW_SHARDED = ("w_in", "w_out", "w_ffn_in", "w_ffn_out")


def _chip_major(grads):
    out = []
    for k, g in grads.items():
        if k in ROW_SHARDED:
            out.append(g.reshape(N_CHIPS, g.shape[0] // N_CHIPS, g.shape[1]))
        else:
            out.append(_shard_cols(g))
    return out


def _pair_sums(full, others, names):
    return [_pair_sum(g, o, "grad_pair_sum_" + k, k in SWAPPED) for g, o, k in zip(full, others, names)]


def _chip_sums(pair, landed, names):
    return {k: _chip_sum(p[0], l, "grad_chip_sum_" + k) for p, l, k in zip(pair, landed, names)}


def _fwd_bwd(x, loss_target, g_mix, g_ffn, g_fin, slab_in, late_shards):
    b_sz, s_len, d_model = x.shape
    t = b_sz * s_len
    d_ff = late_shards[-1].shape[0] * N_CHIPS
    x2d = x.reshape(t, d_model)
    tgt2d = loss_target.reshape(t, d_model)

    u, (slab_in, *late_slabs) = _rowwise(
        lambda xv, g: (_rms_stats(xv)[0] * g,), [_in_hbm(x2d)], [g_mix], [(d_model, BF16)], [], tm=512, name="norm_mix",
        carry=_gather_carry([slab_in], ["w_in"]) + _cast_carry(late_shards, LATE), out_type=_hbm_array)
    u, wt_in = _in_hbm(u), _in_hbm(slab_in.reshape(-1, d_model))
    qkv, (slab_ffn_out,) = _mm(u, wt_in, tb=True, b_cols=(0, QKV_WIDTH), tm=2048, tn=768, tk=d_model, name="proj_qkv",
                               carry=_gather_carry(late_slabs[4:], LATE[4:]))
    gates = _mm(u, wt_in, tb=True, b_cols=(QKV_WIDTH, 2 * d_model), out_dtype=BF16, tm=t, tn=256, tk=d_model,
                name="proj_gates")
    qkv3 = qkv.reshape(b_sz, s_len, QKV_WIDTH)
    o_sb, (slab_ffn_in,) = _sb_fwd(qkv3, b_sz, s_len, _gather_carry(late_slabs[3:4], LATE[3:4]))
    o_dl, lse, small_slabs = _dil_fwd(qkv3, b_sz, s_len, _gather_carry(late_slabs[:3], LATE[:3]))
    wf_sb_up, wf_dil_up, wf_out, wf_ffn_in, wf_ffn_out = _late_weights(
        list(small_slabs) + [slab_ffn_in, slab_ffn_out], d_model, d_ff)
    o_sb2, o_dl2 = o_sb.reshape(t, SB_WIDTH), o_dl.reshape(t, DIL_OUT_WIDTH)
    y_sb = _mm(o_sb2, wf_sb_up, out_dtype=BF16, tm=1024, tn=1024, tk=SB_WIDTH, name="sb_up", out_type=_hbm_array)
    y_dl = _mm(o_dl2, wf_dil_up, out_dtype=BF16, tm=1024, tn=1024, tk=DIL_OUT_WIDTH, name="dil_up", out_type=_hbm_array)

    def merge_fn(gt, ys, yd):
        return (_sigmoid(gt[:, :d_model]) * ys + _sigmoid(gt[:, d_model:]) * yd,)

    (merged,) = _rowwise(merge_fn, [gates, y_sb, y_dl], [], [(d_model, BF16)], [], tm=512, name="merge")
    x1 = _mm(merged, wf_out, add=x2d, tm=512, tn=1024, tk=d_model, name="mix_out")
    (u2,) = _rowwise(lambda xv, g: (_rms_stats(xv)[0] * g,), [_in_hbm(x1)], [g_ffn], [(d_model, BF16)], [], tm=512, name="norm_ffn",
                     out_type=_hbm_array)
    u2 = _in_hbm(u2)
    half_ff = d_ff // 2

    def act_fn(hv):
        gate = hv[:, :half_ff]
        return hv, gate * _sigmoid(gate) * hv[:, half_ff:]

    h, act = _mm(u2, wf_ffn_in, tm=512, tn=d_ff, tk=d_model, name="ffn_in",
                 epilogue=(act_fn, [], [], [(d_ff, BF16), (half_ff, BF16)], []))
    def head_fn(xv, tg, g):
        xhat, r = _rms_stats(xv)
        err = xhat * g - tg
        dy = err * (1.0 / d_model)
        dx, dg_rows = _rms_bwd(dy, xhat, r, g)
        loss_lanes = (0.5 / d_model) * jnp.sum(err * err, axis=0, keepdims=True)
        return dx, dx, jnp.sum(dg_rows, axis=0, keepdims=True), loss_lanes

    dx2, dx2_b, dg_fin, loss_lanes = _mm(
        act, wf_ffn_out, add=x1, tm=512, tn=1024, tk=d_ff, name="ffn_out",
        epilogue=(head_fn, [tgt2d], [g_fin], [(d_model, F32), (d_model, BF16)], [(1, d_model), (1, d_model)]))

    def dact_fn(da, hv):
        gate, up = hv[:, :half_ff], hv[:, half_ff:]
        sg = _sigmoid(gate)
        dgate = da * up * (sg * (1.0 + gate * (1.0 - sg)))
        return (jnp.concatenate([dgate, da * (gate * sg)], axis=1),)

    dx2_b = _in_hbm(dx2_b)
    (dh,) = _mm(dx2_b, wf_ffn_out, tb=True, tm=512, tn=half_ff, tk=d_model, name="ffn_out_dx",
                epilogue=(dact_fn, [h], [], [(d_ff, BF16)], []))
    gw_ffn_out = _mm(act, dx2_b, ta=True, tm=256, tn=d_model, tk=t, name="ffn_out_dw")
    def norm_bwd_fn(du_, dres, xv, g):
        xhat, r = _rms_stats(xv)
        dx, dg_rows = _rms_bwd(du_, xhat, r, g)
        return dres + dx, jnp.sum(dg_rows, axis=0, keepdims=True)

    def norm_bwd_twice(*args):
        dx, dg = norm_bwd_fn(*args)
        return dx, dx, dg

    dx1, dx1_b, dg_ffn = _mm(dh, wf_ffn_in, tb=True, tm=512, tn=1024, tk=2 * d_ff, name="ffn_in_dx",
                             epilogue=(norm_bwd_twice, [dx2, x1], [g_ffn], [(d_model, F32), (d_model, BF16)], [(1, d_model)]))
    gwt_ffn_in = _mm(dh, u2, ta=True, tm=512, tn=d_model, tk=t, name="ffn_in_dw")

    dx1_b = _in_hbm(dx1_b)
    dmerged = _mm(dx1_b, wf_out, tb=True, out_dtype=BF16, tm=512, tn=1024, tk=d_model, name="mix_out_dx",
                  out_type=_hbm_array)
    gw_out = _mm(merged, dx1_b, ta=True, tm=256, tn=d_model, tk=t, name="mix_out_dw")

    def merge_bwd_fn(gt, ys, yd, dm):
        s_sb, s_dl = _sigmoid(gt[:, :d_model]), _sigmoid(gt[:, d_model:])
        dgates = jnp.concatenate([dm * ys * s_sb * (1.0 - s_sb), dm * yd * s_dl * (1.0 - s_dl)], axis=1)
        return dgates, dm * s_sb, dm * s_dl

    full_big = _chip_major({"w_out": gw_out, "w_ffn_in": gwt_ffn_in, "w_ffn_out": gw_ffn_out})
    dgates, dy_sb, dy_dl, others_big = _rowwise(
        merge_bwd_fn, [gates, y_sb, y_dl, dmerged], [], [(2 * d_model, BF16), (d_model, BF16), (d_model, BF16)], [],
        tm=256, name="merge_bwd", carry=_pair_carry(full_big))
    pair_big = _pair_sums(full_big, others_big, LATE[2:])
    do_sb = _mm(dy_sb, wf_sb_up, tb=True, out_dtype=BF16, tm=1024, tn=SB_WIDTH, tk=d_model, name="sb_up_dx")
    gw_sb_up = _mm(o_sb2, dy_sb, ta=True, tm=SB_WIDTH, tn=1024, tk=512, name="sb_up_dw")
    do_dl = _mm(dy_dl, wf_dil_up, tb=True, tm=1024, tn=DIL_OUT_WIDTH, tk=d_model, name="dil_up_dx")
    gw_dil_up = _mm(o_dl2, dy_dl, ta=True, tm=DIL_OUT_WIDTH, tn=1024, tk=512, name="dil_up_dw")
    full_small = _chip_major({"w_sb_up": gw_sb_up, "w_dil_up": gw_dil_up})
    (dq_sb, dk_sb, dv_sb), brought = _sb_bwd(
        qkv3, o_sb, do_sb.reshape(b_sz, s_len, SB_WIDTH), b_sz, s_len,
        _chip_carry([p[1] for p in pair_big[:2]], LATE[2:4]) + _pair_carry(full_small))
    pair_small = _pair_sums(full_small, brought[2:], LATE[:2])
    (dq_dl, dk_dl, dv_dl), landed_b = _dil_bwd(
        qkv3, o_dl, lse, do_dl.reshape(b_sz, s_len, DIL_OUT_WIDTH), b_sz, s_len,
        _chip_carry([pair_big[2][1], pair_small[0][1], pair_small[1][1]], [LATE[4], LATE[0], LATE[1]]))
    pair = pair_small + pair_big
    landed = [landed_b[1], landed_b[2], brought[0], brought[1], landed_b[0]]
    dproj = [a.reshape(t, -1) for a in (dq_sb, dk_sb, dv_sb, dq_dl, dk_dl, dv_dl)] + [dgates]
    gwt_in, gwt_in_b = _mm(dproj, u, ta=True, tm=256, tn=d_model, tk=t, name="proj_dw",
                           epilogue=(lambda tile: (tile, tile), [], [], [(d_model, F32), (d_model, BF16)], []))
    full_in = _chip_major({"w_in": gwt_in})
    pair_in = _pair_sums(full_in, _pair_exchange(_chip_major({"w_in": gwt_in_b}), "w_in"), ["w_in"])
    late_halves = _chip_sums(pair, landed, LATE)
    (dx, dg_mix), brought_in = _mm(
        dproj, wt_in, tm=512, tn=1024, tk=wt_in.shape[0], name="proj_dx",
        carry=_halves_carry([late_halves[k] for k in LATE]) + _chip_carry([p[1] for p in pair_in], ["w_in"]),
        epilogue=(norm_bwd_fn, [dx1, x2d], [g_mix], [(d_model, F32)], [(1, d_model)]))

    grads = dict(zip(LATE, brought_in[:len(LATE)]))
    grads.update(_chip_sums(pair_in, brought_in[len(LATE):], ["w_in"]))
    return dx, grads, dg_mix, dg_ffn, dg_fin, loss_lanes


def kernel(x, norm_mix_g, w_in, w_sb_up, w_dil_up, w_out, norm_ffn_g, w_ffn_in, w_ffn_out, norm_final_g, loss_target, m_norm_mix_g, m_w_in, m_w_sb_up, m_w_dil_up, m_w_out, m_norm_ffn_g, m_w_ffn_in, m_w_ffn_out, m_norm_final_g, v_norm_mix_g, v_w_in, v_w_sb_up, v_w_dil_up, v_w_out, v_norm_ffn_g, v_w_ffn_in, v_w_ffn_out, v_norm_final_g):
    b_sz, s_len, d_model = x.shape
    d_ff = w_ffn_out.shape[1] * N_CHIPS
    g_mix, g_ffn, g_fin = norm_mix_g, norm_ffn_g, norm_final_g.reshape(1, d_model)

    names = ["w_in", "w_sb_up", "w_dil_up", "w_out", "w_ffn_in", "w_ffn_out"]
    shards = {"w_in": jnp.swapaxes(w_in[0], 0, 1), "w_sb_up": w_sb_up[0], "w_dil_up": w_dil_up[0], "w_out": w_out[0],
              "w_ffn_in": w_ffn_in[0], "w_ffn_out": w_ffn_out[0]}
    slab_in = _cast_to_slab(shards["w_in"], "cast_w_in")

    dx, grads, dg_mix, dg_ffn, dg_fin, loss_lanes = _fwd_bwd(
        x, loss_target, g_mix, g_ffn, g_fin, slab_in, [shards[k] for k in LATE])

    small = jnp.concatenate([dg_mix, dg_ffn, dg_fin, loss_lanes, jnp.zeros((4, d_model), F32)], axis=0)
    (grads["w_in"],), small = _final_exchange([grads["w_in"]], small)
    grads["w_ffn_in"] = jnp.swapaxes(grads["w_ffn_in"], 0, 1)
    loss = small[3, 0]
    gains = jnp.concatenate([g_mix, g_ffn, g_fin, jnp.zeros((5, d_model), F32)], axis=0)
    gains_m = jnp.concatenate([m_norm_mix_g, m_norm_ffn_g, m_norm_final_g.reshape(1, d_model), jnp.zeros((5, d_model), F32)], axis=0)
    gains_v = jnp.concatenate([v_norm_mix_g, v_norm_ffn_g, v_norm_final_g.reshape(1, d_model), jnp.ones((5, d_model), F32)], axis=0)
    gd, gm, gv = _rowwise(_adamw_math, [gains, small, gains_m, gains_v], [], [(d_model, F32)] * 3, [], tm=8, name="adamw_gains")

    moments = {"w_in": (jnp.swapaxes(m_w_in[0], 0, 1), jnp.swapaxes(v_w_in[0], 0, 1)),
               "w_sb_up": (m_w_sb_up[0], v_w_sb_up[0]), "w_dil_up": (m_w_dil_up[0], v_w_dil_up[0]),
               "w_out": (m_w_out[0], v_w_out[0]), "w_ffn_in": (m_w_ffn_in[0], v_w_ffn_in[0]),
               "w_ffn_out": (m_w_ffn_out[0], v_w_ffn_out[0])}
    upd = {k: _adamw(shards[k], grads[k], moments[k][0], moments[k][1], "adamw_" + k) for k in names}

    def as_output(k, a):
        return (jnp.swapaxes(a, 0, 1) if k == "w_in" else a)[None]

    def w_out_of(i):
        return [as_output(k, upd[k][i]) for k in names]

    def ordered(mix, ws, ffn_g, fin):
        return [mix, ws[0], ws[1], ws[2], ws[3], ffn_g, ws[4], ws[5], fin]

    grad_ws = [as_output(k, grads[k]) for k in names]
    outs = [loss, dx.reshape(b_sz, s_len, d_model)]
    outs += ordered(small[0:1], grad_ws, small[1:2], small[2])
    outs += ordered(gd[0:1], w_out_of(0), gd[1:2], gd[2])
    outs += ordered(gm[0:1], w_out_of(1), gm[1:2], gm[2])
    outs += ordered(gv[0:1], w_out_of(2), gv[1:2], gv[2])
    return tuple(outs)
```

```python
import functools
import math

import jax
import jax.numpy as jnp
from jax import lax
from jax.experimental import pallas as pl
from jax.experimental.pallas import tpu as pltpu

F32 = jnp.float32
BF16 = jnp.bfloat16
MESH = pl.DeviceIdType.MESH

HEAD_DIM = 64
SB_HEADS = 8
DIL_PAIRS = ((128, 1), (512, 4), (2048, 16))
DIL_HEADS_PER_GROUP = 4
DIL_HEADS = DIL_HEADS_PER_GROUP * len(DIL_PAIRS)
SB_WIDTH = SB_HEADS * HEAD_DIM
DIL_WIDTH = DIL_HEADS * HEAD_DIM
DIL_OUT_WIDTH = DIL_HEADS_PER_GROUP * HEAD_DIM
QKV_WIDTH = 3 * SB_WIDTH + 3 * DIL_WIDTH
RMS_EPS = 1e-6
ALIBI_MAX_BIAS = 8.0
ADAM_LR = 0.001
ADAM_B1 = 0.9
ADAM_B2 = 0.999
ADAM_EPS = 1e-08
ADAM_WD = 0.01
ADAM_STEP = 10

LANES = 128
BLK = 128
NEG = -1e30
EXP_UNDERFLOW = -104.0
SB_FWD_CHAINS = 4
SB_BWD_CHAINS = 4
DIL_CHAINS = 4
N_CHIPS = 4
VMEM_CAP = 56 * 1024 * 1024


def _vmem_limit(tile_bytes):
    return int(min(VMEM_CAP, max(32 * 1024 * 1024, 3 * tile_bytes + 8 * 1024 * 1024)))


def _hbm_array(shape, dtype):
    return pltpu.HBM(shape, dtype)


def _nbytes(shape, dtype):
    return math.prod(shape) * jnp.dtype(dtype).itemsize


def _in_hbm(x):
    return pltpu.with_memory_space_constraint(x, pltpu.HBM)


def _dot(a, b):
    return jnp.dot(a, b, preferred_element_type=F32)


def _dot_nt(a, b):
    return lax.dot_general(a, b, (((1,), (1,)), ((), ())), preferred_element_type=F32)


def _dot_tn(a, b):
    return lax.dot_general(a, b, (((0,), (0,)), ((), ())), preferred_element_type=F32)


def _split2(x):
    hi = x.astype(BF16)
    lo = (x - hi.astype(F32)).astype(BF16)
    return hi, lo


def _sigmoid(x):
    return pl.reciprocal(1.0 + jnp.exp(-x), approx=True)


class _Carry:
    def __init__(self, arrays=(), out_shapes=(), aliased=False, sems=(), start=None, finish=None):
        self.arrays, self.out_shapes = list(arrays), list(out_shapes)
        self.n_aliased = len(self.arrays) if aliased is True else int(aliased)
        self.sems, self.start, self.finish = list(sems), start, finish

    def __bool__(self):
        return bool(self.arrays)

    def __add__(self, other):
        assert not other.n_aliased and self.n_aliased in (0, len(self.out_shapes))
        n_a, n_o, n_s = len(self.arrays), len(self.out_shapes), len(self.sems)
        return _Carry(
            self.arrays + other.arrays, self.out_shapes + other.out_shapes, self.n_aliased, self.sems + other.sems,
            lambda i, o, s: (self.start(i[:n_a], o[:n_o], s[:n_s]), other.start(i[n_a:], o[n_o:], s[n_s:])),
            lambda i, o, s: (self.finish(i[:n_a], o[:n_o], s[:n_s]), other.finish(i[n_a:], o[n_o:], s[n_s:])))

    def call_args(self, n_in, n_out):
        aliases = {n_in + k: n_out + k for k in range(self.n_aliased)}
        return _hbm_specs(len(self.arrays)), _hbm_specs(len(self.out_shapes)), self.out_shapes, aliases, self.sems

    def run(self, refs, n_in, n_out, step, n_steps, compute):
        if not self:
            compute()
            return
        n_c, n_o, n_s = len(self.arrays), len(self.out_shapes), len(self.sems)
        ins = refs[n_in:n_in + n_c]
        outs = refs[n_in + n_c + n_out:n_in + n_c + n_out + n_o]
        sems = refs[len(refs) - n_s:]

        @pl.when(step == 0)
        def _():
            self.start(ins, outs, sems)

        compute()

        @pl.when(step == n_steps - 1)
        def _():
            self.finish(ins, outs, sems)


def _mm(a, b, *, ta=False, tb=False, add=None, out_dtype=F32, tm, tn, tk, name, carry=None, epilogue=None,
        b_cols=None, out_type=jax.ShapeDtypeStruct):
    carry = carry or _Carry()
    n_car = len(carry.arrays)
    pieces = list(a) if isinstance(a, (list, tuple)) else [a]
    n_a = len(pieces)
    widths = [p.shape[1] for p in pieces]
    starts = [sum(widths[:p]) for p in range(n_a)]
    if ta:
        kdim, m = pieces[0].shape[0], sum(widths)
    else:
        m, kdim = pieces[0].shape[0], sum(widths)
    if tb:
        n, k2 = b.shape
    else:
        k2, n = b.shape
    col0 = 0
    if b_cols is not None:
        assert b_cols[0] % tn == 0, name
        col0, n = b_cols[0] // tn, b_cols[1]
    assert kdim == k2 and m % tm == 0 and n % tn == 0 and kdim % tk == 0, (name, a.shape, b.shape)
    nk = kdim // tk
    assert n_a == 1 or (nk == 1 and not tb and (not ta or all(w % tm == 0 for w in widths))), name
    grid = (m // tm, n // tn, nk)
    a_mode = dict(pipeline_mode=pl.Buffered(1)) if grid[0] == 1 and nk == 1 else {}
    b_mode = dict(pipeline_mode=pl.Buffered(1)) if grid[1] == 1 and nk == 1 else {}
    if n_a == 1:
        a_specs = [pl.BlockSpec((tk, tm), lambda i, j, k: (k, i), **a_mode) if ta
                   else pl.BlockSpec((tm, tk), lambda i, j, k: (i, k), **a_mode)]
    elif ta:
        a_specs = [pl.BlockSpec((tk, tm), lambda i, j, k, s=s // tm, w=w // tm: (0, jnp.clip(i - s, 0, w - 1)))
                   for s, w in zip(starts, widths)]
    else:
        a_specs = [pl.BlockSpec((tm, w), lambda i, j, k: (i, 0)) for w in widths]
    b_spec = (pl.BlockSpec((tn, tk), lambda i, j, k: (j + col0, k), **b_mode) if tb
              else pl.BlockSpec((tk, tn), lambda i, j, k: (k, j + col0), **b_mode))
    o_spec = pl.BlockSpec((tm, tn), lambda i, j, k: (i, j))
    dims = ((((0,) if ta else (1,)), ((1,) if tb else (0,))), ((), ()))
    has_add = add is not None
    if epilogue is None:
        ep_fn, ep_rows, ep_params, ep_outs, ep_accs = None, [], [], [], []
        out_sds, out_specs = [out_type((m, n), out_dtype)], [o_spec]
    else:
        ep_fn, ep_rows, ep_params, ep_outs, ep_accs = epilogue
        assert grid[1] == 1 or not ep_accs, name
        out_sds = [out_type((m, w * grid[1]), d) for w, d in ep_outs]
        out_sds += [jax.ShapeDtypeStruct(sh, F32) for sh in ep_accs]
        out_specs = [pl.BlockSpec((tm, w), lambda i, j, k: (i, j)) for w, _ in ep_outs]
        out_specs += [pl.BlockSpec(sh, lambda i, j, k: (0, 0)) for sh in ep_accs]
    n_main = len(out_sds)
    use_scratch = nk > 1 and (ep_fn is not None or jnp.dtype(out_dtype) != jnp.dtype(F32))
    n_in = n_a + 1 + has_add + len(ep_rows) + len(ep_params)

    def finish(total, refs, pid):
        outs = refs[n_in + n_car:n_in + n_car + n_main]
        if ep_fn is None:
            outs[0][...] = total.astype(out_dtype)
            return
        first = n_a + 1 + has_add
        rows = [r[...].astype(F32) for r in refs[first:first + len(ep_rows)]]
        params = [p[...] for p in refs[first + len(ep_rows):n_in]]
        res = ep_fn(total, *rows, *params)
        for o_ref, v in zip(outs[:len(ep_outs)], res):
            o_ref[...] = v.astype(o_ref.dtype)
        acc_refs = outs[len(ep_outs):]
        if acc_refs:
            @pl.when(pid[0] == 0)
            def _():
                for r in acc_refs:
                    r[...] = jnp.zeros(r.shape, F32)

            for r, v in zip(acc_refs, res[len(ep_outs):]):
                r[...] += v

    def compute(refs, pid):
        a_ref, b_ref = refs[0], refs[n_a]
        add_ref = refs[n_a + 1] if has_add else None

        def dot(x, y):
            return lax.dot_general(x.astype(BF16), y.astype(BF16), dims, preferred_element_type=F32)

        if n_a > 1 and ta:
            for p_ref, s, w in zip(refs[:n_a], starts, widths):
                @pl.when((pid[0] >= s // tm) & (pid[0] < (s + w) // tm))
                def _(p_ref=p_ref):
                    prod = dot(p_ref[...], b_ref[...])
                    finish(prod + add_ref[...] if has_add else prod, refs, pid)
            return
        if n_a > 1:
            prod = dot(a_ref[...], b_ref[:widths[0], :])
            for p_ref, s, w in zip(refs[1:n_a], starts[1:], widths[1:]):
                prod += dot(p_ref[...], b_ref[s:s + w, :])
        else:
            prod = dot(a_ref[...], b_ref[...])
        if nk == 1:
            finish(prod + add_ref[...] if has_add else prod, refs, pid)
            return
        acc_ref = refs[n_in + n_car + n_main + len(carry.out_shapes)] if use_scratch else refs[n_in + n_car]
        k = pid[2]

        @pl.when(k == 0)
        def _():
            acc_ref[...] = prod + add_ref[...] if has_add else prod

        @pl.when(k > 0)
        def _():
            acc_ref[...] += prod

        if use_scratch:
            @pl.when(k == nk - 1)
            def _():
                finish(acc_ref[...], refs, pid)

    def body(*refs):
        pid = (pl.program_id(0), pl.program_id(1), pl.program_id(2))
        step = (pid[0] * grid[1] + pid[1]) * nk + pid[2]
        carry.run(refs, n_in, n_main, step, grid[0] * grid[1] * nk, lambda: compute(refs, pid))

    tile_bytes = ((n_a if ta else 1) * _nbytes((tm, tk), pieces[0].dtype)
                  + _nbytes((tk, tn), b.dtype) + 2 * _nbytes((tm, tn), F32)
                  + (_nbytes((tm, tn), F32) if has_add else 0)
                  + sum(_nbytes((tm, r.shape[1]), r.dtype) for r in ep_rows) + sum(_nbytes((tm, w), d) for w, d in ep_outs))
    in_specs = a_specs + [b_spec] + ([o_spec] if has_add else [])
    in_specs += [pl.BlockSpec((tm, r.shape[1] // grid[1]), lambda i, j, k: (i, j)) for r in ep_rows]
    in_specs += [pl.BlockSpec(p.shape, lambda i, j, k: (0, 0)) for p in ep_params]
    args = tuple(pieces) + (b,) + ((add,) if has_add else ()) + tuple(ep_rows) + tuple(ep_params)
    scratch = [pltpu.VMEM((tm, tn), F32)] if use_scratch else []
    serial = bool(carry) or bool(ep_accs)
    c_in, c_out, c_shapes, c_alias, c_sems = carry.call_args(n_in, n_main)
    res = pl.pallas_call(
        body, name=name, grid=grid,
        in_specs=in_specs + c_in, out_specs=out_specs + c_out, out_shape=out_sds + c_shapes,
        input_output_aliases=c_alias, scratch_shapes=scratch + c_sems,
        compiler_params=pltpu.CompilerParams(
            dimension_semantics=("arbitrary",) * 3 if serial else ("parallel", "parallel", "arbitrary"),
            vmem_limit_bytes=_vmem_limit(tile_bytes)),
    )(*args, *carry.arrays)
    main = res[0] if ep_fn is None else list(res[:n_main])
    return (main, res[n_main:]) if carry else main


def _rowwise(fn, rows, params, outs, accs, *, tm, name, carry=None, out_type=jax.ShapeDtypeStruct):
    carry = carry or _Carry()
    t = rows[0].shape[0]
    assert t % tm == 0, (name, t, tm)
    n_r, n_p, n_o, n_c = len(rows), len(params), len(outs), len(carry.arrays)

    def compute(refs, first):
        vals = [r[...].astype(F32) for r in refs[:n_r]] + [p[...] for p in refs[n_r:n_r + n_p]]
        res = fn(*vals)
        o_refs = refs[n_r + n_p + n_c:n_r + n_p + n_c + n_o]
        a_refs = refs[n_r + n_p + n_c + n_o:n_r + n_p + n_c + n_o + len(accs)]
        for o_ref, v in zip(o_refs, res[:n_o]):
            o_ref[...] = v.astype(o_ref.dtype)
        if accs:
            @pl.when(first)
            def _():
                for a_ref in a_refs:
                    a_ref[...] = jnp.zeros(a_ref.shape, F32)

            for a_ref, v in zip(a_refs, res[n_o:]):
                a_ref[...] += v

    def body(*refs):
        step = pl.program_id(0)
        carry.run(refs, n_r + n_p, n_o + len(accs), step, t // tm, lambda: compute(refs, step == 0))

    in_specs = [pl.BlockSpec((tm, r.shape[1]), lambda i: (i, 0)) for r in rows]
    in_specs += [pl.BlockSpec(p.shape, lambda i: (0, 0)) for p in params]
    out_specs = [pl.BlockSpec((tm, w), lambda i: (i, 0)) for w, _ in outs]
    out_specs += [pl.BlockSpec(s, lambda i: (0, 0)) for s in accs]
    out_shape = [out_type((t, w), d) for w, d in outs]
    out_shape += [jax.ShapeDtypeStruct(s, F32) for s in accs]
    tile_bytes = sum(_nbytes((tm, r.shape[1]), r.dtype) for r in rows) + sum(_nbytes((tm, w), F32) for w, _ in outs)
    c_in, c_out, c_shapes, c_alias, c_sems = carry.call_args(n_r + n_p, n_o + len(accs))
    res = pl.pallas_call(
        body, name=name, grid=(t // tm,), in_specs=in_specs + c_in, out_specs=out_specs + c_out,
        out_shape=out_shape + c_shapes, input_output_aliases=c_alias, scratch_shapes=c_sems,
        compiler_params=pltpu.CompilerParams(
            dimension_semantics=("arbitrary",) if accs or carry else ("parallel",),
            vmem_limit_bytes=_vmem_limit(2 * tile_bytes)),
    )(*rows, *params, *carry.arrays)
    own = n_o + len(accs)
    return (list(res[:own]) + [res[own:]]) if carry else res


def _rms_stats(x):
    r = lax.rsqrt(jnp.mean(x * x, axis=-1, keepdims=True) + RMS_EPS)
    return x * r, r


def _rms_bwd(dy, xhat, r, g):
    dxhat = dy * g
    dx = r * (dxhat - xhat * jnp.mean(dxhat * xhat, axis=-1, keepdims=True))
    return dx, dy * xhat


def _sb_consts():
    lane = lax.broadcasted_iota(jnp.int32, (BLK, LANES), 1)
    head0 = lane < HEAD_DIM
    row = lax.broadcasted_iota(jnp.int32, (2 * BLK, BLK), 0) % BLK
    col = lax.broadcasted_iota(jnp.int32, (2 * BLK, BLK), 1)
    causal = col < row
    jj = lax.broadcasted_iota(jnp.int32, (BLK, BLK), 0)
    ss = lax.broadcasted_iota(jnp.int32, (BLK, BLK), 1)
    suffix = jnp.where(jj > ss, 1.0, 0.0).astype(BF16)
    return head0, causal, suffix


def _stack_heads(x, head0):
    zero = jnp.zeros_like(x)
    return jnp.concatenate([jnp.where(head0, x, zero), jnp.where(head0, zero, x)], axis=0)


def _sb_logits(z, causal, masked):
    sp = jnp.log(1.0 + jnp.exp(-jnp.abs(z)))
    log_keep = -(jnp.maximum(z, 0.0) + sp)
    log_beta = jnp.minimum(z, 0.0) - sp
    if masked:
        log_keep = jnp.where(causal, log_keep, 0.0)
    return log_keep, log_beta


def _suffix_sums(x, suffix):
    hi, lo = _split2(x)
    after = _dot(hi, suffix) + _dot(lo, suffix)
    total = jnp.broadcast_to(after[:, 0:1] + x[:, 0:1], x.shape)
    return after, total


def _sb_walk_back(i, state, per_chain, tile):
    def alive(st):
        worst = functools.reduce(jnp.maximum, [st[p][:, 0:1] for p in range(0, len(st), per_chain)])
        return jnp.max(worst) > EXP_UNDERFLOW

    def cond(c):
        return jnp.logical_and(c[0] < i, alive(c[1]))

    def body(c):
        return c[0] + 1, tile(i - 1 - c[0], c[1], False)

    return lax.while_loop(cond, body, (jnp.int32(0), state))[1]


def _lane_blocks(x, n):
    return [x[:, p * LANES:(p + 1) * LANES] for p in range(n)]


def _sb_fwd(qkv, b_sz, s_len, carry):
    nq = s_len // BLK
    n_pairs = SB_WIDTH // LANES
    ch = SB_FWD_CHAINS
    n_steps = n_pairs // ch
    scale = 1.0 / math.sqrt(HEAD_DIM)

    def compute(q_ref, k_ref, v_ref, o_ref):
        head0, causal, suffix = _sb_consts()

        def q_block(i, _):
            qs = pl.multiple_of(i * BLK, BLK)
            q_all = (q_ref[pl.ds(qs, BLK), :] * scale).astype(BF16)
            q01 = [_stack_heads(q, head0) for q in _lane_blocks(q_all, ch)]

            def tile(j, state, masked):
                ks = pl.multiple_of(j * BLK, BLK)
                ks_ = _lane_blocks(k_ref[pl.ds(ks, BLK), :].astype(BF16), ch)
                vs_ = _lane_blocks(v_ref[pl.ds(ks, BLK), :].astype(BF16), ch)
                zs = [_dot_nt(q01[p], ks_[p]) for p in range(ch)]
                logits = [_sb_logits(z, causal, masked) for z in zs]
                sums = [_suffix_sums(lg[0], suffix) for lg in logits]
                out = []
                for p in range(ch):
                    carry, acc = state[2 * p], state[2 * p + 1]
                    after, total = sums[p]
                    a = jnp.exp(logits[p][1] + carry + after)
                    if masked:
                        a = jnp.where(causal, a, 0.0)
                    a_hi, a_lo = _split2(a)
                    a_cat = jnp.concatenate([a_hi[:BLK], a_hi[BLK:], a_lo[:BLK], a_lo[BLK:]], axis=1)
                    v01 = _stack_heads(vs_[p], head0)
                    out += [carry + total, acc + _dot(a_cat, jnp.concatenate([v01, v01], axis=0))]
                return tuple(out)

            state = (jnp.zeros((2 * BLK, BLK), F32), jnp.zeros((BLK, LANES), F32)) * ch
            state = tile(i, state, True)
            state = _sb_walk_back(i, state, 2, tile)
            o_ref[pl.ds(qs, BLK), :] = jnp.concatenate([state[2 * p + 1] for p in range(ch)], axis=1)
            return 0

        lax.fori_loop(0, nq, q_block, 0)

    def body(*refs):
        step = pl.program_id(0) * n_steps + pl.program_id(1)
        o_ref = refs[3 + len(carry.arrays)]
        carry.run(refs, 3, 1, step, b_sz * n_steps, lambda: compute(refs[0], refs[1], refs[2], o_ref))

    blk = lambda off: pl.BlockSpec((None, s_len, ch * LANES), lambda b, p: (b, 0, off + p))
    c_in, c_out, c_shapes, c_alias, c_sems = carry.call_args(3, 1)
    res = pl.pallas_call(
        body, name="sb_fwd", grid=(b_sz, n_steps),
        in_specs=[blk(0), blk(n_steps), blk(2 * n_steps)] + c_in, out_specs=[blk(0)] + c_out,
        out_shape=[jax.ShapeDtypeStruct((b_sz, s_len, SB_WIDTH), F32)] + c_shapes,
        input_output_aliases=c_alias, scratch_shapes=c_sems,
        compiler_params=pltpu.CompilerParams(dimension_semantics=("arbitrary", "arbitrary"),
                                             vmem_limit_bytes=VMEM_CAP),
    )(qkv, qkv, qkv, *carry.arrays)
    return res[0], res[1:]


def _sb_bwd(qkv, o_sb, do_sb, b_sz, s_len, carry):
    nq = s_len // BLK
    n_pairs = SB_WIDTH // LANES
    ch = SB_BWD_CHAINS
    n_steps = n_pairs // ch
    scale = 1.0 / math.sqrt(HEAD_DIM)

    def compute(q_ref, k_ref, v_ref, o_ref, do_ref, dq_ref, dk_ref, dv_ref, dk_acc, dv_acc):
        head0, causal, suffix = _sb_consts()
        lrow = lax.broadcasted_iota(jnp.int32, (LANES, LANES), 0)
        ones_h0 = jnp.where(lrow < HEAD_DIM, 1.0, 0.0).astype(BF16)
        ones_h1 = jnp.where(lrow >= HEAD_DIM, 1.0, 0.0).astype(BF16)
        dk_acc[...] = jnp.zeros(dk_acc.shape, F32)
        dv_acc[...] = jnp.zeros(dv_acc.shape, F32)

        def q_block(i, _):
            qs = pl.multiple_of(i * BLK, BLK)
            q_all = (q_ref[pl.ds(qs, BLK), :] * scale).astype(BF16)
            do_all = do_ref[pl.ds(qs, BLK), :].astype(BF16)
            dd_all = do_all.astype(F32) * o_ref[pl.ds(qs, BLK), :]
            q01 = [_stack_heads(q, head0) for q in _lane_blocks(q_all, ch)]
            do01 = [_stack_heads(d, head0) for d in _lane_blocks(do_all, ch)]
            tot = []
            for dd in _lane_blocks(dd_all, ch):
                dd_hi, dd_lo = _split2(dd)
                tot.append(jnp.concatenate([_dot(dd_hi, ones_h0) + _dot(dd_lo, ones_h0),
                                            _dot(dd_hi, ones_h1) + _dot(dd_lo, ones_h1)], axis=0))

            def tile(j, state, masked):
                ks = pl.multiple_of(j * BLK, BLK)
                ks_ = _lane_blocks(k_ref[pl.ds(ks, BLK), :].astype(BF16), ch)
                vs_ = _lane_blocks(v_ref[pl.ds(ks, BLK), :].astype(BF16), ch)
                zs = [_dot_nt(q01[p], ks_[p]) for p in range(ch)]
                das = [_dot_nt(do01[p], vs_[p]) for p in range(ch)]
                logits = [_sb_logits(z, causal, masked) for z in zs]
                sums = [_suffix_sums(lg[0], suffix) for lg in logits]
                a_s, e_s = [], []
                for p in range(ch):
                    a = jnp.exp(logits[p][1] + state[3 * p] + sums[p][0])
                    if masked:
                        a = jnp.where(causal, a, 0.0)
                    a_s.append(a)
                    e_s.append(a * das[p])
                e_sums = [_suffix_sums(e, suffix) for e in e_s]
                out, dks, dvs = [], [], []
                for p in range(ch):
                    carry, rcarry, dq = state[3 * p:3 * p + 3]
                    e = e_s[p]
                    before = tot[p] - (rcarry + e_sums[p][0] + e)
                    beta = jnp.exp(logits[p][1])
                    dz = e * (1.0 - beta) - beta * before
                    if masked:
                        dz = jnp.where(causal, dz, 0.0)
                    dz_b = dz.astype(BF16)
                    dks.append(_dot_tn(dz_b, q01[p]))
                    dvs.append(_dot_tn(a_s[p].astype(BF16), do01[p]))
                    out += [carry + sums[p][1], rcarry + e_sums[p][1], dq + _dot(dz_b, ks_[p])]
                dk_acc[pl.ds(ks, BLK), :] += jnp.concatenate(dks, axis=1)
                dv_acc[pl.ds(ks, BLK), :] += jnp.concatenate(dvs, axis=1)
                return tuple(out)

            state = (jnp.zeros((2 * BLK, BLK), F32),) * (3 * ch)
            state = tile(i, state, True)
            state = _sb_walk_back(i, state, 3, tile)
            dq = [jnp.where(head0, state[3 * p + 2][:BLK], state[3 * p + 2][BLK:]) for p in range(ch)]
            dq_ref[pl.ds(qs, BLK), :] = (jnp.concatenate(dq, axis=1) * scale).astype(dq_ref.dtype)
            return 0

        lax.fori_loop(0, nq, q_block, 0)
        dk_ref[...] = dk_acc[...].astype(dk_ref.dtype)
        dv_ref[...] = dv_acc[...].astype(dv_ref.dtype)

    def body(*refs):
        step = pl.program_id(0) * n_steps + pl.program_id(1)
        n_c, n_o = len(carry.arrays), len(carry.out_shapes)
        own = refs[:5] + refs[5 + n_c:8 + n_c] + refs[8 + n_c + n_o:10 + n_c + n_o]
        carry.run(refs, 5, 3, step, b_sz * n_steps, lambda: compute(*own))

    blk = lambda off: pl.BlockSpec((None, s_len, ch * LANES), lambda b, p: (b, 0, off + p))
    once = lambda off: pl.BlockSpec((None, s_len, ch * LANES), lambda b, p: (b, 0, off + p),
                                    pipeline_mode=pl.Buffered(1))
    out_sd = jax.ShapeDtypeStruct((b_sz, s_len, SB_WIDTH), BF16)
    c_in, c_out, c_shapes, c_alias, c_sems = carry.call_args(5, 3)
    res = pl.pallas_call(
        body, name="sb_bwd", grid=(b_sz, n_steps),
        in_specs=[once(0), once(n_steps), once(2 * n_steps), once(0), once(0)] + c_in,
        out_specs=[blk(0), blk(0), blk(0)] + c_out, out_shape=[out_sd, out_sd, out_sd] + c_shapes,
        input_output_aliases=c_alias,
        scratch_shapes=[pltpu.VMEM((s_len, ch * LANES), F32), pltpu.VMEM((s_len, ch * LANES), F32)] + c_sems,
        compiler_params=pltpu.CompilerParams(dimension_semantics=("arbitrary", "arbitrary"),
                                             vmem_limit_bytes=VMEM_CAP),
    )(qkv, qkv, qkv, o_sb, do_sb, *carry.arrays)
    return res[:3], res[3:]


def _dil_consts(group, pair_idx, dilation):
    lane = lax.broadcasted_iota(jnp.int32, (BLK, LANES), 1)
    head0 = lane < HEAD_DIM
    row = lax.broadcasted_iota(jnp.int32, (2 * BLK, BLK), 0)
    qa = row % BLK
    kb = lax.broadcasted_iota(jnp.int32, (2 * BLK, BLK), 1)
    head = (group * DIL_HEADS_PER_GROUP + 2 * pair_idx + row // BLK).astype(F32)
    slope = jnp.exp((-ALIBI_MAX_BIAS * math.log(2.0) / DIL_HEADS) * (head + 1.0))
    valid_cur = kb <= qa
    valid_prev = kb >= qa
    bias_cur = -slope * ((qa - kb) * dilation).astype(F32)
    bias_prev = -slope * ((BLK + qa - kb) * dilation).astype(F32)
    return head0, valid_cur, valid_prev, bias_cur, bias_prev


def _dil_units(s_len, dilation):
    nb = s_len // dilation // BLK
    return [(r, n) for r in range(dilation) for n in range(nb)]


def _dil_rows(n, r, dilation):
    if dilation == 1:
        return pl.ds(n * BLK, BLK)
    return pl.ds(n * BLK * dilation + r, BLK, stride=dilation)


def _dil_scores(q01, k, bias, valid):
    s = _dot_nt(q01, k) * (1.0 / math.sqrt(HEAD_DIM)) + bias
    return jnp.where(valid, s, NEG)


def _dil_fwd(qkv, b_sz, s_len, carry):
    n_pairs = DIL_OUT_WIDTH // LANES
    q_off = 3 * SB_WIDTH // LANES
    per_kind = DIL_WIDTH // LANES

    def compute(pair_idx, qkv_refs, o_ref, lse_ref, m_s, l_s):
        m_s[...] = jnp.full(m_s.shape, NEG, F32)
        l_s[...] = jnp.zeros(l_s.shape, F32)
        o_ref[...] = jnp.zeros(o_ref.shape, F32)
        for g, (_, dilation) in enumerate(DIL_PAIRS):
            q_ref, k_ref, v_ref = qkv_refs[3 * g:3 * g + 3]
            head0, valid_cur, valid_prev, bias_cur, bias_prev = _dil_consts(g, pair_idx, dilation)
            units = _dil_units(s_len, dilation)
            for u0 in range(0, len(units), DIL_CHAINS):
                group = units[u0:u0 + DIL_CHAINS]
                rows_of = [_dil_rows(n, r, dilation) for r, n in group]
                scores, values = [], []
                for (r, n), rows in zip(group, rows_of):
                    q01 = _stack_heads(q_ref[rows, :].astype(BF16), head0)
                    sc = [_dil_scores(q01, k_ref[rows, :].astype(BF16), bias_cur, valid_cur)]
                    vals = [_stack_heads(v_ref[rows, :].astype(BF16), head0)]
                    if n > 0:
                        prev = _dil_rows(n - 1, r, dilation)
                        sc.append(_dil_scores(q01, k_ref[prev, :].astype(BF16), bias_prev, valid_prev))
                        vals.append(_stack_heads(v_ref[prev, :].astype(BF16), head0))
                    scores.append(sc)
                    values.append(vals)
                stats = []
                for sc, rows in zip(scores, rows_of):
                    m_blk = functools.reduce(jnp.maximum, [jnp.max(x, axis=-1, keepdims=True) for x in sc])
                    m_old = jnp.concatenate([m_s.at[0][rows, :], m_s.at[1][rows, :]], axis=0)
                    l_old = jnp.concatenate([l_s.at[0][rows, :], l_s.at[1][rows, :]], axis=0)
                    m_new = jnp.maximum(m_old, m_blk)
                    probs = [jnp.exp(x - m_new) for x in sc]
                    l_blk = functools.reduce(jnp.add, [jnp.sum(p, axis=-1, keepdims=True) for p in probs])
                    alpha = jnp.exp(m_old - m_new)
                    stats.append((m_new, alpha * l_old + l_blk, alpha, probs))
                for (m_new, l_new, alpha, probs), vals, rows in zip(stats, values, rows_of):
                    alpha_tok = jnp.where(head0, alpha[:BLK], alpha[BLK:])
                    p_cat = jnp.concatenate(
                        [h for p in probs for h in (p[:BLK].astype(BF16), p[BLK:].astype(BF16))], axis=1)
                    o_ref[rows, :] = alpha_tok * o_ref[rows, :] + _dot(p_cat, jnp.concatenate(vals, axis=0))
                    m_s.at[0][rows, :] = m_new[:BLK]
                    m_s.at[1][rows, :] = m_new[BLK:]
                    l_s.at[0][rows, :] = l_new[:BLK]
                    l_s.at[1][rows, :] = l_new[BLK:]
        lane = lax.broadcasted_iota(jnp.int32, (BLK, LANES), 1)
        for c in range(s_len // BLK):
            rows = pl.ds(c * BLK, BLK)
            l0, l1 = l_s.at[0][rows, :], l_s.at[1][rows, :]
            o_ref[rows, :] = o_ref[rows, :] / jnp.where(lane < HEAD_DIM, l0, l1)
            lse_ref.at[0][rows, :] = m_s.at[0][rows, :] + jnp.log(l0)
            lse_ref.at[1][rows, :] = m_s.at[1][rows, :] + jnp.log(l1)

    def body(*refs):
        pair_idx = pl.program_id(1)
        step = pl.program_id(0) * n_pairs + pair_idx
        n_c, n_o = len(carry.arrays), len(carry.out_shapes)
        o_ref, lse_ref = refs[9 + n_c:11 + n_c]
        m_s, l_s = refs[11 + n_c + n_o:13 + n_c + n_o]
        carry.run(refs, 9, 2, step, b_sz * n_pairs, lambda: compute(pair_idx, refs[:9], o_ref, lse_ref, m_s, l_s))

    in_specs = []
    for g in range(len(DIL_PAIRS)):
        for kind in range(3):
            off = q_off + kind * per_kind + g * n_pairs
            in_specs.append(pl.BlockSpec((None, s_len, LANES), lambda b, p, off=off: (b, 0, off + p)))
    c_in, c_out, c_shapes, c_alias, c_sems = carry.call_args(9, 2)
    res = pl.pallas_call(
        body, name="dil_fwd", grid=(b_sz, n_pairs),
        in_specs=in_specs + c_in,
        out_specs=[pl.BlockSpec((None, s_len, LANES), lambda b, p: (b, 0, p)),
                   pl.BlockSpec((None, None, 2, s_len, LANES), lambda b, p: (b, p, 0, 0, 0))] + c_out,
        out_shape=[jax.ShapeDtypeStruct((b_sz, s_len, DIL_OUT_WIDTH), F32),
                   jax.ShapeDtypeStruct((b_sz, n_pairs, 2, s_len, LANES), F32)] + c_shapes,
        input_output_aliases=c_alias,
        scratch_shapes=[pltpu.VMEM((2, s_len, LANES), F32), pltpu.VMEM((2, s_len, LANES), F32)] + c_sems,
        compiler_params=pltpu.CompilerParams(dimension_semantics=("arbitrary", "arbitrary"),
                                             vmem_limit_bytes=VMEM_CAP),
    )(*([qkv] * 9), *carry.arrays)
    return res[0], res[1], res[2:]


def _dil_bwd(qkv, o_dl, lse, do_dl, b_sz, s_len, carry):
    n_pairs = DIL_OUT_WIDTH // LANES
    n_groups = len(DIL_PAIRS)
    q_off = 3 * SB_WIDTH // LANES
    per_kind = DIL_WIDTH // LANES

    def compute(pair_idx, group, q_ref, k_ref, v_ref, o_ref, lse_ref, do_ref, dq_ref, dk_ref, dv_ref, d_s, dq_s, dk_s, dv_s):
        lrow = lax.broadcasted_iota(jnp.int32, (LANES, LANES), 0)
        ones_h0 = jnp.where(lrow < HEAD_DIM, 1.0, 0.0).astype(BF16)
        ones_h1 = jnp.where(lrow >= HEAD_DIM, 1.0, 0.0).astype(BF16)
        for c in range(s_len // BLK):
            rows = pl.ds(c * BLK, BLK)
            dd_hi, dd_lo = _split2(do_ref[rows, :] * o_ref[rows, :])
            d_s.at[0][rows, :] = _dot(dd_hi, ones_h0) + _dot(dd_lo, ones_h0)
            d_s.at[1][rows, :] = _dot(dd_hi, ones_h1) + _dot(dd_lo, ones_h1)
        dk_s[...] = jnp.zeros(dk_s.shape, F32)
        dv_s[...] = jnp.zeros(dv_s.shape, F32)

        def one_group(g, dilation):
            head0, valid_cur, valid_prev, bias_cur, bias_prev = _dil_consts(g, pair_idx, dilation)
            units = _dil_units(s_len, dilation)
            scale = 1.0 / math.sqrt(HEAD_DIM)
            for u0 in range(0, len(units), DIL_CHAINS):
                chunk = units[u0:u0 + DIL_CHAINS]
                loaded = []
                for r, n in chunk:
                    rows = _dil_rows(n, r, dilation)
                    q01 = _stack_heads(q_ref[rows, :].astype(BF16), head0)
                    do01 = _stack_heads(do_ref[rows, :].astype(BF16), head0)
                    lse01 = jnp.concatenate([lse_ref.at[0][rows, :], lse_ref.at[1][rows, :]], axis=0)
                    d01 = jnp.concatenate([d_s.at[0][rows, :], d_s.at[1][rows, :]], axis=0)
                    blocks = [(rows, bias_cur, valid_cur)]
                    if n > 0:
                        blocks.append((_dil_rows(n - 1, r, dilation), bias_prev, valid_prev))
                    parts = []
                    for krows, bias, valid in blocks:
                        k = k_ref[krows, :].astype(BF16)
                        v = v_ref[krows, :].astype(BF16)
                        parts.append((krows, k, _dil_scores(q01, k, bias, valid), _dot_nt(do01, v)))
                    loaded.append((rows, q01, do01, lse01, d01, parts))
                grads = []
                for rows, q01, do01, lse01, d01, parts in loaded:
                    for krows, k, sc, dp in parts:
                        p = jnp.exp(sc - lse01)
                        grads.append((p.astype(BF16), (p * (dp - d01) * scale).astype(BF16)))
                it = iter(grads)
                updates = []
                for rows, q01, do01, lse01, d01, parts in loaded:
                    dq = jnp.zeros((2 * BLK, LANES), F32)
                    for krows, k, sc, dp in parts:
                        p_b, ds = next(it)
                        dq = dq + _dot(ds, k)
                        updates.append((krows, _dot_tn(ds, q01), _dot_tn(p_b, do01)))
                    dq_s[rows, :] = jnp.where(head0, dq[:BLK], dq[BLK:])
                for krows, dk, dv in updates:
                    dk_s[krows, :] = dk_s[krows, :] + dk
                    dv_s[krows, :] = dv_s[krows, :] + dv

        for g, (_, dilation) in enumerate(DIL_PAIRS):
            pl.when(group == g)(functools.partial(one_group, g, dilation))
        dq_ref[...] = dq_s[...].astype(dq_ref.dtype)
        dk_ref[...] = dk_s[...].astype(dk_ref.dtype)
        dv_ref[...] = dv_s[...].astype(dv_ref.dtype)

    def body(*refs):
        pair_idx, group = pl.program_id(1), pl.program_id(2)
        step = (pl.program_id(0) * n_pairs + pair_idx) * n_groups + group
        n_c, n_o = len(carry.arrays), len(carry.out_shapes)
        own = refs[:6] + refs[6 + n_c:9 + n_c] + refs[9 + n_c + n_o:13 + n_c + n_o]
        carry.run(refs, 6, 3, step, b_sz * n_pairs * n_groups, lambda: compute(pair_idx, group, *own))

    def qkv_spec(kind):
        return pl.BlockSpec((None, s_len, LANES),
                            lambda b, p, g: (b, 0, q_off + kind * per_kind + g * n_pairs + p))

    tok_spec = pl.BlockSpec((None, s_len, LANES), lambda b, p, g: (b, 0, p))
    out_spec = pl.BlockSpec((None, s_len, LANES), lambda b, p, g: (b, 0, g * n_pairs + p))
    out_sd = jax.ShapeDtypeStruct((b_sz, s_len, DIL_WIDTH), BF16)
    c_in, c_out, c_shapes, c_alias, c_sems = carry.call_args(6, 3)
    res = pl.pallas_call(
        body, name="dil_bwd", grid=(b_sz, n_pairs, n_groups),
        in_specs=[qkv_spec(0), qkv_spec(1), qkv_spec(2), tok_spec,
                  pl.BlockSpec((None, None, 2, s_len, LANES), lambda b, p, g: (b, p, 0, 0, 0)), tok_spec] + c_in,
        out_specs=[out_spec, out_spec, out_spec] + c_out,
        out_shape=[out_sd, out_sd, out_sd] + c_shapes,
        input_output_aliases=c_alias,
        scratch_shapes=[pltpu.VMEM((2, s_len, LANES), F32)] + [pltpu.VMEM((s_len, LANES), F32)] * 3 + c_sems,
        compiler_params=pltpu.CompilerParams(dimension_semantics=("arbitrary", "arbitrary", "arbitrary"),
                                             vmem_limit_bytes=VMEM_CAP),
    )(qkv, qkv, qkv, o_dl, lse, do_dl, *carry.arrays)
    return res[:3], res[3:]


def _mesh_pos():
    return lax.axis_index("x"), lax.axis_index("y"), lax.axis_index("c")


def _other_chips(x, y):
    return [(1 - x, y), (x, 1 - y), (1 - x, 1 - y)]


def _hbm_specs(n):
    return [pl.BlockSpec(memory_space=pl.ANY)] * n


SWAPPED = ("w_ffn_in",)


def _slot(x, y, swapped):
    return 2 * y + x if swapped else 2 * x + y


def _cast_to_slab(w, name):
    rows, cols = w.shape
    mine = jnp.reshape(_slot(lax.axis_index("x"), lax.axis_index("y"), False), (1,)).astype(jnp.int32)

    def body(idx_ref, w_ref, o_ref):
        o_ref[...] = w_ref[...].astype(BF16)

    return pl.pallas_call(
        body, name=name,
        grid_spec=pltpu.PrefetchScalarGridSpec(
            num_scalar_prefetch=1, grid=(1,),
            in_specs=[pl.BlockSpec((rows, cols), lambda i, idx: (0, 0))],
            out_specs=pl.BlockSpec((None, rows, cols), lambda i, idx: (idx[0], 0, 0))),
        out_shape=_hbm_array((N_CHIPS, rows, cols), BF16),
        compiler_params=pltpu.CompilerParams(vmem_limit_bytes=_vmem_limit(rows * cols * 6)),
    )(mine, w)


def _gather_issue(slabs, send_sems, recv_sems, swapped):
    x, y, c = _mesh_pos()
    for k, slab in enumerate(slabs):
        half = slab.shape[1] // 2
        rows = slab.at[_slot(x, y, swapped[k]), pl.ds(c * half, half), :]
        for r, (px, py) in enumerate(_other_chips(x, y)):
            pltpu.make_async_remote_copy(
                src_ref=rows, dst_ref=rows, send_sem=send_sems.at[6 * k + r], recv_sem=recv_sems.at[6 * k + r],
                device_id=(px, py, c), device_id_type=MESH).start()


def _gather_complete(slabs, send_sems, recv_sems, swapped):
    x, y, c = _mesh_pos()
    chips = _other_chips(x, y)

    def copy(k, sem, block, rows, to):
        ref = slabs[k].at[block, rows, :]
        return pltpu.make_async_remote_copy(
            src_ref=ref, dst_ref=ref, send_sem=send_sems.at[sem], recv_sem=recv_sems.at[sem],
            device_id=to, device_id_type=MESH)

    for k, slab in enumerate(slabs):
        half = slab.shape[1] // 2
        for r, (px, py) in enumerate(chips):
            theirs = _slot(px, py, swapped[k])
            copy(k, 6 * k + r, theirs, pl.ds(c * half, half), (px, py, c)).wait_recv()
            copy(k, 6 * k + 3 + r, theirs, pl.ds(c * half, half), (x, y, 1 - c)).start()
    for k, slab in enumerate(slabs):
        half = slab.shape[1] // 2
        for r, (px, py) in enumerate(chips):
            copy(k, 6 * k + 3 + r, _slot(px, py, swapped[k]), pl.ds((1 - c) * half, half), (x, y, 1 - c)).wait_recv()
    for k, slab in enumerate(slabs):
        half = slab.shape[1] // 2
        for r, (px, py) in enumerate(chips):
            copy(k, 6 * k + r, _slot(x, y, swapped[k]), pl.ds(c * half, half), (px, py, c)).wait_send()
            copy(k, 6 * k + 3 + r, _slot(px, py, swapped[k]), pl.ds(c * half, half), (x, y, 1 - c)).wait_send()


def _gather_sems(n):
    return [pltpu.SemaphoreType.DMA((6 * n,)), pltpu.SemaphoreType.DMA((6 * n,))]


def _gather_carry(slabs, names):
    swapped = [k in SWAPPED for k in names]
    return _Carry(slabs, [_hbm_array(a.shape, a.dtype) for a in slabs], True, _gather_sems(len(slabs)),
                  lambda ins, outs, sems: _gather_issue(outs, *sems, swapped),
                  lambda ins, outs, sems: _gather_complete(outs, *sems, swapped))


def _cast_carry(shards, names):
    swapped = [k in SWAPPED for k in names]

    def start(ins, outs, sems):
        x, y, _ = _mesh_pos()
        for w, slab, sw in zip(ins, outs, swapped):
            def cast(f32_buf, bf16_buf, sem, w=w, slab=slab, sw=sw):
                load = pltpu.make_async_copy(w, f32_buf, sem)
                load.start()
                load.wait()
                bf16_buf[...] = f32_buf[...].astype(BF16)
                store = pltpu.make_async_copy(bf16_buf, slab.at[_slot(x, y, sw)], sem)
                store.start()
                store.wait()

            pl.run_scoped(cast, pltpu.VMEM(w.shape, F32), pltpu.VMEM(w.shape, BF16), pltpu.SemaphoreType.DMA)

    return _Carry(shards, [_hbm_array((N_CHIPS,) + w.shape, BF16) for w in shards], False, [], start,
                  lambda ins, outs, sems: None)


def _pair_copies(ins, outs, send_sems, recv_sems):
    x, y, c = _mesh_pos()
    copies = []
    for k, g in enumerate(ins):
        half = g.shape[1] // 2
        copies.append(pltpu.make_async_remote_copy(
            src_ref=g.at[:, pl.ds((1 - c) * half, half), :], dst_ref=outs[k],
            send_sem=send_sems.at[k], recv_sem=recv_sems.at[k],
            device_id=(x, y, 1 - c), device_id_type=MESH))
    return copies


def _pair_carry(grads):
    n = len(grads)

    def start(ins, outs, sems):
        for cp in _pair_copies(ins, outs, *sems):
            cp.start()

    def finish(ins, outs, sems):
        for cp in _pair_copies(ins, outs, *sems):
            cp.wait()

    return _Carry(grads, [jax.ShapeDtypeStruct((N_CHIPS, g.shape[1] // 2, g.shape[2]), g.dtype) for g in grads], False,
                  [pltpu.SemaphoreType.DMA((n,)), pltpu.SemaphoreType.DMA((n,))], start, finish)


def _pair_exchange(grads, tag):
    carry = _pair_carry(grads)
    n = len(grads)

    def body(*refs):
        carry.start(refs[:n], refs[n:2 * n], refs[2 * n:])
        carry.finish(refs[:n], refs[n:2 * n], refs[2 * n:])

    return pl.pallas_call(
        body, name="grad_pair_exchange_" + tag, in_specs=_hbm_specs(n), out_specs=_hbm_specs(n),
        out_shape=carry.out_shapes, scratch_shapes=carry.sems,
    )(*grads)


def _pair_sum(grad, other, name, swapped):
    _, rows, cols = grad.shape
    half = rows // 2
    x, y, c = _mesh_pos()
    idx = jnp.stack([c, _slot(x, y, swapped)]).astype(jnp.int32)

    def body(idx_ref, g_ref, p_ref, own_ref, sb_ref):
        s = g_ref[...] + p_ref[...].astype(F32)
        sb_ref[...] = s.astype(BF16)

        @pl.when(pl.program_id(0) == idx_ref[1])
        def _():
            own_ref[...] = s

    blk = pl.BlockSpec((None, half, cols), lambda p, idx: (p, 0, 0))
    return pl.pallas_call(
        body, name=name,
        grid_spec=pltpu.PrefetchScalarGridSpec(
            num_scalar_prefetch=1, grid=(N_CHIPS,),
            in_specs=[pl.BlockSpec((None, half, cols), lambda p, idx: (p, idx[0], 0)), blk],
            out_specs=[pl.BlockSpec((half, cols), lambda p, idx: (0, 0)), blk]),
        out_shape=[jax.ShapeDtypeStruct((half, cols), F32), jax.ShapeDtypeStruct((N_CHIPS, half, cols), BF16)],
        compiler_params=pltpu.CompilerParams(dimension_semantics=("arbitrary",),
                                             vmem_limit_bytes=_vmem_limit(4 * half * cols * 4)),
    )(idx, grad, other)


def _chip_copies(sums_bf16, lands, send_sems, recv_sems, swapped):
    x, y, c = _mesh_pos()
    return [pltpu.make_async_remote_copy(
        src_ref=sums_bf16[k].at[_slot(px, py, swapped[k])], dst_ref=lands[k].at[r],
        send_sem=send_sems.at[3 * k + r], recv_sem=recv_sems.at[3 * k + r],
        device_id=(px, py, c), device_id_type=MESH)
        for k in range(len(sums_bf16)) for r, (px, py) in enumerate(_other_chips(x, y))]


def _chip_carry(sums_bf16, names):
    swapped = [k in SWAPPED for k in names]

    def start(ins, outs, sems):
        for cp in _chip_copies(ins, outs, *sems, swapped):
            cp.start()

    def finish(ins, outs, sems):
        for cp in _chip_copies(ins, outs, *sems, swapped):
            cp.wait()

    return _Carry(sums_bf16, _chip_landing(sums_bf16), False, _chip_sems(len(sums_bf16)), start, finish)


def _chip_sems(n):
    return [pltpu.SemaphoreType.DMA((3 * n,)), pltpu.SemaphoreType.DMA((3 * n,))]


def _chip_landing(sums_bf16):
    return [jax.ShapeDtypeStruct((N_CHIPS - 1,) + s.shape[1:], BF16) for s in sums_bf16]


def _chip_sum(own, landed, name):
    rows, cols = own.shape
    core = jnp.reshape(lax.axis_index("c"), (1,)).astype(jnp.int32)

    def body(core_ref, o_ref, l_ref, out_ref):
        out_ref[...] = ((o_ref[...] + l_ref[0].astype(F32)) + l_ref[1].astype(F32)) + l_ref[2].astype(F32)

    return pl.pallas_call(
        body, name=name,
        grid_spec=pltpu.PrefetchScalarGridSpec(
            num_scalar_prefetch=1, grid=(1,),
            in_specs=[pl.BlockSpec((rows, cols), lambda i, core_ref: (0, 0)),
                      pl.BlockSpec((N_CHIPS - 1, rows, cols), lambda i, core_ref: (0, 0, 0))],
            out_specs=pl.BlockSpec((rows, cols), lambda i, core_ref: (core_ref[0], 0))),
        out_shape=jax.ShapeDtypeStruct((2 * rows, cols), F32),
        compiler_params=pltpu.CompilerParams(vmem_limit_bytes=_vmem_limit(3 * rows * cols * 4)),
    )(core, own, landed)


def _halves_carry(fulls):
    n = len(fulls)

    def copies(outs, send_sems, recv_sems, own):
        x, y, c = _mesh_pos()
        res = []
        for k, out in enumerate(outs):
            half = out.shape[0] // 2
            rows = out.at[pl.ds((c if own else 1 - c) * half, half), :]
            res.append(pltpu.make_async_remote_copy(
                src_ref=rows, dst_ref=rows, send_sem=send_sems.at[k], recv_sem=recv_sems.at[k],
                device_id=(x, y, 1 - c), device_id_type=MESH))
        return res

    def start(ins, outs, sems):
        for cp in copies(outs, *sems, True):
            cp.start()

    def finish(ins, outs, sems):
        for cp in copies(outs, *sems, False):
            cp.wait_recv()
        for cp in copies(outs, *sems, True):
            cp.wait_send()

    return _Carry(fulls, [jax.ShapeDtypeStruct(f.shape, F32) for f in fulls], True,
                  [pltpu.SemaphoreType.DMA((n,)), pltpu.SemaphoreType.DMA((n,))], start, finish)


def _final_exchange(fulls, v):
    n = len(fulls)
    rows, cols = v.shape
    n_dev = 8

    def body(*refs):
        v_ref, out_ref = refs[0], refs[1 + 2 * n]
        outs = refs[1 + n:1 + 2 * n]
        buf, v_send, v_recv, h_send, h_recv = refs[2 + 2 * n:]
        x, y, c = _mesh_pos()
        me = 4 * x + 2 * y + c
        buf[me] = v_ref[...]
        peers = [(1 - x if r & 4 else x, 1 - y if r & 2 else y, 1 - c if r & 1 else c) for r in range(1, n_dev)]
        copies = []
        for r, peer in enumerate(peers):
            copies.append(pltpu.make_async_remote_copy(
                src_ref=v_ref, dst_ref=buf.at[me], send_sem=v_send.at[r], recv_sem=v_recv.at[r],
                device_id=peer, device_id_type=MESH))
        for k in range(n):
            half = fulls[k].shape[0] // 2
            mine = outs[k].at[pl.ds(c * half, half), :]
            copies.append(pltpu.make_async_remote_copy(
                src_ref=mine, dst_ref=mine, send_sem=h_send.at[k], recv_sem=h_recv.at[k],
                device_id=(x, y, 1 - c), device_id_type=MESH))
        for cp in copies:
            cp.start()
        for r, (px, py, pc) in enumerate(peers):
            pltpu.make_async_remote_copy(
                src_ref=v_ref, dst_ref=buf.at[4 * px + 2 * py + pc], send_sem=v_send.at[r], recv_sem=v_recv.at[r],
                device_id=(px, py, pc), device_id_type=MESH).wait_recv()
        for k in range(n):
            half = fulls[k].shape[0] // 2
            theirs = outs[k].at[pl.ds((1 - c) * half, half), :]
            pltpu.make_async_remote_copy(
                src_ref=theirs, dst_ref=theirs, send_sem=h_send.at[k], recv_sem=h_recv.at[k],
                device_id=(x, y, 1 - c), device_id_type=MESH).wait_recv()
        for cp in copies:
            cp.wait_send()
        acc = buf[0]
        for d in range(1, n_dev):
            acc = acc + buf[d]
        out_ref[...] = acc
        out_ref[3:4, :] = jnp.broadcast_to(jnp.sum(acc[3:4, :], axis=1, keepdims=True), (1, cols))

    vm = pl.BlockSpec(memory_space=pltpu.VMEM)
    res = pl.pallas_call(
        body, name="final_exchange",
        in_specs=[vm] + _hbm_specs(n), out_specs=_hbm_specs(n) + [vm],
        out_shape=[jax.ShapeDtypeStruct(f.shape, F32) for f in fulls] + [jax.ShapeDtypeStruct((rows, cols), F32)],
        input_output_aliases={1 + k: k for k in range(n)},
        scratch_shapes=[pltpu.VMEM((n_dev, rows, cols), F32),
                        pltpu.SemaphoreType.DMA((n_dev - 1,)), pltpu.SemaphoreType.DMA((n_dev - 1,)),
                        pltpu.SemaphoreType.DMA((n,)), pltpu.SemaphoreType.DMA((n,))],
    )(v, *fulls)
    return res[:n], res[n]


def _adamw_math(w, g, m, v):
    m = ADAM_B1 * m + (1.0 - ADAM_B1) * g
    v = ADAM_B2 * v + (1.0 - ADAM_B2) * (g * g)
    m_hat = m / (1.0 - ADAM_B1 ** ADAM_STEP)
    v_hat = v / (1.0 - ADAM_B2 ** ADAM_STEP)
    delta = -ADAM_LR * (m_hat / (jnp.sqrt(v_hat) + ADAM_EPS) + ADAM_WD * w)
    return delta, m, v


def _adamw(w, g, m, v, name):
    rows, cols = w.shape
    tm = rows // 2 if (rows // 2) % 8 == 0 else rows
    return _rowwise(_adamw_math, [w, g, m, v], [], [(cols, F32)] * 3, [], tm=tm, name=name)


def _unshard_cols(gathered):
    n, r, c = gathered.shape
    return jnp.transpose(gathered, (1, 0, 2)).reshape(r, n * c)


def _shard_cols(full):
    r, nc = full.shape
    return jnp.transpose(full.reshape(r, N_CHIPS, nc // N_CHIPS), (1, 0, 2))


LATE = ["w_sb_up", "w_dil_up", "w_out", "w_ffn_in", "w_ffn_out"]


def _late_weights(slabs, d_model, d_ff):
    g = dict(zip(LATE, slabs))
    return (_unshard_cols(g["w_sb_up"]), _unshard_cols(g["w_dil_up"]), g["w_out"].reshape(d_model, d_model),
            _unshard_cols(g["w_ffn_in"]), g["w_ffn_out"].reshape(d_ff, d_model))


ROW_SHARDED = ("w_in", "w_out", "w_ffn_in", "w_ffn_out")


def _chip_major(grads):
    out = []
    for k, g in grads.items():
        if k in ROW_SHARDED:
            out.append(g.reshape(N_CHIPS, g.shape[0] // N_CHIPS, g.shape[1]))
        else:
            out.append(_shard_cols(g))
    return out


def _pair_sums(full, others, names):
    return [_pair_sum(g, o, "grad_pair_sum_" + k, k in SWAPPED) for g, o, k in zip(full, others, names)]


def _chip_sums(pair, landed, names):
    return {k: _chip_sum(p[0], l, "grad_chip_sum_" + k) for p, l, k in zip(pair, landed, names)}


def _fwd_bwd(x, loss_target, g_mix, g_ffn, g_fin, slab_in, late_shards):
    b_sz, s_len, d_model = x.shape
    t = b_sz * s_len
    d_ff = late_shards[-1].shape[0] * N_CHIPS
    x2d = x.reshape(t, d_model)
    tgt2d = loss_target.reshape(t, d_model)

    u, (slab_in, *late_slabs) = _rowwise(
        lambda xv, g: (_rms_stats(xv)[0] * g,), [_in_hbm(x2d)], [g_mix], [(d_model, BF16)], [], tm=512, name="norm_mix",
        carry=_gather_carry([slab_in], ["w_in"]) + _cast_carry(late_shards, LATE), out_type=_hbm_array)
    u, wt_in = _in_hbm(u), _in_hbm(slab_in.reshape(-1, d_model))
    qkv, (slab_ffn_out,) = _mm(u, wt_in, tb=True, b_cols=(0, QKV_WIDTH), tm=2048, tn=768, tk=d_model, name="proj_qkv",
                               carry=_gather_carry(late_slabs[4:], LATE[4:]))
    gates = _mm(u, wt_in, tb=True, b_cols=(QKV_WIDTH, 2 * d_model), out_dtype=BF16, tm=t, tn=256, tk=d_model,
                name="proj_gates")
    qkv3 = qkv.reshape(b_sz, s_len, QKV_WIDTH)
    o_sb, (slab_ffn_in,) = _sb_fwd(qkv3, b_sz, s_len, _gather_carry(late_slabs[3:4], LATE[3:4]))
    o_dl, lse, small_slabs = _dil_fwd(qkv3, b_sz, s_len, _gather_carry(late_slabs[:3], LATE[:3]))
    wf_sb_up, wf_dil_up, wf_out, wf_ffn_in, wf_ffn_out = _late_weights(
        list(small_slabs) + [slab_ffn_in, slab_ffn_out], d_model, d_ff)
    o_sb2, o_dl2 = o_sb.reshape(t, SB_WIDTH), o_dl.reshape(t, DIL_OUT_WIDTH)
    y_sb = _mm(o_sb2, wf_sb_up, out_dtype=BF16, tm=1024, tn=1024, tk=SB_WIDTH, name="sb_up", out_type=_hbm_array)
    y_dl = _mm(o_dl2, wf_dil_up, out_dtype=BF16, tm=1024, tn=1024, tk=DIL_OUT_WIDTH, name="dil_up", out_type=_hbm_array)

    def merge_fn(gt, ys, yd):
        return (_sigmoid(gt[:, :d_model]) * ys + _sigmoid(gt[:, d_model:]) * yd,)

    (merged,) = _rowwise(merge_fn, [gates, y_sb, y_dl], [], [(d_model, BF16)], [], tm=512, name="merge")
    x1 = _mm(merged, wf_out, add=x2d, tm=512, tn=1024, tk=d_model, name="mix_out")
    (u2,) = _rowwise(lambda xv, g: (_rms_stats(xv)[0] * g,), [_in_hbm(x1)], [g_ffn], [(d_model, BF16)], [], tm=512, name="norm_ffn",
                     out_type=_hbm_array)
    u2 = _in_hbm(u2)
    half_ff = d_ff // 2

    def act_fn(hv):
        gate = hv[:, :half_ff]
        return hv, gate * _sigmoid(gate) * hv[:, half_ff:]

    h, act = _mm(u2, wf_ffn_in, tm=512, tn=d_ff, tk=d_model, name="ffn_in",
                 epilogue=(act_fn, [], [], [(d_ff, BF16), (half_ff, BF16)], []))
    def head_fn(xv, tg, g):
        xhat, r = _rms_stats(xv)
        err = xhat * g - tg
        dy = err * (1.0 / d_model)
        dx, dg_rows = _rms_bwd(dy, xhat, r, g)
        loss_lanes = (0.5 / d_model) * jnp.sum(err * err, axis=0, keepdims=True)
        return dx, dx, jnp.sum(dg_rows, axis=0, keepdims=True), loss_lanes

    dx2, dx2_b, dg_fin, loss_lanes = _mm(
        act, wf_ffn_out, add=x1, tm=512, tn=1024, tk=d_ff, name="ffn_out",
        epilogue=(head_fn, [tgt2d], [g_fin], [(d_model, F32), (d_model, BF16)], [(1, d_model), (1, d_model)]))

    def dact_fn(da, hv):
        gate, up = hv[:, :half_ff], hv[:, half_ff:]
        sg = _sigmoid(gate)
        dgate = da * up * (sg * (1.0 + gate * (1.0 - sg)))
        return (jnp.concatenate([dgate, da * (gate * sg)], axis=1),)

    dx2_b = _in_hbm(dx2_b)
    (dh,) = _mm(dx2_b, wf_ffn_out, tb=True, tm=512, tn=half_ff, tk=d_model, name="ffn_out_dx",
                epilogue=(dact_fn, [h], [], [(d_ff, BF16)], []))
    gw_ffn_out = _mm(act, dx2_b, ta=True, tm=256, tn=d_model, tk=t, name="ffn_out_dw")
    def norm_bwd_fn(du_, dres, xv, g):
        xhat, r = _rms_stats(xv)
        dx, dg_rows = _rms_bwd(du_, xhat, r, g)
        return dres + dx, jnp.sum(dg_rows, axis=0, keepdims=True)

    def norm_bwd_twice(*args):
        dx, dg = norm_bwd_fn(*args)
        return dx, dx, dg

    dx1, dx1_b, dg_ffn = _mm(dh, wf_ffn_in, tb=True, tm=512, tn=1024, tk=2 * d_ff, name="ffn_in_dx",
                             epilogue=(norm_bwd_twice, [dx2, x1], [g_ffn], [(d_model, F32), (d_model, BF16)], [(1, d_model)]))
    gwt_ffn_in = _mm(dh, u2, ta=True, tm=512, tn=d_model, tk=t, name="ffn_in_dw")

    dx1_b = _in_hbm(dx1_b)
    dmerged = _mm(dx1_b, wf_out, tb=True, out_dtype=BF16, tm=512, tn=1024, tk=d_model, name="mix_out_dx",
                  out_type=_hbm_array)
    gw_out = _mm(merged, dx1_b, ta=True, tm=256, tn=d_model, tk=t, name="mix_out_dw")

    def merge_bwd_fn(gt, ys, yd, dm):
        s_sb, s_dl = _sigmoid(gt[:, :d_model]), _sigmoid(gt[:, d_model:])
        dgates = jnp.concatenate([dm * ys * s_sb * (1.0 - s_sb), dm * yd * s_dl * (1.0 - s_dl)], axis=1)
        return dgates, dm * s_sb, dm * s_dl

    full_big = _chip_major({"w_out": gw_out, "w_ffn_in": gwt_ffn_in})
    dgates, dy_sb, dy_dl, others_big = _rowwise(
        merge_bwd_fn, [gates, y_sb, y_dl, dmerged], [], [(2 * d_model, BF16), (d_model, BF16), (d_model, BF16)], [],
        tm=256, name="merge_bwd", carry=_pair_carry(full_big))
    pair_big = _pair_sums(full_big, others_big, LATE[2:4])
    do_sb = _mm(dy_sb, wf_sb_up, tb=True, out_dtype=BF16, tm=1024, tn=SB_WIDTH, tk=d_model, name="sb_up_dx")
    gw_sb_up = _mm(o_sb2, dy_sb, ta=True, tm=SB_WIDTH, tn=1024, tk=512, name="sb_up_dw")
    do_dl = _mm(dy_dl, wf_dil_up, tb=True, tm=1024, tn=DIL_OUT_WIDTH, tk=d_model, name="dil_up_dx")
    gw_dil_up = _mm(o_dl2, dy_dl, ta=True, tm=DIL_OUT_WIDTH, tn=1024, tk=512, name="dil_up_dw")
    rest = [LATE[0], LATE[1], LATE[4]]
    full_rest = _chip_major({"w_sb_up": gw_sb_up, "w_dil_up": gw_dil_up, "w_ffn_out": gw_ffn_out})
    (dq_sb, dk_sb, dv_sb), brought = _sb_bwd(
        qkv3, o_sb, do_sb.reshape(b_sz, s_len, SB_WIDTH), b_sz, s_len,
        _chip_carry([p[1] for p in pair_big], LATE[2:4]) + _pair_carry(full_rest))
    pair_rest = _pair_sums(full_rest, brought[2:], rest)
    (dq_dl, dk_dl, dv_dl), landed_b = _dil_bwd(
        qkv3, o_dl, lse, do_dl.reshape(b_sz, s_len, DIL_OUT_WIDTH), b_sz, s_len,
        _chip_carry([p[1] for p in pair_rest], rest))
    pair = pair_rest[:2] + pair_big + pair_rest[2:]
    landed = [landed_b[0], landed_b[1], brought[0], brought[1], landed_b[2]]
    dproj = [a.reshape(t, -1) for a in (dq_sb, dk_sb, dv_sb, dq_dl, dk_dl, dv_dl)] + [dgates]
    gwt_in, gwt_in_b = _mm(dproj, u, ta=True, tm=256, tn=d_model, tk=t, name="proj_dw",
                           epilogue=(lambda tile: (tile, tile), [], [], [(d_model, F32), (d_model, BF16)], []))
    full_in = _chip_major({"w_in": gwt_in})
    pair_in = _pair_sums(full_in, _pair_exchange(_chip_major({"w_in": gwt_in_b}), "w_in"), ["w_in"])
    late_halves = _chip_sums(pair, landed, LATE)
    (dx, dg_mix), brought_in = _mm(
        dproj, wt_in, tm=512, tn=1024, tk=wt_in.shape[0], name="proj_dx",
        carry=_halves_carry([late_halves[k] for k in LATE]) + _chip_carry([p[1] for p in pair_in], ["w_in"]),
        epilogue=(norm_bwd_fn, [dx1, x2d], [g_mix], [(d_model, F32)], [(1, d_model)]))

    grads = dict(zip(LATE, brought_in[:len(LATE)]))
    grads.update(_chip_sums(pair_in, brought_in[len(LATE):], ["w_in"]))
    return dx, grads, dg_mix, dg_ffn, dg_fin, loss_lanes


def kernel(x, norm_mix_g, w_in, w_sb_up, w_dil_up, w_out, norm_ffn_g, w_ffn_in, w_ffn_out, norm_final_g, loss_target, m_norm_mix_g, m_w_in, m_w_sb_up, m_w_dil_up, m_w_out, m_norm_ffn_g, m_w_ffn_in, m_w_ffn_out, m_norm_final_g, v_norm_mix_g, v_w_in, v_w_sb_up, v_w_dil_up, v_w_out, v_norm_ffn_g, v_w_ffn_in, v_w_ffn_out, v_norm_final_g):
    b_sz, s_len, d_model = x.shape
    d_ff = w_ffn_out.shape[1] * N_CHIPS
    g_mix, g_ffn, g_fin = norm_mix_g, norm_ffn_g, norm_final_g.reshape(1, d_model)

    names = ["w_in", "w_sb_up", "w_dil_up", "w_out", "w_ffn_in", "w_ffn_out"]
    shards = {"w_in": jnp.swapaxes(w_in[0], 0, 1), "w_sb_up": w_sb_up[0], "w_dil_up": w_dil_up[0], "w_out": w_out[0],
              "w_ffn_in": w_ffn_in[0], "w_ffn_out": w_ffn_out[0]}
    slab_in = _cast_to_slab(shards["w_in"], "cast_w_in")

    dx, grads, dg_mix, dg_ffn, dg_fin, loss_lanes = _fwd_bwd(
        x, loss_target, g_mix, g_ffn, g_fin, slab_in, [shards[k] for k in LATE])

    small = jnp.concatenate([dg_mix, dg_ffn, dg_fin, loss_lanes, jnp.zeros((4, d_model), F32)], axis=0)
    (grads["w_in"],), small = _final_exchange([grads["w_in"]], small)
    grads["w_ffn_in"] = jnp.swapaxes(grads["w_ffn_in"], 0, 1)
    loss = small[3, 0]
    gains = jnp.concatenate([g_mix, g_ffn, g_fin, jnp.zeros((5, d_model), F32)], axis=0)
    gains_m = jnp.concatenate([m_norm_mix_g, m_norm_ffn_g, m_norm_final_g.reshape(1, d_model), jnp.zeros((5, d_model), F32)], axis=0)
    gains_v = jnp.concatenate([v_norm_mix_g, v_norm_ffn_g, v_norm_final_g.reshape(1, d_model), jnp.ones((5, d_model), F32)], axis=0)
    gd, gm, gv = _rowwise(_adamw_math, [gains, small, gains_m, gains_v], [], [(d_model, F32)] * 3, [], tm=8, name="adamw_gains")

    moments = {"w_in": (jnp.swapaxes(m_w_in[0], 0, 1), jnp.swapaxes(v_w_in[0], 0, 1)),
               "w_sb_up": (m_w_sb_up[0], v_w_sb_up[0]), "w_dil_up": (m_w_dil_up[0], v_w_dil_up[0]),
               "w_out": (m_w_out[0], v_w_out[0]), "w_ffn_in": (m_w_ffn_in[0], v_w_ffn_in[0]),
               "w_ffn_out": (m_w_ffn_out[0], v_w_ffn_out[0])}
    upd = {k: _adamw(shards[k], grads[k], moments[k][0], moments[k][1], "adamw_" + k) for k in names}

    def as_output(k, a):
        return (jnp.swapaxes(a, 0, 1) if k == "w_in" else a)[None]

    def w_out_of(i):
        return [as_output(k, upd[k][i]) for k in names]

    def ordered(mix, ws, ffn_g, fin):
        return [mix, ws[0], ws[1], ws[2], ws[3], ffn_g, ws[4], ws[5], fin]

    grad_ws = [as_output(k, grads[k]) for k in names]
    outs = [loss, dx.reshape(b_sz, s_len, d_model)]
    outs += ordered(small[0:1], grad_ws, small[1:2], small[2])
    outs += ordered(gd[0:1], w_out_of(0), gd[1:2], gd[2])
    outs += ordered(gm[0:1], w_out_of(1), gm[1:2], gm[2])
    outs += ordered(gv[0:1], w_out_of(2), gv[1:2], gv[2])
    return tuple(outs)
```

```python
import functools
import math

import jax
import jax.numpy as jnp
from jax import lax
from jax.experimental import pallas as pl
from jax.experimental.pallas import tpu as pltpu

F32 = jnp.float32
BF16 = jnp.bfloat16
MESH = pl.DeviceIdType.MESH

HEAD_DIM = 64
SB_HEADS = 8
DIL_PAIRS = ((128, 1), (512, 4), (2048, 16))
DIL_HEADS_PER_GROUP = 4
DIL_HEADS = DIL_HEADS_PER_GROUP * len(DIL_PAIRS)
SB_WIDTH = SB_HEADS * HEAD_DIM
DIL_WIDTH = DIL_HEADS * HEAD_DIM
DIL_OUT_WIDTH = DIL_HEADS_PER_GROUP * HEAD_DIM
QKV_WIDTH = 3 * SB_WIDTH + 3 * DIL_WIDTH
RMS_EPS = 1e-6
ALIBI_MAX_BIAS = 8.0
ADAM_LR = 0.001
ADAM_B1 = 0.9
ADAM_B2 = 0.999
ADAM_EPS = 1e-08
ADAM_WD = 0.01
ADAM_STEP = 10

LANES = 128
BLK = 128
NEG = -1e30
EXP_UNDERFLOW = -104.0
SB_FWD_CHAINS = 4
SB_BWD_CHAINS = 4
DIL_CHAINS = 4
N_CHIPS = 4
VMEM_CAP = 56 * 1024 * 1024


def _vmem_limit(tile_bytes):
    return int(min(VMEM_CAP, max(32 * 1024 * 1024, 3 * tile_bytes + 8 * 1024 * 1024)))


def _hbm_array(shape, dtype):
    return pltpu.HBM(shape, dtype)


def _nbytes(shape, dtype):
    return math.prod(shape) * jnp.dtype(dtype).itemsize


def _in_hbm(x):
    return pltpu.with_memory_space_constraint(x, pltpu.HBM)


def _dot(a, b):
    return jnp.dot(a, b, preferred_element_type=F32)


def _dot_nt(a, b):
    return lax.dot_general(a, b, (((1,), (1,)), ((), ())), preferred_element_type=F32)


def _dot_tn(a, b):
    return lax.dot_general(a, b, (((0,), (0,)), ((), ())), preferred_element_type=F32)


def _split2(x):
    hi = x.astype(BF16)
    lo = (x - hi.astype(F32)).astype(BF16)
    return hi, lo


def _sigmoid(x):
    return pl.reciprocal(1.0 + jnp.exp(-x), approx=True)


class _Carry:
    def __init__(self, arrays=(), out_shapes=(), aliased=False, sems=(), start=None, finish=None):
        self.arrays, self.out_shapes = list(arrays), list(out_shapes)
        self.n_aliased = len(self.arrays) if aliased is True else int(aliased)
        self.sems, self.start, self.finish = list(sems), start, finish

    def __bool__(self):
        return bool(self.arrays)

    def __add__(self, other):
        assert not other.n_aliased and self.n_aliased in (0, len(self.out_shapes))
        n_a, n_o, n_s = len(self.arrays), len(self.out_shapes), len(self.sems)
        return _Carry(
            self.arrays + other.arrays, self.out_shapes + other.out_shapes, self.n_aliased, self.sems + other.sems,
            lambda i, o, s: (self.start(i[:n_a], o[:n_o], s[:n_s]), other.start(i[n_a:], o[n_o:], s[n_s:])),
            lambda i, o, s: (self.finish(i[:n_a], o[:n_o], s[:n_s]), other.finish(i[n_a:], o[n_o:], s[n_s:])))

    def call_args(self, n_in, n_out):
        aliases = {n_in + k: n_out + k for k in range(self.n_aliased)}
        return _hbm_specs(len(self.arrays)), _hbm_specs(len(self.out_shapes)), self.out_shapes, aliases, self.sems

    def run(self, refs, n_in, n_out, step, n_steps, compute):
        if not self:
            compute()
            return
        n_c, n_o, n_s = len(self.arrays), len(self.out_shapes), len(self.sems)
        ins = refs[n_in:n_in + n_c]
        outs = refs[n_in + n_c + n_out:n_in + n_c + n_out + n_o]
        sems = refs[len(refs) - n_s:]

        @pl.when(step == 0)
        def _():
            self.start(ins, outs, sems)

        compute()

        @pl.when(step == n_steps - 1)
        def _():
            self.finish(ins, outs, sems)


def _mm(a, b, *, ta=False, tb=False, add=None, out_dtype=F32, tm, tn, tk, name, carry=None, epilogue=None,
        b_cols=None, out_type=jax.ShapeDtypeStruct):
    carry = carry or _Carry()
    n_car = len(carry.arrays)
    pieces = list(a) if isinstance(a, (list, tuple)) else [a]
    n_a = len(pieces)
    widths = [p.shape[1] for p in pieces]
    starts = [sum(widths[:p]) for p in range(n_a)]
    if ta:
        kdim, m = pieces[0].shape[0], sum(widths)
    else:
        m, kdim = pieces[0].shape[0], sum(widths)
    if tb:
        n, k2 = b.shape
    else:
        k2, n = b.shape
    col0 = 0
    if b_cols is not None:
        assert b_cols[0] % tn == 0, name
        col0, n = b_cols[0] // tn, b_cols[1]
    assert kdim == k2 and m % tm == 0 and n % tn == 0 and kdim % tk == 0, (name, a.shape, b.shape)
    nk = kdim // tk
    assert n_a == 1 or (nk == 1 and not tb and (not ta or all(w % tm == 0 for w in widths))), name
    grid = (m // tm, n // tn, nk)
    a_mode = dict(pipeline_mode=pl.Buffered(1)) if grid[0] == 1 and nk == 1 else {}
    b_mode = dict(pipeline_mode=pl.Buffered(1)) if grid[1] == 1 and nk == 1 else {}
    if n_a == 1:
        a_specs = [pl.BlockSpec((tk, tm), lambda i, j, k: (k, i), **a_mode) if ta
                   else pl.BlockSpec((tm, tk), lambda i, j, k: (i, k), **a_mode)]
    elif ta:
        a_specs = [pl.BlockSpec((tk, tm), lambda i, j, k, s=s // tm, w=w // tm: (0, jnp.clip(i - s, 0, w - 1)))
                   for s, w in zip(starts, widths)]
    else:
        a_specs = [pl.BlockSpec((tm, w), lambda i, j, k: (i, 0)) for w in widths]
    b_spec = (pl.BlockSpec((tn, tk), lambda i, j, k: (j + col0, k), **b_mode) if tb
              else pl.BlockSpec((tk, tn), lambda i, j, k: (k, j + col0), **b_mode))
    o_spec = pl.BlockSpec((tm, tn), lambda i, j, k: (i, j))
    dims = ((((0,) if ta else (1,)), ((1,) if tb else (0,))), ((), ()))
    has_add = add is not None
    if epilogue is None:
        ep_fn, ep_rows, ep_params, ep_outs, ep_accs = None, [], [], [], []
        out_sds, out_specs = [out_type((m, n), out_dtype)], [o_spec]
    else:
        ep_fn, ep_rows, ep_params, ep_outs, ep_accs = epilogue
        assert grid[1] == 1 or not ep_accs, name
        out_sds = [out_type((m, w * grid[1]), d) for w, d in ep_outs]
        out_sds += [jax.ShapeDtypeStruct(sh, F32) for sh in ep_accs]
        out_specs = [pl.BlockSpec((tm, w), lambda i, j, k: (i, j)) for w, _ in ep_outs]
        out_specs += [pl.BlockSpec(sh, lambda i, j, k: (0, 0)) for sh in ep_accs]
    n_main = len(out_sds)
    use_scratch = nk > 1 and (ep_fn is not None or jnp.dtype(out_dtype) != jnp.dtype(F32))
    n_in = n_a + 1 + has_add + len(ep_rows) + len(ep_params)

    def finish(total, refs, pid):
        outs = refs[n_in + n_car:n_in + n_car + n_main]
        if ep_fn is None:
            outs[0][...] = total.astype(out_dtype)
            return
        first = n_a + 1 + has_add
        rows = [r[...].astype(F32) for r in refs[first:first + len(ep_rows)]]
        params = [p[...] for p in refs[first + len(ep_rows):n_in]]
        res = ep_fn(total, *rows, *params)
        for o_ref, v in zip(outs[:len(ep_outs)], res):
            o_ref[...] = v.astype(o_ref.dtype)
        acc_refs = outs[len(ep_outs):]
        if acc_refs:
            @pl.when(pid[0] == 0)
            def _():
                for r in acc_refs:
                    r[...] = jnp.zeros(r.shape, F32)

            for r, v in zip(acc_refs, res[len(ep_outs):]):
                r[...] += v

    def compute(refs, pid):
        a_ref, b_ref = refs[0], refs[n_a]
        add_ref = refs[n_a + 1] if has_add else None

        def dot(x, y):
            return lax.dot_general(x.astype(BF16), y.astype(BF16), dims, preferred_element_type=F32)

        if n_a > 1 and ta:
            for p_ref, s, w in zip(refs[:n_a], starts, widths):
                @pl.when((pid[0] >= s // tm) & (pid[0] < (s + w) // tm))
                def _(p_ref=p_ref):
                    prod = dot(p_ref[...], b_ref[...])
                    finish(prod + add_ref[...] if has_add else prod, refs, pid)
            return
        if n_a > 1:
            prod = dot(a_ref[...], b_ref[:widths[0], :])
            for p_ref, s, w in zip(refs[1:n_a], starts[1:], widths[1:]):
                prod += dot(p_ref[...], b_ref[s:s + w, :])
        else:
            prod = dot(a_ref[...], b_ref[...])
        if nk == 1:
            finish(prod + add_ref[...] if has_add else prod, refs, pid)
            return
        acc_ref = refs[n_in + n_car + n_main + len(carry.out_shapes)] if use_scratch else refs[n_in + n_car]
        k = pid[2]

        @pl.when(k == 0)
        def _():
            acc_ref[...] = prod + add_ref[...] if has_add else prod

        @pl.when(k > 0)
        def _():
            acc_ref[...] += prod

        if use_scratch:
            @pl.when(k == nk - 1)
            def _():
                finish(acc_ref[...], refs, pid)

    def body(*refs):
        pid = (pl.program_id(0), pl.program_id(1), pl.program_id(2))
        step = (pid[0] * grid[1] + pid[1]) * nk + pid[2]
        carry.run(refs, n_in, n_main, step, grid[0] * grid[1] * nk, lambda: compute(refs, pid))

    tile_bytes = ((n_a if ta else 1) * _nbytes((tm, tk), pieces[0].dtype)
                  + _nbytes((tk, tn), b.dtype) + 2 * _nbytes((tm, tn), F32)
                  + (_nbytes((tm, tn), F32) if has_add else 0)
                  + sum(_nbytes((tm, r.shape[1]), r.dtype) for r in ep_rows) + sum(_nbytes((tm, w), d) for w, d in ep_outs))
    in_specs = a_specs + [b_spec] + ([o_spec] if has_add else [])
    in_specs += [pl.BlockSpec((tm, r.shape[1] // grid[1]), lambda i, j, k: (i, j)) for r in ep_rows]
    in_specs += [pl.BlockSpec(p.shape, lambda i, j, k: (0, 0)) for p in ep_params]
    args = tuple(pieces) + (b,) + ((add,) if has_add else ()) + tuple(ep_rows) + tuple(ep_params)
    scratch = [pltpu.VMEM((tm, tn), F32)] if use_scratch else []
    serial = bool(carry) or bool(ep_accs)
    c_in, c_out, c_shapes, c_alias, c_sems = carry.call_args(n_in, n_main)
    res = pl.pallas_call(
        body, name=name, grid=grid,
        in_specs=in_specs + c_in, out_specs=out_specs + c_out, out_shape=out_sds + c_shapes,
        input_output_aliases=c_alias, scratch_shapes=scratch + c_sems,
        compiler_params=pltpu.CompilerParams(
            dimension_semantics=("arbitrary",) * 3 if serial else ("parallel", "parallel", "arbitrary"),
            vmem_limit_bytes=_vmem_limit(tile_bytes)),
    )(*args, *carry.arrays)
    main = res[0] if ep_fn is None else list(res[:n_main])
    return (main, res[n_main:]) if carry else main


def _rowwise(fn, rows, params, outs, accs, *, tm, name, carry=None, out_type=jax.ShapeDtypeStruct):
    carry = carry or _Carry()
    t = rows[0].shape[0]
    assert t % tm == 0, (name, t, tm)
    n_r, n_p, n_o, n_c = len(rows), len(params), len(outs), len(carry.arrays)

    def compute(refs, first):
        vals = [r[...].astype(F32) for r in refs[:n_r]] + [p[...] for p in refs[n_r:n_r + n_p]]
        res = fn(*vals)
        o_refs = refs[n_r + n_p + n_c:n_r + n_p + n_c + n_o]
        a_refs = refs[n_r + n_p + n_c + n_o:n_r + n_p + n_c + n_o + len(accs)]
        for o_ref, v in zip(o_refs, res[:n_o]):
            o_ref[...] = v.astype(o_ref.dtype)
        if accs:
            @pl.when(first)
            def _():
                for a_ref in a_refs:
                    a_ref[...] = jnp.zeros(a_ref.shape, F32)

            for a_ref, v in zip(a_refs, res[n_o:]):
                a_ref[...] += v

    def body(*refs):
        step = pl.program_id(0)
        carry.run(refs, n_r + n_p, n_o + len(accs), step, t // tm, lambda: compute(refs, step == 0))

    in_specs = [pl.BlockSpec((tm, r.shape[1]), lambda i: (i, 0)) for r in rows]
    in_specs += [pl.BlockSpec(p.shape, lambda i: (0, 0)) for p in params]
    out_specs = [pl.BlockSpec((tm, w), lambda i: (i, 0)) for w, _ in outs]
    out_specs += [pl.BlockSpec(s, lambda i: (0, 0)) for s in accs]
    out_shape = [out_type((t, w), d) for w, d in outs]
    out_shape += [jax.ShapeDtypeStruct(s, F32) for s in accs]
    tile_bytes = sum(_nbytes((tm, r.shape[1]), r.dtype) for r in rows) + sum(_nbytes((tm, w), F32) for w, _ in outs)
    c_in, c_out, c_shapes, c_alias, c_sems = carry.call_args(n_r + n_p, n_o + len(accs))
    res = pl.pallas_call(
        body, name=name, grid=(t // tm,), in_specs=in_specs + c_in, out_specs=out_specs + c_out,
        out_shape=out_shape + c_shapes, input_output_aliases=c_alias, scratch_shapes=c_sems,
        compiler_params=pltpu.CompilerParams(
            dimension_semantics=("arbitrary",) if accs or carry else ("parallel",),
            vmem_limit_bytes=_vmem_limit(2 * tile_bytes)),
    )(*rows, *params, *carry.arrays)
    own = n_o + len(accs)
    return (list(res[:own]) + [res[own:]]) if carry else res


def _rms_stats(x):
    r = lax.rsqrt(jnp.mean(x * x, axis=-1, keepdims=True) + RMS_EPS)
    return x * r, r


def _rms_bwd(dy, xhat, r, g):
    dxhat = dy * g
    dx = r * (dxhat - xhat * jnp.mean(dxhat * xhat, axis=-1, keepdims=True))
    return dx, dy * xhat


def _sb_consts():
    lane = lax.broadcasted_iota(jnp.int32, (BLK, LANES), 1)
    head0 = lane < HEAD_DIM
    row = lax.broadcasted_iota(jnp.int32, (2 * BLK, BLK), 0) % BLK
    col = lax.broadcasted_iota(jnp.int32, (2 * BLK, BLK), 1)
    causal = col < row
    jj = lax.broadcasted_iota(jnp.int32, (BLK, BLK), 0)
    ss = lax.broadcasted_iota(jnp.int32, (BLK, BLK), 1)
    suffix = jnp.where(jj > ss, 1.0, 0.0).astype(BF16)
    return head0, causal, suffix


def _stack_heads(x, head0):
    zero = jnp.zeros_like(x)
    return jnp.concatenate([jnp.where(head0, x, zero), jnp.where(head0, zero, x)], axis=0)


def _sb_logits(z, causal, masked):
    sp = jnp.log(1.0 + jnp.exp(-jnp.abs(z)))
    log_keep = -(jnp.maximum(z, 0.0) + sp)
    log_beta = jnp.minimum(z, 0.0) - sp
    if masked:
        log_keep = jnp.where(causal, log_keep, 0.0)
    return log_keep, log_beta


def _suffix_sums(x, suffix):
    hi, lo = _split2(x)
    after = _dot(hi, suffix) + _dot(lo, suffix)
    total = jnp.broadcast_to(after[:, 0:1] + x[:, 0:1], x.shape)
    return after, total


def _sb_walk_back(i, state, per_chain, tile):
    def alive(st):
        worst = functools.reduce(jnp.maximum, [st[p][:, 0:1] for p in range(0, len(st), per_chain)])
        return jnp.max(worst) > EXP_UNDERFLOW

    def cond(c):
        return jnp.logical_and(c[0] < i, alive(c[1]))

    def body(c):
        return c[0] + 1, tile(i - 1 - c[0], c[1], False)

    return lax.while_loop(cond, body, (jnp.int32(0), state))[1]


def _lane_blocks(x, n):
    return [x[:, p * LANES:(p + 1) * LANES] for p in range(n)]


def _sb_fwd(qkv, b_sz, s_len, carry):
    nq = s_len // BLK
    n_pairs = SB_WIDTH // LANES
    ch = SB_FWD_CHAINS
    n_steps = n_pairs // ch
    scale = 1.0 / math.sqrt(HEAD_DIM)

    def compute(q_ref, k_ref, v_ref, o_ref):
        head0, causal, suffix = _sb_consts()

        def q_block(i, _):
            qs = pl.multiple_of(i * BLK, BLK)
            q_all = (q_ref[pl.ds(qs, BLK), :] * scale).astype(BF16)
            q01 = [_stack_heads(q, head0) for q in _lane_blocks(q_all, ch)]

            def tile(j, state, masked):
                ks = pl.multiple_of(j * BLK, BLK)
                ks_ = _lane_blocks(k_ref[pl.ds(ks, BLK), :].astype(BF16), ch)
                vs_ = _lane_blocks(v_ref[pl.ds(ks, BLK), :].astype(BF16), ch)
                zs = [_dot_nt(q01[p], ks_[p]) for p in range(ch)]
                logits = [_sb_logits(z, causal, masked) for z in zs]
                sums = [_suffix_sums(lg[0], suffix) for lg in logits]
                out = []
                for p in range(ch):
                    carry, acc = state[2 * p], state[2 * p + 1]
                    after, total = sums[p]
                    a = jnp.exp(logits[p][1] + carry + after)
                    if masked:
                        a = jnp.where(causal, a, 0.0)
                    a_hi, a_lo = _split2(a)
                    a_cat = jnp.concatenate([a_hi[:BLK], a_hi[BLK:], a_lo[:BLK], a_lo[BLK:]], axis=1)
                    v01 = _stack_heads(vs_[p], head0)
                    out += [carry + total, acc + _dot(a_cat, jnp.concatenate([v01, v01], axis=0))]
                return tuple(out)

            state = (jnp.zeros((2 * BLK, BLK), F32), jnp.zeros((BLK, LANES), F32)) * ch
            state = tile(i, state, True)
            state = _sb_walk_back(i, state, 2, tile)
            o_ref[pl.ds(qs, BLK), :] = jnp.concatenate([state[2 * p + 1] for p in range(ch)], axis=1)
            return 0

        lax.fori_loop(0, nq, q_block, 0)

    def body(*refs):
        step = pl.program_id(0) * n_steps + pl.program_id(1)
        o_ref = refs[3 + len(carry.arrays)]
        carry.run(refs, 3, 1, step, b_sz * n_steps, lambda: compute(refs[0], refs[1], refs[2], o_ref))

    blk = lambda off: pl.BlockSpec((None, s_len, ch * LANES), lambda b, p: (b, 0, off + p))
    c_in, c_out, c_shapes, c_alias, c_sems = carry.call_args(3, 1)
    res = pl.pallas_call(
        body, name="sb_fwd", grid=(b_sz, n_steps),
        in_specs=[blk(0), blk(n_steps), blk(2 * n_steps)] + c_in, out_specs=[blk(0)] + c_out,
        out_shape=[jax.ShapeDtypeStruct((b_sz, s_len, SB_WIDTH), F32)] + c_shapes,
        input_output_aliases=c_alias, scratch_shapes=c_sems,
        compiler_params=pltpu.CompilerParams(dimension_semantics=("arbitrary", "arbitrary"),
                                             vmem_limit_bytes=VMEM_CAP),
    )(qkv, qkv, qkv, *carry.arrays)
    return res[0], res[1:]


def _sb_bwd(qkv, o_sb, do_sb, b_sz, s_len, carry):
    nq = s_len // BLK
    n_pairs = SB_WIDTH // LANES
    ch = SB_BWD_CHAINS
    n_steps = n_pairs // ch
    scale = 1.0 / math.sqrt(HEAD_DIM)

    def compute(q_ref, k_ref, v_ref, o_ref, do_ref, dq_ref, dk_ref, dv_ref, dk_acc, dv_acc):
        head0, causal, suffix = _sb_consts()
        lrow = lax.broadcasted_iota(jnp.int32, (LANES, LANES), 0)
        ones_h0 = jnp.where(lrow < HEAD_DIM, 1.0, 0.0).astype(BF16)
        ones_h1 = jnp.where(lrow >= HEAD_DIM, 1.0, 0.0).astype(BF16)
        dk_acc[...] = jnp.zeros(dk_acc.shape, F32)
        dv_acc[...] = jnp.zeros(dv_acc.shape, F32)

        def q_block(i, _):
            qs = pl.multiple_of(i * BLK, BLK)
            q_all = (q_ref[pl.ds(qs, BLK), :] * scale).astype(BF16)
            do_all = do_ref[pl.ds(qs, BLK), :].astype(BF16)
            dd_all = do_all.astype(F32) * o_ref[pl.ds(qs, BLK), :]
            q01 = [_stack_heads(q, head0) for q in _lane_blocks(q_all, ch)]
            do01 = [_stack_heads(d, head0) for d in _lane_blocks(do_all, ch)]
            tot = []
            for dd in _lane_blocks(dd_all, ch):
                dd_hi, dd_lo = _split2(dd)
                tot.append(jnp.concatenate([_dot(dd_hi, ones_h0) + _dot(dd_lo, ones_h0),
                                            _dot(dd_hi, ones_h1) + _dot(dd_lo, ones_h1)], axis=0))

            def tile(j, state, masked):
                ks = pl.multiple_of(j * BLK, BLK)
                ks_ = _lane_blocks(k_ref[pl.ds(ks, BLK), :].astype(BF16), ch)
                vs_ = _lane_blocks(v_ref[pl.ds(ks, BLK), :].astype(BF16), ch)
                zs = [_dot_nt(q01[p], ks_[p]) for p in range(ch)]
                das = [_dot_nt(do01[p], vs_[p]) for p in range(ch)]
                logits = [_sb_logits(z, causal, masked) for z in zs]
                sums = [_suffix_sums(lg[0], suffix) for lg in logits]
                a_s, e_s = [], []
                for p in range(ch):
                    a = jnp.exp(logits[p][1] + state[3 * p] + sums[p][0])
                    if masked:
                        a = jnp.where(causal, a, 0.0)
                    a_s.append(a)
                    e_s.append(a * das[p])
                e_sums = [_suffix_sums(e, suffix) for e in e_s]
                out, dks, dvs = [], [], []
                for p in range(ch):
                    carry, rcarry, dq = state[3 * p:3 * p + 3]
                    e = e_s[p]
                    before = tot[p] - (rcarry + e_sums[p][0] + e)
                    beta = jnp.exp(logits[p][1])
                    dz = e * (1.0 - beta) - beta * before
                    if masked:
                        dz = jnp.where(causal, dz, 0.0)
                    dz_b = dz.astype(BF16)
                    dks.append(_dot_tn(dz_b, q01[p]))
                    dvs.append(_dot_tn(a_s[p].astype(BF16), do01[p]))
                    out += [carry + sums[p][1], rcarry + e_sums[p][1], dq + _dot(dz_b, ks_[p])]
                dk_acc[pl.ds(ks, BLK), :] += jnp.concatenate(dks, axis=1)
                dv_acc[pl.ds(ks, BLK), :] += jnp.concatenate(dvs, axis=1)
                return tuple(out)

            state = (jnp.zeros((2 * BLK, BLK), F32),) * (3 * ch)
            state = tile(i, state, True)
            state = _sb_walk_back(i, state, 3, tile)
            dq = [jnp.where(head0, state[3 * p + 2][:BLK], state[3 * p + 2][BLK:]) for p in range(ch)]
            dq_ref[pl.ds(qs, BLK), :] = (jnp.concatenate(dq, axis=1) * scale).astype(dq_ref.dtype)
            return 0

        lax.fori_loop(0, nq, q_block, 0)
        dk_ref[...] = dk_acc[...].astype(dk_ref.dtype)
        dv_ref[...] = dv_acc[...].astype(dv_ref.dtype)

    def body(*refs):
        step = pl.program_id(0) * n_steps + pl.program_id(1)
        n_c, n_o = len(carry.arrays), len(carry.out_shapes)
        own = refs[:5] + refs[5 + n_c:8 + n_c] + refs[8 + n_c + n_o:10 + n_c + n_o]
        carry.run(refs, 5, 3, step, b_sz * n_steps, lambda: compute(*own))

    blk = lambda off: pl.BlockSpec((None, s_len, ch * LANES), lambda b, p: (b, 0, off + p))
    once = lambda off: pl.BlockSpec((None, s_len, ch * LANES), lambda b, p: (b, 0, off + p),
                                    pipeline_mode=pl.Buffered(1))
    out_sd = jax.ShapeDtypeStruct((b_sz, s_len, SB_WIDTH), BF16)
    c_in, c_out, c_shapes, c_alias, c_sems = carry.call_args(5, 3)
    res = pl.pallas_call(
        body, name="sb_bwd", grid=(b_sz, n_steps),
        in_specs=[once(0), once(n_steps), once(2 * n_steps), once(0), once(0)] + c_in,
        out_specs=[blk(0), blk(0), blk(0)] + c_out, out_shape=[out_sd, out_sd, out_sd] + c_shapes,
        input_output_aliases=c_alias,
        scratch_shapes=[pltpu.VMEM((s_len, ch * LANES), F32), pltpu.VMEM((s_len, ch * LANES), F32)] + c_sems,
        compiler_params=pltpu.CompilerParams(dimension_semantics=("arbitrary", "arbitrary"),
                                             vmem_limit_bytes=VMEM_CAP),
    )(qkv, qkv, qkv, o_sb, do_sb, *carry.arrays)
    return res[:3], res[3:]


def _dil_consts(group, pair_idx, dilation):
    lane = lax.broadcasted_iota(jnp.int32, (BLK, LANES), 1)
    head0 = lane < HEAD_DIM
    row = lax.broadcasted_iota(jnp.int32, (2 * BLK, BLK), 0)
    qa = row % BLK
    kb = lax.broadcasted_iota(jnp.int32, (2 * BLK, BLK), 1)
    head = (group * DIL_HEADS_PER_GROUP + 2 * pair_idx + row // BLK).astype(F32)
    slope = jnp.exp((-ALIBI_MAX_BIAS * math.log(2.0) / DIL_HEADS) * (head + 1.0))
    valid_cur = kb <= qa
    valid_prev = kb >= qa
    bias_cur = -slope * ((qa - kb) * dilation).astype(F32)
    bias_prev = -slope * ((BLK + qa - kb) * dilation).astype(F32)
    return head0, valid_cur, valid_prev, bias_cur, bias_prev


def _dil_units(s_len, dilation):
    nb = s_len // dilation // BLK
    return [(r, n) for r in range(dilation) for n in range(nb)]


def _dil_rows(n, r, dilation):
    if dilation == 1:
        return pl.ds(n * BLK, BLK)
    return pl.ds(n * BLK * dilation + r, BLK, stride=dilation)


def _dil_scores(q01, k, bias, valid):
    s = _dot_nt(q01, k) * (1.0 / math.sqrt(HEAD_DIM)) + bias
    return jnp.where(valid, s, NEG)


def _dil_fwd(qkv, b_sz, s_len, carry):
    n_pairs = DIL_OUT_WIDTH // LANES
    q_off = 3 * SB_WIDTH // LANES
    per_kind = DIL_WIDTH // LANES

    def compute(pair_idx, qkv_refs, o_ref, lse_ref, m_s, l_s):
        m_s[...] = jnp.full(m_s.shape, NEG, F32)
        l_s[...] = jnp.zeros(l_s.shape, F32)
        o_ref[...] = jnp.zeros(o_ref.shape, F32)
        for g, (_, dilation) in enumerate(DIL_PAIRS):
            q_ref, k_ref, v_ref = qkv_refs[3 * g:3 * g + 3]
            head0, valid_cur, valid_prev, bias_cur, bias_prev = _dil_consts(g, pair_idx, dilation)
            units = _dil_units(s_len, dilation)
            for u0 in range(0, len(units), DIL_CHAINS):
                group = units[u0:u0 + DIL_CHAINS]
                rows_of = [_dil_rows(n, r, dilation) for r, n in group]
                scores, values = [], []
                for (r, n), rows in zip(group, rows_of):
                    q01 = _stack_heads(q_ref[rows, :].astype(BF16), head0)
                    sc = [_dil_scores(q01, k_ref[rows, :].astype(BF16), bias_cur, valid_cur)]
                    vals = [_stack_heads(v_ref[rows, :].astype(BF16), head0)]
                    if n > 0:
                        prev = _dil_rows(n - 1, r, dilation)
                        sc.append(_dil_scores(q01, k_ref[prev, :].astype(BF16), bias_prev, valid_prev))
                        vals.append(_stack_heads(v_ref[prev, :].astype(BF16), head0))
                    scores.append(sc)
                    values.append(vals)
                stats = []
                for sc, rows in zip(scores, rows_of):
                    m_blk = functools.reduce(jnp.maximum, [jnp.max(x, axis=-1, keepdims=True) for x in sc])
                    m_old = jnp.concatenate([m_s.at[0][rows, :], m_s.at[1][rows, :]], axis=0)
                    l_old = jnp.concatenate([l_s.at[0][rows, :], l_s.at[1][rows, :]], axis=0)
                    m_new = jnp.maximum(m_old, m_blk)
                    probs = [jnp.exp(x - m_new) for x in sc]
                    l_blk = functools.reduce(jnp.add, [jnp.sum(p, axis=-1, keepdims=True) for p in probs])
                    alpha = jnp.exp(m_old - m_new)
                    stats.append((m_new, alpha * l_old + l_blk, alpha, probs))
                for (m_new, l_new, alpha, probs), vals, rows in zip(stats, values, rows_of):
                    alpha_tok = jnp.where(head0, alpha[:BLK], alpha[BLK:])
                    p_cat = jnp.concatenate(
                        [h for p in probs for h in (p[:BLK].astype(BF16), p[BLK:].astype(BF16))], axis=1)
                    o_ref[rows, :] = alpha_tok * o_ref[rows, :] + _dot(p_cat, jnp.concatenate(vals, axis=0))
                    m_s.at[0][rows, :] = m_new[:BLK]
                    m_s.at[1][rows, :] = m_new[BLK:]
                    l_s.at[0][rows, :] = l_new[:BLK]
                    l_s.at[1][rows, :] = l_new[BLK:]
        lane = lax.broadcasted_iota(jnp.int32, (BLK, LANES), 1)
        for c in range(s_len // BLK):
            rows = pl.ds(c * BLK, BLK)
            l0, l1 = l_s.at[0][rows, :], l_s.at[1][rows, :]
            o_ref[rows, :] = o_ref[rows, :] / jnp.where(lane < HEAD_DIM, l0, l1)
            lse_ref.at[0][rows, :] = m_s.at[0][rows, :] + jnp.log(l0)
            lse_ref.at[1][rows, :] = m_s.at[1][rows, :] + jnp.log(l1)

    def body(*refs):
        pair_idx = pl.program_id(1)
        step = pl.program_id(0) * n_pairs + pair_idx
        n_c, n_o = len(carry.arrays), len(carry.out_shapes)
        o_ref, lse_ref = refs[9 + n_c:11 + n_c]
        m_s, l_s = refs[11 + n_c + n_o:13 + n_c + n_o]
        carry.run(refs, 9, 2, step, b_sz * n_pairs, lambda: compute(pair_idx, refs[:9], o_ref, lse_ref, m_s, l_s))

    in_specs = []
    for g in range(len(DIL_PAIRS)):
        for kind in range(3):
            off = q_off + kind * per_kind + g * n_pairs
            in_specs.append(pl.BlockSpec((None, s_len, LANES), lambda b, p, off=off: (b, 0, off + p)))
    c_in, c_out, c_shapes, c_alias, c_sems = carry.call_args(9, 2)
    res = pl.pallas_call(
        body, name="dil_fwd", grid=(b_sz, n_pairs),
        in_specs=in_specs + c_in,
        out_specs=[pl.BlockSpec((None, s_len, LANES), lambda b, p: (b, 0, p)),
                   pl.BlockSpec((None, None, 2, s_len, LANES), lambda b, p: (b, p, 0, 0, 0))] + c_out,
        out_shape=[jax.ShapeDtypeStruct((b_sz, s_len, DIL_OUT_WIDTH), F32),
                   jax.ShapeDtypeStruct((b_sz, n_pairs, 2, s_len, LANES), F32)] + c_shapes,
        input_output_aliases=c_alias,
        scratch_shapes=[pltpu.VMEM((2, s_len, LANES), F32), pltpu.VMEM((2, s_len, LANES), F32)] + c_sems,
        compiler_params=pltpu.CompilerParams(dimension_semantics=("arbitrary", "arbitrary"),
                                             vmem_limit_bytes=VMEM_CAP),
    )(*([qkv] * 9), *carry.arrays)
    return res[0], res[1], res[2:]


def _dil_bwd(qkv, o_dl, lse, do_dl, b_sz, s_len, carry):
    n_pairs = DIL_OUT_WIDTH // LANES
    n_groups = len(DIL_PAIRS)
    q_off = 3 * SB_WIDTH // LANES
    per_kind = DIL_WIDTH // LANES

    def compute(pair_idx, group, q_ref, k_ref, v_ref, o_ref, lse_ref, do_ref, dq_ref, dk_ref, dv_ref, d_s, dq_s, dk_s, dv_s):
        lrow = lax.broadcasted_iota(jnp.int32, (LANES, LANES), 0)
        ones_h0 = jnp.where(lrow < HEAD_DIM, 1.0, 0.0).astype(BF16)
        ones_h1 = jnp.where(lrow >= HEAD_DIM, 1.0, 0.0).astype(BF16)
        for c in range(s_len // BLK):
            rows = pl.ds(c * BLK, BLK)
            dd_hi, dd_lo = _split2(do_ref[rows, :] * o_ref[rows, :])
            d_s.at[0][rows, :] = _dot(dd_hi, ones_h0) + _dot(dd_lo, ones_h0)
            d_s.at[1][rows, :] = _dot(dd_hi, ones_h1) + _dot(dd_lo, ones_h1)
        dk_s[...] = jnp.zeros(dk_s.shape, F32)
        dv_s[...] = jnp.zeros(dv_s.shape, F32)

        def one_group(g, dilation):
            head0, valid_cur, valid_prev, bias_cur, bias_prev = _dil_consts(g, pair_idx, dilation)
            units = _dil_units(s_len, dilation)
            scale = 1.0 / math.sqrt(HEAD_DIM)
            for u0 in range(0, len(units), DIL_CHAINS):
                chunk = units[u0:u0 + DIL_CHAINS]
                loaded = []
                for r, n in chunk:
                    rows = _dil_rows(n, r, dilation)
                    q01 = _stack_heads(q_ref[rows, :].astype(BF16), head0)
                    do01 = _stack_heads(do_ref[rows, :].astype(BF16), head0)
                    lse01 = jnp.concatenate([lse_ref.at[0][rows, :], lse_ref.at[1][rows, :]], axis=0)
                    d01 = jnp.concatenate([d_s.at[0][rows, :], d_s.at[1][rows, :]], axis=0)
                    blocks = [(rows, bias_cur, valid_cur)]
                    if n > 0:
                        blocks.append((_dil_rows(n - 1, r, dilation), bias_prev, valid_prev))
                    parts = []
                    for krows, bias, valid in blocks:
                        k = k_ref[krows, :].astype(BF16)
                        v = v_ref[krows, :].astype(BF16)
                        parts.append((krows, k, _dil_scores(q01, k, bias, valid), _dot_nt(do01, v)))
                    loaded.append((rows, q01, do01, lse01, d01, parts))
                grads = []
                for rows, q01, do01, lse01, d01, parts in loaded:
                    for krows, k, sc, dp in parts:
                        p = jnp.exp(sc - lse01)
                        grads.append((p.astype(BF16), (p * (dp - d01) * scale).astype(BF16)))
                it = iter(grads)
                updates = []
                for rows, q01, do01, lse01, d01, parts in loaded:
                    dq = jnp.zeros((2 * BLK, LANES), F32)
                    for krows, k, sc, dp in parts:
                        p_b, ds = next(it)
                        dq = dq + _dot(ds, k)
                        updates.append((krows, _dot_tn(ds, q01), _dot_tn(p_b, do01)))
                    dq_s[rows, :] = jnp.where(head0, dq[:BLK], dq[BLK:])
                for krows, dk, dv in updates:
                    dk_s[krows, :] = dk_s[krows, :] + dk
                    dv_s[krows, :] = dv_s[krows, :] + dv

        for g, (_, dilation) in enumerate(DIL_PAIRS):
            pl.when(group == g)(functools.partial(one_group, g, dilation))
        dq_ref[...] = dq_s[...].astype(dq_ref.dtype)
        dk_ref[...] = dk_s[...].astype(dk_ref.dtype)
        dv_ref[...] = dv_s[...].astype(dv_ref.dtype)

    def body(*refs):
        pair_idx, group = pl.program_id(1), pl.program_id(2)
        step = (pl.program_id(0) * n_pairs + pair_idx) * n_groups + group
        n_c, n_o = len(carry.arrays), len(carry.out_shapes)
        own = refs[:6] + refs[6 + n_c:9 + n_c] + refs[9 + n_c + n_o:13 + n_c + n_o]
        carry.run(refs, 6, 3, step, b_sz * n_pairs * n_groups, lambda: compute(pair_idx, group, *own))

    def qkv_spec(kind):
        return pl.BlockSpec((None, s_len, LANES),
                            lambda b, p, g: (b, 0, q_off + kind * per_kind + g * n_pairs + p))

    tok_spec = pl.BlockSpec((None, s_len, LANES), lambda b, p, g: (b, 0, p))
    out_spec = pl.BlockSpec((None, s_len, LANES), lambda b, p, g: (b, 0, g * n_pairs + p))
    out_sd = jax.ShapeDtypeStruct((b_sz, s_len, DIL_WIDTH), BF16)
    c_in, c_out, c_shapes, c_alias, c_sems = carry.call_args(6, 3)
    res = pl.pallas_call(
        body, name="dil_bwd", grid=(b_sz, n_pairs, n_groups),
        in_specs=[qkv_spec(0), qkv_spec(1), qkv_spec(2), tok_spec,
                  pl.BlockSpec((None, None, 2, s_len, LANES), lambda b, p, g: (b, p, 0, 0, 0)), tok_spec] + c_in,
        out_specs=[out_spec, out_spec, out_spec] + c_out,
        out_shape=[out_sd, out_sd, out_sd] + c_shapes,
        input_output_aliases=c_alias,
        scratch_shapes=[pltpu.VMEM((2, s_len, LANES), F32)] + [pltpu.VMEM((s_len, LANES), F32)] * 3 + c_sems,
        compiler_params=pltpu.CompilerParams(dimension_semantics=("arbitrary", "arbitrary", "arbitrary"),
                                             vmem_limit_bytes=VMEM_CAP),
    )(qkv, qkv, qkv, o_dl, lse, do_dl, *carry.arrays)
    return res[:3], res[3:]


def _mesh_pos():
    return lax.axis_index("x"), lax.axis_index("y"), lax.axis_index("c")


def _other_chips(x, y):
    return [(1 - x, y), (x, 1 - y), (1 - x, 1 - y)]


def _hbm_specs(n):
    return [pl.BlockSpec(memory_space=pl.ANY)] * n


SWAPPED = ("w_ffn_in",)


def _slot(x, y, swapped):
    return 2 * y + x if swapped else 2 * x + y


def _cast_to_slab(w, name):
    rows, cols = w.shape
    mine = jnp.reshape(_slot(lax.axis_index("x"), lax.axis_index("y"), False), (1,)).astype(jnp.int32)

    def body(idx_ref, w_ref, o_ref):
        o_ref[...] = w_ref[...].astype(BF16)

    return pl.pallas_call(
        body, name=name,
        grid_spec=pltpu.PrefetchScalarGridSpec(
            num_scalar_prefetch=1, grid=(1,),
            in_specs=[pl.BlockSpec((rows, cols), lambda i, idx: (0, 0))],
            out_specs=pl.BlockSpec((None, rows, cols), lambda i, idx: (idx[0], 0, 0))),
        out_shape=_hbm_array((N_CHIPS, rows, cols), BF16),
        compiler_params=pltpu.CompilerParams(vmem_limit_bytes=_vmem_limit(rows * cols * 6)),
    )(mine, w)


def _gather_issue(slabs, send_sems, recv_sems, swapped):
    x, y, c = _mesh_pos()
    for k, slab in enumerate(slabs):
        half = slab.shape[1] // 2
        rows = slab.at[_slot(x, y, swapped[k]), pl.ds(c * half, half), :]
        for r, (px, py) in enumerate(_other_chips(x, y)):
            pltpu.make_async_remote_copy(
                src_ref=rows, dst_ref=rows, send_sem=send_sems.at[6 * k + r], recv_sem=recv_sems.at[6 * k + r],
                device_id=(px, py, c), device_id_type=MESH).start()


def _gather_complete(slabs, send_sems, recv_sems, swapped):
    x, y, c = _mesh_pos()
    chips = _other_chips(x, y)

    def copy(k, sem, block, rows, to):
        ref = slabs[k].at[block, rows, :]
        return pltpu.make_async_remote_copy(
            src_ref=ref, dst_ref=ref, send_sem=send_sems.at[sem], recv_sem=recv_sems.at[sem],
            device_id=to, device_id_type=MESH)

    for k, slab in enumerate(slabs):
        half = slab.shape[1] // 2
        for r, (px, py) in enumerate(chips):
            theirs = _slot(px, py, swapped[k])
            copy(k, 6 * k + r, theirs, pl.ds(c * half, half), (px, py, c)).wait_recv()
            copy(k, 6 * k + 3 + r, theirs, pl.ds(c * half, half), (x, y, 1 - c)).start()
    for k, slab in enumerate(slabs):
        half = slab.shape[1] // 2
        for r, (px, py) in enumerate(chips):
            copy(k, 6 * k + 3 + r, _slot(px, py, swapped[k]), pl.ds((1 - c) * half, half), (x, y, 1 - c)).wait_recv()
    for k, slab in enumerate(slabs):
        half = slab.shape[1] // 2
        for r, (px, py) in enumerate(chips):
            copy(k, 6 * k + r, _slot(x, y, swapped[k]), pl.ds(c * half, half), (px, py, c)).wait_send()
            copy(k, 6 * k + 3 + r, _slot(px, py, swapped[k]), pl.ds(c * half, half), (x, y, 1 - c)).wait_send()


def _gather_sems(n):
    return [pltpu.SemaphoreType.DMA((6 * n,)), pltpu.SemaphoreType.DMA((6 * n,))]


def _gather_carry(slabs, names):
    swapped = [k in SWAPPED for k in names]
    return _Carry(slabs, [_hbm_array(a.shape, a.dtype) for a in slabs], True, _gather_sems(len(slabs)),
                  lambda ins, outs, sems: _gather_issue(outs, *sems, swapped),
                  lambda ins, outs, sems: _gather_complete(outs, *sems, swapped))


def _cast_carry(shards, names):
    swapped = [k in SWAPPED for k in names]

    def start(ins, outs, sems):
        x, y, _ = _mesh_pos()
        for w, slab, sw in zip(ins, outs, swapped):
            def cast(f32_buf, bf16_buf, sem, w=w, slab=slab, sw=sw):
                load = pltpu.make_async_copy(w, f32_buf, sem)
                load.start()
                load.wait()
                bf16_buf[...] = f32_buf[...].astype(BF16)
                store = pltpu.make_async_copy(bf16_buf, slab.at[_slot(x, y, sw)], sem)
                store.start()
                store.wait()

            pl.run_scoped(cast, pltpu.VMEM(w.shape, F32), pltpu.VMEM(w.shape, BF16), pltpu.SemaphoreType.DMA)

    return _Carry(shards, [_hbm_array((N_CHIPS,) + w.shape, BF16) for w in shards], False, [], start,
                  lambda ins, outs, sems: None)


def _pair_copies(ins, outs, send_sems, recv_sems):
    x, y, c = _mesh_pos()
    copies = []
    for k, g in enumerate(ins):
        half = g.shape[1] // 2
        copies.append(pltpu.make_async_remote_copy(
            src_ref=g.at[:, pl.ds((1 - c) * half, half), :], dst_ref=outs[k],
            send_sem=send_sems.at[k], recv_sem=recv_sems.at[k],
            device_id=(x, y, 1 - c), device_id_type=MESH))
    return copies


def _pair_carry(grads):
    n = len(grads)

    def start(ins, outs, sems):
        for cp in _pair_copies(ins, outs, *sems):
            cp.start()

    def finish(ins, outs, sems):
        for cp in _pair_copies(ins, outs, *sems):
            cp.wait()

    return _Carry(grads, [jax.ShapeDtypeStruct((N_CHIPS, g.shape[1] // 2, g.shape[2]), g.dtype) for g in grads], False,
                  [pltpu.SemaphoreType.DMA((n,)), pltpu.SemaphoreType.DMA((n,))], start, finish)


def _pair_exchange(grads, tag):
    carry = _pair_carry(grads)
    n = len(grads)

    def body(*refs):
        carry.start(refs[:n], refs[n:2 * n], refs[2 * n:])
        carry.finish(refs[:n], refs[n:2 * n], refs[2 * n:])

    return pl.pallas_call(
        body, name="grad_pair_exchange_" + tag, in_specs=_hbm_specs(n), out_specs=_hbm_specs(n),
        out_shape=carry.out_shapes, scratch_shapes=carry.sems,
    )(*grads)


def _pair_sum(grad, other, name, swapped):
    _, rows, cols = grad.shape
    half = rows // 2
    x, y, c = _mesh_pos()
    idx = jnp.stack([c, _slot(x, y, swapped)]).astype(jnp.int32)

    def body(idx_ref, g_ref, p_ref, own_ref, sb_ref):
        s = g_ref[...] + p_ref[...].astype(F32)
        sb_ref[...] = s.astype(BF16)

        @pl.when(pl.program_id(0) == idx_ref[1])
        def _():
            own_ref[...] = s

    blk = pl.BlockSpec((None, half, cols), lambda p, idx: (p, 0, 0))
    return pl.pallas_call(
        body, name=name,
        grid_spec=pltpu.PrefetchScalarGridSpec(
            num_scalar_prefetch=1, grid=(N_CHIPS,),
            in_specs=[pl.BlockSpec((None, half, cols), lambda p, idx: (p, idx[0], 0)), blk],
            out_specs=[pl.BlockSpec((half, cols), lambda p, idx: (0, 0)), blk]),
        out_shape=[jax.ShapeDtypeStruct((half, cols), F32), jax.ShapeDtypeStruct((N_CHIPS, half, cols), BF16)],
        compiler_params=pltpu.CompilerParams(dimension_semantics=("arbitrary",),
                                             vmem_limit_bytes=_vmem_limit(4 * half * cols * 4)),
    )(idx, grad, other)


def _chip_copies(sums_bf16, lands, send_sems, recv_sems, swapped):
    x, y, c = _mesh_pos()
    return [pltpu.make_async_remote_copy(
        src_ref=sums_bf16[k].at[_slot(px, py, swapped[k])], dst_ref=lands[k].at[r],
        send_sem=send_sems.at[3 * k + r], recv_sem=recv_sems.at[3 * k + r],
        device_id=(px, py, c), device_id_type=MESH)
        for k in range(len(sums_bf16)) for r, (px, py) in enumerate(_other_chips(x, y))]


def _chip_carry(sums_bf16, names):
    swapped = [k in SWAPPED for k in names]

    def start(ins, outs, sems):
        for cp in _chip_copies(ins, outs, *sems, swapped):
            cp.start()

    def finish(ins, outs, sems):
        for cp in _chip_copies(ins, outs, *sems, swapped):
            cp.wait()

    return _Carry(sums_bf16, _chip_landing(sums_bf16), False, _chip_sems(len(sums_bf16)), start, finish)


def _chip_sems(n):
    return [pltpu.SemaphoreType.DMA((3 * n,)), pltpu.SemaphoreType.DMA((3 * n,))]


def _chip_landing(sums_bf16):
    return [jax.ShapeDtypeStruct((N_CHIPS - 1,) + s.shape[1:], BF16) for s in sums_bf16]


def _chip_sum(own, landed, name):
    rows, cols = own.shape
    core = jnp.reshape(lax.axis_index("c"), (1,)).astype(jnp.int32)

    def body(core_ref, o_ref, l_ref, out_ref):
        out_ref[...] = ((o_ref[...] + l_ref[0].astype(F32)) + l_ref[1].astype(F32)) + l_ref[2].astype(F32)

    return pl.pallas_call(
        body, name=name,
        grid_spec=pltpu.PrefetchScalarGridSpec(
            num_scalar_prefetch=1, grid=(1,),
            in_specs=[pl.BlockSpec((rows, cols), lambda i, core_ref: (0, 0)),
                      pl.BlockSpec((N_CHIPS - 1, rows, cols), lambda i, core_ref: (0, 0, 0))],
            out_specs=pl.BlockSpec((rows, cols), lambda i, core_ref: (core_ref[0], 0))),
        out_shape=jax.ShapeDtypeStruct((2 * rows, cols), F32),
        compiler_params=pltpu.CompilerParams(vmem_limit_bytes=_vmem_limit(3 * rows * cols * 4)),
    )(core, own, landed)


def _halves_carry(fulls):
    n = len(fulls)

    def copies(outs, send_sems, recv_sems, own):
        x, y, c = _mesh_pos()
        res = []
        for k, out in enumerate(outs):
            half = out.shape[0] // 2
            rows = out.at[pl.ds((c if own else 1 - c) * half, half), :]
            res.append(pltpu.make_async_remote_copy(
                src_ref=rows, dst_ref=rows, send_sem=send_sems.at[k], recv_sem=recv_sems.at[k],
                device_id=(x, y, 1 - c), device_id_type=MESH))
        return res

    def start(ins, outs, sems):
        for cp in copies(outs, *sems, True):
            cp.start()

    def finish(ins, outs, sems):
        for cp in copies(outs, *sems, False):
            cp.wait_recv()
        for cp in copies(outs, *sems, True):
            cp.wait_send()

    return _Carry(fulls, [jax.ShapeDtypeStruct(f.shape, F32) for f in fulls], True,
                  [pltpu.SemaphoreType.DMA((n,)), pltpu.SemaphoreType.DMA((n,))], start, finish)


def _final_exchange(fulls, v):
    n = len(fulls)
    rows, cols = v.shape
    n_dev = 8

    def body(*refs):
        v_ref, out_ref = refs[0], refs[1 + 2 * n]
        outs = refs[1 + n:1 + 2 * n]
        buf, v_send, v_recv, h_send, h_recv = refs[2 + 2 * n:]
        x, y, c = _mesh_pos()
        me = 4 * x + 2 * y + c
        buf[me] = v_ref[...]
        peers = [(1 - x if r & 4 else x, 1 - y if r & 2 else y, 1 - c if r & 1 else c) for r in range(1, n_dev)]
        copies = []
        for r, peer in enumerate(peers):
            copies.append(pltpu.make_async_remote_copy(
                src_ref=v_ref, dst_ref=buf.at[me], send_sem=v_send.at[r], recv_sem=v_recv.at[r],
                device_id=peer, device_id_type=MESH))
        for k in range(n):
            half = fulls[k].shape[0] // 2
            mine = outs[k].at[pl.ds(c * half, half), :]
            copies.append(pltpu.make_async_remote_copy(
                src_ref=mine, dst_ref=mine, send_sem=h_send.at[k], recv_sem=h_recv.at[k],
                device_id=(x, y, 1 - c), device_id_type=MESH))
        for cp in copies:
            cp.start()
        for r, (px, py, pc) in enumerate(peers):
            pltpu.make_async_remote_copy(
                src_ref=v_ref, dst_ref=buf.at[4 * px + 2 * py + pc], send_sem=v_send.at[r], recv_sem=v_recv.at[r],
                device_id=(px, py, pc), device_id_type=MESH).wait_recv()
        for k in range(n):
            half = fulls[k].shape[0] // 2
            theirs = outs[k].at[pl.ds((1 - c) * half, half), :]
            pltpu.make_async_remote_copy(
                src_ref=theirs, dst_ref=theirs, send_sem=h_send.at[k], recv_sem=h_recv.at[k],
                device_id=(x, y, 1 - c), device_id_type=MESH).wait_recv()
        for cp in copies:
            cp.wait_send()
        acc = buf[0]
        for d in range(1, n_dev):
            acc = acc + buf[d]
        out_ref[...] = acc
        out_ref[3:4, :] = jnp.broadcast_to(jnp.sum(acc[3:4, :], axis=1, keepdims=True), (1, cols))

    vm = pl.BlockSpec(memory_space=pltpu.VMEM)
    res = pl.pallas_call(
        body, name="final_exchange",
        in_specs=[vm] + _hbm_specs(n), out_specs=_hbm_specs(n) + [vm],
        out_shape=[jax.ShapeDtypeStruct(f.shape, F32) for f in fulls] + [jax.ShapeDtypeStruct((rows, cols), F32)],
        input_output_aliases={1 + k: k for k in range(n)},
        scratch_shapes=[pltpu.VMEM((n_dev, rows, cols), F32),
                        pltpu.SemaphoreType.DMA((n_dev - 1,)), pltpu.SemaphoreType.DMA((n_dev - 1,)),
                        pltpu.SemaphoreType.DMA((n,)), pltpu.SemaphoreType.DMA((n,))],
    )(v, *fulls)
    return res[:n], res[n]


def _adamw_math(w, g, m, v):
    m = ADAM_B1 * m + (1.0 - ADAM_B1) * g
    v = ADAM_B2 * v + (1.0 - ADAM_B2) * (g * g)
    m_hat = m / (1.0 - ADAM_B1 ** ADAM_STEP)
    v_hat = v / (1.0 - ADAM_B2 ** ADAM_STEP)
    delta = -ADAM_LR * (m_hat / (jnp.sqrt(v_hat) + ADAM_EPS) + ADAM_WD * w)
    return delta, m, v


def _adamw(w, g, m, v, name):
    rows, cols = w.shape
    tm = rows // 2 if (rows // 2) % 8 == 0 else rows
    return _rowwise(_adamw_math, [w, g, m, v], [], [(cols, F32)] * 3, [], tm=tm, name=name)


def _unshard_cols(gathered):
    n, r, c = gathered.shape
    return jnp.transpose(gathered, (1, 0, 2)).reshape(r, n * c)


def _shard_cols(full):
    r, nc = full.shape
    return jnp.transpose(full.reshape(r, N_CHIPS, nc // N_CHIPS), (1, 0, 2))


LATE = ["w_sb_up", "w_dil_up", "w_out", "w_ffn_in", "w_ffn_out"]


def _late_weights(slabs, d_model, d_ff):
    g = dict(zip(LATE, slabs))
    return (_unshard_cols(g["w_sb_up"]), _unshard_cols(g["w_dil_up"]), g["w_out"].reshape(d_model, d_model),
            _unshard_cols(g["w_ffn_in"]), g["w_ffn_out"].reshape(d_ff, d_model))


ROW_SHARDED = ("w_in", "w_out", "w_ffn_in", "w_ffn_out")


def _chip_major(grads):
    out = []
    for k, g in grads.items():
        if k in ROW_SHARDED:
            out.append(g.reshape(N_CHIPS, g.shape[0] // N_CHIPS, g.shape[1]))
        else:
            out.append(_shard_cols(g))
    return out


def _pair_sums(full, others, names):
    return [_pair_sum(g, o, "grad_pair_sum_" + k, k in SWAPPED) for g, o, k in zip(full, others, names)]


def _chip_sums(pair, landed, names):
    return {k: _chip_sum(p[0], l, "grad_chip_sum_" + k) for p, l, k in zip(pair, landed, names)}


def _fwd_bwd(x, loss_target, g_mix, g_ffn, g_fin, slab_in, late_shards):
    b_sz, s_len, d_model = x.shape
    t = b_sz * s_len
    d_ff = late_shards[-1].shape[0] * N_CHIPS
    x2d = x.reshape(t, d_model)
    tgt2d = loss_target.reshape(t, d_model)

    u, (slab_in, *late_slabs) = _rowwise(
        lambda xv, g: (_rms_stats(xv)[0] * g,), [_in_hbm(x2d)], [g_mix], [(d_model, BF16)], [], tm=512, name="norm_mix",
        carry=_gather_carry([slab_in], ["w_in"]) + _cast_carry(late_shards, LATE), out_type=_hbm_array)
    u, wt_in = _in_hbm(u), _in_hbm(slab_in.reshape(-1, d_model))
    qkv, (slab_ffn_out,) = _mm(u, wt_in, tb=True, b_cols=(0, QKV_WIDTH), tm=2048, tn=768, tk=d_model, name="proj_qkv",
                               carry=_gather_carry(late_slabs[4:], LATE[4:]))
    gates = _mm(u, wt_in, tb=True, b_cols=(QKV_WIDTH, 2 * d_model), out_dtype=BF16, tm=t, tn=256, tk=d_model,
                name="proj_gates")
    qkv3 = qkv.reshape(b_sz, s_len, QKV_WIDTH)
    o_sb, (slab_ffn_in,) = _sb_fwd(qkv3, b_sz, s_len, _gather_carry(late_slabs[3:4], LATE[3:4]))
    o_dl, lse, small_slabs = _dil_fwd(qkv3, b_sz, s_len, _gather_carry(late_slabs[:3], LATE[:3]))
    wf_sb_up, wf_dil_up, wf_out, wf_ffn_in, wf_ffn_out = _late_weights(
        list(small_slabs) + [slab_ffn_in, slab_ffn_out], d_model, d_ff)
    o_sb2, o_dl2 = o_sb.reshape(t, SB_WIDTH), o_dl.reshape(t, DIL_OUT_WIDTH)
    y_sb = _mm(o_sb2, wf_sb_up, out_dtype=BF16, tm=1024, tn=1024, tk=SB_WIDTH, name="sb_up", out_type=_hbm_array)
    y_dl = _mm(o_dl2, wf_dil_up, out_dtype=BF16, tm=1024, tn=1024, tk=DIL_OUT_WIDTH, name="dil_up", out_type=_hbm_array)

    def merge_fn(gt, ys, yd):
        return (_sigmoid(gt[:, :d_model]) * ys + _sigmoid(gt[:, d_model:]) * yd,)

    (merged,) = _rowwise(merge_fn, [gates, y_sb, y_dl], [], [(d_model, BF16)], [], tm=512, name="merge")
    x1 = _mm(merged, wf_out, add=x2d, tm=512, tn=1024, tk=d_model, name="mix_out")
    (u2,) = _rowwise(lambda xv, g: (_rms_stats(xv)[0] * g,), [_in_hbm(x1)], [g_ffn], [(d_model, BF16)], [], tm=512, name="norm_ffn",
                     out_type=_hbm_array)
    u2 = _in_hbm(u2)
    half_ff = d_ff // 2

    def act_fn(hv):
        gate = hv[:, :half_ff]
        return hv, gate * _sigmoid(gate) * hv[:, half_ff:]

    h, act = _mm(u2, wf_ffn_in, tm=512, tn=d_ff, tk=d_model, name="ffn_in",
                 epilogue=(act_fn, [], [], [(d_ff, BF16), (half_ff, BF16)], []))
    def head_fn(xv, tg, g):
        xhat, r = _rms_stats(xv)
        err = xhat * g - tg
        dy = err * (1.0 / d_model)
        dx, dg_rows = _rms_bwd(dy, xhat, r, g)
        loss_lanes = (0.5 / d_model) * jnp.sum(err * err, axis=0, keepdims=True)
        return dx, dx, jnp.sum(dg_rows, axis=0, keepdims=True), loss_lanes

    dx2, dx2_b, dg_fin, loss_lanes = _mm(
        act, wf_ffn_out, add=x1, tm=512, tn=1024, tk=d_ff, name="ffn_out",
        epilogue=(head_fn, [tgt2d], [g_fin], [(d_model, F32), (d_model, BF16)], [(1, d_model), (1, d_model)]))

    def dact_fn(da, hv):
        gate, up = hv[:, :half_ff], hv[:, half_ff:]
        sg = _sigmoid(gate)
        dgate = da * up * (sg * (1.0 + gate * (1.0 - sg)))
        return (jnp.concatenate([dgate, da * (gate * sg)], axis=1),)

    dx2_b = _in_hbm(dx2_b)
    (dh,) = _mm(dx2_b, wf_ffn_out, tb=True, tm=512, tn=half_ff, tk=d_model, name="ffn_out_dx",
                epilogue=(dact_fn, [h], [], [(d_ff, BF16)], []))
    gw_ffn_out = _mm(act, dx2_b, ta=True, tm=256, tn=d_model, tk=t, name="ffn_out_dw")
    def norm_bwd_fn(du_, dres, xv, g):
        xhat, r = _rms_stats(xv)
        dx, dg_rows = _rms_bwd(du_, xhat, r, g)
        return dres + dx, jnp.sum(dg_rows, axis=0, keepdims=True)

    def norm_bwd_twice(*args):
        dx, dg = norm_bwd_fn(*args)
        return dx, dx, dg

    dx1, dx1_b, dg_ffn = _mm(dh, wf_ffn_in, tb=True, tm=512, tn=1024, tk=2 * d_ff, name="ffn_in_dx",
                             epilogue=(norm_bwd_twice, [dx2, x1], [g_ffn], [(d_model, F32), (d_model, BF16)], [(1, d_model)]))
    gwt_ffn_in = _mm(dh, u2, ta=True, tm=512, tn=d_model, tk=t, name="ffn_in_dw")

    dx1_b = _in_hbm(dx1_b)
    dmerged = _mm(dx1_b, wf_out, tb=True, out_dtype=BF16, tm=512, tn=1024, tk=d_model, name="mix_out_dx",
                  out_type=_hbm_array)
    gw_out = _mm(merged, dx1_b, ta=True, tm=256, tn=d_model, tk=t, name="mix_out_dw")

    def merge_bwd_fn(gt, ys, yd, dm):
        s_sb, s_dl = _sigmoid(gt[:, :d_model]), _sigmoid(gt[:, d_model:])
        dgates = jnp.concatenate([dm * ys * s_sb * (1.0 - s_sb), dm * yd * s_dl * (1.0 - s_dl)], axis=1)
        return dgates, dm * s_sb, dm * s_dl

    full_big = _chip_major({"w_out": gw_out})
    dgates, dy_sb, dy_dl, others_big = _rowwise(
        merge_bwd_fn, [gates, y_sb, y_dl, dmerged], [], [(2 * d_model, BF16), (d_model, BF16), (d_model, BF16)], [],
        tm=256, name="merge_bwd", carry=_pair_carry(full_big))
    pair_big = _pair_sums(full_big, others_big, LATE[2:3])
    do_sb = _mm(dy_sb, wf_sb_up, tb=True, out_dtype=BF16, tm=1024, tn=SB_WIDTH, tk=d_model, name="sb_up_dx")
    gw_sb_up = _mm(o_sb2, dy_sb, ta=True, tm=SB_WIDTH, tn=1024, tk=512, name="sb_up_dw")
    do_dl = _mm(dy_dl, wf_dil_up, tb=True, tm=1024, tn=DIL_OUT_WIDTH, tk=d_model, name="dil_up_dx")
    gw_dil_up = _mm(o_dl2, dy_dl, ta=True, tm=DIL_OUT_WIDTH, tn=1024, tk=512, name="dil_up_dw")
    rest = [LATE[0], LATE[1], LATE[3], LATE[4]]
    full_rest = _chip_major({"w_sb_up": gw_sb_up, "w_dil_up": gw_dil_up, "w_ffn_in": gwt_ffn_in, "w_ffn_out": gw_ffn_out})
    (dq_sb, dk_sb, dv_sb), brought = _sb_bwd(
        qkv3, o_sb, do_sb.reshape(b_sz, s_len, SB_WIDTH), b_sz, s_len,
        _chip_carry([p[1] for p in pair_big], LATE[2:3]) + _pair_carry(full_rest))
    pair_rest = _pair_sums(full_rest, brought[1:], rest)
    (dq_dl, dk_dl, dv_dl), landed_b = _dil_bwd(
        qkv3, o_dl, lse, do_dl.reshape(b_sz, s_len, DIL_OUT_WIDTH), b_sz, s_len,
        _chip_carry([p[1] for p in pair_rest], rest))
    pair = pair_rest[:2] + pair_big + pair_rest[2:]
    landed = [landed_b[0], landed_b[1], brought[0], landed_b[2], landed_b[3]]
    dproj = [a.reshape(t, -1) for a in (dq_sb, dk_sb, dv_sb, dq_dl, dk_dl, dv_dl)] + [dgates]
    gwt_in, gwt_in_b = _mm(dproj, u, ta=True, tm=256, tn=d_model, tk=t, name="proj_dw",
                           epilogue=(lambda tile: (tile, tile), [], [], [(d_model, F32), (d_model, BF16)], []))
    full_in = _chip_major({"w_in": gwt_in})
    pair_in = _pair_sums(full_in, _pair_exchange(_chip_major({"w_in": gwt_in_b}), "w_in"), ["w_in"])
    late_halves = _chip_sums(pair, landed, LATE)
    (dx, dg_mix), brought_in = _mm(
        dproj, wt_in, tm=512, tn=1024, tk=wt_in.shape[0], name="proj_dx",
        carry=_halves_carry([late_halves[k] for k in LATE]) + _chip_carry([p[1] for p in pair_in], ["w_in"]),
        epilogue=(norm_bwd_fn, [dx1, x2d], [g_mix], [(d_model, F32)], [(1, d_model)]))

    grads = dict(zip(LATE, brought_in[:len(LATE)]))
    grads.update(_chip_sums(pair_in, brought_in[len(LATE):], ["w_in"]))
    return dx, grads, dg_mix, dg_ffn, dg_fin, loss_lanes


def kernel(x, norm_mix_g, w_in, w_sb_up, w_dil_up, w_out, norm_ffn_g, w_ffn_in, w_ffn_out, norm_final_g, loss_target, m_norm_mix_g, m_w_in, m_w_sb_up, m_w_dil_up, m_w_out, m_norm_ffn_g, m_w_ffn_in, m_w_ffn_out, m_norm_final_g, v_norm_mix_g, v_w_in, v_w_sb_up, v_w_dil_up, v_w_out, v_norm_ffn_g, v_w_ffn_in, v_w_ffn_out, v_norm_final_g):
    b_sz, s_len, d_model = x.shape
    d_ff = w_ffn_out.shape[1] * N_CHIPS
    g_mix, g_ffn, g_fin = norm_mix_g, norm_ffn_g, norm_final_g.reshape(1, d_model)

    names = ["w_in", "w_sb_up", "w_dil_up", "w_out", "w_ffn_in", "w_ffn_out"]
    shards = {"w_in": jnp.swapaxes(w_in[0], 0, 1), "w_sb_up": w_sb_up[0], "w_dil_up": w_dil_up[0], "w_out": w_out[0],
              "w_ffn_in": w_ffn_in[0], "w_ffn_out": w_ffn_out[0]}
    slab_in = _cast_to_slab(shards["w_in"], "cast_w_in")

    dx, grads, dg_mix, dg_ffn, dg_fin, loss_lanes = _fwd_bwd(
        x, loss_target, g_mix, g_ffn, g_fin, slab_in, [shards[k] for k in LATE])

    small = jnp.concatenate([dg_mix, dg_ffn, dg_fin, loss_lanes, jnp.zeros((4, d_model), F32)], axis=0)
    (grads["w_in"],), small = _final_exchange([grads["w_in"]], small)
    grads["w_ffn_in"] = jnp.swapaxes(grads["w_ffn_in"], 0, 1)
    loss = small[3, 0]
    gains = jnp.concatenate([g_mix, g_ffn, g_fin, jnp.zeros((5, d_model), F32)], axis=0)
    gains_m = jnp.concatenate([m_norm_mix_g, m_norm_ffn_g, m_norm_final_g.reshape(1, d_model), jnp.zeros((5, d_model), F32)], axis=0)
    gains_v = jnp.concatenate([v_norm_mix_g, v_norm_ffn_g, v_norm_final_g.reshape(1, d_model), jnp.ones((5, d_model), F32)], axis=0)
    gd, gm, gv = _rowwise(_adamw_math, [gains, small, gains_m, gains_v], [], [(d_model, F32)] * 3, [], tm=8, name="adamw_gains")

    moments = {"w_in": (jnp.swapaxes(m_w_in[0], 0, 1), jnp.swapaxes(v_w_in[0], 0, 1)),
               "w_sb_up": (m_w_sb_up[0], v_w_sb_up[0]), "w_dil_up": (m_w_dil_up[0], v_w_dil_up[0]),
               "w_out": (m_w_out[0], v_w_out[0]), "w_ffn_in": (m_w_ffn_in[0], v_w_ffn_in[0]),
               "w_ffn_out": (m_w_ffn_out[0], v_w_ffn_out[0])}
    upd = {k: _adamw(shards[k], grads[k], moments[k][0], moments[k][1], "adamw_" + k) for k in names}

    def as_output(k, a):
        return (jnp.swapaxes(a, 0, 1) if k == "w_in" else a)[None]

    def w_out_of(i):
        return [as_output(k, upd[k][i]) for k in names]

    def ordered(mix, ws, ffn_g, fin):
        return [mix, ws[0], ws[1], ws[2], ws[3], ffn_g, ws[4], ws[5], fin]

    grad_ws = [as_output(k, grads[k]) for k in names]
    outs = [loss, dx.reshape(b_sz, s_len, d_model)]
    outs += ordered(small[0:1], grad_ws, small[1:2], small[2])
    outs += ordered(gd[0:1], w_out_of(0), gd[1:2], gd[2])
    outs += ordered(gm[0:1], w_out_of(1), gm[1:2], gm[2])
    outs += ordered(gv[0:1], w_out_of(2), gv[1:2], gv[2])
    return tuple(outs)
```

```python
import functools
import math

import jax
import jax.numpy as jnp
from jax import lax
from jax.experimental import pallas as pl
from jax.experimental.pallas import tpu as pltpu

F32 = jnp.float32
BF16 = jnp.bfloat16
MESH = pl.DeviceIdType.MESH

HEAD_DIM = 64
SB_HEADS = 8
DIL_PAIRS = ((128, 1), (512, 4), (2048, 16))
DIL_HEADS_PER_GROUP = 4
DIL_HEADS = DIL_HEADS_PER_GROUP * len(DIL_PAIRS)
SB_WIDTH = SB_HEADS * HEAD_DIM
DIL_WIDTH = DIL_HEADS * HEAD_DIM
DIL_OUT_WIDTH = DIL_HEADS_PER_GROUP * HEAD_DIM
QKV_WIDTH = 3 * SB_WIDTH + 3 * DIL_WIDTH
RMS_EPS = 1e-6
ALIBI_MAX_BIAS = 8.0
ADAM_LR = 0.001
ADAM_B1 = 0.9
ADAM_B2 = 0.999
ADAM_EPS = 1e-08
ADAM_WD = 0.01
ADAM_STEP = 10

LANES = 128
BLK = 128
NEG = -1e30
EXP_UNDERFLOW = -104.0
SB_FWD_CHAINS = 4
SB_BWD_CHAINS = 4
DIL_CHAINS = 4
N_CHIPS = 4
VMEM_CAP = 56 * 1024 * 1024


def _vmem_limit(tile_bytes):
    return int(min(VMEM_CAP, max(16 * 1024 * 1024, 3 * tile_bytes + 8 * 1024 * 1024)))


def _hbm_array(shape, dtype):
    return pltpu.HBM(shape, dtype)


def _nbytes(shape, dtype):
    return math.prod(shape) * jnp.dtype(dtype).itemsize


def _in_hbm(x):
    return pltpu.with_memory_space_constraint(x, pltpu.HBM)


def _dot(a, b):
    return jnp.dot(a, b, preferred_element_type=F32)


def _dot_nt(a, b):
    return lax.dot_general(a, b, (((1,), (1,)), ((), ())), preferred_element_type=F32)


def _dot_tn(a, b):
    return lax.dot_general(a, b, (((0,), (0,)), ((), ())), preferred_element_type=F32)


def _split2(x):
    hi = x.astype(BF16)
    lo = (x - hi.astype(F32)).astype(BF16)
    return hi, lo


def _sigmoid(x):
    return pl.reciprocal(1.0 + jnp.exp(-x), approx=True)


class _Carry:
    def __init__(self, arrays=(), out_shapes=(), aliased=False, sems=(), start=None, finish=None):
        self.arrays, self.out_shapes = list(arrays), list(out_shapes)
        self.n_aliased = len(self.arrays) if aliased is True else int(aliased)
        self.sems, self.start, self.finish = list(sems), start, finish

    def __bool__(self):
        return bool(self.arrays)

    def __add__(self, other):
        assert not other.n_aliased and self.n_aliased in (0, len(self.out_shapes))
        n_a, n_o, n_s = len(self.arrays), len(self.out_shapes), len(self.sems)
        return _Carry(
            self.arrays + other.arrays, self.out_shapes + other.out_shapes, self.n_aliased, self.sems + other.sems,
            lambda i, o, s: (self.start(i[:n_a], o[:n_o], s[:n_s]), other.start(i[n_a:], o[n_o:], s[n_s:])),
            lambda i, o, s: (self.finish(i[:n_a], o[:n_o], s[:n_s]), other.finish(i[n_a:], o[n_o:], s[n_s:])))

    def call_args(self, n_in, n_out):
        aliases = {n_in + k: n_out + k for k in range(self.n_aliased)}
        return _hbm_specs(len(self.arrays)), _hbm_specs(len(self.out_shapes)), self.out_shapes, aliases, self.sems

    def run(self, refs, n_in, n_out, step, n_steps, compute):
        if not self:
            compute()
            return
        n_c, n_o, n_s = len(self.arrays), len(self.out_shapes), len(self.sems)
        ins = refs[n_in:n_in + n_c]
        outs = refs[n_in + n_c + n_out:n_in + n_c + n_out + n_o]
        sems = refs[len(refs) - n_s:]

        @pl.when(step == 0)
        def _():
            self.start(ins, outs, sems)

        compute()

        @pl.when(step == n_steps - 1)
        def _():
            self.finish(ins, outs, sems)


def _mm(a, b, *, ta=False, tb=False, add=None, out_dtype=F32, tm, tn, tk, name, carry=None, epilogue=None,
        b_cols=None, out_type=jax.ShapeDtypeStruct):
    carry = carry or _Carry()
    n_car = len(carry.arrays)
    pieces = list(a) if isinstance(a, (list, tuple)) else [a]
    n_a = len(pieces)
    widths = [p.shape[1] for p in pieces]
    starts = [sum(widths[:p]) for p in range(n_a)]
    if ta:
        kdim, m = pieces[0].shape[0], sum(widths)
    else:
        m, kdim = pieces[0].shape[0], sum(widths)
    if tb:
        n, k2 = b.shape
    else:
        k2, n = b.shape
    col0 = 0
    if b_cols is not None:
        assert b_cols[0] % tn == 0, name
        col0, n = b_cols[0] // tn, b_cols[1]
    assert kdim == k2 and m % tm == 0 and n % tn == 0 and kdim % tk == 0, (name, a.shape, b.shape)
    nk = kdim // tk
    assert n_a == 1 or (nk == 1 and not tb and (not ta or all(w % tm == 0 for w in widths))), name
    grid = (m // tm, n // tn, nk)
    a_mode = dict(pipeline_mode=pl.Buffered(1)) if grid[0] == 1 and nk == 1 else {}
    b_mode = dict(pipeline_mode=pl.Buffered(1)) if grid[1] == 1 and nk == 1 else {}
    if n_a == 1:
        a_specs = [pl.BlockSpec((tk, tm), lambda i, j, k: (k, i), **a_mode) if ta
                   else pl.BlockSpec((tm, tk), lambda i, j, k: (i, k), **a_mode)]
    elif ta:
        a_specs = [pl.BlockSpec((tk, tm), lambda i, j, k, s=s // tm, w=w // tm: (0, jnp.clip(i - s, 0, w - 1)))
                   for s, w in zip(starts, widths)]
    else:
        a_specs = [pl.BlockSpec((tm, w), lambda i, j, k: (i, 0)) for w in widths]
    b_spec = (pl.BlockSpec((tn, tk), lambda i, j, k: (j + col0, k), **b_mode) if tb
              else pl.BlockSpec((tk, tn), lambda i, j, k: (k, j + col0), **b_mode))
    o_spec = pl.BlockSpec((tm, tn), lambda i, j, k: (i, j))
    dims = ((((0,) if ta else (1,)), ((1,) if tb else (0,))), ((), ()))
    has_add = add is not None
    if epilogue is None:
        ep_fn, ep_rows, ep_params, ep_outs, ep_accs = None, [], [], [], []
        out_sds, out_specs = [out_type((m, n), out_dtype)], [o_spec]
    else:
        ep_fn, ep_rows, ep_params, ep_outs, ep_accs = epilogue
        assert grid[1] == 1 or not ep_accs, name
        out_sds = [out_type((m, w * grid[1]), d) for w, d in ep_outs]
        out_sds += [jax.ShapeDtypeStruct(sh, F32) for sh in ep_accs]
        out_specs = [pl.BlockSpec((tm, w), lambda i, j, k: (i, j)) for w, _ in ep_outs]
        out_specs += [pl.BlockSpec(sh, lambda i, j, k: (0, 0)) for sh in ep_accs]
    n_main = len(out_sds)
    use_scratch = nk > 1 and (ep_fn is not None or jnp.dtype(out_dtype) != jnp.dtype(F32))
    n_in = n_a + 1 + has_add + len(ep_rows) + len(ep_params)

    def finish(total, refs, pid):
        outs = refs[n_in + n_car:n_in + n_car + n_main]
        if ep_fn is None:
            outs[0][...] = total.astype(out_dtype)
            return
        first = n_a + 1 + has_add
        rows = [r[...].astype(F32) for r in refs[first:first + len(ep_rows)]]
        params = [p[...] for p in refs[first + len(ep_rows):n_in]]
        res = ep_fn(total, *rows, *params)
        for o_ref, v in zip(outs[:len(ep_outs)], res):
            o_ref[...] = v.astype(o_ref.dtype)
        acc_refs = outs[len(ep_outs):]
        if acc_refs:
            @pl.when(pid[0] == 0)
            def _():
                for r in acc_refs:
                    r[...] = jnp.zeros(r.shape, F32)

            for r, v in zip(acc_refs, res[len(ep_outs):]):
                r[...] += v

    def compute(refs, pid):
        a_ref, b_ref = refs[0], refs[n_a]
        add_ref = refs[n_a + 1] if has_add else None

        def dot(x, y):
            return lax.dot_general(x.astype(BF16), y.astype(BF16), dims, preferred_element_type=F32)

        if n_a > 1 and ta:
            for p_ref, s, w in zip(refs[:n_a], starts, widths):
                @pl.when((pid[0] >= s // tm) & (pid[0] < (s + w) // tm))
                def _(p_ref=p_ref):
                    prod = dot(p_ref[...], b_ref[...])
                    finish(prod + add_ref[...] if has_add else prod, refs, pid)
            return
        if n_a > 1:
            prod = dot(a_ref[...], b_ref[:widths[0], :])
            for p_ref, s, w in zip(refs[1:n_a], starts[1:], widths[1:]):
                prod += dot(p_ref[...], b_ref[s:s + w, :])
        else:
            prod = dot(a_ref[...], b_ref[...])
        if nk == 1:
            finish(prod + add_ref[...] if has_add else prod, refs, pid)
            return
        acc_ref = refs[n_in + n_car + n_main + len(carry.out_shapes)] if use_scratch else refs[n_in + n_car]
        k = pid[2]

        @pl.when(k == 0)
        def _():
            acc_ref[...] = prod + add_ref[...] if has_add else prod

        @pl.when(k > 0)
        def _():
            acc_ref[...] += prod

        if use_scratch:
            @pl.when(k == nk - 1)
            def _():
                finish(acc_ref[...], refs, pid)

    def body(*refs):
        pid = (pl.program_id(0), pl.program_id(1), pl.program_id(2))
        step = (pid[0] * grid[1] + pid[1]) * nk + pid[2]
        carry.run(refs, n_in, n_main, step, grid[0] * grid[1] * nk, lambda: compute(refs, pid))

    tile_bytes = ((n_a if ta else 1) * _nbytes((tm, tk), pieces[0].dtype)
                  + _nbytes((tk, tn), b.dtype) + 2 * _nbytes((tm, tn), F32)
                  + (_nbytes((tm, tn), F32) if has_add else 0)
                  + sum(_nbytes((tm, r.shape[1]), r.dtype) for r in ep_rows) + sum(_nbytes((tm, w), d) for w, d in ep_outs))
    in_specs = a_specs + [b_spec] + ([o_spec] if has_add else [])
    in_specs += [pl.BlockSpec((tm, r.shape[1] // grid[1]), lambda i, j, k: (i, j)) for r in ep_rows]
    in_specs += [pl.BlockSpec(p.shape, lambda i, j, k: (0, 0)) for p in ep_params]
    args = tuple(pieces) + (b,) + ((add,) if has_add else ()) + tuple(ep_rows) + tuple(ep_params)
    scratch = [pltpu.VMEM((tm, tn), F32)] if use_scratch else []
    serial = bool(carry) or bool(ep_accs)
    c_in, c_out, c_shapes, c_alias, c_sems = carry.call_args(n_in, n_main)
    res = pl.pallas_call(
        body, name=name, grid=grid,
        in_specs=in_specs + c_in, out_specs=out_specs + c_out, out_shape=out_sds + c_shapes,
        input_output_aliases=c_alias, scratch_shapes=scratch + c_sems,
        compiler_params=pltpu.CompilerParams(
            dimension_semantics=("arbitrary",) * 3 if serial else ("parallel", "parallel", "arbitrary"),
            vmem_limit_bytes=_vmem_limit(tile_bytes)),
    )(*args, *carry.arrays)
    main = res[0] if ep_fn is None else list(res[:n_main])
    return (main, res[n_main:]) if carry else main


def _rowwise(fn, rows, params, outs, accs, *, tm, name, carry=None, out_type=jax.ShapeDtypeStruct):
    carry = carry or _Carry()
    t = rows[0].shape[0]
    assert t % tm == 0, (name, t, tm)
    n_r, n_p, n_o, n_c = len(rows), len(params), len(outs), len(carry.arrays)

    def compute(refs, first):
        vals = [r[...].astype(F32) for r in refs[:n_r]] + [p[...] for p in refs[n_r:n_r + n_p]]
        res = fn(*vals)
        o_refs = refs[n_r + n_p + n_c:n_r + n_p + n_c + n_o]
        a_refs = refs[n_r + n_p + n_c + n_o:n_r + n_p + n_c + n_o + len(accs)]
        for o_ref, v in zip(o_refs, res[:n_o]):
            o_ref[...] = v.astype(o_ref.dtype)
        if accs:
            @pl.when(first)
            def _():
                for a_ref in a_refs:
                    a_ref[...] = jnp.zeros(a_ref.shape, F32)

            for a_ref, v in zip(a_refs, res[n_o:]):
                a_ref[...] += v

    def body(*refs):
        step = pl.program_id(0)
        carry.run(refs, n_r + n_p, n_o + len(accs), step, t // tm, lambda: compute(refs, step == 0))

    in_specs = [pl.BlockSpec((tm, r.shape[1]), lambda i: (i, 0)) for r in rows]
    in_specs += [pl.BlockSpec(p.shape, lambda i: (0, 0)) for p in params]
    out_specs = [pl.BlockSpec((tm, w), lambda i: (i, 0)) for w, _ in outs]
    out_specs += [pl.BlockSpec(s, lambda i: (0, 0)) for s in accs]
    out_shape = [out_type((t, w), d) for w, d in outs]
    out_shape += [jax.ShapeDtypeStruct(s, F32) for s in accs]
    tile_bytes = sum(_nbytes((tm, r.shape[1]), r.dtype) for r in rows) + sum(_nbytes((tm, w), F32) for w, _ in outs)
    c_in, c_out, c_shapes, c_alias, c_sems = carry.call_args(n_r + n_p, n_o + len(accs))
    res = pl.pallas_call(
        body, name=name, grid=(t // tm,), in_specs=in_specs + c_in, out_specs=out_specs + c_out,
        out_shape=out_shape + c_shapes, input_output_aliases=c_alias, scratch_shapes=c_sems,
        compiler_params=pltpu.CompilerParams(
            dimension_semantics=("arbitrary",) if accs or carry else ("parallel",),
            vmem_limit_bytes=_vmem_limit(2 * tile_bytes)),
    )(*rows, *params, *carry.arrays)
    own = n_o + len(accs)
    return (list(res[:own]) + [res[own:]]) if carry else res


def _rms_stats(x):
    r = lax.rsqrt(jnp.mean(x * x, axis=-1, keepdims=True) + RMS_EPS)
    return x * r, r


def _rms_bwd(dy, xhat, r, g):
    dxhat = dy * g
    dx = r * (dxhat - xhat * jnp.mean(dxhat * xhat, axis=-1, keepdims=True))
    return dx, dy * xhat


def _sb_consts():
    lane = lax.broadcasted_iota(jnp.int32, (BLK, LANES), 1)
    head0 = lane < HEAD_DIM
    row = lax.broadcasted_iota(jnp.int32, (2 * BLK, BLK), 0) % BLK
    col = lax.broadcasted_iota(jnp.int32, (2 * BLK, BLK), 1)
    causal = col < row
    jj = lax.broadcasted_iota(jnp.int32, (BLK, BLK), 0)
    ss = lax.broadcasted_iota(jnp.int32, (BLK, BLK), 1)
    suffix = jnp.where(jj > ss, 1.0, 0.0).astype(BF16)
    return head0, causal, suffix


def _stack_heads(x, head0):
    zero = jnp.zeros_like(x)
    return jnp.concatenate([jnp.where(head0, x, zero), jnp.where(head0, zero, x)], axis=0)


def _sb_logits(z, causal, masked):
    sp = jnp.log(1.0 + jnp.exp(-jnp.abs(z)))
    log_keep = -(jnp.maximum(z, 0.0) + sp)
    log_beta = jnp.minimum(z, 0.0) - sp
    if masked:
        log_keep = jnp.where(causal, log_keep, 0.0)
    return log_keep, log_beta


def _suffix_sums(x, suffix):
    hi, lo = _split2(x)
    after = _dot(hi, suffix) + _dot(lo, suffix)
    total = jnp.broadcast_to(after[:, 0:1] + x[:, 0:1], x.shape)
    return after, total


def _sb_walk_back(i, state, per_chain, tile):
    def alive(st):
        worst = functools.reduce(jnp.maximum, [st[p][:, 0:1] for p in range(0, len(st), per_chain)])
        return jnp.max(worst) > EXP_UNDERFLOW

    def cond(c):
        return jnp.logical_and(c[0] < i, alive(c[1]))

    def body(c):
        return c[0] + 1, tile(i - 1 - c[0], c[1], False)

    return lax.while_loop(cond, body, (jnp.int32(0), state))[1]


def _lane_blocks(x, n):
    return [x[:, p * LANES:(p + 1) * LANES] for p in range(n)]


def _sb_fwd(qkv, b_sz, s_len, carry):
    nq = s_len // BLK
    n_pairs = SB_WIDTH // LANES
    ch = SB_FWD_CHAINS
    n_steps = n_pairs // ch
    scale = 1.0 / math.sqrt(HEAD_DIM)

    def compute(q_ref, k_ref, v_ref, o_ref):
        head0, causal, suffix = _sb_consts()

        def q_block(i, _):
            qs = pl.multiple_of(i * BLK, BLK)
            q_all = (q_ref[pl.ds(qs, BLK), :] * scale).astype(BF16)
            q01 = [_stack_heads(q, head0) for q in _lane_blocks(q_all, ch)]

            def tile(j, state, masked):
                ks = pl.multiple_of(j * BLK, BLK)
                ks_ = _lane_blocks(k_ref[pl.ds(ks, BLK), :].astype(BF16), ch)
                vs_ = _lane_blocks(v_ref[pl.ds(ks, BLK), :].astype(BF16), ch)
                zs = [_dot_nt(q01[p], ks_[p]) for p in range(ch)]
                logits = [_sb_logits(z, causal, masked) for z in zs]
                sums = [_suffix_sums(lg[0], suffix) for lg in logits]
                out = []
                for p in range(ch):
                    carry, acc = state[2 * p], state[2 * p + 1]
                    after, total = sums[p]
                    a = jnp.exp(logits[p][1] + carry + after)
                    if masked:
                        a = jnp.where(causal, a, 0.0)
                    a_hi, a_lo = _split2(a)
                    a_cat = jnp.concatenate([a_hi[:BLK], a_hi[BLK:], a_lo[:BLK], a_lo[BLK:]], axis=1)
                    v01 = _stack_heads(vs_[p], head0)
                    out += [carry + total, acc + _dot(a_cat, jnp.concatenate([v01, v01], axis=0))]
                return tuple(out)

            state = (jnp.zeros((2 * BLK, BLK), F32), jnp.zeros((BLK, LANES), F32)) * ch
            state = tile(i, state, True)
            state = _sb_walk_back(i, state, 2, tile)
            o_ref[pl.ds(qs, BLK), :] = jnp.concatenate([state[2 * p + 1] for p in range(ch)], axis=1)
            return 0

        lax.fori_loop(0, nq, q_block, 0)

    def body(*refs):
        step = pl.program_id(0) * n_steps + pl.program_id(1)
        o_ref = refs[3 + len(carry.arrays)]
        carry.run(refs, 3, 1, step, b_sz * n_steps, lambda: compute(refs[0], refs[1], refs[2], o_ref))

    blk = lambda off: pl.BlockSpec((None, s_len, ch * LANES), lambda b, p: (b, 0, off + p))
    c_in, c_out, c_shapes, c_alias, c_sems = carry.call_args(3, 1)
    res = pl.pallas_call(
        body, name="sb_fwd", grid=(b_sz, n_steps),
        in_specs=[blk(0), blk(n_steps), blk(2 * n_steps)] + c_in, out_specs=[blk(0)] + c_out,
        out_shape=[jax.ShapeDtypeStruct((b_sz, s_len, SB_WIDTH), F32)] + c_shapes,
        input_output_aliases=c_alias, scratch_shapes=c_sems,
        compiler_params=pltpu.CompilerParams(dimension_semantics=("arbitrary", "arbitrary"),
                                             vmem_limit_bytes=VMEM_CAP),
    )(qkv, qkv, qkv, *carry.arrays)
    return res[0], res[1:]


def _sb_bwd(qkv, o_sb, do_sb, b_sz, s_len, carry):
    nq = s_len // BLK
    n_pairs = SB_WIDTH // LANES
    ch = SB_BWD_CHAINS
    n_steps = n_pairs // ch
    scale = 1.0 / math.sqrt(HEAD_DIM)

    def compute(q_ref, k_ref, v_ref, o_ref, do_ref, dq_ref, dk_ref, dv_ref, dk_acc, dv_acc):
        head0, causal, suffix = _sb_consts()
        lrow = lax.broadcasted_iota(jnp.int32, (LANES, LANES), 0)
        ones_h0 = jnp.where(lrow < HEAD_DIM, 1.0, 0.0).astype(BF16)
        ones_h1 = jnp.where(lrow >= HEAD_DIM, 1.0, 0.0).astype(BF16)
        dk_acc[...] = jnp.zeros(dk_acc.shape, F32)
        dv_acc[...] = jnp.zeros(dv_acc.shape, F32)

        def q_block(i, _):
            qs = pl.multiple_of(i * BLK, BLK)
            q_all = (q_ref[pl.ds(qs, BLK), :] * scale).astype(BF16)
            do_all = do_ref[pl.ds(qs, BLK), :].astype(BF16)
            dd_all = do_all.astype(F32) * o_ref[pl.ds(qs, BLK), :]
            q01 = [_stack_heads(q, head0) for q in _lane_blocks(q_all, ch)]
            do01 = [_stack_heads(d, head0) for d in _lane_blocks(do_all, ch)]
            tot = []
            for dd in _lane_blocks(dd_all, ch):
                dd_hi, dd_lo = _split2(dd)
                tot.append(jnp.concatenate([_dot(dd_hi, ones_h0) + _dot(dd_lo, ones_h0),
                                            _dot(dd_hi, ones_h1) + _dot(dd_lo, ones_h1)], axis=0))

            def tile(j, state, masked):
                ks = pl.multiple_of(j * BLK, BLK)
                ks_ = _lane_blocks(k_ref[pl.ds(ks, BLK), :].astype(BF16), ch)
                vs_ = _lane_blocks(v_ref[pl.ds(ks, BLK), :].astype(BF16), ch)
                zs = [_dot_nt(q01[p], ks_[p]) for p in range(ch)]
                das = [_dot_nt(do01[p], vs_[p]) for p in range(ch)]
                logits = [_sb_logits(z, causal, masked) for z in zs]
                sums = [_suffix_sums(lg[0], suffix) for lg in logits]
                a_s, e_s = [], []
                for p in range(ch):
                    a = jnp.exp(logits[p][1] + state[3 * p] + sums[p][0])
                    if masked:
                        a = jnp.where(causal, a, 0.0)
                    a_s.append(a)
                    e_s.append(a * das[p])
                e_sums = [_suffix_sums(e, suffix) for e in e_s]
                out, dks, dvs = [], [], []
                for p in range(ch):
                    carry, rcarry, dq = state[3 * p:3 * p + 3]
                    e = e_s[p]
                    before = tot[p] - (rcarry + e_sums[p][0] + e)
                    beta = jnp.exp(logits[p][1])
                    dz = e * (1.0 - beta) - beta * before
                    if masked:
                        dz = jnp.where(causal, dz, 0.0)
                    dz_b = dz.astype(BF16)
                    dks.append(_dot_tn(dz_b, q01[p]))
                    dvs.append(_dot_tn(a_s[p].astype(BF16), do01[p]))
                    out += [carry + sums[p][1], rcarry + e_sums[p][1], dq + _dot(dz_b, ks_[p])]
                dk_acc[pl.ds(ks, BLK), :] += jnp.concatenate(dks, axis=1)
                dv_acc[pl.ds(ks, BLK), :] += jnp.concatenate(dvs, axis=1)
                return tuple(out)

            state = (jnp.zeros((2 * BLK, BLK), F32),) * (3 * ch)
            state = tile(i, state, True)
            state = _sb_walk_back(i, state, 3, tile)
            dq = [jnp.where(head0, state[3 * p + 2][:BLK], state[3 * p + 2][BLK:]) for p in range(ch)]
            dq_ref[pl.ds(qs, BLK), :] = (jnp.concatenate(dq, axis=1) * scale).astype(dq_ref.dtype)
            return 0

        lax.fori_loop(0, nq, q_block, 0)
        dk_ref[...] = dk_acc[...].astype(dk_ref.dtype)
        dv_ref[...] = dv_acc[...].astype(dv_ref.dtype)

    def body(*refs):
        step = pl.program_id(0) * n_steps + pl.program_id(1)
        n_c, n_o = len(carry.arrays), len(carry.out_shapes)
        own = refs[:5] + refs[5 + n_c:8 + n_c] + refs[8 + n_c + n_o:10 + n_c + n_o]
        carry.run(refs, 5, 3, step, b_sz * n_steps, lambda: compute(*own))

    blk = lambda off: pl.BlockSpec((None, s_len, ch * LANES), lambda b, p: (b, 0, off + p))
    once = lambda off: pl.BlockSpec((None, s_len, ch * LANES), lambda b, p: (b, 0, off + p),
                                    pipeline_mode=pl.Buffered(1))
    out_sd = jax.ShapeDtypeStruct((b_sz, s_len, SB_WIDTH), BF16)
    c_in, c_out, c_shapes, c_alias, c_sems = carry.call_args(5, 3)
    res = pl.pallas_call(
        body, name="sb_bwd", grid=(b_sz, n_steps),
        in_specs=[once(0), once(n_steps), once(2 * n_steps), once(0), once(0)] + c_in,
        out_specs=[blk(0), blk(0), blk(0)] + c_out, out_shape=[out_sd, out_sd, out_sd] + c_shapes,
        input_output_aliases=c_alias,
        scratch_shapes=[pltpu.VMEM((s_len, ch * LANES), F32), pltpu.VMEM((s_len, ch * LANES), F32)] + c_sems,
        compiler_params=pltpu.CompilerParams(dimension_semantics=("arbitrary", "arbitrary"),
                                             vmem_limit_bytes=VMEM_CAP),
    )(qkv, qkv, qkv, o_sb, do_sb, *carry.arrays)
    return res[:3], res[3:]


def _dil_consts(group, pair_idx, dilation):
    lane = lax.broadcasted_iota(jnp.int32, (BLK, LANES), 1)
    head0 = lane < HEAD_DIM
    row = lax.broadcasted_iota(jnp.int32, (2 * BLK, BLK), 0)
    qa = row % BLK
    kb = lax.broadcasted_iota(jnp.int32, (2 * BLK, BLK), 1)
    head = (group * DIL_HEADS_PER_GROUP + 2 * pair_idx + row // BLK).astype(F32)
    slope = jnp.exp((-ALIBI_MAX_BIAS * math.log(2.0) / DIL_HEADS) * (head + 1.0))
    valid_cur = kb <= qa
    valid_prev = kb >= qa
    bias_cur = -slope * ((qa - kb) * dilation).astype(F32)
    bias_prev = -slope * ((BLK + qa - kb) * dilation).astype(F32)
    return head0, valid_cur, valid_prev, bias_cur, bias_prev


def _dil_units(s_len, dilation):
    nb = s_len // dilation // BLK
    return [(r, n) for r in range(dilation) for n in range(nb)]


def _dil_rows(n, r, dilation):
    if dilation == 1:
        return pl.ds(n * BLK, BLK)
    return pl.ds(n * BLK * dilation + r, BLK, stride=dilation)


def _dil_scores(q01, k, bias, valid):
    s = _dot_nt(q01, k) * (1.0 / math.sqrt(HEAD_DIM)) + bias
    return jnp.where(valid, s, NEG)


def _dil_fwd(qkv, b_sz, s_len, carry):
    n_pairs = DIL_OUT_WIDTH // LANES
    q_off = 3 * SB_WIDTH // LANES
    per_kind = DIL_WIDTH // LANES

    def compute(pair_idx, qkv_refs, o_ref, lse_ref, m_s, l_s):
        m_s[...] = jnp.full(m_s.shape, NEG, F32)
        l_s[...] = jnp.zeros(l_s.shape, F32)
        o_ref[...] = jnp.zeros(o_ref.shape, F32)
        for g, (_, dilation) in enumerate(DIL_PAIRS):
            q_ref, k_ref, v_ref = qkv_refs[3 * g:3 * g + 3]
            head0, valid_cur, valid_prev, bias_cur, bias_prev = _dil_consts(g, pair_idx, dilation)
            units = _dil_units(s_len, dilation)
            for u0 in range(0, len(units), DIL_CHAINS):
                group = units[u0:u0 + DIL_CHAINS]
                rows_of = [_dil_rows(n, r, dilation) for r, n in group]
                scores, values = [], []
                for (r, n), rows in zip(group, rows_of):
                    q01 = _stack_heads(q_ref[rows, :].astype(BF16), head0)
                    sc = [_dil_scores(q01, k_ref[rows, :].astype(BF16), bias_cur, valid_cur)]
                    vals = [_stack_heads(v_ref[rows, :].astype(BF16), head0)]
                    if n > 0:
                        prev = _dil_rows(n - 1, r, dilation)
                        sc.append(_dil_scores(q01, k_ref[prev, :].astype(BF16), bias_prev, valid_prev))
                        vals.append(_stack_heads(v_ref[prev, :].astype(BF16), head0))
                    scores.append(sc)
                    values.append(vals)
                stats = []
                for sc, rows in zip(scores, rows_of):
                    m_blk = functools.reduce(jnp.maximum, [jnp.max(x, axis=-1, keepdims=True) for x in sc])
                    m_old = jnp.concatenate([m_s.at[0][rows, :], m_s.at[1][rows, :]], axis=0)
                    l_old = jnp.concatenate([l_s.at[0][rows, :], l_s.at[1][rows, :]], axis=0)
                    m_new = jnp.maximum(m_old, m_blk)
                    probs = [jnp.exp(x - m_new) for x in sc]
                    l_blk = functools.reduce(jnp.add, [jnp.sum(p, axis=-1, keepdims=True) for p in probs])
                    alpha = jnp.exp(m_old - m_new)
                    stats.append((m_new, alpha * l_old + l_blk, alpha, probs))
                for (m_new, l_new, alpha, probs), vals, rows in zip(stats, values, rows_of):
                    alpha_tok = jnp.where(head0, alpha[:BLK], alpha[BLK:])
                    p_cat = jnp.concatenate(
                        [h for p in probs for h in (p[:BLK].astype(BF16), p[BLK:].astype(BF16))], axis=1)
                    o_ref[rows, :] = alpha_tok * o_ref[rows, :] + _dot(p_cat, jnp.concatenate(vals, axis=0))
                    m_s.at[0][rows, :] = m_new[:BLK]
                    m_s.at[1][rows, :] = m_new[BLK:]
                    l_s.at[0][rows, :] = l_new[:BLK]
                    l_s.at[1][rows, :] = l_new[BLK:]
        lane = lax.broadcasted_iota(jnp.int32, (BLK, LANES), 1)
        for c in range(s_len // BLK):
            rows = pl.ds(c * BLK, BLK)
            l0, l1 = l_s.at[0][rows, :], l_s.at[1][rows, :]
            o_ref[rows, :] = o_ref[rows, :] / jnp.where(lane < HEAD_DIM, l0, l1)
            lse_ref.at[0][rows, :] = m_s.at[0][rows, :] + jnp.log(l0)
            lse_ref.at[1][rows, :] = m_s.at[1][rows, :] + jnp.log(l1)

    def body(*refs):
        pair_idx = pl.program_id(1)
        step = pl.program_id(0) * n_pairs + pair_idx
        n_c, n_o = len(carry.arrays), len(carry.out_shapes)
        o_ref, lse_ref = refs[9 + n_c:11 + n_c]
        m_s, l_s = refs[11 + n_c + n_o:13 + n_c + n_o]
        carry.run(refs, 9, 2, step, b_sz * n_pairs, lambda: compute(pair_idx, refs[:9], o_ref, lse_ref, m_s, l_s))

    in_specs = []
    for g in range(len(DIL_PAIRS)):
        for kind in range(3):
            off = q_off + kind * per_kind + g * n_pairs
            in_specs.append(pl.BlockSpec((None, s_len, LANES), lambda b, p, off=off: (b, 0, off + p)))
    c_in, c_out, c_shapes, c_alias, c_sems = carry.call_args(9, 2)
    res = pl.pallas_call(
        body, name="dil_fwd", grid=(b_sz, n_pairs),
        in_specs=in_specs + c_in,
        out_specs=[pl.BlockSpec((None, s_len, LANES), lambda b, p: (b, 0, p)),
                   pl.BlockSpec((None, None, 2, s_len, LANES), lambda b, p: (b, p, 0, 0, 0))] + c_out,
        out_shape=[jax.ShapeDtypeStruct((b_sz, s_len, DIL_OUT_WIDTH), F32),
                   jax.ShapeDtypeStruct((b_sz, n_pairs, 2, s_len, LANES), F32)] + c_shapes,
        input_output_aliases=c_alias,
        scratch_shapes=[pltpu.VMEM((2, s_len, LANES), F32), pltpu.VMEM((2, s_len, LANES), F32)] + c_sems,
        compiler_params=pltpu.CompilerParams(dimension_semantics=("arbitrary", "arbitrary"),
                                             vmem_limit_bytes=VMEM_CAP),
    )(*([qkv] * 9), *carry.arrays)
    return res[0], res[1], res[2:]


def _dil_bwd(qkv, o_dl, lse, do_dl, b_sz, s_len, carry):
    n_pairs = DIL_OUT_WIDTH // LANES
    n_groups = len(DIL_PAIRS)
    q_off = 3 * SB_WIDTH // LANES
    per_kind = DIL_WIDTH // LANES

    def compute(pair_idx, group, q_ref, k_ref, v_ref, o_ref, lse_ref, do_ref, dq_ref, dk_ref, dv_ref, d_s, dq_s, dk_s, dv_s):
        lrow = lax.broadcasted_iota(jnp.int32, (LANES, LANES), 0)
        ones_h0 = jnp.where(lrow < HEAD_DIM, 1.0, 0.0).astype(BF16)
        ones_h1 = jnp.where(lrow >= HEAD_DIM, 1.0, 0.0).astype(BF16)
        for c in range(s_len // BLK):
            rows = pl.ds(c * BLK, BLK)
            dd_hi, dd_lo = _split2(do_ref[rows, :] * o_ref[rows, :])
            d_s.at[0][rows, :] = _dot(dd_hi, ones_h0) + _dot(dd_lo, ones_h0)
            d_s.at[1][rows, :] = _dot(dd_hi, ones_h1) + _dot(dd_lo, ones_h1)
        dk_s[...] = jnp.zeros(dk_s.shape, F32)
        dv_s[...] = jnp.zeros(dv_s.shape, F32)

        def one_group(g, dilation):
            head0, valid_cur, valid_prev, bias_cur, bias_prev = _dil_consts(g, pair_idx, dilation)
            units = _dil_units(s_len, dilation)
            scale = 1.0 / math.sqrt(HEAD_DIM)
            for u0 in range(0, len(units), DIL_CHAINS):
                chunk = units[u0:u0 + DIL_CHAINS]
                loaded = []
                for r, n in chunk:
                    rows = _dil_rows(n, r, dilation)
                    q01 = _stack_heads(q_ref[rows, :].astype(BF16), head0)
                    do01 = _stack_heads(do_ref[rows, :].astype(BF16), head0)
                    lse01 = jnp.concatenate([lse_ref.at[0][rows, :], lse_ref.at[1][rows, :]], axis=0)
                    d01 = jnp.concatenate([d_s.at[0][rows, :], d_s.at[1][rows, :]], axis=0)
                    blocks = [(rows, bias_cur, valid_cur)]
                    if n > 0:
                        blocks.append((_dil_rows(n - 1, r, dilation), bias_prev, valid_prev))
                    parts = []
                    for krows, bias, valid in blocks:
                        k = k_ref[krows, :].astype(BF16)
                        v = v_ref[krows, :].astype(BF16)
                        parts.append((krows, k, _dil_scores(q01, k, bias, valid), _dot_nt(do01, v)))
                    loaded.append((rows, q01, do01, lse01, d01, parts))
                grads = []
                for rows, q01, do01, lse01, d01, parts in loaded:
                    for krows, k, sc, dp in parts:
                        p = jnp.exp(sc - lse01)
                        grads.append((p.astype(BF16), (p * (dp - d01) * scale).astype(BF16)))
                it = iter(grads)
                updates = []
                for rows, q01, do01, lse01, d01, parts in loaded:
                    dq = jnp.zeros((2 * BLK, LANES), F32)
                    for krows, k, sc, dp in parts:
                        p_b, ds = next(it)
                        dq = dq + _dot(ds, k)
                        updates.append((krows, _dot_tn(ds, q01), _dot_tn(p_b, do01)))
                    dq_s[rows, :] = jnp.where(head0, dq[:BLK], dq[BLK:])
                for krows, dk, dv in updates:
                    dk_s[krows, :] = dk_s[krows, :] + dk
                    dv_s[krows, :] = dv_s[krows, :] + dv

        for g, (_, dilation) in enumerate(DIL_PAIRS):
            pl.when(group == g)(functools.partial(one_group, g, dilation))
        dq_ref[...] = dq_s[...].astype(dq_ref.dtype)
        dk_ref[...] = dk_s[...].astype(dk_ref.dtype)
        dv_ref[...] = dv_s[...].astype(dv_ref.dtype)

    def body(*refs):
        pair_idx, group = pl.program_id(1), pl.program_id(2)
        step = (pl.program_id(0) * n_pairs + pair_idx) * n_groups + group
        n_c, n_o = len(carry.arrays), len(carry.out_shapes)
        own = refs[:6] + refs[6 + n_c:9 + n_c] + refs[9 + n_c + n_o:13 + n_c + n_o]
        carry.run(refs, 6, 3, step, b_sz * n_pairs * n_groups, lambda: compute(pair_idx, group, *own))

    def qkv_spec(kind):
        return pl.BlockSpec((None, s_len, LANES),
                            lambda b, p, g: (b, 0, q_off + kind * per_kind + g * n_pairs + p))

    tok_spec = pl.BlockSpec((None, s_len, LANES), lambda b, p, g: (b, 0, p))
    out_spec = pl.BlockSpec((None, s_len, LANES), lambda b, p, g: (b, 0, g * n_pairs + p))
    out_sd = jax.ShapeDtypeStruct((b_sz, s_len, DIL_WIDTH), BF16)
    c_in, c_out, c_shapes, c_alias, c_sems = carry.call_args(6, 3)
    res = pl.pallas_call(
        body, name="dil_bwd", grid=(b_sz, n_pairs, n_groups),
        in_specs=[qkv_spec(0), qkv_spec(1), qkv_spec(2), tok_spec,
                  pl.BlockSpec((None, None, 2, s_len, LANES), lambda b, p, g: (b, p, 0, 0, 0)), tok_spec] + c_in,
        out_specs=[out_spec, out_spec, out_spec] + c_out,
        out_shape=[out_sd, out_sd, out_sd] + c_shapes,
        input_output_aliases=c_alias,
        scratch_shapes=[pltpu.VMEM((2, s_len, LANES), F32)] + [pltpu.VMEM((s_len, LANES), F32)] * 3 + c_sems,
        compiler_params=pltpu.CompilerParams(dimension_semantics=("arbitrary", "arbitrary", "arbitrary"),
                                             vmem_limit_bytes=VMEM_CAP),
    )(qkv, qkv, qkv, o_dl, lse, do_dl, *carry.arrays)
    return res[:3], res[3:]


def _mesh_pos():
    return lax.axis_index("x"), lax.axis_index("y"), lax.axis_index("c")


def _other_chips(x, y):
    return [(1 - x, y), (x, 1 - y), (1 - x, 1 - y)]


def _hbm_specs(n):
    return [pl.BlockSpec(memory_space=pl.ANY)] * n


SWAPPED = ("w_ffn_in",)


def _slot(x, y, swapped):
    return 2 * y + x if swapped else 2 * x + y


def _cast_to_slab(w, name):
    rows, cols = w.shape
    mine = jnp.reshape(_slot(lax.axis_index("x"), lax.axis_index("y"), False), (1,)).astype(jnp.int32)

    def body(idx_ref, w_ref, o_ref):
        o_ref[...] = w_ref[...].astype(BF16)

    return pl.pallas_call(
        body, name=name,
        grid_spec=pltpu.PrefetchScalarGridSpec(
            num_scalar_prefetch=1, grid=(1,),
            in_specs=[pl.BlockSpec((rows, cols), lambda i, idx: (0, 0))],
            out_specs=pl.BlockSpec((None, rows, cols), lambda i, idx: (idx[0], 0, 0))),
        out_shape=_hbm_array((N_CHIPS, rows, cols), BF16),
        compiler_params=pltpu.CompilerParams(vmem_limit_bytes=_vmem_limit(rows * cols * 6)),
    )(mine, w)


def _gather_issue(slabs, send_sems, recv_sems, swapped):
    x, y, c = _mesh_pos()
    for k, slab in enumerate(slabs):
        half = slab.shape[1] // 2
        rows = slab.at[_slot(x, y, swapped[k]), pl.ds(c * half, half), :]
        for r, (px, py) in enumerate(_other_chips(x, y)):
            pltpu.make_async_remote_copy(
                src_ref=rows, dst_ref=rows, send_sem=send_sems.at[6 * k + r], recv_sem=recv_sems.at[6 * k + r],
                device_id=(px, py, c), device_id_type=MESH).start()


def _gather_complete(slabs, send_sems, recv_sems, swapped):
    x, y, c = _mesh_pos()
    chips = _other_chips(x, y)

    def copy(k, sem, block, rows, to):
        ref = slabs[k].at[block, rows, :]
        return pltpu.make_async_remote_copy(
            src_ref=ref, dst_ref=ref, send_sem=send_sems.at[sem], recv_sem=recv_sems.at[sem],
            device_id=to, device_id_type=MESH)

    for k, slab in enumerate(slabs):
        half = slab.shape[1] // 2
        for r, (px, py) in enumerate(chips):
            theirs = _slot(px, py, swapped[k])
            copy(k, 6 * k + r, theirs, pl.ds(c * half, half), (px, py, c)).wait_recv()
            copy(k, 6 * k + 3 + r, theirs, pl.ds(c * half, half), (x, y, 1 - c)).start()
    for k, slab in enumerate(slabs):
        half = slab.shape[1] // 2
        for r, (px, py) in enumerate(chips):
            copy(k, 6 * k + 3 + r, _slot(px, py, swapped[k]), pl.ds((1 - c) * half, half), (x, y, 1 - c)).wait_recv()
    for k, slab in enumerate(slabs):
        half = slab.shape[1] // 2
        for r, (px, py) in enumerate(chips):
            copy(k, 6 * k + r, _slot(x, y, swapped[k]), pl.ds(c * half, half), (px, py, c)).wait_send()
            copy(k, 6 * k + 3 + r, _slot(px, py, swapped[k]), pl.ds(c * half, half), (x, y, 1 - c)).wait_send()


def _gather_sems(n):
    return [pltpu.SemaphoreType.DMA((6 * n,)), pltpu.SemaphoreType.DMA((6 * n,))]


def _gather_carry(slabs, names):
    swapped = [k in SWAPPED for k in names]
    return _Carry(slabs, [_hbm_array(a.shape, a.dtype) for a in slabs], True, _gather_sems(len(slabs)),
                  lambda ins, outs, sems: _gather_issue(outs, *sems, swapped),
                  lambda ins, outs, sems: _gather_complete(outs, *sems, swapped))


def _cast_carry(shards, names):
    swapped = [k in SWAPPED for k in names]

    def start(ins, outs, sems):
        x, y, _ = _mesh_pos()
        for w, slab, sw in zip(ins, outs, swapped):
            def cast(f32_buf, bf16_buf, sem, w=w, slab=slab, sw=sw):
                load = pltpu.make_async_copy(w, f32_buf, sem)
                load.start()
                load.wait()
                bf16_buf[...] = f32_buf[...].astype(BF16)
                store = pltpu.make_async_copy(bf16_buf, slab.at[_slot(x, y, sw)], sem)
                store.start()
                store.wait()

            pl.run_scoped(cast, pltpu.VMEM(w.shape, F32), pltpu.VMEM(w.shape, BF16), pltpu.SemaphoreType.DMA)

    return _Carry(shards, [_hbm_array((N_CHIPS,) + w.shape, BF16) for w in shards], False, [], start,
                  lambda ins, outs, sems: None)


def _pair_copies(ins, outs, send_sems, recv_sems):
    x, y, c = _mesh_pos()
    copies = []
    for k, g in enumerate(ins):
        half = g.shape[1] // 2
        copies.append(pltpu.make_async_remote_copy(
            src_ref=g.at[:, pl.ds((1 - c) * half, half), :], dst_ref=outs[k],
            send_sem=send_sems.at[k], recv_sem=recv_sems.at[k],
            device_id=(x, y, 1 - c), device_id_type=MESH))
    return copies


def _pair_carry(grads):
    n = len(grads)

    def start(ins, outs, sems):
        for cp in _pair_copies(ins, outs, *sems):
            cp.start()

    def finish(ins, outs, sems):
        for cp in _pair_copies(ins, outs, *sems):
            cp.wait()

    return _Carry(grads, [jax.ShapeDtypeStruct((N_CHIPS, g.shape[1] // 2, g.shape[2]), g.dtype) for g in grads], False,
                  [pltpu.SemaphoreType.DMA((n,)), pltpu.SemaphoreType.DMA((n,))], start, finish)


def _pair_exchange(grads, tag):
    carry = _pair_carry(grads)
    n = len(grads)

    def body(*refs):
        carry.start(refs[:n], refs[n:2 * n], refs[2 * n:])
        carry.finish(refs[:n], refs[n:2 * n], refs[2 * n:])

    return pl.pallas_call(
        body, name="grad_pair_exchange_" + tag, in_specs=_hbm_specs(n), out_specs=_hbm_specs(n),
        out_shape=carry.out_shapes, scratch_shapes=carry.sems,
    )(*grads)


def _pair_sum(grad, other, name, swapped):
    _, rows, cols = grad.shape
    half = rows // 2
    x, y, c = _mesh_pos()
    idx = jnp.stack([c, _slot(x, y, swapped)]).astype(jnp.int32)

    def body(idx_ref, g_ref, p_ref, own_ref, sb_ref):
        s = g_ref[...] + p_ref[...].astype(F32)
        sb_ref[...] = s.astype(BF16)

        @pl.when(pl.program_id(0) == idx_ref[1])
        def _():
            own_ref[...] = s

    blk = pl.BlockSpec((None, half, cols), lambda p, idx: (p, 0, 0))
    return pl.pallas_call(
        body, name=name,
        grid_spec=pltpu.PrefetchScalarGridSpec(
            num_scalar_prefetch=1, grid=(N_CHIPS,),
            in_specs=[pl.BlockSpec((None, half, cols), lambda p, idx: (p, idx[0], 0)), blk],
            out_specs=[pl.BlockSpec((half, cols), lambda p, idx: (0, 0)), blk]),
        out_shape=[jax.ShapeDtypeStruct((half, cols), F32), jax.ShapeDtypeStruct((N_CHIPS, half, cols), BF16)],
        compiler_params=pltpu.CompilerParams(dimension_semantics=("arbitrary",),
                                             vmem_limit_bytes=_vmem_limit(4 * half * cols * 4)),
    )(idx, grad, other)


def _chip_copies(sums_bf16, lands, send_sems, recv_sems, swapped):
    x, y, c = _mesh_pos()
    return [pltpu.make_async_remote_copy(
        src_ref=sums_bf16[k].at[_slot(px, py, swapped[k])], dst_ref=lands[k].at[r],
        send_sem=send_sems.at[3 * k + r], recv_sem=recv_sems.at[3 * k + r],
        device_id=(px, py, c), device_id_type=MESH)
        for k in range(len(sums_bf16)) for r, (px, py) in enumerate(_other_chips(x, y))]


def _chip_carry(sums_bf16, names):
    swapped = [k in SWAPPED for k in names]

    def start(ins, outs, sems):
        for cp in _chip_copies(ins, outs, *sems, swapped):
            cp.start()

    def finish(ins, outs, sems):
        for cp in _chip_copies(ins, outs, *sems, swapped):
            cp.wait()

    return _Carry(sums_bf16, _chip_landing(sums_bf16), False, _chip_sems(len(sums_bf16)), start, finish)


def _chip_sems(n):
    return [pltpu.SemaphoreType.DMA((3 * n,)), pltpu.SemaphoreType.DMA((3 * n,))]


def _chip_landing(sums_bf16):
    return [jax.ShapeDtypeStruct((N_CHIPS - 1,) + s.shape[1:], BF16) for s in sums_bf16]


def _chip_sum(own, landed, name):
    rows, cols = own.shape
    core = jnp.reshape(lax.axis_index("c"), (1,)).astype(jnp.int32)

    def body(core_ref, o_ref, l_ref, out_ref):
        out_ref[...] = ((o_ref[...] + l_ref[0].astype(F32)) + l_ref[1].astype(F32)) + l_ref[2].astype(F32)

    return pl.pallas_call(
        body, name=name,
        grid_spec=pltpu.PrefetchScalarGridSpec(
            num_scalar_prefetch=1, grid=(1,),
            in_specs=[pl.BlockSpec((rows, cols), lambda i, core_ref: (0, 0)),
                      pl.BlockSpec((N_CHIPS - 1, rows, cols), lambda i, core_ref: (0, 0, 0))],
            out_specs=pl.BlockSpec((rows, cols), lambda i, core_ref: (core_ref[0], 0))),
        out_shape=jax.ShapeDtypeStruct((2 * rows, cols), F32),
        compiler_params=pltpu.CompilerParams(vmem_limit_bytes=_vmem_limit(3 * rows * cols * 4)),
    )(core, own, landed)


def _halves_carry(fulls):
    n = len(fulls)

    def copies(outs, send_sems, recv_sems, own):
        x, y, c = _mesh_pos()
        res = []
        for k, out in enumerate(outs):
            half = out.shape[0] // 2
            rows = out.at[pl.ds((c if own else 1 - c) * half, half), :]
            res.append(pltpu.make_async_remote_copy(
                src_ref=rows, dst_ref=rows, send_sem=send_sems.at[k], recv_sem=recv_sems.at[k],
                device_id=(x, y, 1 - c), device_id_type=MESH))
        return res

    def start(ins, outs, sems):
        for cp in copies(outs, *sems, True):
            cp.start()

    def finish(ins, outs, sems):
        for cp in copies(outs, *sems, False):
            cp.wait_recv()
        for cp in copies(outs, *sems, True):
            cp.wait_send()

    return _Carry(fulls, [jax.ShapeDtypeStruct(f.shape, F32) for f in fulls], True,
                  [pltpu.SemaphoreType.DMA((n,)), pltpu.SemaphoreType.DMA((n,))], start, finish)


def _final_exchange(fulls, v):
    n = len(fulls)
    rows, cols = v.shape
    n_dev = 8

    def body(*refs):
        v_ref, out_ref = refs[0], refs[1 + 2 * n]
        outs = refs[1 + n:1 + 2 * n]
        buf, v_send, v_recv, h_send, h_recv = refs[2 + 2 * n:]
        x, y, c = _mesh_pos()
        me = 4 * x + 2 * y + c
        buf[me] = v_ref[...]
        peers = [(1 - x if r & 4 else x, 1 - y if r & 2 else y, 1 - c if r & 1 else c) for r in range(1, n_dev)]
        copies = []
        for r, peer in enumerate(peers):
            copies.append(pltpu.make_async_remote_copy(
                src_ref=v_ref, dst_ref=buf.at[me], send_sem=v_send.at[r], recv_sem=v_recv.at[r],
                device_id=peer, device_id_type=MESH))
        for k in range(n):
            half = fulls[k].shape[0] // 2
            mine = outs[k].at[pl.ds(c * half, half), :]
            copies.append(pltpu.make_async_remote_copy(
                src_ref=mine, dst_ref=mine, send_sem=h_send.at[k], recv_sem=h_recv.at[k],
                device_id=(x, y, 1 - c), device_id_type=MESH))
        for cp in copies:
            cp.start()
        for r, (px, py, pc) in enumerate(peers):
            pltpu.make_async_remote_copy(
                src_ref=v_ref, dst_ref=buf.at[4 * px + 2 * py + pc], send_sem=v_send.at[r], recv_sem=v_recv.at[r],
                device_id=(px, py, pc), device_id_type=MESH).wait_recv()
        for k in range(n):
            half = fulls[k].shape[0] // 2
            theirs = outs[k].at[pl.ds((1 - c) * half, half), :]
            pltpu.make_async_remote_copy(
                src_ref=theirs, dst_ref=theirs, send_sem=h_send.at[k], recv_sem=h_recv.at[k],
                device_id=(x, y, 1 - c), device_id_type=MESH).wait_recv()
        for cp in copies:
            cp.wait_send()
        acc = buf[0]
        for d in range(1, n_dev):
            acc = acc + buf[d]
        out_ref[...] = acc
        out_ref[3:4, :] = jnp.broadcast_to(jnp.sum(acc[3:4, :], axis=1, keepdims=True), (1, cols))

    vm = pl.BlockSpec(memory_space=pltpu.VMEM)
    res = pl.pallas_call(
        body, name="final_exchange",
        in_specs=[vm] + _hbm_specs(n), out_specs=_hbm_specs(n) + [vm],
        out_shape=[jax.ShapeDtypeStruct(f.shape, F32) for f in fulls] + [jax.ShapeDtypeStruct((rows, cols), F32)],
        input_output_aliases={1 + k: k for k in range(n)},
        scratch_shapes=[pltpu.VMEM((n_dev, rows, cols), F32),
                        pltpu.SemaphoreType.DMA((n_dev - 1,)), pltpu.SemaphoreType.DMA((n_dev - 1,)),
                        pltpu.SemaphoreType.DMA((n,)), pltpu.SemaphoreType.DMA((n,))],
    )(v, *fulls)
    return res[:n], res[n]


def _adamw_math(w, g, m, v):
    m = ADAM_B1 * m + (1.0 - ADAM_B1) * g
    v = ADAM_B2 * v + (1.0 - ADAM_B2) * (g * g)
    m_hat = m / (1.0 - ADAM_B1 ** ADAM_STEP)
    v_hat = v / (1.0 - ADAM_B2 ** ADAM_STEP)
    delta = -ADAM_LR * (m_hat / (jnp.sqrt(v_hat) + ADAM_EPS) + ADAM_WD * w)
    return delta, m, v


def _adamw(w, g, m, v, name):
    rows, cols = w.shape
    tm = rows // 2 if (rows // 2) % 8 == 0 else rows
    return _rowwise(_adamw_math, [w, g, m, v], [], [(cols, F32)] * 3, [], tm=tm, name=name)


def _unshard_cols(gathered):
    n, r, c = gathered.shape
    return jnp.transpose(gathered, (1, 0, 2)).reshape(r, n * c)


def _shard_cols(full):
    r, nc = full.shape
    return jnp.transpose(full.reshape(r, N_CHIPS, nc // N_CHIPS), (1, 0, 2))


LATE = ["w_sb_up", "w_dil_up", "w_out", "w_ffn_in", "w_ffn_out"]


def _late_weights(slabs, d_model, d_ff):
    g = dict(zip(LATE, slabs))
    return (_unshard_cols(g["w_sb_up"]), _unshard_cols(g["w_dil_up"]), g["w_out"].reshape(d_model, d_model),
            _unshard_cols(g["w_ffn_in"]), g["w_ffn_out"].reshape(d_ff, d_model))


ROW_SHARDED = ("w_in", "w_out", "w_ffn_in", "w_ffn_out")


def _chip_major(grads):
    out = []
    for k, g in grads.items():
        if k in ROW_SHARDED:
            out.append(g.reshape(N_CHIPS, g.shape[0] // N_CHIPS, g.shape[1]))
        else:
            out.append(_shard_cols(g))
    return out


def _pair_sums(full, others, names):
    return [_pair_sum(g, o, "grad_pair_sum_" + k, k in SWAPPED) for g, o, k in zip(full, others, names)]


def _chip_sums(pair, landed, names):
    return {k: _chip_sum(p[0], l, "grad_chip_sum_" + k) for p, l, k in zip(pair, landed, names)}


def _fwd_bwd(x, loss_target, g_mix, g_ffn, g_fin, slab_in, late_shards):
    b_sz, s_len, d_model = x.shape
    t = b_sz * s_len
    d_ff = late_shards[-1].shape[0] * N_CHIPS
    x2d = x.reshape(t, d_model)
    tgt2d = loss_target.reshape(t, d_model)

    u, (slab_in, *late_slabs) = _rowwise(
        lambda xv, g: (_rms_stats(xv)[0] * g,), [_in_hbm(x2d)], [g_mix], [(d_model, BF16)], [], tm=512, name="norm_mix",
        carry=_gather_carry([slab_in], ["w_in"]) + _cast_carry(late_shards, LATE), out_type=_hbm_array)
    u, wt_in = _in_hbm(u), _in_hbm(slab_in.reshape(-1, d_model))
    qkv, (slab_ffn_out,) = _mm(u, wt_in, tb=True, b_cols=(0, QKV_WIDTH), tm=2048, tn=768, tk=d_model, name="proj_qkv",
                               carry=_gather_carry(late_slabs[4:], LATE[4:]))
    gates = _mm(u, wt_in, tb=True, b_cols=(QKV_WIDTH, 2 * d_model), out_dtype=BF16, tm=t, tn=256, tk=d_model,
                name="proj_gates")
    qkv3 = qkv.reshape(b_sz, s_len, QKV_WIDTH)
    o_sb, (slab_ffn_in,) = _sb_fwd(qkv3, b_sz, s_len, _gather_carry(late_slabs[3:4], LATE[3:4]))
    o_dl, lse, small_slabs = _dil_fwd(qkv3, b_sz, s_len, _gather_carry(late_slabs[:3], LATE[:3]))
    wf_sb_up, wf_dil_up, wf_out, wf_ffn_in, wf_ffn_out = _late_weights(
        list(small_slabs) + [slab_ffn_in, slab_ffn_out], d_model, d_ff)
    o_sb2, o_dl2 = o_sb.reshape(t, SB_WIDTH), o_dl.reshape(t, DIL_OUT_WIDTH)
    y_sb = _mm(o_sb2, wf_sb_up, out_dtype=BF16, tm=1024, tn=1024, tk=SB_WIDTH, name="sb_up", out_type=_hbm_array)
    y_dl = _mm(o_dl2, wf_dil_up, out_dtype=BF16, tm=1024, tn=1024, tk=DIL_OUT_WIDTH, name="dil_up", out_type=_hbm_array)

    def merge_fn(gt, ys, yd):
        return (_sigmoid(gt[:, :d_model]) * ys + _sigmoid(gt[:, d_model:]) * yd,)

    (merged,) = _rowwise(merge_fn, [gates, y_sb, y_dl], [], [(d_model, BF16)], [], tm=512, name="merge")
    x1 = _mm(merged, wf_out, add=x2d, tm=512, tn=1024, tk=d_model, name="mix_out")
    (u2,) = _rowwise(lambda xv, g: (_rms_stats(xv)[0] * g,), [_in_hbm(x1)], [g_ffn], [(d_model, BF16)], [], tm=512, name="norm_ffn",
                     out_type=_hbm_array)
    u2 = _in_hbm(u2)
    half_ff = d_ff // 2

    def act_fn(hv):
        gate = hv[:, :half_ff]
        return hv, gate * _sigmoid(gate) * hv[:, half_ff:]

    h, act = _mm(u2, wf_ffn_in, tm=512, tn=d_ff, tk=d_model, name="ffn_in",
                 epilogue=(act_fn, [], [], [(d_ff, BF16), (half_ff, BF16)], []))
    def head_fn(xv, tg, g):
        xhat, r = _rms_stats(xv)
        err = xhat * g - tg
        dy = err * (1.0 / d_model)
        dx, dg_rows = _rms_bwd(dy, xhat, r, g)
        loss_lanes = (0.5 / d_model) * jnp.sum(err * err, axis=0, keepdims=True)
        return dx, dx, jnp.sum(dg_rows, axis=0, keepdims=True), loss_lanes

    dx2, dx2_b, dg_fin, loss_lanes = _mm(
        act, wf_ffn_out, add=x1, tm=512, tn=1024, tk=d_ff, name="ffn_out",
        epilogue=(head_fn, [tgt2d], [g_fin], [(d_model, F32), (d_model, BF16)], [(1, d_model), (1, d_model)]))

    def dact_fn(da, hv):
        gate, up = hv[:, :half_ff], hv[:, half_ff:]
        sg = _sigmoid(gate)
        dgate = da * up * (sg * (1.0 + gate * (1.0 - sg)))
        return (jnp.concatenate([dgate, da * (gate * sg)], axis=1),)

    dx2_b = _in_hbm(dx2_b)
    (dh,) = _mm(dx2_b, wf_ffn_out, tb=True, tm=512, tn=half_ff, tk=d_model, name="ffn_out_dx",
                epilogue=(dact_fn, [h], [], [(d_ff, BF16)], []))
    gw_ffn_out = _mm(act, dx2_b, ta=True, tm=256, tn=d_model, tk=t, name="ffn_out_dw")
    def norm_bwd_fn(du_, dres, xv, g):
        xhat, r = _rms_stats(xv)
        dx, dg_rows = _rms_bwd(du_, xhat, r, g)
        return dres + dx, jnp.sum(dg_rows, axis=0, keepdims=True)

    def norm_bwd_twice(*args):
        dx, dg = norm_bwd_fn(*args)
        return dx, dx, dg

    dx1, dx1_b, dg_ffn = _mm(dh, wf_ffn_in, tb=True, tm=512, tn=1024, tk=2 * d_ff, name="ffn_in_dx",
                             epilogue=(norm_bwd_twice, [dx2, x1], [g_ffn], [(d_model, F32), (d_model, BF16)], [(1, d_model)]))
    gwt_ffn_in = _mm(dh, u2, ta=True, tm=512, tn=d_model, tk=t, name="ffn_in_dw")

    dx1_b = _in_hbm(dx1_b)
    dmerged = _mm(dx1_b, wf_out, tb=True, out_dtype=BF16, tm=512, tn=1024, tk=d_model, name="mix_out_dx",
                  out_type=_hbm_array)
    gw_out = _mm(merged, dx1_b, ta=True, tm=256, tn=d_model, tk=t, name="mix_out_dw")

    def merge_bwd_fn(gt, ys, yd, dm):
        s_sb, s_dl = _sigmoid(gt[:, :d_model]), _sigmoid(gt[:, d_model:])
        dgates = jnp.concatenate([dm * ys * s_sb * (1.0 - s_sb), dm * yd * s_dl * (1.0 - s_dl)], axis=1)
        return dgates, dm * s_sb, dm * s_dl

    full_big = _chip_major({"w_out": gw_out, "w_ffn_in": gwt_ffn_in})
    dgates, dy_sb, dy_dl, others_big = _rowwise(
        merge_bwd_fn, [gates, y_sb, y_dl, dmerged], [], [(2 * d_model, BF16), (d_model, BF16), (d_model, BF16)], [],
        tm=256, name="merge_bwd", carry=_pair_carry(full_big))
    pair_big = _pair_sums(full_big, others_big, LATE[2:4])
    do_sb = _mm(dy_sb, wf_sb_up, tb=True, out_dtype=BF16, tm=1024, tn=SB_WIDTH, tk=d_model, name="sb_up_dx")
    gw_sb_up = _mm(o_sb2, dy_sb, ta=True, tm=SB_WIDTH, tn=1024, tk=512, name="sb_up_dw")
    do_dl = _mm(dy_dl, wf_dil_up, tb=True, tm=1024, tn=DIL_OUT_WIDTH, tk=d_model, name="dil_up_dx")
    gw_dil_up = _mm(o_dl2, dy_dl, ta=True, tm=DIL_OUT_WIDTH, tn=1024, tk=512, name="dil_up_dw")
    rest = [LATE[0], LATE[1], LATE[4]]
    full_rest = _chip_major({"w_sb_up": gw_sb_up, "w_dil_up": gw_dil_up, "w_ffn_out": gw_ffn_out})
    (dq_sb, dk_sb, dv_sb), brought = _sb_bwd(
        qkv3, o_sb, do_sb.reshape(b_sz, s_len, SB_WIDTH), b_sz, s_len,
        _chip_carry([p[1] for p in pair_big], LATE[2:4]) + _pair_carry(full_rest))
    pair_rest = _pair_sums(full_rest, brought[2:], rest)
    (dq_dl, dk_dl, dv_dl), landed_b = _dil_bwd(
        qkv3, o_dl, lse, do_dl.reshape(b_sz, s_len, DIL_OUT_WIDTH), b_sz, s_len,
        _chip_carry([p[1] for p in pair_rest], rest))
    pair = pair_rest[:2] + pair_big + pair_rest[2:]
    landed = [landed_b[0], landed_b[1], brought[0], brought[1], landed_b[2]]
    dproj = [a.reshape(t, -1) for a in (dq_sb, dk_sb, dv_sb, dq_dl, dk_dl, dv_dl)] + [dgates]
    gwt_in, gwt_in_b = _mm(dproj, u, ta=True, tm=256, tn=d_model, tk=t, name="proj_dw",
                           epilogue=(lambda tile: (tile, tile), [], [], [(d_model, F32), (d_model, BF16)], []))
    full_in = _chip_major({"w_in": gwt_in})
    pair_in = _pair_sums(full_in, _pair_exchange(_chip_major({"w_in": gwt_in_b}), "w_in"), ["w_in"])
    late_halves = _chip_sums(pair, landed, LATE)
    (dx, dg_mix), brought_in = _mm(
        dproj, wt_in, tm=512, tn=1024, tk=wt_in.shape[0], name="proj_dx",
        carry=_halves_carry([late_halves[k] for k in LATE]) + _chip_carry([p[1] for p in pair_in], ["w_in"]),
        epilogue=(norm_bwd_fn, [dx1, x2d], [g_mix], [(d_model, F32)], [(1, d_model)]))

    grads = dict(zip(LATE, brought_in[:len(LATE)]))
    grads.update(_chip_sums(pair_in, brought_in[len(LATE):], ["w_in"]))
    return dx, grads, dg_mix, dg_ffn, dg_fin, loss_lanes


def kernel(x, norm_mix_g, w_in, w_sb_up, w_dil_up, w_out, norm_ffn_g, w_ffn_in, w_ffn_out, norm_final_g, loss_target, m_norm_mix_g, m_w_in, m_w_sb_up, m_w_dil_up, m_w_out, m_norm_ffn_g, m_w_ffn_in, m_w_ffn_out, m_norm_final_g, v_norm_mix_g, v_w_in, v_w_sb_up, v_w_dil_up, v_w_out, v_norm_ffn_g, v_w_ffn_in, v_w_ffn_out, v_norm_final_g):
    b_sz, s_len, d_model = x.shape
    d_ff = w_ffn_out.shape[1] * N_CHIPS
    g_mix, g_ffn, g_fin = norm_mix_g, norm_ffn_g, norm_final_g.reshape(1, d_model)

    names = ["w_in", "w_sb_up", "w_dil_up", "w_out", "w_ffn_in", "w_ffn_out"]
    shards = {"w_in": jnp.swapaxes(w_in[0], 0, 1), "w_sb_up": w_sb_up[0], "w_dil_up": w_dil_up[0], "w_out": w_out[0],
              "w_ffn_in": w_ffn_in[0], "w_ffn_out": w_ffn_out[0]}
    slab_in = _cast_to_slab(shards["w_in"], "cast_w_in")

    dx, grads, dg_mix, dg_ffn, dg_fin, loss_lanes = _fwd_bwd(
        x, loss_target, g_mix, g_ffn, g_fin, slab_in, [shards[k] for k in LATE])

    small = jnp.concatenate([dg_mix, dg_ffn, dg_fin, loss_lanes, jnp.zeros((4, d_model), F32)], axis=0)
    (grads["w_in"],), small = _final_exchange([grads["w_in"]], small)
    grads["w_ffn_in"] = jnp.swapaxes(grads["w_ffn_in"], 0, 1)
    loss = small[3, 0]
    gains = jnp.concatenate([g_mix, g_ffn, g_fin, jnp.zeros((5, d_model), F32)], axis=0)
    gains_m = jnp.concatenate([m_norm_mix_g, m_norm_ffn_g, m_norm_final_g.reshape(1, d_model), jnp.zeros((5, d_model), F32)], axis=0)
    gains_v = jnp.concatenate([v_norm_mix_g, v_norm_ffn_g, v_norm_final_g.reshape(1, d_model), jnp.ones((5, d_model), F32)], axis=0)
    gd, gm, gv = _rowwise(_adamw_math, [gains, small, gains_m, gains_v], [], [(d_model, F32)] * 3, [], tm=8, name="adamw_gains")

    moments = {"w_in": (jnp.swapaxes(m_w_in[0], 0, 1), jnp.swapaxes(v_w_in[0], 0, 1)),
               "w_sb_up": (m_w_sb_up[0], v_w_sb_up[0]), "w_dil_up": (m_w_dil_up[0], v_w_dil_up[0]),
               "w_out": (m_w_out[0], v_w_out[0]), "w_ffn_in": (m_w_ffn_in[0], v_w_ffn_in[0]),
               "w_ffn_out": (m_w_ffn_out[0], v_w_ffn_out[0])}
    upd = {k: _adamw(shards[k], grads[k], moments[k][0], moments[k][1], "adamw_" + k) for k in names}

    def as_output(k, a):
        return (jnp.swapaxes(a, 0, 1) if k == "w_in" else a)[None]

    def w_out_of(i):
        return [as_output(k, upd[k][i]) for k in names]

    def ordered(mix, ws, ffn_g, fin):
        return [mix, ws[0], ws[1], ws[2], ws[3], ffn_g, ws[4], ws[5], fin]

    grad_ws = [as_output(k, grads[k]) for k in names]
    outs = [loss, dx.reshape(b_sz, s_len, d_model)]
    outs += ordered(small[0:1], grad_ws, small[1:2], small[2])
    outs += ordered(gd[0:1], w_out_of(0), gd[1:2], gd[2])
    outs += ordered(gm[0:1], w_out_of(1), gm[1:2], gm[2])
    outs += ordered(gv[0:1], w_out_of(2), gv[1:2], gv[2])
    return tuple(outs)
```

```python
import functools
import math

import jax
import jax.numpy as jnp
from jax import lax
from jax.experimental import pallas as pl
from jax.experimental.pallas import tpu as pltpu

F32 = jnp.float32
BF16 = jnp.bfloat16
MESH = pl.DeviceIdType.MESH

HEAD_DIM = 64
SB_HEADS = 8
DIL_PAIRS = ((128, 1), (512, 4), (2048, 16))
DIL_HEADS_PER_GROUP = 4
DIL_HEADS = DIL_HEADS_PER_GROUP * len(DIL_PAIRS)
SB_WIDTH = SB_HEADS * HEAD_DIM
DIL_WIDTH = DIL_HEADS * HEAD_DIM
DIL_OUT_WIDTH = DIL_HEADS_PER_GROUP * HEAD_DIM
QKV_WIDTH = 3 * SB_WIDTH + 3 * DIL_WIDTH
RMS_EPS = 1e-6
ALIBI_MAX_BIAS = 8.0
ADAM_LR = 0.001
ADAM_B1 = 0.9
ADAM_B2 = 0.999
ADAM_EPS = 1e-08
ADAM_WD = 0.01
ADAM_STEP = 10

LANES = 128
BLK = 128
NEG = -1e30
EXP_UNDERFLOW = -104.0
SB_FWD_CHAINS = 4
SB_BWD_CHAINS = 4
DIL_CHAINS = 4
N_CHIPS = 4
VMEM_CAP = 56 * 1024 * 1024


def _vmem_limit(tile_bytes):
    return int(min(VMEM_CAP, max(48 * 1024 * 1024, 3 * tile_bytes + 8 * 1024 * 1024)))


def _hbm_array(shape, dtype):
    return pltpu.HBM(shape, dtype)


def _nbytes(shape, dtype):
    return math.prod(shape) * jnp.dtype(dtype).itemsize


def _in_hbm(x):
    return pltpu.with_memory_space_constraint(x, pltpu.HBM)


def _dot(a, b):
    return jnp.dot(a, b, preferred_element_type=F32)


def _dot_nt(a, b):
    return lax.dot_general(a, b, (((1,), (1,)), ((), ())), preferred_element_type=F32)


def _dot_tn(a, b):
    return lax.dot_general(a, b, (((0,), (0,)), ((), ())), preferred_element_type=F32)


def _split2(x):
    hi = x.astype(BF16)
    lo = (x - hi.astype(F32)).astype(BF16)
    return hi, lo


def _sigmoid(x):
    return pl.reciprocal(1.0 + jnp.exp(-x), approx=True)


class _Carry:
    def __init__(self, arrays=(), out_shapes=(), aliased=False, sems=(), start=None, finish=None):
        self.arrays, self.out_shapes = list(arrays), list(out_shapes)
        self.n_aliased = len(self.arrays) if aliased is True else int(aliased)
        self.sems, self.start, self.finish = list(sems), start, finish

    def __bool__(self):
        return bool(self.arrays)

    def __add__(self, other):
        assert not other.n_aliased and self.n_aliased in (0, len(self.out_shapes))
        n_a, n_o, n_s = len(self.arrays), len(self.out_shapes), len(self.sems)
        return _Carry(
            self.arrays + other.arrays, self.out_shapes + other.out_shapes, self.n_aliased, self.sems + other.sems,
            lambda i, o, s: (self.start(i[:n_a], o[:n_o], s[:n_s]), other.start(i[n_a:], o[n_o:], s[n_s:])),
            lambda i, o, s: (self.finish(i[:n_a], o[:n_o], s[:n_s]), other.finish(i[n_a:], o[n_o:], s[n_s:])))

    def call_args(self, n_in, n_out):
        aliases = {n_in + k: n_out + k for k in range(self.n_aliased)}
        return _hbm_specs(len(self.arrays)), _hbm_specs(len(self.out_shapes)), self.out_shapes, aliases, self.sems

    def run(self, refs, n_in, n_out, step, n_steps, compute):
        if not self:
            compute()
            return
        n_c, n_o, n_s = len(self.arrays), len(self.out_shapes), len(self.sems)
        ins = refs[n_in:n_in + n_c]
        outs = refs[n_in + n_c + n_out:n_in + n_c + n_out + n_o]
        sems = refs[len(refs) - n_s:]

        @pl.when(step == 0)
        def _():
            self.start(ins, outs, sems)

        compute()

        @pl.when(step == n_steps - 1)
        def _():
            self.finish(ins, outs, sems)


def _mm(a, b, *, ta=False, tb=False, add=None, out_dtype=F32, tm, tn, tk, name, carry=None, epilogue=None,
        b_cols=None, out_type=jax.ShapeDtypeStruct):
    carry = carry or _Carry()
    n_car = len(carry.arrays)
    pieces = list(a) if isinstance(a, (list, tuple)) else [a]
    n_a = len(pieces)
    widths = [p.shape[1] for p in pieces]
    starts = [sum(widths[:p]) for p in range(n_a)]
    if ta:
        kdim, m = pieces[0].shape[0], sum(widths)
    else:
        m, kdim = pieces[0].shape[0], sum(widths)
    if tb:
        n, k2 = b.shape
    else:
        k2, n = b.shape
    col0 = 0
    if b_cols is not None:
        assert b_cols[0] % tn == 0, name
        col0, n = b_cols[0] // tn, b_cols[1]
    assert kdim == k2 and m % tm == 0 and n % tn == 0 and kdim % tk == 0, (name, a.shape, b.shape)
    nk = kdim // tk
    assert n_a == 1 or (nk == 1 and not tb and (not ta or all(w % tm == 0 for w in widths))), name
    grid = (m // tm, n // tn, nk)
    a_mode = dict(pipeline_mode=pl.Buffered(1)) if grid[0] == 1 and nk == 1 else {}
    b_mode = dict(pipeline_mode=pl.Buffered(1)) if grid[1] == 1 and nk == 1 else {}
    if n_a == 1:
        a_specs = [pl.BlockSpec((tk, tm), lambda i, j, k: (k, i), **a_mode) if ta
                   else pl.BlockSpec((tm, tk), lambda i, j, k: (i, k), **a_mode)]
    elif ta:
        a_specs = [pl.BlockSpec((tk, tm), lambda i, j, k, s=s // tm, w=w // tm: (0, jnp.clip(i - s, 0, w - 1)))
                   for s, w in zip(starts, widths)]
    else:
        a_specs = [pl.BlockSpec((tm, w), lambda i, j, k: (i, 0)) for w in widths]
    b_spec = (pl.BlockSpec((tn, tk), lambda i, j, k: (j + col0, k), **b_mode) if tb
              else pl.BlockSpec((tk, tn), lambda i, j, k: (k, j + col0), **b_mode))
    o_spec = pl.BlockSpec((tm, tn), lambda i, j, k: (i, j))
    dims = ((((0,) if ta else (1,)), ((1,) if tb else (0,))), ((), ()))
    has_add = add is not None
    if epilogue is None:
        ep_fn, ep_rows, ep_params, ep_outs, ep_accs = None, [], [], [], []
        out_sds, out_specs = [out_type((m, n), out_dtype)], [o_spec]
    else:
        ep_fn, ep_rows, ep_params, ep_outs, ep_accs = epilogue
        assert grid[1] == 1 or not ep_accs, name
        out_sds = [out_type((m, w * grid[1]), d) for w, d in ep_outs]
        out_sds += [jax.ShapeDtypeStruct(sh, F32) for sh in ep_accs]
        out_specs = [pl.BlockSpec((tm, w), lambda i, j, k: (i, j)) for w, _ in ep_outs]
        out_specs += [pl.BlockSpec(sh, lambda i, j, k: (0, 0)) for sh in ep_accs]
    n_main = len(out_sds)
    use_scratch = nk > 1 and (ep_fn is not None or jnp.dtype(out_dtype) != jnp.dtype(F32))
    n_in = n_a + 1 + has_add + len(ep_rows) + len(ep_params)

    def finish(total, refs, pid):
        outs = refs[n_in + n_car:n_in + n_car + n_main]
        if ep_fn is None:
            outs[0][...] = total.astype(out_dtype)
            return
        first = n_a + 1 + has_add
        rows = [r[...].astype(F32) for r in refs[first:first + len(ep_rows)]]
        params = [p[...] for p in refs[first + len(ep_rows):n_in]]
        res = ep_fn(total, *rows, *params)
        for o_ref, v in zip(outs[:len(ep_outs)], res):
            o_ref[...] = v.astype(o_ref.dtype)
        acc_refs = outs[len(ep_outs):]
        if acc_refs:
            @pl.when(pid[0] == 0)
            def _():
                for r in acc_refs:
                    r[...] = jnp.zeros(r.shape, F32)

            for r, v in zip(acc_refs, res[len(ep_outs):]):
                r[...] += v

    def compute(refs, pid):
        a_ref, b_ref = refs[0], refs[n_a]
        add_ref = refs[n_a + 1] if has_add else None

        def dot(x, y):
            return lax.dot_general(x.astype(BF16), y.astype(BF16), dims, preferred_element_type=F32)

        if n_a > 1 and ta:
            for p_ref, s, w in zip(refs[:n_a], starts, widths):
                @pl.when((pid[0] >= s // tm) & (pid[0] < (s + w) // tm))
                def _(p_ref=p_ref):
                    prod = dot(p_ref[...], b_ref[...])
                    finish(prod + add_ref[...] if has_add else prod, refs, pid)
            return
        if n_a > 1:
            prod = dot(a_ref[...], b_ref[:widths[0], :])
            for p_ref, s, w in zip(refs[1:n_a], starts[1:], widths[1:]):
                prod += dot(p_ref[...], b_ref[s:s + w, :])
        else:
            prod = dot(a_ref[...], b_ref[...])
        if nk == 1:
            finish(prod + add_ref[...] if has_add else prod, refs, pid)
            return
        acc_ref = refs[n_in + n_car + n_main + len(carry.out_shapes)] if use_scratch else refs[n_in + n_car]
        k = pid[2]

        @pl.when(k == 0)
        def _():
            acc_ref[...] = prod + add_ref[...] if has_add else prod

        @pl.when(k > 0)
        def _():
            acc_ref[...] += prod

        if use_scratch:
            @pl.when(k == nk - 1)
            def _():
                finish(acc_ref[...], refs, pid)

    def body(*refs):
        pid = (pl.program_id(0), pl.program_id(1), pl.program_id(2))
        step = (pid[0] * grid[1] + pid[1]) * nk + pid[2]
        carry.run(refs, n_in, n_main, step, grid[0] * grid[1] * nk, lambda: compute(refs, pid))

    tile_bytes = ((n_a if ta else 1) * _nbytes((tm, tk), pieces[0].dtype)
                  + _nbytes((tk, tn), b.dtype) + 2 * _nbytes((tm, tn), F32)
                  + (_nbytes((tm, tn), F32) if has_add else 0)
                  + sum(_nbytes((tm, r.shape[1]), r.dtype) for r in ep_rows) + sum(_nbytes((tm, w), d) for w, d in ep_outs))
    in_specs = a_specs + [b_spec] + ([o_spec] if has_add else [])
    in_specs += [pl.BlockSpec((tm, r.shape[1] // grid[1]), lambda i, j, k: (i, j)) for r in ep_rows]
    in_specs += [pl.BlockSpec(p.shape, lambda i, j, k: (0, 0)) for p in ep_params]
    args = tuple(pieces) + (b,) + ((add,) if has_add else ()) + tuple(ep_rows) + tuple(ep_params)
    scratch = [pltpu.VMEM((tm, tn), F32)] if use_scratch else []
    serial = bool(carry) or bool(ep_accs)
    c_in, c_out, c_shapes, c_alias, c_sems = carry.call_args(n_in, n_main)
    res = pl.pallas_call(
        body, name=name, grid=grid,
        in_specs=in_specs + c_in, out_specs=out_specs + c_out, out_shape=out_sds + c_shapes,
        input_output_aliases=c_alias, scratch_shapes=scratch + c_sems,
        compiler_params=pltpu.CompilerParams(
            dimension_semantics=("arbitrary",) * 3 if serial else ("parallel", "parallel", "arbitrary"),
            vmem_limit_bytes=_vmem_limit(tile_bytes)),
    )(*args, *carry.arrays)
    main = res[0] if ep_fn is None else list(res[:n_main])
    return (main, res[n_main:]) if carry else main


def _rowwise(fn, rows, params, outs, accs, *, tm, name, carry=None, out_type=jax.ShapeDtypeStruct):
    carry = carry or _Carry()
    t = rows[0].shape[0]
    assert t % tm == 0, (name, t, tm)
    n_r, n_p, n_o, n_c = len(rows), len(params), len(outs), len(carry.arrays)

    def compute(refs, first):
        vals = [r[...].astype(F32) for r in refs[:n_r]] + [p[...] for p in refs[n_r:n_r + n_p]]
        res = fn(*vals)
        o_refs = refs[n_r + n_p + n_c:n_r + n_p + n_c + n_o]
        a_refs = refs[n_r + n_p + n_c + n_o:n_r + n_p + n_c + n_o + len(accs)]
        for o_ref, v in zip(o_refs, res[:n_o]):
            o_ref[...] = v.astype(o_ref.dtype)
        if accs:
            @pl.when(first)
            def _():
                for a_ref in a_refs:
                    a_ref[...] = jnp.zeros(a_ref.shape, F32)

            for a_ref, v in zip(a_refs, res[n_o:]):
                a_ref[...] += v

    def body(*refs):
        step = pl.program_id(0)
        carry.run(refs, n_r + n_p, n_o + len(accs), step, t // tm, lambda: compute(refs, step == 0))

    in_specs = [pl.BlockSpec((tm, r.shape[1]), lambda i: (i, 0)) for r in rows]
    in_specs += [pl.BlockSpec(p.shape, lambda i: (0, 0)) for p in params]
    out_specs = [pl.BlockSpec((tm, w), lambda i: (i, 0)) for w, _ in outs]
    out_specs += [pl.BlockSpec(s, lambda i: (0, 0)) for s in accs]
    out_shape = [out_type((t, w), d) for w, d in outs]
    out_shape += [jax.ShapeDtypeStruct(s, F32) for s in accs]
    tile_bytes = sum(_nbytes((tm, r.shape[1]), r.dtype) for r in rows) + sum(_nbytes((tm, w), F32) for w, _ in outs)
    c_in, c_out, c_shapes, c_alias, c_sems = carry.call_args(n_r + n_p, n_o + len(accs))
    res = pl.pallas_call(
        body, name=name, grid=(t // tm,), in_specs=in_specs + c_in, out_specs=out_specs + c_out,
        out_shape=out_shape + c_shapes, input_output_aliases=c_alias, scratch_shapes=c_sems,
        compiler_params=pltpu.CompilerParams(
            dimension_semantics=("arbitrary",) if accs or carry else ("parallel",),
            vmem_limit_bytes=_vmem_limit(2 * tile_bytes)),
    )(*rows, *params, *carry.arrays)
    own = n_o + len(accs)
    return (list(res[:own]) + [res[own:]]) if carry else res


def _rms_stats(x):
    r = lax.rsqrt(jnp.mean(x * x, axis=-1, keepdims=True) + RMS_EPS)
    return x * r, r


def _rms_bwd(dy, xhat, r, g):
    dxhat = dy * g
    dx = r * (dxhat - xhat * jnp.mean(dxhat * xhat, axis=-1, keepdims=True))
    return dx, dy * xhat


def _sb_consts():
    lane = lax.broadcasted_iota(jnp.int32, (BLK, LANES), 1)
    head0 = lane < HEAD_DIM
    row = lax.broadcasted_iota(jnp.int32, (2 * BLK, BLK), 0) % BLK
    col = lax.broadcasted_iota(jnp.int32, (2 * BLK, BLK), 1)
    causal = col < row
    jj = lax.broadcasted_iota(jnp.int32, (BLK, BLK), 0)
    ss = lax.broadcasted_iota(jnp.int32, (BLK, BLK), 1)
    suffix = jnp.where(jj > ss, 1.0, 0.0).astype(BF16)
    return head0, causal, suffix


def _stack_heads(x, head0):
    zero = jnp.zeros_like(x)
    return jnp.concatenate([jnp.where(head0, x, zero), jnp.where(head0, zero, x)], axis=0)


def _sb_logits(z, causal, masked):
    sp = jnp.log(1.0 + jnp.exp(-jnp.abs(z)))
    log_keep = -(jnp.maximum(z, 0.0) + sp)
    log_beta = jnp.minimum(z, 0.0) - sp
    if masked:
        log_keep = jnp.where(causal, log_keep, 0.0)
    return log_keep, log_beta


def _suffix_sums(x, suffix):
    hi, lo = _split2(x)
    after = _dot(hi, suffix) + _dot(lo, suffix)
    total = jnp.broadcast_to(after[:, 0:1] + x[:, 0:1], x.shape)
    return after, total


def _sb_walk_back(i, state, per_chain, tile):
    def alive(st):
        worst = functools.reduce(jnp.maximum, [st[p][:, 0:1] for p in range(0, len(st), per_chain)])
        return jnp.max(worst) > EXP_UNDERFLOW

    def cond(c):
        return jnp.logical_and(c[0] < i, alive(c[1]))

    def body(c):
        return c[0] + 1, tile(i - 1 - c[0], c[1], False)

    return lax.while_loop(cond, body, (jnp.int32(0), state))[1]


def _lane_blocks(x, n):
    return [x[:, p * LANES:(p + 1) * LANES] for p in range(n)]


def _sb_fwd(qkv, b_sz, s_len, carry):
    nq = s_len // BLK
    n_pairs = SB_WIDTH // LANES
    ch = SB_FWD_CHAINS
    n_steps = n_pairs // ch
    scale = 1.0 / math.sqrt(HEAD_DIM)

    def compute(q_ref, k_ref, v_ref, o_ref):
        head0, causal, suffix = _sb_consts()

        def q_block(i, _):
            qs = pl.multiple_of(i * BLK, BLK)
            q_all = (q_ref[pl.ds(qs, BLK), :] * scale).astype(BF16)
            q01 = [_stack_heads(q, head0) for q in _lane_blocks(q_all, ch)]

            def tile(j, state, masked):
                ks = pl.multiple_of(j * BLK, BLK)
                ks_ = _lane_blocks(k_ref[pl.ds(ks, BLK), :].astype(BF16), ch)
                vs_ = _lane_blocks(v_ref[pl.ds(ks, BLK), :].astype(BF16), ch)
                zs = [_dot_nt(q01[p], ks_[p]) for p in range(ch)]
                logits = [_sb_logits(z, causal, masked) for z in zs]
                sums = [_suffix_sums(lg[0], suffix) for lg in logits]
                out = []
                for p in range(ch):
                    carry, acc = state[2 * p], state[2 * p + 1]
                    after, total = sums[p]
                    a = jnp.exp(logits[p][1] + carry + after)
                    if masked:
                        a = jnp.where(causal, a, 0.0)
                    a_hi, a_lo = _split2(a)
                    a_cat = jnp.concatenate([a_hi[:BLK], a_hi[BLK:], a_lo[:BLK], a_lo[BLK:]], axis=1)
                    v01 = _stack_heads(vs_[p], head0)
                    out += [carry + total, acc + _dot(a_cat, jnp.concatenate([v01, v01], axis=0))]
                return tuple(out)

            state = (jnp.zeros((2 * BLK, BLK), F32), jnp.zeros((BLK, LANES), F32)) * ch
            state = tile(i, state, True)
            state = _sb_walk_back(i, state, 2, tile)
            o_ref[pl.ds(qs, BLK), :] = jnp.concatenate([state[2 * p + 1] for p in range(ch)], axis=1)
            return 0

        lax.fori_loop(0, nq, q_block, 0)

    def body(*refs):
        step = pl.program_id(0) * n_steps + pl.program_id(1)
        o_ref = refs[3 + len(carry.arrays)]
        carry.run(refs, 3, 1, step, b_sz * n_steps, lambda: compute(refs[0], refs[1], refs[2], o_ref))

    blk = lambda off: pl.BlockSpec((None, s_len, ch * LANES), lambda b, p: (b, 0, off + p))
    c_in, c_out, c_shapes, c_alias, c_sems = carry.call_args(3, 1)
    res = pl.pallas_call(
        body, name="sb_fwd", grid=(b_sz, n_steps),
        in_specs=[blk(0), blk(n_steps), blk(2 * n_steps)] + c_in, out_specs=[blk(0)] + c_out,
        out_shape=[jax.ShapeDtypeStruct((b_sz, s_len, SB_WIDTH), F32)] + c_shapes,
        input_output_aliases=c_alias, scratch_shapes=c_sems,
        compiler_params=pltpu.CompilerParams(dimension_semantics=("arbitrary", "arbitrary"),
                                             vmem_limit_bytes=VMEM_CAP),
    )(qkv, qkv, qkv, *carry.arrays)
    return res[0], res[1:]


def _sb_bwd(qkv, o_sb, do_sb, b_sz, s_len, carry):
    nq = s_len // BLK
    n_pairs = SB_WIDTH // LANES
    ch = SB_BWD_CHAINS
    n_steps = n_pairs // ch
    scale = 1.0 / math.sqrt(HEAD_DIM)

    def compute(q_ref, k_ref, v_ref, o_ref, do_ref, dq_ref, dk_ref, dv_ref, dk_acc, dv_acc):
        head0, causal, suffix = _sb_consts()
        lrow = lax.broadcasted_iota(jnp.int32, (LANES, LANES), 0)
        ones_h0 = jnp.where(lrow < HEAD_DIM, 1.0, 0.0).astype(BF16)
        ones_h1 = jnp.where(lrow >= HEAD_DIM, 1.0, 0.0).astype(BF16)
        dk_acc[...] = jnp.zeros(dk_acc.shape, F32)
        dv_acc[...] = jnp.zeros(dv_acc.shape, F32)

        def q_block(i, _):
            qs = pl.multiple_of(i * BLK, BLK)
            q_all = (q_ref[pl.ds(qs, BLK), :] * scale).astype(BF16)
            do_all = do_ref[pl.ds(qs, BLK), :].astype(BF16)
            dd_all = do_all.astype(F32) * o_ref[pl.ds(qs, BLK), :]
            q01 = [_stack_heads(q, head0) for q in _lane_blocks(q_all, ch)]
            do01 = [_stack_heads(d, head0) for d in _lane_blocks(do_all, ch)]
            tot = []
            for dd in _lane_blocks(dd_all, ch):
                dd_hi, dd_lo = _split2(dd)
                tot.append(jnp.concatenate([_dot(dd_hi, ones_h0) + _dot(dd_lo, ones_h0),
                                            _dot(dd_hi, ones_h1) + _dot(dd_lo, ones_h1)], axis=0))

            def tile(j, state, masked):
                ks = pl.multiple_of(j * BLK, BLK)
                ks_ = _lane_blocks(k_ref[pl.ds(ks, BLK), :].astype(BF16), ch)
                vs_ = _lane_blocks(v_ref[pl.ds(ks, BLK), :].astype(BF16), ch)
                zs = [_dot_nt(q01[p], ks_[p]) for p in range(ch)]
                das = [_dot_nt(do01[p], vs_[p]) for p in range(ch)]
                logits = [_sb_logits(z, causal, masked) for z in zs]
                sums = [_suffix_sums(lg[0], suffix) for lg in logits]
                a_s, e_s = [], []
                for p in range(ch):
                    a = jnp.exp(logits[p][1] + state[3 * p] + sums[p][0])
                    if masked:
                        a = jnp.where(causal, a, 0.0)
                    a_s.append(a)
                    e_s.append(a * das[p])
                e_sums = [_suffix_sums(e, suffix) for e in e_s]
                out, dks, dvs = [], [], []
                for p in range(ch):
                    carry, rcarry, dq = state[3 * p:3 * p + 3]
                    e = e_s[p]
                    before = tot[p] - (rcarry + e_sums[p][0] + e)
                    beta = jnp.exp(logits[p][1])
                    dz = e * (1.0 - beta) - beta * before
                    if masked:
                        dz = jnp.where(causal, dz, 0.0)
                    dz_b = dz.astype(BF16)
                    dks.append(_dot_tn(dz_b, q01[p]))
                    dvs.append(_dot_tn(a_s[p].astype(BF16), do01[p]))
                    out += [carry + sums[p][1], rcarry + e_sums[p][1], dq + _dot(dz_b, ks_[p])]
                dk_acc[pl.ds(ks, BLK), :] += jnp.concatenate(dks, axis=1)
                dv_acc[pl.ds(ks, BLK), :] += jnp.concatenate(dvs, axis=1)
                return tuple(out)

            state = (jnp.zeros((2 * BLK, BLK), F32),) * (3 * ch)
            state = tile(i, state, True)
            state = _sb_walk_back(i, state, 3, tile)
            dq = [jnp.where(head0, state[3 * p + 2][:BLK], state[3 * p + 2][BLK:]) for p in range(ch)]
            dq_ref[pl.ds(qs, BLK), :] = (jnp.concatenate(dq, axis=1) * scale).astype(dq_ref.dtype)
            return 0

        lax.fori_loop(0, nq, q_block, 0)
        dk_ref[...] = dk_acc[...].astype(dk_ref.dtype)
        dv_ref[...] = dv_acc[...].astype(dv_ref.dtype)

    def body(*refs):
        step = pl.program_id(0) * n_steps + pl.program_id(1)
        n_c, n_o = len(carry.arrays), len(carry.out_shapes)
        own = refs[:5] + refs[5 + n_c:8 + n_c] + refs[8 + n_c + n_o:10 + n_c + n_o]
        carry.run(refs, 5, 3, step, b_sz * n_steps, lambda: compute(*own))

    blk = lambda off: pl.BlockSpec((None, s_len, ch * LANES), lambda b, p: (b, 0, off + p))
    once = lambda off: pl.BlockSpec((None, s_len, ch * LANES), lambda b, p: (b, 0, off + p),
                                    pipeline_mode=pl.Buffered(1))
    out_sd = jax.ShapeDtypeStruct((b_sz, s_len, SB_WIDTH), BF16)
    c_in, c_out, c_shapes, c_alias, c_sems = carry.call_args(5, 3)
    res = pl.pallas_call(
        body, name="sb_bwd", grid=(b_sz, n_steps),
        in_specs=[once(0), once(n_steps), once(2 * n_steps), once(0), once(0)] + c_in,
        out_specs=[blk(0), blk(0), blk(0)] + c_out, out_shape=[out_sd, out_sd, out_sd] + c_shapes,
        input_output_aliases=c_alias,
        scratch_shapes=[pltpu.VMEM((s_len, ch * LANES), F32), pltpu.VMEM((s_len, ch * LANES), F32)] + c_sems,
        compiler_params=pltpu.CompilerParams(dimension_semantics=("arbitrary", "arbitrary"),
                                             vmem_limit_bytes=VMEM_CAP),
    )(qkv, qkv, qkv, o_sb, do_sb, *carry.arrays)
    return res[:3], res[3:]


def _dil_consts(group, pair_idx, dilation):
    lane = lax.broadcasted_iota(jnp.int32, (BLK, LANES), 1)
    head0 = lane < HEAD_DIM
    row = lax.broadcasted_iota(jnp.int32, (2 * BLK, BLK), 0)
    qa = row % BLK
    kb = lax.broadcasted_iota(jnp.int32, (2 * BLK, BLK), 1)
    head = (group * DIL_HEADS_PER_GROUP + 2 * pair_idx + row // BLK).astype(F32)
    slope = jnp.exp((-ALIBI_MAX_BIAS * math.log(2.0) / DIL_HEADS) * (head + 1.0))
    valid_cur = kb <= qa
    valid_prev = kb >= qa
    bias_cur = -slope * ((qa - kb) * dilation).astype(F32)
    bias_prev = -slope * ((BLK + qa - kb) * dilation).astype(F32)
    return head0, valid_cur, valid_prev, bias_cur, bias_prev


def _dil_units(s_len, dilation):
    nb = s_len // dilation // BLK
    return [(r, n) for r in range(dilation) for n in range(nb)]


def _dil_rows(n, r, dilation):
    if dilation == 1:
        return pl.ds(n * BLK, BLK)
    return pl.ds(n * BLK * dilation + r, BLK, stride=dilation)


def _dil_scores(q01, k, bias, valid):
    s = _dot_nt(q01, k) * (1.0 / math.sqrt(HEAD_DIM)) + bias
    return jnp.where(valid, s, NEG)


def _dil_fwd(qkv, b_sz, s_len, carry):
    n_pairs = DIL_OUT_WIDTH // LANES
    q_off = 3 * SB_WIDTH // LANES
    per_kind = DIL_WIDTH // LANES

    def compute(pair_idx, qkv_refs, o_ref, lse_ref, m_s, l_s):
        m_s[...] = jnp.full(m_s.shape, NEG, F32)
        l_s[...] = jnp.zeros(l_s.shape, F32)
        o_ref[...] = jnp.zeros(o_ref.shape, F32)
        for g, (_, dilation) in enumerate(DIL_PAIRS):
            q_ref, k_ref, v_ref = qkv_refs[3 * g:3 * g + 3]
            head0, valid_cur, valid_prev, bias_cur, bias_prev = _dil_consts(g, pair_idx, dilation)
            units = _dil_units(s_len, dilation)
            for u0 in range(0, len(units), DIL_CHAINS):
                group = units[u0:u0 + DIL_CHAINS]
                rows_of = [_dil_rows(n, r, dilation) for r, n in group]
                scores, values = [], []
                for (r, n), rows in zip(group, rows_of):
                    q01 = _stack_heads(q_ref[rows, :].astype(BF16), head0)
                    sc = [_dil_scores(q01, k_ref[rows, :].astype(BF16), bias_cur, valid_cur)]
                    vals = [_stack_heads(v_ref[rows, :].astype(BF16), head0)]
                    if n > 0:
                        prev = _dil_rows(n - 1, r, dilation)
                        sc.append(_dil_scores(q01, k_ref[prev, :].astype(BF16), bias_prev, valid_prev))
                        vals.append(_stack_heads(v_ref[prev, :].astype(BF16), head0))
                    scores.append(sc)
                    values.append(vals)
                stats = []
                for sc, rows in zip(scores, rows_of):
                    m_blk = functools.reduce(jnp.maximum, [jnp.max(x, axis=-1, keepdims=True) for x in sc])
                    m_old = jnp.concatenate([m_s.at[0][rows, :], m_s.at[1][rows, :]], axis=0)
                    l_old = jnp.concatenate([l_s.at[0][rows, :], l_s.at[1][rows, :]], axis=0)
                    m_new = jnp.maximum(m_old, m_blk)
                    probs = [jnp.exp(x - m_new) for x in sc]
                    l_blk = functools.reduce(jnp.add, [jnp.sum(p, axis=-1, keepdims=True) for p in probs])
                    alpha = jnp.exp(m_old - m_new)
                    stats.append((m_new, alpha * l_old + l_blk, alpha, probs))
                for (m_new, l_new, alpha, probs), vals, rows in zip(stats, values, rows_of):
                    alpha_tok = jnp.where(head0, alpha[:BLK], alpha[BLK:])
                    p_cat = jnp.concatenate(
                        [h for p in probs for h in (p[:BLK].astype(BF16), p[BLK:].astype(BF16))], axis=1)
                    o_ref[rows, :] = alpha_tok * o_ref[rows, :] + _dot(p_cat, jnp.concatenate(vals, axis=0))
                    m_s.at[0][rows, :] = m_new[:BLK]
                    m_s.at[1][rows, :] = m_new[BLK:]
                    l_s.at[0][rows, :] = l_new[:BLK]
                    l_s.at[1][rows, :] = l_new[BLK:]
        lane = lax.broadcasted_iota(jnp.int32, (BLK, LANES), 1)
        for c in range(s_len // BLK):
            rows = pl.ds(c * BLK, BLK)
            l0, l1 = l_s.at[0][rows, :], l_s.at[1][rows, :]
            o_ref[rows, :] = o_ref[rows, :] / jnp.where(lane < HEAD_DIM, l0, l1)
            lse_ref.at[0][rows, :] = m_s.at[0][rows, :] + jnp.log(l0)
            lse_ref.at[1][rows, :] = m_s.at[1][rows, :] + jnp.log(l1)

    def body(*refs):
        pair_idx = pl.program_id(1)
        step = pl.program_id(0) * n_pairs + pair_idx
        n_c, n_o = len(carry.arrays), len(carry.out_shapes)
        o_ref, lse_ref = refs[9 + n_c:11 + n_c]
        m_s, l_s = refs[11 + n_c + n_o:13 + n_c + n_o]
        carry.run(refs, 9, 2, step, b_sz * n_pairs, lambda: compute(pair_idx, refs[:9], o_ref, lse_ref, m_s, l_s))

    in_specs = []
    for g in range(len(DIL_PAIRS)):
        for kind in range(3):
            off = q_off + kind * per_kind + g * n_pairs
            in_specs.append(pl.BlockSpec((None, s_len, LANES), lambda b, p, off=off: (b, 0, off + p)))
    c_in, c_out, c_shapes, c_alias, c_sems = carry.call_args(9, 2)
    res = pl.pallas_call(
        body, name="dil_fwd", grid=(b_sz, n_pairs),
        in_specs=in_specs + c_in,
        out_specs=[pl.BlockSpec((None, s_len, LANES), lambda b, p: (b, 0, p)),
                   pl.BlockSpec((None, None, 2, s_len, LANES), lambda b, p: (b, p, 0, 0, 0))] + c_out,
        out_shape=[jax.ShapeDtypeStruct((b_sz, s_len, DIL_OUT_WIDTH), F32),
                   jax.ShapeDtypeStruct((b_sz, n_pairs, 2, s_len, LANES), F32)] + c_shapes,
        input_output_aliases=c_alias,
        scratch_shapes=[pltpu.VMEM((2, s_len, LANES), F32), pltpu.VMEM((2, s_len, LANES), F32)] + c_sems,
        compiler_params=pltpu.CompilerParams(dimension_semantics=("arbitrary", "arbitrary"),
                                             vmem_limit_bytes=VMEM_CAP),
    )(*([qkv] * 9), *carry.arrays)
    return res[0], res[1], res[2:]


def _dil_bwd(qkv, o_dl, lse, do_dl, b_sz, s_len, carry):
    n_pairs = DIL_OUT_WIDTH // LANES
    n_groups = len(DIL_PAIRS)
    q_off = 3 * SB_WIDTH // LANES
    per_kind = DIL_WIDTH // LANES

    def compute(pair_idx, group, q_ref, k_ref, v_ref, o_ref, lse_ref, do_ref, dq_ref, dk_ref, dv_ref, d_s, dq_s, dk_s, dv_s):
        lrow = lax.broadcasted_iota(jnp.int32, (LANES, LANES), 0)
        ones_h0 = jnp.where(lrow < HEAD_DIM, 1.0, 0.0).astype(BF16)
        ones_h1 = jnp.where(lrow >= HEAD_DIM, 1.0, 0.0).astype(BF16)
        for c in range(s_len // BLK):
            rows = pl.ds(c * BLK, BLK)
            dd_hi, dd_lo = _split2(do_ref[rows, :] * o_ref[rows, :])
            d_s.at[0][rows, :] = _dot(dd_hi, ones_h0) + _dot(dd_lo, ones_h0)
            d_s.at[1][rows, :] = _dot(dd_hi, ones_h1) + _dot(dd_lo, ones_h1)
        dk_s[...] = jnp.zeros(dk_s.shape, F32)
        dv_s[...] = jnp.zeros(dv_s.shape, F32)

        def one_group(g, dilation):
            head0, valid_cur, valid_prev, bias_cur, bias_prev = _dil_consts(g, pair_idx, dilation)
            units = _dil_units(s_len, dilation)
            scale = 1.0 / math.sqrt(HEAD_DIM)
            for u0 in range(0, len(units), DIL_CHAINS):
                chunk = units[u0:u0 + DIL_CHAINS]
                loaded = []
                for r, n in chunk:
                    rows = _dil_rows(n, r, dilation)
                    q01 = _stack_heads(q_ref[rows, :].astype(BF16), head0)
                    do01 = _stack_heads(do_ref[rows, :].astype(BF16), head0)
                    lse01 = jnp.concatenate([lse_ref.at[0][rows, :], lse_ref.at[1][rows, :]], axis=0)
                    d01 = jnp.concatenate([d_s.at[0][rows, :], d_s.at[1][rows, :]], axis=0)
                    blocks = [(rows, bias_cur, valid_cur)]
                    if n > 0:
                        blocks.append((_dil_rows(n - 1, r, dilation), bias_prev, valid_prev))
                    parts = []
                    for krows, bias, valid in blocks:
                        k = k_ref[krows, :].astype(BF16)
                        v = v_ref[krows, :].astype(BF16)
                        parts.append((krows, k, _dil_scores(q01, k, bias, valid), _dot_nt(do01, v)))
                    loaded.append((rows, q01, do01, lse01, d01, parts))
                grads = []
                for rows, q01, do01, lse01, d01, parts in loaded:
                    for krows, k, sc, dp in parts:
                        p = jnp.exp(sc - lse01)
                        grads.append((p.astype(BF16), (p * (dp - d01) * scale).astype(BF16)))
                it = iter(grads)
                updates = []
                for rows, q01, do01, lse01, d01, parts in loaded:
                    dq = jnp.zeros((2 * BLK, LANES), F32)
                    for krows, k, sc, dp in parts:
                        p_b, ds = next(it)
                        dq = dq + _dot(ds, k)
                        updates.append((krows, _dot_tn(ds, q01), _dot_tn(p_b, do01)))
                    dq_s[rows, :] = jnp.where(head0, dq[:BLK], dq[BLK:])
                for krows, dk, dv in updates:
                    dk_s[krows, :] = dk_s[krows, :] + dk
                    dv_s[krows, :] = dv_s[krows, :] + dv

        for g, (_, dilation) in enumerate(DIL_PAIRS):
            pl.when(group == g)(functools.partial(one_group, g, dilation))
        dq_ref[...] = dq_s[...].astype(dq_ref.dtype)
        dk_ref[...] = dk_s[...].astype(dk_ref.dtype)
        dv_ref[...] = dv_s[...].astype(dv_ref.dtype)

    def body(*refs):
        pair_idx, group = pl.program_id(1), pl.program_id(2)
        step = (pl.program_id(0) * n_pairs + pair_idx) * n_groups + group
        n_c, n_o = len(carry.arrays), len(carry.out_shapes)
        own = refs[:6] + refs[6 + n_c:9 + n_c] + refs[9 + n_c + n_o:13 + n_c + n_o]
        carry.run(refs, 6, 3, step, b_sz * n_pairs * n_groups, lambda: compute(pair_idx, group, *own))

    def qkv_spec(kind):
        return pl.BlockSpec((None, s_len, LANES),
                            lambda b, p, g: (b, 0, q_off + kind * per_kind + g * n_pairs + p))

    tok_spec = pl.BlockSpec((None, s_len, LANES), lambda b, p, g: (b, 0, p))
    out_spec = pl.BlockSpec((None, s_len, LANES), lambda b, p, g: (b, 0, g * n_pairs + p))
    out_sd = jax.ShapeDtypeStruct((b_sz, s_len, DIL_WIDTH), BF16)
    c_in, c_out, c_shapes, c_alias, c_sems = carry.call_args(6, 3)
    res = pl.pallas_call(
        body, name="dil_bwd", grid=(b_sz, n_pairs, n_groups),
        in_specs=[qkv_spec(0), qkv_spec(1), qkv_spec(2), tok_spec,
                  pl.BlockSpec((None, None, 2, s_len, LANES), lambda b, p, g: (b, p, 0, 0, 0)), tok_spec] + c_in,
        out_specs=[out_spec, out_spec, out_spec] + c_out,
        out_shape=[out_sd, out_sd, out_sd] + c_shapes,
        input_output_aliases=c_alias,
        scratch_shapes=[pltpu.VMEM((2, s_len, LANES), F32)] + [pltpu.VMEM((s_len, LANES), F32)] * 3 + c_sems,
        compiler_params=pltpu.CompilerParams(dimension_semantics=("arbitrary", "arbitrary", "arbitrary"),
                                             vmem_limit_bytes=VMEM_CAP),
    )(qkv, qkv, qkv, o_dl, lse, do_dl, *carry.arrays)
    return res[:3], res[3:]


def _mesh_pos():
    return lax.axis_index("x"), lax.axis_index("y"), lax.axis_index("c")


def _other_chips(x, y):
    return [(1 - x, y), (x, 1 - y), (1 - x, 1 - y)]


def _hbm_specs(n):
    return [pl.BlockSpec(memory_space=pl.ANY)] * n


SWAPPED = ("w_ffn_in",)


def _slot(x, y, swapped):
    return 2 * y + x if swapped else 2 * x + y


def _cast_to_slab(w, name):
    rows, cols = w.shape
    mine = jnp.reshape(_slot(lax.axis_index("x"), lax.axis_index("y"), False), (1,)).astype(jnp.int32)

    def body(idx_ref, w_ref, o_ref):
        o_ref[...] = w_ref[...].astype(BF16)

    return pl.pallas_call(
        body, name=name,
        grid_spec=pltpu.PrefetchScalarGridSpec(
            num_scalar_prefetch=1, grid=(1,),
            in_specs=[pl.BlockSpec((rows, cols), lambda i, idx: (0, 0))],
            out_specs=pl.BlockSpec((None, rows, cols), lambda i, idx: (idx[0], 0, 0))),
        out_shape=_hbm_array((N_CHIPS, rows, cols), BF16),
        compiler_params=pltpu.CompilerParams(vmem_limit_bytes=_vmem_limit(rows * cols * 6)),
    )(mine, w)


def _gather_issue(slabs, send_sems, recv_sems, swapped):
    x, y, c = _mesh_pos()
    for k, slab in enumerate(slabs):
        half = slab.shape[1] // 2
        rows = slab.at[_slot(x, y, swapped[k]), pl.ds(c * half, half), :]
        for r, (px, py) in enumerate(_other_chips(x, y)):
            pltpu.make_async_remote_copy(
                src_ref=rows, dst_ref=rows, send_sem=send_sems.at[6 * k + r], recv_sem=recv_sems.at[6 * k + r],
                device_id=(px, py, c), device_id_type=MESH).start()


def _gather_complete(slabs, send_sems, recv_sems, swapped):
    x, y, c = _mesh_pos()
    chips = _other_chips(x, y)

    def copy(k, sem, block, rows, to):
        ref = slabs[k].at[block, rows, :]
        return pltpu.make_async_remote_copy(
            src_ref=ref, dst_ref=ref, send_sem=send_sems.at[sem], recv_sem=recv_sems.at[sem],
            device_id=to, device_id_type=MESH)

    for k, slab in enumerate(slabs):
        half = slab.shape[1] // 2
        for r, (px, py) in enumerate(chips):
            theirs = _slot(px, py, swapped[k])
            copy(k, 6 * k + r, theirs, pl.ds(c * half, half), (px, py, c)).wait_recv()
            copy(k, 6 * k + 3 + r, theirs, pl.ds(c * half, half), (x, y, 1 - c)).start()
    for k, slab in enumerate(slabs):
        half = slab.shape[1] // 2
        for r, (px, py) in enumerate(chips):
            copy(k, 6 * k + 3 + r, _slot(px, py, swapped[k]), pl.ds((1 - c) * half, half), (x, y, 1 - c)).wait_recv()
    for k, slab in enumerate(slabs):
        half = slab.shape[1] // 2
        for r, (px, py) in enumerate(chips):
            copy(k, 6 * k + r, _slot(x, y, swapped[k]), pl.ds(c * half, half), (px, py, c)).wait_send()
            copy(k, 6 * k + 3 + r, _slot(px, py, swapped[k]), pl.ds(c * half, half), (x, y, 1 - c)).wait_send()


def _gather_sems(n):
    return [pltpu.SemaphoreType.DMA((6 * n,)), pltpu.SemaphoreType.DMA((6 * n,))]


def _gather_carry(slabs, names):
    swapped = [k in SWAPPED for k in names]
    return _Carry(slabs, [_hbm_array(a.shape, a.dtype) for a in slabs], True, _gather_sems(len(slabs)),
                  lambda ins, outs, sems: _gather_issue(outs, *sems, swapped),
                  lambda ins, outs, sems: _gather_complete(outs, *sems, swapped))


def _cast_carry(shards, names):
    swapped = [k in SWAPPED for k in names]

    def start(ins, outs, sems):
        x, y, _ = _mesh_pos()
        for w, slab, sw in zip(ins, outs, swapped):
            def cast(f32_buf, bf16_buf, sem, w=w, slab=slab, sw=sw):
                load = pltpu.make_async_copy(w, f32_buf, sem)
                load.start()
                load.wait()
                bf16_buf[...] = f32_buf[...].astype(BF16)
                store = pltpu.make_async_copy(bf16_buf, slab.at[_slot(x, y, sw)], sem)
                store.start()
                store.wait()

            pl.run_scoped(cast, pltpu.VMEM(w.shape, F32), pltpu.VMEM(w.shape, BF16), pltpu.SemaphoreType.DMA)

    return _Carry(shards, [_hbm_array((N_CHIPS,) + w.shape, BF16) for w in shards], False, [], start,
                  lambda ins, outs, sems: None)


def _pair_copies(ins, outs, send_sems, recv_sems):
    x, y, c = _mesh_pos()
    copies = []
    for k, g in enumerate(ins):
        half = g.shape[1] // 2
        copies.append(pltpu.make_async_remote_copy(
            src_ref=g.at[:, pl.ds((1 - c) * half, half), :], dst_ref=outs[k],
            send_sem=send_sems.at[k], recv_sem=recv_sems.at[k],
            device_id=(x, y, 1 - c), device_id_type=MESH))
    return copies


def _pair_carry(grads):
    n = len(grads)

    def start(ins, outs, sems):
        for cp in _pair_copies(ins, outs, *sems):
            cp.start()

    def finish(ins, outs, sems):
        for cp in _pair_copies(ins, outs, *sems):
            cp.wait()

    return _Carry(grads, [jax.ShapeDtypeStruct((N_CHIPS, g.shape[1] // 2, g.shape[2]), g.dtype) for g in grads], False,
                  [pltpu.SemaphoreType.DMA((n,)), pltpu.SemaphoreType.DMA((n,))], start, finish)


def _pair_exchange(grads, tag):
    carry = _pair_carry(grads)
    n = len(grads)

    def body(*refs):
        carry.start(refs[:n], refs[n:2 * n], refs[2 * n:])
        carry.finish(refs[:n], refs[n:2 * n], refs[2 * n:])

    return pl.pallas_call(
        body, name="grad_pair_exchange_" + tag, in_specs=_hbm_specs(n), out_specs=_hbm_specs(n),
        out_shape=carry.out_shapes, scratch_shapes=carry.sems,
    )(*grads)


def _pair_sum(grad, other, name, swapped):
    _, rows, cols = grad.shape
    half = rows // 2
    x, y, c = _mesh_pos()
    idx = jnp.stack([c, _slot(x, y, swapped)]).astype(jnp.int32)

    def body(idx_ref, g_ref, p_ref, own_ref, sb_ref):
        s = g_ref[...] + p_ref[...].astype(F32)
        sb_ref[...] = s.astype(BF16)

        @pl.when(pl.program_id(0) == idx_ref[1])
        def _():
            own_ref[...] = s

    blk = pl.BlockSpec((None, half, cols), lambda p, idx: (p, 0, 0))
    return pl.pallas_call(
        body, name=name,
        grid_spec=pltpu.PrefetchScalarGridSpec(
            num_scalar_prefetch=1, grid=(N_CHIPS,),
            in_specs=[pl.BlockSpec((None, half, cols), lambda p, idx: (p, idx[0], 0)), blk],
            out_specs=[pl.BlockSpec((half, cols), lambda p, idx: (0, 0)), blk]),
        out_shape=[jax.ShapeDtypeStruct((half, cols), F32), jax.ShapeDtypeStruct((N_CHIPS, half, cols), BF16)],
        compiler_params=pltpu.CompilerParams(dimension_semantics=("arbitrary",),
                                             vmem_limit_bytes=_vmem_limit(4 * half * cols * 4)),
    )(idx, grad, other)


def _chip_copies(sums_bf16, lands, send_sems, recv_sems, swapped):
    x, y, c = _mesh_pos()
    return [pltpu.make_async_remote_copy(
        src_ref=sums_bf16[k].at[_slot(px, py, swapped[k])], dst_ref=lands[k].at[r],
        send_sem=send_sems.at[3 * k + r], recv_sem=recv_sems.at[3 * k + r],
        device_id=(px, py, c), device_id_type=MESH)
        for k in range(len(sums_bf16)) for r, (px, py) in enumerate(_other_chips(x, y))]


def _chip_carry(sums_bf16, names):
    swapped = [k in SWAPPED for k in names]

    def start(ins, outs, sems):
        for cp in _chip_copies(ins, outs, *sems, swapped):
            cp.start()

    def finish(ins, outs, sems):
        for cp in _chip_copies(ins, outs, *sems, swapped):
            cp.wait()

    return _Carry(sums_bf16, _chip_landing(sums_bf16), False, _chip_sems(len(sums_bf16)), start, finish)


def _chip_sems(n):
    return [pltpu.SemaphoreType.DMA((3 * n,)), pltpu.SemaphoreType.DMA((3 * n,))]


def _chip_landing(sums_bf16):
    return [jax.ShapeDtypeStruct((N_CHIPS - 1,) + s.shape[1:], BF16) for s in sums_bf16]


def _chip_sum(own, landed, name):
    rows, cols = own.shape
    core = jnp.reshape(lax.axis_index("c"), (1,)).astype(jnp.int32)

    def body(core_ref, o_ref, l_ref, out_ref):
        out_ref[...] = ((o_ref[...] + l_ref[0].astype(F32)) + l_ref[1].astype(F32)) + l_ref[2].astype(F32)

    return pl.pallas_call(
        body, name=name,
        grid_spec=pltpu.PrefetchScalarGridSpec(
            num_scalar_prefetch=1, grid=(1,),
            in_specs=[pl.BlockSpec((rows, cols), lambda i, core_ref: (0, 0)),
                      pl.BlockSpec((N_CHIPS - 1, rows, cols), lambda i, core_ref: (0, 0, 0))],
            out_specs=pl.BlockSpec((rows, cols), lambda i, core_ref: (core_ref[0], 0))),
        out_shape=jax.ShapeDtypeStruct((2 * rows, cols), F32),
        compiler_params=pltpu.CompilerParams(vmem_limit_bytes=_vmem_limit(3 * rows * cols * 4)),
    )(core, own, landed)


def _halves_carry(fulls):
    n = len(fulls)

    def copies(outs, send_sems, recv_sems, own):
        x, y, c = _mesh_pos()
        res = []
        for k, out in enumerate(outs):
            half = out.shape[0] // 2
            rows = out.at[pl.ds((c if own else 1 - c) * half, half), :]
            res.append(pltpu.make_async_remote_copy(
                src_ref=rows, dst_ref=rows, send_sem=send_sems.at[k], recv_sem=recv_sems.at[k],
                device_id=(x, y, 1 - c), device_id_type=MESH))
        return res

    def start(ins, outs, sems):
        for cp in copies(outs, *sems, True):
            cp.start()

    def finish(ins, outs, sems):
        for cp in copies(outs, *sems, False):
            cp.wait_recv()
        for cp in copies(outs, *sems, True):
            cp.wait_send()

    return _Carry(fulls, [jax.ShapeDtypeStruct(f.shape, F32) for f in fulls], True,
                  [pltpu.SemaphoreType.DMA((n,)), pltpu.SemaphoreType.DMA((n,))], start, finish)


def _final_exchange(fulls, v):
    n = len(fulls)
    rows, cols = v.shape
    n_dev = 8

    def body(*refs):
        v_ref, out_ref = refs[0], refs[1 + 2 * n]
        outs = refs[1 + n:1 + 2 * n]
        buf, v_send, v_recv, h_send, h_recv = refs[2 + 2 * n:]
        x, y, c = _mesh_pos()
        me = 4 * x + 2 * y + c
        buf[me] = v_ref[...]
        peers = [(1 - x if r & 4 else x, 1 - y if r & 2 else y, 1 - c if r & 1 else c) for r in range(1, n_dev)]
        copies = []
        for r, peer in enumerate(peers):
            copies.append(pltpu.make_async_remote_copy(
                src_ref=v_ref, dst_ref=buf.at[me], send_sem=v_send.at[r], recv_sem=v_recv.at[r],
                device_id=peer, device_id_type=MESH))
        for k in range(n):
            half = fulls[k].shape[0] // 2
            mine = outs[k].at[pl.ds(c * half, half), :]
            copies.append(pltpu.make_async_remote_copy(
                src_ref=mine, dst_ref=mine, send_sem=h_send.at[k], recv_sem=h_recv.at[k],
                device_id=(x, y, 1 - c), device_id_type=MESH))
        for cp in copies:
            cp.start()
        for r, (px, py, pc) in enumerate(peers):
            pltpu.make_async_remote_copy(
                src_ref=v_ref, dst_ref=buf.at[4 * px + 2 * py + pc], send_sem=v_send.at[r], recv_sem=v_recv.at[r],
                device_id=(px, py, pc), device_id_type=MESH).wait_recv()
        for k in range(n):
            half = fulls[k].shape[0] // 2
            theirs = outs[k].at[pl.ds((1 - c) * half, half), :]
            pltpu.make_async_remote_copy(
                src_ref=theirs, dst_ref=theirs, send_sem=h_send.at[k], recv_sem=h_recv.at[k],
                device_id=(x, y, 1 - c), device_id_type=MESH).wait_recv()
        for cp in copies:
            cp.wait_send()
        acc = buf[0]
        for d in range(1, n_dev):
            acc = acc + buf[d]
        out_ref[...] = acc
        out_ref[3:4, :] = jnp.broadcast_to(jnp.sum(acc[3:4, :], axis=1, keepdims=True), (1, cols))

    vm = pl.BlockSpec(memory_space=pltpu.VMEM)
    res = pl.pallas_call(
        body, name="final_exchange",
        in_specs=[vm] + _hbm_specs(n), out_specs=_hbm_specs(n) + [vm],
        out_shape=[jax.ShapeDtypeStruct(f.shape, F32) for f in fulls] + [jax.ShapeDtypeStruct((rows, cols), F32)],
        input_output_aliases={1 + k: k for k in range(n)},
        scratch_shapes=[pltpu.VMEM((n_dev, rows, cols), F32),
                        pltpu.SemaphoreType.DMA((n_dev - 1,)), pltpu.SemaphoreType.DMA((n_dev - 1,)),
                        pltpu.SemaphoreType.DMA((n,)), pltpu.SemaphoreType.DMA((n,))],
    )(v, *fulls)
    return res[:n], res[n]


def _adamw_math(w, g, m, v):
    m = ADAM_B1 * m + (1.0 - ADAM_B1) * g
    v = ADAM_B2 * v + (1.0 - ADAM_B2) * (g * g)
    m_hat = m / (1.0 - ADAM_B1 ** ADAM_STEP)
    v_hat = v / (1.0 - ADAM_B2 ** ADAM_STEP)
    delta = -ADAM_LR * (m_hat / (jnp.sqrt(v_hat) + ADAM_EPS) + ADAM_WD * w)
    return delta, m, v


def _adamw(w, g, m, v, name):
    rows, cols = w.shape
    tm = rows // 2 if (rows // 2) % 8 == 0 else rows
    return _rowwise(_adamw_math, [w, g, m, v], [], [(cols, F32)] * 3, [], tm=tm, name=name)


def _unshard_cols(gathered):
    n, r, c = gathered.shape
    return jnp.transpose(gathered, (1, 0, 2)).reshape(r, n * c)


def _shard_cols(full):
    r, nc = full.shape
    return jnp.transpose(full.reshape(r, N_CHIPS, nc // N_CHIPS), (1, 0, 2))


LATE = ["w_sb_up", "w_dil_up", "w_out", "w_ffn_in", "w_ffn_out"]


def _late_weights(slabs, d_model, d_ff):
    g = dict(zip(LATE, slabs))
    return (_unshard_cols(g["w_sb_up"]), _unshard_cols(g["w_dil_up"]), g["w_out"].reshape(d_model, d_model),
            _unshard_cols(g["w_ffn_in"]), g["w_ffn_out"].reshape(d_ff, d_model))


ROW_SHARDED = ("w_in", "w_out", "w_ffn_in", "w_ffn_out")


def _chip_major(grads):
    out = []
    for k, g in grads.items():
        if k in ROW_SHARDED:
            out.append(g.reshape(N_CHIPS, g.shape[0] // N_CHIPS, g.shape[1]))
        else:
            out.append(_shard_cols(g))
    return out


def _pair_sums(full, others, names):
    return [_pair_sum(g, o, "grad_pair_sum_" + k, k in SWAPPED) for g, o, k in zip(full, others, names)]


def _chip_sums(pair, landed, names):
    return {k: _chip_sum(p[0], l, "grad_chip_sum_" + k) for p, l, k in zip(pair, landed, names)}


def _fwd_bwd(x, loss_target, g_mix, g_ffn, g_fin, slab_in, late_shards):
    b_sz, s_len, d_model = x.shape
    t = b_sz * s_len
    d_ff = late_shards[-1].shape[0] * N_CHIPS
    x2d = x.reshape(t, d_model)
    tgt2d = loss_target.reshape(t, d_model)

    u, (slab_in, *late_slabs) = _rowwise(
        lambda xv, g: (_rms_stats(xv)[0] * g,), [_in_hbm(x2d)], [g_mix], [(d_model, BF16)], [], tm=512, name="norm_mix",
        carry=_gather_carry([slab_in], ["w_in"]) + _cast_carry(late_shards, LATE), out_type=_hbm_array)
    u, wt_in = _in_hbm(u), _in_hbm(slab_in.reshape(-1, d_model))
    qkv, (slab_ffn_out,) = _mm(u, wt_in, tb=True, b_cols=(0, QKV_WIDTH), tm=2048, tn=768, tk=d_model, name="proj_qkv",
                               carry=_gather_carry(late_slabs[4:], LATE[4:]))
    gates = _mm(u, wt_in, tb=True, b_cols=(QKV_WIDTH, 2 * d_model), out_dtype=BF16, tm=t, tn=256, tk=d_model,
                name="proj_gates")
    qkv3 = qkv.reshape(b_sz, s_len, QKV_WIDTH)
    o_sb, (slab_ffn_in,) = _sb_fwd(qkv3, b_sz, s_len, _gather_carry(late_slabs[3:4], LATE[3:4]))
    o_dl, lse, small_slabs = _dil_fwd(qkv3, b_sz, s_len, _gather_carry(late_slabs[:3], LATE[:3]))
    wf_sb_up, wf_dil_up, wf_out, wf_ffn_in, wf_ffn_out = _late_weights(
        list(small_slabs) + [slab_ffn_in, slab_ffn_out], d_model, d_ff)
    o_sb2, o_dl2 = o_sb.reshape(t, SB_WIDTH), o_dl.reshape(t, DIL_OUT_WIDTH)
    y_sb = _mm(o_sb2, wf_sb_up, out_dtype=BF16, tm=1024, tn=1024, tk=SB_WIDTH, name="sb_up", out_type=_hbm_array)
    y_dl = _mm(o_dl2, wf_dil_up, out_dtype=BF16, tm=1024, tn=1024, tk=DIL_OUT_WIDTH, name="dil_up", out_type=_hbm_array)

    def merge_fn(gt, ys, yd):
        return (_sigmoid(gt[:, :d_model]) * ys + _sigmoid(gt[:, d_model:]) * yd,)

    (merged,) = _rowwise(merge_fn, [gates, y_sb, y_dl], [], [(d_model, BF16)], [], tm=512, name="merge")
    x1 = _mm(merged, wf_out, add=x2d, tm=512, tn=1024, tk=d_model, name="mix_out")
    (u2,) = _rowwise(lambda xv, g: (_rms_stats(xv)[0] * g,), [_in_hbm(x1)], [g_ffn], [(d_model, BF16)], [], tm=512, name="norm_ffn",
                     out_type=_hbm_array)
    u2 = _in_hbm(u2)
    half_ff = d_ff // 2

    def act_fn(hv):
        gate = hv[:, :half_ff]
        return hv, gate * _sigmoid(gate) * hv[:, half_ff:]

    h, act = _mm(u2, wf_ffn_in, tm=512, tn=d_ff, tk=d_model, name="ffn_in",
                 epilogue=(act_fn, [], [], [(d_ff, BF16), (half_ff, BF16)], []))
    def head_fn(xv, tg, g):
        xhat, r = _rms_stats(xv)
        err = xhat * g - tg
        dy = err * (1.0 / d_model)
        dx, dg_rows = _rms_bwd(dy, xhat, r, g)
        loss_lanes = (0.5 / d_model) * jnp.sum(err * err, axis=0, keepdims=True)
        return dx, dx, jnp.sum(dg_rows, axis=0, keepdims=True), loss_lanes

    dx2, dx2_b, dg_fin, loss_lanes = _mm(
        act, wf_ffn_out, add=x1, tm=512, tn=1024, tk=d_ff, name="ffn_out",
        epilogue=(head_fn, [tgt2d], [g_fin], [(d_model, F32), (d_model, BF16)], [(1, d_model), (1, d_model)]))

    def dact_fn(da, hv):
        gate, up = hv[:, :half_ff], hv[:, half_ff:]
        sg = _sigmoid(gate)
        dgate = da * up * (sg * (1.0 + gate * (1.0 - sg)))
        return (jnp.concatenate([dgate, da * (gate * sg)], axis=1),)

    dx2_b = _in_hbm(dx2_b)
    (dh,) = _mm(dx2_b, wf_ffn_out, tb=True, tm=512, tn=half_ff, tk=d_model, name="ffn_out_dx",
                epilogue=(dact_fn, [h], [], [(d_ff, BF16)], []))
    gw_ffn_out = _mm(act, dx2_b, ta=True, tm=256, tn=d_model, tk=t, name="ffn_out_dw")
    def norm_bwd_fn(du_, dres, xv, g):
        xhat, r = _rms_stats(xv)
        dx, dg_rows = _rms_bwd(du_, xhat, r, g)
        return dres + dx, jnp.sum(dg_rows, axis=0, keepdims=True)

    def norm_bwd_twice(*args):
        dx, dg = norm_bwd_fn(*args)
        return dx, dx, dg

    dx1, dx1_b, dg_ffn = _mm(dh, wf_ffn_in, tb=True, tm=512, tn=1024, tk=2 * d_ff, name="ffn_in_dx",
                             epilogue=(norm_bwd_twice, [dx2, x1], [g_ffn], [(d_model, F32), (d_model, BF16)], [(1, d_model)]))
    gwt_ffn_in = _mm(dh, u2, ta=True, tm=512, tn=d_model, tk=t, name="ffn_in_dw")

    dx1_b = _in_hbm(dx1_b)
    dmerged = _mm(dx1_b, wf_out, tb=True, out_dtype=BF16, tm=512, tn=1024, tk=d_model, name="mix_out_dx",
                  out_type=_hbm_array)
    gw_out = _mm(merged, dx1_b, ta=True, tm=256, tn=d_model, tk=t, name="mix_out_dw")

    def merge_bwd_fn(gt, ys, yd, dm):
        s_sb, s_dl = _sigmoid(gt[:, :d_model]), _sigmoid(gt[:, d_model:])
        dgates = jnp.concatenate([dm * ys * s_sb * (1.0 - s_sb), dm * yd * s_dl * (1.0 - s_dl)], axis=1)
        return dgates, dm * s_sb, dm * s_dl

    full_big = _chip_major({"w_out": gw_out, "w_ffn_in": gwt_ffn_in})
    dgates, dy_sb, dy_dl, others_big = _rowwise(
        merge_bwd_fn, [gates, y_sb, y_dl, dmerged], [], [(2 * d_model, BF16), (d_model, BF16), (d_model, BF16)], [],
        tm=256, name="merge_bwd", carry=_pair_carry(full_big))
    pair_big = _pair_sums(full_big, others_big, LATE[2:4])
    do_sb = _mm(dy_sb, wf_sb_up, tb=True, out_dtype=BF16, tm=1024, tn=SB_WIDTH, tk=d_model, name="sb_up_dx")
    gw_sb_up = _mm(o_sb2, dy_sb, ta=True, tm=SB_WIDTH, tn=1024, tk=512, name="sb_up_dw")
    do_dl = _mm(dy_dl, wf_dil_up, tb=True, tm=1024, tn=DIL_OUT_WIDTH, tk=d_model, name="dil_up_dx")
    gw_dil_up = _mm(o_dl2, dy_dl, ta=True, tm=DIL_OUT_WIDTH, tn=1024, tk=512, name="dil_up_dw")
    rest = [LATE[0], LATE[1], LATE[4]]
    full_rest = _chip_major({"w_sb_up": gw_sb_up, "w_dil_up": gw_dil_up, "w_ffn_out": gw_ffn_out})
    (dq_sb, dk_sb, dv_sb), brought = _sb_bwd(
        qkv3, o_sb, do_sb.reshape(b_sz, s_len, SB_WIDTH), b_sz, s_len,
        _chip_carry([p[1] for p in pair_big], LATE[2:4]) + _pair_carry(full_rest))
    pair_rest = _pair_sums(full_rest, brought[2:], rest)
    (dq_dl, dk_dl, dv_dl), landed_b = _dil_bwd(
        qkv3, o_dl, lse, do_dl.reshape(b_sz, s_len, DIL_OUT_WIDTH), b_sz, s_len,
        _chip_carry([p[1] for p in pair_rest], rest))
    pair = pair_rest[:2] + pair_big + pair_rest[2:]
    landed = [landed_b[0], landed_b[1], brought[0], brought[1], landed_b[2]]
    dproj = [a.reshape(t, -1) for a in (dq_sb, dk_sb, dv_sb, dq_dl, dk_dl, dv_dl)] + [dgates]
    gwt_in, gwt_in_b = _mm(dproj, u, ta=True, tm=256, tn=d_model, tk=t, name="proj_dw",
                           epilogue=(lambda tile: (tile, tile), [], [], [(d_model, F32), (d_model, BF16)], []))
    full_in = _chip_major({"w_in": gwt_in})
    pair_in = _pair_sums(full_in, _pair_exchange(_chip_major({"w_in": gwt_in_b}), "w_in"), ["w_in"])
    late_halves = _chip_sums(pair, landed, LATE)
    (dx, dg_mix), brought_in = _mm(
        dproj, wt_in, tm=512, tn=1024, tk=wt_in.shape[0], name="proj_dx",
        carry=_halves_carry([late_halves[k] for k in LATE]) + _chip_carry([p[1] for p in pair_in], ["w_in"]),
        epilogue=(norm_bwd_fn, [dx1, x2d], [g_mix], [(d_model, F32)], [(1, d_model)]))

    grads = dict(zip(LATE, brought_in[:len(LATE)]))
    grads.update(_chip_sums(pair_in, brought_in[len(LATE):], ["w_in"]))
    return dx, grads, dg_mix, dg_ffn, dg_fin, loss_lanes


def kernel(x, norm_mix_g, w_in, w_sb_up, w_dil_up, w_out, norm_ffn_g, w_ffn_in, w_ffn_out, norm_final_g, loss_target, m_norm_mix_g, m_w_in, m_w_sb_up, m_w_dil_up, m_w_out, m_norm_ffn_g, m_w_ffn_in, m_w_ffn_out, m_norm_final_g, v_norm_mix_g, v_w_in, v_w_sb_up, v_w_dil_up, v_w_out, v_norm_ffn_g, v_w_ffn_in, v_w_ffn_out, v_norm_final_g):
    b_sz, s_len, d_model = x.shape
    d_ff = w_ffn_out.shape[1] * N_CHIPS
    g_mix, g_ffn, g_fin = norm_mix_g, norm_ffn_g, norm_final_g.reshape(1, d_model)

    names = ["w_in", "w_sb_up", "w_dil_up", "w_out", "w_ffn_in", "w_ffn_out"]
    shards = {"w_in": jnp.swapaxes(w_in[0], 0, 1), "w_sb_up": w_sb_up[0], "w_dil_up": w_dil_up[0], "w_out": w_out[0],
              "w_ffn_in": w_ffn_in[0], "w_ffn_out": w_ffn_out[0]}
    slab_in = _cast_to_slab(shards["w_in"], "cast_w_in")

    dx, grads, dg_mix, dg_ffn, dg_fin, loss_lanes = _fwd_bwd(
        x, loss_target, g_mix, g_ffn, g_fin, slab_in, [shards[k] for k in LATE])

    small = jnp.concatenate([dg_mix, dg_ffn, dg_fin, loss_lanes, jnp.zeros((4, d_model), F32)], axis=0)
    (grads["w_in"],), small = _final_exchange([grads["w_in"]], small)
    grads["w_ffn_in"] = jnp.swapaxes(grads["w_ffn_in"], 0, 1)
    loss = small[3, 0]
    gains = jnp.concatenate([g_mix, g_ffn, g_fin, jnp.zeros((5, d_model), F32)], axis=0)
    gains_m = jnp.concatenate([m_norm_mix_g, m_norm_ffn_g, m_norm_final_g.reshape(1, d_model), jnp.zeros((5, d_model), F32)], axis=0)
    gains_v = jnp.concatenate([v_norm_mix_g, v_norm_ffn_g, v_norm_final_g.reshape(1, d_model), jnp.ones((5, d_model), F32)], axis=0)
    gd, gm, gv = _rowwise(_adamw_math, [gains, small, gains_m, gains_v], [], [(d_model, F32)] * 3, [], tm=8, name="adamw_gains")

    moments = {"w_in": (jnp.swapaxes(m_w_in[0], 0, 1), jnp.swapaxes(v_w_in[0], 0, 1)),
               "w_sb_up": (m_w_sb_up[0], v_w_sb_up[0]), "w_dil_up": (m_w_dil_up[0], v_w_dil_up[0]),
               "w_out": (m_w_out[0], v_w_out[0]), "w_ffn_in": (m_w_ffn_in[0], v_w_ffn_in[0]),
               "w_ffn_out": (m_w_ffn_out[0], v_w_ffn_out[0])}
    upd = {k: _adamw(shards[k], grads[k], moments[k][0], moments[k][1], "adamw_" + k) for k in names}

    def as_output(k, a):
        return (jnp.swapaxes(a, 0, 1) if k == "w_in" else a)[None]

    def w_out_of(i):
        return [as_output(k, upd[k][i]) for k in names]

    def ordered(mix, ws, ffn_g, fin):
        return [mix, ws[0], ws[1], ws[2], ws[3], ffn_g, ws[4], ws[5], fin]

    grad_ws = [as_output(k, grads[k]) for k in names]
    outs = [loss, dx.reshape(b_sz, s_len, d_model)]
    outs += ordered(small[0:1], grad_ws, small[1:2], small[2])
    outs += ordered(gd[0:1], w_out_of(0), gd[1:2], gd[2])
    outs += ordered(gm[0:1], w_out_of(1), gm[1:2], gm[2])
    outs += ordered(gv[0:1], w_out_of(2), gv[1:2], gv[2])
    return tuple(outs)
```

```python
import functools
import math

import jax
import jax.numpy as jnp
from jax import lax
from jax.experimental import pallas as pl
from jax.experimental.pallas import tpu as pltpu

F32 = jnp.float32
BF16 = jnp.bfloat16
MESH = pl.DeviceIdType.MESH

HEAD_DIM = 64
SB_HEADS = 8
DIL_PAIRS = ((128, 1), (512, 4), (2048, 16))
DIL_HEADS_PER_GROUP = 4
DIL_HEADS = DIL_HEADS_PER_GROUP * len(DIL_PAIRS)
SB_WIDTH = SB_HEADS * HEAD_DIM
DIL_WIDTH = DIL_HEADS * HEAD_DIM
DIL_OUT_WIDTH = DIL_HEADS_PER_GROUP * HEAD_DIM
QKV_WIDTH = 3 * SB_WIDTH + 3 * DIL_WIDTH
RMS_EPS = 1e-6
ALIBI_MAX_BIAS = 8.0
ADAM_LR = 0.001
ADAM_B1 = 0.9
ADAM_B2 = 0.999
ADAM_EPS = 1e-08
ADAM_WD = 0.01
ADAM_STEP = 10

LANES = 128
BLK = 128
NEG = -1e30
EXP_UNDERFLOW = -104.0
SB_FWD_CHAINS = 2
SB_BWD_CHAINS = 2
DIL_CHAINS = 4
N_CHIPS = 4
VMEM_CAP = 56 * 1024 * 1024


def _vmem_limit(tile_bytes):
    return int(min(VMEM_CAP, max(32 * 1024 * 1024, 3 * tile_bytes + 8 * 1024 * 1024)))


def _hbm_array(shape, dtype):
    return pltpu.HBM(shape, dtype)


def _nbytes(shape, dtype):
    return math.prod(shape) * jnp.dtype(dtype).itemsize


def _in_hbm(x):
    return pltpu.with_memory_space_constraint(x, pltpu.HBM)


def _dot(a, b):
    return jnp.dot(a, b, preferred_element_type=F32)


def _dot_nt(a, b):
    return lax.dot_general(a, b, (((1,), (1,)), ((), ())), preferred_element_type=F32)


def _dot_tn(a, b):
    return lax.dot_general(a, b, (((0,), (0,)), ((), ())), preferred_element_type=F32)


def _split2(x):
    hi = x.astype(BF16)
    lo = (x - hi.astype(F32)).astype(BF16)
    return hi, lo


def _sigmoid(x):
    return pl.reciprocal(1.0 + jnp.exp(-x), approx=True)


class _Carry:
    def __init__(self, arrays=(), out_shapes=(), aliased=False, sems=(), start=None, finish=None):
        self.arrays, self.out_shapes = list(arrays), list(out_shapes)
        self.n_aliased = len(self.arrays) if aliased is True else int(aliased)
        self.sems, self.start, self.finish = list(sems), start, finish

    def __bool__(self):
        return bool(self.arrays)

    def __add__(self, other):
        assert not other.n_aliased and self.n_aliased in (0, len(self.out_shapes))
        n_a, n_o, n_s = len(self.arrays), len(self.out_shapes), len(self.sems)
        return _Carry(
            self.arrays + other.arrays, self.out_shapes + other.out_shapes, self.n_aliased, self.sems + other.sems,
            lambda i, o, s: (self.start(i[:n_a], o[:n_o], s[:n_s]), other.start(i[n_a:], o[n_o:], s[n_s:])),
            lambda i, o, s: (self.finish(i[:n_a], o[:n_o], s[:n_s]), other.finish(i[n_a:], o[n_o:], s[n_s:])))

    def call_args(self, n_in, n_out):
        aliases = {n_in + k: n_out + k for k in range(self.n_aliased)}
        return _hbm_specs(len(self.arrays)), _hbm_specs(len(self.out_shapes)), self.out_shapes, aliases, self.sems

    def run(self, refs, n_in, n_out, step, n_steps, compute):
        if not self:
            compute()
            return
        n_c, n_o, n_s = len(self.arrays), len(self.out_shapes), len(self.sems)
        ins = refs[n_in:n_in + n_c]
        outs = refs[n_in + n_c + n_out:n_in + n_c + n_out + n_o]
        sems = refs[len(refs) - n_s:]

        @pl.when(step == 0)
        def _():
            self.start(ins, outs, sems)

        compute()

        @pl.when(step == n_steps - 1)
        def _():
            self.finish(ins, outs, sems)


def _mm(a, b, *, ta=False, tb=False, add=None, out_dtype=F32, tm, tn, tk, name, carry=None, epilogue=None,
        b_cols=None, out_type=jax.ShapeDtypeStruct):
    carry = carry or _Carry()
    n_car = len(carry.arrays)
    pieces = list(a) if isinstance(a, (list, tuple)) else [a]
    n_a = len(pieces)
    widths = [p.shape[1] for p in pieces]
    starts = [sum(widths[:p]) for p in range(n_a)]
    if ta:
        kdim, m = pieces[0].shape[0], sum(widths)
    else:
        m, kdim = pieces[0].shape[0], sum(widths)
    if tb:
        n, k2 = b.shape
    else:
        k2, n = b.shape
    col0 = 0
    if b_cols is not None:
        assert b_cols[0] % tn == 0, name
        col0, n = b_cols[0] // tn, b_cols[1]
    assert kdim == k2 and m % tm == 0 and n % tn == 0 and kdim % tk == 0, (name, a.shape, b.shape)
    nk = kdim // tk
    assert n_a == 1 or (nk == 1 and not tb and (not ta or all(w % tm == 0 for w in widths))), name
    grid = (m // tm, n // tn, nk)
    a_mode = dict(pipeline_mode=pl.Buffered(1)) if grid[0] == 1 and nk == 1 else {}
    b_mode = dict(pipeline_mode=pl.Buffered(1)) if grid[1] == 1 and nk == 1 else {}
    if n_a == 1:
        a_specs = [pl.BlockSpec((tk, tm), lambda i, j, k: (k, i), **a_mode) if ta
                   else pl.BlockSpec((tm, tk), lambda i, j, k: (i, k), **a_mode)]
    elif ta:
        a_specs = [pl.BlockSpec((tk, tm), lambda i, j, k, s=s // tm, w=w // tm: (0, jnp.clip(i - s, 0, w - 1)))
                   for s, w in zip(starts, widths)]
    else:
        a_specs = [pl.BlockSpec((tm, w), lambda i, j, k: (i, 0)) for w in widths]
    b_spec = (pl.BlockSpec((tn, tk), lambda i, j, k: (j + col0, k), **b_mode) if tb
              else pl.BlockSpec((tk, tn), lambda i, j, k: (k, j + col0), **b_mode))
    o_spec = pl.BlockSpec((tm, tn), lambda i, j, k: (i, j))
    dims = ((((0,) if ta else (1,)), ((1,) if tb else (0,))), ((), ()))
    has_add = add is not None
    if epilogue is None:
        ep_fn, ep_rows, ep_params, ep_outs, ep_accs = None, [], [], [], []
        out_sds, out_specs = [out_type((m, n), out_dtype)], [o_spec]
    else:
        ep_fn, ep_rows, ep_params, ep_outs, ep_accs = epilogue
        assert grid[1] == 1 or not ep_accs, name
        out_sds = [out_type((m, w * grid[1]), d) for w, d in ep_outs]
        out_sds += [jax.ShapeDtypeStruct(sh, F32) for sh in ep_accs]
        out_specs = [pl.BlockSpec((tm, w), lambda i, j, k: (i, j)) for w, _ in ep_outs]
        out_specs += [pl.BlockSpec(sh, lambda i, j, k: (0, 0)) for sh in ep_accs]
    n_main = len(out_sds)
    use_scratch = nk > 1 and (ep_fn is not None or jnp.dtype(out_dtype) != jnp.dtype(F32))
    n_in = n_a + 1 + has_add + len(ep_rows) + len(ep_params)

    def finish(total, refs, pid):
        outs = refs[n_in + n_car:n_in + n_car + n_main]
        if ep_fn is None:
            outs[0][...] = total.astype(out_dtype)
            return
        first = n_a + 1 + has_add
        rows = [r[...].astype(F32) for r in refs[first:first + len(ep_rows)]]
        params = [p[...] for p in refs[first + len(ep_rows):n_in]]
        res = ep_fn(total, *rows, *params)
        for o_ref, v in zip(outs[:len(ep_outs)], res):
            o_ref[...] = v.astype(o_ref.dtype)
        acc_refs = outs[len(ep_outs):]
        if acc_refs:
            @pl.when(pid[0] == 0)
            def _():
                for r in acc_refs:
                    r[...] = jnp.zeros(r.shape, F32)

            for r, v in zip(acc_refs, res[len(ep_outs):]):
                r[...] += v

    def compute(refs, pid):
        a_ref, b_ref = refs[0], refs[n_a]
        add_ref = refs[n_a + 1] if has_add else None

        def dot(x, y):
            return lax.dot_general(x.astype(BF16), y.astype(BF16), dims, preferred_element_type=F32)

        if n_a > 1 and ta:
            for p_ref, s, w in zip(refs[:n_a], starts, widths):
                @pl.when((pid[0] >= s // tm) & (pid[0] < (s + w) // tm))
                def _(p_ref=p_ref):
                    prod = dot(p_ref[...], b_ref[...])
                    finish(prod + add_ref[...] if has_add else prod, refs, pid)
            return
        if n_a > 1:
            prod = dot(a_ref[...], b_ref[:widths[0], :])
            for p_ref, s, w in zip(refs[1:n_a], starts[1:], widths[1:]):
                prod += dot(p_ref[...], b_ref[s:s + w, :])
        else:
            prod = dot(a_ref[...], b_ref[...])
        if nk == 1:
            finish(prod + add_ref[...] if has_add else prod, refs, pid)
            return
        acc_ref = refs[n_in + n_car + n_main + len(carry.out_shapes)] if use_scratch else refs[n_in + n_car]
        k = pid[2]

        @pl.when(k == 0)
        def _():
            acc_ref[...] = prod + add_ref[...] if has_add else prod

        @pl.when(k > 0)
        def _():
            acc_ref[...] += prod

        if use_scratch:
            @pl.when(k == nk - 1)
            def _():
                finish(acc_ref[...], refs, pid)

    def body(*refs):
        pid = (pl.program_id(0), pl.program_id(1), pl.program_id(2))
        step = (pid[0] * grid[1] + pid[1]) * nk + pid[2]
        carry.run(refs, n_in, n_main, step, grid[0] * grid[1] * nk, lambda: compute(refs, pid))

    tile_bytes = ((n_a if ta else 1) * _nbytes((tm, tk), pieces[0].dtype)
                  + _nbytes((tk, tn), b.dtype) + 2 * _nbytes((tm, tn), F32)
                  + (_nbytes((tm, tn), F32) if has_add else 0)
                  + sum(_nbytes((tm, r.shape[1]), r.dtype) for r in ep_rows) + sum(_nbytes((tm, w), d) for w, d in ep_outs))
    in_specs = a_specs + [b_spec] + ([o_spec] if has_add else [])
    in_specs += [pl.BlockSpec((tm, r.shape[1] // grid[1]), lambda i, j, k: (i, j)) for r in ep_rows]
    in_specs += [pl.BlockSpec(p.shape, lambda i, j, k: (0, 0)) for p in ep_params]
    args = tuple(pieces) + (b,) + ((add,) if has_add else ()) + tuple(ep_rows) + tuple(ep_params)
    scratch = [pltpu.VMEM((tm, tn), F32)] if use_scratch else []
    serial = bool(carry) or bool(ep_accs)
    c_in, c_out, c_shapes, c_alias, c_sems = carry.call_args(n_in, n_main)
    res = pl.pallas_call(
        body, name=name, grid=grid,
        in_specs=in_specs + c_in, out_specs=out_specs + c_out, out_shape=out_sds + c_shapes,
        input_output_aliases=c_alias, scratch_shapes=scratch + c_sems,
        compiler_params=pltpu.CompilerParams(
            dimension_semantics=("arbitrary",) * 3 if serial else ("parallel", "parallel", "arbitrary"),
            vmem_limit_bytes=_vmem_limit(tile_bytes)),
    )(*args, *carry.arrays)
    main = res[0] if ep_fn is None else list(res[:n_main])
    return (main, res[n_main:]) if carry else main


def _rowwise(fn, rows, params, outs, accs, *, tm, name, carry=None, out_type=jax.ShapeDtypeStruct):
    carry = carry or _Carry()
    t = rows[0].shape[0]
    assert t % tm == 0, (name, t, tm)
    n_r, n_p, n_o, n_c = len(rows), len(params), len(outs), len(carry.arrays)

    def compute(refs, first):
        vals = [r[...].astype(F32) for r in refs[:n_r]] + [p[...] for p in refs[n_r:n_r + n_p]]
        res = fn(*vals)
        o_refs = refs[n_r + n_p + n_c:n_r + n_p + n_c + n_o]
        a_refs = refs[n_r + n_p + n_c + n_o:n_r + n_p + n_c + n_o + len(accs)]
        for o_ref, v in zip(o_refs, res[:n_o]):
            o_ref[...] = v.astype(o_ref.dtype)
        if accs:
            @pl.when(first)
            def _():
                for a_ref in a_refs:
                    a_ref[...] = jnp.zeros(a_ref.shape, F32)

            for a_ref, v in zip(a_refs, res[n_o:]):
                a_ref[...] += v

    def body(*refs):
        step = pl.program_id(0)
        carry.run(refs, n_r + n_p, n_o + len(accs), step, t // tm, lambda: compute(refs, step == 0))

    in_specs = [pl.BlockSpec((tm, r.shape[1]), lambda i: (i, 0)) for r in rows]
    in_specs += [pl.BlockSpec(p.shape, lambda i: (0, 0)) for p in params]
    out_specs = [pl.BlockSpec((tm, w), lambda i: (i, 0)) for w, _ in outs]
    out_specs += [pl.BlockSpec(s, lambda i: (0, 0)) for s in accs]
    out_shape = [out_type((t, w), d) for w, d in outs]
    out_shape += [jax.ShapeDtypeStruct(s, F32) for s in accs]
    tile_bytes = sum(_nbytes((tm, r.shape[1]), r.dtype) for r in rows) + sum(_nbytes((tm, w), F32) for w, _ in outs)
    c_in, c_out, c_shapes, c_alias, c_sems = carry.call_args(n_r + n_p, n_o + len(accs))
    res = pl.pallas_call(
        body, name=name, grid=(t // tm,), in_specs=in_specs + c_in, out_specs=out_specs + c_out,
        out_shape=out_shape + c_shapes, input_output_aliases=c_alias, scratch_shapes=c_sems,
        compiler_params=pltpu.CompilerParams(
            dimension_semantics=("arbitrary",) if accs or carry else ("parallel",),
            vmem_limit_bytes=_vmem_limit(2 * tile_bytes)),
    )(*rows, *params, *carry.arrays)
    own = n_o + len(accs)
    return (list(res[:own]) + [res[own:]]) if carry else res


def _rms_stats(x):
    r = lax.rsqrt(jnp.mean(x * x, axis=-1, keepdims=True) + RMS_EPS)
    return x * r, r


def _rms_bwd(dy, xhat, r, g):
    dxhat = dy * g
    dx = r * (dxhat - xhat * jnp.mean(dxhat * xhat, axis=-1, keepdims=True))
    return dx, dy * xhat


def _sb_consts():
    lane = lax.broadcasted_iota(jnp.int32, (BLK, LANES), 1)
    head0 = lane < HEAD_DIM
    row = lax.broadcasted_iota(jnp.int32, (2 * BLK, BLK), 0) % BLK
    col = lax.broadcasted_iota(jnp.int32, (2 * BLK, BLK), 1)
    causal = col < row
    jj = lax.broadcasted_iota(jnp.int32, (BLK, BLK), 0)
    ss = lax.broadcasted_iota(jnp.int32, (BLK, BLK), 1)
    suffix = jnp.where(jj > ss, 1.0, 0.0).astype(BF16)
    return head0, causal, suffix


def _stack_heads(x, head0):
    zero = jnp.zeros_like(x)
    return jnp.concatenate([jnp.where(head0, x, zero), jnp.where(head0, zero, x)], axis=0)


def _sb_logits(z, causal, masked):
    sp = jnp.log(1.0 + jnp.exp(-jnp.abs(z)))
    log_keep = -(jnp.maximum(z, 0.0) + sp)
    log_beta = jnp.minimum(z, 0.0) - sp
    if masked:
        log_keep = jnp.where(causal, log_keep, 0.0)
    return log_keep, log_beta


def _suffix_sums(x, suffix):
    hi, lo = _split2(x)
    after = _dot(hi, suffix) + _dot(lo, suffix)
    total = jnp.broadcast_to(after[:, 0:1] + x[:, 0:1], x.shape)
    return after, total


def _sb_walk_back(i, state, per_chain, tile):
    def alive(st):
        worst = functools.reduce(jnp.maximum, [st[p][:, 0:1] for p in range(0, len(st), per_chain)])
        return jnp.max(worst) > EXP_UNDERFLOW

    def cond(c):
        return jnp.logical_and(c[0] < i, alive(c[1]))

    def body(c):
        return c[0] + 1, tile(i - 1 - c[0], c[1], False)

    return lax.while_loop(cond, body, (jnp.int32(0), state))[1]


def _lane_blocks(x, n):
    return [x[:, p * LANES:(p + 1) * LANES] for p in range(n)]


def _sb_fwd(qkv, b_sz, s_len, carry):
    nq = s_len // BLK
    n_pairs = SB_WIDTH // LANES
    ch = SB_FWD_CHAINS
    n_steps = n_pairs // ch
    scale = 1.0 / math.sqrt(HEAD_DIM)

    def compute(q_ref, k_ref, v_ref, o_ref):
        head0, causal, suffix = _sb_consts()

        def q_block(i, _):
            qs = pl.multiple_of(i * BLK, BLK)
            q_all = (q_ref[pl.ds(qs, BLK), :] * scale).astype(BF16)
            q01 = [_stack_heads(q, head0) for q in _lane_blocks(q_all, ch)]

            def tile(j, state, masked):
                ks = pl.multiple_of(j * BLK, BLK)
                ks_ = _lane_blocks(k_ref[pl.ds(ks, BLK), :].astype(BF16), ch)
                vs_ = _lane_blocks(v_ref[pl.ds(ks, BLK), :].astype(BF16), ch)
                zs = [_dot_nt(q01[p], ks_[p]) for p in range(ch)]
                logits = [_sb_logits(z, causal, masked) for z in zs]
                sums = [_suffix_sums(lg[0], suffix) for lg in logits]
                out = []
                for p in range(ch):
                    carry, acc = state[2 * p], state[2 * p + 1]
                    after, total = sums[p]
                    a = jnp.exp(logits[p][1] + carry + after)
                    if masked:
                        a = jnp.where(causal, a, 0.0)
                    a_hi, a_lo = _split2(a)
                    a_cat = jnp.concatenate([a_hi[:BLK], a_hi[BLK:], a_lo[:BLK], a_lo[BLK:]], axis=1)
                    v01 = _stack_heads(vs_[p], head0)
                    out += [carry + total, acc + _dot(a_cat, jnp.concatenate([v01, v01], axis=0))]
                return tuple(out)

            state = (jnp.zeros((2 * BLK, BLK), F32), jnp.zeros((BLK, LANES), F32)) * ch
            state = tile(i, state, True)
            state = _sb_walk_back(i, state, 2, tile)
            o_ref[pl.ds(qs, BLK), :] = jnp.concatenate([state[2 * p + 1] for p in range(ch)], axis=1)
            return 0

        lax.fori_loop(0, nq, q_block, 0)

    def body(*refs):
        step = pl.program_id(0) * n_steps + pl.program_id(1)
        o_ref = refs[3 + len(carry.arrays)]
        carry.run(refs, 3, 1, step, b_sz * n_steps, lambda: compute(refs[0], refs[1], refs[2], o_ref))

    blk = lambda off: pl.BlockSpec((None, s_len, ch * LANES), lambda b, p: (b, 0, off + p))
    c_in, c_out, c_shapes, c_alias, c_sems = carry.call_args(3, 1)
    res = pl.pallas_call(
        body, name="sb_fwd", grid=(b_sz, n_steps),
        in_specs=[blk(0), blk(n_steps), blk(2 * n_steps)] + c_in, out_specs=[blk(0)] + c_out,
        out_shape=[jax.ShapeDtypeStruct((b_sz, s_len, SB_WIDTH), F32)] + c_shapes,
        input_output_aliases=c_alias, scratch_shapes=c_sems,
        compiler_params=pltpu.CompilerParams(dimension_semantics=("arbitrary", "arbitrary"),
                                             vmem_limit_bytes=VMEM_CAP),
    )(qkv, qkv, qkv, *carry.arrays)
    return res[0], res[1:]


def _sb_bwd(qkv, o_sb, do_sb, b_sz, s_len, carry):
    nq = s_len // BLK
    n_pairs = SB_WIDTH // LANES
    ch = SB_BWD_CHAINS
    n_steps = n_pairs // ch
    scale = 1.0 / math.sqrt(HEAD_DIM)

    def compute(q_ref, k_ref, v_ref, o_ref, do_ref, dq_ref, dk_ref, dv_ref, dk_acc, dv_acc):
        head0, causal, suffix = _sb_consts()
        lrow = lax.broadcasted_iota(jnp.int32, (LANES, LANES), 0)
        ones_h0 = jnp.where(lrow < HEAD_DIM, 1.0, 0.0).astype(BF16)
        ones_h1 = jnp.where(lrow >= HEAD_DIM, 1.0, 0.0).astype(BF16)
        dk_acc[...] = jnp.zeros(dk_acc.shape, F32)
        dv_acc[...] = jnp.zeros(dv_acc.shape, F32)

        def q_block(i, _):
            qs = pl.multiple_of(i * BLK, BLK)
            q_all = (q_ref[pl.ds(qs, BLK), :] * scale).astype(BF16)
            do_all = do_ref[pl.ds(qs, BLK), :].astype(BF16)
            dd_all = do_all.astype(F32) * o_ref[pl.ds(qs, BLK), :]
            q01 = [_stack_heads(q, head0) for q in _lane_blocks(q_all, ch)]
            do01 = [_stack_heads(d, head0) for d in _lane_blocks(do_all, ch)]
            tot = []
            for dd in _lane_blocks(dd_all, ch):
                dd_hi, dd_lo = _split2(dd)
                tot.append(jnp.concatenate([_dot(dd_hi, ones_h0) + _dot(dd_lo, ones_h0),
                                            _dot(dd_hi, ones_h1) + _dot(dd_lo, ones_h1)], axis=0))

            def tile(j, state, masked):
                ks = pl.multiple_of(j * BLK, BLK)
                ks_ = _lane_blocks(k_ref[pl.ds(ks, BLK), :].astype(BF16), ch)
                vs_ = _lane_blocks(v_ref[pl.ds(ks, BLK), :].astype(BF16), ch)
                zs = [_dot_nt(q01[p], ks_[p]) for p in range(ch)]
                das = [_dot_nt(do01[p], vs_[p]) for p in range(ch)]
                logits = [_sb_logits(z, causal, masked) for z in zs]
                sums = [_suffix_sums(lg[0], suffix) for lg in logits]
                a_s, e_s = [], []
                for p in range(ch):
                    a = jnp.exp(logits[p][1] + state[3 * p] + sums[p][0])
                    if masked:
                        a = jnp.where(causal, a, 0.0)
                    a_s.append(a)
                    e_s.append(a * das[p])
                e_sums = [_suffix_sums(e, suffix) for e in e_s]
                out, dks, dvs = [], [], []
                for p in range(ch):
                    carry, rcarry, dq = state[3 * p:3 * p + 3]
                    e = e_s[p]
                    before = tot[p] - (rcarry + e_sums[p][0] + e)
                    beta = jnp.exp(logits[p][1])
                    dz = e * (1.0 - beta) - beta * before
                    if masked:
                        dz = jnp.where(causal, dz, 0.0)
                    dz_b = dz.astype(BF16)
                    dks.append(_dot_tn(dz_b, q01[p]))
                    dvs.append(_dot_tn(a_s[p].astype(BF16), do01[p]))
                    out += [carry + sums[p][1], rcarry + e_sums[p][1], dq + _dot(dz_b, ks_[p])]
                dk_acc[pl.ds(ks, BLK), :] += jnp.concatenate(dks, axis=1)
                dv_acc[pl.ds(ks, BLK), :] += jnp.concatenate(dvs, axis=1)
                return tuple(out)

            state = (jnp.zeros((2 * BLK, BLK), F32),) * (3 * ch)
            state = tile(i, state, True)
            state = _sb_walk_back(i, state, 3, tile)
            dq = [jnp.where(head0, state[3 * p + 2][:BLK], state[3 * p + 2][BLK:]) for p in range(ch)]
            dq_ref[pl.ds(qs, BLK), :] = (jnp.concatenate(dq, axis=1) * scale).astype(dq_ref.dtype)
            return 0

        lax.fori_loop(0, nq, q_block, 0)
        dk_ref[...] = dk_acc[...].astype(dk_ref.dtype)
        dv_ref[...] = dv_acc[...].astype(dv_ref.dtype)

    def body(*refs):
        step = pl.program_id(0) * n_steps + pl.program_id(1)
        n_c, n_o = len(carry.arrays), len(carry.out_shapes)
        own = refs[:5] + refs[5 + n_c:8 + n_c] + refs[8 + n_c + n_o:10 + n_c + n_o]
        carry.run(refs, 5, 3, step, b_sz * n_steps, lambda: compute(*own))

    blk = lambda off: pl.BlockSpec((None, s_len, ch * LANES), lambda b, p: (b, 0, off + p))
    once = lambda off: pl.BlockSpec((None, s_len, ch * LANES), lambda b, p: (b, 0, off + p),
                                    pipeline_mode=pl.Buffered(1))
    out_sd = jax.ShapeDtypeStruct((b_sz, s_len, SB_WIDTH), BF16)
    c_in, c_out, c_shapes, c_alias, c_sems = carry.call_args(5, 3)
    res = pl.pallas_call(
        body, name="sb_bwd", grid=(b_sz, n_steps),
        in_specs=[once(0), once(n_steps), once(2 * n_steps), once(0), once(0)] + c_in,
        out_specs=[blk(0), blk(0), blk(0)] + c_out, out_shape=[out_sd, out_sd, out_sd] + c_shapes,
        input_output_aliases=c_alias,
        scratch_shapes=[pltpu.VMEM((s_len, ch * LANES), F32), pltpu.VMEM((s_len, ch * LANES), F32)] + c_sems,
        compiler_params=pltpu.CompilerParams(dimension_semantics=("arbitrary", "arbitrary"),
                                             vmem_limit_bytes=VMEM_CAP),
    )(qkv, qkv, qkv, o_sb, do_sb, *carry.arrays)
    return res[:3], res[3:]


def _dil_consts(group, pair_idx, dilation):
    lane = lax.broadcasted_iota(jnp.int32, (BLK, LANES), 1)
    head0 = lane < HEAD_DIM
    row = lax.broadcasted_iota(jnp.int32, (2 * BLK, BLK), 0)
    qa = row % BLK
    kb = lax.broadcasted_iota(jnp.int32, (2 * BLK, BLK), 1)
    head = (group * DIL_HEADS_PER_GROUP + 2 * pair_idx + row // BLK).astype(F32)
    slope = jnp.exp((-ALIBI_MAX_BIAS * math.log(2.0) / DIL_HEADS) * (head + 1.0))
    valid_cur = kb <= qa
    valid_prev = kb >= qa
    bias_cur = -slope * ((qa - kb) * dilation).astype(F32)
    bias_prev = -slope * ((BLK + qa - kb) * dilation).astype(F32)
    return head0, valid_cur, valid_prev, bias_cur, bias_prev


def _dil_units(s_len, dilation):
    nb = s_len // dilation // BLK
    return [(r, n) for r in range(dilation) for n in range(nb)]


def _dil_rows(n, r, dilation):
    if dilation == 1:
        return pl.ds(n * BLK, BLK)
    return pl.ds(n * BLK * dilation + r, BLK, stride=dilation)


def _dil_scores(q01, k, bias, valid):
    s = _dot_nt(q01, k) * (1.0 / math.sqrt(HEAD_DIM)) + bias
    return jnp.where(valid, s, NEG)


def _dil_fwd(qkv, b_sz, s_len, carry):
    n_pairs = DIL_OUT_WIDTH // LANES
    q_off = 3 * SB_WIDTH // LANES
    per_kind = DIL_WIDTH // LANES

    def compute(pair_idx, qkv_refs, o_ref, lse_ref, m_s, l_s):
        m_s[...] = jnp.full(m_s.shape, NEG, F32)
        l_s[...] = jnp.zeros(l_s.shape, F32)
        o_ref[...] = jnp.zeros(o_ref.shape, F32)
        for g, (_, dilation) in enumerate(DIL_PAIRS):
            q_ref, k_ref, v_ref = qkv_refs[3 * g:3 * g + 3]
            head0, valid_cur, valid_prev, bias_cur, bias_prev = _dil_consts(g, pair_idx, dilation)
            units = _dil_units(s_len, dilation)
            for u0 in range(0, len(units), DIL_CHAINS):
                group = units[u0:u0 + DIL_CHAINS]
                rows_of = [_dil_rows(n, r, dilation) for r, n in group]
                scores, values = [], []
                for (r, n), rows in zip(group, rows_of):
                    q01 = _stack_heads(q_ref[rows, :].astype(BF16), head0)
                    sc = [_dil_scores(q01, k_ref[rows, :].astype(BF16), bias_cur, valid_cur)]
                    vals = [_stack_heads(v_ref[rows, :].astype(BF16), head0)]
                    if n > 0:
                        prev = _dil_rows(n - 1, r, dilation)
                        sc.append(_dil_scores(q01, k_ref[prev, :].astype(BF16), bias_prev, valid_prev))
                        vals.append(_stack_heads(v_ref[prev, :].astype(BF16), head0))
                    scores.append(sc)
                    values.append(vals)
                stats = []
                for sc, rows in zip(scores, rows_of):
                    m_blk = functools.reduce(jnp.maximum, [jnp.max(x, axis=-1, keepdims=True) for x in sc])
                    m_old = jnp.concatenate([m_s.at[0][rows, :], m_s.at[1][rows, :]], axis=0)
                    l_old = jnp.concatenate([l_s.at[0][rows, :], l_s.at[1][rows, :]], axis=0)
                    m_new = jnp.maximum(m_old, m_blk)
                    probs = [jnp.exp(x - m_new) for x in sc]
                    l_blk = functools.reduce(jnp.add, [jnp.sum(p, axis=-1, keepdims=True) for p in probs])
                    alpha = jnp.exp(m_old - m_new)
                    stats.append((m_new, alpha * l_old + l_blk, alpha, probs))
                for (m_new, l_new, alpha, probs), vals, rows in zip(stats, values, rows_of):
                    alpha_tok = jnp.where(head0, alpha[:BLK], alpha[BLK:])
                    p_cat = jnp.concatenate(
                        [h for p in probs for h in (p[:BLK].astype(BF16), p[BLK:].astype(BF16))], axis=1)
                    o_ref[rows, :] = alpha_tok * o_ref[rows, :] + _dot(p_cat, jnp.concatenate(vals, axis=0))
                    m_s.at[0][rows, :] = m_new[:BLK]
                    m_s.at[1][rows, :] = m_new[BLK:]
                    l_s.at[0][rows, :] = l_new[:BLK]
                    l_s.at[1][rows, :] = l_new[BLK:]
        lane = lax.broadcasted_iota(jnp.int32, (BLK, LANES), 1)
        for c in range(s_len // BLK):
            rows = pl.ds(c * BLK, BLK)
            l0, l1 = l_s.at[0][rows, :], l_s.at[1][rows, :]
            o_ref[rows, :] = o_ref[rows, :] / jnp.where(lane < HEAD_DIM, l0, l1)
            lse_ref.at[0][rows, :] = m_s.at[0][rows, :] + jnp.log(l0)
            lse_ref.at[1][rows, :] = m_s.at[1][rows, :] + jnp.log(l1)

    def body(*refs):
        pair_idx = pl.program_id(1)
        step = pl.program_id(0) * n_pairs + pair_idx
        n_c, n_o = len(carry.arrays), len(carry.out_shapes)
        o_ref, lse_ref = refs[9 + n_c:11 + n_c]
        m_s, l_s = refs[11 + n_c + n_o:13 + n_c + n_o]
        carry.run(refs, 9, 2, step, b_sz * n_pairs, lambda: compute(pair_idx, refs[:9], o_ref, lse_ref, m_s, l_s))

    in_specs = []
    for g in range(len(DIL_PAIRS)):
        for kind in range(3):
            off = q_off + kind * per_kind + g * n_pairs
            in_specs.append(pl.BlockSpec((None, s_len, LANES), lambda b, p, off=off: (b, 0, off + p)))
    c_in, c_out, c_shapes, c_alias, c_sems = carry.call_args(9, 2)
    res = pl.pallas_call(
        body, name="dil_fwd", grid=(b_sz, n_pairs),
        in_specs=in_specs + c_in,
        out_specs=[pl.BlockSpec((None, s_len, LANES), lambda b, p: (b, 0, p)),
                   pl.BlockSpec((None, None, 2, s_len, LANES), lambda b, p: (b, p, 0, 0, 0))] + c_out,
        out_shape=[jax.ShapeDtypeStruct((b_sz, s_len, DIL_OUT_WIDTH), F32),
                   jax.ShapeDtypeStruct((b_sz, n_pairs, 2, s_len, LANES), F32)] + c_shapes,
        input_output_aliases=c_alias,
        scratch_shapes=[pltpu.VMEM((2, s_len, LANES), F32), pltpu.VMEM((2, s_len, LANES), F32)] + c_sems,
        compiler_params=pltpu.CompilerParams(dimension_semantics=("arbitrary", "arbitrary"),
                                             vmem_limit_bytes=VMEM_CAP),
    )(*([qkv] * 9), *carry.arrays)
    return res[0], res[1], res[2:]


def _dil_bwd(qkv, o_dl, lse, do_dl, b_sz, s_len, carry):
    n_pairs = DIL_OUT_WIDTH // LANES
    n_groups = len(DIL_PAIRS)
    q_off = 3 * SB_WIDTH // LANES
    per_kind = DIL_WIDTH // LANES

    def compute(pair_idx, group, q_ref, k_ref, v_ref, o_ref, lse_ref, do_ref, dq_ref, dk_ref, dv_ref, d_s, dq_s, dk_s, dv_s):
        lrow = lax.broadcasted_iota(jnp.int32, (LANES, LANES), 0)
        ones_h0 = jnp.where(lrow < HEAD_DIM, 1.0, 0.0).astype(BF16)
        ones_h1 = jnp.where(lrow >= HEAD_DIM, 1.0, 0.0).astype(BF16)
        for c in range(s_len // BLK):
            rows = pl.ds(c * BLK, BLK)
            dd_hi, dd_lo = _split2(do_ref[rows, :] * o_ref[rows, :])
            d_s.at[0][rows, :] = _dot(dd_hi, ones_h0) + _dot(dd_lo, ones_h0)
            d_s.at[1][rows, :] = _dot(dd_hi, ones_h1) + _dot(dd_lo, ones_h1)
        dk_s[...] = jnp.zeros(dk_s.shape, F32)
        dv_s[...] = jnp.zeros(dv_s.shape, F32)

        def one_group(g, dilation):
            head0, valid_cur, valid_prev, bias_cur, bias_prev = _dil_consts(g, pair_idx, dilation)
            units = _dil_units(s_len, dilation)
            scale = 1.0 / math.sqrt(HEAD_DIM)
            for u0 in range(0, len(units), DIL_CHAINS):
                chunk = units[u0:u0 + DIL_CHAINS]
                loaded = []
                for r, n in chunk:
                    rows = _dil_rows(n, r, dilation)
                    q01 = _stack_heads(q_ref[rows, :].astype(BF16), head0)
                    do01 = _stack_heads(do_ref[rows, :].astype(BF16), head0)
                    lse01 = jnp.concatenate([lse_ref.at[0][rows, :], lse_ref.at[1][rows, :]], axis=0)
                    d01 = jnp.concatenate([d_s.at[0][rows, :], d_s.at[1][rows, :]], axis=0)
                    blocks = [(rows, bias_cur, valid_cur)]
                    if n > 0:
                        blocks.append((_dil_rows(n - 1, r, dilation), bias_prev, valid_prev))
                    parts = []
                    for krows, bias, valid in blocks:
                        k = k_ref[krows, :].astype(BF16)
                        v = v_ref[krows, :].astype(BF16)
                        parts.append((krows, k, _dil_scores(q01, k, bias, valid), _dot_nt(do01, v)))
                    loaded.append((rows, q01, do01, lse01, d01, parts))
                grads = []
                for rows, q01, do01, lse01, d01, parts in loaded:
                    for krows, k, sc, dp in parts:
                        p = jnp.exp(sc - lse01)
                        grads.append((p.astype(BF16), (p * (dp - d01) * scale).astype(BF16)))
                it = iter(grads)
                updates = []
                for rows, q01, do01, lse01, d01, parts in loaded:
                    dq = jnp.zeros((2 * BLK, LANES), F32)
                    for krows, k, sc, dp in parts:
                        p_b, ds = next(it)
                        dq = dq + _dot(ds, k)
                        updates.append((krows, _dot_tn(ds, q01), _dot_tn(p_b, do01)))
                    dq_s[rows, :] = jnp.where(head0, dq[:BLK], dq[BLK:])
                for krows, dk, dv in updates:
                    dk_s[krows, :] = dk_s[krows, :] + dk
                    dv_s[krows, :] = dv_s[krows, :] + dv

        for g, (_, dilation) in enumerate(DIL_PAIRS):
            pl.when(group == g)(functools.partial(one_group, g, dilation))
        dq_ref[...] = dq_s[...].astype(dq_ref.dtype)
        dk_ref[...] = dk_s[...].astype(dk_ref.dtype)
        dv_ref[...] = dv_s[...].astype(dv_ref.dtype)

    def body(*refs):
        pair_idx, group = pl.program_id(1), pl.program_id(2)
        step = (pl.program_id(0) * n_pairs + pair_idx) * n_groups + group
        n_c, n_o = len(carry.arrays), len(carry.out_shapes)
        own = refs[:6] + refs[6 + n_c:9 + n_c] + refs[9 + n_c + n_o:13 + n_c + n_o]
        carry.run(refs, 6, 3, step, b_sz * n_pairs * n_groups, lambda: compute(pair_idx, group, *own))

    def qkv_spec(kind):
        return pl.BlockSpec((None, s_len, LANES),
                            lambda b, p, g: (b, 0, q_off + kind * per_kind + g * n_pairs + p))

    tok_spec = pl.BlockSpec((None, s_len, LANES), lambda b, p, g: (b, 0, p))
    out_spec = pl.BlockSpec((None, s_len, LANES), lambda b, p, g: (b, 0, g * n_pairs + p))
    out_sd = jax.ShapeDtypeStruct((b_sz, s_len, DIL_WIDTH), BF16)
    c_in, c_out, c_shapes, c_alias, c_sems = carry.call_args(6, 3)
    res = pl.pallas_call(
        body, name="dil_bwd", grid=(b_sz, n_pairs, n_groups),
        in_specs=[qkv_spec(0), qkv_spec(1), qkv_spec(2), tok_spec,
                  pl.BlockSpec((None, None, 2, s_len, LANES), lambda b, p, g: (b, p, 0, 0, 0)), tok_spec] + c_in,
        out_specs=[out_spec, out_spec, out_spec] + c_out,
        out_shape=[out_sd, out_sd, out_sd] + c_shapes,
        input_output_aliases=c_alias,
        scratch_shapes=[pltpu.VMEM((2, s_len, LANES), F32)] + [pltpu.VMEM((s_len, LANES), F32)] * 3 + c_sems,
        compiler_params=pltpu.CompilerParams(dimension_semantics=("arbitrary", "arbitrary", "arbitrary"),
                                             vmem_limit_bytes=VMEM_CAP),
    )(qkv, qkv, qkv, o_dl, lse, do_dl, *carry.arrays)
    return res[:3], res[3:]


def _mesh_pos():
    return lax.axis_index("x"), lax.axis_index("y"), lax.axis_index("c")


def _other_chips(x, y):
    return [(1 - x, y), (x, 1 - y), (1 - x, 1 - y)]


def _hbm_specs(n):
    return [pl.BlockSpec(memory_space=pl.ANY)] * n


SWAPPED = ("w_ffn_in",)


def _slot(x, y, swapped):
    return 2 * y + x if swapped else 2 * x + y


def _cast_to_slab(w, name):
    rows, cols = w.shape
    mine = jnp.reshape(_slot(lax.axis_index("x"), lax.axis_index("y"), False), (1,)).astype(jnp.int32)

    def body(idx_ref, w_ref, o_ref):
        o_ref[...] = w_ref[...].astype(BF16)

    return pl.pallas_call(
        body, name=name,
        grid_spec=pltpu.PrefetchScalarGridSpec(
            num_scalar_prefetch=1, grid=(1,),
            in_specs=[pl.BlockSpec((rows, cols), lambda i, idx: (0, 0))],
            out_specs=pl.BlockSpec((None, rows, cols), lambda i, idx: (idx[0], 0, 0))),
        out_shape=_hbm_array((N_CHIPS, rows, cols), BF16),
        compiler_params=pltpu.CompilerParams(vmem_limit_bytes=_vmem_limit(rows * cols * 6)),
    )(mine, w)


def _gather_issue(slabs, send_sems, recv_sems, swapped):
    x, y, c = _mesh_pos()
    for k, slab in enumerate(slabs):
        half = slab.shape[1] // 2
        rows = slab.at[_slot(x, y, swapped[k]), pl.ds(c * half, half), :]
        for r, (px, py) in enumerate(_other_chips(x, y)):
            pltpu.make_async_remote_copy(
                src_ref=rows, dst_ref=rows, send_sem=send_sems.at[6 * k + r], recv_sem=recv_sems.at[6 * k + r],
                device_id=(px, py, c), device_id_type=MESH).start()


def _gather_complete(slabs, send_sems, recv_sems, swapped):
    x, y, c = _mesh_pos()
    chips = _other_chips(x, y)

    def copy(k, sem, block, rows, to):
        ref = slabs[k].at[block, rows, :]
        return pltpu.make_async_remote_copy(
            src_ref=ref, dst_ref=ref, send_sem=send_sems.at[sem], recv_sem=recv_sems.at[sem],
            device_id=to, device_id_type=MESH)

    for k, slab in enumerate(slabs):
        half = slab.shape[1] // 2
        for r, (px, py) in enumerate(chips):
            theirs = _slot(px, py, swapped[k])
            copy(k, 6 * k + r, theirs, pl.ds(c * half, half), (px, py, c)).wait_recv()
            copy(k, 6 * k + 3 + r, theirs, pl.ds(c * half, half), (x, y, 1 - c)).start()
    for k, slab in enumerate(slabs):
        half = slab.shape[1] // 2
        for r, (px, py) in enumerate(chips):
            copy(k, 6 * k + 3 + r, _slot(px, py, swapped[k]), pl.ds((1 - c) * half, half), (x, y, 1 - c)).wait_recv()
    for k, slab in enumerate(slabs):
        half = slab.shape[1] // 2
        for r, (px, py) in enumerate(chips):
            copy(k, 6 * k + r, _slot(x, y, swapped[k]), pl.ds(c * half, half), (px, py, c)).wait_send()
            copy(k, 6 * k + 3 + r, _slot(px, py, swapped[k]), pl.ds(c * half, half), (x, y, 1 - c)).wait_send()


def _gather_sems(n):
    return [pltpu.SemaphoreType.DMA((6 * n,)), pltpu.SemaphoreType.DMA((6 * n,))]


def _gather_carry(slabs, names):
    swapped = [k in SWAPPED for k in names]
    return _Carry(slabs, [_hbm_array(a.shape, a.dtype) for a in slabs], True, _gather_sems(len(slabs)),
                  lambda ins, outs, sems: _gather_issue(outs, *sems, swapped),
                  lambda ins, outs, sems: _gather_complete(outs, *sems, swapped))


def _cast_carry(shards, names):
    swapped = [k in SWAPPED for k in names]

    def start(ins, outs, sems):
        x, y, _ = _mesh_pos()
        for w, slab, sw in zip(ins, outs, swapped):
            def cast(f32_buf, bf16_buf, sem, w=w, slab=slab, sw=sw):
                load = pltpu.make_async_copy(w, f32_buf, sem)
                load.start()
                load.wait()
                bf16_buf[...] = f32_buf[...].astype(BF16)
                store = pltpu.make_async_copy(bf16_buf, slab.at[_slot(x, y, sw)], sem)
                store.start()
                store.wait()

            pl.run_scoped(cast, pltpu.VMEM(w.shape, F32), pltpu.VMEM(w.shape, BF16), pltpu.SemaphoreType.DMA)

    return _Carry(shards, [_hbm_array((N_CHIPS,) + w.shape, BF16) for w in shards], False, [], start,
                  lambda ins, outs, sems: None)


def _pair_copies(ins, outs, send_sems, recv_sems):
    x, y, c = _mesh_pos()
    copies = []
    for k, g in enumerate(ins):
        half = g.shape[1] // 2
        copies.append(pltpu.make_async_remote_copy(
            src_ref=g.at[:, pl.ds((1 - c) * half, half), :], dst_ref=outs[k],
            send_sem=send_sems.at[k], recv_sem=recv_sems.at[k],
            device_id=(x, y, 1 - c), device_id_type=MESH))
    return copies


def _pair_carry(grads):
    n = len(grads)

    def start(ins, outs, sems):
        for cp in _pair_copies(ins, outs, *sems):
            cp.start()

    def finish(ins, outs, sems):
        for cp in _pair_copies(ins, outs, *sems):
            cp.wait()

    return _Carry(grads, [jax.ShapeDtypeStruct((N_CHIPS, g.shape[1] // 2, g.shape[2]), g.dtype) for g in grads], False,
                  [pltpu.SemaphoreType.DMA((n,)), pltpu.SemaphoreType.DMA((n,))], start, finish)


def _pair_exchange(grads, tag):
    carry = _pair_carry(grads)
    n = len(grads)

    def body(*refs):
        carry.start(refs[:n], refs[n:2 * n], refs[2 * n:])
        carry.finish(refs[:n], refs[n:2 * n], refs[2 * n:])

    return pl.pallas_call(
        body, name="grad_pair_exchange_" + tag, in_specs=_hbm_specs(n), out_specs=_hbm_specs(n),
        out_shape=carry.out_shapes, scratch_shapes=carry.sems,
    )(*grads)


def _pair_sum(grad, other, name, swapped):
    _, rows, cols = grad.shape
    half = rows // 2
    x, y, c = _mesh_pos()
    idx = jnp.stack([c, _slot(x, y, swapped)]).astype(jnp.int32)

    def body(idx_ref, g_ref, p_ref, own_ref, sb_ref):
        s = g_ref[...] + p_ref[...].astype(F32)
        sb_ref[...] = s.astype(BF16)

        @pl.when(pl.program_id(0) == idx_ref[1])
        def _():
            own_ref[...] = s

    blk = pl.BlockSpec((None, half, cols), lambda p, idx: (p, 0, 0))
    return pl.pallas_call(
        body, name=name,
        grid_spec=pltpu.PrefetchScalarGridSpec(
            num_scalar_prefetch=1, grid=(N_CHIPS,),
            in_specs=[pl.BlockSpec((None, half, cols), lambda p, idx: (p, idx[0], 0)), blk],
            out_specs=[pl.BlockSpec((half, cols), lambda p, idx: (0, 0)), blk]),
        out_shape=[jax.ShapeDtypeStruct((half, cols), F32), jax.ShapeDtypeStruct((N_CHIPS, half, cols), BF16)],
        compiler_params=pltpu.CompilerParams(dimension_semantics=("arbitrary",),
                                             vmem_limit_bytes=_vmem_limit(4 * half * cols * 4)),
    )(idx, grad, other)


def _chip_copies(sums_bf16, lands, send_sems, recv_sems, swapped):
    x, y, c = _mesh_pos()
    return [pltpu.make_async_remote_copy(
        src_ref=sums_bf16[k].at[_slot(px, py, swapped[k])], dst_ref=lands[k].at[r],
        send_sem=send_sems.at[3 * k + r], recv_sem=recv_sems.at[3 * k + r],
        device_id=(px, py, c), device_id_type=MESH)
        for k in range(len(sums_bf16)) for r, (px, py) in enumerate(_other_chips(x, y))]


def _chip_carry(sums_bf16, names):
    swapped = [k in SWAPPED for k in names]

    def start(ins, outs, sems):
        for cp in _chip_copies(ins, outs, *sems, swapped):
            cp.start()

    def finish(ins, outs, sems):
        for cp in _chip_copies(ins, outs, *sems, swapped):
            cp.wait()

    return _Carry(sums_bf16, _chip_landing(sums_bf16), False, _chip_sems(len(sums_bf16)), start, finish)


def _chip_sems(n):
    return [pltpu.SemaphoreType.DMA((3 * n,)), pltpu.SemaphoreType.DMA((3 * n,))]


def _chip_landing(sums_bf16):
    return [jax.ShapeDtypeStruct((N_CHIPS - 1,) + s.shape[1:], BF16) for s in sums_bf16]


def _chip_sum(own, landed, name):
    rows, cols = own.shape
    core = jnp.reshape(lax.axis_index("c"), (1,)).astype(jnp.int32)

    def body(core_ref, o_ref, l_ref, out_ref):
        out_ref[...] = ((o_ref[...] + l_ref[0].astype(F32)) + l_ref[1].astype(F32)) + l_ref[2].astype(F32)

    return pl.pallas_call(
        body, name=name,
        grid_spec=pltpu.PrefetchScalarGridSpec(
            num_scalar_prefetch=1, grid=(1,),
            in_specs=[pl.BlockSpec((rows, cols), lambda i, core_ref: (0, 0)),
                      pl.BlockSpec((N_CHIPS - 1, rows, cols), lambda i, core_ref: (0, 0, 0))],
            out_specs=pl.BlockSpec((rows, cols), lambda i, core_ref: (core_ref[0], 0))),
        out_shape=jax.ShapeDtypeStruct((2 * rows, cols), F32),
        compiler_params=pltpu.CompilerParams(vmem_limit_bytes=_vmem_limit(3 * rows * cols * 4)),
    )(core, own, landed)


def _halves_carry(fulls):
    n = len(fulls)

    def copies(outs, send_sems, recv_sems, own):
        x, y, c = _mesh_pos()
        res = []
        for k, out in enumerate(outs):
            half = out.shape[0] // 2
            rows = out.at[pl.ds((c if own else 1 - c) * half, half), :]
            res.append(pltpu.make_async_remote_copy(
                src_ref=rows, dst_ref=rows, send_sem=send_sems.at[k], recv_sem=recv_sems.at[k],
                device_id=(x, y, 1 - c), device_id_type=MESH))
        return res

    def start(ins, outs, sems):
        for cp in copies(outs, *sems, True):
            cp.start()

    def finish(ins, outs, sems):
        for cp in copies(outs, *sems, False):
            cp.wait_recv()
        for cp in copies(outs, *sems, True):
            cp.wait_send()

    return _Carry(fulls, [jax.ShapeDtypeStruct(f.shape, F32) for f in fulls], True,
                  [pltpu.SemaphoreType.DMA((n,)), pltpu.SemaphoreType.DMA((n,))], start, finish)


def _final_exchange(fulls, v):
    n = len(fulls)
    rows, cols = v.shape
    n_dev = 8

    def body(*refs):
        v_ref, out_ref = refs[0], refs[1 + 2 * n]
        outs = refs[1 + n:1 + 2 * n]
        buf, v_send, v_recv, h_send, h_recv = refs[2 + 2 * n:]
        x, y, c = _mesh_pos()
        me = 4 * x + 2 * y + c
        buf[me] = v_ref[...]
        peers = [(1 - x if r & 4 else x, 1 - y if r & 2 else y, 1 - c if r & 1 else c) for r in range(1, n_dev)]
        copies = []
        for r, peer in enumerate(peers):
            copies.append(pltpu.make_async_remote_copy(
                src_ref=v_ref, dst_ref=buf.at[me], send_sem=v_send.at[r], recv_sem=v_recv.at[r],
                device_id=peer, device_id_type=MESH))
        for k in range(n):
            half = fulls[k].shape[0] // 2
            mine = outs[k].at[pl.ds(c * half, half), :]
            copies.append(pltpu.make_async_remote_copy(
                src_ref=mine, dst_ref=mine, send_sem=h_send.at[k], recv_sem=h_recv.at[k],
                device_id=(x, y, 1 - c), device_id_type=MESH))
        for cp in copies:
            cp.start()
        for r, (px, py, pc) in enumerate(peers):
            pltpu.make_async_remote_copy(
                src_ref=v_ref, dst_ref=buf.at[4 * px + 2 * py + pc], send_sem=v_send.at[r], recv_sem=v_recv.at[r],
                device_id=(px, py, pc), device_id_type=MESH).wait_recv()
        for k in range(n):
            half = fulls[k].shape[0] // 2
            theirs = outs[k].at[pl.ds((1 - c) * half, half), :]
            pltpu.make_async_remote_copy(
                src_ref=theirs, dst_ref=theirs, send_sem=h_send.at[k], recv_sem=h_recv.at[k],
                device_id=(x, y, 1 - c), device_id_type=MESH).wait_recv()
        for cp in copies:
            cp.wait_send()
        acc = buf[0]
        for d in range(1, n_dev):
            acc = acc + buf[d]
        out_ref[...] = acc
        out_ref[3:4, :] = jnp.broadcast_to(jnp.sum(acc[3:4, :], axis=1, keepdims=True), (1, cols))

    vm = pl.BlockSpec(memory_space=pltpu.VMEM)
    res = pl.pallas_call(
        body, name="final_exchange",
        in_specs=[vm] + _hbm_specs(n), out_specs=_hbm_specs(n) + [vm],
        out_shape=[jax.ShapeDtypeStruct(f.shape, F32) for f in fulls] + [jax.ShapeDtypeStruct((rows, cols), F32)],
        input_output_aliases={1 + k: k for k in range(n)},
        scratch_shapes=[pltpu.VMEM((n_dev, rows, cols), F32),
                        pltpu.SemaphoreType.DMA((n_dev - 1,)), pltpu.SemaphoreType.DMA((n_dev - 1,)),
                        pltpu.SemaphoreType.DMA((n,)), pltpu.SemaphoreType.DMA((n,))],
    )(v, *fulls)
    return res[:n], res[n]


def _adamw_math(w, g, m, v):
    m = ADAM_B1 * m + (1.0 - ADAM_B1) * g
    v = ADAM_B2 * v + (1.0 - ADAM_B2) * (g * g)
    m_hat = m / (1.0 - ADAM_B1 ** ADAM_STEP)
    v_hat = v / (1.0 - ADAM_B2 ** ADAM_STEP)
    delta = -ADAM_LR * (m_hat / (jnp.sqrt(v_hat) + ADAM_EPS) + ADAM_WD * w)
    return delta, m, v


def _adamw(w, g, m, v, name):
    rows, cols = w.shape
    tm = rows // 2 if (rows // 2) % 8 == 0 else rows
    return _rowwise(_adamw_math, [w, g, m, v], [], [(cols, F32)] * 3, [], tm=tm, name=name)


def _unshard_cols(gathered):
    n, r, c = gathered.shape
    return jnp.transpose(gathered, (1, 0, 2)).reshape(r, n * c)


def _shard_cols(full):
    r, nc = full.shape
    return jnp.transpose(full.reshape(r, N_CHIPS, nc // N_CHIPS), (1, 0, 2))


LATE = ["w_sb_up", "w_dil_up", "w_out", "w_ffn_in", "w_ffn_out"]


def _late_weights(slabs, d_model, d_ff):
    g = dict(zip(LATE, slabs))
    return (_unshard_cols(g["w_sb_up"]), _unshard_cols(g["w_dil_up"]), g["w_out"].reshape(d_model, d_model),
            _unshard_cols(g["w_ffn_in"]), g["w_ffn_out"].reshape(d_ff, d_model))


ROW_SHARDED = ("w_in", "w_out", "w_ffn_in", "w_ffn_out")


def _chip_major(grads):
    out = []
    for k, g in grads.items():
        if k in ROW_SHARDED:
            out.append(g.reshape(N_CHIPS, g.shape[0] // N_CHIPS, g.shape[1]))
        else:
            out.append(_shard_cols(g))
    return out


def _pair_sums(full, others, names):
    return [_pair_sum(g, o, "grad_pair_sum_" + k, k in SWAPPED) for g, o, k in zip(full, others, names)]


def _chip_sums(pair, landed, names):
    return {k: _chip_sum(p[0], l, "grad_chip_sum_" + k) for p, l, k in zip(pair, landed, names)}


def _fwd_bwd(x, loss_target, g_mix, g_ffn, g_fin, slab_in, late_shards):
    b_sz, s_len, d_model = x.shape
    t = b_sz * s_len
    d_ff = late_shards[-1].shape[0] * N_CHIPS
    x2d = x.reshape(t, d_model)
    tgt2d = loss_target.reshape(t, d_model)

    u, (slab_in, *late_slabs) = _rowwise(
        lambda xv, g: (_rms_stats(xv)[0] * g,), [_in_hbm(x2d)], [g_mix], [(d_model, BF16)], [], tm=512, name="norm_mix",
        carry=_gather_carry([slab_in], ["w_in"]) + _cast_carry(late_shards, LATE), out_type=_hbm_array)
    u, wt_in = _in_hbm(u), _in_hbm(slab_in.reshape(-1, d_model))
    qkv, (slab_ffn_out,) = _mm(u, wt_in, tb=True, b_cols=(0, QKV_WIDTH), tm=2048, tn=768, tk=d_model, name="proj_qkv",
                               carry=_gather_carry(late_slabs[4:], LATE[4:]))
    gates = _mm(u, wt_in, tb=True, b_cols=(QKV_WIDTH, 2 * d_model), out_dtype=BF16, tm=t, tn=256, tk=d_model,
                name="proj_gates")
    qkv3 = qkv.reshape(b_sz, s_len, QKV_WIDTH)
    o_sb, (slab_ffn_in,) = _sb_fwd(qkv3, b_sz, s_len, _gather_carry(late_slabs[3:4], LATE[3:4]))
    o_dl, lse, small_slabs = _dil_fwd(qkv3, b_sz, s_len, _gather_carry(late_slabs[:3], LATE[:3]))
    wf_sb_up, wf_dil_up, wf_out, wf_ffn_in, wf_ffn_out = _late_weights(
        list(small_slabs) + [slab_ffn_in, slab_ffn_out], d_model, d_ff)
    o_sb2, o_dl2 = o_sb.reshape(t, SB_WIDTH), o_dl.reshape(t, DIL_OUT_WIDTH)
    y_sb = _mm(o_sb2, wf_sb_up, out_dtype=BF16, tm=1024, tn=1024, tk=SB_WIDTH, name="sb_up", out_type=_hbm_array)
    y_dl = _mm(o_dl2, wf_dil_up, out_dtype=BF16, tm=1024, tn=1024, tk=DIL_OUT_WIDTH, name="dil_up", out_type=_hbm_array)

    def merge_fn(gt, ys, yd):
        return (_sigmoid(gt[:, :d_model]) * ys + _sigmoid(gt[:, d_model:]) * yd,)

    (merged,) = _rowwise(merge_fn, [gates, y_sb, y_dl], [], [(d_model, BF16)], [], tm=512, name="merge")
    x1 = _mm(merged, wf_out, add=x2d, tm=512, tn=1024, tk=d_model, name="mix_out")
    (u2,) = _rowwise(lambda xv, g: (_rms_stats(xv)[0] * g,), [_in_hbm(x1)], [g_ffn], [(d_model, BF16)], [], tm=512, name="norm_ffn",
                     out_type=_hbm_array)
    u2 = _in_hbm(u2)
    half_ff = d_ff // 2

    def act_fn(hv):
        gate = hv[:, :half_ff]
        return hv, gate * _sigmoid(gate) * hv[:, half_ff:]

    h, act = _mm(u2, wf_ffn_in, tm=512, tn=d_ff, tk=d_model, name="ffn_in",
                 epilogue=(act_fn, [], [], [(d_ff, BF16), (half_ff, BF16)], []))
    def head_fn(xv, tg, g):
        xhat, r = _rms_stats(xv)
        err = xhat * g - tg
        dy = err * (1.0 / d_model)
        dx, dg_rows = _rms_bwd(dy, xhat, r, g)
        loss_lanes = (0.5 / d_model) * jnp.sum(err * err, axis=0, keepdims=True)
        return dx, dx, jnp.sum(dg_rows, axis=0, keepdims=True), loss_lanes

    dx2, dx2_b, dg_fin, loss_lanes = _mm(
        act, wf_ffn_out, add=x1, tm=512, tn=1024, tk=d_ff, name="ffn_out",
        epilogue=(head_fn, [tgt2d], [g_fin], [(d_model, F32), (d_model, BF16)], [(1, d_model), (1, d_model)]))

    def dact_fn(da, hv):
        gate, up = hv[:, :half_ff], hv[:, half_ff:]
        sg = _sigmoid(gate)
        dgate = da * up * (sg * (1.0 + gate * (1.0 - sg)))
        return (jnp.concatenate([dgate, da * (gate * sg)], axis=1),)

    dx2_b = _in_hbm(dx2_b)
    (dh,) = _mm(dx2_b, wf_ffn_out, tb=True, tm=512, tn=half_ff, tk=d_model, name="ffn_out_dx",
                epilogue=(dact_fn, [h], [], [(d_ff, BF16)], []))
    gw_ffn_out = _mm(act, dx2_b, ta=True, tm=256, tn=d_model, tk=t, name="ffn_out_dw")
    def norm_bwd_fn(du_, dres, xv, g):
        xhat, r = _rms_stats(xv)
        dx, dg_rows = _rms_bwd(du_, xhat, r, g)
        return dres + dx, jnp.sum(dg_rows, axis=0, keepdims=True)

    def norm_bwd_twice(*args):
        dx, dg = norm_bwd_fn(*args)
        return dx, dx, dg

    dx1, dx1_b, dg_ffn = _mm(dh, wf_ffn_in, tb=True, tm=512, tn=1024, tk=2 * d_ff, name="ffn_in_dx",
                             epilogue=(norm_bwd_twice, [dx2, x1], [g_ffn], [(d_model, F32), (d_model, BF16)], [(1, d_model)]))
    gwt_ffn_in = _mm(dh, u2, ta=True, tm=512, tn=d_model, tk=t, name="ffn_in_dw")

    dx1_b = _in_hbm(dx1_b)
    dmerged = _mm(dx1_b, wf_out, tb=True, out_dtype=BF16, tm=512, tn=1024, tk=d_model, name="mix_out_dx",
                  out_type=_hbm_array)
    gw_out = _mm(merged, dx1_b, ta=True, tm=256, tn=d_model, tk=t, name="mix_out_dw")

    def merge_bwd_fn(gt, ys, yd, dm):
        s_sb, s_dl = _sigmoid(gt[:, :d_model]), _sigmoid(gt[:, d_model:])
        dgates = jnp.concatenate([dm * ys * s_sb * (1.0 - s_sb), dm * yd * s_dl * (1.0 - s_dl)], axis=1)
        return dgates, dm * s_sb, dm * s_dl

    full_big = _chip_major({"w_out": gw_out, "w_ffn_in": gwt_ffn_in})
    dgates, dy_sb, dy_dl, others_big = _rowwise(
        merge_bwd_fn, [gates, y_sb, y_dl, dmerged], [], [(2 * d_model, BF16), (d_model, BF16), (d_model, BF16)], [],
        tm=256, name="merge_bwd", carry=_pair_carry(full_big))
    pair_big = _pair_sums(full_big, others_big, LATE[2:4])
    do_sb = _mm(dy_sb, wf_sb_up, tb=True, out_dtype=BF16, tm=1024, tn=SB_WIDTH, tk=d_model, name="sb_up_dx")
    gw_sb_up = _mm(o_sb2, dy_sb, ta=True, tm=SB_WIDTH, tn=1024, tk=512, name="sb_up_dw")
    do_dl = _mm(dy_dl, wf_dil_up, tb=True, tm=1024, tn=DIL_OUT_WIDTH, tk=d_model, name="dil_up_dx")
    gw_dil_up = _mm(o_dl2, dy_dl, ta=True, tm=DIL_OUT_WIDTH, tn=1024, tk=512, name="dil_up_dw")
    rest = [LATE[0], LATE[1], LATE[4]]
    full_rest = _chip_major({"w_sb_up": gw_sb_up, "w_dil_up": gw_dil_up, "w_ffn_out": gw_ffn_out})
    (dq_sb, dk_sb, dv_sb), brought = _sb_bwd(
        qkv3, o_sb, do_sb.reshape(b_sz, s_len, SB_WIDTH), b_sz, s_len,
        _chip_carry([p[1] for p in pair_big], LATE[2:4]) + _pair_carry(full_rest))
    pair_rest = _pair_sums(full_rest, brought[2:], rest)
    (dq_dl, dk_dl, dv_dl), landed_b = _dil_bwd(
        qkv3, o_dl, lse, do_dl.reshape(b_sz, s_len, DIL_OUT_WIDTH), b_sz, s_len,
        _chip_carry([p[1] for p in pair_rest], rest))
    pair = pair_rest[:2] + pair_big + pair_rest[2:]
    landed = [landed_b[0], landed_b[1], brought[0], brought[1], landed_b[2]]
    dproj = [a.reshape(t, -1) for a in (dq_sb, dk_sb, dv_sb, dq_dl, dk_dl, dv_dl)] + [dgates]
    gwt_in, gwt_in_b = _mm(dproj, u, ta=True, tm=256, tn=d_model, tk=t, name="proj_dw",
                           epilogue=(lambda tile: (tile, tile), [], [], [(d_model, F32), (d_model, BF16)], []))
    full_in = _chip_major({"w_in": gwt_in})
    pair_in = _pair_sums(full_in, _pair_exchange(_chip_major({"w_in": gwt_in_b}), "w_in"), ["w_in"])
    late_halves = _chip_sums(pair, landed, LATE)
    (dx, dg_mix), brought_in = _mm(
        dproj, wt_in, tm=512, tn=1024, tk=wt_in.shape[0], name="proj_dx",
        carry=_halves_carry([late_halves[k] for k in LATE]) + _chip_carry([p[1] for p in pair_in], ["w_in"]),
        epilogue=(norm_bwd_fn, [dx1, x2d], [g_mix], [(d_model, F32)], [(1, d_model)]))

    grads = dict(zip(LATE, brought_in[:len(LATE)]))
    grads.update(_chip_sums(pair_in, brought_in[len(LATE):], ["w_in"]))
    return dx, grads, dg_mix, dg_ffn, dg_fin, loss_lanes


def kernel(x, norm_mix_g, w_in, w_sb_up, w_dil_up, w_out, norm_ffn_g, w_ffn_in, w_ffn_out, norm_final_g, loss_target, m_norm_mix_g, m_w_in, m_w_sb_up, m_w_dil_up, m_w_out, m_norm_ffn_g, m_w_ffn_in, m_w_ffn_out, m_norm_final_g, v_norm_mix_g, v_w_in, v_w_sb_up, v_w_dil_up, v_w_out, v_norm_ffn_g, v_w_ffn_in, v_w_ffn_out, v_norm_final_g):
    b_sz, s_len, d_model = x.shape
    d_ff = w_ffn_out.shape[1] * N_CHIPS
    g_mix, g_ffn, g_fin = norm_mix_g, norm_ffn_g, norm_final_g.reshape(1, d_model)

    names = ["w_in", "w_sb_up", "w_dil_up", "w_out", "w_ffn_in", "w_ffn_out"]
    shards = {"w_in": jnp.swapaxes(w_in[0], 0, 1), "w_sb_up": w_sb_up[0], "w_dil_up": w_dil_up[0], "w_out": w_out[0],
              "w_ffn_in": w_ffn_in[0], "w_ffn_out": w_ffn_out[0]}
    slab_in = _cast_to_slab(shards["w_in"], "cast_w_in")

    dx, grads, dg_mix, dg_ffn, dg_fin, loss_lanes = _fwd_bwd(
        x, loss_target, g_mix, g_ffn, g_fin, slab_in, [shards[k] for k in LATE])

    small = jnp.concatenate([dg_mix, dg_ffn, dg_fin, loss_lanes, jnp.zeros((4, d_model), F32)], axis=0)
    (grads["w_in"],), small = _final_exchange([grads["w_in"]], small)
    grads["w_ffn_in"] = jnp.swapaxes(grads["w_ffn_in"], 0, 1)
    loss = small[3, 0]
    gains = jnp.concatenate([g_mix, g_ffn, g_fin, jnp.zeros((5, d_model), F32)], axis=0)
    gains_m = jnp.concatenate([m_norm_mix_g, m_norm_ffn_g, m_norm_final_g.reshape(1, d_model), jnp.zeros((5, d_model), F32)], axis=0)
    gains_v = jnp.concatenate([v_norm_mix_g, v_norm_ffn_g, v_norm_final_g.reshape(1, d_model), jnp.ones((5, d_model), F32)], axis=0)
    gd, gm, gv = _rowwise(_adamw_math, [gains, small, gains_m, gains_v], [], [(d_model, F32)] * 3, [], tm=8, name="adamw_gains")

    moments = {"w_in": (jnp.swapaxes(m_w_in[0], 0, 1), jnp.swapaxes(v_w_in[0], 0, 1)),
               "w_sb_up": (m_w_sb_up[0], v_w_sb_up[0]), "w_dil_up": (m_w_dil_up[0], v_w_dil_up[0]),
               "w_out": (m_w_out[0], v_w_out[0]), "w_ffn_in": (m_w_ffn_in[0], v_w_ffn_in[0]),
               "w_ffn_out": (m_w_ffn_out[0], v_w_ffn_out[0])}
    upd = {k: _adamw(shards[k], grads[k], moments[k][0], moments[k][1], "adamw_" + k) for k in names}

    def as_output(k, a):
        return (jnp.swapaxes(a, 0, 1) if k == "w_in" else a)[None]

    def w_out_of(i):
        return [as_output(k, upd[k][i]) for k in names]

    def ordered(mix, ws, ffn_g, fin):
        return [mix, ws[0], ws[1], ws[2], ws[3], ffn_g, ws[4], ws[5], fin]

    grad_ws = [as_output(k, grads[k]) for k in names]
    outs = [loss, dx.reshape(b_sz, s_len, d_model)]
    outs += ordered(small[0:1], grad_ws, small[1:2], small[2])
    outs += ordered(gd[0:1], w_out_of(0), gd[1:2], gd[2])
    outs += ordered(gm[0:1], w_out_of(1), gm[1:2], gm[2])
    outs += ordered(gv[0:1], w_out_of(2), gv[1:2], gv[2])
    return tuple(outs)
```

```python
import functools
import math

import jax
import jax.numpy as jnp
from jax import lax
from jax.experimental import pallas as pl
from jax.experimental.pallas import tpu as pltpu

F32 = jnp.float32
BF16 = jnp.bfloat16
MESH = pl.DeviceIdType.MESH

HEAD_DIM = 64
SB_HEADS = 8
DIL_PAIRS = ((128, 1), (512, 4), (2048, 16))
DIL_HEADS_PER_GROUP = 4
DIL_HEADS = DIL_HEADS_PER_GROUP * len(DIL_PAIRS)
SB_WIDTH = SB_HEADS * HEAD_DIM
DIL_WIDTH = DIL_HEADS * HEAD_DIM
DIL_OUT_WIDTH = DIL_HEADS_PER_GROUP * HEAD_DIM
QKV_WIDTH = 3 * SB_WIDTH + 3 * DIL_WIDTH
RMS_EPS = 1e-6
ALIBI_MAX_BIAS = 8.0
ADAM_LR = 0.001
ADAM_B1 = 0.9
ADAM_B2 = 0.999
ADAM_EPS = 1e-08
ADAM_WD = 0.01
ADAM_STEP = 10

LANES = 128
BLK = 128
NEG = -1e30
EXP_UNDERFLOW = -104.0
SB_FWD_CHAINS = 4
SB_BWD_CHAINS = 4
DIL_CHAINS = 8
N_CHIPS = 4
VMEM_CAP = 56 * 1024 * 1024


def _vmem_limit(tile_bytes):
    return int(min(VMEM_CAP, max(32 * 1024 * 1024, 3 * tile_bytes + 8 * 1024 * 1024)))


def _hbm_array(shape, dtype):
    return pltpu.HBM(shape, dtype)


def _nbytes(shape, dtype):
    return math.prod(shape) * jnp.dtype(dtype).itemsize


def _in_hbm(x):
    return pltpu.with_memory_space_constraint(x, pltpu.HBM)


def _dot(a, b):
    return jnp.dot(a, b, preferred_element_type=F32)


def _dot_nt(a, b):
    return lax.dot_general(a, b, (((1,), (1,)), ((), ())), preferred_element_type=F32)


def _dot_tn(a, b):
    return lax.dot_general(a, b, (((0,), (0,)), ((), ())), preferred_element_type=F32)


def _split2(x):
    hi = x.astype(BF16)
    lo = (x - hi.astype(F32)).astype(BF16)
    return hi, lo


def _sigmoid(x):
    return pl.reciprocal(1.0 + jnp.exp(-x), approx=True)


class _Carry:
    def __init__(self, arrays=(), out_shapes=(), aliased=False, sems=(), start=None, finish=None):
        self.arrays, self.out_shapes = list(arrays), list(out_shapes)
        self.n_aliased = len(self.arrays) if aliased is True else int(aliased)
        self.sems, self.start, self.finish = list(sems), start, finish

    def __bool__(self):
        return bool(self.arrays)

    def __add__(self, other):
        assert not other.n_aliased and self.n_aliased in (0, len(self.out_shapes))
        n_a, n_o, n_s = len(self.arrays), len(self.out_shapes), len(self.sems)
        return _Carry(
            self.arrays + other.arrays, self.out_shapes + other.out_shapes, self.n_aliased, self.sems + other.sems,
            lambda i, o, s: (self.start(i[:n_a], o[:n_o], s[:n_s]), other.start(i[n_a:], o[n_o:], s[n_s:])),
            lambda i, o, s: (self.finish(i[:n_a], o[:n_o], s[:n_s]), other.finish(i[n_a:], o[n_o:], s[n_s:])))

    def call_args(self, n_in, n_out):
        aliases = {n_in + k: n_out + k for k in range(self.n_aliased)}
        return _hbm_specs(len(self.arrays)), _hbm_specs(len(self.out_shapes)), self.out_shapes, aliases, self.sems

    def run(self, refs, n_in, n_out, step, n_steps, compute):
        if not self:
            compute()
            return
        n_c, n_o, n_s = len(self.arrays), len(self.out_shapes), len(self.sems)
        ins = refs[n_in:n_in + n_c]
        outs = refs[n_in + n_c + n_out:n_in + n_c + n_out + n_o]
        sems = refs[len(refs) - n_s:]

        @pl.when(step == 0)
        def _():
            self.start(ins, outs, sems)

        compute()

        @pl.when(step == n_steps - 1)
        def _():
            self.finish(ins, outs, sems)


def _mm(a, b, *, ta=False, tb=False, add=None, out_dtype=F32, tm, tn, tk, name, carry=None, epilogue=None,
        b_cols=None, out_type=jax.ShapeDtypeStruct):
    carry = carry or _Carry()
    n_car = len(carry.arrays)
    pieces = list(a) if isinstance(a, (list, tuple)) else [a]
    n_a = len(pieces)
    widths = [p.shape[1] for p in pieces]
    starts = [sum(widths[:p]) for p in range(n_a)]
    if ta:
        kdim, m = pieces[0].shape[0], sum(widths)
    else:
        m, kdim = pieces[0].shape[0], sum(widths)
    if tb:
        n, k2 = b.shape
    else:
        k2, n = b.shape
    col0 = 0
    if b_cols is not None:
        assert b_cols[0] % tn == 0, name
        col0, n = b_cols[0] // tn, b_cols[1]
    assert kdim == k2 and m % tm == 0 and n % tn == 0 and kdim % tk == 0, (name, a.shape, b.shape)
    nk = kdim // tk
    assert n_a == 1 or (nk == 1 and not tb and (not ta or all(w % tm == 0 for w in widths))), name
    grid = (m // tm, n // tn, nk)
    a_mode = dict(pipeline_mode=pl.Buffered(1)) if grid[0] == 1 and nk == 1 else {}
    b_mode = dict(pipeline_mode=pl.Buffered(1)) if grid[1] == 1 and nk == 1 else {}
    if n_a == 1:
        a_specs = [pl.BlockSpec((tk, tm), lambda i, j, k: (k, i), **a_mode) if ta
                   else pl.BlockSpec((tm, tk), lambda i, j, k: (i, k), **a_mode)]
    elif ta:
        a_specs = [pl.BlockSpec((tk, tm), lambda i, j, k, s=s // tm, w=w // tm: (0, jnp.clip(i - s, 0, w - 1)))
                   for s, w in zip(starts, widths)]
    else:
        a_specs = [pl.BlockSpec((tm, w), lambda i, j, k: (i, 0)) for w in widths]
    b_spec = (pl.BlockSpec((tn, tk), lambda i, j, k: (j + col0, k), **b_mode) if tb
              else pl.BlockSpec((tk, tn), lambda i, j, k: (k, j + col0), **b_mode))
    o_spec = pl.BlockSpec((tm, tn), lambda i, j, k: (i, j))
    dims = ((((0,) if ta else (1,)), ((1,) if tb else (0,))), ((), ()))
    has_add = add is not None
    if epilogue is None:
        ep_fn, ep_rows, ep_params, ep_outs, ep_accs = None, [], [], [], []
        out_sds, out_specs = [out_type((m, n), out_dtype)], [o_spec]
    else:
        ep_fn, ep_rows, ep_params, ep_outs, ep_accs = epilogue
        assert grid[1] == 1 or not ep_accs, name
        out_sds = [out_type((m, w * grid[1]), d) for w, d in ep_outs]
        out_sds += [jax.ShapeDtypeStruct(sh, F32) for sh in ep_accs]
        out_specs = [pl.BlockSpec((tm, w), lambda i, j, k: (i, j)) for w, _ in ep_outs]
        out_specs += [pl.BlockSpec(sh, lambda i, j, k: (0, 0)) for sh in ep_accs]
    n_main = len(out_sds)
    use_scratch = nk > 1 and (ep_fn is not None or jnp.dtype(out_dtype) != jnp.dtype(F32))
    n_in = n_a + 1 + has_add + len(ep_rows) + len(ep_params)

    def finish(total, refs, pid):
        outs = refs[n_in + n_car:n_in + n_car + n_main]
        if ep_fn is None:
            outs[0][...] = total.astype(out_dtype)
            return
        first = n_a + 1 + has_add
        rows = [r[...].astype(F32) for r in refs[first:first + len(ep_rows)]]
        params = [p[...] for p in refs[first + len(ep_rows):n_in]]
        res = ep_fn(total, *rows, *params)
        for o_ref, v in zip(outs[:len(ep_outs)], res):
            o_ref[...] = v.astype(o_ref.dtype)
        acc_refs = outs[len(ep_outs):]
        if acc_refs:
            @pl.when(pid[0] == 0)
            def _():
                for r in acc_refs:
                    r[...] = jnp.zeros(r.shape, F32)

            for r, v in zip(acc_refs, res[len(ep_outs):]):
                r[...] += v

    def compute(refs, pid):
        a_ref, b_ref = refs[0], refs[n_a]
        add_ref = refs[n_a + 1] if has_add else None

        def dot(x, y):
            return lax.dot_general(x.astype(BF16), y.astype(BF16), dims, preferred_element_type=F32)

        if n_a > 1 and ta:
            for p_ref, s, w in zip(refs[:n_a], starts, widths):
                @pl.when((pid[0] >= s // tm) & (pid[0] < (s + w) // tm))
                def _(p_ref=p_ref):
                    prod = dot(p_ref[...], b_ref[...])
                    finish(prod + add_ref[...] if has_add else prod, refs, pid)
            return
        if n_a > 1:
            prod = dot(a_ref[...], b_ref[:widths[0], :])
            for p_ref, s, w in zip(refs[1:n_a], starts[1:], widths[1:]):
                prod += dot(p_ref[...], b_ref[s:s + w, :])
        else:
            prod = dot(a_ref[...], b_ref[...])
        if nk == 1:
            finish(prod + add_ref[...] if has_add else prod, refs, pid)
            return
        acc_ref = refs[n_in + n_car + n_main + len(carry.out_shapes)] if use_scratch else refs[n_in + n_car]
        k = pid[2]

        @pl.when(k == 0)
        def _():
            acc_ref[...] = prod + add_ref[...] if has_add else prod

        @pl.when(k > 0)
        def _():
            acc_ref[...] += prod

        if use_scratch:
            @pl.when(k == nk - 1)
            def _():
                finish(acc_ref[...], refs, pid)

    def body(*refs):
        pid = (pl.program_id(0), pl.program_id(1), pl.program_id(2))
        step = (pid[0] * grid[1] + pid[1]) * nk + pid[2]
        carry.run(refs, n_in, n_main, step, grid[0] * grid[1] * nk, lambda: compute(refs, pid))

    tile_bytes = ((n_a if ta else 1) * _nbytes((tm, tk), pieces[0].dtype)
                  + _nbytes((tk, tn), b.dtype) + 2 * _nbytes((tm, tn), F32)
                  + (_nbytes((tm, tn), F32) if has_add else 0)
                  + sum(_nbytes((tm, r.shape[1]), r.dtype) for r in ep_rows) + sum(_nbytes((tm, w), d) for w, d in ep_outs))
    in_specs = a_specs + [b_spec] + ([o_spec] if has_add else [])
    in_specs += [pl.BlockSpec((tm, r.shape[1] // grid[1]), lambda i, j, k: (i, j)) for r in ep_rows]
    in_specs += [pl.BlockSpec(p.shape, lambda i, j, k: (0, 0)) for p in ep_params]
    args = tuple(pieces) + (b,) + ((add,) if has_add else ()) + tuple(ep_rows) + tuple(ep_params)
    scratch = [pltpu.VMEM((tm, tn), F32)] if use_scratch else []
    serial = bool(carry) or bool(ep_accs)
    c_in, c_out, c_shapes, c_alias, c_sems = carry.call_args(n_in, n_main)
    res = pl.pallas_call(
        body, name=name, grid=grid,
        in_specs=in_specs + c_in, out_specs=out_specs + c_out, out_shape=out_sds + c_shapes,
        input_output_aliases=c_alias, scratch_shapes=scratch + c_sems,
        compiler_params=pltpu.CompilerParams(
            dimension_semantics=("arbitrary",) * 3 if serial else ("parallel", "parallel", "arbitrary"),
            vmem_limit_bytes=_vmem_limit(tile_bytes)),
    )(*args, *carry.arrays)
    main = res[0] if ep_fn is None else list(res[:n_main])
    return (main, res[n_main:]) if carry else main


def _rowwise(fn, rows, params, outs, accs, *, tm, name, carry=None, out_type=jax.ShapeDtypeStruct):
    carry = carry or _Carry()
    t = rows[0].shape[0]
    assert t % tm == 0, (name, t, tm)
    n_r, n_p, n_o, n_c = len(rows), len(params), len(outs), len(carry.arrays)

    def compute(refs, first):
        vals = [r[...].astype(F32) for r in refs[:n_r]] + [p[...] for p in refs[n_r:n_r + n_p]]
        res = fn(*vals)
        o_refs = refs[n_r + n_p + n_c:n_r + n_p + n_c + n_o]
        a_refs = refs[n_r + n_p + n_c + n_o:n_r + n_p + n_c + n_o + len(accs)]
        for o_ref, v in zip(o_refs, res[:n_o]):
            o_ref[...] = v.astype(o_ref.dtype)
        if accs:
            @pl.when(first)
            def _():
                for a_ref in a_refs:
                    a_ref[...] = jnp.zeros(a_ref.shape, F32)

            for a_ref, v in zip(a_refs, res[n_o:]):
                a_ref[...] += v

    def body(*refs):
        step = pl.program_id(0)
        carry.run(refs, n_r + n_p, n_o + len(accs), step, t // tm, lambda: compute(refs, step == 0))

    in_specs = [pl.BlockSpec((tm, r.shape[1]), lambda i: (i, 0)) for r in rows]
    in_specs += [pl.BlockSpec(p.shape, lambda i: (0, 0)) for p in params]
    out_specs = [pl.BlockSpec((tm, w), lambda i: (i, 0)) for w, _ in outs]
    out_specs += [pl.BlockSpec(s, lambda i: (0, 0)) for s in accs]
    out_shape = [out_type((t, w), d) for w, d in outs]
    out_shape += [jax.ShapeDtypeStruct(s, F32) for s in accs]
    tile_bytes = sum(_nbytes((tm, r.shape[1]), r.dtype) for r in rows) + sum(_nbytes((tm, w), F32) for w, _ in outs)
    c_in, c_out, c_shapes, c_alias, c_sems = carry.call_args(n_r + n_p, n_o + len(accs))
    res = pl.pallas_call(
        body, name=name, grid=(t // tm,), in_specs=in_specs + c_in, out_specs=out_specs + c_out,
        out_shape=out_shape + c_shapes, input_output_aliases=c_alias, scratch_shapes=c_sems,
        compiler_params=pltpu.CompilerParams(
            dimension_semantics=("arbitrary",) if accs or carry else ("parallel",),
            vmem_limit_bytes=_vmem_limit(2 * tile_bytes)),
    )(*rows, *params, *carry.arrays)
    own = n_o + len(accs)
    return (list(res[:own]) + [res[own:]]) if carry else res


def _rms_stats(x):
    r = lax.rsqrt(jnp.mean(x * x, axis=-1, keepdims=True) + RMS_EPS)
    return x * r, r


def _rms_bwd(dy, xhat, r, g):
    dxhat = dy * g
    dx = r * (dxhat - xhat * jnp.mean(dxhat * xhat, axis=-1, keepdims=True))
    return dx, dy * xhat


def _sb_consts():
    lane = lax.broadcasted_iota(jnp.int32, (BLK, LANES), 1)
    head0 = lane < HEAD_DIM
    row = lax.broadcasted_iota(jnp.int32, (2 * BLK, BLK), 0) % BLK
    col = lax.broadcasted_iota(jnp.int32, (2 * BLK, BLK), 1)
    causal = col < row
    jj = lax.broadcasted_iota(jnp.int32, (BLK, BLK), 0)
    ss = lax.broadcasted_iota(jnp.int32, (BLK, BLK), 1)
    suffix = jnp.where(jj > ss, 1.0, 0.0).astype(BF16)
    return head0, causal, suffix


def _stack_heads(x, head0):
    zero = jnp.zeros_like(x)
    return jnp.concatenate([jnp.where(head0, x, zero), jnp.where(head0, zero, x)], axis=0)


def _sb_logits(z, causal, masked):
    sp = jnp.log(1.0 + jnp.exp(-jnp.abs(z)))
    log_keep = -(jnp.maximum(z, 0.0) + sp)
    log_beta = jnp.minimum(z, 0.0) - sp
    if masked:
        log_keep = jnp.where(causal, log_keep, 0.0)
    return log_keep, log_beta


def _suffix_sums(x, suffix):
    hi, lo = _split2(x)
    after = _dot(hi, suffix) + _dot(lo, suffix)
    total = jnp.broadcast_to(after[:, 0:1] + x[:, 0:1], x.shape)
    return after, total


def _sb_walk_back(i, state, per_chain, tile):
    def alive(st):
        worst = functools.reduce(jnp.maximum, [st[p][:, 0:1] for p in range(0, len(st), per_chain)])
        return jnp.max(worst) > EXP_UNDERFLOW

    def cond(c):
        return jnp.logical_and(c[0] < i, alive(c[1]))

    def body(c):
        return c[0] + 1, tile(i - 1 - c[0], c[1], False)

    return lax.while_loop(cond, body, (jnp.int32(0), state))[1]


def _lane_blocks(x, n):
    return [x[:, p * LANES:(p + 1) * LANES] for p in range(n)]


def _sb_fwd(qkv, b_sz, s_len, carry):
    nq = s_len // BLK
    n_pairs = SB_WIDTH // LANES
    ch = SB_FWD_CHAINS
    n_steps = n_pairs // ch
    scale = 1.0 / math.sqrt(HEAD_DIM)

    def compute(q_ref, k_ref, v_ref, o_ref):
        head0, causal, suffix = _sb_consts()

        def q_block(i, _):
            qs = pl.multiple_of(i * BLK, BLK)
            q_all = (q_ref[pl.ds(qs, BLK), :] * scale).astype(BF16)
            q01 = [_stack_heads(q, head0) for q in _lane_blocks(q_all, ch)]

            def tile(j, state, masked):
                ks = pl.multiple_of(j * BLK, BLK)
                ks_ = _lane_blocks(k_ref[pl.ds(ks, BLK), :].astype(BF16), ch)
                vs_ = _lane_blocks(v_ref[pl.ds(ks, BLK), :].astype(BF16), ch)
                zs = [_dot_nt(q01[p], ks_[p]) for p in range(ch)]
                logits = [_sb_logits(z, causal, masked) for z in zs]
                sums = [_suffix_sums(lg[0], suffix) for lg in logits]
                out = []
                for p in range(ch):
                    carry, acc = state[2 * p], state[2 * p + 1]
                    after, total = sums[p]
                    a = jnp.exp(logits[p][1] + carry + after)
                    if masked:
                        a = jnp.where(causal, a, 0.0)
                    a_hi, a_lo = _split2(a)
                    a_cat = jnp.concatenate([a_hi[:BLK], a_hi[BLK:], a_lo[:BLK], a_lo[BLK:]], axis=1)
                    v01 = _stack_heads(vs_[p], head0)
                    out += [carry + total, acc + _dot(a_cat, jnp.concatenate([v01, v01], axis=0))]
                return tuple(out)

            state = (jnp.zeros((2 * BLK, BLK), F32), jnp.zeros((BLK, LANES), F32)) * ch
            state = tile(i, state, True)
            state = _sb_walk_back(i, state, 2, tile)
            o_ref[pl.ds(qs, BLK), :] = jnp.concatenate([state[2 * p + 1] for p in range(ch)], axis=1)
            return 0

        lax.fori_loop(0, nq, q_block, 0)

    def body(*refs):
        step = pl.program_id(0) * n_steps + pl.program_id(1)
        o_ref = refs[3 + len(carry.arrays)]
        carry.run(refs, 3, 1, step, b_sz * n_steps, lambda: compute(refs[0], refs[1], refs[2], o_ref))

    blk = lambda off: pl.BlockSpec((None, s_len, ch * LANES), lambda b, p: (b, 0, off + p))
    c_in, c_out, c_shapes, c_alias, c_sems = carry.call_args(3, 1)
    res = pl.pallas_call(
        body, name="sb_fwd", grid=(b_sz, n_steps),
        in_specs=[blk(0), blk(n_steps), blk(2 * n_steps)] + c_in, out_specs=[blk(0)] + c_out,
        out_shape=[jax.ShapeDtypeStruct((b_sz, s_len, SB_WIDTH), F32)] + c_shapes,
        input_output_aliases=c_alias, scratch_shapes=c_sems,
        compiler_params=pltpu.CompilerParams(dimension_semantics=("arbitrary", "arbitrary"),
                                             vmem_limit_bytes=VMEM_CAP),
    )(qkv, qkv, qkv, *carry.arrays)
    return res[0], res[1:]


def _sb_bwd(qkv, o_sb, do_sb, b_sz, s_len, carry):
    nq = s_len // BLK
    n_pairs = SB_WIDTH // LANES
    ch = SB_BWD_CHAINS
    n_steps = n_pairs // ch
    scale = 1.0 / math.sqrt(HEAD_DIM)

    def compute(q_ref, k_ref, v_ref, o_ref, do_ref, dq_ref, dk_ref, dv_ref, dk_acc, dv_acc):
        head0, causal, suffix = _sb_consts()
        lrow = lax.broadcasted_iota(jnp.int32, (LANES, LANES), 0)
        ones_h0 = jnp.where(lrow < HEAD_DIM, 1.0, 0.0).astype(BF16)
        ones_h1 = jnp.where(lrow >= HEAD_DIM, 1.0, 0.0).astype(BF16)
        dk_acc[...] = jnp.zeros(dk_acc.shape, F32)
        dv_acc[...] = jnp.zeros(dv_acc.shape, F32)

        def q_block(i, _):
            qs = pl.multiple_of(i * BLK, BLK)
            q_all = (q_ref[pl.ds(qs, BLK), :] * scale).astype(BF16)
            do_all = do_ref[pl.ds(qs, BLK), :].astype(BF16)
            dd_all = do_all.astype(F32) * o_ref[pl.ds(qs, BLK), :]
            q01 = [_stack_heads(q, head0) for q in _lane_blocks(q_all, ch)]
            do01 = [_stack_heads(d, head0) for d in _lane_blocks(do_all, ch)]
            tot = []
            for dd in _lane_blocks(dd_all, ch):
                dd_hi, dd_lo = _split2(dd)
                tot.append(jnp.concatenate([_dot(dd_hi, ones_h0) + _dot(dd_lo, ones_h0),
                                            _dot(dd_hi, ones_h1) + _dot(dd_lo, ones_h1)], axis=0))

            def tile(j, state, masked):
                ks = pl.multiple_of(j * BLK, BLK)
                ks_ = _lane_blocks(k_ref[pl.ds(ks, BLK), :].astype(BF16), ch)
                vs_ = _lane_blocks(v_ref[pl.ds(ks, BLK), :].astype(BF16), ch)
                zs = [_dot_nt(q01[p], ks_[p]) for p in range(ch)]
                das = [_dot_nt(do01[p], vs_[p]) for p in range(ch)]
                logits = [_sb_logits(z, causal, masked) for z in zs]
                sums = [_suffix_sums(lg[0], suffix) for lg in logits]
                a_s, e_s = [], []
                for p in range(ch):
                    a = jnp.exp(logits[p][1] + state[3 * p] + sums[p][0])
                    if masked:
                        a = jnp.where(causal, a, 0.0)
                    a_s.append(a)
                    e_s.append(a * das[p])
                e_sums = [_suffix_sums(e, suffix) for e in e_s]
                out, dks, dvs = [], [], []
                for p in range(ch):
                    carry, rcarry, dq = state[3 * p:3 * p + 3]
                    e = e_s[p]
                    before = tot[p] - (rcarry + e_sums[p][0] + e)
                    beta = jnp.exp(logits[p][1])
                    dz = e * (1.0 - beta) - beta * before
                    if masked:
                        dz = jnp.where(causal, dz, 0.0)
                    dz_b = dz.astype(BF16)
                    dks.append(_dot_tn(dz_b, q01[p]))
                    dvs.append(_dot_tn(a_s[p].astype(BF16), do01[p]))
                    out += [carry + sums[p][1], rcarry + e_sums[p][1], dq + _dot(dz_b, ks_[p])]
                dk_acc[pl.ds(ks, BLK), :] += jnp.concatenate(dks, axis=1)
                dv_acc[pl.ds(ks, BLK), :] += jnp.concatenate(dvs, axis=1)
                return tuple(out)

            state = (jnp.zeros((2 * BLK, BLK), F32),) * (3 * ch)
            state = tile(i, state, True)
            state = _sb_walk_back(i, state, 3, tile)
            dq = [jnp.where(head0, state[3 * p + 2][:BLK], state[3 * p + 2][BLK:]) for p in range(ch)]
            dq_ref[pl.ds(qs, BLK), :] = (jnp.concatenate(dq, axis=1) * scale).astype(dq_ref.dtype)
            return 0

        lax.fori_loop(0, nq, q_block, 0)
        dk_ref[...] = dk_acc[...].astype(dk_ref.dtype)
        dv_ref[...] = dv_acc[...].astype(dv_ref.dtype)

    def body(*refs):
        step = pl.program_id(0) * n_steps + pl.program_id(1)
        n_c, n_o = len(carry.arrays), len(carry.out_shapes)
        own = refs[:5] + refs[5 + n_c:8 + n_c] + refs[8 + n_c + n_o:10 + n_c + n_o]
        carry.run(refs, 5, 3, step, b_sz * n_steps, lambda: compute(*own))

    blk = lambda off: pl.BlockSpec((None, s_len, ch * LANES), lambda b, p: (b, 0, off + p))
    once = lambda off: pl.BlockSpec((None, s_len, ch * LANES), lambda b, p: (b, 0, off + p),
                                    pipeline_mode=pl.Buffered(1))
    out_sd = jax.ShapeDtypeStruct((b_sz, s_len, SB_WIDTH), BF16)
    c_in, c_out, c_shapes, c_alias, c_sems = carry.call_args(5, 3)
    res = pl.pallas_call(
        body, name="sb_bwd", grid=(b_sz, n_steps),
        in_specs=[once(0), once(n_steps), once(2 * n_steps), once(0), once(0)] + c_in,
        out_specs=[blk(0), blk(0), blk(0)] + c_out, out_shape=[out_sd, out_sd, out_sd] + c_shapes,
        input_output_aliases=c_alias,
        scratch_shapes=[pltpu.VMEM((s_len, ch * LANES), F32), pltpu.VMEM((s_len, ch * LANES), F32)] + c_sems,
        compiler_params=pltpu.CompilerParams(dimension_semantics=("arbitrary", "arbitrary"),
                                             vmem_limit_bytes=VMEM_CAP),
    )(qkv, qkv, qkv, o_sb, do_sb, *carry.arrays)
    return res[:3], res[3:]


def _dil_consts(group, pair_idx, dilation):
    lane = lax.broadcasted_iota(jnp.int32, (BLK, LANES), 1)
    head0 = lane < HEAD_DIM
    row = lax.broadcasted_iota(jnp.int32, (2 * BLK, BLK), 0)
    qa = row % BLK
    kb = lax.broadcasted_iota(jnp.int32, (2 * BLK, BLK), 1)
    head = (group * DIL_HEADS_PER_GROUP + 2 * pair_idx + row // BLK).astype(F32)
    slope = jnp.exp((-ALIBI_MAX_BIAS * math.log(2.0) / DIL_HEADS) * (head + 1.0))
    valid_cur = kb <= qa
    valid_prev = kb >= qa
    bias_cur = -slope * ((qa - kb) * dilation).astype(F32)
    bias_prev = -slope * ((BLK + qa - kb) * dilation).astype(F32)
    return head0, valid_cur, valid_prev, bias_cur, bias_prev


def _dil_units(s_len, dilation):
    nb = s_len // dilation // BLK
    return [(r, n) for r in range(dilation) for n in range(nb)]


def _dil_rows(n, r, dilation):
    if dilation == 1:
        return pl.ds(n * BLK, BLK)
    return pl.ds(n * BLK * dilation + r, BLK, stride=dilation)


def _dil_scores(q01, k, bias, valid):
    s = _dot_nt(q01, k) * (1.0 / math.sqrt(HEAD_DIM)) + bias
    return jnp.where(valid, s, NEG)


def _dil_fwd(qkv, b_sz, s_len, carry):
    n_pairs = DIL_OUT_WIDTH // LANES
    q_off = 3 * SB_WIDTH // LANES
    per_kind = DIL_WIDTH // LANES

    def compute(pair_idx, qkv_refs, o_ref, lse_ref, m_s, l_s):
        m_s[...] = jnp.full(m_s.shape, NEG, F32)
        l_s[...] = jnp.zeros(l_s.shape, F32)
        o_ref[...] = jnp.zeros(o_ref.shape, F32)
        for g, (_, dilation) in enumerate(DIL_PAIRS):
            q_ref, k_ref, v_ref = qkv_refs[3 * g:3 * g + 3]
            head0, valid_cur, valid_prev, bias_cur, bias_prev = _dil_consts(g, pair_idx, dilation)
            units = _dil_units(s_len, dilation)
            for u0 in range(0, len(units), DIL_CHAINS):
                group = units[u0:u0 + DIL_CHAINS]
                rows_of = [_dil_rows(n, r, dilation) for r, n in group]
                scores, values = [], []
                for (r, n), rows in zip(group, rows_of):
                    q01 = _stack_heads(q_ref[rows, :].astype(BF16), head0)
                    sc = [_dil_scores(q01, k_ref[rows, :].astype(BF16), bias_cur, valid_cur)]
                    vals = [_stack_heads(v_ref[rows, :].astype(BF16), head0)]
                    if n > 0:
                        prev = _dil_rows(n - 1, r, dilation)
                        sc.append(_dil_scores(q01, k_ref[prev, :].astype(BF16), bias_prev, valid_prev))
                        vals.append(_stack_heads(v_ref[prev, :].astype(BF16), head0))
                    scores.append(sc)
                    values.append(vals)
                stats = []
                for sc, rows in zip(scores, rows_of):
                    m_blk = functools.reduce(jnp.maximum, [jnp.max(x, axis=-1, keepdims=True) for x in sc])
                    m_old = jnp.concatenate([m_s.at[0][rows, :], m_s.at[1][rows, :]], axis=0)
                    l_old = jnp.concatenate([l_s.at[0][rows, :], l_s.at[1][rows, :]], axis=0)
                    m_new = jnp.maximum(m_old, m_blk)
                    probs = [jnp.exp(x - m_new) for x in sc]
                    l_blk = functools.reduce(jnp.add, [jnp.sum(p, axis=-1, keepdims=True) for p in probs])
                    alpha = jnp.exp(m_old - m_new)
                    stats.append((m_new, alpha * l_old + l_blk, alpha, probs))
                for (m_new, l_new, alpha, probs), vals, rows in zip(stats, values, rows_of):
                    alpha_tok = jnp.where(head0, alpha[:BLK], alpha[BLK:])
                    p_cat = jnp.concatenate(
                        [h for p in probs for h in (p[:BLK].astype(BF16), p[BLK:].astype(BF16))], axis=1)
                    o_ref[rows, :] = alpha_tok * o_ref[rows, :] + _dot(p_cat, jnp.concatenate(vals, axis=0))
                    m_s.at[0][rows, :] = m_new[:BLK]
                    m_s.at[1][rows, :] = m_new[BLK:]
                    l_s.at[0][rows, :] = l_new[:BLK]
                    l_s.at[1][rows, :] = l_new[BLK:]
        lane = lax.broadcasted_iota(jnp.int32, (BLK, LANES), 1)
        for c in range(s_len // BLK):
            rows = pl.ds(c * BLK, BLK)
            l0, l1 = l_s.at[0][rows, :], l_s.at[1][rows, :]
            o_ref[rows, :] = o_ref[rows, :] / jnp.where(lane < HEAD_DIM, l0, l1)
            lse_ref.at[0][rows, :] = m_s.at[0][rows, :] + jnp.log(l0)
            lse_ref.at[1][rows, :] = m_s.at[1][rows, :] + jnp.log(l1)

    def body(*refs):
        pair_idx = pl.program_id(1)
        step = pl.program_id(0) * n_pairs + pair_idx
        n_c, n_o = len(carry.arrays), len(carry.out_shapes)
        o_ref, lse_ref = refs[9 + n_c:11 + n_c]
        m_s, l_s = refs[11 + n_c + n_o:13 + n_c + n_o]
        carry.run(refs, 9, 2, step, b_sz * n_pairs, lambda: compute(pair_idx, refs[:9], o_ref, lse_ref, m_s, l_s))

    in_specs = []
    for g in range(len(DIL_PAIRS)):
        for kind in range(3):
            off = q_off + kind * per_kind + g * n_pairs
            in_specs.append(pl.BlockSpec((None, s_len, LANES), lambda b, p, off=off: (b, 0, off + p)))
    c_in, c_out, c_shapes, c_alias, c_sems = carry.call_args(9, 2)
    res = pl.pallas_call(
        body, name="dil_fwd", grid=(b_sz, n_pairs),
        in_specs=in_specs + c_in,
        out_specs=[pl.BlockSpec((None, s_len, LANES), lambda b, p: (b, 0, p)),
                   pl.BlockSpec((None, None, 2, s_len, LANES), lambda b, p: (b, p, 0, 0, 0))] + c_out,
        out_shape=[jax.ShapeDtypeStruct((b_sz, s_len, DIL_OUT_WIDTH), F32),
                   jax.ShapeDtypeStruct((b_sz, n_pairs, 2, s_len, LANES), F32)] + c_shapes,
        input_output_aliases=c_alias,
        scratch_shapes=[pltpu.VMEM((2, s_len, LANES), F32), pltpu.VMEM((2, s_len, LANES), F32)] + c_sems,
        compiler_params=pltpu.CompilerParams(dimension_semantics=("arbitrary", "arbitrary"),
                                             vmem_limit_bytes=VMEM_CAP),
    )(*([qkv] * 9), *carry.arrays)
    return res[0], res[1], res[2:]


def _dil_bwd(qkv, o_dl, lse, do_dl, b_sz, s_len, carry):
    n_pairs = DIL_OUT_WIDTH // LANES
    n_groups = len(DIL_PAIRS)
    q_off = 3 * SB_WIDTH // LANES
    per_kind = DIL_WIDTH // LANES

    def compute(pair_idx, group, q_ref, k_ref, v_ref, o_ref, lse_ref, do_ref, dq_ref, dk_ref, dv_ref, d_s, dq_s, dk_s, dv_s):
        lrow = lax.broadcasted_iota(jnp.int32, (LANES, LANES), 0)
        ones_h0 = jnp.where(lrow < HEAD_DIM, 1.0, 0.0).astype(BF16)
        ones_h1 = jnp.where(lrow >= HEAD_DIM, 1.0, 0.0).astype(BF16)
        for c in range(s_len // BLK):
            rows = pl.ds(c * BLK, BLK)
            dd_hi, dd_lo = _split2(do_ref[rows, :] * o_ref[rows, :])
            d_s.at[0][rows, :] = _dot(dd_hi, ones_h0) + _dot(dd_lo, ones_h0)
            d_s.at[1][rows, :] = _dot(dd_hi, ones_h1) + _dot(dd_lo, ones_h1)
        dk_s[...] = jnp.zeros(dk_s.shape, F32)
        dv_s[...] = jnp.zeros(dv_s.shape, F32)

        def one_group(g, dilation):
            head0, valid_cur, valid_prev, bias_cur, bias_prev = _dil_consts(g, pair_idx, dilation)
            units = _dil_units(s_len, dilation)
            scale = 1.0 / math.sqrt(HEAD_DIM)
            for u0 in range(0, len(units), DIL_CHAINS):
                chunk = units[u0:u0 + DIL_CHAINS]
                loaded = []
                for r, n in chunk:
                    rows = _dil_rows(n, r, dilation)
                    q01 = _stack_heads(q_ref[rows, :].astype(BF16), head0)
                    do01 = _stack_heads(do_ref[rows, :].astype(BF16), head0)
                    lse01 = jnp.concatenate([lse_ref.at[0][rows, :], lse_ref.at[1][rows, :]], axis=0)
                    d01 = jnp.concatenate([d_s.at[0][rows, :], d_s.at[1][rows, :]], axis=0)
                    blocks = [(rows, bias_cur, valid_cur)]
                    if n > 0:
                        blocks.append((_dil_rows(n - 1, r, dilation), bias_prev, valid_prev))
                    parts = []
                    for krows, bias, valid in blocks:
                        k = k_ref[krows, :].astype(BF16)
                        v = v_ref[krows, :].astype(BF16)
                        parts.append((krows, k, _dil_scores(q01, k, bias, valid), _dot_nt(do01, v)))
                    loaded.append((rows, q01, do01, lse01, d01, parts))
                grads = []
                for rows, q01, do01, lse01, d01, parts in loaded:
                    for krows, k, sc, dp in parts:
                        p = jnp.exp(sc - lse01)
                        grads.append((p.astype(BF16), (p * (dp - d01) * scale).astype(BF16)))
                it = iter(grads)
                updates = []
                for rows, q01, do01, lse01, d01, parts in loaded:
                    dq = jnp.zeros((2 * BLK, LANES), F32)
                    for krows, k, sc, dp in parts:
                        p_b, ds = next(it)
                        dq = dq + _dot(ds, k)
                        updates.append((krows, _dot_tn(ds, q01), _dot_tn(p_b, do01)))
                    dq_s[rows, :] = jnp.where(head0, dq[:BLK], dq[BLK:])
                for krows, dk, dv in updates:
                    dk_s[krows, :] = dk_s[krows, :] + dk
                    dv_s[krows, :] = dv_s[krows, :] + dv

        for g, (_, dilation) in enumerate(DIL_PAIRS):
            pl.when(group == g)(functools.partial(one_group, g, dilation))
        dq_ref[...] = dq_s[...].astype(dq_ref.dtype)
        dk_ref[...] = dk_s[...].astype(dk_ref.dtype)
        dv_ref[...] = dv_s[...].astype(dv_ref.dtype)

    def body(*refs):
        pair_idx, group = pl.program_id(1), pl.program_id(2)
        step = (pl.program_id(0) * n_pairs + pair_idx) * n_groups + group
        n_c, n_o = len(carry.arrays), len(carry.out_shapes)
        own = refs[:6] + refs[6 + n_c:9 + n_c] + refs[9 + n_c + n_o:13 + n_c + n_o]
        carry.run(refs, 6, 3, step, b_sz * n_pairs * n_groups, lambda: compute(pair_idx, group, *own))

    def qkv_spec(kind):
        return pl.BlockSpec((None, s_len, LANES),
                            lambda b, p, g: (b, 0, q_off + kind * per_kind + g * n_pairs + p))

    tok_spec = pl.BlockSpec((None, s_len, LANES), lambda b, p, g: (b, 0, p))
    out_spec = pl.BlockSpec((None, s_len, LANES), lambda b, p, g: (b, 0, g * n_pairs + p))
    out_sd = jax.ShapeDtypeStruct((b_sz, s_len, DIL_WIDTH), BF16)
    c_in, c_out, c_shapes, c_alias, c_sems = carry.call_args(6, 3)
    res = pl.pallas_call(
        body, name="dil_bwd", grid=(b_sz, n_pairs, n_groups),
        in_specs=[qkv_spec(0), qkv_spec(1), qkv_spec(2), tok_spec,
                  pl.BlockSpec((None, None, 2, s_len, LANES), lambda b, p, g: (b, p, 0, 0, 0)), tok_spec] + c_in,
        out_specs=[out_spec, out_spec, out_spec] + c_out,
        out_shape=[out_sd, out_sd, out_sd] + c_shapes,
        input_output_aliases=c_alias,
        scratch_shapes=[pltpu.VMEM((2, s_len, LANES), F32)] + [pltpu.VMEM((s_len, LANES), F32)] * 3 + c_sems,
        compiler_params=pltpu.CompilerParams(dimension_semantics=("arbitrary", "arbitrary", "arbitrary"),
                                             vmem_limit_bytes=VMEM_CAP),
    )(qkv, qkv, qkv, o_dl, lse, do_dl, *carry.arrays)
    return res[:3], res[3:]


def _mesh_pos():
    return lax.axis_index("x"), lax.axis_index("y"), lax.axis_index("c")


def _other_chips(x, y):
    return [(1 - x, y), (x, 1 - y), (1 - x, 1 - y)]


def _hbm_specs(n):
    return [pl.BlockSpec(memory_space=pl.ANY)] * n


SWAPPED = ("w_ffn_in",)


def _slot(x, y, swapped):
    return 2 * y + x if swapped else 2 * x + y


def _cast_to_slab(w, name):
    rows, cols = w.shape
    mine = jnp.reshape(_slot(lax.axis_index("x"), lax.axis_index("y"), False), (1,)).astype(jnp.int32)

    def body(idx_ref, w_ref, o_ref):
        o_ref[...] = w_ref[...].astype(BF16)

    return pl.pallas_call(
        body, name=name,
        grid_spec=pltpu.PrefetchScalarGridSpec(
            num_scalar_prefetch=1, grid=(1,),
            in_specs=[pl.BlockSpec((rows, cols), lambda i, idx: (0, 0))],
            out_specs=pl.BlockSpec((None, rows, cols), lambda i, idx: (idx[0], 0, 0))),
        out_shape=_hbm_array((N_CHIPS, rows, cols), BF16),
        compiler_params=pltpu.CompilerParams(vmem_limit_bytes=_vmem_limit(rows * cols * 6)),
    )(mine, w)


def _gather_issue(slabs, send_sems, recv_sems, swapped):
    x, y, c = _mesh_pos()
    for k, slab in enumerate(slabs):
        half = slab.shape[1] // 2
        rows = slab.at[_slot(x, y, swapped[k]), pl.ds(c * half, half), :]
        for r, (px, py) in enumerate(_other_chips(x, y)):
            pltpu.make_async_remote_copy(
                src_ref=rows, dst_ref=rows, send_sem=send_sems.at[6 * k + r], recv_sem=recv_sems.at[6 * k + r],
                device_id=(px, py, c), device_id_type=MESH).start()


def _gather_complete(slabs, send_sems, recv_sems, swapped):
    x, y, c = _mesh_pos()
    chips = _other_chips(x, y)

    def copy(k, sem, block, rows, to):
        ref = slabs[k].at[block, rows, :]
        return pltpu.make_async_remote_copy(
            src_ref=ref, dst_ref=ref, send_sem=send_sems.at[sem], recv_sem=recv_sems.at[sem],
            device_id=to, device_id_type=MESH)

    for k, slab in enumerate(slabs):
        half = slab.shape[1] // 2
        for r, (px, py) in enumerate(chips):
            theirs = _slot(px, py, swapped[k])
            copy(k, 6 * k + r, theirs, pl.ds(c * half, half), (px, py, c)).wait_recv()
            copy(k, 6 * k + 3 + r, theirs, pl.ds(c * half, half), (x, y, 1 - c)).start()
    for k, slab in enumerate(slabs):
        half = slab.shape[1] // 2
        for r, (px, py) in enumerate(chips):
            copy(k, 6 * k + 3 + r, _slot(px, py, swapped[k]), pl.ds((1 - c) * half, half), (x, y, 1 - c)).wait_recv()
    for k, slab in enumerate(slabs):
        half = slab.shape[1] // 2
        for r, (px, py) in enumerate(chips):
            copy(k, 6 * k + r, _slot(x, y, swapped[k]), pl.ds(c * half, half), (px, py, c)).wait_send()
            copy(k, 6 * k + 3 + r, _slot(px, py, swapped[k]), pl.ds(c * half, half), (x, y, 1 - c)).wait_send()


def _gather_sems(n):
    return [pltpu.SemaphoreType.DMA((6 * n,)), pltpu.SemaphoreType.DMA((6 * n,))]


def _gather_carry(slabs, names):
    swapped = [k in SWAPPED for k in names]
    return _Carry(slabs, [_hbm_array(a.shape, a.dtype) for a in slabs], True, _gather_sems(len(slabs)),
                  lambda ins, outs, sems: _gather_issue(outs, *sems, swapped),
                  lambda ins, outs, sems: _gather_complete(outs, *sems, swapped))


def _cast_carry(shards, names):
    swapped = [k in SWAPPED for k in names]

    def start(ins, outs, sems):
        x, y, _ = _mesh_pos()
        for w, slab, sw in zip(ins, outs, swapped):
            def cast(f32_buf, bf16_buf, sem, w=w, slab=slab, sw=sw):
                load = pltpu.make_async_copy(w, f32_buf, sem)
                load.start()
                load.wait()
                bf16_buf[...] = f32_buf[...].astype(BF16)
                store = pltpu.make_async_copy(bf16_buf, slab.at[_slot(x, y, sw)], sem)
                store.start()
                store.wait()

            pl.run_scoped(cast, pltpu.VMEM(w.shape, F32), pltpu.VMEM(w.shape, BF16), pltpu.SemaphoreType.DMA)

    return _Carry(shards, [_hbm_array((N_CHIPS,) + w.shape, BF16) for w in shards], False, [], start,
                  lambda ins, outs, sems: None)


def _pair_copies(ins, outs, send_sems, recv_sems):
    x, y, c = _mesh_pos()
    copies = []
    for k, g in enumerate(ins):
        half = g.shape[1] // 2
        copies.append(pltpu.make_async_remote_copy(
            src_ref=g.at[:, pl.ds((1 - c) * half, half), :], dst_ref=outs[k],
            send_sem=send_sems.at[k], recv_sem=recv_sems.at[k],
            device_id=(x, y, 1 - c), device_id_type=MESH))
    return copies


def _pair_carry(grads):
    n = len(grads)

    def start(ins, outs, sems):
        for cp in _pair_copies(ins, outs, *sems):
            cp.start()

    def finish(ins, outs, sems):
        for cp in _pair_copies(ins, outs, *sems):
            cp.wait()

    return _Carry(grads, [jax.ShapeDtypeStruct((N_CHIPS, g.shape[1] // 2, g.shape[2]), g.dtype) for g in grads], False,
                  [pltpu.SemaphoreType.DMA((n,)), pltpu.SemaphoreType.DMA((n,))], start, finish)


def _pair_exchange(grads, tag):
    carry = _pair_carry(grads)
    n = len(grads)

    def body(*refs):
        carry.start(refs[:n], refs[n:2 * n], refs[2 * n:])
        carry.finish(refs[:n], refs[n:2 * n], refs[2 * n:])

    return pl.pallas_call(
        body, name="grad_pair_exchange_" + tag, in_specs=_hbm_specs(n), out_specs=_hbm_specs(n),
        out_shape=carry.out_shapes, scratch_shapes=carry.sems,
    )(*grads)


def _pair_sum(grad, other, name, swapped):
    _, rows, cols = grad.shape
    half = rows // 2
    x, y, c = _mesh_pos()
    idx = jnp.stack([c, _slot(x, y, swapped)]).astype(jnp.int32)

    def body(idx_ref, g_ref, p_ref, own_ref, sb_ref):
        s = g_ref[...] + p_ref[...].astype(F32)
        sb_ref[...] = s.astype(BF16)

        @pl.when(pl.program_id(0) == idx_ref[1])
        def _():
            own_ref[...] = s

    blk = pl.BlockSpec((None, half, cols), lambda p, idx: (p, 0, 0))
    return pl.pallas_call(
        body, name=name,
        grid_spec=pltpu.PrefetchScalarGridSpec(
            num_scalar_prefetch=1, grid=(N_CHIPS,),
            in_specs=[pl.BlockSpec((None, half, cols), lambda p, idx: (p, idx[0], 0)), blk],
            out_specs=[pl.BlockSpec((half, cols), lambda p, idx: (0, 0)), blk]),
        out_shape=[jax.ShapeDtypeStruct((half, cols), F32), jax.ShapeDtypeStruct((N_CHIPS, half, cols), BF16)],
        compiler_params=pltpu.CompilerParams(dimension_semantics=("arbitrary",),
                                             vmem_limit_bytes=_vmem_limit(4 * half * cols * 4)),
    )(idx, grad, other)


def _chip_copies(sums_bf16, lands, send_sems, recv_sems, swapped):
    x, y, c = _mesh_pos()
    return [pltpu.make_async_remote_copy(
        src_ref=sums_bf16[k].at[_slot(px, py, swapped[k])], dst_ref=lands[k].at[r],
        send_sem=send_sems.at[3 * k + r], recv_sem=recv_sems.at[3 * k + r],
        device_id=(px, py, c), device_id_type=MESH)
        for k in range(len(sums_bf16)) for r, (px, py) in enumerate(_other_chips(x, y))]


def _chip_carry(sums_bf16, names):
    swapped = [k in SWAPPED for k in names]

    def start(ins, outs, sems):
        for cp in _chip_copies(ins, outs, *sems, swapped):
            cp.start()

    def finish(ins, outs, sems):
        for cp in _chip_copies(ins, outs, *sems, swapped):
            cp.wait()

    return _Carry(sums_bf16, _chip_landing(sums_bf16), False, _chip_sems(len(sums_bf16)), start, finish)


def _chip_sems(n):
    return [pltpu.SemaphoreType.DMA((3 * n,)), pltpu.SemaphoreType.DMA((3 * n,))]


def _chip_landing(sums_bf16):
    return [jax.ShapeDtypeStruct((N_CHIPS - 1,) + s.shape[1:], BF16) for s in sums_bf16]


def _chip_sum(own, landed, name):
    rows, cols = own.shape
    core = jnp.reshape(lax.axis_index("c"), (1,)).astype(jnp.int32)

    def body(core_ref, o_ref, l_ref, out_ref):
        out_ref[...] = ((o_ref[...] + l_ref[0].astype(F32)) + l_ref[1].astype(F32)) + l_ref[2].astype(F32)

    return pl.pallas_call(
        body, name=name,
        grid_spec=pltpu.PrefetchScalarGridSpec(
            num_scalar_prefetch=1, grid=(1,),
            in_specs=[pl.BlockSpec((rows, cols), lambda i, core_ref: (0, 0)),
                      pl.BlockSpec((N_CHIPS - 1, rows, cols), lambda i, core_ref: (0, 0, 0))],
            out_specs=pl.BlockSpec((rows, cols), lambda i, core_ref: (core_ref[0], 0))),
        out_shape=jax.ShapeDtypeStruct((2 * rows, cols), F32),
        compiler_params=pltpu.CompilerParams(vmem_limit_bytes=_vmem_limit(3 * rows * cols * 4)),
    )(core, own, landed)


def _halves_carry(fulls):
    n = len(fulls)

    def copies(outs, send_sems, recv_sems, own):
        x, y, c = _mesh_pos()
        res = []
        for k, out in enumerate(outs):
            half = out.shape[0] // 2
            rows = out.at[pl.ds((c if own else 1 - c) * half, half), :]
            res.append(pltpu.make_async_remote_copy(
                src_ref=rows, dst_ref=rows, send_sem=send_sems.at[k], recv_sem=recv_sems.at[k],
                device_id=(x, y, 1 - c), device_id_type=MESH))
        return res

    def start(ins, outs, sems):
        for cp in copies(outs, *sems, True):
            cp.start()

    def finish(ins, outs, sems):
        for cp in copies(outs, *sems, False):
            cp.wait_recv()
        for cp in copies(outs, *sems, True):
            cp.wait_send()

    return _Carry(fulls, [jax.ShapeDtypeStruct(f.shape, F32) for f in fulls], True,
                  [pltpu.SemaphoreType.DMA((n,)), pltpu.SemaphoreType.DMA((n,))], start, finish)


def _final_exchange(fulls, v):
    n = len(fulls)
    rows, cols = v.shape
    n_dev = 8

    def body(*refs):
        v_ref, out_ref = refs[0], refs[1 + 2 * n]
        outs = refs[1 + n:1 + 2 * n]
        buf, v_send, v_recv, h_send, h_recv = refs[2 + 2 * n:]
        x, y, c = _mesh_pos()
        me = 4 * x + 2 * y + c
        buf[me] = v_ref[...]
        peers = [(1 - x if r & 4 else x, 1 - y if r & 2 else y, 1 - c if r & 1 else c) for r in range(1, n_dev)]
        copies = []
        for r, peer in enumerate(peers):
            copies.append(pltpu.make_async_remote_copy(
                src_ref=v_ref, dst_ref=buf.at[me], send_sem=v_send.at[r], recv_sem=v_recv.at[r],
                device_id=peer, device_id_type=MESH))
        for k in range(n):
            half = fulls[k].shape[0] // 2
            mine = outs[k].at[pl.ds(c * half, half), :]
            copies.append(pltpu.make_async_remote_copy(
                src_ref=mine, dst_ref=mine, send_sem=h_send.at[k], recv_sem=h_recv.at[k],
                device_id=(x, y, 1 - c), device_id_type=MESH))
        for cp in copies:
            cp.start()
        for r, (px, py, pc) in enumerate(peers):
            pltpu.make_async_remote_copy(
                src_ref=v_ref, dst_ref=buf.at[4 * px + 2 * py + pc], send_sem=v_send.at[r], recv_sem=v_recv.at[r],
                device_id=(px, py, pc), device_id_type=MESH).wait_recv()
        for k in range(n):
            half = fulls[k].shape[0] // 2
            theirs = outs[k].at[pl.ds((1 - c) * half, half), :]
            pltpu.make_async_remote_copy(
                src_ref=theirs, dst_ref=theirs, send_sem=h_send.at[k], recv_sem=h_recv.at[k],
                device_id=(x, y, 1 - c), device_id_type=MESH).wait_recv()
        for cp in copies:
            cp.wait_send()
        acc = buf[0]
        for d in range(1, n_dev):
            acc = acc + buf[d]
        out_ref[...] = acc
        out_ref[3:4, :] = jnp.broadcast_to(jnp.sum(acc[3:4, :], axis=1, keepdims=True), (1, cols))

    vm = pl.BlockSpec(memory_space=pltpu.VMEM)
    res = pl.pallas_call(
        body, name="final_exchange",
        in_specs=[vm] + _hbm_specs(n), out_specs=_hbm_specs(n) + [vm],
        out_shape=[jax.ShapeDtypeStruct(f.shape, F32) for f in fulls] + [jax.ShapeDtypeStruct((rows, cols), F32)],
        input_output_aliases={1 + k: k for k in range(n)},
        scratch_shapes=[pltpu.VMEM((n_dev, rows, cols), F32),
                        pltpu.SemaphoreType.DMA((n_dev - 1,)), pltpu.SemaphoreType.DMA((n_dev - 1,)),
                        pltpu.SemaphoreType.DMA((n,)), pltpu.SemaphoreType.DMA((n,))],
    )(v, *fulls)
    return res[:n], res[n]


def _adamw_math(w, g, m, v):
    m = ADAM_B1 * m + (1.0 - ADAM_B1) * g
    v = ADAM_B2 * v + (1.0 - ADAM_B2) * (g * g)
    m_hat = m / (1.0 - ADAM_B1 ** ADAM_STEP)
    v_hat = v / (1.0 - ADAM_B2 ** ADAM_STEP)
    delta = -ADAM_LR * (m_hat / (jnp.sqrt(v_hat) + ADAM_EPS) + ADAM_WD * w)
    return delta, m, v


def _adamw(w, g, m, v, name):
    rows, cols = w.shape
    tm = rows // 2 if (rows // 2) % 8 == 0 else rows
    return _rowwise(_adamw_math, [w, g, m, v], [], [(cols, F32)] * 3, [], tm=tm, name=name)


def _unshard_cols(gathered):
    n, r, c = gathered.shape
    return jnp.transpose(gathered, (1, 0, 2)).reshape(r, n * c)


def _shard_cols(full):
    r, nc = full.shape
    return jnp.transpose(full.reshape(r, N_CHIPS, nc // N_CHIPS), (1, 0, 2))


LATE = ["w_sb_up", "w_dil_up", "w_out", "w_ffn_in", "w_ffn_out"]


def _late_weights(slabs, d_model, d_ff):
    g = dict(zip(LATE, slabs))
    return (_unshard_cols(g["w_sb_up"]), _unshard_cols(g["w_dil_up"]), g["w_out"].reshape(d_model, d_model),
            _unshard_cols(g["w_ffn_in"]), g["w_ffn_out"].reshape(d_ff, d_model))


ROW_SHARDED = ("w_in", "w_out", "w_ffn_in", "w_ffn_out")


def _chip_major(grads):
    out = []
    for k, g in grads.items():
        if k in ROW_SHARDED:
            out.append(g.reshape(N_CHIPS, g.shape[0] // N_CHIPS, g.shape[1]))
        else:
            out.append(_shard_cols(g))
    return out


def _pair_sums(full, others, names):
    return [_pair_sum(g, o, "grad_pair_sum_" + k, k in SWAPPED) for g, o, k in zip(full, others, names)]


def _chip_sums(pair, landed, names):
    return {k: _chip_sum(p[0], l, "grad_chip_sum_" + k) for p, l, k in zip(pair, landed, names)}


def _fwd_bwd(x, loss_target, g_mix, g_ffn, g_fin, slab_in, late_shards):
    b_sz, s_len, d_model = x.shape
    t = b_sz * s_len
    d_ff = late_shards[-1].shape[0] * N_CHIPS
    x2d = x.reshape(t, d_model)
    tgt2d = loss_target.reshape(t, d_model)

    u, (slab_in, *late_slabs) = _rowwise(
        lambda xv, g: (_rms_stats(xv)[0] * g,), [_in_hbm(x2d)], [g_mix], [(d_model, BF16)], [], tm=512, name="norm_mix",
        carry=_gather_carry([slab_in], ["w_in"]) + _cast_carry(late_shards, LATE), out_type=_hbm_array)
    u, wt_in = _in_hbm(u), _in_hbm(slab_in.reshape(-1, d_model))
    qkv, (slab_ffn_out,) = _mm(u, wt_in, tb=True, b_cols=(0, QKV_WIDTH), tm=2048, tn=768, tk=d_model, name="proj_qkv",
                               carry=_gather_carry(late_slabs[4:], LATE[4:]))
    gates = _mm(u, wt_in, tb=True, b_cols=(QKV_WIDTH, 2 * d_model), out_dtype=BF16, tm=t, tn=256, tk=d_model,
                name="proj_gates")
    qkv3 = qkv.reshape(b_sz, s_len, QKV_WIDTH)
    o_sb, (slab_ffn_in,) = _sb_fwd(qkv3, b_sz, s_len, _gather_carry(late_slabs[3:4], LATE[3:4]))
    o_dl, lse, small_slabs = _dil_fwd(qkv3, b_sz, s_len, _gather_carry(late_slabs[:3], LATE[:3]))
    wf_sb_up, wf_dil_up, wf_out, wf_ffn_in, wf_ffn_out = _late_weights(
        list(small_slabs) + [slab_ffn_in, slab_ffn_out], d_model, d_ff)
    o_sb2, o_dl2 = o_sb.reshape(t, SB_WIDTH), o_dl.reshape(t, DIL_OUT_WIDTH)
    y_sb = _mm(o_sb2, wf_sb_up, out_dtype=BF16, tm=1024, tn=1024, tk=SB_WIDTH, name="sb_up", out_type=_hbm_array)
    y_dl = _mm(o_dl2, wf_dil_up, out_dtype=BF16, tm=1024, tn=1024, tk=DIL_OUT_WIDTH, name="dil_up", out_type=_hbm_array)

    def merge_fn(gt, ys, yd):
        return (_sigmoid(gt[:, :d_model]) * ys + _sigmoid(gt[:, d_model:]) * yd,)

    (merged,) = _rowwise(merge_fn, [gates, y_sb, y_dl], [], [(d_model, BF16)], [], tm=512, name="merge")
    x1 = _mm(merged, wf_out, add=x2d, tm=512, tn=1024, tk=d_model, name="mix_out")
    (u2,) = _rowwise(lambda xv, g: (_rms_stats(xv)[0] * g,), [_in_hbm(x1)], [g_ffn], [(d_model, BF16)], [], tm=512, name="norm_ffn",
                     out_type=_hbm_array)
    u2 = _in_hbm(u2)
    half_ff = d_ff // 2

    def act_fn(hv):
        gate = hv[:, :half_ff]
        return hv, gate * _sigmoid(gate) * hv[:, half_ff:]

    h, act = _mm(u2, wf_ffn_in, tm=512, tn=d_ff, tk=d_model, name="ffn_in",
                 epilogue=(act_fn, [], [], [(d_ff, BF16), (half_ff, BF16)], []))
    def head_fn(xv, tg, g):
        xhat, r = _rms_stats(xv)
        err = xhat * g - tg
        dy = err * (1.0 / d_model)
        dx, dg_rows = _rms_bwd(dy, xhat, r, g)
        loss_lanes = (0.5 / d_model) * jnp.sum(err * err, axis=0, keepdims=True)
        return dx, dx, jnp.sum(dg_rows, axis=0, keepdims=True), loss_lanes

    dx2, dx2_b, dg_fin, loss_lanes = _mm(
        act, wf_ffn_out, add=x1, tm=512, tn=1024, tk=d_ff, name="ffn_out",
        epilogue=(head_fn, [tgt2d], [g_fin], [(d_model, F32), (d_model, BF16)], [(1, d_model), (1, d_model)]))

    def dact_fn(da, hv):
        gate, up = hv[:, :half_ff], hv[:, half_ff:]
        sg = _sigmoid(gate)
        dgate = da * up * (sg * (1.0 + gate * (1.0 - sg)))
        return (jnp.concatenate([dgate, da * (gate * sg)], axis=1),)

    dx2_b = _in_hbm(dx2_b)
    (dh,) = _mm(dx2_b, wf_ffn_out, tb=True, tm=512, tn=half_ff, tk=d_model, name="ffn_out_dx",
                epilogue=(dact_fn, [h], [], [(d_ff, BF16)], []))
    gw_ffn_out = _mm(act, dx2_b, ta=True, tm=256, tn=d_model, tk=t, name="ffn_out_dw")
    def norm_bwd_fn(du_, dres, xv, g):
        xhat, r = _rms_stats(xv)
        dx, dg_rows = _rms_bwd(du_, xhat, r, g)
        return dres + dx, jnp.sum(dg_rows, axis=0, keepdims=True)

    def norm_bwd_twice(*args):
        dx, dg = norm_bwd_fn(*args)
        return dx, dx, dg

    dx1, dx1_b, dg_ffn = _mm(dh, wf_ffn_in, tb=True, tm=512, tn=1024, tk=2 * d_ff, name="ffn_in_dx",
                             epilogue=(norm_bwd_twice, [dx2, x1], [g_ffn], [(d_model, F32), (d_model, BF16)], [(1, d_model)]))
    gwt_ffn_in = _mm(dh, u2, ta=True, tm=512, tn=d_model, tk=t, name="ffn_in_dw")

    dx1_b = _in_hbm(dx1_b)
    dmerged = _mm(dx1_b, wf_out, tb=True, out_dtype=BF16, tm=512, tn=1024, tk=d_model, name="mix_out_dx",
                  out_type=_hbm_array)
    gw_out = _mm(merged, dx1_b, ta=True, tm=256, tn=d_model, tk=t, name="mix_out_dw")

    def merge_bwd_fn(gt, ys, yd, dm):
        s_sb, s_dl = _sigmoid(gt[:, :d_model]), _sigmoid(gt[:, d_model:])
        dgates = jnp.concatenate([dm * ys * s_sb * (1.0 - s_sb), dm * yd * s_dl * (1.0 - s_dl)], axis=1)
        return dgates, dm * s_sb, dm * s_dl

    full_big = _chip_major({"w_out": gw_out, "w_ffn_in": gwt_ffn_in})
    dgates, dy_sb, dy_dl, others_big = _rowwise(
        merge_bwd_fn, [gates, y_sb, y_dl, dmerged], [], [(2 * d_model, BF16), (d_model, BF16), (d_model, BF16)], [],
        tm=256, name="merge_bwd", carry=_pair_carry(full_big))
    pair_big = _pair_sums(full_big, others_big, LATE[2:4])
    do_sb = _mm(dy_sb, wf_sb_up, tb=True, out_dtype=BF16, tm=1024, tn=SB_WIDTH, tk=d_model, name="sb_up_dx")
    gw_sb_up = _mm(o_sb2, dy_sb, ta=True, tm=SB_WIDTH, tn=1024, tk=512, name="sb_up_dw")
    do_dl = _mm(dy_dl, wf_dil_up, tb=True, tm=1024, tn=DIL_OUT_WIDTH, tk=d_model, name="dil_up_dx")
    gw_dil_up = _mm(o_dl2, dy_dl, ta=True, tm=DIL_OUT_WIDTH, tn=1024, tk=512, name="dil_up_dw")
    rest = [LATE[0], LATE[1], LATE[4]]
    full_rest = _chip_major({"w_sb_up": gw_sb_up, "w_dil_up": gw_dil_up, "w_ffn_out": gw_ffn_out})
    (dq_sb, dk_sb, dv_sb), brought = _sb_bwd(
        qkv3, o_sb, do_sb.reshape(b_sz, s_len, SB_WIDTH), b_sz, s_len,
        _chip_carry([p[1] for p in pair_big], LATE[2:4]) + _pair_carry(full_rest))
    pair_rest = _pair_sums(full_rest, brought[2:], rest)
    (dq_dl, dk_dl, dv_dl), landed_b = _dil_bwd(
        qkv3, o_dl, lse, do_dl.reshape(b_sz, s_len, DIL_OUT_WIDTH), b_sz, s_len,
        _chip_carry([p[1] for p in pair_rest], rest))
    pair = pair_rest[:2] + pair_big + pair_rest[2:]
    landed = [landed_b[0], landed_b[1], brought[0], brought[1], landed_b[2]]
    dproj = [a.reshape(t, -1) for a in (dq_sb, dk_sb, dv_sb, dq_dl, dk_dl, dv_dl)] + [dgates]
    gwt_in, gwt_in_b = _mm(dproj, u, ta=True, tm=256, tn=d_model, tk=t, name="proj_dw",
                           epilogue=(lambda tile: (tile, tile), [], [], [(d_model, F32), (d_model, BF16)], []))
    full_in = _chip_major({"w_in": gwt_in})
    pair_in = _pair_sums(full_in, _pair_exchange(_chip_major({"w_in": gwt_in_b}), "w_in"), ["w_in"])
    late_halves = _chip_sums(pair, landed, LATE)
    (dx, dg_mix), brought_in = _mm(
        dproj, wt_in, tm=512, tn=1024, tk=wt_in.shape[0], name="proj_dx",
        carry=_halves_carry([late_halves[k] for k in LATE]) + _chip_carry([p[1] for p in pair_in], ["w_in"]),
        epilogue=(norm_bwd_fn, [dx1, x2d], [g_mix], [(d_model, F32)], [(1, d_model)]))

    grads = dict(zip(LATE, brought_in[:len(LATE)]))
    grads.update(_chip_sums(pair_in, brought_in[len(LATE):], ["w_in"]))
    return dx, grads, dg_mix, dg_ffn, dg_fin, loss_lanes


def kernel(x, norm_mix_g, w_in, w_sb_up, w_dil_up, w_out, norm_ffn_g, w_ffn_in, w_ffn_out, norm_final_g, loss_target, m_norm_mix_g, m_w_in, m_w_sb_up, m_w_dil_up, m_w_out, m_norm_ffn_g, m_w_ffn_in, m_w_ffn_out, m_norm_final_g, v_norm_mix_g, v_w_in, v_w_sb_up, v_w_dil_up, v_w_out, v_norm_ffn_g, v_w_ffn_in, v_w_ffn_out, v_norm_final_g):
    b_sz, s_len, d_model = x.shape
    d_ff = w_ffn_out.shape[1] * N_CHIPS
    g_mix, g_ffn, g_fin = norm_mix_g, norm_ffn_g, norm_final_g.reshape(1, d_model)

    names = ["w_in", "w_sb_up", "w_dil_up", "w_out", "w_ffn_in", "w_ffn_out"]
    shards = {"w_in": jnp.swapaxes(w_in[0], 0, 1), "w_sb_up": w_sb_up[0], "w_dil_up": w_dil_up[0], "w_out": w_out[0],
              "w_ffn_in": w_ffn_in[0], "w_ffn_out": w_ffn_out[0]}
    slab_in = _cast_to_slab(shards["w_in"], "cast_w_in")

    dx, grads, dg_mix, dg_ffn, dg_fin, loss_lanes = _fwd_bwd(
        x, loss_target, g_mix, g_ffn, g_fin, slab_in, [shards[k] for k in LATE])

    small = jnp.concatenate([dg_mix, dg_ffn, dg_fin, loss_lanes, jnp.zeros((4, d_model), F32)], axis=0)
    (grads["w_in"],), small = _final_exchange([grads["w_in"]], small)
    grads["w_ffn_in"] = jnp.swapaxes(grads["w_ffn_in"], 0, 1)
    loss = small[3, 0]
    gains = jnp.concatenate([g_mix, g_ffn, g_fin, jnp.zeros((5, d_model), F32)], axis=0)
    gains_m = jnp.concatenate([m_norm_mix_g, m_norm_ffn_g, m_norm_final_g.reshape(1, d_model), jnp.zeros((5, d_model), F32)], axis=0)
    gains_v = jnp.concatenate([v_norm_mix_g, v_norm_ffn_g, v_norm_final_g.reshape(1, d_model), jnp.ones((5, d_model), F32)], axis=0)
    gd, gm, gv = _rowwise(_adamw_math, [gains, small, gains_m, gains_v], [], [(d_model, F32)] * 3, [], tm=8, name="adamw_gains")

    moments = {"w_in": (jnp.swapaxes(m_w_in[0], 0, 1), jnp.swapaxes(v_w_in[0], 0, 1)),
               "w_sb_up": (m_w_sb_up[0], v_w_sb_up[0]), "w_dil_up": (m_w_dil_up[0], v_w_dil_up[0]),
               "w_out": (m_w_out[0], v_w_out[0]), "w_ffn_in": (m_w_ffn_in[0], v_w_ffn_in[0]),
               "w_ffn_out": (m_w_ffn_out[0], v_w_ffn_out[0])}
    upd = {k: _adamw(shards[k], grads[k], moments[k][0], moments[k][1], "adamw_" + k) for k in names}

    def as_output(k, a):
        return (jnp.swapaxes(a, 0, 1) if k == "w_in" else a)[None]

    def w_out_of(i):
        return [as_output(k, upd[k][i]) for k in names]

    def ordered(mix, ws, ffn_g, fin):
        return [mix, ws[0], ws[1], ws[2], ws[3], ffn_g, ws[4], ws[5], fin]

    grad_ws = [as_output(k, grads[k]) for k in names]
    outs = [loss, dx.reshape(b_sz, s_len, d_model)]
    outs += ordered(small[0:1], grad_ws, small[1:2], small[2])
    outs += ordered(gd[0:1], w_out_of(0), gd[1:2], gd[2])
    outs += ordered(gm[0:1], w_out_of(1), gm[1:2], gm[2])
    outs += ordered(gv[0:1], w_out_of(2), gv[1:2], gv[2])
    return tuple(outs)
```

```python
import functools
import math

import jax
import jax.numpy as jnp
from jax import lax
from jax.experimental import pallas as pl
from jax.experimental.pallas import tpu as pltpu

F32 = jnp.float32
BF16 = jnp.bfloat16
MESH = pl.DeviceIdType.MESH

HEAD_DIM = 64
SB_HEADS = 8
DIL_PAIRS = ((128, 1), (512, 4), (2048, 16))
DIL_HEADS_PER_GROUP = 4
DIL_HEADS = DIL_HEADS_PER_GROUP * len(DIL_PAIRS)
SB_WIDTH = SB_HEADS * HEAD_DIM
DIL_WIDTH = DIL_HEADS * HEAD_DIM
DIL_OUT_WIDTH = DIL_HEADS_PER_GROUP * HEAD_DIM
QKV_WIDTH = 3 * SB_WIDTH + 3 * DIL_WIDTH
RMS_EPS = 1e-6
ALIBI_MAX_BIAS = 8.0
ADAM_LR = 0.001
ADAM_B1 = 0.9
ADAM_B2 = 0.999
ADAM_EPS = 1e-08
ADAM_WD = 0.01
ADAM_STEP = 10

LANES = 128
BLK = 128
NEG = -1e30
EXP_UNDERFLOW = -104.0
SB_FWD_CHAINS = 4
SB_BWD_CHAINS = 4
DIL_CHAINS = 16
N_CHIPS = 4
VMEM_CAP = 56 * 1024 * 1024


def _vmem_limit(tile_bytes):
    return int(min(VMEM_CAP, max(32 * 1024 * 1024, 3 * tile_bytes + 8 * 1024 * 1024)))


def _hbm_array(shape, dtype):
    return pltpu.HBM(shape, dtype)


def _nbytes(shape, dtype):
    return math.prod(shape) * jnp.dtype(dtype).itemsize


def _in_hbm(x):
    return pltpu.with_memory_space_constraint(x, pltpu.HBM)


def _dot(a, b):
    return jnp.dot(a, b, preferred_element_type=F32)


def _dot_nt(a, b):
    return lax.dot_general(a, b, (((1,), (1,)), ((), ())), preferred_element_type=F32)


def _dot_tn(a, b):
    return lax.dot_general(a, b, (((0,), (0,)), ((), ())), preferred_element_type=F32)


def _split2(x):
    hi = x.astype(BF16)
    lo = (x - hi.astype(F32)).astype(BF16)
    return hi, lo


def _sigmoid(x):
    return pl.reciprocal(1.0 + jnp.exp(-x), approx=True)


class _Carry:
    def __init__(self, arrays=(), out_shapes=(), aliased=False, sems=(), start=None, finish=None):
        self.arrays, self.out_shapes = list(arrays), list(out_shapes)
        self.n_aliased = len(self.arrays) if aliased is True else int(aliased)
        self.sems, self.start, self.finish = list(sems), start, finish

    def __bool__(self):
        return bool(self.arrays)

    def __add__(self, other):
        assert not other.n_aliased and self.n_aliased in (0, len(self.out_shapes))
        n_a, n_o, n_s = len(self.arrays), len(self.out_shapes), len(self.sems)
        return _Carry(
            self.arrays + other.arrays, self.out_shapes + other.out_shapes, self.n_aliased, self.sems + other.sems,
            lambda i, o, s: (self.start(i[:n_a], o[:n_o], s[:n_s]), other.start(i[n_a:], o[n_o:], s[n_s:])),
            lambda i, o, s: (self.finish(i[:n_a], o[:n_o], s[:n_s]), other.finish(i[n_a:], o[n_o:], s[n_s:])))

    def call_args(self, n_in, n_out):
        aliases = {n_in + k: n_out + k for k in range(self.n_aliased)}
        return _hbm_specs(len(self.arrays)), _hbm_specs(len(self.out_shapes)), self.out_shapes, aliases, self.sems

    def run(self, refs, n_in, n_out, step, n_steps, compute):
        if not self:
            compute()
            return
        n_c, n_o, n_s = len(self.arrays), len(self.out_shapes), len(self.sems)
        ins = refs[n_in:n_in + n_c]
        outs = refs[n_in + n_c + n_out:n_in + n_c + n_out + n_o]
        sems = refs[len(refs) - n_s:]

        @pl.when(step == 0)
        def _():
            self.start(ins, outs, sems)

        compute()

        @pl.when(step == n_steps - 1)
        def _():
            self.finish(ins, outs, sems)


def _mm(a, b, *, ta=False, tb=False, add=None, out_dtype=F32, tm, tn, tk, name, carry=None, epilogue=None,
        b_cols=None, out_type=jax.ShapeDtypeStruct):
    carry = carry or _Carry()
    n_car = len(carry.arrays)
    pieces = list(a) if isinstance(a, (list, tuple)) else [a]
    n_a = len(pieces)
    widths = [p.shape[1] for p in pieces]
    starts = [sum(widths[:p]) for p in range(n_a)]
    if ta:
        kdim, m = pieces[0].shape[0], sum(widths)
    else:
        m, kdim = pieces[0].shape[0], sum(widths)
    if tb:
        n, k2 = b.shape
    else:
        k2, n = b.shape
    col0 = 0
    if b_cols is not None:
        assert b_cols[0] % tn == 0, name
        col0, n = b_cols[0] // tn, b_cols[1]
    assert kdim == k2 and m % tm == 0 and n % tn == 0 and kdim % tk == 0, (name, a.shape, b.shape)
    nk = kdim // tk
    assert n_a == 1 or (nk == 1 and not tb and (not ta or all(w % tm == 0 for w in widths))), name
    grid = (m // tm, n // tn, nk)
    a_mode = dict(pipeline_mode=pl.Buffered(1)) if grid[0] == 1 and nk == 1 else {}
    b_mode = dict(pipeline_mode=pl.Buffered(1)) if grid[1] == 1 and nk == 1 else {}
    if n_a == 1:
        a_specs = [pl.BlockSpec((tk, tm), lambda i, j, k: (k, i), **a_mode) if ta
                   else pl.BlockSpec((tm, tk), lambda i, j, k: (i, k), **a_mode)]
    elif ta:
        a_specs = [pl.BlockSpec((tk, tm), lambda i, j, k, s=s // tm, w=w // tm: (0, jnp.clip(i - s, 0, w - 1)))
                   for s, w in zip(starts, widths)]
    else:
        a_specs = [pl.BlockSpec((tm, w), lambda i, j, k: (i, 0)) for w in widths]
    b_spec = (pl.BlockSpec((tn, tk), lambda i, j, k: (j + col0, k), **b_mode) if tb
              else pl.BlockSpec((tk, tn), lambda i, j, k: (k, j + col0), **b_mode))
    o_spec = pl.BlockSpec((tm, tn), lambda i, j, k: (i, j))
    dims = ((((0,) if ta else (1,)), ((1,) if tb else (0,))), ((), ()))
    has_add = add is not None
    if epilogue is None:
        ep_fn, ep_rows, ep_params, ep_outs, ep_accs = None, [], [], [], []
        out_sds, out_specs = [out_type((m, n), out_dtype)], [o_spec]
    else:
        ep_fn, ep_rows, ep_params, ep_outs, ep_accs = epilogue
        assert grid[1] == 1 or not ep_accs, name
        out_sds = [out_type((m, w * grid[1]), d) for w, d in ep_outs]
        out_sds += [jax.ShapeDtypeStruct(sh, F32) for sh in ep_accs]
        out_specs = [pl.BlockSpec((tm, w), lambda i, j, k: (i, j)) for w, _ in ep_outs]
        out_specs += [pl.BlockSpec(sh, lambda i, j, k: (0, 0)) for sh in ep_accs]
    n_main = len(out_sds)
    use_scratch = nk > 1 and (ep_fn is not None or jnp.dtype(out_dtype) != jnp.dtype(F32))
    n_in = n_a + 1 + has_add + len(ep_rows) + len(ep_params)

    def finish(total, refs, pid):
        outs = refs[n_in + n_car:n_in + n_car + n_main]
        if ep_fn is None:
            outs[0][...] = total.astype(out_dtype)
            return
        first = n_a + 1 + has_add
        rows = [r[...].astype(F32) for r in refs[first:first + len(ep_rows)]]
        params = [p[...] for p in refs[first + len(ep_rows):n_in]]
        res = ep_fn(total, *rows, *params)
        for o_ref, v in zip(outs[:len(ep_outs)], res):
            o_ref[...] = v.astype(o_ref.dtype)
        acc_refs = outs[len(ep_outs):]
        if acc_refs:
            @pl.when(pid[0] == 0)
            def _():
                for r in acc_refs:
                    r[...] = jnp.zeros(r.shape, F32)

            for r, v in zip(acc_refs, res[len(ep_outs):]):
                r[...] += v

    def compute(refs, pid):
        a_ref, b_ref = refs[0], refs[n_a]
        add_ref = refs[n_a + 1] if has_add else None

        def dot(x, y):
            return lax.dot_general(x.astype(BF16), y.astype(BF16), dims, preferred_element_type=F32)

        if n_a > 1 and ta:
            for p_ref, s, w in zip(refs[:n_a], starts, widths):
                @pl.when((pid[0] >= s // tm) & (pid[0] < (s + w) // tm))
                def _(p_ref=p_ref):
                    prod = dot(p_ref[...], b_ref[...])
                    finish(prod + add_ref[...] if has_add else prod, refs, pid)
            return
        if n_a > 1:
            prod = dot(a_ref[...], b_ref[:widths[0], :])
            for p_ref, s, w in zip(refs[1:n_a], starts[1:], widths[1:]):
                prod += dot(p_ref[...], b_ref[s:s + w, :])
        else:
            prod = dot(a_ref[...], b_ref[...])
        if nk == 1:
            finish(prod + add_ref[...] if has_add else prod, refs, pid)
            return
        acc_ref = refs[n_in + n_car + n_main + len(carry.out_shapes)] if use_scratch else refs[n_in + n_car]
        k = pid[2]

        @pl.when(k == 0)
        def _():
            acc_ref[...] = prod + add_ref[...] if has_add else prod

        @pl.when(k > 0)
        def _():
            acc_ref[...] += prod

        if use_scratch:
            @pl.when(k == nk - 1)
            def _():
                finish(acc_ref[...], refs, pid)

    def body(*refs):
        pid = (pl.program_id(0), pl.program_id(1), pl.program_id(2))
        step = (pid[0] * grid[1] + pid[1]) * nk + pid[2]
        carry.run(refs, n_in, n_main, step, grid[0] * grid[1] * nk, lambda: compute(refs, pid))

    tile_bytes = ((n_a if ta else 1) * _nbytes((tm, tk), pieces[0].dtype)
                  + _nbytes((tk, tn), b.dtype) + 2 * _nbytes((tm, tn), F32)
                  + (_nbytes((tm, tn), F32) if has_add else 0)
                  + sum(_nbytes((tm, r.shape[1]), r.dtype) for r in ep_rows) + sum(_nbytes((tm, w), d) for w, d in ep_outs))
    in_specs = a_specs + [b_spec] + ([o_spec] if has_add else [])
    in_specs += [pl.BlockSpec((tm, r.shape[1] // grid[1]), lambda i, j, k: (i, j)) for r in ep_rows]
    in_specs += [pl.BlockSpec(p.shape, lambda i, j, k: (0, 0)) for p in ep_params]
    args = tuple(pieces) + (b,) + ((add,) if has_add else ()) + tuple(ep_rows) + tuple(ep_params)
    scratch = [pltpu.VMEM((tm, tn), F32)] if use_scratch else []
    serial = bool(carry) or bool(ep_accs)
    c_in, c_out, c_shapes, c_alias, c_sems = carry.call_args(n_in, n_main)
    res = pl.pallas_call(
        body, name=name, grid=grid,
        in_specs=in_specs + c_in, out_specs=out_specs + c_out, out_shape=out_sds + c_shapes,
        input_output_aliases=c_alias, scratch_shapes=scratch + c_sems,
        compiler_params=pltpu.CompilerParams(
            dimension_semantics=("arbitrary",) * 3 if serial else ("parallel", "parallel", "arbitrary"),
            vmem_limit_bytes=_vmem_limit(tile_bytes)),
    )(*args, *carry.arrays)
    main = res[0] if ep_fn is None else list(res[:n_main])
    return (main, res[n_main:]) if carry else main


def _rowwise(fn, rows, params, outs, accs, *, tm, name, carry=None, out_type=jax.ShapeDtypeStruct):
    carry = carry or _Carry()
    t = rows[0].shape[0]
    assert t % tm == 0, (name, t, tm)
    n_r, n_p, n_o, n_c = len(rows), len(params), len(outs), len(carry.arrays)

    def compute(refs, first):
        vals = [r[...].astype(F32) for r in refs[:n_r]] + [p[...] for p in refs[n_r:n_r + n_p]]
        res = fn(*vals)
        o_refs = refs[n_r + n_p + n_c:n_r + n_p + n_c + n_o]
        a_refs = refs[n_r + n_p + n_c + n_o:n_r + n_p + n_c + n_o + len(accs)]
        for o_ref, v in zip(o_refs, res[:n_o]):
            o_ref[...] = v.astype(o_ref.dtype)
        if accs:
            @pl.when(first)
            def _():
                for a_ref in a_refs:
                    a_ref[...] = jnp.zeros(a_ref.shape, F32)

            for a_ref, v in zip(a_refs, res[n_o:]):
                a_ref[...] += v

    def body(*refs):
        step = pl.program_id(0)
        carry.run(refs, n_r + n_p, n_o + len(accs), step, t // tm, lambda: compute(refs, step == 0))

    in_specs = [pl.BlockSpec((tm, r.shape[1]), lambda i: (i, 0)) for r in rows]
    in_specs += [pl.BlockSpec(p.shape, lambda i: (0, 0)) for p in params]
    out_specs = [pl.BlockSpec((tm, w), lambda i: (i, 0)) for w, _ in outs]
    out_specs += [pl.BlockSpec(s, lambda i: (0, 0)) for s in accs]
    out_shape = [out_type((t, w), d) for w, d in outs]
    out_shape += [jax.ShapeDtypeStruct(s, F32) for s in accs]
    tile_bytes = sum(_nbytes((tm, r.shape[1]), r.dtype) for r in rows) + sum(_nbytes((tm, w), F32) for w, _ in outs)
    c_in, c_out, c_shapes, c_alias, c_sems = carry.call_args(n_r + n_p, n_o + len(accs))
    res = pl.pallas_call(
        body, name=name, grid=(t // tm,), in_specs=in_specs + c_in, out_specs=out_specs + c_out,
        out_shape=out_shape + c_shapes, input_output_aliases=c_alias, scratch_shapes=c_sems,
        compiler_params=pltpu.CompilerParams(
            dimension_semantics=("arbitrary",) if accs or carry else ("parallel",),
            vmem_limit_bytes=_vmem_limit(2 * tile_bytes)),
    )(*rows, *params, *carry.arrays)
    own = n_o + len(accs)
    return (list(res[:own]) + [res[own:]]) if carry else res


def _rms_stats(x):
    r = lax.rsqrt(jnp.mean(x * x, axis=-1, keepdims=True) + RMS_EPS)
    return x * r, r


def _rms_bwd(dy, xhat, r, g):
    dxhat = dy * g
    dx = r * (dxhat - xhat * jnp.mean(dxhat * xhat, axis=-1, keepdims=True))
    return dx, dy * xhat


def _sb_consts():
    lane = lax.broadcasted_iota(jnp.int32, (BLK, LANES), 1)
    head0 = lane < HEAD_DIM
    row = lax.broadcasted_iota(jnp.int32, (2 * BLK, BLK), 0) % BLK
    col = lax.broadcasted_iota(jnp.int32, (2 * BLK, BLK), 1)
    causal = col < row
    jj = lax.broadcasted_iota(jnp.int32, (BLK, BLK), 0)
    ss = lax.broadcasted_iota(jnp.int32, (BLK, BLK), 1)
    suffix = jnp.where(jj > ss, 1.0, 0.0).astype(BF16)
    return head0, causal, suffix


def _stack_heads(x, head0):
    zero = jnp.zeros_like(x)
    return jnp.concatenate([jnp.where(head0, x, zero), jnp.where(head0, zero, x)], axis=0)


def _sb_logits(z, causal, masked):
    sp = jnp.log(1.0 + jnp.exp(-jnp.abs(z)))
    log_keep = -(jnp.maximum(z, 0.0) + sp)
    log_beta = jnp.minimum(z, 0.0) - sp
    if masked:
        log_keep = jnp.where(causal, log_keep, 0.0)
    return log_keep, log_beta


def _suffix_sums(x, suffix):
    hi, lo = _split2(x)
    after = _dot(hi, suffix) + _dot(lo, suffix)
    total = jnp.broadcast_to(after[:, 0:1] + x[:, 0:1], x.shape)
    return after, total


def _sb_walk_back(i, state, per_chain, tile):
    def alive(st):
        worst = functools.reduce(jnp.maximum, [st[p][:, 0:1] for p in range(0, len(st), per_chain)])
        return jnp.max(worst) > EXP_UNDERFLOW

    def cond(c):
        return jnp.logical_and(c[0] < i, alive(c[1]))

    def body(c):
        return c[0] + 1, tile(i - 1 - c[0], c[1], False)

    return lax.while_loop(cond, body, (jnp.int32(0), state))[1]


def _lane_blocks(x, n):
    return [x[:, p * LANES:(p + 1) * LANES] for p in range(n)]


def _sb_fwd(qkv, b_sz, s_len, carry):
    nq = s_len // BLK
    n_pairs = SB_WIDTH // LANES
    ch = SB_FWD_CHAINS
    n_steps = n_pairs // ch
    scale = 1.0 / math.sqrt(HEAD_DIM)

    def compute(q_ref, k_ref, v_ref, o_ref):
        head0, causal, suffix = _sb_consts()

        def q_block(i, _):
            qs = pl.multiple_of(i * BLK, BLK)
            q_all = (q_ref[pl.ds(qs, BLK), :] * scale).astype(BF16)
            q01 = [_stack_heads(q, head0) for q in _lane_blocks(q_all, ch)]

            def tile(j, state, masked):
                ks = pl.multiple_of(j * BLK, BLK)
                ks_ = _lane_blocks(k_ref[pl.ds(ks, BLK), :].astype(BF16), ch)
                vs_ = _lane_blocks(v_ref[pl.ds(ks, BLK), :].astype(BF16), ch)
                zs = [_dot_nt(q01[p], ks_[p]) for p in range(ch)]
                logits = [_sb_logits(z, causal, masked) for z in zs]
                sums = [_suffix_sums(lg[0], suffix) for lg in logits]
                out = []
                for p in range(ch):
                    carry, acc = state[2 * p], state[2 * p + 1]
                    after, total = sums[p]
                    a = jnp.exp(logits[p][1] + carry + after)
                    if masked:
                        a = jnp.where(causal, a, 0.0)
                    a_hi, a_lo = _split2(a)
                    a_cat = jnp.concatenate([a_hi[:BLK], a_hi[BLK:], a_lo[:BLK], a_lo[BLK:]], axis=1)
                    v01 = _stack_heads(vs_[p], head0)
                    out += [carry + total, acc + _dot(a_cat, jnp.concatenate([v01, v01], axis=0))]
                return tuple(out)

            state = (jnp.zeros((2 * BLK, BLK), F32), jnp.zeros((BLK, LANES), F32)) * ch
            state = tile(i, state, True)
            state = _sb_walk_back(i, state, 2, tile)
            o_ref[pl.ds(qs, BLK), :] = jnp.concatenate([state[2 * p + 1] for p in range(ch)], axis=1)
            return 0

        lax.fori_loop(0, nq, q_block, 0)

    def body(*refs):
        step = pl.program_id(0) * n_steps + pl.program_id(1)
        o_ref = refs[3 + len(carry.arrays)]
        carry.run(refs, 3, 1, step, b_sz * n_steps, lambda: compute(refs[0], refs[1], refs[2], o_ref))

    blk = lambda off: pl.BlockSpec((None, s_len, ch * LANES), lambda b, p: (b, 0, off + p))
    c_in, c_out, c_shapes, c_alias, c_sems = carry.call_args(3, 1)
    res = pl.pallas_call(
        body, name="sb_fwd", grid=(b_sz, n_steps),
        in_specs=[blk(0), blk(n_steps), blk(2 * n_steps)] + c_in, out_specs=[blk(0)] + c_out,
        out_shape=[jax.ShapeDtypeStruct((b_sz, s_len, SB_WIDTH), F32)] + c_shapes,
        input_output_aliases=c_alias, scratch_shapes=c_sems,
        compiler_params=pltpu.CompilerParams(dimension_semantics=("arbitrary", "arbitrary"),
                                             vmem_limit_bytes=VMEM_CAP),
    )(qkv, qkv, qkv, *carry.arrays)
    return res[0], res[1:]


def _sb_bwd(qkv, o_sb, do_sb, b_sz, s_len, carry):
    nq = s_len // BLK
    n_pairs = SB_WIDTH // LANES
    ch = SB_BWD_CHAINS
    n_steps = n_pairs // ch
    scale = 1.0 / math.sqrt(HEAD_DIM)

    def compute(q_ref, k_ref, v_ref, o_ref, do_ref, dq_ref, dk_ref, dv_ref, dk_acc, dv_acc):
        head0, causal, suffix = _sb_consts()
        lrow = lax.broadcasted_iota(jnp.int32, (LANES, LANES), 0)
        ones_h0 = jnp.where(lrow < HEAD_DIM, 1.0, 0.0).astype(BF16)
        ones_h1 = jnp.where(lrow >= HEAD_DIM, 1.0, 0.0).astype(BF16)
        dk_acc[...] = jnp.zeros(dk_acc.shape, F32)
        dv_acc[...] = jnp.zeros(dv_acc.shape, F32)

        def q_block(i, _):
            qs = pl.multiple_of(i * BLK, BLK)
            q_all = (q_ref[pl.ds(qs, BLK), :] * scale).astype(BF16)
            do_all = do_ref[pl.ds(qs, BLK), :].astype(BF16)
            dd_all = do_all.astype(F32) * o_ref[pl.ds(qs, BLK), :]
            q01 = [_stack_heads(q, head0) for q in _lane_blocks(q_all, ch)]
            do01 = [_stack_heads(d, head0) for d in _lane_blocks(do_all, ch)]
            tot = []
            for dd in _lane_blocks(dd_all, ch):
                dd_hi, dd_lo = _split2(dd)
                tot.append(jnp.concatenate([_dot(dd_hi, ones_h0) + _dot(dd_lo, ones_h0),
                                            _dot(dd_hi, ones_h1) + _dot(dd_lo, ones_h1)], axis=0))

            def tile(j, state, masked):
                ks = pl.multiple_of(j * BLK, BLK)
                ks_ = _lane_blocks(k_ref[pl.ds(ks, BLK), :].astype(BF16), ch)
                vs_ = _lane_blocks(v_ref[pl.ds(ks, BLK), :].astype(BF16), ch)
                zs = [_dot_nt(q01[p], ks_[p]) for p in range(ch)]
                das = [_dot_nt(do01[p], vs_[p]) for p in range(ch)]
                logits = [_sb_logits(z, causal, masked) for z in zs]
                sums = [_suffix_sums(lg[0], suffix) for lg in logits]
                a_s, e_s = [], []
                for p in range(ch):
                    a = jnp.exp(logits[p][1] + state[3 * p] + sums[p][0])
                    if masked:
                        a = jnp.where(causal, a, 0.0)
                    a_s.append(a)
                    e_s.append(a * das[p])
                e_sums = [_suffix_sums(e, suffix) for e in e_s]
                out, dks, dvs = [], [], []
                for p in range(ch):
                    carry, rcarry, dq = state[3 * p:3 * p + 3]
                    e = e_s[p]
                    before = tot[p] - (rcarry + e_sums[p][0] + e)
                    beta = jnp.exp(logits[p][1])
                    dz = e * (1.0 - beta) - beta * before
                    if masked:
                        dz = jnp.where(causal, dz, 0.0)
                    dz_b = dz.astype(BF16)
                    dks.append(_dot_tn(dz_b, q01[p]))
                    dvs.append(_dot_tn(a_s[p].astype(BF16), do01[p]))
                    out += [carry + sums[p][1], rcarry + e_sums[p][1], dq + _dot(dz_b, ks_[p])]
                dk_acc[pl.ds(ks, BLK), :] += jnp.concatenate(dks, axis=1)
                dv_acc[pl.ds(ks, BLK), :] += jnp.concatenate(dvs, axis=1)
                return tuple(out)

            state = (jnp.zeros((2 * BLK, BLK), F32),) * (3 * ch)
            state = tile(i, state, True)
            state = _sb_walk_back(i, state, 3, tile)
            dq = [jnp.where(head0, state[3 * p + 2][:BLK], state[3 * p + 2][BLK:]) for p in range(ch)]
            dq_ref[pl.ds(qs, BLK), :] = (jnp.concatenate(dq, axis=1) * scale).astype(dq_ref.dtype)
            return 0

        lax.fori_loop(0, nq, q_block, 0)
        dk_ref[...] = dk_acc[...].astype(dk_ref.dtype)
        dv_ref[...] = dv_acc[...].astype(dv_ref.dtype)

    def body(*refs):
        step = pl.program_id(0) * n_steps + pl.program_id(1)
        n_c, n_o = len(carry.arrays), len(carry.out_shapes)
        own = refs[:5] + refs[5 + n_c:8 + n_c] + refs[8 + n_c + n_o:10 + n_c + n_o]
        carry.run(refs, 5, 3, step, b_sz * n_steps, lambda: compute(*own))

    blk = lambda off: pl.BlockSpec((None, s_len, ch * LANES), lambda b, p: (b, 0, off + p))
    once = lambda off: pl.BlockSpec((None, s_len, ch * LANES), lambda b, p: (b, 0, off + p),
                                    pipeline_mode=pl.Buffered(1))
    out_sd = jax.ShapeDtypeStruct((b_sz, s_len, SB_WIDTH), BF16)
    c_in, c_out, c_shapes, c_alias, c_sems = carry.call_args(5, 3)
    res = pl.pallas_call(
        body, name="sb_bwd", grid=(b_sz, n_steps),
        in_specs=[once(0), once(n_steps), once(2 * n_steps), once(0), once(0)] + c_in,
        out_specs=[blk(0), blk(0), blk(0)] + c_out, out_shape=[out_sd, out_sd, out_sd] + c_shapes,
        input_output_aliases=c_alias,
        scratch_shapes=[pltpu.VMEM((s_len, ch * LANES), F32), pltpu.VMEM((s_len, ch * LANES), F32)] + c_sems,
        compiler_params=pltpu.CompilerParams(dimension_semantics=("arbitrary", "arbitrary"),
                                             vmem_limit_bytes=VMEM_CAP),
    )(qkv, qkv, qkv, o_sb, do_sb, *carry.arrays)
    return res[:3], res[3:]


def _dil_consts(group, pair_idx, dilation):
    lane = lax.broadcasted_iota(jnp.int32, (BLK, LANES), 1)
    head0 = lane < HEAD_DIM
    row = lax.broadcasted_iota(jnp.int32, (2 * BLK, BLK), 0)
    qa = row % BLK
    kb = lax.broadcasted_iota(jnp.int32, (2 * BLK, BLK), 1)
    head = (group * DIL_HEADS_PER_GROUP + 2 * pair_idx + row // BLK).astype(F32)
    slope = jnp.exp((-ALIBI_MAX_BIAS * math.log(2.0) / DIL_HEADS) * (head + 1.0))
    valid_cur = kb <= qa
    valid_prev = kb >= qa
    bias_cur = -slope * ((qa - kb) * dilation).astype(F32)
    bias_prev = -slope * ((BLK + qa - kb) * dilation).astype(F32)
    return head0, valid_cur, valid_prev, bias_cur, bias_prev


def _dil_units(s_len, dilation):
    nb = s_len // dilation // BLK
    return [(r, n) for r in range(dilation) for n in range(nb)]


def _dil_rows(n, r, dilation):
    if dilation == 1:
        return pl.ds(n * BLK, BLK)
    return pl.ds(n * BLK * dilation + r, BLK, stride=dilation)


def _dil_scores(q01, k, bias, valid):
    s = _dot_nt(q01, k) * (1.0 / math.sqrt(HEAD_DIM)) + bias
    return jnp.where(valid, s, NEG)


def _dil_fwd(qkv, b_sz, s_len, carry):
    n_pairs = DIL_OUT_WIDTH // LANES
    q_off = 3 * SB_WIDTH // LANES
    per_kind = DIL_WIDTH // LANES

    def compute(pair_idx, qkv_refs, o_ref, lse_ref, m_s, l_s):
        m_s[...] = jnp.full(m_s.shape, NEG, F32)
        l_s[...] = jnp.zeros(l_s.shape, F32)
        o_ref[...] = jnp.zeros(o_ref.shape, F32)
        for g, (_, dilation) in enumerate(DIL_PAIRS):
            q_ref, k_ref, v_ref = qkv_refs[3 * g:3 * g + 3]
            head0, valid_cur, valid_prev, bias_cur, bias_prev = _dil_consts(g, pair_idx, dilation)
            units = _dil_units(s_len, dilation)
            for u0 in range(0, len(units), DIL_CHAINS):
                group = units[u0:u0 + DIL_CHAINS]
                rows_of = [_dil_rows(n, r, dilation) for r, n in group]
                scores, values = [], []
                for (r, n), rows in zip(group, rows_of):
                    q01 = _stack_heads(q_ref[rows, :].astype(BF16), head0)
                    sc = [_dil_scores(q01, k_ref[rows, :].astype(BF16), bias_cur, valid_cur)]
                    vals = [_stack_heads(v_ref[rows, :].astype(BF16), head0)]
                    if n > 0:
                        prev = _dil_rows(n - 1, r, dilation)
                        sc.append(_dil_scores(q01, k_ref[prev, :].astype(BF16), bias_prev, valid_prev))
                        vals.append(_stack_heads(v_ref[prev, :].astype(BF16), head0))
                    scores.append(sc)
                    values.append(vals)
                stats = []
                for sc, rows in zip(scores, rows_of):
                    m_blk = functools.reduce(jnp.maximum, [jnp.max(x, axis=-1, keepdims=True) for x in sc])
                    m_old = jnp.concatenate([m_s.at[0][rows, :], m_s.at[1][rows, :]], axis=0)
                    l_old = jnp.concatenate([l_s.at[0][rows, :], l_s.at[1][rows, :]], axis=0)
                    m_new = jnp.maximum(m_old, m_blk)
                    probs = [jnp.exp(x - m_new) for x in sc]
                    l_blk = functools.reduce(jnp.add, [jnp.sum(p, axis=-1, keepdims=True) for p in probs])
                    alpha = jnp.exp(m_old - m_new)
                    stats.append((m_new, alpha * l_old + l_blk, alpha, probs))
                for (m_new, l_new, alpha, probs), vals, rows in zip(stats, values, rows_of):
                    alpha_tok = jnp.where(head0, alpha[:BLK], alpha[BLK:])
                    p_cat = jnp.concatenate(
                        [h for p in probs for h in (p[:BLK].astype(BF16), p[BLK:].astype(BF16))], axis=1)
                    o_ref[rows, :] = alpha_tok * o_ref[rows, :] + _dot(p_cat, jnp.concatenate(vals, axis=0))
                    m_s.at[0][rows, :] = m_new[:BLK]
                    m_s.at[1][rows, :] = m_new[BLK:]
                    l_s.at[0][rows, :] = l_new[:BLK]
                    l_s.at[1][rows, :] = l_new[BLK:]
        lane = lax.broadcasted_iota(jnp.int32, (BLK, LANES), 1)
        for c in range(s_len // BLK):
            rows = pl.ds(c * BLK, BLK)
            l0, l1 = l_s.at[0][rows, :], l_s.at[1][rows, :]
            o_ref[rows, :] = o_ref[rows, :] / jnp.where(lane < HEAD_DIM, l0, l1)
            lse_ref.at[0][rows, :] = m_s.at[0][rows, :] + jnp.log(l0)
            lse_ref.at[1][rows, :] = m_s.at[1][rows, :] + jnp.log(l1)

    def body(*refs):
        pair_idx = pl.program_id(1)
        step = pl.program_id(0) * n_pairs + pair_idx
        n_c, n_o = len(carry.arrays), len(carry.out_shapes)
        o_ref, lse_ref = refs[9 + n_c:11 + n_c]
        m_s, l_s = refs[11 + n_c + n_o:13 + n_c + n_o]
        carry.run(refs, 9, 2, step, b_sz * n_pairs, lambda: compute(pair_idx, refs[:9], o_ref, lse_ref, m_s, l_s))

    in_specs = []
    for g in range(len(DIL_PAIRS)):
        for kind in range(3):
            off = q_off + kind * per_kind + g * n_pairs
            in_specs.append(pl.BlockSpec((None, s_len, LANES), lambda b, p, off=off: (b, 0, off + p)))
    c_in, c_out, c_shapes, c_alias, c_sems = carry.call_args(9, 2)
    res = pl.pallas_call(
        body, name="dil_fwd", grid=(b_sz, n_pairs),
        in_specs=in_specs + c_in,
        out_specs=[pl.BlockSpec((None, s_len, LANES), lambda b, p: (b, 0, p)),
                   pl.BlockSpec((None, None, 2, s_len, LANES), lambda b, p: (b, p, 0, 0, 0))] + c_out,
        out_shape=[jax.ShapeDtypeStruct((b_sz, s_len, DIL_OUT_WIDTH), F32),
                   jax.ShapeDtypeStruct((b_sz, n_pairs, 2, s_len, LANES), F32)] + c_shapes,
        input_output_aliases=c_alias,
        scratch_shapes=[pltpu.VMEM((2, s_len, LANES), F32), pltpu.VMEM((2, s_len, LANES), F32)] + c_sems,
        compiler_params=pltpu.CompilerParams(dimension_semantics=("arbitrary", "arbitrary"),
                                             vmem_limit_bytes=VMEM_CAP),
    )(*([qkv] * 9), *carry.arrays)
    return res[0], res[1], res[2:]


def _dil_bwd(qkv, o_dl, lse, do_dl, b_sz, s_len, carry):
    n_pairs = DIL_OUT_WIDTH // LANES
    n_groups = len(DIL_PAIRS)
    q_off = 3 * SB_WIDTH // LANES
    per_kind = DIL_WIDTH // LANES

    def compute(pair_idx, group, q_ref, k_ref, v_ref, o_ref, lse_ref, do_ref, dq_ref, dk_ref, dv_ref, d_s, dq_s, dk_s, dv_s):
        lrow = lax.broadcasted_iota(jnp.int32, (LANES, LANES), 0)
        ones_h0 = jnp.where(lrow < HEAD_DIM, 1.0, 0.0).astype(BF16)
        ones_h1 = jnp.where(lrow >= HEAD_DIM, 1.0, 0.0).astype(BF16)
        for c in range(s_len // BLK):
            rows = pl.ds(c * BLK, BLK)
            dd_hi, dd_lo = _split2(do_ref[rows, :] * o_ref[rows, :])
            d_s.at[0][rows, :] = _dot(dd_hi, ones_h0) + _dot(dd_lo, ones_h0)
            d_s.at[1][rows, :] = _dot(dd_hi, ones_h1) + _dot(dd_lo, ones_h1)
        dk_s[...] = jnp.zeros(dk_s.shape, F32)
        dv_s[...] = jnp.zeros(dv_s.shape, F32)

        def one_group(g, dilation):
            head0, valid_cur, valid_prev, bias_cur, bias_prev = _dil_consts(g, pair_idx, dilation)
            units = _dil_units(s_len, dilation)
            scale = 1.0 / math.sqrt(HEAD_DIM)
            for u0 in range(0, len(units), DIL_CHAINS):
                chunk = units[u0:u0 + DIL_CHAINS]
                loaded = []
                for r, n in chunk:
                    rows = _dil_rows(n, r, dilation)
                    q01 = _stack_heads(q_ref[rows, :].astype(BF16), head0)
                    do01 = _stack_heads(do_ref[rows, :].astype(BF16), head0)
                    lse01 = jnp.concatenate([lse_ref.at[0][rows, :], lse_ref.at[1][rows, :]], axis=0)
                    d01 = jnp.concatenate([d_s.at[0][rows, :], d_s.at[1][rows, :]], axis=0)
                    blocks = [(rows, bias_cur, valid_cur)]
                    if n > 0:
                        blocks.append((_dil_rows(n - 1, r, dilation), bias_prev, valid_prev))
                    parts = []
                    for krows, bias, valid in blocks:
                        k = k_ref[krows, :].astype(BF16)
                        v = v_ref[krows, :].astype(BF16)
                        parts.append((krows, k, _dil_scores(q01, k, bias, valid), _dot_nt(do01, v)))
                    loaded.append((rows, q01, do01, lse01, d01, parts))
                grads = []
                for rows, q01, do01, lse01, d01, parts in loaded:
                    for krows, k, sc, dp in parts:
                        p = jnp.exp(sc - lse01)
                        grads.append((p.astype(BF16), (p * (dp - d01) * scale).astype(BF16)))
                it = iter(grads)
                updates = []
                for rows, q01, do01, lse01, d01, parts in loaded:
                    dq = jnp.zeros((2 * BLK, LANES), F32)
                    for krows, k, sc, dp in parts:
                        p_b, ds = next(it)
                        dq = dq + _dot(ds, k)
                        updates.append((krows, _dot_tn(ds, q01), _dot_tn(p_b, do01)))
                    dq_s[rows, :] = jnp.where(head0, dq[:BLK], dq[BLK:])
                for krows, dk, dv in updates:
                    dk_s[krows, :] = dk_s[krows, :] + dk
                    dv_s[krows, :] = dv_s[krows, :] + dv

        for g, (_, dilation) in enumerate(DIL_PAIRS):
            pl.when(group == g)(functools.partial(one_group, g, dilation))
        dq_ref[...] = dq_s[...].astype(dq_ref.dtype)
        dk_ref[...] = dk_s[...].astype(dk_ref.dtype)
        dv_ref[...] = dv_s[...].astype(dv_ref.dtype)

    def body(*refs):
        pair_idx, group = pl.program_id(1), pl.program_id(2)
        step = (pl.program_id(0) * n_pairs + pair_idx) * n_groups + group
        n_c, n_o = len(carry.arrays), len(carry.out_shapes)
        own = refs[:6] + refs[6 + n_c:9 + n_c] + refs[9 + n_c + n_o:13 + n_c + n_o]
        carry.run(refs, 6, 3, step, b_sz * n_pairs * n_groups, lambda: compute(pair_idx, group, *own))

    def qkv_spec(kind):
        return pl.BlockSpec((None, s_len, LANES),
                            lambda b, p, g: (b, 0, q_off + kind * per_kind + g * n_pairs + p))

    tok_spec = pl.BlockSpec((None, s_len, LANES), lambda b, p, g: (b, 0, p))
    out_spec = pl.BlockSpec((None, s_len, LANES), lambda b, p, g: (b, 0, g * n_pairs + p))
    out_sd = jax.ShapeDtypeStruct((b_sz, s_len, DIL_WIDTH), BF16)
    c_in, c_out, c_shapes, c_alias, c_sems = carry.call_args(6, 3)
    res = pl.pallas_call(
        body, name="dil_bwd", grid=(b_sz, n_pairs, n_groups),
        in_specs=[qkv_spec(0), qkv_spec(1), qkv_spec(2), tok_spec,
                  pl.BlockSpec((None, None, 2, s_len, LANES), lambda b, p, g: (b, p, 0, 0, 0)), tok_spec] + c_in,
        out_specs=[out_spec, out_spec, out_spec] + c_out,
        out_shape=[out_sd, out_sd, out_sd] + c_shapes,
        input_output_aliases=c_alias,
        scratch_shapes=[pltpu.VMEM((2, s_len, LANES), F32)] + [pltpu.VMEM((s_len, LANES), F32)] * 3 + c_sems,
        compiler_params=pltpu.CompilerParams(dimension_semantics=("arbitrary", "arbitrary", "arbitrary"),
                                             vmem_limit_bytes=VMEM_CAP),
    )(qkv, qkv, qkv, o_dl, lse, do_dl, *carry.arrays)
    return res[:3], res[3:]


def _mesh_pos():
    return lax.axis_index("x"), lax.axis_index("y"), lax.axis_index("c")


def _other_chips(x, y):
    return [(1 - x, y), (x, 1 - y), (1 - x, 1 - y)]


def _hbm_specs(n):
    return [pl.BlockSpec(memory_space=pl.ANY)] * n


SWAPPED = ("w_ffn_in",)


def _slot(x, y, swapped):
    return 2 * y + x if swapped else 2 * x + y


def _cast_to_slab(w, name):
    rows, cols = w.shape
    mine = jnp.reshape(_slot(lax.axis_index("x"), lax.axis_index("y"), False), (1,)).astype(jnp.int32)

    def body(idx_ref, w_ref, o_ref):
        o_ref[...] = w_ref[...].astype(BF16)

    return pl.pallas_call(
        body, name=name,
        grid_spec=pltpu.PrefetchScalarGridSpec(
            num_scalar_prefetch=1, grid=(1,),
            in_specs=[pl.BlockSpec((rows, cols), lambda i, idx: (0, 0))],
            out_specs=pl.BlockSpec((None, rows, cols), lambda i, idx: (idx[0], 0, 0))),
        out_shape=_hbm_array((N_CHIPS, rows, cols), BF16),
        compiler_params=pltpu.CompilerParams(vmem_limit_bytes=_vmem_limit(rows * cols * 6)),
    )(mine, w)


def _gather_issue(slabs, send_sems, recv_sems, swapped):
    x, y, c = _mesh_pos()
    for k, slab in enumerate(slabs):
        half = slab.shape[1] // 2
        rows = slab.at[_slot(x, y, swapped[k]), pl.ds(c * half, half), :]
        for r, (px, py) in enumerate(_other_chips(x, y)):
            pltpu.make_async_remote_copy(
                src_ref=rows, dst_ref=rows, send_sem=send_sems.at[6 * k + r], recv_sem=recv_sems.at[6 * k + r],
                device_id=(px, py, c), device_id_type=MESH).start()


def _gather_complete(slabs, send_sems, recv_sems, swapped):
    x, y, c = _mesh_pos()
    chips = _other_chips(x, y)

    def copy(k, sem, block, rows, to):
        ref = slabs[k].at[block, rows, :]
        return pltpu.make_async_remote_copy(
            src_ref=ref, dst_ref=ref, send_sem=send_sems.at[sem], recv_sem=recv_sems.at[sem],
            device_id=to, device_id_type=MESH)

    for k, slab in enumerate(slabs):
        half = slab.shape[1] // 2
        for r, (px, py) in enumerate(chips):
            theirs = _slot(px, py, swapped[k])
            copy(k, 6 * k + r, theirs, pl.ds(c * half, half), (px, py, c)).wait_recv()
            copy(k, 6 * k + 3 + r, theirs, pl.ds(c * half, half), (x, y, 1 - c)).start()
    for k, slab in enumerate(slabs):
        half = slab.shape[1] // 2
        for r, (px, py) in enumerate(chips):
            copy(k, 6 * k + 3 + r, _slot(px, py, swapped[k]), pl.ds((1 - c) * half, half), (x, y, 1 - c)).wait_recv()
    for k, slab in enumerate(slabs):
        half = slab.shape[1] // 2
        for r, (px, py) in enumerate(chips):
            copy(k, 6 * k + r, _slot(x, y, swapped[k]), pl.ds(c * half, half), (px, py, c)).wait_send()
            copy(k, 6 * k + 3 + r, _slot(px, py, swapped[k]), pl.ds(c * half, half), (x, y, 1 - c)).wait_send()


def _gather_sems(n):
    return [pltpu.SemaphoreType.DMA((6 * n,)), pltpu.SemaphoreType.DMA((6 * n,))]


def _gather_carry(slabs, names):
    swapped = [k in SWAPPED for k in names]
    return _Carry(slabs, [_hbm_array(a.shape, a.dtype) for a in slabs], True, _gather_sems(len(slabs)),
                  lambda ins, outs, sems: _gather_issue(outs, *sems, swapped),
                  lambda ins, outs, sems: _gather_complete(outs, *sems, swapped))


def _cast_carry(shards, names):
    swapped = [k in SWAPPED for k in names]

    def start(ins, outs, sems):
        x, y, _ = _mesh_pos()
        for w, slab, sw in zip(ins, outs, swapped):
            def cast(f32_buf, bf16_buf, sem, w=w, slab=slab, sw=sw):
                load = pltpu.make_async_copy(w, f32_buf, sem)
                load.start()
                load.wait()
                bf16_buf[...] = f32_buf[...].astype(BF16)
                store = pltpu.make_async_copy(bf16_buf, slab.at[_slot(x, y, sw)], sem)
                store.start()
                store.wait()

            pl.run_scoped(cast, pltpu.VMEM(w.shape, F32), pltpu.VMEM(w.shape, BF16), pltpu.SemaphoreType.DMA)

    return _Carry(shards, [_hbm_array((N_CHIPS,) + w.shape, BF16) for w in shards], False, [], start,
                  lambda ins, outs, sems: None)


def _pair_copies(ins, outs, send_sems, recv_sems):
    x, y, c = _mesh_pos()
    copies = []
    for k, g in enumerate(ins):
        half = g.shape[1] // 2
        copies.append(pltpu.make_async_remote_copy(
            src_ref=g.at[:, pl.ds((1 - c) * half, half), :], dst_ref=outs[k],
            send_sem=send_sems.at[k], recv_sem=recv_sems.at[k],
            device_id=(x, y, 1 - c), device_id_type=MESH))
    return copies


def _pair_carry(grads):
    n = len(grads)

    def start(ins, outs, sems):
        for cp in _pair_copies(ins, outs, *sems):
            cp.start()

    def finish(ins, outs, sems):
        for cp in _pair_copies(ins, outs, *sems):
            cp.wait()

    return _Carry(grads, [jax.ShapeDtypeStruct((N_CHIPS, g.shape[1] // 2, g.shape[2]), g.dtype) for g in grads], False,
                  [pltpu.SemaphoreType.DMA((n,)), pltpu.SemaphoreType.DMA((n,))], start, finish)


def _pair_exchange(grads, tag):
    carry = _pair_carry(grads)
    n = len(grads)

    def body(*refs):
        carry.start(refs[:n], refs[n:2 * n], refs[2 * n:])
        carry.finish(refs[:n], refs[n:2 * n], refs[2 * n:])

    return pl.pallas_call(
        body, name="grad_pair_exchange_" + tag, in_specs=_hbm_specs(n), out_specs=_hbm_specs(n),
        out_shape=carry.out_shapes, scratch_shapes=carry.sems,
    )(*grads)


def _pair_sum(grad, other, name, swapped):
    _, rows, cols = grad.shape
    half = rows // 2
    x, y, c = _mesh_pos()
    idx = jnp.stack([c, _slot(x, y, swapped)]).astype(jnp.int32)

    def body(idx_ref, g_ref, p_ref, own_ref, sb_ref):
        s = g_ref[...] + p_ref[...].astype(F32)
        sb_ref[...] = s.astype(BF16)

        @pl.when(pl.program_id(0) == idx_ref[1])
        def _():
            own_ref[...] = s

    blk = pl.BlockSpec((None, half, cols), lambda p, idx: (p, 0, 0))
    return pl.pallas_call(
        body, name=name,
        grid_spec=pltpu.PrefetchScalarGridSpec(
            num_scalar_prefetch=1, grid=(N_CHIPS,),
            in_specs=[pl.BlockSpec((None, half, cols), lambda p, idx: (p, idx[0], 0)), blk],
            out_specs=[pl.BlockSpec((half, cols), lambda p, idx: (0, 0)), blk]),
        out_shape=[jax.ShapeDtypeStruct((half, cols), F32), jax.ShapeDtypeStruct((N_CHIPS, half, cols), BF16)],
        compiler_params=pltpu.CompilerParams(dimension_semantics=("arbitrary",),
                                             vmem_limit_bytes=_vmem_limit(4 * half * cols * 4)),
    )(idx, grad, other)


def _chip_copies(sums_bf16, lands, send_sems, recv_sems, swapped):
    x, y, c = _mesh_pos()
    return [pltpu.make_async_remote_copy(
        src_ref=sums_bf16[k].at[_slot(px, py, swapped[k])], dst_ref=lands[k].at[r],
        send_sem=send_sems.at[3 * k + r], recv_sem=recv_sems.at[3 * k + r],
        device_id=(px, py, c), device_id_type=MESH)
        for k in range(len(sums_bf16)) for r, (px, py) in enumerate(_other_chips(x, y))]


def _chip_carry(sums_bf16, names):
    swapped = [k in SWAPPED for k in names]

    def start(ins, outs, sems):
        for cp in _chip_copies(ins, outs, *sems, swapped):
            cp.start()

    def finish(ins, outs, sems):
        for cp in _chip_copies(ins, outs, *sems, swapped):
            cp.wait()

    return _Carry(sums_bf16, _chip_landing(sums_bf16), False, _chip_sems(len(sums_bf16)), start, finish)


def _chip_sems(n):
    return [pltpu.SemaphoreType.DMA((3 * n,)), pltpu.SemaphoreType.DMA((3 * n,))]


def _chip_landing(sums_bf16):
    return [jax.ShapeDtypeStruct((N_CHIPS - 1,) + s.shape[1:], BF16) for s in sums_bf16]


def _chip_sum(own, landed, name):
    rows, cols = own.shape
    core = jnp.reshape(lax.axis_index("c"), (1,)).astype(jnp.int32)

    def body(core_ref, o_ref, l_ref, out_ref):
        out_ref[...] = ((o_ref[...] + l_ref[0].astype(F32)) + l_ref[1].astype(F32)) + l_ref[2].astype(F32)

    return pl.pallas_call(
        body, name=name,
        grid_spec=pltpu.PrefetchScalarGridSpec(
            num_scalar_prefetch=1, grid=(1,),
            in_specs=[pl.BlockSpec((rows, cols), lambda i, core_ref: (0, 0)),
                      pl.BlockSpec((N_CHIPS - 1, rows, cols), lambda i, core_ref: (0, 0, 0))],
            out_specs=pl.BlockSpec((rows, cols), lambda i, core_ref: (core_ref[0], 0))),
        out_shape=jax.ShapeDtypeStruct((2 * rows, cols), F32),
        compiler_params=pltpu.CompilerParams(vmem_limit_bytes=_vmem_limit(3 * rows * cols * 4)),
    )(core, own, landed)


def _halves_carry(fulls):
    n = len(fulls)

    def copies(outs, send_sems, recv_sems, own):
        x, y, c = _mesh_pos()
        res = []
        for k, out in enumerate(outs):
            half = out.shape[0] // 2
            rows = out.at[pl.ds((c if own else 1 - c) * half, half), :]
            res.append(pltpu.make_async_remote_copy(
                src_ref=rows, dst_ref=rows, send_sem=send_sems.at[k], recv_sem=recv_sems.at[k],
                device_id=(x, y, 1 - c), device_id_type=MESH))
        return res

    def start(ins, outs, sems):
        for cp in copies(outs, *sems, True):
            cp.start()

    def finish(ins, outs, sems):
        for cp in copies(outs, *sems, False):
            cp.wait_recv()
        for cp in copies(outs, *sems, True):
            cp.wait_send()

    return _Carry(fulls, [jax.ShapeDtypeStruct(f.shape, F32) for f in fulls], True,
                  [pltpu.SemaphoreType.DMA((n,)), pltpu.SemaphoreType.DMA((n,))], start, finish)


def _final_exchange(fulls, v):
    n = len(fulls)
    rows, cols = v.shape
    n_dev = 8

    def body(*refs):
        v_ref, out_ref = refs[0], refs[1 + 2 * n]
        outs = refs[1 + n:1 + 2 * n]
        buf, v_send, v_recv, h_send, h_recv = refs[2 + 2 * n:]
        x, y, c = _mesh_pos()
        me = 4 * x + 2 * y + c
        buf[me] = v_ref[...]
        peers = [(1 - x if r & 4 else x, 1 - y if r & 2 else y, 1 - c if r & 1 else c) for r in range(1, n_dev)]
        copies = []
        for r, peer in enumerate(peers):
            copies.append(pltpu.make_async_remote_copy(
                src_ref=v_ref, dst_ref=buf.at[me], send_sem=v_send.at[r], recv_sem=v_recv.at[r],
                device_id=peer, device_id_type=MESH))
        for k in range(n):
            half = fulls[k].shape[0] // 2
            mine = outs[k].at[pl.ds(c * half, half), :]
            copies.append(pltpu.make_async_remote_copy(
                src_ref=mine, dst_ref=mine, send_sem=h_send.at[k], recv_sem=h_recv.at[k],
                device_id=(x, y, 1 - c), device_id_type=MESH))
        for cp in copies:
            cp.start()
        for r, (px, py, pc) in enumerate(peers):
            pltpu.make_async_remote_copy(
                src_ref=v_ref, dst_ref=buf.at[4 * px + 2 * py + pc], send_sem=v_send.at[r], recv_sem=v_recv.at[r],
                device_id=(px, py, pc), device_id_type=MESH).wait_recv()
        for k in range(n):
            half = fulls[k].shape[0] // 2
            theirs = outs[k].at[pl.ds((1 - c) * half, half), :]
            pltpu.make_async_remote_copy(
                src_ref=theirs, dst_ref=theirs, send_sem=h_send.at[k], recv_sem=h_recv.at[k],
                device_id=(x, y, 1 - c), device_id_type=MESH).wait_recv()
        for cp in copies:
            cp.wait_send()
        acc = buf[0]
        for d in range(1, n_dev):
            acc = acc + buf[d]
        out_ref[...] = acc
        out_ref[3:4, :] = jnp.broadcast_to(jnp.sum(acc[3:4, :], axis=1, keepdims=True), (1, cols))

    vm = pl.BlockSpec(memory_space=pltpu.VMEM)
    res = pl.pallas_call(
        body, name="final_exchange",
        in_specs=[vm] + _hbm_specs(n), out_specs=_hbm_specs(n) + [vm],
        out_shape=[jax.ShapeDtypeStruct(f.shape, F32) for f in fulls] + [jax.ShapeDtypeStruct((rows, cols), F32)],
        input_output_aliases={1 + k: k for k in range(n)},
        scratch_shapes=[pltpu.VMEM((n_dev, rows, cols), F32),
                        pltpu.SemaphoreType.DMA((n_dev - 1,)), pltpu.SemaphoreType.DMA((n_dev - 1,)),
                        pltpu.SemaphoreType.DMA((n,)), pltpu.SemaphoreType.DMA((n,))],
    )(v, *fulls)
    return res[:n], res[n]


def _adamw_math(w, g, m, v):
    m = ADAM_B1 * m + (1.0 - ADAM_B1) * g
    v = ADAM_B2 * v + (1.0 - ADAM_B2) * (g * g)
    m_hat = m / (1.0 - ADAM_B1 ** ADAM_STEP)
    v_hat = v / (1.0 - ADAM_B2 ** ADAM_STEP)
    delta = -ADAM_LR * (m_hat / (jnp.sqrt(v_hat) + ADAM_EPS) + ADAM_WD * w)
    return delta, m, v


def _adamw(w, g, m, v, name):
    rows, cols = w.shape
    tm = rows // 2 if (rows // 2) % 8 == 0 else rows
    return _rowwise(_adamw_math, [w, g, m, v], [], [(cols, F32)] * 3, [], tm=tm, name=name)


def _unshard_cols(gathered):
    n, r, c = gathered.shape
    return jnp.transpose(gathered, (1, 0, 2)).reshape(r, n * c)


def _shard_cols(full):
    r, nc = full.shape
    return jnp.transpose(full.reshape(r, N_CHIPS, nc // N_CHIPS), (1, 0, 2))


LATE = ["w_sb_up", "w_dil_up", "w_out", "w_ffn_in", "w_ffn_out"]


def _late_weights(slabs, d_model, d_ff):
    g = dict(zip(LATE, slabs))
    return (_unshard_cols(g["w_sb_up"]), _unshard_cols(g["w_dil_up"]), g["w_out"].reshape(d_model, d_model),
            _unshard_cols(g["w_ffn_in"]), g["w_ffn_out"].reshape(d_ff, d_model))


ROW_SHARDED = ("w_in", "w_out", "w_ffn_in", "w_ffn_out")


def _chip_major(grads):
    out = []
    for k, g in grads.items():
        if k in ROW_SHARDED:
            out.append(g.reshape(N_CHIPS, g.shape[0] // N_CHIPS, g.shape[1]))
        else:
            out.append(_shard_cols(g))
    return out


def _pair_sums(full, others, names):
    return [_pair_sum(g, o, "grad_pair_sum_" + k, k in SWAPPED) for g, o, k in zip(full, others, names)]


def _chip_sums(pair, landed, names):
    return {k: _chip_sum(p[0], l, "grad_chip_sum_" + k) for p, l, k in zip(pair, landed, names)}


def _fwd_bwd(x, loss_target, g_mix, g_ffn, g_fin, slab_in, late_shards):
    b_sz, s_len, d_model = x.shape
    t = b_sz * s_len
    d_ff = late_shards[-1].shape[0] * N_CHIPS
    x2d = x.reshape(t, d_model)
    tgt2d = loss_target.reshape(t, d_model)

    u, (slab_in, *late_slabs) = _rowwise(
        lambda xv, g: (_rms_stats(xv)[0] * g,), [_in_hbm(x2d)], [g_mix], [(d_model, BF16)], [], tm=512, name="norm_mix",
        carry=_gather_carry([slab_in], ["w_in"]) + _cast_carry(late_shards, LATE), out_type=_hbm_array)
    u, wt_in = _in_hbm(u), _in_hbm(slab_in.reshape(-1, d_model))
    qkv, (slab_ffn_out,) = _mm(u, wt_in, tb=True, b_cols=(0, QKV_WIDTH), tm=2048, tn=768, tk=d_model, name="proj_qkv",
                               carry=_gather_carry(late_slabs[4:], LATE[4:]))
    gates = _mm(u, wt_in, tb=True, b_cols=(QKV_WIDTH, 2 * d_model), out_dtype=BF16, tm=t, tn=256, tk=d_model,
                name="proj_gates")
    qkv3 = qkv.reshape(b_sz, s_len, QKV_WIDTH)
    o_sb, (slab_ffn_in,) = _sb_fwd(qkv3, b_sz, s_len, _gather_carry(late_slabs[3:4], LATE[3:4]))
    o_dl, lse, small_slabs = _dil_fwd(qkv3, b_sz, s_len, _gather_carry(late_slabs[:3], LATE[:3]))
    wf_sb_up, wf_dil_up, wf_out, wf_ffn_in, wf_ffn_out = _late_weights(
        list(small_slabs) + [slab_ffn_in, slab_ffn_out], d_model, d_ff)
    o_sb2, o_dl2 = o_sb.reshape(t, SB_WIDTH), o_dl.reshape(t, DIL_OUT_WIDTH)
    y_sb = _mm(o_sb2, wf_sb_up, out_dtype=BF16, tm=1024, tn=1024, tk=SB_WIDTH, name="sb_up", out_type=_hbm_array)
    y_dl = _mm(o_dl2, wf_dil_up, out_dtype=BF16, tm=1024, tn=1024, tk=DIL_OUT_WIDTH, name="dil_up", out_type=_hbm_array)

    def merge_fn(gt, ys, yd):
        return (_sigmoid(gt[:, :d_model]) * ys + _sigmoid(gt[:, d_model:]) * yd,)

    (merged,) = _rowwise(merge_fn, [gates, y_sb, y_dl], [], [(d_model, BF16)], [], tm=512, name="merge")
    x1 = _mm(merged, wf_out, add=x2d, tm=512, tn=1024, tk=d_model, name="mix_out")
    (u2,) = _rowwise(lambda xv, g: (_rms_stats(xv)[0] * g,), [_in_hbm(x1)], [g_ffn], [(d_model, BF16)], [], tm=512, name="norm_ffn",
                     out_type=_hbm_array)
    u2 = _in_hbm(u2)
    half_ff = d_ff // 2

    def act_fn(hv):
        gate = hv[:, :half_ff]
        return hv, gate * _sigmoid(gate) * hv[:, half_ff:]

    h, act = _mm(u2, wf_ffn_in, tm=512, tn=d_ff, tk=d_model, name="ffn_in",
                 epilogue=(act_fn, [], [], [(d_ff, BF16), (half_ff, BF16)], []))
    def head_fn(xv, tg, g):
        xhat, r = _rms_stats(xv)
        err = xhat * g - tg
        dy = err * (1.0 / d_model)
        dx, dg_rows = _rms_bwd(dy, xhat, r, g)
        loss_lanes = (0.5 / d_model) * jnp.sum(err * err, axis=0, keepdims=True)
        return dx, dx, jnp.sum(dg_rows, axis=0, keepdims=True), loss_lanes

    dx2, dx2_b, dg_fin, loss_lanes = _mm(
        act, wf_ffn_out, add=x1, tm=512, tn=1024, tk=d_ff, name="ffn_out",
        epilogue=(head_fn, [tgt2d], [g_fin], [(d_model, F32), (d_model, BF16)], [(1, d_model), (1, d_model)]))

    def dact_fn(da, hv):
        gate, up = hv[:, :half_ff], hv[:, half_ff:]
        sg = _sigmoid(gate)
        dgate = da * up * (sg * (1.0 + gate * (1.0 - sg)))
        return (jnp.concatenate([dgate, da * (gate * sg)], axis=1),)

    dx2_b = _in_hbm(dx2_b)
    (dh,) = _mm(dx2_b, wf_ffn_out, tb=True, tm=512, tn=half_ff, tk=d_model, name="ffn_out_dx",
                epilogue=(dact_fn, [h], [], [(d_ff, BF16)], []))
    gw_ffn_out = _mm(act, dx2_b, ta=True, tm=256, tn=d_model, tk=t, name="ffn_out_dw")
    def norm_bwd_fn(du_, dres, xv, g):
        xhat, r = _rms_stats(xv)
        dx, dg_rows = _rms_bwd(du_, xhat, r, g)
        return dres + dx, jnp.sum(dg_rows, axis=0, keepdims=True)

    def norm_bwd_twice(*args):
        dx, dg = norm_bwd_fn(*args)
        return dx, dx, dg

    dx1, dx1_b, dg_ffn = _mm(dh, wf_ffn_in, tb=True, tm=512, tn=1024, tk=2 * d_ff, name="ffn_in_dx",
                             epilogue=(norm_bwd_twice, [dx2, x1], [g_ffn], [(d_model, F32), (d_model, BF16)], [(1, d_model)]))
    gwt_ffn_in = _mm(dh, u2, ta=True, tm=512, tn=d_model, tk=t, name="ffn_in_dw")

    dx1_b = _in_hbm(dx1_b)
    dmerged = _mm(dx1_b, wf_out, tb=True, out_dtype=BF16, tm=512, tn=1024, tk=d_model, name="mix_out_dx",
                  out_type=_hbm_array)
    gw_out = _mm(merged, dx1_b, ta=True, tm=256, tn=d_model, tk=t, name="mix_out_dw")

    def merge_bwd_fn(gt, ys, yd, dm):
        s_sb, s_dl = _sigmoid(gt[:, :d_model]), _sigmoid(gt[:, d_model:])
        dgates = jnp.concatenate([dm * ys * s_sb * (1.0 - s_sb), dm * yd * s_dl * (1.0 - s_dl)], axis=1)
        return dgates, dm * s_sb, dm * s_dl

    full_big = _chip_major({"w_out": gw_out, "w_ffn_in": gwt_ffn_in})
    dgates, dy_sb, dy_dl, others_big = _rowwise(
        merge_bwd_fn, [gates, y_sb, y_dl, dmerged], [], [(2 * d_model, BF16), (d_model, BF16), (d_model, BF16)], [],
        tm=256, name="merge_bwd", carry=_pair_carry(full_big))
    pair_big = _pair_sums(full_big, others_big, LATE[2:4])
    do_sb = _mm(dy_sb, wf_sb_up, tb=True, out_dtype=BF16, tm=1024, tn=SB_WIDTH, tk=d_model, name="sb_up_dx")
    gw_sb_up = _mm(o_sb2, dy_sb, ta=True, tm=SB_WIDTH, tn=1024, tk=512, name="sb_up_dw")
    do_dl = _mm(dy_dl, wf_dil_up, tb=True, tm=1024, tn=DIL_OUT_WIDTH, tk=d_model, name="dil_up_dx")
    gw_dil_up = _mm(o_dl2, dy_dl, ta=True, tm=DIL_OUT_WIDTH, tn=1024, tk=512, name="dil_up_dw")
    rest = [LATE[0], LATE[1], LATE[4]]
    full_rest = _chip_major({"w_sb_up": gw_sb_up, "w_dil_up": gw_dil_up, "w_ffn_out": gw_ffn_out})
    (dq_sb, dk_sb, dv_sb), brought = _sb_bwd(
        qkv3, o_sb, do_sb.reshape(b_sz, s_len, SB_WIDTH), b_sz, s_len,
        _chip_carry([p[1] for p in pair_big], LATE[2:4]) + _pair_carry(full_rest))
    pair_rest = _pair_sums(full_rest, brought[2:], rest)
    (dq_dl, dk_dl, dv_dl), landed_b = _dil_bwd(
        qkv3, o_dl, lse, do_dl.reshape(b_sz, s_len, DIL_OUT_WIDTH), b_sz, s_len,
        _chip_carry([p[1] for p in pair_rest], rest))
    pair = pair_rest[:2] + pair_big + pair_rest[2:]
    landed = [landed_b[0], landed_b[1], brought[0], brought[1], landed_b[2]]
    dproj = [a.reshape(t, -1) for a in (dq_sb, dk_sb, dv_sb, dq_dl, dk_dl, dv_dl)] + [dgates]
    gwt_in, gwt_in_b = _mm(dproj, u, ta=True, tm=256, tn=d_model, tk=t, name="proj_dw",
                           epilogue=(lambda tile: (tile, tile), [], [], [(d_model, F32), (d_model, BF16)], []))
    full_in = _chip_major({"w_in": gwt_in})
    pair_in = _pair_sums(full_in, _pair_exchange(_chip_major({"w_in": gwt_in_b}), "w_in"), ["w_in"])
    late_halves = _chip_sums(pair, landed, LATE)
    (dx, dg_mix), brought_in = _mm(
        dproj, wt_in, tm=512, tn=1024, tk=wt_in.shape[0], name="proj_dx",
        carry=_halves_carry([late_halves[k] for k in LATE]) + _chip_carry([p[1] for p in pair_in], ["w_in"]),
        epilogue=(norm_bwd_fn, [dx1, x2d], [g_mix], [(d_model, F32)], [(1, d_model)]))

    grads = dict(zip(LATE, brought_in[:len(LATE)]))
    grads.update(_chip_sums(pair_in, brought_in[len(LATE):], ["w_in"]))
    return dx, grads, dg_mix, dg_ffn, dg_fin, loss_lanes


def kernel(x, norm_mix_g, w_in, w_sb_up, w_dil_up, w_out, norm_ffn_g, w_ffn_in, w_ffn_out, norm_final_g, loss_target, m_norm_mix_g, m_w_in, m_w_sb_up, m_w_dil_up, m_w_out, m_norm_ffn_g, m_w_ffn_in, m_w_ffn_out, m_norm_final_g, v_norm_mix_g, v_w_in, v_w_sb_up, v_w_dil_up, v_w_out, v_norm_ffn_g, v_w_ffn_in, v_w_ffn_out, v_norm_final_g):
    b_sz, s_len, d_model = x.shape
    d_ff = w_ffn_out.shape[1] * N_CHIPS
    g_mix, g_ffn, g_fin = norm_mix_g, norm_ffn_g, norm_final_g.reshape(1, d_model)

    names = ["w_in", "w_sb_up", "w_dil_up", "w_out", "w_ffn_in", "w_ffn_out"]
    shards = {"w_in": jnp.swapaxes(w_in[0], 0, 1), "w_sb_up": w_sb_up[0], "w_dil_up": w_dil_up[0], "w_out": w_out[0],
              "w_ffn_in": w_ffn_in[0], "w_ffn_out": w_ffn_out[0]}
    slab_in = _cast_to_slab(shards["w_in"], "cast_w_in")

    dx, grads, dg_mix, dg_ffn, dg_fin, loss_lanes = _fwd_bwd(
        x, loss_target, g_mix, g_ffn, g_fin, slab_in, [shards[k] for k in LATE])

    small = jnp.concatenate([dg_mix, dg_ffn, dg_fin, loss_lanes, jnp.zeros((4, d_model), F32)], axis=0)
    (grads["w_in"],), small = _final_exchange([grads["w_in"]], small)
    grads["w_ffn_in"] = jnp.swapaxes(grads["w_ffn_in"], 0, 1)
    loss = small[3, 0]
    gains = jnp.concatenate([g_mix, g_ffn, g_fin, jnp.zeros((5, d_model), F32)], axis=0)
    gains_m = jnp.concatenate([m_norm_mix_g, m_norm_ffn_g, m_norm_final_g.reshape(1, d_model), jnp.zeros((5, d_model), F32)], axis=0)
    gains_v = jnp.concatenate([v_norm_mix_g, v_norm_ffn_g, v_norm_final_g.reshape(1, d_model), jnp.ones((5, d_model), F32)], axis=0)
    gd, gm, gv = _rowwise(_adamw_math, [gains, small, gains_m, gains_v], [], [(d_model, F32)] * 3, [], tm=8, name="adamw_gains")

    moments = {"w_in": (jnp.swapaxes(m_w_in[0], 0, 1), jnp.swapaxes(v_w_in[0], 0, 1)),
               "w_sb_up": (m_w_sb_up[0], v_w_sb_up[0]), "w_dil_up": (m_w_dil_up[0], v_w_dil_up[0]),
               "w_out": (m_w_out[0], v_w_out[0]), "w_ffn_in": (m_w_ffn_in[0], v_w_ffn_in[0]),
               "w_ffn_out": (m_w_ffn_out[0], v_w_ffn_out[0])}
    upd = {k: _adamw(shards[k], grads[k], moments[k][0], moments[k][1], "adamw_" + k) for k in names}

    def as_output(k, a):
        return (jnp.swapaxes(a, 0, 1) if k == "w_in" else a)[None]

    def w_out_of(i):
        return [as_output(k, upd[k][i]) for k in names]

    def ordered(mix, ws, ffn_g, fin):
        return [mix, ws[0], ws[1], ws[2], ws[3], ffn_g, ws[4], ws[5], fin]

    grad_ws = [as_output(k, grads[k]) for k in names]
    outs = [loss, dx.reshape(b_sz, s_len, d_model)]
    outs += ordered(small[0:1], grad_ws, small[1:2], small[2])
    outs += ordered(gd[0:1], w_out_of(0), gd[1:2], gd[2])
    outs += ordered(gm[0:1], w_out_of(1), gm[1:2], gm[2])
    outs += ordered(gv[0:1], w_out_of(2), gv[1:2], gv[2])
    return tuple(outs)
```

```python
import functools
import math

import jax
import jax.numpy as jnp
from jax import lax
from jax.experimental import pallas as pl
from jax.experimental.pallas import tpu as pltpu

F32 = jnp.float32
BF16 = jnp.bfloat16
MESH = pl.DeviceIdType.MESH

HEAD_DIM = 64
SB_HEADS = 8
DIL_PAIRS = ((128, 1), (512, 4), (2048, 16))
DIL_HEADS_PER_GROUP = 4
DIL_HEADS = DIL_HEADS_PER_GROUP * len(DIL_PAIRS)
SB_WIDTH = SB_HEADS * HEAD_DIM
DIL_WIDTH = DIL_HEADS * HEAD_DIM
DIL_OUT_WIDTH = DIL_HEADS_PER_GROUP * HEAD_DIM
QKV_WIDTH = 3 * SB_WIDTH + 3 * DIL_WIDTH
RMS_EPS = 1e-6
ALIBI_MAX_BIAS = 8.0
ADAM_LR = 0.001
ADAM_B1 = 0.9
ADAM_B2 = 0.999
ADAM_EPS = 1e-08
ADAM_WD = 0.01
ADAM_STEP = 10

LANES = 128
BLK = 128
NEG = -1e30
EXP_UNDERFLOW = -104.0
SB_FWD_CHAINS = 4
SB_BWD_CHAINS = 4
DIL_CHAINS = 8
N_CHIPS = 4
VMEM_CAP = 56 * 1024 * 1024


def _vmem_limit(tile_bytes):
    return int(min(VMEM_CAP, max(32 * 1024 * 1024, 3 * tile_bytes + 8 * 1024 * 1024)))


def _hbm_array(shape, dtype):
    return pltpu.HBM(shape, dtype)


def _nbytes(shape, dtype):
    return math.prod(shape) * jnp.dtype(dtype).itemsize


def _in_hbm(x):
    return pltpu.with_memory_space_constraint(x, pltpu.HBM)


def _dot(a, b):
    return jnp.dot(a, b, preferred_element_type=F32)


def _dot_nt(a, b):
    return lax.dot_general(a, b, (((1,), (1,)), ((), ())), preferred_element_type=F32)


def _dot_tn(a, b):
    return lax.dot_general(a, b, (((0,), (0,)), ((), ())), preferred_element_type=F32)


def _split2(x):
    hi = x.astype(BF16)
    lo = (x - hi.astype(F32)).astype(BF16)
    return hi, lo


def _sigmoid(x):
    return pl.reciprocal(1.0 + jnp.exp(-x), approx=True)


class _Carry:
    def __init__(self, arrays=(), out_shapes=(), aliased=False, sems=(), start=None, finish=None):
        self.arrays, self.out_shapes = list(arrays), list(out_shapes)
        self.n_aliased = len(self.arrays) if aliased is True else int(aliased)
        self.sems, self.start, self.finish = list(sems), start, finish

    def __bool__(self):
        return bool(self.arrays)

    def __add__(self, other):
        assert not other.n_aliased and self.n_aliased in (0, len(self.out_shapes))
        n_a, n_o, n_s = len(self.arrays), len(self.out_shapes), len(self.sems)
        return _Carry(
            self.arrays + other.arrays, self.out_shapes + other.out_shapes, self.n_aliased, self.sems + other.sems,
            lambda i, o, s: (self.start(i[:n_a], o[:n_o], s[:n_s]), other.start(i[n_a:], o[n_o:], s[n_s:])),
            lambda i, o, s: (self.finish(i[:n_a], o[:n_o], s[:n_s]), other.finish(i[n_a:], o[n_o:], s[n_s:])))

    def call_args(self, n_in, n_out):
        aliases = {n_in + k: n_out + k for k in range(self.n_aliased)}
        return _hbm_specs(len(self.arrays)), _hbm_specs(len(self.out_shapes)), self.out_shapes, aliases, self.sems

    def run(self, refs, n_in, n_out, step, n_steps, compute):
        if not self:
            compute()
            return
        n_c, n_o, n_s = len(self.arrays), len(self.out_shapes), len(self.sems)
        ins = refs[n_in:n_in + n_c]
        outs = refs[n_in + n_c + n_out:n_in + n_c + n_out + n_o]
        sems = refs[len(refs) - n_s:]

        @pl.when(step == 0)
        def _():
            self.start(ins, outs, sems)

        compute()

        @pl.when(step == n_steps - 1)
        def _():
            self.finish(ins, outs, sems)


def _mm(a, b, *, ta=False, tb=False, add=None, out_dtype=F32, tm, tn, tk, name, carry=None, epilogue=None,
        b_cols=None, out_type=jax.ShapeDtypeStruct):
    carry = carry or _Carry()
    n_car = len(carry.arrays)
    pieces = list(a) if isinstance(a, (list, tuple)) else [a]
    n_a = len(pieces)
    widths = [p.shape[1] for p in pieces]
    starts = [sum(widths[:p]) for p in range(n_a)]
    if ta:
        kdim, m = pieces[0].shape[0], sum(widths)
    else:
        m, kdim = pieces[0].shape[0], sum(widths)
    if tb:
        n, k2 = b.shape
    else:
        k2, n = b.shape
    col0 = 0
    if b_cols is not None:
        assert b_cols[0] % tn == 0, name
        col0, n = b_cols[0] // tn, b_cols[1]
    assert kdim == k2 and m % tm == 0 and n % tn == 0 and kdim % tk == 0, (name, a.shape, b.shape)
    nk = kdim // tk
    assert n_a == 1 or (nk == 1 and not tb and (not ta or all(w % tm == 0 for w in widths))), name
    grid = (m // tm, n // tn, nk)
    a_mode = dict(pipeline_mode=pl.Buffered(1)) if grid[0] == 1 and nk == 1 else {}
    b_mode = dict(pipeline_mode=pl.Buffered(1)) if grid[1] == 1 and nk == 1 else {}
    if n_a == 1:
        a_specs = [pl.BlockSpec((tk, tm), lambda i, j, k: (k, i), **a_mode) if ta
                   else pl.BlockSpec((tm, tk), lambda i, j, k: (i, k), **a_mode)]
    elif ta:
        a_specs = [pl.BlockSpec((tk, tm), lambda i, j, k, s=s // tm, w=w // tm: (0, jnp.clip(i - s, 0, w - 1)))
                   for s, w in zip(starts, widths)]
    else:
        a_specs = [pl.BlockSpec((tm, w), lambda i, j, k: (i, 0)) for w in widths]
    b_spec = (pl.BlockSpec((tn, tk), lambda i, j, k: (j + col0, k), **b_mode) if tb
              else pl.BlockSpec((tk, tn), lambda i, j, k: (k, j + col0), **b_mode))
    o_spec = pl.BlockSpec((tm, tn), lambda i, j, k: (i, j))
    dims = ((((0,) if ta else (1,)), ((1,) if tb else (0,))), ((), ()))
    has_add = add is not None
    if epilogue is None:
        ep_fn, ep_rows, ep_params, ep_outs, ep_accs = None, [], [], [], []
        out_sds, out_specs = [out_type((m, n), out_dtype)], [o_spec]
    else:
        ep_fn, ep_rows, ep_params, ep_outs, ep_accs = epilogue
        assert grid[1] == 1 or not ep_accs, name
        out_sds = [out_type((m, w * grid[1]), d) for w, d in ep_outs]
        out_sds += [jax.ShapeDtypeStruct(sh, F32) for sh in ep_accs]
        out_specs = [pl.BlockSpec((tm, w), lambda i, j, k: (i, j)) for w, _ in ep_outs]
        out_specs += [pl.BlockSpec(sh, lambda i, j, k: (0, 0)) for sh in ep_accs]
    n_main = len(out_sds)
    use_scratch = nk > 1 and (ep_fn is not None or jnp.dtype(out_dtype) != jnp.dtype(F32))
    n_in = n_a + 1 + has_add + len(ep_rows) + len(ep_params)

    def finish(total, refs, pid):
        outs = refs[n_in + n_car:n_in + n_car + n_main]
        if ep_fn is None:
            outs[0][...] = total.astype(out_dtype)
            return
        first = n_a + 1 + has_add
        rows = [r[...].astype(F32) for r in refs[first:first + len(ep_rows)]]
        params = [p[...] for p in refs[first + len(ep_rows):n_in]]
        res = ep_fn(total, *rows, *params)
        for o_ref, v in zip(outs[:len(ep_outs)], res):
            o_ref[...] = v.astype(o_ref.dtype)
        acc_refs = outs[len(ep_outs):]
        if acc_refs:
            @pl.when(pid[0] == 0)
            def _():
                for r in acc_refs:
                    r[...] = jnp.zeros(r.shape, F32)

            for r, v in zip(acc_refs, res[len(ep_outs):]):
                r[...] += v

    def compute(refs, pid):
        a_ref, b_ref = refs[0], refs[n_a]
        add_ref = refs[n_a + 1] if has_add else None

        def dot(x, y):
            return lax.dot_general(x.astype(BF16), y.astype(BF16), dims, preferred_element_type=F32)

        if n_a > 1 and ta:
            for p_ref, s, w in zip(refs[:n_a], starts, widths):
                @pl.when((pid[0] >= s // tm) & (pid[0] < (s + w) // tm))
                def _(p_ref=p_ref):
                    prod = dot(p_ref[...], b_ref[...])
                    finish(prod + add_ref[...] if has_add else prod, refs, pid)
            return
        if n_a > 1:
            prod = dot(a_ref[...], b_ref[:widths[0], :])
            for p_ref, s, w in zip(refs[1:n_a], starts[1:], widths[1:]):
                prod += dot(p_ref[...], b_ref[s:s + w, :])
        else:
            prod = dot(a_ref[...], b_ref[...])
        if nk == 1:
            finish(prod + add_ref[...] if has_add else prod, refs, pid)
            return
        acc_ref = refs[n_in + n_car + n_main + len(carry.out_shapes)] if use_scratch else refs[n_in + n_car]
        k = pid[2]

        @pl.when(k == 0)
        def _():
            acc_ref[...] = prod + add_ref[...] if has_add else prod

        @pl.when(k > 0)
        def _():
            acc_ref[...] += prod

        if use_scratch:
            @pl.when(k == nk - 1)
            def _():
                finish(acc_ref[...], refs, pid)

    def body(*refs):
        pid = (pl.program_id(0), pl.program_id(1), pl.program_id(2))
        step = (pid[0] * grid[1] + pid[1]) * nk + pid[2]
        carry.run(refs, n_in, n_main, step, grid[0] * grid[1] * nk, lambda: compute(refs, pid))

    tile_bytes = ((n_a if ta else 1) * _nbytes((tm, tk), pieces[0].dtype)
                  + _nbytes((tk, tn), b.dtype) + 2 * _nbytes((tm, tn), F32)
                  + (_nbytes((tm, tn), F32) if has_add else 0)
                  + sum(_nbytes((tm, r.shape[1]), r.dtype) for r in ep_rows) + sum(_nbytes((tm, w), d) for w, d in ep_outs))
    in_specs = a_specs + [b_spec] + ([o_spec] if has_add else [])
    in_specs += [pl.BlockSpec((tm, r.shape[1] // grid[1]), lambda i, j, k: (i, j)) for r in ep_rows]
    in_specs += [pl.BlockSpec(p.shape, lambda i, j, k: (0, 0)) for p in ep_params]
    args = tuple(pieces) + (b,) + ((add,) if has_add else ()) + tuple(ep_rows) + tuple(ep_params)
    scratch = [pltpu.VMEM((tm, tn), F32)] if use_scratch else []
    serial = bool(carry) or bool(ep_accs)
    c_in, c_out, c_shapes, c_alias, c_sems = carry.call_args(n_in, n_main)
    res = pl.pallas_call(
        body, name=name, grid=grid,
        in_specs=in_specs + c_in, out_specs=out_specs + c_out, out_shape=out_sds + c_shapes,
        input_output_aliases=c_alias, scratch_shapes=scratch + c_sems,
        compiler_params=pltpu.CompilerParams(
            dimension_semantics=("arbitrary",) * 3 if serial else ("parallel", "parallel", "arbitrary"),
            vmem_limit_bytes=_vmem_limit(tile_bytes)),
    )(*args, *carry.arrays)
    main = res[0] if ep_fn is None else list(res[:n_main])
    return (main, res[n_main:]) if carry else main


def _rowwise(fn, rows, params, outs, accs, *, tm, name, carry=None, out_type=jax.ShapeDtypeStruct):
    carry = carry or _Carry()
    t = rows[0].shape[0]
    assert t % tm == 0, (name, t, tm)
    n_r, n_p, n_o, n_c = len(rows), len(params), len(outs), len(carry.arrays)

    def compute(refs, first):
        vals = [r[...].astype(F32) for r in refs[:n_r]] + [p[...] for p in refs[n_r:n_r + n_p]]
        res = fn(*vals)
        o_refs = refs[n_r + n_p + n_c:n_r + n_p + n_c + n_o]
        a_refs = refs[n_r + n_p + n_c + n_o:n_r + n_p + n_c + n_o + len(accs)]
        for o_ref, v in zip(o_refs, res[:n_o]):
            o_ref[...] = v.astype(o_ref.dtype)
        if accs:
            @pl.when(first)
            def _():
                for a_ref in a_refs:
                    a_ref[...] = jnp.zeros(a_ref.shape, F32)

            for a_ref, v in zip(a_refs, res[n_o:]):
                a_ref[...] += v

    def body(*refs):
        step = pl.program_id(0)
        carry.run(refs, n_r + n_p, n_o + len(accs), step, t // tm, lambda: compute(refs, step == 0))

    in_specs = [pl.BlockSpec((tm, r.shape[1]), lambda i: (i, 0)) for r in rows]
    in_specs += [pl.BlockSpec(p.shape, lambda i: (0, 0)) for p in params]
    out_specs = [pl.BlockSpec((tm, w), lambda i: (i, 0)) for w, _ in outs]
    out_specs += [pl.BlockSpec(s, lambda i: (0, 0)) for s in accs]
    out_shape = [out_type((t, w), d) for w, d in outs]
    out_shape += [jax.ShapeDtypeStruct(s, F32) for s in accs]
    tile_bytes = sum(_nbytes((tm, r.shape[1]), r.dtype) for r in rows) + sum(_nbytes((tm, w), F32) for w, _ in outs)
    c_in, c_out, c_shapes, c_alias, c_sems = carry.call_args(n_r + n_p, n_o + len(accs))
    res = pl.pallas_call(
        body, name=name, grid=(t // tm,), in_specs=in_specs + c_in, out_specs=out_specs + c_out,
        out_shape=out_shape + c_shapes, input_output_aliases=c_alias, scratch_shapes=c_sems,
        compiler_params=pltpu.CompilerParams(
            dimension_semantics=("arbitrary",) if accs or carry else ("parallel",),
            vmem_limit_bytes=_vmem_limit(2 * tile_bytes)),
    )(*rows, *params, *carry.arrays)
    own = n_o + len(accs)
    return (list(res[:own]) + [res[own:]]) if carry else res


def _rms_stats(x):
    r = lax.rsqrt(jnp.mean(x * x, axis=-1, keepdims=True) + RMS_EPS)
    return x * r, r


def _rms_bwd(dy, xhat, r, g):
    dxhat = dy * g
    dx = r * (dxhat - xhat * jnp.mean(dxhat * xhat, axis=-1, keepdims=True))
    return dx, dy * xhat


def _sb_consts():
    lane = lax.broadcasted_iota(jnp.int32, (BLK, LANES), 1)
    head0 = lane < HEAD_DIM
    row = lax.broadcasted_iota(jnp.int32, (2 * BLK, BLK), 0) % BLK
    col = lax.broadcasted_iota(jnp.int32, (2 * BLK, BLK), 1)
    causal = col < row
    jj = lax.broadcasted_iota(jnp.int32, (BLK, BLK), 0)
    ss = lax.broadcasted_iota(jnp.int32, (BLK, BLK), 1)
    suffix = jnp.where(jj > ss, 1.0, 0.0).astype(BF16)
    return head0, causal, suffix


def _stack_heads(x, head0):
    zero = jnp.zeros_like(x)
    return jnp.concatenate([jnp.where(head0, x, zero), jnp.where(head0, zero, x)], axis=0)


def _sb_logits(z, causal, masked):
    sp = jnp.log(1.0 + jnp.exp(-jnp.abs(z)))
    log_keep = -(jnp.maximum(z, 0.0) + sp)
    log_beta = jnp.minimum(z, 0.0) - sp
    if masked:
        log_keep = jnp.where(causal, log_keep, 0.0)
    return log_keep, log_beta


def _suffix_sums(x, suffix):
    hi, lo = _split2(x)
    after = _dot(hi, suffix) + _dot(lo, suffix)
    total = jnp.broadcast_to(after[:, 0:1] + x[:, 0:1], x.shape)
    return after, total


def _sb_walk_back(i, state, per_chain, tile):
    def alive(st):
        worst = functools.reduce(jnp.maximum, [st[p][:, 0:1] for p in range(0, len(st), per_chain)])
        return jnp.max(worst) > EXP_UNDERFLOW

    def cond(c):
        return jnp.logical_and(c[0] < i, alive(c[1]))

    def body(c):
        return c[0] + 1, tile(i - 1 - c[0], c[1], False)

    return lax.while_loop(cond, body, (jnp.int32(0), state))[1]


def _lane_blocks(x, n):
    return [x[:, p * LANES:(p + 1) * LANES] for p in range(n)]


def _sb_fwd(qkv, b_sz, s_len, carry):
    nq = s_len // BLK
    n_pairs = SB_WIDTH // LANES
    ch = SB_FWD_CHAINS
    n_steps = n_pairs // ch
    scale = 1.0 / math.sqrt(HEAD_DIM)

    def compute(q_ref, k_ref, v_ref, o_ref):
        head0, causal, suffix = _sb_consts()

        def q_block(i, _):
            qs = pl.multiple_of(i * BLK, BLK)
            q_all = (q_ref[pl.ds(qs, BLK), :] * scale).astype(BF16)
            q01 = [_stack_heads(q, head0) for q in _lane_blocks(q_all, ch)]

            def tile(j, state, masked):
                ks = pl.multiple_of(j * BLK, BLK)
                ks_ = _lane_blocks(k_ref[pl.ds(ks, BLK), :].astype(BF16), ch)
                vs_ = _lane_blocks(v_ref[pl.ds(ks, BLK), :].astype(BF16), ch)
                zs = [_dot_nt(q01[p], ks_[p]) for p in range(ch)]
                logits = [_sb_logits(z, causal, masked) for z in zs]
                sums = [_suffix_sums(lg[0], suffix) for lg in logits]
                out = []
                for p in range(ch):
                    carry, acc = state[2 * p], state[2 * p + 1]
                    after, total = sums[p]
                    a = jnp.exp(logits[p][1] + carry + after)
                    if masked:
                        a = jnp.where(causal, a, 0.0)
                    a_hi, a_lo = _split2(a)
                    a_cat = jnp.concatenate([a_hi[:BLK], a_hi[BLK:], a_lo[:BLK], a_lo[BLK:]], axis=1)
                    v01 = _stack_heads(vs_[p], head0)
                    out += [carry + total, acc + _dot(a_cat, jnp.concatenate([v01, v01], axis=0))]
                return tuple(out)

            state = (jnp.zeros((2 * BLK, BLK), F32), jnp.zeros((BLK, LANES), F32)) * ch
            state = tile(i, state, True)
            state = _sb_walk_back(i, state, 2, tile)
            o_ref[pl.ds(qs, BLK), :] = jnp.concatenate([state[2 * p + 1] for p in range(ch)], axis=1)
            return 0

        lax.fori_loop(0, nq, q_block, 0)

    def body(*refs):
        step = pl.program_id(0) * n_steps + pl.program_id(1)
        o_ref = refs[3 + len(carry.arrays)]
        carry.run(refs, 3, 1, step, b_sz * n_steps, lambda: compute(refs[0], refs[1], refs[2], o_ref))

    blk = lambda off: pl.BlockSpec((None, s_len, ch * LANES), lambda b, p: (b, 0, off + p))
    c_in, c_out, c_shapes, c_alias, c_sems = carry.call_args(3, 1)
    res = pl.pallas_call(
        body, name="sb_fwd", grid=(b_sz, n_steps),
        in_specs=[blk(0), blk(n_steps), blk(2 * n_steps)] + c_in, out_specs=[blk(0)] + c_out,
        out_shape=[jax.ShapeDtypeStruct((b_sz, s_len, SB_WIDTH), F32)] + c_shapes,
        input_output_aliases=c_alias, scratch_shapes=c_sems,
        compiler_params=pltpu.CompilerParams(dimension_semantics=("arbitrary", "arbitrary"),
                                             vmem_limit_bytes=VMEM_CAP),
    )(qkv, qkv, qkv, *carry.arrays)
    return res[0], res[1:]


def _sb_bwd(qkv, o_sb, do_sb, b_sz, s_len, carry):
    nq = s_len // BLK
    n_pairs = SB_WIDTH // LANES
    ch = SB_BWD_CHAINS
    n_steps = n_pairs // ch
    scale = 1.0 / math.sqrt(HEAD_DIM)

    def compute(q_ref, k_ref, v_ref, o_ref, do_ref, dq_ref, dk_ref, dv_ref, dk_acc, dv_acc):
        head0, causal, suffix = _sb_consts()
        lrow = lax.broadcasted_iota(jnp.int32, (LANES, LANES), 0)
        ones_h0 = jnp.where(lrow < HEAD_DIM, 1.0, 0.0).astype(BF16)
        ones_h1 = jnp.where(lrow >= HEAD_DIM, 1.0, 0.0).astype(BF16)
        dk_acc[...] = jnp.zeros(dk_acc.shape, F32)
        dv_acc[...] = jnp.zeros(dv_acc.shape, F32)

        def q_block(i, _):
            qs = pl.multiple_of(i * BLK, BLK)
            q_all = (q_ref[pl.ds(qs, BLK), :] * scale).astype(BF16)
            do_all = do_ref[pl.ds(qs, BLK), :].astype(BF16)
            dd_all = do_all.astype(F32) * o_ref[pl.ds(qs, BLK), :]
            q01 = [_stack_heads(q, head0) for q in _lane_blocks(q_all, ch)]
            do01 = [_stack_heads(d, head0) for d in _lane_blocks(do_all, ch)]
            tot = []
            for dd in _lane_blocks(dd_all, ch):
                dd_hi, dd_lo = _split2(dd)
                tot.append(jnp.concatenate([_dot(dd_hi, ones_h0) + _dot(dd_lo, ones_h0),
                                            _dot(dd_hi, ones_h1) + _dot(dd_lo, ones_h1)], axis=0))

            def tile(j, state, masked):
                ks = pl.multiple_of(j * BLK, BLK)
                ks_ = _lane_blocks(k_ref[pl.ds(ks, BLK), :].astype(BF16), ch)
                vs_ = _lane_blocks(v_ref[pl.ds(ks, BLK), :].astype(BF16), ch)
                zs = [_dot_nt(q01[p], ks_[p]) for p in range(ch)]
                das = [_dot_nt(do01[p], vs_[p]) for p in range(ch)]
                logits = [_sb_logits(z, causal, masked) for z in zs]
                sums = [_suffix_sums(lg[0], suffix) for lg in logits]
                a_s, e_s = [], []
                for p in range(ch):
                    a = jnp.exp(logits[p][1] + state[3 * p] + sums[p][0])
                    if masked:
                        a = jnp.where(causal, a, 0.0)
                    a_s.append(a)
                    e_s.append(a * das[p])
                e_sums = [_suffix_sums(e, suffix) for e in e_s]
                out, dks, dvs = [], [], []
                for p in range(ch):
                    carry, rcarry, dq = state[3 * p:3 * p + 3]
                    e = e_s[p]
                    before = tot[p] - (rcarry + e_sums[p][0] + e)
                    beta = jnp.exp(logits[p][1])
                    dz = e * (1.0 - beta) - beta * before
                    if masked:
                        dz = jnp.where(causal, dz, 0.0)
                    dz_b = dz.astype(BF16)
                    dks.append(_dot_tn(dz_b, q01[p]))
                    dvs.append(_dot_tn(a_s[p].astype(BF16), do01[p]))
                    out += [carry + sums[p][1], rcarry + e_sums[p][1], dq + _dot(dz_b, ks_[p])]
                dk_acc[pl.ds(ks, BLK), :] += jnp.concatenate(dks, axis=1)
                dv_acc[pl.ds(ks, BLK), :] += jnp.concatenate(dvs, axis=1)
                return tuple(out)

            state = (jnp.zeros((2 * BLK, BLK), F32),) * (3 * ch)
            state = tile(i, state, True)
            state = _sb_walk_back(i, state, 3, tile)
            dq = [jnp.where(head0, state[3 * p + 2][:BLK], state[3 * p + 2][BLK:]) for p in range(ch)]
            dq_ref[pl.ds(qs, BLK), :] = (jnp.concatenate(dq, axis=1) * scale).astype(dq_ref.dtype)
            return 0

        lax.fori_loop(0, nq, q_block, 0)
        dk_ref[...] = dk_acc[...].astype(dk_ref.dtype)
        dv_ref[...] = dv_acc[...].astype(dv_ref.dtype)

    def body(*refs):
        step = pl.program_id(0) * n_steps + pl.program_id(1)
        n_c, n_o = len(carry.arrays), len(carry.out_shapes)
        own = refs[:5] + refs[5 + n_c:8 + n_c] + refs[8 + n_c + n_o:10 + n_c + n_o]
        carry.run(refs, 5, 3, step, b_sz * n_steps, lambda: compute(*own))

    blk = lambda off: pl.BlockSpec((None, s_len, ch * LANES), lambda b, p: (b, 0, off + p))
    once = lambda off: pl.BlockSpec((None, s_len, ch * LANES), lambda b, p: (b, 0, off + p),
                                    pipeline_mode=pl.Buffered(1))
    out_sd = jax.ShapeDtypeStruct((b_sz, s_len, SB_WIDTH), BF16)
    c_in, c_out, c_shapes, c_alias, c_sems = carry.call_args(5, 3)
    res = pl.pallas_call(
        body, name="sb_bwd", grid=(b_sz, n_steps),
        in_specs=[blk(0), blk(n_steps), once(2 * n_steps), once(0), once(0)] + c_in,
        out_specs=[blk(0), blk(0), blk(0)] + c_out, out_shape=[out_sd, out_sd, out_sd] + c_shapes,
        input_output_aliases=c_alias,
        scratch_shapes=[pltpu.VMEM((s_len, ch * LANES), F32), pltpu.VMEM((s_len, ch * LANES), F32)] + c_sems,
        compiler_params=pltpu.CompilerParams(dimension_semantics=("arbitrary", "arbitrary"),
                                             vmem_limit_bytes=VMEM_CAP),
    )(qkv, qkv, qkv, o_sb, do_sb, *carry.arrays)
    return res[:3], res[3:]


def _dil_consts(group, pair_idx, dilation):
    lane = lax.broadcasted_iota(jnp.int32, (BLK, LANES), 1)
    head0 = lane < HEAD_DIM
    row = lax.broadcasted_iota(jnp.int32, (2 * BLK, BLK), 0)
    qa = row % BLK
    kb = lax.broadcasted_iota(jnp.int32, (2 * BLK, BLK), 1)
    head = (group * DIL_HEADS_PER_GROUP + 2 * pair_idx + row // BLK).astype(F32)
    slope = jnp.exp((-ALIBI_MAX_BIAS * math.log(2.0) / DIL_HEADS) * (head + 1.0))
    valid_cur = kb <= qa
    valid_prev = kb >= qa
    bias_cur = -slope * ((qa - kb) * dilation).astype(F32)
    bias_prev = -slope * ((BLK + qa - kb) * dilation).astype(F32)
    return head0, valid_cur, valid_prev, bias_cur, bias_prev


def _dil_units(s_len, dilation):
    nb = s_len // dilation // BLK
    return [(r, n) for r in range(dilation) for n in range(nb)]


def _dil_rows(n, r, dilation):
    if dilation == 1:
        return pl.ds(n * BLK, BLK)
    return pl.ds(n * BLK * dilation + r, BLK, stride=dilation)


def _dil_scores(q01, k, bias, valid):
    s = _dot_nt(q01, k) * (1.0 / math.sqrt(HEAD_DIM)) + bias
    return jnp.where(valid, s, NEG)


def _dil_fwd(qkv, b_sz, s_len, carry):
    n_pairs = DIL_OUT_WIDTH // LANES
    q_off = 3 * SB_WIDTH // LANES
    per_kind = DIL_WIDTH // LANES

    def compute(pair_idx, qkv_refs, o_ref, lse_ref, m_s, l_s):
        m_s[...] = jnp.full(m_s.shape, NEG, F32)
        l_s[...] = jnp.zeros(l_s.shape, F32)
        o_ref[...] = jnp.zeros(o_ref.shape, F32)
        for g, (_, dilation) in enumerate(DIL_PAIRS):
            q_ref, k_ref, v_ref = qkv_refs[3 * g:3 * g + 3]
            head0, valid_cur, valid_prev, bias_cur, bias_prev = _dil_consts(g, pair_idx, dilation)
            units = _dil_units(s_len, dilation)
            for u0 in range(0, len(units), DIL_CHAINS):
                group = units[u0:u0 + DIL_CHAINS]
                rows_of = [_dil_rows(n, r, dilation) for r, n in group]
                scores, values = [], []
                for (r, n), rows in zip(group, rows_of):
                    q01 = _stack_heads(q_ref[rows, :].astype(BF16), head0)
                    sc = [_dil_scores(q01, k_ref[rows, :].astype(BF16), bias_cur, valid_cur)]
                    vals = [_stack_heads(v_ref[rows, :].astype(BF16), head0)]
                    if n > 0:
                        prev = _dil_rows(n - 1, r, dilation)
                        sc.append(_dil_scores(q01, k_ref[prev, :].astype(BF16), bias_prev, valid_prev))
                        vals.append(_stack_heads(v_ref[prev, :].astype(BF16), head0))
                    scores.append(sc)
                    values.append(vals)
                stats = []
                for sc, rows in zip(scores, rows_of):
                    m_blk = functools.reduce(jnp.maximum, [jnp.max(x, axis=-1, keepdims=True) for x in sc])
                    m_old = jnp.concatenate([m_s.at[0][rows, :], m_s.at[1][rows, :]], axis=0)
                    l_old = jnp.concatenate([l_s.at[0][rows, :], l_s.at[1][rows, :]], axis=0)
                    m_new = jnp.maximum(m_old, m_blk)
                    probs = [jnp.exp(x - m_new) for x in sc]
                    l_blk = functools.reduce(jnp.add, [jnp.sum(p, axis=-1, keepdims=True) for p in probs])
                    alpha = jnp.exp(m_old - m_new)
                    stats.append((m_new, alpha * l_old + l_blk, alpha, probs))
                for (m_new, l_new, alpha, probs), vals, rows in zip(stats, values, rows_of):
                    alpha_tok = jnp.where(head0, alpha[:BLK], alpha[BLK:])
                    p_cat = jnp.concatenate(
                        [h for p in probs for h in (p[:BLK].astype(BF16), p[BLK:].astype(BF16))], axis=1)
                    o_ref[rows, :] = alpha_tok * o_ref[rows, :] + _dot(p_cat, jnp.concatenate(vals, axis=0))
                    m_s.at[0][rows, :] = m_new[:BLK]
                    m_s.at[1][rows, :] = m_new[BLK:]
                    l_s.at[0][rows, :] = l_new[:BLK]
                    l_s.at[1][rows, :] = l_new[BLK:]
        lane = lax.broadcasted_iota(jnp.int32, (BLK, LANES), 1)
        for c in range(s_len // BLK):
            rows = pl.ds(c * BLK, BLK)
            l0, l1 = l_s.at[0][rows, :], l_s.at[1][rows, :]
            o_ref[rows, :] = o_ref[rows, :] / jnp.where(lane < HEAD_DIM, l0, l1)
            lse_ref.at[0][rows, :] = m_s.at[0][rows, :] + jnp.log(l0)
            lse_ref.at[1][rows, :] = m_s.at[1][rows, :] + jnp.log(l1)

    def body(*refs):
        pair_idx = pl.program_id(1)
        step = pl.program_id(0) * n_pairs + pair_idx
        n_c, n_o = len(carry.arrays), len(carry.out_shapes)
        o_ref, lse_ref = refs[9 + n_c:11 + n_c]
        m_s, l_s = refs[11 + n_c + n_o:13 + n_c + n_o]
        carry.run(refs, 9, 2, step, b_sz * n_pairs, lambda: compute(pair_idx, refs[:9], o_ref, lse_ref, m_s, l_s))

    in_specs = []
    for g in range(len(DIL_PAIRS)):
        for kind in range(3):
            off = q_off + kind * per_kind + g * n_pairs
            in_specs.append(pl.BlockSpec((None, s_len, LANES), lambda b, p, off=off: (b, 0, off + p)))
    c_in, c_out, c_shapes, c_alias, c_sems = carry.call_args(9, 2)
    res = pl.pallas_call(
        body, name="dil_fwd", grid=(b_sz, n_pairs),
        in_specs=in_specs + c_in,
        out_specs=[pl.BlockSpec((None, s_len, LANES), lambda b, p: (b, 0, p)),
                   pl.BlockSpec((None, None, 2, s_len, LANES), lambda b, p: (b, p, 0, 0, 0))] + c_out,
        out_shape=[jax.ShapeDtypeStruct((b_sz, s_len, DIL_OUT_WIDTH), F32),
                   jax.ShapeDtypeStruct((b_sz, n_pairs, 2, s_len, LANES), F32)] + c_shapes,
        input_output_aliases=c_alias,
        scratch_shapes=[pltpu.VMEM((2, s_len, LANES), F32), pltpu.VMEM((2, s_len, LANES), F32)] + c_sems,
        compiler_params=pltpu.CompilerParams(dimension_semantics=("arbitrary", "arbitrary"),
                                             vmem_limit_bytes=VMEM_CAP),
    )(*([qkv] * 9), *carry.arrays)
    return res[0], res[1], res[2:]


def _dil_bwd(qkv, o_dl, lse, do_dl, b_sz, s_len, carry):
    n_pairs = DIL_OUT_WIDTH // LANES
    n_groups = len(DIL_PAIRS)
    q_off = 3 * SB_WIDTH // LANES
    per_kind = DIL_WIDTH // LANES

    def compute(pair_idx, group, q_ref, k_ref, v_ref, o_ref, lse_ref, do_ref, dq_ref, dk_ref, dv_ref, d_s, dq_s, dk_s, dv_s):
        lrow = lax.broadcasted_iota(jnp.int32, (LANES, LANES), 0)
        ones_h0 = jnp.where(lrow < HEAD_DIM, 1.0, 0.0).astype(BF16)
        ones_h1 = jnp.where(lrow >= HEAD_DIM, 1.0, 0.0).astype(BF16)
        for c in range(s_len // BLK):
            rows = pl.ds(c * BLK, BLK)
            dd_hi, dd_lo = _split2(do_ref[rows, :] * o_ref[rows, :])
            d_s.at[0][rows, :] = _dot(dd_hi, ones_h0) + _dot(dd_lo, ones_h0)
            d_s.at[1][rows, :] = _dot(dd_hi, ones_h1) + _dot(dd_lo, ones_h1)
        dk_s[...] = jnp.zeros(dk_s.shape, F32)
        dv_s[...] = jnp.zeros(dv_s.shape, F32)

        def one_group(g, dilation):
            head0, valid_cur, valid_prev, bias_cur, bias_prev = _dil_consts(g, pair_idx, dilation)
            units = _dil_units(s_len, dilation)
            scale = 1.0 / math.sqrt(HEAD_DIM)
            for u0 in range(0, len(units), DIL_CHAINS):
                chunk = units[u0:u0 + DIL_CHAINS]
                loaded = []
                for r, n in chunk:
                    rows = _dil_rows(n, r, dilation)
                    q01 = _stack_heads(q_ref[rows, :].astype(BF16), head0)
                    do01 = _stack_heads(do_ref[rows, :].astype(BF16), head0)
                    lse01 = jnp.concatenate([lse_ref.at[0][rows, :], lse_ref.at[1][rows, :]], axis=0)
                    d01 = jnp.concatenate([d_s.at[0][rows, :], d_s.at[1][rows, :]], axis=0)
                    blocks = [(rows, bias_cur, valid_cur)]
                    if n > 0:
                        blocks.append((_dil_rows(n - 1, r, dilation), bias_prev, valid_prev))
                    parts = []
                    for krows, bias, valid in blocks:
                        k = k_ref[krows, :].astype(BF16)
                        v = v_ref[krows, :].astype(BF16)
                        parts.append((krows, k, _dil_scores(q01, k, bias, valid), _dot_nt(do01, v)))
                    loaded.append((rows, q01, do01, lse01, d01, parts))
                grads = []
                for rows, q01, do01, lse01, d01, parts in loaded:
                    for krows, k, sc, dp in parts:
                        p = jnp.exp(sc - lse01)
                        grads.append((p.astype(BF16), (p * (dp - d01) * scale).astype(BF16)))
                it = iter(grads)
                updates = []
                for rows, q01, do01, lse01, d01, parts in loaded:
                    dq = jnp.zeros((2 * BLK, LANES), F32)
                    for krows, k, sc, dp in parts:
                        p_b, ds = next(it)
                        dq = dq + _dot(ds, k)
                        updates.append((krows, _dot_tn(ds, q01), _dot_tn(p_b, do01)))
                    dq_s[rows, :] = jnp.where(head0, dq[:BLK], dq[BLK:])
                for krows, dk, dv in updates:
                    dk_s[krows, :] = dk_s[krows, :] + dk
                    dv_s[krows, :] = dv_s[krows, :] + dv

        for g, (_, dilation) in enumerate(DIL_PAIRS):
            pl.when(group == g)(functools.partial(one_group, g, dilation))
        dq_ref[...] = dq_s[...].astype(dq_ref.dtype)
        dk_ref[...] = dk_s[...].astype(dk_ref.dtype)
        dv_ref[...] = dv_s[...].astype(dv_ref.dtype)

    def body(*refs):
        pair_idx, group = pl.program_id(1), pl.program_id(2)
        step = (pl.program_id(0) * n_pairs + pair_idx) * n_groups + group
        n_c, n_o = len(carry.arrays), len(carry.out_shapes)
        own = refs[:6] + refs[6 + n_c:9 + n_c] + refs[9 + n_c + n_o:13 + n_c + n_o]
        carry.run(refs, 6, 3, step, b_sz * n_pairs * n_groups, lambda: compute(pair_idx, group, *own))

    def qkv_spec(kind):
        return pl.BlockSpec((None, s_len, LANES),
                            lambda b, p, g: (b, 0, q_off + kind * per_kind + g * n_pairs + p))

    tok_spec = pl.BlockSpec((None, s_len, LANES), lambda b, p, g: (b, 0, p))
    out_spec = pl.BlockSpec((None, s_len, LANES), lambda b, p, g: (b, 0, g * n_pairs + p))
    out_sd = jax.ShapeDtypeStruct((b_sz, s_len, DIL_WIDTH), BF16)
    c_in, c_out, c_shapes, c_alias, c_sems = carry.call_args(6, 3)
    res = pl.pallas_call(
        body, name="dil_bwd", grid=(b_sz, n_pairs, n_groups),
        in_specs=[qkv_spec(0), qkv_spec(1), qkv_spec(2), tok_spec,
                  pl.BlockSpec((None, None, 2, s_len, LANES), lambda b, p, g: (b, p, 0, 0, 0)), tok_spec] + c_in,
        out_specs=[out_spec, out_spec, out_spec] + c_out,
        out_shape=[out_sd, out_sd, out_sd] + c_shapes,
        input_output_aliases=c_alias,
        scratch_shapes=[pltpu.VMEM((2, s_len, LANES), F32)] + [pltpu.VMEM((s_len, LANES), F32)] * 3 + c_sems,
        compiler_params=pltpu.CompilerParams(dimension_semantics=("arbitrary", "arbitrary", "arbitrary"),
                                             vmem_limit_bytes=VMEM_CAP),
    )(qkv, qkv, qkv, o_dl, lse, do_dl, *carry.arrays)
    return res[:3], res[3:]


def _mesh_pos():
    return lax.axis_index("x"), lax.axis_index("y"), lax.axis_index("c")


def _other_chips(x, y):
    return [(1 - x, y), (x, 1 - y), (1 - x, 1 - y)]


def _hbm_specs(n):
    return [pl.BlockSpec(memory_space=pl.ANY)] * n


SWAPPED = ("w_ffn_in",)


def _slot(x, y, swapped):
    return 2 * y + x if swapped else 2 * x + y


def _cast_to_slab(w, name):
    rows, cols = w.shape
    mine = jnp.reshape(_slot(lax.axis_index("x"), lax.axis_index("y"), False), (1,)).astype(jnp.int32)

    def body(idx_ref, w_ref, o_ref):
        o_ref[...] = w_ref[...].astype(BF16)

    return pl.pallas_call(
        body, name=name,
        grid_spec=pltpu.PrefetchScalarGridSpec(
            num_scalar_prefetch=1, grid=(1,),
            in_specs=[pl.BlockSpec((rows, cols), lambda i, idx: (0, 0))],
            out_specs=pl.BlockSpec((None, rows, cols), lambda i, idx: (idx[0], 0, 0))),
        out_shape=_hbm_array((N_CHIPS, rows, cols), BF16),
        compiler_params=pltpu.CompilerParams(vmem_limit_bytes=_vmem_limit(rows * cols * 6)),
    )(mine, w)


def _gather_issue(slabs, send_sems, recv_sems, swapped):
    x, y, c = _mesh_pos()
    for k, slab in enumerate(slabs):
        half = slab.shape[1] // 2
        rows = slab.at[_slot(x, y, swapped[k]), pl.ds(c * half, half), :]
        for r, (px, py) in enumerate(_other_chips(x, y)):
            pltpu.make_async_remote_copy(
                src_ref=rows, dst_ref=rows, send_sem=send_sems.at[6 * k + r], recv_sem=recv_sems.at[6 * k + r],
                device_id=(px, py, c), device_id_type=MESH).start()


def _gather_complete(slabs, send_sems, recv_sems, swapped):
    x, y, c = _mesh_pos()
    chips = _other_chips(x, y)

    def copy(k, sem, block, rows, to):
        ref = slabs[k].at[block, rows, :]
        return pltpu.make_async_remote_copy(
            src_ref=ref, dst_ref=ref, send_sem=send_sems.at[sem], recv_sem=recv_sems.at[sem],
            device_id=to, device_id_type=MESH)

    for k, slab in enumerate(slabs):
        half = slab.shape[1] // 2
        for r, (px, py) in enumerate(chips):
            theirs = _slot(px, py, swapped[k])
            copy(k, 6 * k + r, theirs, pl.ds(c * half, half), (px, py, c)).wait_recv()
            copy(k, 6 * k + 3 + r, theirs, pl.ds(c * half, half), (x, y, 1 - c)).start()
    for k, slab in enumerate(slabs):
        half = slab.shape[1] // 2
        for r, (px, py) in enumerate(chips):
            copy(k, 6 * k + 3 + r, _slot(px, py, swapped[k]), pl.ds((1 - c) * half, half), (x, y, 1 - c)).wait_recv()
    for k, slab in enumerate(slabs):
        half = slab.shape[1] // 2
        for r, (px, py) in enumerate(chips):
            copy(k, 6 * k + r, _slot(x, y, swapped[k]), pl.ds(c * half, half), (px, py, c)).wait_send()
            copy(k, 6 * k + 3 + r, _slot(px, py, swapped[k]), pl.ds(c * half, half), (x, y, 1 - c)).wait_send()


def _gather_sems(n):
    return [pltpu.SemaphoreType.DMA((6 * n,)), pltpu.SemaphoreType.DMA((6 * n,))]


def _gather_carry(slabs, names):
    swapped = [k in SWAPPED for k in names]
    return _Carry(slabs, [_hbm_array(a.shape, a.dtype) for a in slabs], True, _gather_sems(len(slabs)),
                  lambda ins, outs, sems: _gather_issue(outs, *sems, swapped),
                  lambda ins, outs, sems: _gather_complete(outs, *sems, swapped))


def _cast_carry(shards, names):
    swapped = [k in SWAPPED for k in names]

    def start(ins, outs, sems):
        x, y, _ = _mesh_pos()
        for w, slab, sw in zip(ins, outs, swapped):
            def cast(f32_buf, bf16_buf, sem, w=w, slab=slab, sw=sw):
                load = pltpu.make_async_copy(w, f32_buf, sem)
                load.start()
                load.wait()
                bf16_buf[...] = f32_buf[...].astype(BF16)
                store = pltpu.make_async_copy(bf16_buf, slab.at[_slot(x, y, sw)], sem)
                store.start()
                store.wait()

            pl.run_scoped(cast, pltpu.VMEM(w.shape, F32), pltpu.VMEM(w.shape, BF16), pltpu.SemaphoreType.DMA)

    return _Carry(shards, [_hbm_array((N_CHIPS,) + w.shape, BF16) for w in shards], False, [], start,
                  lambda ins, outs, sems: None)


def _pair_copies(ins, outs, send_sems, recv_sems):
    x, y, c = _mesh_pos()
    copies = []
    for k, g in enumerate(ins):
        half = g.shape[1] // 2
        copies.append(pltpu.make_async_remote_copy(
            src_ref=g.at[:, pl.ds((1 - c) * half, half), :], dst_ref=outs[k],
            send_sem=send_sems.at[k], recv_sem=recv_sems.at[k],
            device_id=(x, y, 1 - c), device_id_type=MESH))
    return copies


def _pair_carry(grads):
    n = len(grads)

    def start(ins, outs, sems):
        for cp in _pair_copies(ins, outs, *sems):
            cp.start()

    def finish(ins, outs, sems):
        for cp in _pair_copies(ins, outs, *sems):
            cp.wait()

    return _Carry(grads, [jax.ShapeDtypeStruct((N_CHIPS, g.shape[1] // 2, g.shape[2]), g.dtype) for g in grads], False,
                  [pltpu.SemaphoreType.DMA((n,)), pltpu.SemaphoreType.DMA((n,))], start, finish)


def _pair_exchange(grads, tag):
    carry = _pair_carry(grads)
    n = len(grads)

    def body(*refs):
        carry.start(refs[:n], refs[n:2 * n], refs[2 * n:])
        carry.finish(refs[:n], refs[n:2 * n], refs[2 * n:])

    return pl.pallas_call(
        body, name="grad_pair_exchange_" + tag, in_specs=_hbm_specs(n), out_specs=_hbm_specs(n),
        out_shape=carry.out_shapes, scratch_shapes=carry.sems,
    )(*grads)


def _pair_sum(grad, other, name, swapped):
    _, rows, cols = grad.shape
    half = rows // 2
    x, y, c = _mesh_pos()
    idx = jnp.stack([c, _slot(x, y, swapped)]).astype(jnp.int32)

    def body(idx_ref, g_ref, p_ref, own_ref, sb_ref):
        s = g_ref[...] + p_ref[...].astype(F32)
        sb_ref[...] = s.astype(BF16)

        @pl.when(pl.program_id(0) == idx_ref[1])
        def _():
            own_ref[...] = s

    blk = pl.BlockSpec((None, half, cols), lambda p, idx: (p, 0, 0))
    return pl.pallas_call(
        body, name=name,
        grid_spec=pltpu.PrefetchScalarGridSpec(
            num_scalar_prefetch=1, grid=(N_CHIPS,),
            in_specs=[pl.BlockSpec((None, half, cols), lambda p, idx: (p, idx[0], 0)), blk],
            out_specs=[pl.BlockSpec((half, cols), lambda p, idx: (0, 0)), blk]),
        out_shape=[jax.ShapeDtypeStruct((half, cols), F32), jax.ShapeDtypeStruct((N_CHIPS, half, cols), BF16)],
        compiler_params=pltpu.CompilerParams(dimension_semantics=("arbitrary",),
                                             vmem_limit_bytes=_vmem_limit(4 * half * cols * 4)),
    )(idx, grad, other)


def _chip_copies(sums_bf16, lands, send_sems, recv_sems, swapped):
    x, y, c = _mesh_pos()
    return [pltpu.make_async_remote_copy(
        src_ref=sums_bf16[k].at[_slot(px, py, swapped[k])], dst_ref=lands[k].at[r],
        send_sem=send_sems.at[3 * k + r], recv_sem=recv_sems.at[3 * k + r],
        device_id=(px, py, c), device_id_type=MESH)
        for k in range(len(sums_bf16)) for r, (px, py) in enumerate(_other_chips(x, y))]


def _chip_carry(sums_bf16, names):
    swapped = [k in SWAPPED for k in names]

    def start(ins, outs, sems):
        for cp in _chip_copies(ins, outs, *sems, swapped):
            cp.start()

    def finish(ins, outs, sems):
        for cp in _chip_copies(ins, outs, *sems, swapped):
            cp.wait()

    return _Carry(sums_bf16, _chip_landing(sums_bf16), False, _chip_sems(len(sums_bf16)), start, finish)


def _chip_sems(n):
    return [pltpu.SemaphoreType.DMA((3 * n,)), pltpu.SemaphoreType.DMA((3 * n,))]


def _chip_landing(sums_bf16):
    return [jax.ShapeDtypeStruct((N_CHIPS - 1,) + s.shape[1:], BF16) for s in sums_bf16]


def _chip_sum(own, landed, name):
    rows, cols = own.shape
    core = jnp.reshape(lax.axis_index("c"), (1,)).astype(jnp.int32)

    def body(core_ref, o_ref, l_ref, out_ref):
        out_ref[...] = ((o_ref[...] + l_ref[0].astype(F32)) + l_ref[1].astype(F32)) + l_ref[2].astype(F32)

    return pl.pallas_call(
        body, name=name,
        grid_spec=pltpu.PrefetchScalarGridSpec(
            num_scalar_prefetch=1, grid=(1,),
            in_specs=[pl.BlockSpec((rows, cols), lambda i, core_ref: (0, 0)),
                      pl.BlockSpec((N_CHIPS - 1, rows, cols), lambda i, core_ref: (0, 0, 0))],
            out_specs=pl.BlockSpec((rows, cols), lambda i, core_ref: (core_ref[0], 0))),
        out_shape=jax.ShapeDtypeStruct((2 * rows, cols), F32),
        compiler_params=pltpu.CompilerParams(vmem_limit_bytes=_vmem_limit(3 * rows * cols * 4)),
    )(core, own, landed)


def _halves_carry(fulls):
    n = len(fulls)

    def copies(outs, send_sems, recv_sems, own):
        x, y, c = _mesh_pos()
        res = []
        for k, out in enumerate(outs):
            half = out.shape[0] // 2
            rows = out.at[pl.ds((c if own else 1 - c) * half, half), :]
            res.append(pltpu.make_async_remote_copy(
                src_ref=rows, dst_ref=rows, send_sem=send_sems.at[k], recv_sem=recv_sems.at[k],
                device_id=(x, y, 1 - c), device_id_type=MESH))
        return res

    def start(ins, outs, sems):
        for cp in copies(outs, *sems, True):
            cp.start()

    def finish(ins, outs, sems):
        for cp in copies(outs, *sems, False):
            cp.wait_recv()
        for cp in copies(outs, *sems, True):
            cp.wait_send()

    return _Carry(fulls, [jax.ShapeDtypeStruct(f.shape, F32) for f in fulls], True,
                  [pltpu.SemaphoreType.DMA((n,)), pltpu.SemaphoreType.DMA((n,))], start, finish)


def _final_exchange(fulls, v):
    n = len(fulls)
    rows, cols = v.shape
    n_dev = 8

    def body(*refs):
        v_ref, out_ref = refs[0], refs[1 + 2 * n]
        outs = refs[1 + n:1 + 2 * n]
        buf, v_send, v_recv, h_send, h_recv = refs[2 + 2 * n:]
        x, y, c = _mesh_pos()
        me = 4 * x + 2 * y + c
        buf[me] = v_ref[...]
        peers = [(1 - x if r & 4 else x, 1 - y if r & 2 else y, 1 - c if r & 1 else c) for r in range(1, n_dev)]
        copies = []
        for r, peer in enumerate(peers):
            copies.append(pltpu.make_async_remote_copy(
                src_ref=v_ref, dst_ref=buf.at[me], send_sem=v_send.at[r], recv_sem=v_recv.at[r],
                device_id=peer, device_id_type=MESH))
        for k in range(n):
            half = fulls[k].shape[0] // 2
            mine = outs[k].at[pl.ds(c * half, half), :]
            copies.append(pltpu.make_async_remote_copy(
                src_ref=mine, dst_ref=mine, send_sem=h_send.at[k], recv_sem=h_recv.at[k],
                device_id=(x, y, 1 - c), device_id_type=MESH))
        for cp in copies:
            cp.start()
        for r, (px, py, pc) in enumerate(peers):
            pltpu.make_async_remote_copy(
                src_ref=v_ref, dst_ref=buf.at[4 * px + 2 * py + pc], send_sem=v_send.at[r], recv_sem=v_recv.at[r],
                device_id=(px, py, pc), device_id_type=MESH).wait_recv()
        for k in range(n):
            half = fulls[k].shape[0] // 2
            theirs = outs[k].at[pl.ds((1 - c) * half, half), :]
            pltpu.make_async_remote_copy(
                src_ref=theirs, dst_ref=theirs, send_sem=h_send.at[k], recv_sem=h_recv.at[k],
                device_id=(x, y, 1 - c), device_id_type=MESH).wait_recv()
        for cp in copies:
            cp.wait_send()
        acc = buf[0]
        for d in range(1, n_dev):
            acc = acc + buf[d]
        out_ref[...] = acc
        out_ref[3:4, :] = jnp.broadcast_to(jnp.sum(acc[3:4, :], axis=1, keepdims=True), (1, cols))

    vm = pl.BlockSpec(memory_space=pltpu.VMEM)
    res = pl.pallas_call(
        body, name="final_exchange",
        in_specs=[vm] + _hbm_specs(n), out_specs=_hbm_specs(n) + [vm],
        out_shape=[jax.ShapeDtypeStruct(f.shape, F32) for f in fulls] + [jax.ShapeDtypeStruct((rows, cols), F32)],
        input_output_aliases={1 + k: k for k in range(n)},
        scratch_shapes=[pltpu.VMEM((n_dev, rows, cols), F32),
                        pltpu.SemaphoreType.DMA((n_dev - 1,)), pltpu.SemaphoreType.DMA((n_dev - 1,)),
                        pltpu.SemaphoreType.DMA((n,)), pltpu.SemaphoreType.DMA((n,))],
    )(v, *fulls)
    return res[:n], res[n]


def _adamw_math(w, g, m, v):
    m = ADAM_B1 * m + (1.0 - ADAM_B1) * g
    v = ADAM_B2 * v + (1.0 - ADAM_B2) * (g * g)
    m_hat = m / (1.0 - ADAM_B1 ** ADAM_STEP)
    v_hat = v / (1.0 - ADAM_B2 ** ADAM_STEP)
    delta = -ADAM_LR * (m_hat / (jnp.sqrt(v_hat) + ADAM_EPS) + ADAM_WD * w)
    return delta, m, v


def _adamw(w, g, m, v, name):
    rows, cols = w.shape
    tm = rows // 2 if (rows // 2) % 8 == 0 else rows
    return _rowwise(_adamw_math, [w, g, m, v], [], [(cols, F32)] * 3, [], tm=tm, name=name)


def _unshard_cols(gathered):
    n, r, c = gathered.shape
    return jnp.transpose(gathered, (1, 0, 2)).reshape(r, n * c)


def _shard_cols(full):
    r, nc = full.shape
    return jnp.transpose(full.reshape(r, N_CHIPS, nc // N_CHIPS), (1, 0, 2))


LATE = ["w_sb_up", "w_dil_up", "w_out", "w_ffn_in", "w_ffn_out"]


def _late_weights(slabs, d_model, d_ff):
    g = dict(zip(LATE, slabs))
    return (_unshard_cols(g["w_sb_up"]), _unshard_cols(g["w_dil_up"]), g["w_out"].reshape(d_model, d_model),
            _unshard_cols(g["w_ffn_in"]), g["w_ffn_out"].reshape(d_ff, d_model))


ROW_SHARDED = ("w_in", "w_out", "w_ffn_in", "w_ffn_out")


def _chip_major(grads):
    out = []
    for k, g in grads.items():
        if k in ROW_SHARDED:
            out.append(g.reshape(N_CHIPS, g.shape[0] // N_CHIPS, g.shape[1]))
        else:
            out.append(_shard_cols(g))
    return out


def _pair_sums(full, others, names):
    return [_pair_sum(g, o, "grad_pair_sum_" + k, k in SWAPPED) for g, o, k in zip(full, others, names)]


def _chip_sums(pair, landed, names):
    return {k: _chip_sum(p[0], l, "grad_chip_sum_" + k) for p, l, k in zip(pair, landed, names)}


def _fwd_bwd(x, loss_target, g_mix, g_ffn, g_fin, slab_in, late_shards):
    b_sz, s_len, d_model = x.shape
    t = b_sz * s_len
    d_ff = late_shards[-1].shape[0] * N_CHIPS
    x2d = x.reshape(t, d_model)
    tgt2d = loss_target.reshape(t, d_model)

    u, (slab_in, *late_slabs) = _rowwise(
        lambda xv, g: (_rms_stats(xv)[0] * g,), [_in_hbm(x2d)], [g_mix], [(d_model, BF16)], [], tm=512, name="norm_mix",
        carry=_gather_carry([slab_in], ["w_in"]) + _cast_carry(late_shards, LATE), out_type=_hbm_array)
    u, wt_in = _in_hbm(u), _in_hbm(slab_in.reshape(-1, d_model))
    qkv, (slab_ffn_out,) = _mm(u, wt_in, tb=True, b_cols=(0, QKV_WIDTH), tm=2048, tn=768, tk=d_model, name="proj_qkv",
                               carry=_gather_carry(late_slabs[4:], LATE[4:]))
    gates = _mm(u, wt_in, tb=True, b_cols=(QKV_WIDTH, 2 * d_model), out_dtype=BF16, tm=t, tn=256, tk=d_model,
                name="proj_gates")
    qkv3 = qkv.reshape(b_sz, s_len, QKV_WIDTH)
    o_sb, (slab_ffn_in,) = _sb_fwd(qkv3, b_sz, s_len, _gather_carry(late_slabs[3:4], LATE[3:4]))
    o_dl, lse, small_slabs = _dil_fwd(qkv3, b_sz, s_len, _gather_carry(late_slabs[:3], LATE[:3]))
    wf_sb_up, wf_dil_up, wf_out, wf_ffn_in, wf_ffn_out = _late_weights(
        list(small_slabs) + [slab_ffn_in, slab_ffn_out], d_model, d_ff)
    o_sb2, o_dl2 = o_sb.reshape(t, SB_WIDTH), o_dl.reshape(t, DIL_OUT_WIDTH)
    y_sb = _mm(o_sb2, wf_sb_up, out_dtype=BF16, tm=1024, tn=1024, tk=SB_WIDTH, name="sb_up", out_type=_hbm_array)
    y_dl = _mm(o_dl2, wf_dil_up, out_dtype=BF16, tm=1024, tn=1024, tk=DIL_OUT_WIDTH, name="dil_up", out_type=_hbm_array)

    def merge_fn(gt, ys, yd):
        return (_sigmoid(gt[:, :d_model]) * ys + _sigmoid(gt[:, d_model:]) * yd,)

    (merged,) = _rowwise(merge_fn, [gates, y_sb, y_dl], [], [(d_model, BF16)], [], tm=512, name="merge")
    x1 = _mm(merged, wf_out, add=x2d, tm=512, tn=1024, tk=d_model, name="mix_out")
    (u2,) = _rowwise(lambda xv, g: (_rms_stats(xv)[0] * g,), [_in_hbm(x1)], [g_ffn], [(d_model, BF16)], [], tm=512, name="norm_ffn",
                     out_type=_hbm_array)
    u2 = _in_hbm(u2)
    half_ff = d_ff // 2

    def act_fn(hv):
        gate = hv[:, :half_ff]
        return hv, gate * _sigmoid(gate) * hv[:, half_ff:]

    h, act = _mm(u2, wf_ffn_in, tm=512, tn=d_ff, tk=d_model, name="ffn_in",
                 epilogue=(act_fn, [], [], [(d_ff, BF16), (half_ff, BF16)], []))
    def head_fn(xv, tg, g):
        xhat, r = _rms_stats(xv)
        err = xhat * g - tg
        dy = err * (1.0 / d_model)
        dx, dg_rows = _rms_bwd(dy, xhat, r, g)
        loss_lanes = (0.5 / d_model) * jnp.sum(err * err, axis=0, keepdims=True)
        return dx, dx, jnp.sum(dg_rows, axis=0, keepdims=True), loss_lanes

    dx2, dx2_b, dg_fin, loss_lanes = _mm(
        act, wf_ffn_out, add=x1, tm=512, tn=1024, tk=d_ff, name="ffn_out",
        epilogue=(head_fn, [tgt2d], [g_fin], [(d_model, F32), (d_model, BF16)], [(1, d_model), (1, d_model)]))

    def dact_fn(da, hv):
        gate, up = hv[:, :half_ff], hv[:, half_ff:]
        sg = _sigmoid(gate)
        dgate = da * up * (sg * (1.0 + gate * (1.0 - sg)))
        return (jnp.concatenate([dgate, da * (gate * sg)], axis=1),)

    dx2_b = _in_hbm(dx2_b)
    (dh,) = _mm(dx2_b, wf_ffn_out, tb=True, tm=512, tn=half_ff, tk=d_model, name="ffn_out_dx",
                epilogue=(dact_fn, [h], [], [(d_ff, BF16)], []))
    gw_ffn_out = _mm(act, dx2_b, ta=True, tm=256, tn=d_model, tk=t, name="ffn_out_dw")
    def norm_bwd_fn(du_, dres, xv, g):
        xhat, r = _rms_stats(xv)
        dx, dg_rows = _rms_bwd(du_, xhat, r, g)
        return dres + dx, jnp.sum(dg_rows, axis=0, keepdims=True)

    def norm_bwd_twice(*args):
        dx, dg = norm_bwd_fn(*args)
        return dx, dx, dg

    dx1, dx1_b, dg_ffn = _mm(dh, wf_ffn_in, tb=True, tm=512, tn=1024, tk=2 * d_ff, name="ffn_in_dx",
                             epilogue=(norm_bwd_twice, [dx2, x1], [g_ffn], [(d_model, F32), (d_model, BF16)], [(1, d_model)]))
    gwt_ffn_in = _mm(dh, u2, ta=True, tm=512, tn=d_model, tk=t, name="ffn_in_dw")

    dx1_b = _in_hbm(dx1_b)
    dmerged = _mm(dx1_b, wf_out, tb=True, out_dtype=BF16, tm=512, tn=1024, tk=d_model, name="mix_out_dx",
                  out_type=_hbm_array)
    gw_out = _mm(merged, dx1_b, ta=True, tm=256, tn=d_model, tk=t, name="mix_out_dw")

    def merge_bwd_fn(gt, ys, yd, dm):
        s_sb, s_dl = _sigmoid(gt[:, :d_model]), _sigmoid(gt[:, d_model:])
        dgates = jnp.concatenate([dm * ys * s_sb * (1.0 - s_sb), dm * yd * s_dl * (1.0 - s_dl)], axis=1)
        return dgates, dm * s_sb, dm * s_dl

    full_big = _chip_major({"w_out": gw_out, "w_ffn_in": gwt_ffn_in})
    dgates, dy_sb, dy_dl, others_big = _rowwise(
        merge_bwd_fn, [gates, y_sb, y_dl, dmerged], [], [(2 * d_model, BF16), (d_model, BF16), (d_model, BF16)], [],
        tm=256, name="merge_bwd", carry=_pair_carry(full_big))
    pair_big = _pair_sums(full_big, others_big, LATE[2:4])
    do_sb = _mm(dy_sb, wf_sb_up, tb=True, out_dtype=BF16, tm=1024, tn=SB_WIDTH, tk=d_model, name="sb_up_dx")
    gw_sb_up = _mm(o_sb2, dy_sb, ta=True, tm=SB_WIDTH, tn=1024, tk=512, name="sb_up_dw")
    do_dl = _mm(dy_dl, wf_dil_up, tb=True, tm=1024, tn=DIL_OUT_WIDTH, tk=d_model, name="dil_up_dx")
    gw_dil_up = _mm(o_dl2, dy_dl, ta=True, tm=DIL_OUT_WIDTH, tn=1024, tk=512, name="dil_up_dw")
    rest = [LATE[0], LATE[1], LATE[4]]
    full_rest = _chip_major({"w_sb_up": gw_sb_up, "w_dil_up": gw_dil_up, "w_ffn_out": gw_ffn_out})
    (dq_sb, dk_sb, dv_sb), brought = _sb_bwd(
        qkv3, o_sb, do_sb.reshape(b_sz, s_len, SB_WIDTH), b_sz, s_len,
        _chip_carry([p[1] for p in pair_big], LATE[2:4]) + _pair_carry(full_rest))
    pair_rest = _pair_sums(full_rest, brought[2:], rest)
    (dq_dl, dk_dl, dv_dl), landed_b = _dil_bwd(
        qkv3, o_dl, lse, do_dl.reshape(b_sz, s_len, DIL_OUT_WIDTH), b_sz, s_len,
        _chip_carry([p[1] for p in pair_rest], rest))
    pair = pair_rest[:2] + pair_big + pair_rest[2:]
    landed = [landed_b[0], landed_b[1], brought[0], brought[1], landed_b[2]]
    dproj = [a.reshape(t, -1) for a in (dq_sb, dk_sb, dv_sb, dq_dl, dk_dl, dv_dl)] + [dgates]
    gwt_in, gwt_in_b = _mm(dproj, u, ta=True, tm=256, tn=d_model, tk=t, name="proj_dw",
                           epilogue=(lambda tile: (tile, tile), [], [], [(d_model, F32), (d_model, BF16)], []))
    full_in = _chip_major({"w_in": gwt_in})
    pair_in = _pair_sums(full_in, _pair_exchange(_chip_major({"w_in": gwt_in_b}), "w_in"), ["w_in"])
    late_halves = _chip_sums(pair, landed, LATE)
    (dx, dg_mix), brought_in = _mm(
        dproj, wt_in, tm=512, tn=1024, tk=wt_in.shape[0], name="proj_dx",
        carry=_halves_carry([late_halves[k] for k in LATE]) + _chip_carry([p[1] for p in pair_in], ["w_in"]),
        epilogue=(norm_bwd_fn, [dx1, x2d], [g_mix], [(d_model, F32)], [(1, d_model)]))

    grads = dict(zip(LATE, brought_in[:len(LATE)]))
    grads.update(_chip_sums(pair_in, brought_in[len(LATE):], ["w_in"]))
    return dx, grads, dg_mix, dg_ffn, dg_fin, loss_lanes


def kernel(x, norm_mix_g, w_in, w_sb_up, w_dil_up, w_out, norm_ffn_g, w_ffn_in, w_ffn_out, norm_final_g, loss_target, m_norm_mix_g, m_w_in, m_w_sb_up, m_w_dil_up, m_w_out, m_norm_ffn_g, m_w_ffn_in, m_w_ffn_out, m_norm_final_g, v_norm_mix_g, v_w_in, v_w_sb_up, v_w_dil_up, v_w_out, v_norm_ffn_g, v_w_ffn_in, v_w_ffn_out, v_norm_final_g):
    b_sz, s_len, d_model = x.shape
    d_ff = w_ffn_out.shape[1] * N_CHIPS
    g_mix, g_ffn, g_fin = norm_mix_g, norm_ffn_g, norm_final_g.reshape(1, d_model)

    names = ["w_in", "w_sb_up", "w_dil_up", "w_out", "w_ffn_in", "w_ffn_out"]
    shards = {"w_in": jnp.swapaxes(w_in[0], 0, 1), "w_sb_up": w_sb_up[0], "w_dil_up": w_dil_up[0], "w_out": w_out[0],
              "w_ffn_in": w_ffn_in[0], "w_ffn_out": w_ffn_out[0]}
    slab_in = _cast_to_slab(shards["w_in"], "cast_w_in")

    dx, grads, dg_mix, dg_ffn, dg_fin, loss_lanes = _fwd_bwd(
        x, loss_target, g_mix, g_ffn, g_fin, slab_in, [shards[k] for k in LATE])

    small = jnp.concatenate([dg_mix, dg_ffn, dg_fin, loss_lanes, jnp.zeros((4, d_model), F32)], axis=0)
    (grads["w_in"],), small = _final_exchange([grads["w_in"]], small)
    grads["w_ffn_in"] = jnp.swapaxes(grads["w_ffn_in"], 0, 1)
    loss = small[3, 0]
    gains = jnp.concatenate([g_mix, g_ffn, g_fin, jnp.zeros((5, d_model), F32)], axis=0)
    gains_m = jnp.concatenate([m_norm_mix_g, m_norm_ffn_g, m_norm_final_g.reshape(1, d_model), jnp.zeros((5, d_model), F32)], axis=0)
    gains_v = jnp.concatenate([v_norm_mix_g, v_norm_ffn_g, v_norm_final_g.reshape(1, d_model), jnp.ones((5, d_model), F32)], axis=0)
    gd, gm, gv = _rowwise(_adamw_math, [gains, small, gains_m, gains_v], [], [(d_model, F32)] * 3, [], tm=8, name="adamw_gains")

    moments = {"w_in": (jnp.swapaxes(m_w_in[0], 0, 1), jnp.swapaxes(v_w_in[0], 0, 1)),
               "w_sb_up": (m_w_sb_up[0], v_w_sb_up[0]), "w_dil_up": (m_w_dil_up[0], v_w_dil_up[0]),
               "w_out": (m_w_out[0], v_w_out[0]), "w_ffn_in": (m_w_ffn_in[0], v_w_ffn_in[0]),
               "w_ffn_out": (m_w_ffn_out[0], v_w_ffn_out[0])}
    upd = {k: _adamw(shards[k], grads[k], moments[k][0], moments[k][1], "adamw_" + k) for k in names}

    def as_output(k, a):
        return (jnp.swapaxes(a, 0, 1) if k == "w_in" else a)[None]

    def w_out_of(i):
        return [as_output(k, upd[k][i]) for k in names]

    def ordered(mix, ws, ffn_g, fin):
        return [mix, ws[0], ws[1], ws[2], ws[3], ffn_g, ws[4], ws[5], fin]

    grad_ws = [as_output(k, grads[k]) for k in names]
    outs = [loss, dx.reshape(b_sz, s_len, d_model)]
    outs += ordered(small[0:1], grad_ws, small[1:2], small[2])
    outs += ordered(gd[0:1], w_out_of(0), gd[1:2], gd[2])
    outs += ordered(gm[0:1], w_out_of(1), gm[1:2], gm[2])
    outs += ordered(gv[0:1], w_out_of(2), gv[1:2], gv[2])
    return tuple(outs)
```
